```python
import math
import jax, jax.numpy as jnp
from jax import lax
import numpy as np

D_MODEL = 1024
BATCH = 8
SEQ = 16384
DEPTH = 1

HEAD_DIM = 128
HEADS_PER_GROUP = 4
ATTN_GROUPS = ((128, 1), (512, 4), (2048, 16))
N_ATTN_GROUPS = len(ATTN_GROUPS)
ATTN_HEADS = HEADS_PER_GROUP * N_ATTN_GROUPS
ATTN_QK_WIDTH = ATTN_HEADS * HEAD_DIM
ATTN_OUT_WIDTH = HEADS_PER_GROUP * HEAD_DIM
BLOCK = 128
ROPE_THETA = 500000.0
ROPE_DIM = HEAD_DIM // 4
SSM_WIDTH = 512
SSM_GROUP = 16
SSM_GROUPS = SSM_WIDTH // SSM_GROUP
SSM_STATE = 64
DT_MIN = 0.001
DT_MAX = 0.1
D_FF = -(-8 * D_MODEL // (3 * 256)) * 256
PLE_DIM = 256
EPS = 1e-6
IN_WIDTH = 3 * ATTN_QK_WIDTH + SSM_WIDTH + 2 * D_MODEL

kernel_name = "hybrid_dilated_attn_s5_gated_block"


def rmsnorm(x, g):
    xf = x.astype(jnp.float32)
    y = xf * lax.rsqrt(jnp.mean(xf * xf, axis=-1, keepdims=True) + EPS)
    return (y * g.astype(jnp.float32)).astype(x.dtype)


def partial_rotary(x, positions):
    half = ROPE_DIM // 2
    inv_freq = ROPE_THETA ** (-jnp.arange(half, dtype=jnp.float32) * 2.0 / ROPE_DIM)
    ang = positions.astype(jnp.float32)[..., None] * inv_freq
    cos = jnp.cos(ang)[:, :, None, :]
    sin = jnp.sin(ang)[:, :, None, :]
    xr = x[..., :ROPE_DIM].astype(jnp.float32)
    x1, x2 = xr[..., :half], xr[..., half:]
    rot = jnp.concatenate([x1 * cos - x2 * sin, x2 * cos + x1 * sin], axis=-1).astype(x.dtype)
    return jnp.concatenate([rot, x[..., ROPE_DIM:]], axis=-1)


def dilated_band_attention(q, k, v, dilation, band):
    B, S, H, Dh = q.shape
    L = S // dilation
    Lp = -(-L // BLOCK) * BLOCK
    nb = Lp // BLOCK

    def to_sub(t):
        t = jnp.moveaxis(t.reshape(B, L, dilation, H, Dh), 2, 1)
        t = jnp.pad(t, ((0, 0), (0, 0), (0, Lp - L), (0, 0), (0, 0)))
        return t.reshape(B, dilation, nb, BLOCK, H, Dh)

    def with_prev(t):
        prev = jnp.pad(t, ((0, 0), (0, 0), (1, 0), (0, 0), (0, 0), (0, 0)))[:, :, :-1]
        return jnp.concatenate([prev, t], axis=3)

    def from_sub(t):
        rest = t.shape[4:]
        t = t.reshape((B, dilation, Lp) + rest)[:, :, :L]
        return jnp.moveaxis(t, 1, 2).reshape((B, S) + rest)

    qb = to_sub(q)
    kk = with_prev(to_sub(k))
    vv = with_prev(to_sub(v))
    scale = 1.0 / math.sqrt(Dh)
    s = jnp.einsum('brnqhd,brnkhd->brnhqk', qb, kk,
                   preferred_element_type=jnp.float32) * scale
    qi = jnp.arange(BLOCK)[:, None]
    kj = jnp.arange(2 * BLOCK)[None, :]
    rel = BLOCK + qi - kj
    in_band = (rel >= 0) & (rel <= band)
    not_first = (jnp.arange(nb) > 0)[:, None, None]
    valid = in_band[None] & (not_first | (kj >= BLOCK)[None])
    s = jnp.where(valid[None, None, :, None], s, -jnp.inf)
    m = jnp.max(s, axis=-1, keepdims=True)
    pexp = jnp.exp(s - m)
    den = jnp.sum(pexp, axis=-1, keepdims=True)
    o = jnp.einsum('brnhqk,brnkhd->brnqhd', pexp, vv.astype(jnp.float32))
    o = o / jnp.swapaxes(den[..., 0], -1, -2)[..., None]
    lse = jnp.swapaxes(m[..., 0] + jnp.log(den[..., 0]), -1, -2)
    return from_sub(o), from_sub(lse)


def _complex_affine_combine(e1, e2):
    a1r, a1i, b1r, b1i = e1
    a2r, a2i, b2r, b2i = e2
    ar = a1r * a2r - a1i * a2i
    ai = a1r * a2i + a1i * a2r
    br = a2r * b1r - a2i * b1i + b2r
    bi = a2r * b1i + a2i * b1r + b2i
    return (ar, ai, br, bi)


def s5_ssm(u, a_re, a_im, log_dt, b_re, b_im, c_re, c_im, d_skip):
    B, S, _ = u.shape
    uf = u.astype(jnp.float32).reshape(B, S, SSM_GROUPS, SSM_GROUP)
    dt = jnp.exp(log_dt.astype(jnp.float32))[:, None]
    lr = a_re.astype(jnp.float32)
    li = a_im.astype(jnp.float32)
    mag = jnp.exp(lr * dt)
    bar_re = mag * jnp.cos(li * dt)
    bar_im = mag * jnp.sin(li * dt)
    nr = bar_re - 1.0
    ni = bar_im
    den = lr * lr + li * li
    z_re = (nr * lr + ni * li) / den
    z_im = (ni * lr - nr * li) / den
    br_ = b_re.astype(jnp.float32)
    bi_ = b_im.astype(jnp.float32)
    bb_re = z_re[..., None] * br_ - z_im[..., None] * bi_
    bb_im = z_re[..., None] * bi_ + z_im[..., None] * br_
    bu_re = jnp.einsum('bsgc,gpc->bsgp', uf, bb_re)
    bu_im = jnp.einsum('bsgc,gpc->bsgp', uf, bb_im)
    ar = jnp.broadcast_to(bar_re, bu_re.shape)
    ai = jnp.broadcast_to(bar_im, bu_im.shape)
    _, _, h_re, h_im = lax.associative_scan(_complex_affine_combine, (ar, ai, bu_re, bu_im), axis=1)
    y = (jnp.einsum('bsgp,gcp->bsgc', h_re, c_re.astype(jnp.float32))
         - jnp.einsum('bsgp,gcp->bsgc', h_im, c_im.astype(jnp.float32))
         + d_skip.astype(jnp.float32) * uf)
    return y.reshape(B, S, SSM_WIDTH).astype(u.dtype)


def hybrid_layer(h, p_l, positions, g_mix, w_in, a_re, a_im, log_dt, b_re, b_im, c_re, c_im,
                 d_skip, w_attn_proj, w_glu_a, w_glu_b, w_out, g_ffn, w_ffn_gate, w_ffn_up,
                 w_ffn_down, w_ple_gate, w_ple_proj):
    B, S, _ = h.shape
    n = rmsnorm(h, g_mix)
    z = n @ w_in
    o0 = ATTN_QK_WIDTH
    q = z[..., 0:o0].reshape(B, S, ATTN_HEADS, HEAD_DIM)
    k = z[..., o0:2 * o0].reshape(B, S, ATTN_HEADS, HEAD_DIM)
    v = z[..., 2 * o0:3 * o0].reshape(B, S, ATTN_HEADS, HEAD_DIM)
    o1 = 3 * o0
    u = z[..., o1:o1 + SSM_WIDTH]
    o2 = o1 + SSM_WIDTH
    gate_attn = jax.nn.sigmoid(z[..., o2:o2 + D_MODEL])
    gate_ssm = jax.nn.sigmoid(z[..., o2 + D_MODEL:o2 + 2 * D_MODEL])

    q = partial_rotary(q, positions)
    k = partial_rotary(k, positions)
    outs = []
    lses = []
    for gi, (window, dilation) in enumerate(ATTN_GROUPS):
        hs = slice(gi * HEADS_PER_GROUP, (gi + 1) * HEADS_PER_GROUP)
        o_g, l_g = dilated_band_attention(q[:, :, hs], k[:, :, hs], v[:, :, hs],
                                          dilation, window // dilation)
        outs.append(o_g)
        lses.append(l_g)
    wts = jax.nn.softmax(jnp.stack(lses, axis=0), axis=0)
    attn = jnp.sum(wts[..., None] * jnp.stack(outs, axis=0), axis=0)
    attn_d = attn.reshape(B, S, ATTN_OUT_WIDTH).astype(h.dtype) @ w_attn_proj

    y = jax.nn.gelu(s5_ssm(u, a_re, a_im, log_dt, b_re, b_im, c_re, c_im, d_skip))
    ssm_d = (y @ w_glu_a) * jax.nn.sigmoid(y @ w_glu_b)

    h = h + (gate_attn * attn_d + gate_ssm * ssm_d) @ w_out

    n2 = rmsnorm(h, g_ffn)
    h = h + (jax.nn.silu(n2 @ w_ffn_gate) * (n2 @ w_ffn_up)) @ w_ffn_down

    h = h + jax.nn.sigmoid(h @ w_ple_gate) * (p_l.astype(h.dtype) @ w_ple_proj)
    return h


def _fwd_setup_inputs(seed: int = 0) -> dict:
    key = jax.random.key(seed)
    ks = jax.random.split(key, 26)
    f32 = jnp.float32

    def nrm(k, shape, fan_in):
        return jax.random.normal(k, shape, f32) * (fan_in ** -0.5)

    x = jax.random.normal(ks[0], (BATCH, SEQ, D_MODEL), f32)
    p = jax.random.normal(ks[1], (DEPTH, BATCH, SEQ, PLE_DIM), f32)
    positions = jnp.broadcast_to(jnp.arange(SEQ, dtype=jnp.int32)[None, :], (BATCH, SEQ))
    g_mix = 1.0 + 0.05 * jax.random.normal(ks[2], (DEPTH, D_MODEL), f32)
    w_in = nrm(ks[3], (DEPTH, D_MODEL, IN_WIDTH), D_MODEL)
    a_re = -0.5 + 0.01 * jax.random.normal(ks[4], (DEPTH, SSM_GROUPS, SSM_STATE), f32)
    a_im = (jnp.pi * jnp.arange(SSM_STATE, dtype=f32)[None, None, :]
            + 0.01 * jax.random.normal(ks[5], (DEPTH, SSM_GROUPS, SSM_STATE), f32))
    log_dt = jax.random.uniform(ks[6], (DEPTH, SSM_GROUPS), f32,
                                minval=math.log(DT_MIN), maxval=math.log(DT_MAX))
    b_re = nrm(ks[7], (DEPTH, SSM_GROUPS, SSM_STATE, SSM_GROUP), 2 * SSM_GROUP)
    b_im = nrm(ks[8], (DEPTH, SSM_GROUPS, SSM_STATE, SSM_GROUP), 2 * SSM_GROUP)
    c_re = nrm(ks[9], (DEPTH, SSM_GROUPS, SSM_GROUP, SSM_STATE), SSM_STATE)
    c_im = nrm(ks[10], (DEPTH, SSM_GROUPS, SSM_GROUP, SSM_STATE), SSM_STATE)
    d_skip = jax.random.normal(ks[11], (DEPTH, SSM_GROUPS, SSM_GROUP), f32)
    w_attn_proj = nrm(ks[12], (DEPTH, ATTN_OUT_WIDTH, D_MODEL), ATTN_OUT_WIDTH)
    w_glu_a = nrm(ks[13], (DEPTH, SSM_WIDTH, D_MODEL), SSM_WIDTH)
    w_glu_b = nrm(ks[14], (DEPTH, SSM_WIDTH, D_MODEL), SSM_WIDTH)
    w_out = nrm(ks[15], (DEPTH, D_MODEL, D_MODEL), D_MODEL)
    g_ffn = 1.0 + 0.05 * jax.random.normal(ks[16], (DEPTH, D_MODEL), f32)
    w_ffn_gate = nrm(ks[17], (DEPTH, D_MODEL, D_FF), D_MODEL)
    w_ffn_up = nrm(ks[18], (DEPTH, D_MODEL, D_FF), D_MODEL)
    w_ffn_down = nrm(ks[19], (DEPTH, D_FF, D_MODEL), D_FF)
    w_ple_gate = nrm(ks[20], (DEPTH, D_MODEL, D_MODEL), D_MODEL)
    w_ple_proj = nrm(ks[21], (DEPTH, PLE_DIM, D_MODEL), PLE_DIM)
    g_final = 1.0 + 0.05 * jax.random.normal(ks[22], (D_MODEL,), f32)
    return {"x": x, "p": p, "positions": positions, "g_mix": g_mix, "w_in": w_in,
            "a_re": a_re, "a_im": a_im, "log_dt": log_dt, "b_re": b_re, "b_im": b_im,
            "c_re": c_re, "c_im": c_im, "d_skip": d_skip, "w_attn_proj": w_attn_proj,
            "w_glu_a": w_glu_a, "w_glu_b": w_glu_b, "w_out": w_out, "g_ffn": g_ffn,
            "w_ffn_gate": w_ffn_gate, "w_ffn_up": w_ffn_up, "w_ffn_down": w_ffn_down,
            "w_ple_gate": w_ple_gate, "w_ple_proj": w_ple_proj, "g_final": g_final}


def _fwd_reference(x, p, positions, g_mix, w_in, a_re, a_im, log_dt, b_re, b_im, c_re, c_im, d_skip,
              w_attn_proj, w_glu_a, w_glu_b, w_out, g_ffn, w_ffn_gate, w_ffn_up, w_ffn_down,
              w_ple_gate, w_ple_proj, g_final):
    h = x
    for i in range(DEPTH):
        h = hybrid_layer(h, p[i], positions, g_mix[i], w_in[i], a_re[i], a_im[i], log_dt[i],
                         b_re[i], b_im[i], c_re[i], c_im[i], d_skip[i], w_attn_proj[i],
                         w_glu_a[i], w_glu_b[i], w_out[i], g_ffn[i], w_ffn_gate[i], w_ffn_up[i],
                         w_ffn_down[i], w_ple_gate[i], w_ple_proj[i])
    return rmsnorm(h, g_final)


import jax as _jax
import jax.numpy as _jnp

TWIN_FORMAT = 'train_step'
FWD_PARAMS = ['x', 'p', 'positions', 'g_mix', 'w_in', 'a_re', 'a_im', 'log_dt', 'b_re', 'b_im', 'c_re', 'c_im', 'd_skip', 'w_attn_proj', 'w_glu_a', 'w_glu_b', 'w_out', 'g_ffn', 'w_ffn_gate', 'w_ffn_up', 'w_ffn_down', 'w_ple_gate', 'w_ple_proj', 'g_final']
TWIN_WEIGHTS = ['g_mix', 'w_in', 'a_re', 'a_im', 'log_dt', 'b_re', 'b_im', 'c_re', 'c_im', 'd_skip', 'w_attn_proj', 'w_glu_a', 'w_glu_b', 'w_out', 'g_ffn', 'w_ffn_gate', 'w_ffn_up', 'w_ffn_down', 'w_ple_gate', 'w_ple_proj', 'g_final']
TWIN_DIFF_INPUT = 'x'
TWIN_INPUTS = ['x', 'p', 'positions', 'g_mix', 'w_in', 'a_re', 'a_im', 'log_dt', 'b_re', 'b_im', 'c_re', 'c_im', 'd_skip', 'w_attn_proj', 'w_glu_a', 'w_glu_b', 'w_out', 'g_ffn', 'w_ffn_gate', 'w_ffn_up', 'w_ffn_down', 'w_ple_gate', 'w_ple_proj', 'g_final', 'loss_target', 'm_g_mix', 'm_w_in', 'm_a_re', 'm_a_im', 'm_log_dt', 'm_b_re', 'm_b_im', 'm_c_re', 'm_c_im', 'm_d_skip', 'm_w_attn_proj', 'm_w_glu_a', 'm_w_glu_b', 'm_w_out', 'm_g_ffn', 'm_w_ffn_gate', 'm_w_ffn_up', 'm_w_ffn_down', 'm_w_ple_gate', 'm_w_ple_proj', 'm_g_final', 'v_g_mix', 'v_w_in', 'v_a_re', 'v_a_im', 'v_log_dt', 'v_b_re', 'v_b_im', 'v_c_re', 'v_c_im', 'v_d_skip', 'v_w_attn_proj', 'v_w_glu_a', 'v_w_glu_b', 'v_w_out', 'v_g_ffn', 'v_w_ffn_gate', 'v_w_ffn_up', 'v_w_ffn_down', 'v_w_ple_gate', 'v_w_ple_proj', 'v_g_final']
TWIN_OUTPUTS = ['loss', 'grad_x', 'grad_g_mix', 'grad_w_in', 'grad_a_re', 'grad_a_im', 'grad_log_dt', 'grad_b_re', 'grad_b_im', 'grad_c_re', 'grad_c_im', 'grad_d_skip', 'grad_w_attn_proj', 'grad_w_glu_a', 'grad_w_glu_b', 'grad_w_out', 'grad_g_ffn', 'grad_w_ffn_gate', 'grad_w_ffn_up', 'grad_w_ffn_down', 'grad_w_ple_gate', 'grad_w_ple_proj', 'grad_g_final', 'delta_g_mix', 'delta_w_in', 'delta_a_re', 'delta_a_im', 'delta_log_dt', 'delta_b_re', 'delta_b_im', 'delta_c_re', 'delta_c_im', 'delta_d_skip', 'delta_w_attn_proj', 'delta_w_glu_a', 'delta_w_glu_b', 'delta_w_out', 'delta_g_ffn', 'delta_w_ffn_gate', 'delta_w_ffn_up', 'delta_w_ffn_down', 'delta_w_ple_gate', 'delta_w_ple_proj', 'delta_g_final', 'new_m_g_mix', 'new_m_w_in', 'new_m_a_re', 'new_m_a_im', 'new_m_log_dt', 'new_m_b_re', 'new_m_b_im', 'new_m_c_re', 'new_m_c_im', 'new_m_d_skip', 'new_m_w_attn_proj', 'new_m_w_glu_a', 'new_m_w_glu_b', 'new_m_w_out', 'new_m_g_ffn', 'new_m_w_ffn_gate', 'new_m_w_ffn_up', 'new_m_w_ffn_down', 'new_m_w_ple_gate', 'new_m_w_ple_proj', 'new_m_g_final', 'new_v_g_mix', 'new_v_w_in', 'new_v_a_re', 'new_v_a_im', 'new_v_log_dt', 'new_v_b_re', 'new_v_b_im', 'new_v_c_re', 'new_v_c_im', 'new_v_d_skip', 'new_v_w_attn_proj', 'new_v_w_glu_a', 'new_v_w_glu_b', 'new_v_w_out', 'new_v_g_ffn', 'new_v_w_ffn_gate', 'new_v_w_ffn_up', 'new_v_w_ffn_down', 'new_v_w_ple_gate', 'new_v_w_ple_proj', 'new_v_g_final']
TWIN_LEAF_KINDS = {'loss': 'loss', 'grad_x': 'grad_x', 'grad_g_mix': 'grad_w', 'grad_w_in': 'grad_w', 'grad_a_re': 'grad_w', 'grad_a_im': 'grad_w', 'grad_log_dt': 'grad_w', 'grad_b_re': 'grad_w', 'grad_b_im': 'grad_w', 'grad_c_re': 'grad_w', 'grad_c_im': 'grad_w', 'grad_d_skip': 'grad_w', 'grad_w_attn_proj': 'grad_w', 'grad_w_glu_a': 'grad_w', 'grad_w_glu_b': 'grad_w', 'grad_w_out': 'grad_w', 'grad_g_ffn': 'grad_w', 'grad_w_ffn_gate': 'grad_w', 'grad_w_ffn_up': 'grad_w', 'grad_w_ffn_down': 'grad_w', 'grad_w_ple_gate': 'grad_w', 'grad_w_ple_proj': 'grad_w', 'grad_g_final': 'grad_w', 'delta_g_mix': 'delta_w', 'delta_w_in': 'delta_w', 'delta_a_re': 'delta_w', 'delta_a_im': 'delta_w', 'delta_log_dt': 'delta_w', 'delta_b_re': 'delta_w', 'delta_b_im': 'delta_w', 'delta_c_re': 'delta_w', 'delta_c_im': 'delta_w', 'delta_d_skip': 'delta_w', 'delta_w_attn_proj': 'delta_w', 'delta_w_glu_a': 'delta_w', 'delta_w_glu_b': 'delta_w', 'delta_w_out': 'delta_w', 'delta_g_ffn': 'delta_w', 'delta_w_ffn_gate': 'delta_w', 'delta_w_ffn_up': 'delta_w', 'delta_w_ffn_down': 'delta_w', 'delta_w_ple_gate': 'delta_w', 'delta_w_ple_proj': 'delta_w', 'delta_g_final': 'delta_w', 'new_m_g_mix': 'new_m', 'new_m_w_in': 'new_m', 'new_m_a_re': 'new_m', 'new_m_a_im': 'new_m', 'new_m_log_dt': 'new_m', 'new_m_b_re': 'new_m', 'new_m_b_im': 'new_m', 'new_m_c_re': 'new_m', 'new_m_c_im': 'new_m', 'new_m_d_skip': 'new_m', 'new_m_w_attn_proj': 'new_m', 'new_m_w_glu_a': 'new_m', 'new_m_w_glu_b': 'new_m', 'new_m_w_out': 'new_m', 'new_m_g_ffn': 'new_m', 'new_m_w_ffn_gate': 'new_m', 'new_m_w_ffn_up': 'new_m', 'new_m_w_ffn_down': 'new_m', 'new_m_w_ple_gate': 'new_m', 'new_m_w_ple_proj': 'new_m', 'new_m_g_final': 'new_m', 'new_v_g_mix': 'new_v', 'new_v_w_in': 'new_v', 'new_v_a_re': 'new_v', 'new_v_a_im': 'new_v', 'new_v_log_dt': 'new_v', 'new_v_b_re': 'new_v', 'new_v_b_im': 'new_v', 'new_v_c_re': 'new_v', 'new_v_c_im': 'new_v', 'new_v_d_skip': 'new_v', 'new_v_w_attn_proj': 'new_v', 'new_v_w_glu_a': 'new_v', 'new_v_w_glu_b': 'new_v', 'new_v_w_out': 'new_v', 'new_v_g_ffn': 'new_v', 'new_v_w_ffn_gate': 'new_v', 'new_v_w_ffn_up': 'new_v', 'new_v_w_ffn_down': 'new_v', 'new_v_w_ple_gate': 'new_v', 'new_v_w_ple_proj': 'new_v', 'new_v_g_final': 'new_v'}


def _forward(args):
    return _fwd_reference(*[args[k] for k in FWD_PARAMS])


def _output_shape():
    def fwd():
        inp = _fwd_setup_inputs(0)
        return _fwd_reference(*[inp[k] for k in FWD_PARAMS])
    out = _jax.eval_shape(fwd)
    return out.shape, out.dtype

N_MICROBATCH = 1
ADAM_LR = 0.001
ADAM_B1 = 0.9
ADAM_B2 = 0.999
ADAM_EPS = 1e-08
ADAM_WD = 0.01
ADAM_STEP = 10
PER_EXAMPLE_BATCH_AXIS = {'x': 0, 'p': 1, 'positions': 0, 'loss_target': 0}
SHARED_INPUTS = []
_WEIGHT_DTYPES = {'g_mix': _jnp.float32, 'w_in': _jnp.float32, 'a_re': _jnp.float32, 'a_im': _jnp.float32, 'log_dt': _jnp.float32, 'b_re': _jnp.float32, 'b_im': _jnp.float32, 'c_re': _jnp.float32, 'c_im': _jnp.float32, 'd_skip': _jnp.float32, 'w_attn_proj': _jnp.float32, 'w_glu_a': _jnp.float32, 'w_glu_b': _jnp.float32, 'w_out': _jnp.float32, 'g_ffn': _jnp.float32, 'w_ffn_gate': _jnp.float32, 'w_ffn_up': _jnp.float32, 'w_ffn_down': _jnp.float32, 'w_ple_gate': _jnp.float32, 'w_ple_proj': _jnp.float32, 'g_final': _jnp.float32}
MOMENT_SCALE = {'g_mix': 1.023371e-01, 'w_in': 3.796221e-02, 'a_re': 8.157405e-03, 'a_im': 7.482336e-03, 'log_dt': 5.860891e+00, 'b_re': 5.334753e-03, 'b_im': 5.243791e-03, 'c_re': 7.358136e-03, 'c_im': 7.361146e-03, 'd_skip': 1.057157e-01, 'w_attn_proj': 3.686167e-02, 'w_glu_a': 7.836800e-02, 'w_glu_b': 2.132087e-02, 'w_out': 7.938297e-02, 'g_ffn': 2.427536e-01, 'w_ffn_gate': 1.044427e-01, 'w_ffn_up': 1.019582e-01, 'w_ffn_down': 1.703203e-01, 'w_ple_gate': 6.974878e-02, 'w_ple_proj': 1.616174e-01, 'g_final': 1.282461e+02}


def _to_microbatches(a, axis):
    t = _jnp.moveaxis(a, axis, 0)
    t = t.reshape((N_MICROBATCH, t.shape[0] // N_MICROBATCH) + t.shape[1:])
    return _jnp.moveaxis(t, 1, axis + 1)


def setup_inputs(seed: int = 0) -> dict:
    inp = _fwd_setup_inputs(seed)
    key = _jax.random.fold_in(_jax.random.key(seed), 7919)
    shape, _ = _output_shape()
    out = dict(inp)
    out["loss_target"] = _jax.random.normal(_jax.random.fold_in(key, 0), shape, _jnp.float32)
    for i, name in enumerate(TWIN_WEIGHTS):
        w = inp[name].astype(_jnp.float32)
        if MOMENT_SCALE is None:
            s = _jnp.sqrt(_jnp.mean(_jnp.square(w)) + 1e-30)
        else:
            s = MOMENT_SCALE[name]
        km, kv = _jax.random.split(_jax.random.fold_in(key, i + 1))
        out[name] = w
        out["m_" + name] = s * _jax.random.normal(km, w.shape, _jnp.float32)
        out["v_" + name] = (s * s) * _jax.random.uniform(kv, w.shape, _jnp.float32, 0.5, 1.5)
    if N_MICROBATCH > 1:
        for name, axis in PER_EXAMPLE_BATCH_AXIS.items():
            out[name] = _to_microbatches(out[name], axis)
    return {'x': out['x'], 'p': out['p'], 'positions': out['positions'], 'g_mix': out['g_mix'], 'w_in': out['w_in'], 'a_re': out['a_re'], 'a_im': out['a_im'], 'log_dt': out['log_dt'], 'b_re': out['b_re'], 'b_im': out['b_im'], 'c_re': out['c_re'], 'c_im': out['c_im'], 'd_skip': out['d_skip'], 'w_attn_proj': out['w_attn_proj'], 'w_glu_a': out['w_glu_a'], 'w_glu_b': out['w_glu_b'], 'w_out': out['w_out'], 'g_ffn': out['g_ffn'], 'w_ffn_gate': out['w_ffn_gate'], 'w_ffn_up': out['w_ffn_up'], 'w_ffn_down': out['w_ffn_down'], 'w_ple_gate': out['w_ple_gate'], 'w_ple_proj': out['w_ple_proj'], 'g_final': out['g_final'], 'loss_target': out['loss_target'], 'm_g_mix': out['m_g_mix'], 'm_w_in': out['m_w_in'], 'm_a_re': out['m_a_re'], 'm_a_im': out['m_a_im'], 'm_log_dt': out['m_log_dt'], 'm_b_re': out['m_b_re'], 'm_b_im': out['m_b_im'], 'm_c_re': out['m_c_re'], 'm_c_im': out['m_c_im'], 'm_d_skip': out['m_d_skip'], 'm_w_attn_proj': out['m_w_attn_proj'], 'm_w_glu_a': out['m_w_glu_a'], 'm_w_glu_b': out['m_w_glu_b'], 'm_w_out': out['m_w_out'], 'm_g_ffn': out['m_g_ffn'], 'm_w_ffn_gate': out['m_w_ffn_gate'], 'm_w_ffn_up': out['m_w_ffn_up'], 'm_w_ffn_down': out['m_w_ffn_down'], 'm_w_ple_gate': out['m_w_ple_gate'], 'm_w_ple_proj': out['m_w_ple_proj'], 'm_g_final': out['m_g_final'], 'v_g_mix': out['v_g_mix'], 'v_w_in': out['v_w_in'], 'v_a_re': out['v_a_re'], 'v_a_im': out['v_a_im'], 'v_log_dt': out['v_log_dt'], 'v_b_re': out['v_b_re'], 'v_b_im': out['v_b_im'], 'v_c_re': out['v_c_re'], 'v_c_im': out['v_c_im'], 'v_d_skip': out['v_d_skip'], 'v_w_attn_proj': out['v_w_attn_proj'], 'v_w_glu_a': out['v_w_glu_a'], 'v_w_glu_b': out['v_w_glu_b'], 'v_w_out': out['v_w_out'], 'v_g_ffn': out['v_g_ffn'], 'v_w_ffn_gate': out['v_w_ffn_gate'], 'v_w_ffn_up': out['v_w_ffn_up'], 'v_w_ffn_down': out['v_w_ffn_down'], 'v_w_ple_gate': out['v_w_ple_gate'], 'v_w_ple_proj': out['v_w_ple_proj'], 'v_g_final': out['v_g_final']}


def _loss(weights, diff, rest, loss_target):
    with _jax.named_scope("forward"):
        args = {**rest, TWIN_DIFF_INPUT: diff, **{k: w.astype(_WEIGHT_DTYPES[k]) for k, w in weights.items()}}
        y = _forward(args)
    with _jax.named_scope("loss_head"):
        err = _jnp.square(y.astype(_jnp.float32) - loss_target)
        return 0.5 * _jnp.sum(_jnp.mean(err, axis=-1)) if err.ndim else 0.5 * err


def _adamw(w, g, m, v):
    m = ADAM_B1 * m + (1.0 - ADAM_B1) * g
    v = ADAM_B2 * v + (1.0 - ADAM_B2) * _jnp.square(g)
    m_hat = m / (1.0 - ADAM_B1 ** ADAM_STEP)
    v_hat = v / (1.0 - ADAM_B2 ** ADAM_STEP)
    delta = -ADAM_LR * (m_hat / (_jnp.sqrt(v_hat) + ADAM_EPS) + ADAM_WD * w)
    return delta, m, v


def reference(x, p, positions, g_mix, w_in, a_re, a_im, log_dt, b_re, b_im, c_re, c_im, d_skip, w_attn_proj, w_glu_a, w_glu_b, w_out, g_ffn, w_ffn_gate, w_ffn_up, w_ffn_down, w_ple_gate, w_ple_proj, g_final, loss_target, m_g_mix, m_w_in, m_a_re, m_a_im, m_log_dt, m_b_re, m_b_im, m_c_re, m_c_im, m_d_skip, m_w_attn_proj, m_w_glu_a, m_w_glu_b, m_w_out, m_g_ffn, m_w_ffn_gate, m_w_ffn_up, m_w_ffn_down, m_w_ple_gate, m_w_ple_proj, m_g_final, v_g_mix, v_w_in, v_a_re, v_a_im, v_log_dt, v_b_re, v_b_im, v_c_re, v_c_im, v_d_skip, v_w_attn_proj, v_w_glu_a, v_w_glu_b, v_w_out, v_g_ffn, v_w_ffn_gate, v_w_ffn_up, v_w_ffn_down, v_w_ple_gate, v_w_ple_proj, v_g_final):
    given = dict(x=x, p=p, positions=positions, g_mix=g_mix, w_in=w_in, a_re=a_re, a_im=a_im, log_dt=log_dt, b_re=b_re, b_im=b_im, c_re=c_re, c_im=c_im, d_skip=d_skip, w_attn_proj=w_attn_proj, w_glu_a=w_glu_a, w_glu_b=w_glu_b, w_out=w_out, g_ffn=g_ffn, w_ffn_gate=w_ffn_gate, w_ffn_up=w_ffn_up, w_ffn_down=w_ffn_down, w_ple_gate=w_ple_gate, w_ple_proj=w_ple_proj, g_final=g_final, loss_target=loss_target, m_g_mix=m_g_mix, m_w_in=m_w_in, m_a_re=m_a_re, m_a_im=m_a_im, m_log_dt=m_log_dt, m_b_re=m_b_re, m_b_im=m_b_im, m_c_re=m_c_re, m_c_im=m_c_im, m_d_skip=m_d_skip, m_w_attn_proj=m_w_attn_proj, m_w_glu_a=m_w_glu_a, m_w_glu_b=m_w_glu_b, m_w_out=m_w_out, m_g_ffn=m_g_ffn, m_w_ffn_gate=m_w_ffn_gate, m_w_ffn_up=m_w_ffn_up, m_w_ffn_down=m_w_ffn_down, m_w_ple_gate=m_w_ple_gate, m_w_ple_proj=m_w_ple_proj, m_g_final=m_g_final, v_g_mix=v_g_mix, v_w_in=v_w_in, v_a_re=v_a_re, v_a_im=v_a_im, v_log_dt=v_log_dt, v_b_re=v_b_re, v_b_im=v_b_im, v_c_re=v_c_re, v_c_im=v_c_im, v_d_skip=v_d_skip, v_w_attn_proj=v_w_attn_proj, v_w_glu_a=v_w_glu_a, v_w_glu_b=v_w_glu_b, v_w_out=v_w_out, v_g_ffn=v_g_ffn, v_w_ffn_gate=v_w_ffn_gate, v_w_ffn_up=v_w_ffn_up, v_w_ffn_down=v_w_ffn_down, v_w_ple_gate=v_w_ple_gate, v_w_ple_proj=v_w_ple_proj, v_g_final=v_g_final)
    weights = {n: given[n] for n in TWIN_WEIGHTS}
    shared = {n: given[n] for n in SHARED_INPUTS}
    per_example = {n: given[n] for n in ['x', 'p', 'positions']}
    grad_fn = _jax.value_and_grad(_loss, argnums=(0, 1))

    def one_microbatch(ex, loss_target):
        ex = dict(ex)
        diff = ex.pop(TWIN_DIFF_INPUT)
        return grad_fn(weights, diff, {**shared, **ex}, loss_target)

    if N_MICROBATCH == 1:
        loss, (grad_w, grad_x) = one_microbatch(per_example, given["loss_target"])
    else:
        def body(carry, xs):
            loss_sum, grad_sum = carry
            l_k, (gw_k, gx_k) = one_microbatch(xs[0], xs[1])
            with _jax.named_scope("update"):
                return (loss_sum + l_k, _jax.tree.map(_jnp.add, grad_sum, gw_k)), gx_k

        init = (_jnp.zeros((), _jnp.float32), _jax.tree.map(_jnp.zeros_like, weights))
        (loss, grad_w), grad_x = _jax.lax.scan(body, init, (per_example, given["loss_target"]))
    with _jax.named_scope("update"):
        delta_w, new_m, new_v = {}, {}, {}
        for n in TWIN_WEIGHTS:
            delta_w[n], new_m[n], new_v[n] = _adamw(weights[n], grad_w[n], given["m_" + n], given["v_" + n])
    return (loss, grad_x, *[grad_w[n] for n in TWIN_WEIGHTS], *[delta_w[n] for n in TWIN_WEIGHTS],
            *[new_m[n] for n in TWIN_WEIGHTS], *[new_v[n] for n in TWIN_WEIGHTS])
```

```python
import math

import jax
import jax.numpy as jnp
from jax import lax
from jax.experimental import pallas as pl
from jax.experimental.pallas import tpu as pltpu

F32 = jnp.float32
BF16 = jnp.bfloat16

D_MODEL = 1024
HEAD_DIM = 128
HEADS_PER_GROUP = 4
GROUP_WIDTH = HEADS_PER_GROUP * HEAD_DIM
GROUP_DILATIONS = (1, 4, 16)
ATTN_BLOCK = 128
ROPE_DIM = 32
ROPE_HALF = 16
ROPE_THETA = 500000.0
SSM_WIDTH = 512
SSM_GROUPS = 32
SSM_GROUP = 16
SSM_STATE = 64
N_STATE = SSM_GROUPS * SSM_STATE
SSM_SUPER = 4
IN_WIDTH = 7168
COL_U = 4608
COL_GA = 5120
COL_GS = 6144
D_FF = 2816
N_CHIPS = 4
D_FF_Q = D_FF // N_CHIPS
PLE_DIM = 256
EPS = 1e-6
ADAM_LR = 0.001
ADAM_B1 = 0.9
ADAM_B2 = 0.999
ADAM_EPS = 1e-08
ADAM_WD = 0.01
ADAM_STEP = 10
NEG_BIG = -1e30
VMEM_LIMIT_BYTES = 56 * 1024 * 1024
MESH = pl.DeviceIdType.MESH

_DIMS = {
    "nn": (((1,), (0,)), ((), ())),
    "nt": (((1,), (1,)), ((), ())),
    "tn": (((0,), (0,)), ((), ())),
}


def _params(n_grid):
    return pltpu.CompilerParams(dimension_semantics=("arbitrary",) * n_grid, vmem_limit_bytes=VMEM_LIMIT_BYTES)


def _sig(v):
    return 1.0 / (1.0 + jnp.exp(-v))


def _dot(a, b, mode):
    return lax.dot_general(a, b, _DIMS[mode], preferred_element_type=F32)


def _mm(name, grid, pairs, mode, outs, epilogue=None, extras=(), acc_outs=(), acc_shape=None):
    gi, gj, gk = grid
    n_p, n_e, n_o, n_a = len(pairs), len(extras), len(outs), len(acc_outs)
    assert not n_a or gj == 1

    def body(*refs):
        pair_refs = refs[: 2 * n_p]
        extra_refs = refs[2 * n_p: 2 * n_p + n_e]
        out_refs = refs[2 * n_p + n_e: 2 * n_p + n_e + n_o]
        sum_refs = refs[2 * n_p + n_e + n_o: 2 * n_p + n_e + n_o + n_a]
        i = pl.program_id(0)
        k = pl.program_id(2)
        part = None
        for t in range(n_p):
            a = pair_refs[2 * t][...].astype(BF16)
            b = pair_refs[2 * t + 1][...].astype(BF16)
            d = _dot(a, b, mode)
            part = d if part is None else part + d

        def finish(acc):
            tiles, sums = epilogue(acc, *[e[...] for e in extra_refs]) if epilogue is not None else ((acc,), ())
            for o_ref, tile in zip(out_refs, tiles):
                o_ref[...] = tile.astype(o_ref.dtype)
            if n_a:
                @pl.when(i == 0)
                def _():
                    for s_ref in sum_refs:
                        s_ref[...] = jnp.zeros_like(s_ref)

                for s_ref, s in zip(sum_refs, sums):
                    s_ref[...] += s

        if gk == 1:
            finish(part)
        else:
            acc_ref = refs[-1]

            @pl.when(k == 0)
            def _():
                acc_ref[...] = part

            @pl.when(k > 0)
            def _():
                acc_ref[...] += part

            @pl.when(k == gk - 1)
            def _():
                finish(acc_ref[...])

    in_specs, args = [], []
    for a, a_block, a_imap, b, b_block, b_imap in pairs:
        in_specs += [pl.BlockSpec(a_block, a_imap), pl.BlockSpec(b_block, b_imap)]
        args += [a, b]
    for e, e_block, e_imap in extras:
        in_specs.append(pl.BlockSpec(e_block, e_imap))
        args.append(e)
    out_shape = [jax.ShapeDtypeStruct(shape, dtype) for shape, dtype, _, _ in outs]
    out_specs = [pl.BlockSpec(block, imap) for _, _, block, imap in outs]
    for shape, dtype in acc_outs:
        out_shape.append(jax.ShapeDtypeStruct(shape, dtype))
        out_specs.append(pl.BlockSpec(shape, lambda i, j, k: (0, 0)))
    scratch = [pltpu.VMEM(acc_shape, F32)] if gk > 1 else []
    res = pl.pallas_call(
        body, name=name, grid=grid, in_specs=in_specs, out_specs=out_specs, out_shape=out_shape,
        scratch_shapes=scratch, compiler_params=_params(3),
    )(*args)
    return res


def _ew(name, grid, ins, outs, fn, acc_outs=()):
    n_i, n_o, n_a = len(ins), len(outs), len(acc_outs)
    ng = len(grid)

    def body(*refs):
        in_refs = refs[:n_i]
        out_refs = refs[n_i: n_i + n_o]
        sum_refs = refs[n_i + n_o:]
        pids = tuple(pl.program_id(a) for a in range(ng))
        tiles, sums = fn(pids, *[r[...] for r in in_refs])
        for o_ref, tile in zip(out_refs, tiles):
            o_ref[...] = tile.astype(o_ref.dtype)
        if n_a:
            first = pids[0] == 0
            for p_ in pids[1:]:
                first = jnp.logical_and(first, p_ == 0)

            @pl.when(first)
            def _():
                for s_ref in sum_refs:
                    s_ref[...] = jnp.zeros_like(s_ref)

            for s_ref, s in zip(sum_refs, sums):
                s_ref[...] += s

    in_specs = [pl.BlockSpec(block, imap) for _, block, imap in ins]
    out_shape = [jax.ShapeDtypeStruct(shape, dtype) for shape, dtype, _, _ in outs]
    out_specs = [pl.BlockSpec(block, imap) for _, _, block, imap in outs]
    for shape, dtype in acc_outs:
        out_shape.append(jax.ShapeDtypeStruct(shape, dtype))
        out_specs.append(pl.BlockSpec(shape, lambda *_, nd=len(shape): (0,) * nd))
    return pl.pallas_call(
        body, name=name, grid=grid, in_specs=in_specs, out_specs=out_specs, out_shape=out_shape,
        compiler_params=_params(ng),
    )(*[a for a, _, _ in ins])


def _rows(tm, width):
    return (tm, width), (lambda i: (i, 0))


def _rms_fwd_tile(h, g):
    r = lax.rsqrt(jnp.mean(h * h, axis=-1, keepdims=True) + EPS)
    return h * r * g


def _rms_bwd_tile(dn, h, g):
    r = lax.rsqrt(jnp.mean(h * h, axis=-1, keepdims=True) + EPS)
    hhat = h * r
    gy = dn * g
    dh = r * (gy - hhat * jnp.mean(gy * hhat, axis=-1, keepdims=True))
    dg = jnp.sum(dn * hhat, axis=0, keepdims=True)
    return dh, dg


def _rope_tables(pos_col, inv_row, tm):
    s = pos_col.shape[0]

    def fn(pids, pos, inv):
        ang = pos * inv
        lane = lax.broadcasted_iota(jnp.int32, ang.shape, 1)
        cs = jnp.where(lane < ROPE_DIM, jnp.cos(ang), 1.0)
        sn = jnp.sin(ang)
        s_lo = jnp.where(lane < ROPE_HALF, -sn, 0.0)
        s_hi = jnp.where(jnp.logical_and(lane >= ROPE_HALF, lane < ROPE_DIM), sn, 0.0)
        return (cs, s_lo, s_hi), ()

    blk, imap = _rows(tm, 128)
    return _ew(
        "rope_tables", (s // tm,),
        [(pos_col, (tm, 1), lambda i: (i, 0)), (inv_row, (1, 128), lambda i: (0, 0))],
        [((s, 128), F32, blk, imap)] * 3, fn,
    )


def _rope(xh, cs, s_lo, s_hi):
    return xh * cs + pltpu.roll(xh, HEAD_DIM - ROPE_HALF, 1) * s_lo + pltpu.roll(xh, ROPE_HALF, 1) * s_hi


def _rope_t(gh, cs, s_lo, s_hi):
    return gh * cs + pltpu.roll(gh * s_lo, ROPE_HALF, 1) + pltpu.roll(gh * s_hi, HEAD_DIM - ROPE_HALF, 1)


def _attn_geometry(length):
    nb = length // ATTN_BLOCK
    gq = min(4, nb)
    assert nb % gq == 0
    return nb, gq, gq * ATTN_BLOCK, nb // gq


def _band_masks():
    qi = lax.broadcasted_iota(jnp.int32, (ATTN_BLOCK, ATTN_BLOCK), 0)
    kj = lax.broadcasted_iota(jnp.int32, (ATTN_BLOCK, ATTN_BLOCK), 1)
    return kj <= qi, kj >= qi


def _attn_fwd(qv, kv, vv, tabs_v, dil):
    length = qv.shape[0]
    nb, gq, rows, ni = _attn_geometry(length)
    scale = 1.0 / math.sqrt(HEAD_DIM)

    def body(q_ref, kc_ref, kp_ref, vc_ref, vp_ref, cc, lc, hc, cp_, lp, hp, o_ref, l_ref):
        i = pl.program_id(1)
        mask_c, mask_p = _band_masks()
        has_prev0 = i > 0
        tc = (cc[...], lc[...], hc[...])
        tp = (cp_[...], lp[...], hp[...])
        for h in range(HEADS_PER_GROUP):
            cols = slice(h * HEAD_DIM, (h + 1) * HEAD_DIM)
            qh = (_rope(q_ref[:, cols].astype(F32), *tc) * scale).astype(BF16)
            kch = _rope(kc_ref[:, cols].astype(F32), *tc).astype(BF16)
            kph = _rope(kp_ref[:, cols].astype(F32), *tp).astype(BF16)
            vch = vc_ref[:, cols]
            vph = vp_ref[:, cols]
            for jj in range(gq):
                rws = slice(jj * ATTN_BLOCK, (jj + 1) * ATTN_BLOCK)
                prv = slice((jj - 1) * ATTN_BLOCK, jj * ATTN_BLOCK)
                qb = qh[rws]
                k_prev = kph if jj == 0 else kch[prv]
                v_prev = vph if jj == 0 else vch[prv]
                s_c = jnp.where(mask_c, _dot(qb, kch[rws], "nt"), NEG_BIG)
                m_p = jnp.logical_and(mask_p, has_prev0) if jj == 0 else mask_p
                s_p = jnp.where(m_p, _dot(qb, k_prev, "nt"), NEG_BIG)
                m = jnp.maximum(jnp.max(s_c, axis=-1, keepdims=True), jnp.max(s_p, axis=-1, keepdims=True))
                p_c = jnp.exp(s_c - m)
                p_p = jnp.exp(s_p - m)
                den = jnp.sum(p_c, axis=-1, keepdims=True) + jnp.sum(p_p, axis=-1, keepdims=True)
                o = _dot(p_c.astype(BF16), vch[rws], "nn") + _dot(p_p.astype(BF16), v_prev, "nn")
                o_ref[rws, cols] = o / den
                l_ref[rws, cols] = jnp.broadcast_to(m + jnp.log(den), (ATTN_BLOCK, HEAD_DIM))

    cur = lambda r, i: (i, r)
    prev = lambda r, i: (jnp.maximum(i * gq - 1, 0), r)
    wide = pl.BlockSpec((rows, GROUP_WIDTH), cur)
    wide_prev = pl.BlockSpec((ATTN_BLOCK, GROUP_WIDTH), prev)
    tab = pl.BlockSpec((rows, HEAD_DIM), cur)
    tab_prev = pl.BlockSpec((ATTN_BLOCK, HEAD_DIM), prev)
    return pl.pallas_call(
        body, name=f"attn_fwd_d{dil}", grid=(dil, ni),
        in_specs=[wide, wide, wide_prev, wide, wide_prev, tab, tab, tab, tab_prev, tab_prev, tab_prev],
        out_specs=[wide, wide],
        out_shape=[jax.ShapeDtypeStruct(qv.shape, F32)] * 2,
        compiler_params=_params(2),
    )(qv, kv, kv, vv, vv, *tabs_v, *tabs_v)


def _attn_bwd(qv, kv, vv, dov, ov, lv, tabs_v, dil):
    length = qv.shape[0]
    nb, gq, rows, ni = _attn_geometry(length)
    scale = 1.0 / math.sqrt(HEAD_DIM)

    def body(qc_ref, qn_ref, kc_ref, kp_ref, vc_ref, vp_ref, doc_ref, don_ref, oc_ref, on_ref, lc_ref, ln_ref,
             cc, lc, hc, cp_, lp, hp, cn, ln, hn, dq_ref, dk_ref, dv_ref):
        i = pl.program_id(1)
        mask_c, mask_p = _band_masks()
        has_prev0 = i > 0
        has_next = i < ni - 1
        tc = (cc[...], lc[...], hc[...])
        tp = (cp_[...], lp[...], hp[...])
        tn = (cn[...], ln[...], hn[...])

        def tile(qb, kb, vb, dob, lb, delta, mask):
            s = _dot(qb, kb, "nt")
            p = jnp.where(mask, jnp.exp(s - lb), 0.0)
            dp = _dot(dob, vb, "nt")
            ds = p * (dp - delta)
            return p, ds

        for h in range(HEADS_PER_GROUP):
            cols = slice(h * HEAD_DIM, (h + 1) * HEAD_DIM)
            q_c = (_rope(qc_ref[:, cols].astype(F32), *tc) * scale).astype(BF16)
            q_n = (_rope(qn_ref[:, cols].astype(F32), *tn) * scale).astype(BF16)
            k_c = _rope(kc_ref[:, cols].astype(F32), *tc).astype(BF16)
            k_p = _rope(kp_ref[:, cols].astype(F32), *tp).astype(BF16)
            v_c = vc_ref[:, cols]
            v_p = vp_ref[:, cols]
            do_c = doc_ref[:, cols]
            do_n = don_ref[:, cols]
            l_c = lc_ref[:, cols]
            l_n = ln_ref[:, cols]
            dl_c = jnp.sum(do_c.astype(F32) * oc_ref[:, cols].astype(F32), axis=-1, keepdims=True)
            dl_n = jnp.sum(do_n.astype(F32) * on_ref[:, cols].astype(F32), axis=-1, keepdims=True)
            dq_blocks, dk_blocks, dv_blocks = [], [None] * gq, [None] * gq

            def add(lst, idx, val):
                lst[idx] = val if lst[idx] is None else lst[idx] + val

            for jj in range(gq):
                rws = slice(jj * ATTN_BLOCK, (jj + 1) * ATTN_BLOCK)
                prv = slice((jj - 1) * ATTN_BLOCK, jj * ATTN_BLOCK)
                qb, dob, lb, dlb = q_c[rws], do_c[rws], l_c[rws], dl_c[rws]
                p, ds = tile(qb, k_c[rws], v_c[rws], dob, lb, dlb, mask_c)
                dsb = ds.astype(BF16)
                dq = _dot(dsb, k_c[rws], "nn")
                add(dk_blocks, jj, _dot(dsb, qb, "tn"))
                add(dv_blocks, jj, _dot(p.astype(BF16), dob, "tn"))
                if jj == 0:
                    p, ds = tile(qb, k_p, v_p, dob, lb, dlb, jnp.logical_and(mask_p, has_prev0))
                    dq = dq + _dot(ds.astype(BF16), k_p, "nn")
                else:
                    p, ds = tile(qb, k_c[prv], v_c[prv], dob, lb, dlb, mask_p)
                    dsb = ds.astype(BF16)
                    dq = dq + _dot(dsb, k_c[prv], "nn")
                    add(dk_blocks, jj - 1, _dot(dsb, qb, "tn"))
                    add(dv_blocks, jj - 1, _dot(p.astype(BF16), dob, "tn"))
                dq_blocks.append(dq)
            last = slice((gq - 1) * ATTN_BLOCK, gq * ATTN_BLOCK)
            p, ds = tile(q_n, k_c[last], v_c[last], do_n, l_n, dl_n, jnp.logical_and(mask_p, has_next))
            add(dk_blocks, gq - 1, _dot(ds.astype(BF16), q_n, "tn"))
            add(dv_blocks, gq - 1, _dot(p.astype(BF16), do_n, "tn"))
            for jj in range(gq):
                rws = slice(jj * ATTN_BLOCK, (jj + 1) * ATTN_BLOCK)
                t_rows = tuple(t[rws] for t in tc)
                dq_ref[rws, cols] = _rope_t(dq_blocks[jj] * scale, *t_rows).astype(dq_ref.dtype)
                dk_ref[rws, cols] = _rope_t(dk_blocks[jj], *t_rows).astype(dk_ref.dtype)
                dv_ref[rws, cols] = dv_blocks[jj].astype(dv_ref.dtype)

    cur = lambda r, i: (i, r)
    prev = lambda r, i: (jnp.maximum(i * gq - 1, 0), r)
    nxt = lambda r, i: (jnp.minimum((i + 1) * gq, nb - 1), r)
    wide = pl.BlockSpec((rows, GROUP_WIDTH), cur)
    wide_prev = pl.BlockSpec((ATTN_BLOCK, GROUP_WIDTH), prev)
    wide_next = pl.BlockSpec((ATTN_BLOCK, GROUP_WIDTH), nxt)
    tab = pl.BlockSpec((rows, HEAD_DIM), cur)
    tab_prev = pl.BlockSpec((ATTN_BLOCK, HEAD_DIM), prev)
    tab_next = pl.BlockSpec((ATTN_BLOCK, HEAD_DIM), nxt)
    return pl.pallas_call(
        body, name=f"attn_bwd_d{dil}", grid=(dil, ni),
        in_specs=[wide, wide_next, wide, wide_prev, wide, wide_prev, wide, wide_next, wide, wide_next, wide, wide_next,
                  tab, tab, tab, tab_prev, tab_prev, tab_prev, tab_next, tab_next, tab_next],
        out_specs=[wide, wide, wide],
        out_shape=[jax.ShapeDtypeStruct(qv.shape, BF16)] * 3,
        compiler_params=_params(2),
    )(qv, qv, kv, kv, vv, vv, dov, dov, ov, ov, lv, lv, *tabs_v, *tabs_v, *tabs_v)


def _to_view(a, dil):
    s, w = a.shape
    return a.reshape(s // dil, dil * w)


def _from_view(a, dil):
    length, dw = a.shape
    return a.reshape(length * dil, dw // dil)


def _discretise(a_re, a_im, log_dt, bt_re, bt_im):
    dt = jnp.exp(log_dt)
    mag = jnp.exp(a_re * dt)
    bar_re = mag * jnp.cos(a_im * dt)
    bar_im = mag * jnp.sin(a_im * dt)
    nr = bar_re - 1.0
    ni = bar_im
    den = a_re * a_re + a_im * a_im
    z_re = (nr * a_re + ni * a_im) / den
    z_im = (ni * a_re - nr * a_im) / den
    bb_re = z_re[:, None, :] * bt_re - z_im[:, None, :] * bt_im
    bb_im = z_re[:, None, :] * bt_im + z_im[:, None, :] * bt_re
    return bar_re, bar_im, bb_re, bb_im


def _ssm_prep(a_re, a_im, log_dt, bt_re, bt_im):
    def body(ar, ai, ld, br, bi, o_lr, o_li, o_br, o_bi):
        lr, li, bbr, bbi = _discretise(ar[...], ai[...], ld[...], br[...], bi[...])
        o_lr[...] = lr
        o_li[...] = li
        o_br[...] = bbr
        o_bi[...] = bbi

    sm = jax.ShapeDtypeStruct((SSM_GROUPS, SSM_STATE), F32)
    bg = jax.ShapeDtypeStruct((SSM_GROUPS, SSM_GROUP, SSM_STATE), F32)
    return pl.pallas_call(body, name="ssm_prep", out_shape=[sm, sm, bg, bg])(a_re, a_im, log_dt, bt_re, bt_im)


def _ssm_param_bwd(a_re, a_im, log_dt, bt_re, bt_im, d_lr, d_li, d_bbr, d_bbi):
    def body(ar, ai, ld, br, bi, g_lr, g_li, g_br, g_bi, o_ar, o_ai, o_ld, o_br, o_bi):
        _, vjp = jax.vjp(_discretise, ar[...], ai[...], ld[...], br[...], bi[...])
        d_ar, d_ai, d_ld, d_br, d_bi = vjp((g_lr[...], g_li[...], g_br[...], g_bi[...]))
        o_ar[...] = d_ar
        o_ai[...] = d_ai
        o_ld[...] = d_ld
        o_br[...] = d_br
        o_bi[...] = d_bi

    sm = jax.ShapeDtypeStruct((SSM_GROUPS, SSM_STATE), F32)
    col = jax.ShapeDtypeStruct((SSM_GROUPS, 1), F32)
    bg = jax.ShapeDtypeStruct((SSM_GROUPS, SSM_GROUP, SSM_STATE), F32)
    return pl.pallas_call(body, name="ssm_param_bwd", out_shape=[sm, sm, col, bg, bg])(
        a_re, a_im, log_dt, bt_re, bt_im, d_lr, d_li, d_bbr, d_bbi)


def _block_diag(t, rows_per, cols_per):
    t4 = t.reshape(SSM_SUPER, 8, rows_per, cols_per)
    eye = jnp.eye(8, dtype=t.dtype)
    return jnp.einsum("bgrc,gh->bgrhc", t4, eye).reshape(SSM_SUPER, 8 * rows_per, 8 * cols_per)


def _block_diag_t(dense, rows_per, cols_per):
    t = dense.reshape(SSM_SUPER, 8, rows_per, 8, cols_per)
    eye = jnp.eye(8, dtype=dense.dtype)
    return jnp.einsum("bgrhc,gh->bgrc", t, eye).reshape(SSM_GROUPS, rows_per, cols_per)


def _gelu(v):
    c = math.sqrt(2.0 / math.pi)
    return 0.5 * v * (1.0 + jnp.tanh(c * (v + 0.044715 * v * v * v)))


def _gelu_grad(v):
    c = math.sqrt(2.0 / math.pi)
    t = jnp.tanh(c * (v + 0.044715 * v * v * v))
    return 0.5 * (1.0 + t) + 0.5 * v * (1.0 - t * t) * c * (1.0 + 3.0 * 0.044715 * v * v)


SUB = 8


def _scan_rows(g_re_ref, g_im_ref, lam_re, lam_im, carry, n_rows, reverse, conj):
    sign = -1.0 if conj else 1.0
    row_id = lax.broadcasted_iota(jnp.int32, (SUB, N_STATE), 0)
    lr = jnp.broadcast_to(lam_re, (SUB, N_STATE))
    li = jnp.broadcast_to(lam_im, (SUB, N_STATE)) * sign

    def tile_step(tt, state):
        sr, si = state
        t8 = (n_rows // SUB - 1 - tt) if reverse else tt
        start = pl.multiple_of(t8 * SUB, SUB)
        g_r = g_re_ref[pl.ds(start, SUB), :]
        g_i = g_im_ref[pl.ds(start, SUB), :]
        out_r, out_i = g_r, g_i
        order = range(SUB - 1, -1, -1) if reverse else range(SUB)
        for j in order:
            gr_j = jnp.broadcast_to(g_r[j:j + 1, :], (SUB, N_STATE))
            gi_j = jnp.broadcast_to(g_i[j:j + 1, :], (SUB, N_STATE))
            nr = lr * sr - li * si + gr_j
            ni = lr * si + li * sr + gi_j
            sr, si = nr, ni
            out_r = jnp.where(row_id == j, sr, out_r)
            out_i = jnp.where(row_id == j, si, out_i)
        g_re_ref[pl.ds(start, SUB), :] = out_r
        g_im_ref[pl.ds(start, SUB), :] = out_i
        return sr, si

    return lax.fori_loop(0, n_rows // SUB, tile_step, carry)


def _ssm_fwd(z, b_re, b_im, c_re, c_im, lam_re, lam_im, d_skip, chunk):
    s = z.shape[0]

    def body(u_ref, bre, bim, cre, cim, lre, lim, dsk, hre_ref, him_ref, ys_ref, yg_ref, car_re, car_im):
        i = pl.program_id(0)

        @pl.when(i == 0)
        def _():
            car_re[...] = jnp.zeros_like(car_re)
            car_im[...] = jnp.zeros_like(car_im)

        u = u_ref[...]
        for b in range(SSM_SUPER):
            ub = u[:, b * 128:(b + 1) * 128]
            st = slice(b * 512, (b + 1) * 512)
            hre_ref[:, st] = _dot(ub, bre[b], "nn")
            him_ref[:, st] = _dot(ub, bim[b], "nn")
        sr, si = _scan_rows(hre_ref, him_ref, lre[...], lim[...], (car_re[...], car_im[...]), chunk, False, False)
        car_re[...] = sr
        car_im[...] = si
        uf = u.astype(F32)
        for b in range(SSM_SUPER):
            st = slice(b * 512, (b + 1) * 512)
            ch = slice(b * 128, (b + 1) * 128)
            y = _dot(hre_ref[:, st].astype(BF16), cre[b], "nn") - _dot(him_ref[:, st].astype(BF16), cim[b], "nn")
            y = y + dsk[:, ch] * uf[:, ch]
            ys_ref[:, ch] = y
            yg_ref[:, ch] = _gelu(y).astype(BF16)

    full3 = lambda i: (0, 0, 0)
    full2 = lambda i: (0, 0)
    row = lambda i: (i, 0)
    u_col = COL_U // SSM_WIDTH
    return pl.pallas_call(
        body, name="ssm_fwd", grid=(s // chunk,),
        in_specs=[pl.BlockSpec((chunk, SSM_WIDTH), lambda i: (i, u_col)),
                  pl.BlockSpec((SSM_SUPER, 128, 512), full3), pl.BlockSpec((SSM_SUPER, 128, 512), full3),
                  pl.BlockSpec((SSM_SUPER, 512, 128), full3), pl.BlockSpec((SSM_SUPER, 512, 128), full3),
                  pl.BlockSpec((1, N_STATE), full2), pl.BlockSpec((1, N_STATE), full2), pl.BlockSpec((1, SSM_WIDTH), full2)],
        out_specs=[pl.BlockSpec((chunk, N_STATE), row), pl.BlockSpec((chunk, N_STATE), row),
                   pl.BlockSpec((chunk, SSM_WIDTH), row), pl.BlockSpec((chunk, SSM_WIDTH), row)],
        out_shape=[jax.ShapeDtypeStruct((s, N_STATE), F32), jax.ShapeDtypeStruct((s, N_STATE), F32),
                   jax.ShapeDtypeStruct((s, SSM_WIDTH), F32), jax.ShapeDtypeStruct((s, SSM_WIDTH), BF16)],
        scratch_shapes=[pltpu.VMEM((SUB, N_STATE), F32), pltpu.VMEM((SUB, N_STATE), F32)],
        compiler_params=_params(1),
    )(z, b_re, b_im, c_re, c_im, lam_re, lam_im, d_skip)


def _ssm_bwd(dys, z, h_re, h_im, b_re, b_im, c_re, c_im, lam_re, lam_im, d_skip, chunk):
    s = z.shape[0]
    n_chunks = s // chunk

    def body(dy_ref, u_ref, hre_ref, him_ref, hpr_ref, hpi_ref, bre, bim, cre, cim, lre, lim, dsk,
             du_ref, dlr_ref, dli_ref, dbr_ref, dbi_ref, dcr_ref, dci_ref, dd_ref, are, aim, car_re, car_im):
        i = pl.program_id(0)
        n = n_chunks - 1 - i

        @pl.when(i == 0)
        def _():
            car_re[...] = jnp.zeros_like(car_re)
            car_im[...] = jnp.zeros_like(car_im)
            for r in (dlr_ref, dli_ref, dbr_ref, dbi_ref, dcr_ref, dci_ref, dd_ref):
                r[...] = jnp.zeros_like(r)

        dy = dy_ref[...]
        dyb = dy.astype(BF16)
        u = u_ref[...]
        for b in range(SSM_SUPER):
            ch = slice(b * 128, (b + 1) * 128)
            st = slice(b * 512, (b + 1) * 512)
            are[:, st] = _dot(dyb[:, ch], cre[b], "nt")
            aim[:, st] = -_dot(dyb[:, ch], cim[b], "nt")
        sr, si = _scan_rows(are, aim, lre[...], lim[...], (car_re[...], car_im[...]), chunk, True, True)
        car_re[...] = sr
        car_im[...] = si
        row_id = lax.broadcasted_iota(jnp.int32, (chunk, N_STATE), 0)
        top_scale = jnp.where(n > 0, 1.0, 0.0)
        h_r = hre_ref[...]
        h_i = him_ref[...]
        hp_r = jnp.where(row_id == 0, hpr_ref[SUB - 1:SUB, :] * top_scale, pltpu.roll(h_r, 1, 0))
        hp_i = jnp.where(row_id == 0, hpi_ref[SUB - 1:SUB, :] * top_scale, pltpu.roll(h_i, 1, 0))
        a_r = are[...]
        a_i = aim[...]
        dlr_ref[...] += jnp.sum(a_r * hp_r + a_i * hp_i, axis=0, keepdims=True)
        dli_ref[...] += jnp.sum(a_i * hp_r - a_r * hp_i, axis=0, keepdims=True)
        dd_ref[...] += jnp.sum(dy * u.astype(F32), axis=0, keepdims=True)
        a_rb = a_r.astype(BF16)
        a_ib = a_i.astype(BF16)
        h_rb = h_r.astype(BF16)
        h_ib = h_i.astype(BF16)
        for b in range(SSM_SUPER):
            ch = slice(b * 128, (b + 1) * 128)
            st = slice(b * 512, (b + 1) * 512)
            dbr_ref[b] += _dot(u[:, ch], a_rb[:, st], "tn")
            dbi_ref[b] += _dot(u[:, ch], a_ib[:, st], "tn")
            dcr_ref[b] += _dot(h_rb[:, st], dyb[:, ch], "tn")
            dci_ref[b] += -_dot(h_ib[:, st], dyb[:, ch], "tn")
            du = _dot(a_rb[:, st], bre[b], "nt") + _dot(a_ib[:, st], bim[b], "nt") + dsk[:, ch] * dy[:, ch]
            du_ref[:, ch] = du.astype(du_ref.dtype)

    full3 = lambda i: (0, 0, 0)
    full2 = lambda i: (0, 0)
    rev = lambda i: (n_chunks - 1 - i, 0)
    above = lambda i: (jnp.maximum((n_chunks - 1 - i) * (chunk // SUB) - 1, 0), 0)
    u_col = COL_U // SSM_WIDTH
    b_spec = pl.BlockSpec((SSM_SUPER, 128, 512), full3)
    c_spec = pl.BlockSpec((SSM_SUPER, 512, 128), full3)
    vec = pl.BlockSpec((1, N_STATE), full2)
    return pl.pallas_call(
        body, name="ssm_bwd", grid=(n_chunks,),
        in_specs=[pl.BlockSpec((chunk, SSM_WIDTH), rev),
                  pl.BlockSpec((chunk, SSM_WIDTH), lambda i: (n_chunks - 1 - i, u_col)),
                  pl.BlockSpec((chunk, N_STATE), rev), pl.BlockSpec((chunk, N_STATE), rev),
                  pl.BlockSpec((SUB, N_STATE), above), pl.BlockSpec((SUB, N_STATE), above),
                  b_spec, b_spec, c_spec, c_spec, vec, vec, pl.BlockSpec((1, SSM_WIDTH), full2)],
        out_specs=[pl.BlockSpec((chunk, SSM_WIDTH), rev), vec, vec, b_spec, b_spec, c_spec, c_spec,
                   pl.BlockSpec((1, SSM_WIDTH), full2)],
        out_shape=[jax.ShapeDtypeStruct((s, SSM_WIDTH), BF16),
                   jax.ShapeDtypeStruct((1, N_STATE), F32), jax.ShapeDtypeStruct((1, N_STATE), F32),
                   jax.ShapeDtypeStruct((SSM_SUPER, 128, 512), F32), jax.ShapeDtypeStruct((SSM_SUPER, 128, 512), F32),
                   jax.ShapeDtypeStruct((SSM_SUPER, 512, 128), F32), jax.ShapeDtypeStruct((SSM_SUPER, 512, 128), F32),
                   jax.ShapeDtypeStruct((1, SSM_WIDTH), F32)],
        scratch_shapes=[pltpu.VMEM((chunk, N_STATE), F32), pltpu.VMEM((chunk, N_STATE), F32),
                        pltpu.VMEM((SUB, N_STATE), F32), pltpu.VMEM((SUB, N_STATE), F32)],
        compiler_params=_params(1),
    )(dys, z, h_re, h_im, h_re, h_im, b_re, b_im, c_re, c_im, lam_re, lam_im, d_skip)


def _local_step(x, p, pos, tgt, sm, wts):
    s = x.shape[0]
    tm = min(512, s)
    ts = min(1024, s)
    chunk = min(256, s)
    ni = s // tm
    nk = s // ts
    w_in, w_ap, w_ga, w_gb, w_out, w_fg, w_fu, w_fd, w_pg, w_pp = (
        wts[k] for k in ("w_in", "w_attn_proj", "w_glu_a", "w_glu_b", "w_out", "w_ffn_gate", "w_ffn_up", "w_ffn_down",
                         "w_ple_gate", "w_ple_proj"))
    w_out2 = w_out.reshape(D_MODEL, D_MODEL)
    w_pg2 = w_pg.reshape(D_MODEL, D_MODEL)
    g_mix, g_ffn, g_final = sm["g_mix"], sm["g_ffn"], sm["g_final"]
    rowblk, rowmap = _rows(tm, D_MODEL)
    vec1k = ((1, D_MODEL), lambda *_: (0, 0))

    (n1,) = _ew("rms_mix", (ni,), [(x, rowblk, rowmap), (g_mix, *vec1k)], [((s, D_MODEL), BF16, rowblk, rowmap)],
                lambda pids, h, g: ((_rms_fwd_tile(h, g),), ()))

    half_in = IN_WIDTH // 8
    (z,) = _mm("in_proj", (ni, 8, 1),
               [(n1, (tm, D_MODEL), lambda i, j, k: (i, 0), w_in, (None, D_MODEL, half_in), lambda i, j, k: (j // 2, 0, j % 2))],
               "nn", [((s, IN_WIDTH), BF16, (tm, half_in), lambda i, j, k: (i, j))])

    inv = ROPE_THETA ** (-jnp.arange(ROPE_HALF, dtype=F32) * 2.0 / ROPE_DIM)
    inv_row = jnp.concatenate([inv, inv, jnp.zeros((HEAD_DIM - ROPE_DIM,), F32)]).reshape(1, HEAD_DIM)
    tabs = _rope_tables(pos.astype(F32).reshape(s, 1), inv_row, tm)

    views, outs_g, lses_g = [], [], []
    for gi, dil in enumerate(GROUP_DILATIONS):
        q_g = _to_view(z[:, gi * GROUP_WIDTH:(gi + 1) * GROUP_WIDTH], dil)
        k_g = _to_view(z[:, 1536 + gi * GROUP_WIDTH:1536 + (gi + 1) * GROUP_WIDTH], dil)
        v_g = _to_view(z[:, 3072 + gi * GROUP_WIDTH:3072 + (gi + 1) * GROUP_WIDTH], dil)
        tabs_v = tuple(_to_view(t, dil) for t in tabs)
        views.append((q_g, k_g, v_g, tabs_v))
        o_g, l_g = _attn_fwd(q_g, k_g, v_g, tabs_v, dil)
        outs_g.append(_from_view(o_g, dil))
        lses_g.append(_from_view(l_g, dil))

    def merge_fn(pids, o0, o1, o2, l0, l1, l2):
        m = jnp.maximum(jnp.maximum(l0, l1), l2)
        e0, e1, e2 = jnp.exp(l0 - m), jnp.exp(l1 - m), jnp.exp(l2 - m)
        den = e0 + e1 + e2
        return ((e0 * o0 + e1 * o1 + e2 * o2) / den, m + jnp.log(den)), ()

    gblk, gmap = _rows(tm, GROUP_WIDTH)
    attn, lse = _ew("attn_merge", (ni,), [(a, gblk, gmap) for a in outs_g + lses_g],
                    [((s, GROUP_WIDTH), BF16, gblk, gmap), ((s, GROUP_WIDTH), F32, gblk, gmap)], merge_fn)

    def proj512(name, a, w):
        return _mm(name, (ni, N_CHIPS, 1),
                   [(a, (tm, GROUP_WIDTH), lambda i, j, k: (i, 0), w, (None, GROUP_WIDTH, 256), lambda i, j, k: (j, 0, 0))],
                   "nn", [((s, D_MODEL), BF16, (tm, 256), lambda i, j, k: (i, j))])[0]

    attn_d = proj512("attn_proj", attn, w_ap)

    bt_re = jnp.transpose(sm["b_re"], (0, 2, 1))
    bt_im = jnp.transpose(sm["b_im"], (0, 2, 1))
    log_dt_col = sm["log_dt"].reshape(SSM_GROUPS, 1)
    lam_re, lam_im, bbt_re, bbt_im = _ssm_prep(sm["a_re"], sm["a_im"], log_dt_col, bt_re, bt_im)
    b_re_m = _block_diag(bbt_re, SSM_GROUP, SSM_STATE).astype(BF16)
    b_im_m = _block_diag(bbt_im, SSM_GROUP, SSM_STATE).astype(BF16)
    c_re_m = _block_diag(jnp.transpose(sm["c_re"], (0, 2, 1)), SSM_STATE, SSM_GROUP).astype(BF16)
    c_im_m = _block_diag(jnp.transpose(sm["c_im"], (0, 2, 1)), SSM_STATE, SSM_GROUP).astype(BF16)
    lam_re_row = lam_re.reshape(1, N_STATE)
    lam_im_row = lam_im.reshape(1, N_STATE)
    d_skip_row = sm["d_skip"].reshape(1, SSM_WIDTH)
    h_re, h_im, ys, yg = _ssm_fwd(z, b_re_m, b_im_m, c_re_m, c_im_m, lam_re_row, lam_im_row, d_skip_row, chunk)

    pa = proj512("glu_a", yg, w_ga)
    pb = proj512("glu_b", yg, w_gb)

    ga_blk = ((tm, D_MODEL), lambda i: (i, COL_GA // D_MODEL))
    gs_blk = ((tm, D_MODEL), lambda i: (i, COL_GS // D_MODEL))

    def mix_fn(pids, ga, gs, ad, a, b):
        ga, gs, ad, a, b = (t.astype(F32) for t in (ga, gs, ad, a, b))
        return (_sig(ga) * ad + _sig(gs) * (a * _sig(b)),), ()

    (mix,) = _ew("gate_mix", (ni,), [(z, *ga_blk), (z, *gs_blk), (attn_d, rowblk, rowmap), (pa, rowblk, rowmap),
                                     (pb, rowblk, rowmap)], [((s, D_MODEL), BF16, rowblk, rowmap)], mix_fn)

    def out_epi(acc, xr, g):
        h1 = acc + xr
        return (h1, _rms_fwd_tile(h1, g)), ()

    m3 = lambda i, j, k: (i, 0)
    w3 = lambda i, j, k: (0, 0)
    h1, n2 = _mm("out_proj", (ni, 1, 1), [(mix, (tm, D_MODEL), m3, w_out2, (D_MODEL, D_MODEL), w3)], "nn",
                 [((s, D_MODEL), F32, (tm, D_MODEL), m3), ((s, D_MODEL), BF16, (tm, D_MODEL), m3)],
                 epilogue=out_epi, extras=[(x, (tm, D_MODEL), m3), (g_ffn, (1, D_MODEL), w3)])

    ffq = (None, tm, D_FF_Q)
    ffq_map = lambda i, j, k: (j, i, 0)

    def ffn_in(name, w):
        return _mm(name, (ni, N_CHIPS, 1),
                   [(n2, (tm, D_MODEL), m3, w, (None, D_MODEL, D_FF_Q), lambda i, j, k: (j, 0, 0))], "nn",
                   [((N_CHIPS, s, D_FF_Q), BF16, ffq, ffq_map)])[0]

    gate = ffn_in("ffn_gate", w_fg)
    up = ffn_in("ffn_up", w_fu)

    def act_fn(pids, gt, u_):
        gt, u_ = gt.astype(F32), u_.astype(F32)
        return (gt * _sig(gt) * u_,), ()

    ff2 = ((None, tm, D_FF_Q), lambda j, i: (j, i, 0))
    (act,) = _ew("ffn_act", (N_CHIPS, ni), [(gate, *ff2), (up, *ff2)], [((N_CHIPS, s, D_FF_Q), BF16, *ff2)], act_fn)

    (h2,) = _mm("ffn_down", (ni, 1, N_CHIPS),
                [(act, ffq, lambda i, j, k: (k, i, 0), w_fd, (None, D_FF_Q, D_MODEL), lambda i, j, k: (k, 0, 0))], "nn",
                [((s, D_MODEL), F32, (tm, D_MODEL), m3)], epilogue=lambda acc, hr: ((acc + hr,), ()),
                extras=[(h1, (tm, D_MODEL), m3)], acc_shape=(tm, D_MODEL))

    (pp,) = _mm("ple_proj", (ni, N_CHIPS, 1),
                [(p, (tm, PLE_DIM), m3, w_pp, (None, PLE_DIM, 256), lambda i, j, k: (j, 0, 0))], "nn",
                [((s, D_MODEL), BF16, (tm, 256), lambda i, j, k: (i, j))])

    def ple_epi(acc, hr, ppr):
        return (acc, hr + _sig(acc) * ppr.astype(F32)), ()

    gl, h3 = _mm("ple_gate", (ni, 1, 1), [(h2, (tm, D_MODEL), m3, w_pg2, (D_MODEL, D_MODEL), w3)], "nn",
                 [((s, D_MODEL), BF16, (tm, D_MODEL), m3), ((s, D_MODEL), F32, (tm, D_MODEL), m3)],
                 epilogue=ple_epi, extras=[(h2, (tm, D_MODEL), m3), (pp, (tm, D_MODEL), m3)])

    def head_fn(pids, h, t, g):
        r = lax.rsqrt(jnp.mean(h * h, axis=-1, keepdims=True) + EPS)
        hhat = h * r
        diff = hhat * g - t
        loss = 0.5 * jnp.sum(jnp.mean(diff * diff, axis=-1, keepdims=True))
        dy = diff * (1.0 / D_MODEL)
        gy = dy * g
        dh = r * (gy - hhat * jnp.mean(gy * hhat, axis=-1, keepdims=True))
        return (dh,), (jnp.full((SUB, 128), loss, F32), jnp.sum(dy * hhat, axis=0, keepdims=True))

    dh3, loss_acc, dg_final = _ew("loss_head", (ni,), [(h3, rowblk, rowmap), (tgt, rowblk, rowmap), (g_final, *vec1k)],
                                  [((s, D_MODEL), F32, rowblk, rowmap)], head_fn,
                                  acc_outs=[((SUB, 128), F32), ((1, D_MODEL), F32)])

    def ple_bwd_fn(pids, dh, g_, ppr):
        sg = _sig(g_.astype(F32))
        return (dh * ppr.astype(F32) * sg * (1.0 - sg), dh * sg), ()

    dgl, dpp = _ew("ple_bwd", (ni,), [(dh3, rowblk, rowmap), (gl, rowblk, rowmap), (pp, rowblk, rowmap)],
                   [((s, D_MODEL), BF16, rowblk, rowmap)] * 2, ple_bwd_fn)

    def wgrad(name, a, a_block, a_imap, b, b_block, b_imap, out_shape, out_block, out_imap, nj, acc_shape):
        return _mm(name, (1, nj, nk), [(a, a_block, a_imap, b, b_block, b_imap)], "tn",
                   [(out_shape, F32, out_block, out_imap)], acc_shape=acc_shape)[0]

    tk0 = lambda i, j, k: (k, 0)
    tkj = lambda i, j, k: (k, j)
    d_w_pp = wgrad("d_ple_proj", p, (ts, PLE_DIM), tk0, dpp, (ts, 256), tkj, (N_CHIPS, PLE_DIM, 256),
                   (None, PLE_DIM, 256), lambda i, j, k: (j, 0, 0), N_CHIPS, (PLE_DIM, 256))
    d_w_pg = wgrad("d_ple_gate", h2, (ts, D_MODEL), tk0, dgl, (ts, D_MODEL), tk0, (D_MODEL, D_MODEL),
                   (D_MODEL, D_MODEL), w3, 1, (D_MODEL, D_MODEL))

    (dh2,) = _mm("ple_gate_bwd", (ni, 1, 1), [(dgl, (tm, D_MODEL), m3, w_pg2, (D_MODEL, D_MODEL), w3)], "nt",
                 [((s, D_MODEL), F32, (tm, D_MODEL), m3)], epilogue=lambda acc, d_: ((acc + d_,), ()),
                 extras=[(dh3, (tm, D_MODEL), m3)])

    def ffn_bwd_epi(acc, gt, u_):
        gt, u_ = gt.astype(F32), u_.astype(F32)
        sg = _sig(gt)
        return (acc * u_ * (sg * (1.0 + gt * (1.0 - sg))), acc * gt * sg), ()

    dgate, dup = _mm("ffn_down_bwd", (ni, N_CHIPS, 1),
                     [(dh2, (tm, D_MODEL), m3, w_fd, (None, D_FF_Q, D_MODEL), lambda i, j, k: (j, 0, 0))], "nt",
                     [((N_CHIPS, s, D_FF_Q), BF16, ffq, ffq_map)] * 2, epilogue=ffn_bwd_epi,
                     extras=[(gate, ffq, ffq_map), (up, ffq, ffq_map)])

    ffq_t = (None, ts, D_FF_Q)
    ffq_tmap = lambda i, j, k: (j, k, 0)
    blk_j = lambda i, j, k: (j, 0, 0)
    d_w_fd = wgrad("d_ffn_down", act, ffq_t, ffq_tmap, dh2, (ts, D_MODEL), tk0, (N_CHIPS, D_FF_Q, D_MODEL),
                   (None, D_FF_Q, D_MODEL), blk_j, N_CHIPS, (D_FF_Q, D_MODEL))
    d_w_fg = wgrad("d_ffn_gate", n2, (ts, D_MODEL), tk0, dgate, ffq_t, ffq_tmap, (N_CHIPS, D_MODEL, D_FF_Q),
                   (None, D_MODEL, D_FF_Q), blk_j, N_CHIPS, (D_MODEL, D_FF_Q))
    d_w_fu = wgrad("d_ffn_up", n2, (ts, D_MODEL), tk0, dup, ffq_t, ffq_tmap, (N_CHIPS, D_MODEL, D_FF_Q),
                   (None, D_MODEL, D_FF_Q), blk_j, N_CHIPS, (D_MODEL, D_FF_Q))

    def norm_bwd_epi(acc, h, d_res, g):
        dh, dg = _rms_bwd_tile(acc, h, g)
        return (d_res + dh,), (dg,)

    ffq_k = lambda i, j, k: (k, i, 0)
    blk_k = lambda i, j, k: (k, 0, 0)
    dh1, dg_ffn = _mm("ffn_in_bwd", (ni, 1, N_CHIPS),
                      [(dgate, ffq, ffq_k, w_fg, (None, D_MODEL, D_FF_Q), blk_k),
                       (dup, ffq, ffq_k, w_fu, (None, D_MODEL, D_FF_Q), blk_k)], "nt",
                      [((s, D_MODEL), F32, (tm, D_MODEL), m3)], epilogue=norm_bwd_epi,
                      extras=[(h1, (tm, D_MODEL), m3), (dh2, (tm, D_MODEL), m3), (g_ffn, (1, D_MODEL), w3)],
                      acc_outs=[((1, D_MODEL), F32)], acc_shape=(tm, D_MODEL))

    (dmix,) = _mm("out_proj_bwd", (ni, 1, 1), [(dh1, (tm, D_MODEL), m3, w_out2, (D_MODEL, D_MODEL), w3)], "nt",
                  [((s, D_MODEL), BF16, (tm, D_MODEL), m3)])
    d_w_out = wgrad("d_out_proj", mix, (ts, D_MODEL), tk0, dh1, (ts, D_MODEL), tk0, (D_MODEL, D_MODEL),
                    (D_MODEL, D_MODEL), w3, 1, (D_MODEL, D_MODEL))

    def mix_bwd_fn(pids, dm, ga, gs, ad, a, b):
        dm, ga, gs, ad, a, b = (t.astype(F32) for t in (dm, ga, gs, ad, a, b))
        s_a, s_s, s_b = _sig(ga), _sig(gs), _sig(b)
        d_ssm = dm * s_s
        return (dm * ad * s_a * (1.0 - s_a), dm * (a * s_b) * s_s * (1.0 - s_s), dm * s_a, d_ssm * s_b,
                d_ssm * a * s_b * (1.0 - s_b)), ()

    dga, dgs, dattn_d, dpa, dpb = _ew(
        "gate_mix_bwd", (ni,),
        [(dmix, rowblk, rowmap), (z, *ga_blk), (z, *gs_blk), (attn_d, rowblk, rowmap), (pa, rowblk, rowmap),
         (pb, rowblk, rowmap)], [((s, D_MODEL), BF16, rowblk, rowmap)] * 5, mix_bwd_fn)

    def wgrad512(name, a, dy_):
        return wgrad(name, a, (ts, GROUP_WIDTH), tk0, dy_, (ts, 256), tkj, (N_CHIPS, GROUP_WIDTH, 256),
                     (None, GROUP_WIDTH, 256), blk_j, N_CHIPS, (GROUP_WIDTH, 256))

    d_w_ap = wgrad512("d_attn_proj", attn, dattn_d)
    d_w_ga = wgrad512("d_glu_a", yg, dpa)
    d_w_gb = wgrad512("d_glu_b", yg, dpb)

    ik = lambda i, j, k: (i, k)
    (dattn,) = _mm("attn_proj_bwd", (ni, 1, N_CHIPS),
                   [(dattn_d, (tm, 256), ik, w_ap, (None, GROUP_WIDTH, 256), blk_k)], "nt",
                   [((s, GROUP_WIDTH), BF16, (tm, GROUP_WIDTH), m3)], acc_shape=(tm, GROUP_WIDTH))

    (dys,) = _mm("glu_bwd", (ni, 1, N_CHIPS),
                 [(dpa, (tm, 256), ik, w_ga, (None, GROUP_WIDTH, 256), blk_k),
                  (dpb, (tm, 256), ik, w_gb, (None, GROUP_WIDTH, 256), blk_k)], "nt",
                 [((s, GROUP_WIDTH), F32, (tm, GROUP_WIDTH), m3)],
                 epilogue=lambda acc, y_: ((acc * _gelu_grad(y_),), ()),
                 extras=[(ys, (tm, GROUP_WIDTH), m3)], acc_shape=(tm, GROUP_WIDTH))

    du, d_lr, d_li, d_bre, d_bim, d_cre, d_cim, d_dskip = _ssm_bwd(
        dys, z, h_re, h_im, b_re_m, b_im_m, c_re_m, c_im_m, lam_re_row, lam_im_row, d_skip_row, chunk)

    dq_parts, dk_parts, dv_parts = [], [], []
    for gi, dil in enumerate(GROUP_DILATIONS):
        q_g, k_g, v_g, tabs_v = views[gi]
        dq_g, dk_g, dv_g = _attn_bwd(q_g, k_g, v_g, _to_view(dattn, dil), _to_view(attn, dil), _to_view(lse, dil), tabs_v, dil)
        dq_parts.append(_from_view(dq_g, dil))
        dk_parts.append(_from_view(dk_g, dil))
        dv_parts.append(_from_view(dv_g, dil))
    dz = jnp.concatenate(dq_parts + dk_parts + dv_parts + [du, dga, dgs], axis=1)

    kb = lambda i, j, k: (k // 2, 0, k % 2)
    grad_x, dg_mix = _mm("in_proj_bwd", (ni, 1, 8), [(dz, (tm, half_in), ik, w_in, (None, D_MODEL, half_in), kb)], "nt",
                         [((s, D_MODEL), F32, (tm, D_MODEL), m3)], epilogue=norm_bwd_epi,
                         extras=[(x, (tm, D_MODEL), m3), (dh1, (tm, D_MODEL), m3), (g_mix, (1, D_MODEL), w3)],
                         acc_outs=[((1, D_MODEL), F32)], acc_shape=(tm, D_MODEL))
    d_w_in = wgrad("d_in_proj", n1, (ts, D_MODEL), tk0, dz, (ts, half_in), tkj, (N_CHIPS, D_MODEL, IN_WIDTH // N_CHIPS),
                   (None, D_MODEL, half_in), lambda i, j, k: (j // 2, 0, j % 2), 8, (D_MODEL, half_in))

    d_bbt_re = _block_diag_t(d_bre, SSM_GROUP, SSM_STATE)
    d_bbt_im = _block_diag_t(d_bim, SSM_GROUP, SSM_STATE)
    d_a_re, d_a_im, d_log_dt, d_bt_re, d_bt_im = _ssm_param_bwd(
        sm["a_re"], sm["a_im"], log_dt_col, bt_re, bt_im,
        d_lr.reshape(SSM_GROUPS, SSM_STATE), d_li.reshape(SSM_GROUPS, SSM_STATE), d_bbt_re, d_bbt_im)
    small = {
        "g_mix": dg_mix, "a_re": d_a_re, "a_im": d_a_im, "log_dt": d_log_dt,
        "b_re": jnp.transpose(d_bt_re, (0, 2, 1)), "b_im": jnp.transpose(d_bt_im, (0, 2, 1)),
        "c_re": jnp.transpose(_block_diag_t(d_cre, SSM_STATE, SSM_GROUP), (0, 2, 1)),
        "c_im": jnp.transpose(_block_diag_t(d_cim, SSM_STATE, SSM_GROUP), (0, 2, 1)),
        "d_skip": d_dskip, "g_ffn": dg_ffn, "g_final": dg_final,
    }
    big = {
        "w_in": d_w_in, "w_attn_proj": d_w_ap, "w_glu_a": d_w_ga, "w_glu_b": d_w_gb,
        "w_out": d_w_out.reshape(N_CHIPS, D_MODEL // N_CHIPS, D_MODEL), "w_ffn_gate": d_w_fg, "w_ffn_up": d_w_fu,
        "w_ffn_down": d_w_fd, "w_ple_gate": d_w_pg.reshape(N_CHIPS, D_MODEL // N_CHIPS, D_MODEL), "w_ple_proj": d_w_pp,
    }
    return loss_acc[0, 0], grad_x, big, small


BIG = ("w_in", "w_attn_proj", "w_glu_a", "w_glu_b", "w_out", "w_ffn_gate", "w_ffn_up", "w_ffn_down", "w_ple_gate",
       "w_ple_proj")
SMALL = ("g_mix", "a_re", "a_im", "log_dt", "b_re", "b_im", "c_re", "c_im", "d_skip", "g_ffn", "g_final")
ANY = pl.BlockSpec(memory_space=pl.ANY)


def _place():
    x, y, c = lax.axis_index("x"), lax.axis_index("y"), lax.axis_index("c")
    chips = [(1 - x, y), (x, 1 - y), (1 - x, 1 - y)]
    return x, y, c, chips


def _remote(src, dst, send_sem, recv_sem, to):
    return pltpu.make_async_remote_copy(src_ref=src, dst_ref=dst, send_sem=send_sem, recv_sem=recv_sem, device_id=to,
                                        device_id_type=MESH)


def _comm_call(name, body, ins, out_shapes, n_sems):
    n_w = len(ins)
    return pl.pallas_call(
        body, name=name, in_specs=[ANY] * n_w, out_specs=[ANY] * len(out_shapes), out_shape=out_shapes,
        scratch_shapes=[pltpu.SemaphoreType.DMA((n,)) for n in n_sems],
    )(*ins)


def _gather_weights(shards):
    n_w = len(shards)

    def body(*refs):
        ins, outs = refs[:n_w], refs[n_w:2 * n_w]
        ici_send, ici_recv, d2d_send, d2d_recv, local = refs[2 * n_w:]
        x, y, c, chips = _place()
        me = 2 * x + y
        sib = (x, y, 1 - c)
        started = []
        for w in range(n_w):
            own = pltpu.make_async_copy(ins[w], outs[w].at[me], local.at[w])
            own.start()
            started.append(own)
        sends = []
        for w in range(n_w):
            for j, (cx, cy) in enumerate(chips):
                k = 3 * w + j
                cp = _remote(ins[w].at[c], outs[w].at[me, c], ici_send.at[k], ici_recv.at[k], (cx, cy, c))
                cp.start()
                sends.append(cp)
        for w in range(n_w):
            for j, (cx, cy) in enumerate(chips):
                k = 3 * w + j
                src_chip = 2 * cx + cy
                landed = outs[w].at[src_chip, c]
                _remote(landed, landed, ici_send.at[k], ici_recv.at[k], (cx, cy, c)).wait_recv()
                fwd = _remote(landed, landed, d2d_send.at[k], d2d_recv.at[k], sib)
                fwd.start()
                sends.append(fwd)
        for w in range(n_w):
            for j, (cx, cy) in enumerate(chips):
                k = 3 * w + j
                other = outs[w].at[2 * cx + cy, 1 - c]
                _remote(other, other, d2d_send.at[k], d2d_recv.at[k], sib).wait_recv()
        for cp in sends:
            cp.wait_send()
        for own in started:
            own.wait()

    out_shapes = [jax.ShapeDtypeStruct((N_CHIPS,) + s.shape, s.dtype) for s in shards]
    return _comm_call("gather_weights", body, shards, out_shapes, [3 * n_w] * 4 + [n_w])


def _pair_exchange(grads):
    n_w = len(grads)

    def body(*refs):
        ins, outs = refs[:n_w], refs[n_w:2 * n_w]
        send, recv, local = refs[2 * n_w:]
        x, y, c, _ = _place()
        sib = (x, y, 1 - c)
        cps, owns = [], []
        for w in range(n_w):
            for q in range(N_CHIPS):
                k = N_CHIPS * w + q
                own = pltpu.make_async_copy(ins[w].at[q, c], outs[w].at[0, q], local.at[k])
                own.start()
                owns.append(own)
                cp = _remote(ins[w].at[q, 1 - c], outs[w].at[1, q], send.at[k], recv.at[k], sib)
                cp.start()
                cps.append(cp)
        for cp in cps:
            cp.wait()
        for own in owns:
            own.wait()

    out_shapes = [jax.ShapeDtypeStruct((2, N_CHIPS) + g.shape[2:], g.dtype) for g in grads]
    return _comm_call("grad_pair_exchange", body, grads, out_shapes, [N_CHIPS * n_w] * 3)


def _chip_exchange(parts):
    n_w = len(parts)

    def body(*refs):
        ins, outs = refs[:n_w], refs[n_w:2 * n_w]
        send, recv, local = refs[2 * n_w:]
        x, y, c, chips = _place()
        me = 2 * x + y
        cps, owns = [], []
        for w in range(n_w):
            own = pltpu.make_async_copy(ins[w].at[me], outs[w].at[me], local.at[w])
            own.start()
            owns.append(own)
            for j, (cx, cy) in enumerate(chips):
                k = 3 * w + j
                cp = _remote(ins[w].at[2 * cx + cy], outs[w].at[me], send.at[k], recv.at[k], (cx, cy, c))
                cp.start()
                cps.append(cp)
        for w in range(n_w):
            for j, (cx, cy) in enumerate(chips):
                k = 3 * w + j
                got = outs[w].at[2 * cx + cy]
                _remote(got, got, send.at[k], recv.at[k], (cx, cy, c)).wait_recv()
        for cp in cps:
            cp.wait_send()
        for own in owns:
            own.wait()

    out_shapes = [jax.ShapeDtypeStruct(t.shape, t.dtype) for t in parts]
    return _comm_call("grad_chip_exchange", body, parts, out_shapes, [3 * n_w, 3 * n_w, n_w])


def _pair_gather(halves):
    n_w = len(halves)

    def body(*refs):
        ins, outs = refs[:n_w], refs[n_w:2 * n_w]
        send, recv, local = refs[2 * n_w:]
        x, y, c, _ = _place()
        sib = (x, y, 1 - c)
        cps, owns = [], []
        for w in range(n_w):
            own = pltpu.make_async_copy(ins[w], outs[w].at[c], local.at[w])
            own.start()
            owns.append(own)
            cp = _remote(ins[w], outs[w].at[c], send.at[w], recv.at[w], sib)
            cp.start()
            cps.append(cp)
        for w in range(n_w):
            got = outs[w].at[1 - c]
            _remote(got, got, send.at[w], recv.at[w], sib).wait_recv()
        for cp in cps:
            cp.wait_send()
        for own in owns:
            own.wait()

    out_shapes = [jax.ShapeDtypeStruct((2,) + h.shape, h.dtype) for h in halves]
    return _comm_call("grad_pair_gather", body, halves, out_shapes, [n_w] * 3)


def _all_exchange(vec):
    def body(in_ref, out_ref, send, recv, local):
        x, y, c, _ = _place()
        me = 4 * x + 2 * y + c
        own = pltpu.make_async_copy(in_ref, out_ref.at[me], local.at[0])
        own.start()
        cps = []
        for k in range(1, 8):
            fx, fy, fc = (k >> 2) & 1, (k >> 1) & 1, k & 1
            to = (x ^ fx, y ^ fy, c ^ fc)
            cp = _remote(in_ref, out_ref.at[me], send.at[k - 1], recv.at[k - 1], to)
            cp.start()
            cps.append(cp)
        for k in range(1, 8):
            fx, fy, fc = (k >> 2) & 1, (k >> 1) & 1, k & 1
            src = 4 * (x ^ fx) + 2 * (y ^ fy) + (c ^ fc)
            got = out_ref.at[src]
            _remote(got, got, send.at[k - 1], recv.at[k - 1], (x ^ fx, y ^ fy, c ^ fc)).wait_recv()
        for cp in cps:
            cp.wait_send()
        own.wait()

    return _comm_call("small_all_exchange", body, [vec], [jax.ShapeDtypeStruct((8,) + vec.shape, vec.dtype)], [7, 7, 1])[0]


def _row_tile(r):
    for t in (256, 128, 176, 64, 32, 16, 8):
        if r % t == 0:
            return t
    return r


def _cast_shard(w2):
    r, c = w2.shape
    t = _row_tile(r)
    blk, imap = _rows(t, c)
    return _ew("cast_shard", (r // t,), [(w2, blk, imap)], [((r, c), BF16, blk, imap)], lambda pids, a: ((a,), ()))[0]


def _pair_sum(ex):
    _, _, r, c = ex.shape
    t = _row_tile(r)
    spec0 = ((None, None, t, c), lambda q, i: (0, q, i, 0))
    spec1 = ((None, None, t, c), lambda q, i: (1, q, i, 0))
    out = ((None, t, c), lambda q, i: (q, i, 0))
    return _ew("grad_pair_sum", (N_CHIPS, r // t), [(ex, *spec0), (ex, *spec1)], [((N_CHIPS, r, c), BF16, *out)],
               lambda pids, a, b: ((a + b,), ()))[0]


def _chip_sum(parts):
    _, r, c = parts.shape
    t = _row_tile(r)
    ins = [(parts, (None, t, c), (lambda i, q=q: (q, i, 0))) for q in range(N_CHIPS)]
    blk, imap = _rows(t, c)

    def fn(pids, a, b, c_, d):
        return (((a.astype(F32) + b.astype(F32)) + c_.astype(F32)) + d.astype(F32),), ()

    return _ew("grad_chip_sum", (r // t,), ins, [((r, c), F32, blk, imap)], fn)[0]


def _adamw_tile(w, g, m, v):
    m = ADAM_B1 * m + (1.0 - ADAM_B1) * g
    v = ADAM_B2 * v + (1.0 - ADAM_B2) * (g * g)
    m_hat = m / (1.0 - ADAM_B1 ** ADAM_STEP)
    v_hat = v / (1.0 - ADAM_B2 ** ADAM_STEP)
    delta = -ADAM_LR * (m_hat / (jnp.sqrt(v_hat) + ADAM_EPS) + ADAM_WD * w)
    return delta, m, v


def _adamw(name, g2, w2, m2, v2):
    r, c = w2.shape
    t = _row_tile(r)
    blk, imap = _rows(t, c)

    def fn(pids, g, w, m, v):
        delta, nm, nv = _adamw_tile(w, g, m, v)
        return (g, delta, nm, nv), ()

    return _ew(name, (r // t,), [(a, blk, imap) for a in (g2, w2, m2, v2)], [((r, c), F32, blk, imap)] * 4, fn)


def _device_sum(allv):
    _, r, c = allv.shape
    t = _row_tile(r)
    ins = [(allv, (None, t, c), (lambda i, q=q: (q, i, 0))) for q in range(8)]
    blk, imap = _rows(t, c)

    def fn(pids, *parts):
        tot = parts[0]
        for part in parts[1:]:
            tot = tot + part
        return (tot,), ()

    return _ew("small_device_sum", (r // t,), ins, [((r, c), F32, blk, imap)], fn)[0]


def _pack(parts):
    flat = jnp.concatenate([a.reshape(-1) for a in parts])
    pad = (-flat.shape[0]) % (SUB * 128)
    return jnp.pad(flat, (0, pad)).reshape(-1, 128)


def _unpack(mat, shapes):
    flat = mat.reshape(-1)
    out, off = [], 0
    for shp in shapes:
        n = math.prod(shp)
        out.append(flat[off:off + n].reshape(shp))
        off += n
    return out


def kernel(x, p, positions, g_mix, w_in, a_re, a_im, log_dt, b_re, b_im, c_re, c_im, d_skip, w_attn_proj, w_glu_a, w_glu_b, w_out, g_ffn, w_ffn_gate, w_ffn_up, w_ffn_down, w_ple_gate, w_ple_proj, g_final, loss_target, m_g_mix, m_w_in, m_a_re, m_a_im, m_log_dt, m_b_re, m_b_im, m_c_re, m_c_im, m_d_skip, m_w_attn_proj, m_w_glu_a, m_w_glu_b, m_w_out, m_g_ffn, m_w_ffn_gate, m_w_ffn_up, m_w_ffn_down, m_w_ple_gate, m_w_ple_proj, m_g_final, v_g_mix, v_w_in, v_a_re, v_a_im, v_log_dt, v_b_re, v_b_im, v_c_re, v_c_im, v_d_skip, v_w_attn_proj, v_w_glu_a, v_w_glu_b, v_w_out, v_g_ffn, v_w_ffn_gate, v_w_ffn_up, v_w_ffn_down, v_w_ple_gate, v_w_ple_proj, v_g_final):
    given = dict(locals())
    big_w = {n: given[n] for n in BIG}
    w_mats = {n: big_w[n].reshape(big_w[n].shape[1:]) for n in BIG}

    shards = []
    for n in BIG:
        r, c = w_mats[n].shape
        shards.append(_cast_shard(w_mats[n]).reshape(2, r // 2, c))
    gathered = _gather_weights(shards)
    wts = {}
    for n, g in zip(BIG, gathered):
        r, c = w_mats[n].shape
        wts[n] = g.reshape(N_CHIPS, r, c)

    sm = {
        "g_mix": g_mix.reshape(1, D_MODEL), "g_ffn": g_ffn.reshape(1, D_MODEL), "g_final": g_final.reshape(1, D_MODEL),
        "a_re": a_re[0], "a_im": a_im[0], "log_dt": log_dt[0], "b_re": b_re[0], "b_im": b_im[0], "c_re": c_re[0],
        "c_im": c_im[0], "d_skip": d_skip[0],
    }
    s = x.shape[1]
    loss_part, grad_x, big_g, small_g = _local_step(x[0], p[0, 0], positions[0], loss_target[0], sm, wts)

    g5 = []
    for n in BIG:
        r, c = w_mats[n].shape
        g5.append(big_g[n].reshape(N_CHIPS, 2, r // 2, c))
    pair = _pair_exchange(g5)
    chip_parts = [_pair_sum(e) for e in pair]
    chip_all = _chip_exchange(chip_parts)
    halves = [_chip_sum(t) for t in chip_all]
    full = _pair_gather(halves)

    results = {}
    for n, g in zip(BIG, full):
        r, c = w_mats[n].shape
        shp = big_w[n].shape
        outs = _adamw("adamw_" + n, g.reshape(r, c), w_mats[n], given["m_" + n].reshape(r, c), given["v_" + n].reshape(r, c))
        results[n] = [o.reshape(shp) for o in outs]

    small_shapes = [given[n].shape for n in SMALL]
    vec = _pack([small_g[n] for n in SMALL] + [loss_part.reshape(1)])
    tot = _device_sum(_all_exchange(vec))
    n_small = sum(math.prod(shp) for shp in small_shapes)
    loss = tot.reshape(-1)[n_small]
    w_s = _pack([given[n] for n in SMALL])
    m_s = _pack([given["m_" + n] for n in SMALL])
    v_s = _pack([given["v_" + n] for n in SMALL])
    rows_s = w_s.shape[0]
    g_s = tot.reshape(-1)[: rows_s * 128].reshape(rows_s, 128)
    outs_s = _adamw("adamw_small", g_s, w_s, m_s, v_s)
    for kind, mat in enumerate(outs_s):
        for n, arr in zip(SMALL, _unpack(mat, small_shapes)):
            results.setdefault(n, [None] * 4)[kind] = arr

    order = ("g_mix", "w_in", "a_re", "a_im", "log_dt", "b_re", "b_im", "c_re", "c_im", "d_skip", "w_attn_proj", "w_glu_a",
             "w_glu_b", "w_out", "g_ffn", "w_ffn_gate", "w_ffn_up", "w_ffn_down", "w_ple_gate", "w_ple_proj", "g_final")
    out = [loss, grad_x.reshape(1, s, D_MODEL)]
    for kind in range(4):
        out += [results[n][kind] for n in order]
    return tuple(out)
```

```python
import math

import jax
import jax.numpy as jnp
from jax import lax
from jax.experimental import pallas as pl
from jax.experimental.pallas import tpu as pltpu

F32 = jnp.float32
BF16 = jnp.bfloat16

D_MODEL = 1024
HEAD_DIM = 128
HEADS_PER_GROUP = 4
GROUP_WIDTH = HEADS_PER_GROUP * HEAD_DIM
GROUP_DILATIONS = (1, 4, 16)
ATTN_BLOCK = 128
ROPE_DIM = 32
ROPE_HALF = 16
ROPE_THETA = 500000.0
SSM_WIDTH = 512
SSM_GROUPS = 32
SSM_GROUP = 16
SSM_STATE = 64
N_STATE = SSM_GROUPS * SSM_STATE
SSM_SUPER = 4
IN_WIDTH = 7168
COL_U = 4608
COL_GA = 5120
COL_GS = 6144
D_FF = 2816
N_CHIPS = 4
D_FF_Q = D_FF // N_CHIPS
PLE_DIM = 256
EPS = 1e-6
ADAM_LR = 0.001
ADAM_B1 = 0.9
ADAM_B2 = 0.999
ADAM_EPS = 1e-08
ADAM_WD = 0.01
ADAM_STEP = 10
NEG_BIG = -1e30
VMEM_LIMIT_BYTES = 56 * 1024 * 1024
MESH = pl.DeviceIdType.MESH

_DIMS = {
    "nn": (((1,), (0,)), ((), ())),
    "nt": (((1,), (1,)), ((), ())),
    "tn": (((0,), (0,)), ((), ())),
}


def _params(n_grid):
    return pltpu.CompilerParams(dimension_semantics=("arbitrary",) * n_grid, vmem_limit_bytes=VMEM_LIMIT_BYTES)


def _sig(v):
    return 1.0 / (1.0 + jnp.exp(-v))


def _dot(a, b, mode):
    return lax.dot_general(a, b, _DIMS[mode], preferred_element_type=F32)


def _mm(name, grid, pairs, mode, outs, epilogue=None, extras=(), acc_outs=(), acc_shape=None, j_outer=False,
        sum_pairs=True):
    gi, gj, gk = grid
    n_p, n_e, n_o, n_a = len(pairs), len(extras), len(outs), len(acc_outs)
    assert not n_a or gj == 1
    assert sum_pairs or gk == 1

    def order(imap):
        return (lambda j, i, k: imap(i, j, k)) if j_outer else imap

    shared_a = [pr[0] is None for pr in pairs]
    n_in = 2 * n_p - sum(shared_a)

    def body(*refs):
        pair_refs = list(refs[:n_in])
        extra_refs = refs[n_in: n_in + n_e]
        out_refs = refs[n_in + n_e: n_in + n_e + n_o]
        sum_refs = refs[n_in + n_e + n_o: n_in + n_e + n_o + n_a]
        i = pl.program_id(1 if j_outer else 0)
        k = pl.program_id(2)
        part = None if sum_pairs else []
        a = None
        for t in range(n_p):
            if not shared_a[t]:
                a = pair_refs.pop(0)[...].astype(BF16)
            b = pair_refs.pop(0)[...].astype(BF16)
            d = _dot(a, b, mode)
            if sum_pairs:
                part = d if part is None else part + d
            else:
                part.append(d)

        def finish(acc):
            tiles, sums = epilogue(acc, *[e[...] for e in extra_refs]) if epilogue is not None else ((acc,), ())
            for o_ref, tile in zip(out_refs, tiles):
                o_ref[...] = tile.astype(o_ref.dtype)
            if n_a:
                @pl.when(i == 0)
                def _():
                    for s_ref in sum_refs:
                        s_ref[...] = jnp.zeros_like(s_ref)

                for s_ref, s in zip(sum_refs, sums):
                    s_ref[...] += s

        if gk == 1:
            finish(part)
        else:
            acc_ref = refs[-1]

            @pl.when(k == 0)
            def _():
                acc_ref[...] = part

            @pl.when(k > 0)
            def _():
                acc_ref[...] += part

            @pl.when(k == gk - 1)
            def _():
                finish(acc_ref[...])

    in_specs, args = [], []
    for a, a_block, a_imap, b, b_block, b_imap in pairs:
        if a is not None:
            in_specs.append(pl.BlockSpec(a_block, order(a_imap)))
            args.append(a)
        in_specs.append(pl.BlockSpec(b_block, order(b_imap)))
        args.append(b)
    for e, e_block, e_imap in extras:
        in_specs.append(pl.BlockSpec(e_block, order(e_imap)))
        args.append(e)
    out_shape = [jax.ShapeDtypeStruct(shape, dtype) for shape, dtype, _, _ in outs]
    out_specs = [pl.BlockSpec(block, order(imap)) for _, _, block, imap in outs]
    for shape, dtype in acc_outs:
        out_shape.append(jax.ShapeDtypeStruct(shape, dtype))
        out_specs.append(pl.BlockSpec(shape, lambda i, j, k: (0, 0)))
    scratch = [pltpu.VMEM(acc_shape, F32)] if gk > 1 else []
    return pl.pallas_call(
        body, name=name, grid=(gj, gi, gk) if j_outer else grid, in_specs=in_specs, out_specs=out_specs,
        out_shape=out_shape, scratch_shapes=scratch, compiler_params=_params(3),
    )(*args)


def _ew(name, grid, ins, outs, fn, acc_outs=(), place=None):
    n_i, n_o, n_a = len(ins), len(outs), len(acc_outs)
    ng = len(grid)
    n_s = 0 if place is None else 1

    def body(*refs):
        in_refs = refs[n_s: n_s + n_i]
        out_refs = refs[n_s + n_i: n_s + n_i + n_o]
        sum_refs = refs[n_s + n_i + n_o:]
        pids = tuple(pl.program_id(a) for a in range(ng))
        if n_s:
            pids = (refs[0],) + pids
        tiles, sums = fn(pids, *[r[...] for r in in_refs])
        for o_ref, tile in zip(out_refs, tiles):
            o_ref[...] = tile.astype(o_ref.dtype)
        if n_a:
            first = pids[0] == 0
            for p_ in pids[1:]:
                first = jnp.logical_and(first, p_ == 0)

            @pl.when(first)
            def _():
                for s_ref in sum_refs:
                    s_ref[...] = jnp.zeros_like(s_ref)

            for s_ref, s in zip(sum_refs, sums):
                s_ref[...] += s

    in_specs = [pl.BlockSpec(block, imap) for _, block, imap in ins]
    out_shape = [jax.ShapeDtypeStruct(shape, dtype) for shape, dtype, _, _ in outs]
    out_specs = [pl.BlockSpec(block, imap) for _, _, block, imap in outs]
    for shape, dtype in acc_outs:
        out_shape.append(jax.ShapeDtypeStruct(shape, dtype))
        out_specs.append(pl.BlockSpec(shape, lambda *_, nd=len(shape): (0,) * nd))
    arrays = [a for a, _, _ in ins]
    if n_s:
        assert not n_a
        spec = pltpu.PrefetchScalarGridSpec(num_scalar_prefetch=1, grid=grid, in_specs=in_specs, out_specs=out_specs)
        return pl.pallas_call(body, name=name, grid_spec=spec, out_shape=out_shape, compiler_params=_params(ng))(
            place, *arrays)
    return pl.pallas_call(
        body, name=name, grid=grid, in_specs=in_specs, out_specs=out_specs, out_shape=out_shape,
        compiler_params=_params(ng),
    )(*arrays)


def _rows(tm, width):
    return (tm, width), (lambda i: (i, 0))


def _rms_fwd_tile(h, g):
    r = lax.rsqrt(jnp.mean(h * h, axis=-1, keepdims=True) + EPS)
    return h * r * g


def _rms_bwd_tile(dn, h, g):
    r = lax.rsqrt(jnp.mean(h * h, axis=-1, keepdims=True) + EPS)
    hhat = h * r
    gy = dn * g
    dh = r * (gy - hhat * jnp.mean(gy * hhat, axis=-1, keepdims=True))
    dg = jnp.sum(dn * hhat, axis=0, keepdims=True)
    return dh, dg


def _rope_tables(pos_col, inv_row, tm):
    s = pos_col.shape[0]

    def fn(pids, pos, inv):
        ang = pos * inv
        lane = lax.broadcasted_iota(jnp.int32, ang.shape, 1)
        cs = jnp.where(lane < ROPE_DIM, jnp.cos(ang), 1.0)
        sn = jnp.sin(ang)
        s_lo = jnp.where(lane < ROPE_HALF, -sn, 0.0)
        s_hi = jnp.where(jnp.logical_and(lane >= ROPE_HALF, lane < ROPE_DIM), sn, 0.0)
        return (cs, s_lo, s_hi), ()

    blk, imap = _rows(tm, 128)
    return _ew(
        "rope_tables", (s // tm,),
        [(pos_col, (tm, 1), lambda i: (i, 0)), (inv_row, (1, 128), lambda i: (0, 0))],
        [((s, 128), F32, blk, imap)] * 3, fn,
    )


def _rope(xh, cs, s_lo, s_hi):
    return xh * cs + pltpu.roll(xh, HEAD_DIM - ROPE_HALF, 1) * s_lo + pltpu.roll(xh, ROPE_HALF, 1) * s_hi


def _rope_t(gh, cs, s_lo, s_hi):
    return gh * cs + pltpu.roll(gh * s_lo, ROPE_HALF, 1) + pltpu.roll(gh * s_hi, HEAD_DIM - ROPE_HALF, 1)


def _attn_geometry(length):
    nb = length // ATTN_BLOCK
    gq = min(4, nb)
    assert nb % gq == 0
    return nb, gq, gq * ATTN_BLOCK, nb // gq


def _band_masks():
    qi = lax.broadcasted_iota(jnp.int32, (ATTN_BLOCK, ATTN_BLOCK), 0)
    kj = lax.broadcasted_iota(jnp.int32, (ATTN_BLOCK, ATTN_BLOCK), 1)
    return kj <= qi, kj >= qi


def _attn_fwd(qv, kv, vv, tabs_v, dil):
    length = qv.shape[0]
    nb, gq, rows, ni = _attn_geometry(length)
    scale = 1.0 / math.sqrt(HEAD_DIM)

    def body(q_ref, kc_ref, kp_ref, vc_ref, vp_ref, cc, lc, hc, cp_, lp, hp, o_ref, l_ref):
        i = pl.program_id(1)
        mask_c, mask_p = _band_masks()
        has_prev0 = i > 0
        tc = (cc[...], lc[...], hc[...])
        tp = (cp_[...], lp[...], hp[...])
        for h in range(HEADS_PER_GROUP):
            cols = slice(h * HEAD_DIM, (h + 1) * HEAD_DIM)
            qh = (_rope(q_ref[:, cols].astype(F32), *tc) * scale).astype(BF16)
            kch = _rope(kc_ref[:, cols].astype(F32), *tc).astype(BF16)
            kph = _rope(kp_ref[:, cols].astype(F32), *tp).astype(BF16)
            vch = vc_ref[:, cols]
            vph = vp_ref[:, cols]
            for jj in range(gq):
                rws = slice(jj * ATTN_BLOCK, (jj + 1) * ATTN_BLOCK)
                prv = slice((jj - 1) * ATTN_BLOCK, jj * ATTN_BLOCK)
                qb = qh[rws]
                k_prev = kph if jj == 0 else kch[prv]
                v_prev = vph if jj == 0 else vch[prv]
                s_c = jnp.where(mask_c, _dot(qb, kch[rws], "nt"), NEG_BIG)
                m_p = jnp.logical_and(mask_p, has_prev0) if jj == 0 else mask_p
                s_p = jnp.where(m_p, _dot(qb, k_prev, "nt"), NEG_BIG)
                m = jnp.maximum(jnp.max(s_c, axis=-1, keepdims=True), jnp.max(s_p, axis=-1, keepdims=True))
                p_c = jnp.exp(s_c - m)
                p_p = jnp.exp(s_p - m)
                den = jnp.sum(p_c, axis=-1, keepdims=True) + jnp.sum(p_p, axis=-1, keepdims=True)
                o = _dot(p_c.astype(BF16), vch[rws], "nn") + _dot(p_p.astype(BF16), v_prev, "nn")
                o_ref[rws, cols] = o / den
                l_ref[rws, cols] = jnp.broadcast_to(m + jnp.log(den), (ATTN_BLOCK, HEAD_DIM))

    cur = lambda r, i: (i, r)
    prev = lambda r, i: (jnp.maximum(i * gq - 1, 0), r)
    wide = pl.BlockSpec((rows, GROUP_WIDTH), cur)
    wide_prev = pl.BlockSpec((ATTN_BLOCK, GROUP_WIDTH), prev)
    tab = pl.BlockSpec((rows, HEAD_DIM), cur)
    tab_prev = pl.BlockSpec((ATTN_BLOCK, HEAD_DIM), prev)
    return pl.pallas_call(
        body, name=f"attn_fwd_d{dil}", grid=(dil, ni),
        in_specs=[wide, wide, wide_prev, wide, wide_prev, tab, tab, tab, tab_prev, tab_prev, tab_prev],
        out_specs=[wide, wide],
        out_shape=[jax.ShapeDtypeStruct(qv.shape, F32)] * 2,
        compiler_params=_params(2),
    )(qv, kv, kv, vv, vv, *tabs_v, *tabs_v)


def _attn_bwd(qv, kv, vv, dov, ov, lv, tabs_v, dil):
    length = qv.shape[0]
    nb, gq, rows, ni = _attn_geometry(length)
    scale = 1.0 / math.sqrt(HEAD_DIM)

    def body(qc_ref, qn_ref, kc_ref, kp_ref, vc_ref, vp_ref, doc_ref, don_ref, oc_ref, on_ref, lc_ref, ln_ref,
             cc, lc, hc, cp_, lp, hp, cn, ln, hn, dq_ref, dk_ref, dv_ref):
        i = pl.program_id(1)
        mask_c, mask_p = _band_masks()
        has_prev0 = i > 0
        has_next = i < ni - 1
        tc = (cc[...], lc[...], hc[...])
        tp = (cp_[...], lp[...], hp[...])
        tn = (cn[...], ln[...], hn[...])

        def tile(qb, kb, vb, dob, lb, delta, mask):
            s = _dot(qb, kb, "nt")
            p = jnp.where(mask, jnp.exp(s - lb), 0.0)
            dp = _dot(dob, vb, "nt")
            ds = p * (dp - delta)
            return p, ds

        for h in range(HEADS_PER_GROUP):
            cols = slice(h * HEAD_DIM, (h + 1) * HEAD_DIM)
            q_c = (_rope(qc_ref[:, cols].astype(F32), *tc) * scale).astype(BF16)
            q_n = (_rope(qn_ref[:, cols].astype(F32), *tn) * scale).astype(BF16)
            k_c = _rope(kc_ref[:, cols].astype(F32), *tc).astype(BF16)
            k_p = _rope(kp_ref[:, cols].astype(F32), *tp).astype(BF16)
            v_c = vc_ref[:, cols]
            v_p = vp_ref[:, cols]
            do_c = doc_ref[:, cols]
            do_n = don_ref[:, cols]
            l_c = lc_ref[:, cols]
            l_n = ln_ref[:, cols]
            dl_c = jnp.sum(do_c.astype(F32) * oc_ref[:, cols].astype(F32), axis=-1, keepdims=True)
            dl_n = jnp.sum(do_n.astype(F32) * on_ref[:, cols].astype(F32), axis=-1, keepdims=True)
            dq_blocks, dk_blocks, dv_blocks = [], [None] * gq, [None] * gq

            def add(lst, idx, val):
                lst[idx] = val if lst[idx] is None else lst[idx] + val

            for jj in range(gq):
                rws = slice(jj * ATTN_BLOCK, (jj + 1) * ATTN_BLOCK)
                prv = slice((jj - 1) * ATTN_BLOCK, jj * ATTN_BLOCK)
                qb, dob, lb, dlb = q_c[rws], do_c[rws], l_c[rws], dl_c[rws]
                p, ds = tile(qb, k_c[rws], v_c[rws], dob, lb, dlb, mask_c)
                dsb = ds.astype(BF16)
                dq = _dot(dsb, k_c[rws], "nn")
                add(dk_blocks, jj, _dot(dsb, qb, "tn"))
                add(dv_blocks, jj, _dot(p.astype(BF16), dob, "tn"))
                if jj == 0:
                    p, ds = tile(qb, k_p, v_p, dob, lb, dlb, jnp.logical_and(mask_p, has_prev0))
                    dq = dq + _dot(ds.astype(BF16), k_p, "nn")
                else:
                    p, ds = tile(qb, k_c[prv], v_c[prv], dob, lb, dlb, mask_p)
                    dsb = ds.astype(BF16)
                    dq = dq + _dot(dsb, k_c[prv], "nn")
                    add(dk_blocks, jj - 1, _dot(dsb, qb, "tn"))
                    add(dv_blocks, jj - 1, _dot(p.astype(BF16), dob, "tn"))
                dq_blocks.append(dq)
            last = slice((gq - 1) * ATTN_BLOCK, gq * ATTN_BLOCK)
            p, ds = tile(q_n, k_c[last], v_c[last], do_n, l_n, dl_n, jnp.logical_and(mask_p, has_next))
            add(dk_blocks, gq - 1, _dot(ds.astype(BF16), q_n, "tn"))
            add(dv_blocks, gq - 1, _dot(p.astype(BF16), do_n, "tn"))
            for jj in range(gq):
                rws = slice(jj * ATTN_BLOCK, (jj + 1) * ATTN_BLOCK)
                t_rows = tuple(t[rws] for t in tc)
                dq_ref[rws, cols] = _rope_t(dq_blocks[jj] * scale, *t_rows).astype(dq_ref.dtype)
                dk_ref[rws, cols] = _rope_t(dk_blocks[jj], *t_rows).astype(dk_ref.dtype)
                dv_ref[rws, cols] = dv_blocks[jj].astype(dv_ref.dtype)

    cur = lambda r, i: (i, r)
    prev = lambda r, i: (jnp.maximum(i * gq - 1, 0), r)
    nxt = lambda r, i: (jnp.minimum((i + 1) * gq, nb - 1), r)
    wide = pl.BlockSpec((rows, GROUP_WIDTH), cur)
    wide_prev = pl.BlockSpec((ATTN_BLOCK, GROUP_WIDTH), prev)
    wide_next = pl.BlockSpec((ATTN_BLOCK, GROUP_WIDTH), nxt)
    tab = pl.BlockSpec((rows, HEAD_DIM), cur)
    tab_prev = pl.BlockSpec((ATTN_BLOCK, HEAD_DIM), prev)
    tab_next = pl.BlockSpec((ATTN_BLOCK, HEAD_DIM), nxt)
    return pl.pallas_call(
        body, name=f"attn_bwd_d{dil}", grid=(dil, ni),
        in_specs=[wide, wide_next, wide, wide_prev, wide, wide_prev, wide, wide_next, wide, wide_next, wide, wide_next,
                  tab, tab, tab, tab_prev, tab_prev, tab_prev, tab_next, tab_next, tab_next],
        out_specs=[wide, wide, wide],
        out_shape=[jax.ShapeDtypeStruct(qv.shape, BF16)] * 3,
        compiler_params=_params(2),
    )(qv, qv, kv, kv, vv, vv, dov, dov, ov, ov, lv, lv, *tabs_v, *tabs_v, *tabs_v)


def _to_view(a, dil):
    s, w = a.shape
    return a.reshape(s // dil, dil * w)


def _from_view(a, dil):
    length, dw = a.shape
    return a.reshape(length * dil, dw // dil)


def _discretise(a_re, a_im, log_dt, bt_re, bt_im):
    dt = jnp.exp(log_dt)
    mag = jnp.exp(a_re * dt)
    bar_re = mag * jnp.cos(a_im * dt)
    bar_im = mag * jnp.sin(a_im * dt)
    nr = bar_re - 1.0
    ni = bar_im
    den = a_re * a_re + a_im * a_im
    z_re = (nr * a_re + ni * a_im) / den
    z_im = (ni * a_re - nr * a_im) / den
    bb_re = z_re[:, None, :] * bt_re - z_im[:, None, :] * bt_im
    bb_im = z_re[:, None, :] * bt_im + z_im[:, None, :] * bt_re
    return bar_re, bar_im, bb_re, bb_im


def _ssm_prep(a_re, a_im, log_dt, bt_re, bt_im):
    def body(ar, ai, ld, br, bi, o_lr, o_li, o_br, o_bi):
        lr, li, bbr, bbi = _discretise(ar[...], ai[...], ld[...], br[...], bi[...])
        o_lr[...] = lr
        o_li[...] = li
        o_br[...] = bbr
        o_bi[...] = bbi

    sm = jax.ShapeDtypeStruct((SSM_GROUPS, SSM_STATE), F32)
    bg = jax.ShapeDtypeStruct((SSM_GROUPS, SSM_GROUP, SSM_STATE), F32)
    return pl.pallas_call(body, name="ssm_prep", out_shape=[sm, sm, bg, bg])(a_re, a_im, log_dt, bt_re, bt_im)


def _ssm_param_bwd(a_re, a_im, log_dt, bt_re, bt_im, d_lr, d_li, d_bbr, d_bbi):
    def body(ar, ai, ld, br, bi, g_lr, g_li, g_br, g_bi, o_ar, o_ai, o_ld, o_br, o_bi):
        _, vjp = jax.vjp(_discretise, ar[...], ai[...], ld[...], br[...], bi[...])
        d_ar, d_ai, d_ld, d_br, d_bi = vjp((g_lr[...], g_li[...], g_br[...], g_bi[...]))
        o_ar[...] = d_ar
        o_ai[...] = d_ai
        o_ld[...] = d_ld
        o_br[...] = d_br
        o_bi[...] = d_bi

    sm = jax.ShapeDtypeStruct((SSM_GROUPS, SSM_STATE), F32)
    col = jax.ShapeDtypeStruct((SSM_GROUPS, 1), F32)
    bg = jax.ShapeDtypeStruct((SSM_GROUPS, SSM_GROUP, SSM_STATE), F32)
    return pl.pallas_call(body, name="ssm_param_bwd", out_shape=[sm, sm, col, bg, bg])(
        a_re, a_im, log_dt, bt_re, bt_im, d_lr, d_li, d_bbr, d_bbi)


def _block_diag(t, rows_per, cols_per):
    t4 = t.reshape(SSM_SUPER, 8, rows_per, cols_per)
    eye = jnp.eye(8, dtype=t.dtype)
    return jnp.einsum("bgrc,gh->bgrhc", t4, eye).reshape(SSM_SUPER, 8 * rows_per, 8 * cols_per)


def _block_diag_t(dense, rows_per, cols_per):
    t = dense.reshape(SSM_SUPER, 8, rows_per, 8, cols_per)
    eye = jnp.eye(8, dtype=dense.dtype)
    return jnp.einsum("bgrhc,gh->bgrc", t, eye).reshape(SSM_GROUPS, rows_per, cols_per)


def _gelu(v):
    c = math.sqrt(2.0 / math.pi)
    return 0.5 * v * (1.0 + jnp.tanh(c * (v + 0.044715 * v * v * v)))


def _gelu_grad(v):
    c = math.sqrt(2.0 / math.pi)
    t = jnp.tanh(c * (v + 0.044715 * v * v * v))
    return 0.5 * (1.0 + t) + 0.5 * v * (1.0 - t * t) * c * (1.0 + 3.0 * 0.044715 * v * v)


SUB = 8


def _scan_rows(g_re_ref, g_im_ref, lam_re, lam_im, carry, n_rows, reverse, conj):
    sign = -1.0 if conj else 1.0
    row_id = lax.broadcasted_iota(jnp.int32, (SUB, N_STATE), 0)
    lr = jnp.broadcast_to(lam_re, (SUB, N_STATE))
    li = jnp.broadcast_to(lam_im, (SUB, N_STATE)) * sign

    def tile_step(tt, state):
        sr, si = state
        t8 = (n_rows // SUB - 1 - tt) if reverse else tt
        start = pl.multiple_of(t8 * SUB, SUB)
        g_r = g_re_ref[pl.ds(start, SUB), :]
        g_i = g_im_ref[pl.ds(start, SUB), :]
        out_r, out_i = g_r, g_i
        order = range(SUB - 1, -1, -1) if reverse else range(SUB)
        for j in order:
            gr_j = jnp.broadcast_to(g_r[j:j + 1, :], (SUB, N_STATE))
            gi_j = jnp.broadcast_to(g_i[j:j + 1, :], (SUB, N_STATE))
            nr = lr * sr - li * si + gr_j
            ni = lr * si + li * sr + gi_j
            sr, si = nr, ni
            out_r = jnp.where(row_id == j, sr, out_r)
            out_i = jnp.where(row_id == j, si, out_i)
        g_re_ref[pl.ds(start, SUB), :] = out_r
        g_im_ref[pl.ds(start, SUB), :] = out_i
        return sr, si

    return lax.fori_loop(0, n_rows // SUB, tile_step, carry)


def _ssm_fwd(z, b_re, b_im, c_re, c_im, lam_re, lam_im, d_skip, chunk):
    s = z.shape[0]

    def body(u_ref, bre, bim, cre, cim, lre, lim, dsk, hre_ref, him_ref, ys_ref, yg_ref, car_re, car_im):
        i = pl.program_id(0)

        @pl.when(i == 0)
        def _():
            car_re[...] = jnp.zeros_like(car_re)
            car_im[...] = jnp.zeros_like(car_im)

        u = u_ref[...]
        for b in range(SSM_SUPER):
            ub = u[:, b * 128:(b + 1) * 128]
            st = slice(b * 512, (b + 1) * 512)
            hre_ref[:, st] = _dot(ub, bre[b], "nn")
            him_ref[:, st] = _dot(ub, bim[b], "nn")
        sr, si = _scan_rows(hre_ref, him_ref, lre[...], lim[...], (car_re[...], car_im[...]), chunk, False, False)
        car_re[...] = sr
        car_im[...] = si
        uf = u.astype(F32)
        for b in range(SSM_SUPER):
            st = slice(b * 512, (b + 1) * 512)
            ch = slice(b * 128, (b + 1) * 128)
            y = _dot(hre_ref[:, st].astype(BF16), cre[b], "nn") - _dot(him_ref[:, st].astype(BF16), cim[b], "nn")
            y = y + dsk[:, ch] * uf[:, ch]
            ys_ref[:, ch] = y
            yg_ref[:, ch] = _gelu(y).astype(BF16)

    full3 = lambda i: (0, 0, 0)
    full2 = lambda i: (0, 0)
    row = lambda i: (i, 0)
    u_col = COL_U // SSM_WIDTH
    return pl.pallas_call(
        body, name="ssm_fwd", grid=(s // chunk,),
        in_specs=[pl.BlockSpec((chunk, SSM_WIDTH), lambda i: (i, u_col)),
                  pl.BlockSpec((SSM_SUPER, 128, 512), full3), pl.BlockSpec((SSM_SUPER, 128, 512), full3),
                  pl.BlockSpec((SSM_SUPER, 512, 128), full3), pl.BlockSpec((SSM_SUPER, 512, 128), full3),
                  pl.BlockSpec((1, N_STATE), full2), pl.BlockSpec((1, N_STATE), full2), pl.BlockSpec((1, SSM_WIDTH), full2)],
        out_specs=[pl.BlockSpec((chunk, N_STATE), row), pl.BlockSpec((chunk, N_STATE), row),
                   pl.BlockSpec((chunk, SSM_WIDTH), row), pl.BlockSpec((chunk, SSM_WIDTH), row)],
        out_shape=[jax.ShapeDtypeStruct((s, N_STATE), F32), jax.ShapeDtypeStruct((s, N_STATE), F32),
                   jax.ShapeDtypeStruct((s, SSM_WIDTH), F32), jax.ShapeDtypeStruct((s, SSM_WIDTH), BF16)],
        scratch_shapes=[pltpu.VMEM((SUB, N_STATE), F32), pltpu.VMEM((SUB, N_STATE), F32)],
        compiler_params=_params(1),
    )(z, b_re, b_im, c_re, c_im, lam_re, lam_im, d_skip)


def _ssm_bwd(dys, z, h_re, h_im, b_re, b_im, c_re, c_im, lam_re, lam_im, d_skip, chunk):
    s = z.shape[0]
    n_chunks = s // chunk

    def body(dy_ref, u_ref, hre_ref, him_ref, hpr_ref, hpi_ref, bre, bim, cre, cim, lre, lim, dsk,
             du_ref, dlr_ref, dli_ref, dbr_ref, dbi_ref, dcr_ref, dci_ref, dd_ref, are, aim, car_re, car_im):
        i = pl.program_id(0)
        n = n_chunks - 1 - i

        @pl.when(i == 0)
        def _():
            car_re[...] = jnp.zeros_like(car_re)
            car_im[...] = jnp.zeros_like(car_im)
            for r in (dlr_ref, dli_ref, dbr_ref, dbi_ref, dcr_ref, dci_ref, dd_ref):
                r[...] = jnp.zeros_like(r)

        dy = dy_ref[...]
        dyb = dy.astype(BF16)
        u = u_ref[...]
        for b in range(SSM_SUPER):
            ch = slice(b * 128, (b + 1) * 128)
            st = slice(b * 512, (b + 1) * 512)
            are[:, st] = _dot(dyb[:, ch], cre[b], "nt")
            aim[:, st] = -_dot(dyb[:, ch], cim[b], "nt")
        sr, si = _scan_rows(are, aim, lre[...], lim[...], (car_re[...], car_im[...]), chunk, True, True)
        car_re[...] = sr
        car_im[...] = si
        row_id = lax.broadcasted_iota(jnp.int32, (chunk, N_STATE), 0)
        top_scale = jnp.where(n > 0, 1.0, 0.0)
        h_r = hre_ref[...]
        h_i = him_ref[...]
        hp_r = jnp.where(row_id == 0, hpr_ref[SUB - 1:SUB, :] * top_scale, pltpu.roll(h_r, 1, 0))
        hp_i = jnp.where(row_id == 0, hpi_ref[SUB - 1:SUB, :] * top_scale, pltpu.roll(h_i, 1, 0))
        a_r = are[...]
        a_i = aim[...]
        dlr_ref[...] += jnp.sum(a_r * hp_r + a_i * hp_i, axis=0, keepdims=True)
        dli_ref[...] += jnp.sum(a_i * hp_r - a_r * hp_i, axis=0, keepdims=True)
        dd_ref[...] += jnp.sum(dy * u.astype(F32), axis=0, keepdims=True)
        a_rb = a_r.astype(BF16)
        a_ib = a_i.astype(BF16)
        h_rb = h_r.astype(BF16)
        h_ib = h_i.astype(BF16)
        for b in range(SSM_SUPER):
            ch = slice(b * 128, (b + 1) * 128)
            st = slice(b * 512, (b + 1) * 512)
            dbr_ref[b] += _dot(u[:, ch], a_rb[:, st], "tn")
            dbi_ref[b] += _dot(u[:, ch], a_ib[:, st], "tn")
            dcr_ref[b] += _dot(h_rb[:, st], dyb[:, ch], "tn")
            dci_ref[b] += -_dot(h_ib[:, st], dyb[:, ch], "tn")
            du = _dot(a_rb[:, st], bre[b], "nt") + _dot(a_ib[:, st], bim[b], "nt") + dsk[:, ch] * dy[:, ch]
            du_ref[:, ch] = du.astype(du_ref.dtype)

    full3 = lambda i: (0, 0, 0)
    full2 = lambda i: (0, 0)
    rev = lambda i: (n_chunks - 1 - i, 0)
    above = lambda i: (jnp.maximum((n_chunks - 1 - i) * (chunk // SUB) - 1, 0), 0)
    u_col = COL_U // SSM_WIDTH
    b_spec = pl.BlockSpec((SSM_SUPER, 128, 512), full3)
    c_spec = pl.BlockSpec((SSM_SUPER, 512, 128), full3)
    vec = pl.BlockSpec((1, N_STATE), full2)
    return pl.pallas_call(
        body, name="ssm_bwd", grid=(n_chunks,),
        in_specs=[pl.BlockSpec((chunk, SSM_WIDTH), rev),
                  pl.BlockSpec((chunk, SSM_WIDTH), lambda i: (n_chunks - 1 - i, u_col)),
                  pl.BlockSpec((chunk, N_STATE), rev), pl.BlockSpec((chunk, N_STATE), rev),
                  pl.BlockSpec((SUB, N_STATE), above), pl.BlockSpec((SUB, N_STATE), above),
                  b_spec, b_spec, c_spec, c_spec, vec, vec, pl.BlockSpec((1, SSM_WIDTH), full2)],
        out_specs=[pl.BlockSpec((chunk, SSM_WIDTH), rev), vec, vec, b_spec, b_spec, c_spec, c_spec,
                   pl.BlockSpec((1, SSM_WIDTH), full2)],
        out_shape=[jax.ShapeDtypeStruct((s, SSM_WIDTH), BF16),
                   jax.ShapeDtypeStruct((1, N_STATE), F32), jax.ShapeDtypeStruct((1, N_STATE), F32),
                   jax.ShapeDtypeStruct((SSM_SUPER, 128, 512), F32), jax.ShapeDtypeStruct((SSM_SUPER, 128, 512), F32),
                   jax.ShapeDtypeStruct((SSM_SUPER, 512, 128), F32), jax.ShapeDtypeStruct((SSM_SUPER, 512, 128), F32),
                   jax.ShapeDtypeStruct((1, SSM_WIDTH), F32)],
        scratch_shapes=[pltpu.VMEM((chunk, N_STATE), F32), pltpu.VMEM((chunk, N_STATE), F32),
                        pltpu.VMEM((SUB, N_STATE), F32), pltpu.VMEM((SUB, N_STATE), F32)],
        compiler_params=_params(1),
    )(dys, z, h_re, h_im, h_re, h_im, b_re, b_im, c_re, c_im, lam_re, lam_im, d_skip)


def _local_step(x, p, pos, tgt, sm, wts):
    s = x.shape[0]
    tm = min(512, s)
    ts = min(1024, s)
    chunk = min(256, s)
    ni = s // tm
    nk = s // ts
    w_in, w_ap, w_ga, w_gb, w_out, w_fg, w_fu, w_fd, w_pg, w_pp = (
        wts[k] for k in ("w_in", "w_attn_proj", "w_glu_a", "w_glu_b", "w_out", "w_ffn_gate", "w_ffn_up", "w_ffn_down",
                         "w_ple_gate", "w_ple_proj"))
    w_out2 = w_out.reshape(D_MODEL, D_MODEL)
    w_pg2 = w_pg.reshape(D_MODEL, D_MODEL)
    g_mix, g_ffn, g_final = sm["g_mix"], sm["g_ffn"], sm["g_final"]
    rowblk, rowmap = _rows(tm, D_MODEL)
    vec1k = ((1, D_MODEL), lambda *_: (0, 0))

    (n1,) = _ew("rms_mix", (ni,), [(x, rowblk, rowmap), (g_mix, *vec1k)], [((s, D_MODEL), BF16, rowblk, rowmap)],
                lambda pids, h, g: ((_rms_fwd_tile(h, g),), ()))

    half_in = IN_WIDTH // 8
    tmb = min(1024, s)
    nib = s // tmb
    (z,) = _mm("in_proj", (nib, 8, 1),
               [(n1, (tmb, D_MODEL), lambda i, j, k: (i, 0), w_in, (None, D_MODEL, half_in), lambda i, j, k: (j // 2, 0, j % 2))],
               "nn", [((s, IN_WIDTH), BF16, (tmb, half_in), lambda i, j, k: (i, j))], j_outer=True)

    inv = ROPE_THETA ** (-jnp.arange(ROPE_HALF, dtype=F32) * 2.0 / ROPE_DIM)
    inv_row = jnp.concatenate([inv, inv, jnp.zeros((HEAD_DIM - ROPE_DIM,), F32)]).reshape(1, HEAD_DIM)
    tabs = _rope_tables(pos.astype(F32).reshape(s, 1), inv_row, tm)

    views, outs_g, lses_g = [], [], []
    for gi, dil in enumerate(GROUP_DILATIONS):
        q_g = _to_view(z[:, gi * GROUP_WIDTH:(gi + 1) * GROUP_WIDTH], dil)
        k_g = _to_view(z[:, 1536 + gi * GROUP_WIDTH:1536 + (gi + 1) * GROUP_WIDTH], dil)
        v_g = _to_view(z[:, 3072 + gi * GROUP_WIDTH:3072 + (gi + 1) * GROUP_WIDTH], dil)
        tabs_v = tuple(_to_view(t, dil) for t in tabs)
        views.append((q_g, k_g, v_g, tabs_v))
        o_g, l_g = _attn_fwd(q_g, k_g, v_g, tabs_v, dil)
        outs_g.append(_from_view(o_g, dil))
        lses_g.append(_from_view(l_g, dil))

    def merge_fn(pids, o0, o1, o2, l0, l1, l2):
        m = jnp.maximum(jnp.maximum(l0, l1), l2)
        e0, e1, e2 = jnp.exp(l0 - m), jnp.exp(l1 - m), jnp.exp(l2 - m)
        den = e0 + e1 + e2
        return ((e0 * o0 + e1 * o1 + e2 * o2) / den, m + jnp.log(den)), ()

    gblk, gmap = _rows(tm, GROUP_WIDTH)
    attn, lse = _ew("attn_merge", (ni,), [(a, gblk, gmap) for a in outs_g + lses_g],
                    [((s, GROUP_WIDTH), BF16, gblk, gmap), ((s, GROUP_WIDTH), F32, gblk, gmap)], merge_fn)

    def proj512(name, a, w):
        return _mm(name, (ni, N_CHIPS, 1),
                   [(a, (tm, GROUP_WIDTH), lambda i, j, k: (i, 0), w, (None, GROUP_WIDTH, 256), lambda i, j, k: (j, 0, 0))],
                   "nn", [((s, D_MODEL), BF16, (tm, 256), lambda i, j, k: (i, j))])[0]

    attn_d = proj512("attn_proj", attn, w_ap)

    bt_re = jnp.transpose(sm["b_re"], (0, 2, 1))
    bt_im = jnp.transpose(sm["b_im"], (0, 2, 1))
    log_dt_col = sm["log_dt"].reshape(SSM_GROUPS, 1)
    lam_re, lam_im, bbt_re, bbt_im = _ssm_prep(sm["a_re"], sm["a_im"], log_dt_col, bt_re, bt_im)
    b_re_m = _block_diag(bbt_re, SSM_GROUP, SSM_STATE).astype(BF16)
    b_im_m = _block_diag(bbt_im, SSM_GROUP, SSM_STATE).astype(BF16)
    c_re_m = _block_diag(jnp.transpose(sm["c_re"], (0, 2, 1)), SSM_STATE, SSM_GROUP).astype(BF16)
    c_im_m = _block_diag(jnp.transpose(sm["c_im"], (0, 2, 1)), SSM_STATE, SSM_GROUP).astype(BF16)
    lam_re_row = lam_re.reshape(1, N_STATE)
    lam_im_row = lam_im.reshape(1, N_STATE)
    d_skip_row = sm["d_skip"].reshape(1, SSM_WIDTH)
    h_re, h_im, ys, yg = _ssm_fwd(z, b_re_m, b_im_m, c_re_m, c_im_m, lam_re_row, lam_im_row, d_skip_row, chunk)

    pa = proj512("glu_a", yg, w_ga)
    pb = proj512("glu_b", yg, w_gb)

    ga_blk = ((tm, D_MODEL), lambda i: (i, COL_GA // D_MODEL))
    gs_blk = ((tm, D_MODEL), lambda i: (i, COL_GS // D_MODEL))

    def mix_fn(pids, ga, gs, ad, a, b):
        ga, gs, ad, a, b = (t.astype(F32) for t in (ga, gs, ad, a, b))
        return (_sig(ga) * ad + _sig(gs) * (a * _sig(b)),), ()

    (mix,) = _ew("gate_mix", (ni,), [(z, *ga_blk), (z, *gs_blk), (attn_d, rowblk, rowmap), (pa, rowblk, rowmap),
                                     (pb, rowblk, rowmap)], [((s, D_MODEL), BF16, rowblk, rowmap)], mix_fn)

    def out_epi(acc, xr, g):
        h1 = acc + xr
        return (h1, _rms_fwd_tile(h1, g)), ()

    m3 = lambda i, j, k: (i, 0)
    w3 = lambda i, j, k: (0, 0)
    h1, n2 = _mm("out_proj", (ni, 1, 1), [(mix, (tm, D_MODEL), m3, w_out2, (D_MODEL, D_MODEL), w3)], "nn",
                 [((s, D_MODEL), F32, (tm, D_MODEL), m3), ((s, D_MODEL), BF16, (tm, D_MODEL), m3)],
                 epilogue=out_epi, extras=[(x, (tm, D_MODEL), m3), (g_ffn, (1, D_MODEL), w3)])

    ffq = (None, tm, D_FF_Q)
    ffq_map = lambda i, j, k: (j, i, 0)

    def ffn_in_epi(parts):
        gt, u_ = parts
        return (gt, u_, gt * _sig(gt) * u_), ()

    w_ffq = (None, D_MODEL, D_FF_Q)
    w_ffq_j = lambda i, j, k: (j, 0, 0)
    gate, up, act = _mm("ffn_gate_up", (ni, N_CHIPS, 1),
                        [(n2, (tm, D_MODEL), m3, w_fg, w_ffq, w_ffq_j), (None, None, None, w_fu, w_ffq, w_ffq_j)], "nn",
                        [((N_CHIPS, s, D_FF_Q), BF16, ffq, ffq_map)] * 3, epilogue=ffn_in_epi, j_outer=True,
                        sum_pairs=False)

    (h2,) = _mm("ffn_down", (nib, 1, 1),
                [(act, (None, tmb, D_FF_Q), (lambda i, j, k, q=q: (q, i, 0)), w_fd, (None, D_FF_Q, D_MODEL),
                  (lambda i, j, k, q=q: (q, 0, 0))) for q in range(N_CHIPS)], "nn",
                [((s, D_MODEL), F32, (tmb, D_MODEL), m3)], epilogue=lambda acc, hr: ((acc + hr,), ()),
                extras=[(h1, (tmb, D_MODEL), m3)])

    (pp,) = _mm("ple_proj", (ni, N_CHIPS, 1),
                [(p, (tm, PLE_DIM), m3, w_pp, (None, PLE_DIM, 256), lambda i, j, k: (j, 0, 0))], "nn",
                [((s, D_MODEL), BF16, (tm, 256), lambda i, j, k: (i, j))])

    def ple_epi(acc, hr, ppr):
        return (acc, hr + _sig(acc) * ppr.astype(F32)), ()

    gl, h3 = _mm("ple_gate", (ni, 1, 1), [(h2, (tm, D_MODEL), m3, w_pg2, (D_MODEL, D_MODEL), w3)], "nn",
                 [((s, D_MODEL), BF16, (tm, D_MODEL), m3), ((s, D_MODEL), F32, (tm, D_MODEL), m3)],
                 epilogue=ple_epi, extras=[(h2, (tm, D_MODEL), m3), (pp, (tm, D_MODEL), m3)])

    def head_fn(pids, h, t, g):
        r = lax.rsqrt(jnp.mean(h * h, axis=-1, keepdims=True) + EPS)
        hhat = h * r
        diff = hhat * g - t
        loss = 0.5 * jnp.sum(jnp.mean(diff * diff, axis=-1, keepdims=True))
        dy = diff * (1.0 / D_MODEL)
        gy = dy * g
        dh = r * (gy - hhat * jnp.mean(gy * hhat, axis=-1, keepdims=True))
        return (dh,), (jnp.full((SUB, 128), loss, F32), jnp.sum(dy * hhat, axis=0, keepdims=True))

    dh3, loss_acc, dg_final = _ew("loss_head", (ni,), [(h3, rowblk, rowmap), (tgt, rowblk, rowmap), (g_final, *vec1k)],
                                  [((s, D_MODEL), F32, rowblk, rowmap)], head_fn,
                                  acc_outs=[((SUB, 128), F32), ((1, D_MODEL), F32)])

    def ple_bwd_fn(pids, dh, g_, ppr):
        sg = _sig(g_.astype(F32))
        return (dh * ppr.astype(F32) * sg * (1.0 - sg), dh * sg), ()

    dgl, dpp = _ew("ple_bwd", (ni,), [(dh3, rowblk, rowmap), (gl, rowblk, rowmap), (pp, rowblk, rowmap)],
                   [((s, D_MODEL), BF16, rowblk, rowmap)] * 2, ple_bwd_fn)

    def wgrad(name, a, a_block, a_imap, b, b_block, b_imap, out_shape, out_block, out_imap, nj, acc_shape):
        return _mm(name, (1, nj, nk), [(a, a_block, a_imap, b, b_block, b_imap)], "tn",
                   [(out_shape, F32, out_block, out_imap)], acc_shape=acc_shape)[0]

    tk0 = lambda i, j, k: (k, 0)
    tkj = lambda i, j, k: (k, j)
    d_w_pp = wgrad("d_ple_proj", p, (ts, PLE_DIM), tk0, dpp, (ts, 256), tkj, (N_CHIPS, PLE_DIM, 256),
                   (None, PLE_DIM, 256), lambda i, j, k: (j, 0, 0), N_CHIPS, (PLE_DIM, 256))
    d_w_pg = wgrad("d_ple_gate", h2, (ts, D_MODEL), tk0, dgl, (ts, D_MODEL), tk0, (D_MODEL, D_MODEL),
                   (D_MODEL, D_MODEL), w3, 1, (D_MODEL, D_MODEL))

    (dh2,) = _mm("ple_gate_bwd", (ni, 1, 1), [(dgl, (tm, D_MODEL), m3, w_pg2, (D_MODEL, D_MODEL), w3)], "nt",
                 [((s, D_MODEL), F32, (tm, D_MODEL), m3)], epilogue=lambda acc, d_: ((acc + d_,), ()),
                 extras=[(dh3, (tm, D_MODEL), m3)])

    def ffn_bwd_epi(acc, gt, u_):
        gt, u_ = gt.astype(F32), u_.astype(F32)
        sg = _sig(gt)
        return (acc * u_ * (sg * (1.0 + gt * (1.0 - sg))), acc * gt * sg), ()

    ffq_big = (None, tmb, D_FF_Q)
    dgate, dup = _mm("ffn_down_bwd", (nib, N_CHIPS, 1),
                     [(dh2, (tmb, D_MODEL), m3, w_fd, (None, D_FF_Q, D_MODEL), lambda i, j, k: (j, 0, 0))], "nt",
                     [((N_CHIPS, s, D_FF_Q), BF16, ffq_big, ffq_map)] * 2, epilogue=ffn_bwd_epi,
                     extras=[(gate, ffq_big, ffq_map), (up, ffq_big, ffq_map)])

    ffq_t = (None, ts, D_FF_Q)
    ffq_tmap = lambda i, j, k: (j, k, 0)
    blk_j = lambda i, j, k: (j, 0, 0)
    d_w_fd = wgrad("d_ffn_down", act, ffq_t, ffq_tmap, dh2, (ts, D_MODEL), tk0, (N_CHIPS, D_FF_Q, D_MODEL),
                   (None, D_FF_Q, D_MODEL), blk_j, N_CHIPS, (D_FF_Q, D_MODEL))
    d_w_fg = wgrad("d_ffn_gate", n2, (ts, D_MODEL), tk0, dgate, ffq_t, ffq_tmap, (N_CHIPS, D_MODEL, D_FF_Q),
                   (None, D_MODEL, D_FF_Q), blk_j, N_CHIPS, (D_MODEL, D_FF_Q))
    d_w_fu = wgrad("d_ffn_up", n2, (ts, D_MODEL), tk0, dup, ffq_t, ffq_tmap, (N_CHIPS, D_MODEL, D_FF_Q),
                   (None, D_MODEL, D_FF_Q), blk_j, N_CHIPS, (D_MODEL, D_FF_Q))

    def norm_bwd_epi(acc, h, d_res, g):
        dh, dg = _rms_bwd_tile(acc, h, g)
        return (d_res + dh,), (dg,)

    ffq_k = lambda i, j, k: (k, i, 0)
    blk_k = lambda i, j, k: (k, 0, 0)
    ffq_b = (None, tmb, D_FF_Q)
    dh1, dg_ffn = _mm("ffn_in_bwd", (nib, 1, N_CHIPS),
                      [(dgate, ffq_b, ffq_k, w_fg, (None, D_MODEL, D_FF_Q), blk_k),
                       (dup, ffq_b, ffq_k, w_fu, (None, D_MODEL, D_FF_Q), blk_k)], "nt",
                      [((s, D_MODEL), F32, (tmb, D_MODEL), m3)], epilogue=norm_bwd_epi,
                      extras=[(h1, (tmb, D_MODEL), m3), (dh2, (tmb, D_MODEL), m3), (g_ffn, (1, D_MODEL), w3)],
                      acc_outs=[((1, D_MODEL), F32)], acc_shape=(tmb, D_MODEL))

    (dmix,) = _mm("out_proj_bwd", (ni, 1, 1), [(dh1, (tm, D_MODEL), m3, w_out2, (D_MODEL, D_MODEL), w3)], "nt",
                  [((s, D_MODEL), BF16, (tm, D_MODEL), m3)])
    d_w_out = wgrad("d_out_proj", mix, (ts, D_MODEL), tk0, dh1, (ts, D_MODEL), tk0, (D_MODEL, D_MODEL),
                    (D_MODEL, D_MODEL), w3, 1, (D_MODEL, D_MODEL))

    def mix_bwd_fn(pids, dm, ga, gs, ad, a, b):
        dm, ga, gs, ad, a, b = (t.astype(F32) for t in (dm, ga, gs, ad, a, b))
        s_a, s_s, s_b = _sig(ga), _sig(gs), _sig(b)
        d_ssm = dm * s_s
        return (dm * ad * s_a * (1.0 - s_a), dm * (a * s_b) * s_s * (1.0 - s_s), dm * s_a, d_ssm * s_b,
                d_ssm * a * s_b * (1.0 - s_b)), ()

    dga, dgs, dattn_d, dpa, dpb = _ew(
        "gate_mix_bwd", (ni,),
        [(dmix, rowblk, rowmap), (z, *ga_blk), (z, *gs_blk), (attn_d, rowblk, rowmap), (pa, rowblk, rowmap),
         (pb, rowblk, rowmap)], [((s, D_MODEL), BF16, rowblk, rowmap)] * 5, mix_bwd_fn)

    def wgrad512(name, a, dy_):
        return wgrad(name, a, (ts, GROUP_WIDTH), tk0, dy_, (ts, 256), tkj, (N_CHIPS, GROUP_WIDTH, 256),
                     (None, GROUP_WIDTH, 256), blk_j, N_CHIPS, (GROUP_WIDTH, 256))

    d_w_ap = wgrad512("d_attn_proj", attn, dattn_d)
    d_w_ga = wgrad512("d_glu_a", yg, dpa)
    d_w_gb = wgrad512("d_glu_b", yg, dpb)

    ik = lambda i, j, k: (i, k)
    (dattn,) = _mm("attn_proj_bwd", (ni, 1, N_CHIPS),
                   [(dattn_d, (tm, 256), ik, w_ap, (None, GROUP_WIDTH, 256), blk_k)], "nt",
                   [((s, GROUP_WIDTH), BF16, (tm, GROUP_WIDTH), m3)], acc_shape=(tm, GROUP_WIDTH))

    (dys,) = _mm("glu_bwd", (ni, 1, N_CHIPS),
                 [(dpa, (tm, 256), ik, w_ga, (None, GROUP_WIDTH, 256), blk_k),
                  (dpb, (tm, 256), ik, w_gb, (None, GROUP_WIDTH, 256), blk_k)], "nt",
                 [((s, GROUP_WIDTH), F32, (tm, GROUP_WIDTH), m3)],
                 epilogue=lambda acc, y_: ((acc * _gelu_grad(y_),), ()),
                 extras=[(ys, (tm, GROUP_WIDTH), m3)], acc_shape=(tm, GROUP_WIDTH))

    du, d_lr, d_li, d_bre, d_bim, d_cre, d_cim, d_dskip = _ssm_bwd(
        dys, z, h_re, h_im, b_re_m, b_im_m, c_re_m, c_im_m, lam_re_row, lam_im_row, d_skip_row, chunk)

    dq_parts, dk_parts, dv_parts = [], [], []
    for gi, dil in enumerate(GROUP_DILATIONS):
        q_g, k_g, v_g, tabs_v = views[gi]
        dq_g, dk_g, dv_g = _attn_bwd(q_g, k_g, v_g, _to_view(dattn, dil), _to_view(attn, dil), _to_view(lse, dil), tabs_v, dil)
        dq_parts.append(_from_view(dq_g, dil))
        dk_parts.append(_from_view(dk_g, dil))
        dv_parts.append(_from_view(dv_g, dil))
    dz = jnp.concatenate(dq_parts + dk_parts + dv_parts + [du, dga, dgs], axis=1)

    kb = lambda i, j, k: (k // 2, 0, k % 2)
    grad_x, dg_mix = _mm("in_proj_bwd", (nib, 1, 8), [(dz, (tmb, half_in), ik, w_in, (None, D_MODEL, half_in), kb)], "nt",
                         [((s, D_MODEL), F32, (tmb, D_MODEL), m3)], epilogue=norm_bwd_epi,
                         extras=[(x, (tmb, D_MODEL), m3), (dh1, (tmb, D_MODEL), m3), (g_mix, (1, D_MODEL), w3)],
                         acc_outs=[((1, D_MODEL), F32)], acc_shape=(tmb, D_MODEL))
    d_w_in = wgrad("d_in_proj", n1, (ts, D_MODEL), tk0, dz, (ts, half_in), tkj, (N_CHIPS, D_MODEL, IN_WIDTH // N_CHIPS),
                   (None, D_MODEL, half_in), lambda i, j, k: (j // 2, 0, j % 2), 8, (D_MODEL, half_in))

    d_bbt_re = _block_diag_t(d_bre, SSM_GROUP, SSM_STATE)
    d_bbt_im = _block_diag_t(d_bim, SSM_GROUP, SSM_STATE)
    d_a_re, d_a_im, d_log_dt, d_bt_re, d_bt_im = _ssm_param_bwd(
        sm["a_re"], sm["a_im"], log_dt_col, bt_re, bt_im,
        d_lr.reshape(SSM_GROUPS, SSM_STATE), d_li.reshape(SSM_GROUPS, SSM_STATE), d_bbt_re, d_bbt_im)
    small = {
        "g_mix": dg_mix, "a_re": d_a_re, "a_im": d_a_im, "log_dt": d_log_dt,
        "b_re": jnp.transpose(d_bt_re, (0, 2, 1)), "b_im": jnp.transpose(d_bt_im, (0, 2, 1)),
        "c_re": jnp.transpose(_block_diag_t(d_cre, SSM_STATE, SSM_GROUP), (0, 2, 1)),
        "c_im": jnp.transpose(_block_diag_t(d_cim, SSM_STATE, SSM_GROUP), (0, 2, 1)),
        "d_skip": d_dskip, "g_ffn": dg_ffn, "g_final": dg_final,
    }
    big = {
        "w_in": d_w_in, "w_attn_proj": d_w_ap, "w_glu_a": d_w_ga, "w_glu_b": d_w_gb,
        "w_out": d_w_out.reshape(N_CHIPS, D_MODEL // N_CHIPS, D_MODEL), "w_ffn_gate": d_w_fg, "w_ffn_up": d_w_fu,
        "w_ffn_down": d_w_fd, "w_ple_gate": d_w_pg.reshape(N_CHIPS, D_MODEL // N_CHIPS, D_MODEL), "w_ple_proj": d_w_pp,
    }
    return loss_acc[0, 0], grad_x, big, small


BIG = ("w_in", "w_attn_proj", "w_glu_a", "w_glu_b", "w_out", "w_ffn_gate", "w_ffn_up", "w_ffn_down", "w_ple_gate",
       "w_ple_proj")
SMALL = ("g_mix", "a_re", "a_im", "log_dt", "b_re", "b_im", "c_re", "c_im", "d_skip", "g_ffn", "g_final")
ANY = pl.BlockSpec(memory_space=pl.ANY)


def _place():
    x, y, c = lax.axis_index("x"), lax.axis_index("y"), lax.axis_index("c")
    chips = [(1 - x, y), (x, 1 - y), (1 - x, 1 - y)]
    return x, y, c, chips


def _remote(src, dst, send_sem, recv_sem, to):
    return pltpu.make_async_remote_copy(src_ref=src, dst_ref=dst, send_sem=send_sem, recv_sem=recv_sem, device_id=to,
                                        device_id_type=MESH)


def _comm_call(name, body, ins, out_shapes, n_sems, aliases=None):
    n_w = len(ins)
    return pl.pallas_call(
        body, name=name, in_specs=[ANY] * n_w, out_specs=[ANY] * len(out_shapes), out_shape=out_shapes,
        scratch_shapes=[pltpu.SemaphoreType.DMA((n,)) for n in n_sems], input_output_aliases=aliases or {},
    )(*ins)


def _gather_weights(bufs):
    n_w = len(bufs)

    def body(*refs):
        outs = refs[n_w:2 * n_w]
        ici_send, ici_recv, d2d_send, d2d_recv = refs[2 * n_w:]
        x, y, c, chips = _place()
        me = 2 * x + y
        sib = (x, y, 1 - c)
        sends = []
        for w in range(n_w):
            for j, (cx, cy) in enumerate(chips):
                k = 3 * w + j
                mine = outs[w].at[me, c]
                cp = _remote(mine, mine, ici_send.at[k], ici_recv.at[k], (cx, cy, c))
                cp.start()
                sends.append(cp)
        for w in range(n_w):
            for j, (cx, cy) in enumerate(chips):
                k = 3 * w + j
                src_chip = 2 * cx + cy
                landed = outs[w].at[src_chip, c]
                _remote(landed, landed, ici_send.at[k], ici_recv.at[k], (cx, cy, c)).wait_recv()
                fwd = _remote(landed, landed, d2d_send.at[k], d2d_recv.at[k], sib)
                fwd.start()
                sends.append(fwd)
        for w in range(n_w):
            for j, (cx, cy) in enumerate(chips):
                k = 3 * w + j
                other = outs[w].at[2 * cx + cy, 1 - c]
                _remote(other, other, d2d_send.at[k], d2d_recv.at[k], sib).wait_recv()
        for cp in sends:
            cp.wait_send()

    out_shapes = [jax.ShapeDtypeStruct(b.shape, b.dtype) for b in bufs]
    return _comm_call("gather_weights", body, bufs, out_shapes, [3 * n_w] * 4, aliases={w: w for w in range(n_w)})


def _pair_exchange(grads):
    n_w = len(grads)

    def body(*refs):
        ins, outs = refs[:n_w], refs[n_w:2 * n_w]
        send, recv = refs[2 * n_w:]
        x, y, c, _ = _place()
        sib = (x, y, 1 - c)
        cps = []
        for w in range(n_w):
            for q in range(N_CHIPS):
                k = N_CHIPS * w + q
                cp = _remote(ins[w].at[q, 1 - c], outs[w].at[q], send.at[k], recv.at[k], sib)
                cp.start()
                cps.append(cp)
        for cp in cps:
            cp.wait()

    out_shapes = [jax.ShapeDtypeStruct((N_CHIPS,) + g.shape[2:], g.dtype) for g in grads]
    return _comm_call("grad_pair_exchange", body, grads, out_shapes, [N_CHIPS * n_w] * 2)


def _chip_exchange(parts):
    n_w = len(parts)

    def body(*refs):
        ins, outs = refs[:n_w], refs[n_w:2 * n_w]
        send, recv = refs[2 * n_w:]
        x, y, c, chips = _place()
        me = 2 * x + y
        cps = []
        for w in range(n_w):
            for j, (cx, cy) in enumerate(chips):
                k = 3 * w + j
                cp = _remote(ins[w].at[2 * cx + cy], outs[w].at[me], send.at[k], recv.at[k], (cx, cy, c))
                cp.start()
                cps.append(cp)
        for w in range(n_w):
            for j, (cx, cy) in enumerate(chips):
                k = 3 * w + j
                got = outs[w].at[2 * cx + cy]
                _remote(got, got, send.at[k], recv.at[k], (cx, cy, c)).wait_recv()
        for cp in cps:
            cp.wait_send()

    out_shapes = [jax.ShapeDtypeStruct(t.shape, t.dtype) for t in parts]
    return _comm_call("grad_chip_exchange", body, parts, out_shapes, [3 * n_w, 3 * n_w])


def _pair_gather(halves):
    n_w = len(halves)

    def body(*refs):
        ins, outs = refs[:n_w], refs[n_w:2 * n_w]
        send, recv = refs[2 * n_w:]
        x, y, c, _ = _place()
        sib = (x, y, 1 - c)
        cps = []
        for w in range(n_w):
            cp = _remote(ins[w], outs[w], send.at[w], recv.at[w], sib)
            cp.start()
            cps.append(cp)
        for cp in cps:
            cp.wait()

    out_shapes = [jax.ShapeDtypeStruct(h.shape, h.dtype) for h in halves]
    return _comm_call("grad_pair_gather", body, halves, out_shapes, [n_w] * 2)


def _all_exchange(vec):
    def body(in_ref, out_ref, send, recv):
        x, y, c, _ = _place()
        me = 4 * x + 2 * y + c
        cps = []
        for k in range(1, 8):
            fx, fy, fc = (k >> 2) & 1, (k >> 1) & 1, k & 1
            to = (x ^ fx, y ^ fy, c ^ fc)
            cp = _remote(in_ref, out_ref.at[me], send.at[k - 1], recv.at[k - 1], to)
            cp.start()
            cps.append(cp)
        for k in range(1, 8):
            fx, fy, fc = (k >> 2) & 1, (k >> 1) & 1, k & 1
            src = 4 * (x ^ fx) + 2 * (y ^ fy) + (c ^ fc)
            got = out_ref.at[src]
            _remote(got, got, send.at[k - 1], recv.at[k - 1], (x ^ fx, y ^ fy, c ^ fc)).wait_recv()
        for cp in cps:
            cp.wait_send()

    return _comm_call("small_all_exchange", body, [vec], [jax.ShapeDtypeStruct((8,) + vec.shape, vec.dtype)], [7, 7])[0]


def _row_tile(r):
    for t in (256, 128, 176, 64, 32, 16, 8):
        if r % t == 0:
            return t
    return r


P_C, P_CHIP, P_DEV = 2, 3, 4


def _cast_into_slot(w2, place):
    r, c = w2.shape
    t = _row_tile(r)
    return _ew("cast_shard", (r // t,), [(w2, (t, c), lambda i, pv: (i, 0))],
               [((N_CHIPS, r, c), BF16, (None, t, c), lambda i, pv: (pv[P_CHIP], i, 0))],
               lambda pids, a: ((a,), ()), place=place)[0]


def _pair_sum(mine, theirs, place):
    _, r, c = theirs.shape
    t = _row_tile(r)
    own = ((None, None, t, c), lambda q, i, pv: (q, pv[P_C], i, 0))
    blk = ((None, t, c), lambda q, i, pv: (q, i, 0))
    return _ew("grad_pair_sum", (N_CHIPS, r // t), [(mine, *own), (theirs, *blk)], [((N_CHIPS, r, c), BF16, *blk)],
               lambda pids, a, b: ((a + b,), ()), place=place)[0]


def _chip_sum(own, got, place):
    _, r, c = own.shape
    t = _row_tile(r)
    ins = []
    for q in range(N_CHIPS):
        ins.append((own, (None, t, c), (lambda i, pv, q=q: (q, i, 0))))
        ins.append((got, (None, t, c), (lambda i, pv, q=q: (jnp.where(pv[P_CHIP] == q, (q + 1) % N_CHIPS, q), i, 0))))

    def fn(pids, *tiles):
        me = pids[0][P_CHIP]
        tot = None
        for q in range(N_CHIPS):
            term = jnp.where(me == q, tiles[2 * q], tiles[2 * q + 1]).astype(F32)
            tot = term if tot is None else tot + term
        return (tot,), ()

    return _ew("grad_chip_sum", (r // t,), ins, [((r, c), F32, (t, c), lambda i, pv: (i, 0))], fn, place=place)[0]


def _adamw_tile(w, g, m, v):
    m = ADAM_B1 * m + (1.0 - ADAM_B1) * g
    v = ADAM_B2 * v + (1.0 - ADAM_B2) * (g * g)
    m_hat = m / (1.0 - ADAM_B1 ** ADAM_STEP)
    v_hat = v / (1.0 - ADAM_B2 ** ADAM_STEP)
    delta = -ADAM_LR * (m_hat / (jnp.sqrt(v_hat) + ADAM_EPS) + ADAM_WD * w)
    return delta, m, v


def _adamw(name, g2, w2, m2, v2):
    r, c = w2.shape
    t = _row_tile(r)
    blk, imap = _rows(t, c)

    def fn(pids, g, w, m, v):
        delta, nm, nv = _adamw_tile(w, g, m, v)
        return (g, delta, nm, nv), ()

    return _ew(name, (r // t,), [(a, blk, imap) for a in (g2, w2, m2, v2)], [((r, c), F32, blk, imap)] * 4, fn)


def _adamw_halves(name, mine, theirs, w2, m2, v2, place):
    r, c = w2.shape
    t = _row_tile(r // 2)
    n_t = (r // 2) // t
    half = ((t, c), lambda h, i, pv: (i, 0))
    whole = ((t, c), lambda h, i, pv: (h * n_t + i, 0))

    def fn(pids, ga, gb, w, m, v):
        g = jnp.where(pids[1] == pids[0][P_C], ga, gb)
        delta, nm, nv = _adamw_tile(w, g, m, v)
        return (g, delta, nm, nv), ()

    return _ew(name, (2, n_t), [(mine, *half), (theirs, *half), (w2, *whole), (m2, *whole), (v2, *whole)],
               [((r, c), F32, *whole)] * 4, fn, place=place)


def _device_sum(own, got, place):
    r, c = own.shape
    t = _row_tile(r)
    ins = [(own, (t, c), lambda i, pv: (i, 0))]
    for q in range(8):
        ins.append((got, (None, t, c), (lambda i, pv, q=q: (jnp.where(pv[P_DEV] == q, (q + 1) % 8, q), i, 0))))

    def fn(pids, mine, *parts):
        me = pids[0][P_DEV]
        tot = None
        for q in range(8):
            term = jnp.where(me == q, mine, parts[q])
            tot = term if tot is None else tot + term
        return (tot,), ()

    return _ew("small_device_sum", (r // t,), ins, [((r, c), F32, (t, c), lambda i, pv: (i, 0))], fn, place=place)[0]


def _pack(parts):
    flat = jnp.concatenate([a.reshape(-1) for a in parts])
    pad = (-flat.shape[0]) % (SUB * 128)
    return jnp.pad(flat, (0, pad)).reshape(-1, 128)


def _unpack(mat, shapes):
    flat = mat.reshape(-1)
    out, off = [], 0
    for shp in shapes:
        n = math.prod(shp)
        out.append(flat[off:off + n].reshape(shp))
        off += n
    return out


def kernel(x, p, positions, g_mix, w_in, a_re, a_im, log_dt, b_re, b_im, c_re, c_im, d_skip, w_attn_proj, w_glu_a, w_glu_b, w_out, g_ffn, w_ffn_gate, w_ffn_up, w_ffn_down, w_ple_gate, w_ple_proj, g_final, loss_target, m_g_mix, m_w_in, m_a_re, m_a_im, m_log_dt, m_b_re, m_b_im, m_c_re, m_c_im, m_d_skip, m_w_attn_proj, m_w_glu_a, m_w_glu_b, m_w_out, m_g_ffn, m_w_ffn_gate, m_w_ffn_up, m_w_ffn_down, m_w_ple_gate, m_w_ple_proj, m_g_final, v_g_mix, v_w_in, v_a_re, v_a_im, v_log_dt, v_b_re, v_b_im, v_c_re, v_c_im, v_d_skip, v_w_attn_proj, v_w_glu_a, v_w_glu_b, v_w_out, v_g_ffn, v_w_ffn_gate, v_w_ffn_up, v_w_ffn_down, v_w_ple_gate, v_w_ple_proj, v_g_final):
    given = dict(locals())
    big_w = {n: given[n] for n in BIG}
    w_mats = {n: big_w[n].reshape(big_w[n].shape[1:]) for n in BIG}

    ax, ay, ac = lax.axis_index("x"), lax.axis_index("y"), lax.axis_index("c")
    place = jnp.stack([ax, ay, ac, 2 * ax + ay, 4 * ax + 2 * ay + ac]).astype(jnp.int32)

    bufs = []
    for n in BIG:
        r, c = w_mats[n].shape
        bufs.append(_cast_into_slot(w_mats[n], place).reshape(N_CHIPS, 2, r // 2, c))
    gathered = _gather_weights(bufs)
    wts = {}
    for n, g in zip(BIG, gathered):
        r, c = w_mats[n].shape
        wts[n] = g.reshape(N_CHIPS, r, c)

    sm = {
        "g_mix": g_mix.reshape(1, D_MODEL), "g_ffn": g_ffn.reshape(1, D_MODEL), "g_final": g_final.reshape(1, D_MODEL),
        "a_re": a_re[0], "a_im": a_im[0], "log_dt": log_dt[0], "b_re": b_re[0], "b_im": b_im[0], "c_re": c_re[0],
        "c_im": c_im[0], "d_skip": d_skip[0],
    }
    s = x.shape[1]
    loss_part, grad_x, big_g, small_g = _local_step(x[0], p[0, 0], positions[0], loss_target[0], sm, wts)

    g5 = []
    for n in BIG:
        r, c = w_mats[n].shape
        g5.append(big_g[n].reshape(N_CHIPS, 2, r // 2, c))
    theirs = _pair_exchange(g5)
    chip_parts = [_pair_sum(g, t, place) for g, t in zip(g5, theirs)]
    chip_got = _chip_exchange(chip_parts)
    halves = [_chip_sum(own, got, place) for own, got in zip(chip_parts, chip_got)]
    other_halves = _pair_gather(halves)

    results = {}
    for n, mine, other in zip(BIG, halves, other_halves):
        r, c = w_mats[n].shape
        shp = big_w[n].shape
        outs = _adamw_halves("adamw_" + n, mine, other, w_mats[n], given["m_" + n].reshape(r, c),
                             given["v_" + n].reshape(r, c), place)
        results[n] = [o.reshape(shp) for o in outs]

    small_shapes = [given[n].shape for n in SMALL]
    vec = _pack([small_g[n] for n in SMALL] + [loss_part.reshape(1)])
    tot = _device_sum(vec, _all_exchange(vec), place)
    n_small = sum(math.prod(shp) for shp in small_shapes)
    loss = tot.reshape(-1)[n_small]
    w_s = _pack([given[n] for n in SMALL])
    m_s = _pack([given["m_" + n] for n in SMALL])
    v_s = _pack([given["v_" + n] for n in SMALL])
    rows_s = w_s.shape[0]
    g_s = tot.reshape(-1)[: rows_s * 128].reshape(rows_s, 128)
    outs_s = _adamw("adamw_small", g_s, w_s, m_s, v_s)
    for kind, mat in enumerate(outs_s):
        for n, arr in zip(SMALL, _unpack(mat, small_shapes)):
            results.setdefault(n, [None] * 4)[kind] = arr

    order = ("g_mix", "w_in", "a_re", "a_im", "log_dt", "b_re", "b_im", "c_re", "c_im", "d_skip", "w_attn_proj", "w_glu_a",
             "w_glu_b", "w_out", "g_ffn", "w_ffn_gate", "w_ffn_up", "w_ffn_down", "w_ple_gate", "w_ple_proj", "g_final")
    out = [loss, grad_x.reshape(1, s, D_MODEL)]
    for kind in range(4):
        out += [results[n][kind] for n in order]
    return tuple(out)
```

```python
import math

import jax
import jax.numpy as jnp
from jax import lax
from jax.experimental import pallas as pl
from jax.experimental.pallas import tpu as pltpu

F32 = jnp.float32
BF16 = jnp.bfloat16

D_MODEL = 1024
HEAD_DIM = 128
HEADS_PER_GROUP = 4
GROUP_WIDTH = HEADS_PER_GROUP * HEAD_DIM
GROUP_DILATIONS = (1, 4, 16)
ATTN_BLOCK = 128
ROPE_DIM = 32
ROPE_HALF = 16
ROPE_THETA = 500000.0
SSM_WIDTH = 512
SSM_GROUPS = 32
SSM_GROUP = 16
SSM_STATE = 64
N_STATE = SSM_GROUPS * SSM_STATE
SSM_SUPER = 4
IN_WIDTH = 7168
COL_U = 4608
COL_GA = 5120
COL_GS = 6144
D_FF = 2816
N_CHIPS = 4
D_FF_Q = D_FF // N_CHIPS
PLE_DIM = 256
EPS = 1e-6
ADAM_LR = 0.001
ADAM_B1 = 0.9
ADAM_B2 = 0.999
ADAM_EPS = 1e-08
ADAM_WD = 0.01
ADAM_STEP = 10
NEG_BIG = -1e30
VMEM_LIMIT_BYTES = 56 * 1024 * 1024
MESH = pl.DeviceIdType.MESH

_DIMS = {
    "nn": (((1,), (0,)), ((), ())),
    "nt": (((1,), (1,)), ((), ())),
    "tn": (((0,), (0,)), ((), ())),
}


def _params(n_grid):
    return pltpu.CompilerParams(dimension_semantics=("arbitrary",) * n_grid, vmem_limit_bytes=VMEM_LIMIT_BYTES)


def _sig(v):
    return 1.0 / (1.0 + jnp.exp(-v))


def _dot(a, b, mode):
    return lax.dot_general(a, b, _DIMS[mode], preferred_element_type=F32)


def _mm(name, grid, pairs, mode, outs, epilogue=None, extras=(), acc_outs=(), acc_shape=None, j_outer=False,
        sum_pairs=True):
    gi, gj, gk = grid
    n_p, n_e, n_o, n_a = len(pairs), len(extras), len(outs), len(acc_outs)
    assert not n_a or gj == 1
    assert sum_pairs or gk == 1

    def order(imap):
        return (lambda j, i, k: imap(i, j, k)) if j_outer else imap

    shared_a = [pr[0] is None for pr in pairs]
    n_in = 2 * n_p - sum(shared_a)

    def body(*refs):
        pair_refs = list(refs[:n_in])
        extra_refs = refs[n_in: n_in + n_e]
        out_refs = refs[n_in + n_e: n_in + n_e + n_o]
        sum_refs = refs[n_in + n_e + n_o: n_in + n_e + n_o + n_a]
        i = pl.program_id(1 if j_outer else 0)
        k = pl.program_id(2)
        part = None if sum_pairs else []
        a = None
        for t in range(n_p):
            if not shared_a[t]:
                a = pair_refs.pop(0)[...].astype(BF16)
            b = pair_refs.pop(0)[...].astype(BF16)
            d = _dot(a, b, mode)
            if sum_pairs:
                part = d if part is None else part + d
            else:
                part.append(d)

        def finish(acc):
            tiles, sums = epilogue(acc, *[e[...] for e in extra_refs]) if epilogue is not None else ((acc,), ())
            for o_ref, tile in zip(out_refs, tiles):
                o_ref[...] = tile.astype(o_ref.dtype)
            if n_a:
                @pl.when(i == 0)
                def _():
                    for s_ref in sum_refs:
                        s_ref[...] = jnp.zeros_like(s_ref)

                for s_ref, s in zip(sum_refs, sums):
                    s_ref[...] += s

        if gk == 1:
            finish(part)
        else:
            acc_ref = refs[-1]

            @pl.when(k == 0)
            def _():
                acc_ref[...] = part

            @pl.when(k > 0)
            def _():
                acc_ref[...] += part

            @pl.when(k == gk - 1)
            def _():
                finish(acc_ref[...])

    in_specs, args = [], []
    for a, a_block, a_imap, b, b_block, b_imap in pairs:
        if a is not None:
            in_specs.append(pl.BlockSpec(a_block, order(a_imap)))
            args.append(a)
        in_specs.append(pl.BlockSpec(b_block, order(b_imap)))
        args.append(b)
    for e, e_block, e_imap in extras:
        in_specs.append(pl.BlockSpec(e_block, order(e_imap)))
        args.append(e)
    out_shape = [jax.ShapeDtypeStruct(shape, dtype) for shape, dtype, _, _ in outs]
    out_specs = [pl.BlockSpec(block, order(imap)) for _, _, block, imap in outs]
    for shape, dtype in acc_outs:
        out_shape.append(jax.ShapeDtypeStruct(shape, dtype))
        out_specs.append(pl.BlockSpec(shape, lambda i, j, k: (0, 0)))
    scratch = [pltpu.VMEM(acc_shape, F32)] if gk > 1 else []
    return pl.pallas_call(
        body, name=name, grid=(gj, gi, gk) if j_outer else grid, in_specs=in_specs, out_specs=out_specs,
        out_shape=out_shape, scratch_shapes=scratch, compiler_params=_params(3),
    )(*args)


def _ew(name, grid, ins, outs, fn, acc_outs=(), place=None):
    n_i, n_o, n_a = len(ins), len(outs), len(acc_outs)
    ng = len(grid)
    n_s = 0 if place is None else 1

    def body(*refs):
        in_refs = refs[n_s: n_s + n_i]
        out_refs = refs[n_s + n_i: n_s + n_i + n_o]
        sum_refs = refs[n_s + n_i + n_o:]
        pids = tuple(pl.program_id(a) for a in range(ng))
        if n_s:
            pids = (refs[0],) + pids
        tiles, sums = fn(pids, *[r[...] for r in in_refs])
        for o_ref, tile in zip(out_refs, tiles):
            o_ref[...] = tile.astype(o_ref.dtype)
        if n_a:
            first = pids[0] == 0
            for p_ in pids[1:]:
                first = jnp.logical_and(first, p_ == 0)

            @pl.when(first)
            def _():
                for s_ref in sum_refs:
                    s_ref[...] = jnp.zeros_like(s_ref)

            for s_ref, s in zip(sum_refs, sums):
                s_ref[...] += s

    in_specs = [pl.BlockSpec(block, imap) for _, block, imap in ins]
    out_shape = [jax.ShapeDtypeStruct(shape, dtype) for shape, dtype, _, _ in outs]
    out_specs = [pl.BlockSpec(block, imap) for _, _, block, imap in outs]
    for shape, dtype in acc_outs:
        out_shape.append(jax.ShapeDtypeStruct(shape, dtype))
        out_specs.append(pl.BlockSpec(shape, lambda *_, nd=len(shape): (0,) * nd))
    arrays = [a for a, _, _ in ins]
    if n_s:
        assert not n_a
        spec = pltpu.PrefetchScalarGridSpec(num_scalar_prefetch=1, grid=grid, in_specs=in_specs, out_specs=out_specs)
        return pl.pallas_call(body, name=name, grid_spec=spec, out_shape=out_shape, compiler_params=_params(ng))(
            place, *arrays)
    return pl.pallas_call(
        body, name=name, grid=grid, in_specs=in_specs, out_specs=out_specs, out_shape=out_shape,
        compiler_params=_params(ng),
    )(*arrays)


def _rows(tm, width):
    return (tm, width), (lambda i: (i, 0))


def _rms_fwd_tile(h, g):
    r = lax.rsqrt(jnp.mean(h * h, axis=-1, keepdims=True) + EPS)
    return h * r * g


def _rms_bwd_tile(dn, h, g):
    r = lax.rsqrt(jnp.mean(h * h, axis=-1, keepdims=True) + EPS)
    hhat = h * r
    gy = dn * g
    dh = r * (gy - hhat * jnp.mean(gy * hhat, axis=-1, keepdims=True))
    dg = jnp.sum(dn * hhat, axis=0, keepdims=True)
    return dh, dg


def _rope_tables(pos_col, inv_row, tm):
    s = pos_col.shape[0]

    def fn(pids, pos, inv):
        ang = pos * inv
        lane = lax.broadcasted_iota(jnp.int32, ang.shape, 1)
        cs = jnp.where(lane < ROPE_DIM, jnp.cos(ang), 1.0)
        sn = jnp.sin(ang)
        s_lo = jnp.where(lane < ROPE_HALF, -sn, 0.0)
        s_hi = jnp.where(jnp.logical_and(lane >= ROPE_HALF, lane < ROPE_DIM), sn, 0.0)
        return (cs, s_lo, s_hi), ()

    blk, imap = _rows(tm, 128)
    return _ew(
        "rope_tables", (s // tm,),
        [(pos_col, (tm, 1), lambda i: (i, 0)), (inv_row, (1, 128), lambda i: (0, 0))],
        [((s, 128), F32, blk, imap)] * 3, fn,
    )


def _rope(xh, cs, s_lo, s_hi):
    return xh * cs + pltpu.roll(xh, HEAD_DIM - ROPE_HALF, 1) * s_lo + pltpu.roll(xh, ROPE_HALF, 1) * s_hi


def _rope_t(gh, cs, s_lo, s_hi):
    return gh * cs + pltpu.roll(gh * s_lo, ROPE_HALF, 1) + pltpu.roll(gh * s_hi, HEAD_DIM - ROPE_HALF, 1)


def _attn_geometry(length):
    nb = length // ATTN_BLOCK
    gq = min(4, nb)
    assert nb % gq == 0
    return nb, gq, gq * ATTN_BLOCK, nb // gq


def _band_masks():
    qi = lax.broadcasted_iota(jnp.int32, (ATTN_BLOCK, ATTN_BLOCK), 0)
    kj = lax.broadcasted_iota(jnp.int32, (ATTN_BLOCK, ATTN_BLOCK), 1)
    return kj <= qi, kj >= qi


def _band_mask_pair():
    qi = lax.broadcasted_iota(jnp.int32, (ATTN_BLOCK, 2 * ATTN_BLOCK), 0)
    cj = lax.broadcasted_iota(jnp.int32, (ATTN_BLOCK, 2 * ATTN_BLOCK), 1)
    in_cur = cj >= ATTN_BLOCK
    band = jnp.logical_or(jnp.logical_and(in_cur, cj - ATTN_BLOCK <= qi),
                          jnp.logical_and(cj < ATTN_BLOCK, cj >= qi))
    return band, in_cur


def _attn_fwd(qv, kv, vv, tabs_v, dil):
    length = qv.shape[0]
    nb, gq, rows, ni = _attn_geometry(length)
    scale = 1.0 / math.sqrt(HEAD_DIM)

    def body(q_ref, kc_ref, kp_ref, vc_ref, vp_ref, cc, lc, hc, cp_, lp, hp, o_ref, l_ref):
        i = pl.program_id(1)
        band, in_cur = _band_mask_pair()
        band_first = jnp.logical_and(band, jnp.logical_or(in_cur, i > 0))
        tc = (cc[...], lc[...], hc[...])
        tp = (cp_[...], lp[...], hp[...])
        for h in range(HEADS_PER_GROUP):
            cols = slice(h * HEAD_DIM, (h + 1) * HEAD_DIM)
            qh = (_rope(q_ref[:, cols].astype(F32), *tc) * scale).astype(BF16)
            k_all = jnp.concatenate([_rope(kp_ref[:, cols].astype(F32), *tp).astype(BF16),
                                     _rope(kc_ref[:, cols].astype(F32), *tc).astype(BF16)], axis=0)
            v_all = jnp.concatenate([vp_ref[:, cols], vc_ref[:, cols]], axis=0)
            for jj in range(gq):
                rws = slice(jj * ATTN_BLOCK, (jj + 1) * ATTN_BLOCK)
                two = slice(jj * ATTN_BLOCK, (jj + 2) * ATTN_BLOCK)
                s = jnp.where(band_first if jj == 0 else band, _dot(qh[rws], k_all[two], "nt"), NEG_BIG)
                m = jnp.max(s, axis=-1, keepdims=True)
                pexp = jnp.exp(s - m)
                den = jnp.sum(pexp, axis=-1, keepdims=True)
                o = _dot(pexp.astype(BF16), v_all[two], "nn")
                o_ref[rws, cols] = (o / den).astype(o_ref.dtype)
                l_ref[rws, cols] = jnp.broadcast_to(m + jnp.log(den), (ATTN_BLOCK, HEAD_DIM))

    cur = lambda r, i: (i, r)
    prev = lambda r, i: (jnp.maximum(i * gq - 1, 0), r)
    wide = pl.BlockSpec((rows, GROUP_WIDTH), cur)
    wide_prev = pl.BlockSpec((ATTN_BLOCK, GROUP_WIDTH), prev)
    tab = pl.BlockSpec((rows, HEAD_DIM), cur)
    tab_prev = pl.BlockSpec((ATTN_BLOCK, HEAD_DIM), prev)
    return pl.pallas_call(
        body, name=f"attn_fwd_d{dil}", grid=(dil, ni),
        in_specs=[wide, wide, wide_prev, wide, wide_prev, tab, tab, tab, tab_prev, tab_prev, tab_prev],
        out_specs=[wide, wide],
        out_shape=[jax.ShapeDtypeStruct(qv.shape, BF16), jax.ShapeDtypeStruct(qv.shape, F32)],
        compiler_params=_params(2),
    )(qv, kv, kv, vv, vv, *tabs_v, *tabs_v)


def _attn_bwd(qv, kv, vv, dov, ov, lv, tabs_v, dil):
    length = qv.shape[0]
    nb, gq, rows, ni = _attn_geometry(length)
    scale = 1.0 / math.sqrt(HEAD_DIM)

    def body(qc_ref, qn_ref, kc_ref, kp_ref, vc_ref, vp_ref, doc_ref, don_ref, oc_ref, on_ref, lc_ref, ln_ref,
             cc, lc, hc, cp_, lp, hp, cn, ln, hn, dq_ref, dk_ref, dv_ref):
        i = pl.program_id(1)
        _, mask_p = _band_masks()
        band, in_cur = _band_mask_pair()
        band_first = jnp.logical_and(band, jnp.logical_or(in_cur, i > 0))
        has_next = i < ni - 1
        tc = (cc[...], lc[...], hc[...])
        tp = (cp_[...], lp[...], hp[...])
        tn = (cn[...], ln[...], hn[...])

        def tile(qb, kb, vb, dob, l_col, delta, mask):
            s = _dot(qb, kb, "nt")
            p = jnp.where(mask, jnp.exp(s - l_col), 0.0)
            dp = _dot(dob, vb, "nt")
            ds = p * (dp - delta)
            return p, ds

        for h in range(HEADS_PER_GROUP):
            cols = slice(h * HEAD_DIM, (h + 1) * HEAD_DIM)
            q_c = (_rope(qc_ref[:, cols].astype(F32), *tc) * scale).astype(BF16)
            q_n = (_rope(qn_ref[:, cols].astype(F32), *tn) * scale).astype(BF16)
            k_all = jnp.concatenate([_rope(kp_ref[:, cols].astype(F32), *tp).astype(BF16),
                                     _rope(kc_ref[:, cols].astype(F32), *tc).astype(BF16)], axis=0)
            v_all = jnp.concatenate([vp_ref[:, cols], vc_ref[:, cols]], axis=0)
            do_c = doc_ref[:, cols]
            do_n = don_ref[:, cols]
            l_c = lc_ref[:, h * HEAD_DIM:h * HEAD_DIM + 1]
            l_n = ln_ref[:, h * HEAD_DIM:h * HEAD_DIM + 1]
            dl_c = jnp.sum(do_c.astype(F32) * oc_ref[:, cols].astype(F32), axis=-1, keepdims=True)
            dl_n = jnp.sum(do_n.astype(F32) * on_ref[:, cols].astype(F32), axis=-1, keepdims=True)
            dq_blocks, dk_blocks, dv_blocks = [], [None] * (gq + 1), [None] * (gq + 1)

            def add(lst, idx, val):
                lst[idx] = val if lst[idx] is None else lst[idx] + val

            for jj in range(gq):
                rws = slice(jj * ATTN_BLOCK, (jj + 1) * ATTN_BLOCK)
                two = slice(jj * ATTN_BLOCK, (jj + 2) * ATTN_BLOCK)
                qb, dob = q_c[rws], do_c[rws]
                p, ds = tile(qb, k_all[two], v_all[two], dob, l_c[rws], dl_c[rws], band_first if jj == 0 else band)
                dsb = ds.astype(BF16)
                dq_blocks.append(_dot(dsb, k_all[two], "nn"))
                dk2 = _dot(dsb, qb, "tn")
                dv2 = _dot(p.astype(BF16), dob, "tn")
                add(dk_blocks, jj, dk2[:ATTN_BLOCK])
                add(dk_blocks, jj + 1, dk2[ATTN_BLOCK:])
                add(dv_blocks, jj, dv2[:ATTN_BLOCK])
                add(dv_blocks, jj + 1, dv2[ATTN_BLOCK:])
            last = slice(gq * ATTN_BLOCK, (gq + 1) * ATTN_BLOCK)
            p, ds = tile(q_n, k_all[last], v_all[last], do_n, l_n, dl_n, jnp.logical_and(mask_p, has_next))
            add(dk_blocks, gq, _dot(ds.astype(BF16), q_n, "tn"))
            add(dv_blocks, gq, _dot(p.astype(BF16), do_n, "tn"))
            dk_blocks, dv_blocks = dk_blocks[1:], dv_blocks[1:]
            for jj in range(gq):
                rws = slice(jj * ATTN_BLOCK, (jj + 1) * ATTN_BLOCK)
                t_rows = tuple(t[rws] for t in tc)
                dq_ref[rws, cols] = _rope_t(dq_blocks[jj] * scale, *t_rows).astype(dq_ref.dtype)
                dk_ref[rws, cols] = _rope_t(dk_blocks[jj], *t_rows).astype(dk_ref.dtype)
                dv_ref[rws, cols] = dv_blocks[jj].astype(dv_ref.dtype)

    cur = lambda r, i: (i, r)
    prev = lambda r, i: (jnp.maximum(i * gq - 1, 0), r)
    nxt = lambda r, i: (jnp.minimum((i + 1) * gq, nb - 1), r)
    wide = pl.BlockSpec((rows, GROUP_WIDTH), cur)
    wide_prev = pl.BlockSpec((ATTN_BLOCK, GROUP_WIDTH), prev)
    wide_next = pl.BlockSpec((ATTN_BLOCK, GROUP_WIDTH), nxt)
    tab = pl.BlockSpec((rows, HEAD_DIM), cur)
    tab_prev = pl.BlockSpec((ATTN_BLOCK, HEAD_DIM), prev)
    tab_next = pl.BlockSpec((ATTN_BLOCK, HEAD_DIM), nxt)
    return pl.pallas_call(
        body, name=f"attn_bwd_d{dil}", grid=(dil, ni),
        in_specs=[wide, wide_next, wide, wide_prev, wide, wide_prev, wide, wide_next, wide, wide_next, wide, wide_next,
                  tab, tab, tab, tab_prev, tab_prev, tab_prev, tab_next, tab_next, tab_next],
        out_specs=[wide, wide, wide],
        out_shape=[jax.ShapeDtypeStruct(qv.shape, BF16)] * 3,
        compiler_params=_params(2),
    )(qv, qv, kv, kv, vv, vv, dov, dov, ov, ov, lv, lv, *tabs_v, *tabs_v, *tabs_v)


def _to_view(a, dil):
    s, w = a.shape
    return a.reshape(s // dil, dil * w)


def _from_view(a, dil):
    length, dw = a.shape
    return a.reshape(length * dil, dw // dil)


def _discretise(a_re, a_im, log_dt, bt_re, bt_im):
    dt = jnp.exp(log_dt)
    mag = jnp.exp(a_re * dt)
    bar_re = mag * jnp.cos(a_im * dt)
    bar_im = mag * jnp.sin(a_im * dt)
    nr = bar_re - 1.0
    ni = bar_im
    den = a_re * a_re + a_im * a_im
    z_re = (nr * a_re + ni * a_im) / den
    z_im = (ni * a_re - nr * a_im) / den
    bb_re = z_re[:, None, :] * bt_re - z_im[:, None, :] * bt_im
    bb_im = z_re[:, None, :] * bt_im + z_im[:, None, :] * bt_re
    return bar_re, bar_im, bb_re, bb_im


def _ssm_prep(a_re, a_im, log_dt, bt_re, bt_im):
    def body(ar, ai, ld, br, bi, o_lr, o_li, o_br, o_bi):
        lr, li, bbr, bbi = _discretise(ar[...], ai[...], ld[...], br[...], bi[...])
        o_lr[...] = lr
        o_li[...] = li
        o_br[...] = bbr
        o_bi[...] = bbi

    sm = jax.ShapeDtypeStruct((SSM_GROUPS, SSM_STATE), F32)
    bg = jax.ShapeDtypeStruct((SSM_GROUPS, SSM_GROUP, SSM_STATE), F32)
    return pl.pallas_call(body, name="ssm_prep", out_shape=[sm, sm, bg, bg])(a_re, a_im, log_dt, bt_re, bt_im)


def _ssm_param_bwd(a_re, a_im, log_dt, bt_re, bt_im, d_lr, d_li, d_bbr, d_bbi):
    def body(ar, ai, ld, br, bi, g_lr, g_li, g_br, g_bi, o_ar, o_ai, o_ld, o_br, o_bi):
        _, vjp = jax.vjp(_discretise, ar[...], ai[...], ld[...], br[...], bi[...])
        d_ar, d_ai, d_ld, d_br, d_bi = vjp((g_lr[...], g_li[...], g_br[...], g_bi[...]))
        o_ar[...] = d_ar
        o_ai[...] = d_ai
        o_ld[...] = d_ld
        o_br[...] = d_br
        o_bi[...] = d_bi

    sm = jax.ShapeDtypeStruct((SSM_GROUPS, SSM_STATE), F32)
    col = jax.ShapeDtypeStruct((SSM_GROUPS, 1), F32)
    bg = jax.ShapeDtypeStruct((SSM_GROUPS, SSM_GROUP, SSM_STATE), F32)
    return pl.pallas_call(body, name="ssm_param_bwd", out_shape=[sm, sm, col, bg, bg])(
        a_re, a_im, log_dt, bt_re, bt_im, d_lr, d_li, d_bbr, d_bbi)


def _block_diag(t, rows_per, cols_per):
    t4 = t.reshape(SSM_SUPER, 8, rows_per, cols_per)
    eye = jnp.eye(8, dtype=t.dtype)
    return jnp.einsum("bgrc,gh->bgrhc", t4, eye).reshape(SSM_SUPER, 8 * rows_per, 8 * cols_per)


def _block_diag_t(dense, rows_per, cols_per):
    t = dense.reshape(SSM_SUPER, 8, rows_per, 8, cols_per)
    eye = jnp.eye(8, dtype=dense.dtype)
    return jnp.einsum("bgrhc,gh->bgrc", t, eye).reshape(SSM_GROUPS, rows_per, cols_per)


def _gelu(v):
    c = math.sqrt(2.0 / math.pi)
    return 0.5 * v * (1.0 + jnp.tanh(c * (v + 0.044715 * v * v * v)))


def _gelu_grad(v):
    c = math.sqrt(2.0 / math.pi)
    t = jnp.tanh(c * (v + 0.044715 * v * v * v))
    return 0.5 * (1.0 + t) + 0.5 * v * (1.0 - t * t) * c * (1.0 + 3.0 * 0.044715 * v * v)


SUB = 8


def _scan_rows(g_re_ref, g_im_ref, lam_re, lam_im, carry, n_rows, reverse, conj):
    sign = -1.0 if conj else 1.0
    row_id = lax.broadcasted_iota(jnp.int32, (SUB, N_STATE), 0)
    lr = jnp.broadcast_to(lam_re, (SUB, N_STATE))
    li = jnp.broadcast_to(lam_im, (SUB, N_STATE)) * sign

    def tile_step(tt, state):
        sr, si = state
        t8 = (n_rows // SUB - 1 - tt) if reverse else tt
        start = pl.multiple_of(t8 * SUB, SUB)
        g_r = g_re_ref[pl.ds(start, SUB), :]
        g_i = g_im_ref[pl.ds(start, SUB), :]
        out_r, out_i = g_r, g_i
        order = range(SUB - 1, -1, -1) if reverse else range(SUB)
        for j in order:
            gr_j = jnp.broadcast_to(g_r[j:j + 1, :], (SUB, N_STATE))
            gi_j = jnp.broadcast_to(g_i[j:j + 1, :], (SUB, N_STATE))
            nr = lr * sr - li * si + gr_j
            ni = lr * si + li * sr + gi_j
            sr, si = nr, ni
            out_r = jnp.where(row_id == j, sr, out_r)
            out_i = jnp.where(row_id == j, si, out_i)
        g_re_ref[pl.ds(start, SUB), :] = out_r
        g_im_ref[pl.ds(start, SUB), :] = out_i
        return sr, si

    return lax.fori_loop(0, n_rows // SUB, tile_step, carry)


def _ssm_fwd(z, b_re, b_im, c_re, c_im, lam_re, lam_im, d_skip, chunk):
    s = z.shape[0]

    def body(u_ref, bre, bim, cre, cim, lre, lim, dsk, hre_ref, him_ref, ys_ref, yg_ref, car_re, car_im):
        i = pl.program_id(0)

        @pl.when(i == 0)
        def _():
            car_re[...] = jnp.zeros_like(car_re)
            car_im[...] = jnp.zeros_like(car_im)

        u = u_ref[...]
        for b in range(SSM_SUPER):
            ub = u[:, b * 128:(b + 1) * 128]
            st = slice(b * 512, (b + 1) * 512)
            hre_ref[:, st] = _dot(ub, bre[b], "nn")
            him_ref[:, st] = _dot(ub, bim[b], "nn")
        sr, si = _scan_rows(hre_ref, him_ref, lre[...], lim[...], (car_re[...], car_im[...]), chunk, False, False)
        car_re[...] = sr
        car_im[...] = si
        uf = u.astype(F32)
        for b in range(SSM_SUPER):
            st = slice(b * 512, (b + 1) * 512)
            ch = slice(b * 128, (b + 1) * 128)
            y = _dot(hre_ref[:, st].astype(BF16), cre[b], "nn") - _dot(him_ref[:, st].astype(BF16), cim[b], "nn")
            y = y + dsk[:, ch] * uf[:, ch]
            ys_ref[:, ch] = y
            yg_ref[:, ch] = _gelu(y).astype(BF16)

    full3 = lambda i: (0, 0, 0)
    full2 = lambda i: (0, 0)
    row = lambda i: (i, 0)
    u_col = COL_U // SSM_WIDTH
    return pl.pallas_call(
        body, name="ssm_fwd", grid=(s // chunk,),
        in_specs=[pl.BlockSpec((chunk, SSM_WIDTH), lambda i: (i, u_col)),
                  pl.BlockSpec((SSM_SUPER, 128, 512), full3), pl.BlockSpec((SSM_SUPER, 128, 512), full3),
                  pl.BlockSpec((SSM_SUPER, 512, 128), full3), pl.BlockSpec((SSM_SUPER, 512, 128), full3),
                  pl.BlockSpec((1, N_STATE), full2), pl.BlockSpec((1, N_STATE), full2), pl.BlockSpec((1, SSM_WIDTH), full2)],
        out_specs=[pl.BlockSpec((chunk, N_STATE), row), pl.BlockSpec((chunk, N_STATE), row),
                   pl.BlockSpec((chunk, SSM_WIDTH), row), pl.BlockSpec((chunk, SSM_WIDTH), row)],
        out_shape=[jax.ShapeDtypeStruct((s, N_STATE), F32), jax.ShapeDtypeStruct((s, N_STATE), F32),
                   jax.ShapeDtypeStruct((s, SSM_WIDTH), F32), jax.ShapeDtypeStruct((s, SSM_WIDTH), BF16)],
        scratch_shapes=[pltpu.VMEM((SUB, N_STATE), F32), pltpu.VMEM((SUB, N_STATE), F32)],
        compiler_params=_params(1),
    )(z, b_re, b_im, c_re, c_im, lam_re, lam_im, d_skip)


def _ssm_bwd(dys, z, h_re, h_im, b_re, b_im, c_re, c_im, lam_re, lam_im, d_skip, chunk):
    s = z.shape[0]
    n_chunks = s // chunk

    def body(dy_ref, u_ref, hre_ref, him_ref, hpr_ref, hpi_ref, bre, bim, cre, cim, lre, lim, dsk,
             du_ref, dlr_ref, dli_ref, dbr_ref, dbi_ref, dcr_ref, dci_ref, dd_ref, are, aim, car_re, car_im):
        i = pl.program_id(0)
        n = n_chunks - 1 - i

        @pl.when(i == 0)
        def _():
            car_re[...] = jnp.zeros_like(car_re)
            car_im[...] = jnp.zeros_like(car_im)
            for r in (dlr_ref, dli_ref, dbr_ref, dbi_ref, dcr_ref, dci_ref, dd_ref):
                r[...] = jnp.zeros_like(r)

        dy = dy_ref[...]
        dyb = dy.astype(BF16)
        u = u_ref[...]
        for b in range(SSM_SUPER):
            ch = slice(b * 128, (b + 1) * 128)
            st = slice(b * 512, (b + 1) * 512)
            are[:, st] = _dot(dyb[:, ch], cre[b], "nt")
            aim[:, st] = -_dot(dyb[:, ch], cim[b], "nt")
        sr, si = _scan_rows(are, aim, lre[...], lim[...], (car_re[...], car_im[...]), chunk, True, True)
        car_re[...] = sr
        car_im[...] = si
        row_id = lax.broadcasted_iota(jnp.int32, (chunk, N_STATE), 0)
        top_scale = jnp.where(n > 0, 1.0, 0.0)
        h_r = hre_ref[...]
        h_i = him_ref[...]
        hp_r = jnp.where(row_id == 0, hpr_ref[SUB - 1:SUB, :] * top_scale, pltpu.roll(h_r, 1, 0))
        hp_i = jnp.where(row_id == 0, hpi_ref[SUB - 1:SUB, :] * top_scale, pltpu.roll(h_i, 1, 0))
        a_r = are[...]
        a_i = aim[...]
        dlr_ref[...] += jnp.sum(a_r * hp_r + a_i * hp_i, axis=0, keepdims=True)
        dli_ref[...] += jnp.sum(a_i * hp_r - a_r * hp_i, axis=0, keepdims=True)
        dd_ref[...] += jnp.sum(dy * u.astype(F32), axis=0, keepdims=True)
        a_rb = a_r.astype(BF16)
        a_ib = a_i.astype(BF16)
        h_rb = h_r.astype(BF16)
        h_ib = h_i.astype(BF16)
        for b in range(SSM_SUPER):
            ch = slice(b * 128, (b + 1) * 128)
            st = slice(b * 512, (b + 1) * 512)
            dbr_ref[b] += _dot(u[:, ch], a_rb[:, st], "tn")
            dbi_ref[b] += _dot(u[:, ch], a_ib[:, st], "tn")
            dcr_ref[b] += _dot(h_rb[:, st], dyb[:, ch], "tn")
            dci_ref[b] += -_dot(h_ib[:, st], dyb[:, ch], "tn")
            du = _dot(a_rb[:, st], bre[b], "nt") + _dot(a_ib[:, st], bim[b], "nt") + dsk[:, ch] * dy[:, ch]
            du_ref[:, ch] = du.astype(du_ref.dtype)

    full3 = lambda i: (0, 0, 0)
    full2 = lambda i: (0, 0)
    rev = lambda i: (n_chunks - 1 - i, 0)
    above = lambda i: (jnp.maximum((n_chunks - 1 - i) * (chunk // SUB) - 1, 0), 0)
    u_col = COL_U // SSM_WIDTH
    b_spec = pl.BlockSpec((SSM_SUPER, 128, 512), full3)
    c_spec = pl.BlockSpec((SSM_SUPER, 512, 128), full3)
    vec = pl.BlockSpec((1, N_STATE), full2)
    return pl.pallas_call(
        body, name="ssm_bwd", grid=(n_chunks,),
        in_specs=[pl.BlockSpec((chunk, SSM_WIDTH), rev),
                  pl.BlockSpec((chunk, SSM_WIDTH), lambda i: (n_chunks - 1 - i, u_col)),
                  pl.BlockSpec((chunk, N_STATE), rev), pl.BlockSpec((chunk, N_STATE), rev),
                  pl.BlockSpec((SUB, N_STATE), above), pl.BlockSpec((SUB, N_STATE), above),
                  b_spec, b_spec, c_spec, c_spec, vec, vec, pl.BlockSpec((1, SSM_WIDTH), full2)],
        out_specs=[pl.BlockSpec((chunk, SSM_WIDTH), rev), vec, vec, b_spec, b_spec, c_spec, c_spec,
                   pl.BlockSpec((1, SSM_WIDTH), full2)],
        out_shape=[jax.ShapeDtypeStruct((s, SSM_WIDTH), BF16),
                   jax.ShapeDtypeStruct((1, N_STATE), F32), jax.ShapeDtypeStruct((1, N_STATE), F32),
                   jax.ShapeDtypeStruct((SSM_SUPER, 128, 512), F32), jax.ShapeDtypeStruct((SSM_SUPER, 128, 512), F32),
                   jax.ShapeDtypeStruct((SSM_SUPER, 512, 128), F32), jax.ShapeDtypeStruct((SSM_SUPER, 512, 128), F32),
                   jax.ShapeDtypeStruct((1, SSM_WIDTH), F32)],
        scratch_shapes=[pltpu.VMEM((chunk, N_STATE), F32), pltpu.VMEM((chunk, N_STATE), F32),
                        pltpu.VMEM((SUB, N_STATE), F32), pltpu.VMEM((SUB, N_STATE), F32)],
        compiler_params=_params(1),
    )(dys, z, h_re, h_im, h_re, h_im, b_re, b_im, c_re, c_im, lam_re, lam_im, d_skip)


def _local_step(x, p, pos, tgt, sm, wts):
    s = x.shape[0]
    tm = min(512, s)
    ts = min(1024, s)
    chunk = min(256, s)
    ni = s // tm
    nk = s // ts
    w_in, w_ap, w_ga, w_gb, w_out, w_fg, w_fu, w_fd, w_pg, w_pp = (
        wts[k] for k in ("w_in", "w_attn_proj", "w_glu_a", "w_glu_b", "w_out", "w_ffn_gate", "w_ffn_up", "w_ffn_down",
                         "w_ple_gate", "w_ple_proj"))
    w_out2 = w_out.reshape(D_MODEL, D_MODEL)
    w_pg2 = w_pg.reshape(D_MODEL, D_MODEL)
    g_mix, g_ffn, g_final = sm["g_mix"], sm["g_ffn"], sm["g_final"]
    rowblk, rowmap = _rows(tm, D_MODEL)
    vec1k = ((1, D_MODEL), lambda *_: (0, 0))

    (n1,) = _ew("rms_mix", (ni,), [(x, rowblk, rowmap), (g_mix, *vec1k)], [((s, D_MODEL), BF16, rowblk, rowmap)],
                lambda pids, h, g: ((_rms_fwd_tile(h, g),), ()))

    half_in = IN_WIDTH // 8
    tmb = min(1024, s)
    nib = s // tmb
    (z,) = _mm("in_proj", (nib, 8, 1),
               [(n1, (tmb, D_MODEL), lambda i, j, k: (i, 0), w_in, (None, D_MODEL, half_in), lambda i, j, k: (j // 2, 0, j % 2))],
               "nn", [((s, IN_WIDTH), BF16, (tmb, half_in), lambda i, j, k: (i, j))], j_outer=True)

    inv = ROPE_THETA ** (-jnp.arange(ROPE_HALF, dtype=F32) * 2.0 / ROPE_DIM)
    inv_row = jnp.concatenate([inv, inv, jnp.zeros((HEAD_DIM - ROPE_DIM,), F32)]).reshape(1, HEAD_DIM)
    tabs = _rope_tables(pos.astype(F32).reshape(s, 1), inv_row, tm)

    views, outs_g, lses_g = [], [], []
    for gi, dil in enumerate(GROUP_DILATIONS):
        q_g = _to_view(z[:, gi * GROUP_WIDTH:(gi + 1) * GROUP_WIDTH], dil)
        k_g = _to_view(z[:, 1536 + gi * GROUP_WIDTH:1536 + (gi + 1) * GROUP_WIDTH], dil)
        v_g = _to_view(z[:, 3072 + gi * GROUP_WIDTH:3072 + (gi + 1) * GROUP_WIDTH], dil)
        tabs_v = tuple(_to_view(t, dil) for t in tabs)
        views.append((q_g, k_g, v_g, tabs_v))
        o_g, l_g = _attn_fwd(q_g, k_g, v_g, tabs_v, dil)
        outs_g.append(_from_view(o_g, dil))
        lses_g.append(_from_view(l_g, dil))

    def merge_fn(pids, o0, o1, o2, l0, l1, l2):
        m = jnp.maximum(jnp.maximum(l0, l1), l2)
        e0, e1, e2 = jnp.exp(l0 - m), jnp.exp(l1 - m), jnp.exp(l2 - m)
        den = e0 + e1 + e2
        return ((e0 * o0 + e1 * o1 + e2 * o2) / den, m + jnp.log(den)), ()

    gblk, gmap = _rows(tm, GROUP_WIDTH)
    attn, lse = _ew("attn_merge", (ni,), [(a, gblk, gmap) for a in outs_g + lses_g],
                    [((s, GROUP_WIDTH), BF16, gblk, gmap), ((s, GROUP_WIDTH), F32, gblk, gmap)], merge_fn)

    def chip_cols(parts):
        return (jnp.concatenate(parts, axis=1),), ()

    def proj_cols(name, a, width, w):
        blk = (None, width, 256)
        pairs = [(a, (tmb, width), lambda i, j, k: (i, 0), w, blk, lambda i, j, k: (0, 0, 0))]
        pairs += [(None, None, None, w, blk, (lambda i, j, k, q=q: (q, 0, 0))) for q in range(1, N_CHIPS)]
        return _mm(name, (nib, 1, 1), pairs, "nn", [((s, D_MODEL), BF16, (tmb, D_MODEL), lambda i, j, k: (i, 0))],
                   epilogue=chip_cols, sum_pairs=False)[0]

    def proj512(name, a, w):
        return proj_cols(name, a, GROUP_WIDTH, w)

    attn_d = proj512("attn_proj", attn, w_ap)

    bt_re = jnp.transpose(sm["b_re"], (0, 2, 1))
    bt_im = jnp.transpose(sm["b_im"], (0, 2, 1))
    log_dt_col = sm["log_dt"].reshape(SSM_GROUPS, 1)
    lam_re, lam_im, bbt_re, bbt_im = _ssm_prep(sm["a_re"], sm["a_im"], log_dt_col, bt_re, bt_im)
    b_re_m = _block_diag(bbt_re, SSM_GROUP, SSM_STATE).astype(BF16)
    b_im_m = _block_diag(bbt_im, SSM_GROUP, SSM_STATE).astype(BF16)
    c_re_m = _block_diag(jnp.transpose(sm["c_re"], (0, 2, 1)), SSM_STATE, SSM_GROUP).astype(BF16)
    c_im_m = _block_diag(jnp.transpose(sm["c_im"], (0, 2, 1)), SSM_STATE, SSM_GROUP).astype(BF16)
    lam_re_row = lam_re.reshape(1, N_STATE)
    lam_im_row = lam_im.reshape(1, N_STATE)
    d_skip_row = sm["d_skip"].reshape(1, SSM_WIDTH)
    h_re, h_im, ys, yg = _ssm_fwd(z, b_re_m, b_im_m, c_re_m, c_im_m, lam_re_row, lam_im_row, d_skip_row, chunk)

    pa = proj512("glu_a", yg, w_ga)
    pb = proj512("glu_b", yg, w_gb)

    ga_blk = ((tm, D_MODEL), lambda i: (i, COL_GA // D_MODEL))
    gs_blk = ((tm, D_MODEL), lambda i: (i, COL_GS // D_MODEL))

    def mix_fn(pids, ga, gs, ad, a, b):
        ga, gs, ad, a, b = (t.astype(F32) for t in (ga, gs, ad, a, b))
        return (_sig(ga) * ad + _sig(gs) * (a * _sig(b)),), ()

    (mix,) = _ew("gate_mix", (ni,), [(z, *ga_blk), (z, *gs_blk), (attn_d, rowblk, rowmap), (pa, rowblk, rowmap),
                                     (pb, rowblk, rowmap)], [((s, D_MODEL), BF16, rowblk, rowmap)], mix_fn)

    def out_epi(acc, xr, g):
        h1 = acc + xr
        return (h1, _rms_fwd_tile(h1, g)), ()

    m3 = lambda i, j, k: (i, 0)
    w3 = lambda i, j, k: (0, 0)
    h1, n2 = _mm("out_proj", (nib, 1, 1), [(mix, (tmb, D_MODEL), m3, w_out2, (D_MODEL, D_MODEL), w3)], "nn",
                 [((s, D_MODEL), F32, (tmb, D_MODEL), m3), ((s, D_MODEL), BF16, (tmb, D_MODEL), m3)],
                 epilogue=out_epi, extras=[(x, (tmb, D_MODEL), m3), (g_ffn, (1, D_MODEL), w3)])

    ffq = (None, tm, D_FF_Q)
    ffq_map = lambda i, j, k: (j, i, 0)

    def ffn_in_epi(parts):
        gt, u_ = parts
        return (gt, u_, gt * _sig(gt) * u_), ()

    w_ffq = (None, D_MODEL, D_FF_Q)
    w_ffq_j = lambda i, j, k: (j, 0, 0)
    gate, up, act = _mm("ffn_gate_up", (ni, N_CHIPS, 1),
                        [(n2, (tm, D_MODEL), m3, w_fg, w_ffq, w_ffq_j), (None, None, None, w_fu, w_ffq, w_ffq_j)], "nn",
                        [((N_CHIPS, s, D_FF_Q), BF16, ffq, ffq_map)] * 3, epilogue=ffn_in_epi, j_outer=True,
                        sum_pairs=False)

    (h2,) = _mm("ffn_down", (nib, 1, 1),
                [(act, (None, tmb, D_FF_Q), (lambda i, j, k, q=q: (q, i, 0)), w_fd, (None, D_FF_Q, D_MODEL),
                  (lambda i, j, k, q=q: (q, 0, 0))) for q in range(N_CHIPS)], "nn",
                [((s, D_MODEL), F32, (tmb, D_MODEL), m3)], epilogue=lambda acc, hr: ((acc + hr,), ()),
                extras=[(h1, (tmb, D_MODEL), m3)])

    pp = proj_cols("ple_proj", p, PLE_DIM, w_pp)

    def ple_epi(acc, hr, ppr):
        return (acc, hr + _sig(acc) * ppr.astype(F32)), ()

    gl, h3 = _mm("ple_gate", (nib, 1, 1), [(h2, (tmb, D_MODEL), m3, w_pg2, (D_MODEL, D_MODEL), w3)], "nn",
                 [((s, D_MODEL), BF16, (tmb, D_MODEL), m3), ((s, D_MODEL), F32, (tmb, D_MODEL), m3)],
                 epilogue=ple_epi, extras=[(h2, (tmb, D_MODEL), m3), (pp, (tmb, D_MODEL), m3)])

    def head_fn(pids, h, t, g):
        r = lax.rsqrt(jnp.mean(h * h, axis=-1, keepdims=True) + EPS)
        hhat = h * r
        diff = hhat * g - t
        loss = 0.5 * jnp.sum(jnp.mean(diff * diff, axis=-1, keepdims=True))
        dy = diff * (1.0 / D_MODEL)
        gy = dy * g
        dh = r * (gy - hhat * jnp.mean(gy * hhat, axis=-1, keepdims=True))
        return (dh,), (jnp.full((SUB, 128), loss, F32), jnp.sum(dy * hhat, axis=0, keepdims=True))

    dh3, loss_acc, dg_final = _ew("loss_head", (ni,), [(h3, rowblk, rowmap), (tgt, rowblk, rowmap), (g_final, *vec1k)],
                                  [((s, D_MODEL), F32, rowblk, rowmap)], head_fn,
                                  acc_outs=[((SUB, 128), F32), ((1, D_MODEL), F32)])

    def ple_bwd_fn(pids, dh, g_, ppr):
        sg = _sig(g_.astype(F32))
        return (dh * ppr.astype(F32) * sg * (1.0 - sg), dh * sg), ()

    dgl, dpp = _ew("ple_bwd", (ni,), [(dh3, rowblk, rowmap), (gl, rowblk, rowmap), (pp, rowblk, rowmap)],
                   [((s, D_MODEL), BF16, rowblk, rowmap)] * 2, ple_bwd_fn)

    def wgrad(name, a, a_block, a_imap, b, b_block, b_imap, out_shape, out_block, out_imap, nj, acc_shape):
        return _mm(name, (1, nj, nk), [(a, a_block, a_imap, b, b_block, b_imap)], "tn",
                   [(out_shape, F32, out_block, out_imap)], acc_shape=acc_shape)[0]

    tk0 = lambda i, j, k: (k, 0)
    tkj = lambda i, j, k: (k, j)
    def wgrad_cols(name, a, width, dy_):
        def split(acc):
            return (jnp.stack([acc[:, q * 256:(q + 1) * 256] for q in range(N_CHIPS)], axis=0),), ()

        return _mm(name, (1, 1, nk), [(a, (ts, width), tk0, dy_, (ts, D_MODEL), tk0)], "tn",
                   [((N_CHIPS, width, 256), F32, (N_CHIPS, width, 256), lambda i, j, k: (0, 0, 0))], epilogue=split,
                   acc_shape=(width, D_MODEL))[0]

    d_w_pp = wgrad_cols("d_ple_proj", p, PLE_DIM, dpp)
    d_w_pg = wgrad("d_ple_gate", h2, (ts, D_MODEL), tk0, dgl, (ts, D_MODEL), tk0, (D_MODEL, D_MODEL),
                   (D_MODEL, D_MODEL), w3, 1, (D_MODEL, D_MODEL))

    (dh2,) = _mm("ple_gate_bwd", (nib, 1, 1), [(dgl, (tmb, D_MODEL), m3, w_pg2, (D_MODEL, D_MODEL), w3)], "nt",
                 [((s, D_MODEL), F32, (tmb, D_MODEL), m3)], epilogue=lambda acc, d_: ((acc + d_,), ()),
                 extras=[(dh3, (tmb, D_MODEL), m3)])

    def ffn_bwd_epi(acc, gt, u_):
        gt, u_ = gt.astype(F32), u_.astype(F32)
        sg = _sig(gt)
        return (acc * u_ * (sg * (1.0 + gt * (1.0 - sg))), acc * gt * sg), ()

    ffq_big = (None, tmb, D_FF_Q)
    dgate, dup = _mm("ffn_down_bwd", (nib, N_CHIPS, 1),
                     [(dh2, (tmb, D_MODEL), m3, w_fd, (None, D_FF_Q, D_MODEL), lambda i, j, k: (j, 0, 0))], "nt",
                     [((N_CHIPS, s, D_FF_Q), BF16, ffq_big, ffq_map)] * 2, epilogue=ffn_bwd_epi,
                     extras=[(gate, ffq_big, ffq_map), (up, ffq_big, ffq_map)])

    ffq_t = (None, ts, D_FF_Q)
    ffq_tmap = lambda i, j, k: (j, k, 0)
    blk_j = lambda i, j, k: (j, 0, 0)
    d_w_fd = wgrad("d_ffn_down", act, ffq_t, ffq_tmap, dh2, (ts, D_MODEL), tk0, (N_CHIPS, D_FF_Q, D_MODEL),
                   (None, D_FF_Q, D_MODEL), blk_j, N_CHIPS, (D_FF_Q, D_MODEL))
    d_w_fg = wgrad("d_ffn_gate", n2, (ts, D_MODEL), tk0, dgate, ffq_t, ffq_tmap, (N_CHIPS, D_MODEL, D_FF_Q),
                   (None, D_MODEL, D_FF_Q), blk_j, N_CHIPS, (D_MODEL, D_FF_Q))
    d_w_fu = wgrad("d_ffn_up", n2, (ts, D_MODEL), tk0, dup, ffq_t, ffq_tmap, (N_CHIPS, D_MODEL, D_FF_Q),
                   (None, D_MODEL, D_FF_Q), blk_j, N_CHIPS, (D_MODEL, D_FF_Q))

    def norm_bwd_epi(acc, h, d_res, g):
        dh, dg = _rms_bwd_tile(acc, h, g)
        return (d_res + dh,), (dg,)

    ffq_k = lambda i, j, k: (k, i, 0)
    blk_k = lambda i, j, k: (k, 0, 0)
    ffq_b = (None, tmb, D_FF_Q)
    dh1, dg_ffn = _mm("ffn_in_bwd", (nib, 1, N_CHIPS),
                      [(dgate, ffq_b, ffq_k, w_fg, (None, D_MODEL, D_FF_Q), blk_k),
                       (dup, ffq_b, ffq_k, w_fu, (None, D_MODEL, D_FF_Q), blk_k)], "nt",
                      [((s, D_MODEL), F32, (tmb, D_MODEL), m3)], epilogue=norm_bwd_epi,
                      extras=[(h1, (tmb, D_MODEL), m3), (dh2, (tmb, D_MODEL), m3), (g_ffn, (1, D_MODEL), w3)],
                      acc_outs=[((1, D_MODEL), F32)], acc_shape=(tmb, D_MODEL))

    (dmix,) = _mm("out_proj_bwd", (nib, 1, 1), [(dh1, (tmb, D_MODEL), m3, w_out2, (D_MODEL, D_MODEL), w3)], "nt",
                  [((s, D_MODEL), BF16, (tmb, D_MODEL), m3)])
    d_w_out = wgrad("d_out_proj", mix, (ts, D_MODEL), tk0, dh1, (ts, D_MODEL), tk0, (D_MODEL, D_MODEL),
                    (D_MODEL, D_MODEL), w3, 1, (D_MODEL, D_MODEL))

    def mix_bwd_fn(pids, dm, ga, gs, ad, a, b):
        dm, ga, gs, ad, a, b = (t.astype(F32) for t in (dm, ga, gs, ad, a, b))
        s_a, s_s, s_b = _sig(ga), _sig(gs), _sig(b)
        d_ssm = dm * s_s
        return (dm * ad * s_a * (1.0 - s_a), dm * (a * s_b) * s_s * (1.0 - s_s), dm * s_a, d_ssm * s_b,
                d_ssm * a * s_b * (1.0 - s_b)), ()

    dga, dgs, dattn_d, dpa, dpb = _ew(
        "gate_mix_bwd", (ni,),
        [(dmix, rowblk, rowmap), (z, *ga_blk), (z, *gs_blk), (attn_d, rowblk, rowmap), (pa, rowblk, rowmap),
         (pb, rowblk, rowmap)], [((s, D_MODEL), BF16, rowblk, rowmap)] * 5, mix_bwd_fn)

    d_w_ap = wgrad_cols("d_attn_proj", attn, GROUP_WIDTH, dattn_d)
    d_w_ga = wgrad_cols("d_glu_a", yg, GROUP_WIDTH, dpa)
    d_w_gb = wgrad_cols("d_glu_b", yg, GROUP_WIDTH, dpb)

    ik = lambda i, j, k: (i, k)

    def cols_bwd(dy_, w):
        return [(dy_, (tmb, 256), (lambda i, j, k, q=q: (i, q)), w, (None, GROUP_WIDTH, 256),
                 (lambda i, j, k, q=q: (q, 0, 0))) for q in range(N_CHIPS)]

    (dattn,) = _mm("attn_proj_bwd", (nib, 1, 1), cols_bwd(dattn_d, w_ap), "nt",
                   [((s, GROUP_WIDTH), BF16, (tmb, GROUP_WIDTH), m3)])

    (dys,) = _mm("glu_bwd", (nib, 1, 1), cols_bwd(dpa, w_ga) + cols_bwd(dpb, w_gb), "nt",
                 [((s, GROUP_WIDTH), F32, (tmb, GROUP_WIDTH), m3)],
                 epilogue=lambda acc, y_: ((acc * _gelu_grad(y_),), ()),
                 extras=[(ys, (tmb, GROUP_WIDTH), m3)])

    du, d_lr, d_li, d_bre, d_bim, d_cre, d_cim, d_dskip = _ssm_bwd(
        dys, z, h_re, h_im, b_re_m, b_im_m, c_re_m, c_im_m, lam_re_row, lam_im_row, d_skip_row, chunk)

    dq_parts, dk_parts, dv_parts = [], [], []
    for gi, dil in enumerate(GROUP_DILATIONS):
        q_g, k_g, v_g, tabs_v = views[gi]
        dq_g, dk_g, dv_g = _attn_bwd(q_g, k_g, v_g, _to_view(dattn, dil), _to_view(attn, dil), _to_view(lse, dil), tabs_v, dil)
        dq_parts.append(_from_view(dq_g, dil))
        dk_parts.append(_from_view(dk_g, dil))
        dv_parts.append(_from_view(dv_g, dil))
    dz = jnp.concatenate(dq_parts + dk_parts + dv_parts + [du, dga, dgs], axis=1)

    kb = lambda i, j, k: (k // 2, 0, k % 2)
    grad_x, dg_mix = _mm("in_proj_bwd", (nib, 1, 8), [(dz, (tmb, half_in), ik, w_in, (None, D_MODEL, half_in), kb)], "nt",
                         [((s, D_MODEL), F32, (tmb, D_MODEL), m3)], epilogue=norm_bwd_epi,
                         extras=[(x, (tmb, D_MODEL), m3), (dh1, (tmb, D_MODEL), m3), (g_mix, (1, D_MODEL), w3)],
                         acc_outs=[((1, D_MODEL), F32)], acc_shape=(tmb, D_MODEL))
    d_w_in = wgrad("d_in_proj", n1, (ts, D_MODEL), tk0, dz, (ts, half_in), tkj, (N_CHIPS, D_MODEL, IN_WIDTH // N_CHIPS),
                   (None, D_MODEL, half_in), lambda i, j, k: (j // 2, 0, j % 2), 8, (D_MODEL, half_in))

    d_bbt_re = _block_diag_t(d_bre, SSM_GROUP, SSM_STATE)
    d_bbt_im = _block_diag_t(d_bim, SSM_GROUP, SSM_STATE)
    d_a_re, d_a_im, d_log_dt, d_bt_re, d_bt_im = _ssm_param_bwd(
        sm["a_re"], sm["a_im"], log_dt_col, bt_re, bt_im,
        d_lr.reshape(SSM_GROUPS, SSM_STATE), d_li.reshape(SSM_GROUPS, SSM_STATE), d_bbt_re, d_bbt_im)
    small = {
        "g_mix": dg_mix, "a_re": d_a_re, "a_im": d_a_im, "log_dt": d_log_dt,
        "b_re": jnp.transpose(d_bt_re, (0, 2, 1)), "b_im": jnp.transpose(d_bt_im, (0, 2, 1)),
        "c_re": jnp.transpose(_block_diag_t(d_cre, SSM_STATE, SSM_GROUP), (0, 2, 1)),
        "c_im": jnp.transpose(_block_diag_t(d_cim, SSM_STATE, SSM_GROUP), (0, 2, 1)),
        "d_skip": d_dskip, "g_ffn": dg_ffn, "g_final": dg_final,
    }
    big = {
        "w_in": d_w_in, "w_attn_proj": d_w_ap, "w_glu_a": d_w_ga, "w_glu_b": d_w_gb,
        "w_out": d_w_out.reshape(N_CHIPS, D_MODEL // N_CHIPS, D_MODEL), "w_ffn_gate": d_w_fg, "w_ffn_up": d_w_fu,
        "w_ffn_down": d_w_fd, "w_ple_gate": d_w_pg.reshape(N_CHIPS, D_MODEL // N_CHIPS, D_MODEL), "w_ple_proj": d_w_pp,
    }
    return loss_acc[0, 0], grad_x, big, small


BIG = ("w_in", "w_attn_proj", "w_glu_a", "w_glu_b", "w_out", "w_ffn_gate", "w_ffn_up", "w_ffn_down", "w_ple_gate",
       "w_ple_proj")
SMALL = ("g_mix", "a_re", "a_im", "log_dt", "b_re", "b_im", "c_re", "c_im", "d_skip", "g_ffn", "g_final")
ANY = pl.BlockSpec(memory_space=pl.ANY)


def _place():
    x, y, c = lax.axis_index("x"), lax.axis_index("y"), lax.axis_index("c")
    chips = [(1 - x, y), (x, 1 - y), (1 - x, 1 - y)]
    return x, y, c, chips


def _remote(src, dst, send_sem, recv_sem, to):
    return pltpu.make_async_remote_copy(src_ref=src, dst_ref=dst, send_sem=send_sem, recv_sem=recv_sem, device_id=to,
                                        device_id_type=MESH)


def _comm_call(name, body, ins, out_shapes, n_sems, aliases=None):
    n_w = len(ins)
    return pl.pallas_call(
        body, name=name, in_specs=[ANY] * n_w, out_specs=[ANY] * len(out_shapes), out_shape=out_shapes,
        scratch_shapes=[pltpu.SemaphoreType.DMA((n,)) for n in n_sems], input_output_aliases=aliases or {},
    )(*ins)


def _gather_weights(bufs):
    n_w = len(bufs)

    def body(*refs):
        outs = refs[n_w:2 * n_w]
        ici_send, ici_recv, d2d_send, d2d_recv = refs[2 * n_w:]
        x, y, c, chips = _place()
        me = 2 * x + y
        sib = (x, y, 1 - c)
        sends = []
        for w in range(n_w):
            for j, (cx, cy) in enumerate(chips):
                k = 3 * w + j
                mine = outs[w].at[me, c]
                cp = _remote(mine, mine, ici_send.at[k], ici_recv.at[k], (cx, cy, c))
                cp.start()
                sends.append(cp)
        for w in range(n_w):
            for j, (cx, cy) in enumerate(chips):
                k = 3 * w + j
                src_chip = 2 * cx + cy
                landed = outs[w].at[src_chip, c]
                _remote(landed, landed, ici_send.at[k], ici_recv.at[k], (cx, cy, c)).wait_recv()
                fwd = _remote(landed, landed, d2d_send.at[k], d2d_recv.at[k], sib)
                fwd.start()
                sends.append(fwd)
        for w in range(n_w):
            for j, (cx, cy) in enumerate(chips):
                k = 3 * w + j
                other = outs[w].at[2 * cx + cy, 1 - c]
                _remote(other, other, d2d_send.at[k], d2d_recv.at[k], sib).wait_recv()
        for cp in sends:
            cp.wait_send()

    out_shapes = [jax.ShapeDtypeStruct(b.shape, b.dtype) for b in bufs]
    return _comm_call("gather_weights", body, bufs, out_shapes, [3 * n_w] * 4, aliases={w: w for w in range(n_w)})


def _pair_exchange(grads):
    n_w = len(grads)

    def body(*refs):
        ins, outs = refs[:n_w], refs[n_w:2 * n_w]
        send, recv = refs[2 * n_w:]
        x, y, c, _ = _place()
        sib = (x, y, 1 - c)
        cps = []
        for w in range(n_w):
            for q in range(N_CHIPS):
                k = N_CHIPS * w + q
                cp = _remote(ins[w].at[q, 1 - c], outs[w].at[q], send.at[k], recv.at[k], sib)
                cp.start()
                cps.append(cp)
        for cp in cps:
            cp.wait()

    out_shapes = [jax.ShapeDtypeStruct((N_CHIPS,) + g.shape[2:], g.dtype) for g in grads]
    return _comm_call("grad_pair_exchange", body, grads, out_shapes, [N_CHIPS * n_w] * 2)


def _chip_exchange(parts):
    n_w = len(parts)

    def body(*refs):
        ins, outs = refs[:n_w], refs[n_w:2 * n_w]
        send, recv = refs[2 * n_w:]
        x, y, c, chips = _place()
        me = 2 * x + y
        cps = []
        for w in range(n_w):
            for j, (cx, cy) in enumerate(chips):
                k = 3 * w + j
                cp = _remote(ins[w].at[2 * cx + cy], outs[w].at[me], send.at[k], recv.at[k], (cx, cy, c))
                cp.start()
                cps.append(cp)
        for w in range(n_w):
            for j, (cx, cy) in enumerate(chips):
                k = 3 * w + j
                got = outs[w].at[2 * cx + cy]
                _remote(got, got, send.at[k], recv.at[k], (cx, cy, c)).wait_recv()
        for cp in cps:
            cp.wait_send()

    out_shapes = [jax.ShapeDtypeStruct(t.shape, t.dtype) for t in parts]
    return _comm_call("grad_chip_exchange", body, parts, out_shapes, [3 * n_w, 3 * n_w])


def _pair_gather(halves):
    n_w = len(halves)

    def body(*refs):
        ins, outs = refs[:n_w], refs[n_w:2 * n_w]
        send, recv = refs[2 * n_w:]
        x, y, c, _ = _place()
        sib = (x, y, 1 - c)
        cps = []
        for w in range(n_w):
            cp = _remote(ins[w], outs[w], send.at[w], recv.at[w], sib)
            cp.start()
            cps.append(cp)
        for cp in cps:
            cp.wait()

    out_shapes = [jax.ShapeDtypeStruct(h.shape, h.dtype) for h in halves]
    return _comm_call("grad_pair_gather", body, halves, out_shapes, [n_w] * 2)


def _all_exchange(vec):
    def body(in_ref, out_ref, send, recv):
        x, y, c, _ = _place()
        me = 4 * x + 2 * y + c
        cps = []
        for k in range(1, 8):
            fx, fy, fc = (k >> 2) & 1, (k >> 1) & 1, k & 1
            to = (x ^ fx, y ^ fy, c ^ fc)
            cp = _remote(in_ref, out_ref.at[me], send.at[k - 1], recv.at[k - 1], to)
            cp.start()
            cps.append(cp)
        for k in range(1, 8):
            fx, fy, fc = (k >> 2) & 1, (k >> 1) & 1, k & 1
            src = 4 * (x ^ fx) + 2 * (y ^ fy) + (c ^ fc)
            got = out_ref.at[src]
            _remote(got, got, send.at[k - 1], recv.at[k - 1], (x ^ fx, y ^ fy, c ^ fc)).wait_recv()
        for cp in cps:
            cp.wait_send()

    return _comm_call("small_all_exchange", body, [vec], [jax.ShapeDtypeStruct((8,) + vec.shape, vec.dtype)], [7, 7])[0]


def _row_tile(r):
    for t in (256, 128, 176, 64, 32, 16, 8):
        if r % t == 0:
            return t
    return r


P_C, P_CHIP, P_DEV = 2, 3, 4


def _cast_into_slot(w2, place):
    r, c = w2.shape
    t = _row_tile(r)
    return _ew("cast_shard", (r // t,), [(w2, (t, c), lambda i, pv: (i, 0))],
               [((N_CHIPS, r, c), BF16, (None, t, c), lambda i, pv: (pv[P_CHIP], i, 0))],
               lambda pids, a: ((a,), ()), place=place)[0]


def _pair_sum(mine, theirs, place):
    _, r, c = theirs.shape
    t = _row_tile(r)
    own = ((None, None, t, c), lambda q, i, pv: (q, pv[P_C], i, 0))
    blk = ((None, t, c), lambda q, i, pv: (q, i, 0))
    return _ew("grad_pair_sum", (N_CHIPS, r // t), [(mine, *own), (theirs, *blk)], [((N_CHIPS, r, c), BF16, *blk)],
               lambda pids, a, b: ((a + b,), ()), place=place)[0]


def _chip_sum(own, got, place):
    _, r, c = own.shape
    t = _row_tile(r)
    ins = []
    for q in range(N_CHIPS):
        ins.append((own, (None, t, c), (lambda i, pv, q=q: (q, i, 0))))
        ins.append((got, (None, t, c), (lambda i, pv, q=q: (jnp.where(pv[P_CHIP] == q, (q + 1) % N_CHIPS, q), i, 0))))

    def fn(pids, *tiles):
        me = pids[0][P_CHIP]
        tot = None
        for q in range(N_CHIPS):
            term = jnp.where(me == q, tiles[2 * q], tiles[2 * q + 1]).astype(F32)
            tot = term if tot is None else tot + term
        return (tot,), ()

    return _ew("grad_chip_sum", (r // t,), ins, [((r, c), F32, (t, c), lambda i, pv: (i, 0))], fn, place=place)[0]


def _adamw_tile(w, g, m, v):
    m = ADAM_B1 * m + (1.0 - ADAM_B1) * g
    v = ADAM_B2 * v + (1.0 - ADAM_B2) * (g * g)
    m_hat = m / (1.0 - ADAM_B1 ** ADAM_STEP)
    v_hat = v / (1.0 - ADAM_B2 ** ADAM_STEP)
    delta = -ADAM_LR * (m_hat / (jnp.sqrt(v_hat) + ADAM_EPS) + ADAM_WD * w)
    return delta, m, v


def _adamw(name, g2, w2, m2, v2):
    r, c = w2.shape
    t = _row_tile(r)
    blk, imap = _rows(t, c)

    def fn(pids, g, w, m, v):
        delta, nm, nv = _adamw_tile(w, g, m, v)
        return (g, delta, nm, nv), ()

    return _ew(name, (r // t,), [(a, blk, imap) for a in (g2, w2, m2, v2)], [((r, c), F32, blk, imap)] * 4, fn)


def _adamw_halves(name, mine, theirs, w2, m2, v2, place):
    r, c = w2.shape
    t = _row_tile(r // 2)
    n_t = (r // 2) // t
    half = ((t, c), lambda h, i, pv: (i, 0))
    whole = ((t, c), lambda h, i, pv: (h * n_t + i, 0))

    def fn(pids, ga, gb, w, m, v):
        g = jnp.where(pids[1] == pids[0][P_C], ga, gb)
        delta, nm, nv = _adamw_tile(w, g, m, v)
        return (g, delta, nm, nv), ()

    return _ew(name, (2, n_t), [(mine, *half), (theirs, *half), (w2, *whole), (m2, *whole), (v2, *whole)],
               [((r, c), F32, *whole)] * 4, fn, place=place)


def _device_sum(own, got, place):
    r, c = own.shape
    t = _row_tile(r)
    ins = [(own, (t, c), lambda i, pv: (i, 0))]
    for q in range(8):
        ins.append((got, (None, t, c), (lambda i, pv, q=q: (jnp.where(pv[P_DEV] == q, (q + 1) % 8, q), i, 0))))

    def fn(pids, mine, *parts):
        me = pids[0][P_DEV]
        tot = None
        for q in range(8):
            term = jnp.where(me == q, mine, parts[q])
            tot = term if tot is None else tot + term
        return (tot,), ()

    return _ew("small_device_sum", (r // t,), ins, [((r, c), F32, (t, c), lambda i, pv: (i, 0))], fn, place=place)[0]


def _pack(parts):
    flat = jnp.concatenate([a.reshape(-1) for a in parts])
    pad = (-flat.shape[0]) % (SUB * 128)
    return jnp.pad(flat, (0, pad)).reshape(-1, 128)


def _unpack(mat, shapes):
    flat = mat.reshape(-1)
    out, off = [], 0
    for shp in shapes:
        n = math.prod(shp)
        out.append(flat[off:off + n].reshape(shp))
        off += n
    return out


def kernel(x, p, positions, g_mix, w_in, a_re, a_im, log_dt, b_re, b_im, c_re, c_im, d_skip, w_attn_proj, w_glu_a, w_glu_b, w_out, g_ffn, w_ffn_gate, w_ffn_up, w_ffn_down, w_ple_gate, w_ple_proj, g_final, loss_target, m_g_mix, m_w_in, m_a_re, m_a_im, m_log_dt, m_b_re, m_b_im, m_c_re, m_c_im, m_d_skip, m_w_attn_proj, m_w_glu_a, m_w_glu_b, m_w_out, m_g_ffn, m_w_ffn_gate, m_w_ffn_up, m_w_ffn_down, m_w_ple_gate, m_w_ple_proj, m_g_final, v_g_mix, v_w_in, v_a_re, v_a_im, v_log_dt, v_b_re, v_b_im, v_c_re, v_c_im, v_d_skip, v_w_attn_proj, v_w_glu_a, v_w_glu_b, v_w_out, v_g_ffn, v_w_ffn_gate, v_w_ffn_up, v_w_ffn_down, v_w_ple_gate, v_w_ple_proj, v_g_final):
    given = dict(locals())
    big_w = {n: given[n] for n in BIG}
    w_mats = {n: big_w[n].reshape(big_w[n].shape[1:]) for n in BIG}

    ax, ay, ac = lax.axis_index("x"), lax.axis_index("y"), lax.axis_index("c")
    place = jnp.stack([ax, ay, ac, 2 * ax + ay, 4 * ax + 2 * ay + ac]).astype(jnp.int32)

    bufs = []
    for n in BIG:
        r, c = w_mats[n].shape
        bufs.append(_cast_into_slot(w_mats[n], place).reshape(N_CHIPS, 2, r // 2, c))
    gathered = _gather_weights(bufs)
    wts = {}
    for n, g in zip(BIG, gathered):
        r, c = w_mats[n].shape
        wts[n] = g.reshape(N_CHIPS, r, c)

    sm = {
        "g_mix": g_mix.reshape(1, D_MODEL), "g_ffn": g_ffn.reshape(1, D_MODEL), "g_final": g_final.reshape(1, D_MODEL),
        "a_re": a_re[0], "a_im": a_im[0], "log_dt": log_dt[0], "b_re": b_re[0], "b_im": b_im[0], "c_re": c_re[0],
        "c_im": c_im[0], "d_skip": d_skip[0],
    }
    s = x.shape[1]
    loss_part, grad_x, big_g, small_g = _local_step(x[0], p[0, 0], positions[0], loss_target[0], sm, wts)

    g5 = []
    for n in BIG:
        r, c = w_mats[n].shape
        g5.append(big_g[n].reshape(N_CHIPS, 2, r // 2, c))
    theirs = _pair_exchange(g5)
    chip_parts = [_pair_sum(g, t, place) for g, t in zip(g5, theirs)]
    chip_got = _chip_exchange(chip_parts)
    halves = [_chip_sum(own, got, place) for own, got in zip(chip_parts, chip_got)]
    other_halves = _pair_gather(halves)

    results = {}
    for n, mine, other in zip(BIG, halves, other_halves):
        r, c = w_mats[n].shape
        shp = big_w[n].shape
        outs = _adamw_halves("adamw_" + n, mine, other, w_mats[n], given["m_" + n].reshape(r, c),
                             given["v_" + n].reshape(r, c), place)
        results[n] = [o.reshape(shp) for o in outs]

    small_shapes = [given[n].shape for n in SMALL]
    vec = _pack([small_g[n] for n in SMALL] + [loss_part.reshape(1)])
    tot = _device_sum(vec, _all_exchange(vec), place)
    n_small = sum(math.prod(shp) for shp in small_shapes)
    loss = tot.reshape(-1)[n_small]
    w_s = _pack([given[n] for n in SMALL])
    m_s = _pack([given["m_" + n] for n in SMALL])
    v_s = _pack([given["v_" + n] for n in SMALL])
    rows_s = w_s.shape[0]
    g_s = tot.reshape(-1)[: rows_s * 128].reshape(rows_s, 128)
    outs_s = _adamw("adamw_small", g_s, w_s, m_s, v_s)
    for kind, mat in enumerate(outs_s):
        for n, arr in zip(SMALL, _unpack(mat, small_shapes)):
            results.setdefault(n, [None] * 4)[kind] = arr

    order = ("g_mix", "w_in", "a_re", "a_im", "log_dt", "b_re", "b_im", "c_re", "c_im", "d_skip", "w_attn_proj", "w_glu_a",
             "w_glu_b", "w_out", "g_ffn", "w_ffn_gate", "w_ffn_up", "w_ffn_down", "w_ple_gate", "w_ple_proj", "g_final")
    out = [loss, grad_x.reshape(1, s, D_MODEL)]
    for kind in range(4):
        out += [results[n][kind] for n in order]
    return tuple(out)
```

```python
import math

import jax
import jax.numpy as jnp
from jax import lax
from jax.experimental import pallas as pl
from jax.experimental.pallas import tpu as pltpu

F32 = jnp.float32
BF16 = jnp.bfloat16

D_MODEL = 1024
HEAD_DIM = 128
HEADS_PER_GROUP = 4
GROUP_WIDTH = HEADS_PER_GROUP * HEAD_DIM
GROUP_DILATIONS = (1, 4, 16)
N_GROUPS = len(GROUP_DILATIONS)
ATTN_BLOCK = 128
ROPE_DIM = 32
ROPE_HALF = 16
ROPE_THETA = 500000.0
SSM_WIDTH = 512
SSM_GROUPS = 32
SSM_GROUP = 16
SSM_STATE = 64
N_STATE = SSM_GROUPS * SSM_STATE
SSM_SUPER = 4
IN_WIDTH = 7168
COL_U = 4608
COL_GA = 5120
COL_GS = 6144
D_FF = 2816
N_CHIPS = 4
D_FF_Q = D_FF // N_CHIPS
PLE_DIM = 256
EPS = 1e-6
ADAM_LR = 0.001
ADAM_B1 = 0.9
ADAM_B2 = 0.999
ADAM_EPS = 1e-08
ADAM_WD = 0.01
ADAM_STEP = 10
NEG_BIG = -1e30
VMEM_LIMIT_BYTES = 56 * 1024 * 1024
MESH = pl.DeviceIdType.MESH

_DIMS = {
    "nn": (((1,), (0,)), ((), ())),
    "nt": (((1,), (1,)), ((), ())),
    "tn": (((0,), (0,)), ((), ())),
}


def _params(n_grid):
    return pltpu.CompilerParams(dimension_semantics=("arbitrary",) * n_grid, vmem_limit_bytes=VMEM_LIMIT_BYTES)


def _sig(v):
    return 1.0 / (1.0 + jnp.exp(-v))


def _dot(a, b, mode):
    return lax.dot_general(a, b, _DIMS[mode], preferred_element_type=F32)


def _mm(name, grid, pairs, mode, outs, epilogue=None, extras=(), acc_outs=(), acc_shape=None, j_outer=False,
        sum_pairs=True):
    gi, gj, gk = grid
    n_p, n_e, n_o, n_a = len(pairs), len(extras), len(outs), len(acc_outs)
    assert not n_a or gj == 1
    assert sum_pairs or gk == 1

    def order(imap):
        return (lambda j, i, k: imap(i, j, k)) if j_outer else imap

    shared_a = [pr[0] is None for pr in pairs]
    n_in = 2 * n_p - sum(shared_a)

    def body(*refs):
        pair_refs = list(refs[:n_in])
        extra_refs = refs[n_in: n_in + n_e]
        out_refs = refs[n_in + n_e: n_in + n_e + n_o]
        sum_refs = refs[n_in + n_e + n_o: n_in + n_e + n_o + n_a]
        i = pl.program_id(1 if j_outer else 0)
        k = pl.program_id(2)
        part = None if sum_pairs else []
        a = None
        for t in range(n_p):
            if not shared_a[t]:
                a = pair_refs.pop(0)[...].astype(BF16)
            b = pair_refs.pop(0)[...].astype(BF16)
            d = _dot(a, b, mode)
            if sum_pairs:
                part = d if part is None else part + d
            else:
                part.append(d)

        def finish(acc):
            tiles, sums = epilogue(acc, *[e[...] for e in extra_refs]) if epilogue is not None else ((acc,), ())
            for o_ref, tile in zip(out_refs, tiles):
                o_ref[...] = tile.astype(o_ref.dtype)
            if n_a:
                @pl.when(i == 0)
                def _():
                    for s_ref in sum_refs:
                        s_ref[...] = jnp.zeros_like(s_ref)

                for s_ref, s in zip(sum_refs, sums):
                    s_ref[...] += s

        if gk == 1:
            finish(part)
        else:
            acc_ref = refs[-1]

            @pl.when(k == 0)
            def _():
                acc_ref[...] = part

            @pl.when(k > 0)
            def _():
                acc_ref[...] += part

            @pl.when(k == gk - 1)
            def _():
                finish(acc_ref[...])

    in_specs, args = [], []
    for a, a_block, a_imap, b, b_block, b_imap in pairs:
        if a is not None:
            in_specs.append(pl.BlockSpec(a_block, order(a_imap)))
            args.append(a)
        in_specs.append(pl.BlockSpec(b_block, order(b_imap)))
        args.append(b)
    for e, e_block, e_imap in extras:
        in_specs.append(pl.BlockSpec(e_block, order(e_imap)))
        args.append(e)
    out_shape = [jax.ShapeDtypeStruct(shape, dtype) for shape, dtype, _, _ in outs]
    out_specs = [pl.BlockSpec(block, order(imap)) for _, _, block, imap in outs]
    for shape, dtype in acc_outs:
        out_shape.append(jax.ShapeDtypeStruct(shape, dtype))
        out_specs.append(pl.BlockSpec(shape, lambda i, j, k: (0, 0)))
    scratch = [pltpu.VMEM(acc_shape, F32)] if gk > 1 else []
    return pl.pallas_call(
        body, name=name, grid=(gj, gi, gk) if j_outer else grid, in_specs=in_specs, out_specs=out_specs,
        out_shape=out_shape, scratch_shapes=scratch, compiler_params=_params(3),
    )(*args)


def _ew(name, grid, ins, outs, fn, acc_outs=(), place=None):
    n_i, n_o, n_a = len(ins), len(outs), len(acc_outs)
    ng = len(grid)
    n_s = 0 if place is None else 1

    def body(*refs):
        in_refs = refs[n_s: n_s + n_i]
        out_refs = refs[n_s + n_i: n_s + n_i + n_o]
        sum_refs = refs[n_s + n_i + n_o:]
        pids = tuple(pl.program_id(a) for a in range(ng))
        if n_s:
            pids = (refs[0],) + pids
        tiles, sums = fn(pids, *[r[...] for r in in_refs])
        for o_ref, tile in zip(out_refs, tiles):
            o_ref[...] = tile.astype(o_ref.dtype)
        if n_a:
            first = pids[0] == 0
            for p_ in pids[1:]:
                first = jnp.logical_and(first, p_ == 0)

            @pl.when(first)
            def _():
                for s_ref in sum_refs:
                    s_ref[...] = jnp.zeros_like(s_ref)

            for s_ref, s in zip(sum_refs, sums):
                s_ref[...] += s

    in_specs = [pl.BlockSpec(block, imap) for _, block, imap in ins]
    out_shape = [jax.ShapeDtypeStruct(shape, dtype) for shape, dtype, _, _ in outs]
    out_specs = [pl.BlockSpec(block, imap) for _, _, block, imap in outs]
    for shape, dtype in acc_outs:
        out_shape.append(jax.ShapeDtypeStruct(shape, dtype))
        out_specs.append(pl.BlockSpec(shape, lambda *_, nd=len(shape): (0,) * nd))
    arrays = [a for a, _, _ in ins]
    if n_s:
        assert not n_a
        spec = pltpu.PrefetchScalarGridSpec(num_scalar_prefetch=1, grid=grid, in_specs=in_specs, out_specs=out_specs)
        return pl.pallas_call(body, name=name, grid_spec=spec, out_shape=out_shape, compiler_params=_params(ng))(
            place, *arrays)
    return pl.pallas_call(
        body, name=name, grid=grid, in_specs=in_specs, out_specs=out_specs, out_shape=out_shape,
        compiler_params=_params(ng),
    )(*arrays)


def _rows(tm, width):
    return (tm, width), (lambda i: (i, 0))


def _rms_fwd_tile(h, g):
    r = lax.rsqrt(jnp.mean(h * h, axis=-1, keepdims=True) + EPS)
    return h * r * g


def _rms_bwd_tile(dn, h, g):
    r = lax.rsqrt(jnp.mean(h * h, axis=-1, keepdims=True) + EPS)
    hhat = h * r
    gy = dn * g
    dh = r * (gy - hhat * jnp.mean(gy * hhat, axis=-1, keepdims=True))
    dg = jnp.sum(dn * hhat, axis=0, keepdims=True)
    return dh, dg


def _rope_tables(pos_col, inv_row, tm):
    s = pos_col.shape[0]

    def fn(pids, pos, inv):
        ang = pos * inv
        lane = lax.broadcasted_iota(jnp.int32, ang.shape, 1)
        cs = jnp.where(lane < ROPE_DIM, jnp.cos(ang), 1.0)
        sn = jnp.sin(ang)
        s_lo = jnp.where(lane < ROPE_HALF, -sn, 0.0)
        s_hi = jnp.where(jnp.logical_and(lane >= ROPE_HALF, lane < ROPE_DIM), sn, 0.0)
        return (cs, s_lo, s_hi), ()

    blk, imap = _rows(tm, 128)
    return _ew(
        "rope_tables", (s // tm,),
        [(pos_col, (tm, 1), lambda i: (i, 0)), (inv_row, (1, 128), lambda i: (0, 0))],
        [((s, 128), F32, blk, imap)] * 3, fn,
    )


def _rope(xh, cs, s_lo, s_hi):
    return xh * cs + pltpu.roll(xh, HEAD_DIM - ROPE_HALF, 1) * s_lo + pltpu.roll(xh, ROPE_HALF, 1) * s_hi


def _rope_t(gh, cs, s_lo, s_hi):
    return gh * cs + pltpu.roll(gh * s_lo, ROPE_HALF, 1) + pltpu.roll(gh * s_hi, HEAD_DIM - ROPE_HALF, 1)


def _attn_geometry(length):
    nb = length // ATTN_BLOCK
    gq = min(4, nb)
    assert nb % gq == 0
    return nb, gq, gq * ATTN_BLOCK, nb // gq


def _band_masks():
    qi = lax.broadcasted_iota(jnp.int32, (ATTN_BLOCK, ATTN_BLOCK), 0)
    kj = lax.broadcasted_iota(jnp.int32, (ATTN_BLOCK, ATTN_BLOCK), 1)
    return kj <= qi, kj >= qi


def _band_mask_pair():
    qi = lax.broadcasted_iota(jnp.int32, (ATTN_BLOCK, 2 * ATTN_BLOCK), 0)
    cj = lax.broadcasted_iota(jnp.int32, (ATTN_BLOCK, 2 * ATTN_BLOCK), 1)
    in_cur = cj >= ATTN_BLOCK
    band = jnp.logical_or(jnp.logical_and(in_cur, cj - ATTN_BLOCK <= qi),
                          jnp.logical_and(cj < ATTN_BLOCK, cj >= qi))
    return band, in_cur


def _attn_fwd(qv, kv, vv, dil):
    length = qv.shape[0]
    nb, gq, rows, ni = _attn_geometry(length)

    def body(q_ref, kc_ref, kp_ref, vc_ref, vp_ref, o_ref, l_ref):
        i = pl.program_id(1)
        band, in_cur = _band_mask_pair()
        band_first = jnp.logical_and(band, jnp.logical_or(in_cur, i > 0))
        for h in range(HEADS_PER_GROUP):
            cols = slice(h * HEAD_DIM, (h + 1) * HEAD_DIM)
            qh = q_ref[:, cols]
            k_all = jnp.concatenate([kp_ref[:, cols], kc_ref[:, cols]], axis=0)
            v_all = jnp.concatenate([vp_ref[:, cols], vc_ref[:, cols]], axis=0)
            for jj in range(gq):
                rws = slice(jj * ATTN_BLOCK, (jj + 1) * ATTN_BLOCK)
                two = slice(jj * ATTN_BLOCK, (jj + 2) * ATTN_BLOCK)
                s = jnp.where(band_first if jj == 0 else band, _dot(qh[rws], k_all[two], "nt"), NEG_BIG)
                m = jnp.max(s, axis=-1, keepdims=True)
                pexp = jnp.exp(s - m)
                den = jnp.sum(pexp, axis=-1, keepdims=True)
                o = _dot(pexp.astype(BF16), v_all[two], "nn")
                o_ref[rws, cols] = (o / den).astype(o_ref.dtype)
                l_ref[rws, cols] = jnp.broadcast_to(m + jnp.log(den), (ATTN_BLOCK, HEAD_DIM))

    cur = lambda r, i: (i, r)
    prev = lambda r, i: (jnp.maximum(i * gq - 1, 0), r)
    wide = pl.BlockSpec((rows, GROUP_WIDTH), cur)
    wide_prev = pl.BlockSpec((ATTN_BLOCK, GROUP_WIDTH), prev)
    return pl.pallas_call(
        body, name=f"attn_fwd_d{dil}", grid=(dil, ni),
        in_specs=[wide, wide, wide_prev, wide, wide_prev],
        out_specs=[wide, wide],
        out_shape=[jax.ShapeDtypeStruct(qv.shape, BF16), jax.ShapeDtypeStruct(qv.shape, F32)],
        compiler_params=_params(2),
    )(qv, kv, kv, vv, vv)


def _attn_bwd(qv, kv, vv, dov, ov, lv, dil):
    length = qv.shape[0]
    nb, gq, rows, ni = _attn_geometry(length)

    def body(qc_ref, qn_ref, kc_ref, kp_ref, vc_ref, vp_ref, doc_ref, don_ref, oc_ref, on_ref, lc_ref, ln_ref,
             dq_ref, dk_ref, dv_ref):
        i = pl.program_id(1)
        _, mask_p = _band_masks()
        band, in_cur = _band_mask_pair()
        band_first = jnp.logical_and(band, jnp.logical_or(in_cur, i > 0))
        has_next = i < ni - 1

        def tile(qb, kb, vb, dob, l_col, delta, mask):
            s = _dot(qb, kb, "nt")
            p = jnp.where(mask, jnp.exp(s - l_col), 0.0)
            dp = _dot(dob, vb, "nt")
            ds = p * (dp - delta)
            return p, ds

        for h in range(HEADS_PER_GROUP):
            cols = slice(h * HEAD_DIM, (h + 1) * HEAD_DIM)
            q_c = qc_ref[:, cols]
            q_n = qn_ref[:, cols]
            k_all = jnp.concatenate([kp_ref[:, cols], kc_ref[:, cols]], axis=0)
            v_all = jnp.concatenate([vp_ref[:, cols], vc_ref[:, cols]], axis=0)
            do_c = doc_ref[:, cols]
            do_n = don_ref[:, cols]
            l_c = lc_ref[:, h * HEAD_DIM:h * HEAD_DIM + 1]
            l_n = ln_ref[:, h * HEAD_DIM:h * HEAD_DIM + 1]
            dl_c = jnp.sum(do_c.astype(F32) * oc_ref[:, cols].astype(F32), axis=-1, keepdims=True)
            dl_n = jnp.sum(do_n.astype(F32) * on_ref[:, cols].astype(F32), axis=-1, keepdims=True)
            dq_blocks, dk_blocks, dv_blocks = [], [None] * (gq + 1), [None] * (gq + 1)

            def add(lst, idx, val):
                lst[idx] = val if lst[idx] is None else lst[idx] + val

            for jj in range(gq):
                rws = slice(jj * ATTN_BLOCK, (jj + 1) * ATTN_BLOCK)
                two = slice(jj * ATTN_BLOCK, (jj + 2) * ATTN_BLOCK)
                qb, dob = q_c[rws], do_c[rws]
                p, ds = tile(qb, k_all[two], v_all[two], dob, l_c[rws], dl_c[rws], band_first if jj == 0 else band)
                dsb = ds.astype(BF16)
                dq_blocks.append(_dot(dsb, k_all[two], "nn"))
                dk2 = _dot(dsb, qb, "tn")
                dv2 = _dot(p.astype(BF16), dob, "tn")
                add(dk_blocks, jj, dk2[:ATTN_BLOCK])
                add(dk_blocks, jj + 1, dk2[ATTN_BLOCK:])
                add(dv_blocks, jj, dv2[:ATTN_BLOCK])
                add(dv_blocks, jj + 1, dv2[ATTN_BLOCK:])
            last = slice(gq * ATTN_BLOCK, (gq + 1) * ATTN_BLOCK)
            p, ds = tile(q_n, k_all[last], v_all[last], do_n, l_n, dl_n, jnp.logical_and(mask_p, has_next))
            add(dk_blocks, gq, _dot(ds.astype(BF16), q_n, "tn"))
            add(dv_blocks, gq, _dot(p.astype(BF16), do_n, "tn"))
            dk_blocks, dv_blocks = dk_blocks[1:], dv_blocks[1:]
            for jj in range(gq):
                rws = slice(jj * ATTN_BLOCK, (jj + 1) * ATTN_BLOCK)
                dq_ref[rws, cols] = dq_blocks[jj].astype(dq_ref.dtype)
                dk_ref[rws, cols] = dk_blocks[jj].astype(dk_ref.dtype)
                dv_ref[rws, cols] = dv_blocks[jj].astype(dv_ref.dtype)

    cur = lambda r, i: (i, r)
    prev = lambda r, i: (jnp.maximum(i * gq - 1, 0), r)
    nxt = lambda r, i: (jnp.minimum((i + 1) * gq, nb - 1), r)
    wide = pl.BlockSpec((rows, GROUP_WIDTH), cur)
    wide_prev = pl.BlockSpec((ATTN_BLOCK, GROUP_WIDTH), prev)
    wide_next = pl.BlockSpec((ATTN_BLOCK, GROUP_WIDTH), nxt)
    return pl.pallas_call(
        body, name=f"attn_bwd_d{dil}", grid=(dil, ni),
        in_specs=[wide, wide_next, wide, wide_prev, wide, wide_prev, wide, wide_next, wide, wide_next, wide, wide_next],
        out_specs=[wide, wide, wide],
        out_shape=[jax.ShapeDtypeStruct(qv.shape, BF16)] * 3,
        compiler_params=_params(2),
    )(qv, qv, kv, kv, vv, vv, dov, dov, ov, ov, lv, lv)


def _rope_qk(z, tabs, tm):
    s = z.shape[0]
    width = 2 * N_GROUPS * GROUP_WIDTH
    scale = 1.0 / math.sqrt(HEAD_DIM)

    def fn(pids, zt, cs, s_lo, s_hi):
        heads = []
        for hh in range(width // HEAD_DIM):
            r = _rope(zt[:, hh * HEAD_DIM:(hh + 1) * HEAD_DIM].astype(F32), cs, s_lo, s_hi)
            heads.append((r * scale if hh < width // HEAD_DIM // 2 else r).astype(BF16))
        return (jnp.concatenate(heads, axis=1),), ()

    tab = ((tm, HEAD_DIM), lambda i: (i, 0))
    blk = ((tm, width), lambda i: (i, 0))
    return _ew("rope_qk", (s // tm,), [(z, *blk)] + [(t, *tab) for t in tabs], [((s, width), BF16, *blk)], fn)[0]


def _rope_qk_bwd(parts, tabs, tm):
    s = parts[0].shape[0]
    width = 2 * N_GROUPS * GROUP_WIDTH
    scale = 1.0 / math.sqrt(HEAD_DIM)

    def fn(pids, *tiles):
        cs, s_lo, s_hi = tiles[len(parts):]
        heads = []
        for a in range(len(parts)):
            for h in range(HEADS_PER_GROUP):
                g = _rope_t(tiles[a][:, h * HEAD_DIM:(h + 1) * HEAD_DIM].astype(F32), cs, s_lo, s_hi)
                heads.append((g * scale if a < N_GROUPS else g).astype(BF16))
        return (jnp.concatenate(heads, axis=1),), ()

    tab = ((tm, HEAD_DIM), lambda i: (i, 0))
    grp = ((tm, GROUP_WIDTH), lambda i: (i, 0))
    return _ew("rope_qk_bwd", (s // tm,), [(a, *grp) for a in parts] + [(t, *tab) for t in tabs],
               [((s, width), BF16, (tm, width), lambda i: (i, 0))], fn)[0]


def _to_view(a, dil):
    s, w = a.shape
    return a.reshape(s // dil, dil * w)


def _from_view(a, dil):
    length, dw = a.shape
    return a.reshape(length * dil, dw // dil)


def _discretise(a_re, a_im, log_dt, bt_re, bt_im):
    dt = jnp.exp(log_dt)
    mag = jnp.exp(a_re * dt)
    bar_re = mag * jnp.cos(a_im * dt)
    bar_im = mag * jnp.sin(a_im * dt)
    nr = bar_re - 1.0
    ni = bar_im
    den = a_re * a_re + a_im * a_im
    z_re = (nr * a_re + ni * a_im) / den
    z_im = (ni * a_re - nr * a_im) / den
    bb_re = z_re[:, None, :] * bt_re - z_im[:, None, :] * bt_im
    bb_im = z_re[:, None, :] * bt_im + z_im[:, None, :] * bt_re
    return bar_re, bar_im, bb_re, bb_im


def _ssm_prep(a_re, a_im, log_dt, bt_re, bt_im):
    def body(ar, ai, ld, br, bi, o_lr, o_li, o_br, o_bi):
        lr, li, bbr, bbi = _discretise(ar[...], ai[...], ld[...], br[...], bi[...])
        o_lr[...] = lr
        o_li[...] = li
        o_br[...] = bbr
        o_bi[...] = bbi

    sm = jax.ShapeDtypeStruct((SSM_GROUPS, SSM_STATE), F32)
    bg = jax.ShapeDtypeStruct((SSM_GROUPS, SSM_GROUP, SSM_STATE), F32)
    return pl.pallas_call(body, name="ssm_prep", out_shape=[sm, sm, bg, bg])(a_re, a_im, log_dt, bt_re, bt_im)


def _ssm_param_bwd(a_re, a_im, log_dt, bt_re, bt_im, d_lr, d_li, d_bbr, d_bbi):
    def body(ar, ai, ld, br, bi, g_lr, g_li, g_br, g_bi, o_ar, o_ai, o_ld, o_br, o_bi):
        _, vjp = jax.vjp(_discretise, ar[...], ai[...], ld[...], br[...], bi[...])
        d_ar, d_ai, d_ld, d_br, d_bi = vjp((g_lr[...], g_li[...], g_br[...], g_bi[...]))
        o_ar[...] = d_ar
        o_ai[...] = d_ai
        o_ld[...] = d_ld
        o_br[...] = d_br
        o_bi[...] = d_bi

    sm = jax.ShapeDtypeStruct((SSM_GROUPS, SSM_STATE), F32)
    col = jax.ShapeDtypeStruct((SSM_GROUPS, 1), F32)
    bg = jax.ShapeDtypeStruct((SSM_GROUPS, SSM_GROUP, SSM_STATE), F32)
    return pl.pallas_call(body, name="ssm_param_bwd", out_shape=[sm, sm, col, bg, bg])(
        a_re, a_im, log_dt, bt_re, bt_im, d_lr, d_li, d_bbr, d_bbi)


def _block_diag(t, rows_per, cols_per):
    t4 = t.reshape(SSM_SUPER, 8, rows_per, cols_per)
    eye = jnp.eye(8, dtype=t.dtype)
    return jnp.einsum("bgrc,gh->bgrhc", t4, eye).reshape(SSM_SUPER, 8 * rows_per, 8 * cols_per)


def _block_diag_t(dense, rows_per, cols_per):
    t = dense.reshape(SSM_SUPER, 8, rows_per, 8, cols_per)
    eye = jnp.eye(8, dtype=dense.dtype)
    return jnp.einsum("bgrhc,gh->bgrc", t, eye).reshape(SSM_GROUPS, rows_per, cols_per)


def _gelu(v):
    c = math.sqrt(2.0 / math.pi)
    return 0.5 * v * (1.0 + jnp.tanh(c * (v + 0.044715 * v * v * v)))


def _gelu_grad(v):
    c = math.sqrt(2.0 / math.pi)
    t = jnp.tanh(c * (v + 0.044715 * v * v * v))
    return 0.5 * (1.0 + t) + 0.5 * v * (1.0 - t * t) * c * (1.0 + 3.0 * 0.044715 * v * v)


SUB = 8


SCAN_STEPS = (1, 2, 4)
N_SCAN_TABLES = 2 + 2 * len(SCAN_STEPS)


def _scan_tables(tab_ref, lam_re, lam_im, reverse, conj):
    lr = lam_re
    li = -lam_im if conj else lam_im
    powers = [(lr, li)]
    for _ in range(SUB - 1):
        pr, pi = powers[-1]
        powers.append((pr * lr - pi * li, pr * li + pi * lr))
    row = lax.broadcasted_iota(jnp.int32, (SUB, N_STATE), 0)
    if reverse:
        row = SUB - 1 - row
    wide = lambda v: jnp.broadcast_to(v, (SUB, N_STATE))
    p_re = jnp.zeros((SUB, N_STATE), F32)
    p_im = jnp.zeros((SUB, N_STATE), F32)
    for j in range(SUB):
        p_re = jnp.where(row == j, wide(powers[j][0]), p_re)
        p_im = jnp.where(row == j, wide(powers[j][1]), p_im)
    tab_ref[0] = p_re
    tab_ref[1] = p_im
    for idx, k in enumerate(SCAN_STEPS):
        tab_ref[2 + 2 * idx] = jnp.where(row >= k, wide(powers[k - 1][0]), 0.0)
        tab_ref[3 + 2 * idx] = jnp.where(row >= k, wide(powers[k - 1][1]), 0.0)


def _scan_rows(g_re_ref, g_im_ref, tab_ref, carry, n_rows, reverse):
    last = 0 if reverse else SUB - 1

    def tile_step(tt, state):
        cr, ci = state
        t8 = (n_rows // SUB - 1 - tt) if reverse else tt
        start = pl.multiple_of(t8 * SUB, SUB)
        xr = g_re_ref[pl.ds(start, SUB), :]
        xi = g_im_ref[pl.ds(start, SUB), :]
        for idx, k in enumerate(SCAN_STEPS):
            mr = tab_ref[2 + 2 * idx]
            mi = tab_ref[3 + 2 * idx]
            shift = SUB - k if reverse else k
            sr = pltpu.roll(xr, shift, 0)
            si = pltpu.roll(xi, shift, 0)
            xr, xi = xr + (mr * sr - mi * si), xi + (mr * si + mi * sr)
        pr = tab_ref[0]
        pi = tab_ref[1]
        xr, xi = xr + (pr * cr - pi * ci), xi + (pr * ci + pi * cr)
        g_re_ref[pl.ds(start, SUB), :] = xr
        g_im_ref[pl.ds(start, SUB), :] = xi
        return (jnp.broadcast_to(xr[last:last + 1, :], (SUB, N_STATE)),
                jnp.broadcast_to(xi[last:last + 1, :], (SUB, N_STATE)))

    return lax.fori_loop(0, n_rows // SUB, tile_step, carry)


def _ssm_fwd(z, b_re, b_im, c_re, c_im, lam_re, lam_im, d_skip, chunk):
    s = z.shape[0]

    def body(u_ref, bre, bim, cre, cim, lre, lim, dsk, hre_ref, him_ref, ys_ref, yg_ref, car_re, car_im, tabs):
        i = pl.program_id(0)

        @pl.when(i == 0)
        def _():
            car_re[...] = jnp.zeros_like(car_re)
            car_im[...] = jnp.zeros_like(car_im)
            _scan_tables(tabs, lre[...], lim[...], False, False)

        u = u_ref[...]
        for b in range(SSM_SUPER):
            ub = u[:, b * 128:(b + 1) * 128]
            st = slice(b * 512, (b + 1) * 512)
            hre_ref[:, st] = _dot(ub, bre[b], "nn")
            him_ref[:, st] = _dot(ub, bim[b], "nn")
        sr, si = _scan_rows(hre_ref, him_ref, tabs, (car_re[...], car_im[...]), chunk, False)
        car_re[...] = sr
        car_im[...] = si
        uf = u.astype(F32)
        for b in range(SSM_SUPER):
            st = slice(b * 512, (b + 1) * 512)
            ch = slice(b * 128, (b + 1) * 128)
            y = _dot(hre_ref[:, st].astype(BF16), cre[b], "nn") - _dot(him_ref[:, st].astype(BF16), cim[b], "nn")
            y = y + dsk[:, ch] * uf[:, ch]
            ys_ref[:, ch] = y
            yg_ref[:, ch] = _gelu(y).astype(BF16)

    full3 = lambda i: (0, 0, 0)
    full2 = lambda i: (0, 0)
    row = lambda i: (i, 0)
    u_col = COL_U // SSM_WIDTH
    return pl.pallas_call(
        body, name="ssm_fwd", grid=(s // chunk,),
        in_specs=[pl.BlockSpec((chunk, SSM_WIDTH), lambda i: (i, u_col)),
                  pl.BlockSpec((SSM_SUPER, 128, 512), full3), pl.BlockSpec((SSM_SUPER, 128, 512), full3),
                  pl.BlockSpec((SSM_SUPER, 512, 128), full3), pl.BlockSpec((SSM_SUPER, 512, 128), full3),
                  pl.BlockSpec((1, N_STATE), full2), pl.BlockSpec((1, N_STATE), full2), pl.BlockSpec((1, SSM_WIDTH), full2)],
        out_specs=[pl.BlockSpec((chunk, N_STATE), row), pl.BlockSpec((chunk, N_STATE), row),
                   pl.BlockSpec((chunk, SSM_WIDTH), row), pl.BlockSpec((chunk, SSM_WIDTH), row)],
        out_shape=[jax.ShapeDtypeStruct((s, N_STATE), F32), jax.ShapeDtypeStruct((s, N_STATE), F32),
                   jax.ShapeDtypeStruct((s, SSM_WIDTH), F32), jax.ShapeDtypeStruct((s, SSM_WIDTH), BF16)],
        scratch_shapes=[pltpu.VMEM((SUB, N_STATE), F32), pltpu.VMEM((SUB, N_STATE), F32),
                        pltpu.VMEM((N_SCAN_TABLES, SUB, N_STATE), F32)],
        compiler_params=_params(1),
    )(z, b_re, b_im, c_re, c_im, lam_re, lam_im, d_skip)


def _ssm_bwd(dys, z, h_re, h_im, b_re, b_im, c_re, c_im, lam_re, lam_im, d_skip, chunk):
    s = z.shape[0]
    n_chunks = s // chunk

    def body(dy_ref, u_ref, hre_ref, him_ref, hpr_ref, hpi_ref, bre, bim, cre, cim, lre, lim, dsk,
             du_ref, dlr_ref, dli_ref, dbr_ref, dbi_ref, dcr_ref, dci_ref, dd_ref, are, aim, car_re, car_im, tabs):
        i = pl.program_id(0)
        n = n_chunks - 1 - i

        @pl.when(i == 0)
        def _():
            car_re[...] = jnp.zeros_like(car_re)
            car_im[...] = jnp.zeros_like(car_im)
            _scan_tables(tabs, lre[...], lim[...], True, True)
            for r in (dlr_ref, dli_ref, dbr_ref, dbi_ref, dcr_ref, dci_ref, dd_ref):
                r[...] = jnp.zeros_like(r)

        dy = dy_ref[...]
        dyb = dy.astype(BF16)
        u = u_ref[...]
        for b in range(SSM_SUPER):
            ch = slice(b * 128, (b + 1) * 128)
            st = slice(b * 512, (b + 1) * 512)
            are[:, st] = _dot(dyb[:, ch], cre[b], "nt")
            aim[:, st] = -_dot(dyb[:, ch], cim[b], "nt")
        sr, si = _scan_rows(are, aim, tabs, (car_re[...], car_im[...]), chunk, True)
        car_re[...] = sr
        car_im[...] = si
        row_id = lax.broadcasted_iota(jnp.int32, (chunk, N_STATE), 0)
        top_scale = jnp.where(n > 0, 1.0, 0.0)
        h_r = hre_ref[...]
        h_i = him_ref[...]
        hp_r = jnp.where(row_id == 0, hpr_ref[SUB - 1:SUB, :] * top_scale, pltpu.roll(h_r, 1, 0))
        hp_i = jnp.where(row_id == 0, hpi_ref[SUB - 1:SUB, :] * top_scale, pltpu.roll(h_i, 1, 0))
        a_r = are[...]
        a_i = aim[...]
        dlr_ref[...] += jnp.sum(a_r * hp_r + a_i * hp_i, axis=0, keepdims=True)
        dli_ref[...] += jnp.sum(a_i * hp_r - a_r * hp_i, axis=0, keepdims=True)
        dd_ref[...] += jnp.sum(dy * u.astype(F32), axis=0, keepdims=True)
        a_rb = a_r.astype(BF16)
        a_ib = a_i.astype(BF16)
        h_rb = h_r.astype(BF16)
        h_ib = h_i.astype(BF16)
        for b in range(SSM_SUPER):
            ch = slice(b * 128, (b + 1) * 128)
            st = slice(b * 512, (b + 1) * 512)
            dbr_ref[b] += _dot(u[:, ch], a_rb[:, st], "tn")
            dbi_ref[b] += _dot(u[:, ch], a_ib[:, st], "tn")
            dcr_ref[b] += _dot(h_rb[:, st], dyb[:, ch], "tn")
            dci_ref[b] += -_dot(h_ib[:, st], dyb[:, ch], "tn")
            du = _dot(a_rb[:, st], bre[b], "nt") + _dot(a_ib[:, st], bim[b], "nt") + dsk[:, ch] * dy[:, ch]
            du_ref[:, ch] = du.astype(du_ref.dtype)

    full3 = lambda i: (0, 0, 0)
    full2 = lambda i: (0, 0)
    rev = lambda i: (n_chunks - 1 - i, 0)
    above = lambda i: (jnp.maximum((n_chunks - 1 - i) * (chunk // SUB) - 1, 0), 0)
    u_col = COL_U // SSM_WIDTH
    b_spec = pl.BlockSpec((SSM_SUPER, 128, 512), full3)
    c_spec = pl.BlockSpec((SSM_SUPER, 512, 128), full3)
    vec = pl.BlockSpec((1, N_STATE), full2)
    return pl.pallas_call(
        body, name="ssm_bwd", grid=(n_chunks,),
        in_specs=[pl.BlockSpec((chunk, SSM_WIDTH), rev),
                  pl.BlockSpec((chunk, SSM_WIDTH), lambda i: (n_chunks - 1 - i, u_col)),
                  pl.BlockSpec((chunk, N_STATE), rev), pl.BlockSpec((chunk, N_STATE), rev),
                  pl.BlockSpec((SUB, N_STATE), above), pl.BlockSpec((SUB, N_STATE), above),
                  b_spec, b_spec, c_spec, c_spec, vec, vec, pl.BlockSpec((1, SSM_WIDTH), full2)],
        out_specs=[pl.BlockSpec((chunk, SSM_WIDTH), rev), vec, vec, b_spec, b_spec, c_spec, c_spec,
                   pl.BlockSpec((1, SSM_WIDTH), full2)],
        out_shape=[jax.ShapeDtypeStruct((s, SSM_WIDTH), BF16),
                   jax.ShapeDtypeStruct((1, N_STATE), F32), jax.ShapeDtypeStruct((1, N_STATE), F32),
                   jax.ShapeDtypeStruct((SSM_SUPER, 128, 512), F32), jax.ShapeDtypeStruct((SSM_SUPER, 128, 512), F32),
                   jax.ShapeDtypeStruct((SSM_SUPER, 512, 128), F32), jax.ShapeDtypeStruct((SSM_SUPER, 512, 128), F32),
                   jax.ShapeDtypeStruct((1, SSM_WIDTH), F32)],
        scratch_shapes=[pltpu.VMEM((chunk, N_STATE), F32), pltpu.VMEM((chunk, N_STATE), F32),
                        pltpu.VMEM((SUB, N_STATE), F32), pltpu.VMEM((SUB, N_STATE), F32),
                        pltpu.VMEM((N_SCAN_TABLES, SUB, N_STATE), F32)],
        compiler_params=_params(1),
    )(dys, z, h_re, h_im, h_re, h_im, b_re, b_im, c_re, c_im, lam_re, lam_im, d_skip)


def _local_step(x, p, pos, tgt, sm, wts):
    s = x.shape[0]
    tm = min(512, s)
    ts = min(1024, s)
    chunk = min(256, s)
    ni = s // tm
    nk = s // ts
    w_in, w_ap, w_ga, w_gb, w_out, w_fg, w_fu, w_fd, w_pg, w_pp = (
        wts[k] for k in ("w_in", "w_attn_proj", "w_glu_a", "w_glu_b", "w_out", "w_ffn_gate", "w_ffn_up", "w_ffn_down",
                         "w_ple_gate", "w_ple_proj"))
    w_out2 = w_out.reshape(D_MODEL, D_MODEL)
    w_pg2 = w_pg.reshape(D_MODEL, D_MODEL)
    g_mix, g_ffn, g_final = sm["g_mix"], sm["g_ffn"], sm["g_final"]
    rowblk, rowmap = _rows(tm, D_MODEL)
    vec1k = ((1, D_MODEL), lambda *_: (0, 0))

    (n1,) = _ew("rms_mix", (ni,), [(x, rowblk, rowmap), (g_mix, *vec1k)], [((s, D_MODEL), BF16, rowblk, rowmap)],
                lambda pids, h, g: ((_rms_fwd_tile(h, g),), ()))

    half_in = IN_WIDTH // 8
    tmb = min(1024, s)
    nib = s // tmb
    (z,) = _mm("in_proj", (nib, 8, 1),
               [(n1, (tmb, D_MODEL), lambda i, j, k: (i, 0), w_in, (None, D_MODEL, half_in), lambda i, j, k: (j // 2, 0, j % 2))],
               "nn", [((s, IN_WIDTH), BF16, (tmb, half_in), lambda i, j, k: (i, j))], j_outer=True)

    inv = ROPE_THETA ** (-jnp.arange(ROPE_HALF, dtype=F32) * 2.0 / ROPE_DIM)
    inv_row = jnp.concatenate([inv, inv, jnp.zeros((HEAD_DIM - ROPE_DIM,), F32)]).reshape(1, HEAD_DIM)
    tabs = _rope_tables(pos.astype(F32).reshape(s, 1), inv_row, tm)

    qk = _rope_qk(z, tabs, tm)
    views, outs_g, lses_g = [], [], []
    for gi, dil in enumerate(GROUP_DILATIONS):
        q_g = _to_view(qk[:, gi * GROUP_WIDTH:(gi + 1) * GROUP_WIDTH], dil)
        k_g = _to_view(qk[:, 1536 + gi * GROUP_WIDTH:1536 + (gi + 1) * GROUP_WIDTH], dil)
        v_g = _to_view(z[:, 3072 + gi * GROUP_WIDTH:3072 + (gi + 1) * GROUP_WIDTH], dil)
        views.append((q_g, k_g, v_g))
        o_g, l_g = _attn_fwd(q_g, k_g, v_g, dil)
        outs_g.append(_from_view(o_g, dil))
        lses_g.append(_from_view(l_g, dil))

    def merge_fn(pids, o0, o1, o2, l0, l1, l2):
        m = jnp.maximum(jnp.maximum(l0, l1), l2)
        e0, e1, e2 = jnp.exp(l0 - m), jnp.exp(l1 - m), jnp.exp(l2 - m)
        den = e0 + e1 + e2
        return ((e0 * o0 + e1 * o1 + e2 * o2) / den, m + jnp.log(den)), ()

    gblk, gmap = _rows(tm, GROUP_WIDTH)
    attn, lse = _ew("attn_merge", (ni,), [(a, gblk, gmap) for a in outs_g + lses_g],
                    [((s, GROUP_WIDTH), BF16, gblk, gmap), ((s, GROUP_WIDTH), F32, gblk, gmap)], merge_fn)

    def chip_cols(parts):
        return (jnp.concatenate(parts, axis=1),), ()

    def proj_cols(name, a, width, w):
        blk = (None, width, 256)
        pairs = [(a, (tmb, width), lambda i, j, k: (i, 0), w, blk, lambda i, j, k: (0, 0, 0))]
        pairs += [(None, None, None, w, blk, (lambda i, j, k, q=q: (q, 0, 0))) for q in range(1, N_CHIPS)]
        return _mm(name, (nib, 1, 1), pairs, "nn", [((s, D_MODEL), BF16, (tmb, D_MODEL), lambda i, j, k: (i, 0))],
                   epilogue=chip_cols, sum_pairs=False)[0]

    def proj512(name, a, w):
        return proj_cols(name, a, GROUP_WIDTH, w)

    attn_d = proj512("attn_proj", attn, w_ap)

    bt_re = jnp.transpose(sm["b_re"], (0, 2, 1))
    bt_im = jnp.transpose(sm["b_im"], (0, 2, 1))
    log_dt_col = sm["log_dt"].reshape(SSM_GROUPS, 1)
    lam_re, lam_im, bbt_re, bbt_im = _ssm_prep(sm["a_re"], sm["a_im"], log_dt_col, bt_re, bt_im)
    b_re_m = _block_diag(bbt_re, SSM_GROUP, SSM_STATE).astype(BF16)
    b_im_m = _block_diag(bbt_im, SSM_GROUP, SSM_STATE).astype(BF16)
    c_re_m = _block_diag(jnp.transpose(sm["c_re"], (0, 2, 1)), SSM_STATE, SSM_GROUP).astype(BF16)
    c_im_m = _block_diag(jnp.transpose(sm["c_im"], (0, 2, 1)), SSM_STATE, SSM_GROUP).astype(BF16)
    lam_re_row = lam_re.reshape(1, N_STATE)
    lam_im_row = lam_im.reshape(1, N_STATE)
    d_skip_row = sm["d_skip"].reshape(1, SSM_WIDTH)
    h_re, h_im, ys, yg = _ssm_fwd(z, b_re_m, b_im_m, c_re_m, c_im_m, lam_re_row, lam_im_row, d_skip_row, chunk)

    pa = proj512("glu_a", yg, w_ga)
    pb = proj512("glu_b", yg, w_gb)

    ga_blk = ((tm, D_MODEL), lambda i: (i, COL_GA // D_MODEL))
    gs_blk = ((tm, D_MODEL), lambda i: (i, COL_GS // D_MODEL))

    def mix_fn(pids, ga, gs, ad, a, b):
        ga, gs, ad, a, b = (t.astype(F32) for t in (ga, gs, ad, a, b))
        return (_sig(ga) * ad + _sig(gs) * (a * _sig(b)),), ()

    (mix,) = _ew("gate_mix", (ni,), [(z, *ga_blk), (z, *gs_blk), (attn_d, rowblk, rowmap), (pa, rowblk, rowmap),
                                     (pb, rowblk, rowmap)], [((s, D_MODEL), BF16, rowblk, rowmap)], mix_fn)

    def out_epi(acc, xr, g):
        h1 = acc + xr
        return (h1, _rms_fwd_tile(h1, g)), ()

    m3 = lambda i, j, k: (i, 0)
    w3 = lambda i, j, k: (0, 0)
    h1, n2 = _mm("out_proj", (nib, 1, 1), [(mix, (tmb, D_MODEL), m3, w_out2, (D_MODEL, D_MODEL), w3)], "nn",
                 [((s, D_MODEL), F32, (tmb, D_MODEL), m3), ((s, D_MODEL), BF16, (tmb, D_MODEL), m3)],
                 epilogue=out_epi, extras=[(x, (tmb, D_MODEL), m3), (g_ffn, (1, D_MODEL), w3)])

    ffq = (None, tm, D_FF_Q)
    ffq_map = lambda i, j, k: (j, i, 0)

    def ffn_in_epi(parts):
        gt, u_ = parts
        return (gt, u_, gt * _sig(gt) * u_), ()

    w_ffq = (None, D_MODEL, D_FF_Q)
    w_ffq_j = lambda i, j, k: (j, 0, 0)
    gate, up, act = _mm("ffn_gate_up", (ni, N_CHIPS, 1),
                        [(n2, (tm, D_MODEL), m3, w_fg, w_ffq, w_ffq_j), (None, None, None, w_fu, w_ffq, w_ffq_j)], "nn",
                        [((N_CHIPS, s, D_FF_Q), BF16, ffq, ffq_map)] * 3, epilogue=ffn_in_epi, j_outer=True,
                        sum_pairs=False)

    (h2,) = _mm("ffn_down", (nib, 1, 1),
                [(act, (None, tmb, D_FF_Q), (lambda i, j, k, q=q: (q, i, 0)), w_fd, (None, D_FF_Q, D_MODEL),
                  (lambda i, j, k, q=q: (q, 0, 0))) for q in range(N_CHIPS)], "nn",
                [((s, D_MODEL), F32, (tmb, D_MODEL), m3)], epilogue=lambda acc, hr: ((acc + hr,), ()),
                extras=[(h1, (tmb, D_MODEL), m3)])

    pp = proj_cols("ple_proj", p, PLE_DIM, w_pp)

    def ple_epi(acc, hr, ppr):
        return (acc, hr + _sig(acc) * ppr.astype(F32)), ()

    gl, h3 = _mm("ple_gate", (nib, 1, 1), [(h2, (tmb, D_MODEL), m3, w_pg2, (D_MODEL, D_MODEL), w3)], "nn",
                 [((s, D_MODEL), BF16, (tmb, D_MODEL), m3), ((s, D_MODEL), F32, (tmb, D_MODEL), m3)],
                 epilogue=ple_epi, extras=[(h2, (tmb, D_MODEL), m3), (pp, (tmb, D_MODEL), m3)])

    def head_fn(pids, h, t, g):
        r = lax.rsqrt(jnp.mean(h * h, axis=-1, keepdims=True) + EPS)
        hhat = h * r
        diff = hhat * g - t
        loss = 0.5 * jnp.sum(jnp.mean(diff * diff, axis=-1, keepdims=True))
        dy = diff * (1.0 / D_MODEL)
        gy = dy * g
        dh = r * (gy - hhat * jnp.mean(gy * hhat, axis=-1, keepdims=True))
        return (dh,), (jnp.full((SUB, 128), loss, F32), jnp.sum(dy * hhat, axis=0, keepdims=True))

    dh3, loss_acc, dg_final = _ew("loss_head", (ni,), [(h3, rowblk, rowmap), (tgt, rowblk, rowmap), (g_final, *vec1k)],
                                  [((s, D_MODEL), F32, rowblk, rowmap)], head_fn,
                                  acc_outs=[((SUB, 128), F32), ((1, D_MODEL), F32)])

    def ple_bwd_fn(pids, dh, g_, ppr):
        sg = _sig(g_.astype(F32))
        return (dh * ppr.astype(F32) * sg * (1.0 - sg), dh * sg), ()

    dgl, dpp = _ew("ple_bwd", (ni,), [(dh3, rowblk, rowmap), (gl, rowblk, rowmap), (pp, rowblk, rowmap)],
                   [((s, D_MODEL), BF16, rowblk, rowmap)] * 2, ple_bwd_fn)

    def wgrad(name, a, a_block, a_imap, b, b_block, b_imap, out_shape, out_block, out_imap, nj, acc_shape):
        return _mm(name, (1, nj, nk), [(a, a_block, a_imap, b, b_block, b_imap)], "tn",
                   [(out_shape, F32, out_block, out_imap)], acc_shape=acc_shape)[0]

    tk0 = lambda i, j, k: (k, 0)
    tkj = lambda i, j, k: (k, j)
    def wgrad_cols(name, a, width, dy_):
        def split(acc):
            return (jnp.stack([acc[:, q * 256:(q + 1) * 256] for q in range(N_CHIPS)], axis=0),), ()

        return _mm(name, (1, 1, nk), [(a, (ts, width), tk0, dy_, (ts, D_MODEL), tk0)], "tn",
                   [((N_CHIPS, width, 256), F32, (N_CHIPS, width, 256), lambda i, j, k: (0, 0, 0))], epilogue=split,
                   acc_shape=(width, D_MODEL))[0]

    d_w_pp = wgrad_cols("d_ple_proj", p, PLE_DIM, dpp)
    d_w_pg = wgrad("d_ple_gate", h2, (ts, D_MODEL), tk0, dgl, (ts, D_MODEL), tk0, (D_MODEL, D_MODEL),
                   (D_MODEL, D_MODEL), w3, 1, (D_MODEL, D_MODEL))

    (dh2,) = _mm("ple_gate_bwd", (nib, 1, 1), [(dgl, (tmb, D_MODEL), m3, w_pg2, (D_MODEL, D_MODEL), w3)], "nt",
                 [((s, D_MODEL), F32, (tmb, D_MODEL), m3)], epilogue=lambda acc, d_: ((acc + d_,), ()),
                 extras=[(dh3, (tmb, D_MODEL), m3)])

    def ffn_bwd_epi(acc, gt, u_):
        gt, u_ = gt.astype(F32), u_.astype(F32)
        sg = _sig(gt)
        return (acc * u_ * (sg * (1.0 + gt * (1.0 - sg))), acc * gt * sg), ()

    ffq_big = (None, tmb, D_FF_Q)
    dgate, dup = _mm("ffn_down_bwd", (nib, N_CHIPS, 1),
                     [(dh2, (tmb, D_MODEL), m3, w_fd, (None, D_FF_Q, D_MODEL), lambda i, j, k: (j, 0, 0))], "nt",
                     [((N_CHIPS, s, D_FF_Q), BF16, ffq_big, ffq_map)] * 2, epilogue=ffn_bwd_epi,
                     extras=[(gate, ffq_big, ffq_map), (up, ffq_big, ffq_map)])

    ffq_t = (None, ts, D_FF_Q)
    ffq_tmap = lambda i, j, k: (j, k, 0)
    blk_j = lambda i, j, k: (j, 0, 0)
    d_w_fd = wgrad("d_ffn_down", act, ffq_t, ffq_tmap, dh2, (ts, D_MODEL), tk0, (N_CHIPS, D_FF_Q, D_MODEL),
                   (None, D_FF_Q, D_MODEL), blk_j, N_CHIPS, (D_FF_Q, D_MODEL))
    d_w_fg = wgrad("d_ffn_gate", n2, (ts, D_MODEL), tk0, dgate, ffq_t, ffq_tmap, (N_CHIPS, D_MODEL, D_FF_Q),
                   (None, D_MODEL, D_FF_Q), blk_j, N_CHIPS, (D_MODEL, D_FF_Q))
    d_w_fu = wgrad("d_ffn_up", n2, (ts, D_MODEL), tk0, dup, ffq_t, ffq_tmap, (N_CHIPS, D_MODEL, D_FF_Q),
                   (None, D_MODEL, D_FF_Q), blk_j, N_CHIPS, (D_MODEL, D_FF_Q))

    def norm_bwd_epi(acc, h, d_res, g):
        dh, dg = _rms_bwd_tile(acc, h, g)
        return (d_res + dh,), (dg,)

    ffq_k = lambda i, j, k: (k, i, 0)
    blk_k = lambda i, j, k: (k, 0, 0)
    ffq_b = (None, tmb, D_FF_Q)
    dh1, dg_ffn = _mm("ffn_in_bwd", (nib, 1, N_CHIPS),
                      [(dgate, ffq_b, ffq_k, w_fg, (None, D_MODEL, D_FF_Q), blk_k),
                       (dup, ffq_b, ffq_k, w_fu, (None, D_MODEL, D_FF_Q), blk_k)], "nt",
                      [((s, D_MODEL), F32, (tmb, D_MODEL), m3)], epilogue=norm_bwd_epi,
                      extras=[(h1, (tmb, D_MODEL), m3), (dh2, (tmb, D_MODEL), m3), (g_ffn, (1, D_MODEL), w3)],
                      acc_outs=[((1, D_MODEL), F32)], acc_shape=(tmb, D_MODEL))

    (dmix,) = _mm("out_proj_bwd", (nib, 1, 1), [(dh1, (tmb, D_MODEL), m3, w_out2, (D_MODEL, D_MODEL), w3)], "nt",
                  [((s, D_MODEL), BF16, (tmb, D_MODEL), m3)])
    d_w_out = wgrad("d_out_proj", mix, (ts, D_MODEL), tk0, dh1, (ts, D_MODEL), tk0, (D_MODEL, D_MODEL),
                    (D_MODEL, D_MODEL), w3, 1, (D_MODEL, D_MODEL))

    def mix_bwd_fn(pids, dm, ga, gs, ad, a, b):
        dm, ga, gs, ad, a, b = (t.astype(F32) for t in (dm, ga, gs, ad, a, b))
        s_a, s_s, s_b = _sig(ga), _sig(gs), _sig(b)
        d_ssm = dm * s_s
        return (dm * ad * s_a * (1.0 - s_a), dm * (a * s_b) * s_s * (1.0 - s_s), dm * s_a, d_ssm * s_b,
                d_ssm * a * s_b * (1.0 - s_b)), ()

    dga, dgs, dattn_d, dpa, dpb = _ew(
        "gate_mix_bwd", (ni,),
        [(dmix, rowblk, rowmap), (z, *ga_blk), (z, *gs_blk), (attn_d, rowblk, rowmap), (pa, rowblk, rowmap),
         (pb, rowblk, rowmap)], [((s, D_MODEL), BF16, rowblk, rowmap)] * 5, mix_bwd_fn)

    d_w_ap = wgrad_cols("d_attn_proj", attn, GROUP_WIDTH, dattn_d)
    d_w_ga = wgrad_cols("d_glu_a", yg, GROUP_WIDTH, dpa)
    d_w_gb = wgrad_cols("d_glu_b", yg, GROUP_WIDTH, dpb)

    ik = lambda i, j, k: (i, k)

    def cols_bwd(dy_, w):
        return [(dy_, (tmb, 256), (lambda i, j, k, q=q: (i, q)), w, (None, GROUP_WIDTH, 256),
                 (lambda i, j, k, q=q: (q, 0, 0))) for q in range(N_CHIPS)]

    (dattn,) = _mm("attn_proj_bwd", (nib, 1, 1), cols_bwd(dattn_d, w_ap), "nt",
                   [((s, GROUP_WIDTH), BF16, (tmb, GROUP_WIDTH), m3)])

    (dys,) = _mm("glu_bwd", (nib, 1, 1), cols_bwd(dpa, w_ga) + cols_bwd(dpb, w_gb), "nt",
                 [((s, GROUP_WIDTH), F32, (tmb, GROUP_WIDTH), m3)],
                 epilogue=lambda acc, y_: ((acc * _gelu_grad(y_),), ()),
                 extras=[(ys, (tmb, GROUP_WIDTH), m3)])

    du, d_lr, d_li, d_bre, d_bim, d_cre, d_cim, d_dskip = _ssm_bwd(
        dys, z, h_re, h_im, b_re_m, b_im_m, c_re_m, c_im_m, lam_re_row, lam_im_row, d_skip_row, chunk)

    dq_parts, dk_parts, dv_parts = [], [], []
    for gi, dil in enumerate(GROUP_DILATIONS):
        q_g, k_g, v_g = views[gi]
        dq_g, dk_g, dv_g = _attn_bwd(q_g, k_g, v_g, _to_view(dattn, dil), _to_view(attn, dil), _to_view(lse, dil), dil)
        dq_parts.append(_from_view(dq_g, dil))
        dk_parts.append(_from_view(dk_g, dil))
        dv_parts.append(_from_view(dv_g, dil))
    dqk = _rope_qk_bwd(dq_parts + dk_parts, tabs, tm)
    dz = jnp.concatenate([dqk] + dv_parts + [du, dga, dgs], axis=1)

    kb = lambda i, j, k: (k // 2, 0, k % 2)
    grad_x, dg_mix = _mm("in_proj_bwd", (nib, 1, 8), [(dz, (tmb, half_in), ik, w_in, (None, D_MODEL, half_in), kb)], "nt",
                         [((s, D_MODEL), F32, (tmb, D_MODEL), m3)], epilogue=norm_bwd_epi,
                         extras=[(x, (tmb, D_MODEL), m3), (dh1, (tmb, D_MODEL), m3), (g_mix, (1, D_MODEL), w3)],
                         acc_outs=[((1, D_MODEL), F32)], acc_shape=(tmb, D_MODEL))
    d_w_in = wgrad("d_in_proj", n1, (ts, D_MODEL), tk0, dz, (ts, half_in), tkj, (N_CHIPS, D_MODEL, IN_WIDTH // N_CHIPS),
                   (None, D_MODEL, half_in), lambda i, j, k: (j // 2, 0, j % 2), 8, (D_MODEL, half_in))

    d_bbt_re = _block_diag_t(d_bre, SSM_GROUP, SSM_STATE)
    d_bbt_im = _block_diag_t(d_bim, SSM_GROUP, SSM_STATE)
    d_a_re, d_a_im, d_log_dt, d_bt_re, d_bt_im = _ssm_param_bwd(
        sm["a_re"], sm["a_im"], log_dt_col, bt_re, bt_im,
        d_lr.reshape(SSM_GROUPS, SSM_STATE), d_li.reshape(SSM_GROUPS, SSM_STATE), d_bbt_re, d_bbt_im)
    small = {
        "g_mix": dg_mix, "a_re": d_a_re, "a_im": d_a_im, "log_dt": d_log_dt,
        "b_re": jnp.transpose(d_bt_re, (0, 2, 1)), "b_im": jnp.transpose(d_bt_im, (0, 2, 1)),
        "c_re": jnp.transpose(_block_diag_t(d_cre, SSM_STATE, SSM_GROUP), (0, 2, 1)),
        "c_im": jnp.transpose(_block_diag_t(d_cim, SSM_STATE, SSM_GROUP), (0, 2, 1)),
        "d_skip": d_dskip, "g_ffn": dg_ffn, "g_final": dg_final,
    }
    big = {
        "w_in": d_w_in, "w_attn_proj": d_w_ap, "w_glu_a": d_w_ga, "w_glu_b": d_w_gb,
        "w_out": d_w_out.reshape(N_CHIPS, D_MODEL // N_CHIPS, D_MODEL), "w_ffn_gate": d_w_fg, "w_ffn_up": d_w_fu,
        "w_ffn_down": d_w_fd, "w_ple_gate": d_w_pg.reshape(N_CHIPS, D_MODEL // N_CHIPS, D_MODEL), "w_ple_proj": d_w_pp,
    }
    return loss_acc[0, 0], grad_x, big, small


BIG = ("w_in", "w_attn_proj", "w_glu_a", "w_glu_b", "w_out", "w_ffn_gate", "w_ffn_up", "w_ffn_down", "w_ple_gate",
       "w_ple_proj")
SMALL = ("g_mix", "a_re", "a_im", "log_dt", "b_re", "b_im", "c_re", "c_im", "d_skip", "g_ffn", "g_final")
ANY = pl.BlockSpec(memory_space=pl.ANY)


def _place():
    x, y, c = lax.axis_index("x"), lax.axis_index("y"), lax.axis_index("c")
    chips = [(1 - x, y), (x, 1 - y), (1 - x, 1 - y)]
    return x, y, c, chips


def _remote(src, dst, send_sem, recv_sem, to):
    return pltpu.make_async_remote_copy(src_ref=src, dst_ref=dst, send_sem=send_sem, recv_sem=recv_sem, device_id=to,
                                        device_id_type=MESH)


def _comm_call(name, body, ins, out_shapes, n_sems, aliases=None):
    n_w = len(ins)
    return pl.pallas_call(
        body, name=name, in_specs=[ANY] * n_w, out_specs=[ANY] * len(out_shapes), out_shape=out_shapes,
        scratch_shapes=[pltpu.SemaphoreType.DMA((n,)) for n in n_sems], input_output_aliases=aliases or {},
    )(*ins)


def _gather_weights(bufs):
    n_w = len(bufs)

    def body(*refs):
        outs = refs[n_w:2 * n_w]
        ici_send, ici_recv, d2d_send, d2d_recv = refs[2 * n_w:]
        x, y, c, chips = _place()
        me = 2 * x + y
        sib = (x, y, 1 - c)
        sends = []
        for w in range(n_w):
            for j, (cx, cy) in enumerate(chips):
                k = 3 * w + j
                mine = outs[w].at[me, c]
                cp = _remote(mine, mine, ici_send.at[k], ici_recv.at[k], (cx, cy, c))
                cp.start()
                sends.append(cp)
        for w in range(n_w):
            for j, (cx, cy) in enumerate(chips):
                k = 3 * w + j
                src_chip = 2 * cx + cy
                landed = outs[w].at[src_chip, c]
                _remote(landed, landed, ici_send.at[k], ici_recv.at[k], (cx, cy, c)).wait_recv()
                fwd = _remote(landed, landed, d2d_send.at[k], d2d_recv.at[k], sib)
                fwd.start()
                sends.append(fwd)
        for w in range(n_w):
            for j, (cx, cy) in enumerate(chips):
                k = 3 * w + j
                other = outs[w].at[2 * cx + cy, 1 - c]
                _remote(other, other, d2d_send.at[k], d2d_recv.at[k], sib).wait_recv()
        for cp in sends:
            cp.wait_send()

    out_shapes = [jax.ShapeDtypeStruct(b.shape, b.dtype) for b in bufs]
    return _comm_call("gather_weights", body, bufs, out_shapes, [3 * n_w] * 4, aliases={w: w for w in range(n_w)})


def _pair_exchange(grads):
    n_w = len(grads)

    def body(*refs):
        ins, outs = refs[:n_w], refs[n_w:2 * n_w]
        send, recv = refs[2 * n_w:]
        x, y, c, _ = _place()
        sib = (x, y, 1 - c)
        cps = []
        for w in range(n_w):
            for q in range(N_CHIPS):
                k = N_CHIPS * w + q
                cp = _remote(ins[w].at[q, 1 - c], outs[w].at[q], send.at[k], recv.at[k], sib)
                cp.start()
                cps.append(cp)
        for cp in cps:
            cp.wait()

    out_shapes = [jax.ShapeDtypeStruct((N_CHIPS,) + g.shape[2:], g.dtype) for g in grads]
    return _comm_call("grad_pair_exchange", body, grads, out_shapes, [N_CHIPS * n_w] * 2)


def _chip_exchange(parts):
    n_w = len(parts)

    def body(*refs):
        ins, outs = refs[:n_w], refs[n_w:2 * n_w]
        send, recv = refs[2 * n_w:]
        x, y, c, chips = _place()
        me = 2 * x + y
        cps = []
        for w in range(n_w):
            for j, (cx, cy) in enumerate(chips):
                k = 3 * w + j
                cp = _remote(ins[w].at[2 * cx + cy], outs[w].at[me], send.at[k], recv.at[k], (cx, cy, c))
                cp.start()
                cps.append(cp)
        for w in range(n_w):
            for j, (cx, cy) in enumerate(chips):
                k = 3 * w + j
                got = outs[w].at[2 * cx + cy]
                _remote(got, got, send.at[k], recv.at[k], (cx, cy, c)).wait_recv()
        for cp in cps:
            cp.wait_send()

    out_shapes = [jax.ShapeDtypeStruct(t.shape, t.dtype) for t in parts]
    return _comm_call("grad_chip_exchange", body, parts, out_shapes, [3 * n_w, 3 * n_w])


def _pair_gather(halves):
    n_w = len(halves)

    def body(*refs):
        ins, outs = refs[:n_w], refs[n_w:2 * n_w]
        send, recv = refs[2 * n_w:]
        x, y, c, _ = _place()
        sib = (x, y, 1 - c)
        cps = []
        for w in range(n_w):
            cp = _remote(ins[w], outs[w], send.at[w], recv.at[w], sib)
            cp.start()
            cps.append(cp)
        for cp in cps:
            cp.wait()

    out_shapes = [jax.ShapeDtypeStruct(h.shape, h.dtype) for h in halves]
    return _comm_call("grad_pair_gather", body, halves, out_shapes, [n_w] * 2)


def _all_exchange(vec):
    def body(in_ref, out_ref, send, recv):
        x, y, c, _ = _place()
        me = 4 * x + 2 * y + c
        cps = []
        for k in range(1, 8):
            fx, fy, fc = (k >> 2) & 1, (k >> 1) & 1, k & 1
            to = (x ^ fx, y ^ fy, c ^ fc)
            cp = _remote(in_ref, out_ref.at[me], send.at[k - 1], recv.at[k - 1], to)
            cp.start()
            cps.append(cp)
        for k in range(1, 8):
            fx, fy, fc = (k >> 2) & 1, (k >> 1) & 1, k & 1
            src = 4 * (x ^ fx) + 2 * (y ^ fy) + (c ^ fc)
            got = out_ref.at[src]
            _remote(got, got, send.at[k - 1], recv.at[k - 1], (x ^ fx, y ^ fy, c ^ fc)).wait_recv()
        for cp in cps:
            cp.wait_send()

    return _comm_call("small_all_exchange", body, [vec], [jax.ShapeDtypeStruct((8,) + vec.shape, vec.dtype)], [7, 7])[0]


def _row_tile(r):
    for t in (256, 128, 176, 64, 32, 16, 8):
        if r % t == 0:
            return t
    return r


P_C, P_CHIP, P_DEV = 2, 3, 4


def _cast_into_slot(w2, place):
    r, c = w2.shape
    t = _row_tile(r)
    return _ew("cast_shard", (r // t,), [(w2, (t, c), lambda i, pv: (i, 0))],
               [((N_CHIPS, r, c), BF16, (None, t, c), lambda i, pv: (pv[P_CHIP], i, 0))],
               lambda pids, a: ((a,), ()), place=place)[0]


def _pair_sum(mine, theirs, place):
    _, r, c = theirs.shape
    t = _row_tile(r)
    own = ((None, None, t, c), lambda q, i, pv: (q, pv[P_C], i, 0))
    blk = ((None, t, c), lambda q, i, pv: (q, i, 0))
    return _ew("grad_pair_sum", (N_CHIPS, r // t), [(mine, *own), (theirs, *blk)], [((N_CHIPS, r, c), BF16, *blk)],
               lambda pids, a, b: ((a + b,), ()), place=place)[0]


def _chip_sum(own, got, place):
    _, r, c = own.shape
    t = _row_tile(r)
    ins = []
    for q in range(N_CHIPS):
        ins.append((own, (None, t, c), (lambda i, pv, q=q: (q, i, 0))))
        ins.append((got, (None, t, c), (lambda i, pv, q=q: (jnp.where(pv[P_CHIP] == q, (q + 1) % N_CHIPS, q), i, 0))))

    def fn(pids, *tiles):
        me = pids[0][P_CHIP]
        tot = None
        for q in range(N_CHIPS):
            term = jnp.where(me == q, tiles[2 * q], tiles[2 * q + 1]).astype(F32)
            tot = term if tot is None else tot + term
        return (tot,), ()

    return _ew("grad_chip_sum", (r // t,), ins, [((r, c), F32, (t, c), lambda i, pv: (i, 0))], fn, place=place)[0]


def _adamw_tile(w, g, m, v):
    m = ADAM_B1 * m + (1.0 - ADAM_B1) * g
    v = ADAM_B2 * v + (1.0 - ADAM_B2) * (g * g)
    m_hat = m / (1.0 - ADAM_B1 ** ADAM_STEP)
    v_hat = v / (1.0 - ADAM_B2 ** ADAM_STEP)
    delta = -ADAM_LR * (m_hat / (jnp.sqrt(v_hat) + ADAM_EPS) + ADAM_WD * w)
    return delta, m, v


def _adamw(name, g2, w2, m2, v2):
    r, c = w2.shape
    t = _row_tile(r)
    blk, imap = _rows(t, c)

    def fn(pids, g, w, m, v):
        delta, nm, nv = _adamw_tile(w, g, m, v)
        return (g, delta, nm, nv), ()

    return _ew(name, (r // t,), [(a, blk, imap) for a in (g2, w2, m2, v2)], [((r, c), F32, blk, imap)] * 4, fn)


def _adamw_halves(name, mine, theirs, w2, m2, v2, place):
    r, c = w2.shape
    t = _row_tile(r // 2)
    n_t = (r // 2) // t
    half = ((t, c), lambda h, i, pv: (i, 0))
    whole = ((t, c), lambda h, i, pv: (h * n_t + i, 0))

    def fn(pids, ga, gb, w, m, v):
        g = jnp.where(pids[1] == pids[0][P_C], ga, gb)
        delta, nm, nv = _adamw_tile(w, g, m, v)
        return (g, delta, nm, nv), ()

    return _ew(name, (2, n_t), [(mine, *half), (theirs, *half), (w2, *whole), (m2, *whole), (v2, *whole)],
               [((r, c), F32, *whole)] * 4, fn, place=place)


def _device_sum(own, got, place):
    r, c = own.shape
    t = _row_tile(r)
    ins = [(own, (t, c), lambda i, pv: (i, 0))]
    for q in range(8):
        ins.append((got, (None, t, c), (lambda i, pv, q=q: (jnp.where(pv[P_DEV] == q, (q + 1) % 8, q), i, 0))))

    def fn(pids, mine, *parts):
        me = pids[0][P_DEV]
        tot = None
        for q in range(8):
            term = jnp.where(me == q, mine, parts[q])
            tot = term if tot is None else tot + term
        return (tot,), ()

    return _ew("small_device_sum", (r // t,), ins, [((r, c), F32, (t, c), lambda i, pv: (i, 0))], fn, place=place)[0]


def _pack(parts):
    flat = jnp.concatenate([a.reshape(-1) for a in parts])
    pad = (-flat.shape[0]) % (SUB * 128)
    return jnp.pad(flat, (0, pad)).reshape(-1, 128)


def _unpack(mat, shapes):
    flat = mat.reshape(-1)
    out, off = [], 0
    for shp in shapes:
        n = math.prod(shp)
        out.append(flat[off:off + n].reshape(shp))
        off += n
    return out


def kernel(x, p, positions, g_mix, w_in, a_re, a_im, log_dt, b_re, b_im, c_re, c_im, d_skip, w_attn_proj, w_glu_a, w_glu_b, w_out, g_ffn, w_ffn_gate, w_ffn_up, w_ffn_down, w_ple_gate, w_ple_proj, g_final, loss_target, m_g_mix, m_w_in, m_a_re, m_a_im, m_log_dt, m_b_re, m_b_im, m_c_re, m_c_im, m_d_skip, m_w_attn_proj, m_w_glu_a, m_w_glu_b, m_w_out, m_g_ffn, m_w_ffn_gate, m_w_ffn_up, m_w_ffn_down, m_w_ple_gate, m_w_ple_proj, m_g_final, v_g_mix, v_w_in, v_a_re, v_a_im, v_log_dt, v_b_re, v_b_im, v_c_re, v_c_im, v_d_skip, v_w_attn_proj, v_w_glu_a, v_w_glu_b, v_w_out, v_g_ffn, v_w_ffn_gate, v_w_ffn_up, v_w_ffn_down, v_w_ple_gate, v_w_ple_proj, v_g_final):
    given = dict(locals())
    big_w = {n: given[n] for n in BIG}
    w_mats = {n: big_w[n].reshape(big_w[n].shape[1:]) for n in BIG}

    ax, ay, ac = lax.axis_index("x"), lax.axis_index("y"), lax.axis_index("c")
    place = jnp.stack([ax, ay, ac, 2 * ax + ay, 4 * ax + 2 * ay + ac]).astype(jnp.int32)

    bufs = []
    for n in BIG:
        r, c = w_mats[n].shape
        bufs.append(_cast_into_slot(w_mats[n], place).reshape(N_CHIPS, 2, r // 2, c))
    gathered = _gather_weights(bufs)
    wts = {}
    for n, g in zip(BIG, gathered):
        r, c = w_mats[n].shape
        wts[n] = g.reshape(N_CHIPS, r, c)

    sm = {
        "g_mix": g_mix.reshape(1, D_MODEL), "g_ffn": g_ffn.reshape(1, D_MODEL), "g_final": g_final.reshape(1, D_MODEL),
        "a_re": a_re[0], "a_im": a_im[0], "log_dt": log_dt[0], "b_re": b_re[0], "b_im": b_im[0], "c_re": c_re[0],
        "c_im": c_im[0], "d_skip": d_skip[0],
    }
    s = x.shape[1]
    loss_part, grad_x, big_g, small_g = _local_step(x[0], p[0, 0], positions[0], loss_target[0], sm, wts)

    g5 = []
    for n in BIG:
        r, c = w_mats[n].shape
        g5.append(big_g[n].reshape(N_CHIPS, 2, r // 2, c))
    theirs = _pair_exchange(g5)
    chip_parts = [_pair_sum(g, t, place) for g, t in zip(g5, theirs)]
    chip_got = _chip_exchange(chip_parts)
    halves = [_chip_sum(own, got, place) for own, got in zip(chip_parts, chip_got)]
    other_halves = _pair_gather(halves)

    results = {}
    for n, mine, other in zip(BIG, halves, other_halves):
        r, c = w_mats[n].shape
        shp = big_w[n].shape
        outs = _adamw_halves("adamw_" + n, mine, other, w_mats[n], given["m_" + n].reshape(r, c),
                             given["v_" + n].reshape(r, c), place)
        results[n] = [o.reshape(shp) for o in outs]

    small_shapes = [given[n].shape for n in SMALL]
    vec = _pack([small_g[n] for n in SMALL] + [loss_part.reshape(1)])
    tot = _device_sum(vec, _all_exchange(vec), place)
    n_small = sum(math.prod(shp) for shp in small_shapes)
    loss = tot.reshape(-1)[n_small]
    w_s = _pack([given[n] for n in SMALL])
    m_s = _pack([given["m_" + n] for n in SMALL])
    v_s = _pack([given["v_" + n] for n in SMALL])
    rows_s = w_s.shape[0]
    g_s = tot.reshape(-1)[: rows_s * 128].reshape(rows_s, 128)
    outs_s = _adamw("adamw_small", g_s, w_s, m_s, v_s)
    for kind, mat in enumerate(outs_s):
        for n, arr in zip(SMALL, _unpack(mat, small_shapes)):
            results.setdefault(n, [None] * 4)[kind] = arr

    order = ("g_mix", "w_in", "a_re", "a_im", "log_dt", "b_re", "b_im", "c_re", "c_im", "d_skip", "w_attn_proj", "w_glu_a",
             "w_glu_b", "w_out", "g_ffn", "w_ffn_gate", "w_ffn_up", "w_ffn_down", "w_ple_gate", "w_ple_proj", "g_final")
    out = [loss, grad_x.reshape(1, s, D_MODEL)]
    for kind in range(4):
        out += [results[n][kind] for n in order]
    return tuple(out)
```

```python
import math

import jax
import jax.numpy as jnp
from jax import lax
from jax.experimental import pallas as pl
from jax.experimental.pallas import tpu as pltpu

F32 = jnp.float32
BF16 = jnp.bfloat16

D_MODEL = 1024
HEAD_DIM = 128
HEADS_PER_GROUP = 4
GROUP_WIDTH = HEADS_PER_GROUP * HEAD_DIM
GROUP_DILATIONS = (1, 4, 16)
N_GROUPS = len(GROUP_DILATIONS)
ATTN_BLOCK = 128
ROPE_DIM = 32
ROPE_HALF = 16
ROPE_THETA = 500000.0
SSM_WIDTH = 512
SSM_GROUPS = 32
SSM_GROUP = 16
SSM_STATE = 64
N_STATE = SSM_GROUPS * SSM_STATE
SSM_SUPER = 4
IN_WIDTH = 7168
COL_U = 4608
COL_GA = 5120
COL_GS = 6144
D_FF = 2816
N_CHIPS = 4
D_FF_Q = D_FF // N_CHIPS
PLE_DIM = 256
EPS = 1e-6
ADAM_LR = 0.001
ADAM_B1 = 0.9
ADAM_B2 = 0.999
ADAM_EPS = 1e-08
ADAM_WD = 0.01
ADAM_STEP = 10
NEG_BIG = -1e30
VMEM_LIMIT_BYTES = 56 * 1024 * 1024
MESH = pl.DeviceIdType.MESH

_DIMS = {
    "nn": (((1,), (0,)), ((), ())),
    "nt": (((1,), (1,)), ((), ())),
    "tn": (((0,), (0,)), ((), ())),
}


def _params(n_grid):
    return pltpu.CompilerParams(dimension_semantics=("arbitrary",) * n_grid, vmem_limit_bytes=VMEM_LIMIT_BYTES)


def _sig(v):
    return 1.0 / (1.0 + jnp.exp(-v))


def _dot(a, b, mode):
    return lax.dot_general(a, b, _DIMS[mode], preferred_element_type=F32)


def _mm(name, grid, pairs, mode, outs, epilogue=None, extras=(), acc_outs=(), acc_shape=None, j_outer=False,
        sum_pairs=True):
    gi, gj, gk = grid
    n_p, n_e, n_o, n_a = len(pairs), len(extras), len(outs), len(acc_outs)
    assert not n_a or gj == 1
    assert sum_pairs or gk == 1

    def order(imap):
        return (lambda j, i, k: imap(i, j, k)) if j_outer else imap

    shared_a = [pr[0] is None for pr in pairs]
    n_in = 2 * n_p - sum(shared_a)

    def body(*refs):
        pair_refs = list(refs[:n_in])
        extra_refs = refs[n_in: n_in + n_e]
        out_refs = refs[n_in + n_e: n_in + n_e + n_o]
        sum_refs = refs[n_in + n_e + n_o: n_in + n_e + n_o + n_a]
        i = pl.program_id(1 if j_outer else 0)
        k = pl.program_id(2)
        part = None if sum_pairs else []
        a = None
        for t in range(n_p):
            if not shared_a[t]:
                a = pair_refs.pop(0)[...].astype(BF16)
            b = pair_refs.pop(0)[...].astype(BF16)
            d = _dot(a, b, mode)
            if sum_pairs:
                part = d if part is None else part + d
            else:
                part.append(d)

        def finish(acc):
            tiles, sums = epilogue(acc, *[e[...] for e in extra_refs]) if epilogue is not None else ((acc,), ())
            for o_ref, tile in zip(out_refs, tiles):
                o_ref[...] = tile.astype(o_ref.dtype)
            if n_a:
                @pl.when(i == 0)
                def _():
                    for s_ref in sum_refs:
                        s_ref[...] = jnp.zeros_like(s_ref)

                for s_ref, s in zip(sum_refs, sums):
                    s_ref[...] += s

        if gk == 1:
            finish(part)
        else:
            acc_ref = refs[-1]

            @pl.when(k == 0)
            def _():
                acc_ref[...] = part

            @pl.when(k > 0)
            def _():
                acc_ref[...] += part

            @pl.when(k == gk - 1)
            def _():
                finish(acc_ref[...])

    in_specs, args = [], []
    for a, a_block, a_imap, b, b_block, b_imap in pairs:
        if a is not None:
            in_specs.append(pl.BlockSpec(a_block, order(a_imap)))
            args.append(a)
        in_specs.append(pl.BlockSpec(b_block, order(b_imap)))
        args.append(b)
    for e, e_block, e_imap in extras:
        in_specs.append(pl.BlockSpec(e_block, order(e_imap)))
        args.append(e)
    out_shape = [jax.ShapeDtypeStruct(shape, dtype) for shape, dtype, _, _ in outs]
    out_specs = [pl.BlockSpec(block, order(imap)) for _, _, block, imap in outs]
    for shape, dtype in acc_outs:
        out_shape.append(jax.ShapeDtypeStruct(shape, dtype))
        out_specs.append(pl.BlockSpec(shape, lambda i, j, k: (0, 0)))
    scratch = [pltpu.VMEM(acc_shape, F32)] if gk > 1 else []
    return pl.pallas_call(
        body, name=name, grid=(gj, gi, gk) if j_outer else grid, in_specs=in_specs, out_specs=out_specs,
        out_shape=out_shape, scratch_shapes=scratch, compiler_params=_params(3),
    )(*args)


def _ew(name, grid, ins, outs, fn, acc_outs=(), place=None):
    n_i, n_o, n_a = len(ins), len(outs), len(acc_outs)
    ng = len(grid)
    n_s = 0 if place is None else 1

    def body(*refs):
        in_refs = refs[n_s: n_s + n_i]
        out_refs = refs[n_s + n_i: n_s + n_i + n_o]
        sum_refs = refs[n_s + n_i + n_o:]
        pids = tuple(pl.program_id(a) for a in range(ng))
        if n_s:
            pids = (refs[0],) + pids
        tiles, sums = fn(pids, *[r[...] for r in in_refs])
        for o_ref, tile in zip(out_refs, tiles):
            o_ref[...] = tile.astype(o_ref.dtype)
        if n_a:
            first = pids[0] == 0
            for p_ in pids[1:]:
                first = jnp.logical_and(first, p_ == 0)

            @pl.when(first)
            def _():
                for s_ref in sum_refs:
                    s_ref[...] = jnp.zeros_like(s_ref)

            for s_ref, s in zip(sum_refs, sums):
                s_ref[...] += s

    in_specs = [pl.BlockSpec(block, imap) for _, block, imap in ins]
    out_shape = [jax.ShapeDtypeStruct(shape, dtype) for shape, dtype, _, _ in outs]
    out_specs = [pl.BlockSpec(block, imap) for _, _, block, imap in outs]
    for shape, dtype in acc_outs:
        out_shape.append(jax.ShapeDtypeStruct(shape, dtype))
        out_specs.append(pl.BlockSpec(shape, lambda *_, nd=len(shape): (0,) * nd))
    arrays = [a for a, _, _ in ins]
    if n_s:
        assert not n_a
        spec = pltpu.PrefetchScalarGridSpec(num_scalar_prefetch=1, grid=grid, in_specs=in_specs, out_specs=out_specs)
        return pl.pallas_call(body, name=name, grid_spec=spec, out_shape=out_shape, compiler_params=_params(ng))(
            place, *arrays)
    return pl.pallas_call(
        body, name=name, grid=grid, in_specs=in_specs, out_specs=out_specs, out_shape=out_shape,
        compiler_params=_params(ng),
    )(*arrays)


def _rows(tm, width):
    return (tm, width), (lambda i: (i, 0))


def _rms_fwd_tile(h, g):
    r = lax.rsqrt(jnp.mean(h * h, axis=-1, keepdims=True) + EPS)
    return h * r * g


def _rms_bwd_tile(dn, h, g):
    r = lax.rsqrt(jnp.mean(h * h, axis=-1, keepdims=True) + EPS)
    hhat = h * r
    gy = dn * g
    dh = r * (gy - hhat * jnp.mean(gy * hhat, axis=-1, keepdims=True))
    dg = jnp.sum(dn * hhat, axis=0, keepdims=True)
    return dh, dg


def _rope_tables(pos_col, inv_row, tm):
    s = pos_col.shape[0]

    def fn(pids, pos, inv):
        ang = pos * inv
        lane = lax.broadcasted_iota(jnp.int32, ang.shape, 1)
        cs = jnp.where(lane < ROPE_DIM, jnp.cos(ang), 1.0)
        sn = jnp.sin(ang)
        s_lo = jnp.where(lane < ROPE_HALF, -sn, 0.0)
        s_hi = jnp.where(jnp.logical_and(lane >= ROPE_HALF, lane < ROPE_DIM), sn, 0.0)
        return (cs, s_lo, s_hi), ()

    blk, imap = _rows(tm, 128)
    return _ew(
        "rope_tables", (s // tm,),
        [(pos_col, (tm, 1), lambda i: (i, 0)), (inv_row, (1, 128), lambda i: (0, 0))],
        [((s, 128), F32, blk, imap)] * 3, fn,
    )


def _rope(xh, cs, s_lo, s_hi):
    return xh * cs + pltpu.roll(xh, HEAD_DIM - ROPE_HALF, 1) * s_lo + pltpu.roll(xh, ROPE_HALF, 1) * s_hi


def _rope_t(gh, cs, s_lo, s_hi):
    return gh * cs + pltpu.roll(gh * s_lo, ROPE_HALF, 1) + pltpu.roll(gh * s_hi, HEAD_DIM - ROPE_HALF, 1)


def _attn_geometry(length):
    nb = length // ATTN_BLOCK
    gq = min(4, nb)
    assert nb % gq == 0
    return nb, gq, gq * ATTN_BLOCK, nb // gq


def _band_masks():
    qi = lax.broadcasted_iota(jnp.int32, (ATTN_BLOCK, ATTN_BLOCK), 0)
    kj = lax.broadcasted_iota(jnp.int32, (ATTN_BLOCK, ATTN_BLOCK), 1)
    return kj <= qi, kj >= qi


def _band_mask_pair():
    qi = lax.broadcasted_iota(jnp.int32, (ATTN_BLOCK, 2 * ATTN_BLOCK), 0)
    cj = lax.broadcasted_iota(jnp.int32, (ATTN_BLOCK, 2 * ATTN_BLOCK), 1)
    in_cur = cj >= ATTN_BLOCK
    band = jnp.logical_or(jnp.logical_and(in_cur, cj - ATTN_BLOCK <= qi),
                          jnp.logical_and(cj < ATTN_BLOCK, cj >= qi))
    return band, in_cur


def _attn_fwd(qv, kv, vv, dil, cols3=(0, 0, 0)):
    length = qv.shape[0]
    nb, gq, rows, ni = _attn_geometry(length)
    out_shape = (length, dil * GROUP_WIDTH)

    def body(q_ref, kc_ref, kp_ref, vc_ref, vp_ref, o_ref, l_ref):
        i = pl.program_id(1)
        band, in_cur = _band_mask_pair()
        band_first = jnp.logical_and(band, jnp.logical_or(in_cur, i > 0))
        for h in range(HEADS_PER_GROUP):
            cols = slice(h * HEAD_DIM, (h + 1) * HEAD_DIM)
            qh = q_ref[:, cols]
            k_all = jnp.concatenate([kp_ref[:, cols], kc_ref[:, cols]], axis=0)
            v_all = jnp.concatenate([vp_ref[:, cols], vc_ref[:, cols]], axis=0)
            for jj in range(gq):
                rws = slice(jj * ATTN_BLOCK, (jj + 1) * ATTN_BLOCK)
                two = slice(jj * ATTN_BLOCK, (jj + 2) * ATTN_BLOCK)
                s = jnp.where(band_first if jj == 0 else band, _dot(qh[rws], k_all[two], "nt"), NEG_BIG)
                m = jnp.max(s, axis=-1, keepdims=True)
                pexp = jnp.exp(s - m)
                den = jnp.sum(pexp, axis=-1, keepdims=True)
                o = _dot(pexp.astype(BF16), v_all[two], "nn")
                o_ref[rws, cols] = (o / den).astype(o_ref.dtype)
                l_ref[rws, cols] = jnp.broadcast_to(m + jnp.log(den), (ATTN_BLOCK, HEAD_DIM))

    def cur(c):
        return pl.BlockSpec((rows, GROUP_WIDTH), lambda r, i: (i, r + c))

    def prev(c):
        return pl.BlockSpec((ATTN_BLOCK, GROUP_WIDTH), lambda r, i: (jnp.maximum(i * gq - 1, 0), r + c))

    cq, ck, cv = cols3
    return pl.pallas_call(
        body, name=f"attn_fwd_d{dil}", grid=(dil, ni),
        in_specs=[cur(cq), cur(ck), prev(ck), cur(cv), prev(cv)],
        out_specs=[cur(0), cur(0)],
        out_shape=[jax.ShapeDtypeStruct(out_shape, BF16), jax.ShapeDtypeStruct(out_shape, F32)],
        compiler_params=_params(2),
    )(qv, kv, kv, vv, vv)


def _attn_bwd(qv, kv, vv, dov, ov, lv, dil, cols3=(0, 0, 0)):
    length = qv.shape[0]
    nb, gq, rows, ni = _attn_geometry(length)
    out_shape = (length, dil * GROUP_WIDTH)

    def body(qc_ref, qn_ref, kc_ref, kp_ref, vc_ref, vp_ref, doc_ref, don_ref, oc_ref, on_ref, lc_ref, ln_ref,
             dq_ref, dk_ref, dv_ref):
        i = pl.program_id(1)
        _, mask_p = _band_masks()
        band, in_cur = _band_mask_pair()
        band_first = jnp.logical_and(band, jnp.logical_or(in_cur, i > 0))
        has_next = i < ni - 1

        def tile(qb, kb, vb, dob, l_col, delta, mask):
            s = _dot(qb, kb, "nt")
            p = jnp.where(mask, jnp.exp(s - l_col), 0.0)
            dp = _dot(dob, vb, "nt")
            ds = p * (dp - delta)
            return p, ds

        for h in range(HEADS_PER_GROUP):
            cols = slice(h * HEAD_DIM, (h + 1) * HEAD_DIM)
            q_c = qc_ref[:, cols]
            q_n = qn_ref[:, cols]
            k_all = jnp.concatenate([kp_ref[:, cols], kc_ref[:, cols]], axis=0)
            v_all = jnp.concatenate([vp_ref[:, cols], vc_ref[:, cols]], axis=0)
            do_c = doc_ref[:, cols]
            do_n = don_ref[:, cols]
            l_c = lc_ref[:, h * HEAD_DIM:h * HEAD_DIM + 1]
            l_n = ln_ref[:, h * HEAD_DIM:h * HEAD_DIM + 1]
            dl_c = jnp.sum(do_c.astype(F32) * oc_ref[:, cols].astype(F32), axis=-1, keepdims=True)
            dl_n = jnp.sum(do_n.astype(F32) * on_ref[:, cols].astype(F32), axis=-1, keepdims=True)
            dq_blocks, dk_blocks, dv_blocks = [], [None] * (gq + 1), [None] * (gq + 1)

            def add(lst, idx, val):
                lst[idx] = val if lst[idx] is None else lst[idx] + val

            for jj in range(gq):
                rws = slice(jj * ATTN_BLOCK, (jj + 1) * ATTN_BLOCK)
                two = slice(jj * ATTN_BLOCK, (jj + 2) * ATTN_BLOCK)
                qb, dob = q_c[rws], do_c[rws]
                p, ds = tile(qb, k_all[two], v_all[two], dob, l_c[rws], dl_c[rws], band_first if jj == 0 else band)
                dsb = ds.astype(BF16)
                dq_blocks.append(_dot(dsb, k_all[two], "nn"))
                dk2 = _dot(dsb, qb, "tn")
                dv2 = _dot(p.astype(BF16), dob, "tn")
                add(dk_blocks, jj, dk2[:ATTN_BLOCK])
                add(dk_blocks, jj + 1, dk2[ATTN_BLOCK:])
                add(dv_blocks, jj, dv2[:ATTN_BLOCK])
                add(dv_blocks, jj + 1, dv2[ATTN_BLOCK:])
            last = slice(gq * ATTN_BLOCK, (gq + 1) * ATTN_BLOCK)
            p, ds = tile(q_n, k_all[last], v_all[last], do_n, l_n, dl_n, jnp.logical_and(mask_p, has_next))
            add(dk_blocks, gq, _dot(ds.astype(BF16), q_n, "tn"))
            add(dv_blocks, gq, _dot(p.astype(BF16), do_n, "tn"))
            dk_blocks, dv_blocks = dk_blocks[1:], dv_blocks[1:]
            for jj in range(gq):
                rws = slice(jj * ATTN_BLOCK, (jj + 1) * ATTN_BLOCK)
                dq_ref[rws, cols] = dq_blocks[jj].astype(dq_ref.dtype)
                dk_ref[rws, cols] = dk_blocks[jj].astype(dk_ref.dtype)
                dv_ref[rws, cols] = dv_blocks[jj].astype(dv_ref.dtype)

    def cur(c):
        return pl.BlockSpec((rows, GROUP_WIDTH), lambda r, i: (i, r + c))

    def prev(c):
        return pl.BlockSpec((ATTN_BLOCK, GROUP_WIDTH), lambda r, i: (jnp.maximum(i * gq - 1, 0), r + c))

    def nxt(c):
        return pl.BlockSpec((ATTN_BLOCK, GROUP_WIDTH), lambda r, i: (jnp.minimum((i + 1) * gq, nb - 1), r + c))

    cq, ck, cv = cols3
    return pl.pallas_call(
        body, name=f"attn_bwd_d{dil}", grid=(dil, ni),
        in_specs=[cur(cq), nxt(cq), cur(ck), prev(ck), cur(cv), prev(cv), cur(0), nxt(0), cur(0), nxt(0), cur(0), nxt(0)],
        out_specs=[cur(0), cur(0), cur(0)],
        out_shape=[jax.ShapeDtypeStruct(out_shape, BF16)] * 3,
        compiler_params=_params(2),
    )(qv, qv, kv, kv, vv, vv, dov, dov, ov, ov, lv, lv)


DILATED = tuple((g, d) for g, d in enumerate(GROUP_DILATIONS) if d > 1)


def _spread(scr, slot, tile, out_ref, dil, col):
    tm = tile.shape[0]
    buf = scr.at[slot]
    buf[...] = tile
    for r in range(dil):
        c0 = r * GROUP_WIDTH + col
        out_ref[:, c0:c0 + HEAD_DIM] = buf[pl.ds(r, tm // dil, stride=dil), :].astype(out_ref.dtype)


def _collect(scr, slot, in_ref, dil, col):
    tm = scr.shape[1]
    buf = scr.at[slot]
    for r in range(dil):
        c0 = r * GROUP_WIDTH + col
        buf[pl.ds(r, tm // dil, stride=dil), :] = in_ref[:, c0:c0 + HEAD_DIM].astype(F32)
    return buf[...]


def _view_spec(tm, dil):
    return pl.BlockSpec((tm // dil, dil * GROUP_WIDTH), lambda i: (i, 0))


def _view_shape(s, dil, dtype):
    return jax.ShapeDtypeStruct((s // dil, dil * GROUP_WIDTH), dtype)


def _qkv_layout(z, tabs, tm):
    s = z.shape[0]
    scale = 1.0 / math.sqrt(HEAD_DIM)
    qkv_width = 3 * N_GROUPS * GROUP_WIDTH

    def body(z_ref, cs_ref, lo_ref, hi_ref, qk0_ref, *rest):
        views, scr = rest[:-1], rest[-1]
        tabs_ = (cs_ref[...], lo_ref[...], hi_ref[...])
        for part in range(3):
            for g, dil in enumerate(GROUP_DILATIONS):
                if part == 2 and dil == 1:
                    continue
                for h in range(HEADS_PER_GROUP):
                    col = part * N_GROUPS * GROUP_WIDTH + g * GROUP_WIDTH + h * HEAD_DIM
                    t = z_ref[:, col:col + HEAD_DIM].astype(F32)
                    if part < 2:
                        t = _rope(t, *tabs_)
                    if part == 0:
                        t = t * scale
                    if dil == 1:
                        c0 = part * GROUP_WIDTH + h * HEAD_DIM
                        qk0_ref[:, c0:c0 + HEAD_DIM] = t.astype(BF16)
                    else:
                        out = views[3 * [gg for gg, _ in DILATED].index(g) + part]
                        _spread(scr, h, t, out, dil, h * HEAD_DIM)

    row = lambda i: (i, 0)
    out_shape = [jax.ShapeDtypeStruct((s, 2 * GROUP_WIDTH), BF16)]
    out_specs = [pl.BlockSpec((tm, 2 * GROUP_WIDTH), row)]
    for _, dil in DILATED:
        out_shape += [_view_shape(s, dil, BF16)] * 3
        out_specs += [_view_spec(tm, dil)] * 3
    res = pl.pallas_call(
        body, name="qkv_layout", grid=(s // tm,),
        in_specs=[pl.BlockSpec((tm, qkv_width), row)] + [pl.BlockSpec((tm, HEAD_DIM), row)] * 3,
        out_specs=out_specs, out_shape=out_shape,
        scratch_shapes=[pltpu.VMEM((HEADS_PER_GROUP, tm, HEAD_DIM), F32)], compiler_params=_params(1),
    )(z, *tabs)
    return res[0], [tuple(res[1 + 3 * n:4 + 3 * n]) for n in range(len(DILATED))]


def _attn_merge(o0, l0, dilated, tm):
    s = o0.shape[0]
    n_d = len(DILATED)

    def body(*refs):
        o0_ref, l0_ref = refs[:2]
        in_views = refs[2:2 + 2 * n_d]
        attn_ref, lse_ref = refs[2 + 2 * n_d:4 + 2 * n_d]
        out_views = refs[4 + 2 * n_d:4 + 4 * n_d]
        scr = refs[-1]
        for h in range(HEADS_PER_GROUP):
            cols = slice(h * HEAD_DIM, (h + 1) * HEAD_DIM)
            os_ = [o0_ref[:, cols].astype(F32)]
            ls_ = [l0_ref[:, cols]]
            for n, (_, dil) in enumerate(DILATED):
                os_.append(_collect(scr, 2 * n, in_views[2 * n], dil, h * HEAD_DIM))
                ls_.append(_collect(scr, 2 * n + 1, in_views[2 * n + 1], dil, h * HEAD_DIM))
            m = ls_[0]
            for l_ in ls_[1:]:
                m = jnp.maximum(m, l_)
            es = [jnp.exp(l_ - m) for l_ in ls_]
            den = es[0]
            num = es[0] * os_[0]
            for e, o in zip(es[1:], os_[1:]):
                den = den + e
                num = num + e * o
            attn = num / den
            lse = m + jnp.log(den)
            attn_ref[:, cols] = attn.astype(BF16)
            lse_ref[:, cols] = lse
            for n, (_, dil) in enumerate(DILATED):
                _spread(scr, 2 * n_d, attn, out_views[2 * n], dil, h * HEAD_DIM)
                _spread(scr, 2 * n_d + 1, lse, out_views[2 * n + 1], dil, h * HEAD_DIM)

    row = lambda i: (i, 0)
    nat = pl.BlockSpec((tm, GROUP_WIDTH), row)
    in_specs = [nat, nat]
    args = [o0, l0]
    out_specs = [nat, nat]
    out_shape = [jax.ShapeDtypeStruct((s, GROUP_WIDTH), BF16), jax.ShapeDtypeStruct((s, GROUP_WIDTH), F32)]
    for (_, dil), (ov, lv) in zip(DILATED, dilated):
        in_specs += [_view_spec(tm, dil)] * 2
        args += [ov, lv]
        out_specs += [_view_spec(tm, dil)] * 2
        out_shape += [_view_shape(s, dil, BF16), _view_shape(s, dil, F32)]
    res = pl.pallas_call(
        body, name="attn_merge", grid=(s // tm,), in_specs=in_specs, out_specs=out_specs, out_shape=out_shape,
        scratch_shapes=[pltpu.VMEM((2 * n_d + 2, tm, HEAD_DIM), F32)], compiler_params=_params(1),
    )(*args)
    return res[0], res[1], [tuple(res[2 + 2 * n:4 + 2 * n]) for n in range(n_d)]


def _to_views(a, tm):
    s = a.shape[0]

    def body(a_ref, *rest):
        outs, scr = rest[:-1], rest[-1]
        for h in range(HEADS_PER_GROUP):
            t = a_ref[:, h * HEAD_DIM:(h + 1) * HEAD_DIM].astype(F32)
            for n, (_, dil) in enumerate(DILATED):
                _spread(scr, n, t, outs[n], dil, h * HEAD_DIM)

    return pl.pallas_call(
        body, name="to_views", grid=(s // tm,), in_specs=[pl.BlockSpec((tm, GROUP_WIDTH), lambda i: (i, 0))],
        out_specs=[_view_spec(tm, dil) for _, dil in DILATED], out_shape=[_view_shape(s, dil, BF16) for _, dil in DILATED],
        scratch_shapes=[pltpu.VMEM((len(DILATED), tm, HEAD_DIM), F32)], compiler_params=_params(1),
    )(a)


def _dz_layout(grads, du, dga, dgs, tabs, tm):
    s = du.shape[0]
    scale = 1.0 / math.sqrt(HEAD_DIM)

    def body(*refs):
        g_refs = refs[:3 * N_GROUPS]
        du_ref, dga_ref, dgs_ref, cs_ref, lo_ref, hi_ref, dz_ref, scr = refs[3 * N_GROUPS:]
        tabs_ = (cs_ref[...], lo_ref[...], hi_ref[...])
        for part in range(3):
            for g, dil in enumerate(GROUP_DILATIONS):
                src = g_refs[3 * g + part]
                for h in range(HEADS_PER_GROUP):
                    if dil == 1:
                        t = src[:, h * HEAD_DIM:(h + 1) * HEAD_DIM].astype(F32)
                    else:
                        t = _collect(scr, h, src, dil, h * HEAD_DIM)
                    if part < 2:
                        t = _rope_t(t, *tabs_)
                    if part == 0:
                        t = t * scale
                    col = part * N_GROUPS * GROUP_WIDTH + g * GROUP_WIDTH + h * HEAD_DIM
                    dz_ref[:, col:col + HEAD_DIM] = t.astype(BF16)
        dz_ref[:, COL_U:COL_GA] = du_ref[...]
        dz_ref[:, COL_GA:COL_GS] = dga_ref[...]
        dz_ref[:, COL_GS:IN_WIDTH] = dgs_ref[...]

    row = lambda i: (i, 0)
    in_specs, args = [], []
    for (g, dil), trio in zip(enumerate(GROUP_DILATIONS), grads):
        in_specs += [pl.BlockSpec((tm, GROUP_WIDTH), row) if dil == 1 else _view_spec(tm, dil)] * 3
        args += list(trio)
    in_specs += [pl.BlockSpec((tm, SSM_WIDTH), row), pl.BlockSpec((tm, D_MODEL), row), pl.BlockSpec((tm, D_MODEL), row)]
    in_specs += [pl.BlockSpec((tm, HEAD_DIM), row)] * 3
    return pl.pallas_call(
        body, name="dz_layout", grid=(s // tm,), in_specs=in_specs, out_specs=pl.BlockSpec((tm, IN_WIDTH), row),
        out_shape=jax.ShapeDtypeStruct((s, IN_WIDTH), BF16),
        scratch_shapes=[pltpu.VMEM((HEADS_PER_GROUP, tm, HEAD_DIM), F32)], compiler_params=_params(1),
    )(*args, du, dga, dgs, *tabs)


def _discretise(a_re, a_im, log_dt, bt_re, bt_im):
    dt = jnp.exp(log_dt)
    mag = jnp.exp(a_re * dt)
    bar_re = mag * jnp.cos(a_im * dt)
    bar_im = mag * jnp.sin(a_im * dt)
    nr = bar_re - 1.0
    ni = bar_im
    den = a_re * a_re + a_im * a_im
    z_re = (nr * a_re + ni * a_im) / den
    z_im = (ni * a_re - nr * a_im) / den
    bb_re = z_re[:, None, :] * bt_re - z_im[:, None, :] * bt_im
    bb_im = z_re[:, None, :] * bt_im + z_im[:, None, :] * bt_re
    return bar_re, bar_im, bb_re, bb_im


def _ssm_prep(a_re, a_im, log_dt, bt_re, bt_im):
    def body(ar, ai, ld, br, bi, o_lr, o_li, o_br, o_bi):
        lr, li, bbr, bbi = _discretise(ar[...], ai[...], ld[...], br[...], bi[...])
        o_lr[...] = lr
        o_li[...] = li
        o_br[...] = bbr
        o_bi[...] = bbi

    sm = jax.ShapeDtypeStruct((SSM_GROUPS, SSM_STATE), F32)
    bg = jax.ShapeDtypeStruct((SSM_GROUPS, SSM_GROUP, SSM_STATE), F32)
    return pl.pallas_call(body, name="ssm_prep", out_shape=[sm, sm, bg, bg])(a_re, a_im, log_dt, bt_re, bt_im)


def _ssm_param_bwd(a_re, a_im, log_dt, bt_re, bt_im, d_lr, d_li, d_bbr, d_bbi):
    def body(ar, ai, ld, br, bi, g_lr, g_li, g_br, g_bi, o_ar, o_ai, o_ld, o_br, o_bi):
        _, vjp = jax.vjp(_discretise, ar[...], ai[...], ld[...], br[...], bi[...])
        d_ar, d_ai, d_ld, d_br, d_bi = vjp((g_lr[...], g_li[...], g_br[...], g_bi[...]))
        o_ar[...] = d_ar
        o_ai[...] = d_ai
        o_ld[...] = d_ld
        o_br[...] = d_br
        o_bi[...] = d_bi

    sm = jax.ShapeDtypeStruct((SSM_GROUPS, SSM_STATE), F32)
    col = jax.ShapeDtypeStruct((SSM_GROUPS, 1), F32)
    bg = jax.ShapeDtypeStruct((SSM_GROUPS, SSM_GROUP, SSM_STATE), F32)
    return pl.pallas_call(body, name="ssm_param_bwd", out_shape=[sm, sm, col, bg, bg])(
        a_re, a_im, log_dt, bt_re, bt_im, d_lr, d_li, d_bbr, d_bbi)


def _block_diag(t, rows_per, cols_per):
    t4 = t.reshape(SSM_SUPER, 8, rows_per, cols_per)
    eye = jnp.eye(8, dtype=t.dtype)
    return jnp.einsum("bgrc,gh->bgrhc", t4, eye).reshape(SSM_SUPER, 8 * rows_per, 8 * cols_per)


def _block_diag_t(dense, rows_per, cols_per):
    t = dense.reshape(SSM_SUPER, 8, rows_per, 8, cols_per)
    eye = jnp.eye(8, dtype=dense.dtype)
    return jnp.einsum("bgrhc,gh->bgrc", t, eye).reshape(SSM_GROUPS, rows_per, cols_per)


def _gelu(v):
    c = math.sqrt(2.0 / math.pi)
    return 0.5 * v * (1.0 + jnp.tanh(c * (v + 0.044715 * v * v * v)))


def _gelu_grad(v):
    c = math.sqrt(2.0 / math.pi)
    t = jnp.tanh(c * (v + 0.044715 * v * v * v))
    return 0.5 * (1.0 + t) + 0.5 * v * (1.0 - t * t) * c * (1.0 + 3.0 * 0.044715 * v * v)


SUB = 8


SCAN_STEPS = (1, 2, 4)
N_SCAN_TABLES = 2 + 2 * len(SCAN_STEPS)


def _scan_tables(tab_ref, lam_re, lam_im, reverse, conj):
    lr = lam_re
    li = -lam_im if conj else lam_im
    powers = [(lr, li)]
    for _ in range(SUB - 1):
        pr, pi = powers[-1]
        powers.append((pr * lr - pi * li, pr * li + pi * lr))
    row = lax.broadcasted_iota(jnp.int32, (SUB, N_STATE), 0)
    if reverse:
        row = SUB - 1 - row
    wide = lambda v: jnp.broadcast_to(v, (SUB, N_STATE))
    p_re = jnp.zeros((SUB, N_STATE), F32)
    p_im = jnp.zeros((SUB, N_STATE), F32)
    for j in range(SUB):
        p_re = jnp.where(row == j, wide(powers[j][0]), p_re)
        p_im = jnp.where(row == j, wide(powers[j][1]), p_im)
    tab_ref[0] = p_re
    tab_ref[1] = p_im
    for idx, k in enumerate(SCAN_STEPS):
        tab_ref[2 + 2 * idx] = jnp.where(row >= k, wide(powers[k - 1][0]), 0.0)
        tab_ref[3 + 2 * idx] = jnp.where(row >= k, wide(powers[k - 1][1]), 0.0)


def _scan_rows(g_re_ref, g_im_ref, tab_ref, carry, n_rows, reverse):
    last = 0 if reverse else SUB - 1

    def tile_step(tt, state):
        cr, ci = state
        t8 = (n_rows // SUB - 1 - tt) if reverse else tt
        start = pl.multiple_of(t8 * SUB, SUB)
        xr = g_re_ref[pl.ds(start, SUB), :]
        xi = g_im_ref[pl.ds(start, SUB), :]
        for idx, k in enumerate(SCAN_STEPS):
            mr = tab_ref[2 + 2 * idx]
            mi = tab_ref[3 + 2 * idx]
            shift = SUB - k if reverse else k
            sr = pltpu.roll(xr, shift, 0)
            si = pltpu.roll(xi, shift, 0)
            xr, xi = xr + (mr * sr - mi * si), xi + (mr * si + mi * sr)
        pr = tab_ref[0]
        pi = tab_ref[1]
        xr, xi = xr + (pr * cr - pi * ci), xi + (pr * ci + pi * cr)
        g_re_ref[pl.ds(start, SUB), :] = xr
        g_im_ref[pl.ds(start, SUB), :] = xi
        return (jnp.broadcast_to(xr[last:last + 1, :], (SUB, N_STATE)),
                jnp.broadcast_to(xi[last:last + 1, :], (SUB, N_STATE)))

    return lax.fori_loop(0, n_rows // SUB, tile_step, carry)


def _ssm_fwd(z, b_re, b_im, c_re, c_im, lam_re, lam_im, d_skip, chunk):
    s = z.shape[0]

    def body(u_ref, bre, bim, cre, cim, lre, lim, dsk, hre_ref, him_ref, ys_ref, yg_ref, car_re, car_im, tabs):
        i = pl.program_id(0)

        @pl.when(i == 0)
        def _():
            car_re[...] = jnp.zeros_like(car_re)
            car_im[...] = jnp.zeros_like(car_im)
            _scan_tables(tabs, lre[...], lim[...], False, False)

        u = u_ref[...]
        for b in range(SSM_SUPER):
            ub = u[:, b * 128:(b + 1) * 128]
            st = slice(b * 512, (b + 1) * 512)
            hre_ref[:, st] = _dot(ub, bre[b], "nn")
            him_ref[:, st] = _dot(ub, bim[b], "nn")
        sr, si = _scan_rows(hre_ref, him_ref, tabs, (car_re[...], car_im[...]), chunk, False)
        car_re[...] = sr
        car_im[...] = si
        uf = u.astype(F32)
        for b in range(SSM_SUPER):
            st = slice(b * 512, (b + 1) * 512)
            ch = slice(b * 128, (b + 1) * 128)
            y = _dot(hre_ref[:, st].astype(BF16), cre[b], "nn") - _dot(him_ref[:, st].astype(BF16), cim[b], "nn")
            y = y + dsk[:, ch] * uf[:, ch]
            ys_ref[:, ch] = y
            yg_ref[:, ch] = _gelu(y).astype(BF16)

    full3 = lambda i: (0, 0, 0)
    full2 = lambda i: (0, 0)
    row = lambda i: (i, 0)
    u_col = COL_U // SSM_WIDTH
    return pl.pallas_call(
        body, name="ssm_fwd", grid=(s // chunk,),
        in_specs=[pl.BlockSpec((chunk, SSM_WIDTH), lambda i: (i, u_col)),
                  pl.BlockSpec((SSM_SUPER, 128, 512), full3), pl.BlockSpec((SSM_SUPER, 128, 512), full3),
                  pl.BlockSpec((SSM_SUPER, 512, 128), full3), pl.BlockSpec((SSM_SUPER, 512, 128), full3),
                  pl.BlockSpec((1, N_STATE), full2), pl.BlockSpec((1, N_STATE), full2), pl.BlockSpec((1, SSM_WIDTH), full2)],
        out_specs=[pl.BlockSpec((chunk, N_STATE), row), pl.BlockSpec((chunk, N_STATE), row),
                   pl.BlockSpec((chunk, SSM_WIDTH), row), pl.BlockSpec((chunk, SSM_WIDTH), row)],
        out_shape=[jax.ShapeDtypeStruct((s, N_STATE), F32), jax.ShapeDtypeStruct((s, N_STATE), F32),
                   jax.ShapeDtypeStruct((s, SSM_WIDTH), F32), jax.ShapeDtypeStruct((s, SSM_WIDTH), BF16)],
        scratch_shapes=[pltpu.VMEM((SUB, N_STATE), F32), pltpu.VMEM((SUB, N_STATE), F32),
                        pltpu.VMEM((N_SCAN_TABLES, SUB, N_STATE), F32)],
        compiler_params=_params(1),
    )(z, b_re, b_im, c_re, c_im, lam_re, lam_im, d_skip)


def _ssm_bwd(dys, z, h_re, h_im, b_re, b_im, c_re, c_im, lam_re, lam_im, d_skip, chunk):
    s = z.shape[0]
    n_chunks = s // chunk

    def body(dy_ref, u_ref, hre_ref, him_ref, hpr_ref, hpi_ref, bre, bim, cre, cim, lre, lim, dsk,
             du_ref, dlr_ref, dli_ref, dbr_ref, dbi_ref, dcr_ref, dci_ref, dd_ref, are, aim, car_re, car_im, tabs):
        i = pl.program_id(0)
        n = n_chunks - 1 - i

        @pl.when(i == 0)
        def _():
            car_re[...] = jnp.zeros_like(car_re)
            car_im[...] = jnp.zeros_like(car_im)
            _scan_tables(tabs, lre[...], lim[...], True, True)
            for r in (dlr_ref, dli_ref, dbr_ref, dbi_ref, dcr_ref, dci_ref, dd_ref):
                r[...] = jnp.zeros_like(r)

        dy = dy_ref[...]
        dyb = dy.astype(BF16)
        u = u_ref[...]
        for b in range(SSM_SUPER):
            ch = slice(b * 128, (b + 1) * 128)
            st = slice(b * 512, (b + 1) * 512)
            are[:, st] = _dot(dyb[:, ch], cre[b], "nt")
            aim[:, st] = -_dot(dyb[:, ch], cim[b], "nt")
        sr, si = _scan_rows(are, aim, tabs, (car_re[...], car_im[...]), chunk, True)
        car_re[...] = sr
        car_im[...] = si
        row_id = lax.broadcasted_iota(jnp.int32, (chunk, N_STATE), 0)
        top_scale = jnp.where(n > 0, 1.0, 0.0)
        h_r = hre_ref[...]
        h_i = him_ref[...]
        hp_r = jnp.where(row_id == 0, hpr_ref[SUB - 1:SUB, :] * top_scale, pltpu.roll(h_r, 1, 0))
        hp_i = jnp.where(row_id == 0, hpi_ref[SUB - 1:SUB, :] * top_scale, pltpu.roll(h_i, 1, 0))
        a_r = are[...]
        a_i = aim[...]
        dlr_ref[...] += jnp.sum(a_r * hp_r + a_i * hp_i, axis=0, keepdims=True)
        dli_ref[...] += jnp.sum(a_i * hp_r - a_r * hp_i, axis=0, keepdims=True)
        dd_ref[...] += jnp.sum(dy * u.astype(F32), axis=0, keepdims=True)
        a_rb = a_r.astype(BF16)
        a_ib = a_i.astype(BF16)
        h_rb = h_r.astype(BF16)
        h_ib = h_i.astype(BF16)
        for b in range(SSM_SUPER):
            ch = slice(b * 128, (b + 1) * 128)
            st = slice(b * 512, (b + 1) * 512)
            dbr_ref[b] += _dot(u[:, ch], a_rb[:, st], "tn")
            dbi_ref[b] += _dot(u[:, ch], a_ib[:, st], "tn")
            dcr_ref[b] += _dot(h_rb[:, st], dyb[:, ch], "tn")
            dci_ref[b] += -_dot(h_ib[:, st], dyb[:, ch], "tn")
            du = _dot(a_rb[:, st], bre[b], "nt") + _dot(a_ib[:, st], bim[b], "nt") + dsk[:, ch] * dy[:, ch]
            du_ref[:, ch] = du.astype(du_ref.dtype)

    full3 = lambda i: (0, 0, 0)
    full2 = lambda i: (0, 0)
    rev = lambda i: (n_chunks - 1 - i, 0)
    above = lambda i: (jnp.maximum((n_chunks - 1 - i) * (chunk // SUB) - 1, 0), 0)
    u_col = COL_U // SSM_WIDTH
    b_spec = pl.BlockSpec((SSM_SUPER, 128, 512), full3)
    c_spec = pl.BlockSpec((SSM_SUPER, 512, 128), full3)
    vec = pl.BlockSpec((1, N_STATE), full2)
    return pl.pallas_call(
        body, name="ssm_bwd", grid=(n_chunks,),
        in_specs=[pl.BlockSpec((chunk, SSM_WIDTH), rev),
                  pl.BlockSpec((chunk, SSM_WIDTH), lambda i: (n_chunks - 1 - i, u_col)),
                  pl.BlockSpec((chunk, N_STATE), rev), pl.BlockSpec((chunk, N_STATE), rev),
                  pl.BlockSpec((SUB, N_STATE), above), pl.BlockSpec((SUB, N_STATE), above),
                  b_spec, b_spec, c_spec, c_spec, vec, vec, pl.BlockSpec((1, SSM_WIDTH), full2)],
        out_specs=[pl.BlockSpec((chunk, SSM_WIDTH), rev), vec, vec, b_spec, b_spec, c_spec, c_spec,
                   pl.BlockSpec((1, SSM_WIDTH), full2)],
        out_shape=[jax.ShapeDtypeStruct((s, SSM_WIDTH), BF16),
                   jax.ShapeDtypeStruct((1, N_STATE), F32), jax.ShapeDtypeStruct((1, N_STATE), F32),
                   jax.ShapeDtypeStruct((SSM_SUPER, 128, 512), F32), jax.ShapeDtypeStruct((SSM_SUPER, 128, 512), F32),
                   jax.ShapeDtypeStruct((SSM_SUPER, 512, 128), F32), jax.ShapeDtypeStruct((SSM_SUPER, 512, 128), F32),
                   jax.ShapeDtypeStruct((1, SSM_WIDTH), F32)],
        scratch_shapes=[pltpu.VMEM((chunk, N_STATE), F32), pltpu.VMEM((chunk, N_STATE), F32),
                        pltpu.VMEM((SUB, N_STATE), F32), pltpu.VMEM((SUB, N_STATE), F32),
                        pltpu.VMEM((N_SCAN_TABLES, SUB, N_STATE), F32)],
        compiler_params=_params(1),
    )(dys, z, h_re, h_im, h_re, h_im, b_re, b_im, c_re, c_im, lam_re, lam_im, d_skip)


def _local_step(x, p, pos, tgt, sm, wts):
    s = x.shape[0]
    tm = min(512, s)
    ts = min(1024, s)
    chunk = min(256, s)
    ni = s // tm
    nk = s // ts
    w_in, w_ap, w_ga, w_gb, w_out, w_fg, w_fu, w_fd, w_pg, w_pp = (
        wts[k] for k in ("w_in", "w_attn_proj", "w_glu_a", "w_glu_b", "w_out", "w_ffn_gate", "w_ffn_up", "w_ffn_down",
                         "w_ple_gate", "w_ple_proj"))
    w_out2 = w_out.reshape(D_MODEL, D_MODEL)
    w_pg2 = w_pg.reshape(D_MODEL, D_MODEL)
    g_mix, g_ffn, g_final = sm["g_mix"], sm["g_ffn"], sm["g_final"]
    rowblk, rowmap = _rows(tm, D_MODEL)
    vec1k = ((1, D_MODEL), lambda *_: (0, 0))

    (n1,) = _ew("rms_mix", (ni,), [(x, rowblk, rowmap), (g_mix, *vec1k)], [((s, D_MODEL), BF16, rowblk, rowmap)],
                lambda pids, h, g: ((_rms_fwd_tile(h, g),), ()))

    half_in = IN_WIDTH // 8
    tmb = min(1024, s)
    nib = s // tmb
    (z,) = _mm("in_proj", (nib, 8, 1),
               [(n1, (tmb, D_MODEL), lambda i, j, k: (i, 0), w_in, (None, D_MODEL, half_in), lambda i, j, k: (j // 2, 0, j % 2))],
               "nn", [((s, IN_WIDTH), BF16, (tmb, half_in), lambda i, j, k: (i, j))], j_outer=True)

    inv = ROPE_THETA ** (-jnp.arange(ROPE_HALF, dtype=F32) * 2.0 / ROPE_DIM)
    inv_row = jnp.concatenate([inv, inv, jnp.zeros((HEAD_DIM - ROPE_DIM,), F32)]).reshape(1, HEAD_DIM)
    tabs = _rope_tables(pos.astype(F32).reshape(s, 1), inv_row, tm)

    qk0, qkv_views = _qkv_layout(z, tabs, tm)
    v0_col = (2 * N_GROUPS * GROUP_WIDTH) // GROUP_WIDTH
    group_in = [((qk0, qk0, z), (0, 1, v0_col))] + [(trio, (0, 0, 0)) for trio in qkv_views]
    fwd_out = [_attn_fwd(*arrs, dil, cols3) for (arrs, cols3), dil in zip(group_in, GROUP_DILATIONS)]
    attn, lse, merged_views = _attn_merge(fwd_out[0][0], fwd_out[0][1], fwd_out[1:], tm)

    def chip_cols(parts):
        return (jnp.concatenate(parts, axis=1),), ()

    def proj_cols(name, a, width, w):
        blk = (None, width, 256)
        pairs = [(a, (tmb, width), lambda i, j, k: (i, 0), w, blk, lambda i, j, k: (0, 0, 0))]
        pairs += [(None, None, None, w, blk, (lambda i, j, k, q=q: (q, 0, 0))) for q in range(1, N_CHIPS)]
        return _mm(name, (nib, 1, 1), pairs, "nn", [((s, D_MODEL), BF16, (tmb, D_MODEL), lambda i, j, k: (i, 0))],
                   epilogue=chip_cols, sum_pairs=False)[0]

    def proj512(name, a, w):
        return proj_cols(name, a, GROUP_WIDTH, w)

    attn_d = proj512("attn_proj", attn, w_ap)

    bt_re = jnp.transpose(sm["b_re"], (0, 2, 1))
    bt_im = jnp.transpose(sm["b_im"], (0, 2, 1))
    log_dt_col = sm["log_dt"].reshape(SSM_GROUPS, 1)
    lam_re, lam_im, bbt_re, bbt_im = _ssm_prep(sm["a_re"], sm["a_im"], log_dt_col, bt_re, bt_im)
    b_re_m = _block_diag(bbt_re, SSM_GROUP, SSM_STATE).astype(BF16)
    b_im_m = _block_diag(bbt_im, SSM_GROUP, SSM_STATE).astype(BF16)
    c_re_m = _block_diag(jnp.transpose(sm["c_re"], (0, 2, 1)), SSM_STATE, SSM_GROUP).astype(BF16)
    c_im_m = _block_diag(jnp.transpose(sm["c_im"], (0, 2, 1)), SSM_STATE, SSM_GROUP).astype(BF16)
    lam_re_row = lam_re.reshape(1, N_STATE)
    lam_im_row = lam_im.reshape(1, N_STATE)
    d_skip_row = sm["d_skip"].reshape(1, SSM_WIDTH)
    h_re, h_im, ys, yg = _ssm_fwd(z, b_re_m, b_im_m, c_re_m, c_im_m, lam_re_row, lam_im_row, d_skip_row, chunk)

    pa = proj512("glu_a", yg, w_ga)
    pb = proj512("glu_b", yg, w_gb)

    ga_blk = ((tm, D_MODEL), lambda i: (i, COL_GA // D_MODEL))
    gs_blk = ((tm, D_MODEL), lambda i: (i, COL_GS // D_MODEL))

    def mix_fn(pids, ga, gs, ad, a, b):
        ga, gs, ad, a, b = (t.astype(F32) for t in (ga, gs, ad, a, b))
        return (_sig(ga) * ad + _sig(gs) * (a * _sig(b)),), ()

    (mix,) = _ew("gate_mix", (ni,), [(z, *ga_blk), (z, *gs_blk), (attn_d, rowblk, rowmap), (pa, rowblk, rowmap),
                                     (pb, rowblk, rowmap)], [((s, D_MODEL), BF16, rowblk, rowmap)], mix_fn)

    def out_epi(acc, xr, g):
        h1 = acc + xr
        return (h1, _rms_fwd_tile(h1, g)), ()

    m3 = lambda i, j, k: (i, 0)
    w3 = lambda i, j, k: (0, 0)
    h1, n2 = _mm("out_proj", (nib, 1, 1), [(mix, (tmb, D_MODEL), m3, w_out2, (D_MODEL, D_MODEL), w3)], "nn",
                 [((s, D_MODEL), F32, (tmb, D_MODEL), m3), ((s, D_MODEL), BF16, (tmb, D_MODEL), m3)],
                 epilogue=out_epi, extras=[(x, (tmb, D_MODEL), m3), (g_ffn, (1, D_MODEL), w3)])

    ffq = (None, tm, D_FF_Q)
    ffq_map = lambda i, j, k: (j, i, 0)

    def ffn_in_epi(parts):
        gt, u_ = parts
        return (gt, u_, gt * _sig(gt) * u_), ()

    w_ffq = (None, D_MODEL, D_FF_Q)
    w_ffq_j = lambda i, j, k: (j, 0, 0)
    gate, up, act = _mm("ffn_gate_up", (ni, N_CHIPS, 1),
                        [(n2, (tm, D_MODEL), m3, w_fg, w_ffq, w_ffq_j), (None, None, None, w_fu, w_ffq, w_ffq_j)], "nn",
                        [((N_CHIPS, s, D_FF_Q), BF16, ffq, ffq_map)] * 3, epilogue=ffn_in_epi, j_outer=True,
                        sum_pairs=False)

    (h2,) = _mm("ffn_down", (nib, 1, 1),
                [(act, (None, tmb, D_FF_Q), (lambda i, j, k, q=q: (q, i, 0)), w_fd, (None, D_FF_Q, D_MODEL),
                  (lambda i, j, k, q=q: (q, 0, 0))) for q in range(N_CHIPS)], "nn",
                [((s, D_MODEL), F32, (tmb, D_MODEL), m3)], epilogue=lambda acc, hr: ((acc + hr,), ()),
                extras=[(h1, (tmb, D_MODEL), m3)])

    pp = proj_cols("ple_proj", p, PLE_DIM, w_pp)

    def ple_epi(acc, hr, ppr):
        return (acc, hr + _sig(acc) * ppr.astype(F32)), ()

    gl, h3 = _mm("ple_gate", (nib, 1, 1), [(h2, (tmb, D_MODEL), m3, w_pg2, (D_MODEL, D_MODEL), w3)], "nn",
                 [((s, D_MODEL), BF16, (tmb, D_MODEL), m3), ((s, D_MODEL), F32, (tmb, D_MODEL), m3)],
                 epilogue=ple_epi, extras=[(h2, (tmb, D_MODEL), m3), (pp, (tmb, D_MODEL), m3)])

    def head_fn(pids, h, t, g):
        r = lax.rsqrt(jnp.mean(h * h, axis=-1, keepdims=True) + EPS)
        hhat = h * r
        diff = hhat * g - t
        loss = 0.5 * jnp.sum(jnp.mean(diff * diff, axis=-1, keepdims=True))
        dy = diff * (1.0 / D_MODEL)
        gy = dy * g
        dh = r * (gy - hhat * jnp.mean(gy * hhat, axis=-1, keepdims=True))
        return (dh,), (jnp.full((SUB, 128), loss, F32), jnp.sum(dy * hhat, axis=0, keepdims=True))

    dh3, loss_acc, dg_final = _ew("loss_head", (ni,), [(h3, rowblk, rowmap), (tgt, rowblk, rowmap), (g_final, *vec1k)],
                                  [((s, D_MODEL), F32, rowblk, rowmap)], head_fn,
                                  acc_outs=[((SUB, 128), F32), ((1, D_MODEL), F32)])

    def ple_bwd_fn(pids, dh, g_, ppr):
        sg = _sig(g_.astype(F32))
        return (dh * ppr.astype(F32) * sg * (1.0 - sg), dh * sg), ()

    dgl, dpp = _ew("ple_bwd", (ni,), [(dh3, rowblk, rowmap), (gl, rowblk, rowmap), (pp, rowblk, rowmap)],
                   [((s, D_MODEL), BF16, rowblk, rowmap)] * 2, ple_bwd_fn)

    def wgrad(name, a, a_block, a_imap, b, b_block, b_imap, out_shape, out_block, out_imap, nj, acc_shape):
        return _mm(name, (1, nj, nk), [(a, a_block, a_imap, b, b_block, b_imap)], "tn",
                   [(out_shape, F32, out_block, out_imap)], acc_shape=acc_shape)[0]

    tk0 = lambda i, j, k: (k, 0)
    tkj = lambda i, j, k: (k, j)
    def wgrad_cols(name, a, width, dy_):
        def split(acc):
            return (jnp.stack([acc[:, q * 256:(q + 1) * 256] for q in range(N_CHIPS)], axis=0),), ()

        return _mm(name, (1, 1, nk), [(a, (ts, width), tk0, dy_, (ts, D_MODEL), tk0)], "tn",
                   [((N_CHIPS, width, 256), F32, (N_CHIPS, width, 256), lambda i, j, k: (0, 0, 0))], epilogue=split,
                   acc_shape=(width, D_MODEL))[0]

    d_w_pp = wgrad_cols("d_ple_proj", p, PLE_DIM, dpp)
    d_w_pg = wgrad("d_ple_gate", h2, (ts, D_MODEL), tk0, dgl, (ts, D_MODEL), tk0, (D_MODEL, D_MODEL),
                   (D_MODEL, D_MODEL), w3, 1, (D_MODEL, D_MODEL))

    (dh2,) = _mm("ple_gate_bwd", (nib, 1, 1), [(dgl, (tmb, D_MODEL), m3, w_pg2, (D_MODEL, D_MODEL), w3)], "nt",
                 [((s, D_MODEL), F32, (tmb, D_MODEL), m3)], epilogue=lambda acc, d_: ((acc + d_,), ()),
                 extras=[(dh3, (tmb, D_MODEL), m3)])

    def ffn_bwd_epi(acc, gt, u_):
        gt, u_ = gt.astype(F32), u_.astype(F32)
        sg = _sig(gt)
        return (acc * u_ * (sg * (1.0 + gt * (1.0 - sg))), acc * gt * sg), ()

    ffq_big = (None, tmb, D_FF_Q)
    dgate, dup = _mm("ffn_down_bwd", (nib, N_CHIPS, 1),
                     [(dh2, (tmb, D_MODEL), m3, w_fd, (None, D_FF_Q, D_MODEL), lambda i, j, k: (j, 0, 0))], "nt",
                     [((N_CHIPS, s, D_FF_Q), BF16, ffq_big, ffq_map)] * 2, epilogue=ffn_bwd_epi,
                     extras=[(gate, ffq_big, ffq_map), (up, ffq_big, ffq_map)])

    ffq_t = (None, ts, D_FF_Q)
    ffq_tmap = lambda i, j, k: (j, k, 0)
    blk_j = lambda i, j, k: (j, 0, 0)
    d_w_fd = wgrad("d_ffn_down", act, ffq_t, ffq_tmap, dh2, (ts, D_MODEL), tk0, (N_CHIPS, D_FF_Q, D_MODEL),
                   (None, D_FF_Q, D_MODEL), blk_j, N_CHIPS, (D_FF_Q, D_MODEL))
    d_w_fg = wgrad("d_ffn_gate", n2, (ts, D_MODEL), tk0, dgate, ffq_t, ffq_tmap, (N_CHIPS, D_MODEL, D_FF_Q),
                   (None, D_MODEL, D_FF_Q), blk_j, N_CHIPS, (D_MODEL, D_FF_Q))
    d_w_fu = wgrad("d_ffn_up", n2, (ts, D_MODEL), tk0, dup, ffq_t, ffq_tmap, (N_CHIPS, D_MODEL, D_FF_Q),
                   (None, D_MODEL, D_FF_Q), blk_j, N_CHIPS, (D_MODEL, D_FF_Q))

    def norm_bwd_epi(acc, h, d_res, g):
        dh, dg = _rms_bwd_tile(acc, h, g)
        return (d_res + dh,), (dg,)

    ffq_k = lambda i, j, k: (k, i, 0)
    blk_k = lambda i, j, k: (k, 0, 0)
    ffq_b = (None, tmb, D_FF_Q)
    dh1, dg_ffn = _mm("ffn_in_bwd", (nib, 1, N_CHIPS),
                      [(dgate, ffq_b, ffq_k, w_fg, (None, D_MODEL, D_FF_Q), blk_k),
                       (dup, ffq_b, ffq_k, w_fu, (None, D_MODEL, D_FF_Q), blk_k)], "nt",
                      [((s, D_MODEL), F32, (tmb, D_MODEL), m3)], epilogue=norm_bwd_epi,
                      extras=[(h1, (tmb, D_MODEL), m3), (dh2, (tmb, D_MODEL), m3), (g_ffn, (1, D_MODEL), w3)],
                      acc_outs=[((1, D_MODEL), F32)], acc_shape=(tmb, D_MODEL))

    (dmix,) = _mm("out_proj_bwd", (nib, 1, 1), [(dh1, (tmb, D_MODEL), m3, w_out2, (D_MODEL, D_MODEL), w3)], "nt",
                  [((s, D_MODEL), BF16, (tmb, D_MODEL), m3)])
    d_w_out = wgrad("d_out_proj", mix, (ts, D_MODEL), tk0, dh1, (ts, D_MODEL), tk0, (D_MODEL, D_MODEL),
                    (D_MODEL, D_MODEL), w3, 1, (D_MODEL, D_MODEL))

    def mix_bwd_fn(pids, dm, ga, gs, ad, a, b):
        dm, ga, gs, ad, a, b = (t.astype(F32) for t in (dm, ga, gs, ad, a, b))
        s_a, s_s, s_b = _sig(ga), _sig(gs), _sig(b)
        d_ssm = dm * s_s
        return (dm * ad * s_a * (1.0 - s_a), dm * (a * s_b) * s_s * (1.0 - s_s), dm * s_a, d_ssm * s_b,
                d_ssm * a * s_b * (1.0 - s_b)), ()

    dga, dgs, dattn_d, dpa, dpb = _ew(
        "gate_mix_bwd", (ni,),
        [(dmix, rowblk, rowmap), (z, *ga_blk), (z, *gs_blk), (attn_d, rowblk, rowmap), (pa, rowblk, rowmap),
         (pb, rowblk, rowmap)], [((s, D_MODEL), BF16, rowblk, rowmap)] * 5, mix_bwd_fn)

    d_w_ap = wgrad_cols("d_attn_proj", attn, GROUP_WIDTH, dattn_d)
    d_w_ga = wgrad_cols("d_glu_a", yg, GROUP_WIDTH, dpa)
    d_w_gb = wgrad_cols("d_glu_b", yg, GROUP_WIDTH, dpb)

    ik = lambda i, j, k: (i, k)

    def cols_bwd(dy_, w):
        return [(dy_, (tmb, 256), (lambda i, j, k, q=q: (i, q)), w, (None, GROUP_WIDTH, 256),
                 (lambda i, j, k, q=q: (q, 0, 0))) for q in range(N_CHIPS)]

    (dattn,) = _mm("attn_proj_bwd", (nib, 1, 1), cols_bwd(dattn_d, w_ap), "nt",
                   [((s, GROUP_WIDTH), BF16, (tmb, GROUP_WIDTH), m3)])

    (dys,) = _mm("glu_bwd", (nib, 1, 1), cols_bwd(dpa, w_ga) + cols_bwd(dpb, w_gb), "nt",
                 [((s, GROUP_WIDTH), F32, (tmb, GROUP_WIDTH), m3)],
                 epilogue=lambda acc, y_: ((acc * _gelu_grad(y_),), ()),
                 extras=[(ys, (tmb, GROUP_WIDTH), m3)])

    du, d_lr, d_li, d_bre, d_bim, d_cre, d_cim, d_dskip = _ssm_bwd(
        dys, z, h_re, h_im, b_re_m, b_im_m, c_re_m, c_im_m, lam_re_row, lam_im_row, d_skip_row, chunk)

    dattn_views = _to_views(dattn, tm)
    bwd_in = [(dattn, attn, lse)] + [(dv_, ov_, lv_) for dv_, (ov_, lv_) in zip(dattn_views, merged_views)]
    qkv_grads = [_attn_bwd(*arrs, *dol, dil, cols3)
                 for (arrs, cols3), dol, dil in zip(group_in, bwd_in, GROUP_DILATIONS)]
    dz = _dz_layout(qkv_grads, du, dga, dgs, tabs, tm)

    kb = lambda i, j, k: (k // 2, 0, k % 2)
    grad_x, dg_mix = _mm("in_proj_bwd", (nib, 1, 8), [(dz, (tmb, half_in), ik, w_in, (None, D_MODEL, half_in), kb)], "nt",
                         [((s, D_MODEL), F32, (tmb, D_MODEL), m3)], epilogue=norm_bwd_epi,
                         extras=[(x, (tmb, D_MODEL), m3), (dh1, (tmb, D_MODEL), m3), (g_mix, (1, D_MODEL), w3)],
                         acc_outs=[((1, D_MODEL), F32)], acc_shape=(tmb, D_MODEL))
    d_w_in = wgrad("d_in_proj", n1, (ts, D_MODEL), tk0, dz, (ts, half_in), tkj, (N_CHIPS, D_MODEL, IN_WIDTH // N_CHIPS),
                   (None, D_MODEL, half_in), lambda i, j, k: (j // 2, 0, j % 2), 8, (D_MODEL, half_in))

    d_bbt_re = _block_diag_t(d_bre, SSM_GROUP, SSM_STATE)
    d_bbt_im = _block_diag_t(d_bim, SSM_GROUP, SSM_STATE)
    d_a_re, d_a_im, d_log_dt, d_bt_re, d_bt_im = _ssm_param_bwd(
        sm["a_re"], sm["a_im"], log_dt_col, bt_re, bt_im,
        d_lr.reshape(SSM_GROUPS, SSM_STATE), d_li.reshape(SSM_GROUPS, SSM_STATE), d_bbt_re, d_bbt_im)
    small = {
        "g_mix": dg_mix, "a_re": d_a_re, "a_im": d_a_im, "log_dt": d_log_dt,
        "b_re": jnp.transpose(d_bt_re, (0, 2, 1)), "b_im": jnp.transpose(d_bt_im, (0, 2, 1)),
        "c_re": jnp.transpose(_block_diag_t(d_cre, SSM_STATE, SSM_GROUP), (0, 2, 1)),
        "c_im": jnp.transpose(_block_diag_t(d_cim, SSM_STATE, SSM_GROUP), (0, 2, 1)),
        "d_skip": d_dskip, "g_ffn": dg_ffn, "g_final": dg_final,
    }
    big = {
        "w_in": d_w_in, "w_attn_proj": d_w_ap, "w_glu_a": d_w_ga, "w_glu_b": d_w_gb,
        "w_out": d_w_out.reshape(N_CHIPS, D_MODEL // N_CHIPS, D_MODEL), "w_ffn_gate": d_w_fg, "w_ffn_up": d_w_fu,
        "w_ffn_down": d_w_fd, "w_ple_gate": d_w_pg.reshape(N_CHIPS, D_MODEL // N_CHIPS, D_MODEL), "w_ple_proj": d_w_pp,
    }
    return loss_acc[0, 0], grad_x, big, small


BIG = ("w_in", "w_attn_proj", "w_glu_a", "w_glu_b", "w_out", "w_ffn_gate", "w_ffn_up", "w_ffn_down", "w_ple_gate",
       "w_ple_proj")
SMALL = ("g_mix", "a_re", "a_im", "log_dt", "b_re", "b_im", "c_re", "c_im", "d_skip", "g_ffn", "g_final")
ANY = pl.BlockSpec(memory_space=pl.ANY)


def _place():
    x, y, c = lax.axis_index("x"), lax.axis_index("y"), lax.axis_index("c")
    chips = [(1 - x, y), (x, 1 - y), (1 - x, 1 - y)]
    return x, y, c, chips


def _remote(src, dst, send_sem, recv_sem, to):
    return pltpu.make_async_remote_copy(src_ref=src, dst_ref=dst, send_sem=send_sem, recv_sem=recv_sem, device_id=to,
                                        device_id_type=MESH)


def _comm_call(name, body, ins, out_shapes, n_sems, aliases=None):
    n_w = len(ins)
    return pl.pallas_call(
        body, name=name, in_specs=[ANY] * n_w, out_specs=[ANY] * len(out_shapes), out_shape=out_shapes,
        scratch_shapes=[pltpu.SemaphoreType.DMA((n,)) for n in n_sems], input_output_aliases=aliases or {},
    )(*ins)


def _gather_weights(bufs):
    n_w = len(bufs)

    def body(*refs):
        outs = refs[n_w:2 * n_w]
        ici_send, ici_recv, d2d_send, d2d_recv = refs[2 * n_w:]
        x, y, c, chips = _place()
        me = 2 * x + y
        sib = (x, y, 1 - c)
        sends = []
        for w in range(n_w):
            for j, (cx, cy) in enumerate(chips):
                k = 3 * w + j
                mine = outs[w].at[me, c]
                cp = _remote(mine, mine, ici_send.at[k], ici_recv.at[k], (cx, cy, c))
                cp.start()
                sends.append(cp)
        for w in range(n_w):
            for j, (cx, cy) in enumerate(chips):
                k = 3 * w + j
                src_chip = 2 * cx + cy
                landed = outs[w].at[src_chip, c]
                _remote(landed, landed, ici_send.at[k], ici_recv.at[k], (cx, cy, c)).wait_recv()
                fwd = _remote(landed, landed, d2d_send.at[k], d2d_recv.at[k], sib)
                fwd.start()
                sends.append(fwd)
        for w in range(n_w):
            for j, (cx, cy) in enumerate(chips):
                k = 3 * w + j
                other = outs[w].at[2 * cx + cy, 1 - c]
                _remote(other, other, d2d_send.at[k], d2d_recv.at[k], sib).wait_recv()
        for cp in sends:
            cp.wait_send()

    out_shapes = [jax.ShapeDtypeStruct(b.shape, b.dtype) for b in bufs]
    return _comm_call("gather_weights", body, bufs, out_shapes, [3 * n_w] * 4, aliases={w: w for w in range(n_w)})


def _pair_exchange(grads):
    n_w = len(grads)

    def body(*refs):
        ins, outs = refs[:n_w], refs[n_w:2 * n_w]
        send, recv = refs[2 * n_w:]
        x, y, c, _ = _place()
        sib = (x, y, 1 - c)
        cps = []
        for w in range(n_w):
            for q in range(N_CHIPS):
                k = N_CHIPS * w + q
                cp = _remote(ins[w].at[q, 1 - c], outs[w].at[q], send.at[k], recv.at[k], sib)
                cp.start()
                cps.append(cp)
        for cp in cps:
            cp.wait()

    out_shapes = [jax.ShapeDtypeStruct((N_CHIPS,) + g.shape[2:], g.dtype) for g in grads]
    return _comm_call("grad_pair_exchange", body, grads, out_shapes, [N_CHIPS * n_w] * 2)


def _chip_exchange(parts):
    n_w = len(parts)

    def body(*refs):
        ins, outs = refs[:n_w], refs[n_w:2 * n_w]
        send, recv = refs[2 * n_w:]
        x, y, c, chips = _place()
        me = 2 * x + y
        cps = []
        for w in range(n_w):
            for j, (cx, cy) in enumerate(chips):
                k = 3 * w + j
                cp = _remote(ins[w].at[2 * cx + cy], outs[w].at[me], send.at[k], recv.at[k], (cx, cy, c))
                cp.start()
                cps.append(cp)
        for w in range(n_w):
            for j, (cx, cy) in enumerate(chips):
                k = 3 * w + j
                got = outs[w].at[2 * cx + cy]
                _remote(got, got, send.at[k], recv.at[k], (cx, cy, c)).wait_recv()
        for cp in cps:
            cp.wait_send()

    out_shapes = [jax.ShapeDtypeStruct(t.shape, t.dtype) for t in parts]
    return _comm_call("grad_chip_exchange", body, parts, out_shapes, [3 * n_w, 3 * n_w])


def _pair_gather(halves):
    n_w = len(halves)

    def body(*refs):
        ins, outs = refs[:n_w], refs[n_w:2 * n_w]
        send, recv = refs[2 * n_w:]
        x, y, c, _ = _place()
        sib = (x, y, 1 - c)
        cps = []
        for w in range(n_w):
            cp = _remote(ins[w], outs[w], send.at[w], recv.at[w], sib)
            cp.start()
            cps.append(cp)
        for cp in cps:
            cp.wait()

    out_shapes = [jax.ShapeDtypeStruct(h.shape, h.dtype) for h in halves]
    return _comm_call("grad_pair_gather", body, halves, out_shapes, [n_w] * 2)


def _all_exchange(vec):
    def body(in_ref, out_ref, send, recv):
        x, y, c, _ = _place()
        me = 4 * x + 2 * y + c
        cps = []
        for k in range(1, 8):
            fx, fy, fc = (k >> 2) & 1, (k >> 1) & 1, k & 1
            to = (x ^ fx, y ^ fy, c ^ fc)
            cp = _remote(in_ref, out_ref.at[me], send.at[k - 1], recv.at[k - 1], to)
            cp.start()
            cps.append(cp)
        for k in range(1, 8):
            fx, fy, fc = (k >> 2) & 1, (k >> 1) & 1, k & 1
            src = 4 * (x ^ fx) + 2 * (y ^ fy) + (c ^ fc)
            got = out_ref.at[src]
            _remote(got, got, send.at[k - 1], recv.at[k - 1], (x ^ fx, y ^ fy, c ^ fc)).wait_recv()
        for cp in cps:
            cp.wait_send()

    return _comm_call("small_all_exchange", body, [vec], [jax.ShapeDtypeStruct((8,) + vec.shape, vec.dtype)], [7, 7])[0]


def _row_tile(r):
    for t in (256, 128, 176, 64, 32, 16, 8):
        if r % t == 0:
            return t
    return r


P_C, P_CHIP, P_DEV = 2, 3, 4


def _cast_into_slot(w2, place):
    r, c = w2.shape
    t = _row_tile(r)
    return _ew("cast_shard", (r // t,), [(w2, (t, c), lambda i, pv: (i, 0))],
               [((N_CHIPS, r, c), BF16, (None, t, c), lambda i, pv: (pv[P_CHIP], i, 0))],
               lambda pids, a: ((a,), ()), place=place)[0]


def _pair_sum(mine, theirs, place):
    _, r, c = theirs.shape
    t = _row_tile(r)
    own = ((None, None, t, c), lambda q, i, pv: (q, pv[P_C], i, 0))
    blk = ((None, t, c), lambda q, i, pv: (q, i, 0))
    return _ew("grad_pair_sum", (N_CHIPS, r // t), [(mine, *own), (theirs, *blk)], [((N_CHIPS, r, c), BF16, *blk)],
               lambda pids, a, b: ((a + b,), ()), place=place)[0]


def _chip_sum(own, got, place):
    _, r, c = own.shape
    t = _row_tile(r)
    ins = []
    for q in range(N_CHIPS):
        ins.append((own, (None, t, c), (lambda i, pv, q=q: (q, i, 0))))
        ins.append((got, (None, t, c), (lambda i, pv, q=q: (jnp.where(pv[P_CHIP] == q, (q + 1) % N_CHIPS, q), i, 0))))

    def fn(pids, *tiles):
        me = pids[0][P_CHIP]
        tot = None
        for q in range(N_CHIPS):
            term = jnp.where(me == q, tiles[2 * q], tiles[2 * q + 1]).astype(F32)
            tot = term if tot is None else tot + term
        return (tot,), ()

    return _ew("grad_chip_sum", (r // t,), ins, [((r, c), F32, (t, c), lambda i, pv: (i, 0))], fn, place=place)[0]


def _adamw_tile(w, g, m, v):
    m = ADAM_B1 * m + (1.0 - ADAM_B1) * g
    v = ADAM_B2 * v + (1.0 - ADAM_B2) * (g * g)
    m_hat = m / (1.0 - ADAM_B1 ** ADAM_STEP)
    v_hat = v / (1.0 - ADAM_B2 ** ADAM_STEP)
    delta = -ADAM_LR * (m_hat / (jnp.sqrt(v_hat) + ADAM_EPS) + ADAM_WD * w)
    return delta, m, v


def _adamw(name, g2, w2, m2, v2):
    r, c = w2.shape
    t = _row_tile(r)
    blk, imap = _rows(t, c)

    def fn(pids, g, w, m, v):
        delta, nm, nv = _adamw_tile(w, g, m, v)
        return (g, delta, nm, nv), ()

    return _ew(name, (r // t,), [(a, blk, imap) for a in (g2, w2, m2, v2)], [((r, c), F32, blk, imap)] * 4, fn)


def _adamw_halves(name, mine, theirs, w2, m2, v2, place):
    r, c = w2.shape
    t = _row_tile(r // 2)
    n_t = (r // 2) // t
    half = ((t, c), lambda h, i, pv: (i, 0))
    whole = ((t, c), lambda h, i, pv: (h * n_t + i, 0))

    def fn(pids, ga, gb, w, m, v):
        g = jnp.where(pids[1] == pids[0][P_C], ga, gb)
        delta, nm, nv = _adamw_tile(w, g, m, v)
        return (g, delta, nm, nv), ()

    return _ew(name, (2, n_t), [(mine, *half), (theirs, *half), (w2, *whole), (m2, *whole), (v2, *whole)],
               [((r, c), F32, *whole)] * 4, fn, place=place)


def _device_sum(own, got, place):
    r, c = own.shape
    t = _row_tile(r)
    ins = [(own, (t, c), lambda i, pv: (i, 0))]
    for q in range(8):
        ins.append((got, (None, t, c), (lambda i, pv, q=q: (jnp.where(pv[P_DEV] == q, (q + 1) % 8, q), i, 0))))

    def fn(pids, mine, *parts):
        me = pids[0][P_DEV]
        tot = None
        for q in range(8):
            term = jnp.where(me == q, mine, parts[q])
            tot = term if tot is None else tot + term
        return (tot,), ()

    return _ew("small_device_sum", (r // t,), ins, [((r, c), F32, (t, c), lambda i, pv: (i, 0))], fn, place=place)[0]


def _pack(parts):
    flat = jnp.concatenate([a.reshape(-1) for a in parts])
    pad = (-flat.shape[0]) % (SUB * 128)
    return jnp.pad(flat, (0, pad)).reshape(-1, 128)


def _unpack(mat, shapes):
    flat = mat.reshape(-1)
    out, off = [], 0
    for shp in shapes:
        n = math.prod(shp)
        out.append(flat[off:off + n].reshape(shp))
        off += n
    return out


def kernel(x, p, positions, g_mix, w_in, a_re, a_im, log_dt, b_re, b_im, c_re, c_im, d_skip, w_attn_proj, w_glu_a, w_glu_b, w_out, g_ffn, w_ffn_gate, w_ffn_up, w_ffn_down, w_ple_gate, w_ple_proj, g_final, loss_target, m_g_mix, m_w_in, m_a_re, m_a_im, m_log_dt, m_b_re, m_b_im, m_c_re, m_c_im, m_d_skip, m_w_attn_proj, m_w_glu_a, m_w_glu_b, m_w_out, m_g_ffn, m_w_ffn_gate, m_w_ffn_up, m_w_ffn_down, m_w_ple_gate, m_w_ple_proj, m_g_final, v_g_mix, v_w_in, v_a_re, v_a_im, v_log_dt, v_b_re, v_b_im, v_c_re, v_c_im, v_d_skip, v_w_attn_proj, v_w_glu_a, v_w_glu_b, v_w_out, v_g_ffn, v_w_ffn_gate, v_w_ffn_up, v_w_ffn_down, v_w_ple_gate, v_w_ple_proj, v_g_final):
    given = dict(locals())
    big_w = {n: given[n] for n in BIG}
    w_mats = {n: big_w[n].reshape(big_w[n].shape[1:]) for n in BIG}

    ax, ay, ac = lax.axis_index("x"), lax.axis_index("y"), lax.axis_index("c")
    place = jnp.stack([ax, ay, ac, 2 * ax + ay, 4 * ax + 2 * ay + ac]).astype(jnp.int32)

    bufs = []
    for n in BIG:
        r, c = w_mats[n].shape
        bufs.append(_cast_into_slot(w_mats[n], place).reshape(N_CHIPS, 2, r // 2, c))
    gathered = _gather_weights(bufs)
    wts = {}
    for n, g in zip(BIG, gathered):
        r, c = w_mats[n].shape
        wts[n] = g.reshape(N_CHIPS, r, c)

    sm = {
        "g_mix": g_mix.reshape(1, D_MODEL), "g_ffn": g_ffn.reshape(1, D_MODEL), "g_final": g_final.reshape(1, D_MODEL),
        "a_re": a_re[0], "a_im": a_im[0], "log_dt": log_dt[0], "b_re": b_re[0], "b_im": b_im[0], "c_re": c_re[0],
        "c_im": c_im[0], "d_skip": d_skip[0],
    }
    s = x.shape[1]
    loss_part, grad_x, big_g, small_g = _local_step(x[0], p[0, 0], positions[0], loss_target[0], sm, wts)

    g5 = []
    for n in BIG:
        r, c = w_mats[n].shape
        g5.append(big_g[n].reshape(N_CHIPS, 2, r // 2, c))
    theirs = _pair_exchange(g5)
    chip_parts = [_pair_sum(g, t, place) for g, t in zip(g5, theirs)]
    chip_got = _chip_exchange(chip_parts)
    halves = [_chip_sum(own, got, place) for own, got in zip(chip_parts, chip_got)]
    other_halves = _pair_gather(halves)

    results = {}
    for n, mine, other in zip(BIG, halves, other_halves):
        r, c = w_mats[n].shape
        shp = big_w[n].shape
        outs = _adamw_halves("adamw_" + n, mine, other, w_mats[n], given["m_" + n].reshape(r, c),
                             given["v_" + n].reshape(r, c), place)
        results[n] = [o.reshape(shp) for o in outs]

    small_shapes = [given[n].shape for n in SMALL]
    vec = _pack([small_g[n] for n in SMALL] + [loss_part.reshape(1)])
    tot = _device_sum(vec, _all_exchange(vec), place)
    n_small = sum(math.prod(shp) for shp in small_shapes)
    loss = tot.reshape(-1)[n_small]
    w_s = _pack([given[n] for n in SMALL])
    m_s = _pack([given["m_" + n] for n in SMALL])
    v_s = _pack([given["v_" + n] for n in SMALL])
    rows_s = w_s.shape[0]
    g_s = tot.reshape(-1)[: rows_s * 128].reshape(rows_s, 128)
    outs_s = _adamw("adamw_small", g_s, w_s, m_s, v_s)
    for kind, mat in enumerate(outs_s):
        for n, arr in zip(SMALL, _unpack(mat, small_shapes)):
            results.setdefault(n, [None] * 4)[kind] = arr

    order = ("g_mix", "w_in", "a_re", "a_im", "log_dt", "b_re", "b_im", "c_re", "c_im", "d_skip", "w_attn_proj", "w_glu_a",
             "w_glu_b", "w_out", "g_ffn", "w_ffn_gate", "w_ffn_up", "w_ffn_down", "w_ple_gate", "w_ple_proj", "g_final")
    out = [loss, grad_x.reshape(1, s, D_MODEL)]
    for kind in range(4):
        out += [results[n][kind] for n in order]
    return tuple(out)
```

```python
import math

import jax
import jax.numpy as jnp
from jax import lax
from jax.experimental import pallas as pl
from jax.experimental.pallas import tpu as pltpu

F32 = jnp.float32
BF16 = jnp.bfloat16

D_MODEL = 1024
HEAD_DIM = 128
HEADS_PER_GROUP = 4
GROUP_WIDTH = HEADS_PER_GROUP * HEAD_DIM
GROUP_DILATIONS = (1, 4, 16)
N_GROUPS = len(GROUP_DILATIONS)
ATTN_BLOCK = 128
ROPE_DIM = 32
ROPE_HALF = 16
ROPE_THETA = 500000.0
SSM_WIDTH = 512
SSM_GROUPS = 32
SSM_GROUP = 16
SSM_STATE = 64
N_STATE = SSM_GROUPS * SSM_STATE
SSM_SUPER = 4
IN_WIDTH = 7168
COL_U = 4608
COL_GA = 5120
COL_GS = 6144
D_FF = 2816
N_CHIPS = 4
D_FF_Q = D_FF // N_CHIPS
PLE_DIM = 256
EPS = 1e-6
ADAM_LR = 0.001
ADAM_B1 = 0.9
ADAM_B2 = 0.999
ADAM_EPS = 1e-08
ADAM_WD = 0.01
ADAM_STEP = 10
NEG_BIG = -1e30
VMEM_LIMIT_BYTES = 56 * 1024 * 1024
MESH = pl.DeviceIdType.MESH

_DIMS = {
    "nn": (((1,), (0,)), ((), ())),
    "nt": (((1,), (1,)), ((), ())),
    "tn": (((0,), (0,)), ((), ())),
}


def _params(n_grid):
    return pltpu.CompilerParams(dimension_semantics=("arbitrary",) * n_grid, vmem_limit_bytes=VMEM_LIMIT_BYTES)


def _sig(v):
    return 1.0 / (1.0 + jnp.exp(-v))


def _dot(a, b, mode):
    return lax.dot_general(a, b, _DIMS[mode], preferred_element_type=F32)


def _mm(name, grid, pairs, mode, outs, epilogue=None, extras=(), acc_outs=(), acc_shape=None, j_outer=False,
        sum_pairs=True):
    gi, gj, gk = grid
    n_p, n_e, n_o, n_a = len(pairs), len(extras), len(outs), len(acc_outs)
    assert not n_a or gj == 1
    assert sum_pairs or gk == 1

    def order(imap):
        return (lambda j, i, k: imap(i, j, k)) if j_outer else imap

    shared_a = [pr[0] is None for pr in pairs]
    n_in = 2 * n_p - sum(shared_a)

    def body(*refs):
        pair_refs = list(refs[:n_in])
        extra_refs = refs[n_in: n_in + n_e]
        out_refs = refs[n_in + n_e: n_in + n_e + n_o]
        sum_refs = refs[n_in + n_e + n_o: n_in + n_e + n_o + n_a]
        i = pl.program_id(1 if j_outer else 0)
        k = pl.program_id(2)
        part = None if sum_pairs else []
        a = None
        for t in range(n_p):
            if not shared_a[t]:
                a = pair_refs.pop(0)[...].astype(BF16)
            b = pair_refs.pop(0)[...].astype(BF16)
            d = _dot(a, b, mode)
            if sum_pairs:
                part = d if part is None else part + d
            else:
                part.append(d)

        def finish(acc):
            tiles, sums = epilogue(acc, *[e[...] for e in extra_refs]) if epilogue is not None else ((acc,), ())
            for o_ref, tile in zip(out_refs, tiles):
                o_ref[...] = tile.astype(o_ref.dtype)
            if n_a:
                @pl.when(i == 0)
                def _():
                    for s_ref in sum_refs:
                        s_ref[...] = jnp.zeros_like(s_ref)

                for s_ref, s in zip(sum_refs, sums):
                    s_ref[...] += s

        if gk == 1:
            finish(part)
        else:
            acc_ref = refs[-1]

            @pl.when(k == 0)
            def _():
                acc_ref[...] = part

            @pl.when(k > 0)
            def _():
                acc_ref[...] += part

            @pl.when(k == gk - 1)
            def _():
                finish(acc_ref[...])

    in_specs, args = [], []
    for a, a_block, a_imap, b, b_block, b_imap in pairs:
        if a is not None:
            in_specs.append(pl.BlockSpec(a_block, order(a_imap)))
            args.append(a)
        in_specs.append(pl.BlockSpec(b_block, order(b_imap)))
        args.append(b)
    for e, e_block, e_imap in extras:
        in_specs.append(pl.BlockSpec(e_block, order(e_imap)))
        args.append(e)
    out_shape = [jax.ShapeDtypeStruct(shape, dtype) for shape, dtype, _, _ in outs]
    out_specs = [pl.BlockSpec(block, order(imap)) for _, _, block, imap in outs]
    for shape, dtype in acc_outs:
        out_shape.append(jax.ShapeDtypeStruct(shape, dtype))
        out_specs.append(pl.BlockSpec(shape, lambda i, j, k: (0, 0)))
    scratch = [pltpu.VMEM(acc_shape, F32)] if gk > 1 else []
    return pl.pallas_call(
        body, name=name, grid=(gj, gi, gk) if j_outer else grid, in_specs=in_specs, out_specs=out_specs,
        out_shape=out_shape, scratch_shapes=scratch, compiler_params=_params(3),
    )(*args)


def _ew(name, grid, ins, outs, fn, acc_outs=(), place=None):
    n_i, n_o, n_a = len(ins), len(outs), len(acc_outs)
    ng = len(grid)
    n_s = 0 if place is None else 1

    def body(*refs):
        in_refs = refs[n_s: n_s + n_i]
        out_refs = refs[n_s + n_i: n_s + n_i + n_o]
        sum_refs = refs[n_s + n_i + n_o:]
        pids = tuple(pl.program_id(a) for a in range(ng))
        if n_s:
            pids = (refs[0],) + pids
        tiles, sums = fn(pids, *[r[...] for r in in_refs])
        for o_ref, tile in zip(out_refs, tiles):
            o_ref[...] = tile.astype(o_ref.dtype)
        if n_a:
            first = pids[0] == 0
            for p_ in pids[1:]:
                first = jnp.logical_and(first, p_ == 0)

            @pl.when(first)
            def _():
                for s_ref in sum_refs:
                    s_ref[...] = jnp.zeros_like(s_ref)

            for s_ref, s in zip(sum_refs, sums):
                s_ref[...] += s

    in_specs = [pl.BlockSpec(block, imap) for _, block, imap in ins]
    out_shape = [jax.ShapeDtypeStruct(shape, dtype) for shape, dtype, _, _ in outs]
    out_specs = [pl.BlockSpec(block, imap) for _, _, block, imap in outs]
    for shape, dtype in acc_outs:
        out_shape.append(jax.ShapeDtypeStruct(shape, dtype))
        out_specs.append(pl.BlockSpec(shape, lambda *_, nd=len(shape): (0,) * nd))
    arrays = [a for a, _, _ in ins]
    if n_s:
        assert not n_a
        spec = pltpu.PrefetchScalarGridSpec(num_scalar_prefetch=1, grid=grid, in_specs=in_specs, out_specs=out_specs)
        return pl.pallas_call(body, name=name, grid_spec=spec, out_shape=out_shape, compiler_params=_params(ng))(
            place, *arrays)
    return pl.pallas_call(
        body, name=name, grid=grid, in_specs=in_specs, out_specs=out_specs, out_shape=out_shape,
        compiler_params=_params(ng),
    )(*arrays)


def _rows(tm, width):
    return (tm, width), (lambda i: (i, 0))


def _rms_fwd_tile(h, g):
    r = lax.rsqrt(jnp.mean(h * h, axis=-1, keepdims=True) + EPS)
    return h * r * g


def _rms_bwd_tile(dn, h, g):
    r = lax.rsqrt(jnp.mean(h * h, axis=-1, keepdims=True) + EPS)
    hhat = h * r
    gy = dn * g
    dh = r * (gy - hhat * jnp.mean(gy * hhat, axis=-1, keepdims=True))
    dg = jnp.sum(dn * hhat, axis=0, keepdims=True)
    return dh, dg


def _rope_tables(pos_col, inv_row, tm):
    s = pos_col.shape[0]

    def fn(pids, pos, inv):
        ang = pos * inv
        lane = lax.broadcasted_iota(jnp.int32, ang.shape, 1)
        cs = jnp.where(lane < ROPE_DIM, jnp.cos(ang), 1.0)
        sn = jnp.sin(ang)
        s_lo = jnp.where(lane < ROPE_HALF, -sn, 0.0)
        s_hi = jnp.where(jnp.logical_and(lane >= ROPE_HALF, lane < ROPE_DIM), sn, 0.0)
        return (cs, s_lo, s_hi), ()

    blk, imap = _rows(tm, 128)
    return _ew(
        "rope_tables", (s // tm,),
        [(pos_col, (tm, 1), lambda i: (i, 0)), (inv_row, (1, 128), lambda i: (0, 0))],
        [((s, 128), F32, blk, imap)] * 3, fn,
    )


def _rope(xh, cs, s_lo, s_hi):
    return xh * cs + pltpu.roll(xh, HEAD_DIM - ROPE_HALF, 1) * s_lo + pltpu.roll(xh, ROPE_HALF, 1) * s_hi


def _rope_t(gh, cs, s_lo, s_hi):
    return gh * cs + pltpu.roll(gh * s_lo, ROPE_HALF, 1) + pltpu.roll(gh * s_hi, HEAD_DIM - ROPE_HALF, 1)


def _attn_geometry(length):
    nb = length // ATTN_BLOCK
    gq = min(4, nb)
    assert nb % gq == 0
    return nb, gq, gq * ATTN_BLOCK, nb // gq


def _band_masks():
    qi = lax.broadcasted_iota(jnp.int32, (ATTN_BLOCK, ATTN_BLOCK), 0)
    kj = lax.broadcasted_iota(jnp.int32, (ATTN_BLOCK, ATTN_BLOCK), 1)
    return kj <= qi, kj >= qi


def _band_mask_pair():
    qi = lax.broadcasted_iota(jnp.int32, (ATTN_BLOCK, 2 * ATTN_BLOCK), 0)
    cj = lax.broadcasted_iota(jnp.int32, (ATTN_BLOCK, 2 * ATTN_BLOCK), 1)
    in_cur = cj >= ATTN_BLOCK
    band = jnp.logical_or(jnp.logical_and(in_cur, cj - ATTN_BLOCK <= qi),
                          jnp.logical_and(cj < ATTN_BLOCK, cj >= qi))
    return band, in_cur


def _attn_fwd(qv, kv, vv, dil, cols3=(0, 0, 0)):
    length = qv.shape[0]
    nb, gq, rows, ni = _attn_geometry(length)
    out_shape = (length, dil * GROUP_WIDTH)

    def body(q_ref, kc_ref, kp_ref, vc_ref, vp_ref, o_ref, l_ref):
        i = pl.program_id(1)
        band, in_cur = _band_mask_pair()
        band_first = jnp.logical_and(band, jnp.logical_or(in_cur, i > 0))
        for h in range(HEADS_PER_GROUP):
            cols = slice(h * HEAD_DIM, (h + 1) * HEAD_DIM)
            qh = q_ref[:, cols]
            k_all = jnp.concatenate([kp_ref[:, cols], kc_ref[:, cols]], axis=0)
            v_all = jnp.concatenate([vp_ref[:, cols], vc_ref[:, cols]], axis=0)
            for jj in range(gq):
                rws = slice(jj * ATTN_BLOCK, (jj + 1) * ATTN_BLOCK)
                two = slice(jj * ATTN_BLOCK, (jj + 2) * ATTN_BLOCK)
                s = jnp.where(band_first if jj == 0 else band, _dot(qh[rws], k_all[two], "nt"), NEG_BIG)
                m = jnp.max(s, axis=-1, keepdims=True)
                pexp = jnp.exp(s - m)
                den = jnp.sum(pexp, axis=-1, keepdims=True)
                o = _dot(pexp.astype(BF16), v_all[two], "nn")
                o_ref[rws, cols] = (o / den).astype(o_ref.dtype)
                l_ref[rws, cols] = jnp.broadcast_to(m + jnp.log(den), (ATTN_BLOCK, HEAD_DIM))

    def cur(c):
        return pl.BlockSpec((rows, GROUP_WIDTH), lambda r, i: (i, r + c))

    def prev(c):
        return pl.BlockSpec((ATTN_BLOCK, GROUP_WIDTH), lambda r, i: (jnp.maximum(i * gq - 1, 0), r + c))

    cq, ck, cv = cols3
    return pl.pallas_call(
        body, name=f"attn_fwd_d{dil}", grid=(dil, ni),
        in_specs=[cur(cq), cur(ck), prev(ck), cur(cv), prev(cv)],
        out_specs=[cur(0), cur(0)],
        out_shape=[jax.ShapeDtypeStruct(out_shape, BF16), jax.ShapeDtypeStruct(out_shape, F32)],
        compiler_params=_params(2),
    )(qv, kv, kv, vv, vv)


def _attn_bwd(qv, kv, vv, dov, ov, lv, dil, cols3=(0, 0, 0)):
    length = qv.shape[0]
    nb, gq, rows, ni = _attn_geometry(length)
    out_shape = (length, dil * GROUP_WIDTH)

    def body(qc_ref, qn_ref, kc_ref, kp_ref, vc_ref, vp_ref, doc_ref, don_ref, oc_ref, on_ref, lc_ref, ln_ref,
             dq_ref, dk_ref, dv_ref):
        i = pl.program_id(1)
        _, mask_p = _band_masks()
        band, in_cur = _band_mask_pair()
        band_first = jnp.logical_and(band, jnp.logical_or(in_cur, i > 0))
        has_next = i < ni - 1

        last = slice(gq * ATTN_BLOCK, (gq + 1) * ATTN_BLOCK)
        mask_next = jnp.logical_and(mask_p, has_next)

        def rows_of(jj):
            return slice(jj * ATTN_BLOCK, (jj + 1) * ATTN_BLOCK)

        def keys_of(jj):
            return slice(jj * ATTN_BLOCK, (jj + 2) * ATTN_BLOCK)

        heads = []
        for h in range(HEADS_PER_GROUP):
            cols = slice(h * HEAD_DIM, (h + 1) * HEAD_DIM)
            hd = dict(
                cols=cols, q_c=qc_ref[:, cols], q_n=qn_ref[:, cols],
                k_all=jnp.concatenate([kp_ref[:, cols], kc_ref[:, cols]], axis=0),
                v_all=jnp.concatenate([vp_ref[:, cols], vc_ref[:, cols]], axis=0),
                do_c=doc_ref[:, cols], do_n=don_ref[:, cols],
                l_c=lc_ref[:, h * HEAD_DIM:h * HEAD_DIM + 1], l_n=ln_ref[:, h * HEAD_DIM:h * HEAD_DIM + 1],
            )
            hd["dl_c"] = jnp.sum(hd["do_c"].astype(F32) * oc_ref[:, cols].astype(F32), axis=-1, keepdims=True)
            hd["dl_n"] = jnp.sum(hd["do_n"].astype(F32) * on_ref[:, cols].astype(F32), axis=-1, keepdims=True)
            hd["s"] = [_dot(hd["q_c"][rows_of(jj)], hd["k_all"][keys_of(jj)], "nt") for jj in range(gq)]
            hd["dp"] = [_dot(hd["do_c"][rows_of(jj)], hd["v_all"][keys_of(jj)], "nt") for jj in range(gq)]
            hd["s"].append(_dot(hd["q_n"], hd["k_all"][last], "nt"))
            hd["dp"].append(_dot(hd["do_n"], hd["v_all"][last], "nt"))
            heads.append(hd)
        for hd in heads:
            hd["p"], hd["ds"] = [], []
            for jj in range(gq + 1):
                if jj < gq:
                    mask, l_col, delta = (band_first if jj == 0 else band), hd["l_c"][rows_of(jj)], hd["dl_c"][rows_of(jj)]
                else:
                    mask, l_col, delta = mask_next, hd["l_n"], hd["dl_n"]
                p = jnp.where(mask, jnp.exp(hd["s"][jj] - l_col), 0.0)
                hd["p"].append(p.astype(BF16))
                hd["ds"].append((p * (hd["dp"][jj] - delta)).astype(BF16))
        for hd in heads:
            cols = hd["cols"]
            dk_blocks, dv_blocks = [None] * (gq + 1), [None] * (gq + 1)

            def add(lst, idx, val):
                lst[idx] = val if lst[idx] is None else lst[idx] + val

            for jj in range(gq):
                qb, dob = hd["q_c"][rows_of(jj)], hd["do_c"][rows_of(jj)]
                dq_ref[rows_of(jj), cols] = _dot(hd["ds"][jj], hd["k_all"][keys_of(jj)], "nn").astype(dq_ref.dtype)
                dk2 = _dot(hd["ds"][jj], qb, "tn")
                dv2 = _dot(hd["p"][jj], dob, "tn")
                add(dk_blocks, jj, dk2[:ATTN_BLOCK])
                add(dk_blocks, jj + 1, dk2[ATTN_BLOCK:])
                add(dv_blocks, jj, dv2[:ATTN_BLOCK])
                add(dv_blocks, jj + 1, dv2[ATTN_BLOCK:])
            add(dk_blocks, gq, _dot(hd["ds"][gq], hd["q_n"], "tn"))
            add(dv_blocks, gq, _dot(hd["p"][gq], hd["do_n"], "tn"))
            for jj in range(gq):
                dk_ref[rows_of(jj), cols] = dk_blocks[jj + 1].astype(dk_ref.dtype)
                dv_ref[rows_of(jj), cols] = dv_blocks[jj + 1].astype(dv_ref.dtype)

    def cur(c):
        return pl.BlockSpec((rows, GROUP_WIDTH), lambda r, i: (i, r + c))

    def prev(c):
        return pl.BlockSpec((ATTN_BLOCK, GROUP_WIDTH), lambda r, i: (jnp.maximum(i * gq - 1, 0), r + c))

    def nxt(c):
        return pl.BlockSpec((ATTN_BLOCK, GROUP_WIDTH), lambda r, i: (jnp.minimum((i + 1) * gq, nb - 1), r + c))

    cq, ck, cv = cols3
    return pl.pallas_call(
        body, name=f"attn_bwd_d{dil}", grid=(dil, ni),
        in_specs=[cur(cq), nxt(cq), cur(ck), prev(ck), cur(cv), prev(cv), cur(0), nxt(0), cur(0), nxt(0), cur(0), nxt(0)],
        out_specs=[cur(0), cur(0), cur(0)],
        out_shape=[jax.ShapeDtypeStruct(out_shape, BF16)] * 3,
        compiler_params=_params(2),
    )(qv, qv, kv, kv, vv, vv, dov, dov, ov, ov, lv, lv)


DILATED = tuple((g, d) for g, d in enumerate(GROUP_DILATIONS) if d > 1)


def _spread(scr, slot, tile, out_ref, dil, col):
    tm = tile.shape[0]
    buf = scr.at[slot]
    buf[...] = tile
    for r in range(dil):
        c0 = r * GROUP_WIDTH + col
        out_ref[:, c0:c0 + HEAD_DIM] = buf[pl.ds(r, tm // dil, stride=dil), :].astype(out_ref.dtype)


def _collect(scr, slot, in_ref, dil, col):
    tm = scr.shape[1]
    buf = scr.at[slot]
    for r in range(dil):
        c0 = r * GROUP_WIDTH + col
        buf[pl.ds(r, tm // dil, stride=dil), :] = in_ref[:, c0:c0 + HEAD_DIM].astype(F32)
    return buf[...]


def _view_spec(tm, dil):
    return pl.BlockSpec((tm // dil, dil * GROUP_WIDTH), lambda i: (i, 0))


def _view_shape(s, dil, dtype):
    return jax.ShapeDtypeStruct((s // dil, dil * GROUP_WIDTH), dtype)


def _qkv_layout(z, tabs, tm):
    s = z.shape[0]
    scale = 1.0 / math.sqrt(HEAD_DIM)
    qkv_width = 3 * N_GROUPS * GROUP_WIDTH

    def body(z_ref, cs_ref, lo_ref, hi_ref, qk0_ref, *rest):
        views, scr = rest[:-1], rest[-1]
        tabs_ = (cs_ref[...], lo_ref[...], hi_ref[...])
        for part in range(3):
            for g, dil in enumerate(GROUP_DILATIONS):
                if part == 2 and dil == 1:
                    continue
                for h in range(HEADS_PER_GROUP):
                    col = part * N_GROUPS * GROUP_WIDTH + g * GROUP_WIDTH + h * HEAD_DIM
                    t = z_ref[:, col:col + HEAD_DIM].astype(F32)
                    if part < 2:
                        t = _rope(t, *tabs_)
                    if part == 0:
                        t = t * scale
                    if dil == 1:
                        c0 = part * GROUP_WIDTH + h * HEAD_DIM
                        qk0_ref[:, c0:c0 + HEAD_DIM] = t.astype(BF16)
                    else:
                        out = views[3 * [gg for gg, _ in DILATED].index(g) + part]
                        _spread(scr, h, t, out, dil, h * HEAD_DIM)

    row = lambda i: (i, 0)
    out_shape = [jax.ShapeDtypeStruct((s, 2 * GROUP_WIDTH), BF16)]
    out_specs = [pl.BlockSpec((tm, 2 * GROUP_WIDTH), row)]
    for _, dil in DILATED:
        out_shape += [_view_shape(s, dil, BF16)] * 3
        out_specs += [_view_spec(tm, dil)] * 3
    res = pl.pallas_call(
        body, name="qkv_layout", grid=(s // tm,),
        in_specs=[pl.BlockSpec((tm, qkv_width), row)] + [pl.BlockSpec((tm, HEAD_DIM), row)] * 3,
        out_specs=out_specs, out_shape=out_shape,
        scratch_shapes=[pltpu.VMEM((HEADS_PER_GROUP, tm, HEAD_DIM), F32)], compiler_params=_params(1),
    )(z, *tabs)
    return res[0], [tuple(res[1 + 3 * n:4 + 3 * n]) for n in range(len(DILATED))]


def _attn_merge(o0, l0, dilated, tm):
    s = o0.shape[0]
    n_d = len(DILATED)

    def body(*refs):
        o0_ref, l0_ref = refs[:2]
        in_views = refs[2:2 + 2 * n_d]
        attn_ref, lse_ref = refs[2 + 2 * n_d:4 + 2 * n_d]
        out_views = refs[4 + 2 * n_d:4 + 4 * n_d]
        scr = refs[-1]
        for h in range(HEADS_PER_GROUP):
            cols = slice(h * HEAD_DIM, (h + 1) * HEAD_DIM)
            os_ = [o0_ref[:, cols].astype(F32)]
            ls_ = [l0_ref[:, cols]]
            for n, (_, dil) in enumerate(DILATED):
                os_.append(_collect(scr, 2 * n, in_views[2 * n], dil, h * HEAD_DIM))
                ls_.append(_collect(scr, 2 * n + 1, in_views[2 * n + 1], dil, h * HEAD_DIM))
            m = ls_[0]
            for l_ in ls_[1:]:
                m = jnp.maximum(m, l_)
            es = [jnp.exp(l_ - m) for l_ in ls_]
            den = es[0]
            num = es[0] * os_[0]
            for e, o in zip(es[1:], os_[1:]):
                den = den + e
                num = num + e * o
            attn = num / den
            lse = m + jnp.log(den)
            attn_ref[:, cols] = attn.astype(BF16)
            lse_ref[:, cols] = lse
            for n, (_, dil) in enumerate(DILATED):
                _spread(scr, 2 * n_d, attn, out_views[2 * n], dil, h * HEAD_DIM)
                _spread(scr, 2 * n_d + 1, lse, out_views[2 * n + 1], dil, h * HEAD_DIM)

    row = lambda i: (i, 0)
    nat = pl.BlockSpec((tm, GROUP_WIDTH), row)
    in_specs = [nat, nat]
    args = [o0, l0]
    out_specs = [nat, nat]
    out_shape = [jax.ShapeDtypeStruct((s, GROUP_WIDTH), BF16), jax.ShapeDtypeStruct((s, GROUP_WIDTH), F32)]
    for (_, dil), (ov, lv) in zip(DILATED, dilated):
        in_specs += [_view_spec(tm, dil)] * 2
        args += [ov, lv]
        out_specs += [_view_spec(tm, dil)] * 2
        out_shape += [_view_shape(s, dil, BF16), _view_shape(s, dil, F32)]
    res = pl.pallas_call(
        body, name="attn_merge", grid=(s // tm,), in_specs=in_specs, out_specs=out_specs, out_shape=out_shape,
        scratch_shapes=[pltpu.VMEM((2 * n_d + 2, tm, HEAD_DIM), F32)], compiler_params=_params(1),
    )(*args)
    return res[0], res[1], [tuple(res[2 + 2 * n:4 + 2 * n]) for n in range(n_d)]


def _to_views(a, tm):
    s = a.shape[0]

    def body(a_ref, *rest):
        outs, scr = rest[:-1], rest[-1]
        for h in range(HEADS_PER_GROUP):
            t = a_ref[:, h * HEAD_DIM:(h + 1) * HEAD_DIM].astype(F32)
            for n, (_, dil) in enumerate(DILATED):
                _spread(scr, n, t, outs[n], dil, h * HEAD_DIM)

    return pl.pallas_call(
        body, name="to_views", grid=(s // tm,), in_specs=[pl.BlockSpec((tm, GROUP_WIDTH), lambda i: (i, 0))],
        out_specs=[_view_spec(tm, dil) for _, dil in DILATED], out_shape=[_view_shape(s, dil, BF16) for _, dil in DILATED],
        scratch_shapes=[pltpu.VMEM((len(DILATED), tm, HEAD_DIM), F32)], compiler_params=_params(1),
    )(a)


def _dz_layout(grads, du, dga, dgs, tabs, tm):
    s = du.shape[0]
    scale = 1.0 / math.sqrt(HEAD_DIM)

    def body(*refs):
        g_refs = refs[:3 * N_GROUPS]
        du_ref, dga_ref, dgs_ref, cs_ref, lo_ref, hi_ref, dz_ref, scr = refs[3 * N_GROUPS:]
        tabs_ = (cs_ref[...], lo_ref[...], hi_ref[...])
        for part in range(3):
            for g, dil in enumerate(GROUP_DILATIONS):
                src = g_refs[3 * g + part]
                for h in range(HEADS_PER_GROUP):
                    if dil == 1:
                        t = src[:, h * HEAD_DIM:(h + 1) * HEAD_DIM].astype(F32)
                    else:
                        t = _collect(scr, h, src, dil, h * HEAD_DIM)
                    if part < 2:
                        t = _rope_t(t, *tabs_)
                    if part == 0:
                        t = t * scale
                    col = part * N_GROUPS * GROUP_WIDTH + g * GROUP_WIDTH + h * HEAD_DIM
                    dz_ref[:, col:col + HEAD_DIM] = t.astype(BF16)
        dz_ref[:, COL_U:COL_GA] = du_ref[...]
        dz_ref[:, COL_GA:COL_GS] = dga_ref[...]
        dz_ref[:, COL_GS:IN_WIDTH] = dgs_ref[...]

    row = lambda i: (i, 0)
    in_specs, args = [], []
    for (g, dil), trio in zip(enumerate(GROUP_DILATIONS), grads):
        in_specs += [pl.BlockSpec((tm, GROUP_WIDTH), row) if dil == 1 else _view_spec(tm, dil)] * 3
        args += list(trio)
    in_specs += [pl.BlockSpec((tm, SSM_WIDTH), row), pl.BlockSpec((tm, D_MODEL), row), pl.BlockSpec((tm, D_MODEL), row)]
    in_specs += [pl.BlockSpec((tm, HEAD_DIM), row)] * 3
    return pl.pallas_call(
        body, name="dz_layout", grid=(s // tm,), in_specs=in_specs, out_specs=pl.BlockSpec((tm, IN_WIDTH), row),
        out_shape=jax.ShapeDtypeStruct((s, IN_WIDTH), BF16),
        scratch_shapes=[pltpu.VMEM((HEADS_PER_GROUP, tm, HEAD_DIM), F32)], compiler_params=_params(1),
    )(*args, du, dga, dgs, *tabs)


def _discretise(a_re, a_im, log_dt, bt_re, bt_im):
    dt = jnp.exp(log_dt)
    mag = jnp.exp(a_re * dt)
    bar_re = mag * jnp.cos(a_im * dt)
    bar_im = mag * jnp.sin(a_im * dt)
    nr = bar_re - 1.0
    ni = bar_im
    den = a_re * a_re + a_im * a_im
    z_re = (nr * a_re + ni * a_im) / den
    z_im = (ni * a_re - nr * a_im) / den
    bb_re = z_re[:, None, :] * bt_re - z_im[:, None, :] * bt_im
    bb_im = z_re[:, None, :] * bt_im + z_im[:, None, :] * bt_re
    return bar_re, bar_im, bb_re, bb_im


def _ssm_prep(a_re, a_im, log_dt, bt_re, bt_im):
    def body(ar, ai, ld, br, bi, o_lr, o_li, o_br, o_bi):
        lr, li, bbr, bbi = _discretise(ar[...], ai[...], ld[...], br[...], bi[...])
        o_lr[...] = lr
        o_li[...] = li
        o_br[...] = bbr
        o_bi[...] = bbi

    sm = jax.ShapeDtypeStruct((SSM_GROUPS, SSM_STATE), F32)
    bg = jax.ShapeDtypeStruct((SSM_GROUPS, SSM_GROUP, SSM_STATE), F32)
    return pl.pallas_call(body, name="ssm_prep", out_shape=[sm, sm, bg, bg])(a_re, a_im, log_dt, bt_re, bt_im)


def _ssm_param_bwd(a_re, a_im, log_dt, bt_re, bt_im, d_lr, d_li, d_bbr, d_bbi):
    def body(ar, ai, ld, br, bi, g_lr, g_li, g_br, g_bi, o_ar, o_ai, o_ld, o_br, o_bi):
        _, vjp = jax.vjp(_discretise, ar[...], ai[...], ld[...], br[...], bi[...])
        d_ar, d_ai, d_ld, d_br, d_bi = vjp((g_lr[...], g_li[...], g_br[...], g_bi[...]))
        o_ar[...] = d_ar
        o_ai[...] = d_ai
        o_ld[...] = d_ld
        o_br[...] = d_br
        o_bi[...] = d_bi

    sm = jax.ShapeDtypeStruct((SSM_GROUPS, SSM_STATE), F32)
    col = jax.ShapeDtypeStruct((SSM_GROUPS, 1), F32)
    bg = jax.ShapeDtypeStruct((SSM_GROUPS, SSM_GROUP, SSM_STATE), F32)
    return pl.pallas_call(body, name="ssm_param_bwd", out_shape=[sm, sm, col, bg, bg])(
        a_re, a_im, log_dt, bt_re, bt_im, d_lr, d_li, d_bbr, d_bbi)


def _block_diag(t, rows_per, cols_per):
    t4 = t.reshape(SSM_SUPER, 8, rows_per, cols_per)
    eye = jnp.eye(8, dtype=t.dtype)
    return jnp.einsum("bgrc,gh->bgrhc", t4, eye).reshape(SSM_SUPER, 8 * rows_per, 8 * cols_per)


def _block_diag_t(dense, rows_per, cols_per):
    t = dense.reshape(SSM_SUPER, 8, rows_per, 8, cols_per)
    eye = jnp.eye(8, dtype=dense.dtype)
    return jnp.einsum("bgrhc,gh->bgrc", t, eye).reshape(SSM_GROUPS, rows_per, cols_per)


def _gelu(v):
    c = math.sqrt(2.0 / math.pi)
    return 0.5 * v * (1.0 + jnp.tanh(c * (v + 0.044715 * v * v * v)))


def _gelu_grad(v):
    c = math.sqrt(2.0 / math.pi)
    t = jnp.tanh(c * (v + 0.044715 * v * v * v))
    return 0.5 * (1.0 + t) + 0.5 * v * (1.0 - t * t) * c * (1.0 + 3.0 * 0.044715 * v * v)


SUB = 8


SCAN_STEPS = (1, 2, 4)
N_SCAN_TABLES = 2 + 2 * len(SCAN_STEPS)


def _scan_tables(tab_ref, lam_re, lam_im, reverse, conj):
    lr = lam_re
    li = -lam_im if conj else lam_im
    powers = [(lr, li)]
    for _ in range(SUB - 1):
        pr, pi = powers[-1]
        powers.append((pr * lr - pi * li, pr * li + pi * lr))
    row = lax.broadcasted_iota(jnp.int32, (SUB, N_STATE), 0)
    if reverse:
        row = SUB - 1 - row
    wide = lambda v: jnp.broadcast_to(v, (SUB, N_STATE))
    p_re = jnp.zeros((SUB, N_STATE), F32)
    p_im = jnp.zeros((SUB, N_STATE), F32)
    for j in range(SUB):
        p_re = jnp.where(row == j, wide(powers[j][0]), p_re)
        p_im = jnp.where(row == j, wide(powers[j][1]), p_im)
    tab_ref[0] = p_re
    tab_ref[1] = p_im
    for idx, k in enumerate(SCAN_STEPS):
        tab_ref[2 + 2 * idx] = jnp.where(row >= k, wide(powers[k - 1][0]), 0.0)
        tab_ref[3 + 2 * idx] = jnp.where(row >= k, wide(powers[k - 1][1]), 0.0)


def _scan_rows(g_re_ref, g_im_ref, tab_ref, carry, n_rows, reverse):
    last = 0 if reverse else SUB - 1

    def tile_step(tt, state):
        cr, ci = state
        t8 = (n_rows // SUB - 1 - tt) if reverse else tt
        start = pl.multiple_of(t8 * SUB, SUB)
        xr = g_re_ref[pl.ds(start, SUB), :]
        xi = g_im_ref[pl.ds(start, SUB), :]
        for idx, k in enumerate(SCAN_STEPS):
            mr = tab_ref[2 + 2 * idx]
            mi = tab_ref[3 + 2 * idx]
            shift = SUB - k if reverse else k
            sr = pltpu.roll(xr, shift, 0)
            si = pltpu.roll(xi, shift, 0)
            xr, xi = xr + (mr * sr - mi * si), xi + (mr * si + mi * sr)
        pr = tab_ref[0]
        pi = tab_ref[1]
        xr, xi = xr + (pr * cr - pi * ci), xi + (pr * ci + pi * cr)
        g_re_ref[pl.ds(start, SUB), :] = xr
        g_im_ref[pl.ds(start, SUB), :] = xi
        return (jnp.broadcast_to(xr[last:last + 1, :], (SUB, N_STATE)),
                jnp.broadcast_to(xi[last:last + 1, :], (SUB, N_STATE)))

    return lax.fori_loop(0, n_rows // SUB, tile_step, carry)


def _ssm_fwd(z, b_re, b_im, c_re, c_im, lam_re, lam_im, d_skip, chunk):
    s = z.shape[0]

    def body(u_ref, bre, bim, cre, cim, lre, lim, dsk, hre_ref, him_ref, ys_ref, yg_ref, car_re, car_im, tabs):
        i = pl.program_id(0)

        @pl.when(i == 0)
        def _():
            car_re[...] = jnp.zeros_like(car_re)
            car_im[...] = jnp.zeros_like(car_im)
            _scan_tables(tabs, lre[...], lim[...], False, False)

        u = u_ref[...]
        for b in range(SSM_SUPER):
            ub = u[:, b * 128:(b + 1) * 128]
            st = slice(b * 512, (b + 1) * 512)
            hre_ref[:, st] = _dot(ub, bre[b], "nn")
            him_ref[:, st] = _dot(ub, bim[b], "nn")
        sr, si = _scan_rows(hre_ref, him_ref, tabs, (car_re[...], car_im[...]), chunk, False)
        car_re[...] = sr
        car_im[...] = si
        uf = u.astype(F32)
        for b in range(SSM_SUPER):
            st = slice(b * 512, (b + 1) * 512)
            ch = slice(b * 128, (b + 1) * 128)
            y = _dot(hre_ref[:, st].astype(BF16), cre[b], "nn") - _dot(him_ref[:, st].astype(BF16), cim[b], "nn")
            y = y + dsk[:, ch] * uf[:, ch]
            ys_ref[:, ch] = y
            yg_ref[:, ch] = _gelu(y).astype(BF16)

    full3 = lambda i: (0, 0, 0)
    full2 = lambda i: (0, 0)
    row = lambda i: (i, 0)
    u_col = COL_U // SSM_WIDTH
    return pl.pallas_call(
        body, name="ssm_fwd", grid=(s // chunk,),
        in_specs=[pl.BlockSpec((chunk, SSM_WIDTH), lambda i: (i, u_col)),
                  pl.BlockSpec((SSM_SUPER, 128, 512), full3), pl.BlockSpec((SSM_SUPER, 128, 512), full3),
                  pl.BlockSpec((SSM_SUPER, 512, 128), full3), pl.BlockSpec((SSM_SUPER, 512, 128), full3),
                  pl.BlockSpec((1, N_STATE), full2), pl.BlockSpec((1, N_STATE), full2), pl.BlockSpec((1, SSM_WIDTH), full2)],
        out_specs=[pl.BlockSpec((chunk, N_STATE), row), pl.BlockSpec((chunk, N_STATE), row),
                   pl.BlockSpec((chunk, SSM_WIDTH), row), pl.BlockSpec((chunk, SSM_WIDTH), row)],
        out_shape=[jax.ShapeDtypeStruct((s, N_STATE), F32), jax.ShapeDtypeStruct((s, N_STATE), F32),
                   jax.ShapeDtypeStruct((s, SSM_WIDTH), F32), jax.ShapeDtypeStruct((s, SSM_WIDTH), BF16)],
        scratch_shapes=[pltpu.VMEM((SUB, N_STATE), F32), pltpu.VMEM((SUB, N_STATE), F32),
                        pltpu.VMEM((N_SCAN_TABLES, SUB, N_STATE), F32)],
        compiler_params=_params(1),
    )(z, b_re, b_im, c_re, c_im, lam_re, lam_im, d_skip)


def _ssm_bwd(dys, z, h_re, h_im, b_re, b_im, c_re, c_im, lam_re, lam_im, d_skip, chunk):
    s = z.shape[0]
    n_chunks = s // chunk

    def body(dy_ref, u_ref, hre_ref, him_ref, hpr_ref, hpi_ref, bre, bim, cre, cim, lre, lim, dsk,
             du_ref, dlr_ref, dli_ref, dbr_ref, dbi_ref, dcr_ref, dci_ref, dd_ref, are, aim, car_re, car_im, tabs):
        i = pl.program_id(0)
        n = n_chunks - 1 - i

        @pl.when(i == 0)
        def _():
            car_re[...] = jnp.zeros_like(car_re)
            car_im[...] = jnp.zeros_like(car_im)
            _scan_tables(tabs, lre[...], lim[...], True, True)
            for r in (dlr_ref, dli_ref, dbr_ref, dbi_ref, dcr_ref, dci_ref, dd_ref):
                r[...] = jnp.zeros_like(r)

        dy = dy_ref[...]
        dyb = dy.astype(BF16)
        u = u_ref[...]
        for b in range(SSM_SUPER):
            ch = slice(b * 128, (b + 1) * 128)
            st = slice(b * 512, (b + 1) * 512)
            are[:, st] = _dot(dyb[:, ch], cre[b], "nt")
            aim[:, st] = -_dot(dyb[:, ch], cim[b], "nt")
        sr, si = _scan_rows(are, aim, tabs, (car_re[...], car_im[...]), chunk, True)
        car_re[...] = sr
        car_im[...] = si
        row_id = lax.broadcasted_iota(jnp.int32, (chunk, N_STATE), 0)
        top_scale = jnp.where(n > 0, 1.0, 0.0)
        h_r = hre_ref[...]
        h_i = him_ref[...]
        hp_r = jnp.where(row_id == 0, hpr_ref[SUB - 1:SUB, :] * top_scale, pltpu.roll(h_r, 1, 0))
        hp_i = jnp.where(row_id == 0, hpi_ref[SUB - 1:SUB, :] * top_scale, pltpu.roll(h_i, 1, 0))
        a_r = are[...]
        a_i = aim[...]
        dlr_ref[...] += jnp.sum(a_r * hp_r + a_i * hp_i, axis=0, keepdims=True)
        dli_ref[...] += jnp.sum(a_i * hp_r - a_r * hp_i, axis=0, keepdims=True)
        dd_ref[...] += jnp.sum(dy * u.astype(F32), axis=0, keepdims=True)
        a_rb = a_r.astype(BF16)
        a_ib = a_i.astype(BF16)
        h_rb = h_r.astype(BF16)
        h_ib = h_i.astype(BF16)
        for b in range(SSM_SUPER):
            ch = slice(b * 128, (b + 1) * 128)
            st = slice(b * 512, (b + 1) * 512)
            dbr_ref[b] += _dot(u[:, ch], a_rb[:, st], "tn")
            dbi_ref[b] += _dot(u[:, ch], a_ib[:, st], "tn")
            dcr_ref[b] += _dot(h_rb[:, st], dyb[:, ch], "tn")
            dci_ref[b] += -_dot(h_ib[:, st], dyb[:, ch], "tn")
            du = _dot(a_rb[:, st], bre[b], "nt") + _dot(a_ib[:, st], bim[b], "nt") + dsk[:, ch] * dy[:, ch]
            du_ref[:, ch] = du.astype(du_ref.dtype)

    full3 = lambda i: (0, 0, 0)
    full2 = lambda i: (0, 0)
    rev = lambda i: (n_chunks - 1 - i, 0)
    above = lambda i: (jnp.maximum((n_chunks - 1 - i) * (chunk // SUB) - 1, 0), 0)
    u_col = COL_U // SSM_WIDTH
    b_spec = pl.BlockSpec((SSM_SUPER, 128, 512), full3)
    c_spec = pl.BlockSpec((SSM_SUPER, 512, 128), full3)
    vec = pl.BlockSpec((1, N_STATE), full2)
    return pl.pallas_call(
        body, name="ssm_bwd", grid=(n_chunks,),
        in_specs=[pl.BlockSpec((chunk, SSM_WIDTH), rev),
                  pl.BlockSpec((chunk, SSM_WIDTH), lambda i: (n_chunks - 1 - i, u_col)),
                  pl.BlockSpec((chunk, N_STATE), rev), pl.BlockSpec((chunk, N_STATE), rev),
                  pl.BlockSpec((SUB, N_STATE), above), pl.BlockSpec((SUB, N_STATE), above),
                  b_spec, b_spec, c_spec, c_spec, vec, vec, pl.BlockSpec((1, SSM_WIDTH), full2)],
        out_specs=[pl.BlockSpec((chunk, SSM_WIDTH), rev), vec, vec, b_spec, b_spec, c_spec, c_spec,
                   pl.BlockSpec((1, SSM_WIDTH), full2)],
        out_shape=[jax.ShapeDtypeStruct((s, SSM_WIDTH), BF16),
                   jax.ShapeDtypeStruct((1, N_STATE), F32), jax.ShapeDtypeStruct((1, N_STATE), F32),
                   jax.ShapeDtypeStruct((SSM_SUPER, 128, 512), F32), jax.ShapeDtypeStruct((SSM_SUPER, 128, 512), F32),
                   jax.ShapeDtypeStruct((SSM_SUPER, 512, 128), F32), jax.ShapeDtypeStruct((SSM_SUPER, 512, 128), F32),
                   jax.ShapeDtypeStruct((1, SSM_WIDTH), F32)],
        scratch_shapes=[pltpu.VMEM((chunk, N_STATE), F32), pltpu.VMEM((chunk, N_STATE), F32),
                        pltpu.VMEM((SUB, N_STATE), F32), pltpu.VMEM((SUB, N_STATE), F32),
                        pltpu.VMEM((N_SCAN_TABLES, SUB, N_STATE), F32)],
        compiler_params=_params(1),
    )(dys, z, h_re, h_im, h_re, h_im, b_re, b_im, c_re, c_im, lam_re, lam_im, d_skip)


def _local_step(x, p, pos, tgt, sm, wts):
    s = x.shape[0]
    tm = min(512, s)
    ts = min(1024, s)
    chunk = min(256, s)
    ni = s // tm
    nk = s // ts
    w_in, w_ap, w_ga, w_gb, w_out, w_fg, w_fu, w_fd, w_pg, w_pp = (
        wts[k] for k in ("w_in", "w_attn_proj", "w_glu_a", "w_glu_b", "w_out", "w_ffn_gate", "w_ffn_up", "w_ffn_down",
                         "w_ple_gate", "w_ple_proj"))
    w_out2 = w_out.reshape(D_MODEL, D_MODEL)
    w_pg2 = w_pg.reshape(D_MODEL, D_MODEL)
    g_mix, g_ffn, g_final = sm["g_mix"], sm["g_ffn"], sm["g_final"]
    rowblk, rowmap = _rows(tm, D_MODEL)
    vec1k = ((1, D_MODEL), lambda *_: (0, 0))

    (n1,) = _ew("rms_mix", (ni,), [(x, rowblk, rowmap), (g_mix, *vec1k)], [((s, D_MODEL), BF16, rowblk, rowmap)],
                lambda pids, h, g: ((_rms_fwd_tile(h, g),), ()))

    half_in = IN_WIDTH // 8
    tmb = min(1024, s)
    nib = s // tmb
    (z,) = _mm("in_proj", (nib, 8, 1),
               [(n1, (tmb, D_MODEL), lambda i, j, k: (i, 0), w_in, (None, D_MODEL, half_in), lambda i, j, k: (j // 2, 0, j % 2))],
               "nn", [((s, IN_WIDTH), BF16, (tmb, half_in), lambda i, j, k: (i, j))], j_outer=True)

    inv = ROPE_THETA ** (-jnp.arange(ROPE_HALF, dtype=F32) * 2.0 / ROPE_DIM)
    inv_row = jnp.concatenate([inv, inv, jnp.zeros((HEAD_DIM - ROPE_DIM,), F32)]).reshape(1, HEAD_DIM)
    tabs = _rope_tables(pos.astype(F32).reshape(s, 1), inv_row, tm)

    qk0, qkv_views = _qkv_layout(z, tabs, tm)
    v0_col = (2 * N_GROUPS * GROUP_WIDTH) // GROUP_WIDTH
    group_in = [((qk0, qk0, z), (0, 1, v0_col))] + [(trio, (0, 0, 0)) for trio in qkv_views]
    fwd_out = [_attn_fwd(*arrs, dil, cols3) for (arrs, cols3), dil in zip(group_in, GROUP_DILATIONS)]
    attn, lse, merged_views = _attn_merge(fwd_out[0][0], fwd_out[0][1], fwd_out[1:], tm)

    def chip_cols(parts):
        return (jnp.concatenate(parts, axis=1),), ()

    def proj_cols(name, a, width, w):
        blk = (None, width, 256)
        pairs = [(a, (tmb, width), lambda i, j, k: (i, 0), w, blk, lambda i, j, k: (0, 0, 0))]
        pairs += [(None, None, None, w, blk, (lambda i, j, k, q=q: (q, 0, 0))) for q in range(1, N_CHIPS)]
        return _mm(name, (nib, 1, 1), pairs, "nn", [((s, D_MODEL), BF16, (tmb, D_MODEL), lambda i, j, k: (i, 0))],
                   epilogue=chip_cols, sum_pairs=False)[0]

    def proj512(name, a, w):
        return proj_cols(name, a, GROUP_WIDTH, w)

    attn_d = proj512("attn_proj", attn, w_ap)

    bt_re = jnp.transpose(sm["b_re"], (0, 2, 1))
    bt_im = jnp.transpose(sm["b_im"], (0, 2, 1))
    log_dt_col = sm["log_dt"].reshape(SSM_GROUPS, 1)
    lam_re, lam_im, bbt_re, bbt_im = _ssm_prep(sm["a_re"], sm["a_im"], log_dt_col, bt_re, bt_im)
    b_re_m = _block_diag(bbt_re, SSM_GROUP, SSM_STATE).astype(BF16)
    b_im_m = _block_diag(bbt_im, SSM_GROUP, SSM_STATE).astype(BF16)
    c_re_m = _block_diag(jnp.transpose(sm["c_re"], (0, 2, 1)), SSM_STATE, SSM_GROUP).astype(BF16)
    c_im_m = _block_diag(jnp.transpose(sm["c_im"], (0, 2, 1)), SSM_STATE, SSM_GROUP).astype(BF16)
    lam_re_row = lam_re.reshape(1, N_STATE)
    lam_im_row = lam_im.reshape(1, N_STATE)
    d_skip_row = sm["d_skip"].reshape(1, SSM_WIDTH)
    h_re, h_im, ys, yg = _ssm_fwd(z, b_re_m, b_im_m, c_re_m, c_im_m, lam_re_row, lam_im_row, d_skip_row, chunk)

    pa = proj512("glu_a", yg, w_ga)
    pb = proj512("glu_b", yg, w_gb)

    ga_blk = ((tm, D_MODEL), lambda i: (i, COL_GA // D_MODEL))
    gs_blk = ((tm, D_MODEL), lambda i: (i, COL_GS // D_MODEL))

    def mix_fn(pids, ga, gs, ad, a, b):
        ga, gs, ad, a, b = (t.astype(F32) for t in (ga, gs, ad, a, b))
        return (_sig(ga) * ad + _sig(gs) * (a * _sig(b)),), ()

    (mix,) = _ew("gate_mix", (ni,), [(z, *ga_blk), (z, *gs_blk), (attn_d, rowblk, rowmap), (pa, rowblk, rowmap),
                                     (pb, rowblk, rowmap)], [((s, D_MODEL), BF16, rowblk, rowmap)], mix_fn)

    def out_epi(acc, xr, g):
        h1 = acc + xr
        return (h1, _rms_fwd_tile(h1, g)), ()

    m3 = lambda i, j, k: (i, 0)
    w3 = lambda i, j, k: (0, 0)
    h1, n2 = _mm("out_proj", (nib, 1, 1), [(mix, (tmb, D_MODEL), m3, w_out2, (D_MODEL, D_MODEL), w3)], "nn",
                 [((s, D_MODEL), F32, (tmb, D_MODEL), m3), ((s, D_MODEL), BF16, (tmb, D_MODEL), m3)],
                 epilogue=out_epi, extras=[(x, (tmb, D_MODEL), m3), (g_ffn, (1, D_MODEL), w3)])

    ffq = (None, tm, D_FF_Q)
    ffq_map = lambda i, j, k: (j, i, 0)

    def ffn_in_epi(parts):
        gt, u_ = parts
        return (gt, u_, gt * _sig(gt) * u_), ()

    w_ffq = (None, D_MODEL, D_FF_Q)
    w_ffq_j = lambda i, j, k: (j, 0, 0)
    gate, up, act = _mm("ffn_gate_up", (ni, N_CHIPS, 1),
                        [(n2, (tm, D_MODEL), m3, w_fg, w_ffq, w_ffq_j), (None, None, None, w_fu, w_ffq, w_ffq_j)], "nn",
                        [((N_CHIPS, s, D_FF_Q), BF16, ffq, ffq_map)] * 3, epilogue=ffn_in_epi, j_outer=True,
                        sum_pairs=False)

    (h2,) = _mm("ffn_down", (nib, 1, 1),
                [(act, (None, tmb, D_FF_Q), (lambda i, j, k, q=q: (q, i, 0)), w_fd, (None, D_FF_Q, D_MODEL),
                  (lambda i, j, k, q=q: (q, 0, 0))) for q in range(N_CHIPS)], "nn",
                [((s, D_MODEL), F32, (tmb, D_MODEL), m3)], epilogue=lambda acc, hr: ((acc + hr,), ()),
                extras=[(h1, (tmb, D_MODEL), m3)])

    pp = proj_cols("ple_proj", p, PLE_DIM, w_pp)

    def ple_epi(acc, hr, ppr):
        return (acc, hr + _sig(acc) * ppr.astype(F32)), ()

    gl, h3 = _mm("ple_gate", (nib, 1, 1), [(h2, (tmb, D_MODEL), m3, w_pg2, (D_MODEL, D_MODEL), w3)], "nn",
                 [((s, D_MODEL), BF16, (tmb, D_MODEL), m3), ((s, D_MODEL), F32, (tmb, D_MODEL), m3)],
                 epilogue=ple_epi, extras=[(h2, (tmb, D_MODEL), m3), (pp, (tmb, D_MODEL), m3)])

    def head_fn(pids, h, t, g, g_, ppr):
        r = lax.rsqrt(jnp.mean(h * h, axis=-1, keepdims=True) + EPS)
        hhat = h * r
        diff = hhat * g - t
        loss = 0.5 * jnp.sum(jnp.mean(diff * diff, axis=-1, keepdims=True))
        dy = diff * (1.0 / D_MODEL)
        gy = dy * g
        dh = r * (gy - hhat * jnp.mean(gy * hhat, axis=-1, keepdims=True))
        sg = _sig(g_.astype(F32))
        return ((dh, dh * ppr.astype(F32) * sg * (1.0 - sg), dh * sg),
                (jnp.full((SUB, 128), loss, F32), jnp.sum(dy * hhat, axis=0, keepdims=True)))

    dh3, dgl, dpp, loss_acc, dg_final = _ew(
        "loss_head", (ni,),
        [(h3, rowblk, rowmap), (tgt, rowblk, rowmap), (g_final, *vec1k), (gl, rowblk, rowmap), (pp, rowblk, rowmap)],
        [((s, D_MODEL), F32, rowblk, rowmap), ((s, D_MODEL), BF16, rowblk, rowmap), ((s, D_MODEL), BF16, rowblk, rowmap)],
        head_fn, acc_outs=[((SUB, 128), F32), ((1, D_MODEL), F32)])

    def wgrad(name, a, a_block, a_imap, b, b_block, b_imap, out_shape, out_block, out_imap, nj, acc_shape):
        return _mm(name, (1, nj, nk), [(a, a_block, a_imap, b, b_block, b_imap)], "tn",
                   [(out_shape, F32, out_block, out_imap)], acc_shape=acc_shape)[0]

    tk0 = lambda i, j, k: (k, 0)
    tkj = lambda i, j, k: (k, j)
    def wgrad_cols(name, a, width, dy_):
        def split(acc):
            return (jnp.stack([acc[:, q * 256:(q + 1) * 256] for q in range(N_CHIPS)], axis=0),), ()

        return _mm(name, (1, 1, nk), [(a, (ts, width), tk0, dy_, (ts, D_MODEL), tk0)], "tn",
                   [((N_CHIPS, width, 256), F32, (N_CHIPS, width, 256), lambda i, j, k: (0, 0, 0))], epilogue=split,
                   acc_shape=(width, D_MODEL))[0]

    d_w_pp = wgrad_cols("d_ple_proj", p, PLE_DIM, dpp)
    d_w_pg = wgrad("d_ple_gate", h2, (ts, D_MODEL), tk0, dgl, (ts, D_MODEL), tk0, (D_MODEL, D_MODEL),
                   (D_MODEL, D_MODEL), w3, 1, (D_MODEL, D_MODEL))

    (dh2,) = _mm("ple_gate_bwd", (nib, 1, 1), [(dgl, (tmb, D_MODEL), m3, w_pg2, (D_MODEL, D_MODEL), w3)], "nt",
                 [((s, D_MODEL), F32, (tmb, D_MODEL), m3)], epilogue=lambda acc, d_: ((acc + d_,), ()),
                 extras=[(dh3, (tmb, D_MODEL), m3)])

    def ffn_bwd_epi(acc, gt, u_):
        gt, u_ = gt.astype(F32), u_.astype(F32)
        sg = _sig(gt)
        return (acc * u_ * (sg * (1.0 + gt * (1.0 - sg))), acc * gt * sg), ()

    ffq_big = (None, tmb, D_FF_Q)
    dgate, dup = _mm("ffn_down_bwd", (nib, N_CHIPS, 1),
                     [(dh2, (tmb, D_MODEL), m3, w_fd, (None, D_FF_Q, D_MODEL), lambda i, j, k: (j, 0, 0))], "nt",
                     [((N_CHIPS, s, D_FF_Q), BF16, ffq_big, ffq_map)] * 2, epilogue=ffn_bwd_epi,
                     extras=[(gate, ffq_big, ffq_map), (up, ffq_big, ffq_map)])

    ffq_t = (None, ts, D_FF_Q)
    ffq_tmap = lambda i, j, k: (j, k, 0)
    blk_j = lambda i, j, k: (j, 0, 0)
    d_w_fd = wgrad("d_ffn_down", act, ffq_t, ffq_tmap, dh2, (ts, D_MODEL), tk0, (N_CHIPS, D_FF_Q, D_MODEL),
                   (None, D_FF_Q, D_MODEL), blk_j, N_CHIPS, (D_FF_Q, D_MODEL))
    d_w_fg = wgrad("d_ffn_gate", n2, (ts, D_MODEL), tk0, dgate, ffq_t, ffq_tmap, (N_CHIPS, D_MODEL, D_FF_Q),
                   (None, D_MODEL, D_FF_Q), blk_j, N_CHIPS, (D_MODEL, D_FF_Q))
    d_w_fu = wgrad("d_ffn_up", n2, (ts, D_MODEL), tk0, dup, ffq_t, ffq_tmap, (N_CHIPS, D_MODEL, D_FF_Q),
                   (None, D_MODEL, D_FF_Q), blk_j, N_CHIPS, (D_MODEL, D_FF_Q))

    def norm_bwd_epi(acc, h, d_res, g):
        dh, dg = _rms_bwd_tile(acc, h, g)
        return (d_res + dh,), (dg,)

    ffq_k = lambda i, j, k: (k, i, 0)
    blk_k = lambda i, j, k: (k, 0, 0)
    ffq_b = (None, tmb, D_FF_Q)
    dh1, dg_ffn = _mm("ffn_in_bwd", (nib, 1, N_CHIPS),
                      [(dgate, ffq_b, ffq_k, w_fg, (None, D_MODEL, D_FF_Q), blk_k),
                       (dup, ffq_b, ffq_k, w_fu, (None, D_MODEL, D_FF_Q), blk_k)], "nt",
                      [((s, D_MODEL), F32, (tmb, D_MODEL), m3)], epilogue=norm_bwd_epi,
                      extras=[(h1, (tmb, D_MODEL), m3), (dh2, (tmb, D_MODEL), m3), (g_ffn, (1, D_MODEL), w3)],
                      acc_outs=[((1, D_MODEL), F32)], acc_shape=(tmb, D_MODEL))

    d_w_out = wgrad("d_out_proj", mix, (ts, D_MODEL), tk0, dh1, (ts, D_MODEL), tk0, (D_MODEL, D_MODEL),
                    (D_MODEL, D_MODEL), w3, 1, (D_MODEL, D_MODEL))

    def mix_bwd_epi(dm, ga, gs, ad, a, b):
        ga, gs, ad, a, b = (t.astype(F32) for t in (ga, gs, ad, a, b))
        s_a, s_s, s_b = _sig(ga), _sig(gs), _sig(b)
        d_ssm = dm * s_s
        return (dm * ad * s_a * (1.0 - s_a), dm * (a * s_b) * s_s * (1.0 - s_s), dm * s_a, d_ssm * s_b,
                d_ssm * a * s_b * (1.0 - s_b)), ()

    tile_m = (tm, D_MODEL)
    dga, dgs, dattn_d, dpa, dpb = _mm(
        "out_proj_bwd", (ni, 1, 1), [(dh1, tile_m, m3, w_out2, (D_MODEL, D_MODEL), w3)], "nt",
        [((s, D_MODEL), BF16, tile_m, m3)] * 5, epilogue=mix_bwd_epi,
        extras=[(z, tile_m, lambda i, j, k: (i, COL_GA // D_MODEL)), (z, tile_m, lambda i, j, k: (i, COL_GS // D_MODEL)),
                (attn_d, tile_m, m3), (pa, tile_m, m3), (pb, tile_m, m3)])

    d_w_ap = wgrad_cols("d_attn_proj", attn, GROUP_WIDTH, dattn_d)
    d_w_ga = wgrad_cols("d_glu_a", yg, GROUP_WIDTH, dpa)
    d_w_gb = wgrad_cols("d_glu_b", yg, GROUP_WIDTH, dpb)

    ik = lambda i, j, k: (i, k)

    def cols_bwd(dy_, w):
        return [(dy_, (tmb, 256), (lambda i, j, k, q=q: (i, q)), w, (None, GROUP_WIDTH, 256),
                 (lambda i, j, k, q=q: (q, 0, 0))) for q in range(N_CHIPS)]

    (dattn,) = _mm("attn_proj_bwd", (nib, 1, 1), cols_bwd(dattn_d, w_ap), "nt",
                   [((s, GROUP_WIDTH), BF16, (tmb, GROUP_WIDTH), m3)])

    (dys,) = _mm("glu_bwd", (nib, 1, 1), cols_bwd(dpa, w_ga) + cols_bwd(dpb, w_gb), "nt",
                 [((s, GROUP_WIDTH), F32, (tmb, GROUP_WIDTH), m3)],
                 epilogue=lambda acc, y_: ((acc * _gelu_grad(y_),), ()),
                 extras=[(ys, (tmb, GROUP_WIDTH), m3)])

    du, d_lr, d_li, d_bre, d_bim, d_cre, d_cim, d_dskip = _ssm_bwd(
        dys, z, h_re, h_im, b_re_m, b_im_m, c_re_m, c_im_m, lam_re_row, lam_im_row, d_skip_row, chunk)

    dattn_views = _to_views(dattn, tm)
    bwd_in = [(dattn, attn, lse)] + [(dv_, ov_, lv_) for dv_, (ov_, lv_) in zip(dattn_views, merged_views)]
    qkv_grads = [_attn_bwd(*arrs, *dol, dil, cols3)
                 for (arrs, cols3), dol, dil in zip(group_in, bwd_in, GROUP_DILATIONS)]
    dz = _dz_layout(qkv_grads, du, dga, dgs, tabs, tm)

    kb = lambda i, j, k: (k // 2, 0, k % 2)
    grad_x, dg_mix = _mm("in_proj_bwd", (nib, 1, 8), [(dz, (tmb, half_in), ik, w_in, (None, D_MODEL, half_in), kb)], "nt",
                         [((s, D_MODEL), F32, (tmb, D_MODEL), m3)], epilogue=norm_bwd_epi,
                         extras=[(x, (tmb, D_MODEL), m3), (dh1, (tmb, D_MODEL), m3), (g_mix, (1, D_MODEL), w3)],
                         acc_outs=[((1, D_MODEL), F32)], acc_shape=(tmb, D_MODEL))
    ts_in = min(2048, s)
    (d_w_in,) = _mm("d_in_proj", (1, 8, s // ts_in), [(n1, (ts_in, D_MODEL), tk0, dz, (ts_in, half_in), tkj)], "tn",
                    [((N_CHIPS, D_MODEL, IN_WIDTH // N_CHIPS), F32, (None, D_MODEL, half_in),
                      lambda i, j, k: (j // 2, 0, j % 2))], acc_shape=(D_MODEL, half_in))

    d_bbt_re = _block_diag_t(d_bre, SSM_GROUP, SSM_STATE)
    d_bbt_im = _block_diag_t(d_bim, SSM_GROUP, SSM_STATE)
    d_a_re, d_a_im, d_log_dt, d_bt_re, d_bt_im = _ssm_param_bwd(
        sm["a_re"], sm["a_im"], log_dt_col, bt_re, bt_im,
        d_lr.reshape(SSM_GROUPS, SSM_STATE), d_li.reshape(SSM_GROUPS, SSM_STATE), d_bbt_re, d_bbt_im)
    small = {
        "g_mix": dg_mix, "a_re": d_a_re, "a_im": d_a_im, "log_dt": d_log_dt,
        "b_re": jnp.transpose(d_bt_re, (0, 2, 1)), "b_im": jnp.transpose(d_bt_im, (0, 2, 1)),
        "c_re": jnp.transpose(_block_diag_t(d_cre, SSM_STATE, SSM_GROUP), (0, 2, 1)),
        "c_im": jnp.transpose(_block_diag_t(d_cim, SSM_STATE, SSM_GROUP), (0, 2, 1)),
        "d_skip": d_dskip, "g_ffn": dg_ffn, "g_final": dg_final,
    }
    big = {
        "w_in": d_w_in, "w_attn_proj": d_w_ap, "w_glu_a": d_w_ga, "w_glu_b": d_w_gb,
        "w_out": d_w_out.reshape(N_CHIPS, D_MODEL // N_CHIPS, D_MODEL), "w_ffn_gate": d_w_fg, "w_ffn_up": d_w_fu,
        "w_ffn_down": d_w_fd, "w_ple_gate": d_w_pg.reshape(N_CHIPS, D_MODEL // N_CHIPS, D_MODEL), "w_ple_proj": d_w_pp,
    }
    return loss_acc[0, 0], grad_x, big, small


BIG = ("w_in", "w_attn_proj", "w_glu_a", "w_glu_b", "w_out", "w_ffn_gate", "w_ffn_up", "w_ffn_down", "w_ple_gate",
       "w_ple_proj")
SMALL = ("g_mix", "a_re", "a_im", "log_dt", "b_re", "b_im", "c_re", "c_im", "d_skip", "g_ffn", "g_final")
ANY = pl.BlockSpec(memory_space=pl.ANY)


def _place():
    x, y, c = lax.axis_index("x"), lax.axis_index("y"), lax.axis_index("c")
    chips = [(1 - x, y), (x, 1 - y), (1 - x, 1 - y)]
    return x, y, c, chips


def _remote(src, dst, send_sem, recv_sem, to):
    return pltpu.make_async_remote_copy(src_ref=src, dst_ref=dst, send_sem=send_sem, recv_sem=recv_sem, device_id=to,
                                        device_id_type=MESH)


def _comm_call(name, body, ins, out_shapes, n_sems, aliases=None):
    n_w = len(ins)
    return pl.pallas_call(
        body, name=name, in_specs=[ANY] * n_w, out_specs=[ANY] * len(out_shapes), out_shape=out_shapes,
        scratch_shapes=[pltpu.SemaphoreType.DMA((n,)) for n in n_sems], input_output_aliases=aliases or {},
    )(*ins)


def _gather_weights(bufs):
    n_w = len(bufs)

    def body(*refs):
        outs = refs[n_w:2 * n_w]
        ici_send, ici_recv, d2d_send, d2d_recv = refs[2 * n_w:]
        x, y, c, chips = _place()
        me = 2 * x + y
        sib = (x, y, 1 - c)
        sends = []
        for w in range(n_w):
            for j, (cx, cy) in enumerate(chips):
                k = 3 * w + j
                mine = outs[w].at[me, c]
                cp = _remote(mine, mine, ici_send.at[k], ici_recv.at[k], (cx, cy, c))
                cp.start()
                sends.append(cp)
        for w in range(n_w):
            for j, (cx, cy) in enumerate(chips):
                k = 3 * w + j
                src_chip = 2 * cx + cy
                landed = outs[w].at[src_chip, c]
                _remote(landed, landed, ici_send.at[k], ici_recv.at[k], (cx, cy, c)).wait_recv()
                fwd = _remote(landed, landed, d2d_send.at[k], d2d_recv.at[k], sib)
                fwd.start()
                sends.append(fwd)
        for w in range(n_w):
            for j, (cx, cy) in enumerate(chips):
                k = 3 * w + j
                other = outs[w].at[2 * cx + cy, 1 - c]
                _remote(other, other, d2d_send.at[k], d2d_recv.at[k], sib).wait_recv()
        for cp in sends:
            cp.wait_send()

    out_shapes = [jax.ShapeDtypeStruct(b.shape, b.dtype) for b in bufs]
    return _comm_call("gather_weights", body, bufs, out_shapes, [3 * n_w] * 4, aliases={w: w for w in range(n_w)})


def _pair_exchange(grads):
    n_w = len(grads)

    def body(*refs):
        ins, outs = refs[:n_w], refs[n_w:2 * n_w]
        send, recv = refs[2 * n_w:]
        x, y, c, _ = _place()
        sib = (x, y, 1 - c)
        cps = []
        for w in range(n_w):
            for q in range(N_CHIPS):
                k = N_CHIPS * w + q
                cp = _remote(ins[w].at[q, 1 - c], outs[w].at[q], send.at[k], recv.at[k], sib)
                cp.start()
                cps.append(cp)
        for cp in cps:
            cp.wait()

    out_shapes = [jax.ShapeDtypeStruct((N_CHIPS,) + g.shape[2:], g.dtype) for g in grads]
    return _comm_call("grad_pair_exchange", body, grads, out_shapes, [N_CHIPS * n_w] * 2)


def _chip_exchange(parts):
    n_w = len(parts)

    def body(*refs):
        ins, outs = refs[:n_w], refs[n_w:2 * n_w]
        send, recv = refs[2 * n_w:]
        x, y, c, chips = _place()
        me = 2 * x + y
        cps = []
        for w in range(n_w):
            for j, (cx, cy) in enumerate(chips):
                k = 3 * w + j
                cp = _remote(ins[w].at[2 * cx + cy], outs[w].at[me], send.at[k], recv.at[k], (cx, cy, c))
                cp.start()
                cps.append(cp)
        for w in range(n_w):
            for j, (cx, cy) in enumerate(chips):
                k = 3 * w + j
                got = outs[w].at[2 * cx + cy]
                _remote(got, got, send.at[k], recv.at[k], (cx, cy, c)).wait_recv()
        for cp in cps:
            cp.wait_send()

    out_shapes = [jax.ShapeDtypeStruct(t.shape, t.dtype) for t in parts]
    return _comm_call("grad_chip_exchange", body, parts, out_shapes, [3 * n_w, 3 * n_w])


def _pair_gather(halves):
    n_w = len(halves)

    def body(*refs):
        ins, outs = refs[:n_w], refs[n_w:2 * n_w]
        send, recv = refs[2 * n_w:]
        x, y, c, _ = _place()
        sib = (x, y, 1 - c)
        cps = []
        for w in range(n_w):
            cp = _remote(ins[w], outs[w], send.at[w], recv.at[w], sib)
            cp.start()
            cps.append(cp)
        for cp in cps:
            cp.wait()

    out_shapes = [jax.ShapeDtypeStruct(h.shape, h.dtype) for h in halves]
    return _comm_call("grad_pair_gather", body, halves, out_shapes, [n_w] * 2)


def _all_exchange(vec):
    def body(in_ref, out_ref, send, recv):
        x, y, c, _ = _place()
        me = 4 * x + 2 * y + c
        cps = []
        for k in range(1, 8):
            fx, fy, fc = (k >> 2) & 1, (k >> 1) & 1, k & 1
            to = (x ^ fx, y ^ fy, c ^ fc)
            cp = _remote(in_ref, out_ref.at[me], send.at[k - 1], recv.at[k - 1], to)
            cp.start()
            cps.append(cp)
        for k in range(1, 8):
            fx, fy, fc = (k >> 2) & 1, (k >> 1) & 1, k & 1
            src = 4 * (x ^ fx) + 2 * (y ^ fy) + (c ^ fc)
            got = out_ref.at[src]
            _remote(got, got, send.at[k - 1], recv.at[k - 1], (x ^ fx, y ^ fy, c ^ fc)).wait_recv()
        for cp in cps:
            cp.wait_send()

    return _comm_call("small_all_exchange", body, [vec], [jax.ShapeDtypeStruct((8,) + vec.shape, vec.dtype)], [7, 7])[0]


def _row_tile(r):
    for t in (256, 128, 176, 64, 32, 16, 8):
        if r % t == 0:
            return t
    return r


P_C, P_CHIP, P_DEV = 2, 3, 4


def _cast_into_slot(w2, place):
    r, c = w2.shape
    t = _row_tile(r)
    return _ew("cast_shard", (r // t,), [(w2, (t, c), lambda i, pv: (i, 0))],
               [((N_CHIPS, r, c), BF16, (None, t, c), lambda i, pv: (pv[P_CHIP], i, 0))],
               lambda pids, a: ((a,), ()), place=place)[0]


def _pair_sum(mine, theirs, place):
    _, r, c = theirs.shape
    t = _row_tile(r)
    own = ((None, None, t, c), lambda q, i, pv: (q, pv[P_C], i, 0))
    blk = ((None, t, c), lambda q, i, pv: (q, i, 0))
    return _ew("grad_pair_sum", (N_CHIPS, r // t), [(mine, *own), (theirs, *blk)], [((N_CHIPS, r, c), BF16, *blk)],
               lambda pids, a, b: ((a + b,), ()), place=place)[0]


def _chip_sum(own, got, place):
    _, r, c = own.shape
    t = _row_tile(r)
    ins = []
    for q in range(N_CHIPS):
        ins.append((own, (None, t, c), (lambda i, pv, q=q: (q, i, 0))))
        ins.append((got, (None, t, c), (lambda i, pv, q=q: (jnp.where(pv[P_CHIP] == q, (q + 1) % N_CHIPS, q), i, 0))))

    def fn(pids, *tiles):
        me = pids[0][P_CHIP]
        tot = None
        for q in range(N_CHIPS):
            term = jnp.where(me == q, tiles[2 * q], tiles[2 * q + 1]).astype(F32)
            tot = term if tot is None else tot + term
        return (tot,), ()

    return _ew("grad_chip_sum", (r // t,), ins, [((r, c), F32, (t, c), lambda i, pv: (i, 0))], fn, place=place)[0]


def _adamw_tile(w, g, m, v):
    m = ADAM_B1 * m + (1.0 - ADAM_B1) * g
    v = ADAM_B2 * v + (1.0 - ADAM_B2) * (g * g)
    m_hat = m / (1.0 - ADAM_B1 ** ADAM_STEP)
    v_hat = v / (1.0 - ADAM_B2 ** ADAM_STEP)
    delta = -ADAM_LR * (m_hat / (jnp.sqrt(v_hat) + ADAM_EPS) + ADAM_WD * w)
    return delta, m, v


def _adamw(name, g2, w2, m2, v2):
    r, c = w2.shape
    t = _row_tile(r)
    blk, imap = _rows(t, c)

    def fn(pids, g, w, m, v):
        delta, nm, nv = _adamw_tile(w, g, m, v)
        return (g, delta, nm, nv), ()

    return _ew(name, (r // t,), [(a, blk, imap) for a in (g2, w2, m2, v2)], [((r, c), F32, blk, imap)] * 4, fn)


def _adamw_halves(name, mine, theirs, w2, m2, v2, place):
    r, c = w2.shape
    t = _row_tile(r // 2)
    n_t = (r // 2) // t
    half = ((t, c), lambda h, i, pv: (i, 0))
    whole = ((t, c), lambda h, i, pv: (h * n_t + i, 0))

    def fn(pids, ga, gb, w, m, v):
        g = jnp.where(pids[1] == pids[0][P_C], ga, gb)
        delta, nm, nv = _adamw_tile(w, g, m, v)
        return (g, delta, nm, nv), ()

    return _ew(name, (2, n_t), [(mine, *half), (theirs, *half), (w2, *whole), (m2, *whole), (v2, *whole)],
               [((r, c), F32, *whole)] * 4, fn, place=place)


def _device_sum(own, got, place):
    r, c = own.shape
    t = _row_tile(r)
    ins = [(own, (t, c), lambda i, pv: (i, 0))]
    for q in range(8):
        ins.append((got, (None, t, c), (lambda i, pv, q=q: (jnp.where(pv[P_DEV] == q, (q + 1) % 8, q), i, 0))))

    def fn(pids, mine, *parts):
        me = pids[0][P_DEV]
        tot = None
        for q in range(8):
            term = jnp.where(me == q, mine, parts[q])
            tot = term if tot is None else tot + term
        return (tot,), ()

    return _ew("small_device_sum", (r // t,), ins, [((r, c), F32, (t, c), lambda i, pv: (i, 0))], fn, place=place)[0]


def _pack(parts):
    flat = jnp.concatenate([a.reshape(-1) for a in parts])
    pad = (-flat.shape[0]) % (SUB * 128)
    return jnp.pad(flat, (0, pad)).reshape(-1, 128)


def _unpack(mat, shapes):
    flat = mat.reshape(-1)
    out, off = [], 0
    for shp in shapes:
        n = math.prod(shp)
        out.append(flat[off:off + n].reshape(shp))
        off += n
    return out


def kernel(x, p, positions, g_mix, w_in, a_re, a_im, log_dt, b_re, b_im, c_re, c_im, d_skip, w_attn_proj, w_glu_a, w_glu_b, w_out, g_ffn, w_ffn_gate, w_ffn_up, w_ffn_down, w_ple_gate, w_ple_proj, g_final, loss_target, m_g_mix, m_w_in, m_a_re, m_a_im, m_log_dt, m_b_re, m_b_im, m_c_re, m_c_im, m_d_skip, m_w_attn_proj, m_w_glu_a, m_w_glu_b, m_w_out, m_g_ffn, m_w_ffn_gate, m_w_ffn_up, m_w_ffn_down, m_w_ple_gate, m_w_ple_proj, m_g_final, v_g_mix, v_w_in, v_a_re, v_a_im, v_log_dt, v_b_re, v_b_im, v_c_re, v_c_im, v_d_skip, v_w_attn_proj, v_w_glu_a, v_w_glu_b, v_w_out, v_g_ffn, v_w_ffn_gate, v_w_ffn_up, v_w_ffn_down, v_w_ple_gate, v_w_ple_proj, v_g_final):
    given = dict(locals())
    big_w = {n: given[n] for n in BIG}
    w_mats = {n: big_w[n].reshape(big_w[n].shape[1:]) for n in BIG}

    ax, ay, ac = lax.axis_index("x"), lax.axis_index("y"), lax.axis_index("c")
    place = jnp.stack([ax, ay, ac, 2 * ax + ay, 4 * ax + 2 * ay + ac]).astype(jnp.int32)

    bufs = []
    for n in BIG:
        r, c = w_mats[n].shape
        bufs.append(_cast_into_slot(w_mats[n], place).reshape(N_CHIPS, 2, r // 2, c))
    gathered = _gather_weights(bufs)
    wts = {}
    for n, g in zip(BIG, gathered):
        r, c = w_mats[n].shape
        wts[n] = g.reshape(N_CHIPS, r, c)

    sm = {
        "g_mix": g_mix.reshape(1, D_MODEL), "g_ffn": g_ffn.reshape(1, D_MODEL), "g_final": g_final.reshape(1, D_MODEL),
        "a_re": a_re[0], "a_im": a_im[0], "log_dt": log_dt[0], "b_re": b_re[0], "b_im": b_im[0], "c_re": c_re[0],
        "c_im": c_im[0], "d_skip": d_skip[0],
    }
    s = x.shape[1]
    loss_part, grad_x, big_g, small_g = _local_step(x[0], p[0, 0], positions[0], loss_target[0], sm, wts)

    g5 = []
    for n in BIG:
        r, c = w_mats[n].shape
        g5.append(big_g[n].reshape(N_CHIPS, 2, r // 2, c))
    theirs = _pair_exchange(g5)
    chip_parts = [_pair_sum(g, t, place) for g, t in zip(g5, theirs)]
    chip_got = _chip_exchange(chip_parts)
    halves = [_chip_sum(own, got, place) for own, got in zip(chip_parts, chip_got)]
    other_halves = _pair_gather(halves)

    results = {}
    for n, mine, other in zip(BIG, halves, other_halves):
        r, c = w_mats[n].shape
        shp = big_w[n].shape
        outs = _adamw_halves("adamw_" + n, mine, other, w_mats[n], given["m_" + n].reshape(r, c),
                             given["v_" + n].reshape(r, c), place)
        results[n] = [o.reshape(shp) for o in outs]

    small_shapes = [given[n].shape for n in SMALL]
    vec = _pack([small_g[n] for n in SMALL] + [loss_part.reshape(1)])
    tot = _device_sum(vec, _all_exchange(vec), place)
    n_small = sum(math.prod(shp) for shp in small_shapes)
    loss = tot.reshape(-1)[n_small]
    w_s = _pack([given[n] for n in SMALL])
    m_s = _pack([given["m_" + n] for n in SMALL])
    v_s = _pack([given["v_" + n] for n in SMALL])
    rows_s = w_s.shape[0]
    g_s = tot.reshape(-1)[: rows_s * 128].reshape(rows_s, 128)
    outs_s = _adamw("adamw_small", g_s, w_s, m_s, v_s)
    for kind, mat in enumerate(outs_s):
        for n, arr in zip(SMALL, _unpack(mat, small_shapes)):
            results.setdefault(n, [None] * 4)[kind] = arr

    order = ("g_mix", "w_in", "a_re", "a_im", "log_dt", "b_re", "b_im", "c_re", "c_im", "d_skip", "w_attn_proj", "w_glu_a",
             "w_glu_b", "w_out", "g_ffn", "w_ffn_gate", "w_ffn_up", "w_ffn_down", "w_ple_gate", "w_ple_proj", "g_final")
    out = [loss, grad_x.reshape(1, s, D_MODEL)]
    for kind in range(4):
        out += [results[n][kind] for n in order]
    return tuple(out)
```

```python
import math

import jax
import jax.numpy as jnp
from jax import lax
from jax.experimental import pallas as pl
from jax.experimental.pallas import tpu as pltpu

F32 = jnp.float32
BF16 = jnp.bfloat16

D_MODEL = 1024
HEAD_DIM = 128
HEADS_PER_GROUP = 4
GROUP_WIDTH = HEADS_PER_GROUP * HEAD_DIM
GROUP_DILATIONS = (1, 4, 16)
N_GROUPS = len(GROUP_DILATIONS)
LSE_LANES = 32
LSE_WIDTH = HEADS_PER_GROUP * LSE_LANES
ATTN_BLOCK = 128
ROPE_DIM = 32
ROPE_HALF = 16
ROPE_THETA = 500000.0
SSM_WIDTH = 512
SSM_GROUPS = 32
SSM_GROUP = 16
SSM_STATE = 64
N_STATE = SSM_GROUPS * SSM_STATE
SSM_SUPER = 4
IN_WIDTH = 7168
COL_U = 4608
COL_GA = 5120
COL_GS = 6144
D_FF = 2816
N_CHIPS = 4
D_FF_Q = D_FF // N_CHIPS
PLE_DIM = 256
EPS = 1e-6
ADAM_LR = 0.001
ADAM_B1 = 0.9
ADAM_B2 = 0.999
ADAM_EPS = 1e-08
ADAM_WD = 0.01
ADAM_STEP = 10
NEG_BIG = -1e30
VMEM_LIMIT_BYTES = 56 * 1024 * 1024
MESH = pl.DeviceIdType.MESH

_DIMS = {
    "nn": (((1,), (0,)), ((), ())),
    "nt": (((1,), (1,)), ((), ())),
    "tn": (((0,), (0,)), ((), ())),
}


def _params(n_grid):
    return pltpu.CompilerParams(dimension_semantics=("arbitrary",) * n_grid, vmem_limit_bytes=VMEM_LIMIT_BYTES)


def _sig(v):
    return 1.0 / (1.0 + jnp.exp(-v))


def _dot(a, b, mode):
    return lax.dot_general(a, b, _DIMS[mode], preferred_element_type=F32)


def _mm(name, grid, pairs, mode, outs, epilogue=None, extras=(), acc_outs=(), acc_shape=None, j_outer=False,
        sum_pairs=True, resident_b=False):
    gi, gj, gk = grid
    n_p, n_e, n_o, n_a = len(pairs), len(extras), len(outs), len(acc_outs)
    assert not n_a or gj == 1
    assert sum_pairs or gk == 1

    def order(imap):
        return (lambda j, i, k: imap(i, j, k)) if j_outer else imap

    shared_a = [pr[0] is None for pr in pairs]
    n_in = 2 * n_p - sum(shared_a)

    def body(*refs):
        pair_refs = list(refs[:n_in])
        extra_refs = refs[n_in: n_in + n_e]
        out_refs = refs[n_in + n_e: n_in + n_e + n_o]
        sum_refs = refs[n_in + n_e + n_o: n_in + n_e + n_o + n_a]
        i = pl.program_id(1 if j_outer else 0)
        k = pl.program_id(2)
        part = None if sum_pairs else []
        a = None
        for t in range(n_p):
            if not shared_a[t]:
                a = pair_refs.pop(0)[...].astype(BF16)
            b = pair_refs.pop(0)[...].astype(BF16)
            d = _dot(a, b, mode)
            if sum_pairs:
                part = d if part is None else part + d
            else:
                part.append(d)

        def finish(acc):
            tiles, sums = epilogue(acc, *[e[...] for e in extra_refs]) if epilogue is not None else ((acc,), ())
            for o_ref, tile in zip(out_refs, tiles):
                o_ref[...] = tile.astype(o_ref.dtype)
            if n_a:
                @pl.when(i == 0)
                def _():
                    for s_ref in sum_refs:
                        s_ref[...] = jnp.zeros_like(s_ref)

                for s_ref, s in zip(sum_refs, sums):
                    s_ref[...] += s

        if gk == 1:
            finish(part)
        else:
            acc_ref = refs[-1]

            @pl.when(k == 0)
            def _():
                acc_ref[...] = part

            @pl.when(k > 0)
            def _():
                acc_ref[...] += part

            @pl.when(k == gk - 1)
            def _():
                finish(acc_ref[...])

    in_specs, args = [], []
    for a, a_block, a_imap, b, b_block, b_imap in pairs:
        if a is not None:
            in_specs.append(pl.BlockSpec(a_block, order(a_imap)))
            args.append(a)
        if resident_b:
            in_specs.append(pl.BlockSpec(b_block, order(b_imap), pipeline_mode=pl.Buffered(1)))
        else:
            in_specs.append(pl.BlockSpec(b_block, order(b_imap)))
        args.append(b)
    for e, e_block, e_imap in extras:
        in_specs.append(pl.BlockSpec(e_block, order(e_imap)))
        args.append(e)
    out_shape = [jax.ShapeDtypeStruct(shape, dtype) for shape, dtype, _, _ in outs]
    out_specs = [pl.BlockSpec(block, order(imap)) for _, _, block, imap in outs]
    for shape, dtype in acc_outs:
        out_shape.append(jax.ShapeDtypeStruct(shape, dtype))
        out_specs.append(pl.BlockSpec(shape, lambda i, j, k: (0, 0)))
    scratch = [pltpu.VMEM(acc_shape, F32)] if gk > 1 else []
    return pl.pallas_call(
        body, name=name, grid=(gj, gi, gk) if j_outer else grid, in_specs=in_specs, out_specs=out_specs,
        out_shape=out_shape, scratch_shapes=scratch, compiler_params=_params(3),
    )(*args)


def _ew(name, grid, ins, outs, fn, acc_outs=(), place=None):
    n_i, n_o, n_a = len(ins), len(outs), len(acc_outs)
    ng = len(grid)
    n_s = 0 if place is None else 1

    def body(*refs):
        in_refs = refs[n_s: n_s + n_i]
        out_refs = refs[n_s + n_i: n_s + n_i + n_o]
        sum_refs = refs[n_s + n_i + n_o:]
        pids = tuple(pl.program_id(a) for a in range(ng))
        if n_s:
            pids = (refs[0],) + pids
        tiles, sums = fn(pids, *[r[...] for r in in_refs])
        for o_ref, tile in zip(out_refs, tiles):
            o_ref[...] = tile.astype(o_ref.dtype)
        if n_a:
            first = pids[0] == 0
            for p_ in pids[1:]:
                first = jnp.logical_and(first, p_ == 0)

            @pl.when(first)
            def _():
                for s_ref in sum_refs:
                    s_ref[...] = jnp.zeros_like(s_ref)

            for s_ref, s in zip(sum_refs, sums):
                s_ref[...] += s

    in_specs = [pl.BlockSpec(block, imap) for _, block, imap in ins]
    out_shape = [jax.ShapeDtypeStruct(shape, dtype) for shape, dtype, _, _ in outs]
    out_specs = [pl.BlockSpec(block, imap) for _, _, block, imap in outs]
    for shape, dtype in acc_outs:
        out_shape.append(jax.ShapeDtypeStruct(shape, dtype))
        out_specs.append(pl.BlockSpec(shape, lambda *_, nd=len(shape): (0,) * nd))
    arrays = [a for a, _, _ in ins]
    if n_s:
        assert not n_a
        spec = pltpu.PrefetchScalarGridSpec(num_scalar_prefetch=1, grid=grid, in_specs=in_specs, out_specs=out_specs)
        return pl.pallas_call(body, name=name, grid_spec=spec, out_shape=out_shape, compiler_params=_params(ng))(
            place, *arrays)
    return pl.pallas_call(
        body, name=name, grid=grid, in_specs=in_specs, out_specs=out_specs, out_shape=out_shape,
        compiler_params=_params(ng),
    )(*arrays)


def _rows(tm, width):
    return (tm, width), (lambda i: (i, 0))


def _rms_fwd_tile(h, g):
    r = lax.rsqrt(jnp.mean(h * h, axis=-1, keepdims=True) + EPS)
    return h * r * g


def _rms_bwd_tile(dn, h, g):
    r = lax.rsqrt(jnp.mean(h * h, axis=-1, keepdims=True) + EPS)
    hhat = h * r
    gy = dn * g
    dh = r * (gy - hhat * jnp.mean(gy * hhat, axis=-1, keepdims=True))
    dg = jnp.sum(dn * hhat, axis=0, keepdims=True)
    return dh, dg


def _rope_tables(pos_col, inv_row, tm):
    s = pos_col.shape[0]

    def fn(pids, pos, inv):
        ang = pos * inv
        lane = lax.broadcasted_iota(jnp.int32, ang.shape, 1)
        cs = jnp.where(lane < ROPE_DIM, jnp.cos(ang), 1.0)
        sn = jnp.sin(ang)
        s_lo = jnp.where(lane < ROPE_HALF, -sn, 0.0)
        s_hi = jnp.where(jnp.logical_and(lane >= ROPE_HALF, lane < ROPE_DIM), sn, 0.0)
        return (cs, s_lo, s_hi), ()

    blk, imap = _rows(tm, 128)
    return _ew(
        "rope_tables", (s // tm,),
        [(pos_col, (tm, 1), lambda i: (i, 0)), (inv_row, (1, 128), lambda i: (0, 0))],
        [((s, 128), F32, blk, imap)] * 3, fn,
    )


def _rope(xh, cs, s_lo, s_hi):
    return xh * cs + pltpu.roll(xh, HEAD_DIM - ROPE_HALF, 1) * s_lo + pltpu.roll(xh, ROPE_HALF, 1) * s_hi


def _rope_t(gh, cs, s_lo, s_hi):
    return gh * cs + pltpu.roll(gh * s_lo, ROPE_HALF, 1) + pltpu.roll(gh * s_hi, HEAD_DIM - ROPE_HALF, 1)


def _attn_geometry(length):
    nb = length // ATTN_BLOCK
    gq = min(4, nb)
    assert nb % gq == 0
    return nb, gq, gq * ATTN_BLOCK, nb // gq


def _band_masks():
    qi = lax.broadcasted_iota(jnp.int32, (ATTN_BLOCK, ATTN_BLOCK), 0)
    kj = lax.broadcasted_iota(jnp.int32, (ATTN_BLOCK, ATTN_BLOCK), 1)
    return kj <= qi, kj >= qi


def _band_mask_pair():
    qi = lax.broadcasted_iota(jnp.int32, (ATTN_BLOCK, 2 * ATTN_BLOCK), 0)
    cj = lax.broadcasted_iota(jnp.int32, (ATTN_BLOCK, 2 * ATTN_BLOCK), 1)
    in_cur = cj >= ATTN_BLOCK
    band = jnp.logical_or(jnp.logical_and(in_cur, cj - ATTN_BLOCK <= qi),
                          jnp.logical_and(cj < ATTN_BLOCK, cj >= qi))
    return band, in_cur


def _attn_fwd(qv, kv, vv, dil, cols3=(0, 0, 0)):
    length = qv.shape[0]
    nb, gq, rows, ni = _attn_geometry(length)

    def body(q_ref, kc_ref, kp_ref, vc_ref, vp_ref, o_ref, l_ref):
        i = pl.program_id(1)
        band, in_cur = _band_mask_pair()
        band_first = jnp.logical_and(band, jnp.logical_or(in_cur, i > 0))
        work = []
        for h in range(HEADS_PER_GROUP):
            cols = slice(h * HEAD_DIM, (h + 1) * HEAD_DIM)
            qh = q_ref[:, cols]
            k_all = jnp.concatenate([kp_ref[:, cols], kc_ref[:, cols]], axis=0)
            v_all = jnp.concatenate([vp_ref[:, cols], vc_ref[:, cols]], axis=0)
            for jj in range(gq):
                rws = slice(jj * ATTN_BLOCK, (jj + 1) * ATTN_BLOCK)
                two = slice(jj * ATTN_BLOCK, (jj + 2) * ATTN_BLOCK)
                work.append(dict(h=h, rws=rws, cols=cols, v=v_all[two], first=jj == 0, s=_dot(qh[rws], k_all[two], "nt")))
        for w in work:
            s = jnp.where(band_first if w["first"] else band, w["s"], NEG_BIG)
            m = jnp.max(s, axis=-1, keepdims=True)
            pexp = jnp.exp(s - m)
            w["den"] = jnp.sum(pexp, axis=-1, keepdims=True)
            w["p"] = pexp.astype(BF16)
            w["lse"] = m + jnp.log(w["den"])
        for w in work:
            o = _dot(w["p"], w["v"], "nn")
            o_ref[w["rws"], w["cols"]] = (o / w["den"]).astype(o_ref.dtype)
            l_ref[w["rws"], w["h"] * LSE_LANES:(w["h"] + 1) * LSE_LANES] = jnp.broadcast_to(w["lse"], (ATTN_BLOCK, LSE_LANES))

    def cur(c):
        return pl.BlockSpec((rows, GROUP_WIDTH), lambda r, i: (i, r + c))

    def prev(c):
        return pl.BlockSpec((ATTN_BLOCK, GROUP_WIDTH), lambda r, i: (jnp.maximum(i * gq - 1, 0), r + c))

    cq, ck, cv = cols3
    return pl.pallas_call(
        body, name=f"attn_fwd_d{dil}", grid=(dil, ni),
        in_specs=[cur(cq), cur(ck), prev(ck), cur(cv), prev(cv)],
        out_specs=[cur(0), pl.BlockSpec((rows, LSE_WIDTH), lambda r, i: (i, r))],
        out_shape=[jax.ShapeDtypeStruct((length, dil * GROUP_WIDTH), BF16),
                   jax.ShapeDtypeStruct((length, dil * LSE_WIDTH), F32)],
        compiler_params=_params(2),
    )(qv, kv, kv, vv, vv)


def _attn_bwd(qv, kv, vv, dov, ov, lv, dil, cols3=(0, 0, 0)):
    length = qv.shape[0]
    nb, gq, rows, ni = _attn_geometry(length)
    out_shape = (length, dil * GROUP_WIDTH)

    def body(qc_ref, qn_ref, kc_ref, kp_ref, vc_ref, vp_ref, doc_ref, don_ref, oc_ref, on_ref, lc_ref, ln_ref,
             dq_ref, dk_ref, dv_ref):
        i = pl.program_id(1)
        _, mask_p = _band_masks()
        band, in_cur = _band_mask_pair()
        band_first = jnp.logical_and(band, jnp.logical_or(in_cur, i > 0))
        has_next = i < ni - 1

        last = slice(gq * ATTN_BLOCK, (gq + 1) * ATTN_BLOCK)
        mask_next = jnp.logical_and(mask_p, has_next)

        def rows_of(jj):
            return slice(jj * ATTN_BLOCK, (jj + 1) * ATTN_BLOCK)

        def keys_of(jj):
            return slice(jj * ATTN_BLOCK, (jj + 2) * ATTN_BLOCK)

        heads = []
        for h in range(HEADS_PER_GROUP):
            cols = slice(h * HEAD_DIM, (h + 1) * HEAD_DIM)
            hd = dict(
                cols=cols, q_c=qc_ref[:, cols], q_n=qn_ref[:, cols],
                k_all=jnp.concatenate([kp_ref[:, cols], kc_ref[:, cols]], axis=0),
                v_all=jnp.concatenate([vp_ref[:, cols], vc_ref[:, cols]], axis=0),
                do_c=doc_ref[:, cols], do_n=don_ref[:, cols],
                l_c=lc_ref[:, h * LSE_LANES:h * LSE_LANES + 1], l_n=ln_ref[:, h * LSE_LANES:h * LSE_LANES + 1],
            )
            hd["dl_c"] = jnp.sum(hd["do_c"].astype(F32) * oc_ref[:, cols].astype(F32), axis=-1, keepdims=True)
            hd["dl_n"] = jnp.sum(hd["do_n"].astype(F32) * on_ref[:, cols].astype(F32), axis=-1, keepdims=True)
            hd["s"] = [_dot(hd["q_c"][rows_of(jj)], hd["k_all"][keys_of(jj)], "nt") for jj in range(gq)]
            hd["dp"] = [_dot(hd["do_c"][rows_of(jj)], hd["v_all"][keys_of(jj)], "nt") for jj in range(gq)]
            hd["s"].append(_dot(hd["q_n"], hd["k_all"][last], "nt"))
            hd["dp"].append(_dot(hd["do_n"], hd["v_all"][last], "nt"))
            heads.append(hd)
        for hd in heads:
            hd["p"], hd["ds"] = [], []
            for jj in range(gq + 1):
                if jj < gq:
                    mask, l_col, delta = (band_first if jj == 0 else band), hd["l_c"][rows_of(jj)], hd["dl_c"][rows_of(jj)]
                else:
                    mask, l_col, delta = mask_next, hd["l_n"], hd["dl_n"]
                p = jnp.where(mask, jnp.exp(hd["s"][jj] - l_col), 0.0)
                hd["p"].append(p.astype(BF16))
                hd["ds"].append((p * (hd["dp"][jj] - delta)).astype(BF16))
        for hd in heads:
            cols = hd["cols"]
            dk_blocks, dv_blocks = [None] * (gq + 1), [None] * (gq + 1)

            def add(lst, idx, val):
                lst[idx] = val if lst[idx] is None else lst[idx] + val

            for jj in range(gq):
                qb, dob = hd["q_c"][rows_of(jj)], hd["do_c"][rows_of(jj)]
                dq_ref[rows_of(jj), cols] = _dot(hd["ds"][jj], hd["k_all"][keys_of(jj)], "nn").astype(dq_ref.dtype)
                dk2 = _dot(hd["ds"][jj], qb, "tn")
                dv2 = _dot(hd["p"][jj], dob, "tn")
                add(dk_blocks, jj, dk2[:ATTN_BLOCK])
                add(dk_blocks, jj + 1, dk2[ATTN_BLOCK:])
                add(dv_blocks, jj, dv2[:ATTN_BLOCK])
                add(dv_blocks, jj + 1, dv2[ATTN_BLOCK:])
            add(dk_blocks, gq, _dot(hd["ds"][gq], hd["q_n"], "tn"))
            add(dv_blocks, gq, _dot(hd["p"][gq], hd["do_n"], "tn"))
            for jj in range(gq):
                dk_ref[rows_of(jj), cols] = dk_blocks[jj + 1].astype(dk_ref.dtype)
                dv_ref[rows_of(jj), cols] = dv_blocks[jj + 1].astype(dv_ref.dtype)

    def cur(c):
        return pl.BlockSpec((rows, GROUP_WIDTH), lambda r, i: (i, r + c))

    def prev(c):
        return pl.BlockSpec((ATTN_BLOCK, GROUP_WIDTH), lambda r, i: (jnp.maximum(i * gq - 1, 0), r + c))

    def nxt(c):
        return pl.BlockSpec((ATTN_BLOCK, GROUP_WIDTH), lambda r, i: (jnp.minimum((i + 1) * gq, nb - 1), r + c))

    cq, ck, cv = cols3
    lse_cur = pl.BlockSpec((rows, LSE_WIDTH), lambda r, i: (i, r))
    lse_next = pl.BlockSpec((ATTN_BLOCK, LSE_WIDTH), lambda r, i: (jnp.minimum((i + 1) * gq, nb - 1), r))
    return pl.pallas_call(
        body, name=f"attn_bwd_d{dil}", grid=(dil, ni),
        in_specs=[cur(cq), nxt(cq), cur(ck), prev(ck), cur(cv), prev(cv), cur(0), nxt(0), cur(0), nxt(0), lse_cur, lse_next],
        out_specs=[cur(0), cur(0), cur(0)],
        out_shape=[jax.ShapeDtypeStruct(out_shape, BF16)] * 3,
        compiler_params=_params(2),
    )(qv, qv, kv, kv, vv, vv, dov, dov, ov, ov, lv, lv)


DILATED = tuple((g, d) for g, d in enumerate(GROUP_DILATIONS) if d > 1)


def _spread(scr, slot, tile, out_ref, dil, col, width=GROUP_WIDTH):
    tm = tile.shape[0]
    buf = scr.at[slot]
    buf[...] = tile
    for r in range(dil):
        c0 = r * width + col
        out_ref[:, c0:c0 + HEAD_DIM] = buf[pl.ds(r, tm // dil, stride=dil), :].astype(out_ref.dtype)


def _collect(scr, slot, in_ref, dil, col, width=GROUP_WIDTH):
    tm = scr.shape[1]
    buf = scr.at[slot]
    for r in range(dil):
        c0 = r * width + col
        buf[pl.ds(r, tm // dil, stride=dil), :] = in_ref[:, c0:c0 + HEAD_DIM].astype(F32)
    return buf[...]


def _view_spec(tm, dil, width=GROUP_WIDTH):
    return pl.BlockSpec((tm // dil, dil * width), lambda i: (i, 0))


def _view_shape(s, dil, dtype, width=GROUP_WIDTH):
    return jax.ShapeDtypeStruct((s // dil, dil * width), dtype)


def _qkv_layout(z, tabs, tm):
    s = z.shape[0]
    scale = 1.0 / math.sqrt(HEAD_DIM)
    qkv_width = 3 * N_GROUPS * GROUP_WIDTH

    def body(z_ref, cs_ref, lo_ref, hi_ref, qk0_ref, *rest):
        views, scr = rest[:-1], rest[-1]
        tabs_ = (cs_ref[...], lo_ref[...], hi_ref[...])
        for part in range(3):
            for g, dil in enumerate(GROUP_DILATIONS):
                if part == 2 and dil == 1:
                    continue
                for h in range(HEADS_PER_GROUP):
                    col = part * N_GROUPS * GROUP_WIDTH + g * GROUP_WIDTH + h * HEAD_DIM
                    t = z_ref[:, col:col + HEAD_DIM].astype(F32)
                    if part < 2:
                        t = _rope(t, *tabs_)
                    if part == 0:
                        t = t * scale
                    if dil == 1:
                        c0 = part * GROUP_WIDTH + h * HEAD_DIM
                        qk0_ref[:, c0:c0 + HEAD_DIM] = t.astype(BF16)
                    else:
                        out = views[3 * [gg for gg, _ in DILATED].index(g) + part]
                        _spread(scr, h, t, out, dil, h * HEAD_DIM)

    row = lambda i: (i, 0)
    out_shape = [jax.ShapeDtypeStruct((s, 2 * GROUP_WIDTH), BF16)]
    out_specs = [pl.BlockSpec((tm, 2 * GROUP_WIDTH), row)]
    for _, dil in DILATED:
        out_shape += [_view_shape(s, dil, BF16)] * 3
        out_specs += [_view_spec(tm, dil)] * 3
    res = pl.pallas_call(
        body, name="qkv_layout", grid=(s // tm,),
        in_specs=[pl.BlockSpec((tm, qkv_width), row)] + [pl.BlockSpec((tm, HEAD_DIM), row)] * 3,
        out_specs=out_specs, out_shape=out_shape,
        scratch_shapes=[pltpu.VMEM((HEADS_PER_GROUP, tm, HEAD_DIM), F32)], compiler_params=_params(1),
    )(z, *tabs)
    return res[0], [tuple(res[1 + 3 * n:4 + 3 * n]) for n in range(len(DILATED))]


def _attn_merge(o0, l0, dilated, tm):
    s = o0.shape[0]
    n_d = len(DILATED)

    def body(*refs):
        o0_ref, l0_ref = refs[:2]
        in_views = refs[2:2 + 2 * n_d]
        attn_ref, lse_ref = refs[2 + 2 * n_d:4 + 2 * n_d]
        out_views = refs[4 + 2 * n_d:4 + 4 * n_d]
        scr = refs[-1]
        l_rows = [l0_ref[...]] + [_collect(scr, n, in_views[2 * n + 1], dil, 0, LSE_WIDTH) for n, (_, dil) in enumerate(DILATED)]
        lse_heads = []
        for h in range(HEADS_PER_GROUP):
            cols = slice(h * HEAD_DIM, (h + 1) * HEAD_DIM)
            os_ = [o0_ref[:, cols].astype(F32)]
            for n, (_, dil) in enumerate(DILATED):
                os_.append(_collect(scr, n_d + n, in_views[2 * n], dil, h * HEAD_DIM))
            ls_ = [lr[:, h * LSE_LANES:h * LSE_LANES + 1] for lr in l_rows]
            m = ls_[0]
            for l_ in ls_[1:]:
                m = jnp.maximum(m, l_)
            es = [jnp.exp(l_ - m) for l_ in ls_]
            den = es[0]
            num = es[0] * os_[0]
            for e, o in zip(es[1:], os_[1:]):
                den = den + e
                num = num + e * o
            attn = num / den
            lse_heads.append(jnp.broadcast_to(m + jnp.log(den), (tm, LSE_LANES)))
            attn_ref[:, cols] = attn.astype(BF16)
            for n, (_, dil) in enumerate(DILATED):
                _spread(scr, 2 * n_d, attn, out_views[2 * n], dil, h * HEAD_DIM)
        lse = jnp.concatenate(lse_heads, axis=1)
        lse_ref[...] = lse
        for n, (_, dil) in enumerate(DILATED):
            _spread(scr, 2 * n_d, lse, out_views[2 * n + 1], dil, 0, LSE_WIDTH)

    row = lambda i: (i, 0)
    nat = pl.BlockSpec((tm, GROUP_WIDTH), row)
    nat_l = pl.BlockSpec((tm, LSE_WIDTH), row)
    in_specs = [nat, nat_l]
    args = [o0, l0]
    out_specs = [nat, nat_l]
    out_shape = [jax.ShapeDtypeStruct((s, GROUP_WIDTH), BF16), jax.ShapeDtypeStruct((s, LSE_WIDTH), F32)]
    for (_, dil), (ov, lv) in zip(DILATED, dilated):
        in_specs += [_view_spec(tm, dil), _view_spec(tm, dil, LSE_WIDTH)]
        args += [ov, lv]
        out_specs += [_view_spec(tm, dil), _view_spec(tm, dil, LSE_WIDTH)]
        out_shape += [_view_shape(s, dil, BF16), _view_shape(s, dil, F32, LSE_WIDTH)]
    res = pl.pallas_call(
        body, name="attn_merge", grid=(s // tm,), in_specs=in_specs, out_specs=out_specs, out_shape=out_shape,
        scratch_shapes=[pltpu.VMEM((2 * n_d + 1, tm, HEAD_DIM), F32)], compiler_params=_params(1),
    )(*args)
    return res[0], res[1], [tuple(res[2 + 2 * n:4 + 2 * n]) for n in range(n_d)]


def _to_views(a, tm):
    s = a.shape[0]

    def body(a_ref, *rest):
        outs, scr = rest[:-1], rest[-1]
        for h in range(HEADS_PER_GROUP):
            t = a_ref[:, h * HEAD_DIM:(h + 1) * HEAD_DIM].astype(F32)
            for n, (_, dil) in enumerate(DILATED):
                _spread(scr, n, t, outs[n], dil, h * HEAD_DIM)

    return pl.pallas_call(
        body, name="to_views", grid=(s // tm,), in_specs=[pl.BlockSpec((tm, GROUP_WIDTH), lambda i: (i, 0))],
        out_specs=[_view_spec(tm, dil) for _, dil in DILATED], out_shape=[_view_shape(s, dil, BF16) for _, dil in DILATED],
        scratch_shapes=[pltpu.VMEM((len(DILATED), tm, HEAD_DIM), F32)], compiler_params=_params(1),
    )(a)


def _dz_layout(grads, du, dga, dgs, tabs, tm):
    s = du.shape[0]
    scale = 1.0 / math.sqrt(HEAD_DIM)

    def body(*refs):
        g_refs = refs[:3 * N_GROUPS]
        du_ref, dga_ref, dgs_ref, cs_ref, lo_ref, hi_ref, dz_ref, scr = refs[3 * N_GROUPS:]
        tabs_ = (cs_ref[...], lo_ref[...], hi_ref[...])
        for part in range(3):
            for g, dil in enumerate(GROUP_DILATIONS):
                src = g_refs[3 * g + part]
                for h in range(HEADS_PER_GROUP):
                    if dil == 1:
                        t = src[:, h * HEAD_DIM:(h + 1) * HEAD_DIM].astype(F32)
                    else:
                        t = _collect(scr, h, src, dil, h * HEAD_DIM)
                    if part < 2:
                        t = _rope_t(t, *tabs_)
                    if part == 0:
                        t = t * scale
                    col = part * N_GROUPS * GROUP_WIDTH + g * GROUP_WIDTH + h * HEAD_DIM
                    dz_ref[:, col:col + HEAD_DIM] = t.astype(BF16)
        dz_ref[:, COL_U:COL_GA] = du_ref[...]
        dz_ref[:, COL_GA:COL_GS] = dga_ref[...]
        dz_ref[:, COL_GS:IN_WIDTH] = dgs_ref[...]

    row = lambda i: (i, 0)
    in_specs, args = [], []
    for (g, dil), trio in zip(enumerate(GROUP_DILATIONS), grads):
        in_specs += [pl.BlockSpec((tm, GROUP_WIDTH), row) if dil == 1 else _view_spec(tm, dil)] * 3
        args += list(trio)
    in_specs += [pl.BlockSpec((tm, SSM_WIDTH), row), pl.BlockSpec((tm, D_MODEL), row), pl.BlockSpec((tm, D_MODEL), row)]
    in_specs += [pl.BlockSpec((tm, HEAD_DIM), row)] * 3
    return pl.pallas_call(
        body, name="dz_layout", grid=(s // tm,), in_specs=in_specs, out_specs=pl.BlockSpec((tm, IN_WIDTH), row),
        out_shape=jax.ShapeDtypeStruct((s, IN_WIDTH), BF16),
        scratch_shapes=[pltpu.VMEM((HEADS_PER_GROUP, tm, HEAD_DIM), F32)], compiler_params=_params(1),
    )(*args, du, dga, dgs, *tabs)


def _discretise(a_re, a_im, log_dt, bt_re, bt_im):
    dt = jnp.exp(log_dt)
    mag = jnp.exp(a_re * dt)
    bar_re = mag * jnp.cos(a_im * dt)
    bar_im = mag * jnp.sin(a_im * dt)
    nr = bar_re - 1.0
    ni = bar_im
    den = a_re * a_re + a_im * a_im
    z_re = (nr * a_re + ni * a_im) / den
    z_im = (ni * a_re - nr * a_im) / den
    bb_re = z_re[:, None, :] * bt_re - z_im[:, None, :] * bt_im
    bb_im = z_re[:, None, :] * bt_im + z_im[:, None, :] * bt_re
    return bar_re, bar_im, bb_re, bb_im


def _ssm_prep(a_re, a_im, log_dt, bt_re, bt_im):
    def body(ar, ai, ld, br, bi, o_lr, o_li, o_br, o_bi):
        lr, li, bbr, bbi = _discretise(ar[...], ai[...], ld[...], br[...], bi[...])
        o_lr[...] = lr
        o_li[...] = li
        o_br[...] = bbr
        o_bi[...] = bbi

    sm = jax.ShapeDtypeStruct((SSM_GROUPS, SSM_STATE), F32)
    bg = jax.ShapeDtypeStruct((SSM_GROUPS, SSM_GROUP, SSM_STATE), F32)
    return pl.pallas_call(body, name="ssm_prep", out_shape=[sm, sm, bg, bg])(a_re, a_im, log_dt, bt_re, bt_im)


def _ssm_param_bwd(a_re, a_im, log_dt, bt_re, bt_im, d_lr, d_li, d_bbr, d_bbi):
    def body(ar, ai, ld, br, bi, g_lr, g_li, g_br, g_bi, o_ar, o_ai, o_ld, o_br, o_bi):
        _, vjp = jax.vjp(_discretise, ar[...], ai[...], ld[...], br[...], bi[...])
        d_ar, d_ai, d_ld, d_br, d_bi = vjp((g_lr[...], g_li[...], g_br[...], g_bi[...]))
        o_ar[...] = d_ar
        o_ai[...] = d_ai
        o_ld[...] = d_ld
        o_br[...] = d_br
        o_bi[...] = d_bi

    sm = jax.ShapeDtypeStruct((SSM_GROUPS, SSM_STATE), F32)
    col = jax.ShapeDtypeStruct((SSM_GROUPS, 1), F32)
    bg = jax.ShapeDtypeStruct((SSM_GROUPS, SSM_GROUP, SSM_STATE), F32)
    return pl.pallas_call(body, name="ssm_param_bwd", out_shape=[sm, sm, col, bg, bg])(
        a_re, a_im, log_dt, bt_re, bt_im, d_lr, d_li, d_bbr, d_bbi)


def _block_diag(t, rows_per, cols_per):
    t4 = t.reshape(SSM_SUPER, 8, rows_per, cols_per)
    eye = jnp.eye(8, dtype=t.dtype)
    return jnp.einsum("bgrc,gh->bgrhc", t4, eye).reshape(SSM_SUPER, 8 * rows_per, 8 * cols_per)


def _block_diag_t(dense, rows_per, cols_per):
    t = dense.reshape(SSM_SUPER, 8, rows_per, 8, cols_per)
    eye = jnp.eye(8, dtype=dense.dtype)
    return jnp.einsum("bgrhc,gh->bgrc", t, eye).reshape(SSM_GROUPS, rows_per, cols_per)


def _gelu(v):
    c = math.sqrt(2.0 / math.pi)
    return 0.5 * v * (1.0 + jnp.tanh(c * (v + 0.044715 * v * v * v)))


def _gelu_grad(v):
    c = math.sqrt(2.0 / math.pi)
    t = jnp.tanh(c * (v + 0.044715 * v * v * v))
    return 0.5 * (1.0 + t) + 0.5 * v * (1.0 - t * t) * c * (1.0 + 3.0 * 0.044715 * v * v)


SUB = 8


SCAN_STEPS = (1, 2, 4)
N_SCAN_TABLES = 2 + 2 * len(SCAN_STEPS)


def _scan_tables(tab_ref, lam_re, lam_im, reverse, conj):
    lr = lam_re
    li = -lam_im if conj else lam_im
    powers = [(lr, li)]
    for _ in range(SUB - 1):
        pr, pi = powers[-1]
        powers.append((pr * lr - pi * li, pr * li + pi * lr))
    row = lax.broadcasted_iota(jnp.int32, (SUB, N_STATE), 0)
    if reverse:
        row = SUB - 1 - row
    wide = lambda v: jnp.broadcast_to(v, (SUB, N_STATE))
    p_re = jnp.zeros((SUB, N_STATE), F32)
    p_im = jnp.zeros((SUB, N_STATE), F32)
    for j in range(SUB):
        p_re = jnp.where(row == j, wide(powers[j][0]), p_re)
        p_im = jnp.where(row == j, wide(powers[j][1]), p_im)
    tab_ref[0] = p_re
    tab_ref[1] = p_im
    for idx, k in enumerate(SCAN_STEPS):
        tab_ref[2 + 2 * idx] = jnp.where(row >= k, wide(powers[k - 1][0]), 0.0)
        tab_ref[3 + 2 * idx] = jnp.where(row >= k, wide(powers[k - 1][1]), 0.0)


def _scan_rows(g_re_ref, g_im_ref, tab_ref, carry, n_rows, reverse):
    last = 0 if reverse else SUB - 1

    def tile_step(tt, state):
        cr, ci = state
        t8 = (n_rows // SUB - 1 - tt) if reverse else tt
        start = pl.multiple_of(t8 * SUB, SUB)
        xr = g_re_ref[pl.ds(start, SUB), :]
        xi = g_im_ref[pl.ds(start, SUB), :]
        for idx, k in enumerate(SCAN_STEPS):
            mr = tab_ref[2 + 2 * idx]
            mi = tab_ref[3 + 2 * idx]
            shift = SUB - k if reverse else k
            sr = pltpu.roll(xr, shift, 0)
            si = pltpu.roll(xi, shift, 0)
            xr, xi = xr + (mr * sr - mi * si), xi + (mr * si + mi * sr)
        pr = tab_ref[0]
        pi = tab_ref[1]
        xr, xi = xr + (pr * cr - pi * ci), xi + (pr * ci + pi * cr)
        g_re_ref[pl.ds(start, SUB), :] = xr
        g_im_ref[pl.ds(start, SUB), :] = xi
        return (jnp.broadcast_to(xr[last:last + 1, :], (SUB, N_STATE)),
                jnp.broadcast_to(xi[last:last + 1, :], (SUB, N_STATE)))

    return lax.fori_loop(0, n_rows // SUB, tile_step, carry)


def _ssm_fwd(z, b_re, b_im, c_re, c_im, lam_re, lam_im, d_skip, chunk):
    s = z.shape[0]

    def body(u_ref, bre, bim, cre, cim, lre, lim, dsk, hre_ref, him_ref, ys_ref, yg_ref, car_re, car_im, tabs):
        i = pl.program_id(0)

        @pl.when(i == 0)
        def _():
            car_re[...] = jnp.zeros_like(car_re)
            car_im[...] = jnp.zeros_like(car_im)
            _scan_tables(tabs, lre[...], lim[...], False, False)

        u = u_ref[...]
        for b in range(SSM_SUPER):
            ub = u[:, b * 128:(b + 1) * 128]
            st = slice(b * 512, (b + 1) * 512)
            hre_ref[:, st] = _dot(ub, bre[b], "nn")
            him_ref[:, st] = _dot(ub, bim[b], "nn")
        sr, si = _scan_rows(hre_ref, him_ref, tabs, (car_re[...], car_im[...]), chunk, False)
        car_re[...] = sr
        car_im[...] = si
        uf = u.astype(F32)
        for b in range(SSM_SUPER):
            st = slice(b * 512, (b + 1) * 512)
            ch = slice(b * 128, (b + 1) * 128)
            y = _dot(hre_ref[:, st].astype(BF16), cre[b], "nn") - _dot(him_ref[:, st].astype(BF16), cim[b], "nn")
            y = y + dsk[:, ch] * uf[:, ch]
            ys_ref[:, ch] = y
            yg_ref[:, ch] = _gelu(y).astype(BF16)

    full3 = lambda i: (0, 0, 0)
    full2 = lambda i: (0, 0)
    row = lambda i: (i, 0)
    u_col = COL_U // SSM_WIDTH
    return pl.pallas_call(
        body, name="ssm_fwd", grid=(s // chunk,),
        in_specs=[pl.BlockSpec((chunk, SSM_WIDTH), lambda i: (i, u_col)),
                  pl.BlockSpec((SSM_SUPER, 128, 512), full3), pl.BlockSpec((SSM_SUPER, 128, 512), full3),
                  pl.BlockSpec((SSM_SUPER, 512, 128), full3), pl.BlockSpec((SSM_SUPER, 512, 128), full3),
                  pl.BlockSpec((1, N_STATE), full2), pl.BlockSpec((1, N_STATE), full2), pl.BlockSpec((1, SSM_WIDTH), full2)],
        out_specs=[pl.BlockSpec((chunk, N_STATE), row), pl.BlockSpec((chunk, N_STATE), row),
                   pl.BlockSpec((chunk, SSM_WIDTH), row), pl.BlockSpec((chunk, SSM_WIDTH), row)],
        out_shape=[jax.ShapeDtypeStruct((s, N_STATE), F32), jax.ShapeDtypeStruct((s, N_STATE), F32),
                   jax.ShapeDtypeStruct((s, SSM_WIDTH), F32), jax.ShapeDtypeStruct((s, SSM_WIDTH), BF16)],
        scratch_shapes=[pltpu.VMEM((SUB, N_STATE), F32), pltpu.VMEM((SUB, N_STATE), F32),
                        pltpu.VMEM((N_SCAN_TABLES, SUB, N_STATE), F32)],
        compiler_params=_params(1),
    )(z, b_re, b_im, c_re, c_im, lam_re, lam_im, d_skip)


def _ssm_bwd(dys, z, h_re, h_im, b_re, b_im, c_re, c_im, lam_re, lam_im, d_skip, chunk):
    s = z.shape[0]
    n_chunks = s // chunk

    def body(dy_ref, u_ref, hre_ref, him_ref, hpr_ref, hpi_ref, bre, bim, cre, cim, lre, lim, dsk,
             du_ref, dlr_ref, dli_ref, dbr_ref, dbi_ref, dcr_ref, dci_ref, dd_ref, are, aim, car_re, car_im, tabs):
        i = pl.program_id(0)
        n = n_chunks - 1 - i

        @pl.when(i == 0)
        def _():
            car_re[...] = jnp.zeros_like(car_re)
            car_im[...] = jnp.zeros_like(car_im)
            _scan_tables(tabs, lre[...], lim[...], True, True)
            for r in (dlr_ref, dli_ref, dbr_ref, dbi_ref, dcr_ref, dci_ref, dd_ref):
                r[...] = jnp.zeros_like(r)

        dy = dy_ref[...]
        dyb = dy.astype(BF16)
        u = u_ref[...]
        for b in range(SSM_SUPER):
            ch = slice(b * 128, (b + 1) * 128)
            st = slice(b * 512, (b + 1) * 512)
            are[:, st] = _dot(dyb[:, ch], cre[b], "nt")
            aim[:, st] = -_dot(dyb[:, ch], cim[b], "nt")
        sr, si = _scan_rows(are, aim, tabs, (car_re[...], car_im[...]), chunk, True)
        car_re[...] = sr
        car_im[...] = si
        row_id = lax.broadcasted_iota(jnp.int32, (chunk, N_STATE), 0)
        top_scale = jnp.where(n > 0, 1.0, 0.0)
        h_r = hre_ref[...]
        h_i = him_ref[...]
        hp_r = jnp.where(row_id == 0, hpr_ref[SUB - 1:SUB, :] * top_scale, pltpu.roll(h_r, 1, 0))
        hp_i = jnp.where(row_id == 0, hpi_ref[SUB - 1:SUB, :] * top_scale, pltpu.roll(h_i, 1, 0))
        a_r = are[...]
        a_i = aim[...]
        dlr_ref[...] += jnp.sum(a_r * hp_r + a_i * hp_i, axis=0, keepdims=True)
        dli_ref[...] += jnp.sum(a_i * hp_r - a_r * hp_i, axis=0, keepdims=True)
        dd_ref[...] += jnp.sum(dy * u.astype(F32), axis=0, keepdims=True)
        a_rb = a_r.astype(BF16)
        a_ib = a_i.astype(BF16)
        h_rb = h_r.astype(BF16)
        h_ib = h_i.astype(BF16)
        for b in range(SSM_SUPER):
            ch = slice(b * 128, (b + 1) * 128)
            st = slice(b * 512, (b + 1) * 512)
            dbr_ref[b] += _dot(u[:, ch], a_rb[:, st], "tn")
            dbi_ref[b] += _dot(u[:, ch], a_ib[:, st], "tn")
            dcr_ref[b] += _dot(h_rb[:, st], dyb[:, ch], "tn")
            dci_ref[b] += -_dot(h_ib[:, st], dyb[:, ch], "tn")
            du = _dot(a_rb[:, st], bre[b], "nt") + _dot(a_ib[:, st], bim[b], "nt") + dsk[:, ch] * dy[:, ch]
            du_ref[:, ch] = du.astype(du_ref.dtype)

    full3 = lambda i: (0, 0, 0)
    full2 = lambda i: (0, 0)
    rev = lambda i: (n_chunks - 1 - i, 0)
    above = lambda i: (jnp.maximum((n_chunks - 1 - i) * (chunk // SUB) - 1, 0), 0)
    u_col = COL_U // SSM_WIDTH
    b_spec = pl.BlockSpec((SSM_SUPER, 128, 512), full3)
    c_spec = pl.BlockSpec((SSM_SUPER, 512, 128), full3)
    vec = pl.BlockSpec((1, N_STATE), full2)
    return pl.pallas_call(
        body, name="ssm_bwd", grid=(n_chunks,),
        in_specs=[pl.BlockSpec((chunk, SSM_WIDTH), rev),
                  pl.BlockSpec((chunk, SSM_WIDTH), lambda i: (n_chunks - 1 - i, u_col)),
                  pl.BlockSpec((chunk, N_STATE), rev), pl.BlockSpec((chunk, N_STATE), rev),
                  pl.BlockSpec((SUB, N_STATE), above), pl.BlockSpec((SUB, N_STATE), above),
                  b_spec, b_spec, c_spec, c_spec, vec, vec, pl.BlockSpec((1, SSM_WIDTH), full2)],
        out_specs=[pl.BlockSpec((chunk, SSM_WIDTH), rev), vec, vec, b_spec, b_spec, c_spec, c_spec,
                   pl.BlockSpec((1, SSM_WIDTH), full2)],
        out_shape=[jax.ShapeDtypeStruct((s, SSM_WIDTH), BF16),
                   jax.ShapeDtypeStruct((1, N_STATE), F32), jax.ShapeDtypeStruct((1, N_STATE), F32),
                   jax.ShapeDtypeStruct((SSM_SUPER, 128, 512), F32), jax.ShapeDtypeStruct((SSM_SUPER, 128, 512), F32),
                   jax.ShapeDtypeStruct((SSM_SUPER, 512, 128), F32), jax.ShapeDtypeStruct((SSM_SUPER, 512, 128), F32),
                   jax.ShapeDtypeStruct((1, SSM_WIDTH), F32)],
        scratch_shapes=[pltpu.VMEM((chunk, N_STATE), F32), pltpu.VMEM((chunk, N_STATE), F32),
                        pltpu.VMEM((SUB, N_STATE), F32), pltpu.VMEM((SUB, N_STATE), F32),
                        pltpu.VMEM((N_SCAN_TABLES, SUB, N_STATE), F32)],
        compiler_params=_params(1),
    )(dys, z, h_re, h_im, h_re, h_im, b_re, b_im, c_re, c_im, lam_re, lam_im, d_skip)


def _local_step(x, p, pos, tgt, sm, wts):
    s = x.shape[0]
    tm = min(512, s)
    ts = min(1024, s)
    chunk = min(256, s)
    ni = s // tm
    nk = s // ts
    w_in, w_ap, w_ga, w_gb, w_out, w_fg, w_fu, w_fd, w_pg, w_pp = (
        wts[k] for k in ("w_in", "w_attn_proj", "w_glu_a", "w_glu_b", "w_out", "w_ffn_gate", "w_ffn_up", "w_ffn_down",
                         "w_ple_gate", "w_ple_proj"))
    w_out2 = w_out.reshape(D_MODEL, D_MODEL)
    w_pg2 = w_pg.reshape(D_MODEL, D_MODEL)
    g_mix, g_ffn, g_final = sm["g_mix"], sm["g_ffn"], sm["g_final"]
    rowblk, rowmap = _rows(tm, D_MODEL)
    vec1k = ((1, D_MODEL), lambda *_: (0, 0))

    (n1,) = _ew("rms_mix", (ni,), [(x, rowblk, rowmap), (g_mix, *vec1k)], [((s, D_MODEL), BF16, rowblk, rowmap)],
                lambda pids, h, g: ((_rms_fwd_tile(h, g),), ()))

    half_in = IN_WIDTH // 8
    tmb = min(1024, s)
    nib = s // tmb
    (z,) = _mm("in_proj", (nib, 8, 1),
               [(n1, (tmb, D_MODEL), lambda i, j, k: (i, 0), w_in, (None, D_MODEL, half_in), lambda i, j, k: (j // 2, 0, j % 2))],
               "nn", [((s, IN_WIDTH), BF16, (tmb, half_in), lambda i, j, k: (i, j))], j_outer=True)

    inv = ROPE_THETA ** (-jnp.arange(ROPE_HALF, dtype=F32) * 2.0 / ROPE_DIM)
    inv_row = jnp.concatenate([inv, inv, jnp.zeros((HEAD_DIM - ROPE_DIM,), F32)]).reshape(1, HEAD_DIM)
    tabs = _rope_tables(pos.astype(F32).reshape(s, 1), inv_row, tm)

    qk0, qkv_views = _qkv_layout(z, tabs, tm)
    v0_col = (2 * N_GROUPS * GROUP_WIDTH) // GROUP_WIDTH
    group_in = [((qk0, qk0, z), (0, 1, v0_col))] + [(trio, (0, 0, 0)) for trio in qkv_views]
    fwd_out = [_attn_fwd(*arrs, dil, cols3) for (arrs, cols3), dil in zip(group_in, GROUP_DILATIONS)]
    attn, lse, merged_views = _attn_merge(fwd_out[0][0], fwd_out[0][1], fwd_out[1:], tm)

    def chip_cols(parts):
        return (jnp.concatenate(parts, axis=1),), ()

    def proj_cols(name, a, width, w):
        blk = (None, width, 256)
        pairs = [(a, (tmb, width), lambda i, j, k: (i, 0), w, blk, lambda i, j, k: (0, 0, 0))]
        pairs += [(None, None, None, w, blk, (lambda i, j, k, q=q: (q, 0, 0))) for q in range(1, N_CHIPS)]
        return _mm(name, (nib, 1, 1), pairs, "nn", [((s, D_MODEL), BF16, (tmb, D_MODEL), lambda i, j, k: (i, 0))],
                   epilogue=chip_cols, sum_pairs=False)[0]

    def proj512(name, a, w):
        return proj_cols(name, a, GROUP_WIDTH, w)

    attn_d = proj512("attn_proj", attn, w_ap)

    bt_re = jnp.transpose(sm["b_re"], (0, 2, 1))
    bt_im = jnp.transpose(sm["b_im"], (0, 2, 1))
    log_dt_col = sm["log_dt"].reshape(SSM_GROUPS, 1)
    lam_re, lam_im, bbt_re, bbt_im = _ssm_prep(sm["a_re"], sm["a_im"], log_dt_col, bt_re, bt_im)
    b_re_m = _block_diag(bbt_re, SSM_GROUP, SSM_STATE).astype(BF16)
    b_im_m = _block_diag(bbt_im, SSM_GROUP, SSM_STATE).astype(BF16)
    c_re_m = _block_diag(jnp.transpose(sm["c_re"], (0, 2, 1)), SSM_STATE, SSM_GROUP).astype(BF16)
    c_im_m = _block_diag(jnp.transpose(sm["c_im"], (0, 2, 1)), SSM_STATE, SSM_GROUP).astype(BF16)
    lam_re_row = lam_re.reshape(1, N_STATE)
    lam_im_row = lam_im.reshape(1, N_STATE)
    d_skip_row = sm["d_skip"].reshape(1, SSM_WIDTH)
    h_re, h_im, ys, yg = _ssm_fwd(z, b_re_m, b_im_m, c_re_m, c_im_m, lam_re_row, lam_im_row, d_skip_row, chunk)

    pa = proj512("glu_a", yg, w_ga)
    pb = proj512("glu_b", yg, w_gb)

    ga_blk = ((tm, D_MODEL), lambda i: (i, COL_GA // D_MODEL))
    gs_blk = ((tm, D_MODEL), lambda i: (i, COL_GS // D_MODEL))

    def mix_fn(pids, ga, gs, ad, a, b):
        ga, gs, ad, a, b = (t.astype(F32) for t in (ga, gs, ad, a, b))
        return (_sig(ga) * ad + _sig(gs) * (a * _sig(b)),), ()

    (mix,) = _ew("gate_mix", (ni,), [(z, *ga_blk), (z, *gs_blk), (attn_d, rowblk, rowmap), (pa, rowblk, rowmap),
                                     (pb, rowblk, rowmap)], [((s, D_MODEL), BF16, rowblk, rowmap)], mix_fn)

    def out_epi(acc, xr, g):
        h1 = acc + xr
        return (h1, _rms_fwd_tile(h1, g)), ()

    m3 = lambda i, j, k: (i, 0)
    w3 = lambda i, j, k: (0, 0)
    h1, n2 = _mm("out_proj", (nib, 1, 1), [(mix, (tmb, D_MODEL), m3, w_out2, (D_MODEL, D_MODEL), w3)], "nn",
                 [((s, D_MODEL), F32, (tmb, D_MODEL), m3), ((s, D_MODEL), BF16, (tmb, D_MODEL), m3)],
                 epilogue=out_epi, extras=[(x, (tmb, D_MODEL), m3), (g_ffn, (1, D_MODEL), w3)])

    ffq = (None, tm, D_FF_Q)
    ffq_map = lambda i, j, k: (j, i, 0)

    def ffn_in_epi(parts):
        gts, ups = parts[0::2], parts[1::2]
        acts = [gt * _sig(gt) * u_ for gt, u_ in zip(gts, ups)]
        return (jnp.stack(gts, axis=0), jnp.stack(ups, axis=0), jnp.stack(acts, axis=0)), ()

    w_ffq = (None, D_MODEL, D_FF_Q)
    ff_pairs = []
    for q in range(N_CHIPS):
        blk_q = lambda i, j, k, q=q: (q, 0, 0)
        ff_pairs.append((n2, (tm, D_MODEL), m3, w_fg, w_ffq, blk_q) if q == 0 else (None, None, None, w_fg, w_ffq, blk_q))
        ff_pairs.append((None, None, None, w_fu, w_ffq, blk_q))
    ff_all = (N_CHIPS, tm, D_FF_Q)
    ff_all_map = lambda i, j, k: (0, i, 0)
    gate, up, act = _mm("ffn_gate_up", (ni, 1, 1), ff_pairs, "nn",
                        [((N_CHIPS, s, D_FF_Q), BF16, ff_all, ff_all_map)] * 3, epilogue=ffn_in_epi,
                        sum_pairs=False, resident_b=True)

    (h2,) = _mm("ffn_down", (nib, 1, 1),
                [(act, (None, tmb, D_FF_Q), (lambda i, j, k, q=q: (q, i, 0)), w_fd, (None, D_FF_Q, D_MODEL),
                  (lambda i, j, k, q=q: (q, 0, 0))) for q in range(N_CHIPS)], "nn",
                [((s, D_MODEL), F32, (tmb, D_MODEL), m3)], epilogue=lambda acc, hr: ((acc + hr,), ()),
                extras=[(h1, (tmb, D_MODEL), m3)])

    pp = proj_cols("ple_proj", p, PLE_DIM, w_pp)

    def ple_epi(acc, hr, ppr):
        return (acc, hr + _sig(acc) * ppr.astype(F32)), ()

    gl, h3 = _mm("ple_gate", (nib, 1, 1), [(h2, (tmb, D_MODEL), m3, w_pg2, (D_MODEL, D_MODEL), w3)], "nn",
                 [((s, D_MODEL), BF16, (tmb, D_MODEL), m3), ((s, D_MODEL), F32, (tmb, D_MODEL), m3)],
                 epilogue=ple_epi, extras=[(h2, (tmb, D_MODEL), m3), (pp, (tmb, D_MODEL), m3)])

    def head_fn(pids, h, t, g, g_, ppr):
        r = lax.rsqrt(jnp.mean(h * h, axis=-1, keepdims=True) + EPS)
        hhat = h * r
        diff = hhat * g - t
        loss = 0.5 * jnp.sum(jnp.mean(diff * diff, axis=-1, keepdims=True))
        dy = diff * (1.0 / D_MODEL)
        gy = dy * g
        dh = r * (gy - hhat * jnp.mean(gy * hhat, axis=-1, keepdims=True))
        sg = _sig(g_.astype(F32))
        return ((dh, dh * ppr.astype(F32) * sg * (1.0 - sg), dh * sg),
                (jnp.full((SUB, 128), loss, F32), jnp.sum(dy * hhat, axis=0, keepdims=True)))

    dh3, dgl, dpp, loss_acc, dg_final = _ew(
        "loss_head", (ni,),
        [(h3, rowblk, rowmap), (tgt, rowblk, rowmap), (g_final, *vec1k), (gl, rowblk, rowmap), (pp, rowblk, rowmap)],
        [((s, D_MODEL), F32, rowblk, rowmap), ((s, D_MODEL), BF16, rowblk, rowmap), ((s, D_MODEL), BF16, rowblk, rowmap)],
        head_fn, acc_outs=[((SUB, 128), F32), ((1, D_MODEL), F32)])

    def wgrad(name, a, a_block, a_imap, b, b_block, b_imap, out_shape, out_block, out_imap, nj, acc_shape):
        return _mm(name, (1, nj, nk), [(a, a_block, a_imap, b, b_block, b_imap)], "tn",
                   [(out_shape, F32, out_block, out_imap)], acc_shape=acc_shape)[0]

    tk0 = lambda i, j, k: (k, 0)
    tkj = lambda i, j, k: (k, j)
    def wgrad_cols(name, a, width, dy_):
        def split(acc):
            return (jnp.stack([acc[:, q * 256:(q + 1) * 256] for q in range(N_CHIPS)], axis=0),), ()

        return _mm(name, (1, 1, nk), [(a, (ts, width), tk0, dy_, (ts, D_MODEL), tk0)], "tn",
                   [((N_CHIPS, width, 256), F32, (N_CHIPS, width, 256), lambda i, j, k: (0, 0, 0))], epilogue=split,
                   acc_shape=(width, D_MODEL))[0]

    d_w_pp = wgrad_cols("d_ple_proj", p, PLE_DIM, dpp)
    d_w_pg = wgrad("d_ple_gate", h2, (ts, D_MODEL), tk0, dgl, (ts, D_MODEL), tk0, (D_MODEL, D_MODEL),
                   (D_MODEL, D_MODEL), w3, 1, (D_MODEL, D_MODEL))

    (dh2,) = _mm("ple_gate_bwd", (nib, 1, 1), [(dgl, (tmb, D_MODEL), m3, w_pg2, (D_MODEL, D_MODEL), w3)], "nt",
                 [((s, D_MODEL), F32, (tmb, D_MODEL), m3)], epilogue=lambda acc, d_: ((acc + d_,), ()),
                 extras=[(dh3, (tmb, D_MODEL), m3)])

    def ffn_bwd_epi(parts, gt_all, u_all):
        dgs_, dus_ = [], []
        for q, dact in enumerate(parts):
            gt, u_ = gt_all[q].astype(F32), u_all[q].astype(F32)
            sg = _sig(gt)
            dgs_.append(dact * u_ * (sg * (1.0 + gt * (1.0 - sg))))
            dus_.append(dact * gt * sg)
        return (jnp.stack(dgs_, axis=0), jnp.stack(dus_, axis=0)), ()

    fd_pairs = [((dh2, (tm, D_MODEL), m3) if q == 0 else (None, None, None))
                + (w_fd, (None, D_FF_Q, D_MODEL), (lambda i, j, k, q=q: (q, 0, 0))) for q in range(N_CHIPS)]
    dgate, dup = _mm("ffn_down_bwd", (ni, 1, 1), fd_pairs, "nt",
                     [((N_CHIPS, s, D_FF_Q), BF16, ff_all, ff_all_map)] * 2, epilogue=ffn_bwd_epi,
                     extras=[(gate, ff_all, ff_all_map), (up, ff_all, ff_all_map)], sum_pairs=False, resident_b=True)

    ffq_t = (None, ts, D_FF_Q)
    ffq_tmap = lambda i, j, k: (j, k, 0)
    blk_j = lambda i, j, k: (j, 0, 0)
    d_w_fd = wgrad("d_ffn_down", act, ffq_t, ffq_tmap, dh2, (ts, D_MODEL), tk0, (N_CHIPS, D_FF_Q, D_MODEL),
                   (None, D_FF_Q, D_MODEL), blk_j, N_CHIPS, (D_FF_Q, D_MODEL))
    d_w_fg = wgrad("d_ffn_gate", n2, (ts, D_MODEL), tk0, dgate, ffq_t, ffq_tmap, (N_CHIPS, D_MODEL, D_FF_Q),
                   (None, D_MODEL, D_FF_Q), blk_j, N_CHIPS, (D_MODEL, D_FF_Q))
    d_w_fu = wgrad("d_ffn_up", n2, (ts, D_MODEL), tk0, dup, ffq_t, ffq_tmap, (N_CHIPS, D_MODEL, D_FF_Q),
                   (None, D_MODEL, D_FF_Q), blk_j, N_CHIPS, (D_MODEL, D_FF_Q))

    def norm_bwd_epi(acc, h, d_res, g):
        dh, dg = _rms_bwd_tile(acc, h, g)
        return (d_res + dh,), (dg,)

    ffq_k = lambda i, j, k: (k, i, 0)
    blk_k = lambda i, j, k: (k, 0, 0)
    fi_pairs = []
    for q in range(N_CHIPS):
        a_q = lambda i, j, k, q=q: (q, i, 0)
        b_q = lambda i, j, k, q=q: (q, 0, 0)
        fi_pairs.append((dgate, ffq, a_q, w_fg, (None, D_MODEL, D_FF_Q), b_q))
        fi_pairs.append((dup, ffq, a_q, w_fu, (None, D_MODEL, D_FF_Q), b_q))
    dh1, dg_ffn = _mm("ffn_in_bwd", (ni, 1, 1), fi_pairs, "nt",
                      [((s, D_MODEL), F32, (tm, D_MODEL), m3)], epilogue=norm_bwd_epi,
                      extras=[(h1, (tm, D_MODEL), m3), (dh2, (tm, D_MODEL), m3), (g_ffn, (1, D_MODEL), w3)],
                      acc_outs=[((1, D_MODEL), F32)], resident_b=True)

    d_w_out = wgrad("d_out_proj", mix, (ts, D_MODEL), tk0, dh1, (ts, D_MODEL), tk0, (D_MODEL, D_MODEL),
                    (D_MODEL, D_MODEL), w3, 1, (D_MODEL, D_MODEL))

    def mix_bwd_epi(dm, ga, gs, ad, a, b):
        ga, gs, ad, a, b = (t.astype(F32) for t in (ga, gs, ad, a, b))
        s_a, s_s, s_b = _sig(ga), _sig(gs), _sig(b)
        d_ssm = dm * s_s
        return (dm * ad * s_a * (1.0 - s_a), dm * (a * s_b) * s_s * (1.0 - s_s), dm * s_a, d_ssm * s_b,
                d_ssm * a * s_b * (1.0 - s_b)), ()

    tile_m = (tm, D_MODEL)
    dga, dgs, dattn_d, dpa, dpb = _mm(
        "out_proj_bwd", (ni, 1, 1), [(dh1, tile_m, m3, w_out2, (D_MODEL, D_MODEL), w3)], "nt",
        [((s, D_MODEL), BF16, tile_m, m3)] * 5, epilogue=mix_bwd_epi,
        extras=[(z, tile_m, lambda i, j, k: (i, COL_GA // D_MODEL)), (z, tile_m, lambda i, j, k: (i, COL_GS // D_MODEL)),
                (attn_d, tile_m, m3), (pa, tile_m, m3), (pb, tile_m, m3)])

    d_w_ap = wgrad_cols("d_attn_proj", attn, GROUP_WIDTH, dattn_d)
    d_w_ga = wgrad_cols("d_glu_a", yg, GROUP_WIDTH, dpa)
    d_w_gb = wgrad_cols("d_glu_b", yg, GROUP_WIDTH, dpb)

    ik = lambda i, j, k: (i, k)

    def cols_bwd(dy_, w):
        return [(dy_, (tmb, 256), (lambda i, j, k, q=q: (i, q)), w, (None, GROUP_WIDTH, 256),
                 (lambda i, j, k, q=q: (q, 0, 0))) for q in range(N_CHIPS)]

    (dattn,) = _mm("attn_proj_bwd", (nib, 1, 1), cols_bwd(dattn_d, w_ap), "nt",
                   [((s, GROUP_WIDTH), BF16, (tmb, GROUP_WIDTH), m3)])

    (dys,) = _mm("glu_bwd", (nib, 1, 1), cols_bwd(dpa, w_ga) + cols_bwd(dpb, w_gb), "nt",
                 [((s, GROUP_WIDTH), F32, (tmb, GROUP_WIDTH), m3)],
                 epilogue=lambda acc, y_: ((acc * _gelu_grad(y_),), ()),
                 extras=[(ys, (tmb, GROUP_WIDTH), m3)])

    du, d_lr, d_li, d_bre, d_bim, d_cre, d_cim, d_dskip = _ssm_bwd(
        dys, z, h_re, h_im, b_re_m, b_im_m, c_re_m, c_im_m, lam_re_row, lam_im_row, d_skip_row, chunk)

    dattn_views = _to_views(dattn, tm)
    bwd_in = [(dattn, attn, lse)] + [(dv_, ov_, lv_) for dv_, (ov_, lv_) in zip(dattn_views, merged_views)]
    qkv_grads = [_attn_bwd(*arrs, *dol, dil, cols3)
                 for (arrs, cols3), dol, dil in zip(group_in, bwd_in, GROUP_DILATIONS)]
    dz = _dz_layout(qkv_grads, du, dga, dgs, tabs, tm)

    chip_in = IN_WIDTH // N_CHIPS
    ip_pairs = [(dz, (tm, chip_in), (lambda i, j, k, q=q: (i, q)), w_in, (None, D_MODEL, chip_in),
                 (lambda i, j, k, q=q: (q, 0, 0))) for q in range(N_CHIPS)]
    grad_x, dg_mix = _mm("in_proj_bwd", (ni, 1, 1), ip_pairs, "nt",
                         [((s, D_MODEL), F32, (tm, D_MODEL), m3)], epilogue=norm_bwd_epi,
                         extras=[(x, (tm, D_MODEL), m3), (dh1, (tm, D_MODEL), m3), (g_mix, (1, D_MODEL), w3)],
                         acc_outs=[((1, D_MODEL), F32)], resident_b=True)
    ts_in = min(2048, s)
    (d_w_in,) = _mm("d_in_proj", (1, 8, s // ts_in), [(n1, (ts_in, D_MODEL), tk0, dz, (ts_in, half_in), tkj)], "tn",
                    [((N_CHIPS, D_MODEL, IN_WIDTH // N_CHIPS), F32, (None, D_MODEL, half_in),
                      lambda i, j, k: (j // 2, 0, j % 2))], acc_shape=(D_MODEL, half_in))

    d_bbt_re = _block_diag_t(d_bre, SSM_GROUP, SSM_STATE)
    d_bbt_im = _block_diag_t(d_bim, SSM_GROUP, SSM_STATE)
    d_a_re, d_a_im, d_log_dt, d_bt_re, d_bt_im = _ssm_param_bwd(
        sm["a_re"], sm["a_im"], log_dt_col, bt_re, bt_im,
        d_lr.reshape(SSM_GROUPS, SSM_STATE), d_li.reshape(SSM_GROUPS, SSM_STATE), d_bbt_re, d_bbt_im)
    small = {
        "g_mix": dg_mix, "a_re": d_a_re, "a_im": d_a_im, "log_dt": d_log_dt,
        "b_re": jnp.transpose(d_bt_re, (0, 2, 1)), "b_im": jnp.transpose(d_bt_im, (0, 2, 1)),
        "c_re": jnp.transpose(_block_diag_t(d_cre, SSM_STATE, SSM_GROUP), (0, 2, 1)),
        "c_im": jnp.transpose(_block_diag_t(d_cim, SSM_STATE, SSM_GROUP), (0, 2, 1)),
        "d_skip": d_dskip, "g_ffn": dg_ffn, "g_final": dg_final,
    }
    big = {
        "w_in": d_w_in, "w_attn_proj": d_w_ap, "w_glu_a": d_w_ga, "w_glu_b": d_w_gb,
        "w_out": d_w_out.reshape(N_CHIPS, D_MODEL // N_CHIPS, D_MODEL), "w_ffn_gate": d_w_fg, "w_ffn_up": d_w_fu,
        "w_ffn_down": d_w_fd, "w_ple_gate": d_w_pg.reshape(N_CHIPS, D_MODEL // N_CHIPS, D_MODEL), "w_ple_proj": d_w_pp,
    }
    return loss_acc[0, 0], grad_x, big, small


BIG = ("w_in", "w_attn_proj", "w_glu_a", "w_glu_b", "w_out", "w_ffn_gate", "w_ffn_up", "w_ffn_down", "w_ple_gate",
       "w_ple_proj")
SMALL = ("g_mix", "a_re", "a_im", "log_dt", "b_re", "b_im", "c_re", "c_im", "d_skip", "g_ffn", "g_final")
ANY = pl.BlockSpec(memory_space=pl.ANY)


def _place():
    x, y, c = lax.axis_index("x"), lax.axis_index("y"), lax.axis_index("c")
    chips = [(1 - x, y), (x, 1 - y), (1 - x, 1 - y)]
    return x, y, c, chips


def _remote(src, dst, send_sem, recv_sem, to):
    return pltpu.make_async_remote_copy(src_ref=src, dst_ref=dst, send_sem=send_sem, recv_sem=recv_sem, device_id=to,
                                        device_id_type=MESH)


def _comm_call(name, body, ins, out_shapes, n_sems, aliases=None):
    n_w = len(ins)
    return pl.pallas_call(
        body, name=name, in_specs=[ANY] * n_w, out_specs=[ANY] * len(out_shapes), out_shape=out_shapes,
        scratch_shapes=[pltpu.SemaphoreType.DMA((n,)) for n in n_sems], input_output_aliases=aliases or {},
    )(*ins)


def _gather_weights(bufs):
    n_w = len(bufs)

    def body(*refs):
        outs = refs[n_w:2 * n_w]
        ici_send, ici_recv, d2d_send, d2d_recv = refs[2 * n_w:]
        x, y, c, chips = _place()
        me = 2 * x + y
        sib = (x, y, 1 - c)
        sends = []
        for w in range(n_w):
            for j, (cx, cy) in enumerate(chips):
                k = 3 * w + j
                mine = outs[w].at[me, c]
                cp = _remote(mine, mine, ici_send.at[k], ici_recv.at[k], (cx, cy, c))
                cp.start()
                sends.append(cp)
        for w in range(n_w):
            for j, (cx, cy) in enumerate(chips):
                k = 3 * w + j
                src_chip = 2 * cx + cy
                landed = outs[w].at[src_chip, c]
                _remote(landed, landed, ici_send.at[k], ici_recv.at[k], (cx, cy, c)).wait_recv()
                fwd = _remote(landed, landed, d2d_send.at[k], d2d_recv.at[k], sib)
                fwd.start()
                sends.append(fwd)
        for w in range(n_w):
            for j, (cx, cy) in enumerate(chips):
                k = 3 * w + j
                other = outs[w].at[2 * cx + cy, 1 - c]
                _remote(other, other, d2d_send.at[k], d2d_recv.at[k], sib).wait_recv()
        for cp in sends:
            cp.wait_send()

    out_shapes = [jax.ShapeDtypeStruct(b.shape, b.dtype) for b in bufs]
    return _comm_call("gather_weights", body, bufs, out_shapes, [3 * n_w] * 4, aliases={w: w for w in range(n_w)})


def _pair_exchange(grads):
    n_w = len(grads)

    def body(*refs):
        ins, outs = refs[:n_w], refs[n_w:2 * n_w]
        send, recv = refs[2 * n_w:]
        x, y, c, _ = _place()
        sib = (x, y, 1 - c)
        cps = []
        for w in range(n_w):
            for q in range(N_CHIPS):
                k = N_CHIPS * w + q
                cp = _remote(ins[w].at[q, 1 - c], outs[w].at[q], send.at[k], recv.at[k], sib)
                cp.start()
                cps.append(cp)
        for cp in cps:
            cp.wait()

    out_shapes = [jax.ShapeDtypeStruct((N_CHIPS,) + g.shape[2:], g.dtype) for g in grads]
    return _comm_call("grad_pair_exchange", body, grads, out_shapes, [N_CHIPS * n_w] * 2)


def _chip_exchange(parts):
    n_w = len(parts)

    def body(*refs):
        ins, outs = refs[:n_w], refs[n_w:2 * n_w]
        send, recv = refs[2 * n_w:]
        x, y, c, chips = _place()
        me = 2 * x + y
        cps = []
        for w in range(n_w):
            for j, (cx, cy) in enumerate(chips):
                k = 3 * w + j
                cp = _remote(ins[w].at[2 * cx + cy], outs[w].at[me], send.at[k], recv.at[k], (cx, cy, c))
                cp.start()
                cps.append(cp)
        for w in range(n_w):
            for j, (cx, cy) in enumerate(chips):
                k = 3 * w + j
                got = outs[w].at[2 * cx + cy]
                _remote(got, got, send.at[k], recv.at[k], (cx, cy, c)).wait_recv()
        for cp in cps:
            cp.wait_send()

    out_shapes = [jax.ShapeDtypeStruct(t.shape, t.dtype) for t in parts]
    return _comm_call("grad_chip_exchange", body, parts, out_shapes, [3 * n_w, 3 * n_w])


def _pair_gather(halves):
    n_w = len(halves)

    def body(*refs):
        ins, outs = refs[:n_w], refs[n_w:2 * n_w]
        send, recv = refs[2 * n_w:]
        x, y, c, _ = _place()
        sib = (x, y, 1 - c)
        cps = []
        for w in range(n_w):
            cp = _remote(ins[w], outs[w], send.at[w], recv.at[w], sib)
            cp.start()
            cps.append(cp)
        for cp in cps:
            cp.wait()

    out_shapes = [jax.ShapeDtypeStruct(h.shape, h.dtype) for h in halves]
    return _comm_call("grad_pair_gather", body, halves, out_shapes, [n_w] * 2)


def _all_exchange(vec):
    def body(in_ref, out_ref, send, recv):
        x, y, c, _ = _place()
        me = 4 * x + 2 * y + c
        cps = []
        for k in range(1, 8):
            fx, fy, fc = (k >> 2) & 1, (k >> 1) & 1, k & 1
            to = (x ^ fx, y ^ fy, c ^ fc)
            cp = _remote(in_ref, out_ref.at[me], send.at[k - 1], recv.at[k - 1], to)
            cp.start()
            cps.append(cp)
        for k in range(1, 8):
            fx, fy, fc = (k >> 2) & 1, (k >> 1) & 1, k & 1
            src = 4 * (x ^ fx) + 2 * (y ^ fy) + (c ^ fc)
            got = out_ref.at[src]
            _remote(got, got, send.at[k - 1], recv.at[k - 1], (x ^ fx, y ^ fy, c ^ fc)).wait_recv()
        for cp in cps:
            cp.wait_send()

    return _comm_call("small_all_exchange", body, [vec], [jax.ShapeDtypeStruct((8,) + vec.shape, vec.dtype)], [7, 7])[0]


def _row_tile(r):
    for t in (256, 128, 176, 64, 32, 16, 8):
        if r % t == 0:
            return t
    return r


P_C, P_CHIP, P_DEV = 2, 3, 4


def _cast_into_slot(w2, place):
    r, c = w2.shape
    t = _row_tile(r)
    return _ew("cast_shard", (r // t,), [(w2, (t, c), lambda i, pv: (i, 0))],
               [((N_CHIPS, r, c), BF16, (None, t, c), lambda i, pv: (pv[P_CHIP], i, 0))],
               lambda pids, a: ((a,), ()), place=place)[0]


def _pair_sum(mine, theirs, place):
    _, r, c = theirs.shape
    t = _row_tile(r)
    own = ((None, None, t, c), lambda q, i, pv: (q, pv[P_C], i, 0))
    blk = ((None, t, c), lambda q, i, pv: (q, i, 0))
    return _ew("grad_pair_sum", (N_CHIPS, r // t), [(mine, *own), (theirs, *blk)], [((N_CHIPS, r, c), BF16, *blk)],
               lambda pids, a, b: ((a + b,), ()), place=place)[0]


def _chip_sum(own, got, place):
    _, r, c = own.shape
    t = _row_tile(r)
    ins = []
    for q in range(N_CHIPS):
        ins.append((own, (None, t, c), (lambda i, pv, q=q: (q, i, 0))))
        ins.append((got, (None, t, c), (lambda i, pv, q=q: (jnp.where(pv[P_CHIP] == q, (q + 1) % N_CHIPS, q), i, 0))))

    def fn(pids, *tiles):
        me = pids[0][P_CHIP]
        tot = None
        for q in range(N_CHIPS):
            term = jnp.where(me == q, tiles[2 * q], tiles[2 * q + 1]).astype(F32)
            tot = term if tot is None else tot + term
        return (tot,), ()

    return _ew("grad_chip_sum", (r // t,), ins, [((r, c), F32, (t, c), lambda i, pv: (i, 0))], fn, place=place)[0]


def _adamw_tile(w, g, m, v):
    m = ADAM_B1 * m + (1.0 - ADAM_B1) * g
    v = ADAM_B2 * v + (1.0 - ADAM_B2) * (g * g)
    m_hat = m / (1.0 - ADAM_B1 ** ADAM_STEP)
    v_hat = v / (1.0 - ADAM_B2 ** ADAM_STEP)
    delta = -ADAM_LR * (m_hat / (jnp.sqrt(v_hat) + ADAM_EPS) + ADAM_WD * w)
    return delta, m, v


def _adamw(name, g2, w2, m2, v2):
    r, c = w2.shape
    t = _row_tile(r)
    blk, imap = _rows(t, c)

    def fn(pids, g, w, m, v):
        delta, nm, nv = _adamw_tile(w, g, m, v)
        return (g, delta, nm, nv), ()

    return _ew(name, (r // t,), [(a, blk, imap) for a in (g2, w2, m2, v2)], [((r, c), F32, blk, imap)] * 4, fn)


def _adamw_halves(name, mine, theirs, w2, m2, v2, place):
    r, c = w2.shape
    t = _row_tile(r // 2)
    n_t = (r // 2) // t
    half = ((t, c), lambda h, i, pv: (i, 0))
    whole = ((t, c), lambda h, i, pv: (h * n_t + i, 0))

    def fn(pids, ga, gb, w, m, v):
        g = jnp.where(pids[1] == pids[0][P_C], ga, gb)
        delta, nm, nv = _adamw_tile(w, g, m, v)
        return (g, delta, nm, nv), ()

    return _ew(name, (2, n_t), [(mine, *half), (theirs, *half), (w2, *whole), (m2, *whole), (v2, *whole)],
               [((r, c), F32, *whole)] * 4, fn, place=place)


def _device_sum(own, got, place):
    r, c = own.shape
    t = _row_tile(r)
    ins = [(own, (t, c), lambda i, pv: (i, 0))]
    for q in range(8):
        ins.append((got, (None, t, c), (lambda i, pv, q=q: (jnp.where(pv[P_DEV] == q, (q + 1) % 8, q), i, 0))))

    def fn(pids, mine, *parts):
        me = pids[0][P_DEV]
        tot = None
        for q in range(8):
            term = jnp.where(me == q, mine, parts[q])
            tot = term if tot is None else tot + term
        return (tot,), ()

    return _ew("small_device_sum", (r // t,), ins, [((r, c), F32, (t, c), lambda i, pv: (i, 0))], fn, place=place)[0]


def _pack(parts):
    flat = jnp.concatenate([a.reshape(-1) for a in parts])
    pad = (-flat.shape[0]) % (SUB * 128)
    return jnp.pad(flat, (0, pad)).reshape(-1, 128)


def _unpack(mat, shapes):
    flat = mat.reshape(-1)
    out, off = [], 0
    for shp in shapes:
        n = math.prod(shp)
        out.append(flat[off:off + n].reshape(shp))
        off += n
    return out


def kernel(x, p, positions, g_mix, w_in, a_re, a_im, log_dt, b_re, b_im, c_re, c_im, d_skip, w_attn_proj, w_glu_a, w_glu_b, w_out, g_ffn, w_ffn_gate, w_ffn_up, w_ffn_down, w_ple_gate, w_ple_proj, g_final, loss_target, m_g_mix, m_w_in, m_a_re, m_a_im, m_log_dt, m_b_re, m_b_im, m_c_re, m_c_im, m_d_skip, m_w_attn_proj, m_w_glu_a, m_w_glu_b, m_w_out, m_g_ffn, m_w_ffn_gate, m_w_ffn_up, m_w_ffn_down, m_w_ple_gate, m_w_ple_proj, m_g_final, v_g_mix, v_w_in, v_a_re, v_a_im, v_log_dt, v_b_re, v_b_im, v_c_re, v_c_im, v_d_skip, v_w_attn_proj, v_w_glu_a, v_w_glu_b, v_w_out, v_g_ffn, v_w_ffn_gate, v_w_ffn_up, v_w_ffn_down, v_w_ple_gate, v_w_ple_proj, v_g_final):
    given = dict(locals())
    big_w = {n: given[n] for n in BIG}
    w_mats = {n: big_w[n].reshape(big_w[n].shape[1:]) for n in BIG}

    ax, ay, ac = lax.axis_index("x"), lax.axis_index("y"), lax.axis_index("c")
    place = jnp.stack([ax, ay, ac, 2 * ax + ay, 4 * ax + 2 * ay + ac]).astype(jnp.int32)

    bufs = []
    for n in BIG:
        r, c = w_mats[n].shape
        bufs.append(_cast_into_slot(w_mats[n], place).reshape(N_CHIPS, 2, r // 2, c))
    gathered = _gather_weights(bufs)
    wts = {}
    for n, g in zip(BIG, gathered):
        r, c = w_mats[n].shape
        wts[n] = g.reshape(N_CHIPS, r, c)

    sm = {
        "g_mix": g_mix.reshape(1, D_MODEL), "g_ffn": g_ffn.reshape(1, D_MODEL), "g_final": g_final.reshape(1, D_MODEL),
        "a_re": a_re[0], "a_im": a_im[0], "log_dt": log_dt[0], "b_re": b_re[0], "b_im": b_im[0], "c_re": c_re[0],
        "c_im": c_im[0], "d_skip": d_skip[0],
    }
    s = x.shape[1]
    loss_part, grad_x, big_g, small_g = _local_step(x[0], p[0, 0], positions[0], loss_target[0], sm, wts)

    g5 = []
    for n in BIG:
        r, c = w_mats[n].shape
        g5.append(big_g[n].reshape(N_CHIPS, 2, r // 2, c))
    theirs = _pair_exchange(g5)
    chip_parts = [_pair_sum(g, t, place) for g, t in zip(g5, theirs)]
    chip_got = _chip_exchange(chip_parts)
    halves = [_chip_sum(own, got, place) for own, got in zip(chip_parts, chip_got)]
    other_halves = _pair_gather(halves)

    results = {}
    for n, mine, other in zip(BIG, halves, other_halves):
        r, c = w_mats[n].shape
        shp = big_w[n].shape
        outs = _adamw_halves("adamw_" + n, mine, other, w_mats[n], given["m_" + n].reshape(r, c),
                             given["v_" + n].reshape(r, c), place)
        results[n] = [o.reshape(shp) for o in outs]

    small_shapes = [given[n].shape for n in SMALL]
    vec = _pack([small_g[n] for n in SMALL] + [loss_part.reshape(1)])
    tot = _device_sum(vec, _all_exchange(vec), place)
    n_small = sum(math.prod(shp) for shp in small_shapes)
    loss = tot.reshape(-1)[n_small]
    w_s = _pack([given[n] for n in SMALL])
    m_s = _pack([given["m_" + n] for n in SMALL])
    v_s = _pack([given["v_" + n] for n in SMALL])
    rows_s = w_s.shape[0]
    g_s = tot.reshape(-1)[: rows_s * 128].reshape(rows_s, 128)
    outs_s = _adamw("adamw_small", g_s, w_s, m_s, v_s)
    for kind, mat in enumerate(outs_s):
        for n, arr in zip(SMALL, _unpack(mat, small_shapes)):
            results.setdefault(n, [None] * 4)[kind] = arr

    order = ("g_mix", "w_in", "a_re", "a_im", "log_dt", "b_re", "b_im", "c_re", "c_im", "d_skip", "w_attn_proj", "w_glu_a",
             "w_glu_b", "w_out", "g_ffn", "w_ffn_gate", "w_ffn_up", "w_ffn_down", "w_ple_gate", "w_ple_proj", "g_final")
    out = [loss, grad_x.reshape(1, s, D_MODEL)]
    for kind in range(4):
        out += [results[n][kind] for n in order]
    return tuple(out)
```

```python
import math

import jax
import jax.numpy as jnp
from jax import lax
from jax.experimental import pallas as pl
from jax.experimental.pallas import tpu as pltpu

F32 = jnp.float32
BF16 = jnp.bfloat16

D_MODEL = 1024
HEAD_DIM = 128
HEADS_PER_GROUP = 4
GROUP_WIDTH = HEADS_PER_GROUP * HEAD_DIM
GROUP_DILATIONS = (1, 4, 16)
N_GROUPS = len(GROUP_DILATIONS)
LSE_LANES = 32
LSE_WIDTH = HEADS_PER_GROUP * LSE_LANES
ATTN_BLOCK = 128
ROPE_DIM = 32
ROPE_HALF = 16
ROPE_THETA = 500000.0
SSM_WIDTH = 512
SSM_GROUPS = 32
SSM_GROUP = 16
SSM_STATE = 64
N_STATE = SSM_GROUPS * SSM_STATE
SSM_SUPER = 4
IN_WIDTH = 7168
COL_U = 4608
COL_GA = 5120
COL_GS = 6144
D_FF = 2816
N_CHIPS = 4
D_FF_Q = D_FF // N_CHIPS
PLE_DIM = 256
EPS = 1e-6
ADAM_LR = 0.001
ADAM_B1 = 0.9
ADAM_B2 = 0.999
ADAM_EPS = 1e-08
ADAM_WD = 0.01
ADAM_STEP = 10
NEG_BIG = -1e30
VMEM_LIMIT_BYTES = 56 * 1024 * 1024
MESH = pl.DeviceIdType.MESH

_DIMS = {
    "nn": (((1,), (0,)), ((), ())),
    "nt": (((1,), (1,)), ((), ())),
    "tn": (((0,), (0,)), ((), ())),
}


def _params(n_grid):
    return pltpu.CompilerParams(dimension_semantics=("arbitrary",) * n_grid, vmem_limit_bytes=VMEM_LIMIT_BYTES)


def _sig(v):
    return 1.0 / (1.0 + jnp.exp(-v))


def _dot(a, b, mode):
    return lax.dot_general(a, b, _DIMS[mode], preferred_element_type=F32)


def _mm(name, grid, pairs, mode, outs, epilogue=None, extras=(), acc_outs=(), acc_shape=None, j_outer=False,
        sum_pairs=True, resident_b=False, comm=None):
    gi, gj, gk = grid
    n_p, n_e, n_o, n_a = len(pairs), len(extras), len(outs), len(acc_outs)
    assert not n_a or gj == 1
    assert sum_pairs or gk == 1
    run_grid = (gj, gi, gk) if j_outer else grid
    c_ins = list(comm["ins"]) if comm else []
    c_outs = list(comm["outs"]) if comm else []
    c_sems = list(comm["sems"]) if comm else []
    n_ci, n_co, n_cs = len(c_ins), len(c_outs), len(c_sems)

    def order(imap):
        return (lambda j, i, k: imap(i, j, k)) if j_outer else imap

    shared_a = [pr[0] is None for pr in pairs]
    n_in = 2 * n_p - sum(shared_a)

    def body(*refs):
        pair_refs = list(refs[:n_in])
        extra_refs = refs[n_in: n_in + n_e]
        comm_in = refs[n_in + n_e: n_in + n_e + n_ci]
        at = n_in + n_e + n_ci
        out_refs = refs[at: at + n_o]
        sum_refs = refs[at + n_o: at + n_o + n_a]
        comm_out = refs[at + n_o + n_a: at + n_o + n_a + n_co]
        scratch_refs = refs[at + n_o + n_a + n_co:]
        i = pl.program_id(1 if j_outer else 0)
        k = pl.program_id(2)
        if comm:
            step = (pl.program_id(0) * run_grid[1] + pl.program_id(1)) * run_grid[2] + pl.program_id(2)
            sems = scratch_refs[len(scratch_refs) - n_cs:]
            for at_step, stage in comm["stages"]:
                @pl.when(step == at_step)
                def _(stage=stage):
                    stage(comm_in, comm_out, sems)
        part = None if sum_pairs else []
        a = None
        for t in range(n_p):
            if not shared_a[t]:
                a = pair_refs.pop(0)[...].astype(BF16)
            b = pair_refs.pop(0)[...].astype(BF16)
            d = _dot(a, b, mode)
            if sum_pairs:
                part = d if part is None else part + d
            else:
                part.append(d)

        def finish(acc):
            tiles, sums = epilogue(acc, *[e[...] for e in extra_refs]) if epilogue is not None else ((acc,), ())
            for o_ref, tile in zip(out_refs, tiles):
                o_ref[...] = tile.astype(o_ref.dtype)
            if n_a:
                @pl.when(i == 0)
                def _():
                    for s_ref in sum_refs:
                        s_ref[...] = jnp.zeros_like(s_ref)

                for s_ref, s in zip(sum_refs, sums):
                    s_ref[...] += s

        if gk == 1:
            finish(part)
        else:
            acc_ref = scratch_refs[0]

            @pl.when(k == 0)
            def _():
                acc_ref[...] = part

            @pl.when(k > 0)
            def _():
                acc_ref[...] += part

            @pl.when(k == gk - 1)
            def _():
                finish(acc_ref[...])

    in_specs, args = [], []
    for a, a_block, a_imap, b, b_block, b_imap in pairs:
        if a is not None:
            in_specs.append(pl.BlockSpec(a_block, order(a_imap)))
            args.append(a)
        if resident_b:
            in_specs.append(pl.BlockSpec(b_block, order(b_imap), pipeline_mode=pl.Buffered(1)))
        else:
            in_specs.append(pl.BlockSpec(b_block, order(b_imap)))
        args.append(b)
    for e, e_block, e_imap in extras:
        in_specs.append(pl.BlockSpec(e_block, order(e_imap)))
        args.append(e)
    first_comm_in = len(args)
    for c_in in c_ins:
        in_specs.append(pl.BlockSpec(memory_space=pl.ANY))
        args.append(c_in)
    out_shape = [jax.ShapeDtypeStruct(shape, dtype) for shape, dtype, _, _ in outs]
    out_specs = [pl.BlockSpec(block, order(imap)) for _, _, block, imap in outs]
    for shape, dtype in acc_outs:
        out_shape.append(jax.ShapeDtypeStruct(shape, dtype))
        out_specs.append(pl.BlockSpec(shape, lambda i, j, k: (0, 0)))
    first_comm_out = len(out_shape)
    for c_out in c_outs:
        out_shape.append(c_out)
        out_specs.append(pl.BlockSpec(memory_space=pl.ANY))
    aliases = {first_comm_in + n: first_comm_out + n for n in range(n_ci)} if comm and comm["aliased"] else {}
    scratch = [pltpu.VMEM(acc_shape, F32)] if gk > 1 else []
    scratch += [pltpu.SemaphoreType.DMA((n,)) for n in c_sems]
    return pl.pallas_call(
        body, name=name, grid=run_grid, in_specs=in_specs, out_specs=out_specs,
        out_shape=out_shape, scratch_shapes=scratch, compiler_params=_params(3), input_output_aliases=aliases,
    )(*args)


def _ew(name, grid, ins, outs, fn, acc_outs=(), place=None):
    n_i, n_o, n_a = len(ins), len(outs), len(acc_outs)
    ng = len(grid)
    n_s = 0 if place is None else 1

    def body(*refs):
        in_refs = refs[n_s: n_s + n_i]
        out_refs = refs[n_s + n_i: n_s + n_i + n_o]
        sum_refs = refs[n_s + n_i + n_o:]
        pids = tuple(pl.program_id(a) for a in range(ng))
        if n_s:
            pids = (refs[0],) + pids
        tiles, sums = fn(pids, *[r[...] for r in in_refs])
        for o_ref, tile in zip(out_refs, tiles):
            o_ref[...] = tile.astype(o_ref.dtype)
        if n_a:
            first = pids[0] == 0
            for p_ in pids[1:]:
                first = jnp.logical_and(first, p_ == 0)

            @pl.when(first)
            def _():
                for s_ref in sum_refs:
                    s_ref[...] = jnp.zeros_like(s_ref)

            for s_ref, s in zip(sum_refs, sums):
                s_ref[...] += s

    in_specs = [pl.BlockSpec(block, imap) for _, block, imap in ins]
    out_shape = [jax.ShapeDtypeStruct(shape, dtype) for shape, dtype, _, _ in outs]
    out_specs = [pl.BlockSpec(block, imap) for _, _, block, imap in outs]
    for shape, dtype in acc_outs:
        out_shape.append(jax.ShapeDtypeStruct(shape, dtype))
        out_specs.append(pl.BlockSpec(shape, lambda *_, nd=len(shape): (0,) * nd))
    arrays = [a for a, _, _ in ins]
    if n_s:
        assert not n_a
        spec = pltpu.PrefetchScalarGridSpec(num_scalar_prefetch=1, grid=grid, in_specs=in_specs, out_specs=out_specs)
        return pl.pallas_call(body, name=name, grid_spec=spec, out_shape=out_shape, compiler_params=_params(ng))(
            place, *arrays)
    return pl.pallas_call(
        body, name=name, grid=grid, in_specs=in_specs, out_specs=out_specs, out_shape=out_shape,
        compiler_params=_params(ng),
    )(*arrays)


def _rows(tm, width):
    return (tm, width), (lambda i: (i, 0))


def _rms_fwd_tile(h, g):
    r = lax.rsqrt(jnp.mean(h * h, axis=-1, keepdims=True) + EPS)
    return h * r * g


def _rms_bwd_tile(dn, h, g):
    r = lax.rsqrt(jnp.mean(h * h, axis=-1, keepdims=True) + EPS)
    hhat = h * r
    gy = dn * g
    dh = r * (gy - hhat * jnp.mean(gy * hhat, axis=-1, keepdims=True))
    dg = jnp.sum(dn * hhat, axis=0, keepdims=True)
    return dh, dg


def _rope_tables(pos_col, inv_row, tm):
    s = pos_col.shape[0]

    def fn(pids, pos, inv):
        ang = pos * inv
        lane = lax.broadcasted_iota(jnp.int32, ang.shape, 1)
        cs = jnp.where(lane < ROPE_DIM, jnp.cos(ang), 1.0)
        sn = jnp.sin(ang)
        s_lo = jnp.where(lane < ROPE_HALF, -sn, 0.0)
        s_hi = jnp.where(jnp.logical_and(lane >= ROPE_HALF, lane < ROPE_DIM), sn, 0.0)
        return (cs, s_lo, s_hi), ()

    blk, imap = _rows(tm, 128)
    return _ew(
        "rope_tables", (s // tm,),
        [(pos_col, (tm, 1), lambda i: (i, 0)), (inv_row, (1, 128), lambda i: (0, 0))],
        [((s, 128), F32, blk, imap)] * 3, fn,
    )


def _rope(xh, cs, s_lo, s_hi):
    return xh * cs + pltpu.roll(xh, HEAD_DIM - ROPE_HALF, 1) * s_lo + pltpu.roll(xh, ROPE_HALF, 1) * s_hi


def _rope_t(gh, cs, s_lo, s_hi):
    return gh * cs + pltpu.roll(gh * s_lo, ROPE_HALF, 1) + pltpu.roll(gh * s_hi, HEAD_DIM - ROPE_HALF, 1)


def _attn_geometry(length):
    nb = length // ATTN_BLOCK
    gq = min(4, nb)
    assert nb % gq == 0
    return nb, gq, gq * ATTN_BLOCK, nb // gq


def _band_masks():
    qi = lax.broadcasted_iota(jnp.int32, (ATTN_BLOCK, ATTN_BLOCK), 0)
    kj = lax.broadcasted_iota(jnp.int32, (ATTN_BLOCK, ATTN_BLOCK), 1)
    return kj <= qi, kj >= qi


def _band_mask_pair():
    qi = lax.broadcasted_iota(jnp.int32, (ATTN_BLOCK, 2 * ATTN_BLOCK), 0)
    cj = lax.broadcasted_iota(jnp.int32, (ATTN_BLOCK, 2 * ATTN_BLOCK), 1)
    in_cur = cj >= ATTN_BLOCK
    band = jnp.logical_or(jnp.logical_and(in_cur, cj - ATTN_BLOCK <= qi),
                          jnp.logical_and(cj < ATTN_BLOCK, cj >= qi))
    return band, in_cur


def _attn_fwd(qv, kv, vv, dil, cols3=(0, 0, 0)):
    length = qv.shape[0]
    nb, gq, rows, ni = _attn_geometry(length)

    def body(q_ref, kc_ref, kp_ref, vc_ref, vp_ref, o_ref, l_ref):
        i = pl.program_id(1)
        band, in_cur = _band_mask_pair()
        band_first = jnp.logical_and(band, jnp.logical_or(in_cur, i > 0))
        work = []
        for h in range(HEADS_PER_GROUP):
            cols = slice(h * HEAD_DIM, (h + 1) * HEAD_DIM)
            qh = q_ref[:, cols]
            k_all = jnp.concatenate([kp_ref[:, cols], kc_ref[:, cols]], axis=0)
            v_all = jnp.concatenate([vp_ref[:, cols], vc_ref[:, cols]], axis=0)
            for jj in range(gq):
                rws = slice(jj * ATTN_BLOCK, (jj + 1) * ATTN_BLOCK)
                two = slice(jj * ATTN_BLOCK, (jj + 2) * ATTN_BLOCK)
                work.append(dict(h=h, rws=rws, cols=cols, v=v_all[two], first=jj == 0, s=_dot(qh[rws], k_all[two], "nt")))
        for w in work:
            s = jnp.where(band_first if w["first"] else band, w["s"], NEG_BIG)
            m = jnp.max(s, axis=-1, keepdims=True)
            pexp = jnp.exp(s - m)
            w["den"] = jnp.sum(pexp, axis=-1, keepdims=True)
            w["p"] = pexp.astype(BF16)
            w["lse"] = m + jnp.log(w["den"])
        for w in work:
            o = _dot(w["p"], w["v"], "nn")
            o_ref[w["rws"], w["cols"]] = (o / w["den"]).astype(o_ref.dtype)
            l_ref[w["rws"], w["h"] * LSE_LANES:(w["h"] + 1) * LSE_LANES] = jnp.broadcast_to(w["lse"], (ATTN_BLOCK, LSE_LANES))

    def cur(c):
        return pl.BlockSpec((rows, GROUP_WIDTH), lambda r, i: (i, r + c))

    def prev(c):
        return pl.BlockSpec((ATTN_BLOCK, GROUP_WIDTH), lambda r, i: (jnp.maximum(i * gq - 1, 0), r + c))

    cq, ck, cv = cols3
    return pl.pallas_call(
        body, name=f"attn_fwd_d{dil}", grid=(dil, ni),
        in_specs=[cur(cq), cur(ck), prev(ck), cur(cv), prev(cv)],
        out_specs=[cur(0), pl.BlockSpec((rows, LSE_WIDTH), lambda r, i: (i, r))],
        out_shape=[jax.ShapeDtypeStruct((length, dil * GROUP_WIDTH), BF16),
                   jax.ShapeDtypeStruct((length, dil * LSE_WIDTH), F32)],
        compiler_params=_params(2),
    )(qv, kv, kv, vv, vv)


def _attn_bwd(qv, kv, vv, dov, ov, lv, dil, cols3=(0, 0, 0)):
    length = qv.shape[0]
    nb, gq, rows, ni = _attn_geometry(length)
    out_shape = (length, dil * GROUP_WIDTH)

    def body(qc_ref, qn_ref, kc_ref, kp_ref, vc_ref, vp_ref, doc_ref, don_ref, oc_ref, on_ref, lc_ref, ln_ref,
             dq_ref, dk_ref, dv_ref):
        i = pl.program_id(1)
        _, mask_p = _band_masks()
        band, in_cur = _band_mask_pair()
        band_first = jnp.logical_and(band, jnp.logical_or(in_cur, i > 0))
        has_next = i < ni - 1

        last = slice(gq * ATTN_BLOCK, (gq + 1) * ATTN_BLOCK)
        mask_next = jnp.logical_and(mask_p, has_next)

        def rows_of(jj):
            return slice(jj * ATTN_BLOCK, (jj + 1) * ATTN_BLOCK)

        def keys_of(jj):
            return slice(jj * ATTN_BLOCK, (jj + 2) * ATTN_BLOCK)

        heads = []
        for h in range(HEADS_PER_GROUP):
            cols = slice(h * HEAD_DIM, (h + 1) * HEAD_DIM)
            hd = dict(
                cols=cols, q_c=qc_ref[:, cols], q_n=qn_ref[:, cols],
                k_all=jnp.concatenate([kp_ref[:, cols], kc_ref[:, cols]], axis=0),
                v_all=jnp.concatenate([vp_ref[:, cols], vc_ref[:, cols]], axis=0),
                do_c=doc_ref[:, cols], do_n=don_ref[:, cols],
                l_c=lc_ref[:, h * LSE_LANES:h * LSE_LANES + 1], l_n=ln_ref[:, h * LSE_LANES:h * LSE_LANES + 1],
            )
            hd["dl_c"] = jnp.sum(hd["do_c"].astype(F32) * oc_ref[:, cols].astype(F32), axis=-1, keepdims=True)
            hd["dl_n"] = jnp.sum(hd["do_n"].astype(F32) * on_ref[:, cols].astype(F32), axis=-1, keepdims=True)
            hd["s"] = [_dot(hd["q_c"][rows_of(jj)], hd["k_all"][keys_of(jj)], "nt") for jj in range(gq)]
            hd["dp"] = [_dot(hd["do_c"][rows_of(jj)], hd["v_all"][keys_of(jj)], "nt") for jj in range(gq)]
            hd["s"].append(_dot(hd["q_n"], hd["k_all"][last], "nt"))
            hd["dp"].append(_dot(hd["do_n"], hd["v_all"][last], "nt"))
            heads.append(hd)
        for hd in heads:
            hd["p"], hd["ds"] = [], []
            for jj in range(gq + 1):
                if jj < gq:
                    mask, l_col, delta = (band_first if jj == 0 else band), hd["l_c"][rows_of(jj)], hd["dl_c"][rows_of(jj)]
                else:
                    mask, l_col, delta = mask_next, hd["l_n"], hd["dl_n"]
                p = jnp.where(mask, jnp.exp(hd["s"][jj] - l_col), 0.0)
                hd["p"].append(p.astype(BF16))
                hd["ds"].append((p * (hd["dp"][jj] - delta)).astype(BF16))
        for hd in heads:
            cols = hd["cols"]
            dk_blocks, dv_blocks = [None] * (gq + 1), [None] * (gq + 1)

            def add(lst, idx, val):
                lst[idx] = val if lst[idx] is None else lst[idx] + val

            for jj in range(gq):
                qb, dob = hd["q_c"][rows_of(jj)], hd["do_c"][rows_of(jj)]
                dq_ref[rows_of(jj), cols] = _dot(hd["ds"][jj], hd["k_all"][keys_of(jj)], "nn").astype(dq_ref.dtype)
                dk2 = _dot(hd["ds"][jj], qb, "tn")
                dv2 = _dot(hd["p"][jj], dob, "tn")
                add(dk_blocks, jj, dk2[:ATTN_BLOCK])
                add(dk_blocks, jj + 1, dk2[ATTN_BLOCK:])
                add(dv_blocks, jj, dv2[:ATTN_BLOCK])
                add(dv_blocks, jj + 1, dv2[ATTN_BLOCK:])
            add(dk_blocks, gq, _dot(hd["ds"][gq], hd["q_n"], "tn"))
            add(dv_blocks, gq, _dot(hd["p"][gq], hd["do_n"], "tn"))
            for jj in range(gq):
                dk_ref[rows_of(jj), cols] = dk_blocks[jj + 1].astype(dk_ref.dtype)
                dv_ref[rows_of(jj), cols] = dv_blocks[jj + 1].astype(dv_ref.dtype)

    def cur(c):
        return pl.BlockSpec((rows, GROUP_WIDTH), lambda r, i: (i, r + c))

    def prev(c):
        return pl.BlockSpec((ATTN_BLOCK, GROUP_WIDTH), lambda r, i: (jnp.maximum(i * gq - 1, 0), r + c))

    def nxt(c):
        return pl.BlockSpec((ATTN_BLOCK, GROUP_WIDTH), lambda r, i: (jnp.minimum((i + 1) * gq, nb - 1), r + c))

    cq, ck, cv = cols3
    lse_cur = pl.BlockSpec((rows, LSE_WIDTH), lambda r, i: (i, r))
    lse_next = pl.BlockSpec((ATTN_BLOCK, LSE_WIDTH), lambda r, i: (jnp.minimum((i + 1) * gq, nb - 1), r))
    return pl.pallas_call(
        body, name=f"attn_bwd_d{dil}", grid=(dil, ni),
        in_specs=[cur(cq), nxt(cq), cur(ck), prev(ck), cur(cv), prev(cv), cur(0), nxt(0), cur(0), nxt(0), lse_cur, lse_next],
        out_specs=[cur(0), cur(0), cur(0)],
        out_shape=[jax.ShapeDtypeStruct(out_shape, BF16)] * 3,
        compiler_params=_params(2),
    )(qv, qv, kv, kv, vv, vv, dov, dov, ov, ov, lv, lv)


DILATED = tuple((g, d) for g, d in enumerate(GROUP_DILATIONS) if d > 1)


def _spread(scr, slot, tile, out_ref, dil, col, width=GROUP_WIDTH):
    tm = tile.shape[0]
    buf = scr.at[slot]
    buf[...] = tile
    for r in range(dil):
        c0 = r * width + col
        out_ref[:, c0:c0 + HEAD_DIM] = buf[pl.ds(r, tm // dil, stride=dil), :].astype(out_ref.dtype)


def _collect(scr, slot, in_ref, dil, col, width=GROUP_WIDTH):
    tm = scr.shape[1]
    buf = scr.at[slot]
    for r in range(dil):
        c0 = r * width + col
        buf[pl.ds(r, tm // dil, stride=dil), :] = in_ref[:, c0:c0 + HEAD_DIM].astype(F32)
    return buf[...]


def _view_spec(tm, dil, width=GROUP_WIDTH):
    return pl.BlockSpec((tm // dil, dil * width), lambda i: (i, 0))


def _view_shape(s, dil, dtype, width=GROUP_WIDTH):
    return jax.ShapeDtypeStruct((s // dil, dil * width), dtype)


def _qkv_layout(z, tabs, tm):
    s = z.shape[0]
    scale = 1.0 / math.sqrt(HEAD_DIM)
    qkv_width = 3 * N_GROUPS * GROUP_WIDTH

    def body(z_ref, cs_ref, lo_ref, hi_ref, qk0_ref, *rest):
        views, scr = rest[:-1], rest[-1]
        tabs_ = (cs_ref[...], lo_ref[...], hi_ref[...])
        for part in range(3):
            for g, dil in enumerate(GROUP_DILATIONS):
                if part == 2 and dil == 1:
                    continue
                for h in range(HEADS_PER_GROUP):
                    col = part * N_GROUPS * GROUP_WIDTH + g * GROUP_WIDTH + h * HEAD_DIM
                    t = z_ref[:, col:col + HEAD_DIM].astype(F32)
                    if part < 2:
                        t = _rope(t, *tabs_)
                    if part == 0:
                        t = t * scale
                    if dil == 1:
                        c0 = part * GROUP_WIDTH + h * HEAD_DIM
                        qk0_ref[:, c0:c0 + HEAD_DIM] = t.astype(BF16)
                    else:
                        out = views[3 * [gg for gg, _ in DILATED].index(g) + part]
                        _spread(scr, h, t, out, dil, h * HEAD_DIM)

    row = lambda i: (i, 0)
    out_shape = [jax.ShapeDtypeStruct((s, 2 * GROUP_WIDTH), BF16)]
    out_specs = [pl.BlockSpec((tm, 2 * GROUP_WIDTH), row)]
    for _, dil in DILATED:
        out_shape += [_view_shape(s, dil, BF16)] * 3
        out_specs += [_view_spec(tm, dil)] * 3
    res = pl.pallas_call(
        body, name="qkv_layout", grid=(s // tm,),
        in_specs=[pl.BlockSpec((tm, qkv_width), row)] + [pl.BlockSpec((tm, HEAD_DIM), row)] * 3,
        out_specs=out_specs, out_shape=out_shape,
        scratch_shapes=[pltpu.VMEM((HEADS_PER_GROUP, tm, HEAD_DIM), F32)], compiler_params=_params(1),
    )(z, *tabs)
    return res[0], [tuple(res[1 + 3 * n:4 + 3 * n]) for n in range(len(DILATED))]


def _attn_merge(o0, l0, dilated, tm):
    s = o0.shape[0]
    n_d = len(DILATED)

    def body(*refs):
        o0_ref, l0_ref = refs[:2]
        in_views = refs[2:2 + 2 * n_d]
        attn_ref, lse_ref = refs[2 + 2 * n_d:4 + 2 * n_d]
        out_views = refs[4 + 2 * n_d:4 + 4 * n_d]
        scr = refs[-1]
        l_rows = [l0_ref[...]] + [_collect(scr, n, in_views[2 * n + 1], dil, 0, LSE_WIDTH) for n, (_, dil) in enumerate(DILATED)]
        lse_heads = []
        for h in range(HEADS_PER_GROUP):
            cols = slice(h * HEAD_DIM, (h + 1) * HEAD_DIM)
            os_ = [o0_ref[:, cols].astype(F32)]
            for n, (_, dil) in enumerate(DILATED):
                os_.append(_collect(scr, n_d + n, in_views[2 * n], dil, h * HEAD_DIM))
            ls_ = [lr[:, h * LSE_LANES:h * LSE_LANES + 1] for lr in l_rows]
            m = ls_[0]
            for l_ in ls_[1:]:
                m = jnp.maximum(m, l_)
            es = [jnp.exp(l_ - m) for l_ in ls_]
            den = es[0]
            num = es[0] * os_[0]
            for e, o in zip(es[1:], os_[1:]):
                den = den + e
                num = num + e * o
            attn = num / den
            lse_heads.append(jnp.broadcast_to(m + jnp.log(den), (tm, LSE_LANES)))
            attn_ref[:, cols] = attn.astype(BF16)
            for n, (_, dil) in enumerate(DILATED):
                _spread(scr, 2 * n_d, attn, out_views[2 * n], dil, h * HEAD_DIM)
        lse = jnp.concatenate(lse_heads, axis=1)
        lse_ref[...] = lse
        for n, (_, dil) in enumerate(DILATED):
            _spread(scr, 2 * n_d, lse, out_views[2 * n + 1], dil, 0, LSE_WIDTH)

    row = lambda i: (i, 0)
    nat = pl.BlockSpec((tm, GROUP_WIDTH), row)
    nat_l = pl.BlockSpec((tm, LSE_WIDTH), row)
    in_specs = [nat, nat_l]
    args = [o0, l0]
    out_specs = [nat, nat_l]
    out_shape = [jax.ShapeDtypeStruct((s, GROUP_WIDTH), BF16), jax.ShapeDtypeStruct((s, LSE_WIDTH), F32)]
    for (_, dil), (ov, lv) in zip(DILATED, dilated):
        in_specs += [_view_spec(tm, dil), _view_spec(tm, dil, LSE_WIDTH)]
        args += [ov, lv]
        out_specs += [_view_spec(tm, dil), _view_spec(tm, dil, LSE_WIDTH)]
        out_shape += [_view_shape(s, dil, BF16), _view_shape(s, dil, F32, LSE_WIDTH)]
    res = pl.pallas_call(
        body, name="attn_merge", grid=(s // tm,), in_specs=in_specs, out_specs=out_specs, out_shape=out_shape,
        scratch_shapes=[pltpu.VMEM((2 * n_d + 1, tm, HEAD_DIM), F32)], compiler_params=_params(1),
    )(*args)
    return res[0], res[1], [tuple(res[2 + 2 * n:4 + 2 * n]) for n in range(n_d)]


def _to_views(a, tm):
    s = a.shape[0]

    def body(a_ref, *rest):
        outs, scr = rest[:-1], rest[-1]
        for h in range(HEADS_PER_GROUP):
            t = a_ref[:, h * HEAD_DIM:(h + 1) * HEAD_DIM].astype(F32)
            for n, (_, dil) in enumerate(DILATED):
                _spread(scr, n, t, outs[n], dil, h * HEAD_DIM)

    return pl.pallas_call(
        body, name="to_views", grid=(s // tm,), in_specs=[pl.BlockSpec((tm, GROUP_WIDTH), lambda i: (i, 0))],
        out_specs=[_view_spec(tm, dil) for _, dil in DILATED], out_shape=[_view_shape(s, dil, BF16) for _, dil in DILATED],
        scratch_shapes=[pltpu.VMEM((len(DILATED), tm, HEAD_DIM), F32)], compiler_params=_params(1),
    )(a)


def _dz_layout(grads, du, dga, dgs, tabs, tm):
    s = du.shape[0]
    scale = 1.0 / math.sqrt(HEAD_DIM)

    def body(*refs):
        g_refs = refs[:3 * N_GROUPS]
        du_ref, dga_ref, dgs_ref, cs_ref, lo_ref, hi_ref, dz_ref, scr = refs[3 * N_GROUPS:]
        tabs_ = (cs_ref[...], lo_ref[...], hi_ref[...])
        for part in range(3):
            for g, dil in enumerate(GROUP_DILATIONS):
                src = g_refs[3 * g + part]
                for h in range(HEADS_PER_GROUP):
                    if dil == 1:
                        t = src[:, h * HEAD_DIM:(h + 1) * HEAD_DIM].astype(F32)
                    else:
                        t = _collect(scr, h, src, dil, h * HEAD_DIM)
                    if part < 2:
                        t = _rope_t(t, *tabs_)
                    if part == 0:
                        t = t * scale
                    col = part * N_GROUPS * GROUP_WIDTH + g * GROUP_WIDTH + h * HEAD_DIM
                    dz_ref[:, col:col + HEAD_DIM] = t.astype(BF16)
        dz_ref[:, COL_U:COL_GA] = du_ref[...]
        dz_ref[:, COL_GA:COL_GS] = dga_ref[...]
        dz_ref[:, COL_GS:IN_WIDTH] = dgs_ref[...]

    row = lambda i: (i, 0)
    in_specs, args = [], []
    for (g, dil), trio in zip(enumerate(GROUP_DILATIONS), grads):
        in_specs += [pl.BlockSpec((tm, GROUP_WIDTH), row) if dil == 1 else _view_spec(tm, dil)] * 3
        args += list(trio)
    in_specs += [pl.BlockSpec((tm, SSM_WIDTH), row), pl.BlockSpec((tm, D_MODEL), row), pl.BlockSpec((tm, D_MODEL), row)]
    in_specs += [pl.BlockSpec((tm, HEAD_DIM), row)] * 3
    return pl.pallas_call(
        body, name="dz_layout", grid=(s // tm,), in_specs=in_specs, out_specs=pl.BlockSpec((tm, IN_WIDTH), row),
        out_shape=jax.ShapeDtypeStruct((s, IN_WIDTH), BF16),
        scratch_shapes=[pltpu.VMEM((HEADS_PER_GROUP, tm, HEAD_DIM), F32)], compiler_params=_params(1),
    )(*args, du, dga, dgs, *tabs)


def _discretise(a_re, a_im, log_dt, bt_re, bt_im):
    dt = jnp.exp(log_dt)
    mag = jnp.exp(a_re * dt)
    bar_re = mag * jnp.cos(a_im * dt)
    bar_im = mag * jnp.sin(a_im * dt)
    nr = bar_re - 1.0
    ni = bar_im
    den = a_re * a_re + a_im * a_im
    z_re = (nr * a_re + ni * a_im) / den
    z_im = (ni * a_re - nr * a_im) / den
    bb_re = z_re[:, None, :] * bt_re - z_im[:, None, :] * bt_im
    bb_im = z_re[:, None, :] * bt_im + z_im[:, None, :] * bt_re
    return bar_re, bar_im, bb_re, bb_im


def _ssm_prep(a_re, a_im, log_dt, bt_re, bt_im):
    def body(ar, ai, ld, br, bi, o_lr, o_li, o_br, o_bi):
        lr, li, bbr, bbi = _discretise(ar[...], ai[...], ld[...], br[...], bi[...])
        o_lr[...] = lr
        o_li[...] = li
        o_br[...] = bbr
        o_bi[...] = bbi

    sm = jax.ShapeDtypeStruct((SSM_GROUPS, SSM_STATE), F32)
    bg = jax.ShapeDtypeStruct((SSM_GROUPS, SSM_GROUP, SSM_STATE), F32)
    return pl.pallas_call(body, name="ssm_prep", out_shape=[sm, sm, bg, bg])(a_re, a_im, log_dt, bt_re, bt_im)


def _ssm_param_bwd(a_re, a_im, log_dt, bt_re, bt_im, d_lr, d_li, d_bbr, d_bbi):
    def body(ar, ai, ld, br, bi, g_lr, g_li, g_br, g_bi, o_ar, o_ai, o_ld, o_br, o_bi):
        _, vjp = jax.vjp(_discretise, ar[...], ai[...], ld[...], br[...], bi[...])
        d_ar, d_ai, d_ld, d_br, d_bi = vjp((g_lr[...], g_li[...], g_br[...], g_bi[...]))
        o_ar[...] = d_ar
        o_ai[...] = d_ai
        o_ld[...] = d_ld
        o_br[...] = d_br
        o_bi[...] = d_bi

    sm = jax.ShapeDtypeStruct((SSM_GROUPS, SSM_STATE), F32)
    col = jax.ShapeDtypeStruct((SSM_GROUPS, 1), F32)
    bg = jax.ShapeDtypeStruct((SSM_GROUPS, SSM_GROUP, SSM_STATE), F32)
    return pl.pallas_call(body, name="ssm_param_bwd", out_shape=[sm, sm, col, bg, bg])(
        a_re, a_im, log_dt, bt_re, bt_im, d_lr, d_li, d_bbr, d_bbi)


def _block_diag(t, rows_per, cols_per):
    t4 = t.reshape(SSM_SUPER, 8, rows_per, cols_per)
    eye = jnp.eye(8, dtype=t.dtype)
    return jnp.einsum("bgrc,gh->bgrhc", t4, eye).reshape(SSM_SUPER, 8 * rows_per, 8 * cols_per)


def _block_diag_t(dense, rows_per, cols_per):
    t = dense.reshape(SSM_SUPER, 8, rows_per, 8, cols_per)
    eye = jnp.eye(8, dtype=dense.dtype)
    return jnp.einsum("bgrhc,gh->bgrc", t, eye).reshape(SSM_GROUPS, rows_per, cols_per)


def _gelu(v):
    c = math.sqrt(2.0 / math.pi)
    return 0.5 * v * (1.0 + jnp.tanh(c * (v + 0.044715 * v * v * v)))


def _gelu_grad(v):
    c = math.sqrt(2.0 / math.pi)
    t = jnp.tanh(c * (v + 0.044715 * v * v * v))
    return 0.5 * (1.0 + t) + 0.5 * v * (1.0 - t * t) * c * (1.0 + 3.0 * 0.044715 * v * v)


SUB = 8


SCAN_STEPS = (1, 2, 4)
N_SCAN_TABLES = 2 + 2 * len(SCAN_STEPS)


def _scan_tables(tab_ref, lam_re, lam_im, reverse, conj):
    lr = lam_re
    li = -lam_im if conj else lam_im
    powers = [(lr, li)]
    for _ in range(SUB - 1):
        pr, pi = powers[-1]
        powers.append((pr * lr - pi * li, pr * li + pi * lr))
    row = lax.broadcasted_iota(jnp.int32, (SUB, N_STATE), 0)
    if reverse:
        row = SUB - 1 - row
    wide = lambda v: jnp.broadcast_to(v, (SUB, N_STATE))
    p_re = jnp.zeros((SUB, N_STATE), F32)
    p_im = jnp.zeros((SUB, N_STATE), F32)
    for j in range(SUB):
        p_re = jnp.where(row == j, wide(powers[j][0]), p_re)
        p_im = jnp.where(row == j, wide(powers[j][1]), p_im)
    tab_ref[0] = p_re
    tab_ref[1] = p_im
    for idx, k in enumerate(SCAN_STEPS):
        tab_ref[2 + 2 * idx] = jnp.where(row >= k, wide(powers[k - 1][0]), 0.0)
        tab_ref[3 + 2 * idx] = jnp.where(row >= k, wide(powers[k - 1][1]), 0.0)


def _scan_rows(g_re_ref, g_im_ref, tab_ref, carry, n_rows, reverse):
    last = 0 if reverse else SUB - 1

    def tile_step(tt, state):
        cr, ci = state
        t8 = (n_rows // SUB - 1 - tt) if reverse else tt
        start = pl.multiple_of(t8 * SUB, SUB)
        xr = g_re_ref[pl.ds(start, SUB), :]
        xi = g_im_ref[pl.ds(start, SUB), :]
        for idx, k in enumerate(SCAN_STEPS):
            mr = tab_ref[2 + 2 * idx]
            mi = tab_ref[3 + 2 * idx]
            shift = SUB - k if reverse else k
            sr = pltpu.roll(xr, shift, 0)
            si = pltpu.roll(xi, shift, 0)
            xr, xi = xr + (mr * sr - mi * si), xi + (mr * si + mi * sr)
        pr = tab_ref[0]
        pi = tab_ref[1]
        xr, xi = xr + (pr * cr - pi * ci), xi + (pr * ci + pi * cr)
        g_re_ref[pl.ds(start, SUB), :] = xr
        g_im_ref[pl.ds(start, SUB), :] = xi
        return (jnp.broadcast_to(xr[last:last + 1, :], (SUB, N_STATE)),
                jnp.broadcast_to(xi[last:last + 1, :], (SUB, N_STATE)))

    return lax.fori_loop(0, n_rows // SUB, tile_step, carry)


def _ssm_fwd(z, b_re, b_im, c_re, c_im, lam_re, lam_im, d_skip, chunk):
    s = z.shape[0]

    def body(u_ref, bre, bim, cre, cim, lre, lim, dsk, hre_ref, him_ref, ys_ref, yg_ref, car_re, car_im, tabs):
        i = pl.program_id(0)

        @pl.when(i == 0)
        def _():
            car_re[...] = jnp.zeros_like(car_re)
            car_im[...] = jnp.zeros_like(car_im)
            _scan_tables(tabs, lre[...], lim[...], False, False)

        u = u_ref[...]
        for b in range(SSM_SUPER):
            ub = u[:, b * 128:(b + 1) * 128]
            st = slice(b * 512, (b + 1) * 512)
            hre_ref[:, st] = _dot(ub, bre[b], "nn")
            him_ref[:, st] = _dot(ub, bim[b], "nn")
        sr, si = _scan_rows(hre_ref, him_ref, tabs, (car_re[...], car_im[...]), chunk, False)
        car_re[...] = sr
        car_im[...] = si
        uf = u.astype(F32)
        for b in range(SSM_SUPER):
            st = slice(b * 512, (b + 1) * 512)
            ch = slice(b * 128, (b + 1) * 128)
            y = _dot(hre_ref[:, st].astype(BF16), cre[b], "nn") - _dot(him_ref[:, st].astype(BF16), cim[b], "nn")
            y = y + dsk[:, ch] * uf[:, ch]
            ys_ref[:, ch] = y
            yg_ref[:, ch] = _gelu(y).astype(BF16)

    full3 = lambda i: (0, 0, 0)
    full2 = lambda i: (0, 0)
    row = lambda i: (i, 0)
    u_col = COL_U // SSM_WIDTH
    return pl.pallas_call(
        body, name="ssm_fwd", grid=(s // chunk,),
        in_specs=[pl.BlockSpec((chunk, SSM_WIDTH), lambda i: (i, u_col)),
                  pl.BlockSpec((SSM_SUPER, 128, 512), full3), pl.BlockSpec((SSM_SUPER, 128, 512), full3),
                  pl.BlockSpec((SSM_SUPER, 512, 128), full3), pl.BlockSpec((SSM_SUPER, 512, 128), full3),
                  pl.BlockSpec((1, N_STATE), full2), pl.BlockSpec((1, N_STATE), full2), pl.BlockSpec((1, SSM_WIDTH), full2)],
        out_specs=[pl.BlockSpec((chunk, N_STATE), row), pl.BlockSpec((chunk, N_STATE), row),
                   pl.BlockSpec((chunk, SSM_WIDTH), row), pl.BlockSpec((chunk, SSM_WIDTH), row)],
        out_shape=[jax.ShapeDtypeStruct((s, N_STATE), F32), jax.ShapeDtypeStruct((s, N_STATE), F32),
                   jax.ShapeDtypeStruct((s, SSM_WIDTH), F32), jax.ShapeDtypeStruct((s, SSM_WIDTH), BF16)],
        scratch_shapes=[pltpu.VMEM((SUB, N_STATE), F32), pltpu.VMEM((SUB, N_STATE), F32),
                        pltpu.VMEM((N_SCAN_TABLES, SUB, N_STATE), F32)],
        compiler_params=_params(1),
    )(z, b_re, b_im, c_re, c_im, lam_re, lam_im, d_skip)


def _ssm_bwd(dys, z, h_re, h_im, b_re, b_im, c_re, c_im, lam_re, lam_im, d_skip, chunk):
    s = z.shape[0]
    n_chunks = s // chunk

    def body(dy_ref, u_ref, hre_ref, him_ref, hpr_ref, hpi_ref, bre, bim, cre, cim, lre, lim, dsk,
             du_ref, dlr_ref, dli_ref, dbr_ref, dbi_ref, dcr_ref, dci_ref, dd_ref, are, aim, car_re, car_im, tabs):
        i = pl.program_id(0)
        n = n_chunks - 1 - i

        @pl.when(i == 0)
        def _():
            car_re[...] = jnp.zeros_like(car_re)
            car_im[...] = jnp.zeros_like(car_im)
            _scan_tables(tabs, lre[...], lim[...], True, True)
            for r in (dlr_ref, dli_ref, dbr_ref, dbi_ref, dcr_ref, dci_ref, dd_ref):
                r[...] = jnp.zeros_like(r)

        dy = dy_ref[...]
        dyb = dy.astype(BF16)
        u = u_ref[...]
        for b in range(SSM_SUPER):
            ch = slice(b * 128, (b + 1) * 128)
            st = slice(b * 512, (b + 1) * 512)
            are[:, st] = _dot(dyb[:, ch], cre[b], "nt")
            aim[:, st] = -_dot(dyb[:, ch], cim[b], "nt")
        sr, si = _scan_rows(are, aim, tabs, (car_re[...], car_im[...]), chunk, True)
        car_re[...] = sr
        car_im[...] = si
        row_id = lax.broadcasted_iota(jnp.int32, (chunk, N_STATE), 0)
        top_scale = jnp.where(n > 0, 1.0, 0.0)
        h_r = hre_ref[...]
        h_i = him_ref[...]
        hp_r = jnp.where(row_id == 0, hpr_ref[SUB - 1:SUB, :] * top_scale, pltpu.roll(h_r, 1, 0))
        hp_i = jnp.where(row_id == 0, hpi_ref[SUB - 1:SUB, :] * top_scale, pltpu.roll(h_i, 1, 0))
        a_r = are[...]
        a_i = aim[...]
        dlr_ref[...] += jnp.sum(a_r * hp_r + a_i * hp_i, axis=0, keepdims=True)
        dli_ref[...] += jnp.sum(a_i * hp_r - a_r * hp_i, axis=0, keepdims=True)
        dd_ref[...] += jnp.sum(dy * u.astype(F32), axis=0, keepdims=True)
        a_rb = a_r.astype(BF16)
        a_ib = a_i.astype(BF16)
        h_rb = h_r.astype(BF16)
        h_ib = h_i.astype(BF16)
        for b in range(SSM_SUPER):
            ch = slice(b * 128, (b + 1) * 128)
            st = slice(b * 512, (b + 1) * 512)
            dbr_ref[b] += _dot(u[:, ch], a_rb[:, st], "tn")
            dbi_ref[b] += _dot(u[:, ch], a_ib[:, st], "tn")
            dcr_ref[b] += _dot(h_rb[:, st], dyb[:, ch], "tn")
            dci_ref[b] += -_dot(h_ib[:, st], dyb[:, ch], "tn")
            du = _dot(a_rb[:, st], bre[b], "nt") + _dot(a_ib[:, st], bim[b], "nt") + dsk[:, ch] * dy[:, ch]
            du_ref[:, ch] = du.astype(du_ref.dtype)

    full3 = lambda i: (0, 0, 0)
    full2 = lambda i: (0, 0)
    rev = lambda i: (n_chunks - 1 - i, 0)
    above = lambda i: (jnp.maximum((n_chunks - 1 - i) * (chunk // SUB) - 1, 0), 0)
    u_col = COL_U // SSM_WIDTH
    b_spec = pl.BlockSpec((SSM_SUPER, 128, 512), full3)
    c_spec = pl.BlockSpec((SSM_SUPER, 512, 128), full3)
    vec = pl.BlockSpec((1, N_STATE), full2)
    return pl.pallas_call(
        body, name="ssm_bwd", grid=(n_chunks,),
        in_specs=[pl.BlockSpec((chunk, SSM_WIDTH), rev),
                  pl.BlockSpec((chunk, SSM_WIDTH), lambda i: (n_chunks - 1 - i, u_col)),
                  pl.BlockSpec((chunk, N_STATE), rev), pl.BlockSpec((chunk, N_STATE), rev),
                  pl.BlockSpec((SUB, N_STATE), above), pl.BlockSpec((SUB, N_STATE), above),
                  b_spec, b_spec, c_spec, c_spec, vec, vec, pl.BlockSpec((1, SSM_WIDTH), full2)],
        out_specs=[pl.BlockSpec((chunk, SSM_WIDTH), rev), vec, vec, b_spec, b_spec, c_spec, c_spec,
                   pl.BlockSpec((1, SSM_WIDTH), full2)],
        out_shape=[jax.ShapeDtypeStruct((s, SSM_WIDTH), BF16),
                   jax.ShapeDtypeStruct((1, N_STATE), F32), jax.ShapeDtypeStruct((1, N_STATE), F32),
                   jax.ShapeDtypeStruct((SSM_SUPER, 128, 512), F32), jax.ShapeDtypeStruct((SSM_SUPER, 128, 512), F32),
                   jax.ShapeDtypeStruct((SSM_SUPER, 512, 128), F32), jax.ShapeDtypeStruct((SSM_SUPER, 512, 128), F32),
                   jax.ShapeDtypeStruct((1, SSM_WIDTH), F32)],
        scratch_shapes=[pltpu.VMEM((chunk, N_STATE), F32), pltpu.VMEM((chunk, N_STATE), F32),
                        pltpu.VMEM((SUB, N_STATE), F32), pltpu.VMEM((SUB, N_STATE), F32),
                        pltpu.VMEM((N_SCAN_TABLES, SUB, N_STATE), F32)],
        compiler_params=_params(1),
    )(dys, z, h_re, h_im, h_re, h_im, b_re, b_im, c_re, c_im, lam_re, lam_im, d_skip)


def _local_step(x, p, pos, tgt, sm, w_in, late_bufs, place):
    s = x.shape[0]
    tm = min(512, s)
    ts = min(1024, s)
    chunk = min(256, s)
    ni = s // tm
    nk = s // ts
    g_mix, g_ffn, g_final = sm["g_mix"], sm["g_ffn"], sm["g_final"]
    rowblk, rowmap = _rows(tm, D_MODEL)
    vec1k = ((1, D_MODEL), lambda *_: (0, 0))

    (n1,) = _ew("rms_mix", (ni,), [(x, rowblk, rowmap), (g_mix, *vec1k)], [((s, D_MODEL), BF16, rowblk, rowmap)],
                lambda pids, h, g: ((_rms_fwd_tile(h, g),), ()))

    half_in = IN_WIDTH // 8
    tmb = min(1024, s)
    nib = s // tmb
    n_late = len(late_bufs)
    g_start, g_forward, g_finish = _gather_stages(n_late)
    in_steps = 8 * nib
    gather = dict(ins=late_bufs, outs=[jax.ShapeDtypeStruct(b.shape, b.dtype) for b in late_bufs], aliased=True,
                  sems=[3 * n_late] * 4, stages=[(0, g_start), (in_steps // 2, g_forward), (in_steps - 1, g_finish)])
    z, *late = _mm("in_proj", (nib, 8, 1),
                   [(n1, (tmb, D_MODEL), lambda i, j, k: (i, 0), w_in, (None, D_MODEL, half_in), lambda i, j, k: (j // 2, 0, j % 2))],
                   "nn", [((s, IN_WIDTH), BF16, (tmb, half_in), lambda i, j, k: (i, j))], j_outer=True, comm=gather)
    w_ap, w_ga, w_gb, w_out, w_fg, w_fu, w_fd, w_pg, w_pp = (
        g.reshape(N_CHIPS, 2 * g.shape[2], g.shape[3]) for g in late)
    w_out2 = w_out.reshape(D_MODEL, D_MODEL)
    w_pg2 = w_pg.reshape(D_MODEL, D_MODEL)

    inv = ROPE_THETA ** (-jnp.arange(ROPE_HALF, dtype=F32) * 2.0 / ROPE_DIM)
    inv_row = jnp.concatenate([inv, inv, jnp.zeros((HEAD_DIM - ROPE_DIM,), F32)]).reshape(1, HEAD_DIM)
    tabs = _rope_tables(pos.astype(F32).reshape(s, 1), inv_row, tm)

    qk0, qkv_views = _qkv_layout(z, tabs, tm)
    v0_col = (2 * N_GROUPS * GROUP_WIDTH) // GROUP_WIDTH
    group_in = [((qk0, qk0, z), (0, 1, v0_col))] + [(trio, (0, 0, 0)) for trio in qkv_views]
    fwd_out = [_attn_fwd(*arrs, dil, cols3) for (arrs, cols3), dil in zip(group_in, GROUP_DILATIONS)]
    attn, lse, merged_views = _attn_merge(fwd_out[0][0], fwd_out[0][1], fwd_out[1:], tm)

    def chip_cols(parts):
        return (jnp.concatenate(parts, axis=1),), ()

    def proj_cols(name, a, width, w):
        blk = (None, width, 256)
        pairs = [(a, (tmb, width), lambda i, j, k: (i, 0), w, blk, lambda i, j, k: (0, 0, 0))]
        pairs += [(None, None, None, w, blk, (lambda i, j, k, q=q: (q, 0, 0))) for q in range(1, N_CHIPS)]
        return _mm(name, (nib, 1, 1), pairs, "nn", [((s, D_MODEL), BF16, (tmb, D_MODEL), lambda i, j, k: (i, 0))],
                   epilogue=chip_cols, sum_pairs=False)[0]

    def proj512(name, a, w):
        return proj_cols(name, a, GROUP_WIDTH, w)

    attn_d = proj512("attn_proj", attn, w_ap)

    bt_re = jnp.transpose(sm["b_re"], (0, 2, 1))
    bt_im = jnp.transpose(sm["b_im"], (0, 2, 1))
    log_dt_col = sm["log_dt"].reshape(SSM_GROUPS, 1)
    lam_re, lam_im, bbt_re, bbt_im = _ssm_prep(sm["a_re"], sm["a_im"], log_dt_col, bt_re, bt_im)
    b_re_m = _block_diag(bbt_re, SSM_GROUP, SSM_STATE).astype(BF16)
    b_im_m = _block_diag(bbt_im, SSM_GROUP, SSM_STATE).astype(BF16)
    c_re_m = _block_diag(jnp.transpose(sm["c_re"], (0, 2, 1)), SSM_STATE, SSM_GROUP).astype(BF16)
    c_im_m = _block_diag(jnp.transpose(sm["c_im"], (0, 2, 1)), SSM_STATE, SSM_GROUP).astype(BF16)
    lam_re_row = lam_re.reshape(1, N_STATE)
    lam_im_row = lam_im.reshape(1, N_STATE)
    d_skip_row = sm["d_skip"].reshape(1, SSM_WIDTH)
    h_re, h_im, ys, yg = _ssm_fwd(z, b_re_m, b_im_m, c_re_m, c_im_m, lam_re_row, lam_im_row, d_skip_row, chunk)

    pa = proj512("glu_a", yg, w_ga)
    pb = proj512("glu_b", yg, w_gb)

    ga_blk = ((tm, D_MODEL), lambda i: (i, COL_GA // D_MODEL))
    gs_blk = ((tm, D_MODEL), lambda i: (i, COL_GS // D_MODEL))

    def mix_fn(pids, ga, gs, ad, a, b):
        ga, gs, ad, a, b = (t.astype(F32) for t in (ga, gs, ad, a, b))
        return (_sig(ga) * ad + _sig(gs) * (a * _sig(b)),), ()

    (mix,) = _ew("gate_mix", (ni,), [(z, *ga_blk), (z, *gs_blk), (attn_d, rowblk, rowmap), (pa, rowblk, rowmap),
                                     (pb, rowblk, rowmap)], [((s, D_MODEL), BF16, rowblk, rowmap)], mix_fn)

    def out_epi(acc, xr, g):
        h1 = acc + xr
        return (h1, _rms_fwd_tile(h1, g)), ()

    m3 = lambda i, j, k: (i, 0)
    w3 = lambda i, j, k: (0, 0)
    h1, n2 = _mm("out_proj", (nib, 1, 1), [(mix, (tmb, D_MODEL), m3, w_out2, (D_MODEL, D_MODEL), w3)], "nn",
                 [((s, D_MODEL), F32, (tmb, D_MODEL), m3), ((s, D_MODEL), BF16, (tmb, D_MODEL), m3)],
                 epilogue=out_epi, extras=[(x, (tmb, D_MODEL), m3), (g_ffn, (1, D_MODEL), w3)])

    ffq = (None, tm, D_FF_Q)
    ffq_map = lambda i, j, k: (j, i, 0)

    def ffn_in_epi(parts):
        gts, ups = parts[0::2], parts[1::2]
        acts = [gt * _sig(gt) * u_ for gt, u_ in zip(gts, ups)]
        return (jnp.stack(gts, axis=0), jnp.stack(ups, axis=0), jnp.stack(acts, axis=0)), ()

    w_ffq = (None, D_MODEL, D_FF_Q)
    ff_pairs = []
    for q in range(N_CHIPS):
        blk_q = lambda i, j, k, q=q: (q, 0, 0)
        ff_pairs.append((n2, (tm, D_MODEL), m3, w_fg, w_ffq, blk_q) if q == 0 else (None, None, None, w_fg, w_ffq, blk_q))
        ff_pairs.append((None, None, None, w_fu, w_ffq, blk_q))
    ff_all = (N_CHIPS, tm, D_FF_Q)
    ff_all_map = lambda i, j, k: (0, i, 0)
    gate, up, act = _mm("ffn_gate_up", (ni, 1, 1), ff_pairs, "nn",
                        [((N_CHIPS, s, D_FF_Q), BF16, ff_all, ff_all_map)] * 3, epilogue=ffn_in_epi,
                        sum_pairs=False, resident_b=True)

    (h2,) = _mm("ffn_down", (nib, 1, 1),
                [(act, (None, tmb, D_FF_Q), (lambda i, j, k, q=q: (q, i, 0)), w_fd, (None, D_FF_Q, D_MODEL),
                  (lambda i, j, k, q=q: (q, 0, 0))) for q in range(N_CHIPS)], "nn",
                [((s, D_MODEL), F32, (tmb, D_MODEL), m3)], epilogue=lambda acc, hr: ((acc + hr,), ()),
                extras=[(h1, (tmb, D_MODEL), m3)])

    pp = proj_cols("ple_proj", p, PLE_DIM, w_pp)

    def ple_epi(acc, hr, ppr):
        return (acc, hr + _sig(acc) * ppr.astype(F32)), ()

    gl, h3 = _mm("ple_gate", (nib, 1, 1), [(h2, (tmb, D_MODEL), m3, w_pg2, (D_MODEL, D_MODEL), w3)], "nn",
                 [((s, D_MODEL), BF16, (tmb, D_MODEL), m3), ((s, D_MODEL), F32, (tmb, D_MODEL), m3)],
                 epilogue=ple_epi, extras=[(h2, (tmb, D_MODEL), m3), (pp, (tmb, D_MODEL), m3)])

    def head_fn(pids, h, t, g, g_, ppr):
        r = lax.rsqrt(jnp.mean(h * h, axis=-1, keepdims=True) + EPS)
        hhat = h * r
        diff = hhat * g - t
        loss = 0.5 * jnp.sum(jnp.mean(diff * diff, axis=-1, keepdims=True))
        dy = diff * (1.0 / D_MODEL)
        gy = dy * g
        dh = r * (gy - hhat * jnp.mean(gy * hhat, axis=-1, keepdims=True))
        sg = _sig(g_.astype(F32))
        return ((dh, dh * ppr.astype(F32) * sg * (1.0 - sg), dh * sg),
                (jnp.full((SUB, 128), loss, F32), jnp.sum(dy * hhat, axis=0, keepdims=True)))

    dh3, dgl, dpp, loss_acc, dg_final = _ew(
        "loss_head", (ni,),
        [(h3, rowblk, rowmap), (tgt, rowblk, rowmap), (g_final, *vec1k), (gl, rowblk, rowmap), (pp, rowblk, rowmap)],
        [((s, D_MODEL), F32, rowblk, rowmap), ((s, D_MODEL), BF16, rowblk, rowmap), ((s, D_MODEL), BF16, rowblk, rowmap)],
        head_fn, acc_outs=[((SUB, 128), F32), ((1, D_MODEL), F32)])

    def wgrad(name, a, a_block, a_imap, b, b_block, b_imap, out_shape, out_block, out_imap, nj, acc_shape):
        return _mm(name, (1, nj, nk), [(a, a_block, a_imap, b, b_block, b_imap)], "tn",
                   [(out_shape, F32, out_block, out_imap)], acc_shape=acc_shape)[0]

    tk0 = lambda i, j, k: (k, 0)
    tkj = lambda i, j, k: (k, j)
    def wgrad_cols(name, a, width, dy_):
        def split(acc):
            return (jnp.stack([acc[:, q * 256:(q + 1) * 256] for q in range(N_CHIPS)], axis=0),), ()

        return _mm(name, (1, 1, nk), [(a, (ts, width), tk0, dy_, (ts, D_MODEL), tk0)], "tn",
                   [((N_CHIPS, width, 256), F32, (N_CHIPS, width, 256), lambda i, j, k: (0, 0, 0))], epilogue=split,
                   acc_shape=(width, D_MODEL))[0]

    d_w_pp = wgrad_cols("d_ple_proj", p, PLE_DIM, dpp)
    d_w_pg = wgrad("d_ple_gate", h2, (ts, D_MODEL), tk0, dgl, (ts, D_MODEL), tk0, (D_MODEL, D_MODEL),
                   (D_MODEL, D_MODEL), w3, 1, (D_MODEL, D_MODEL))

    (dh2,) = _mm("ple_gate_bwd", (nib, 1, 1), [(dgl, (tmb, D_MODEL), m3, w_pg2, (D_MODEL, D_MODEL), w3)], "nt",
                 [((s, D_MODEL), F32, (tmb, D_MODEL), m3)], epilogue=lambda acc, d_: ((acc + d_,), ()),
                 extras=[(dh3, (tmb, D_MODEL), m3)])

    def ffn_bwd_epi(parts, gt_all, u_all):
        dgs_, dus_ = [], []
        for q, dact in enumerate(parts):
            gt, u_ = gt_all[q].astype(F32), u_all[q].astype(F32)
            sg = _sig(gt)
            dgs_.append(dact * u_ * (sg * (1.0 + gt * (1.0 - sg))))
            dus_.append(dact * gt * sg)
        return (jnp.stack(dgs_, axis=0), jnp.stack(dus_, axis=0)), ()

    fd_pairs = [((dh2, (tm, D_MODEL), m3) if q == 0 else (None, None, None))
                + (w_fd, (None, D_FF_Q, D_MODEL), (lambda i, j, k, q=q: (q, 0, 0))) for q in range(N_CHIPS)]
    dgate, dup = _mm("ffn_down_bwd", (ni, 1, 1), fd_pairs, "nt",
                     [((N_CHIPS, s, D_FF_Q), BF16, ff_all, ff_all_map)] * 2, epilogue=ffn_bwd_epi,
                     extras=[(gate, ff_all, ff_all_map), (up, ff_all, ff_all_map)], sum_pairs=False, resident_b=True)

    ffq_t = (None, ts, D_FF_Q)
    ffq_tmap = lambda i, j, k: (j, k, 0)
    blk_j = lambda i, j, k: (j, 0, 0)
    d_w_fd = wgrad("d_ffn_down", act, ffq_t, ffq_tmap, dh2, (ts, D_MODEL), tk0, (N_CHIPS, D_FF_Q, D_MODEL),
                   (None, D_FF_Q, D_MODEL), blk_j, N_CHIPS, (D_FF_Q, D_MODEL))
    d_w_fg = wgrad("d_ffn_gate", n2, (ts, D_MODEL), tk0, dgate, ffq_t, ffq_tmap, (N_CHIPS, D_MODEL, D_FF_Q),
                   (None, D_MODEL, D_FF_Q), blk_j, N_CHIPS, (D_MODEL, D_FF_Q))
    d_w_fu = wgrad("d_ffn_up", n2, (ts, D_MODEL), tk0, dup, ffq_t, ffq_tmap, (N_CHIPS, D_MODEL, D_FF_Q),
                   (None, D_MODEL, D_FF_Q), blk_j, N_CHIPS, (D_MODEL, D_FF_Q))

    def norm_bwd_epi(acc, h, d_res, g):
        dh, dg = _rms_bwd_tile(acc, h, g)
        return (d_res + dh,), (dg,)

    ffq_k = lambda i, j, k: (k, i, 0)
    blk_k = lambda i, j, k: (k, 0, 0)
    fi_pairs = []
    for q in range(N_CHIPS):
        a_q = lambda i, j, k, q=q: (q, i, 0)
        b_q = lambda i, j, k, q=q: (q, 0, 0)
        fi_pairs.append((dgate, ffq, a_q, w_fg, (None, D_MODEL, D_FF_Q), b_q))
        fi_pairs.append((dup, ffq, a_q, w_fu, (None, D_MODEL, D_FF_Q), b_q))
    dh1, dg_ffn = _mm("ffn_in_bwd", (ni, 1, 1), fi_pairs, "nt",
                      [((s, D_MODEL), F32, (tm, D_MODEL), m3)], epilogue=norm_bwd_epi,
                      extras=[(h1, (tm, D_MODEL), m3), (dh2, (tm, D_MODEL), m3), (g_ffn, (1, D_MODEL), w3)],
                      acc_outs=[((1, D_MODEL), F32)], resident_b=True)

    d_w_out = wgrad("d_out_proj", mix, (ts, D_MODEL), tk0, dh1, (ts, D_MODEL), tk0, (D_MODEL, D_MODEL),
                    (D_MODEL, D_MODEL), w3, 1, (D_MODEL, D_MODEL))

    def mix_bwd_epi(dm, ga, gs, ad, a, b):
        ga, gs, ad, a, b = (t.astype(F32) for t in (ga, gs, ad, a, b))
        s_a, s_s, s_b = _sig(ga), _sig(gs), _sig(b)
        d_ssm = dm * s_s
        return (dm * ad * s_a * (1.0 - s_a), dm * (a * s_b) * s_s * (1.0 - s_s), dm * s_a, d_ssm * s_b,
                d_ssm * a * s_b * (1.0 - s_b)), ()

    tile_m = (tm, D_MODEL)
    dga, dgs, dattn_d, dpa, dpb = _mm(
        "out_proj_bwd", (ni, 1, 1), [(dh1, tile_m, m3, w_out2, (D_MODEL, D_MODEL), w3)], "nt",
        [((s, D_MODEL), BF16, tile_m, m3)] * 5, epilogue=mix_bwd_epi,
        extras=[(z, tile_m, lambda i, j, k: (i, COL_GA // D_MODEL)), (z, tile_m, lambda i, j, k: (i, COL_GS // D_MODEL)),
                (attn_d, tile_m, m3), (pa, tile_m, m3), (pb, tile_m, m3)])

    d_w_ap = wgrad_cols("d_attn_proj", attn, GROUP_WIDTH, dattn_d)
    d_w_ga = wgrad_cols("d_glu_a", yg, GROUP_WIDTH, dpa)
    d_w_gb = wgrad_cols("d_glu_b", yg, GROUP_WIDTH, dpb)

    ik = lambda i, j, k: (i, k)

    def cols_bwd(dy_, w):
        return [(dy_, (tmb, 256), (lambda i, j, k, q=q: (i, q)), w, (None, GROUP_WIDTH, 256),
                 (lambda i, j, k, q=q: (q, 0, 0))) for q in range(N_CHIPS)]

    (dattn,) = _mm("attn_proj_bwd", (nib, 1, 1), cols_bwd(dattn_d, w_ap), "nt",
                   [((s, GROUP_WIDTH), BF16, (tmb, GROUP_WIDTH), m3)])

    (dys,) = _mm("glu_bwd", (nib, 1, 1), cols_bwd(dpa, w_ga) + cols_bwd(dpb, w_gb), "nt",
                 [((s, GROUP_WIDTH), F32, (tmb, GROUP_WIDTH), m3)],
                 epilogue=lambda acc, y_: ((acc * _gelu_grad(y_),), ()),
                 extras=[(ys, (tmb, GROUP_WIDTH), m3)])

    du, d_lr, d_li, d_bre, d_bim, d_cre, d_cim, d_dskip = _ssm_bwd(
        dys, z, h_re, h_im, b_re_m, b_im_m, c_re_m, c_im_m, lam_re_row, lam_im_row, d_skip_row, chunk)

    dattn_views = _to_views(dattn, tm)
    bwd_in = [(dattn, attn, lse)] + [(dv_, ov_, lv_) for dv_, (ov_, lv_) in zip(dattn_views, merged_views)]
    qkv_grads = [_attn_bwd(*arrs, *dol, dil, cols3)
                 for (arrs, cols3), dol, dil in zip(group_in, bwd_in, GROUP_DILATIONS)]
    dz = _dz_layout(qkv_grads, du, dga, dgs, tabs, tm)

    early = [d_w_ap, d_w_ga, d_w_gb, d_w_out.reshape(N_CHIPS, D_MODEL // N_CHIPS, D_MODEL), d_w_fg, d_w_fu, d_w_fd,
             d_w_pg.reshape(N_CHIPS, D_MODEL // N_CHIPS, D_MODEL), d_w_pp]
    early5 = [g.reshape(N_CHIPS, 2, g.shape[1] // 2, g.shape[2]) for g in early]
    early_parts = [_pair_sum(g, t, place) for g, t in zip(early5, _pair_exchange(early5))]
    x_start, x_finish = _chip_exchange_stages(len(early_parts))
    exchange = dict(ins=early_parts, outs=[jax.ShapeDtypeStruct(t.shape, t.dtype) for t in early_parts], aliased=False,
                    sems=[3 * len(early_parts)] * 2, stages=[(0, x_start), (ni - 1, x_finish)])

    chip_in = IN_WIDTH // N_CHIPS
    ip_pairs = [(dz, (tm, chip_in), (lambda i, j, k, q=q: (i, q)), w_in, (None, D_MODEL, chip_in),
                 (lambda i, j, k, q=q: (q, 0, 0))) for q in range(N_CHIPS)]
    grad_x, dg_mix, *early_got = _mm("in_proj_bwd", (ni, 1, 1), ip_pairs, "nt",
                                     [((s, D_MODEL), F32, (tm, D_MODEL), m3)], epilogue=norm_bwd_epi,
                                     extras=[(x, (tm, D_MODEL), m3), (dh1, (tm, D_MODEL), m3), (g_mix, (1, D_MODEL), w3)],
                                     acc_outs=[((1, D_MODEL), F32)], resident_b=True, comm=exchange)
    ts_in = min(2048, s)
    (d_w_in,) = _mm("d_in_proj", (1, 8, s // ts_in), [(n1, (ts_in, D_MODEL), tk0, dz, (ts_in, half_in), tkj)], "tn",
                    [((N_CHIPS, D_MODEL, IN_WIDTH // N_CHIPS), F32, (None, D_MODEL, half_in),
                      lambda i, j, k: (j // 2, 0, j % 2))], acc_shape=(D_MODEL, half_in))

    d_bbt_re = _block_diag_t(d_bre, SSM_GROUP, SSM_STATE)
    d_bbt_im = _block_diag_t(d_bim, SSM_GROUP, SSM_STATE)
    d_a_re, d_a_im, d_log_dt, d_bt_re, d_bt_im = _ssm_param_bwd(
        sm["a_re"], sm["a_im"], log_dt_col, bt_re, bt_im,
        d_lr.reshape(SSM_GROUPS, SSM_STATE), d_li.reshape(SSM_GROUPS, SSM_STATE), d_bbt_re, d_bbt_im)
    small = {
        "g_mix": dg_mix, "a_re": d_a_re, "a_im": d_a_im, "log_dt": d_log_dt,
        "b_re": jnp.transpose(d_bt_re, (0, 2, 1)), "b_im": jnp.transpose(d_bt_im, (0, 2, 1)),
        "c_re": jnp.transpose(_block_diag_t(d_cre, SSM_STATE, SSM_GROUP), (0, 2, 1)),
        "c_im": jnp.transpose(_block_diag_t(d_cim, SSM_STATE, SSM_GROUP), (0, 2, 1)),
        "d_skip": d_dskip, "g_ffn": dg_ffn, "g_final": dg_final,
    }
    return loss_acc[0, 0], grad_x, d_w_in, early_parts, early_got, small


BIG = ("w_in", "w_attn_proj", "w_glu_a", "w_glu_b", "w_out", "w_ffn_gate", "w_ffn_up", "w_ffn_down", "w_ple_gate",
       "w_ple_proj")
SMALL = ("g_mix", "a_re", "a_im", "log_dt", "b_re", "b_im", "c_re", "c_im", "d_skip", "g_ffn", "g_final")
ANY = pl.BlockSpec(memory_space=pl.ANY)


def _place():
    x, y, c = lax.axis_index("x"), lax.axis_index("y"), lax.axis_index("c")
    chips = [(1 - x, y), (x, 1 - y), (1 - x, 1 - y)]
    return x, y, c, chips


def _remote(src, dst, send_sem, recv_sem, to):
    return pltpu.make_async_remote_copy(src_ref=src, dst_ref=dst, send_sem=send_sem, recv_sem=recv_sem, device_id=to,
                                        device_id_type=MESH)


def _comm_call(name, body, ins, out_shapes, n_sems, aliases=None):
    n_w = len(ins)
    return pl.pallas_call(
        body, name=name, in_specs=[ANY] * n_w, out_specs=[ANY] * len(out_shapes), out_shape=out_shapes,
        scratch_shapes=[pltpu.SemaphoreType.DMA((n,)) for n in n_sems], input_output_aliases=aliases or {},
    )(*ins)


def _gather_weights(bufs):
    n_w = len(bufs)
    start, forward, finish = _gather_stages(n_w)

    def body(*refs):
        ins, outs, sems = refs[:n_w], refs[n_w:2 * n_w], refs[2 * n_w:]
        start(ins, outs, sems)
        forward(ins, outs, sems)
        finish(ins, outs, sems)

    out_shapes = [jax.ShapeDtypeStruct(b.shape, b.dtype) for b in bufs]
    return _comm_call("gather_weights", body, bufs, out_shapes, [3 * n_w] * 4, aliases={w: w for w in range(n_w)})


def _gather_stages(n_w):
    def each():
        x, y, c, chips = _place()
        for w in range(n_w):
            for j, (cx, cy) in enumerate(chips):
                yield w, 3 * w + j, 2 * x + y, 2 * cx + cy, (cx, cy, c), (x, y, 1 - c), c

    def start(ins, outs, sems):
        for w, k, me, _, peer, _, c in each():
            mine = outs[w].at[me, c]
            _remote(mine, mine, sems[0].at[k], sems[1].at[k], peer).start()

    def forward(ins, outs, sems):
        for w, k, _, src_chip, peer, sib, c in each():
            landed = outs[w].at[src_chip, c]
            _remote(landed, landed, sems[0].at[k], sems[1].at[k], peer).wait_recv()
            _remote(landed, landed, sems[2].at[k], sems[3].at[k], sib).start()

    def finish(ins, outs, sems):
        for w, k, me, src_chip, peer, sib, c in each():
            other = outs[w].at[src_chip, 1 - c]
            _remote(other, other, sems[2].at[k], sems[3].at[k], sib).wait_recv()
        for w, k, me, src_chip, peer, sib, c in each():
            mine = outs[w].at[me, c]
            _remote(mine, mine, sems[0].at[k], sems[1].at[k], peer).wait_send()
            landed = outs[w].at[src_chip, c]
            _remote(landed, landed, sems[2].at[k], sems[3].at[k], sib).wait_send()

    return start, forward, finish


def _pair_exchange(grads):
    n_w = len(grads)

    def body(*refs):
        ins, outs = refs[:n_w], refs[n_w:2 * n_w]
        send, recv = refs[2 * n_w:]
        x, y, c, _ = _place()
        sib = (x, y, 1 - c)
        cps = []
        for w in range(n_w):
            for q in range(N_CHIPS):
                k = N_CHIPS * w + q
                cp = _remote(ins[w].at[q, 1 - c], outs[w].at[q], send.at[k], recv.at[k], sib)
                cp.start()
                cps.append(cp)
        for cp in cps:
            cp.wait()

    out_shapes = [jax.ShapeDtypeStruct((N_CHIPS,) + g.shape[2:], g.dtype) for g in grads]
    return _comm_call("grad_pair_exchange", body, grads, out_shapes, [N_CHIPS * n_w] * 2)


def _chip_exchange(parts):
    n_w = len(parts)

    start, finish = _chip_exchange_stages(n_w)

    def body(*refs):
        ins, outs, sems = refs[:n_w], refs[n_w:2 * n_w], refs[2 * n_w:]
        start(ins, outs, sems)
        finish(ins, outs, sems)

    out_shapes = [jax.ShapeDtypeStruct(t.shape, t.dtype) for t in parts]
    return _comm_call("grad_chip_exchange", body, parts, out_shapes, [3 * n_w, 3 * n_w])


def _chip_exchange_stages(n_w):
    def each():
        x, y, c, chips = _place()
        for w in range(n_w):
            for j, (cx, cy) in enumerate(chips):
                yield w, 3 * w + j, 2 * x + y, 2 * cx + cy, (cx, cy, c)

    def start(ins, outs, sems):
        for w, k, me, peer_chip, peer in each():
            _remote(ins[w].at[peer_chip], outs[w].at[me], sems[0].at[k], sems[1].at[k], peer).start()

    def finish(ins, outs, sems):
        for w, k, me, peer_chip, peer in each():
            got = outs[w].at[peer_chip]
            _remote(got, got, sems[0].at[k], sems[1].at[k], peer).wait_recv()
        for w, k, me, peer_chip, peer in each():
            _remote(ins[w].at[peer_chip], outs[w].at[me], sems[0].at[k], sems[1].at[k], peer).wait_send()

    return start, finish


def _pair_gather(halves):
    n_w = len(halves)

    def body(*refs):
        ins, outs = refs[:n_w], refs[n_w:2 * n_w]
        send, recv = refs[2 * n_w:]
        x, y, c, _ = _place()
        sib = (x, y, 1 - c)
        cps = []
        for w in range(n_w):
            cp = _remote(ins[w], outs[w], send.at[w], recv.at[w], sib)
            cp.start()
            cps.append(cp)
        for cp in cps:
            cp.wait()

    out_shapes = [jax.ShapeDtypeStruct(h.shape, h.dtype) for h in halves]
    return _comm_call("grad_pair_gather", body, halves, out_shapes, [n_w] * 2)


def _all_exchange(vec):
    def body(in_ref, out_ref, send, recv):
        x, y, c, _ = _place()
        me = 4 * x + 2 * y + c
        cps = []
        for k in range(1, 8):
            fx, fy, fc = (k >> 2) & 1, (k >> 1) & 1, k & 1
            to = (x ^ fx, y ^ fy, c ^ fc)
            cp = _remote(in_ref, out_ref.at[me], send.at[k - 1], recv.at[k - 1], to)
            cp.start()
            cps.append(cp)
        for k in range(1, 8):
            fx, fy, fc = (k >> 2) & 1, (k >> 1) & 1, k & 1
            src = 4 * (x ^ fx) + 2 * (y ^ fy) + (c ^ fc)
            got = out_ref.at[src]
            _remote(got, got, send.at[k - 1], recv.at[k - 1], (x ^ fx, y ^ fy, c ^ fc)).wait_recv()
        for cp in cps:
            cp.wait_send()

    return _comm_call("small_all_exchange", body, [vec], [jax.ShapeDtypeStruct((8,) + vec.shape, vec.dtype)], [7, 7])[0]


def _row_tile(r):
    for t in (256, 128, 176, 64, 32, 16, 8):
        if r % t == 0:
            return t
    return r


P_C, P_CHIP, P_DEV = 2, 3, 4


def _cast_into_slot(w2, place):
    r, c = w2.shape
    t = _row_tile(r)
    return _ew("cast_shard", (r // t,), [(w2, (t, c), lambda i, pv: (i, 0))],
               [((N_CHIPS, r, c), BF16, (None, t, c), lambda i, pv: (pv[P_CHIP], i, 0))],
               lambda pids, a: ((a,), ()), place=place)[0]


def _pair_sum(mine, theirs, place):
    _, r, c = theirs.shape
    t = _row_tile(r)
    own = ((None, None, t, c), lambda q, i, pv: (q, pv[P_C], i, 0))
    blk = ((None, t, c), lambda q, i, pv: (q, i, 0))
    return _ew("grad_pair_sum", (N_CHIPS, r // t), [(mine, *own), (theirs, *blk)], [((N_CHIPS, r, c), BF16, *blk)],
               lambda pids, a, b: ((a + b,), ()), place=place)[0]


def _chip_sum(own, got, place):
    _, r, c = own.shape
    t = _row_tile(r)
    ins = []
    for q in range(N_CHIPS):
        ins.append((own, (None, t, c), (lambda i, pv, q=q: (q, i, 0))))
        ins.append((got, (None, t, c), (lambda i, pv, q=q: (jnp.where(pv[P_CHIP] == q, (q + 1) % N_CHIPS, q), i, 0))))

    def fn(pids, *tiles):
        me = pids[0][P_CHIP]
        tot = None
        for q in range(N_CHIPS):
            term = jnp.where(me == q, tiles[2 * q], tiles[2 * q + 1]).astype(F32)
            tot = term if tot is None else tot + term
        return (tot,), ()

    return _ew("grad_chip_sum", (r // t,), ins, [((r, c), F32, (t, c), lambda i, pv: (i, 0))], fn, place=place)[0]


def _adamw_tile(w, g, m, v):
    m = ADAM_B1 * m + (1.0 - ADAM_B1) * g
    v = ADAM_B2 * v + (1.0 - ADAM_B2) * (g * g)
    m_hat = m / (1.0 - ADAM_B1 ** ADAM_STEP)
    v_hat = v / (1.0 - ADAM_B2 ** ADAM_STEP)
    delta = -ADAM_LR * (m_hat / (jnp.sqrt(v_hat) + ADAM_EPS) + ADAM_WD * w)
    return delta, m, v


def _adamw(name, g2, w2, m2, v2):
    r, c = w2.shape
    t = _row_tile(r)
    blk, imap = _rows(t, c)

    def fn(pids, g, w, m, v):
        delta, nm, nv = _adamw_tile(w, g, m, v)
        return (g, delta, nm, nv), ()

    return _ew(name, (r // t,), [(a, blk, imap) for a in (g2, w2, m2, v2)], [((r, c), F32, blk, imap)] * 4, fn)


def _adamw_halves(name, mine, theirs, w2, m2, v2, place):
    r, c = w2.shape
    t = _row_tile(r // 2)
    n_t = (r // 2) // t
    half = ((t, c), lambda h, i, pv: (i, 0))
    whole = ((t, c), lambda h, i, pv: (h * n_t + i, 0))

    def fn(pids, ga, gb, w, m, v):
        g = jnp.where(pids[1] == pids[0][P_C], ga, gb)
        delta, nm, nv = _adamw_tile(w, g, m, v)
        return (g, delta, nm, nv), ()

    return _ew(name, (2, n_t), [(mine, *half), (theirs, *half), (w2, *whole), (m2, *whole), (v2, *whole)],
               [((r, c), F32, *whole)] * 4, fn, place=place)


def _device_sum(own, got, place):
    r, c = own.shape
    t = _row_tile(r)
    ins = [(own, (t, c), lambda i, pv: (i, 0))]
    for q in range(8):
        ins.append((got, (None, t, c), (lambda i, pv, q=q: (jnp.where(pv[P_DEV] == q, (q + 1) % 8, q), i, 0))))

    def fn(pids, mine, *parts):
        me = pids[0][P_DEV]
        tot = None
        for q in range(8):
            term = jnp.where(me == q, mine, parts[q])
            tot = term if tot is None else tot + term
        return (tot,), ()

    return _ew("small_device_sum", (r // t,), ins, [((r, c), F32, (t, c), lambda i, pv: (i, 0))], fn, place=place)[0]


def _pack(parts):
    flat = jnp.concatenate([a.reshape(-1) for a in parts])
    pad = (-flat.shape[0]) % (SUB * 128)
    return jnp.pad(flat, (0, pad)).reshape(-1, 128)


def _unpack(mat, shapes):
    flat = mat.reshape(-1)
    out, off = [], 0
    for shp in shapes:
        n = math.prod(shp)
        out.append(flat[off:off + n].reshape(shp))
        off += n
    return out


def kernel(x, p, positions, g_mix, w_in, a_re, a_im, log_dt, b_re, b_im, c_re, c_im, d_skip, w_attn_proj, w_glu_a, w_glu_b, w_out, g_ffn, w_ffn_gate, w_ffn_up, w_ffn_down, w_ple_gate, w_ple_proj, g_final, loss_target, m_g_mix, m_w_in, m_a_re, m_a_im, m_log_dt, m_b_re, m_b_im, m_c_re, m_c_im, m_d_skip, m_w_attn_proj, m_w_glu_a, m_w_glu_b, m_w_out, m_g_ffn, m_w_ffn_gate, m_w_ffn_up, m_w_ffn_down, m_w_ple_gate, m_w_ple_proj, m_g_final, v_g_mix, v_w_in, v_a_re, v_a_im, v_log_dt, v_b_re, v_b_im, v_c_re, v_c_im, v_d_skip, v_w_attn_proj, v_w_glu_a, v_w_glu_b, v_w_out, v_g_ffn, v_w_ffn_gate, v_w_ffn_up, v_w_ffn_down, v_w_ple_gate, v_w_ple_proj, v_g_final):
    given = dict(locals())
    big_w = {n: given[n] for n in BIG}
    w_mats = {n: big_w[n].reshape(big_w[n].shape[1:]) for n in BIG}

    ax, ay, ac = lax.axis_index("x"), lax.axis_index("y"), lax.axis_index("c")
    place = jnp.stack([ax, ay, ac, 2 * ax + ay, 4 * ax + 2 * ay + ac]).astype(jnp.int32)

    bufs = []
    for n in BIG:
        r, c = w_mats[n].shape
        bufs.append(_cast_into_slot(w_mats[n], place).reshape(N_CHIPS, 2, r // 2, c))
    w_in_all = _gather_weights(bufs[:1])[0].reshape((N_CHIPS,) + w_mats["w_in"].shape)

    sm = {
        "g_mix": g_mix.reshape(1, D_MODEL), "g_ffn": g_ffn.reshape(1, D_MODEL), "g_final": g_final.reshape(1, D_MODEL),
        "a_re": a_re[0], "a_im": a_im[0], "log_dt": log_dt[0], "b_re": b_re[0], "b_im": b_im[0], "c_re": c_re[0],
        "c_im": c_im[0], "d_skip": d_skip[0],
    }
    s = x.shape[1]
    loss_part, grad_x, d_w_in, early_parts, early_got, small_g = _local_step(
        x[0], p[0, 0], positions[0], loss_target[0], sm, w_in_all, bufs[1:], place)

    r_in, c_in = w_mats["w_in"].shape
    g5_in = [d_w_in.reshape(N_CHIPS, 2, r_in // 2, c_in)]
    in_parts = [_pair_sum(g, t, place) for g, t in zip(g5_in, _pair_exchange(g5_in))]
    chip_parts = in_parts + list(early_parts)
    chip_got = list(_chip_exchange(in_parts)) + list(early_got)
    halves = [_chip_sum(own, got, place) for own, got in zip(chip_parts, chip_got)]
    other_halves = _pair_gather(halves)

    results = {}
    for n, mine, other in zip(BIG, halves, other_halves):
        r, c = w_mats[n].shape
        shp = big_w[n].shape
        outs = _adamw_halves("adamw_" + n, mine, other, w_mats[n], given["m_" + n].reshape(r, c),
                             given["v_" + n].reshape(r, c), place)
        results[n] = [o.reshape(shp) for o in outs]

    small_shapes = [given[n].shape for n in SMALL]
    vec = _pack([small_g[n] for n in SMALL] + [loss_part.reshape(1)])
    tot = _device_sum(vec, _all_exchange(vec), place)
    n_small = sum(math.prod(shp) for shp in small_shapes)
    loss = tot.reshape(-1)[n_small]
    w_s = _pack([given[n] for n in SMALL])
    m_s = _pack([given["m_" + n] for n in SMALL])
    v_s = _pack([given["v_" + n] for n in SMALL])
    rows_s = w_s.shape[0]
    g_s = tot.reshape(-1)[: rows_s * 128].reshape(rows_s, 128)
    outs_s = _adamw("adamw_small", g_s, w_s, m_s, v_s)
    for kind, mat in enumerate(outs_s):
        for n, arr in zip(SMALL, _unpack(mat, small_shapes)):
            results.setdefault(n, [None] * 4)[kind] = arr

    order = ("g_mix", "w_in", "a_re", "a_im", "log_dt", "b_re", "b_im", "c_re", "c_im", "d_skip", "w_attn_proj", "w_glu_a",
             "w_glu_b", "w_out", "g_ffn", "w_ffn_gate", "w_ffn_up", "w_ffn_down", "w_ple_gate", "w_ple_proj", "g_final")
    out = [loss, grad_x.reshape(1, s, D_MODEL)]
    for kind in range(4):
        out += [results[n][kind] for n in order]
    return tuple(out)
```

```python
import math

import jax
import jax.numpy as jnp
from jax import lax
from jax.experimental import pallas as pl
from jax.experimental.pallas import tpu as pltpu

F32 = jnp.float32
BF16 = jnp.bfloat16

D_MODEL = 1024
HEAD_DIM = 128
HEADS_PER_GROUP = 4
GROUP_WIDTH = HEADS_PER_GROUP * HEAD_DIM
GROUP_DILATIONS = (1, 4, 16)
N_GROUPS = len(GROUP_DILATIONS)
LSE_LANES = 32
LSE_WIDTH = HEADS_PER_GROUP * LSE_LANES
ATTN_BLOCK = 128
ROPE_DIM = 32
ROPE_HALF = 16
ROPE_THETA = 500000.0
SSM_WIDTH = 512
SSM_GROUPS = 32
SSM_GROUP = 16
SSM_STATE = 64
N_STATE = SSM_GROUPS * SSM_STATE
SSM_SUPER = 4
IN_WIDTH = 7168
COL_U = 4608
COL_GA = 5120
COL_GS = 6144
D_FF = 2816
N_CHIPS = 4
D_FF_Q = D_FF // N_CHIPS
PLE_DIM = 256
EPS = 1e-6
ADAM_LR = 0.001
ADAM_B1 = 0.9
ADAM_B2 = 0.999
ADAM_EPS = 1e-08
ADAM_WD = 0.01
ADAM_STEP = 10
NEG_BIG = -1e30
VMEM_LIMIT_BYTES = 56 * 1024 * 1024
MESH = pl.DeviceIdType.MESH

_DIMS = {
    "nn": (((1,), (0,)), ((), ())),
    "nt": (((1,), (1,)), ((), ())),
    "tn": (((0,), (0,)), ((), ())),
}


def _params(n_grid):
    return pltpu.CompilerParams(dimension_semantics=("arbitrary",) * n_grid, vmem_limit_bytes=VMEM_LIMIT_BYTES)


def _sig(v):
    return 1.0 / (1.0 + jnp.exp(-v))


def _dot(a, b, mode):
    return lax.dot_general(a, b, _DIMS[mode], preferred_element_type=F32)


def _mm(name, grid, pairs, mode, outs, epilogue=None, extras=(), acc_outs=(), acc_shape=None, j_outer=False,
        sum_pairs=True, resident_b=False, comm=None):
    gi, gj, gk = grid
    n_p, n_e, n_o, n_a = len(pairs), len(extras), len(outs), len(acc_outs)
    assert not n_a or gj == 1
    assert sum_pairs or gk == 1
    run_grid = (gj, gi, gk) if j_outer else grid
    c_ins = list(comm["ins"]) if comm else []
    c_outs = list(comm["outs"]) if comm else []
    c_sems = list(comm["sems"]) if comm else []
    n_ci, n_co, n_cs = len(c_ins), len(c_outs), len(c_sems)

    def order(imap):
        return (lambda j, i, k: imap(i, j, k)) if j_outer else imap

    shared_a = [pr[0] is None for pr in pairs]
    n_in = 2 * n_p - sum(shared_a)

    def body(*refs):
        pair_refs = list(refs[:n_in])
        extra_refs = refs[n_in: n_in + n_e]
        comm_in = refs[n_in + n_e: n_in + n_e + n_ci]
        at = n_in + n_e + n_ci
        out_refs = refs[at: at + n_o]
        sum_refs = refs[at + n_o: at + n_o + n_a]
        comm_out = refs[at + n_o + n_a: at + n_o + n_a + n_co]
        scratch_refs = refs[at + n_o + n_a + n_co:]
        i = pl.program_id(1 if j_outer else 0)
        k = pl.program_id(2)
        if comm:
            step = (pl.program_id(0) * run_grid[1] + pl.program_id(1)) * run_grid[2] + pl.program_id(2)
            sems = scratch_refs[len(scratch_refs) - n_cs:]
            for at_step, stage in comm["stages"]:
                @pl.when(step == at_step)
                def _(stage=stage):
                    stage(comm_in, comm_out, sems)
        part = None if sum_pairs else []
        a = None
        for t in range(n_p):
            if not shared_a[t]:
                a = pair_refs.pop(0)[...].astype(BF16)
            b = pair_refs.pop(0)[...].astype(BF16)
            d = _dot(a, b, mode)
            if sum_pairs:
                part = d if part is None else part + d
            else:
                part.append(d)

        def finish(acc):
            tiles, sums = epilogue(acc, *[e[...] for e in extra_refs]) if epilogue is not None else ((acc,), ())
            for o_ref, tile in zip(out_refs, tiles):
                o_ref[...] = tile.astype(o_ref.dtype)
            if n_a:
                @pl.when(i == 0)
                def _():
                    for s_ref in sum_refs:
                        s_ref[...] = jnp.zeros_like(s_ref)

                for s_ref, s in zip(sum_refs, sums):
                    s_ref[...] += s

        if gk == 1:
            finish(part)
        else:
            acc_ref = scratch_refs[0]

            @pl.when(k == 0)
            def _():
                acc_ref[...] = part

            @pl.when(k > 0)
            def _():
                acc_ref[...] += part

            @pl.when(k == gk - 1)
            def _():
                finish(acc_ref[...])

    in_specs, args = [], []
    for a, a_block, a_imap, b, b_block, b_imap in pairs:
        if a is not None:
            in_specs.append(pl.BlockSpec(a_block, order(a_imap)))
            args.append(a)
        if resident_b:
            in_specs.append(pl.BlockSpec(b_block, order(b_imap), pipeline_mode=pl.Buffered(1)))
        else:
            in_specs.append(pl.BlockSpec(b_block, order(b_imap)))
        args.append(b)
    for e, e_block, e_imap in extras:
        in_specs.append(pl.BlockSpec(e_block, order(e_imap)))
        args.append(e)
    first_comm_in = len(args)
    for c_in in c_ins:
        in_specs.append(pl.BlockSpec(memory_space=pl.ANY))
        args.append(c_in)
    out_shape = [jax.ShapeDtypeStruct(shape, dtype) for shape, dtype, _, _ in outs]
    out_specs = [pl.BlockSpec(block, order(imap)) for _, _, block, imap in outs]
    for shape, dtype in acc_outs:
        out_shape.append(jax.ShapeDtypeStruct(shape, dtype))
        out_specs.append(pl.BlockSpec(shape, lambda i, j, k: (0, 0)))
    first_comm_out = len(out_shape)
    for c_out in c_outs:
        out_shape.append(c_out)
        out_specs.append(pl.BlockSpec(memory_space=pl.ANY))
    aliases = {first_comm_in + n: first_comm_out + n for n in range(n_ci)} if comm and comm["aliased"] else {}
    scratch = [pltpu.VMEM(acc_shape, F32)] if gk > 1 else []
    scratch += [pltpu.SemaphoreType.DMA((n,)) for n in c_sems]
    return pl.pallas_call(
        body, name=name, grid=run_grid, in_specs=in_specs, out_specs=out_specs,
        out_shape=out_shape, scratch_shapes=scratch, compiler_params=_params(3), input_output_aliases=aliases,
    )(*args)


def _ew(name, grid, ins, outs, fn, acc_outs=(), place=None):
    n_i, n_o, n_a = len(ins), len(outs), len(acc_outs)
    ng = len(grid)
    n_s = 0 if place is None else 1

    def body(*refs):
        in_refs = refs[n_s: n_s + n_i]
        out_refs = refs[n_s + n_i: n_s + n_i + n_o]
        sum_refs = refs[n_s + n_i + n_o:]
        pids = tuple(pl.program_id(a) for a in range(ng))
        if n_s:
            pids = (refs[0],) + pids
        tiles, sums = fn(pids, *[r[...] for r in in_refs])
        for o_ref, tile in zip(out_refs, tiles):
            o_ref[...] = tile.astype(o_ref.dtype)
        if n_a:
            first = pids[0] == 0
            for p_ in pids[1:]:
                first = jnp.logical_and(first, p_ == 0)

            @pl.when(first)
            def _():
                for s_ref in sum_refs:
                    s_ref[...] = jnp.zeros_like(s_ref)

            for s_ref, s in zip(sum_refs, sums):
                s_ref[...] += s

    in_specs = [pl.BlockSpec(block, imap) for _, block, imap in ins]
    out_shape = [jax.ShapeDtypeStruct(shape, dtype) for shape, dtype, _, _ in outs]
    out_specs = [pl.BlockSpec(block, imap) for _, _, block, imap in outs]
    for shape, dtype in acc_outs:
        out_shape.append(jax.ShapeDtypeStruct(shape, dtype))
        out_specs.append(pl.BlockSpec(shape, lambda *_, nd=len(shape): (0,) * nd))
    arrays = [a for a, _, _ in ins]
    if n_s:
        assert not n_a
        spec = pltpu.PrefetchScalarGridSpec(num_scalar_prefetch=1, grid=grid, in_specs=in_specs, out_specs=out_specs)
        return pl.pallas_call(body, name=name, grid_spec=spec, out_shape=out_shape, compiler_params=_params(ng))(
            place, *arrays)
    return pl.pallas_call(
        body, name=name, grid=grid, in_specs=in_specs, out_specs=out_specs, out_shape=out_shape,
        compiler_params=_params(ng),
    )(*arrays)


def _rows(tm, width):
    return (tm, width), (lambda i: (i, 0))


def _rms_fwd_tile(h, g):
    r = lax.rsqrt(jnp.mean(h * h, axis=-1, keepdims=True) + EPS)
    return h * r * g


def _rms_bwd_tile(dn, h, g):
    r = lax.rsqrt(jnp.mean(h * h, axis=-1, keepdims=True) + EPS)
    hhat = h * r
    gy = dn * g
    dh = r * (gy - hhat * jnp.mean(gy * hhat, axis=-1, keepdims=True))
    dg = jnp.sum(dn * hhat, axis=0, keepdims=True)
    return dh, dg


def _rope_tables(pos_col, inv_row, tm):
    s = pos_col.shape[0]

    def fn(pids, pos, inv):
        ang = pos * inv
        lane = lax.broadcasted_iota(jnp.int32, ang.shape, 1)
        cs = jnp.where(lane < ROPE_DIM, jnp.cos(ang), 1.0)
        sn = jnp.sin(ang)
        s_lo = jnp.where(lane < ROPE_HALF, -sn, 0.0)
        s_hi = jnp.where(jnp.logical_and(lane >= ROPE_HALF, lane < ROPE_DIM), sn, 0.0)
        return (cs, s_lo, s_hi), ()

    blk, imap = _rows(tm, 128)
    return _ew(
        "rope_tables", (s // tm,),
        [(pos_col, (tm, 1), lambda i: (i, 0)), (inv_row, (1, 128), lambda i: (0, 0))],
        [((s, 128), F32, blk, imap)] * 3, fn,
    )


def _rope(xh, cs, s_lo, s_hi):
    return xh * cs + pltpu.roll(xh, HEAD_DIM - ROPE_HALF, 1) * s_lo + pltpu.roll(xh, ROPE_HALF, 1) * s_hi


def _rope_t(gh, cs, s_lo, s_hi):
    return gh * cs + pltpu.roll(gh * s_lo, ROPE_HALF, 1) + pltpu.roll(gh * s_hi, HEAD_DIM - ROPE_HALF, 1)


def _attn_geometry(length):
    nb = length // ATTN_BLOCK
    gq = min(4, nb)
    assert nb % gq == 0
    return nb, gq, gq * ATTN_BLOCK, nb // gq


def _band_masks():
    qi = lax.broadcasted_iota(jnp.int32, (ATTN_BLOCK, ATTN_BLOCK), 0)
    kj = lax.broadcasted_iota(jnp.int32, (ATTN_BLOCK, ATTN_BLOCK), 1)
    return kj <= qi, kj >= qi


def _band_mask_pair():
    qi = lax.broadcasted_iota(jnp.int32, (ATTN_BLOCK, 2 * ATTN_BLOCK), 0)
    cj = lax.broadcasted_iota(jnp.int32, (ATTN_BLOCK, 2 * ATTN_BLOCK), 1)
    in_cur = cj >= ATTN_BLOCK
    band = jnp.logical_or(jnp.logical_and(in_cur, cj - ATTN_BLOCK <= qi),
                          jnp.logical_and(cj < ATTN_BLOCK, cj >= qi))
    return band, in_cur


def _attn_fwd(qv, kv, vv, dil, cols3=(0, 0, 0)):
    length = qv.shape[0]
    nb, gq, rows, ni = _attn_geometry(length)

    def body(q_ref, kc_ref, kp_ref, vc_ref, vp_ref, o_ref, l_ref):
        i = pl.program_id(1)
        band, in_cur = _band_mask_pair()
        band_first = jnp.logical_and(band, jnp.logical_or(in_cur, i > 0))
        work = []
        for h in range(HEADS_PER_GROUP):
            cols = slice(h * HEAD_DIM, (h + 1) * HEAD_DIM)
            qh = q_ref[:, cols]
            k_all = jnp.concatenate([kp_ref[:, cols], kc_ref[:, cols]], axis=0)
            v_all = jnp.concatenate([vp_ref[:, cols], vc_ref[:, cols]], axis=0)
            for jj in range(gq):
                rws = slice(jj * ATTN_BLOCK, (jj + 1) * ATTN_BLOCK)
                two = slice(jj * ATTN_BLOCK, (jj + 2) * ATTN_BLOCK)
                work.append(dict(h=h, rws=rws, cols=cols, v=v_all[two], first=jj == 0, s=_dot(qh[rws], k_all[two], "nt")))
        for w in work:
            s = jnp.where(band_first if w["first"] else band, w["s"], NEG_BIG)
            m = jnp.max(s, axis=-1, keepdims=True)
            pexp = jnp.exp(s - m)
            w["den"] = jnp.sum(pexp, axis=-1, keepdims=True)
            w["p"] = pexp.astype(BF16)
            w["lse"] = m + jnp.log(w["den"])
        for w in work:
            o = _dot(w["p"], w["v"], "nn")
            o_ref[w["rws"], w["cols"]] = (o / w["den"]).astype(o_ref.dtype)
            l_ref[w["rws"], w["h"] * LSE_LANES:(w["h"] + 1) * LSE_LANES] = jnp.broadcast_to(w["lse"], (ATTN_BLOCK, LSE_LANES))

    def cur(c):
        return pl.BlockSpec((rows, GROUP_WIDTH), lambda r, i: (i, r + c))

    def prev(c):
        return pl.BlockSpec((ATTN_BLOCK, GROUP_WIDTH), lambda r, i: (jnp.maximum(i * gq - 1, 0), r + c))

    cq, ck, cv = cols3
    return pl.pallas_call(
        body, name=f"attn_fwd_d{dil}", grid=(dil, ni),
        in_specs=[cur(cq), cur(ck), prev(ck), cur(cv), prev(cv)],
        out_specs=[cur(0), pl.BlockSpec((rows, LSE_WIDTH), lambda r, i: (i, r))],
        out_shape=[jax.ShapeDtypeStruct((length, dil * GROUP_WIDTH), BF16),
                   jax.ShapeDtypeStruct((length, dil * LSE_WIDTH), F32)],
        compiler_params=_params(2),
    )(qv, kv, kv, vv, vv)


def _attn_bwd(qv, kv, vv, dov, ov, lv, dil, cols3=(0, 0, 0)):
    length = qv.shape[0]
    nb, gq, rows, ni = _attn_geometry(length)
    out_shape = (length, dil * GROUP_WIDTH)

    def body(qc_ref, qn_ref, kc_ref, kp_ref, vc_ref, vp_ref, doc_ref, don_ref, oc_ref, on_ref, lc_ref, ln_ref,
             dq_ref, dk_ref, dv_ref):
        i = pl.program_id(1)
        _, mask_p = _band_masks()
        band, in_cur = _band_mask_pair()
        band_first = jnp.logical_and(band, jnp.logical_or(in_cur, i > 0))
        has_next = i < ni - 1

        last = slice(gq * ATTN_BLOCK, (gq + 1) * ATTN_BLOCK)
        mask_next = jnp.logical_and(mask_p, has_next)

        def rows_of(jj):
            return slice(jj * ATTN_BLOCK, (jj + 1) * ATTN_BLOCK)

        def keys_of(jj):
            return slice(jj * ATTN_BLOCK, (jj + 2) * ATTN_BLOCK)

        heads = []
        for h in range(HEADS_PER_GROUP):
            cols = slice(h * HEAD_DIM, (h + 1) * HEAD_DIM)
            hd = dict(
                cols=cols, q_c=qc_ref[:, cols], q_n=qn_ref[:, cols],
                k_all=jnp.concatenate([kp_ref[:, cols], kc_ref[:, cols]], axis=0),
                v_all=jnp.concatenate([vp_ref[:, cols], vc_ref[:, cols]], axis=0),
                do_c=doc_ref[:, cols], do_n=don_ref[:, cols],
                l_c=lc_ref[:, h * LSE_LANES:h * LSE_LANES + 1], l_n=ln_ref[:, h * LSE_LANES:h * LSE_LANES + 1],
            )
            hd["dl_c"] = jnp.sum(hd["do_c"].astype(F32) * oc_ref[:, cols].astype(F32), axis=-1, keepdims=True)
            hd["dl_n"] = jnp.sum(hd["do_n"].astype(F32) * on_ref[:, cols].astype(F32), axis=-1, keepdims=True)
            hd["s"] = [_dot(hd["q_c"][rows_of(jj)], hd["k_all"][keys_of(jj)], "nt") for jj in range(gq)]
            hd["dp"] = [_dot(hd["do_c"][rows_of(jj)], hd["v_all"][keys_of(jj)], "nt") for jj in range(gq)]
            hd["s"].append(_dot(hd["q_n"], hd["k_all"][last], "nt"))
            hd["dp"].append(_dot(hd["do_n"], hd["v_all"][last], "nt"))
            heads.append(hd)
        for hd in heads:
            hd["p"], hd["ds"] = [], []
            for jj in range(gq + 1):
                if jj < gq:
                    mask, l_col, delta = (band_first if jj == 0 else band), hd["l_c"][rows_of(jj)], hd["dl_c"][rows_of(jj)]
                else:
                    mask, l_col, delta = mask_next, hd["l_n"], hd["dl_n"]
                p = jnp.where(mask, jnp.exp(hd["s"][jj] - l_col), 0.0)
                hd["p"].append(p.astype(BF16))
                hd["ds"].append((p * (hd["dp"][jj] - delta)).astype(BF16))
        for hd in heads:
            cols = hd["cols"]
            dk_blocks, dv_blocks = [None] * (gq + 1), [None] * (gq + 1)

            def add(lst, idx, val):
                lst[idx] = val if lst[idx] is None else lst[idx] + val

            for jj in range(gq):
                qb, dob = hd["q_c"][rows_of(jj)], hd["do_c"][rows_of(jj)]
                dq_ref[rows_of(jj), cols] = _dot(hd["ds"][jj], hd["k_all"][keys_of(jj)], "nn").astype(dq_ref.dtype)
                dk2 = _dot(hd["ds"][jj], qb, "tn")
                dv2 = _dot(hd["p"][jj], dob, "tn")
                add(dk_blocks, jj, dk2[:ATTN_BLOCK])
                add(dk_blocks, jj + 1, dk2[ATTN_BLOCK:])
                add(dv_blocks, jj, dv2[:ATTN_BLOCK])
                add(dv_blocks, jj + 1, dv2[ATTN_BLOCK:])
            add(dk_blocks, gq, _dot(hd["ds"][gq], hd["q_n"], "tn"))
            add(dv_blocks, gq, _dot(hd["p"][gq], hd["do_n"], "tn"))
            for jj in range(gq):
                dk_ref[rows_of(jj), cols] = dk_blocks[jj + 1].astype(dk_ref.dtype)
                dv_ref[rows_of(jj), cols] = dv_blocks[jj + 1].astype(dv_ref.dtype)

    def cur(c):
        return pl.BlockSpec((rows, GROUP_WIDTH), lambda r, i: (i, r + c))

    def prev(c):
        return pl.BlockSpec((ATTN_BLOCK, GROUP_WIDTH), lambda r, i: (jnp.maximum(i * gq - 1, 0), r + c))

    def nxt(c):
        return pl.BlockSpec((ATTN_BLOCK, GROUP_WIDTH), lambda r, i: (jnp.minimum((i + 1) * gq, nb - 1), r + c))

    cq, ck, cv = cols3
    lse_cur = pl.BlockSpec((rows, LSE_WIDTH), lambda r, i: (i, r))
    lse_next = pl.BlockSpec((ATTN_BLOCK, LSE_WIDTH), lambda r, i: (jnp.minimum((i + 1) * gq, nb - 1), r))
    return pl.pallas_call(
        body, name=f"attn_bwd_d{dil}", grid=(dil, ni),
        in_specs=[cur(cq), nxt(cq), cur(ck), prev(ck), cur(cv), prev(cv), cur(0), nxt(0), cur(0), nxt(0), lse_cur, lse_next],
        out_specs=[cur(0), cur(0), cur(0)],
        out_shape=[jax.ShapeDtypeStruct(out_shape, BF16)] * 3,
        compiler_params=_params(2),
    )(qv, qv, kv, kv, vv, vv, dov, dov, ov, ov, lv, lv)


DILATED = tuple((g, d) for g, d in enumerate(GROUP_DILATIONS) if d > 1)


def _spread(scr, slot, tile, out_ref, dil, col, width=GROUP_WIDTH):
    tm = tile.shape[0]
    buf = scr.at[slot]
    buf[...] = tile
    for r in range(dil):
        c0 = r * width + col
        out_ref[:, c0:c0 + HEAD_DIM] = buf[pl.ds(r, tm // dil, stride=dil), :].astype(out_ref.dtype)


def _collect(scr, slot, in_ref, dil, col, width=GROUP_WIDTH):
    tm = scr.shape[1]
    buf = scr.at[slot]
    for r in range(dil):
        c0 = r * width + col
        buf[pl.ds(r, tm // dil, stride=dil), :] = in_ref[:, c0:c0 + HEAD_DIM].astype(F32)
    return buf[...]


def _view_spec(tm, dil, width=GROUP_WIDTH):
    return pl.BlockSpec((tm // dil, dil * width), lambda i: (i, 0))


def _view_shape(s, dil, dtype, width=GROUP_WIDTH):
    return jax.ShapeDtypeStruct((s // dil, dil * width), dtype)


def _qkv_layout(z, tabs, tm):
    s = z.shape[0]
    scale = 1.0 / math.sqrt(HEAD_DIM)
    qkv_width = 3 * N_GROUPS * GROUP_WIDTH

    def body(z_ref, cs_ref, lo_ref, hi_ref, qk0_ref, *rest):
        views, scr = rest[:-1], rest[-1]
        tabs_ = (cs_ref[...], lo_ref[...], hi_ref[...])
        for part in range(3):
            for g, dil in enumerate(GROUP_DILATIONS):
                if part == 2 and dil == 1:
                    continue
                for h in range(HEADS_PER_GROUP):
                    col = part * N_GROUPS * GROUP_WIDTH + g * GROUP_WIDTH + h * HEAD_DIM
                    t = z_ref[:, col:col + HEAD_DIM].astype(F32)
                    if part < 2:
                        t = _rope(t, *tabs_)
                    if part == 0:
                        t = t * scale
                    if dil == 1:
                        c0 = part * GROUP_WIDTH + h * HEAD_DIM
                        qk0_ref[:, c0:c0 + HEAD_DIM] = t.astype(BF16)
                    else:
                        out = views[3 * [gg for gg, _ in DILATED].index(g) + part]
                        _spread(scr, h, t, out, dil, h * HEAD_DIM)

    row = lambda i: (i, 0)
    out_shape = [jax.ShapeDtypeStruct((s, 2 * GROUP_WIDTH), BF16)]
    out_specs = [pl.BlockSpec((tm, 2 * GROUP_WIDTH), row)]
    for _, dil in DILATED:
        out_shape += [_view_shape(s, dil, BF16)] * 3
        out_specs += [_view_spec(tm, dil)] * 3
    res = pl.pallas_call(
        body, name="qkv_layout", grid=(s // tm,),
        in_specs=[pl.BlockSpec((tm, qkv_width), row)] + [pl.BlockSpec((tm, HEAD_DIM), row)] * 3,
        out_specs=out_specs, out_shape=out_shape,
        scratch_shapes=[pltpu.VMEM((HEADS_PER_GROUP, tm, HEAD_DIM), F32)], compiler_params=_params(1),
    )(z, *tabs)
    return res[0], [tuple(res[1 + 3 * n:4 + 3 * n]) for n in range(len(DILATED))]


def _attn_merge(o0, l0, dilated, tm):
    s = o0.shape[0]
    n_d = len(DILATED)

    def body(*refs):
        o0_ref, l0_ref = refs[:2]
        in_views = refs[2:2 + 2 * n_d]
        attn_ref, lse_ref = refs[2 + 2 * n_d:4 + 2 * n_d]
        out_views = refs[4 + 2 * n_d:4 + 4 * n_d]
        scr = refs[-1]
        l_rows = [l0_ref[...]] + [_collect(scr, n, in_views[2 * n + 1], dil, 0, LSE_WIDTH) for n, (_, dil) in enumerate(DILATED)]
        lse_heads = []
        for h in range(HEADS_PER_GROUP):
            cols = slice(h * HEAD_DIM, (h + 1) * HEAD_DIM)
            os_ = [o0_ref[:, cols].astype(F32)]
            for n, (_, dil) in enumerate(DILATED):
                os_.append(_collect(scr, n_d + n, in_views[2 * n], dil, h * HEAD_DIM))
            ls_ = [lr[:, h * LSE_LANES:h * LSE_LANES + 1] for lr in l_rows]
            m = ls_[0]
            for l_ in ls_[1:]:
                m = jnp.maximum(m, l_)
            es = [jnp.exp(l_ - m) for l_ in ls_]
            den = es[0]
            num = es[0] * os_[0]
            for e, o in zip(es[1:], os_[1:]):
                den = den + e
                num = num + e * o
            attn = num / den
            lse_heads.append(jnp.broadcast_to(m + jnp.log(den), (tm, LSE_LANES)))
            attn_ref[:, cols] = attn.astype(BF16)
            for n, (_, dil) in enumerate(DILATED):
                _spread(scr, 2 * n_d, attn, out_views[2 * n], dil, h * HEAD_DIM)
        lse = jnp.concatenate(lse_heads, axis=1)
        lse_ref[...] = lse
        for n, (_, dil) in enumerate(DILATED):
            _spread(scr, 2 * n_d, lse, out_views[2 * n + 1], dil, 0, LSE_WIDTH)

    row = lambda i: (i, 0)
    nat = pl.BlockSpec((tm, GROUP_WIDTH), row)
    nat_l = pl.BlockSpec((tm, LSE_WIDTH), row)
    in_specs = [nat, nat_l]
    args = [o0, l0]
    out_specs = [nat, nat_l]
    out_shape = [jax.ShapeDtypeStruct((s, GROUP_WIDTH), BF16), jax.ShapeDtypeStruct((s, LSE_WIDTH), F32)]
    for (_, dil), (ov, lv) in zip(DILATED, dilated):
        in_specs += [_view_spec(tm, dil), _view_spec(tm, dil, LSE_WIDTH)]
        args += [ov, lv]
        out_specs += [_view_spec(tm, dil), _view_spec(tm, dil, LSE_WIDTH)]
        out_shape += [_view_shape(s, dil, BF16), _view_shape(s, dil, F32, LSE_WIDTH)]
    res = pl.pallas_call(
        body, name="attn_merge", grid=(s // tm,), in_specs=in_specs, out_specs=out_specs, out_shape=out_shape,
        scratch_shapes=[pltpu.VMEM((2 * n_d + 1, tm, HEAD_DIM), F32)], compiler_params=_params(1),
    )(*args)
    return res[0], res[1], [tuple(res[2 + 2 * n:4 + 2 * n]) for n in range(n_d)]


def _to_views(a, tm):
    s = a.shape[0]

    def body(a_ref, *rest):
        outs, scr = rest[:-1], rest[-1]
        for h in range(HEADS_PER_GROUP):
            t = a_ref[:, h * HEAD_DIM:(h + 1) * HEAD_DIM].astype(F32)
            for n, (_, dil) in enumerate(DILATED):
                _spread(scr, n, t, outs[n], dil, h * HEAD_DIM)

    return pl.pallas_call(
        body, name="to_views", grid=(s // tm,), in_specs=[pl.BlockSpec((tm, GROUP_WIDTH), lambda i: (i, 0))],
        out_specs=[_view_spec(tm, dil) for _, dil in DILATED], out_shape=[_view_shape(s, dil, BF16) for _, dil in DILATED],
        scratch_shapes=[pltpu.VMEM((len(DILATED), tm, HEAD_DIM), F32)], compiler_params=_params(1),
    )(a)


def _dz_layout(grads, du, dga, dgs, tabs, tm):
    s = du.shape[0]
    scale = 1.0 / math.sqrt(HEAD_DIM)

    def body(*refs):
        g_refs = refs[:3 * N_GROUPS]
        du_ref, dga_ref, dgs_ref, cs_ref, lo_ref, hi_ref, dz_ref, scr = refs[3 * N_GROUPS:]
        tabs_ = (cs_ref[...], lo_ref[...], hi_ref[...])
        for part in range(3):
            for g, dil in enumerate(GROUP_DILATIONS):
                src = g_refs[3 * g + part]
                for h in range(HEADS_PER_GROUP):
                    if dil == 1:
                        t = src[:, h * HEAD_DIM:(h + 1) * HEAD_DIM].astype(F32)
                    else:
                        t = _collect(scr, h, src, dil, h * HEAD_DIM)
                    if part < 2:
                        t = _rope_t(t, *tabs_)
                    if part == 0:
                        t = t * scale
                    col = part * N_GROUPS * GROUP_WIDTH + g * GROUP_WIDTH + h * HEAD_DIM
                    dz_ref[:, col:col + HEAD_DIM] = t.astype(BF16)
        dz_ref[:, COL_U:COL_GA] = du_ref[...]
        dz_ref[:, COL_GA:COL_GS] = dga_ref[...]
        dz_ref[:, COL_GS:IN_WIDTH] = dgs_ref[...]

    row = lambda i: (i, 0)
    in_specs, args = [], []
    for (g, dil), trio in zip(enumerate(GROUP_DILATIONS), grads):
        in_specs += [pl.BlockSpec((tm, GROUP_WIDTH), row) if dil == 1 else _view_spec(tm, dil)] * 3
        args += list(trio)
    in_specs += [pl.BlockSpec((tm, SSM_WIDTH), row), pl.BlockSpec((tm, D_MODEL), row), pl.BlockSpec((tm, D_MODEL), row)]
    in_specs += [pl.BlockSpec((tm, HEAD_DIM), row)] * 3
    return pl.pallas_call(
        body, name="dz_layout", grid=(s // tm,), in_specs=in_specs, out_specs=pl.BlockSpec((tm, IN_WIDTH), row),
        out_shape=jax.ShapeDtypeStruct((s, IN_WIDTH), BF16),
        scratch_shapes=[pltpu.VMEM((HEADS_PER_GROUP, tm, HEAD_DIM), F32)], compiler_params=_params(1),
    )(*args, du, dga, dgs, *tabs)


def _discretise(a_re, a_im, log_dt, bt_re, bt_im):
    dt = jnp.exp(log_dt)
    mag = jnp.exp(a_re * dt)
    bar_re = mag * jnp.cos(a_im * dt)
    bar_im = mag * jnp.sin(a_im * dt)
    nr = bar_re - 1.0
    ni = bar_im
    den = a_re * a_re + a_im * a_im
    z_re = (nr * a_re + ni * a_im) / den
    z_im = (ni * a_re - nr * a_im) / den
    bb_re = z_re[:, None, :] * bt_re - z_im[:, None, :] * bt_im
    bb_im = z_re[:, None, :] * bt_im + z_im[:, None, :] * bt_re
    return bar_re, bar_im, bb_re, bb_im


def _ssm_prep(a_re, a_im, log_dt, bt_re, bt_im):
    def body(ar, ai, ld, br, bi, o_lr, o_li, o_br, o_bi):
        lr, li, bbr, bbi = _discretise(ar[...], ai[...], ld[...], br[...], bi[...])
        o_lr[...] = lr
        o_li[...] = li
        o_br[...] = bbr
        o_bi[...] = bbi

    sm = jax.ShapeDtypeStruct((SSM_GROUPS, SSM_STATE), F32)
    bg = jax.ShapeDtypeStruct((SSM_GROUPS, SSM_GROUP, SSM_STATE), F32)
    return pl.pallas_call(body, name="ssm_prep", out_shape=[sm, sm, bg, bg])(a_re, a_im, log_dt, bt_re, bt_im)


def _ssm_param_bwd(a_re, a_im, log_dt, bt_re, bt_im, d_lr, d_li, d_bbr, d_bbi):
    def body(ar, ai, ld, br, bi, g_lr, g_li, g_br, g_bi, o_ar, o_ai, o_ld, o_br, o_bi):
        _, vjp = jax.vjp(_discretise, ar[...], ai[...], ld[...], br[...], bi[...])
        d_ar, d_ai, d_ld, d_br, d_bi = vjp((g_lr[...], g_li[...], g_br[...], g_bi[...]))
        o_ar[...] = d_ar
        o_ai[...] = d_ai
        o_ld[...] = d_ld
        o_br[...] = d_br
        o_bi[...] = d_bi

    sm = jax.ShapeDtypeStruct((SSM_GROUPS, SSM_STATE), F32)
    col = jax.ShapeDtypeStruct((SSM_GROUPS, 1), F32)
    bg = jax.ShapeDtypeStruct((SSM_GROUPS, SSM_GROUP, SSM_STATE), F32)
    return pl.pallas_call(body, name="ssm_param_bwd", out_shape=[sm, sm, col, bg, bg])(
        a_re, a_im, log_dt, bt_re, bt_im, d_lr, d_li, d_bbr, d_bbi)


def _block_diag(t, rows_per, cols_per):
    t4 = t.reshape(SSM_SUPER, 8, rows_per, cols_per)
    eye = jnp.eye(8, dtype=t.dtype)
    return jnp.einsum("bgrc,gh->bgrhc", t4, eye).reshape(SSM_SUPER, 8 * rows_per, 8 * cols_per)


def _block_diag_t(dense, rows_per, cols_per):
    t = dense.reshape(SSM_SUPER, 8, rows_per, 8, cols_per)
    eye = jnp.eye(8, dtype=dense.dtype)
    return jnp.einsum("bgrhc,gh->bgrc", t, eye).reshape(SSM_GROUPS, rows_per, cols_per)


def _gelu(v):
    c = math.sqrt(2.0 / math.pi)
    return 0.5 * v * (1.0 + jnp.tanh(c * (v + 0.044715 * v * v * v)))


def _gelu_grad(v):
    c = math.sqrt(2.0 / math.pi)
    t = jnp.tanh(c * (v + 0.044715 * v * v * v))
    return 0.5 * (1.0 + t) + 0.5 * v * (1.0 - t * t) * c * (1.0 + 3.0 * 0.044715 * v * v)


SUB = 8


SCAN_STEPS = (1, 2, 4)
N_SCAN_TABLES = 2 + 2 * len(SCAN_STEPS)


def _scan_tables(tab_ref, lam_re, lam_im, reverse, conj):
    lr = lam_re
    li = -lam_im if conj else lam_im
    powers = [(lr, li)]
    for _ in range(SUB - 1):
        pr, pi = powers[-1]
        powers.append((pr * lr - pi * li, pr * li + pi * lr))
    row = lax.broadcasted_iota(jnp.int32, (SUB, N_STATE), 0)
    if reverse:
        row = SUB - 1 - row
    wide = lambda v: jnp.broadcast_to(v, (SUB, N_STATE))
    p_re = jnp.zeros((SUB, N_STATE), F32)
    p_im = jnp.zeros((SUB, N_STATE), F32)
    for j in range(SUB):
        p_re = jnp.where(row == j, wide(powers[j][0]), p_re)
        p_im = jnp.where(row == j, wide(powers[j][1]), p_im)
    tab_ref[0] = p_re
    tab_ref[1] = p_im
    for idx, k in enumerate(SCAN_STEPS):
        tab_ref[2 + 2 * idx] = jnp.where(row >= k, wide(powers[k - 1][0]), 0.0)
        tab_ref[3 + 2 * idx] = jnp.where(row >= k, wide(powers[k - 1][1]), 0.0)


def _scan_rows(g_re_ref, g_im_ref, tab_ref, carry, n_rows, reverse):
    last = 0 if reverse else SUB - 1

    def tile_step(tt, state):
        cr, ci = state
        t8 = (n_rows // SUB - 1 - tt) if reverse else tt
        start = pl.multiple_of(t8 * SUB, SUB)
        xr = g_re_ref[pl.ds(start, SUB), :]
        xi = g_im_ref[pl.ds(start, SUB), :]
        for idx, k in enumerate(SCAN_STEPS):
            mr = tab_ref[2 + 2 * idx]
            mi = tab_ref[3 + 2 * idx]
            shift = SUB - k if reverse else k
            sr = pltpu.roll(xr, shift, 0)
            si = pltpu.roll(xi, shift, 0)
            xr, xi = xr + (mr * sr - mi * si), xi + (mr * si + mi * sr)
        pr = tab_ref[0]
        pi = tab_ref[1]
        xr, xi = xr + (pr * cr - pi * ci), xi + (pr * ci + pi * cr)
        g_re_ref[pl.ds(start, SUB), :] = xr
        g_im_ref[pl.ds(start, SUB), :] = xi
        return (jnp.broadcast_to(xr[last:last + 1, :], (SUB, N_STATE)),
                jnp.broadcast_to(xi[last:last + 1, :], (SUB, N_STATE)))

    return lax.fori_loop(0, n_rows // SUB, tile_step, carry)


def _ssm_fwd(z, b_re, b_im, c_re, c_im, lam_re, lam_im, d_skip, chunk):
    s = z.shape[0]

    def body(u_ref, bre, bim, cre, cim, lre, lim, dsk, hre_ref, him_ref, ys_ref, yg_ref, car_re, car_im, tabs):
        i = pl.program_id(0)

        @pl.when(i == 0)
        def _():
            car_re[...] = jnp.zeros_like(car_re)
            car_im[...] = jnp.zeros_like(car_im)
            _scan_tables(tabs, lre[...], lim[...], False, False)

        u = u_ref[...]
        for b in range(SSM_SUPER):
            ub = u[:, b * 128:(b + 1) * 128]
            st = slice(b * 512, (b + 1) * 512)
            hre_ref[:, st] = _dot(ub, bre[b], "nn")
            him_ref[:, st] = _dot(ub, bim[b], "nn")
        sr, si = _scan_rows(hre_ref, him_ref, tabs, (car_re[...], car_im[...]), chunk, False)
        car_re[...] = sr
        car_im[...] = si
        uf = u.astype(F32)
        for b in range(SSM_SUPER):
            st = slice(b * 512, (b + 1) * 512)
            ch = slice(b * 128, (b + 1) * 128)
            y = _dot(hre_ref[:, st].astype(BF16), cre[b], "nn") - _dot(him_ref[:, st].astype(BF16), cim[b], "nn")
            y = y + dsk[:, ch] * uf[:, ch]
            ys_ref[:, ch] = y
            yg_ref[:, ch] = _gelu(y).astype(BF16)

    full3 = lambda i: (0, 0, 0)
    full2 = lambda i: (0, 0)
    row = lambda i: (i, 0)
    u_col = COL_U // SSM_WIDTH
    return pl.pallas_call(
        body, name="ssm_fwd", grid=(s // chunk,),
        in_specs=[pl.BlockSpec((chunk, SSM_WIDTH), lambda i: (i, u_col)),
                  pl.BlockSpec((SSM_SUPER, 128, 512), full3), pl.BlockSpec((SSM_SUPER, 128, 512), full3),
                  pl.BlockSpec((SSM_SUPER, 512, 128), full3), pl.BlockSpec((SSM_SUPER, 512, 128), full3),
                  pl.BlockSpec((1, N_STATE), full2), pl.BlockSpec((1, N_STATE), full2), pl.BlockSpec((1, SSM_WIDTH), full2)],
        out_specs=[pl.BlockSpec((chunk, N_STATE), row), pl.BlockSpec((chunk, N_STATE), row),
                   pl.BlockSpec((chunk, SSM_WIDTH), row), pl.BlockSpec((chunk, SSM_WIDTH), row)],
        out_shape=[jax.ShapeDtypeStruct((s, N_STATE), F32), jax.ShapeDtypeStruct((s, N_STATE), F32),
                   jax.ShapeDtypeStruct((s, SSM_WIDTH), F32), jax.ShapeDtypeStruct((s, SSM_WIDTH), BF16)],
        scratch_shapes=[pltpu.VMEM((SUB, N_STATE), F32), pltpu.VMEM((SUB, N_STATE), F32),
                        pltpu.VMEM((N_SCAN_TABLES, SUB, N_STATE), F32)],
        compiler_params=_params(1),
    )(z, b_re, b_im, c_re, c_im, lam_re, lam_im, d_skip)


def _ssm_bwd(dys, z, h_re, h_im, b_re, b_im, c_re, c_im, lam_re, lam_im, d_skip, chunk):
    s = z.shape[0]
    n_chunks = s // chunk

    def body(dy_ref, u_ref, hre_ref, him_ref, hpr_ref, hpi_ref, bre, bim, cre, cim, lre, lim, dsk,
             du_ref, dlr_ref, dli_ref, dbr_ref, dbi_ref, dcr_ref, dci_ref, dd_ref, are, aim, car_re, car_im, tabs):
        i = pl.program_id(0)
        n = n_chunks - 1 - i

        @pl.when(i == 0)
        def _():
            car_re[...] = jnp.zeros_like(car_re)
            car_im[...] = jnp.zeros_like(car_im)
            _scan_tables(tabs, lre[...], lim[...], True, True)
            for r in (dlr_ref, dli_ref, dbr_ref, dbi_ref, dcr_ref, dci_ref, dd_ref):
                r[...] = jnp.zeros_like(r)

        dy = dy_ref[...]
        dyb = dy.astype(BF16)
        u = u_ref[...]
        for b in range(SSM_SUPER):
            ch = slice(b * 128, (b + 1) * 128)
            st = slice(b * 512, (b + 1) * 512)
            are[:, st] = _dot(dyb[:, ch], cre[b], "nt")
            aim[:, st] = -_dot(dyb[:, ch], cim[b], "nt")
        sr, si = _scan_rows(are, aim, tabs, (car_re[...], car_im[...]), chunk, True)
        car_re[...] = sr
        car_im[...] = si
        row_id = lax.broadcasted_iota(jnp.int32, (chunk, N_STATE), 0)
        top_scale = jnp.where(n > 0, 1.0, 0.0)
        h_r = hre_ref[...]
        h_i = him_ref[...]
        hp_r = jnp.where(row_id == 0, hpr_ref[SUB - 1:SUB, :] * top_scale, pltpu.roll(h_r, 1, 0))
        hp_i = jnp.where(row_id == 0, hpi_ref[SUB - 1:SUB, :] * top_scale, pltpu.roll(h_i, 1, 0))
        a_r = are[...]
        a_i = aim[...]
        dlr_ref[...] += jnp.sum(a_r * hp_r + a_i * hp_i, axis=0, keepdims=True)
        dli_ref[...] += jnp.sum(a_i * hp_r - a_r * hp_i, axis=0, keepdims=True)
        dd_ref[...] += jnp.sum(dy * u.astype(F32), axis=0, keepdims=True)
        a_rb = a_r.astype(BF16)
        a_ib = a_i.astype(BF16)
        h_rb = h_r.astype(BF16)
        h_ib = h_i.astype(BF16)
        for b in range(SSM_SUPER):
            ch = slice(b * 128, (b + 1) * 128)
            st = slice(b * 512, (b + 1) * 512)
            dbr_ref[b] += _dot(u[:, ch], a_rb[:, st], "tn")
            dbi_ref[b] += _dot(u[:, ch], a_ib[:, st], "tn")
            dcr_ref[b] += _dot(h_rb[:, st], dyb[:, ch], "tn")
            dci_ref[b] += -_dot(h_ib[:, st], dyb[:, ch], "tn")
            du = _dot(a_rb[:, st], bre[b], "nt") + _dot(a_ib[:, st], bim[b], "nt") + dsk[:, ch] * dy[:, ch]
            du_ref[:, ch] = du.astype(du_ref.dtype)

    full3 = lambda i: (0, 0, 0)
    full2 = lambda i: (0, 0)
    rev = lambda i: (n_chunks - 1 - i, 0)
    above = lambda i: (jnp.maximum((n_chunks - 1 - i) * (chunk // SUB) - 1, 0), 0)
    u_col = COL_U // SSM_WIDTH
    b_spec = pl.BlockSpec((SSM_SUPER, 128, 512), full3)
    c_spec = pl.BlockSpec((SSM_SUPER, 512, 128), full3)
    vec = pl.BlockSpec((1, N_STATE), full2)
    return pl.pallas_call(
        body, name="ssm_bwd", grid=(n_chunks,),
        in_specs=[pl.BlockSpec((chunk, SSM_WIDTH), rev),
                  pl.BlockSpec((chunk, SSM_WIDTH), lambda i: (n_chunks - 1 - i, u_col)),
                  pl.BlockSpec((chunk, N_STATE), rev), pl.BlockSpec((chunk, N_STATE), rev),
                  pl.BlockSpec((SUB, N_STATE), above), pl.BlockSpec((SUB, N_STATE), above),
                  b_spec, b_spec, c_spec, c_spec, vec, vec, pl.BlockSpec((1, SSM_WIDTH), full2)],
        out_specs=[pl.BlockSpec((chunk, SSM_WIDTH), rev), vec, vec, b_spec, b_spec, c_spec, c_spec,
                   pl.BlockSpec((1, SSM_WIDTH), full2)],
        out_shape=[jax.ShapeDtypeStruct((s, SSM_WIDTH), BF16),
                   jax.ShapeDtypeStruct((1, N_STATE), F32), jax.ShapeDtypeStruct((1, N_STATE), F32),
                   jax.ShapeDtypeStruct((SSM_SUPER, 128, 512), F32), jax.ShapeDtypeStruct((SSM_SUPER, 128, 512), F32),
                   jax.ShapeDtypeStruct((SSM_SUPER, 512, 128), F32), jax.ShapeDtypeStruct((SSM_SUPER, 512, 128), F32),
                   jax.ShapeDtypeStruct((1, SSM_WIDTH), F32)],
        scratch_shapes=[pltpu.VMEM((chunk, N_STATE), F32), pltpu.VMEM((chunk, N_STATE), F32),
                        pltpu.VMEM((SUB, N_STATE), F32), pltpu.VMEM((SUB, N_STATE), F32),
                        pltpu.VMEM((N_SCAN_TABLES, SUB, N_STATE), F32)],
        compiler_params=_params(1),
    )(dys, z, h_re, h_im, h_re, h_im, b_re, b_im, c_re, c_im, lam_re, lam_im, d_skip)


def _local_step(x, p, pos, tgt, sm, w_in, late_bufs, place):
    s = x.shape[0]
    tm = min(512, s)
    ts = min(1024, s)
    chunk = min(256, s)
    ni = s // tm
    nk = s // ts
    g_mix, g_ffn, g_final = sm["g_mix"], sm["g_ffn"], sm["g_final"]
    rowblk, rowmap = _rows(tm, D_MODEL)
    vec1k = ((1, D_MODEL), lambda *_: (0, 0))

    (n1,) = _ew("rms_mix", (ni,), [(x, rowblk, rowmap), (g_mix, *vec1k)], [((s, D_MODEL), BF16, rowblk, rowmap)],
                lambda pids, h, g: ((_rms_fwd_tile(h, g),), ()))

    half_in = IN_WIDTH // 8
    tmb = min(1024, s)
    nib = s // tmb
    n_late = len(late_bufs)
    g_start, g_forward, g_finish = _gather_stages(n_late)
    in_steps = N_CHIPS * nib
    chip_w = IN_WIDTH // N_CHIPS
    gather = dict(ins=late_bufs, outs=[jax.ShapeDtypeStruct(b.shape, b.dtype) for b in late_bufs], aliased=True,
                  sems=[3 * n_late] * 4, stages=[(0, g_start), (in_steps // 2, g_forward), (in_steps - 1, g_finish)])
    z, *late = _mm("in_proj", (nib, N_CHIPS, 1),
                   [(n1, (tmb, D_MODEL), lambda i, j, k: (i, 0), w_in, (None, D_MODEL, chip_w), lambda i, j, k: (j, 0, 0))],
                   "nn", [((s, IN_WIDTH), BF16, (tmb, chip_w), lambda i, j, k: (i, j))], j_outer=True, comm=gather)
    w_ap, w_ga, w_gb, w_out, w_fg, w_fu, w_fd, w_pg, w_pp = (
        g.reshape(N_CHIPS, 2 * g.shape[2], g.shape[3]) for g in late)
    w_out2 = w_out.reshape(D_MODEL, D_MODEL)
    w_pg2 = w_pg.reshape(D_MODEL, D_MODEL)

    inv = ROPE_THETA ** (-jnp.arange(ROPE_HALF, dtype=F32) * 2.0 / ROPE_DIM)
    inv_row = jnp.concatenate([inv, inv, jnp.zeros((HEAD_DIM - ROPE_DIM,), F32)]).reshape(1, HEAD_DIM)
    tabs = _rope_tables(pos.astype(F32).reshape(s, 1), inv_row, tm)

    qk0, qkv_views = _qkv_layout(z, tabs, tm)
    v0_col = (2 * N_GROUPS * GROUP_WIDTH) // GROUP_WIDTH
    group_in = [((qk0, qk0, z), (0, 1, v0_col))] + [(trio, (0, 0, 0)) for trio in qkv_views]
    fwd_out = [_attn_fwd(*arrs, dil, cols3) for (arrs, cols3), dil in zip(group_in, GROUP_DILATIONS)]
    attn, lse, merged_views = _attn_merge(fwd_out[0][0], fwd_out[0][1], fwd_out[1:], tm)

    def chip_cols(parts):
        return (jnp.concatenate(parts, axis=1),), ()

    def proj_cols(name, a, width, w):
        blk = (None, width, 256)
        pairs = [(a, (tmb, width), lambda i, j, k: (i, 0), w, blk, lambda i, j, k: (0, 0, 0))]
        pairs += [(None, None, None, w, blk, (lambda i, j, k, q=q: (q, 0, 0))) for q in range(1, N_CHIPS)]
        return _mm(name, (nib, 1, 1), pairs, "nn", [((s, D_MODEL), BF16, (tmb, D_MODEL), lambda i, j, k: (i, 0))],
                   epilogue=chip_cols, sum_pairs=False)[0]

    def proj512(name, a, w):
        return proj_cols(name, a, GROUP_WIDTH, w)

    attn_d = proj512("attn_proj", attn, w_ap)

    bt_re = jnp.transpose(sm["b_re"], (0, 2, 1))
    bt_im = jnp.transpose(sm["b_im"], (0, 2, 1))
    log_dt_col = sm["log_dt"].reshape(SSM_GROUPS, 1)
    lam_re, lam_im, bbt_re, bbt_im = _ssm_prep(sm["a_re"], sm["a_im"], log_dt_col, bt_re, bt_im)
    b_re_m = _block_diag(bbt_re, SSM_GROUP, SSM_STATE).astype(BF16)
    b_im_m = _block_diag(bbt_im, SSM_GROUP, SSM_STATE).astype(BF16)
    c_re_m = _block_diag(jnp.transpose(sm["c_re"], (0, 2, 1)), SSM_STATE, SSM_GROUP).astype(BF16)
    c_im_m = _block_diag(jnp.transpose(sm["c_im"], (0, 2, 1)), SSM_STATE, SSM_GROUP).astype(BF16)
    lam_re_row = lam_re.reshape(1, N_STATE)
    lam_im_row = lam_im.reshape(1, N_STATE)
    d_skip_row = sm["d_skip"].reshape(1, SSM_WIDTH)
    h_re, h_im, ys, yg = _ssm_fwd(z, b_re_m, b_im_m, c_re_m, c_im_m, lam_re_row, lam_im_row, d_skip_row, chunk)

    pa = proj512("glu_a", yg, w_ga)
    pb = proj512("glu_b", yg, w_gb)

    ga_blk = ((tm, D_MODEL), lambda i: (i, COL_GA // D_MODEL))
    gs_blk = ((tm, D_MODEL), lambda i: (i, COL_GS // D_MODEL))

    def mix_fn(pids, ga, gs, ad, a, b):
        ga, gs, ad, a, b = (t.astype(F32) for t in (ga, gs, ad, a, b))
        return (_sig(ga) * ad + _sig(gs) * (a * _sig(b)),), ()

    (mix,) = _ew("gate_mix", (ni,), [(z, *ga_blk), (z, *gs_blk), (attn_d, rowblk, rowmap), (pa, rowblk, rowmap),
                                     (pb, rowblk, rowmap)], [((s, D_MODEL), BF16, rowblk, rowmap)], mix_fn)

    def out_epi(acc, xr, g):
        h1 = acc + xr
        return (h1, _rms_fwd_tile(h1, g)), ()

    m3 = lambda i, j, k: (i, 0)
    w3 = lambda i, j, k: (0, 0)
    h1, n2 = _mm("out_proj", (nib, 1, 1), [(mix, (tmb, D_MODEL), m3, w_out2, (D_MODEL, D_MODEL), w3)], "nn",
                 [((s, D_MODEL), F32, (tmb, D_MODEL), m3), ((s, D_MODEL), BF16, (tmb, D_MODEL), m3)],
                 epilogue=out_epi, extras=[(x, (tmb, D_MODEL), m3), (g_ffn, (1, D_MODEL), w3)])

    ffq = (None, tm, D_FF_Q)
    ffq_map = lambda i, j, k: (j, i, 0)

    def ffn_in_epi(parts):
        gts, ups = parts[0::2], parts[1::2]
        acts = [gt * _sig(gt) * u_ for gt, u_ in zip(gts, ups)]
        return (jnp.stack(gts, axis=0), jnp.stack(ups, axis=0), jnp.stack(acts, axis=0)), ()

    w_ffq = (None, D_MODEL, D_FF_Q)
    ff_pairs = []
    for q in range(N_CHIPS):
        blk_q = lambda i, j, k, q=q: (q, 0, 0)
        ff_pairs.append((n2, (tm, D_MODEL), m3, w_fg, w_ffq, blk_q) if q == 0 else (None, None, None, w_fg, w_ffq, blk_q))
        ff_pairs.append((None, None, None, w_fu, w_ffq, blk_q))
    ff_all = (N_CHIPS, tm, D_FF_Q)
    ff_all_map = lambda i, j, k: (0, i, 0)
    gate, up, act = _mm("ffn_gate_up", (ni, 1, 1), ff_pairs, "nn",
                        [((N_CHIPS, s, D_FF_Q), BF16, ff_all, ff_all_map)] * 3, epilogue=ffn_in_epi,
                        sum_pairs=False, resident_b=True)

    (h2,) = _mm("ffn_down", (nib, 1, 1),
                [(act, (None, tmb, D_FF_Q), (lambda i, j, k, q=q: (q, i, 0)), w_fd, (None, D_FF_Q, D_MODEL),
                  (lambda i, j, k, q=q: (q, 0, 0))) for q in range(N_CHIPS)], "nn",
                [((s, D_MODEL), F32, (tmb, D_MODEL), m3)], epilogue=lambda acc, hr: ((acc + hr,), ()),
                extras=[(h1, (tmb, D_MODEL), m3)])

    pp = proj_cols("ple_proj", p, PLE_DIM, w_pp)

    def ple_epi(acc, hr, ppr):
        return (acc, hr + _sig(acc) * ppr.astype(F32)), ()

    gl, h3 = _mm("ple_gate", (nib, 1, 1), [(h2, (tmb, D_MODEL), m3, w_pg2, (D_MODEL, D_MODEL), w3)], "nn",
                 [((s, D_MODEL), BF16, (tmb, D_MODEL), m3), ((s, D_MODEL), F32, (tmb, D_MODEL), m3)],
                 epilogue=ple_epi, extras=[(h2, (tmb, D_MODEL), m3), (pp, (tmb, D_MODEL), m3)])

    def head_fn(pids, h, t, g, g_, ppr):
        r = lax.rsqrt(jnp.mean(h * h, axis=-1, keepdims=True) + EPS)
        hhat = h * r
        diff = hhat * g - t
        loss = 0.5 * jnp.sum(jnp.mean(diff * diff, axis=-1, keepdims=True))
        dy = diff * (1.0 / D_MODEL)
        gy = dy * g
        dh = r * (gy - hhat * jnp.mean(gy * hhat, axis=-1, keepdims=True))
        sg = _sig(g_.astype(F32))
        return ((dh, dh * ppr.astype(F32) * sg * (1.0 - sg), dh * sg),
                (jnp.full((SUB, 128), loss, F32), jnp.sum(dy * hhat, axis=0, keepdims=True)))

    dh3, dgl, dpp, loss_acc, dg_final = _ew(
        "loss_head", (ni,),
        [(h3, rowblk, rowmap), (tgt, rowblk, rowmap), (g_final, *vec1k), (gl, rowblk, rowmap), (pp, rowblk, rowmap)],
        [((s, D_MODEL), F32, rowblk, rowmap), ((s, D_MODEL), BF16, rowblk, rowmap), ((s, D_MODEL), BF16, rowblk, rowmap)],
        head_fn, acc_outs=[((SUB, 128), F32), ((1, D_MODEL), F32)])

    def wgrad(name, a, a_block, a_imap, b, b_block, b_imap, out_shape, out_block, out_imap, nj, acc_shape):
        return _mm(name, (1, nj, nk), [(a, a_block, a_imap, b, b_block, b_imap)], "tn",
                   [(out_shape, F32, out_block, out_imap)], acc_shape=acc_shape)[0]

    tk0 = lambda i, j, k: (k, 0)
    tkj = lambda i, j, k: (k, j)
    def wgrad_cols(name, a, width, dy_):
        def split(acc):
            return (jnp.stack([acc[:, q * 256:(q + 1) * 256] for q in range(N_CHIPS)], axis=0),), ()

        return _mm(name, (1, 1, nk), [(a, (ts, width), tk0, dy_, (ts, D_MODEL), tk0)], "tn",
                   [((N_CHIPS, width, 256), F32, (N_CHIPS, width, 256), lambda i, j, k: (0, 0, 0))], epilogue=split,
                   acc_shape=(width, D_MODEL))[0]

    d_w_pp = wgrad_cols("d_ple_proj", p, PLE_DIM, dpp)
    d_w_pg = wgrad("d_ple_gate", h2, (ts, D_MODEL), tk0, dgl, (ts, D_MODEL), tk0, (D_MODEL, D_MODEL),
                   (D_MODEL, D_MODEL), w3, 1, (D_MODEL, D_MODEL))

    (dh2,) = _mm("ple_gate_bwd", (nib, 1, 1), [(dgl, (tmb, D_MODEL), m3, w_pg2, (D_MODEL, D_MODEL), w3)], "nt",
                 [((s, D_MODEL), F32, (tmb, D_MODEL), m3)], epilogue=lambda acc, d_: ((acc + d_,), ()),
                 extras=[(dh3, (tmb, D_MODEL), m3)])

    def ffn_bwd_epi(parts, gt_all, u_all):
        dgs_, dus_ = [], []
        for q, dact in enumerate(parts):
            gt, u_ = gt_all[q].astype(F32), u_all[q].astype(F32)
            sg = _sig(gt)
            dgs_.append(dact * u_ * (sg * (1.0 + gt * (1.0 - sg))))
            dus_.append(dact * gt * sg)
        return (jnp.stack(dgs_, axis=0), jnp.stack(dus_, axis=0)), ()

    fd_pairs = [((dh2, (tm, D_MODEL), m3) if q == 0 else (None, None, None))
                + (w_fd, (None, D_FF_Q, D_MODEL), (lambda i, j, k, q=q: (q, 0, 0))) for q in range(N_CHIPS)]
    dgate, dup = _mm("ffn_down_bwd", (ni, 1, 1), fd_pairs, "nt",
                     [((N_CHIPS, s, D_FF_Q), BF16, ff_all, ff_all_map)] * 2, epilogue=ffn_bwd_epi,
                     extras=[(gate, ff_all, ff_all_map), (up, ff_all, ff_all_map)], sum_pairs=False, resident_b=True)

    ffq_t = (None, ts, D_FF_Q)
    ffq_tmap = lambda i, j, k: (j, k, 0)
    blk_j = lambda i, j, k: (j, 0, 0)
    d_w_fd = wgrad("d_ffn_down", act, ffq_t, ffq_tmap, dh2, (ts, D_MODEL), tk0, (N_CHIPS, D_FF_Q, D_MODEL),
                   (None, D_FF_Q, D_MODEL), blk_j, N_CHIPS, (D_FF_Q, D_MODEL))
    d_w_fg = wgrad("d_ffn_gate", n2, (ts, D_MODEL), tk0, dgate, ffq_t, ffq_tmap, (N_CHIPS, D_MODEL, D_FF_Q),
                   (None, D_MODEL, D_FF_Q), blk_j, N_CHIPS, (D_MODEL, D_FF_Q))
    d_w_fu = wgrad("d_ffn_up", n2, (ts, D_MODEL), tk0, dup, ffq_t, ffq_tmap, (N_CHIPS, D_MODEL, D_FF_Q),
                   (None, D_MODEL, D_FF_Q), blk_j, N_CHIPS, (D_MODEL, D_FF_Q))

    def norm_bwd_epi(acc, h, d_res, g):
        dh, dg = _rms_bwd_tile(acc, h, g)
        return (d_res + dh,), (dg,)

    ffq_k = lambda i, j, k: (k, i, 0)
    blk_k = lambda i, j, k: (k, 0, 0)
    fi_pairs = []
    for q in range(N_CHIPS):
        a_q = lambda i, j, k, q=q: (q, i, 0)
        b_q = lambda i, j, k, q=q: (q, 0, 0)
        fi_pairs.append((dgate, ffq, a_q, w_fg, (None, D_MODEL, D_FF_Q), b_q))
        fi_pairs.append((dup, ffq, a_q, w_fu, (None, D_MODEL, D_FF_Q), b_q))
    dh1, dg_ffn = _mm("ffn_in_bwd", (ni, 1, 1), fi_pairs, "nt",
                      [((s, D_MODEL), F32, (tm, D_MODEL), m3)], epilogue=norm_bwd_epi,
                      extras=[(h1, (tm, D_MODEL), m3), (dh2, (tm, D_MODEL), m3), (g_ffn, (1, D_MODEL), w3)],
                      acc_outs=[((1, D_MODEL), F32)], resident_b=True)

    d_w_out = wgrad("d_out_proj", mix, (ts, D_MODEL), tk0, dh1, (ts, D_MODEL), tk0, (D_MODEL, D_MODEL),
                    (D_MODEL, D_MODEL), w3, 1, (D_MODEL, D_MODEL))

    def mix_bwd_epi(dm, ga, gs, ad, a, b):
        ga, gs, ad, a, b = (t.astype(F32) for t in (ga, gs, ad, a, b))
        s_a, s_s, s_b = _sig(ga), _sig(gs), _sig(b)
        d_ssm = dm * s_s
        return (dm * ad * s_a * (1.0 - s_a), dm * (a * s_b) * s_s * (1.0 - s_s), dm * s_a, d_ssm * s_b,
                d_ssm * a * s_b * (1.0 - s_b)), ()

    tile_m = (tm, D_MODEL)
    dga, dgs, dattn_d, dpa, dpb = _mm(
        "out_proj_bwd", (ni, 1, 1), [(dh1, tile_m, m3, w_out2, (D_MODEL, D_MODEL), w3)], "nt",
        [((s, D_MODEL), BF16, tile_m, m3)] * 5, epilogue=mix_bwd_epi,
        extras=[(z, tile_m, lambda i, j, k: (i, COL_GA // D_MODEL)), (z, tile_m, lambda i, j, k: (i, COL_GS // D_MODEL)),
                (attn_d, tile_m, m3), (pa, tile_m, m3), (pb, tile_m, m3)])

    d_w_ap = wgrad_cols("d_attn_proj", attn, GROUP_WIDTH, dattn_d)
    d_w_ga = wgrad_cols("d_glu_a", yg, GROUP_WIDTH, dpa)
    d_w_gb = wgrad_cols("d_glu_b", yg, GROUP_WIDTH, dpb)

    ik = lambda i, j, k: (i, k)

    def cols_bwd(dy_, w):
        return [(dy_, (tmb, 256), (lambda i, j, k, q=q: (i, q)), w, (None, GROUP_WIDTH, 256),
                 (lambda i, j, k, q=q: (q, 0, 0))) for q in range(N_CHIPS)]

    (dattn,) = _mm("attn_proj_bwd", (nib, 1, 1), cols_bwd(dattn_d, w_ap), "nt",
                   [((s, GROUP_WIDTH), BF16, (tmb, GROUP_WIDTH), m3)])

    (dys,) = _mm("glu_bwd", (nib, 1, 1), cols_bwd(dpa, w_ga) + cols_bwd(dpb, w_gb), "nt",
                 [((s, GROUP_WIDTH), F32, (tmb, GROUP_WIDTH), m3)],
                 epilogue=lambda acc, y_: ((acc * _gelu_grad(y_),), ()),
                 extras=[(ys, (tmb, GROUP_WIDTH), m3)])

    du, d_lr, d_li, d_bre, d_bim, d_cre, d_cim, d_dskip = _ssm_bwd(
        dys, z, h_re, h_im, b_re_m, b_im_m, c_re_m, c_im_m, lam_re_row, lam_im_row, d_skip_row, chunk)

    dattn_views = _to_views(dattn, tm)
    bwd_in = [(dattn, attn, lse)] + [(dv_, ov_, lv_) for dv_, (ov_, lv_) in zip(dattn_views, merged_views)]
    qkv_grads = [_attn_bwd(*arrs, *dol, dil, cols3)
                 for (arrs, cols3), dol, dil in zip(group_in, bwd_in, GROUP_DILATIONS)]
    dz = _dz_layout(qkv_grads, du, dga, dgs, tabs, tm)

    chip_in = IN_WIDTH // N_CHIPS
    ip_pairs = [(dz, (tm, chip_in), (lambda i, j, k, q=q: (i, q)), w_in, (None, D_MODEL, chip_in),
                 (lambda i, j, k, q=q: (q, 0, 0))) for q in range(N_CHIPS)]
    grad_x, dg_mix = _mm("in_proj_bwd", (ni, 1, 1), ip_pairs, "nt",
                         [((s, D_MODEL), F32, (tm, D_MODEL), m3)], epilogue=norm_bwd_epi,
                         extras=[(x, (tm, D_MODEL), m3), (dh1, (tm, D_MODEL), m3), (g_mix, (1, D_MODEL), w3)],
                         acc_outs=[((1, D_MODEL), F32)], resident_b=True)

    early = [d_w_ap, d_w_ga, d_w_gb, d_w_out.reshape(N_CHIPS, D_MODEL // N_CHIPS, D_MODEL), d_w_fg, d_w_fu, d_w_fd,
             d_w_pg.reshape(N_CHIPS, D_MODEL // N_CHIPS, D_MODEL), d_w_pp]
    early5 = [g.reshape(N_CHIPS, 2, g.shape[1] // 2, g.shape[2]) for g in early]
    early_parts = [_pair_sum(g, t, place) for g, t in zip(early5, _pair_exchange(early5))]
    x_start, x_finish = _chip_exchange_stages(len(early_parts))
    ts_in = min(2048, s)
    win_steps = 8 * (s // ts_in)
    exchange = dict(ins=early_parts, outs=[jax.ShapeDtypeStruct(t.shape, t.dtype) for t in early_parts], aliased=False,
                    sems=[3 * len(early_parts)] * 2, stages=[(0, x_start), (win_steps - 1, x_finish)])
    d_w_in, *early_got = _mm("d_in_proj", (1, 8, s // ts_in), [(n1, (ts_in, D_MODEL), tk0, dz, (ts_in, half_in), tkj)], "tn",
                             [((N_CHIPS, D_MODEL, IN_WIDTH // N_CHIPS), F32, (None, D_MODEL, half_in),
                               lambda i, j, k: (j // 2, 0, j % 2))], acc_shape=(D_MODEL, half_in), comm=exchange)

    d_bbt_re = _block_diag_t(d_bre, SSM_GROUP, SSM_STATE)
    d_bbt_im = _block_diag_t(d_bim, SSM_GROUP, SSM_STATE)
    d_a_re, d_a_im, d_log_dt, d_bt_re, d_bt_im = _ssm_param_bwd(
        sm["a_re"], sm["a_im"], log_dt_col, bt_re, bt_im,
        d_lr.reshape(SSM_GROUPS, SSM_STATE), d_li.reshape(SSM_GROUPS, SSM_STATE), d_bbt_re, d_bbt_im)
    small = {
        "g_mix": dg_mix, "a_re": d_a_re, "a_im": d_a_im, "log_dt": d_log_dt,
        "b_re": jnp.transpose(d_bt_re, (0, 2, 1)), "b_im": jnp.transpose(d_bt_im, (0, 2, 1)),
        "c_re": jnp.transpose(_block_diag_t(d_cre, SSM_STATE, SSM_GROUP), (0, 2, 1)),
        "c_im": jnp.transpose(_block_diag_t(d_cim, SSM_STATE, SSM_GROUP), (0, 2, 1)),
        "d_skip": d_dskip, "g_ffn": dg_ffn, "g_final": dg_final,
    }
    return loss_acc[0, 0], grad_x, d_w_in, early_parts, early_got, small


BIG = ("w_in", "w_attn_proj", "w_glu_a", "w_glu_b", "w_out", "w_ffn_gate", "w_ffn_up", "w_ffn_down", "w_ple_gate",
       "w_ple_proj")
SMALL = ("g_mix", "a_re", "a_im", "log_dt", "b_re", "b_im", "c_re", "c_im", "d_skip", "g_ffn", "g_final")
ANY = pl.BlockSpec(memory_space=pl.ANY)


def _place():
    x, y, c = lax.axis_index("x"), lax.axis_index("y"), lax.axis_index("c")
    chips = [(1 - x, y), (x, 1 - y), (1 - x, 1 - y)]
    return x, y, c, chips


def _remote(src, dst, send_sem, recv_sem, to):
    return pltpu.make_async_remote_copy(src_ref=src, dst_ref=dst, send_sem=send_sem, recv_sem=recv_sem, device_id=to,
                                        device_id_type=MESH)


def _comm_call(name, body, ins, out_shapes, n_sems, aliases=None):
    n_w = len(ins)
    return pl.pallas_call(
        body, name=name, in_specs=[ANY] * n_w, out_specs=[ANY] * len(out_shapes), out_shape=out_shapes,
        scratch_shapes=[pltpu.SemaphoreType.DMA((n,)) for n in n_sems], input_output_aliases=aliases or {},
    )(*ins)


def _gather_weights(bufs):
    n_w = len(bufs)
    start, forward, finish = _gather_stages(n_w)

    def body(*refs):
        ins, outs, sems = refs[:n_w], refs[n_w:2 * n_w], refs[2 * n_w:]
        start(ins, outs, sems)
        forward(ins, outs, sems)
        finish(ins, outs, sems)

    out_shapes = [jax.ShapeDtypeStruct(b.shape, b.dtype) for b in bufs]
    return _comm_call("gather_weights", body, bufs, out_shapes, [3 * n_w] * 4, aliases={w: w for w in range(n_w)})


def _gather_stages(n_w):
    def each():
        x, y, c, chips = _place()
        for w in range(n_w):
            for j, (cx, cy) in enumerate(chips):
                yield w, 3 * w + j, 2 * x + y, 2 * cx + cy, (cx, cy, c), (x, y, 1 - c), c

    def start(ins, outs, sems):
        for w, k, me, _, peer, _, c in each():
            mine = outs[w].at[me, c]
            _remote(mine, mine, sems[0].at[k], sems[1].at[k], peer).start()

    def forward(ins, outs, sems):
        for w, k, _, src_chip, peer, sib, c in each():
            landed = outs[w].at[src_chip, c]
            _remote(landed, landed, sems[0].at[k], sems[1].at[k], peer).wait_recv()
            _remote(landed, landed, sems[2].at[k], sems[3].at[k], sib).start()

    def finish(ins, outs, sems):
        for w, k, me, src_chip, peer, sib, c in each():
            other = outs[w].at[src_chip, 1 - c]
            _remote(other, other, sems[2].at[k], sems[3].at[k], sib).wait_recv()
        for w, k, me, src_chip, peer, sib, c in each():
            mine = outs[w].at[me, c]
            _remote(mine, mine, sems[0].at[k], sems[1].at[k], peer).wait_send()
            landed = outs[w].at[src_chip, c]
            _remote(landed, landed, sems[2].at[k], sems[3].at[k], sib).wait_send()

    return start, forward, finish


def _pair_exchange(grads):
    n_w = len(grads)

    def body(*refs):
        ins, outs = refs[:n_w], refs[n_w:2 * n_w]
        send, recv = refs[2 * n_w:]
        x, y, c, _ = _place()
        sib = (x, y, 1 - c)
        cps = []
        for w in range(n_w):
            for q in range(N_CHIPS):
                k = N_CHIPS * w + q
                cp = _remote(ins[w].at[q, 1 - c], outs[w].at[q], send.at[k], recv.at[k], sib)
                cp.start()
                cps.append(cp)
        for cp in cps:
            cp.wait()

    out_shapes = [jax.ShapeDtypeStruct((N_CHIPS,) + g.shape[2:], g.dtype) for g in grads]
    return _comm_call("grad_pair_exchange", body, grads, out_shapes, [N_CHIPS * n_w] * 2)


def _chip_exchange(parts):
    n_w = len(parts)

    start, finish = _chip_exchange_stages(n_w)

    def body(*refs):
        ins, outs, sems = refs[:n_w], refs[n_w:2 * n_w], refs[2 * n_w:]
        start(ins, outs, sems)
        finish(ins, outs, sems)

    out_shapes = [jax.ShapeDtypeStruct(t.shape, t.dtype) for t in parts]
    return _comm_call("grad_chip_exchange", body, parts, out_shapes, [3 * n_w, 3 * n_w])


def _chip_exchange_stages(n_w):
    def each():
        x, y, c, chips = _place()
        for w in range(n_w):
            for j, (cx, cy) in enumerate(chips):
                yield w, 3 * w + j, 2 * x + y, 2 * cx + cy, (cx, cy, c)

    def start(ins, outs, sems):
        for w, k, me, peer_chip, peer in each():
            _remote(ins[w].at[peer_chip], outs[w].at[me], sems[0].at[k], sems[1].at[k], peer).start()

    def finish(ins, outs, sems):
        for w, k, me, peer_chip, peer in each():
            got = outs[w].at[peer_chip]
            _remote(got, got, sems[0].at[k], sems[1].at[k], peer).wait_recv()
        for w, k, me, peer_chip, peer in each():
            _remote(ins[w].at[peer_chip], outs[w].at[me], sems[0].at[k], sems[1].at[k], peer).wait_send()

    return start, finish


def _pair_gather(halves):
    n_w = len(halves)

    def body(*refs):
        ins, outs = refs[:n_w], refs[n_w:2 * n_w]
        send, recv = refs[2 * n_w:]
        x, y, c, _ = _place()
        sib = (x, y, 1 - c)
        cps = []
        for w in range(n_w):
            cp = _remote(ins[w], outs[w], send.at[w], recv.at[w], sib)
            cp.start()
            cps.append(cp)
        for cp in cps:
            cp.wait()

    out_shapes = [jax.ShapeDtypeStruct(h.shape, h.dtype) for h in halves]
    return _comm_call("grad_pair_gather", body, halves, out_shapes, [n_w] * 2)


def _all_exchange(vec):
    def body(in_ref, out_ref, send, recv):
        x, y, c, _ = _place()
        me = 4 * x + 2 * y + c
        cps = []
        for k in range(1, 8):
            fx, fy, fc = (k >> 2) & 1, (k >> 1) & 1, k & 1
            to = (x ^ fx, y ^ fy, c ^ fc)
            cp = _remote(in_ref, out_ref.at[me], send.at[k - 1], recv.at[k - 1], to)
            cp.start()
            cps.append(cp)
        for k in range(1, 8):
            fx, fy, fc = (k >> 2) & 1, (k >> 1) & 1, k & 1
            src = 4 * (x ^ fx) + 2 * (y ^ fy) + (c ^ fc)
            got = out_ref.at[src]
            _remote(got, got, send.at[k - 1], recv.at[k - 1], (x ^ fx, y ^ fy, c ^ fc)).wait_recv()
        for cp in cps:
            cp.wait_send()

    return _comm_call("small_all_exchange", body, [vec], [jax.ShapeDtypeStruct((8,) + vec.shape, vec.dtype)], [7, 7])[0]


def _row_tile(r):
    for t in (256, 128, 176, 64, 32, 16, 8):
        if r % t == 0:
            return t
    return r


P_C, P_CHIP, P_DEV = 2, 3, 4


def _cast_into_slot(w2, place):
    r, c = w2.shape
    t = _row_tile(r)
    return _ew("cast_shard", (r // t,), [(w2, (t, c), lambda i, pv: (i, 0))],
               [((N_CHIPS, r, c), BF16, (None, t, c), lambda i, pv: (pv[P_CHIP], i, 0))],
               lambda pids, a: ((a,), ()), place=place)[0]


def _pair_sum(mine, theirs, place):
    _, r, c = theirs.shape
    t = _row_tile(r)
    own = ((None, None, t, c), lambda q, i, pv: (q, pv[P_C], i, 0))
    blk = ((None, t, c), lambda q, i, pv: (q, i, 0))
    return _ew("grad_pair_sum", (N_CHIPS, r // t), [(mine, *own), (theirs, *blk)], [((N_CHIPS, r, c), BF16, *blk)],
               lambda pids, a, b: ((a + b,), ()), place=place)[0]


def _chip_sum(own, got, place):
    _, r, c = own.shape
    t = _row_tile(r)
    ins = []
    for q in range(N_CHIPS):
        ins.append((own, (None, t, c), (lambda i, pv, q=q: (q, i, 0))))
        ins.append((got, (None, t, c), (lambda i, pv, q=q: (jnp.where(pv[P_CHIP] == q, (q + 1) % N_CHIPS, q), i, 0))))

    def fn(pids, *tiles):
        me = pids[0][P_CHIP]
        tot = None
        for q in range(N_CHIPS):
            term = jnp.where(me == q, tiles[2 * q], tiles[2 * q + 1]).astype(F32)
            tot = term if tot is None else tot + term
        return (tot,), ()

    return _ew("grad_chip_sum", (r // t,), ins, [((r, c), F32, (t, c), lambda i, pv: (i, 0))], fn, place=place)[0]


def _adamw_tile(w, g, m, v):
    m = ADAM_B1 * m + (1.0 - ADAM_B1) * g
    v = ADAM_B2 * v + (1.0 - ADAM_B2) * (g * g)
    m_hat = m / (1.0 - ADAM_B1 ** ADAM_STEP)
    v_hat = v / (1.0 - ADAM_B2 ** ADAM_STEP)
    delta = -ADAM_LR * (m_hat / (jnp.sqrt(v_hat) + ADAM_EPS) + ADAM_WD * w)
    return delta, m, v


def _adamw(name, g2, w2, m2, v2):
    r, c = w2.shape
    t = _row_tile(r)
    blk, imap = _rows(t, c)

    def fn(pids, g, w, m, v):
        delta, nm, nv = _adamw_tile(w, g, m, v)
        return (g, delta, nm, nv), ()

    return _ew(name, (r // t,), [(a, blk, imap) for a in (g2, w2, m2, v2)], [((r, c), F32, blk, imap)] * 4, fn)


def _adamw_halves(name, mine, theirs, w2, m2, v2, place):
    r, c = w2.shape
    t = _row_tile(r // 2)
    n_t = (r // 2) // t
    half = ((t, c), lambda h, i, pv: (i, 0))
    whole = ((t, c), lambda h, i, pv: (h * n_t + i, 0))

    def fn(pids, ga, gb, w, m, v):
        g = jnp.where(pids[1] == pids[0][P_C], ga, gb)
        delta, nm, nv = _adamw_tile(w, g, m, v)
        return (g, delta, nm, nv), ()

    return _ew(name, (2, n_t), [(mine, *half), (theirs, *half), (w2, *whole), (m2, *whole), (v2, *whole)],
               [((r, c), F32, *whole)] * 4, fn, place=place)


def _device_sum(own, got, place):
    r, c = own.shape
    t = _row_tile(r)
    ins = [(own, (t, c), lambda i, pv: (i, 0))]
    for q in range(8):
        ins.append((got, (None, t, c), (lambda i, pv, q=q: (jnp.where(pv[P_DEV] == q, (q + 1) % 8, q), i, 0))))

    def fn(pids, mine, *parts):
        me = pids[0][P_DEV]
        tot = None
        for q in range(8):
            term = jnp.where(me == q, mine, parts[q])
            tot = term if tot is None else tot + term
        return (tot,), ()

    return _ew("small_device_sum", (r // t,), ins, [((r, c), F32, (t, c), lambda i, pv: (i, 0))], fn, place=place)[0]


def _pack(parts):
    flat = jnp.concatenate([a.reshape(-1) for a in parts])
    pad = (-flat.shape[0]) % (SUB * 128)
    return jnp.pad(flat, (0, pad)).reshape(-1, 128)


def _unpack(mat, shapes):
    flat = mat.reshape(-1)
    out, off = [], 0
    for shp in shapes:
        n = math.prod(shp)
        out.append(flat[off:off + n].reshape(shp))
        off += n
    return out


def kernel(x, p, positions, g_mix, w_in, a_re, a_im, log_dt, b_re, b_im, c_re, c_im, d_skip, w_attn_proj, w_glu_a, w_glu_b, w_out, g_ffn, w_ffn_gate, w_ffn_up, w_ffn_down, w_ple_gate, w_ple_proj, g_final, loss_target, m_g_mix, m_w_in, m_a_re, m_a_im, m_log_dt, m_b_re, m_b_im, m_c_re, m_c_im, m_d_skip, m_w_attn_proj, m_w_glu_a, m_w_glu_b, m_w_out, m_g_ffn, m_w_ffn_gate, m_w_ffn_up, m_w_ffn_down, m_w_ple_gate, m_w_ple_proj, m_g_final, v_g_mix, v_w_in, v_a_re, v_a_im, v_log_dt, v_b_re, v_b_im, v_c_re, v_c_im, v_d_skip, v_w_attn_proj, v_w_glu_a, v_w_glu_b, v_w_out, v_g_ffn, v_w_ffn_gate, v_w_ffn_up, v_w_ffn_down, v_w_ple_gate, v_w_ple_proj, v_g_final):
    given = dict(locals())
    big_w = {n: given[n] for n in BIG}
    w_mats = {n: big_w[n].reshape(big_w[n].shape[1:]) for n in BIG}

    ax, ay, ac = lax.axis_index("x"), lax.axis_index("y"), lax.axis_index("c")
    place = jnp.stack([ax, ay, ac, 2 * ax + ay, 4 * ax + 2 * ay + ac]).astype(jnp.int32)

    bufs = []
    for n in BIG:
        r, c = w_mats[n].shape
        bufs.append(_cast_into_slot(w_mats[n], place).reshape(N_CHIPS, 2, r // 2, c))
    w_in_all = _gather_weights(bufs[:1])[0].reshape((N_CHIPS,) + w_mats["w_in"].shape)

    sm = {
        "g_mix": g_mix.reshape(1, D_MODEL), "g_ffn": g_ffn.reshape(1, D_MODEL), "g_final": g_final.reshape(1, D_MODEL),
        "a_re": a_re[0], "a_im": a_im[0], "log_dt": log_dt[0], "b_re": b_re[0], "b_im": b_im[0], "c_re": c_re[0],
        "c_im": c_im[0], "d_skip": d_skip[0],
    }
    s = x.shape[1]
    loss_part, grad_x, d_w_in, early_parts, early_got, small_g = _local_step(
        x[0], p[0, 0], positions[0], loss_target[0], sm, w_in_all, bufs[1:], place)

    r_in, c_in = w_mats["w_in"].shape
    g5_in = [d_w_in.reshape(N_CHIPS, 2, r_in // 2, c_in)]
    in_parts = [_pair_sum(g, t, place) for g, t in zip(g5_in, _pair_exchange(g5_in))]
    chip_parts = in_parts + list(early_parts)
    chip_got = list(_chip_exchange(in_parts)) + list(early_got)
    halves = [_chip_sum(own, got, place) for own, got in zip(chip_parts, chip_got)]
    other_halves = _pair_gather(halves)

    results = {}
    for n, mine, other in zip(BIG, halves, other_halves):
        r, c = w_mats[n].shape
        shp = big_w[n].shape
        outs = _adamw_halves("adamw_" + n, mine, other, w_mats[n], given["m_" + n].reshape(r, c),
                             given["v_" + n].reshape(r, c), place)
        results[n] = [o.reshape(shp) for o in outs]

    small_shapes = [given[n].shape for n in SMALL]
    vec = _pack([small_g[n] for n in SMALL] + [loss_part.reshape(1)])
    tot = _device_sum(vec, _all_exchange(vec), place)
    n_small = sum(math.prod(shp) for shp in small_shapes)
    loss = tot.reshape(-1)[n_small]
    w_s = _pack([given[n] for n in SMALL])
    m_s = _pack([given["m_" + n] for n in SMALL])
    v_s = _pack([given["v_" + n] for n in SMALL])
    rows_s = w_s.shape[0]
    g_s = tot.reshape(-1)[: rows_s * 128].reshape(rows_s, 128)
    outs_s = _adamw("adamw_small", g_s, w_s, m_s, v_s)
    for kind, mat in enumerate(outs_s):
        for n, arr in zip(SMALL, _unpack(mat, small_shapes)):
            results.setdefault(n, [None] * 4)[kind] = arr

    order = ("g_mix", "w_in", "a_re", "a_im", "log_dt", "b_re", "b_im", "c_re", "c_im", "d_skip", "w_attn_proj", "w_glu_a",
             "w_glu_b", "w_out", "g_ffn", "w_ffn_gate", "w_ffn_up", "w_ffn_down", "w_ple_gate", "w_ple_proj", "g_final")
    out = [loss, grad_x.reshape(1, s, D_MODEL)]
    for kind in range(4):
        out += [results[n][kind] for n in order]
    return tuple(out)
```

```python
import math

import jax
import jax.numpy as jnp
from jax import lax
from jax.experimental import pallas as pl
from jax.experimental.pallas import tpu as pltpu

F32 = jnp.float32
BF16 = jnp.bfloat16

D_MODEL = 1024
HEAD_DIM = 128
HEADS_PER_GROUP = 4
GROUP_WIDTH = HEADS_PER_GROUP * HEAD_DIM
GROUP_DILATIONS = (1, 4, 16)
N_GROUPS = len(GROUP_DILATIONS)
LSE_LANES = 32
LSE_WIDTH = HEADS_PER_GROUP * LSE_LANES
ATTN_BLOCK = 128
ROPE_DIM = 32
ROPE_HALF = 16
ROPE_THETA = 500000.0
SSM_WIDTH = 512
SSM_GROUPS = 32
SSM_GROUP = 16
SSM_STATE = 64
N_STATE = SSM_GROUPS * SSM_STATE
SSM_SUPER = 4
IN_WIDTH = 7168
COL_U = 4608
COL_GA = 5120
COL_GS = 6144
D_FF = 2816
N_CHIPS = 4
D_FF_Q = D_FF // N_CHIPS
PLE_DIM = 256
EPS = 1e-6
ADAM_LR = 0.001
ADAM_B1 = 0.9
ADAM_B2 = 0.999
ADAM_EPS = 1e-08
ADAM_WD = 0.01
ADAM_STEP = 10
NEG_BIG = -1e30
VMEM_LIMIT_BYTES = 56 * 1024 * 1024
MESH = pl.DeviceIdType.MESH

_DIMS = {
    "nn": (((1,), (0,)), ((), ())),
    "nt": (((1,), (1,)), ((), ())),
    "tn": (((0,), (0,)), ((), ())),
}


def _params(n_grid):
    return pltpu.CompilerParams(dimension_semantics=("arbitrary",) * n_grid, vmem_limit_bytes=VMEM_LIMIT_BYTES)


def _sig(v):
    return 1.0 / (1.0 + jnp.exp(-v))


def _dot(a, b, mode):
    return lax.dot_general(a, b, _DIMS[mode], preferred_element_type=F32)


def _mm(name, grid, pairs, mode, outs, epilogue=None, extras=(), acc_outs=(), acc_shape=None, j_outer=False,
        sum_pairs=True, resident_b=False, comm=None):
    gi, gj, gk = grid
    n_p, n_e, n_o, n_a = len(pairs), len(extras), len(outs), len(acc_outs)
    assert not n_a or gj == 1
    assert sum_pairs or gk == 1
    run_grid = (gj, gi, gk) if j_outer else grid
    c_ins = list(comm["ins"]) if comm else []
    c_outs = list(comm["outs"]) if comm else []
    c_sems = list(comm["sems"]) if comm else []
    n_ci, n_co, n_cs = len(c_ins), len(c_outs), len(c_sems)

    def order(imap):
        return (lambda j, i, k: imap(i, j, k)) if j_outer else imap

    shared_a = [pr[0] is None for pr in pairs]
    n_in = 2 * n_p - sum(shared_a)

    def body(*refs):
        pair_refs = list(refs[:n_in])
        extra_refs = refs[n_in: n_in + n_e]
        comm_in = refs[n_in + n_e: n_in + n_e + n_ci]
        at = n_in + n_e + n_ci
        out_refs = refs[at: at + n_o]
        sum_refs = refs[at + n_o: at + n_o + n_a]
        comm_out = refs[at + n_o + n_a: at + n_o + n_a + n_co]
        scratch_refs = refs[at + n_o + n_a + n_co:]
        i = pl.program_id(1 if j_outer else 0)
        k = pl.program_id(2)
        if comm:
            step = (pl.program_id(0) * run_grid[1] + pl.program_id(1)) * run_grid[2] + pl.program_id(2)
            sems = scratch_refs[len(scratch_refs) - n_cs:]
            for at_step, stage in comm["stages"]:
                @pl.when(step == at_step)
                def _(stage=stage):
                    stage(comm_in, comm_out, sems)
        part = None if sum_pairs else []
        a = None
        for t in range(n_p):
            if not shared_a[t]:
                a = pair_refs.pop(0)[...].astype(BF16)
            b = pair_refs.pop(0)[...].astype(BF16)
            d = _dot(a, b, mode)
            if sum_pairs:
                part = d if part is None else part + d
            else:
                part.append(d)

        def finish(acc):
            tiles, sums = epilogue(acc, *[e[...] for e in extra_refs]) if epilogue is not None else ((acc,), ())
            for o_ref, tile in zip(out_refs, tiles):
                o_ref[...] = tile.astype(o_ref.dtype)
            if n_a:
                @pl.when(i == 0)
                def _():
                    for s_ref in sum_refs:
                        s_ref[...] = jnp.zeros_like(s_ref)

                for s_ref, s in zip(sum_refs, sums):
                    s_ref[...] += s

        if gk == 1:
            finish(part)
        else:
            acc_ref = scratch_refs[0]

            @pl.when(k == 0)
            def _():
                acc_ref[...] = part

            @pl.when(k > 0)
            def _():
                acc_ref[...] += part

            @pl.when(k == gk - 1)
            def _():
                finish(acc_ref[...])

    in_specs, args = [], []
    for a, a_block, a_imap, b, b_block, b_imap in pairs:
        if a is not None:
            in_specs.append(pl.BlockSpec(a_block, order(a_imap)))
            args.append(a)
        if resident_b:
            in_specs.append(pl.BlockSpec(b_block, order(b_imap), pipeline_mode=pl.Buffered(1)))
        else:
            in_specs.append(pl.BlockSpec(b_block, order(b_imap)))
        args.append(b)
    for e, e_block, e_imap in extras:
        in_specs.append(pl.BlockSpec(e_block, order(e_imap)))
        args.append(e)
    first_comm_in = len(args)
    for c_in in c_ins:
        in_specs.append(pl.BlockSpec(memory_space=pl.ANY))
        args.append(c_in)
    out_shape = [jax.ShapeDtypeStruct(shape, dtype) for shape, dtype, _, _ in outs]
    out_specs = [pl.BlockSpec(block, order(imap)) for _, _, block, imap in outs]
    for shape, dtype in acc_outs:
        out_shape.append(jax.ShapeDtypeStruct(shape, dtype))
        out_specs.append(pl.BlockSpec(shape, lambda i, j, k: (0, 0)))
    first_comm_out = len(out_shape)
    for c_out in c_outs:
        out_shape.append(c_out)
        out_specs.append(pl.BlockSpec(memory_space=pl.ANY))
    aliases = {first_comm_in + n: first_comm_out + n for n in range(n_ci)} if comm and comm["aliased"] else {}
    scratch = [pltpu.VMEM(acc_shape, F32)] if gk > 1 else []
    scratch += [pltpu.SemaphoreType.DMA((n,)) for n in c_sems]
    return pl.pallas_call(
        body, name=name, grid=run_grid, in_specs=in_specs, out_specs=out_specs,
        out_shape=out_shape, scratch_shapes=scratch, compiler_params=_params(3), input_output_aliases=aliases,
    )(*args)


def _ew(name, grid, ins, outs, fn, acc_outs=(), place=None):
    n_i, n_o, n_a = len(ins), len(outs), len(acc_outs)
    ng = len(grid)
    n_s = 0 if place is None else 1

    def body(*refs):
        in_refs = refs[n_s: n_s + n_i]
        out_refs = refs[n_s + n_i: n_s + n_i + n_o]
        sum_refs = refs[n_s + n_i + n_o:]
        pids = tuple(pl.program_id(a) for a in range(ng))
        if n_s:
            pids = (refs[0],) + pids
        tiles, sums = fn(pids, *[r[...] for r in in_refs])
        for o_ref, tile in zip(out_refs, tiles):
            o_ref[...] = tile.astype(o_ref.dtype)
        if n_a:
            first = pids[0] == 0
            for p_ in pids[1:]:
                first = jnp.logical_and(first, p_ == 0)

            @pl.when(first)
            def _():
                for s_ref in sum_refs:
                    s_ref[...] = jnp.zeros_like(s_ref)

            for s_ref, s in zip(sum_refs, sums):
                s_ref[...] += s

    in_specs = [pl.BlockSpec(block, imap) for _, block, imap in ins]
    out_shape = [jax.ShapeDtypeStruct(shape, dtype) for shape, dtype, _, _ in outs]
    out_specs = [pl.BlockSpec(block, imap) for _, _, block, imap in outs]
    for shape, dtype in acc_outs:
        out_shape.append(jax.ShapeDtypeStruct(shape, dtype))
        out_specs.append(pl.BlockSpec(shape, lambda *_, nd=len(shape): (0,) * nd))
    arrays = [a for a, _, _ in ins]
    if n_s:
        assert not n_a
        spec = pltpu.PrefetchScalarGridSpec(num_scalar_prefetch=1, grid=grid, in_specs=in_specs, out_specs=out_specs)
        return pl.pallas_call(body, name=name, grid_spec=spec, out_shape=out_shape, compiler_params=_params(ng))(
            place, *arrays)
    return pl.pallas_call(
        body, name=name, grid=grid, in_specs=in_specs, out_specs=out_specs, out_shape=out_shape,
        compiler_params=_params(ng),
    )(*arrays)


def _rows(tm, width):
    return (tm, width), (lambda i: (i, 0))


def _rms_fwd_tile(h, g):
    r = lax.rsqrt(jnp.mean(h * h, axis=-1, keepdims=True) + EPS)
    return h * r * g


def _rms_bwd_tile(dn, h, g):
    r = lax.rsqrt(jnp.mean(h * h, axis=-1, keepdims=True) + EPS)
    hhat = h * r
    gy = dn * g
    dh = r * (gy - hhat * jnp.mean(gy * hhat, axis=-1, keepdims=True))
    dg = jnp.sum(dn * hhat, axis=0, keepdims=True)
    return dh, dg


def _rope_tables(pos_col, inv_row, tm):
    s = pos_col.shape[0]

    def fn(pids, pos, inv):
        ang = pos * inv
        lane = lax.broadcasted_iota(jnp.int32, ang.shape, 1)
        cs = jnp.where(lane < ROPE_DIM, jnp.cos(ang), 1.0)
        sn = jnp.sin(ang)
        s_lo = jnp.where(lane < ROPE_HALF, -sn, 0.0)
        s_hi = jnp.where(jnp.logical_and(lane >= ROPE_HALF, lane < ROPE_DIM), sn, 0.0)
        return (cs, s_lo, s_hi), ()

    blk, imap = _rows(tm, 128)
    return _ew(
        "rope_tables", (s // tm,),
        [(pos_col, (tm, 1), lambda i: (i, 0)), (inv_row, (1, 128), lambda i: (0, 0))],
        [((s, 128), F32, blk, imap)] * 3, fn,
    )


def _rope(xh, cs, s_lo, s_hi):
    return xh * cs + pltpu.roll(xh, HEAD_DIM - ROPE_HALF, 1) * s_lo + pltpu.roll(xh, ROPE_HALF, 1) * s_hi


def _rope_t(gh, cs, s_lo, s_hi):
    return gh * cs + pltpu.roll(gh * s_lo, ROPE_HALF, 1) + pltpu.roll(gh * s_hi, HEAD_DIM - ROPE_HALF, 1)


def _attn_geometry(length):
    nb = length // ATTN_BLOCK
    gq = min(4, nb)
    assert nb % gq == 0
    return nb, gq, gq * ATTN_BLOCK, nb // gq


def _band_masks():
    qi = lax.broadcasted_iota(jnp.int32, (ATTN_BLOCK, ATTN_BLOCK), 0)
    kj = lax.broadcasted_iota(jnp.int32, (ATTN_BLOCK, ATTN_BLOCK), 1)
    return kj <= qi, kj >= qi


def _band_mask_pair():
    qi = lax.broadcasted_iota(jnp.int32, (ATTN_BLOCK, 2 * ATTN_BLOCK), 0)
    cj = lax.broadcasted_iota(jnp.int32, (ATTN_BLOCK, 2 * ATTN_BLOCK), 1)
    in_cur = cj >= ATTN_BLOCK
    band = jnp.logical_or(jnp.logical_and(in_cur, cj - ATTN_BLOCK <= qi),
                          jnp.logical_and(cj < ATTN_BLOCK, cj >= qi))
    return band, in_cur


def _attn_fwd(qv, kv, vv, dil, cols3=(0, 0, 0)):
    length = qv.shape[0]
    nb, gq, rows, ni = _attn_geometry(length)

    def body(q_ref, kc_ref, kp_ref, vc_ref, vp_ref, o_ref, l_ref):
        i = pl.program_id(1)
        band, in_cur = _band_mask_pair()
        band_first = jnp.logical_and(band, jnp.logical_or(in_cur, i > 0))
        work = []
        for h in range(HEADS_PER_GROUP):
            cols = slice(h * HEAD_DIM, (h + 1) * HEAD_DIM)
            qh = q_ref[:, cols]
            k_all = jnp.concatenate([kp_ref[:, cols], kc_ref[:, cols]], axis=0)
            v_all = jnp.concatenate([vp_ref[:, cols], vc_ref[:, cols]], axis=0)
            for jj in range(gq):
                rws = slice(jj * ATTN_BLOCK, (jj + 1) * ATTN_BLOCK)
                two = slice(jj * ATTN_BLOCK, (jj + 2) * ATTN_BLOCK)
                work.append(dict(h=h, rws=rws, cols=cols, v=v_all[two], first=jj == 0, s=_dot(qh[rws], k_all[two], "nt")))
        for w in work:
            s = jnp.where(band_first if w["first"] else band, w["s"], NEG_BIG)
            m = jnp.max(s, axis=-1, keepdims=True)
            pexp = jnp.exp(s - m)
            w["den"] = jnp.sum(pexp, axis=-1, keepdims=True)
            w["p"] = pexp.astype(BF16)
            w["lse"] = m + jnp.log(w["den"])
        for w in work:
            o = _dot(w["p"], w["v"], "nn")
            o_ref[w["rws"], w["cols"]] = (o / w["den"]).astype(o_ref.dtype)
            l_ref[w["rws"], w["h"] * LSE_LANES:(w["h"] + 1) * LSE_LANES] = jnp.broadcast_to(w["lse"], (ATTN_BLOCK, LSE_LANES))

    def cur(c):
        return pl.BlockSpec((rows, GROUP_WIDTH), lambda r, i: (i, r + c))

    def prev(c):
        return pl.BlockSpec((ATTN_BLOCK, GROUP_WIDTH), lambda r, i: (jnp.maximum(i * gq - 1, 0), r + c))

    cq, ck, cv = cols3
    return pl.pallas_call(
        body, name=f"attn_fwd_d{dil}", grid=(dil, ni),
        in_specs=[cur(cq), cur(ck), prev(ck), cur(cv), prev(cv)],
        out_specs=[cur(0), pl.BlockSpec((rows, LSE_WIDTH), lambda r, i: (i, r))],
        out_shape=[jax.ShapeDtypeStruct((length, dil * GROUP_WIDTH), BF16),
                   jax.ShapeDtypeStruct((length, dil * LSE_WIDTH), F32)],
        compiler_params=_params(2),
    )(qv, kv, kv, vv, vv)


def _attn_bwd(qv, kv, vv, dov, ov, lv, dil, cols3=(0, 0, 0)):
    length = qv.shape[0]
    nb, gq, rows, ni = _attn_geometry(length)
    out_shape = (length, dil * GROUP_WIDTH)

    def body(qc_ref, qn_ref, kc_ref, kp_ref, vc_ref, vp_ref, doc_ref, don_ref, oc_ref, on_ref, lc_ref, ln_ref,
             dq_ref, dk_ref, dv_ref):
        i = pl.program_id(1)
        _, mask_p = _band_masks()
        band, in_cur = _band_mask_pair()
        band_first = jnp.logical_and(band, jnp.logical_or(in_cur, i > 0))
        has_next = i < ni - 1

        last = slice(gq * ATTN_BLOCK, (gq + 1) * ATTN_BLOCK)
        mask_next = jnp.logical_and(mask_p, has_next)

        def rows_of(jj):
            return slice(jj * ATTN_BLOCK, (jj + 1) * ATTN_BLOCK)

        def keys_of(jj):
            return slice(jj * ATTN_BLOCK, (jj + 2) * ATTN_BLOCK)

        heads = []
        for h in range(HEADS_PER_GROUP):
            cols = slice(h * HEAD_DIM, (h + 1) * HEAD_DIM)
            hd = dict(
                cols=cols, q_c=qc_ref[:, cols], q_n=qn_ref[:, cols],
                k_all=jnp.concatenate([kp_ref[:, cols], kc_ref[:, cols]], axis=0),
                v_all=jnp.concatenate([vp_ref[:, cols], vc_ref[:, cols]], axis=0),
                do_c=doc_ref[:, cols], do_n=don_ref[:, cols],
                l_c=lc_ref[:, h * LSE_LANES:h * LSE_LANES + 1], l_n=ln_ref[:, h * LSE_LANES:h * LSE_LANES + 1],
            )
            hd["dl_c"] = jnp.sum(hd["do_c"].astype(F32) * oc_ref[:, cols].astype(F32), axis=-1, keepdims=True)
            hd["dl_n"] = jnp.sum(hd["do_n"].astype(F32) * on_ref[:, cols].astype(F32), axis=-1, keepdims=True)
            hd["s"] = [_dot(hd["q_c"][rows_of(jj)], hd["k_all"][keys_of(jj)], "nt") for jj in range(gq)]
            hd["dp"] = [_dot(hd["do_c"][rows_of(jj)], hd["v_all"][keys_of(jj)], "nt") for jj in range(gq)]
            hd["s"].append(_dot(hd["q_n"], hd["k_all"][last], "nt"))
            hd["dp"].append(_dot(hd["do_n"], hd["v_all"][last], "nt"))
            heads.append(hd)
        for hd in heads:
            hd["p"], hd["ds"] = [], []
            for jj in range(gq + 1):
                if jj < gq:
                    mask, l_col, delta = (band_first if jj == 0 else band), hd["l_c"][rows_of(jj)], hd["dl_c"][rows_of(jj)]
                else:
                    mask, l_col, delta = mask_next, hd["l_n"], hd["dl_n"]
                p = jnp.where(mask, jnp.exp(hd["s"][jj] - l_col), 0.0)
                hd["p"].append(p.astype(BF16))
                hd["ds"].append((p * (hd["dp"][jj] - delta)).astype(BF16))
        for hd in heads:
            cols = hd["cols"]
            dk_blocks, dv_blocks = [None] * (gq + 1), [None] * (gq + 1)

            def add(lst, idx, val):
                lst[idx] = val if lst[idx] is None else lst[idx] + val

            for jj in range(gq):
                qb, dob = hd["q_c"][rows_of(jj)], hd["do_c"][rows_of(jj)]
                dq_ref[rows_of(jj), cols] = _dot(hd["ds"][jj], hd["k_all"][keys_of(jj)], "nn").astype(dq_ref.dtype)
                dk2 = _dot(hd["ds"][jj], qb, "tn")
                dv2 = _dot(hd["p"][jj], dob, "tn")
                add(dk_blocks, jj, dk2[:ATTN_BLOCK])
                add(dk_blocks, jj + 1, dk2[ATTN_BLOCK:])
                add(dv_blocks, jj, dv2[:ATTN_BLOCK])
                add(dv_blocks, jj + 1, dv2[ATTN_BLOCK:])
            add(dk_blocks, gq, _dot(hd["ds"][gq], hd["q_n"], "tn"))
            add(dv_blocks, gq, _dot(hd["p"][gq], hd["do_n"], "tn"))
            for jj in range(gq):
                dk_ref[rows_of(jj), cols] = dk_blocks[jj + 1].astype(dk_ref.dtype)
                dv_ref[rows_of(jj), cols] = dv_blocks[jj + 1].astype(dv_ref.dtype)

    def cur(c):
        return pl.BlockSpec((rows, GROUP_WIDTH), lambda r, i: (i, r + c))

    def prev(c):
        return pl.BlockSpec((ATTN_BLOCK, GROUP_WIDTH), lambda r, i: (jnp.maximum(i * gq - 1, 0), r + c))

    def nxt(c):
        return pl.BlockSpec((ATTN_BLOCK, GROUP_WIDTH), lambda r, i: (jnp.minimum((i + 1) * gq, nb - 1), r + c))

    cq, ck, cv = cols3
    lse_cur = pl.BlockSpec((rows, LSE_WIDTH), lambda r, i: (i, r))
    lse_next = pl.BlockSpec((ATTN_BLOCK, LSE_WIDTH), lambda r, i: (jnp.minimum((i + 1) * gq, nb - 1), r))
    return pl.pallas_call(
        body, name=f"attn_bwd_d{dil}", grid=(dil, ni),
        in_specs=[cur(cq), nxt(cq), cur(ck), prev(ck), cur(cv), prev(cv), cur(0), nxt(0), cur(0), nxt(0), lse_cur, lse_next],
        out_specs=[cur(0), cur(0), cur(0)],
        out_shape=[jax.ShapeDtypeStruct(out_shape, BF16)] * 3,
        compiler_params=_params(2),
    )(qv, qv, kv, kv, vv, vv, dov, dov, ov, ov, lv, lv)


DILATED = tuple((g, d) for g, d in enumerate(GROUP_DILATIONS) if d > 1)


def _spread(scr, slot, tile, out_ref, dil, col, width=GROUP_WIDTH):
    tm = tile.shape[0]
    buf = scr.at[slot]
    buf[...] = tile
    for r in range(dil):
        c0 = r * width + col
        out_ref[:, c0:c0 + HEAD_DIM] = buf[pl.ds(r, tm // dil, stride=dil), :].astype(out_ref.dtype)


def _collect(scr, slot, in_ref, dil, col, width=GROUP_WIDTH):
    tm = scr.shape[1]
    buf = scr.at[slot]
    for r in range(dil):
        c0 = r * width + col
        buf[pl.ds(r, tm // dil, stride=dil), :] = in_ref[:, c0:c0 + HEAD_DIM].astype(F32)
    return buf[...]


def _view_spec(tm, dil, width=GROUP_WIDTH):
    return pl.BlockSpec((tm // dil, dil * width), lambda i: (i, 0))


def _view_shape(s, dil, dtype, width=GROUP_WIDTH):
    return jax.ShapeDtypeStruct((s // dil, dil * width), dtype)


def _qkv_layout(z, tabs, tm):
    s = z.shape[0]
    scale = 1.0 / math.sqrt(HEAD_DIM)
    qkv_width = 3 * N_GROUPS * GROUP_WIDTH

    def body(z_ref, cs_ref, lo_ref, hi_ref, qk0_ref, *rest):
        views, scr = rest[:-1], rest[-1]
        tabs_ = (cs_ref[...], lo_ref[...], hi_ref[...])
        for part in range(3):
            for g, dil in enumerate(GROUP_DILATIONS):
                if part == 2 and dil == 1:
                    continue
                for h in range(HEADS_PER_GROUP):
                    col = part * N_GROUPS * GROUP_WIDTH + g * GROUP_WIDTH + h * HEAD_DIM
                    t = z_ref[:, col:col + HEAD_DIM].astype(F32)
                    if part < 2:
                        t = _rope(t, *tabs_)
                    if part == 0:
                        t = t * scale
                    if dil == 1:
                        c0 = part * GROUP_WIDTH + h * HEAD_DIM
                        qk0_ref[:, c0:c0 + HEAD_DIM] = t.astype(BF16)
                    else:
                        out = views[3 * [gg for gg, _ in DILATED].index(g) + part]
                        _spread(scr, h, t, out, dil, h * HEAD_DIM)

    row = lambda i: (i, 0)
    out_shape = [jax.ShapeDtypeStruct((s, 2 * GROUP_WIDTH), BF16)]
    out_specs = [pl.BlockSpec((tm, 2 * GROUP_WIDTH), row)]
    for _, dil in DILATED:
        out_shape += [_view_shape(s, dil, BF16)] * 3
        out_specs += [_view_spec(tm, dil)] * 3
    res = pl.pallas_call(
        body, name="qkv_layout", grid=(s // tm,),
        in_specs=[pl.BlockSpec((tm, qkv_width), row)] + [pl.BlockSpec((tm, HEAD_DIM), row)] * 3,
        out_specs=out_specs, out_shape=out_shape,
        scratch_shapes=[pltpu.VMEM((HEADS_PER_GROUP, tm, HEAD_DIM), F32)], compiler_params=_params(1),
    )(z, *tabs)
    return res[0], [tuple(res[1 + 3 * n:4 + 3 * n]) for n in range(len(DILATED))]


def _attn_merge(o0, l0, dilated, tm):
    s = o0.shape[0]
    n_d = len(DILATED)

    def body(*refs):
        o0_ref, l0_ref = refs[:2]
        in_views = refs[2:2 + 2 * n_d]
        attn_ref, lse_ref = refs[2 + 2 * n_d:4 + 2 * n_d]
        out_views = refs[4 + 2 * n_d:4 + 4 * n_d]
        scr = refs[-1]
        l_rows = [l0_ref[...]] + [_collect(scr, n, in_views[2 * n + 1], dil, 0, LSE_WIDTH) for n, (_, dil) in enumerate(DILATED)]
        lse_heads = []
        for h in range(HEADS_PER_GROUP):
            cols = slice(h * HEAD_DIM, (h + 1) * HEAD_DIM)
            os_ = [o0_ref[:, cols].astype(F32)]
            for n, (_, dil) in enumerate(DILATED):
                os_.append(_collect(scr, n_d + n, in_views[2 * n], dil, h * HEAD_DIM))
            ls_ = [lr[:, h * LSE_LANES:h * LSE_LANES + 1] for lr in l_rows]
            m = ls_[0]
            for l_ in ls_[1:]:
                m = jnp.maximum(m, l_)
            es = [jnp.exp(l_ - m) for l_ in ls_]
            den = es[0]
            num = es[0] * os_[0]
            for e, o in zip(es[1:], os_[1:]):
                den = den + e
                num = num + e * o
            attn = num / den
            lse_heads.append(jnp.broadcast_to(m + jnp.log(den), (tm, LSE_LANES)))
            attn_ref[:, cols] = attn.astype(BF16)
            for n, (_, dil) in enumerate(DILATED):
                _spread(scr, 2 * n_d, attn, out_views[2 * n], dil, h * HEAD_DIM)
        lse = jnp.concatenate(lse_heads, axis=1)
        lse_ref[...] = lse
        for n, (_, dil) in enumerate(DILATED):
            _spread(scr, 2 * n_d, lse, out_views[2 * n + 1], dil, 0, LSE_WIDTH)

    row = lambda i: (i, 0)
    nat = pl.BlockSpec((tm, GROUP_WIDTH), row)
    nat_l = pl.BlockSpec((tm, LSE_WIDTH), row)
    in_specs = [nat, nat_l]
    args = [o0, l0]
    out_specs = [nat, nat_l]
    out_shape = [jax.ShapeDtypeStruct((s, GROUP_WIDTH), BF16), jax.ShapeDtypeStruct((s, LSE_WIDTH), F32)]
    for (_, dil), (ov, lv) in zip(DILATED, dilated):
        in_specs += [_view_spec(tm, dil), _view_spec(tm, dil, LSE_WIDTH)]
        args += [ov, lv]
        out_specs += [_view_spec(tm, dil), _view_spec(tm, dil, LSE_WIDTH)]
        out_shape += [_view_shape(s, dil, BF16), _view_shape(s, dil, F32, LSE_WIDTH)]
    res = pl.pallas_call(
        body, name="attn_merge", grid=(s // tm,), in_specs=in_specs, out_specs=out_specs, out_shape=out_shape,
        scratch_shapes=[pltpu.VMEM((2 * n_d + 1, tm, HEAD_DIM), F32)], compiler_params=_params(1),
    )(*args)
    return res[0], res[1], [tuple(res[2 + 2 * n:4 + 2 * n]) for n in range(n_d)]


def _to_views(a, tm):
    s = a.shape[0]

    def body(a_ref, *rest):
        outs, scr = rest[:-1], rest[-1]
        for h in range(HEADS_PER_GROUP):
            t = a_ref[:, h * HEAD_DIM:(h + 1) * HEAD_DIM].astype(F32)
            for n, (_, dil) in enumerate(DILATED):
                _spread(scr, n, t, outs[n], dil, h * HEAD_DIM)

    return pl.pallas_call(
        body, name="to_views", grid=(s // tm,), in_specs=[pl.BlockSpec((tm, GROUP_WIDTH), lambda i: (i, 0))],
        out_specs=[_view_spec(tm, dil) for _, dil in DILATED], out_shape=[_view_shape(s, dil, BF16) for _, dil in DILATED],
        scratch_shapes=[pltpu.VMEM((len(DILATED), tm, HEAD_DIM), F32)], compiler_params=_params(1),
    )(a)


def _dz_layout(grads, du, dga, dgs, tabs, tm, comm=None):
    s = du.shape[0]
    scale = 1.0 / math.sqrt(HEAD_DIM)
    n_steps = s // tm
    c_ins = list(comm["ins"]) if comm else []
    c_outs = list(comm["outs"]) if comm else []
    c_sems = list(comm["sems"]) if comm else []
    n_fixed = 3 * N_GROUPS + 6

    def body(*refs):
        g_refs = refs[:3 * N_GROUPS]
        du_ref, dga_ref, dgs_ref, cs_ref, lo_ref, hi_ref = refs[3 * N_GROUPS:n_fixed]
        comm_in = refs[n_fixed:n_fixed + len(c_ins)]
        dz_ref = refs[n_fixed + len(c_ins)]
        comm_out = refs[n_fixed + len(c_ins) + 1:n_fixed + len(c_ins) + 1 + len(c_outs)]
        scr = refs[n_fixed + len(c_ins) + 1 + len(c_outs)]
        sems = refs[n_fixed + len(c_ins) + 2 + len(c_outs):]
        if comm:
            @pl.when(pl.program_id(0) == 0)
            def _():
                comm["start"](comm_in, comm_out, sems)

            @pl.when(pl.program_id(0) == n_steps - 1)
            def _():
                comm["finish"](comm_in, comm_out, sems)

        tabs_ = (cs_ref[...], lo_ref[...], hi_ref[...])
        for part in range(3):
            for g, dil in enumerate(GROUP_DILATIONS):
                src = g_refs[3 * g + part]
                for h in range(HEADS_PER_GROUP):
                    if dil == 1:
                        t = src[:, h * HEAD_DIM:(h + 1) * HEAD_DIM].astype(F32)
                    else:
                        t = _collect(scr, h, src, dil, h * HEAD_DIM)
                    if part < 2:
                        t = _rope_t(t, *tabs_)
                    if part == 0:
                        t = t * scale
                    col = part * N_GROUPS * GROUP_WIDTH + g * GROUP_WIDTH + h * HEAD_DIM
                    dz_ref[:, col:col + HEAD_DIM] = t.astype(BF16)
        dz_ref[:, COL_U:COL_GA] = du_ref[...]
        dz_ref[:, COL_GA:COL_GS] = dga_ref[...]
        dz_ref[:, COL_GS:IN_WIDTH] = dgs_ref[...]

    row = lambda i: (i, 0)
    in_specs, args = [], []
    for (g, dil), trio in zip(enumerate(GROUP_DILATIONS), grads):
        in_specs += [pl.BlockSpec((tm, GROUP_WIDTH), row) if dil == 1 else _view_spec(tm, dil)] * 3
        args += list(trio)
    in_specs += [pl.BlockSpec((tm, SSM_WIDTH), row), pl.BlockSpec((tm, D_MODEL), row), pl.BlockSpec((tm, D_MODEL), row)]
    in_specs += [pl.BlockSpec((tm, HEAD_DIM), row)] * 3
    in_specs += [pl.BlockSpec(memory_space=pl.ANY)] * len(c_ins)
    res = pl.pallas_call(
        body, name="dz_layout", grid=(n_steps,), in_specs=in_specs,
        out_specs=[pl.BlockSpec((tm, IN_WIDTH), row)] + [pl.BlockSpec(memory_space=pl.ANY)] * len(c_outs),
        out_shape=[jax.ShapeDtypeStruct((s, IN_WIDTH), BF16)] + c_outs,
        scratch_shapes=[pltpu.VMEM((HEADS_PER_GROUP, tm, HEAD_DIM), F32)] + [pltpu.SemaphoreType.DMA((n,)) for n in c_sems],
        compiler_params=_params(1),
    )(*args, du, dga, dgs, *tabs, *c_ins)
    return res[0], list(res[1:])


def _discretise(a_re, a_im, log_dt, bt_re, bt_im):
    dt = jnp.exp(log_dt)
    mag = jnp.exp(a_re * dt)
    bar_re = mag * jnp.cos(a_im * dt)
    bar_im = mag * jnp.sin(a_im * dt)
    nr = bar_re - 1.0
    ni = bar_im
    den = a_re * a_re + a_im * a_im
    z_re = (nr * a_re + ni * a_im) / den
    z_im = (ni * a_re - nr * a_im) / den
    bb_re = z_re[:, None, :] * bt_re - z_im[:, None, :] * bt_im
    bb_im = z_re[:, None, :] * bt_im + z_im[:, None, :] * bt_re
    return bar_re, bar_im, bb_re, bb_im


def _ssm_prep(a_re, a_im, log_dt, bt_re, bt_im):
    def body(ar, ai, ld, br, bi, o_lr, o_li, o_br, o_bi):
        lr, li, bbr, bbi = _discretise(ar[...], ai[...], ld[...], br[...], bi[...])
        o_lr[...] = lr
        o_li[...] = li
        o_br[...] = bbr
        o_bi[...] = bbi

    sm = jax.ShapeDtypeStruct((SSM_GROUPS, SSM_STATE), F32)
    bg = jax.ShapeDtypeStruct((SSM_GROUPS, SSM_GROUP, SSM_STATE), F32)
    return pl.pallas_call(body, name="ssm_prep", out_shape=[sm, sm, bg, bg])(a_re, a_im, log_dt, bt_re, bt_im)


def _ssm_param_bwd(a_re, a_im, log_dt, bt_re, bt_im, d_lr, d_li, d_bbr, d_bbi):
    def body(ar, ai, ld, br, bi, g_lr, g_li, g_br, g_bi, o_ar, o_ai, o_ld, o_br, o_bi):
        _, vjp = jax.vjp(_discretise, ar[...], ai[...], ld[...], br[...], bi[...])
        d_ar, d_ai, d_ld, d_br, d_bi = vjp((g_lr[...], g_li[...], g_br[...], g_bi[...]))
        o_ar[...] = d_ar
        o_ai[...] = d_ai
        o_ld[...] = d_ld
        o_br[...] = d_br
        o_bi[...] = d_bi

    sm = jax.ShapeDtypeStruct((SSM_GROUPS, SSM_STATE), F32)
    col = jax.ShapeDtypeStruct((SSM_GROUPS, 1), F32)
    bg = jax.ShapeDtypeStruct((SSM_GROUPS, SSM_GROUP, SSM_STATE), F32)
    return pl.pallas_call(body, name="ssm_param_bwd", out_shape=[sm, sm, col, bg, bg])(
        a_re, a_im, log_dt, bt_re, bt_im, d_lr, d_li, d_bbr, d_bbi)


def _block_diag(t, rows_per, cols_per):
    t4 = t.reshape(SSM_SUPER, 8, rows_per, cols_per)
    eye = jnp.eye(8, dtype=t.dtype)
    return jnp.einsum("bgrc,gh->bgrhc", t4, eye).reshape(SSM_SUPER, 8 * rows_per, 8 * cols_per)


def _block_diag_t(dense, rows_per, cols_per):
    t = dense.reshape(SSM_SUPER, 8, rows_per, 8, cols_per)
    eye = jnp.eye(8, dtype=dense.dtype)
    return jnp.einsum("bgrhc,gh->bgrc", t, eye).reshape(SSM_GROUPS, rows_per, cols_per)


def _gelu(v):
    c = math.sqrt(2.0 / math.pi)
    return 0.5 * v * (1.0 + jnp.tanh(c * (v + 0.044715 * v * v * v)))


def _gelu_grad(v):
    c = math.sqrt(2.0 / math.pi)
    t = jnp.tanh(c * (v + 0.044715 * v * v * v))
    return 0.5 * (1.0 + t) + 0.5 * v * (1.0 - t * t) * c * (1.0 + 3.0 * 0.044715 * v * v)


SUB = 8


SCAN_STEPS = (1, 2, 4)
N_SCAN_TABLES = 2 + 2 * len(SCAN_STEPS)


def _scan_tables(tab_ref, lam_re, lam_im, reverse, conj):
    lr = lam_re
    li = -lam_im if conj else lam_im
    powers = [(lr, li)]
    for _ in range(SUB - 1):
        pr, pi = powers[-1]
        powers.append((pr * lr - pi * li, pr * li + pi * lr))
    row = lax.broadcasted_iota(jnp.int32, (SUB, N_STATE), 0)
    if reverse:
        row = SUB - 1 - row
    wide = lambda v: jnp.broadcast_to(v, (SUB, N_STATE))
    p_re = jnp.zeros((SUB, N_STATE), F32)
    p_im = jnp.zeros((SUB, N_STATE), F32)
    for j in range(SUB):
        p_re = jnp.where(row == j, wide(powers[j][0]), p_re)
        p_im = jnp.where(row == j, wide(powers[j][1]), p_im)
    tab_ref[0] = p_re
    tab_ref[1] = p_im
    for idx, k in enumerate(SCAN_STEPS):
        tab_ref[2 + 2 * idx] = jnp.where(row >= k, wide(powers[k - 1][0]), 0.0)
        tab_ref[3 + 2 * idx] = jnp.where(row >= k, wide(powers[k - 1][1]), 0.0)


def _scan_rows(g_re_ref, g_im_ref, tab_ref, carry, n_rows, reverse):
    last = 0 if reverse else SUB - 1

    def tile_step(tt, state):
        cr, ci = state
        t8 = (n_rows // SUB - 1 - tt) if reverse else tt
        start = pl.multiple_of(t8 * SUB, SUB)
        xr = g_re_ref[pl.ds(start, SUB), :]
        xi = g_im_ref[pl.ds(start, SUB), :]
        for idx, k in enumerate(SCAN_STEPS):
            mr = tab_ref[2 + 2 * idx]
            mi = tab_ref[3 + 2 * idx]
            shift = SUB - k if reverse else k
            sr = pltpu.roll(xr, shift, 0)
            si = pltpu.roll(xi, shift, 0)
            xr, xi = xr + (mr * sr - mi * si), xi + (mr * si + mi * sr)
        pr = tab_ref[0]
        pi = tab_ref[1]
        xr, xi = xr + (pr * cr - pi * ci), xi + (pr * ci + pi * cr)
        g_re_ref[pl.ds(start, SUB), :] = xr
        g_im_ref[pl.ds(start, SUB), :] = xi
        return (jnp.broadcast_to(xr[last:last + 1, :], (SUB, N_STATE)),
                jnp.broadcast_to(xi[last:last + 1, :], (SUB, N_STATE)))

    return lax.fori_loop(0, n_rows // SUB, tile_step, carry)


def _ssm_fwd(z, b_re, b_im, c_re, c_im, lam_re, lam_im, d_skip, chunk):
    s = z.shape[0]

    def body(u_ref, bre, bim, cre, cim, lre, lim, dsk, hre_ref, him_ref, ys_ref, yg_ref, car_re, car_im, tabs):
        i = pl.program_id(0)

        @pl.when(i == 0)
        def _():
            car_re[...] = jnp.zeros_like(car_re)
            car_im[...] = jnp.zeros_like(car_im)
            _scan_tables(tabs, lre[...], lim[...], False, False)

        u = u_ref[...]
        for b in range(SSM_SUPER):
            ub = u[:, b * 128:(b + 1) * 128]
            st = slice(b * 512, (b + 1) * 512)
            hre_ref[:, st] = _dot(ub, bre[b], "nn")
            him_ref[:, st] = _dot(ub, bim[b], "nn")
        sr, si = _scan_rows(hre_ref, him_ref, tabs, (car_re[...], car_im[...]), chunk, False)
        car_re[...] = sr
        car_im[...] = si
        uf = u.astype(F32)
        for b in range(SSM_SUPER):
            st = slice(b * 512, (b + 1) * 512)
            ch = slice(b * 128, (b + 1) * 128)
            y = _dot(hre_ref[:, st].astype(BF16), cre[b], "nn") - _dot(him_ref[:, st].astype(BF16), cim[b], "nn")
            y = y + dsk[:, ch] * uf[:, ch]
            ys_ref[:, ch] = y
            yg_ref[:, ch] = _gelu(y).astype(BF16)

    full3 = lambda i: (0, 0, 0)
    full2 = lambda i: (0, 0)
    row = lambda i: (i, 0)
    u_col = COL_U // SSM_WIDTH
    return pl.pallas_call(
        body, name="ssm_fwd", grid=(s // chunk,),
        in_specs=[pl.BlockSpec((chunk, SSM_WIDTH), lambda i: (i, u_col)),
                  pl.BlockSpec((SSM_SUPER, 128, 512), full3), pl.BlockSpec((SSM_SUPER, 128, 512), full3),
                  pl.BlockSpec((SSM_SUPER, 512, 128), full3), pl.BlockSpec((SSM_SUPER, 512, 128), full3),
                  pl.BlockSpec((1, N_STATE), full2), pl.BlockSpec((1, N_STATE), full2), pl.BlockSpec((1, SSM_WIDTH), full2)],
        out_specs=[pl.BlockSpec((chunk, N_STATE), row), pl.BlockSpec((chunk, N_STATE), row),
                   pl.BlockSpec((chunk, SSM_WIDTH), row), pl.BlockSpec((chunk, SSM_WIDTH), row)],
        out_shape=[jax.ShapeDtypeStruct((s, N_STATE), F32), jax.ShapeDtypeStruct((s, N_STATE), F32),
                   jax.ShapeDtypeStruct((s, SSM_WIDTH), F32), jax.ShapeDtypeStruct((s, SSM_WIDTH), BF16)],
        scratch_shapes=[pltpu.VMEM((SUB, N_STATE), F32), pltpu.VMEM((SUB, N_STATE), F32),
                        pltpu.VMEM((N_SCAN_TABLES, SUB, N_STATE), F32)],
        compiler_params=_params(1),
    )(z, b_re, b_im, c_re, c_im, lam_re, lam_im, d_skip)


def _ssm_bwd(dys, z, h_re, h_im, b_re, b_im, c_re, c_im, lam_re, lam_im, d_skip, chunk):
    s = z.shape[0]
    n_chunks = s // chunk

    def body(dy_ref, u_ref, hre_ref, him_ref, hpr_ref, hpi_ref, bre, bim, cre, cim, lre, lim, dsk,
             du_ref, dlr_ref, dli_ref, dbr_ref, dbi_ref, dcr_ref, dci_ref, dd_ref, are, aim, car_re, car_im, tabs):
        i = pl.program_id(0)
        n = n_chunks - 1 - i

        @pl.when(i == 0)
        def _():
            car_re[...] = jnp.zeros_like(car_re)
            car_im[...] = jnp.zeros_like(car_im)
            _scan_tables(tabs, lre[...], lim[...], True, True)
            for r in (dlr_ref, dli_ref, dbr_ref, dbi_ref, dcr_ref, dci_ref, dd_ref):
                r[...] = jnp.zeros_like(r)

        dy = dy_ref[...]
        dyb = dy.astype(BF16)
        u = u_ref[...]
        for b in range(SSM_SUPER):
            ch = slice(b * 128, (b + 1) * 128)
            st = slice(b * 512, (b + 1) * 512)
            are[:, st] = _dot(dyb[:, ch], cre[b], "nt")
            aim[:, st] = -_dot(dyb[:, ch], cim[b], "nt")
        sr, si = _scan_rows(are, aim, tabs, (car_re[...], car_im[...]), chunk, True)
        car_re[...] = sr
        car_im[...] = si
        row_id = lax.broadcasted_iota(jnp.int32, (chunk, N_STATE), 0)
        top_scale = jnp.where(n > 0, 1.0, 0.0)
        h_r = hre_ref[...]
        h_i = him_ref[...]
        hp_r = jnp.where(row_id == 0, hpr_ref[SUB - 1:SUB, :] * top_scale, pltpu.roll(h_r, 1, 0))
        hp_i = jnp.where(row_id == 0, hpi_ref[SUB - 1:SUB, :] * top_scale, pltpu.roll(h_i, 1, 0))
        a_r = are[...]
        a_i = aim[...]
        dlr_ref[...] += jnp.sum(a_r * hp_r + a_i * hp_i, axis=0, keepdims=True)
        dli_ref[...] += jnp.sum(a_i * hp_r - a_r * hp_i, axis=0, keepdims=True)
        dd_ref[...] += jnp.sum(dy * u.astype(F32), axis=0, keepdims=True)
        a_rb = a_r.astype(BF16)
        a_ib = a_i.astype(BF16)
        h_rb = h_r.astype(BF16)
        h_ib = h_i.astype(BF16)
        for b in range(SSM_SUPER):
            ch = slice(b * 128, (b + 1) * 128)
            st = slice(b * 512, (b + 1) * 512)
            dbr_ref[b] += _dot(u[:, ch], a_rb[:, st], "tn")
            dbi_ref[b] += _dot(u[:, ch], a_ib[:, st], "tn")
            dcr_ref[b] += _dot(h_rb[:, st], dyb[:, ch], "tn")
            dci_ref[b] += -_dot(h_ib[:, st], dyb[:, ch], "tn")
            du = _dot(a_rb[:, st], bre[b], "nt") + _dot(a_ib[:, st], bim[b], "nt") + dsk[:, ch] * dy[:, ch]
            du_ref[:, ch] = du.astype(du_ref.dtype)

    full3 = lambda i: (0, 0, 0)
    full2 = lambda i: (0, 0)
    rev = lambda i: (n_chunks - 1 - i, 0)
    above = lambda i: (jnp.maximum((n_chunks - 1 - i) * (chunk // SUB) - 1, 0), 0)
    u_col = COL_U // SSM_WIDTH
    b_spec = pl.BlockSpec((SSM_SUPER, 128, 512), full3)
    c_spec = pl.BlockSpec((SSM_SUPER, 512, 128), full3)
    vec = pl.BlockSpec((1, N_STATE), full2)
    return pl.pallas_call(
        body, name="ssm_bwd", grid=(n_chunks,),
        in_specs=[pl.BlockSpec((chunk, SSM_WIDTH), rev),
                  pl.BlockSpec((chunk, SSM_WIDTH), lambda i: (n_chunks - 1 - i, u_col)),
                  pl.BlockSpec((chunk, N_STATE), rev), pl.BlockSpec((chunk, N_STATE), rev),
                  pl.BlockSpec((SUB, N_STATE), above), pl.BlockSpec((SUB, N_STATE), above),
                  b_spec, b_spec, c_spec, c_spec, vec, vec, pl.BlockSpec((1, SSM_WIDTH), full2)],
        out_specs=[pl.BlockSpec((chunk, SSM_WIDTH), rev), vec, vec, b_spec, b_spec, c_spec, c_spec,
                   pl.BlockSpec((1, SSM_WIDTH), full2)],
        out_shape=[jax.ShapeDtypeStruct((s, SSM_WIDTH), BF16),
                   jax.ShapeDtypeStruct((1, N_STATE), F32), jax.ShapeDtypeStruct((1, N_STATE), F32),
                   jax.ShapeDtypeStruct((SSM_SUPER, 128, 512), F32), jax.ShapeDtypeStruct((SSM_SUPER, 128, 512), F32),
                   jax.ShapeDtypeStruct((SSM_SUPER, 512, 128), F32), jax.ShapeDtypeStruct((SSM_SUPER, 512, 128), F32),
                   jax.ShapeDtypeStruct((1, SSM_WIDTH), F32)],
        scratch_shapes=[pltpu.VMEM((chunk, N_STATE), F32), pltpu.VMEM((chunk, N_STATE), F32),
                        pltpu.VMEM((SUB, N_STATE), F32), pltpu.VMEM((SUB, N_STATE), F32),
                        pltpu.VMEM((N_SCAN_TABLES, SUB, N_STATE), F32)],
        compiler_params=_params(1),
    )(dys, z, h_re, h_im, h_re, h_im, b_re, b_im, c_re, c_im, lam_re, lam_im, d_skip)


def _local_step(x, p, pos, tgt, sm, w_in, late_bufs, place):
    s = x.shape[0]
    tm = min(512, s)
    ts = min(1024, s)
    chunk = min(256, s)
    ni = s // tm
    nk = s // ts
    g_mix, g_ffn, g_final = sm["g_mix"], sm["g_ffn"], sm["g_final"]
    rowblk, rowmap = _rows(tm, D_MODEL)
    vec1k = ((1, D_MODEL), lambda *_: (0, 0))

    (n1,) = _ew("rms_mix", (ni,), [(x, rowblk, rowmap), (g_mix, *vec1k)], [((s, D_MODEL), BF16, rowblk, rowmap)],
                lambda pids, h, g: ((_rms_fwd_tile(h, g),), ()))

    half_in = IN_WIDTH // 8
    tmb = min(1024, s)
    nib = s // tmb
    n_late = len(late_bufs)
    g_start, g_forward, g_finish = _gather_stages(n_late)
    in_steps = N_CHIPS * nib
    chip_w = IN_WIDTH // N_CHIPS
    gather = dict(ins=late_bufs, outs=[jax.ShapeDtypeStruct(b.shape, b.dtype) for b in late_bufs], aliased=True,
                  sems=[3 * n_late] * 4, stages=[(0, g_start), (in_steps // 2, g_forward), (in_steps - 1, g_finish)])
    z, *late = _mm("in_proj", (nib, N_CHIPS, 1),
                   [(n1, (tmb, D_MODEL), lambda i, j, k: (i, 0), w_in, (None, D_MODEL, chip_w), lambda i, j, k: (j, 0, 0))],
                   "nn", [((s, IN_WIDTH), BF16, (tmb, chip_w), lambda i, j, k: (i, j))], j_outer=True, comm=gather)
    w_ap, w_ga, w_gb, w_out, w_fg, w_fu, w_fd, w_pg, w_pp = (
        g.reshape(N_CHIPS, 2 * g.shape[2], g.shape[3]) for g in late)
    w_out2 = w_out.reshape(D_MODEL, D_MODEL)
    w_pg2 = w_pg.reshape(D_MODEL, D_MODEL)

    inv = ROPE_THETA ** (-jnp.arange(ROPE_HALF, dtype=F32) * 2.0 / ROPE_DIM)
    inv_row = jnp.concatenate([inv, inv, jnp.zeros((HEAD_DIM - ROPE_DIM,), F32)]).reshape(1, HEAD_DIM)
    tabs = _rope_tables(pos.astype(F32).reshape(s, 1), inv_row, tm)

    qk0, qkv_views = _qkv_layout(z, tabs, tm)
    v0_col = (2 * N_GROUPS * GROUP_WIDTH) // GROUP_WIDTH
    group_in = [((qk0, qk0, z), (0, 1, v0_col))] + [(trio, (0, 0, 0)) for trio in qkv_views]
    fwd_out = [_attn_fwd(*arrs, dil, cols3) for (arrs, cols3), dil in zip(group_in, GROUP_DILATIONS)]
    attn, lse, merged_views = _attn_merge(fwd_out[0][0], fwd_out[0][1], fwd_out[1:], tm)

    def chip_cols(parts):
        return (jnp.concatenate(parts, axis=1),), ()

    def proj_cols(name, a, width, w):
        blk = (None, width, 256)
        pairs = [(a, (tmb, width), lambda i, j, k: (i, 0), w, blk, lambda i, j, k: (0, 0, 0))]
        pairs += [(None, None, None, w, blk, (lambda i, j, k, q=q: (q, 0, 0))) for q in range(1, N_CHIPS)]
        return _mm(name, (nib, 1, 1), pairs, "nn", [((s, D_MODEL), BF16, (tmb, D_MODEL), lambda i, j, k: (i, 0))],
                   epilogue=chip_cols, sum_pairs=False)[0]

    def proj512(name, a, w):
        return proj_cols(name, a, GROUP_WIDTH, w)

    attn_d = proj512("attn_proj", attn, w_ap)

    bt_re = jnp.transpose(sm["b_re"], (0, 2, 1))
    bt_im = jnp.transpose(sm["b_im"], (0, 2, 1))
    log_dt_col = sm["log_dt"].reshape(SSM_GROUPS, 1)
    lam_re, lam_im, bbt_re, bbt_im = _ssm_prep(sm["a_re"], sm["a_im"], log_dt_col, bt_re, bt_im)
    b_re_m = _block_diag(bbt_re, SSM_GROUP, SSM_STATE).astype(BF16)
    b_im_m = _block_diag(bbt_im, SSM_GROUP, SSM_STATE).astype(BF16)
    c_re_m = _block_diag(jnp.transpose(sm["c_re"], (0, 2, 1)), SSM_STATE, SSM_GROUP).astype(BF16)
    c_im_m = _block_diag(jnp.transpose(sm["c_im"], (0, 2, 1)), SSM_STATE, SSM_GROUP).astype(BF16)
    lam_re_row = lam_re.reshape(1, N_STATE)
    lam_im_row = lam_im.reshape(1, N_STATE)
    d_skip_row = sm["d_skip"].reshape(1, SSM_WIDTH)
    h_re, h_im, ys, yg = _ssm_fwd(z, b_re_m, b_im_m, c_re_m, c_im_m, lam_re_row, lam_im_row, d_skip_row, chunk)

    pa = proj512("glu_a", yg, w_ga)
    pb = proj512("glu_b", yg, w_gb)

    ga_blk = ((tm, D_MODEL), lambda i: (i, COL_GA // D_MODEL))
    gs_blk = ((tm, D_MODEL), lambda i: (i, COL_GS // D_MODEL))

    def mix_fn(pids, ga, gs, ad, a, b):
        ga, gs, ad, a, b = (t.astype(F32) for t in (ga, gs, ad, a, b))
        return (_sig(ga) * ad + _sig(gs) * (a * _sig(b)),), ()

    (mix,) = _ew("gate_mix", (ni,), [(z, *ga_blk), (z, *gs_blk), (attn_d, rowblk, rowmap), (pa, rowblk, rowmap),
                                     (pb, rowblk, rowmap)], [((s, D_MODEL), BF16, rowblk, rowmap)], mix_fn)

    def out_epi(acc, xr, g):
        h1 = acc + xr
        return (h1, _rms_fwd_tile(h1, g)), ()

    m3 = lambda i, j, k: (i, 0)
    w3 = lambda i, j, k: (0, 0)
    h1, n2 = _mm("out_proj", (nib, 1, 1), [(mix, (tmb, D_MODEL), m3, w_out2, (D_MODEL, D_MODEL), w3)], "nn",
                 [((s, D_MODEL), F32, (tmb, D_MODEL), m3), ((s, D_MODEL), BF16, (tmb, D_MODEL), m3)],
                 epilogue=out_epi, extras=[(x, (tmb, D_MODEL), m3), (g_ffn, (1, D_MODEL), w3)])

    ffq = (None, tm, D_FF_Q)
    ffq_map = lambda i, j, k: (j, i, 0)

    def ffn_in_epi(parts):
        gts, ups = parts[0::2], parts[1::2]
        acts = [gt * _sig(gt) * u_ for gt, u_ in zip(gts, ups)]
        return (jnp.stack(gts, axis=0), jnp.stack(ups, axis=0), jnp.stack(acts, axis=0)), ()

    w_ffq = (None, D_MODEL, D_FF_Q)
    ff_pairs = []
    for q in range(N_CHIPS):
        blk_q = lambda i, j, k, q=q: (q, 0, 0)
        ff_pairs.append((n2, (tm, D_MODEL), m3, w_fg, w_ffq, blk_q) if q == 0 else (None, None, None, w_fg, w_ffq, blk_q))
        ff_pairs.append((None, None, None, w_fu, w_ffq, blk_q))
    ff_all = (N_CHIPS, tm, D_FF_Q)
    ff_all_map = lambda i, j, k: (0, i, 0)
    gate, up, act = _mm("ffn_gate_up", (ni, 1, 1), ff_pairs, "nn",
                        [((N_CHIPS, s, D_FF_Q), BF16, ff_all, ff_all_map)] * 3, epilogue=ffn_in_epi,
                        sum_pairs=False, resident_b=True)

    (h2,) = _mm("ffn_down", (nib, 1, 1),
                [(act, (None, tmb, D_FF_Q), (lambda i, j, k, q=q: (q, i, 0)), w_fd, (None, D_FF_Q, D_MODEL),
                  (lambda i, j, k, q=q: (q, 0, 0))) for q in range(N_CHIPS)], "nn",
                [((s, D_MODEL), F32, (tmb, D_MODEL), m3)], epilogue=lambda acc, hr: ((acc + hr,), ()),
                extras=[(h1, (tmb, D_MODEL), m3)])

    pp = proj_cols("ple_proj", p, PLE_DIM, w_pp)

    def ple_head_epi(acc, hr, ppr, t, g):
        sg = _sig(acc)
        ppf = ppr.astype(F32)
        h = hr + sg * ppf
        r = lax.rsqrt(jnp.mean(h * h, axis=-1, keepdims=True) + EPS)
        hhat = h * r
        diff = hhat * g - t
        loss = 0.5 * jnp.sum(jnp.mean(diff * diff, axis=-1, keepdims=True))
        dy = diff * (1.0 / D_MODEL)
        gy = dy * g
        dh = r * (gy - hhat * jnp.mean(gy * hhat, axis=-1, keepdims=True))
        return ((dh, dh * ppf * sg * (1.0 - sg), dh * sg),
                (jnp.full((SUB, 128), loss, F32), jnp.sum(dy * hhat, axis=0, keepdims=True)))

    tile_row = (tm, D_MODEL)
    dh3, dgl, dpp, loss_acc, dg_final = _mm(
        "ple_gate_head", (ni, 1, 1), [(h2, tile_row, m3, w_pg2, (D_MODEL, D_MODEL), w3)], "nn",
        [((s, D_MODEL), F32, tile_row, m3), ((s, D_MODEL), BF16, tile_row, m3), ((s, D_MODEL), BF16, tile_row, m3)],
        epilogue=ple_head_epi,
        extras=[(h2, tile_row, m3), (pp, tile_row, m3), (tgt, tile_row, m3), (g_final, (1, D_MODEL), w3)],
        acc_outs=[((SUB, 128), F32), ((1, D_MODEL), F32)])

    def wgrad(name, a, a_block, a_imap, b, b_block, b_imap, out_shape, out_block, out_imap, nj, acc_shape):
        return _mm(name, (1, nj, nk), [(a, a_block, a_imap, b, b_block, b_imap)], "tn",
                   [(out_shape, F32, out_block, out_imap)], acc_shape=acc_shape)[0]

    tk0 = lambda i, j, k: (k, 0)
    tkj = lambda i, j, k: (k, j)
    def wgrad_cols(name, a, width, dy_):
        def split(acc):
            return (jnp.stack([acc[:, q * 256:(q + 1) * 256] for q in range(N_CHIPS)], axis=0),), ()

        return _mm(name, (1, 1, nk), [(a, (ts, width), tk0, dy_, (ts, D_MODEL), tk0)], "tn",
                   [((N_CHIPS, width, 256), F32, (N_CHIPS, width, 256), lambda i, j, k: (0, 0, 0))], epilogue=split,
                   acc_shape=(width, D_MODEL))[0]

    d_w_pp = wgrad_cols("d_ple_proj", p, PLE_DIM, dpp)
    d_w_pg = wgrad("d_ple_gate", h2, (ts, D_MODEL), tk0, dgl, (ts, D_MODEL), tk0, (D_MODEL, D_MODEL),
                   (D_MODEL, D_MODEL), w3, 1, (D_MODEL, D_MODEL))

    (dh2,) = _mm("ple_gate_bwd", (nib, 1, 1), [(dgl, (tmb, D_MODEL), m3, w_pg2, (D_MODEL, D_MODEL), w3)], "nt",
                 [((s, D_MODEL), F32, (tmb, D_MODEL), m3)], epilogue=lambda acc, d_: ((acc + d_,), ()),
                 extras=[(dh3, (tmb, D_MODEL), m3)])

    def ffn_bwd_epi(parts, gt_all, u_all):
        dgs_, dus_ = [], []
        for q, dact in enumerate(parts):
            gt, u_ = gt_all[q].astype(F32), u_all[q].astype(F32)
            sg = _sig(gt)
            dgs_.append(dact * u_ * (sg * (1.0 + gt * (1.0 - sg))))
            dus_.append(dact * gt * sg)
        return (jnp.stack(dgs_, axis=0), jnp.stack(dus_, axis=0)), ()

    fd_pairs = [((dh2, (tm, D_MODEL), m3) if q == 0 else (None, None, None))
                + (w_fd, (None, D_FF_Q, D_MODEL), (lambda i, j, k, q=q: (q, 0, 0))) for q in range(N_CHIPS)]
    dgate, dup = _mm("ffn_down_bwd", (ni, 1, 1), fd_pairs, "nt",
                     [((N_CHIPS, s, D_FF_Q), BF16, ff_all, ff_all_map)] * 2, epilogue=ffn_bwd_epi,
                     extras=[(gate, ff_all, ff_all_map), (up, ff_all, ff_all_map)], sum_pairs=False, resident_b=True)

    ffq_t = (None, ts, D_FF_Q)
    ffq_tmap = lambda i, j, k: (j, k, 0)
    blk_j = lambda i, j, k: (j, 0, 0)
    d_w_fd = wgrad("d_ffn_down", act, ffq_t, ffq_tmap, dh2, (ts, D_MODEL), tk0, (N_CHIPS, D_FF_Q, D_MODEL),
                   (None, D_FF_Q, D_MODEL), blk_j, N_CHIPS, (D_FF_Q, D_MODEL))
    d_w_fg = wgrad("d_ffn_gate", n2, (ts, D_MODEL), tk0, dgate, ffq_t, ffq_tmap, (N_CHIPS, D_MODEL, D_FF_Q),
                   (None, D_MODEL, D_FF_Q), blk_j, N_CHIPS, (D_MODEL, D_FF_Q))
    d_w_fu = wgrad("d_ffn_up", n2, (ts, D_MODEL), tk0, dup, ffq_t, ffq_tmap, (N_CHIPS, D_MODEL, D_FF_Q),
                   (None, D_MODEL, D_FF_Q), blk_j, N_CHIPS, (D_MODEL, D_FF_Q))

    def norm_bwd_epi(acc, h, d_res, g):
        dh, dg = _rms_bwd_tile(acc, h, g)
        return (d_res + dh,), (dg,)

    ffq_k = lambda i, j, k: (k, i, 0)
    blk_k = lambda i, j, k: (k, 0, 0)
    fi_pairs = []
    for q in range(N_CHIPS):
        a_q = lambda i, j, k, q=q: (q, i, 0)
        b_q = lambda i, j, k, q=q: (q, 0, 0)
        fi_pairs.append((dgate, ffq, a_q, w_fg, (None, D_MODEL, D_FF_Q), b_q))
        fi_pairs.append((dup, ffq, a_q, w_fu, (None, D_MODEL, D_FF_Q), b_q))
    dh1, dg_ffn = _mm("ffn_in_bwd", (ni, 1, 1), fi_pairs, "nt",
                      [((s, D_MODEL), F32, (tm, D_MODEL), m3)], epilogue=norm_bwd_epi,
                      extras=[(h1, (tm, D_MODEL), m3), (dh2, (tm, D_MODEL), m3), (g_ffn, (1, D_MODEL), w3)],
                      acc_outs=[((1, D_MODEL), F32)], resident_b=True)

    d_w_out = wgrad("d_out_proj", mix, (ts, D_MODEL), tk0, dh1, (ts, D_MODEL), tk0, (D_MODEL, D_MODEL),
                    (D_MODEL, D_MODEL), w3, 1, (D_MODEL, D_MODEL))

    def mix_bwd_epi(dm, ga, gs, ad, a, b):
        ga, gs, ad, a, b = (t.astype(F32) for t in (ga, gs, ad, a, b))
        s_a, s_s, s_b = _sig(ga), _sig(gs), _sig(b)
        d_ssm = dm * s_s
        return (dm * ad * s_a * (1.0 - s_a), dm * (a * s_b) * s_s * (1.0 - s_s), dm * s_a, d_ssm * s_b,
                d_ssm * a * s_b * (1.0 - s_b)), ()

    tile_m = (tm, D_MODEL)
    dga, dgs, dattn_d, dpa, dpb = _mm(
        "out_proj_bwd", (ni, 1, 1), [(dh1, tile_m, m3, w_out2, (D_MODEL, D_MODEL), w3)], "nt",
        [((s, D_MODEL), BF16, tile_m, m3)] * 5, epilogue=mix_bwd_epi,
        extras=[(z, tile_m, lambda i, j, k: (i, COL_GA // D_MODEL)), (z, tile_m, lambda i, j, k: (i, COL_GS // D_MODEL)),
                (attn_d, tile_m, m3), (pa, tile_m, m3), (pb, tile_m, m3)])

    d_w_ap = wgrad_cols("d_attn_proj", attn, GROUP_WIDTH, dattn_d)
    d_w_ga = wgrad_cols("d_glu_a", yg, GROUP_WIDTH, dpa)
    d_w_gb = wgrad_cols("d_glu_b", yg, GROUP_WIDTH, dpb)

    ik = lambda i, j, k: (i, k)

    def cols_bwd(dy_, w):
        return [(dy_, (tmb, 256), (lambda i, j, k, q=q: (i, q)), w, (None, GROUP_WIDTH, 256),
                 (lambda i, j, k, q=q: (q, 0, 0))) for q in range(N_CHIPS)]

    (dattn,) = _mm("attn_proj_bwd", (nib, 1, 1), cols_bwd(dattn_d, w_ap), "nt",
                   [((s, GROUP_WIDTH), BF16, (tmb, GROUP_WIDTH), m3)])

    (dys,) = _mm("glu_bwd", (nib, 1, 1), cols_bwd(dpa, w_ga) + cols_bwd(dpb, w_gb), "nt",
                 [((s, GROUP_WIDTH), F32, (tmb, GROUP_WIDTH), m3)],
                 epilogue=lambda acc, y_: ((acc * _gelu_grad(y_),), ()),
                 extras=[(ys, (tmb, GROUP_WIDTH), m3)])

    du, d_lr, d_li, d_bre, d_bim, d_cre, d_cim, d_dskip = _ssm_bwd(
        dys, z, h_re, h_im, b_re_m, b_im_m, c_re_m, c_im_m, lam_re_row, lam_im_row, d_skip_row, chunk)

    dattn_views = _to_views(dattn, tm)
    bwd_in = [(dattn, attn, lse)] + [(dv_, ov_, lv_) for dv_, (ov_, lv_) in zip(dattn_views, merged_views)]
    qkv_grads = [_attn_bwd(*arrs, *dol, dil, cols3)
                 for (arrs, cols3), dol, dil in zip(group_in, bwd_in, GROUP_DILATIONS)]
    early = [d_w_ap, d_w_ga, d_w_gb, d_w_out.reshape(N_CHIPS, D_MODEL // N_CHIPS, D_MODEL), d_w_fg, d_w_fu, d_w_fd,
             d_w_pg.reshape(N_CHIPS, D_MODEL // N_CHIPS, D_MODEL), d_w_pp]
    early5 = [g.reshape(N_CHIPS, 2, g.shape[1] // 2, g.shape[2]) for g in early]
    n_e = len(early5)
    p_start, p_finish = _pair_exchange_stages(n_e)
    dz, early_theirs = _dz_layout(
        qkv_grads, du, dga, dgs, tabs, tm,
        comm=dict(ins=early5, outs=_pair_exchange_shapes(early5), sems=[N_CHIPS * n_e] * 2, start=p_start, finish=p_finish))
    early_parts = [_pair_sum(g, t, place) for g, t in zip(early5, early_theirs)]

    chip_in = IN_WIDTH // N_CHIPS
    ip_pairs = [(dz, (tm, chip_in), (lambda i, j, k, q=q: (i, q)), w_in, (None, D_MODEL, chip_in),
                 (lambda i, j, k, q=q: (q, 0, 0))) for q in range(N_CHIPS)]
    grad_x, dg_mix = _mm("in_proj_bwd", (ni, 1, 1), ip_pairs, "nt",
                         [((s, D_MODEL), F32, (tm, D_MODEL), m3)], epilogue=norm_bwd_epi,
                         extras=[(x, (tm, D_MODEL), m3), (dh1, (tm, D_MODEL), m3), (g_mix, (1, D_MODEL), w3)],
                         acc_outs=[((1, D_MODEL), F32)], resident_b=True)

    d_bbt_re = _block_diag_t(d_bre, SSM_GROUP, SSM_STATE)
    d_bbt_im = _block_diag_t(d_bim, SSM_GROUP, SSM_STATE)
    d_a_re, d_a_im, d_log_dt, d_bt_re, d_bt_im = _ssm_param_bwd(
        sm["a_re"], sm["a_im"], log_dt_col, bt_re, bt_im,
        d_lr.reshape(SSM_GROUPS, SSM_STATE), d_li.reshape(SSM_GROUPS, SSM_STATE), d_bbt_re, d_bbt_im)
    small = {
        "g_mix": dg_mix, "a_re": d_a_re, "a_im": d_a_im, "log_dt": d_log_dt,
        "b_re": jnp.transpose(d_bt_re, (0, 2, 1)), "b_im": jnp.transpose(d_bt_im, (0, 2, 1)),
        "c_re": jnp.transpose(_block_diag_t(d_cre, SSM_STATE, SSM_GROUP), (0, 2, 1)),
        "c_im": jnp.transpose(_block_diag_t(d_cim, SSM_STATE, SSM_GROUP), (0, 2, 1)),
        "d_skip": d_dskip, "g_ffn": dg_ffn, "g_final": dg_final,
    }
    vec = _pack([small[n] for n in SMALL] + [loss_acc[0, 0].reshape(1)])

    x_start, x_finish = _chip_exchange_stages(n_e)
    v_start, v_finish = _all_exchange_stages()

    def both(f_chips, f_vec):
        def stage(ins, outs, sems):
            f_chips(ins[:n_e], outs[:n_e], sems[:2])
            f_vec(ins[n_e:], outs[n_e:], sems[2:])
        return stage

    ts_in = min(2048, s)
    win_steps = 8 * (s // ts_in)
    exchange = dict(ins=early_parts + [vec],
                    outs=[jax.ShapeDtypeStruct(t.shape, t.dtype) for t in early_parts]
                    + [jax.ShapeDtypeStruct((8,) + vec.shape, vec.dtype)],
                    aliased=False, sems=[3 * n_e, 3 * n_e, 7, 7],
                    stages=[(0, both(x_start, v_start)), (win_steps - 1, both(x_finish, v_finish))])
    d_w_in, *got = _mm("d_in_proj", (1, 8, s // ts_in), [(n1, (ts_in, D_MODEL), tk0, dz, (ts_in, half_in), tkj)], "tn",
                       [((N_CHIPS, D_MODEL, IN_WIDTH // N_CHIPS), F32, (None, D_MODEL, half_in),
                         lambda i, j, k: (j // 2, 0, j % 2))], acc_shape=(D_MODEL, half_in), comm=exchange)
    return grad_x, d_w_in, early_parts, got[:n_e], vec, got[n_e]


BIG = ("w_in", "w_attn_proj", "w_glu_a", "w_glu_b", "w_out", "w_ffn_gate", "w_ffn_up", "w_ffn_down", "w_ple_gate",
       "w_ple_proj")
SMALL = ("g_mix", "a_re", "a_im", "log_dt", "b_re", "b_im", "c_re", "c_im", "d_skip", "g_ffn", "g_final")
ANY = pl.BlockSpec(memory_space=pl.ANY)


def _place():
    x, y, c = lax.axis_index("x"), lax.axis_index("y"), lax.axis_index("c")
    chips = [(1 - x, y), (x, 1 - y), (1 - x, 1 - y)]
    return x, y, c, chips


def _remote(src, dst, send_sem, recv_sem, to):
    return pltpu.make_async_remote_copy(src_ref=src, dst_ref=dst, send_sem=send_sem, recv_sem=recv_sem, device_id=to,
                                        device_id_type=MESH)


def _comm_call(name, body, ins, out_shapes, n_sems, aliases=None):
    n_w = len(ins)
    return pl.pallas_call(
        body, name=name, in_specs=[ANY] * n_w, out_specs=[ANY] * len(out_shapes), out_shape=out_shapes,
        scratch_shapes=[pltpu.SemaphoreType.DMA((n,)) for n in n_sems], input_output_aliases=aliases or {},
    )(*ins)


def _gather_weights(bufs):
    n_w = len(bufs)
    start, forward, finish = _gather_stages(n_w)

    def body(*refs):
        ins, outs, sems = refs[:n_w], refs[n_w:2 * n_w], refs[2 * n_w:]
        start(ins, outs, sems)
        forward(ins, outs, sems)
        finish(ins, outs, sems)

    out_shapes = [jax.ShapeDtypeStruct(b.shape, b.dtype) for b in bufs]
    return _comm_call("gather_weights", body, bufs, out_shapes, [3 * n_w] * 4, aliases={w: w for w in range(n_w)})


def _gather_stages(n_w):
    def each():
        x, y, c, chips = _place()
        for w in range(n_w):
            for j, (cx, cy) in enumerate(chips):
                yield w, 3 * w + j, 2 * x + y, 2 * cx + cy, (cx, cy, c), (x, y, 1 - c), c

    def start(ins, outs, sems):
        for w, k, me, _, peer, _, c in each():
            mine = outs[w].at[me, c]
            _remote(mine, mine, sems[0].at[k], sems[1].at[k], peer).start()

    def forward(ins, outs, sems):
        for w, k, _, src_chip, peer, sib, c in each():
            landed = outs[w].at[src_chip, c]
            _remote(landed, landed, sems[0].at[k], sems[1].at[k], peer).wait_recv()
            _remote(landed, landed, sems[2].at[k], sems[3].at[k], sib).start()

    def finish(ins, outs, sems):
        for w, k, me, src_chip, peer, sib, c in each():
            other = outs[w].at[src_chip, 1 - c]
            _remote(other, other, sems[2].at[k], sems[3].at[k], sib).wait_recv()
        for w, k, me, src_chip, peer, sib, c in each():
            mine = outs[w].at[me, c]
            _remote(mine, mine, sems[0].at[k], sems[1].at[k], peer).wait_send()
            landed = outs[w].at[src_chip, c]
            _remote(landed, landed, sems[2].at[k], sems[3].at[k], sib).wait_send()

    return start, forward, finish


def _pair_exchange(grads):
    n_w = len(grads)
    start, finish = _pair_exchange_stages(n_w)

    def body(*refs):
        ins, outs, sems = refs[:n_w], refs[n_w:2 * n_w], refs[2 * n_w:]
        start(ins, outs, sems)
        finish(ins, outs, sems)

    return _comm_call("grad_pair_exchange", body, grads, _pair_exchange_shapes(grads), [N_CHIPS * n_w] * 2)


def _pair_exchange_shapes(grads):
    return [jax.ShapeDtypeStruct((N_CHIPS,) + g.shape[2:], g.dtype) for g in grads]


def _pair_exchange_stages(n_w):
    def each():
        x, y, c, _ = _place()
        for w in range(n_w):
            for q in range(N_CHIPS):
                yield w, q, N_CHIPS * w + q, c, (x, y, 1 - c)

    def start(ins, outs, sems):
        for w, q, k, c, sib in each():
            _remote(ins[w].at[q, 1 - c], outs[w].at[q], sems[0].at[k], sems[1].at[k], sib).start()

    def finish(ins, outs, sems):
        for w, q, k, c, sib in each():
            _remote(ins[w].at[q, 1 - c], outs[w].at[q], sems[0].at[k], sems[1].at[k], sib).wait()

    return start, finish


def _chip_exchange(parts):
    n_w = len(parts)

    start, finish = _chip_exchange_stages(n_w)

    def body(*refs):
        ins, outs, sems = refs[:n_w], refs[n_w:2 * n_w], refs[2 * n_w:]
        start(ins, outs, sems)
        finish(ins, outs, sems)

    out_shapes = [jax.ShapeDtypeStruct(t.shape, t.dtype) for t in parts]
    return _comm_call("grad_chip_exchange", body, parts, out_shapes, [3 * n_w, 3 * n_w])


def _chip_exchange_stages(n_w):
    def each():
        x, y, c, chips = _place()
        for w in range(n_w):
            for j, (cx, cy) in enumerate(chips):
                yield w, 3 * w + j, 2 * x + y, 2 * cx + cy, (cx, cy, c)

    def start(ins, outs, sems):
        for w, k, me, peer_chip, peer in each():
            _remote(ins[w].at[peer_chip], outs[w].at[me], sems[0].at[k], sems[1].at[k], peer).start()

    def finish(ins, outs, sems):
        for w, k, me, peer_chip, peer in each():
            got = outs[w].at[peer_chip]
            _remote(got, got, sems[0].at[k], sems[1].at[k], peer).wait_recv()
        for w, k, me, peer_chip, peer in each():
            _remote(ins[w].at[peer_chip], outs[w].at[me], sems[0].at[k], sems[1].at[k], peer).wait_send()

    return start, finish


def _pair_gather(halves):
    n_w = len(halves)

    def body(*refs):
        ins, outs = refs[:n_w], refs[n_w:2 * n_w]
        send, recv = refs[2 * n_w:]
        x, y, c, _ = _place()
        sib = (x, y, 1 - c)
        cps = []
        for w in range(n_w):
            cp = _remote(ins[w], outs[w], send.at[w], recv.at[w], sib)
            cp.start()
            cps.append(cp)
        for cp in cps:
            cp.wait()

    out_shapes = [jax.ShapeDtypeStruct(h.shape, h.dtype) for h in halves]
    return _comm_call("grad_pair_gather", body, halves, out_shapes, [n_w] * 2)


def _all_exchange_stages():
    def each():
        x, y, c, _ = _place()
        for k in range(1, 8):
            px, py, pc = x ^ ((k >> 2) & 1), y ^ ((k >> 1) & 1), c ^ (k & 1)
            yield k - 1, 4 * x + 2 * y + c, 4 * px + 2 * py + pc, (px, py, pc)

    def start(ins, outs, sems):
        for k, me, _, peer in each():
            _remote(ins[0], outs[0].at[me], sems[0].at[k], sems[1].at[k], peer).start()

    def finish(ins, outs, sems):
        for k, me, src, peer in each():
            got = outs[0].at[src]
            _remote(got, got, sems[0].at[k], sems[1].at[k], peer).wait_recv()
        for k, me, src, peer in each():
            _remote(ins[0], outs[0].at[me], sems[0].at[k], sems[1].at[k], peer).wait_send()

    return start, finish


def _row_tile(r):
    for t in (256, 128, 176, 64, 32, 16, 8):
        if r % t == 0:
            return t
    return r


P_C, P_CHIP, P_DEV = 2, 3, 4


def _cast_into_slot(w2, place):
    r, c = w2.shape
    t = _row_tile(r)
    return _ew("cast_shard", (r // t,), [(w2, (t, c), lambda i, pv: (i, 0))],
               [((N_CHIPS, r, c), BF16, (None, t, c), lambda i, pv: (pv[P_CHIP], i, 0))],
               lambda pids, a: ((a,), ()), place=place)[0]


def _pair_sum(mine, theirs, place):
    _, r, c = theirs.shape
    t = _row_tile(r)
    own = ((None, None, t, c), lambda q, i, pv: (q, pv[P_C], i, 0))
    blk = ((None, t, c), lambda q, i, pv: (q, i, 0))
    return _ew("grad_pair_sum", (N_CHIPS, r // t), [(mine, *own), (theirs, *blk)], [((N_CHIPS, r, c), BF16, *blk)],
               lambda pids, a, b: ((a + b,), ()), place=place)[0]


def _chip_sum(own, got, place):
    _, r, c = own.shape
    t = _row_tile(r)
    ins = []
    for q in range(N_CHIPS):
        ins.append((own, (None, t, c), (lambda i, pv, q=q: (q, i, 0))))
        ins.append((got, (None, t, c), (lambda i, pv, q=q: (jnp.where(pv[P_CHIP] == q, (q + 1) % N_CHIPS, q), i, 0))))

    def fn(pids, *tiles):
        me = pids[0][P_CHIP]
        tot = None
        for q in range(N_CHIPS):
            term = jnp.where(me == q, tiles[2 * q], tiles[2 * q + 1]).astype(F32)
            tot = term if tot is None else tot + term
        return (tot,), ()

    return _ew("grad_chip_sum", (r // t,), ins, [((r, c), F32, (t, c), lambda i, pv: (i, 0))], fn, place=place)[0]


def _adamw_tile(w, g, m, v):
    m = ADAM_B1 * m + (1.0 - ADAM_B1) * g
    v = ADAM_B2 * v + (1.0 - ADAM_B2) * (g * g)
    m_hat = m / (1.0 - ADAM_B1 ** ADAM_STEP)
    v_hat = v / (1.0 - ADAM_B2 ** ADAM_STEP)
    delta = -ADAM_LR * (m_hat / (jnp.sqrt(v_hat) + ADAM_EPS) + ADAM_WD * w)
    return delta, m, v


def _adamw(name, g2, w2, m2, v2):
    r, c = w2.shape
    t = _row_tile(r)
    blk, imap = _rows(t, c)

    def fn(pids, g, w, m, v):
        delta, nm, nv = _adamw_tile(w, g, m, v)
        return (g, delta, nm, nv), ()

    return _ew(name, (r // t,), [(a, blk, imap) for a in (g2, w2, m2, v2)], [((r, c), F32, blk, imap)] * 4, fn)


def _adamw_halves(name, mine, theirs, w2, m2, v2, place):
    r, c = w2.shape
    t = _row_tile(r // 2)
    n_t = (r // 2) // t
    half = ((t, c), lambda h, i, pv: (i, 0))
    whole = ((t, c), lambda h, i, pv: (h * n_t + i, 0))

    def fn(pids, ga, gb, w, m, v):
        g = jnp.where(pids[1] == pids[0][P_C], ga, gb)
        delta, nm, nv = _adamw_tile(w, g, m, v)
        return (g, delta, nm, nv), ()

    return _ew(name, (2, n_t), [(mine, *half), (theirs, *half), (w2, *whole), (m2, *whole), (v2, *whole)],
               [((r, c), F32, *whole)] * 4, fn, place=place)


def _device_sum(own, got, place):
    r, c = own.shape
    t = _row_tile(r)
    ins = [(own, (t, c), lambda i, pv: (i, 0))]
    for q in range(8):
        ins.append((got, (None, t, c), (lambda i, pv, q=q: (jnp.where(pv[P_DEV] == q, (q + 1) % 8, q), i, 0))))

    def fn(pids, mine, *parts):
        me = pids[0][P_DEV]
        tot = None
        for q in range(8):
            term = jnp.where(me == q, mine, parts[q])
            tot = term if tot is None else tot + term
        return (tot,), ()

    return _ew("small_device_sum", (r // t,), ins, [((r, c), F32, (t, c), lambda i, pv: (i, 0))], fn, place=place)[0]


def _pack(parts):
    flat = jnp.concatenate([a.reshape(-1) for a in parts])
    pad = (-flat.shape[0]) % (SUB * 128)
    return jnp.pad(flat, (0, pad)).reshape(-1, 128)


def _unpack(mat, shapes):
    flat = mat.reshape(-1)
    out, off = [], 0
    for shp in shapes:
        n = math.prod(shp)
        out.append(flat[off:off + n].reshape(shp))
        off += n
    return out


def kernel(x, p, positions, g_mix, w_in, a_re, a_im, log_dt, b_re, b_im, c_re, c_im, d_skip, w_attn_proj, w_glu_a, w_glu_b, w_out, g_ffn, w_ffn_gate, w_ffn_up, w_ffn_down, w_ple_gate, w_ple_proj, g_final, loss_target, m_g_mix, m_w_in, m_a_re, m_a_im, m_log_dt, m_b_re, m_b_im, m_c_re, m_c_im, m_d_skip, m_w_attn_proj, m_w_glu_a, m_w_glu_b, m_w_out, m_g_ffn, m_w_ffn_gate, m_w_ffn_up, m_w_ffn_down, m_w_ple_gate, m_w_ple_proj, m_g_final, v_g_mix, v_w_in, v_a_re, v_a_im, v_log_dt, v_b_re, v_b_im, v_c_re, v_c_im, v_d_skip, v_w_attn_proj, v_w_glu_a, v_w_glu_b, v_w_out, v_g_ffn, v_w_ffn_gate, v_w_ffn_up, v_w_ffn_down, v_w_ple_gate, v_w_ple_proj, v_g_final):
    given = dict(locals())
    big_w = {n: given[n] for n in BIG}
    w_mats = {n: big_w[n].reshape(big_w[n].shape[1:]) for n in BIG}

    ax, ay, ac = lax.axis_index("x"), lax.axis_index("y"), lax.axis_index("c")
    place = jnp.stack([ax, ay, ac, 2 * ax + ay, 4 * ax + 2 * ay + ac]).astype(jnp.int32)

    bufs = []
    for n in BIG:
        r, c = w_mats[n].shape
        bufs.append(_cast_into_slot(w_mats[n], place).reshape(N_CHIPS, 2, r // 2, c))
    w_in_all = _gather_weights(bufs[:1])[0].reshape((N_CHIPS,) + w_mats["w_in"].shape)

    sm = {
        "g_mix": g_mix.reshape(1, D_MODEL), "g_ffn": g_ffn.reshape(1, D_MODEL), "g_final": g_final.reshape(1, D_MODEL),
        "a_re": a_re[0], "a_im": a_im[0], "log_dt": log_dt[0], "b_re": b_re[0], "b_im": b_im[0], "c_re": c_re[0],
        "c_im": c_im[0], "d_skip": d_skip[0],
    }
    s = x.shape[1]
    grad_x, d_w_in, early_parts, early_got, vec, vec_got = _local_step(
        x[0], p[0, 0], positions[0], loss_target[0], sm, w_in_all, bufs[1:], place)

    r_in, c_in = w_mats["w_in"].shape
    g5_in = [d_w_in.reshape(N_CHIPS, 2, r_in // 2, c_in)]
    in_parts = [_pair_sum(g, t, place) for g, t in zip(g5_in, _pair_exchange(g5_in))]
    chip_parts = in_parts + list(early_parts)
    chip_got = list(_chip_exchange(in_parts)) + list(early_got)
    halves = [_chip_sum(own, got, place) for own, got in zip(chip_parts, chip_got)]
    other_halves = _pair_gather(halves)

    results = {}
    for n, mine, other in zip(BIG, halves, other_halves):
        r, c = w_mats[n].shape
        shp = big_w[n].shape
        outs = _adamw_halves("adamw_" + n, mine, other, w_mats[n], given["m_" + n].reshape(r, c),
                             given["v_" + n].reshape(r, c), place)
        results[n] = [o.reshape(shp) for o in outs]

    small_shapes = [given[n].shape for n in SMALL]
    tot = _device_sum(vec, vec_got, place)
    n_small = sum(math.prod(shp) for shp in small_shapes)
    loss = tot.reshape(-1)[n_small]
    w_s = _pack([given[n] for n in SMALL])
    m_s = _pack([given["m_" + n] for n in SMALL])
    v_s = _pack([given["v_" + n] for n in SMALL])
    rows_s = w_s.shape[0]
    g_s = tot.reshape(-1)[: rows_s * 128].reshape(rows_s, 128)
    outs_s = _adamw("adamw_small", g_s, w_s, m_s, v_s)
    for kind, mat in enumerate(outs_s):
        for n, arr in zip(SMALL, _unpack(mat, small_shapes)):
            results.setdefault(n, [None] * 4)[kind] = arr

    order = ("g_mix", "w_in", "a_re", "a_im", "log_dt", "b_re", "b_im", "c_re", "c_im", "d_skip", "w_attn_proj", "w_glu_a",
             "w_glu_b", "w_out", "g_ffn", "w_ffn_gate", "w_ffn_up", "w_ffn_down", "w_ple_gate", "w_ple_proj", "g_final")
    out = [loss, grad_x.reshape(1, s, D_MODEL)]
    for kind in range(4):
        out += [results[n][kind] for n in order]
    return tuple(out)
```

```python
import math

import jax
import jax.numpy as jnp
from jax import lax
from jax.experimental import pallas as pl
from jax.experimental.pallas import tpu as pltpu

F32 = jnp.float32
BF16 = jnp.bfloat16

D_MODEL = 1024
HEAD_DIM = 128
HEADS_PER_GROUP = 4
GROUP_WIDTH = HEADS_PER_GROUP * HEAD_DIM
GROUP_DILATIONS = (1, 4, 16)
N_GROUPS = len(GROUP_DILATIONS)
LSE_LANES = 32
LSE_WIDTH = HEADS_PER_GROUP * LSE_LANES
ATTN_BLOCK = 128
ROPE_DIM = 32
ROPE_HALF = 16
ROPE_THETA = 500000.0
SSM_WIDTH = 512
SSM_GROUPS = 32
SSM_GROUP = 16
SSM_STATE = 64
N_STATE = SSM_GROUPS * SSM_STATE
SSM_SUPER = 4
IN_WIDTH = 7168
COL_U = 4608
COL_GA = 5120
COL_GS = 6144
D_FF = 2816
N_CHIPS = 4
D_FF_Q = D_FF // N_CHIPS
PLE_DIM = 256
EPS = 1e-6
ADAM_LR = 0.001
ADAM_B1 = 0.9
ADAM_B2 = 0.999
ADAM_EPS = 1e-08
ADAM_WD = 0.01
ADAM_STEP = 10
NEG_BIG = -1e30
VMEM_LIMIT_BYTES = 56 * 1024 * 1024
MESH = pl.DeviceIdType.MESH

_DIMS = {
    "nn": (((1,), (0,)), ((), ())),
    "nt": (((1,), (1,)), ((), ())),
    "tn": (((0,), (0,)), ((), ())),
}


def _params(n_grid):
    return pltpu.CompilerParams(dimension_semantics=("arbitrary",) * n_grid, vmem_limit_bytes=VMEM_LIMIT_BYTES)


def _recip(d):
    r = pl.reciprocal(d, approx=True)
    return r * (2.0 - d * r)


def _sig(v):
    return _recip(1.0 + jnp.exp(-v))


def _dot(a, b, mode):
    return lax.dot_general(a, b, _DIMS[mode], preferred_element_type=F32)


def _mm(name, grid, pairs, mode, outs, epilogue=None, extras=(), acc_outs=(), acc_shape=None, j_outer=False,
        sum_pairs=True, resident_b=False, comm=None):
    gi, gj, gk = grid
    n_p, n_e, n_o, n_a = len(pairs), len(extras), len(outs), len(acc_outs)
    assert not n_a or gj == 1
    assert sum_pairs or gk == 1
    run_grid = (gj, gi, gk) if j_outer else grid
    c_ins = list(comm["ins"]) if comm else []
    c_outs = list(comm["outs"]) if comm else []
    c_sems = list(comm["sems"]) if comm else []
    n_ci, n_co, n_cs = len(c_ins), len(c_outs), len(c_sems)

    def order(imap):
        return (lambda j, i, k: imap(i, j, k)) if j_outer else imap

    shared_a = [pr[0] is None for pr in pairs]
    n_in = 2 * n_p - sum(shared_a)

    def body(*refs):
        pair_refs = list(refs[:n_in])
        extra_refs = refs[n_in: n_in + n_e]
        comm_in = refs[n_in + n_e: n_in + n_e + n_ci]
        at = n_in + n_e + n_ci
        out_refs = refs[at: at + n_o]
        sum_refs = refs[at + n_o: at + n_o + n_a]
        comm_out = refs[at + n_o + n_a: at + n_o + n_a + n_co]
        scratch_refs = refs[at + n_o + n_a + n_co:]
        i = pl.program_id(1 if j_outer else 0)
        k = pl.program_id(2)
        if comm:
            step = (pl.program_id(0) * run_grid[1] + pl.program_id(1)) * run_grid[2] + pl.program_id(2)
            sems = scratch_refs[len(scratch_refs) - n_cs:]
            for at_step, stage in comm["stages"]:
                @pl.when(step == at_step)
                def _(stage=stage):
                    stage(comm_in, comm_out, sems)
        part = None if sum_pairs else []
        a = None
        for t in range(n_p):
            if not shared_a[t]:
                a = pair_refs.pop(0)[...].astype(BF16)
            b = pair_refs.pop(0)[...].astype(BF16)
            d = _dot(a, b, mode)
            if sum_pairs:
                part = d if part is None else part + d
            else:
                part.append(d)

        def finish(acc):
            tiles, sums = epilogue(acc, *[e[...] for e in extra_refs]) if epilogue is not None else ((acc,), ())
            for o_ref, tile in zip(out_refs, tiles):
                o_ref[...] = tile.astype(o_ref.dtype)
            if n_a:
                @pl.when(i == 0)
                def _():
                    for s_ref in sum_refs:
                        s_ref[...] = jnp.zeros_like(s_ref)

                for s_ref, s in zip(sum_refs, sums):
                    s_ref[...] += s

        if gk == 1:
            finish(part)
        else:
            acc_ref = scratch_refs[0]

            @pl.when(k == 0)
            def _():
                acc_ref[...] = part

            @pl.when(k > 0)
            def _():
                acc_ref[...] += part

            @pl.when(k == gk - 1)
            def _():
                finish(acc_ref[...])

    in_specs, args = [], []
    for a, a_block, a_imap, b, b_block, b_imap in pairs:
        if a is not None:
            in_specs.append(pl.BlockSpec(a_block, order(a_imap)))
            args.append(a)
        if resident_b:
            in_specs.append(pl.BlockSpec(b_block, order(b_imap), pipeline_mode=pl.Buffered(1)))
        else:
            in_specs.append(pl.BlockSpec(b_block, order(b_imap)))
        args.append(b)
    for e, e_block, e_imap in extras:
        in_specs.append(pl.BlockSpec(e_block, order(e_imap)))
        args.append(e)
    first_comm_in = len(args)
    for c_in in c_ins:
        in_specs.append(pl.BlockSpec(memory_space=pl.ANY))
        args.append(c_in)
    out_shape = [jax.ShapeDtypeStruct(shape, dtype) for shape, dtype, _, _ in outs]
    out_specs = [pl.BlockSpec(block, order(imap)) for _, _, block, imap in outs]
    for shape, dtype in acc_outs:
        out_shape.append(jax.ShapeDtypeStruct(shape, dtype))
        out_specs.append(pl.BlockSpec(shape, lambda i, j, k: (0, 0)))
    first_comm_out = len(out_shape)
    for c_out in c_outs:
        out_shape.append(c_out)
        out_specs.append(pl.BlockSpec(memory_space=pl.ANY))
    aliases = {first_comm_in + n: first_comm_out + n for n in range(n_ci)} if comm and comm["aliased"] else {}
    scratch = [pltpu.VMEM(acc_shape, F32)] if gk > 1 else []
    scratch += [pltpu.SemaphoreType.DMA((n,)) for n in c_sems]
    return pl.pallas_call(
        body, name=name, grid=run_grid, in_specs=in_specs, out_specs=out_specs,
        out_shape=out_shape, scratch_shapes=scratch, compiler_params=_params(3), input_output_aliases=aliases,
    )(*args)


def _ew(name, grid, ins, outs, fn, acc_outs=(), place=None):
    n_i, n_o, n_a = len(ins), len(outs), len(acc_outs)
    ng = len(grid)
    n_s = 0 if place is None else 1

    def body(*refs):
        in_refs = refs[n_s: n_s + n_i]
        out_refs = refs[n_s + n_i: n_s + n_i + n_o]
        sum_refs = refs[n_s + n_i + n_o:]
        pids = tuple(pl.program_id(a) for a in range(ng))
        if n_s:
            pids = (refs[0],) + pids
        tiles, sums = fn(pids, *[r[...] for r in in_refs])
        for o_ref, tile in zip(out_refs, tiles):
            o_ref[...] = tile.astype(o_ref.dtype)
        if n_a:
            first = pids[0] == 0
            for p_ in pids[1:]:
                first = jnp.logical_and(first, p_ == 0)

            @pl.when(first)
            def _():
                for s_ref in sum_refs:
                    s_ref[...] = jnp.zeros_like(s_ref)

            for s_ref, s in zip(sum_refs, sums):
                s_ref[...] += s

    in_specs = [pl.BlockSpec(block, imap) for _, block, imap in ins]
    out_shape = [jax.ShapeDtypeStruct(shape, dtype) for shape, dtype, _, _ in outs]
    out_specs = [pl.BlockSpec(block, imap) for _, _, block, imap in outs]
    for shape, dtype in acc_outs:
        out_shape.append(jax.ShapeDtypeStruct(shape, dtype))
        out_specs.append(pl.BlockSpec(shape, lambda *_, nd=len(shape): (0,) * nd))
    arrays = [a for a, _, _ in ins]
    if n_s:
        assert not n_a
        spec = pltpu.PrefetchScalarGridSpec(num_scalar_prefetch=1, grid=grid, in_specs=in_specs, out_specs=out_specs)
        return pl.pallas_call(body, name=name, grid_spec=spec, out_shape=out_shape, compiler_params=_params(ng))(
            place, *arrays)
    return pl.pallas_call(
        body, name=name, grid=grid, in_specs=in_specs, out_specs=out_specs, out_shape=out_shape,
        compiler_params=_params(ng),
    )(*arrays)


def _rows(tm, width):
    return (tm, width), (lambda i: (i, 0))


def _rms_fwd_tile(h, g):
    r = lax.rsqrt(jnp.mean(h * h, axis=-1, keepdims=True) + EPS)
    return h * r * g


def _rms_bwd_tile(dn, h, g):
    r = lax.rsqrt(jnp.mean(h * h, axis=-1, keepdims=True) + EPS)
    hhat = h * r
    gy = dn * g
    dh = r * (gy - hhat * jnp.mean(gy * hhat, axis=-1, keepdims=True))
    dg = jnp.sum(dn * hhat, axis=0, keepdims=True)
    return dh, dg


def _rope_tables(pos_col, inv_row, tm):
    s = pos_col.shape[0]

    def fn(pids, pos, inv):
        ang = pos * inv
        lane = lax.broadcasted_iota(jnp.int32, ang.shape, 1)
        cs = jnp.where(lane < ROPE_DIM, jnp.cos(ang), 1.0)
        sn = jnp.sin(ang)
        s_lo = jnp.where(lane < ROPE_HALF, -sn, 0.0)
        s_hi = jnp.where(jnp.logical_and(lane >= ROPE_HALF, lane < ROPE_DIM), sn, 0.0)
        return (cs, s_lo, s_hi), ()

    blk, imap = _rows(tm, 128)
    return _ew(
        "rope_tables", (s // tm,),
        [(pos_col, (tm, 1), lambda i: (i, 0)), (inv_row, (1, 128), lambda i: (0, 0))],
        [((s, 128), F32, blk, imap)] * 3, fn,
    )


def _rope(xh, cs, s_lo, s_hi):
    return xh * cs + pltpu.roll(xh, HEAD_DIM - ROPE_HALF, 1) * s_lo + pltpu.roll(xh, ROPE_HALF, 1) * s_hi


def _rope_t(gh, cs, s_lo, s_hi):
    return gh * cs + pltpu.roll(gh * s_lo, ROPE_HALF, 1) + pltpu.roll(gh * s_hi, HEAD_DIM - ROPE_HALF, 1)


def _attn_geometry(length):
    nb = length // ATTN_BLOCK
    gq = min(4, nb)
    assert nb % gq == 0
    return nb, gq, gq * ATTN_BLOCK, nb // gq


def _band_masks():
    qi = lax.broadcasted_iota(jnp.int32, (ATTN_BLOCK, ATTN_BLOCK), 0)
    kj = lax.broadcasted_iota(jnp.int32, (ATTN_BLOCK, ATTN_BLOCK), 1)
    return kj <= qi, kj >= qi


def _band_mask_pair():
    qi = lax.broadcasted_iota(jnp.int32, (ATTN_BLOCK, 2 * ATTN_BLOCK), 0)
    cj = lax.broadcasted_iota(jnp.int32, (ATTN_BLOCK, 2 * ATTN_BLOCK), 1)
    in_cur = cj >= ATTN_BLOCK
    band = jnp.logical_or(jnp.logical_and(in_cur, cj - ATTN_BLOCK <= qi),
                          jnp.logical_and(cj < ATTN_BLOCK, cj >= qi))
    return band, in_cur


def _attn_fwd(qv, kv, vv, dil, cols3=(0, 0, 0)):
    length = qv.shape[0]
    nb, gq, rows, ni = _attn_geometry(length)

    def body(q_ref, kc_ref, kp_ref, vc_ref, vp_ref, o_ref, l_ref):
        i = pl.program_id(1)
        band, in_cur = _band_mask_pair()
        band_first = jnp.logical_and(band, jnp.logical_or(in_cur, i > 0))
        work = []
        for h in range(HEADS_PER_GROUP):
            cols = slice(h * HEAD_DIM, (h + 1) * HEAD_DIM)
            qh = q_ref[:, cols]
            k_all = jnp.concatenate([kp_ref[:, cols], kc_ref[:, cols]], axis=0)
            v_all = jnp.concatenate([vp_ref[:, cols], vc_ref[:, cols]], axis=0)
            for jj in range(gq):
                rws = slice(jj * ATTN_BLOCK, (jj + 1) * ATTN_BLOCK)
                two = slice(jj * ATTN_BLOCK, (jj + 2) * ATTN_BLOCK)
                work.append(dict(h=h, rws=rws, cols=cols, v=v_all[two], first=jj == 0, s=_dot(qh[rws], k_all[two], "nt")))
        for w in work:
            s = jnp.where(band_first if w["first"] else band, w["s"], NEG_BIG)
            m = jnp.max(s, axis=-1, keepdims=True)
            pexp = jnp.exp(s - m)
            w["den"] = jnp.sum(pexp, axis=-1, keepdims=True)
            w["p"] = pexp.astype(BF16)
            w["lse"] = m + jnp.log(w["den"])
        for w in work:
            o = _dot(w["p"], w["v"], "nn")
            o_ref[w["rws"], w["cols"]] = (o * (1.0 / w["den"])).astype(o_ref.dtype)
            l_ref[w["rws"], w["h"] * LSE_LANES:(w["h"] + 1) * LSE_LANES] = jnp.broadcast_to(w["lse"], (ATTN_BLOCK, LSE_LANES))

    def cur(c):
        return pl.BlockSpec((rows, GROUP_WIDTH), lambda r, i: (i, r + c))

    def prev(c):
        return pl.BlockSpec((ATTN_BLOCK, GROUP_WIDTH), lambda r, i: (jnp.maximum(i * gq - 1, 0), r + c))

    cq, ck, cv = cols3
    return pl.pallas_call(
        body, name=f"attn_fwd_d{dil}", grid=(dil, ni),
        in_specs=[cur(cq), cur(ck), prev(ck), cur(cv), prev(cv)],
        out_specs=[cur(0), pl.BlockSpec((rows, LSE_WIDTH), lambda r, i: (i, r))],
        out_shape=[jax.ShapeDtypeStruct((length, dil * GROUP_WIDTH), BF16),
                   jax.ShapeDtypeStruct((length, dil * LSE_WIDTH), F32)],
        compiler_params=_params(2),
    )(qv, kv, kv, vv, vv)


def _attn_bwd(qv, kv, vv, dov, ov, lv, dil, cols3=(0, 0, 0)):
    length = qv.shape[0]
    nb, gq, rows, ni = _attn_geometry(length)
    out_shape = (length, dil * GROUP_WIDTH)

    def body(qc_ref, qn_ref, kc_ref, kp_ref, vc_ref, vp_ref, doc_ref, don_ref, oc_ref, on_ref, lc_ref, ln_ref,
             dq_ref, dk_ref, dv_ref):
        i = pl.program_id(1)
        _, mask_p = _band_masks()
        band, in_cur = _band_mask_pair()
        band_first = jnp.logical_and(band, jnp.logical_or(in_cur, i > 0))
        has_next = i < ni - 1

        last = slice(gq * ATTN_BLOCK, (gq + 1) * ATTN_BLOCK)
        mask_next = jnp.logical_and(mask_p, has_next)

        def rows_of(jj):
            return slice(jj * ATTN_BLOCK, (jj + 1) * ATTN_BLOCK)

        def keys_of(jj):
            return slice(jj * ATTN_BLOCK, (jj + 2) * ATTN_BLOCK)

        heads = []
        for h in range(HEADS_PER_GROUP):
            cols = slice(h * HEAD_DIM, (h + 1) * HEAD_DIM)
            hd = dict(
                cols=cols, q_c=qc_ref[:, cols], q_n=qn_ref[:, cols],
                k_all=jnp.concatenate([kp_ref[:, cols], kc_ref[:, cols]], axis=0),
                v_all=jnp.concatenate([vp_ref[:, cols], vc_ref[:, cols]], axis=0),
                do_c=doc_ref[:, cols], do_n=don_ref[:, cols],
                l_c=lc_ref[:, h * LSE_LANES:h * LSE_LANES + 1], l_n=ln_ref[:, h * LSE_LANES:h * LSE_LANES + 1],
            )
            hd["dl_c"] = jnp.sum(hd["do_c"].astype(F32) * oc_ref[:, cols].astype(F32), axis=-1, keepdims=True)
            hd["dl_n"] = jnp.sum(hd["do_n"].astype(F32) * on_ref[:, cols].astype(F32), axis=-1, keepdims=True)
            hd["s"] = [_dot(hd["q_c"][rows_of(jj)], hd["k_all"][keys_of(jj)], "nt") for jj in range(gq)]
            hd["dp"] = [_dot(hd["do_c"][rows_of(jj)], hd["v_all"][keys_of(jj)], "nt") for jj in range(gq)]
            hd["s"].append(_dot(hd["q_n"], hd["k_all"][last], "nt"))
            hd["dp"].append(_dot(hd["do_n"], hd["v_all"][last], "nt"))
            heads.append(hd)
        for hd in heads:
            hd["p"], hd["ds"] = [], []
            for jj in range(gq + 1):
                if jj < gq:
                    mask, l_col, delta = (band_first if jj == 0 else band), hd["l_c"][rows_of(jj)], hd["dl_c"][rows_of(jj)]
                else:
                    mask, l_col, delta = mask_next, hd["l_n"], hd["dl_n"]
                p = jnp.where(mask, jnp.exp(hd["s"][jj] - l_col), 0.0)
                hd["p"].append(p.astype(BF16))
                hd["ds"].append((p * (hd["dp"][jj] - delta)).astype(BF16))
        for hd in heads:
            cols = hd["cols"]
            dk_blocks, dv_blocks = [None] * (gq + 1), [None] * (gq + 1)

            def add(lst, idx, val):
                lst[idx] = val if lst[idx] is None else lst[idx] + val

            for jj in range(gq):
                qb, dob = hd["q_c"][rows_of(jj)], hd["do_c"][rows_of(jj)]
                dq_ref[rows_of(jj), cols] = _dot(hd["ds"][jj], hd["k_all"][keys_of(jj)], "nn").astype(dq_ref.dtype)
                dk2 = _dot(hd["ds"][jj], qb, "tn")
                dv2 = _dot(hd["p"][jj], dob, "tn")
                add(dk_blocks, jj, dk2[:ATTN_BLOCK])
                add(dk_blocks, jj + 1, dk2[ATTN_BLOCK:])
                add(dv_blocks, jj, dv2[:ATTN_BLOCK])
                add(dv_blocks, jj + 1, dv2[ATTN_BLOCK:])
            add(dk_blocks, gq, _dot(hd["ds"][gq], hd["q_n"], "tn"))
            add(dv_blocks, gq, _dot(hd["p"][gq], hd["do_n"], "tn"))
            for jj in range(gq):
                dk_ref[rows_of(jj), cols] = dk_blocks[jj + 1].astype(dk_ref.dtype)
                dv_ref[rows_of(jj), cols] = dv_blocks[jj + 1].astype(dv_ref.dtype)

    def cur(c):
        return pl.BlockSpec((rows, GROUP_WIDTH), lambda r, i: (i, r + c))

    def prev(c):
        return pl.BlockSpec((ATTN_BLOCK, GROUP_WIDTH), lambda r, i: (jnp.maximum(i * gq - 1, 0), r + c))

    def nxt(c):
        return pl.BlockSpec((ATTN_BLOCK, GROUP_WIDTH), lambda r, i: (jnp.minimum((i + 1) * gq, nb - 1), r + c))

    cq, ck, cv = cols3
    lse_cur = pl.BlockSpec((rows, LSE_WIDTH), lambda r, i: (i, r))
    lse_next = pl.BlockSpec((ATTN_BLOCK, LSE_WIDTH), lambda r, i: (jnp.minimum((i + 1) * gq, nb - 1), r))
    return pl.pallas_call(
        body, name=f"attn_bwd_d{dil}", grid=(dil, ni),
        in_specs=[cur(cq), nxt(cq), cur(ck), prev(ck), cur(cv), prev(cv), cur(0), nxt(0), cur(0), nxt(0), lse_cur, lse_next],
        out_specs=[cur(0), cur(0), cur(0)],
        out_shape=[jax.ShapeDtypeStruct(out_shape, BF16)] * 3,
        compiler_params=_params(2),
    )(qv, qv, kv, kv, vv, vv, dov, dov, ov, ov, lv, lv)


DILATED = tuple((g, d) for g, d in enumerate(GROUP_DILATIONS) if d > 1)


def _spread(scr, slot, tile, out_ref, dil, col, width=GROUP_WIDTH):
    tm = tile.shape[0]
    buf = scr.at[slot]
    buf[...] = tile
    for r in range(dil):
        c0 = r * width + col
        out_ref[:, c0:c0 + HEAD_DIM] = buf[pl.ds(r, tm // dil, stride=dil), :].astype(out_ref.dtype)


def _collect(scr, slot, in_ref, dil, col, width=GROUP_WIDTH):
    tm = scr.shape[1]
    buf = scr.at[slot]
    for r in range(dil):
        c0 = r * width + col
        buf[pl.ds(r, tm // dil, stride=dil), :] = in_ref[:, c0:c0 + HEAD_DIM].astype(F32)
    return buf[...]


def _view_spec(tm, dil, width=GROUP_WIDTH):
    return pl.BlockSpec((tm // dil, dil * width), lambda i: (i, 0))


def _view_shape(s, dil, dtype, width=GROUP_WIDTH):
    return jax.ShapeDtypeStruct((s // dil, dil * width), dtype)


def _qkv_layout(z, tabs, tm):
    s = z.shape[0]
    scale = 1.0 / math.sqrt(HEAD_DIM)
    qkv_width = 3 * N_GROUPS * GROUP_WIDTH

    def body(z_ref, cs_ref, lo_ref, hi_ref, qk0_ref, *rest):
        views, scr = rest[:-1], rest[-1]
        tabs_ = (cs_ref[...], lo_ref[...], hi_ref[...])
        for part in range(3):
            for g, dil in enumerate(GROUP_DILATIONS):
                if part == 2 and dil == 1:
                    continue
                for h in range(HEADS_PER_GROUP):
                    col = part * N_GROUPS * GROUP_WIDTH + g * GROUP_WIDTH + h * HEAD_DIM
                    t = z_ref[:, col:col + HEAD_DIM].astype(F32)
                    if part < 2:
                        t = _rope(t, *tabs_)
                    if part == 0:
                        t = t * scale
                    if dil == 1:
                        c0 = part * GROUP_WIDTH + h * HEAD_DIM
                        qk0_ref[:, c0:c0 + HEAD_DIM] = t.astype(BF16)
                    else:
                        out = views[3 * [gg for gg, _ in DILATED].index(g) + part]
                        _spread(scr, h, t, out, dil, h * HEAD_DIM)

    row = lambda i: (i, 0)
    out_shape = [jax.ShapeDtypeStruct((s, 2 * GROUP_WIDTH), BF16)]
    out_specs = [pl.BlockSpec((tm, 2 * GROUP_WIDTH), row)]
    for _, dil in DILATED:
        out_shape += [_view_shape(s, dil, BF16)] * 3
        out_specs += [_view_spec(tm, dil)] * 3
    res = pl.pallas_call(
        body, name="qkv_layout", grid=(s // tm,),
        in_specs=[pl.BlockSpec((tm, qkv_width), row)] + [pl.BlockSpec((tm, HEAD_DIM), row)] * 3,
        out_specs=out_specs, out_shape=out_shape,
        scratch_shapes=[pltpu.VMEM((HEADS_PER_GROUP, tm, HEAD_DIM), F32)], compiler_params=_params(1),
    )(z, *tabs)
    return res[0], [tuple(res[1 + 3 * n:4 + 3 * n]) for n in range(len(DILATED))]


def _attn_merge(o0, l0, dilated, tm):
    s = o0.shape[0]
    n_d = len(DILATED)

    def body(*refs):
        o0_ref, l0_ref = refs[:2]
        in_views = refs[2:2 + 2 * n_d]
        attn_ref, lse_ref = refs[2 + 2 * n_d:4 + 2 * n_d]
        out_views = refs[4 + 2 * n_d:4 + 4 * n_d]
        scr = refs[-1]
        l_rows = [l0_ref[...]] + [_collect(scr, n, in_views[2 * n + 1], dil, 0, LSE_WIDTH) for n, (_, dil) in enumerate(DILATED)]
        lse_heads = []
        for h in range(HEADS_PER_GROUP):
            cols = slice(h * HEAD_DIM, (h + 1) * HEAD_DIM)
            os_ = [o0_ref[:, cols].astype(F32)]
            for n, (_, dil) in enumerate(DILATED):
                os_.append(_collect(scr, n_d + n, in_views[2 * n], dil, h * HEAD_DIM))
            ls_ = [lr[:, h * LSE_LANES:h * LSE_LANES + 1] for lr in l_rows]
            m = ls_[0]
            for l_ in ls_[1:]:
                m = jnp.maximum(m, l_)
            es = [jnp.exp(l_ - m) for l_ in ls_]
            den = es[0]
            num = es[0] * os_[0]
            for e, o in zip(es[1:], os_[1:]):
                den = den + e
                num = num + e * o
            attn = num * (1.0 / den)
            lse_heads.append(jnp.broadcast_to(m + jnp.log(den), (tm, LSE_LANES)))
            attn_ref[:, cols] = attn.astype(BF16)
            for n, (_, dil) in enumerate(DILATED):
                _spread(scr, 2 * n_d, attn, out_views[2 * n], dil, h * HEAD_DIM)
        lse = jnp.concatenate(lse_heads, axis=1)
        lse_ref[...] = lse
        for n, (_, dil) in enumerate(DILATED):
            _spread(scr, 2 * n_d, lse, out_views[2 * n + 1], dil, 0, LSE_WIDTH)

    row = lambda i: (i, 0)
    nat = pl.BlockSpec((tm, GROUP_WIDTH), row)
    nat_l = pl.BlockSpec((tm, LSE_WIDTH), row)
    in_specs = [nat, nat_l]
    args = [o0, l0]
    out_specs = [nat, nat_l]
    out_shape = [jax.ShapeDtypeStruct((s, GROUP_WIDTH), BF16), jax.ShapeDtypeStruct((s, LSE_WIDTH), F32)]
    for (_, dil), (ov, lv) in zip(DILATED, dilated):
        in_specs += [_view_spec(tm, dil), _view_spec(tm, dil, LSE_WIDTH)]
        args += [ov, lv]
        out_specs += [_view_spec(tm, dil), _view_spec(tm, dil, LSE_WIDTH)]
        out_shape += [_view_shape(s, dil, BF16), _view_shape(s, dil, F32, LSE_WIDTH)]
    res = pl.pallas_call(
        body, name="attn_merge", grid=(s // tm,), in_specs=in_specs, out_specs=out_specs, out_shape=out_shape,
        scratch_shapes=[pltpu.VMEM((2 * n_d + 1, tm, HEAD_DIM), F32)], compiler_params=_params(1),
    )(*args)
    return res[0], res[1], [tuple(res[2 + 2 * n:4 + 2 * n]) for n in range(n_d)]


def _to_views(a, tm):
    s = a.shape[0]

    def body(a_ref, *rest):
        outs, scr = rest[:-1], rest[-1]
        for h in range(HEADS_PER_GROUP):
            t = a_ref[:, h * HEAD_DIM:(h + 1) * HEAD_DIM].astype(F32)
            for n, (_, dil) in enumerate(DILATED):
                _spread(scr, n, t, outs[n], dil, h * HEAD_DIM)

    return pl.pallas_call(
        body, name="to_views", grid=(s // tm,), in_specs=[pl.BlockSpec((tm, GROUP_WIDTH), lambda i: (i, 0))],
        out_specs=[_view_spec(tm, dil) for _, dil in DILATED], out_shape=[_view_shape(s, dil, BF16) for _, dil in DILATED],
        scratch_shapes=[pltpu.VMEM((len(DILATED), tm, HEAD_DIM), F32)], compiler_params=_params(1),
    )(a)


def _dz_layout(grads, du, dga, dgs, tabs, tm, comm=None):
    s = du.shape[0]
    scale = 1.0 / math.sqrt(HEAD_DIM)
    n_steps = s // tm
    c_ins = list(comm["ins"]) if comm else []
    c_outs = list(comm["outs"]) if comm else []
    c_sems = list(comm["sems"]) if comm else []
    n_fixed = 3 * N_GROUPS + 6

    def body(*refs):
        g_refs = refs[:3 * N_GROUPS]
        du_ref, dga_ref, dgs_ref, cs_ref, lo_ref, hi_ref = refs[3 * N_GROUPS:n_fixed]
        comm_in = refs[n_fixed:n_fixed + len(c_ins)]
        dz_ref = refs[n_fixed + len(c_ins)]
        comm_out = refs[n_fixed + len(c_ins) + 1:n_fixed + len(c_ins) + 1 + len(c_outs)]
        scr = refs[n_fixed + len(c_ins) + 1 + len(c_outs)]
        sems = refs[n_fixed + len(c_ins) + 2 + len(c_outs):]
        if comm:
            @pl.when(pl.program_id(0) == 0)
            def _():
                comm["start"](comm_in, comm_out, sems)

            @pl.when(pl.program_id(0) == n_steps - 1)
            def _():
                comm["finish"](comm_in, comm_out, sems)

        tabs_ = (cs_ref[...], lo_ref[...], hi_ref[...])
        for part in range(3):
            for g, dil in enumerate(GROUP_DILATIONS):
                src = g_refs[3 * g + part]
                for h in range(HEADS_PER_GROUP):
                    if dil == 1:
                        t = src[:, h * HEAD_DIM:(h + 1) * HEAD_DIM].astype(F32)
                    else:
                        t = _collect(scr, h, src, dil, h * HEAD_DIM)
                    if part < 2:
                        t = _rope_t(t, *tabs_)
                    if part == 0:
                        t = t * scale
                    col = part * N_GROUPS * GROUP_WIDTH + g * GROUP_WIDTH + h * HEAD_DIM
                    dz_ref[:, col:col + HEAD_DIM] = t.astype(BF16)
        dz_ref[:, COL_U:COL_GA] = du_ref[...]
        dz_ref[:, COL_GA:COL_GS] = dga_ref[...]
        dz_ref[:, COL_GS:IN_WIDTH] = dgs_ref[...]

    row = lambda i: (i, 0)
    in_specs, args = [], []
    for (g, dil), trio in zip(enumerate(GROUP_DILATIONS), grads):
        in_specs += [pl.BlockSpec((tm, GROUP_WIDTH), row) if dil == 1 else _view_spec(tm, dil)] * 3
        args += list(trio)
    in_specs += [pl.BlockSpec((tm, SSM_WIDTH), row), pl.BlockSpec((tm, D_MODEL), row), pl.BlockSpec((tm, D_MODEL), row)]
    in_specs += [pl.BlockSpec((tm, HEAD_DIM), row)] * 3
    in_specs += [pl.BlockSpec(memory_space=pl.ANY)] * len(c_ins)
    res = pl.pallas_call(
        body, name="dz_layout", grid=(n_steps,), in_specs=in_specs,
        out_specs=[pl.BlockSpec((tm, IN_WIDTH), row)] + [pl.BlockSpec(memory_space=pl.ANY)] * len(c_outs),
        out_shape=[jax.ShapeDtypeStruct((s, IN_WIDTH), BF16)] + c_outs,
        scratch_shapes=[pltpu.VMEM((HEADS_PER_GROUP, tm, HEAD_DIM), F32)] + [pltpu.SemaphoreType.DMA((n,)) for n in c_sems],
        compiler_params=_params(1),
    )(*args, du, dga, dgs, *tabs, *c_ins)
    return res[0], list(res[1:])


def _discretise(a_re, a_im, log_dt, bt_re, bt_im):
    dt = jnp.exp(log_dt)
    mag = jnp.exp(a_re * dt)
    bar_re = mag * jnp.cos(a_im * dt)
    bar_im = mag * jnp.sin(a_im * dt)
    nr = bar_re - 1.0
    ni = bar_im
    den = a_re * a_re + a_im * a_im
    z_re = (nr * a_re + ni * a_im) / den
    z_im = (ni * a_re - nr * a_im) / den
    bb_re = z_re[:, None, :] * bt_re - z_im[:, None, :] * bt_im
    bb_im = z_re[:, None, :] * bt_im + z_im[:, None, :] * bt_re
    return bar_re, bar_im, bb_re, bb_im


def _ssm_prep(a_re, a_im, log_dt, bt_re, bt_im):
    def body(ar, ai, ld, br, bi, o_lr, o_li, o_br, o_bi):
        lr, li, bbr, bbi = _discretise(ar[...], ai[...], ld[...], br[...], bi[...])
        o_lr[...] = lr
        o_li[...] = li
        o_br[...] = bbr
        o_bi[...] = bbi

    sm = jax.ShapeDtypeStruct((SSM_GROUPS, SSM_STATE), F32)
    bg = jax.ShapeDtypeStruct((SSM_GROUPS, SSM_GROUP, SSM_STATE), F32)
    return pl.pallas_call(body, name="ssm_prep", out_shape=[sm, sm, bg, bg])(a_re, a_im, log_dt, bt_re, bt_im)


def _ssm_param_bwd(a_re, a_im, log_dt, bt_re, bt_im, d_lr, d_li, d_bbr, d_bbi):
    def body(ar, ai, ld, br, bi, g_lr, g_li, g_br, g_bi, o_ar, o_ai, o_ld, o_br, o_bi):
        _, vjp = jax.vjp(_discretise, ar[...], ai[...], ld[...], br[...], bi[...])
        d_ar, d_ai, d_ld, d_br, d_bi = vjp((g_lr[...], g_li[...], g_br[...], g_bi[...]))
        o_ar[...] = d_ar
        o_ai[...] = d_ai
        o_ld[...] = d_ld
        o_br[...] = d_br
        o_bi[...] = d_bi

    sm = jax.ShapeDtypeStruct((SSM_GROUPS, SSM_STATE), F32)
    col = jax.ShapeDtypeStruct((SSM_GROUPS, 1), F32)
    bg = jax.ShapeDtypeStruct((SSM_GROUPS, SSM_GROUP, SSM_STATE), F32)
    return pl.pallas_call(body, name="ssm_param_bwd", out_shape=[sm, sm, col, bg, bg])(
        a_re, a_im, log_dt, bt_re, bt_im, d_lr, d_li, d_bbr, d_bbi)


def _block_diag(t, rows_per, cols_per):
    t4 = t.reshape(SSM_SUPER, 8, rows_per, cols_per)
    eye = jnp.eye(8, dtype=t.dtype)
    return jnp.einsum("bgrc,gh->bgrhc", t4, eye).reshape(SSM_SUPER, 8 * rows_per, 8 * cols_per)


def _block_diag_t(dense, rows_per, cols_per):
    t = dense.reshape(SSM_SUPER, 8, rows_per, 8, cols_per)
    eye = jnp.eye(8, dtype=dense.dtype)
    return jnp.einsum("bgrhc,gh->bgrc", t, eye).reshape(SSM_GROUPS, rows_per, cols_per)


def _gelu(v):
    c = math.sqrt(2.0 / math.pi)
    return 0.5 * v * (1.0 + jnp.tanh(c * (v + 0.044715 * v * v * v)))


def _gelu_grad(v):
    c = math.sqrt(2.0 / math.pi)
    t = jnp.tanh(c * (v + 0.044715 * v * v * v))
    return 0.5 * (1.0 + t) + 0.5 * v * (1.0 - t * t) * c * (1.0 + 3.0 * 0.044715 * v * v)


SUB = 8


SCAN_STEPS = (1, 2, 4)
N_SCAN_TABLES = 2 + 2 * len(SCAN_STEPS)


def _scan_tables(tab_ref, lam_re, lam_im, reverse, conj):
    lr = lam_re
    li = -lam_im if conj else lam_im
    powers = [(lr, li)]
    for _ in range(SUB - 1):
        pr, pi = powers[-1]
        powers.append((pr * lr - pi * li, pr * li + pi * lr))
    row = lax.broadcasted_iota(jnp.int32, (SUB, N_STATE), 0)
    if reverse:
        row = SUB - 1 - row
    wide = lambda v: jnp.broadcast_to(v, (SUB, N_STATE))
    p_re = jnp.zeros((SUB, N_STATE), F32)
    p_im = jnp.zeros((SUB, N_STATE), F32)
    for j in range(SUB):
        p_re = jnp.where(row == j, wide(powers[j][0]), p_re)
        p_im = jnp.where(row == j, wide(powers[j][1]), p_im)
    tab_ref[0] = p_re
    tab_ref[1] = p_im
    for idx, k in enumerate(SCAN_STEPS):
        tab_ref[2 + 2 * idx] = jnp.where(row >= k, wide(powers[k - 1][0]), 0.0)
        tab_ref[3 + 2 * idx] = jnp.where(row >= k, wide(powers[k - 1][1]), 0.0)


def _scan_rows(g_re_ref, g_im_ref, tab_ref, carry, n_rows, reverse):
    last = 0 if reverse else SUB - 1

    def tile_step(tt, state):
        cr, ci = state
        t8 = (n_rows // SUB - 1 - tt) if reverse else tt
        start = pl.multiple_of(t8 * SUB, SUB)
        xr = g_re_ref[pl.ds(start, SUB), :]
        xi = g_im_ref[pl.ds(start, SUB), :]
        for idx, k in enumerate(SCAN_STEPS):
            mr = tab_ref[2 + 2 * idx]
            mi = tab_ref[3 + 2 * idx]
            shift = SUB - k if reverse else k
            sr = pltpu.roll(xr, shift, 0)
            si = pltpu.roll(xi, shift, 0)
            xr, xi = xr + (mr * sr - mi * si), xi + (mr * si + mi * sr)
        pr = tab_ref[0]
        pi = tab_ref[1]
        xr, xi = xr + (pr * cr - pi * ci), xi + (pr * ci + pi * cr)
        g_re_ref[pl.ds(start, SUB), :] = xr
        g_im_ref[pl.ds(start, SUB), :] = xi
        return (jnp.broadcast_to(xr[last:last + 1, :], (SUB, N_STATE)),
                jnp.broadcast_to(xi[last:last + 1, :], (SUB, N_STATE)))

    return lax.fori_loop(0, n_rows // SUB, tile_step, carry)


def _ssm_fwd(z, b_re, b_im, c_re, c_im, lam_re, lam_im, d_skip, chunk):
    s = z.shape[0]

    def body(u_ref, bre, bim, cre, cim, lre, lim, dsk, hre_ref, him_ref, ys_ref, yg_ref, car_re, car_im, tabs):
        i = pl.program_id(0)

        @pl.when(i == 0)
        def _():
            car_re[...] = jnp.zeros_like(car_re)
            car_im[...] = jnp.zeros_like(car_im)
            _scan_tables(tabs, lre[...], lim[...], False, False)

        u = u_ref[...]
        for b in range(SSM_SUPER):
            ub = u[:, b * 128:(b + 1) * 128]
            st = slice(b * 512, (b + 1) * 512)
            hre_ref[:, st] = _dot(ub, bre[b], "nn")
            him_ref[:, st] = _dot(ub, bim[b], "nn")
        sr, si = _scan_rows(hre_ref, him_ref, tabs, (car_re[...], car_im[...]), chunk, False)
        car_re[...] = sr
        car_im[...] = si
        uf = u.astype(F32)
        for b in range(SSM_SUPER):
            st = slice(b * 512, (b + 1) * 512)
            ch = slice(b * 128, (b + 1) * 128)
            y = _dot(hre_ref[:, st].astype(BF16), cre[b], "nn") - _dot(him_ref[:, st].astype(BF16), cim[b], "nn")
            y = y + dsk[:, ch] * uf[:, ch]
            ys_ref[:, ch] = y
            yg_ref[:, ch] = _gelu(y).astype(BF16)

    full3 = lambda i: (0, 0, 0)
    full2 = lambda i: (0, 0)
    row = lambda i: (i, 0)
    u_col = COL_U // SSM_WIDTH
    return pl.pallas_call(
        body, name="ssm_fwd", grid=(s // chunk,),
        in_specs=[pl.BlockSpec((chunk, SSM_WIDTH), lambda i: (i, u_col)),
                  pl.BlockSpec((SSM_SUPER, 128, 512), full3), pl.BlockSpec((SSM_SUPER, 128, 512), full3),
                  pl.BlockSpec((SSM_SUPER, 512, 128), full3), pl.BlockSpec((SSM_SUPER, 512, 128), full3),
                  pl.BlockSpec((1, N_STATE), full2), pl.BlockSpec((1, N_STATE), full2), pl.BlockSpec((1, SSM_WIDTH), full2)],
        out_specs=[pl.BlockSpec((chunk, N_STATE), row), pl.BlockSpec((chunk, N_STATE), row),
                   pl.BlockSpec((chunk, SSM_WIDTH), row), pl.BlockSpec((chunk, SSM_WIDTH), row)],
        out_shape=[jax.ShapeDtypeStruct((s, N_STATE), F32), jax.ShapeDtypeStruct((s, N_STATE), F32),
                   jax.ShapeDtypeStruct((s, SSM_WIDTH), F32), jax.ShapeDtypeStruct((s, SSM_WIDTH), BF16)],
        scratch_shapes=[pltpu.VMEM((SUB, N_STATE), F32), pltpu.VMEM((SUB, N_STATE), F32),
                        pltpu.VMEM((N_SCAN_TABLES, SUB, N_STATE), F32)],
        compiler_params=_params(1),
    )(z, b_re, b_im, c_re, c_im, lam_re, lam_im, d_skip)


def _ssm_bwd(dys, z, h_re, h_im, b_re, b_im, c_re, c_im, lam_re, lam_im, d_skip, chunk):
    s = z.shape[0]
    n_chunks = s // chunk

    def body(dy_ref, u_ref, hre_ref, him_ref, hpr_ref, hpi_ref, bre, bim, cre, cim, lre, lim, dsk,
             du_ref, dlr_ref, dli_ref, dbr_ref, dbi_ref, dcr_ref, dci_ref, dd_ref, are, aim, car_re, car_im, tabs):
        i = pl.program_id(0)
        n = n_chunks - 1 - i

        @pl.when(i == 0)
        def _():
            car_re[...] = jnp.zeros_like(car_re)
            car_im[...] = jnp.zeros_like(car_im)
            _scan_tables(tabs, lre[...], lim[...], True, True)
            for r in (dlr_ref, dli_ref, dbr_ref, dbi_ref, dcr_ref, dci_ref, dd_ref):
                r[...] = jnp.zeros_like(r)

        dy = dy_ref[...]
        dyb = dy.astype(BF16)
        u = u_ref[...]
        for b in range(SSM_SUPER):
            ch = slice(b * 128, (b + 1) * 128)
            st = slice(b * 512, (b + 1) * 512)
            are[:, st] = _dot(dyb[:, ch], cre[b], "nt")
            aim[:, st] = -_dot(dyb[:, ch], cim[b], "nt")
        sr, si = _scan_rows(are, aim, tabs, (car_re[...], car_im[...]), chunk, True)
        car_re[...] = sr
        car_im[...] = si
        row_id = lax.broadcasted_iota(jnp.int32, (chunk, N_STATE), 0)
        top_scale = jnp.where(n > 0, 1.0, 0.0)
        h_r = hre_ref[...]
        h_i = him_ref[...]
        hp_r = jnp.where(row_id == 0, hpr_ref[SUB - 1:SUB, :] * top_scale, pltpu.roll(h_r, 1, 0))
        hp_i = jnp.where(row_id == 0, hpi_ref[SUB - 1:SUB, :] * top_scale, pltpu.roll(h_i, 1, 0))
        a_r = are[...]
        a_i = aim[...]
        dlr_ref[...] += jnp.sum(a_r * hp_r + a_i * hp_i, axis=0, keepdims=True)
        dli_ref[...] += jnp.sum(a_i * hp_r - a_r * hp_i, axis=0, keepdims=True)
        dd_ref[...] += jnp.sum(dy * u.astype(F32), axis=0, keepdims=True)
        a_rb = a_r.astype(BF16)
        a_ib = a_i.astype(BF16)
        h_rb = h_r.astype(BF16)
        h_ib = h_i.astype(BF16)
        for b in range(SSM_SUPER):
            ch = slice(b * 128, (b + 1) * 128)
            st = slice(b * 512, (b + 1) * 512)
            dbr_ref[b] += _dot(u[:, ch], a_rb[:, st], "tn")
            dbi_ref[b] += _dot(u[:, ch], a_ib[:, st], "tn")
            dcr_ref[b] += _dot(h_rb[:, st], dyb[:, ch], "tn")
            dci_ref[b] += -_dot(h_ib[:, st], dyb[:, ch], "tn")
            du = _dot(a_rb[:, st], bre[b], "nt") + _dot(a_ib[:, st], bim[b], "nt") + dsk[:, ch] * dy[:, ch]
            du_ref[:, ch] = du.astype(du_ref.dtype)

    full3 = lambda i: (0, 0, 0)
    full2 = lambda i: (0, 0)
    rev = lambda i: (n_chunks - 1 - i, 0)
    above = lambda i: (jnp.maximum((n_chunks - 1 - i) * (chunk // SUB) - 1, 0), 0)
    u_col = COL_U // SSM_WIDTH
    b_spec = pl.BlockSpec((SSM_SUPER, 128, 512), full3)
    c_spec = pl.BlockSpec((SSM_SUPER, 512, 128), full3)
    vec = pl.BlockSpec((1, N_STATE), full2)
    return pl.pallas_call(
        body, name="ssm_bwd", grid=(n_chunks,),
        in_specs=[pl.BlockSpec((chunk, SSM_WIDTH), rev),
                  pl.BlockSpec((chunk, SSM_WIDTH), lambda i: (n_chunks - 1 - i, u_col)),
                  pl.BlockSpec((chunk, N_STATE), rev), pl.BlockSpec((chunk, N_STATE), rev),
                  pl.BlockSpec((SUB, N_STATE), above), pl.BlockSpec((SUB, N_STATE), above),
                  b_spec, b_spec, c_spec, c_spec, vec, vec, pl.BlockSpec((1, SSM_WIDTH), full2)],
        out_specs=[pl.BlockSpec((chunk, SSM_WIDTH), rev), vec, vec, b_spec, b_spec, c_spec, c_spec,
                   pl.BlockSpec((1, SSM_WIDTH), full2)],
        out_shape=[jax.ShapeDtypeStruct((s, SSM_WIDTH), BF16),
                   jax.ShapeDtypeStruct((1, N_STATE), F32), jax.ShapeDtypeStruct((1, N_STATE), F32),
                   jax.ShapeDtypeStruct((SSM_SUPER, 128, 512), F32), jax.ShapeDtypeStruct((SSM_SUPER, 128, 512), F32),
                   jax.ShapeDtypeStruct((SSM_SUPER, 512, 128), F32), jax.ShapeDtypeStruct((SSM_SUPER, 512, 128), F32),
                   jax.ShapeDtypeStruct((1, SSM_WIDTH), F32)],
        scratch_shapes=[pltpu.VMEM((chunk, N_STATE), F32), pltpu.VMEM((chunk, N_STATE), F32),
                        pltpu.VMEM((SUB, N_STATE), F32), pltpu.VMEM((SUB, N_STATE), F32),
                        pltpu.VMEM((N_SCAN_TABLES, SUB, N_STATE), F32)],
        compiler_params=_params(1),
    )(dys, z, h_re, h_im, h_re, h_im, b_re, b_im, c_re, c_im, lam_re, lam_im, d_skip)


def _local_step(x, p, pos, tgt, sm, w_in, late_bufs, place):
    s = x.shape[0]
    tm = min(512, s)
    ts = min(2048, s)
    chunk = min(256, s)
    ni = s // tm
    nk = s // ts
    g_mix, g_ffn, g_final = sm["g_mix"], sm["g_ffn"], sm["g_final"]
    rowblk, rowmap = _rows(tm, D_MODEL)
    vec1k = ((1, D_MODEL), lambda *_: (0, 0))

    (n1,) = _ew("rms_mix", (ni,), [(x, rowblk, rowmap), (g_mix, *vec1k)], [((s, D_MODEL), BF16, rowblk, rowmap)],
                lambda pids, h, g: ((_rms_fwd_tile(h, g),), ()))

    half_in = IN_WIDTH // 8
    tmb = min(1024, s)
    nib = s // tmb
    n_late = len(late_bufs)
    g_start, g_forward, g_finish = _gather_stages(n_late)
    in_steps = N_CHIPS * nib
    chip_w = IN_WIDTH // N_CHIPS
    gather = dict(ins=late_bufs, outs=[jax.ShapeDtypeStruct(b.shape, b.dtype) for b in late_bufs], aliased=True,
                  sems=[3 * n_late] * 4, stages=[(0, g_start), (in_steps // 2, g_forward), (in_steps - 1, g_finish)])
    z, *late = _mm("in_proj", (nib, N_CHIPS, 1),
                   [(n1, (tmb, D_MODEL), lambda i, j, k: (i, 0), w_in, (None, D_MODEL, chip_w), lambda i, j, k: (j, 0, 0))],
                   "nn", [((s, IN_WIDTH), BF16, (tmb, chip_w), lambda i, j, k: (i, j))], j_outer=True, comm=gather)
    w_ap, w_ga, w_gb, w_out, w_fg, w_fu, w_fd, w_pg, w_pp = (
        g.reshape(N_CHIPS, 2 * g.shape[2], g.shape[3]) for g in late)
    w_out2 = w_out.reshape(D_MODEL, D_MODEL)
    w_pg2 = w_pg.reshape(D_MODEL, D_MODEL)

    inv = ROPE_THETA ** (-jnp.arange(ROPE_HALF, dtype=F32) * 2.0 / ROPE_DIM)
    inv_row = jnp.concatenate([inv, inv, jnp.zeros((HEAD_DIM - ROPE_DIM,), F32)]).reshape(1, HEAD_DIM)
    tabs = _rope_tables(pos.astype(F32).reshape(s, 1), inv_row, tm)

    qk0, qkv_views = _qkv_layout(z, tabs, tm)
    v0_col = (2 * N_GROUPS * GROUP_WIDTH) // GROUP_WIDTH
    group_in = [((qk0, qk0, z), (0, 1, v0_col))] + [(trio, (0, 0, 0)) for trio in qkv_views]
    fwd_out = [_attn_fwd(*arrs, dil, cols3) for (arrs, cols3), dil in zip(group_in, GROUP_DILATIONS)]
    attn, lse, merged_views = _attn_merge(fwd_out[0][0], fwd_out[0][1], fwd_out[1:], tm)

    def chip_cols(parts):
        return (jnp.concatenate(parts, axis=1),), ()

    def proj_cols(name, a, width, w):
        blk = (None, width, 256)
        pairs = [(a, (tmb, width), lambda i, j, k: (i, 0), w, blk, lambda i, j, k: (0, 0, 0))]
        pairs += [(None, None, None, w, blk, (lambda i, j, k, q=q: (q, 0, 0))) for q in range(1, N_CHIPS)]
        return _mm(name, (nib, 1, 1), pairs, "nn", [((s, D_MODEL), BF16, (tmb, D_MODEL), lambda i, j, k: (i, 0))],
                   epilogue=chip_cols, sum_pairs=False)[0]

    def proj512(name, a, w):
        return proj_cols(name, a, GROUP_WIDTH, w)

    attn_d = proj512("attn_proj", attn, w_ap)

    bt_re = jnp.transpose(sm["b_re"], (0, 2, 1))
    bt_im = jnp.transpose(sm["b_im"], (0, 2, 1))
    log_dt_col = sm["log_dt"].reshape(SSM_GROUPS, 1)
    lam_re, lam_im, bbt_re, bbt_im = _ssm_prep(sm["a_re"], sm["a_im"], log_dt_col, bt_re, bt_im)
    b_re_m = _block_diag(bbt_re, SSM_GROUP, SSM_STATE).astype(BF16)
    b_im_m = _block_diag(bbt_im, SSM_GROUP, SSM_STATE).astype(BF16)
    c_re_m = _block_diag(jnp.transpose(sm["c_re"], (0, 2, 1)), SSM_STATE, SSM_GROUP).astype(BF16)
    c_im_m = _block_diag(jnp.transpose(sm["c_im"], (0, 2, 1)), SSM_STATE, SSM_GROUP).astype(BF16)
    lam_re_row = lam_re.reshape(1, N_STATE)
    lam_im_row = lam_im.reshape(1, N_STATE)
    d_skip_row = sm["d_skip"].reshape(1, SSM_WIDTH)
    h_re, h_im, ys, yg = _ssm_fwd(z, b_re_m, b_im_m, c_re_m, c_im_m, lam_re_row, lam_im_row, d_skip_row, chunk)

    pa = proj512("glu_a", yg, w_ga)
    pb = proj512("glu_b", yg, w_gb)

    ga_blk = ((tm, D_MODEL), lambda i: (i, COL_GA // D_MODEL))
    gs_blk = ((tm, D_MODEL), lambda i: (i, COL_GS // D_MODEL))

    def mix_fn(pids, ga, gs, ad, a, b):
        ga, gs, ad, a, b = (t.astype(F32) for t in (ga, gs, ad, a, b))
        return (_sig(ga) * ad + _sig(gs) * (a * _sig(b)),), ()

    (mix,) = _ew("gate_mix", (ni,), [(z, *ga_blk), (z, *gs_blk), (attn_d, rowblk, rowmap), (pa, rowblk, rowmap),
                                     (pb, rowblk, rowmap)], [((s, D_MODEL), BF16, rowblk, rowmap)], mix_fn)

    def out_epi(acc, xr, g):
        h1 = acc + xr
        return (h1, _rms_fwd_tile(h1, g)), ()

    m3 = lambda i, j, k: (i, 0)
    w3 = lambda i, j, k: (0, 0)
    h1, n2 = _mm("out_proj", (nib, 1, 1), [(mix, (tmb, D_MODEL), m3, w_out2, (D_MODEL, D_MODEL), w3)], "nn",
                 [((s, D_MODEL), F32, (tmb, D_MODEL), m3), ((s, D_MODEL), BF16, (tmb, D_MODEL), m3)],
                 epilogue=out_epi, extras=[(x, (tmb, D_MODEL), m3), (g_ffn, (1, D_MODEL), w3)])

    ffq = (None, tm, D_FF_Q)
    ffq_map = lambda i, j, k: (j, i, 0)

    def ffn_in_epi(parts):
        gts, ups = parts[0::2], parts[1::2]
        acts = [gt * _sig(gt) * u_ for gt, u_ in zip(gts, ups)]
        return (jnp.stack(gts, axis=0), jnp.stack(ups, axis=0), jnp.stack(acts, axis=0)), ()

    w_ffq = (None, D_MODEL, D_FF_Q)
    ff_pairs = []
    for q in range(N_CHIPS):
        blk_q = lambda i, j, k, q=q: (q, 0, 0)
        ff_pairs.append((n2, (tm, D_MODEL), m3, w_fg, w_ffq, blk_q) if q == 0 else (None, None, None, w_fg, w_ffq, blk_q))
        ff_pairs.append((None, None, None, w_fu, w_ffq, blk_q))
    ff_all = (N_CHIPS, tm, D_FF_Q)
    ff_all_map = lambda i, j, k: (0, i, 0)
    gate, up, act = _mm("ffn_gate_up", (ni, 1, 1), ff_pairs, "nn",
                        [((N_CHIPS, s, D_FF_Q), BF16, ff_all, ff_all_map)] * 3, epilogue=ffn_in_epi,
                        sum_pairs=False, resident_b=True)

    (h2,) = _mm("ffn_down", (nib, 1, 1),
                [(act, (None, tmb, D_FF_Q), (lambda i, j, k, q=q: (q, i, 0)), w_fd, (None, D_FF_Q, D_MODEL),
                  (lambda i, j, k, q=q: (q, 0, 0))) for q in range(N_CHIPS)], "nn",
                [((s, D_MODEL), F32, (tmb, D_MODEL), m3)], epilogue=lambda acc, hr: ((acc + hr,), ()),
                extras=[(h1, (tmb, D_MODEL), m3)])

    pp = proj_cols("ple_proj", p, PLE_DIM, w_pp)

    def ple_head_epi(acc, hr, ppr, t, g):
        sg = _sig(acc)
        ppf = ppr.astype(F32)
        h = hr + sg * ppf
        r = lax.rsqrt(jnp.mean(h * h, axis=-1, keepdims=True) + EPS)
        hhat = h * r
        diff = hhat * g - t
        loss = 0.5 * jnp.sum(jnp.mean(diff * diff, axis=-1, keepdims=True))
        dy = diff * (1.0 / D_MODEL)
        gy = dy * g
        dh = r * (gy - hhat * jnp.mean(gy * hhat, axis=-1, keepdims=True))
        return ((dh, dh * ppf * sg * (1.0 - sg), dh * sg),
                (jnp.full((SUB, 128), loss, F32), jnp.sum(dy * hhat, axis=0, keepdims=True)))

    tile_row = (tm, D_MODEL)
    dh3, dgl, dpp, loss_acc, dg_final = _mm(
        "ple_gate_head", (ni, 1, 1), [(h2, tile_row, m3, w_pg2, (D_MODEL, D_MODEL), w3)], "nn",
        [((s, D_MODEL), F32, tile_row, m3), ((s, D_MODEL), BF16, tile_row, m3), ((s, D_MODEL), BF16, tile_row, m3)],
        epilogue=ple_head_epi,
        extras=[(h2, tile_row, m3), (pp, tile_row, m3), (tgt, tile_row, m3), (g_final, (1, D_MODEL), w3)],
        acc_outs=[((SUB, 128), F32), ((1, D_MODEL), F32)])

    def wgrad(name, a, a_block, a_imap, b, b_block, b_imap, out_shape, out_block, out_imap, nj, acc_shape):
        return _mm(name, (1, nj, nk), [(a, a_block, a_imap, b, b_block, b_imap)], "tn",
                   [(out_shape, F32, out_block, out_imap)], acc_shape=acc_shape)[0]

    tk0 = lambda i, j, k: (k, 0)
    tkj = lambda i, j, k: (k, j)
    def wgrad_cols(name, a, width, dy_):
        def split(acc):
            return (jnp.stack([acc[:, q * 256:(q + 1) * 256] for q in range(N_CHIPS)], axis=0),), ()

        return _mm(name, (1, 1, nk), [(a, (ts, width), tk0, dy_, (ts, D_MODEL), tk0)], "tn",
                   [((N_CHIPS, width, 256), F32, (N_CHIPS, width, 256), lambda i, j, k: (0, 0, 0))], epilogue=split,
                   acc_shape=(width, D_MODEL))[0]

    d_w_pp = wgrad_cols("d_ple_proj", p, PLE_DIM, dpp)
    d_w_pg = wgrad("d_ple_gate", h2, (ts, D_MODEL), tk0, dgl, (ts, D_MODEL), tk0, (D_MODEL, D_MODEL),
                   (D_MODEL, D_MODEL), w3, 1, (D_MODEL, D_MODEL))

    (dh2,) = _mm("ple_gate_bwd", (nib, 1, 1), [(dgl, (tmb, D_MODEL), m3, w_pg2, (D_MODEL, D_MODEL), w3)], "nt",
                 [((s, D_MODEL), F32, (tmb, D_MODEL), m3)], epilogue=lambda acc, d_: ((acc + d_,), ()),
                 extras=[(dh3, (tmb, D_MODEL), m3)])

    def ffn_bwd_epi(parts, gt_all, u_all):
        dgs_, dus_ = [], []
        for q, dact in enumerate(parts):
            gt, u_ = gt_all[q].astype(F32), u_all[q].astype(F32)
            sg = _sig(gt)
            dgs_.append(dact * u_ * (sg * (1.0 + gt * (1.0 - sg))))
            dus_.append(dact * gt * sg)
        return (jnp.stack(dgs_, axis=0), jnp.stack(dus_, axis=0)), ()

    fd_pairs = [((dh2, (tm, D_MODEL), m3) if q == 0 else (None, None, None))
                + (w_fd, (None, D_FF_Q, D_MODEL), (lambda i, j, k, q=q: (q, 0, 0))) for q in range(N_CHIPS)]
    dgate, dup = _mm("ffn_down_bwd", (ni, 1, 1), fd_pairs, "nt",
                     [((N_CHIPS, s, D_FF_Q), BF16, ff_all, ff_all_map)] * 2, epilogue=ffn_bwd_epi,
                     extras=[(gate, ff_all, ff_all_map), (up, ff_all, ff_all_map)], sum_pairs=False, resident_b=True)

    ffq_t = (None, ts, D_FF_Q)
    ffq_tmap = lambda i, j, k: (j, k, 0)
    blk_j = lambda i, j, k: (j, 0, 0)
    d_w_fd = wgrad("d_ffn_down", act, ffq_t, ffq_tmap, dh2, (ts, D_MODEL), tk0, (N_CHIPS, D_FF_Q, D_MODEL),
                   (None, D_FF_Q, D_MODEL), blk_j, N_CHIPS, (D_FF_Q, D_MODEL))
    d_w_fg = wgrad("d_ffn_gate", n2, (ts, D_MODEL), tk0, dgate, ffq_t, ffq_tmap, (N_CHIPS, D_MODEL, D_FF_Q),
                   (None, D_MODEL, D_FF_Q), blk_j, N_CHIPS, (D_MODEL, D_FF_Q))
    d_w_fu = wgrad("d_ffn_up", n2, (ts, D_MODEL), tk0, dup, ffq_t, ffq_tmap, (N_CHIPS, D_MODEL, D_FF_Q),
                   (None, D_MODEL, D_FF_Q), blk_j, N_CHIPS, (D_MODEL, D_FF_Q))

    def norm_bwd_epi(acc, h, d_res, g):
        dh, dg = _rms_bwd_tile(acc, h, g)
        return (d_res + dh,), (dg,)

    ffq_k = lambda i, j, k: (k, i, 0)
    blk_k = lambda i, j, k: (k, 0, 0)
    fi_pairs = []
    for q in range(N_CHIPS):
        a_q = lambda i, j, k, q=q: (q, i, 0)
        b_q = lambda i, j, k, q=q: (q, 0, 0)
        fi_pairs.append((dgate, ffq, a_q, w_fg, (None, D_MODEL, D_FF_Q), b_q))
        fi_pairs.append((dup, ffq, a_q, w_fu, (None, D_MODEL, D_FF_Q), b_q))
    dh1, dg_ffn = _mm("ffn_in_bwd", (ni, 1, 1), fi_pairs, "nt",
                      [((s, D_MODEL), F32, (tm, D_MODEL), m3)], epilogue=norm_bwd_epi,
                      extras=[(h1, (tm, D_MODEL), m3), (dh2, (tm, D_MODEL), m3), (g_ffn, (1, D_MODEL), w3)],
                      acc_outs=[((1, D_MODEL), F32)], resident_b=True)

    d_w_out = wgrad("d_out_proj", mix, (ts, D_MODEL), tk0, dh1, (ts, D_MODEL), tk0, (D_MODEL, D_MODEL),
                    (D_MODEL, D_MODEL), w3, 1, (D_MODEL, D_MODEL))

    def mix_bwd_epi(dm, ga, gs, ad, a, b):
        ga, gs, ad, a, b = (t.astype(F32) for t in (ga, gs, ad, a, b))
        s_a, s_s, s_b = _sig(ga), _sig(gs), _sig(b)
        d_ssm = dm * s_s
        return (dm * ad * s_a * (1.0 - s_a), dm * (a * s_b) * s_s * (1.0 - s_s), dm * s_a, d_ssm * s_b,
                d_ssm * a * s_b * (1.0 - s_b)), ()

    tile_m = (tm, D_MODEL)
    dga, dgs, dattn_d, dpa, dpb = _mm(
        "out_proj_bwd", (ni, 1, 1), [(dh1, tile_m, m3, w_out2, (D_MODEL, D_MODEL), w3)], "nt",
        [((s, D_MODEL), BF16, tile_m, m3)] * 5, epilogue=mix_bwd_epi,
        extras=[(z, tile_m, lambda i, j, k: (i, COL_GA // D_MODEL)), (z, tile_m, lambda i, j, k: (i, COL_GS // D_MODEL)),
                (attn_d, tile_m, m3), (pa, tile_m, m3), (pb, tile_m, m3)])

    d_w_ap = wgrad_cols("d_attn_proj", attn, GROUP_WIDTH, dattn_d)
    d_w_ga = wgrad_cols("d_glu_a", yg, GROUP_WIDTH, dpa)
    d_w_gb = wgrad_cols("d_glu_b", yg, GROUP_WIDTH, dpb)

    ik = lambda i, j, k: (i, k)

    def cols_bwd(dy_, w):
        return [(dy_, (tmb, 256), (lambda i, j, k, q=q: (i, q)), w, (None, GROUP_WIDTH, 256),
                 (lambda i, j, k, q=q: (q, 0, 0))) for q in range(N_CHIPS)]

    (dattn,) = _mm("attn_proj_bwd", (nib, 1, 1), cols_bwd(dattn_d, w_ap), "nt",
                   [((s, GROUP_WIDTH), BF16, (tmb, GROUP_WIDTH), m3)])

    (dys,) = _mm("glu_bwd", (nib, 1, 1), cols_bwd(dpa, w_ga) + cols_bwd(dpb, w_gb), "nt",
                 [((s, GROUP_WIDTH), F32, (tmb, GROUP_WIDTH), m3)],
                 epilogue=lambda acc, y_: ((acc * _gelu_grad(y_),), ()),
                 extras=[(ys, (tmb, GROUP_WIDTH), m3)])

    du, d_lr, d_li, d_bre, d_bim, d_cre, d_cim, d_dskip = _ssm_bwd(
        dys, z, h_re, h_im, b_re_m, b_im_m, c_re_m, c_im_m, lam_re_row, lam_im_row, d_skip_row, chunk)

    dattn_views = _to_views(dattn, tm)
    bwd_in = [(dattn, attn, lse)] + [(dv_, ov_, lv_) for dv_, (ov_, lv_) in zip(dattn_views, merged_views)]
    qkv_grads = [_attn_bwd(*arrs, *dol, dil, cols3)
                 for (arrs, cols3), dol, dil in zip(group_in, bwd_in, GROUP_DILATIONS)]
    early = [d_w_ap, d_w_ga, d_w_gb, d_w_out.reshape(N_CHIPS, D_MODEL // N_CHIPS, D_MODEL), d_w_fg, d_w_fu, d_w_fd,
             d_w_pg.reshape(N_CHIPS, D_MODEL // N_CHIPS, D_MODEL), d_w_pp]
    early5 = [g.reshape(N_CHIPS, 2, g.shape[1] // 2, g.shape[2]) for g in early]
    n_e = len(early5)
    p_start, p_finish = _pair_exchange_stages(n_e)
    dz, early_theirs = _dz_layout(
        qkv_grads, du, dga, dgs, tabs, tm,
        comm=dict(ins=early5, outs=_pair_exchange_shapes(early5), sems=[N_CHIPS * n_e] * 2, start=p_start, finish=p_finish))
    early_parts = [_pair_sum(g, t, place) for g, t in zip(early5, early_theirs)]

    chip_in = IN_WIDTH // N_CHIPS
    ip_pairs = [(dz, (tm, chip_in), (lambda i, j, k, q=q: (i, q)), w_in, (None, D_MODEL, chip_in),
                 (lambda i, j, k, q=q: (q, 0, 0))) for q in range(N_CHIPS)]
    grad_x, dg_mix = _mm("in_proj_bwd", (ni, 1, 1), ip_pairs, "nt",
                         [((s, D_MODEL), F32, (tm, D_MODEL), m3)], epilogue=norm_bwd_epi,
                         extras=[(x, (tm, D_MODEL), m3), (dh1, (tm, D_MODEL), m3), (g_mix, (1, D_MODEL), w3)],
                         acc_outs=[((1, D_MODEL), F32)], resident_b=True)

    d_bbt_re = _block_diag_t(d_bre, SSM_GROUP, SSM_STATE)
    d_bbt_im = _block_diag_t(d_bim, SSM_GROUP, SSM_STATE)
    d_a_re, d_a_im, d_log_dt, d_bt_re, d_bt_im = _ssm_param_bwd(
        sm["a_re"], sm["a_im"], log_dt_col, bt_re, bt_im,
        d_lr.reshape(SSM_GROUPS, SSM_STATE), d_li.reshape(SSM_GROUPS, SSM_STATE), d_bbt_re, d_bbt_im)
    small = {
        "g_mix": dg_mix, "a_re": d_a_re, "a_im": d_a_im, "log_dt": d_log_dt,
        "b_re": jnp.transpose(d_bt_re, (0, 2, 1)), "b_im": jnp.transpose(d_bt_im, (0, 2, 1)),
        "c_re": jnp.transpose(_block_diag_t(d_cre, SSM_STATE, SSM_GROUP), (0, 2, 1)),
        "c_im": jnp.transpose(_block_diag_t(d_cim, SSM_STATE, SSM_GROUP), (0, 2, 1)),
        "d_skip": d_dskip, "g_ffn": dg_ffn, "g_final": dg_final,
    }
    vec = _pack([small[n] for n in SMALL] + [loss_acc[0, 0].reshape(1)])

    x_start, x_finish = _chip_exchange_stages(n_e)
    v_start, v_finish = _all_exchange_stages()

    def both(f_chips, f_vec):
        def stage(ins, outs, sems):
            f_chips(ins[:n_e], outs[:n_e], sems[:2])
            f_vec(ins[n_e:], outs[n_e:], sems[2:])
        return stage

    ts_in = min(2048, s)
    win_steps = 8 * (s // ts_in)
    exchange = dict(ins=early_parts + [vec],
                    outs=[jax.ShapeDtypeStruct(t.shape, t.dtype) for t in early_parts]
                    + [jax.ShapeDtypeStruct((8,) + vec.shape, vec.dtype)],
                    aliased=False, sems=[3 * n_e, 3 * n_e, 7, 7],
                    stages=[(0, both(x_start, v_start)), (win_steps - 1, both(x_finish, v_finish))])
    d_w_in, *got = _mm("d_in_proj", (1, 8, s // ts_in), [(n1, (ts_in, D_MODEL), tk0, dz, (ts_in, half_in), tkj)], "tn",
                       [((N_CHIPS, D_MODEL, IN_WIDTH // N_CHIPS), F32, (None, D_MODEL, half_in),
                         lambda i, j, k: (j // 2, 0, j % 2))], acc_shape=(D_MODEL, half_in), comm=exchange)
    return grad_x, d_w_in, early_parts, got[:n_e], vec, got[n_e]


BIG = ("w_in", "w_attn_proj", "w_glu_a", "w_glu_b", "w_out", "w_ffn_gate", "w_ffn_up", "w_ffn_down", "w_ple_gate",
       "w_ple_proj")
SMALL = ("g_mix", "a_re", "a_im", "log_dt", "b_re", "b_im", "c_re", "c_im", "d_skip", "g_ffn", "g_final")
ANY = pl.BlockSpec(memory_space=pl.ANY)


def _place():
    x, y, c = lax.axis_index("x"), lax.axis_index("y"), lax.axis_index("c")
    chips = [(1 - x, y), (x, 1 - y), (1 - x, 1 - y)]
    return x, y, c, chips


def _remote(src, dst, send_sem, recv_sem, to):
    return pltpu.make_async_remote_copy(src_ref=src, dst_ref=dst, send_sem=send_sem, recv_sem=recv_sem, device_id=to,
                                        device_id_type=MESH)


def _comm_call(name, body, ins, out_shapes, n_sems, aliases=None):
    n_w = len(ins)
    return pl.pallas_call(
        body, name=name, in_specs=[ANY] * n_w, out_specs=[ANY] * len(out_shapes), out_shape=out_shapes,
        scratch_shapes=[pltpu.SemaphoreType.DMA((n,)) for n in n_sems], input_output_aliases=aliases or {},
    )(*ins)


def _gather_weights(bufs):
    n_w = len(bufs)
    start, forward, finish = _gather_stages(n_w)

    def body(*refs):
        ins, outs, sems = refs[:n_w], refs[n_w:2 * n_w], refs[2 * n_w:]
        start(ins, outs, sems)
        forward(ins, outs, sems)
        finish(ins, outs, sems)

    out_shapes = [jax.ShapeDtypeStruct(b.shape, b.dtype) for b in bufs]
    return _comm_call("gather_weights", body, bufs, out_shapes, [3 * n_w] * 4, aliases={w: w for w in range(n_w)})


def _gather_stages(n_w):
    def each():
        x, y, c, chips = _place()
        for w in range(n_w):
            for j, (cx, cy) in enumerate(chips):
                yield w, 3 * w + j, 2 * x + y, 2 * cx + cy, (cx, cy, c), (x, y, 1 - c), c

    def start(ins, outs, sems):
        for w, k, me, _, peer, _, c in each():
            mine = outs[w].at[me, c]
            _remote(mine, mine, sems[0].at[k], sems[1].at[k], peer).start()

    def forward(ins, outs, sems):
        for w, k, _, src_chip, peer, sib, c in each():
            landed = outs[w].at[src_chip, c]
            _remote(landed, landed, sems[0].at[k], sems[1].at[k], peer).wait_recv()
            _remote(landed, landed, sems[2].at[k], sems[3].at[k], sib).start()

    def finish(ins, outs, sems):
        for w, k, me, src_chip, peer, sib, c in each():
            other = outs[w].at[src_chip, 1 - c]
            _remote(other, other, sems[2].at[k], sems[3].at[k], sib).wait_recv()
        for w, k, me, src_chip, peer, sib, c in each():
            mine = outs[w].at[me, c]
            _remote(mine, mine, sems[0].at[k], sems[1].at[k], peer).wait_send()
            landed = outs[w].at[src_chip, c]
            _remote(landed, landed, sems[2].at[k], sems[3].at[k], sib).wait_send()

    return start, forward, finish


def _pair_exchange(grads):
    n_w = len(grads)
    start, finish = _pair_exchange_stages(n_w)

    def body(*refs):
        ins, outs, sems = refs[:n_w], refs[n_w:2 * n_w], refs[2 * n_w:]
        start(ins, outs, sems)
        finish(ins, outs, sems)

    return _comm_call("grad_pair_exchange", body, grads, _pair_exchange_shapes(grads), [N_CHIPS * n_w] * 2)


def _pair_exchange_shapes(grads):
    return [jax.ShapeDtypeStruct((N_CHIPS,) + g.shape[2:], g.dtype) for g in grads]


def _pair_exchange_stages(n_w):
    def each():
        x, y, c, _ = _place()
        for w in range(n_w):
            for q in range(N_CHIPS):
                yield w, q, N_CHIPS * w + q, c, (x, y, 1 - c)

    def start(ins, outs, sems):
        for w, q, k, c, sib in each():
            _remote(ins[w].at[q, 1 - c], outs[w].at[q], sems[0].at[k], sems[1].at[k], sib).start()

    def finish(ins, outs, sems):
        for w, q, k, c, sib in each():
            _remote(ins[w].at[q, 1 - c], outs[w].at[q], sems[0].at[k], sems[1].at[k], sib).wait()

    return start, finish


def _chip_exchange(parts):
    n_w = len(parts)

    start, finish = _chip_exchange_stages(n_w)

    def body(*refs):
        ins, outs, sems = refs[:n_w], refs[n_w:2 * n_w], refs[2 * n_w:]
        start(ins, outs, sems)
        finish(ins, outs, sems)

    out_shapes = [jax.ShapeDtypeStruct(t.shape, t.dtype) for t in parts]
    return _comm_call("grad_chip_exchange", body, parts, out_shapes, [3 * n_w, 3 * n_w])


def _chip_exchange_stages(n_w):
    def each():
        x, y, c, chips = _place()
        for w in range(n_w):
            for j, (cx, cy) in enumerate(chips):
                yield w, 3 * w + j, 2 * x + y, 2 * cx + cy, (cx, cy, c)

    def start(ins, outs, sems):
        for w, k, me, peer_chip, peer in each():
            _remote(ins[w].at[peer_chip], outs[w].at[me], sems[0].at[k], sems[1].at[k], peer).start()

    def finish(ins, outs, sems):
        for w, k, me, peer_chip, peer in each():
            got = outs[w].at[peer_chip]
            _remote(got, got, sems[0].at[k], sems[1].at[k], peer).wait_recv()
        for w, k, me, peer_chip, peer in each():
            _remote(ins[w].at[peer_chip], outs[w].at[me], sems[0].at[k], sems[1].at[k], peer).wait_send()

    return start, finish


def _pair_gather(halves):
    n_w = len(halves)

    def body(*refs):
        ins, outs = refs[:n_w], refs[n_w:2 * n_w]
        send, recv = refs[2 * n_w:]
        x, y, c, _ = _place()
        sib = (x, y, 1 - c)
        cps = []
        for w in range(n_w):
            cp = _remote(ins[w], outs[w], send.at[w], recv.at[w], sib)
            cp.start()
            cps.append(cp)
        for cp in cps:
            cp.wait()

    out_shapes = [jax.ShapeDtypeStruct(h.shape, h.dtype) for h in halves]
    return _comm_call("grad_pair_gather", body, halves, out_shapes, [n_w] * 2)


def _all_exchange_stages():
    def each():
        x, y, c, _ = _place()
        for k in range(1, 8):
            px, py, pc = x ^ ((k >> 2) & 1), y ^ ((k >> 1) & 1), c ^ (k & 1)
            yield k - 1, 4 * x + 2 * y + c, 4 * px + 2 * py + pc, (px, py, pc)

    def start(ins, outs, sems):
        for k, me, _, peer in each():
            _remote(ins[0], outs[0].at[me], sems[0].at[k], sems[1].at[k], peer).start()

    def finish(ins, outs, sems):
        for k, me, src, peer in each():
            got = outs[0].at[src]
            _remote(got, got, sems[0].at[k], sems[1].at[k], peer).wait_recv()
        for k, me, src, peer in each():
            _remote(ins[0], outs[0].at[me], sems[0].at[k], sems[1].at[k], peer).wait_send()

    return start, finish


def _row_tile(r):
    for t in (256, 128, 176, 64, 32, 16, 8):
        if r % t == 0:
            return t
    return r


P_C, P_CHIP, P_DEV = 2, 3, 4


def _cast_into_slot(w2, place):
    r, c = w2.shape
    t = _row_tile(r)
    return _ew("cast_shard", (r // t,), [(w2, (t, c), lambda i, pv: (i, 0))],
               [((N_CHIPS, r, c), BF16, (None, t, c), lambda i, pv: (pv[P_CHIP], i, 0))],
               lambda pids, a: ((a,), ()), place=place)[0]


def _pair_sum(mine, theirs, place):
    _, r, c = theirs.shape
    t = _row_tile(r)
    own = ((None, None, t, c), lambda q, i, pv: (q, pv[P_C], i, 0))
    blk = ((None, t, c), lambda q, i, pv: (q, i, 0))
    return _ew("grad_pair_sum", (N_CHIPS, r // t), [(mine, *own), (theirs, *blk)], [((N_CHIPS, r, c), BF16, *blk)],
               lambda pids, a, b: ((a + b,), ()), place=place)[0]


def _chip_sum(own, got, place):
    _, r, c = own.shape
    t = _row_tile(r)
    ins = []
    for q in range(N_CHIPS):
        ins.append((own, (None, t, c), (lambda i, pv, q=q: (q, i, 0))))
        ins.append((got, (None, t, c), (lambda i, pv, q=q: (jnp.where(pv[P_CHIP] == q, (q + 1) % N_CHIPS, q), i, 0))))

    def fn(pids, *tiles):
        me = pids[0][P_CHIP]
        tot = None
        for q in range(N_CHIPS):
            term = jnp.where(me == q, tiles[2 * q], tiles[2 * q + 1]).astype(F32)
            tot = term if tot is None else tot + term
        return (tot,), ()

    return _ew("grad_chip_sum", (r // t,), ins, [((r, c), F32, (t, c), lambda i, pv: (i, 0))], fn, place=place)[0]


def _adamw_tile(w, g, m, v):
    m = ADAM_B1 * m + (1.0 - ADAM_B1) * g
    v = ADAM_B2 * v + (1.0 - ADAM_B2) * (g * g)
    m_hat = m / (1.0 - ADAM_B1 ** ADAM_STEP)
    v_hat = v / (1.0 - ADAM_B2 ** ADAM_STEP)
    delta = -ADAM_LR * (m_hat / (jnp.sqrt(v_hat) + ADAM_EPS) + ADAM_WD * w)
    return delta, m, v


def _adamw(name, g2, w2, m2, v2):
    r, c = w2.shape
    t = _row_tile(r)
    blk, imap = _rows(t, c)

    def fn(pids, g, w, m, v):
        delta, nm, nv = _adamw_tile(w, g, m, v)
        return (g, delta, nm, nv), ()

    return _ew(name, (r // t,), [(a, blk, imap) for a in (g2, w2, m2, v2)], [((r, c), F32, blk, imap)] * 4, fn)


def _adamw_halves(name, mine, theirs, w2, m2, v2, place):
    r, c = w2.shape
    t = _row_tile(r // 2)
    n_t = (r // 2) // t
    half = ((t, c), lambda h, i, pv: (i, 0))
    whole = ((t, c), lambda h, i, pv: (h * n_t + i, 0))

    def fn(pids, ga, gb, w, m, v):
        g = jnp.where(pids[1] == pids[0][P_C], ga, gb)
        delta, nm, nv = _adamw_tile(w, g, m, v)
        return (g, delta, nm, nv), ()

    return _ew(name, (2, n_t), [(mine, *half), (theirs, *half), (w2, *whole), (m2, *whole), (v2, *whole)],
               [((r, c), F32, *whole)] * 4, fn, place=place)


def _device_sum(own, got, place):
    r, c = own.shape
    t = _row_tile(r)
    ins = [(own, (t, c), lambda i, pv: (i, 0))]
    for q in range(8):
        ins.append((got, (None, t, c), (lambda i, pv, q=q: (jnp.where(pv[P_DEV] == q, (q + 1) % 8, q), i, 0))))

    def fn(pids, mine, *parts):
        me = pids[0][P_DEV]
        tot = None
        for q in range(8):
            term = jnp.where(me == q, mine, parts[q])
            tot = term if tot is None else tot + term
        return (tot,), ()

    return _ew("small_device_sum", (r // t,), ins, [((r, c), F32, (t, c), lambda i, pv: (i, 0))], fn, place=place)[0]


def _pack(parts):
    flat = jnp.concatenate([a.reshape(-1) for a in parts])
    pad = (-flat.shape[0]) % (SUB * 128)
    return jnp.pad(flat, (0, pad)).reshape(-1, 128)


def _unpack(mat, shapes):
    flat = mat.reshape(-1)
    out, off = [], 0
    for shp in shapes:
        n = math.prod(shp)
        out.append(flat[off:off + n].reshape(shp))
        off += n
    return out


def kernel(x, p, positions, g_mix, w_in, a_re, a_im, log_dt, b_re, b_im, c_re, c_im, d_skip, w_attn_proj, w_glu_a, w_glu_b, w_out, g_ffn, w_ffn_gate, w_ffn_up, w_ffn_down, w_ple_gate, w_ple_proj, g_final, loss_target, m_g_mix, m_w_in, m_a_re, m_a_im, m_log_dt, m_b_re, m_b_im, m_c_re, m_c_im, m_d_skip, m_w_attn_proj, m_w_glu_a, m_w_glu_b, m_w_out, m_g_ffn, m_w_ffn_gate, m_w_ffn_up, m_w_ffn_down, m_w_ple_gate, m_w_ple_proj, m_g_final, v_g_mix, v_w_in, v_a_re, v_a_im, v_log_dt, v_b_re, v_b_im, v_c_re, v_c_im, v_d_skip, v_w_attn_proj, v_w_glu_a, v_w_glu_b, v_w_out, v_g_ffn, v_w_ffn_gate, v_w_ffn_up, v_w_ffn_down, v_w_ple_gate, v_w_ple_proj, v_g_final):
    given = dict(locals())
    big_w = {n: given[n] for n in BIG}
    w_mats = {n: big_w[n].reshape(big_w[n].shape[1:]) for n in BIG}

    ax, ay, ac = lax.axis_index("x"), lax.axis_index("y"), lax.axis_index("c")
    place = jnp.stack([ax, ay, ac, 2 * ax + ay, 4 * ax + 2 * ay + ac]).astype(jnp.int32)

    bufs = []
    for n in BIG:
        r, c = w_mats[n].shape
        bufs.append(_cast_into_slot(w_mats[n], place).reshape(N_CHIPS, 2, r // 2, c))
    w_in_all = _gather_weights(bufs[:1])[0].reshape((N_CHIPS,) + w_mats["w_in"].shape)

    sm = {
        "g_mix": g_mix.reshape(1, D_MODEL), "g_ffn": g_ffn.reshape(1, D_MODEL), "g_final": g_final.reshape(1, D_MODEL),
        "a_re": a_re[0], "a_im": a_im[0], "log_dt": log_dt[0], "b_re": b_re[0], "b_im": b_im[0], "c_re": c_re[0],
        "c_im": c_im[0], "d_skip": d_skip[0],
    }
    s = x.shape[1]
    grad_x, d_w_in, early_parts, early_got, vec, vec_got = _local_step(
        x[0], p[0, 0], positions[0], loss_target[0], sm, w_in_all, bufs[1:], place)

    r_in, c_in = w_mats["w_in"].shape
    g5_in = [d_w_in.reshape(N_CHIPS, 2, r_in // 2, c_in)]
    in_parts = [_pair_sum(g, t, place) for g, t in zip(g5_in, _pair_exchange(g5_in))]
    chip_parts = in_parts + list(early_parts)
    chip_got = list(_chip_exchange(in_parts)) + list(early_got)
    halves = [_chip_sum(own, got, place) for own, got in zip(chip_parts, chip_got)]
    other_halves = _pair_gather(halves)

    results = {}
    for n, mine, other in zip(BIG, halves, other_halves):
        r, c = w_mats[n].shape
        shp = big_w[n].shape
        outs = _adamw_halves("adamw_" + n, mine, other, w_mats[n], given["m_" + n].reshape(r, c),
                             given["v_" + n].reshape(r, c), place)
        results[n] = [o.reshape(shp) for o in outs]

    small_shapes = [given[n].shape for n in SMALL]
    tot = _device_sum(vec, vec_got, place)
    n_small = sum(math.prod(shp) for shp in small_shapes)
    loss = tot.reshape(-1)[n_small]
    w_s = _pack([given[n] for n in SMALL])
    m_s = _pack([given["m_" + n] for n in SMALL])
    v_s = _pack([given["v_" + n] for n in SMALL])
    rows_s = w_s.shape[0]
    g_s = tot.reshape(-1)[: rows_s * 128].reshape(rows_s, 128)
    outs_s = _adamw("adamw_small", g_s, w_s, m_s, v_s)
    for kind, mat in enumerate(outs_s):
        for n, arr in zip(SMALL, _unpack(mat, small_shapes)):
            results.setdefault(n, [None] * 4)[kind] = arr

    order = ("g_mix", "w_in", "a_re", "a_im", "log_dt", "b_re", "b_im", "c_re", "c_im", "d_skip", "w_attn_proj", "w_glu_a",
             "w_glu_b", "w_out", "g_ffn", "w_ffn_gate", "w_ffn_up", "w_ffn_down", "w_ple_gate", "w_ple_proj", "g_final")
    out = [loss, grad_x.reshape(1, s, D_MODEL)]
    for kind in range(4):
        out += [results[n][kind] for n in order]
    return tuple(out)
```

```python
import math

import jax
import jax.numpy as jnp
from jax import lax
from jax.experimental import pallas as pl
from jax.experimental.pallas import tpu as pltpu

F32 = jnp.float32
BF16 = jnp.bfloat16

D_MODEL = 1024
HEAD_DIM = 128
HEADS_PER_GROUP = 4
GROUP_WIDTH = HEADS_PER_GROUP * HEAD_DIM
GROUP_DILATIONS = (1, 4, 16)
N_GROUPS = len(GROUP_DILATIONS)
LSE_LANES = 32
LSE_WIDTH = HEADS_PER_GROUP * LSE_LANES
ATTN_BLOCK = 128
ROPE_DIM = 32
ROPE_HALF = 16
ROPE_THETA = 500000.0
SSM_WIDTH = 512
SSM_GROUPS = 32
SSM_GROUP = 16
SSM_STATE = 64
N_STATE = SSM_GROUPS * SSM_STATE
SSM_SUPER = 4
IN_WIDTH = 7168
COL_U = 4608
COL_GA = 5120
COL_GS = 6144
D_FF = 2816
N_CHIPS = 4
D_FF_Q = D_FF // N_CHIPS
PLE_DIM = 256
EPS = 1e-6
ADAM_LR = 0.001
ADAM_B1 = 0.9
ADAM_B2 = 0.999
ADAM_EPS = 1e-08
ADAM_WD = 0.01
ADAM_STEP = 10
NEG_BIG = -1e30
VMEM_LIMIT_BYTES = 56 * 1024 * 1024
MESH = pl.DeviceIdType.MESH

_DIMS = {
    "nn": (((1,), (0,)), ((), ())),
    "nt": (((1,), (1,)), ((), ())),
    "tn": (((0,), (0,)), ((), ())),
}


def _params(n_grid):
    return pltpu.CompilerParams(dimension_semantics=("arbitrary",) * n_grid, vmem_limit_bytes=VMEM_LIMIT_BYTES)


def _sig(v):
    return 1.0 / (1.0 + jnp.exp(-v))


def _dot(a, b, mode):
    return lax.dot_general(a, b, _DIMS[mode], preferred_element_type=F32)


def _mm(name, grid, pairs, mode, outs, epilogue=None, extras=(), acc_outs=(), acc_shape=None, j_outer=False,
        sum_pairs=True, resident_b=False, comm=None, place=None, fill=None):
    gi, gj, gk = grid
    n_p, n_e, n_o, n_a = len(pairs), len(extras), len(outs), len(acc_outs)
    assert not n_a or gj == 1
    assert sum_pairs or gk == 1
    run_grid = (gj, gi, gk) if j_outer else grid
    c_ins = list(comm["ins"]) if comm else []
    c_outs = list(comm["outs"]) if comm else []
    c_sems = list(comm["sems"]) if comm else []
    n_ci, n_co, n_cs = len(c_ins), len(c_outs), len(c_sems)
    n_s = 0 if place is None else 1
    n_fill = 0 if fill is None else 1

    def order(imap):
        if place is None:
            return (lambda j, i, k: imap(i, j, k)) if j_outer else imap
        return (lambda j, i, k, pv: imap(i, j, k, pv)) if j_outer else imap

    shared_a = [pr[0] is None for pr in pairs]
    n_in = 2 * n_p - sum(shared_a)

    def body(*refs):
        refs = refs[n_s:]
        pair_refs = list(refs[:n_in])
        extra_refs = refs[n_in: n_in + n_e]
        comm_in = refs[n_in + n_e: n_in + n_e + n_ci]
        at = n_in + n_e + n_ci + n_fill
        out_refs = refs[at: at + n_o]
        sum_refs = refs[at + n_o: at + n_o + n_a]
        comm_out = refs[at + n_o + n_a: at + n_o + n_a + n_co]
        scratch_refs = refs[at + n_o + n_a + n_co:]
        i = pl.program_id(1 if j_outer else 0)
        k = pl.program_id(2)
        if comm:
            step = (pl.program_id(0) * run_grid[1] + pl.program_id(1)) * run_grid[2] + pl.program_id(2)
            sems = scratch_refs[len(scratch_refs) - n_cs:]
            for at_step, stage in comm["stages"]:
                @pl.when(step == at_step)
                def _(stage=stage):
                    stage(comm_in, comm_out, sems)
        part = None if sum_pairs else []
        a = None
        for t in range(n_p):
            if not shared_a[t]:
                a = pair_refs.pop(0)[...].astype(BF16)
            b = pair_refs.pop(0)[...].astype(BF16)
            d = _dot(a, b, mode)
            if sum_pairs:
                part = d if part is None else part + d
            else:
                part.append(d)

        def finish(acc):
            tiles, sums = epilogue(acc, *[e[...] for e in extra_refs]) if epilogue is not None else ((acc,), ())
            for o_ref, tile in zip(out_refs, tiles):
                o_ref[...] = tile.astype(o_ref.dtype)
            if n_a:
                @pl.when(i == 0)
                def _():
                    for s_ref in sum_refs:
                        s_ref[...] = jnp.zeros_like(s_ref)

                for s_ref, s in zip(sum_refs, sums):
                    s_ref[...] += s

        if gk == 1:
            finish(part)
        else:
            acc_ref = scratch_refs[0]

            @pl.when(k == 0)
            def _():
                acc_ref[...] = part

            @pl.when(k > 0)
            def _():
                acc_ref[...] += part

            @pl.when(k == gk - 1)
            def _():
                finish(acc_ref[...])

    in_specs, args = [], []
    for a, a_block, a_imap, b, b_block, b_imap in pairs:
        if a is not None:
            in_specs.append(pl.BlockSpec(a_block, order(a_imap)))
            args.append(a)
        if resident_b:
            in_specs.append(pl.BlockSpec(b_block, order(b_imap), pipeline_mode=pl.Buffered(1)))
        else:
            in_specs.append(pl.BlockSpec(b_block, order(b_imap)))
        args.append(b)
    for e, e_block, e_imap in extras:
        in_specs.append(pl.BlockSpec(e_block, order(e_imap)))
        args.append(e)
    first_comm_in = len(args)
    for c_in in c_ins:
        in_specs.append(pl.BlockSpec(memory_space=pl.ANY))
        args.append(c_in)
    if n_fill:
        in_specs.append(pl.BlockSpec(memory_space=pl.ANY))
        args.append(fill)
    out_shape = [jax.ShapeDtypeStruct(shape, dtype) for shape, dtype, _, _ in outs]
    out_specs = [pl.BlockSpec(block, order(imap)) for _, _, block, imap in outs]
    for shape, dtype in acc_outs:
        out_shape.append(jax.ShapeDtypeStruct(shape, dtype))
        out_specs.append(pl.BlockSpec(shape, lambda *_: (0, 0)))
    first_comm_out = len(out_shape)
    for c_out in c_outs:
        out_shape.append(c_out)
        out_specs.append(pl.BlockSpec(memory_space=pl.ANY))
    aliases = {n_s + first_comm_in + n: first_comm_out + n for n in range(n_ci)} if comm and comm["aliased"] else {}
    if n_fill:
        aliases[n_s + len(args) - 1] = 0
    scratch = [pltpu.VMEM(acc_shape, F32)] if gk > 1 else []
    scratch += [pltpu.SemaphoreType.DMA((n,)) for n in c_sems]
    if n_s:
        spec = pltpu.PrefetchScalarGridSpec(num_scalar_prefetch=1, grid=run_grid, in_specs=in_specs, out_specs=out_specs,
                                            scratch_shapes=scratch)
        return pl.pallas_call(body, name=name, grid_spec=spec, out_shape=out_shape, compiler_params=_params(3),
                              input_output_aliases=aliases)(place, *args)
    return pl.pallas_call(
        body, name=name, grid=run_grid, in_specs=in_specs, out_specs=out_specs,
        out_shape=out_shape, scratch_shapes=scratch, compiler_params=_params(3), input_output_aliases=aliases,
    )(*args)


def _ew(name, grid, ins, outs, fn, acc_outs=(), place=None):
    n_i, n_o, n_a = len(ins), len(outs), len(acc_outs)
    ng = len(grid)
    n_s = 0 if place is None else 1

    def body(*refs):
        in_refs = refs[n_s: n_s + n_i]
        out_refs = refs[n_s + n_i: n_s + n_i + n_o]
        sum_refs = refs[n_s + n_i + n_o:]
        pids = tuple(pl.program_id(a) for a in range(ng))
        if n_s:
            pids = (refs[0],) + pids
        tiles, sums = fn(pids, *[r[...] for r in in_refs])
        for o_ref, tile in zip(out_refs, tiles):
            o_ref[...] = tile.astype(o_ref.dtype)
        if n_a:
            first = pids[0] == 0
            for p_ in pids[1:]:
                first = jnp.logical_and(first, p_ == 0)

            @pl.when(first)
            def _():
                for s_ref in sum_refs:
                    s_ref[...] = jnp.zeros_like(s_ref)

            for s_ref, s in zip(sum_refs, sums):
                s_ref[...] += s

    in_specs = [pl.BlockSpec(block, imap) for _, block, imap in ins]
    out_shape = [jax.ShapeDtypeStruct(shape, dtype) for shape, dtype, _, _ in outs]
    out_specs = [pl.BlockSpec(block, imap) for _, _, block, imap in outs]
    for shape, dtype in acc_outs:
        out_shape.append(jax.ShapeDtypeStruct(shape, dtype))
        out_specs.append(pl.BlockSpec(shape, lambda *_, nd=len(shape): (0,) * nd))
    arrays = [a for a, _, _ in ins]
    if n_s:
        assert not n_a
        spec = pltpu.PrefetchScalarGridSpec(num_scalar_prefetch=1, grid=grid, in_specs=in_specs, out_specs=out_specs)
        return pl.pallas_call(body, name=name, grid_spec=spec, out_shape=out_shape, compiler_params=_params(ng))(
            place, *arrays)
    return pl.pallas_call(
        body, name=name, grid=grid, in_specs=in_specs, out_specs=out_specs, out_shape=out_shape,
        compiler_params=_params(ng),
    )(*arrays)


def _rows(tm, width):
    return (tm, width), (lambda i: (i, 0))


def _rms_fwd_tile(h, g):
    r = lax.rsqrt(jnp.mean(h * h, axis=-1, keepdims=True) + EPS)
    return h * r * g


def _rms_bwd_tile(dn, h, g):
    r = lax.rsqrt(jnp.mean(h * h, axis=-1, keepdims=True) + EPS)
    hhat = h * r
    gy = dn * g
    dh = r * (gy - hhat * jnp.mean(gy * hhat, axis=-1, keepdims=True))
    dg = jnp.sum(dn * hhat, axis=0, keepdims=True)
    return dh, dg


def _rope_tables(pos_col, inv_row, tm):
    s = pos_col.shape[0]

    def fn(pids, pos, inv):
        ang = pos * inv
        lane = lax.broadcasted_iota(jnp.int32, ang.shape, 1)
        cs = jnp.where(lane < ROPE_DIM, jnp.cos(ang), 1.0)
        sn = jnp.sin(ang)
        s_lo = jnp.where(lane < ROPE_HALF, -sn, 0.0)
        s_hi = jnp.where(jnp.logical_and(lane >= ROPE_HALF, lane < ROPE_DIM), sn, 0.0)
        return (cs, s_lo, s_hi), ()

    blk, imap = _rows(tm, 128)
    return _ew(
        "rope_tables", (s // tm,),
        [(pos_col, (tm, 1), lambda i: (i, 0)), (inv_row, (1, 128), lambda i: (0, 0))],
        [((s, 128), F32, blk, imap)] * 3, fn,
    )


def _rope(xh, cs, s_lo, s_hi):
    return xh * cs + pltpu.roll(xh, HEAD_DIM - ROPE_HALF, 1) * s_lo + pltpu.roll(xh, ROPE_HALF, 1) * s_hi


def _rope_t(gh, cs, s_lo, s_hi):
    return gh * cs + pltpu.roll(gh * s_lo, ROPE_HALF, 1) + pltpu.roll(gh * s_hi, HEAD_DIM - ROPE_HALF, 1)


def _attn_geometry(length):
    nb = length // ATTN_BLOCK
    gq = min(4, nb)
    assert nb % gq == 0
    return nb, gq, gq * ATTN_BLOCK, nb // gq


def _band_masks():
    qi = lax.broadcasted_iota(jnp.int32, (ATTN_BLOCK, ATTN_BLOCK), 0)
    kj = lax.broadcasted_iota(jnp.int32, (ATTN_BLOCK, ATTN_BLOCK), 1)
    return kj <= qi, kj >= qi


def _band_mask_pair():
    qi = lax.broadcasted_iota(jnp.int32, (ATTN_BLOCK, 2 * ATTN_BLOCK), 0)
    cj = lax.broadcasted_iota(jnp.int32, (ATTN_BLOCK, 2 * ATTN_BLOCK), 1)
    in_cur = cj >= ATTN_BLOCK
    band = jnp.logical_or(jnp.logical_and(in_cur, cj - ATTN_BLOCK <= qi),
                          jnp.logical_and(cj < ATTN_BLOCK, cj >= qi))
    return band, in_cur


def _attn_fwd(qv, kv, vv, dil, cols3=(0, 0, 0)):
    length = qv.shape[0]
    nb, gq, rows, ni = _attn_geometry(length)

    def body(q_ref, kc_ref, kp_ref, vc_ref, vp_ref, o_ref, l_ref):
        i = pl.program_id(1)
        band, in_cur = _band_mask_pair()
        band_first = jnp.logical_and(band, jnp.logical_or(in_cur, i > 0))
        work = []
        for h in range(HEADS_PER_GROUP):
            cols = slice(h * HEAD_DIM, (h + 1) * HEAD_DIM)
            qh = q_ref[:, cols]
            k_all = jnp.concatenate([kp_ref[:, cols], kc_ref[:, cols]], axis=0)
            v_all = jnp.concatenate([vp_ref[:, cols], vc_ref[:, cols]], axis=0)
            for jj in range(gq):
                rws = slice(jj * ATTN_BLOCK, (jj + 1) * ATTN_BLOCK)
                two = slice(jj * ATTN_BLOCK, (jj + 2) * ATTN_BLOCK)
                work.append(dict(h=h, rws=rws, cols=cols, v=v_all[two], first=jj == 0, s=_dot(qh[rws], k_all[two], "nt")))
        for w in work:
            s = jnp.where(band_first if w["first"] else band, w["s"], NEG_BIG)
            m = jnp.max(s, axis=-1, keepdims=True)
            pexp = jnp.exp(s - m)
            w["den"] = jnp.sum(pexp, axis=-1, keepdims=True)
            w["p"] = pexp.astype(BF16)
            w["lse"] = m + jnp.log(w["den"])
        for w in work:
            o = _dot(w["p"], w["v"], "nn")
            o_ref[w["rws"], w["cols"]] = (o * (1.0 / w["den"])).astype(o_ref.dtype)
            l_ref[w["rws"], w["h"] * LSE_LANES:(w["h"] + 1) * LSE_LANES] = jnp.broadcast_to(w["lse"], (ATTN_BLOCK, LSE_LANES))

    def cur(c):
        return pl.BlockSpec((rows, GROUP_WIDTH), lambda r, i: (i, r + c))

    def prev(c):
        return pl.BlockSpec((ATTN_BLOCK, GROUP_WIDTH), lambda r, i: (jnp.maximum(i * gq - 1, 0), r + c))

    cq, ck, cv = cols3
    return pl.pallas_call(
        body, name=f"attn_fwd_d{dil}", grid=(dil, ni),
        in_specs=[cur(cq), cur(ck), prev(ck), cur(cv), prev(cv)],
        out_specs=[cur(0), pl.BlockSpec((rows, LSE_WIDTH), lambda r, i: (i, r))],
        out_shape=[jax.ShapeDtypeStruct((length, dil * GROUP_WIDTH), BF16),
                   jax.ShapeDtypeStruct((length, dil * LSE_WIDTH), F32)],
        compiler_params=_params(2),
    )(qv, kv, kv, vv, vv)


def _attn_bwd(qv, kv, vv, dov, ov, lv, dil, cols3=(0, 0, 0)):
    length = qv.shape[0]
    nb, gq, rows, ni = _attn_geometry(length)
    out_shape = (length, dil * GROUP_WIDTH)

    def body(qc_ref, qn_ref, kc_ref, kp_ref, vc_ref, vp_ref, doc_ref, don_ref, oc_ref, on_ref, lc_ref, ln_ref,
             dq_ref, dk_ref, dv_ref):
        i = pl.program_id(1)
        _, mask_p = _band_masks()
        band, in_cur = _band_mask_pair()
        band_first = jnp.logical_and(band, jnp.logical_or(in_cur, i > 0))
        has_next = i < ni - 1

        last = slice(gq * ATTN_BLOCK, (gq + 1) * ATTN_BLOCK)
        mask_next = jnp.logical_and(mask_p, has_next)

        def rows_of(jj):
            return slice(jj * ATTN_BLOCK, (jj + 1) * ATTN_BLOCK)

        def keys_of(jj):
            return slice(jj * ATTN_BLOCK, (jj + 2) * ATTN_BLOCK)

        heads = []
        for h in range(HEADS_PER_GROUP):
            cols = slice(h * HEAD_DIM, (h + 1) * HEAD_DIM)
            hd = dict(
                cols=cols, q_c=qc_ref[:, cols], q_n=qn_ref[:, cols],
                k_all=jnp.concatenate([kp_ref[:, cols], kc_ref[:, cols]], axis=0),
                v_all=jnp.concatenate([vp_ref[:, cols], vc_ref[:, cols]], axis=0),
                do_c=doc_ref[:, cols], do_n=don_ref[:, cols],
                l_c=lc_ref[:, h * LSE_LANES:h * LSE_LANES + 1], l_n=ln_ref[:, h * LSE_LANES:h * LSE_LANES + 1],
            )
            hd["dl_c"] = jnp.sum(hd["do_c"].astype(F32) * oc_ref[:, cols].astype(F32), axis=-1, keepdims=True)
            hd["dl_n"] = jnp.sum(hd["do_n"].astype(F32) * on_ref[:, cols].astype(F32), axis=-1, keepdims=True)
            hd["s"] = [_dot(hd["q_c"][rows_of(jj)], hd["k_all"][keys_of(jj)], "nt") for jj in range(gq)]
            hd["dp"] = [_dot(hd["do_c"][rows_of(jj)], hd["v_all"][keys_of(jj)], "nt") for jj in range(gq)]
            hd["s"].append(_dot(hd["q_n"], hd["k_all"][last], "nt"))
            hd["dp"].append(_dot(hd["do_n"], hd["v_all"][last], "nt"))
            heads.append(hd)
        for hd in heads:
            hd["p"], hd["ds"] = [], []
            for jj in range(gq + 1):
                if jj < gq:
                    mask, l_col, delta = (band_first if jj == 0 else band), hd["l_c"][rows_of(jj)], hd["dl_c"][rows_of(jj)]
                else:
                    mask, l_col, delta = mask_next, hd["l_n"], hd["dl_n"]
                p = jnp.where(mask, jnp.exp(hd["s"][jj] - l_col), 0.0)
                hd["p"].append(p.astype(BF16))
                hd["ds"].append((p * (hd["dp"][jj] - delta)).astype(BF16))
        for hd in heads:
            cols = hd["cols"]
            dk_blocks, dv_blocks = [None] * (gq + 1), [None] * (gq + 1)

            def add(lst, idx, val):
                lst[idx] = val if lst[idx] is None else lst[idx] + val

            for jj in range(gq):
                qb, dob = hd["q_c"][rows_of(jj)], hd["do_c"][rows_of(jj)]
                dq_ref[rows_of(jj), cols] = _dot(hd["ds"][jj], hd["k_all"][keys_of(jj)], "nn").astype(dq_ref.dtype)
                dk2 = _dot(hd["ds"][jj], qb, "tn")
                dv2 = _dot(hd["p"][jj], dob, "tn")
                add(dk_blocks, jj, dk2[:ATTN_BLOCK])
                add(dk_blocks, jj + 1, dk2[ATTN_BLOCK:])
                add(dv_blocks, jj, dv2[:ATTN_BLOCK])
                add(dv_blocks, jj + 1, dv2[ATTN_BLOCK:])
            add(dk_blocks, gq, _dot(hd["ds"][gq], hd["q_n"], "tn"))
            add(dv_blocks, gq, _dot(hd["p"][gq], hd["do_n"], "tn"))
            for jj in range(gq):
                dk_ref[rows_of(jj), cols] = dk_blocks[jj + 1].astype(dk_ref.dtype)
                dv_ref[rows_of(jj), cols] = dv_blocks[jj + 1].astype(dv_ref.dtype)

    def cur(c):
        return pl.BlockSpec((rows, GROUP_WIDTH), lambda r, i: (i, r + c))

    def prev(c):
        return pl.BlockSpec((ATTN_BLOCK, GROUP_WIDTH), lambda r, i: (jnp.maximum(i * gq - 1, 0), r + c))

    def nxt(c):
        return pl.BlockSpec((ATTN_BLOCK, GROUP_WIDTH), lambda r, i: (jnp.minimum((i + 1) * gq, nb - 1), r + c))

    cq, ck, cv = cols3
    lse_cur = pl.BlockSpec((rows, LSE_WIDTH), lambda r, i: (i, r))
    lse_next = pl.BlockSpec((ATTN_BLOCK, LSE_WIDTH), lambda r, i: (jnp.minimum((i + 1) * gq, nb - 1), r))
    return pl.pallas_call(
        body, name=f"attn_bwd_d{dil}", grid=(dil, ni),
        in_specs=[cur(cq), nxt(cq), cur(ck), prev(ck), cur(cv), prev(cv), cur(0), nxt(0), cur(0), nxt(0), lse_cur, lse_next],
        out_specs=[cur(0), cur(0), cur(0)],
        out_shape=[jax.ShapeDtypeStruct(out_shape, BF16)] * 3,
        compiler_params=_params(2),
    )(qv, qv, kv, kv, vv, vv, dov, dov, ov, ov, lv, lv)


DILATED = tuple((g, d) for g, d in enumerate(GROUP_DILATIONS) if d > 1)


def _spread(scr, slot, tile, out_ref, dil, col, width=GROUP_WIDTH):
    tm = tile.shape[0]
    buf = scr.at[slot]
    buf[...] = tile
    for r in range(dil):
        c0 = r * width + col
        out_ref[:, c0:c0 + HEAD_DIM] = buf[pl.ds(r, tm // dil, stride=dil), :].astype(out_ref.dtype)


def _collect(scr, slot, in_ref, dil, col, width=GROUP_WIDTH):
    tm = scr.shape[1]
    buf = scr.at[slot]
    for r in range(dil):
        c0 = r * width + col
        buf[pl.ds(r, tm // dil, stride=dil), :] = in_ref[:, c0:c0 + HEAD_DIM].astype(F32)
    return buf[...]


def _view_spec(tm, dil, width=GROUP_WIDTH):
    return pl.BlockSpec((tm // dil, dil * width), lambda i: (i, 0))


def _view_shape(s, dil, dtype, width=GROUP_WIDTH):
    return jax.ShapeDtypeStruct((s // dil, dil * width), dtype)


def _qkv_layout(z, tabs, tm):
    s = z.shape[0]
    scale = 1.0 / math.sqrt(HEAD_DIM)
    qkv_width = 3 * N_GROUPS * GROUP_WIDTH

    def body(z_ref, cs_ref, lo_ref, hi_ref, qk0_ref, *rest):
        views, scr = rest[:-1], rest[-1]
        tabs_ = (cs_ref[...], lo_ref[...], hi_ref[...])
        for part in range(3):
            for g, dil in enumerate(GROUP_DILATIONS):
                if part == 2 and dil == 1:
                    continue
                for h in range(HEADS_PER_GROUP):
                    col = part * N_GROUPS * GROUP_WIDTH + g * GROUP_WIDTH + h * HEAD_DIM
                    t = z_ref[:, col:col + HEAD_DIM].astype(F32)
                    if part < 2:
                        t = _rope(t, *tabs_)
                    if part == 0:
                        t = t * scale
                    if dil == 1:
                        c0 = part * GROUP_WIDTH + h * HEAD_DIM
                        qk0_ref[:, c0:c0 + HEAD_DIM] = t.astype(BF16)
                    else:
                        out = views[3 * [gg for gg, _ in DILATED].index(g) + part]
                        _spread(scr, h, t, out, dil, h * HEAD_DIM)

    row = lambda i: (i, 0)
    out_shape = [jax.ShapeDtypeStruct((s, 2 * GROUP_WIDTH), BF16)]
    out_specs = [pl.BlockSpec((tm, 2 * GROUP_WIDTH), row)]
    for _, dil in DILATED:
        out_shape += [_view_shape(s, dil, BF16)] * 3
        out_specs += [_view_spec(tm, dil)] * 3
    res = pl.pallas_call(
        body, name="qkv_layout", grid=(s // tm,),
        in_specs=[pl.BlockSpec((tm, qkv_width), row)] + [pl.BlockSpec((tm, HEAD_DIM), row)] * 3,
        out_specs=out_specs, out_shape=out_shape,
        scratch_shapes=[pltpu.VMEM((HEADS_PER_GROUP, tm, HEAD_DIM), F32)], compiler_params=_params(1),
    )(z, *tabs)
    return res[0], [tuple(res[1 + 3 * n:4 + 3 * n]) for n in range(len(DILATED))]


def _attn_merge(o0, l0, dilated, tm):
    s = o0.shape[0]
    n_d = len(DILATED)

    def body(*refs):
        o0_ref, l0_ref = refs[:2]
        in_views = refs[2:2 + 2 * n_d]
        attn_ref, lse_ref = refs[2 + 2 * n_d:4 + 2 * n_d]
        out_views = refs[4 + 2 * n_d:4 + 4 * n_d]
        scr = refs[-1]
        l_rows = [l0_ref[...]] + [_collect(scr, n, in_views[2 * n + 1], dil, 0, LSE_WIDTH) for n, (_, dil) in enumerate(DILATED)]
        lse_heads = []
        for h in range(HEADS_PER_GROUP):
            cols = slice(h * HEAD_DIM, (h + 1) * HEAD_DIM)
            os_ = [o0_ref[:, cols].astype(F32)]
            for n, (_, dil) in enumerate(DILATED):
                os_.append(_collect(scr, n_d + n, in_views[2 * n], dil, h * HEAD_DIM))
            ls_ = [lr[:, h * LSE_LANES:h * LSE_LANES + 1] for lr in l_rows]
            m = ls_[0]
            for l_ in ls_[1:]:
                m = jnp.maximum(m, l_)
            es = [jnp.exp(l_ - m) for l_ in ls_]
            den = es[0]
            num = es[0] * os_[0]
            for e, o in zip(es[1:], os_[1:]):
                den = den + e
                num = num + e * o
            attn = num * (1.0 / den)
            lse_heads.append(jnp.broadcast_to(m + jnp.log(den), (tm, LSE_LANES)))
            attn_ref[:, cols] = attn.astype(BF16)
            for n, (_, dil) in enumerate(DILATED):
                _spread(scr, 2 * n_d, attn, out_views[2 * n], dil, h * HEAD_DIM)
        lse = jnp.concatenate(lse_heads, axis=1)
        lse_ref[...] = lse
        for n, (_, dil) in enumerate(DILATED):
            _spread(scr, 2 * n_d, lse, out_views[2 * n + 1], dil, 0, LSE_WIDTH)

    row = lambda i: (i, 0)
    nat = pl.BlockSpec((tm, GROUP_WIDTH), row)
    nat_l = pl.BlockSpec((tm, LSE_WIDTH), row)
    in_specs = [nat, nat_l]
    args = [o0, l0]
    out_specs = [nat, nat_l]
    out_shape = [jax.ShapeDtypeStruct((s, GROUP_WIDTH), BF16), jax.ShapeDtypeStruct((s, LSE_WIDTH), F32)]
    for (_, dil), (ov, lv) in zip(DILATED, dilated):
        in_specs += [_view_spec(tm, dil), _view_spec(tm, dil, LSE_WIDTH)]
        args += [ov, lv]
        out_specs += [_view_spec(tm, dil), _view_spec(tm, dil, LSE_WIDTH)]
        out_shape += [_view_shape(s, dil, BF16), _view_shape(s, dil, F32, LSE_WIDTH)]
    res = pl.pallas_call(
        body, name="attn_merge", grid=(s // tm,), in_specs=in_specs, out_specs=out_specs, out_shape=out_shape,
        scratch_shapes=[pltpu.VMEM((2 * n_d + 1, tm, HEAD_DIM), F32)], compiler_params=_params(1),
    )(*args)
    return res[0], res[1], [tuple(res[2 + 2 * n:4 + 2 * n]) for n in range(n_d)]


def _to_views(a, tm):
    s = a.shape[0]

    def body(a_ref, *rest):
        outs, scr = rest[:-1], rest[-1]
        for h in range(HEADS_PER_GROUP):
            t = a_ref[:, h * HEAD_DIM:(h + 1) * HEAD_DIM].astype(F32)
            for n, (_, dil) in enumerate(DILATED):
                _spread(scr, n, t, outs[n], dil, h * HEAD_DIM)

    return pl.pallas_call(
        body, name="to_views", grid=(s // tm,), in_specs=[pl.BlockSpec((tm, GROUP_WIDTH), lambda i: (i, 0))],
        out_specs=[_view_spec(tm, dil) for _, dil in DILATED], out_shape=[_view_shape(s, dil, BF16) for _, dil in DILATED],
        scratch_shapes=[pltpu.VMEM((len(DILATED), tm, HEAD_DIM), F32)], compiler_params=_params(1),
    )(a)


def _dz_layout(grads, du, dga, dgs, tabs, tm, comm=None):
    s = du.shape[0]
    scale = 1.0 / math.sqrt(HEAD_DIM)
    n_steps = s // tm
    c_ins = list(comm["ins"]) if comm else []
    c_outs = list(comm["outs"]) if comm else []
    c_sems = list(comm["sems"]) if comm else []
    n_fixed = 3 * N_GROUPS + 6

    def body(*refs):
        g_refs = refs[:3 * N_GROUPS]
        du_ref, dga_ref, dgs_ref, cs_ref, lo_ref, hi_ref = refs[3 * N_GROUPS:n_fixed]
        comm_in = refs[n_fixed:n_fixed + len(c_ins)]
        dz_ref = refs[n_fixed + len(c_ins)]
        comm_out = refs[n_fixed + len(c_ins) + 1:n_fixed + len(c_ins) + 1 + len(c_outs)]
        scr = refs[n_fixed + len(c_ins) + 1 + len(c_outs)]
        sems = refs[n_fixed + len(c_ins) + 2 + len(c_outs):]
        if comm:
            @pl.when(pl.program_id(0) == 0)
            def _():
                comm["start"](comm_in, comm_out, sems)

            @pl.when(pl.program_id(0) == n_steps - 1)
            def _():
                comm["finish"](comm_in, comm_out, sems)

        tabs_ = (cs_ref[...], lo_ref[...], hi_ref[...])
        for part in range(3):
            for g, dil in enumerate(GROUP_DILATIONS):
                src = g_refs[3 * g + part]
                for h in range(HEADS_PER_GROUP):
                    if dil == 1:
                        t = src[:, h * HEAD_DIM:(h + 1) * HEAD_DIM].astype(F32)
                    else:
                        t = _collect(scr, h, src, dil, h * HEAD_DIM)
                    if part < 2:
                        t = _rope_t(t, *tabs_)
                    if part == 0:
                        t = t * scale
                    col = part * N_GROUPS * GROUP_WIDTH + g * GROUP_WIDTH + h * HEAD_DIM
                    dz_ref[:, col:col + HEAD_DIM] = t.astype(BF16)
        dz_ref[:, COL_U:COL_GA] = du_ref[...]
        dz_ref[:, COL_GA:COL_GS] = dga_ref[...]
        dz_ref[:, COL_GS:IN_WIDTH] = dgs_ref[...]

    row = lambda i: (i, 0)
    in_specs, args = [], []
    for (g, dil), trio in zip(enumerate(GROUP_DILATIONS), grads):
        in_specs += [pl.BlockSpec((tm, GROUP_WIDTH), row) if dil == 1 else _view_spec(tm, dil)] * 3
        args += list(trio)
    in_specs += [pl.BlockSpec((tm, SSM_WIDTH), row), pl.BlockSpec((tm, D_MODEL), row), pl.BlockSpec((tm, D_MODEL), row)]
    in_specs += [pl.BlockSpec((tm, HEAD_DIM), row)] * 3
    in_specs += [pl.BlockSpec(memory_space=pl.ANY)] * len(c_ins)
    res = pl.pallas_call(
        body, name="dz_layout", grid=(n_steps,), in_specs=in_specs,
        out_specs=[pl.BlockSpec((tm, IN_WIDTH), row)] + [pl.BlockSpec(memory_space=pl.ANY)] * len(c_outs),
        out_shape=[jax.ShapeDtypeStruct((s, IN_WIDTH), BF16)] + c_outs,
        scratch_shapes=[pltpu.VMEM((HEADS_PER_GROUP, tm, HEAD_DIM), F32)] + [pltpu.SemaphoreType.DMA((n,)) for n in c_sems],
        compiler_params=_params(1),
    )(*args, du, dga, dgs, *tabs, *c_ins)
    return res[0], list(res[1:])


def _discretise(a_re, a_im, log_dt, bt_re, bt_im):
    dt = jnp.exp(log_dt)
    mag = jnp.exp(a_re * dt)
    bar_re = mag * jnp.cos(a_im * dt)
    bar_im = mag * jnp.sin(a_im * dt)
    nr = bar_re - 1.0
    ni = bar_im
    den = a_re * a_re + a_im * a_im
    z_re = (nr * a_re + ni * a_im) / den
    z_im = (ni * a_re - nr * a_im) / den
    bb_re = z_re[:, None, :] * bt_re - z_im[:, None, :] * bt_im
    bb_im = z_re[:, None, :] * bt_im + z_im[:, None, :] * bt_re
    return bar_re, bar_im, bb_re, bb_im


def _ssm_prep(a_re, a_im, log_dt, bt_re, bt_im):
    def body(ar, ai, ld, br, bi, o_lr, o_li, o_br, o_bi):
        lr, li, bbr, bbi = _discretise(ar[...], ai[...], ld[...], br[...], bi[...])
        o_lr[...] = lr
        o_li[...] = li
        o_br[...] = bbr
        o_bi[...] = bbi

    sm = jax.ShapeDtypeStruct((SSM_GROUPS, SSM_STATE), F32)
    bg = jax.ShapeDtypeStruct((SSM_GROUPS, SSM_GROUP, SSM_STATE), F32)
    return pl.pallas_call(body, name="ssm_prep", out_shape=[sm, sm, bg, bg])(a_re, a_im, log_dt, bt_re, bt_im)


def _ssm_param_bwd(a_re, a_im, log_dt, bt_re, bt_im, d_lr, d_li, d_bbr, d_bbi):
    def body(ar, ai, ld, br, bi, g_lr, g_li, g_br, g_bi, o_ar, o_ai, o_ld, o_br, o_bi):
        _, vjp = jax.vjp(_discretise, ar[...], ai[...], ld[...], br[...], bi[...])
        d_ar, d_ai, d_ld, d_br, d_bi = vjp((g_lr[...], g_li[...], g_br[...], g_bi[...]))
        o_ar[...] = d_ar
        o_ai[...] = d_ai
        o_ld[...] = d_ld
        o_br[...] = d_br
        o_bi[...] = d_bi

    sm = jax.ShapeDtypeStruct((SSM_GROUPS, SSM_STATE), F32)
    col = jax.ShapeDtypeStruct((SSM_GROUPS, 1), F32)
    bg = jax.ShapeDtypeStruct((SSM_GROUPS, SSM_GROUP, SSM_STATE), F32)
    return pl.pallas_call(body, name="ssm_param_bwd", out_shape=[sm, sm, col, bg, bg])(
        a_re, a_im, log_dt, bt_re, bt_im, d_lr, d_li, d_bbr, d_bbi)


def _block_diag(t, rows_per, cols_per):
    t4 = t.reshape(SSM_SUPER, 8, rows_per, cols_per)
    eye = jnp.eye(8, dtype=t.dtype)
    return jnp.einsum("bgrc,gh->bgrhc", t4, eye).reshape(SSM_SUPER, 8 * rows_per, 8 * cols_per)


def _block_diag_t(dense, rows_per, cols_per):
    t = dense.reshape(SSM_SUPER, 8, rows_per, 8, cols_per)
    eye = jnp.eye(8, dtype=dense.dtype)
    return jnp.einsum("bgrhc,gh->bgrc", t, eye).reshape(SSM_GROUPS, rows_per, cols_per)


def _gelu(v):
    c = math.sqrt(2.0 / math.pi)
    return 0.5 * v * (1.0 + jnp.tanh(c * (v + 0.044715 * v * v * v)))


def _gelu_grad(v):
    c = math.sqrt(2.0 / math.pi)
    t = jnp.tanh(c * (v + 0.044715 * v * v * v))
    return 0.5 * (1.0 + t) + 0.5 * v * (1.0 - t * t) * c * (1.0 + 3.0 * 0.044715 * v * v)


SUB = 8


SCAN_STEPS = (1, 2, 4)
N_SCAN_TABLES = 2 + 2 * len(SCAN_STEPS)


def _scan_tables(tab_ref, lam_re, lam_im, reverse, conj):
    lr = lam_re
    li = -lam_im if conj else lam_im
    powers = [(lr, li)]
    for _ in range(SUB - 1):
        pr, pi = powers[-1]
        powers.append((pr * lr - pi * li, pr * li + pi * lr))
    row = lax.broadcasted_iota(jnp.int32, (SUB, N_STATE), 0)
    if reverse:
        row = SUB - 1 - row
    wide = lambda v: jnp.broadcast_to(v, (SUB, N_STATE))
    p_re = jnp.zeros((SUB, N_STATE), F32)
    p_im = jnp.zeros((SUB, N_STATE), F32)
    for j in range(SUB):
        p_re = jnp.where(row == j, wide(powers[j][0]), p_re)
        p_im = jnp.where(row == j, wide(powers[j][1]), p_im)
    tab_ref[0] = p_re
    tab_ref[1] = p_im
    for idx, k in enumerate(SCAN_STEPS):
        tab_ref[2 + 2 * idx] = jnp.where(row >= k, wide(powers[k - 1][0]), 0.0)
        tab_ref[3 + 2 * idx] = jnp.where(row >= k, wide(powers[k - 1][1]), 0.0)


def _scan_rows(g_re_ref, g_im_ref, tab_ref, carry, n_rows, reverse):
    last = 0 if reverse else SUB - 1

    def tile_step(tt, state):
        cr, ci = state
        t8 = (n_rows // SUB - 1 - tt) if reverse else tt
        start = pl.multiple_of(t8 * SUB, SUB)
        xr = g_re_ref[pl.ds(start, SUB), :]
        xi = g_im_ref[pl.ds(start, SUB), :]
        for idx, k in enumerate(SCAN_STEPS):
            mr = tab_ref[2 + 2 * idx]
            mi = tab_ref[3 + 2 * idx]
            shift = SUB - k if reverse else k
            sr = pltpu.roll(xr, shift, 0)
            si = pltpu.roll(xi, shift, 0)
            xr, xi = xr + (mr * sr - mi * si), xi + (mr * si + mi * sr)
        pr = tab_ref[0]
        pi = tab_ref[1]
        xr, xi = xr + (pr * cr - pi * ci), xi + (pr * ci + pi * cr)
        g_re_ref[pl.ds(start, SUB), :] = xr
        g_im_ref[pl.ds(start, SUB), :] = xi
        return (jnp.broadcast_to(xr[last:last + 1, :], (SUB, N_STATE)),
                jnp.broadcast_to(xi[last:last + 1, :], (SUB, N_STATE)))

    return lax.fori_loop(0, n_rows // SUB, tile_step, carry)


def _ssm_fwd(z, b_re, b_im, c_re, c_im, lam_re, lam_im, d_skip, chunk):
    s = z.shape[0]

    def body(u_ref, bre, bim, cre, cim, lre, lim, dsk, hre_ref, him_ref, ys_ref, yg_ref, car_re, car_im, tabs):
        i = pl.program_id(0)

        @pl.when(i == 0)
        def _():
            car_re[...] = jnp.zeros_like(car_re)
            car_im[...] = jnp.zeros_like(car_im)
            _scan_tables(tabs, lre[...], lim[...], False, False)

        u = u_ref[...]
        for b in range(SSM_SUPER):
            ub = u[:, b * 128:(b + 1) * 128]
            st = slice(b * 512, (b + 1) * 512)
            hre_ref[:, st] = _dot(ub, bre[b], "nn")
            him_ref[:, st] = _dot(ub, bim[b], "nn")
        sr, si = _scan_rows(hre_ref, him_ref, tabs, (car_re[...], car_im[...]), chunk, False)
        car_re[...] = sr
        car_im[...] = si
        uf = u.astype(F32)
        for b in range(SSM_SUPER):
            st = slice(b * 512, (b + 1) * 512)
            ch = slice(b * 128, (b + 1) * 128)
            y = _dot(hre_ref[:, st].astype(BF16), cre[b], "nn") - _dot(him_ref[:, st].astype(BF16), cim[b], "nn")
            y = y + dsk[:, ch] * uf[:, ch]
            ys_ref[:, ch] = y
            yg_ref[:, ch] = _gelu(y).astype(BF16)

    full3 = lambda i: (0, 0, 0)
    full2 = lambda i: (0, 0)
    row = lambda i: (i, 0)
    u_col = COL_U // SSM_WIDTH
    return pl.pallas_call(
        body, name="ssm_fwd", grid=(s // chunk,),
        in_specs=[pl.BlockSpec((chunk, SSM_WIDTH), lambda i: (i, u_col)),
                  pl.BlockSpec((SSM_SUPER, 128, 512), full3), pl.BlockSpec((SSM_SUPER, 128, 512), full3),
                  pl.BlockSpec((SSM_SUPER, 512, 128), full3), pl.BlockSpec((SSM_SUPER, 512, 128), full3),
                  pl.BlockSpec((1, N_STATE), full2), pl.BlockSpec((1, N_STATE), full2), pl.BlockSpec((1, SSM_WIDTH), full2)],
        out_specs=[pl.BlockSpec((chunk, N_STATE), row), pl.BlockSpec((chunk, N_STATE), row),
                   pl.BlockSpec((chunk, SSM_WIDTH), row), pl.BlockSpec((chunk, SSM_WIDTH), row)],
        out_shape=[jax.ShapeDtypeStruct((s, N_STATE), F32), jax.ShapeDtypeStruct((s, N_STATE), F32),
                   jax.ShapeDtypeStruct((s, SSM_WIDTH), F32), jax.ShapeDtypeStruct((s, SSM_WIDTH), BF16)],
        scratch_shapes=[pltpu.VMEM((SUB, N_STATE), F32), pltpu.VMEM((SUB, N_STATE), F32),
                        pltpu.VMEM((N_SCAN_TABLES, SUB, N_STATE), F32)],
        compiler_params=_params(1),
    )(z, b_re, b_im, c_re, c_im, lam_re, lam_im, d_skip)


def _ssm_bwd(dys, z, h_re, h_im, b_re, b_im, c_re, c_im, lam_re, lam_im, d_skip, chunk):
    s = z.shape[0]
    n_chunks = s // chunk

    def body(dy_ref, u_ref, hre_ref, him_ref, hpr_ref, hpi_ref, bre, bim, cre, cim, lre, lim, dsk,
             du_ref, dlr_ref, dli_ref, dbr_ref, dbi_ref, dcr_ref, dci_ref, dd_ref, are, aim, car_re, car_im, tabs):
        i = pl.program_id(0)
        n = n_chunks - 1 - i

        @pl.when(i == 0)
        def _():
            car_re[...] = jnp.zeros_like(car_re)
            car_im[...] = jnp.zeros_like(car_im)
            _scan_tables(tabs, lre[...], lim[...], True, True)
            for r in (dlr_ref, dli_ref, dbr_ref, dbi_ref, dcr_ref, dci_ref, dd_ref):
                r[...] = jnp.zeros_like(r)

        dy = dy_ref[...]
        dyb = dy.astype(BF16)
        u = u_ref[...]
        for b in range(SSM_SUPER):
            ch = slice(b * 128, (b + 1) * 128)
            st = slice(b * 512, (b + 1) * 512)
            are[:, st] = _dot(dyb[:, ch], cre[b], "nt")
            aim[:, st] = -_dot(dyb[:, ch], cim[b], "nt")
        sr, si = _scan_rows(are, aim, tabs, (car_re[...], car_im[...]), chunk, True)
        car_re[...] = sr
        car_im[...] = si
        row_id = lax.broadcasted_iota(jnp.int32, (chunk, N_STATE), 0)
        top_scale = jnp.where(n > 0, 1.0, 0.0)
        h_r = hre_ref[...]
        h_i = him_ref[...]
        hp_r = jnp.where(row_id == 0, hpr_ref[SUB - 1:SUB, :] * top_scale, pltpu.roll(h_r, 1, 0))
        hp_i = jnp.where(row_id == 0, hpi_ref[SUB - 1:SUB, :] * top_scale, pltpu.roll(h_i, 1, 0))
        a_r = are[...]
        a_i = aim[...]
        dlr_ref[...] += jnp.sum(a_r * hp_r + a_i * hp_i, axis=0, keepdims=True)
        dli_ref[...] += jnp.sum(a_i * hp_r - a_r * hp_i, axis=0, keepdims=True)
        dd_ref[...] += jnp.sum(dy * u.astype(F32), axis=0, keepdims=True)
        a_rb = a_r.astype(BF16)
        a_ib = a_i.astype(BF16)
        h_rb = h_r.astype(BF16)
        h_ib = h_i.astype(BF16)
        for b in range(SSM_SUPER):
            ch = slice(b * 128, (b + 1) * 128)
            st = slice(b * 512, (b + 1) * 512)
            dbr_ref[b] += _dot(u[:, ch], a_rb[:, st], "tn")
            dbi_ref[b] += _dot(u[:, ch], a_ib[:, st], "tn")
            dcr_ref[b] += _dot(h_rb[:, st], dyb[:, ch], "tn")
            dci_ref[b] += -_dot(h_ib[:, st], dyb[:, ch], "tn")
            du = _dot(a_rb[:, st], bre[b], "nt") + _dot(a_ib[:, st], bim[b], "nt") + dsk[:, ch] * dy[:, ch]
            du_ref[:, ch] = du.astype(du_ref.dtype)

    full3 = lambda i: (0, 0, 0)
    full2 = lambda i: (0, 0)
    rev = lambda i: (n_chunks - 1 - i, 0)
    above = lambda i: (jnp.maximum((n_chunks - 1 - i) * (chunk // SUB) - 1, 0), 0)
    u_col = COL_U // SSM_WIDTH
    b_spec = pl.BlockSpec((SSM_SUPER, 128, 512), full3)
    c_spec = pl.BlockSpec((SSM_SUPER, 512, 128), full3)
    vec = pl.BlockSpec((1, N_STATE), full2)
    return pl.pallas_call(
        body, name="ssm_bwd", grid=(n_chunks,),
        in_specs=[pl.BlockSpec((chunk, SSM_WIDTH), rev),
                  pl.BlockSpec((chunk, SSM_WIDTH), lambda i: (n_chunks - 1 - i, u_col)),
                  pl.BlockSpec((chunk, N_STATE), rev), pl.BlockSpec((chunk, N_STATE), rev),
                  pl.BlockSpec((SUB, N_STATE), above), pl.BlockSpec((SUB, N_STATE), above),
                  b_spec, b_spec, c_spec, c_spec, vec, vec, pl.BlockSpec((1, SSM_WIDTH), full2)],
        out_specs=[pl.BlockSpec((chunk, SSM_WIDTH), rev), vec, vec, b_spec, b_spec, c_spec, c_spec,
                   pl.BlockSpec((1, SSM_WIDTH), full2)],
        out_shape=[jax.ShapeDtypeStruct((s, SSM_WIDTH), BF16),
                   jax.ShapeDtypeStruct((1, N_STATE), F32), jax.ShapeDtypeStruct((1, N_STATE), F32),
                   jax.ShapeDtypeStruct((SSM_SUPER, 128, 512), F32), jax.ShapeDtypeStruct((SSM_SUPER, 128, 512), F32),
                   jax.ShapeDtypeStruct((SSM_SUPER, 512, 128), F32), jax.ShapeDtypeStruct((SSM_SUPER, 512, 128), F32),
                   jax.ShapeDtypeStruct((1, SSM_WIDTH), F32)],
        scratch_shapes=[pltpu.VMEM((chunk, N_STATE), F32), pltpu.VMEM((chunk, N_STATE), F32),
                        pltpu.VMEM((SUB, N_STATE), F32), pltpu.VMEM((SUB, N_STATE), F32),
                        pltpu.VMEM((N_SCAN_TABLES, SUB, N_STATE), F32)],
        compiler_params=_params(1),
    )(dys, z, h_re, h_im, h_re, h_im, b_re, b_im, c_re, c_im, lam_re, lam_im, d_skip)


def _local_step(x, p, pos, tgt, sm, w_in_own, w_in_buf, late_bufs, place):
    s = x.shape[0]
    tm = min(512, s)
    ts = min(2048, s)
    chunk = min(256, s)
    ni = s // tm
    nk = s // ts
    g_mix, g_ffn, g_final = sm["g_mix"], sm["g_ffn"], sm["g_final"]
    rowblk, rowmap = _rows(tm, D_MODEL)
    vec1k = ((1, D_MODEL), lambda *_: (0, 0))

    (n1,) = _ew("rms_mix", (ni,), [(x, rowblk, rowmap), (g_mix, *vec1k)], [((s, D_MODEL), BF16, rowblk, rowmap)],
                lambda pids, h, g: ((_rms_fwd_tile(h, g),), ()))

    half_in = IN_WIDTH // 8
    tmb = min(1024, s)
    nib = s // tmb
    chip_w = IN_WIDTH // N_CHIPS
    w_start, w_forward, w_finish = _gather_stages(1)
    gather_in = dict(ins=[w_in_buf], outs=[jax.ShapeDtypeStruct(w_in_buf.shape, w_in_buf.dtype)], aliased=True,
                     sems=[3] * 4, stages=[(0, w_start), (nib // 2, w_forward), (nib - 1, w_finish)])
    a_rows = lambda i, j, k, pv: (i, 0)
    z_own, w_in_all = _mm("in_proj_own", (nib, 1, 1),
                          [(n1, (tmb, D_MODEL), a_rows, w_in_own, (D_MODEL, chip_w), lambda i, j, k, pv: (0, 0))], "nn",
                          [((s, IN_WIDTH), BF16, (tmb, chip_w), lambda i, j, k, pv: (i, pv[P_CHIP]))],
                          comm=gather_in, place=place)
    w_in = w_in_all.reshape(N_CHIPS, D_MODEL, chip_w)
    n_late = len(late_bufs)
    g_start, g_forward, g_finish = _gather_stages(n_late)
    in_steps = (N_CHIPS - 1) * nib
    gather = dict(ins=late_bufs, outs=[jax.ShapeDtypeStruct(b.shape, b.dtype) for b in late_bufs], aliased=True,
                  sems=[3 * n_late] * 4, stages=[(0, g_start), (in_steps // 2, g_forward), (in_steps - 1, g_finish)])
    other = lambda j, pv: (pv[P_CHIP] + 1 + j) % N_CHIPS
    z, *late = _mm("in_proj", (nib, N_CHIPS - 1, 1),
                   [(n1, (tmb, D_MODEL), a_rows, w_in, (None, D_MODEL, chip_w), lambda i, j, k, pv: (other(j, pv), 0, 0))],
                   "nn", [((s, IN_WIDTH), BF16, (tmb, chip_w), lambda i, j, k, pv: (i, other(j, pv)))], j_outer=True,
                   comm=gather, place=place, fill=z_own)
    w_ap, w_ga, w_gb, w_out, w_fg, w_fu, w_fd, w_pg, w_pp = (
        g.reshape(N_CHIPS, 2 * g.shape[2], g.shape[3]) for g in late)
    w_out2 = w_out.reshape(D_MODEL, D_MODEL)
    w_pg2 = w_pg.reshape(D_MODEL, D_MODEL)

    inv = ROPE_THETA ** (-jnp.arange(ROPE_HALF, dtype=F32) * 2.0 / ROPE_DIM)
    inv_row = jnp.concatenate([inv, inv, jnp.zeros((HEAD_DIM - ROPE_DIM,), F32)]).reshape(1, HEAD_DIM)
    tabs = _rope_tables(pos.astype(F32).reshape(s, 1), inv_row, tm)

    qk0, qkv_views = _qkv_layout(z, tabs, tm)
    v0_col = (2 * N_GROUPS * GROUP_WIDTH) // GROUP_WIDTH
    group_in = [((qk0, qk0, z), (0, 1, v0_col))] + [(trio, (0, 0, 0)) for trio in qkv_views]
    fwd_out = [_attn_fwd(*arrs, dil, cols3) for (arrs, cols3), dil in zip(group_in, GROUP_DILATIONS)]
    attn, lse, merged_views = _attn_merge(fwd_out[0][0], fwd_out[0][1], fwd_out[1:], tm)

    def chip_cols(parts):
        return (jnp.concatenate(parts, axis=1),), ()

    def proj_cols(name, a, width, w):
        blk = (None, width, 256)
        pairs = [(a, (tmb, width), lambda i, j, k: (i, 0), w, blk, lambda i, j, k: (0, 0, 0))]
        pairs += [(None, None, None, w, blk, (lambda i, j, k, q=q: (q, 0, 0))) for q in range(1, N_CHIPS)]
        return _mm(name, (nib, 1, 1), pairs, "nn", [((s, D_MODEL), BF16, (tmb, D_MODEL), lambda i, j, k: (i, 0))],
                   epilogue=chip_cols, sum_pairs=False)[0]

    def proj512(name, a, w):
        return proj_cols(name, a, GROUP_WIDTH, w)

    attn_d = proj512("attn_proj", attn, w_ap)

    bt_re = jnp.transpose(sm["b_re"], (0, 2, 1))
    bt_im = jnp.transpose(sm["b_im"], (0, 2, 1))
    log_dt_col = sm["log_dt"].reshape(SSM_GROUPS, 1)
    lam_re, lam_im, bbt_re, bbt_im = _ssm_prep(sm["a_re"], sm["a_im"], log_dt_col, bt_re, bt_im)
    b_re_m = _block_diag(bbt_re, SSM_GROUP, SSM_STATE).astype(BF16)
    b_im_m = _block_diag(bbt_im, SSM_GROUP, SSM_STATE).astype(BF16)
    c_re_m = _block_diag(jnp.transpose(sm["c_re"], (0, 2, 1)), SSM_STATE, SSM_GROUP).astype(BF16)
    c_im_m = _block_diag(jnp.transpose(sm["c_im"], (0, 2, 1)), SSM_STATE, SSM_GROUP).astype(BF16)
    lam_re_row = lam_re.reshape(1, N_STATE)
    lam_im_row = lam_im.reshape(1, N_STATE)
    d_skip_row = sm["d_skip"].reshape(1, SSM_WIDTH)
    h_re, h_im, ys, yg = _ssm_fwd(z, b_re_m, b_im_m, c_re_m, c_im_m, lam_re_row, lam_im_row, d_skip_row, chunk)

    pa = proj512("glu_a", yg, w_ga)
    pb = proj512("glu_b", yg, w_gb)

    ga_blk = ((tm, D_MODEL), lambda i: (i, COL_GA // D_MODEL))
    gs_blk = ((tm, D_MODEL), lambda i: (i, COL_GS // D_MODEL))

    def mix_fn(pids, ga, gs, ad, a, b):
        ga, gs, ad, a, b = (t.astype(F32) for t in (ga, gs, ad, a, b))
        return (_sig(ga) * ad + _sig(gs) * (a * _sig(b)),), ()

    (mix,) = _ew("gate_mix", (ni,), [(z, *ga_blk), (z, *gs_blk), (attn_d, rowblk, rowmap), (pa, rowblk, rowmap),
                                     (pb, rowblk, rowmap)], [((s, D_MODEL), BF16, rowblk, rowmap)], mix_fn)

    def out_epi(acc, xr, g):
        h1 = acc + xr
        return (h1, _rms_fwd_tile(h1, g)), ()

    m3 = lambda i, j, k: (i, 0)
    w3 = lambda i, j, k: (0, 0)
    h1, n2 = _mm("out_proj", (nib, 1, 1), [(mix, (tmb, D_MODEL), m3, w_out2, (D_MODEL, D_MODEL), w3)], "nn",
                 [((s, D_MODEL), F32, (tmb, D_MODEL), m3), ((s, D_MODEL), BF16, (tmb, D_MODEL), m3)],
                 epilogue=out_epi, extras=[(x, (tmb, D_MODEL), m3), (g_ffn, (1, D_MODEL), w3)])

    ffq = (None, tm, D_FF_Q)
    ffq_map = lambda i, j, k: (j, i, 0)

    def ffn_in_epi(parts):
        gts, ups = parts[0::2], parts[1::2]
        acts = [gt * _sig(gt) * u_ for gt, u_ in zip(gts, ups)]
        return (jnp.stack(gts, axis=0), jnp.stack(ups, axis=0), jnp.stack(acts, axis=0)), ()

    w_ffq = (None, D_MODEL, D_FF_Q)
    ff_pairs = []
    for q in range(N_CHIPS):
        blk_q = lambda i, j, k, q=q: (q, 0, 0)
        ff_pairs.append((n2, (tm, D_MODEL), m3, w_fg, w_ffq, blk_q) if q == 0 else (None, None, None, w_fg, w_ffq, blk_q))
        ff_pairs.append((None, None, None, w_fu, w_ffq, blk_q))
    ff_all = (N_CHIPS, tm, D_FF_Q)
    ff_all_map = lambda i, j, k: (0, i, 0)
    gate, up, act = _mm("ffn_gate_up", (ni, 1, 1), ff_pairs, "nn",
                        [((N_CHIPS, s, D_FF_Q), BF16, ff_all, ff_all_map)] * 3, epilogue=ffn_in_epi,
                        sum_pairs=False, resident_b=True)

    (h2,) = _mm("ffn_down", (nib, 1, 1),
                [(act, (None, tmb, D_FF_Q), (lambda i, j, k, q=q: (q, i, 0)), w_fd, (None, D_FF_Q, D_MODEL),
                  (lambda i, j, k, q=q: (q, 0, 0))) for q in range(N_CHIPS)], "nn",
                [((s, D_MODEL), F32, (tmb, D_MODEL), m3)], epilogue=lambda acc, hr: ((acc + hr,), ()),
                extras=[(h1, (tmb, D_MODEL), m3)])

    pp = proj_cols("ple_proj", p, PLE_DIM, w_pp)

    def ple_head_epi(acc, hr, ppr, t, g):
        sg = _sig(acc)
        ppf = ppr.astype(F32)
        h = hr + sg * ppf
        r = lax.rsqrt(jnp.mean(h * h, axis=-1, keepdims=True) + EPS)
        hhat = h * r
        diff = hhat * g - t
        loss = 0.5 * jnp.sum(jnp.mean(diff * diff, axis=-1, keepdims=True))
        dy = diff * (1.0 / D_MODEL)
        gy = dy * g
        dh = r * (gy - hhat * jnp.mean(gy * hhat, axis=-1, keepdims=True))
        return ((dh, dh * ppf * sg * (1.0 - sg), dh * sg),
                (jnp.full((SUB, 128), loss, F32), jnp.sum(dy * hhat, axis=0, keepdims=True)))

    tile_row = (tm, D_MODEL)
    dh3, dgl, dpp, loss_acc, dg_final = _mm(
        "ple_gate_head", (ni, 1, 1), [(h2, tile_row, m3, w_pg2, (D_MODEL, D_MODEL), w3)], "nn",
        [((s, D_MODEL), F32, tile_row, m3), ((s, D_MODEL), BF16, tile_row, m3), ((s, D_MODEL), BF16, tile_row, m3)],
        epilogue=ple_head_epi,
        extras=[(h2, tile_row, m3), (pp, tile_row, m3), (tgt, tile_row, m3), (g_final, (1, D_MODEL), w3)],
        acc_outs=[((SUB, 128), F32), ((1, D_MODEL), F32)])

    def wgrad(name, a, a_block, a_imap, b, b_block, b_imap, out_shape, out_block, out_imap, nj, acc_shape):
        return _mm(name, (1, nj, nk), [(a, a_block, a_imap, b, b_block, b_imap)], "tn",
                   [(out_shape, F32, out_block, out_imap)], acc_shape=acc_shape)[0]

    tk0 = lambda i, j, k: (k, 0)
    tkj = lambda i, j, k: (k, j)
    def wgrad_cols(name, a, width, dy_):
        def split(acc):
            return (jnp.stack([acc[:, q * 256:(q + 1) * 256] for q in range(N_CHIPS)], axis=0),), ()

        return _mm(name, (1, 1, nk), [(a, (ts, width), tk0, dy_, (ts, D_MODEL), tk0)], "tn",
                   [((N_CHIPS, width, 256), F32, (N_CHIPS, width, 256), lambda i, j, k: (0, 0, 0))], epilogue=split,
                   acc_shape=(width, D_MODEL))[0]

    d_w_pp = wgrad_cols("d_ple_proj", p, PLE_DIM, dpp)
    d_w_pg = wgrad("d_ple_gate", h2, (ts, D_MODEL), tk0, dgl, (ts, D_MODEL), tk0, (D_MODEL, D_MODEL),
                   (D_MODEL, D_MODEL), w3, 1, (D_MODEL, D_MODEL))

    (dh2,) = _mm("ple_gate_bwd", (nib, 1, 1), [(dgl, (tmb, D_MODEL), m3, w_pg2, (D_MODEL, D_MODEL), w3)], "nt",
                 [((s, D_MODEL), F32, (tmb, D_MODEL), m3)], epilogue=lambda acc, d_: ((acc + d_,), ()),
                 extras=[(dh3, (tmb, D_MODEL), m3)])

    def ffn_bwd_epi(parts, gt_all, u_all):
        dgs_, dus_ = [], []
        for q, dact in enumerate(parts):
            gt, u_ = gt_all[q].astype(F32), u_all[q].astype(F32)
            sg = _sig(gt)
            dgs_.append(dact * u_ * (sg * (1.0 + gt * (1.0 - sg))))
            dus_.append(dact * gt * sg)
        return (jnp.stack(dgs_, axis=0), jnp.stack(dus_, axis=0)), ()

    fd_pairs = [((dh2, (tm, D_MODEL), m3) if q == 0 else (None, None, None))
                + (w_fd, (None, D_FF_Q, D_MODEL), (lambda i, j, k, q=q: (q, 0, 0))) for q in range(N_CHIPS)]
    dgate, dup = _mm("ffn_down_bwd", (ni, 1, 1), fd_pairs, "nt",
                     [((N_CHIPS, s, D_FF_Q), BF16, ff_all, ff_all_map)] * 2, epilogue=ffn_bwd_epi,
                     extras=[(gate, ff_all, ff_all_map), (up, ff_all, ff_all_map)], sum_pairs=False, resident_b=True)

    ffq_t = (None, ts, D_FF_Q)
    ffq_tmap = lambda i, j, k: (j, k, 0)
    blk_j = lambda i, j, k: (j, 0, 0)
    d_w_fd = wgrad("d_ffn_down", act, ffq_t, ffq_tmap, dh2, (ts, D_MODEL), tk0, (N_CHIPS, D_FF_Q, D_MODEL),
                   (None, D_FF_Q, D_MODEL), blk_j, N_CHIPS, (D_FF_Q, D_MODEL))
    d_w_fg = wgrad("d_ffn_gate", n2, (ts, D_MODEL), tk0, dgate, ffq_t, ffq_tmap, (N_CHIPS, D_MODEL, D_FF_Q),
                   (None, D_MODEL, D_FF_Q), blk_j, N_CHIPS, (D_MODEL, D_FF_Q))
    d_w_fu = wgrad("d_ffn_up", n2, (ts, D_MODEL), tk0, dup, ffq_t, ffq_tmap, (N_CHIPS, D_MODEL, D_FF_Q),
                   (None, D_MODEL, D_FF_Q), blk_j, N_CHIPS, (D_MODEL, D_FF_Q))

    def norm_bwd_epi(acc, h, d_res, g):
        dh, dg = _rms_bwd_tile(acc, h, g)
        return (d_res + dh,), (dg,)

    ffq_k = lambda i, j, k: (k, i, 0)
    blk_k = lambda i, j, k: (k, 0, 0)
    fi_pairs = []
    for q in range(N_CHIPS):
        a_q = lambda i, j, k, q=q: (q, i, 0)
        b_q = lambda i, j, k, q=q: (q, 0, 0)
        fi_pairs.append((dgate, ffq, a_q, w_fg, (None, D_MODEL, D_FF_Q), b_q))
        fi_pairs.append((dup, ffq, a_q, w_fu, (None, D_MODEL, D_FF_Q), b_q))
    dh1, dg_ffn = _mm("ffn_in_bwd", (ni, 1, 1), fi_pairs, "nt",
                      [((s, D_MODEL), F32, (tm, D_MODEL), m3)], epilogue=norm_bwd_epi,
                      extras=[(h1, (tm, D_MODEL), m3), (dh2, (tm, D_MODEL), m3), (g_ffn, (1, D_MODEL), w3)],
                      acc_outs=[((1, D_MODEL), F32)], resident_b=True)

    d_w_out = wgrad("d_out_proj", mix, (ts, D_MODEL), tk0, dh1, (ts, D_MODEL), tk0, (D_MODEL, D_MODEL),
                    (D_MODEL, D_MODEL), w3, 1, (D_MODEL, D_MODEL))

    def mix_bwd_epi(dm, ga, gs, ad, a, b):
        ga, gs, ad, a, b = (t.astype(F32) for t in (ga, gs, ad, a, b))
        s_a, s_s, s_b = _sig(ga), _sig(gs), _sig(b)
        d_ssm = dm * s_s
        return (dm * ad * s_a * (1.0 - s_a), dm * (a * s_b) * s_s * (1.0 - s_s), dm * s_a, d_ssm * s_b,
                d_ssm * a * s_b * (1.0 - s_b)), ()

    tile_m = (tm, D_MODEL)
    dga, dgs, dattn_d, dpa, dpb = _mm(
        "out_proj_bwd", (ni, 1, 1), [(dh1, tile_m, m3, w_out2, (D_MODEL, D_MODEL), w3)], "nt",
        [((s, D_MODEL), BF16, tile_m, m3)] * 5, epilogue=mix_bwd_epi,
        extras=[(z, tile_m, lambda i, j, k: (i, COL_GA // D_MODEL)), (z, tile_m, lambda i, j, k: (i, COL_GS // D_MODEL)),
                (attn_d, tile_m, m3), (pa, tile_m, m3), (pb, tile_m, m3)])

    d_w_ap = wgrad_cols("d_attn_proj", attn, GROUP_WIDTH, dattn_d)
    d_w_ga = wgrad_cols("d_glu_a", yg, GROUP_WIDTH, dpa)
    d_w_gb = wgrad_cols("d_glu_b", yg, GROUP_WIDTH, dpb)

    ik = lambda i, j, k: (i, k)

    def cols_bwd(dy_, w):
        return [(dy_, (tmb, 256), (lambda i, j, k, q=q: (i, q)), w, (None, GROUP_WIDTH, 256),
                 (lambda i, j, k, q=q: (q, 0, 0))) for q in range(N_CHIPS)]

    (dattn,) = _mm("attn_proj_bwd", (nib, 1, 1), cols_bwd(dattn_d, w_ap), "nt",
                   [((s, GROUP_WIDTH), BF16, (tmb, GROUP_WIDTH), m3)])

    (dys,) = _mm("glu_bwd", (nib, 1, 1), cols_bwd(dpa, w_ga) + cols_bwd(dpb, w_gb), "nt",
                 [((s, GROUP_WIDTH), F32, (tmb, GROUP_WIDTH), m3)],
                 epilogue=lambda acc, y_: ((acc * _gelu_grad(y_),), ()),
                 extras=[(ys, (tmb, GROUP_WIDTH), m3)])

    du, d_lr, d_li, d_bre, d_bim, d_cre, d_cim, d_dskip = _ssm_bwd(
        dys, z, h_re, h_im, b_re_m, b_im_m, c_re_m, c_im_m, lam_re_row, lam_im_row, d_skip_row, chunk)

    dattn_views = _to_views(dattn, tm)
    bwd_in = [(dattn, attn, lse)] + [(dv_, ov_, lv_) for dv_, (ov_, lv_) in zip(dattn_views, merged_views)]
    qkv_grads = [_attn_bwd(*arrs, *dol, dil, cols3)
                 for (arrs, cols3), dol, dil in zip(group_in, bwd_in, GROUP_DILATIONS)]
    early = [d_w_ap, d_w_ga, d_w_gb, d_w_out.reshape(N_CHIPS, D_MODEL // N_CHIPS, D_MODEL), d_w_fg, d_w_fu, d_w_fd,
             d_w_pg.reshape(N_CHIPS, D_MODEL // N_CHIPS, D_MODEL), d_w_pp]
    early5 = [g.reshape(N_CHIPS, 2, g.shape[1] // 2, g.shape[2]) for g in early]
    n_e = len(early5)
    p_start, p_finish = _pair_exchange_stages(n_e)
    dz, early_theirs = _dz_layout(
        qkv_grads, du, dga, dgs, tabs, tm,
        comm=dict(ins=early5, outs=_pair_exchange_shapes(early5), sems=[N_CHIPS * n_e] * 2, start=p_start, finish=p_finish))
    early_parts = [_pair_sum(g, t, place) for g, t in zip(early5, early_theirs)]

    chip_in = IN_WIDTH // N_CHIPS
    ip_pairs = [(dz, (tm, chip_in), (lambda i, j, k, q=q: (i, q)), w_in, (None, D_MODEL, chip_in),
                 (lambda i, j, k, q=q: (q, 0, 0))) for q in range(N_CHIPS)]
    grad_x, dg_mix = _mm("in_proj_bwd", (ni, 1, 1), ip_pairs, "nt",
                         [((s, D_MODEL), F32, (tm, D_MODEL), m3)], epilogue=norm_bwd_epi,
                         extras=[(x, (tm, D_MODEL), m3), (dh1, (tm, D_MODEL), m3), (g_mix, (1, D_MODEL), w3)],
                         acc_outs=[((1, D_MODEL), F32)], resident_b=True)

    d_bbt_re = _block_diag_t(d_bre, SSM_GROUP, SSM_STATE)
    d_bbt_im = _block_diag_t(d_bim, SSM_GROUP, SSM_STATE)
    d_a_re, d_a_im, d_log_dt, d_bt_re, d_bt_im = _ssm_param_bwd(
        sm["a_re"], sm["a_im"], log_dt_col, bt_re, bt_im,
        d_lr.reshape(SSM_GROUPS, SSM_STATE), d_li.reshape(SSM_GROUPS, SSM_STATE), d_bbt_re, d_bbt_im)
    small = {
        "g_mix": dg_mix, "a_re": d_a_re, "a_im": d_a_im, "log_dt": d_log_dt,
        "b_re": jnp.transpose(d_bt_re, (0, 2, 1)), "b_im": jnp.transpose(d_bt_im, (0, 2, 1)),
        "c_re": jnp.transpose(_block_diag_t(d_cre, SSM_STATE, SSM_GROUP), (0, 2, 1)),
        "c_im": jnp.transpose(_block_diag_t(d_cim, SSM_STATE, SSM_GROUP), (0, 2, 1)),
        "d_skip": d_dskip, "g_ffn": dg_ffn, "g_final": dg_final,
    }
    vec = _pack([small[n] for n in SMALL] + [loss_acc[0, 0].reshape(1)])

    x_start, x_finish = _chip_exchange_stages(n_e)
    v_start, v_finish = _all_exchange_stages()

    def both(f_chips, f_vec):
        def stage(ins, outs, sems):
            f_chips(ins[:n_e], outs[:n_e], sems[:2])
            f_vec(ins[n_e:], outs[n_e:], sems[2:])
        return stage

    ts_in = min(2048, s)
    win_steps = 8 * (s // ts_in)
    exchange = dict(ins=early_parts + [vec],
                    outs=[jax.ShapeDtypeStruct(t.shape, t.dtype) for t in early_parts]
                    + [jax.ShapeDtypeStruct((8,) + vec.shape, vec.dtype)],
                    aliased=False, sems=[3 * n_e, 3 * n_e, 7, 7],
                    stages=[(0, both(x_start, v_start)), (win_steps - 1, both(x_finish, v_finish))])
    d_w_in, *got = _mm("d_in_proj", (1, 8, s // ts_in), [(n1, (ts_in, D_MODEL), tk0, dz, (ts_in, half_in), tkj)], "tn",
                       [((N_CHIPS, D_MODEL, IN_WIDTH // N_CHIPS), F32, (None, D_MODEL, half_in),
                         lambda i, j, k: (j // 2, 0, j % 2))], acc_shape=(D_MODEL, half_in), comm=exchange)
    return grad_x, d_w_in, early_parts, got[:n_e], vec, got[n_e]


BIG = ("w_in", "w_attn_proj", "w_glu_a", "w_glu_b", "w_out", "w_ffn_gate", "w_ffn_up", "w_ffn_down", "w_ple_gate",
       "w_ple_proj")
SMALL = ("g_mix", "a_re", "a_im", "log_dt", "b_re", "b_im", "c_re", "c_im", "d_skip", "g_ffn", "g_final")
ANY = pl.BlockSpec(memory_space=pl.ANY)


def _place():
    x, y, c = lax.axis_index("x"), lax.axis_index("y"), lax.axis_index("c")
    chips = [(1 - x, y), (x, 1 - y), (1 - x, 1 - y)]
    return x, y, c, chips


def _remote(src, dst, send_sem, recv_sem, to):
    return pltpu.make_async_remote_copy(src_ref=src, dst_ref=dst, send_sem=send_sem, recv_sem=recv_sem, device_id=to,
                                        device_id_type=MESH)


def _comm_call(name, body, ins, out_shapes, n_sems, aliases=None):
    n_w = len(ins)
    return pl.pallas_call(
        body, name=name, in_specs=[ANY] * n_w, out_specs=[ANY] * len(out_shapes), out_shape=out_shapes,
        scratch_shapes=[pltpu.SemaphoreType.DMA((n,)) for n in n_sems], input_output_aliases=aliases or {},
    )(*ins)


def _gather_weights(bufs):
    n_w = len(bufs)
    start, forward, finish = _gather_stages(n_w)

    def body(*refs):
        ins, outs, sems = refs[:n_w], refs[n_w:2 * n_w], refs[2 * n_w:]
        start(ins, outs, sems)
        forward(ins, outs, sems)
        finish(ins, outs, sems)

    out_shapes = [jax.ShapeDtypeStruct(b.shape, b.dtype) for b in bufs]
    return _comm_call("gather_weights", body, bufs, out_shapes, [3 * n_w] * 4, aliases={w: w for w in range(n_w)})


def _gather_stages(n_w):
    def each():
        x, y, c, chips = _place()
        for w in range(n_w):
            for j, (cx, cy) in enumerate(chips):
                yield w, 3 * w + j, 2 * x + y, 2 * cx + cy, (cx, cy, c), (x, y, 1 - c), c

    def start(ins, outs, sems):
        for w, k, me, _, peer, _, c in each():
            mine = outs[w].at[me, c]
            _remote(mine, mine, sems[0].at[k], sems[1].at[k], peer).start()

    def forward(ins, outs, sems):
        for w, k, _, src_chip, peer, sib, c in each():
            landed = outs[w].at[src_chip, c]
            _remote(landed, landed, sems[0].at[k], sems[1].at[k], peer).wait_recv()
            _remote(landed, landed, sems[2].at[k], sems[3].at[k], sib).start()

    def finish(ins, outs, sems):
        for w, k, me, src_chip, peer, sib, c in each():
            other = outs[w].at[src_chip, 1 - c]
            _remote(other, other, sems[2].at[k], sems[3].at[k], sib).wait_recv()
        for w, k, me, src_chip, peer, sib, c in each():
            mine = outs[w].at[me, c]
            _remote(mine, mine, sems[0].at[k], sems[1].at[k], peer).wait_send()
            landed = outs[w].at[src_chip, c]
            _remote(landed, landed, sems[2].at[k], sems[3].at[k], sib).wait_send()

    return start, forward, finish


def _pair_exchange(grads):
    n_w = len(grads)
    start, finish = _pair_exchange_stages(n_w)

    def body(*refs):
        ins, outs, sems = refs[:n_w], refs[n_w:2 * n_w], refs[2 * n_w:]
        start(ins, outs, sems)
        finish(ins, outs, sems)

    return _comm_call("grad_pair_exchange", body, grads, _pair_exchange_shapes(grads), [N_CHIPS * n_w] * 2)


def _pair_exchange_shapes(grads):
    return [jax.ShapeDtypeStruct((N_CHIPS,) + g.shape[2:], g.dtype) for g in grads]


def _pair_exchange_stages(n_w):
    def each():
        x, y, c, _ = _place()
        for w in range(n_w):
            for q in range(N_CHIPS):
                yield w, q, N_CHIPS * w + q, c, (x, y, 1 - c)

    def start(ins, outs, sems):
        for w, q, k, c, sib in each():
            _remote(ins[w].at[q, 1 - c], outs[w].at[q], sems[0].at[k], sems[1].at[k], sib).start()

    def finish(ins, outs, sems):
        for w, q, k, c, sib in each():
            _remote(ins[w].at[q, 1 - c], outs[w].at[q], sems[0].at[k], sems[1].at[k], sib).wait()

    return start, finish


def _chip_exchange(parts):
    n_w = len(parts)

    start, finish = _chip_exchange_stages(n_w)

    def body(*refs):
        ins, outs, sems = refs[:n_w], refs[n_w:2 * n_w], refs[2 * n_w:]
        start(ins, outs, sems)
        finish(ins, outs, sems)

    out_shapes = [jax.ShapeDtypeStruct(t.shape, t.dtype) for t in parts]
    return _comm_call("grad_chip_exchange", body, parts, out_shapes, [3 * n_w, 3 * n_w])


def _chip_exchange_stages(n_w):
    def each():
        x, y, c, chips = _place()
        for w in range(n_w):
            for j, (cx, cy) in enumerate(chips):
                yield w, 3 * w + j, 2 * x + y, 2 * cx + cy, (cx, cy, c)

    def start(ins, outs, sems):
        for w, k, me, peer_chip, peer in each():
            _remote(ins[w].at[peer_chip], outs[w].at[me], sems[0].at[k], sems[1].at[k], peer).start()

    def finish(ins, outs, sems):
        for w, k, me, peer_chip, peer in each():
            got = outs[w].at[peer_chip]
            _remote(got, got, sems[0].at[k], sems[1].at[k], peer).wait_recv()
        for w, k, me, peer_chip, peer in each():
            _remote(ins[w].at[peer_chip], outs[w].at[me], sems[0].at[k], sems[1].at[k], peer).wait_send()

    return start, finish


def _pair_gather(halves):
    n_w = len(halves)

    def body(*refs):
        ins, outs = refs[:n_w], refs[n_w:2 * n_w]
        send, recv = refs[2 * n_w:]
        x, y, c, _ = _place()
        sib = (x, y, 1 - c)
        cps = []
        for w in range(n_w):
            cp = _remote(ins[w], outs[w], send.at[w], recv.at[w], sib)
            cp.start()
            cps.append(cp)
        for cp in cps:
            cp.wait()

    out_shapes = [jax.ShapeDtypeStruct(h.shape, h.dtype) for h in halves]
    return _comm_call("grad_pair_gather", body, halves, out_shapes, [n_w] * 2)


def _all_exchange_stages():
    def each():
        x, y, c, _ = _place()
        for k in range(1, 8):
            px, py, pc = x ^ ((k >> 2) & 1), y ^ ((k >> 1) & 1), c ^ (k & 1)
            yield k - 1, 4 * x + 2 * y + c, 4 * px + 2 * py + pc, (px, py, pc)

    def start(ins, outs, sems):
        for k, me, _, peer in each():
            _remote(ins[0], outs[0].at[me], sems[0].at[k], sems[1].at[k], peer).start()

    def finish(ins, outs, sems):
        for k, me, src, peer in each():
            got = outs[0].at[src]
            _remote(got, got, sems[0].at[k], sems[1].at[k], peer).wait_recv()
        for k, me, src, peer in each():
            _remote(ins[0], outs[0].at[me], sems[0].at[k], sems[1].at[k], peer).wait_send()

    return start, finish


def _row_tile(r):
    for t in (256, 128, 176, 64, 32, 16, 8):
        if r % t == 0:
            return t
    return r


P_C, P_CHIP, P_DEV = 2, 3, 4


def _cast_shard(w2):
    r, c = w2.shape
    t = _row_tile(r)
    blk, imap = _rows(t, c)
    return _ew("cast_own", (r // t,), [(w2, blk, imap)], [((r, c), BF16, blk, imap)], lambda pids, a: ((a,), ()))[0]


def _cast_into_slot(w2, place):
    r, c = w2.shape
    t = _row_tile(r)
    return _ew("cast_shard", (r // t,), [(w2, (t, c), lambda i, pv: (i, 0))],
               [((N_CHIPS, r, c), BF16, (None, t, c), lambda i, pv: (pv[P_CHIP], i, 0))],
               lambda pids, a: ((a,), ()), place=place)[0]


def _pair_sum(mine, theirs, place):
    _, r, c = theirs.shape
    t = _row_tile(r)
    own = ((None, None, t, c), lambda q, i, pv: (q, pv[P_C], i, 0))
    blk = ((None, t, c), lambda q, i, pv: (q, i, 0))
    return _ew("grad_pair_sum", (N_CHIPS, r // t), [(mine, *own), (theirs, *blk)], [((N_CHIPS, r, c), BF16, *blk)],
               lambda pids, a, b: ((a + b,), ()), place=place)[0]


def _chip_sum(own, got, place):
    _, r, c = own.shape
    t = _row_tile(r)
    ins = []
    for q in range(N_CHIPS):
        ins.append((own, (None, t, c), (lambda i, pv, q=q: (q, i, 0))))
        ins.append((got, (None, t, c), (lambda i, pv, q=q: (jnp.where(pv[P_CHIP] == q, (q + 1) % N_CHIPS, q), i, 0))))

    def fn(pids, *tiles):
        me = pids[0][P_CHIP]
        tot = None
        for q in range(N_CHIPS):
            term = jnp.where(me == q, tiles[2 * q], tiles[2 * q + 1]).astype(F32)
            tot = term if tot is None else tot + term
        return (tot,), ()

    return _ew("grad_chip_sum", (r // t,), ins, [((r, c), F32, (t, c), lambda i, pv: (i, 0))], fn, place=place)[0]


def _adamw_tile(w, g, m, v):
    m = ADAM_B1 * m + (1.0 - ADAM_B1) * g
    v = ADAM_B2 * v + (1.0 - ADAM_B2) * (g * g)
    m_hat = m / (1.0 - ADAM_B1 ** ADAM_STEP)
    v_hat = v / (1.0 - ADAM_B2 ** ADAM_STEP)
    delta = -ADAM_LR * (m_hat / (jnp.sqrt(v_hat) + ADAM_EPS) + ADAM_WD * w)
    return delta, m, v


def _adamw(name, g2, w2, m2, v2):
    r, c = w2.shape
    t = _row_tile(r)
    blk, imap = _rows(t, c)

    def fn(pids, g, w, m, v):
        delta, nm, nv = _adamw_tile(w, g, m, v)
        return (g, delta, nm, nv), ()

    return _ew(name, (r // t,), [(a, blk, imap) for a in (g2, w2, m2, v2)], [((r, c), F32, blk, imap)] * 4, fn)


def _adamw_halves(name, mine, theirs, w2, m2, v2, place):
    r, c = w2.shape
    t = _row_tile(r // 2)
    n_t = (r // 2) // t
    half = ((t, c), lambda h, i, pv: (i, 0))
    whole = ((t, c), lambda h, i, pv: (h * n_t + i, 0))

    def fn(pids, ga, gb, w, m, v):
        g = jnp.where(pids[1] == pids[0][P_C], ga, gb)
        delta, nm, nv = _adamw_tile(w, g, m, v)
        return (g, delta, nm, nv), ()

    return _ew(name, (2, n_t), [(mine, *half), (theirs, *half), (w2, *whole), (m2, *whole), (v2, *whole)],
               [((r, c), F32, *whole)] * 4, fn, place=place)


def _device_sum(own, got, place):
    r, c = own.shape
    t = _row_tile(r)
    ins = [(own, (t, c), lambda i, pv: (i, 0))]
    for q in range(8):
        ins.append((got, (None, t, c), (lambda i, pv, q=q: (jnp.where(pv[P_DEV] == q, (q + 1) % 8, q), i, 0))))

    def fn(pids, mine, *parts):
        me = pids[0][P_DEV]
        tot = None
        for q in range(8):
            term = jnp.where(me == q, mine, parts[q])
            tot = term if tot is None else tot + term
        return (tot,), ()

    return _ew("small_device_sum", (r // t,), ins, [((r, c), F32, (t, c), lambda i, pv: (i, 0))], fn, place=place)[0]


def _pack(parts):
    flat = jnp.concatenate([a.reshape(-1) for a in parts])
    pad = (-flat.shape[0]) % (SUB * 128)
    return jnp.pad(flat, (0, pad)).reshape(-1, 128)


def _unpack(mat, shapes):
    flat = mat.reshape(-1)
    out, off = [], 0
    for shp in shapes:
        n = math.prod(shp)
        out.append(flat[off:off + n].reshape(shp))
        off += n
    return out


def kernel(x, p, positions, g_mix, w_in, a_re, a_im, log_dt, b_re, b_im, c_re, c_im, d_skip, w_attn_proj, w_glu_a, w_glu_b, w_out, g_ffn, w_ffn_gate, w_ffn_up, w_ffn_down, w_ple_gate, w_ple_proj, g_final, loss_target, m_g_mix, m_w_in, m_a_re, m_a_im, m_log_dt, m_b_re, m_b_im, m_c_re, m_c_im, m_d_skip, m_w_attn_proj, m_w_glu_a, m_w_glu_b, m_w_out, m_g_ffn, m_w_ffn_gate, m_w_ffn_up, m_w_ffn_down, m_w_ple_gate, m_w_ple_proj, m_g_final, v_g_mix, v_w_in, v_a_re, v_a_im, v_log_dt, v_b_re, v_b_im, v_c_re, v_c_im, v_d_skip, v_w_attn_proj, v_w_glu_a, v_w_glu_b, v_w_out, v_g_ffn, v_w_ffn_gate, v_w_ffn_up, v_w_ffn_down, v_w_ple_gate, v_w_ple_proj, v_g_final):
    given = dict(locals())
    big_w = {n: given[n] for n in BIG}
    w_mats = {n: big_w[n].reshape(big_w[n].shape[1:]) for n in BIG}

    ax, ay, ac = lax.axis_index("x"), lax.axis_index("y"), lax.axis_index("c")
    place = jnp.stack([ax, ay, ac, 2 * ax + ay, 4 * ax + 2 * ay + ac]).astype(jnp.int32)

    bufs = []
    for n in BIG:
        r, c = w_mats[n].shape
        bufs.append(_cast_into_slot(w_mats[n], place).reshape(N_CHIPS, 2, r // 2, c))
    w_in_own = _cast_shard(w_mats["w_in"])

    sm = {
        "g_mix": g_mix.reshape(1, D_MODEL), "g_ffn": g_ffn.reshape(1, D_MODEL), "g_final": g_final.reshape(1, D_MODEL),
        "a_re": a_re[0], "a_im": a_im[0], "log_dt": log_dt[0], "b_re": b_re[0], "b_im": b_im[0], "c_re": c_re[0],
        "c_im": c_im[0], "d_skip": d_skip[0],
    }
    s = x.shape[1]
    grad_x, d_w_in, early_parts, early_got, vec, vec_got = _local_step(
        x[0], p[0, 0], positions[0], loss_target[0], sm, w_in_own, bufs[0], bufs[1:], place)

    r_in, c_in = w_mats["w_in"].shape
    g5_in = [d_w_in.reshape(N_CHIPS, 2, r_in // 2, c_in)]
    in_parts = [_pair_sum(g, t, place) for g, t in zip(g5_in, _pair_exchange(g5_in))]
    chip_parts = in_parts + list(early_parts)
    chip_got = list(_chip_exchange(in_parts)) + list(early_got)
    halves = [_chip_sum(own, got, place) for own, got in zip(chip_parts, chip_got)]
    other_halves = _pair_gather(halves)

    results = {}
    for n, mine, other in zip(BIG, halves, other_halves):
        r, c = w_mats[n].shape
        shp = big_w[n].shape
        outs = _adamw_halves("adamw_" + n, mine, other, w_mats[n], given["m_" + n].reshape(r, c),
                             given["v_" + n].reshape(r, c), place)
        results[n] = [o.reshape(shp) for o in outs]

    small_shapes = [given[n].shape for n in SMALL]
    tot = _device_sum(vec, vec_got, place)
    n_small = sum(math.prod(shp) for shp in small_shapes)
    loss = tot.reshape(-1)[n_small]
    w_s = _pack([given[n] for n in SMALL])
    m_s = _pack([given["m_" + n] for n in SMALL])
    v_s = _pack([given["v_" + n] for n in SMALL])
    rows_s = w_s.shape[0]
    g_s = tot.reshape(-1)[: rows_s * 128].reshape(rows_s, 128)
    outs_s = _adamw("adamw_small", g_s, w_s, m_s, v_s)
    for kind, mat in enumerate(outs_s):
        for n, arr in zip(SMALL, _unpack(mat, small_shapes)):
            results.setdefault(n, [None] * 4)[kind] = arr

    order = ("g_mix", "w_in", "a_re", "a_im", "log_dt", "b_re", "b_im", "c_re", "c_im", "d_skip", "w_attn_proj", "w_glu_a",
             "w_glu_b", "w_out", "g_ffn", "w_ffn_gate", "w_ffn_up", "w_ffn_down", "w_ple_gate", "w_ple_proj", "g_final")
    out = [loss, grad_x.reshape(1, s, D_MODEL)]
    for kind in range(4):
        out += [results[n][kind] for n in order]
    return tuple(out)
```

```python
import math

import jax
import jax.numpy as jnp
from jax import lax
from jax.experimental import pallas as pl
from jax.experimental.pallas import tpu as pltpu

F32 = jnp.float32
BF16 = jnp.bfloat16

D_MODEL = 1024
HEAD_DIM = 128
HEADS_PER_GROUP = 4
GROUP_WIDTH = HEADS_PER_GROUP * HEAD_DIM
GROUP_DILATIONS = (1, 4, 16)
N_GROUPS = len(GROUP_DILATIONS)
LSE_LANES = 32
LSE_WIDTH = HEADS_PER_GROUP * LSE_LANES
ATTN_BLOCK = 128
ROPE_DIM = 32
ROPE_HALF = 16
ROPE_THETA = 500000.0
SSM_WIDTH = 512
SSM_GROUPS = 32
SSM_GROUP = 16
SSM_STATE = 64
N_STATE = SSM_GROUPS * SSM_STATE
SSM_SUPER = 4
IN_WIDTH = 7168
COL_U = 4608
COL_GA = 5120
COL_GS = 6144
D_FF = 2816
N_CHIPS = 4
D_FF_Q = D_FF // N_CHIPS
PLE_DIM = 256
EPS = 1e-6
ADAM_LR = 0.001
ADAM_B1 = 0.9
ADAM_B2 = 0.999
ADAM_EPS = 1e-08
ADAM_WD = 0.01
ADAM_STEP = 10
NEG_BIG = -1e30
VMEM_LIMIT_BYTES = 56 * 1024 * 1024
MESH = pl.DeviceIdType.MESH

_DIMS = {
    "nn": (((1,), (0,)), ((), ())),
    "nt": (((1,), (1,)), ((), ())),
    "tn": (((0,), (0,)), ((), ())),
}


def _params(n_grid):
    return pltpu.CompilerParams(dimension_semantics=("arbitrary",) * n_grid, vmem_limit_bytes=VMEM_LIMIT_BYTES)


def _sig(v):
    return 1.0 / (1.0 + jnp.exp(-v))


def _dot(a, b, mode):
    return lax.dot_general(a, b, _DIMS[mode], preferred_element_type=F32)


def _mm(name, grid, pairs, mode, outs, epilogue=None, extras=(), acc_outs=(), acc_shape=None, j_outer=False,
        sum_pairs=True, resident_b=False, comm=None, place=None, fill=None):
    gi, gj, gk = grid
    n_p, n_e, n_o, n_a = len(pairs), len(extras), len(outs), len(acc_outs)
    assert not n_a or gj == 1
    assert sum_pairs or gk == 1
    run_grid = (gj, gi, gk) if j_outer else grid
    c_ins = list(comm["ins"]) if comm else []
    c_outs = list(comm["outs"]) if comm else []
    c_sems = list(comm["sems"]) if comm else []
    n_ci, n_co, n_cs = len(c_ins), len(c_outs), len(c_sems)
    n_s = 0 if place is None else 1
    n_fill = 0 if fill is None else 1

    def order(imap):
        if place is None:
            return (lambda j, i, k: imap(i, j, k)) if j_outer else imap
        return (lambda j, i, k, pv: imap(i, j, k, pv)) if j_outer else imap

    shared_a = [pr[0] is None for pr in pairs]
    n_in = 2 * n_p - sum(shared_a)

    def body(*refs):
        refs = refs[n_s:]
        pair_refs = list(refs[:n_in])
        extra_refs = refs[n_in: n_in + n_e]
        comm_in = refs[n_in + n_e: n_in + n_e + n_ci]
        at = n_in + n_e + n_ci + n_fill
        out_refs = refs[at: at + n_o]
        sum_refs = refs[at + n_o: at + n_o + n_a]
        comm_out = refs[at + n_o + n_a: at + n_o + n_a + n_co]
        scratch_refs = refs[at + n_o + n_a + n_co:]
        i = pl.program_id(1 if j_outer else 0)
        k = pl.program_id(2)
        if comm:
            step = (pl.program_id(0) * run_grid[1] + pl.program_id(1)) * run_grid[2] + pl.program_id(2)
            sems = scratch_refs[len(scratch_refs) - n_cs:]
            for at_step, stage in comm["stages"]:
                @pl.when(step == at_step)
                def _(stage=stage):
                    stage(comm_in, comm_out, sems)
        part = None if sum_pairs else []
        a = None
        for t in range(n_p):
            if not shared_a[t]:
                a = pair_refs.pop(0)[...].astype(BF16)
            b = pair_refs.pop(0)[...].astype(BF16)
            d = _dot(a, b, mode)
            if sum_pairs:
                part = d if part is None else part + d
            else:
                part.append(d)

        def finish(acc):
            tiles, sums = epilogue(acc, *[e[...] for e in extra_refs]) if epilogue is not None else ((acc,), ())
            for o_ref, tile in zip(out_refs, tiles):
                o_ref[...] = tile.astype(o_ref.dtype)
            if n_a:
                @pl.when(i == 0)
                def _():
                    for s_ref in sum_refs:
                        s_ref[...] = jnp.zeros_like(s_ref)

                for s_ref, s in zip(sum_refs, sums):
                    s_ref[...] += s

        if gk == 1:
            finish(part)
        else:
            acc_ref = scratch_refs[0]

            @pl.when(k == 0)
            def _():
                acc_ref[...] = part

            @pl.when(k > 0)
            def _():
                acc_ref[...] += part

            @pl.when(k == gk - 1)
            def _():
                finish(acc_ref[...])

    in_specs, args = [], []
    for a, a_block, a_imap, b, b_block, b_imap in pairs:
        if a is not None:
            in_specs.append(pl.BlockSpec(a_block, order(a_imap)))
            args.append(a)
        if resident_b:
            in_specs.append(pl.BlockSpec(b_block, order(b_imap), pipeline_mode=pl.Buffered(1)))
        else:
            in_specs.append(pl.BlockSpec(b_block, order(b_imap)))
        args.append(b)
    for e, e_block, e_imap in extras:
        in_specs.append(pl.BlockSpec(e_block, order(e_imap)))
        args.append(e)
    first_comm_in = len(args)
    for c_in in c_ins:
        in_specs.append(pl.BlockSpec(memory_space=pl.ANY))
        args.append(c_in)
    if n_fill:
        in_specs.append(pl.BlockSpec(memory_space=pl.ANY))
        args.append(fill)
    out_shape = [jax.ShapeDtypeStruct(shape, dtype) for shape, dtype, _, _ in outs]
    out_specs = [pl.BlockSpec(block, order(imap)) for _, _, block, imap in outs]
    for shape, dtype in acc_outs:
        out_shape.append(jax.ShapeDtypeStruct(shape, dtype))
        out_specs.append(pl.BlockSpec(shape, lambda *_: (0, 0)))
    first_comm_out = len(out_shape)
    for c_out in c_outs:
        out_shape.append(c_out)
        out_specs.append(pl.BlockSpec(memory_space=pl.ANY))
    aliases = {n_s + first_comm_in + n: first_comm_out + n for n in range(n_ci)} if comm and comm["aliased"] else {}
    if n_fill:
        aliases[n_s + len(args) - 1] = 0
    scratch = [pltpu.VMEM(acc_shape, F32)] if gk > 1 else []
    scratch += [pltpu.SemaphoreType.DMA((n,)) for n in c_sems]
    if n_s:
        spec = pltpu.PrefetchScalarGridSpec(num_scalar_prefetch=1, grid=run_grid, in_specs=in_specs, out_specs=out_specs,
                                            scratch_shapes=scratch)
        return pl.pallas_call(body, name=name, grid_spec=spec, out_shape=out_shape, compiler_params=_params(3),
                              input_output_aliases=aliases)(place, *args)
    return pl.pallas_call(
        body, name=name, grid=run_grid, in_specs=in_specs, out_specs=out_specs,
        out_shape=out_shape, scratch_shapes=scratch, compiler_params=_params(3), input_output_aliases=aliases,
    )(*args)


def _ew(name, grid, ins, outs, fn, acc_outs=(), place=None):
    n_i, n_o, n_a = len(ins), len(outs), len(acc_outs)
    ng = len(grid)
    n_s = 0 if place is None else 1

    def body(*refs):
        in_refs = refs[n_s: n_s + n_i]
        out_refs = refs[n_s + n_i: n_s + n_i + n_o]
        sum_refs = refs[n_s + n_i + n_o:]
        pids = tuple(pl.program_id(a) for a in range(ng))
        if n_s:
            pids = (refs[0],) + pids
        tiles, sums = fn(pids, *[r[...] for r in in_refs])
        for o_ref, tile in zip(out_refs, tiles):
            o_ref[...] = tile.astype(o_ref.dtype)
        if n_a:
            first = pids[0] == 0
            for p_ in pids[1:]:
                first = jnp.logical_and(first, p_ == 0)

            @pl.when(first)
            def _():
                for s_ref in sum_refs:
                    s_ref[...] = jnp.zeros_like(s_ref)

            for s_ref, s in zip(sum_refs, sums):
                s_ref[...] += s

    in_specs = [pl.BlockSpec(block, imap) for _, block, imap in ins]
    out_shape = [jax.ShapeDtypeStruct(shape, dtype) for shape, dtype, _, _ in outs]
    out_specs = [pl.BlockSpec(block, imap) for _, _, block, imap in outs]
    for shape, dtype in acc_outs:
        out_shape.append(jax.ShapeDtypeStruct(shape, dtype))
        out_specs.append(pl.BlockSpec(shape, lambda *_, nd=len(shape): (0,) * nd))
    arrays = [a for a, _, _ in ins]
    if n_s:
        assert not n_a
        spec = pltpu.PrefetchScalarGridSpec(num_scalar_prefetch=1, grid=grid, in_specs=in_specs, out_specs=out_specs)
        return pl.pallas_call(body, name=name, grid_spec=spec, out_shape=out_shape, compiler_params=_params(ng))(
            place, *arrays)
    return pl.pallas_call(
        body, name=name, grid=grid, in_specs=in_specs, out_specs=out_specs, out_shape=out_shape,
        compiler_params=_params(ng),
    )(*arrays)


def _rows(tm, width):
    return (tm, width), (lambda i: (i, 0))


def _rms_fwd_tile(h, g):
    r = lax.rsqrt(jnp.mean(h * h, axis=-1, keepdims=True) + EPS)
    return h * r * g


def _rms_bwd_tile(dn, h, g):
    r = lax.rsqrt(jnp.mean(h * h, axis=-1, keepdims=True) + EPS)
    hhat = h * r
    gy = dn * g
    dh = r * (gy - hhat * jnp.mean(gy * hhat, axis=-1, keepdims=True))
    dg = jnp.sum(dn * hhat, axis=0, keepdims=True)
    return dh, dg


def _rope_tables(pos_col, inv_row, tm):
    s = pos_col.shape[0]

    def fn(pids, pos, inv):
        ang = pos * inv
        lane = lax.broadcasted_iota(jnp.int32, ang.shape, 1)
        cs = jnp.where(lane < ROPE_DIM, jnp.cos(ang), 1.0)
        sn = jnp.sin(ang)
        s_lo = jnp.where(lane < ROPE_HALF, -sn, 0.0)
        s_hi = jnp.where(jnp.logical_and(lane >= ROPE_HALF, lane < ROPE_DIM), sn, 0.0)
        return (cs, s_lo, s_hi), ()

    blk, imap = _rows(tm, 128)
    return _ew(
        "rope_tables", (s // tm,),
        [(pos_col, (tm, 1), lambda i: (i, 0)), (inv_row, (1, 128), lambda i: (0, 0))],
        [((s, 128), F32, blk, imap)] * 3, fn,
    )


def _rope(xh, cs, s_lo, s_hi):
    return xh * cs + pltpu.roll(xh, HEAD_DIM - ROPE_HALF, 1) * s_lo + pltpu.roll(xh, ROPE_HALF, 1) * s_hi


def _rope_t(gh, cs, s_lo, s_hi):
    return gh * cs + pltpu.roll(gh * s_lo, ROPE_HALF, 1) + pltpu.roll(gh * s_hi, HEAD_DIM - ROPE_HALF, 1)


def _attn_geometry(length):
    nb = length // ATTN_BLOCK
    gq = min(4, nb)
    assert nb % gq == 0
    return nb, gq, gq * ATTN_BLOCK, nb // gq


def _band_masks():
    qi = lax.broadcasted_iota(jnp.int32, (ATTN_BLOCK, ATTN_BLOCK), 0)
    kj = lax.broadcasted_iota(jnp.int32, (ATTN_BLOCK, ATTN_BLOCK), 1)
    return kj <= qi, kj >= qi


def _band_mask_pair():
    qi = lax.broadcasted_iota(jnp.int32, (ATTN_BLOCK, 2 * ATTN_BLOCK), 0)
    cj = lax.broadcasted_iota(jnp.int32, (ATTN_BLOCK, 2 * ATTN_BLOCK), 1)
    in_cur = cj >= ATTN_BLOCK
    band = jnp.logical_or(jnp.logical_and(in_cur, cj - ATTN_BLOCK <= qi),
                          jnp.logical_and(cj < ATTN_BLOCK, cj >= qi))
    return band, in_cur


def _attn_fwd(qv, kv, vv, dil, cols3=(0, 0, 0)):
    length = qv.shape[0]
    nb, gq, rows, ni = _attn_geometry(length)

    def body(q_ref, kc_ref, kp_ref, vc_ref, vp_ref, o_ref, l_ref):
        i = pl.program_id(1)
        band, in_cur = _band_mask_pair()
        band_first = jnp.logical_and(band, jnp.logical_or(in_cur, i > 0))
        work = []
        for h in range(HEADS_PER_GROUP):
            cols = slice(h * HEAD_DIM, (h + 1) * HEAD_DIM)
            qh = q_ref[:, cols]
            k_all = jnp.concatenate([kp_ref[:, cols], kc_ref[:, cols]], axis=0)
            v_all = jnp.concatenate([vp_ref[:, cols], vc_ref[:, cols]], axis=0)
            for jj in range(gq):
                rws = slice(jj * ATTN_BLOCK, (jj + 1) * ATTN_BLOCK)
                two = slice(jj * ATTN_BLOCK, (jj + 2) * ATTN_BLOCK)
                work.append(dict(h=h, rws=rws, cols=cols, v=v_all[two], first=jj == 0, s=_dot(qh[rws], k_all[two], "nt")))
        for w in work:
            s = jnp.where(band_first if w["first"] else band, w["s"], NEG_BIG)
            m = jnp.max(s, axis=-1, keepdims=True)
            pexp = jnp.exp(s - m)
            w["den"] = jnp.sum(pexp, axis=-1, keepdims=True)
            w["p"] = pexp.astype(BF16)
            w["lse"] = m + jnp.log(w["den"])
        for w in work:
            o = _dot(w["p"], w["v"], "nn")
            o_ref[w["rws"], w["cols"]] = (o * (1.0 / w["den"])).astype(o_ref.dtype)
            l_ref[w["rws"], w["h"] * LSE_LANES:(w["h"] + 1) * LSE_LANES] = jnp.broadcast_to(w["lse"], (ATTN_BLOCK, LSE_LANES))

    def cur(c):
        return pl.BlockSpec((rows, GROUP_WIDTH), lambda r, i: (i, r + c))

    def prev(c):
        return pl.BlockSpec((ATTN_BLOCK, GROUP_WIDTH), lambda r, i: (jnp.maximum(i * gq - 1, 0), r + c))

    cq, ck, cv = cols3
    return pl.pallas_call(
        body, name=f"attn_fwd_d{dil}", grid=(dil, ni),
        in_specs=[cur(cq), cur(ck), prev(ck), cur(cv), prev(cv)],
        out_specs=[cur(0), pl.BlockSpec((rows, LSE_WIDTH), lambda r, i: (i, r))],
        out_shape=[jax.ShapeDtypeStruct((length, dil * GROUP_WIDTH), BF16),
                   jax.ShapeDtypeStruct((length, dil * LSE_WIDTH), F32)],
        compiler_params=_params(2),
    )(qv, kv, kv, vv, vv)


def _attn_bwd(qv, kv, vv, dov, ov, lv, dil, cols3=(0, 0, 0)):
    length = qv.shape[0]
    nb, gq, rows, ni = _attn_geometry(length)
    out_shape = (length, dil * GROUP_WIDTH)

    def body(qc_ref, qn_ref, kc_ref, kp_ref, vc_ref, vp_ref, doc_ref, don_ref, oc_ref, on_ref, lc_ref, ln_ref,
             dq_ref, dk_ref, dv_ref):
        i = pl.program_id(1)
        _, mask_p = _band_masks()
        band, in_cur = _band_mask_pair()
        band_first = jnp.logical_and(band, jnp.logical_or(in_cur, i > 0))
        has_next = i < ni - 1

        last = slice(gq * ATTN_BLOCK, (gq + 1) * ATTN_BLOCK)
        mask_next = jnp.logical_and(mask_p, has_next)

        def rows_of(jj):
            return slice(jj * ATTN_BLOCK, (jj + 1) * ATTN_BLOCK)

        def keys_of(jj):
            return slice(jj * ATTN_BLOCK, (jj + 2) * ATTN_BLOCK)

        heads = []
        for h in range(HEADS_PER_GROUP):
            cols = slice(h * HEAD_DIM, (h + 1) * HEAD_DIM)
            hd = dict(
                cols=cols, q_c=qc_ref[:, cols], q_n=qn_ref[:, cols],
                k_all=jnp.concatenate([kp_ref[:, cols], kc_ref[:, cols]], axis=0),
                v_all=jnp.concatenate([vp_ref[:, cols], vc_ref[:, cols]], axis=0),
                do_c=doc_ref[:, cols], do_n=don_ref[:, cols],
                l_c=lc_ref[:, h * LSE_LANES:h * LSE_LANES + 1], l_n=ln_ref[:, h * LSE_LANES:h * LSE_LANES + 1],
            )
            hd["dl_c"] = jnp.sum(hd["do_c"].astype(F32) * oc_ref[:, cols].astype(F32), axis=-1, keepdims=True)
            hd["dl_n"] = jnp.sum(hd["do_n"].astype(F32) * on_ref[:, cols].astype(F32), axis=-1, keepdims=True)
            hd["s"] = [_dot(hd["q_c"][rows_of(jj)], hd["k_all"][keys_of(jj)], "nt") for jj in range(gq)]
            hd["dp"] = [_dot(hd["do_c"][rows_of(jj)], hd["v_all"][keys_of(jj)], "nt") for jj in range(gq)]
            hd["s"].append(_dot(hd["q_n"], hd["k_all"][last], "nt"))
            hd["dp"].append(_dot(hd["do_n"], hd["v_all"][last], "nt"))
            heads.append(hd)
        for hd in heads:
            hd["p"], hd["ds"] = [], []
            for jj in range(gq + 1):
                if jj < gq:
                    mask, l_col, delta = (band_first if jj == 0 else band), hd["l_c"][rows_of(jj)], hd["dl_c"][rows_of(jj)]
                else:
                    mask, l_col, delta = mask_next, hd["l_n"], hd["dl_n"]
                p = jnp.where(mask, jnp.exp(hd["s"][jj] - l_col), 0.0)
                hd["p"].append(p.astype(BF16))
                hd["ds"].append((p * (hd["dp"][jj] - delta)).astype(BF16))
        for hd in heads:
            cols = hd["cols"]
            dk_blocks, dv_blocks = [None] * (gq + 1), [None] * (gq + 1)

            def add(lst, idx, val):
                lst[idx] = val if lst[idx] is None else lst[idx] + val

            for jj in range(gq):
                qb, dob = hd["q_c"][rows_of(jj)], hd["do_c"][rows_of(jj)]
                dq_ref[rows_of(jj), cols] = _dot(hd["ds"][jj], hd["k_all"][keys_of(jj)], "nn").astype(dq_ref.dtype)
                dk2 = _dot(hd["ds"][jj], qb, "tn")
                dv2 = _dot(hd["p"][jj], dob, "tn")
                add(dk_blocks, jj, dk2[:ATTN_BLOCK])
                add(dk_blocks, jj + 1, dk2[ATTN_BLOCK:])
                add(dv_blocks, jj, dv2[:ATTN_BLOCK])
                add(dv_blocks, jj + 1, dv2[ATTN_BLOCK:])
            add(dk_blocks, gq, _dot(hd["ds"][gq], hd["q_n"], "tn"))
            add(dv_blocks, gq, _dot(hd["p"][gq], hd["do_n"], "tn"))
            for jj in range(gq):
                dk_ref[rows_of(jj), cols] = dk_blocks[jj + 1].astype(dk_ref.dtype)
                dv_ref[rows_of(jj), cols] = dv_blocks[jj + 1].astype(dv_ref.dtype)

    def cur(c):
        return pl.BlockSpec((rows, GROUP_WIDTH), lambda r, i: (i, r + c))

    def prev(c):
        return pl.BlockSpec((ATTN_BLOCK, GROUP_WIDTH), lambda r, i: (jnp.maximum(i * gq - 1, 0), r + c))

    def nxt(c):
        return pl.BlockSpec((ATTN_BLOCK, GROUP_WIDTH), lambda r, i: (jnp.minimum((i + 1) * gq, nb - 1), r + c))

    cq, ck, cv = cols3
    lse_cur = pl.BlockSpec((rows, LSE_WIDTH), lambda r, i: (i, r))
    lse_next = pl.BlockSpec((ATTN_BLOCK, LSE_WIDTH), lambda r, i: (jnp.minimum((i + 1) * gq, nb - 1), r))
    return pl.pallas_call(
        body, name=f"attn_bwd_d{dil}", grid=(dil, ni),
        in_specs=[cur(cq), nxt(cq), cur(ck), prev(ck), cur(cv), prev(cv), cur(0), nxt(0), cur(0), nxt(0), lse_cur, lse_next],
        out_specs=[cur(0), cur(0), cur(0)],
        out_shape=[jax.ShapeDtypeStruct(out_shape, BF16)] * 3,
        compiler_params=_params(2),
    )(qv, qv, kv, kv, vv, vv, dov, dov, ov, ov, lv, lv)


DILATED = tuple((g, d) for g, d in enumerate(GROUP_DILATIONS) if d > 1)


def _spread(scr, slot, tile, out_ref, dil, col, width=GROUP_WIDTH):
    tm = tile.shape[0]
    buf = scr.at[slot]
    buf[...] = tile
    for r in range(dil):
        c0 = r * width + col
        out_ref[:, c0:c0 + HEAD_DIM] = buf[pl.ds(r, tm // dil, stride=dil), :].astype(out_ref.dtype)


def _collect(scr, slot, in_ref, dil, col, width=GROUP_WIDTH):
    tm = scr.shape[1]
    buf = scr.at[slot]
    for r in range(dil):
        c0 = r * width + col
        buf[pl.ds(r, tm // dil, stride=dil), :] = in_ref[:, c0:c0 + HEAD_DIM].astype(F32)
    return buf[...]


def _view_spec(tm, dil, width=GROUP_WIDTH):
    return pl.BlockSpec((tm // dil, dil * width), lambda i: (i, 0))


def _view_shape(s, dil, dtype, width=GROUP_WIDTH):
    return jax.ShapeDtypeStruct((s // dil, dil * width), dtype)


def _qkv_layout(z, tabs, tm):
    s = z.shape[0]
    scale = 1.0 / math.sqrt(HEAD_DIM)
    qkv_width = 3 * N_GROUPS * GROUP_WIDTH

    def body(z_ref, cs_ref, lo_ref, hi_ref, qk0_ref, *rest):
        views, scr = rest[:-1], rest[-1]
        tabs_ = (cs_ref[...], lo_ref[...], hi_ref[...])
        for part in range(3):
            for g, dil in enumerate(GROUP_DILATIONS):
                if part == 2 and dil == 1:
                    continue
                for h in range(HEADS_PER_GROUP):
                    col = part * N_GROUPS * GROUP_WIDTH + g * GROUP_WIDTH + h * HEAD_DIM
                    t = z_ref[:, col:col + HEAD_DIM].astype(F32)
                    if part < 2:
                        t = _rope(t, *tabs_)
                    if part == 0:
                        t = t * scale
                    if dil == 1:
                        c0 = part * GROUP_WIDTH + h * HEAD_DIM
                        qk0_ref[:, c0:c0 + HEAD_DIM] = t.astype(BF16)
                    else:
                        out = views[3 * [gg for gg, _ in DILATED].index(g) + part]
                        _spread(scr, h, t, out, dil, h * HEAD_DIM)

    row = lambda i: (i, 0)
    out_shape = [jax.ShapeDtypeStruct((s, 2 * GROUP_WIDTH), BF16)]
    out_specs = [pl.BlockSpec((tm, 2 * GROUP_WIDTH), row)]
    for _, dil in DILATED:
        out_shape += [_view_shape(s, dil, BF16)] * 3
        out_specs += [_view_spec(tm, dil)] * 3
    res = pl.pallas_call(
        body, name="qkv_layout", grid=(s // tm,),
        in_specs=[pl.BlockSpec((tm, qkv_width), row)] + [pl.BlockSpec((tm, HEAD_DIM), row)] * 3,
        out_specs=out_specs, out_shape=out_shape,
        scratch_shapes=[pltpu.VMEM((HEADS_PER_GROUP, tm, HEAD_DIM), F32)], compiler_params=_params(1),
    )(z, *tabs)
    return res[0], [tuple(res[1 + 3 * n:4 + 3 * n]) for n in range(len(DILATED))]


def _attn_merge(o0, l0, dilated, tm):
    s = o0.shape[0]
    n_d = len(DILATED)

    def body(*refs):
        o0_ref, l0_ref = refs[:2]
        in_views = refs[2:2 + 2 * n_d]
        attn_ref, lse_ref = refs[2 + 2 * n_d:4 + 2 * n_d]
        out_views = refs[4 + 2 * n_d:4 + 4 * n_d]
        scr = refs[-1]
        l_rows = [l0_ref[...]] + [_collect(scr, n, in_views[2 * n + 1], dil, 0, LSE_WIDTH) for n, (_, dil) in enumerate(DILATED)]
        lse_heads = []
        for h in range(HEADS_PER_GROUP):
            cols = slice(h * HEAD_DIM, (h + 1) * HEAD_DIM)
            os_ = [o0_ref[:, cols].astype(F32)]
            for n, (_, dil) in enumerate(DILATED):
                os_.append(_collect(scr, n_d + n, in_views[2 * n], dil, h * HEAD_DIM))
            ls_ = [lr[:, h * LSE_LANES:h * LSE_LANES + 1] for lr in l_rows]
            m = ls_[0]
            for l_ in ls_[1:]:
                m = jnp.maximum(m, l_)
            es = [jnp.exp(l_ - m) for l_ in ls_]
            den = es[0]
            num = es[0] * os_[0]
            for e, o in zip(es[1:], os_[1:]):
                den = den + e
                num = num + e * o
            attn = num * (1.0 / den)
            lse_heads.append(jnp.broadcast_to(m + jnp.log(den), (tm, LSE_LANES)))
            attn_ref[:, cols] = attn.astype(BF16)
            for n, (_, dil) in enumerate(DILATED):
                _spread(scr, 2 * n_d, attn, out_views[2 * n], dil, h * HEAD_DIM)
        lse = jnp.concatenate(lse_heads, axis=1)
        lse_ref[...] = lse
        for n, (_, dil) in enumerate(DILATED):
            _spread(scr, 2 * n_d, lse, out_views[2 * n + 1], dil, 0, LSE_WIDTH)

    row = lambda i: (i, 0)
    nat = pl.BlockSpec((tm, GROUP_WIDTH), row)
    nat_l = pl.BlockSpec((tm, LSE_WIDTH), row)
    in_specs = [nat, nat_l]
    args = [o0, l0]
    out_specs = [nat, nat_l]
    out_shape = [jax.ShapeDtypeStruct((s, GROUP_WIDTH), BF16), jax.ShapeDtypeStruct((s, LSE_WIDTH), F32)]
    for (_, dil), (ov, lv) in zip(DILATED, dilated):
        in_specs += [_view_spec(tm, dil), _view_spec(tm, dil, LSE_WIDTH)]
        args += [ov, lv]
        out_specs += [_view_spec(tm, dil), _view_spec(tm, dil, LSE_WIDTH)]
        out_shape += [_view_shape(s, dil, BF16), _view_shape(s, dil, F32, LSE_WIDTH)]
    res = pl.pallas_call(
        body, name="attn_merge", grid=(s // tm,), in_specs=in_specs, out_specs=out_specs, out_shape=out_shape,
        scratch_shapes=[pltpu.VMEM((2 * n_d + 1, tm, HEAD_DIM), F32)], compiler_params=_params(1),
    )(*args)
    return res[0], res[1], [tuple(res[2 + 2 * n:4 + 2 * n]) for n in range(n_d)]


def _to_views(a, tm):
    s = a.shape[0]

    def body(a_ref, *rest):
        outs, scr = rest[:-1], rest[-1]
        for h in range(HEADS_PER_GROUP):
            t = a_ref[:, h * HEAD_DIM:(h + 1) * HEAD_DIM].astype(F32)
            for n, (_, dil) in enumerate(DILATED):
                _spread(scr, n, t, outs[n], dil, h * HEAD_DIM)

    return pl.pallas_call(
        body, name="to_views", grid=(s // tm,), in_specs=[pl.BlockSpec((tm, GROUP_WIDTH), lambda i: (i, 0))],
        out_specs=[_view_spec(tm, dil) for _, dil in DILATED], out_shape=[_view_shape(s, dil, BF16) for _, dil in DILATED],
        scratch_shapes=[pltpu.VMEM((len(DILATED), tm, HEAD_DIM), F32)], compiler_params=_params(1),
    )(a)


def _dz_layout(grads, du, dga, dgs, tabs, tm, comm=None):
    s = du.shape[0]
    scale = 1.0 / math.sqrt(HEAD_DIM)
    n_steps = s // tm
    c_ins = list(comm["ins"]) if comm else []
    c_outs = list(comm["outs"]) if comm else []
    c_sems = list(comm["sems"]) if comm else []
    n_fixed = 3 * N_GROUPS + 6

    def body(*refs):
        g_refs = refs[:3 * N_GROUPS]
        du_ref, dga_ref, dgs_ref, cs_ref, lo_ref, hi_ref = refs[3 * N_GROUPS:n_fixed]
        comm_in = refs[n_fixed:n_fixed + len(c_ins)]
        dz_ref = refs[n_fixed + len(c_ins)]
        comm_out = refs[n_fixed + len(c_ins) + 1:n_fixed + len(c_ins) + 1 + len(c_outs)]
        scr = refs[n_fixed + len(c_ins) + 1 + len(c_outs)]
        sems = refs[n_fixed + len(c_ins) + 2 + len(c_outs):]
        if comm:
            @pl.when(pl.program_id(0) == 0)
            def _():
                comm["start"](comm_in, comm_out, sems)

            @pl.when(pl.program_id(0) == n_steps - 1)
            def _():
                comm["finish"](comm_in, comm_out, sems)

        tabs_ = (cs_ref[...], lo_ref[...], hi_ref[...])
        for part in range(3):
            for g, dil in enumerate(GROUP_DILATIONS):
                src = g_refs[3 * g + part]
                for h in range(HEADS_PER_GROUP):
                    if dil == 1:
                        t = src[:, h * HEAD_DIM:(h + 1) * HEAD_DIM].astype(F32)
                    else:
                        t = _collect(scr, h, src, dil, h * HEAD_DIM)
                    if part < 2:
                        t = _rope_t(t, *tabs_)
                    if part == 0:
                        t = t * scale
                    col = part * N_GROUPS * GROUP_WIDTH + g * GROUP_WIDTH + h * HEAD_DIM
                    dz_ref[:, col:col + HEAD_DIM] = t.astype(BF16)
        dz_ref[:, COL_U:COL_GA] = du_ref[...]
        dz_ref[:, COL_GA:COL_GS] = dga_ref[...]
        dz_ref[:, COL_GS:IN_WIDTH] = dgs_ref[...]

    row = lambda i: (i, 0)
    in_specs, args = [], []
    for (g, dil), trio in zip(enumerate(GROUP_DILATIONS), grads):
        in_specs += [pl.BlockSpec((tm, GROUP_WIDTH), row) if dil == 1 else _view_spec(tm, dil)] * 3
        args += list(trio)
    in_specs += [pl.BlockSpec((tm, SSM_WIDTH), row), pl.BlockSpec((tm, D_MODEL), row), pl.BlockSpec((tm, D_MODEL), row)]
    in_specs += [pl.BlockSpec((tm, HEAD_DIM), row)] * 3
    in_specs += [pl.BlockSpec(memory_space=pl.ANY)] * len(c_ins)
    res = pl.pallas_call(
        body, name="dz_layout", grid=(n_steps,), in_specs=in_specs,
        out_specs=[pl.BlockSpec((tm, IN_WIDTH), row)] + [pl.BlockSpec(memory_space=pl.ANY)] * len(c_outs),
        out_shape=[jax.ShapeDtypeStruct((s, IN_WIDTH), BF16)] + c_outs,
        scratch_shapes=[pltpu.VMEM((HEADS_PER_GROUP, tm, HEAD_DIM), F32)] + [pltpu.SemaphoreType.DMA((n,)) for n in c_sems],
        compiler_params=_params(1),
    )(*args, du, dga, dgs, *tabs, *c_ins)
    return res[0], list(res[1:])


def _discretise(a_re, a_im, log_dt, bt_re, bt_im):
    dt = jnp.exp(log_dt)
    mag = jnp.exp(a_re * dt)
    bar_re = mag * jnp.cos(a_im * dt)
    bar_im = mag * jnp.sin(a_im * dt)
    nr = bar_re - 1.0
    ni = bar_im
    den = a_re * a_re + a_im * a_im
    z_re = (nr * a_re + ni * a_im) / den
    z_im = (ni * a_re - nr * a_im) / den
    bb_re = z_re[:, None, :] * bt_re - z_im[:, None, :] * bt_im
    bb_im = z_re[:, None, :] * bt_im + z_im[:, None, :] * bt_re
    return bar_re, bar_im, bb_re, bb_im


def _ssm_prep(a_re, a_im, log_dt, bt_re, bt_im):
    def body(ar, ai, ld, br, bi, o_lr, o_li, o_br, o_bi):
        lr, li, bbr, bbi = _discretise(ar[...], ai[...], ld[...], br[...], bi[...])
        o_lr[...] = lr
        o_li[...] = li
        o_br[...] = bbr
        o_bi[...] = bbi

    sm = jax.ShapeDtypeStruct((SSM_GROUPS, SSM_STATE), F32)
    bg = jax.ShapeDtypeStruct((SSM_GROUPS, SSM_GROUP, SSM_STATE), F32)
    return pl.pallas_call(body, name="ssm_prep", out_shape=[sm, sm, bg, bg])(a_re, a_im, log_dt, bt_re, bt_im)


def _ssm_param_bwd(a_re, a_im, log_dt, bt_re, bt_im, d_lr, d_li, d_bbr, d_bbi):
    def body(ar, ai, ld, br, bi, g_lr, g_li, g_br, g_bi, o_ar, o_ai, o_ld, o_br, o_bi):
        _, vjp = jax.vjp(_discretise, ar[...], ai[...], ld[...], br[...], bi[...])
        d_ar, d_ai, d_ld, d_br, d_bi = vjp((g_lr[...], g_li[...], g_br[...], g_bi[...]))
        o_ar[...] = d_ar
        o_ai[...] = d_ai
        o_ld[...] = d_ld
        o_br[...] = d_br
        o_bi[...] = d_bi

    sm = jax.ShapeDtypeStruct((SSM_GROUPS, SSM_STATE), F32)
    col = jax.ShapeDtypeStruct((SSM_GROUPS, 1), F32)
    bg = jax.ShapeDtypeStruct((SSM_GROUPS, SSM_GROUP, SSM_STATE), F32)
    return pl.pallas_call(body, name="ssm_param_bwd", out_shape=[sm, sm, col, bg, bg])(
        a_re, a_im, log_dt, bt_re, bt_im, d_lr, d_li, d_bbr, d_bbi)


def _block_diag(t, rows_per, cols_per):
    t4 = t.reshape(SSM_SUPER, 8, rows_per, cols_per)
    eye = jnp.eye(8, dtype=t.dtype)
    return jnp.einsum("bgrc,gh->bgrhc", t4, eye).reshape(SSM_SUPER, 8 * rows_per, 8 * cols_per)


def _block_diag_t(dense, rows_per, cols_per):
    t = dense.reshape(SSM_SUPER, 8, rows_per, 8, cols_per)
    eye = jnp.eye(8, dtype=dense.dtype)
    return jnp.einsum("bgrhc,gh->bgrc", t, eye).reshape(SSM_GROUPS, rows_per, cols_per)


def _gelu(v):
    c = math.sqrt(2.0 / math.pi)
    return 0.5 * v * (1.0 + jnp.tanh(c * (v + 0.044715 * v * v * v)))


def _gelu_grad(v):
    c = math.sqrt(2.0 / math.pi)
    t = jnp.tanh(c * (v + 0.044715 * v * v * v))
    return 0.5 * (1.0 + t) + 0.5 * v * (1.0 - t * t) * c * (1.0 + 3.0 * 0.044715 * v * v)


SUB = 8


SCAN_STEPS = (1, 2, 4)
N_SCAN_TABLES = 2 + 2 * len(SCAN_STEPS)


def _scan_tables(tab_ref, lam_re, lam_im, reverse, conj):
    lr = lam_re
    li = -lam_im if conj else lam_im
    powers = [(lr, li)]
    for _ in range(SUB - 1):
        pr, pi = powers[-1]
        powers.append((pr * lr - pi * li, pr * li + pi * lr))
    row = lax.broadcasted_iota(jnp.int32, (SUB, N_STATE), 0)
    if reverse:
        row = SUB - 1 - row
    wide = lambda v: jnp.broadcast_to(v, (SUB, N_STATE))
    p_re = jnp.zeros((SUB, N_STATE), F32)
    p_im = jnp.zeros((SUB, N_STATE), F32)
    for j in range(SUB):
        p_re = jnp.where(row == j, wide(powers[j][0]), p_re)
        p_im = jnp.where(row == j, wide(powers[j][1]), p_im)
    tab_ref[0] = p_re
    tab_ref[1] = p_im
    for idx, k in enumerate(SCAN_STEPS):
        tab_ref[2 + 2 * idx] = jnp.where(row >= k, wide(powers[k - 1][0]), 0.0)
        tab_ref[3 + 2 * idx] = jnp.where(row >= k, wide(powers[k - 1][1]), 0.0)


def _scan_rows(g_re_ref, g_im_ref, tab_ref, carry, n_rows, reverse):
    last = 0 if reverse else SUB - 1

    def tile_step(tt, state):
        cr, ci = state
        t8 = (n_rows // SUB - 1 - tt) if reverse else tt
        start = pl.multiple_of(t8 * SUB, SUB)
        xr = g_re_ref[pl.ds(start, SUB), :]
        xi = g_im_ref[pl.ds(start, SUB), :]
        for idx, k in enumerate(SCAN_STEPS):
            mr = tab_ref[2 + 2 * idx]
            mi = tab_ref[3 + 2 * idx]
            shift = SUB - k if reverse else k
            sr = pltpu.roll(xr, shift, 0)
            si = pltpu.roll(xi, shift, 0)
            xr, xi = xr + (mr * sr - mi * si), xi + (mr * si + mi * sr)
        pr = tab_ref[0]
        pi = tab_ref[1]
        xr, xi = xr + (pr * cr - pi * ci), xi + (pr * ci + pi * cr)
        g_re_ref[pl.ds(start, SUB), :] = xr
        g_im_ref[pl.ds(start, SUB), :] = xi
        return (jnp.broadcast_to(xr[last:last + 1, :], (SUB, N_STATE)),
                jnp.broadcast_to(xi[last:last + 1, :], (SUB, N_STATE)))

    return lax.fori_loop(0, n_rows // SUB, tile_step, carry)


def _ssm_fwd(z, b_re, b_im, c_re, c_im, lam_re, lam_im, d_skip, chunk):
    s = z.shape[0]

    def body(u_ref, bre, bim, cre, cim, lre, lim, dsk, hre_ref, him_ref, ys_ref, yg_ref, car_re, car_im, tabs):
        i = pl.program_id(0)

        @pl.when(i == 0)
        def _():
            car_re[...] = jnp.zeros_like(car_re)
            car_im[...] = jnp.zeros_like(car_im)
            _scan_tables(tabs, lre[...], lim[...], False, False)

        u = u_ref[...]
        for b in range(SSM_SUPER):
            ub = u[:, b * 128:(b + 1) * 128]
            st = slice(b * 512, (b + 1) * 512)
            hre_ref[:, st] = _dot(ub, bre[b], "nn")
            him_ref[:, st] = _dot(ub, bim[b], "nn")
        sr, si = _scan_rows(hre_ref, him_ref, tabs, (car_re[...], car_im[...]), chunk, False)
        car_re[...] = sr
        car_im[...] = si
        uf = u.astype(F32)
        for b in range(SSM_SUPER):
            st = slice(b * 512, (b + 1) * 512)
            ch = slice(b * 128, (b + 1) * 128)
            y = _dot(hre_ref[:, st].astype(BF16), cre[b], "nn") - _dot(him_ref[:, st].astype(BF16), cim[b], "nn")
            y = y + dsk[:, ch] * uf[:, ch]
            ys_ref[:, ch] = y
            yg_ref[:, ch] = _gelu(y).astype(BF16)

    full3 = lambda i: (0, 0, 0)
    full2 = lambda i: (0, 0)
    row = lambda i: (i, 0)
    u_col = COL_U // SSM_WIDTH
    return pl.pallas_call(
        body, name="ssm_fwd", grid=(s // chunk,),
        in_specs=[pl.BlockSpec((chunk, SSM_WIDTH), lambda i: (i, u_col)),
                  pl.BlockSpec((SSM_SUPER, 128, 512), full3), pl.BlockSpec((SSM_SUPER, 128, 512), full3),
                  pl.BlockSpec((SSM_SUPER, 512, 128), full3), pl.BlockSpec((SSM_SUPER, 512, 128), full3),
                  pl.BlockSpec((1, N_STATE), full2), pl.BlockSpec((1, N_STATE), full2), pl.BlockSpec((1, SSM_WIDTH), full2)],
        out_specs=[pl.BlockSpec((chunk, N_STATE), row), pl.BlockSpec((chunk, N_STATE), row),
                   pl.BlockSpec((chunk, SSM_WIDTH), row), pl.BlockSpec((chunk, SSM_WIDTH), row)],
        out_shape=[jax.ShapeDtypeStruct((s, N_STATE), F32), jax.ShapeDtypeStruct((s, N_STATE), F32),
                   jax.ShapeDtypeStruct((s, SSM_WIDTH), F32), jax.ShapeDtypeStruct((s, SSM_WIDTH), BF16)],
        scratch_shapes=[pltpu.VMEM((SUB, N_STATE), F32), pltpu.VMEM((SUB, N_STATE), F32),
                        pltpu.VMEM((N_SCAN_TABLES, SUB, N_STATE), F32)],
        compiler_params=_params(1),
    )(z, b_re, b_im, c_re, c_im, lam_re, lam_im, d_skip)


def _ssm_bwd(dys, z, h_re, h_im, b_re, b_im, c_re, c_im, lam_re, lam_im, d_skip, chunk):
    s = z.shape[0]
    n_chunks = s // chunk

    def body(dy_ref, u_ref, hre_ref, him_ref, hpr_ref, hpi_ref, bre, bim, cre, cim, lre, lim, dsk,
             du_ref, dlr_ref, dli_ref, dbr_ref, dbi_ref, dcr_ref, dci_ref, dd_ref, are, aim, car_re, car_im, tabs):
        i = pl.program_id(0)
        n = n_chunks - 1 - i

        @pl.when(i == 0)
        def _():
            car_re[...] = jnp.zeros_like(car_re)
            car_im[...] = jnp.zeros_like(car_im)
            _scan_tables(tabs, lre[...], lim[...], True, True)
            for r in (dlr_ref, dli_ref, dbr_ref, dbi_ref, dcr_ref, dci_ref, dd_ref):
                r[...] = jnp.zeros_like(r)

        dy = dy_ref[...]
        dyb = dy.astype(BF16)
        u = u_ref[...]
        for b in range(SSM_SUPER):
            ch = slice(b * 128, (b + 1) * 128)
            st = slice(b * 512, (b + 1) * 512)
            are[:, st] = _dot(dyb[:, ch], cre[b], "nt")
            aim[:, st] = -_dot(dyb[:, ch], cim[b], "nt")
        sr, si = _scan_rows(are, aim, tabs, (car_re[...], car_im[...]), chunk, True)
        car_re[...] = sr
        car_im[...] = si
        row_id = lax.broadcasted_iota(jnp.int32, (chunk, N_STATE), 0)
        top_scale = jnp.where(n > 0, 1.0, 0.0)
        h_r = hre_ref[...]
        h_i = him_ref[...]
        hp_r = jnp.where(row_id == 0, hpr_ref[SUB - 1:SUB, :] * top_scale, pltpu.roll(h_r, 1, 0))
        hp_i = jnp.where(row_id == 0, hpi_ref[SUB - 1:SUB, :] * top_scale, pltpu.roll(h_i, 1, 0))
        a_r = are[...]
        a_i = aim[...]
        dlr_ref[...] += jnp.sum(a_r * hp_r + a_i * hp_i, axis=0, keepdims=True)
        dli_ref[...] += jnp.sum(a_i * hp_r - a_r * hp_i, axis=0, keepdims=True)
        dd_ref[...] += jnp.sum(dy * u.astype(F32), axis=0, keepdims=True)
        a_rb = a_r.astype(BF16)
        a_ib = a_i.astype(BF16)
        h_rb = h_r.astype(BF16)
        h_ib = h_i.astype(BF16)
        for b in range(SSM_SUPER):
            ch = slice(b * 128, (b + 1) * 128)
            st = slice(b * 512, (b + 1) * 512)
            dbr_ref[b] += _dot(u[:, ch], a_rb[:, st], "tn")
            dbi_ref[b] += _dot(u[:, ch], a_ib[:, st], "tn")
            dcr_ref[b] += _dot(h_rb[:, st], dyb[:, ch], "tn")
            dci_ref[b] += -_dot(h_ib[:, st], dyb[:, ch], "tn")
            du = _dot(a_rb[:, st], bre[b], "nt") + _dot(a_ib[:, st], bim[b], "nt") + dsk[:, ch] * dy[:, ch]
            du_ref[:, ch] = du.astype(du_ref.dtype)

    full3 = lambda i: (0, 0, 0)
    full2 = lambda i: (0, 0)
    rev = lambda i: (n_chunks - 1 - i, 0)
    above = lambda i: (jnp.maximum((n_chunks - 1 - i) * (chunk // SUB) - 1, 0), 0)
    u_col = COL_U // SSM_WIDTH
    b_spec = pl.BlockSpec((SSM_SUPER, 128, 512), full3)
    c_spec = pl.BlockSpec((SSM_SUPER, 512, 128), full3)
    vec = pl.BlockSpec((1, N_STATE), full2)
    return pl.pallas_call(
        body, name="ssm_bwd", grid=(n_chunks,),
        in_specs=[pl.BlockSpec((chunk, SSM_WIDTH), rev),
                  pl.BlockSpec((chunk, SSM_WIDTH), lambda i: (n_chunks - 1 - i, u_col)),
                  pl.BlockSpec((chunk, N_STATE), rev), pl.BlockSpec((chunk, N_STATE), rev),
                  pl.BlockSpec((SUB, N_STATE), above), pl.BlockSpec((SUB, N_STATE), above),
                  b_spec, b_spec, c_spec, c_spec, vec, vec, pl.BlockSpec((1, SSM_WIDTH), full2)],
        out_specs=[pl.BlockSpec((chunk, SSM_WIDTH), rev), vec, vec, b_spec, b_spec, c_spec, c_spec,
                   pl.BlockSpec((1, SSM_WIDTH), full2)],
        out_shape=[jax.ShapeDtypeStruct((s, SSM_WIDTH), BF16),
                   jax.ShapeDtypeStruct((1, N_STATE), F32), jax.ShapeDtypeStruct((1, N_STATE), F32),
                   jax.ShapeDtypeStruct((SSM_SUPER, 128, 512), F32), jax.ShapeDtypeStruct((SSM_SUPER, 128, 512), F32),
                   jax.ShapeDtypeStruct((SSM_SUPER, 512, 128), F32), jax.ShapeDtypeStruct((SSM_SUPER, 512, 128), F32),
                   jax.ShapeDtypeStruct((1, SSM_WIDTH), F32)],
        scratch_shapes=[pltpu.VMEM((chunk, N_STATE), F32), pltpu.VMEM((chunk, N_STATE), F32),
                        pltpu.VMEM((SUB, N_STATE), F32), pltpu.VMEM((SUB, N_STATE), F32),
                        pltpu.VMEM((N_SCAN_TABLES, SUB, N_STATE), F32)],
        compiler_params=_params(1),
    )(dys, z, h_re, h_im, h_re, h_im, b_re, b_im, c_re, c_im, lam_re, lam_im, d_skip)


def _local_step(x, p, pos, tgt, sm, w_in_own, w_in_buf, late_bufs, place):
    s = x.shape[0]
    tm = min(512, s)
    ts = min(2048, s)
    chunk = min(256, s)
    ni = s // tm
    nk = s // ts
    g_mix, g_ffn, g_final = sm["g_mix"], sm["g_ffn"], sm["g_final"]
    rowblk, rowmap = _rows(tm, D_MODEL)
    vec1k = ((1, D_MODEL), lambda *_: (0, 0))

    (n1,) = _ew("rms_mix", (ni,), [(x, rowblk, rowmap), (g_mix, *vec1k)], [((s, D_MODEL), BF16, rowblk, rowmap)],
                lambda pids, h, g: ((_rms_fwd_tile(h, g),), ()))

    half_in = IN_WIDTH // 8
    tmb = min(1024, s)
    nib = s // tmb
    chip_w = IN_WIDTH // N_CHIPS
    w_start, w_forward, w_finish = _gather_stages(1)
    gather_in = dict(ins=[w_in_buf], outs=[jax.ShapeDtypeStruct(w_in_buf.shape, w_in_buf.dtype)], aliased=True,
                     sems=[3] * 4, stages=[(0, w_start), (nib - 1, w_forward), (nib - 1, w_finish)])
    a_rows = lambda i, j, k, pv: (i, 0)
    z_own, w_in_all = _mm("in_proj_own", (nib, 1, 1),
                          [(n1, (tmb, D_MODEL), a_rows, w_in_own, (D_MODEL, chip_w), lambda i, j, k, pv: (0, 0))], "nn",
                          [((s, IN_WIDTH), BF16, (tmb, chip_w), lambda i, j, k, pv: (i, pv[P_CHIP]))],
                          comm=gather_in, place=place)
    w_in = w_in_all.reshape(N_CHIPS, D_MODEL, chip_w)
    n_late = len(late_bufs)
    g_start, g_forward, g_finish = _gather_stages(n_late)
    in_steps = (N_CHIPS - 1) * nib
    gather = dict(ins=late_bufs, outs=[jax.ShapeDtypeStruct(b.shape, b.dtype) for b in late_bufs], aliased=True,
                  sems=[3 * n_late] * 4,
                  stages=[(0, g_start), ((4 * in_steps) // 5, g_forward), (in_steps - 1, g_finish)])
    other = lambda j, pv: (pv[P_CHIP] + 1 + j) % N_CHIPS
    z, *late = _mm("in_proj", (nib, N_CHIPS - 1, 1),
                   [(n1, (tmb, D_MODEL), a_rows, w_in, (None, D_MODEL, chip_w), lambda i, j, k, pv: (other(j, pv), 0, 0))],
                   "nn", [((s, IN_WIDTH), BF16, (tmb, chip_w), lambda i, j, k, pv: (i, other(j, pv)))], j_outer=True,
                   comm=gather, place=place, fill=z_own)
    w_ap, w_ga, w_gb, w_out, w_fg, w_fu, w_fd, w_pg, w_pp = (
        g.reshape(N_CHIPS, 2 * g.shape[2], g.shape[3]) for g in late)
    w_out2 = w_out.reshape(D_MODEL, D_MODEL)
    w_pg2 = w_pg.reshape(D_MODEL, D_MODEL)

    inv = ROPE_THETA ** (-jnp.arange(ROPE_HALF, dtype=F32) * 2.0 / ROPE_DIM)
    inv_row = jnp.concatenate([inv, inv, jnp.zeros((HEAD_DIM - ROPE_DIM,), F32)]).reshape(1, HEAD_DIM)
    tabs = _rope_tables(pos.astype(F32).reshape(s, 1), inv_row, tm)

    qk0, qkv_views = _qkv_layout(z, tabs, tm)
    v0_col = (2 * N_GROUPS * GROUP_WIDTH) // GROUP_WIDTH
    group_in = [((qk0, qk0, z), (0, 1, v0_col))] + [(trio, (0, 0, 0)) for trio in qkv_views]
    fwd_out = [_attn_fwd(*arrs, dil, cols3) for (arrs, cols3), dil in zip(group_in, GROUP_DILATIONS)]
    attn, lse, merged_views = _attn_merge(fwd_out[0][0], fwd_out[0][1], fwd_out[1:], tm)

    def chip_cols(parts):
        return (jnp.concatenate(parts, axis=1),), ()

    def proj_cols(name, a, width, w):
        blk = (None, width, 256)
        pairs = [(a, (tmb, width), lambda i, j, k: (i, 0), w, blk, lambda i, j, k: (0, 0, 0))]
        pairs += [(None, None, None, w, blk, (lambda i, j, k, q=q: (q, 0, 0))) for q in range(1, N_CHIPS)]
        return _mm(name, (nib, 1, 1), pairs, "nn", [((s, D_MODEL), BF16, (tmb, D_MODEL), lambda i, j, k: (i, 0))],
                   epilogue=chip_cols, sum_pairs=False)[0]

    def proj512(name, a, w):
        return proj_cols(name, a, GROUP_WIDTH, w)

    attn_d = proj512("attn_proj", attn, w_ap)

    bt_re = jnp.transpose(sm["b_re"], (0, 2, 1))
    bt_im = jnp.transpose(sm["b_im"], (0, 2, 1))
    log_dt_col = sm["log_dt"].reshape(SSM_GROUPS, 1)
    lam_re, lam_im, bbt_re, bbt_im = _ssm_prep(sm["a_re"], sm["a_im"], log_dt_col, bt_re, bt_im)
    b_re_m = _block_diag(bbt_re, SSM_GROUP, SSM_STATE).astype(BF16)
    b_im_m = _block_diag(bbt_im, SSM_GROUP, SSM_STATE).astype(BF16)
    c_re_m = _block_diag(jnp.transpose(sm["c_re"], (0, 2, 1)), SSM_STATE, SSM_GROUP).astype(BF16)
    c_im_m = _block_diag(jnp.transpose(sm["c_im"], (0, 2, 1)), SSM_STATE, SSM_GROUP).astype(BF16)
    lam_re_row = lam_re.reshape(1, N_STATE)
    lam_im_row = lam_im.reshape(1, N_STATE)
    d_skip_row = sm["d_skip"].reshape(1, SSM_WIDTH)
    h_re, h_im, ys, yg = _ssm_fwd(z, b_re_m, b_im_m, c_re_m, c_im_m, lam_re_row, lam_im_row, d_skip_row, chunk)

    pa = proj512("glu_a", yg, w_ga)
    pb = proj512("glu_b", yg, w_gb)

    ga_blk = ((tm, D_MODEL), lambda i: (i, COL_GA // D_MODEL))
    gs_blk = ((tm, D_MODEL), lambda i: (i, COL_GS // D_MODEL))

    def mix_fn(pids, ga, gs, ad, a, b):
        ga, gs, ad, a, b = (t.astype(F32) for t in (ga, gs, ad, a, b))
        return (_sig(ga) * ad + _sig(gs) * (a * _sig(b)),), ()

    (mix,) = _ew("gate_mix", (ni,), [(z, *ga_blk), (z, *gs_blk), (attn_d, rowblk, rowmap), (pa, rowblk, rowmap),
                                     (pb, rowblk, rowmap)], [((s, D_MODEL), BF16, rowblk, rowmap)], mix_fn)

    def out_epi(acc, xr, g):
        h1 = acc + xr
        return (h1, _rms_fwd_tile(h1, g)), ()

    m3 = lambda i, j, k: (i, 0)
    w3 = lambda i, j, k: (0, 0)
    h1, n2 = _mm("out_proj", (nib, 1, 1), [(mix, (tmb, D_MODEL), m3, w_out2, (D_MODEL, D_MODEL), w3)], "nn",
                 [((s, D_MODEL), F32, (tmb, D_MODEL), m3), ((s, D_MODEL), BF16, (tmb, D_MODEL), m3)],
                 epilogue=out_epi, extras=[(x, (tmb, D_MODEL), m3), (g_ffn, (1, D_MODEL), w3)])

    ffq = (None, tm, D_FF_Q)
    ffq_map = lambda i, j, k: (j, i, 0)

    def ffn_in_epi(parts):
        gts, ups = parts[0::2], parts[1::2]
        acts = [gt * _sig(gt) * u_ for gt, u_ in zip(gts, ups)]
        return (jnp.stack(gts, axis=0), jnp.stack(ups, axis=0), jnp.stack(acts, axis=0)), ()

    w_ffq = (None, D_MODEL, D_FF_Q)
    ff_pairs = []
    for q in range(N_CHIPS):
        blk_q = lambda i, j, k, q=q: (q, 0, 0)
        ff_pairs.append((n2, (tm, D_MODEL), m3, w_fg, w_ffq, blk_q) if q == 0 else (None, None, None, w_fg, w_ffq, blk_q))
        ff_pairs.append((None, None, None, w_fu, w_ffq, blk_q))
    ff_all = (N_CHIPS, tm, D_FF_Q)
    ff_all_map = lambda i, j, k: (0, i, 0)
    gate, up, act = _mm("ffn_gate_up", (ni, 1, 1), ff_pairs, "nn",
                        [((N_CHIPS, s, D_FF_Q), BF16, ff_all, ff_all_map)] * 3, epilogue=ffn_in_epi,
                        sum_pairs=False, resident_b=True)

    (h2,) = _mm("ffn_down", (nib, 1, 1),
                [(act, (None, tmb, D_FF_Q), (lambda i, j, k, q=q: (q, i, 0)), w_fd, (None, D_FF_Q, D_MODEL),
                  (lambda i, j, k, q=q: (q, 0, 0))) for q in range(N_CHIPS)], "nn",
                [((s, D_MODEL), F32, (tmb, D_MODEL), m3)], epilogue=lambda acc, hr: ((acc + hr,), ()),
                extras=[(h1, (tmb, D_MODEL), m3)])

    pp = proj_cols("ple_proj", p, PLE_DIM, w_pp)

    def ple_head_epi(acc, hr, ppr, t, g):
        sg = _sig(acc)
        ppf = ppr.astype(F32)
        h = hr + sg * ppf
        r = lax.rsqrt(jnp.mean(h * h, axis=-1, keepdims=True) + EPS)
        hhat = h * r
        diff = hhat * g - t
        loss = 0.5 * jnp.sum(jnp.mean(diff * diff, axis=-1, keepdims=True))
        dy = diff * (1.0 / D_MODEL)
        gy = dy * g
        dh = r * (gy - hhat * jnp.mean(gy * hhat, axis=-1, keepdims=True))
        return ((dh, dh * ppf * sg * (1.0 - sg), dh * sg),
                (jnp.full((SUB, 128), loss, F32), jnp.sum(dy * hhat, axis=0, keepdims=True)))

    tile_row = (tm, D_MODEL)
    dh3, dgl, dpp, loss_acc, dg_final = _mm(
        "ple_gate_head", (ni, 1, 1), [(h2, tile_row, m3, w_pg2, (D_MODEL, D_MODEL), w3)], "nn",
        [((s, D_MODEL), F32, tile_row, m3), ((s, D_MODEL), BF16, tile_row, m3), ((s, D_MODEL), BF16, tile_row, m3)],
        epilogue=ple_head_epi,
        extras=[(h2, tile_row, m3), (pp, tile_row, m3), (tgt, tile_row, m3), (g_final, (1, D_MODEL), w3)],
        acc_outs=[((SUB, 128), F32), ((1, D_MODEL), F32)])

    def wgrad(name, a, a_block, a_imap, b, b_block, b_imap, out_shape, out_block, out_imap, nj, acc_shape):
        return _mm(name, (1, nj, nk), [(a, a_block, a_imap, b, b_block, b_imap)], "tn",
                   [(out_shape, F32, out_block, out_imap)], acc_shape=acc_shape)[0]

    tk0 = lambda i, j, k: (k, 0)
    tkj = lambda i, j, k: (k, j)
    def wgrad_cols(name, a, width, dy_):
        def split(acc):
            return (jnp.stack([acc[:, q * 256:(q + 1) * 256] for q in range(N_CHIPS)], axis=0),), ()

        return _mm(name, (1, 1, nk), [(a, (ts, width), tk0, dy_, (ts, D_MODEL), tk0)], "tn",
                   [((N_CHIPS, width, 256), F32, (N_CHIPS, width, 256), lambda i, j, k: (0, 0, 0))], epilogue=split,
                   acc_shape=(width, D_MODEL))[0]

    d_w_pp = wgrad_cols("d_ple_proj", p, PLE_DIM, dpp)
    d_w_pg = wgrad("d_ple_gate", h2, (ts, D_MODEL), tk0, dgl, (ts, D_MODEL), tk0, (D_MODEL, D_MODEL),
                   (D_MODEL, D_MODEL), w3, 1, (D_MODEL, D_MODEL))

    (dh2,) = _mm("ple_gate_bwd", (nib, 1, 1), [(dgl, (tmb, D_MODEL), m3, w_pg2, (D_MODEL, D_MODEL), w3)], "nt",
                 [((s, D_MODEL), F32, (tmb, D_MODEL), m3)], epilogue=lambda acc, d_: ((acc + d_,), ()),
                 extras=[(dh3, (tmb, D_MODEL), m3)])

    def ffn_bwd_epi(parts, gt_all, u_all):
        dgs_, dus_ = [], []
        for q, dact in enumerate(parts):
            gt, u_ = gt_all[q].astype(F32), u_all[q].astype(F32)
            sg = _sig(gt)
            dgs_.append(dact * u_ * (sg * (1.0 + gt * (1.0 - sg))))
            dus_.append(dact * gt * sg)
        return (jnp.stack(dgs_, axis=0), jnp.stack(dus_, axis=0)), ()

    fd_pairs = [((dh2, (tm, D_MODEL), m3) if q == 0 else (None, None, None))
                + (w_fd, (None, D_FF_Q, D_MODEL), (lambda i, j, k, q=q: (q, 0, 0))) for q in range(N_CHIPS)]
    dgate, dup = _mm("ffn_down_bwd", (ni, 1, 1), fd_pairs, "nt",
                     [((N_CHIPS, s, D_FF_Q), BF16, ff_all, ff_all_map)] * 2, epilogue=ffn_bwd_epi,
                     extras=[(gate, ff_all, ff_all_map), (up, ff_all, ff_all_map)], sum_pairs=False, resident_b=True)

    ffq_t = (None, ts, D_FF_Q)
    ffq_tmap = lambda i, j, k: (j, k, 0)
    blk_j = lambda i, j, k: (j, 0, 0)
    d_w_fd = wgrad("d_ffn_down", act, ffq_t, ffq_tmap, dh2, (ts, D_MODEL), tk0, (N_CHIPS, D_FF_Q, D_MODEL),
                   (None, D_FF_Q, D_MODEL), blk_j, N_CHIPS, (D_FF_Q, D_MODEL))
    d_w_fg = wgrad("d_ffn_gate", n2, (ts, D_MODEL), tk0, dgate, ffq_t, ffq_tmap, (N_CHIPS, D_MODEL, D_FF_Q),
                   (None, D_MODEL, D_FF_Q), blk_j, N_CHIPS, (D_MODEL, D_FF_Q))
    d_w_fu = wgrad("d_ffn_up", n2, (ts, D_MODEL), tk0, dup, ffq_t, ffq_tmap, (N_CHIPS, D_MODEL, D_FF_Q),
                   (None, D_MODEL, D_FF_Q), blk_j, N_CHIPS, (D_MODEL, D_FF_Q))

    def norm_bwd_epi(acc, h, d_res, g):
        dh, dg = _rms_bwd_tile(acc, h, g)
        return (d_res + dh,), (dg,)

    ffq_k = lambda i, j, k: (k, i, 0)
    blk_k = lambda i, j, k: (k, 0, 0)
    fi_pairs = []
    for q in range(N_CHIPS):
        a_q = lambda i, j, k, q=q: (q, i, 0)
        b_q = lambda i, j, k, q=q: (q, 0, 0)
        fi_pairs.append((dgate, ffq, a_q, w_fg, (None, D_MODEL, D_FF_Q), b_q))
        fi_pairs.append((dup, ffq, a_q, w_fu, (None, D_MODEL, D_FF_Q), b_q))
    dh1, dg_ffn = _mm("ffn_in_bwd", (ni, 1, 1), fi_pairs, "nt",
                      [((s, D_MODEL), F32, (tm, D_MODEL), m3)], epilogue=norm_bwd_epi,
                      extras=[(h1, (tm, D_MODEL), m3), (dh2, (tm, D_MODEL), m3), (g_ffn, (1, D_MODEL), w3)],
                      acc_outs=[((1, D_MODEL), F32)], resident_b=True)

    d_w_out = wgrad("d_out_proj", mix, (ts, D_MODEL), tk0, dh1, (ts, D_MODEL), tk0, (D_MODEL, D_MODEL),
                    (D_MODEL, D_MODEL), w3, 1, (D_MODEL, D_MODEL))

    def mix_bwd_epi(dm, ga, gs, ad, a, b):
        ga, gs, ad, a, b = (t.astype(F32) for t in (ga, gs, ad, a, b))
        s_a, s_s, s_b = _sig(ga), _sig(gs), _sig(b)
        d_ssm = dm * s_s
        return (dm * ad * s_a * (1.0 - s_a), dm * (a * s_b) * s_s * (1.0 - s_s), dm * s_a, d_ssm * s_b,
                d_ssm * a * s_b * (1.0 - s_b)), ()

    tile_m = (tm, D_MODEL)
    dga, dgs, dattn_d, dpa, dpb = _mm(
        "out_proj_bwd", (ni, 1, 1), [(dh1, tile_m, m3, w_out2, (D_MODEL, D_MODEL), w3)], "nt",
        [((s, D_MODEL), BF16, tile_m, m3)] * 5, epilogue=mix_bwd_epi,
        extras=[(z, tile_m, lambda i, j, k: (i, COL_GA // D_MODEL)), (z, tile_m, lambda i, j, k: (i, COL_GS // D_MODEL)),
                (attn_d, tile_m, m3), (pa, tile_m, m3), (pb, tile_m, m3)])

    d_w_ap = wgrad_cols("d_attn_proj", attn, GROUP_WIDTH, dattn_d)
    d_w_ga = wgrad_cols("d_glu_a", yg, GROUP_WIDTH, dpa)
    d_w_gb = wgrad_cols("d_glu_b", yg, GROUP_WIDTH, dpb)

    ik = lambda i, j, k: (i, k)

    def cols_bwd(dy_, w):
        return [(dy_, (tmb, 256), (lambda i, j, k, q=q: (i, q)), w, (None, GROUP_WIDTH, 256),
                 (lambda i, j, k, q=q: (q, 0, 0))) for q in range(N_CHIPS)]

    (dattn,) = _mm("attn_proj_bwd", (nib, 1, 1), cols_bwd(dattn_d, w_ap), "nt",
                   [((s, GROUP_WIDTH), BF16, (tmb, GROUP_WIDTH), m3)])

    (dys,) = _mm("glu_bwd", (nib, 1, 1), cols_bwd(dpa, w_ga) + cols_bwd(dpb, w_gb), "nt",
                 [((s, GROUP_WIDTH), F32, (tmb, GROUP_WIDTH), m3)],
                 epilogue=lambda acc, y_: ((acc * _gelu_grad(y_),), ()),
                 extras=[(ys, (tmb, GROUP_WIDTH), m3)])

    du, d_lr, d_li, d_bre, d_bim, d_cre, d_cim, d_dskip = _ssm_bwd(
        dys, z, h_re, h_im, b_re_m, b_im_m, c_re_m, c_im_m, lam_re_row, lam_im_row, d_skip_row, chunk)

    dattn_views = _to_views(dattn, tm)
    bwd_in = [(dattn, attn, lse)] + [(dv_, ov_, lv_) for dv_, (ov_, lv_) in zip(dattn_views, merged_views)]
    qkv_grads = [_attn_bwd(*arrs, *dol, dil, cols3)
                 for (arrs, cols3), dol, dil in zip(group_in, bwd_in, GROUP_DILATIONS)]
    early = [d_w_ap, d_w_ga, d_w_gb, d_w_out.reshape(N_CHIPS, D_MODEL // N_CHIPS, D_MODEL), d_w_fg, d_w_fu, d_w_fd,
             d_w_pg.reshape(N_CHIPS, D_MODEL // N_CHIPS, D_MODEL), d_w_pp]
    early5 = [g.reshape(N_CHIPS, 2, g.shape[1] // 2, g.shape[2]) for g in early]
    n_e = len(early5)
    p_start, p_finish = _pair_exchange_stages(n_e)
    dz, early_theirs = _dz_layout(
        qkv_grads, du, dga, dgs, tabs, tm,
        comm=dict(ins=early5, outs=_pair_exchange_shapes(early5), sems=[N_CHIPS * n_e] * 2, start=p_start, finish=p_finish))
    early_parts = [_pair_sum(g, t, place) for g, t in zip(early5, early_theirs)]

    chip_in = IN_WIDTH // N_CHIPS
    ip_pairs = [(dz, (tm, chip_in), (lambda i, j, k, q=q: (i, q)), w_in, (None, D_MODEL, chip_in),
                 (lambda i, j, k, q=q: (q, 0, 0))) for q in range(N_CHIPS)]
    grad_x, dg_mix = _mm("in_proj_bwd", (ni, 1, 1), ip_pairs, "nt",
                         [((s, D_MODEL), F32, (tm, D_MODEL), m3)], epilogue=norm_bwd_epi,
                         extras=[(x, (tm, D_MODEL), m3), (dh1, (tm, D_MODEL), m3), (g_mix, (1, D_MODEL), w3)],
                         acc_outs=[((1, D_MODEL), F32)], resident_b=True)

    d_bbt_re = _block_diag_t(d_bre, SSM_GROUP, SSM_STATE)
    d_bbt_im = _block_diag_t(d_bim, SSM_GROUP, SSM_STATE)
    d_a_re, d_a_im, d_log_dt, d_bt_re, d_bt_im = _ssm_param_bwd(
        sm["a_re"], sm["a_im"], log_dt_col, bt_re, bt_im,
        d_lr.reshape(SSM_GROUPS, SSM_STATE), d_li.reshape(SSM_GROUPS, SSM_STATE), d_bbt_re, d_bbt_im)
    small = {
        "g_mix": dg_mix, "a_re": d_a_re, "a_im": d_a_im, "log_dt": d_log_dt,
        "b_re": jnp.transpose(d_bt_re, (0, 2, 1)), "b_im": jnp.transpose(d_bt_im, (0, 2, 1)),
        "c_re": jnp.transpose(_block_diag_t(d_cre, SSM_STATE, SSM_GROUP), (0, 2, 1)),
        "c_im": jnp.transpose(_block_diag_t(d_cim, SSM_STATE, SSM_GROUP), (0, 2, 1)),
        "d_skip": d_dskip, "g_ffn": dg_ffn, "g_final": dg_final,
    }
    vec = _pack([small[n] for n in SMALL] + [loss_acc[0, 0].reshape(1)])

    x_start, x_finish = _chip_exchange_stages(n_e)
    v_start, v_finish = _all_exchange_stages()

    def both(f_chips, f_vec):
        def stage(ins, outs, sems):
            f_chips(ins[:n_e], outs[:n_e], sems[:2])
            f_vec(ins[n_e:], outs[n_e:], sems[2:])
        return stage

    ts_in = min(2048, s)
    win_steps = 8 * (s // ts_in)
    exchange = dict(ins=early_parts + [vec],
                    outs=[jax.ShapeDtypeStruct(t.shape, t.dtype) for t in early_parts]
                    + [jax.ShapeDtypeStruct((8,) + vec.shape, vec.dtype)],
                    aliased=False, sems=[3 * n_e, 3 * n_e, 7, 7],
                    stages=[(0, both(x_start, v_start)), (win_steps - 1, both(x_finish, v_finish))])
    d_w_in, *got = _mm("d_in_proj", (1, 8, s // ts_in), [(n1, (ts_in, D_MODEL), tk0, dz, (ts_in, half_in), tkj)], "tn",
                       [((N_CHIPS, D_MODEL, IN_WIDTH // N_CHIPS), F32, (None, D_MODEL, half_in),
                         lambda i, j, k: (j // 2, 0, j % 2))], acc_shape=(D_MODEL, half_in), comm=exchange)
    return grad_x, d_w_in, early_parts, got[:n_e], vec, got[n_e]


BIG = ("w_in", "w_attn_proj", "w_glu_a", "w_glu_b", "w_out", "w_ffn_gate", "w_ffn_up", "w_ffn_down", "w_ple_gate",
       "w_ple_proj")
SMALL = ("g_mix", "a_re", "a_im", "log_dt", "b_re", "b_im", "c_re", "c_im", "d_skip", "g_ffn", "g_final")
ANY = pl.BlockSpec(memory_space=pl.ANY)


def _place():
    x, y, c = lax.axis_index("x"), lax.axis_index("y"), lax.axis_index("c")
    chips = [(1 - x, y), (x, 1 - y), (1 - x, 1 - y)]
    return x, y, c, chips


def _remote(src, dst, send_sem, recv_sem, to):
    return pltpu.make_async_remote_copy(src_ref=src, dst_ref=dst, send_sem=send_sem, recv_sem=recv_sem, device_id=to,
                                        device_id_type=MESH)


def _comm_call(name, body, ins, out_shapes, n_sems, aliases=None):
    n_w = len(ins)
    return pl.pallas_call(
        body, name=name, in_specs=[ANY] * n_w, out_specs=[ANY] * len(out_shapes), out_shape=out_shapes,
        scratch_shapes=[pltpu.SemaphoreType.DMA((n,)) for n in n_sems], input_output_aliases=aliases or {},
    )(*ins)


def _gather_weights(bufs):
    n_w = len(bufs)
    start, forward, finish = _gather_stages(n_w)

    def body(*refs):
        ins, outs, sems = refs[:n_w], refs[n_w:2 * n_w], refs[2 * n_w:]
        start(ins, outs, sems)
        forward(ins, outs, sems)
        finish(ins, outs, sems)

    out_shapes = [jax.ShapeDtypeStruct(b.shape, b.dtype) for b in bufs]
    return _comm_call("gather_weights", body, bufs, out_shapes, [3 * n_w] * 4, aliases={w: w for w in range(n_w)})


def _gather_stages(n_w):
    def each():
        x, y, c, chips = _place()
        for w in range(n_w):
            for j, (cx, cy) in enumerate(chips):
                yield w, 3 * w + j, 2 * x + y, 2 * cx + cy, (cx, cy, c), (x, y, 1 - c), c

    def start(ins, outs, sems):
        for w, k, me, _, peer, _, c in each():
            mine = outs[w].at[me, c]
            _remote(mine, mine, sems[0].at[k], sems[1].at[k], peer).start()

    def forward(ins, outs, sems):
        for w, k, _, src_chip, peer, sib, c in each():
            landed = outs[w].at[src_chip, c]
            _remote(landed, landed, sems[0].at[k], sems[1].at[k], peer).wait_recv()
            _remote(landed, landed, sems[2].at[k], sems[3].at[k], sib).start()

    def finish(ins, outs, sems):
        for w, k, me, src_chip, peer, sib, c in each():
            other = outs[w].at[src_chip, 1 - c]
            _remote(other, other, sems[2].at[k], sems[3].at[k], sib).wait_recv()
        for w, k, me, src_chip, peer, sib, c in each():
            mine = outs[w].at[me, c]
            _remote(mine, mine, sems[0].at[k], sems[1].at[k], peer).wait_send()
            landed = outs[w].at[src_chip, c]
            _remote(landed, landed, sems[2].at[k], sems[3].at[k], sib).wait_send()

    return start, forward, finish


def _pair_exchange(grads):
    n_w = len(grads)
    start, finish = _pair_exchange_stages(n_w)

    def body(*refs):
        ins, outs, sems = refs[:n_w], refs[n_w:2 * n_w], refs[2 * n_w:]
        start(ins, outs, sems)
        finish(ins, outs, sems)

    return _comm_call("grad_pair_exchange", body, grads, _pair_exchange_shapes(grads), [N_CHIPS * n_w] * 2)


def _pair_exchange_shapes(grads):
    return [jax.ShapeDtypeStruct((N_CHIPS,) + g.shape[2:], g.dtype) for g in grads]


def _pair_exchange_stages(n_w):
    def each():
        x, y, c, _ = _place()
        for w in range(n_w):
            for q in range(N_CHIPS):
                yield w, q, N_CHIPS * w + q, c, (x, y, 1 - c)

    def start(ins, outs, sems):
        for w, q, k, c, sib in each():
            _remote(ins[w].at[q, 1 - c], outs[w].at[q], sems[0].at[k], sems[1].at[k], sib).start()

    def finish(ins, outs, sems):
        for w, q, k, c, sib in each():
            _remote(ins[w].at[q, 1 - c], outs[w].at[q], sems[0].at[k], sems[1].at[k], sib).wait()

    return start, finish


def _chip_exchange(parts):
    n_w = len(parts)

    start, finish = _chip_exchange_stages(n_w)

    def body(*refs):
        ins, outs, sems = refs[:n_w], refs[n_w:2 * n_w], refs[2 * n_w:]
        start(ins, outs, sems)
        finish(ins, outs, sems)

    out_shapes = [jax.ShapeDtypeStruct(t.shape, t.dtype) for t in parts]
    return _comm_call("grad_chip_exchange", body, parts, out_shapes, [3 * n_w, 3 * n_w])


def _chip_exchange_stages(n_w):
    def each():
        x, y, c, chips = _place()
        for w in range(n_w):
            for j, (cx, cy) in enumerate(chips):
                yield w, 3 * w + j, 2 * x + y, 2 * cx + cy, (cx, cy, c)

    def start(ins, outs, sems):
        for w, k, me, peer_chip, peer in each():
            _remote(ins[w].at[peer_chip], outs[w].at[me], sems[0].at[k], sems[1].at[k], peer).start()

    def finish(ins, outs, sems):
        for w, k, me, peer_chip, peer in each():
            got = outs[w].at[peer_chip]
            _remote(got, got, sems[0].at[k], sems[1].at[k], peer).wait_recv()
        for w, k, me, peer_chip, peer in each():
            _remote(ins[w].at[peer_chip], outs[w].at[me], sems[0].at[k], sems[1].at[k], peer).wait_send()

    return start, finish


def _pair_gather(halves):
    n_w = len(halves)

    def body(*refs):
        ins, outs = refs[:n_w], refs[n_w:2 * n_w]
        send, recv = refs[2 * n_w:]
        x, y, c, _ = _place()
        sib = (x, y, 1 - c)
        cps = []
        for w in range(n_w):
            cp = _remote(ins[w], outs[w], send.at[w], recv.at[w], sib)
            cp.start()
            cps.append(cp)
        for cp in cps:
            cp.wait()

    out_shapes = [jax.ShapeDtypeStruct(h.shape, h.dtype) for h in halves]
    return _comm_call("grad_pair_gather", body, halves, out_shapes, [n_w] * 2)


def _all_exchange_stages():
    def each():
        x, y, c, _ = _place()
        for k in range(1, 8):
            px, py, pc = x ^ ((k >> 2) & 1), y ^ ((k >> 1) & 1), c ^ (k & 1)
            yield k - 1, 4 * x + 2 * y + c, 4 * px + 2 * py + pc, (px, py, pc)

    def start(ins, outs, sems):
        for k, me, _, peer in each():
            _remote(ins[0], outs[0].at[me], sems[0].at[k], sems[1].at[k], peer).start()

    def finish(ins, outs, sems):
        for k, me, src, peer in each():
            got = outs[0].at[src]
            _remote(got, got, sems[0].at[k], sems[1].at[k], peer).wait_recv()
        for k, me, src, peer in each():
            _remote(ins[0], outs[0].at[me], sems[0].at[k], sems[1].at[k], peer).wait_send()

    return start, finish


def _row_tile(r):
    for t in (256, 128, 176, 64, 32, 16, 8):
        if r % t == 0:
            return t
    return r


P_C, P_CHIP, P_DEV = 2, 3, 4


def _cast_shard(w2):
    r, c = w2.shape
    t = _row_tile(r)
    blk, imap = _rows(t, c)
    return _ew("cast_own", (r // t,), [(w2, blk, imap)], [((r, c), BF16, blk, imap)], lambda pids, a: ((a,), ()))[0]


def _cast_into_slot(w2, place):
    r, c = w2.shape
    t = _row_tile(r)
    return _ew("cast_shard", (r // t,), [(w2, (t, c), lambda i, pv: (i, 0))],
               [((N_CHIPS, r, c), BF16, (None, t, c), lambda i, pv: (pv[P_CHIP], i, 0))],
               lambda pids, a: ((a,), ()), place=place)[0]


def _pair_sum(mine, theirs, place):
    _, r, c = theirs.shape
    t = _row_tile(r)
    own = ((None, None, t, c), lambda q, i, pv: (q, pv[P_C], i, 0))
    blk = ((None, t, c), lambda q, i, pv: (q, i, 0))
    return _ew("grad_pair_sum", (N_CHIPS, r // t), [(mine, *own), (theirs, *blk)], [((N_CHIPS, r, c), BF16, *blk)],
               lambda pids, a, b: ((a + b,), ()), place=place)[0]


def _chip_sum(own, got, place):
    _, r, c = own.shape
    t = _row_tile(r)
    ins = []
    for q in range(N_CHIPS):
        ins.append((own, (None, t, c), (lambda i, pv, q=q: (q, i, 0))))
        ins.append((got, (None, t, c), (lambda i, pv, q=q: (jnp.where(pv[P_CHIP] == q, (q + 1) % N_CHIPS, q), i, 0))))

    def fn(pids, *tiles):
        me = pids[0][P_CHIP]
        tot = None
        for q in range(N_CHIPS):
            term = jnp.where(me == q, tiles[2 * q], tiles[2 * q + 1]).astype(F32)
            tot = term if tot is None else tot + term
        return (tot,), ()

    return _ew("grad_chip_sum", (r // t,), ins, [((r, c), F32, (t, c), lambda i, pv: (i, 0))], fn, place=place)[0]


def _adamw_tile(w, g, m, v):
    m = ADAM_B1 * m + (1.0 - ADAM_B1) * g
    v = ADAM_B2 * v + (1.0 - ADAM_B2) * (g * g)
    m_hat = m / (1.0 - ADAM_B1 ** ADAM_STEP)
    v_hat = v / (1.0 - ADAM_B2 ** ADAM_STEP)
    delta = -ADAM_LR * (m_hat / (jnp.sqrt(v_hat) + ADAM_EPS) + ADAM_WD * w)
    return delta, m, v


def _adamw(name, g2, w2, m2, v2):
    r, c = w2.shape
    t = _row_tile(r)
    blk, imap = _rows(t, c)

    def fn(pids, g, w, m, v):
        delta, nm, nv = _adamw_tile(w, g, m, v)
        return (g, delta, nm, nv), ()

    return _ew(name, (r // t,), [(a, blk, imap) for a in (g2, w2, m2, v2)], [((r, c), F32, blk, imap)] * 4, fn)


def _adamw_halves(name, mine, theirs, w2, m2, v2, place):
    r, c = w2.shape
    t = _row_tile(r // 2)
    n_t = (r // 2) // t
    half = ((t, c), lambda h, i, pv: (i, 0))
    whole = ((t, c), lambda h, i, pv: (h * n_t + i, 0))

    def fn(pids, ga, gb, w, m, v):
        g = jnp.where(pids[1] == pids[0][P_C], ga, gb)
        delta, nm, nv = _adamw_tile(w, g, m, v)
        return (g, delta, nm, nv), ()

    return _ew(name, (2, n_t), [(mine, *half), (theirs, *half), (w2, *whole), (m2, *whole), (v2, *whole)],
               [((r, c), F32, *whole)] * 4, fn, place=place)


def _device_sum(own, got, place):
    r, c = own.shape
    t = _row_tile(r)
    ins = [(own, (t, c), lambda i, pv: (i, 0))]
    for q in range(8):
        ins.append((got, (None, t, c), (lambda i, pv, q=q: (jnp.where(pv[P_DEV] == q, (q + 1) % 8, q), i, 0))))

    def fn(pids, mine, *parts):
        me = pids[0][P_DEV]
        tot = None
        for q in range(8):
            term = jnp.where(me == q, mine, parts[q])
            tot = term if tot is None else tot + term
        return (tot,), ()

    return _ew("small_device_sum", (r // t,), ins, [((r, c), F32, (t, c), lambda i, pv: (i, 0))], fn, place=place)[0]


def _pack(parts):
    flat = jnp.concatenate([a.reshape(-1) for a in parts])
    pad = (-flat.shape[0]) % (SUB * 128)
    return jnp.pad(flat, (0, pad)).reshape(-1, 128)


def _unpack(mat, shapes):
    flat = mat.reshape(-1)
    out, off = [], 0
    for shp in shapes:
        n = math.prod(shp)
        out.append(flat[off:off + n].reshape(shp))
        off += n
    return out


def kernel(x, p, positions, g_mix, w_in, a_re, a_im, log_dt, b_re, b_im, c_re, c_im, d_skip, w_attn_proj, w_glu_a, w_glu_b, w_out, g_ffn, w_ffn_gate, w_ffn_up, w_ffn_down, w_ple_gate, w_ple_proj, g_final, loss_target, m_g_mix, m_w_in, m_a_re, m_a_im, m_log_dt, m_b_re, m_b_im, m_c_re, m_c_im, m_d_skip, m_w_attn_proj, m_w_glu_a, m_w_glu_b, m_w_out, m_g_ffn, m_w_ffn_gate, m_w_ffn_up, m_w_ffn_down, m_w_ple_gate, m_w_ple_proj, m_g_final, v_g_mix, v_w_in, v_a_re, v_a_im, v_log_dt, v_b_re, v_b_im, v_c_re, v_c_im, v_d_skip, v_w_attn_proj, v_w_glu_a, v_w_glu_b, v_w_out, v_g_ffn, v_w_ffn_gate, v_w_ffn_up, v_w_ffn_down, v_w_ple_gate, v_w_ple_proj, v_g_final):
    given = dict(locals())
    big_w = {n: given[n] for n in BIG}
    w_mats = {n: big_w[n].reshape(big_w[n].shape[1:]) for n in BIG}

    ax, ay, ac = lax.axis_index("x"), lax.axis_index("y"), lax.axis_index("c")
    place = jnp.stack([ax, ay, ac, 2 * ax + ay, 4 * ax + 2 * ay + ac]).astype(jnp.int32)

    bufs = []
    for n in BIG:
        r, c = w_mats[n].shape
        bufs.append(_cast_into_slot(w_mats[n], place).reshape(N_CHIPS, 2, r // 2, c))
    w_in_own = _cast_shard(w_mats["w_in"])

    sm = {
        "g_mix": g_mix.reshape(1, D_MODEL), "g_ffn": g_ffn.reshape(1, D_MODEL), "g_final": g_final.reshape(1, D_MODEL),
        "a_re": a_re[0], "a_im": a_im[0], "log_dt": log_dt[0], "b_re": b_re[0], "b_im": b_im[0], "c_re": c_re[0],
        "c_im": c_im[0], "d_skip": d_skip[0],
    }
    s = x.shape[1]
    grad_x, d_w_in, early_parts, early_got, vec, vec_got = _local_step(
        x[0], p[0, 0], positions[0], loss_target[0], sm, w_in_own, bufs[0], bufs[1:], place)

    r_in, c_in = w_mats["w_in"].shape
    g5_in = [d_w_in.reshape(N_CHIPS, 2, r_in // 2, c_in)]
    in_parts = [_pair_sum(g, t, place) for g, t in zip(g5_in, _pair_exchange(g5_in))]
    chip_parts = in_parts + list(early_parts)
    chip_got = list(_chip_exchange(in_parts)) + list(early_got)
    halves = [_chip_sum(own, got, place) for own, got in zip(chip_parts, chip_got)]
    other_halves = _pair_gather(halves)

    results = {}
    for n, mine, other in zip(BIG, halves, other_halves):
        r, c = w_mats[n].shape
        shp = big_w[n].shape
        outs = _adamw_halves("adamw_" + n, mine, other, w_mats[n], given["m_" + n].reshape(r, c),
                             given["v_" + n].reshape(r, c), place)
        results[n] = [o.reshape(shp) for o in outs]

    small_shapes = [given[n].shape for n in SMALL]
    tot = _device_sum(vec, vec_got, place)
    n_small = sum(math.prod(shp) for shp in small_shapes)
    loss = tot.reshape(-1)[n_small]
    w_s = _pack([given[n] for n in SMALL])
    m_s = _pack([given["m_" + n] for n in SMALL])
    v_s = _pack([given["v_" + n] for n in SMALL])
    rows_s = w_s.shape[0]
    g_s = tot.reshape(-1)[: rows_s * 128].reshape(rows_s, 128)
    outs_s = _adamw("adamw_small", g_s, w_s, m_s, v_s)
    for kind, mat in enumerate(outs_s):
        for n, arr in zip(SMALL, _unpack(mat, small_shapes)):
            results.setdefault(n, [None] * 4)[kind] = arr

    order = ("g_mix", "w_in", "a_re", "a_im", "log_dt", "b_re", "b_im", "c_re", "c_im", "d_skip", "w_attn_proj", "w_glu_a",
             "w_glu_b", "w_out", "g_ffn", "w_ffn_gate", "w_ffn_up", "w_ffn_down", "w_ple_gate", "w_ple_proj", "g_final")
    out = [loss, grad_x.reshape(1, s, D_MODEL)]
    for kind in range(4):
        out += [results[n][kind] for n in order]
    return tuple(out)
```

```python
import math

import jax
import jax.numpy as jnp
from jax import lax
from jax.experimental import pallas as pl
from jax.experimental.pallas import tpu as pltpu

F32 = jnp.float32
BF16 = jnp.bfloat16

D_MODEL = 1024
HEAD_DIM = 128
HEADS_PER_GROUP = 4
GROUP_WIDTH = HEADS_PER_GROUP * HEAD_DIM
GROUP_DILATIONS = (1, 4, 16)
N_GROUPS = len(GROUP_DILATIONS)
LSE_LANES = 32
LSE_WIDTH = HEADS_PER_GROUP * LSE_LANES
ATTN_BLOCK = 128
ROPE_DIM = 32
ROPE_HALF = 16
ROPE_THETA = 500000.0
SSM_WIDTH = 512
SSM_GROUPS = 32
SSM_GROUP = 16
SSM_STATE = 64
N_STATE = SSM_GROUPS * SSM_STATE
SSM_SUPER = 4
IN_WIDTH = 7168
COL_U = 4608
COL_GA = 5120
COL_GS = 6144
D_FF = 2816
N_CHIPS = 4
D_FF_Q = D_FF // N_CHIPS
PLE_DIM = 256
EPS = 1e-6
ADAM_LR = 0.001
ADAM_B1 = 0.9
ADAM_B2 = 0.999
ADAM_EPS = 1e-08
ADAM_WD = 0.01
ADAM_STEP = 10
NEG_BIG = -1e30
VMEM_LIMIT_BYTES = 56 * 1024 * 1024
MESH = pl.DeviceIdType.MESH

_DIMS = {
    "nn": (((1,), (0,)), ((), ())),
    "nt": (((1,), (1,)), ((), ())),
    "tn": (((0,), (0,)), ((), ())),
}


def _params(n_grid):
    return pltpu.CompilerParams(dimension_semantics=("arbitrary",) * n_grid, vmem_limit_bytes=VMEM_LIMIT_BYTES)


def _sig(v):
    return 1.0 / (1.0 + jnp.exp(-v))


def _dot(a, b, mode):
    return lax.dot_general(a, b, _DIMS[mode], preferred_element_type=F32)


def _mm(name, grid, pairs, mode, outs, epilogue=None, extras=(), acc_outs=(), acc_shape=None, j_outer=False,
        sum_pairs=True, resident_b=False, comm=None, place=None, fill=None, prologue=None):
    gi, gj, gk = grid
    n_p, n_e, n_o, n_a = len(pairs), len(extras), len(outs), len(acc_outs)
    assert not n_a or gj == 1
    assert sum_pairs or gk == 1
    run_grid = (gj, gi, gk) if j_outer else grid
    c_ins = list(comm["ins"]) if comm else []
    c_outs = list(comm["outs"]) if comm else []
    c_sems = list(comm["sems"]) if comm else []
    n_ci, n_co, n_cs = len(c_ins), len(c_outs), len(c_sems)
    n_s = 0 if place is None else 1
    n_fill = 0 if fill is None else 1

    def order(imap):
        if place is None:
            return (lambda j, i, k: imap(i, j, k)) if j_outer else imap
        return (lambda j, i, k, pv: imap(i, j, k, pv)) if j_outer else imap

    shared_a = [pr[0] is None for pr in pairs]
    n_in = 2 * n_p - sum(shared_a)

    def body(*refs):
        refs = refs[n_s:]
        pair_refs = list(refs[:n_in])
        extra_refs = refs[n_in: n_in + n_e]
        comm_in = refs[n_in + n_e: n_in + n_e + n_ci]
        at = n_in + n_e + n_ci + n_fill
        out_refs = refs[at: at + n_o]
        sum_refs = refs[at + n_o: at + n_o + n_a]
        comm_out = refs[at + n_o + n_a: at + n_o + n_a + n_co]
        scratch_refs = refs[at + n_o + n_a + n_co:]
        i = pl.program_id(1 if j_outer else 0)
        k = pl.program_id(2)
        if comm:
            step = (pl.program_id(0) * run_grid[1] + pl.program_id(1)) * run_grid[2] + pl.program_id(2)
            sems = scratch_refs[len(scratch_refs) - n_cs:]
            for at_step, stage in comm["stages"]:
                @pl.when(step == at_step)
                def _(stage=stage):
                    stage(comm_in, comm_out, sems)
        part = None if sum_pairs else []
        a = None
        for t in range(n_p):
            if not shared_a[t]:
                a = pair_refs.pop(0)[...]
                if prologue is not None and t == 0:
                    a = prologue(a, *[e[...] for e in extra_refs]).astype(BF16)
                    out_refs[n_o - 1][...] = a
                a = a.astype(BF16)
            b = pair_refs.pop(0)[...].astype(BF16)
            d = _dot(a, b, mode)
            if sum_pairs:
                part = d if part is None else part + d
            else:
                part.append(d)

        def finish(acc):
            tiles, sums = epilogue(acc, *[e[...] for e in extra_refs]) if epilogue is not None else ((acc,), ())
            for o_ref, tile in zip(out_refs, tiles):
                o_ref[...] = tile.astype(o_ref.dtype)
            if n_a:
                @pl.when(i == 0)
                def _():
                    for s_ref in sum_refs:
                        s_ref[...] = jnp.zeros_like(s_ref)

                for s_ref, s in zip(sum_refs, sums):
                    s_ref[...] += s

        if gk == 1:
            finish(part)
        else:
            acc_ref = scratch_refs[0]

            @pl.when(k == 0)
            def _():
                acc_ref[...] = part

            @pl.when(k > 0)
            def _():
                acc_ref[...] += part

            @pl.when(k == gk - 1)
            def _():
                finish(acc_ref[...])

    in_specs, args = [], []
    for a, a_block, a_imap, b, b_block, b_imap in pairs:
        if a is not None:
            in_specs.append(pl.BlockSpec(a_block, order(a_imap)))
            args.append(a)
        if resident_b:
            in_specs.append(pl.BlockSpec(b_block, order(b_imap), pipeline_mode=pl.Buffered(1)))
        else:
            in_specs.append(pl.BlockSpec(b_block, order(b_imap)))
        args.append(b)
    for e, e_block, e_imap in extras:
        in_specs.append(pl.BlockSpec(e_block, order(e_imap)))
        args.append(e)
    first_comm_in = len(args)
    for c_in in c_ins:
        in_specs.append(pl.BlockSpec(memory_space=pl.ANY))
        args.append(c_in)
    if n_fill:
        in_specs.append(pl.BlockSpec(memory_space=pl.ANY))
        args.append(fill)
    out_shape = [jax.ShapeDtypeStruct(shape, dtype) for shape, dtype, _, _ in outs]
    out_specs = [pl.BlockSpec(block, order(imap)) for _, _, block, imap in outs]
    for shape, dtype in acc_outs:
        out_shape.append(jax.ShapeDtypeStruct(shape, dtype))
        out_specs.append(pl.BlockSpec(shape, lambda *_: (0, 0)))
    first_comm_out = len(out_shape)
    for c_out in c_outs:
        out_shape.append(c_out)
        out_specs.append(pl.BlockSpec(memory_space=pl.ANY))
    aliases = {n_s + first_comm_in + n: first_comm_out + n for n in range(n_ci)} if comm and comm["aliased"] else {}
    if n_fill:
        aliases[n_s + len(args) - 1] = 0
    scratch = [pltpu.VMEM(acc_shape, F32)] if gk > 1 else []
    scratch += [pltpu.SemaphoreType.DMA((n,)) for n in c_sems]
    if n_s:
        spec = pltpu.PrefetchScalarGridSpec(num_scalar_prefetch=1, grid=run_grid, in_specs=in_specs, out_specs=out_specs,
                                            scratch_shapes=scratch)
        return pl.pallas_call(body, name=name, grid_spec=spec, out_shape=out_shape, compiler_params=_params(3),
                              input_output_aliases=aliases)(place, *args)
    return pl.pallas_call(
        body, name=name, grid=run_grid, in_specs=in_specs, out_specs=out_specs,
        out_shape=out_shape, scratch_shapes=scratch, compiler_params=_params(3), input_output_aliases=aliases,
    )(*args)


def _ew(name, grid, ins, outs, fn, acc_outs=(), place=None):
    n_i, n_o, n_a = len(ins), len(outs), len(acc_outs)
    ng = len(grid)
    n_s = 0 if place is None else 1

    def body(*refs):
        in_refs = refs[n_s: n_s + n_i]
        out_refs = refs[n_s + n_i: n_s + n_i + n_o]
        sum_refs = refs[n_s + n_i + n_o:]
        pids = tuple(pl.program_id(a) for a in range(ng))
        if n_s:
            pids = (refs[0],) + pids
        tiles, sums = fn(pids, *[r[...] for r in in_refs])
        for o_ref, tile in zip(out_refs, tiles):
            o_ref[...] = tile.astype(o_ref.dtype)
        if n_a:
            first = pids[0] == 0
            for p_ in pids[1:]:
                first = jnp.logical_and(first, p_ == 0)

            @pl.when(first)
            def _():
                for s_ref in sum_refs:
                    s_ref[...] = jnp.zeros_like(s_ref)

            for s_ref, s in zip(sum_refs, sums):
                s_ref[...] += s

    in_specs = [pl.BlockSpec(block, imap) for _, block, imap in ins]
    out_shape = [jax.ShapeDtypeStruct(shape, dtype) for shape, dtype, _, _ in outs]
    out_specs = [pl.BlockSpec(block, imap) for _, _, block, imap in outs]
    for shape, dtype in acc_outs:
        out_shape.append(jax.ShapeDtypeStruct(shape, dtype))
        out_specs.append(pl.BlockSpec(shape, lambda *_, nd=len(shape): (0,) * nd))
    arrays = [a for a, _, _ in ins]
    if n_s:
        assert not n_a
        spec = pltpu.PrefetchScalarGridSpec(num_scalar_prefetch=1, grid=grid, in_specs=in_specs, out_specs=out_specs)
        return pl.pallas_call(body, name=name, grid_spec=spec, out_shape=out_shape, compiler_params=_params(ng))(
            place, *arrays)
    return pl.pallas_call(
        body, name=name, grid=grid, in_specs=in_specs, out_specs=out_specs, out_shape=out_shape,
        compiler_params=_params(ng),
    )(*arrays)


def _rows(tm, width):
    return (tm, width), (lambda i: (i, 0))


def _rms_fwd_tile(h, g):
    r = lax.rsqrt(jnp.mean(h * h, axis=-1, keepdims=True) + EPS)
    return h * r * g


def _rms_bwd_tile(dn, h, g):
    r = lax.rsqrt(jnp.mean(h * h, axis=-1, keepdims=True) + EPS)
    hhat = h * r
    gy = dn * g
    dh = r * (gy - hhat * jnp.mean(gy * hhat, axis=-1, keepdims=True))
    dg = jnp.sum(dn * hhat, axis=0, keepdims=True)
    return dh, dg


def _rope_tables(pos_col, inv_row, tm):
    s = pos_col.shape[0]

    def fn(pids, pos, inv):
        ang = pos * inv
        lane = lax.broadcasted_iota(jnp.int32, ang.shape, 1)
        cs = jnp.where(lane < ROPE_DIM, jnp.cos(ang), 1.0)
        sn = jnp.sin(ang)
        s_lo = jnp.where(lane < ROPE_HALF, -sn, 0.0)
        s_hi = jnp.where(jnp.logical_and(lane >= ROPE_HALF, lane < ROPE_DIM), sn, 0.0)
        return (cs, s_lo, s_hi), ()

    blk, imap = _rows(tm, 128)
    return _ew(
        "rope_tables", (s // tm,),
        [(pos_col, (tm, 1), lambda i: (i, 0)), (inv_row, (1, 128), lambda i: (0, 0))],
        [((s, 128), F32, blk, imap)] * 3, fn,
    )


def _rope(xh, cs, s_lo, s_hi):
    return xh * cs + pltpu.roll(xh, HEAD_DIM - ROPE_HALF, 1) * s_lo + pltpu.roll(xh, ROPE_HALF, 1) * s_hi


def _rope_t(gh, cs, s_lo, s_hi):
    return gh * cs + pltpu.roll(gh * s_lo, ROPE_HALF, 1) + pltpu.roll(gh * s_hi, HEAD_DIM - ROPE_HALF, 1)


def _attn_geometry(length):
    nb = length // ATTN_BLOCK
    gq = min(4, nb)
    assert nb % gq == 0
    return nb, gq, gq * ATTN_BLOCK, nb // gq


def _band_masks():
    qi = lax.broadcasted_iota(jnp.int32, (ATTN_BLOCK, ATTN_BLOCK), 0)
    kj = lax.broadcasted_iota(jnp.int32, (ATTN_BLOCK, ATTN_BLOCK), 1)
    return kj <= qi, kj >= qi


def _band_mask_pair():
    qi = lax.broadcasted_iota(jnp.int32, (ATTN_BLOCK, 2 * ATTN_BLOCK), 0)
    cj = lax.broadcasted_iota(jnp.int32, (ATTN_BLOCK, 2 * ATTN_BLOCK), 1)
    in_cur = cj >= ATTN_BLOCK
    band = jnp.logical_or(jnp.logical_and(in_cur, cj - ATTN_BLOCK <= qi),
                          jnp.logical_and(cj < ATTN_BLOCK, cj >= qi))
    return band, in_cur


def _attn_fwd(qv, kv, vv, dil, cols3=(0, 0, 0)):
    length = qv.shape[0]
    nb, gq, rows, ni = _attn_geometry(length)

    def body(q_ref, kc_ref, kp_ref, vc_ref, vp_ref, o_ref, l_ref):
        i = pl.program_id(1)
        band, in_cur = _band_mask_pair()
        band_first = jnp.logical_and(band, jnp.logical_or(in_cur, i > 0))
        work = []
        for h in range(HEADS_PER_GROUP):
            cols = slice(h * HEAD_DIM, (h + 1) * HEAD_DIM)
            qh = q_ref[:, cols]
            k_all = jnp.concatenate([kp_ref[:, cols], kc_ref[:, cols]], axis=0)
            v_all = jnp.concatenate([vp_ref[:, cols], vc_ref[:, cols]], axis=0)
            for jj in range(gq):
                rws = slice(jj * ATTN_BLOCK, (jj + 1) * ATTN_BLOCK)
                two = slice(jj * ATTN_BLOCK, (jj + 2) * ATTN_BLOCK)
                work.append(dict(h=h, rws=rws, cols=cols, v=v_all[two], first=jj == 0, s=_dot(qh[rws], k_all[two], "nt")))
        for w in work:
            s = jnp.where(band_first if w["first"] else band, w["s"], NEG_BIG)
            m = jnp.max(s, axis=-1, keepdims=True)
            pexp = jnp.exp(s - m)
            w["den"] = jnp.sum(pexp, axis=-1, keepdims=True)
            w["p"] = pexp.astype(BF16)
            w["lse"] = m + jnp.log(w["den"])
        for w in work:
            o = _dot(w["p"], w["v"], "nn")
            o_ref[w["rws"], w["cols"]] = (o * (1.0 / w["den"])).astype(o_ref.dtype)
            l_ref[w["rws"], w["h"] * LSE_LANES:(w["h"] + 1) * LSE_LANES] = jnp.broadcast_to(w["lse"], (ATTN_BLOCK, LSE_LANES))

    def cur(c):
        return pl.BlockSpec((rows, GROUP_WIDTH), lambda r, i: (i, r + c))

    def prev(c):
        return pl.BlockSpec((ATTN_BLOCK, GROUP_WIDTH), lambda r, i: (jnp.maximum(i * gq - 1, 0), r + c))

    cq, ck, cv = cols3
    return pl.pallas_call(
        body, name=f"attn_fwd_d{dil}", grid=(dil, ni),
        in_specs=[cur(cq), cur(ck), prev(ck), cur(cv), prev(cv)],
        out_specs=[cur(0), pl.BlockSpec((rows, LSE_WIDTH), lambda r, i: (i, r))],
        out_shape=[jax.ShapeDtypeStruct((length, dil * GROUP_WIDTH), BF16),
                   jax.ShapeDtypeStruct((length, dil * LSE_WIDTH), F32)],
        compiler_params=_params(2),
    )(qv, kv, kv, vv, vv)


def _attn_bwd(qv, kv, vv, dov, ov, lv, dil, cols3=(0, 0, 0)):
    length = qv.shape[0]
    nb, gq, rows, ni = _attn_geometry(length)
    out_shape = (length, dil * GROUP_WIDTH)

    def body(qc_ref, qn_ref, kc_ref, kp_ref, vc_ref, vp_ref, doc_ref, don_ref, oc_ref, on_ref, lc_ref, ln_ref,
             dq_ref, dk_ref, dv_ref):
        i = pl.program_id(1)
        _, mask_p = _band_masks()
        band, in_cur = _band_mask_pair()
        band_first = jnp.logical_and(band, jnp.logical_or(in_cur, i > 0))
        has_next = i < ni - 1

        last = slice(gq * ATTN_BLOCK, (gq + 1) * ATTN_BLOCK)
        mask_next = jnp.logical_and(mask_p, has_next)

        def rows_of(jj):
            return slice(jj * ATTN_BLOCK, (jj + 1) * ATTN_BLOCK)

        def keys_of(jj):
            return slice(jj * ATTN_BLOCK, (jj + 2) * ATTN_BLOCK)

        heads = []
        for h in range(HEADS_PER_GROUP):
            cols = slice(h * HEAD_DIM, (h + 1) * HEAD_DIM)
            hd = dict(
                cols=cols, q_c=qc_ref[:, cols], q_n=qn_ref[:, cols],
                k_all=jnp.concatenate([kp_ref[:, cols], kc_ref[:, cols]], axis=0),
                v_all=jnp.concatenate([vp_ref[:, cols], vc_ref[:, cols]], axis=0),
                do_c=doc_ref[:, cols], do_n=don_ref[:, cols],
                l_c=lc_ref[:, h * LSE_LANES:h * LSE_LANES + 1], l_n=ln_ref[:, h * LSE_LANES:h * LSE_LANES + 1],
            )
            hd["dl_c"] = jnp.sum(hd["do_c"].astype(F32) * oc_ref[:, cols].astype(F32), axis=-1, keepdims=True)
            hd["dl_n"] = jnp.sum(hd["do_n"].astype(F32) * on_ref[:, cols].astype(F32), axis=-1, keepdims=True)
            hd["s"] = [_dot(hd["q_c"][rows_of(jj)], hd["k_all"][keys_of(jj)], "nt") for jj in range(gq)]
            hd["dp"] = [_dot(hd["do_c"][rows_of(jj)], hd["v_all"][keys_of(jj)], "nt") for jj in range(gq)]
            hd["s"].append(_dot(hd["q_n"], hd["k_all"][last], "nt"))
            hd["dp"].append(_dot(hd["do_n"], hd["v_all"][last], "nt"))
            heads.append(hd)
        for hd in heads:
            hd["p"], hd["ds"] = [], []
            for jj in range(gq + 1):
                if jj < gq:
                    mask, l_col, delta = (band_first if jj == 0 else band), hd["l_c"][rows_of(jj)], hd["dl_c"][rows_of(jj)]
                else:
                    mask, l_col, delta = mask_next, hd["l_n"], hd["dl_n"]
                p = jnp.where(mask, jnp.exp(hd["s"][jj] - l_col), 0.0)
                hd["p"].append(p.astype(BF16))
                hd["ds"].append((p * (hd["dp"][jj] - delta)).astype(BF16))
        for hd in heads:
            cols = hd["cols"]
            dk_blocks, dv_blocks = [None] * (gq + 1), [None] * (gq + 1)

            def add(lst, idx, val):
                lst[idx] = val if lst[idx] is None else lst[idx] + val

            for jj in range(gq):
                qb, dob = hd["q_c"][rows_of(jj)], hd["do_c"][rows_of(jj)]
                dq_ref[rows_of(jj), cols] = _dot(hd["ds"][jj], hd["k_all"][keys_of(jj)], "nn").astype(dq_ref.dtype)
                dk2 = _dot(hd["ds"][jj], qb, "tn")
                dv2 = _dot(hd["p"][jj], dob, "tn")
                add(dk_blocks, jj, dk2[:ATTN_BLOCK])
                add(dk_blocks, jj + 1, dk2[ATTN_BLOCK:])
                add(dv_blocks, jj, dv2[:ATTN_BLOCK])
                add(dv_blocks, jj + 1, dv2[ATTN_BLOCK:])
            add(dk_blocks, gq, _dot(hd["ds"][gq], hd["q_n"], "tn"))
            add(dv_blocks, gq, _dot(hd["p"][gq], hd["do_n"], "tn"))
            for jj in range(gq):
                dk_ref[rows_of(jj), cols] = dk_blocks[jj + 1].astype(dk_ref.dtype)
                dv_ref[rows_of(jj), cols] = dv_blocks[jj + 1].astype(dv_ref.dtype)

    def cur(c):
        return pl.BlockSpec((rows, GROUP_WIDTH), lambda r, i: (i, r + c))

    def prev(c):
        return pl.BlockSpec((ATTN_BLOCK, GROUP_WIDTH), lambda r, i: (jnp.maximum(i * gq - 1, 0), r + c))

    def nxt(c):
        return pl.BlockSpec((ATTN_BLOCK, GROUP_WIDTH), lambda r, i: (jnp.minimum((i + 1) * gq, nb - 1), r + c))

    cq, ck, cv = cols3
    lse_cur = pl.BlockSpec((rows, LSE_WIDTH), lambda r, i: (i, r))
    lse_next = pl.BlockSpec((ATTN_BLOCK, LSE_WIDTH), lambda r, i: (jnp.minimum((i + 1) * gq, nb - 1), r))
    return pl.pallas_call(
        body, name=f"attn_bwd_d{dil}", grid=(dil, ni),
        in_specs=[cur(cq), nxt(cq), cur(ck), prev(ck), cur(cv), prev(cv), cur(0), nxt(0), cur(0), nxt(0), lse_cur, lse_next],
        out_specs=[cur(0), cur(0), cur(0)],
        out_shape=[jax.ShapeDtypeStruct(out_shape, BF16)] * 3,
        compiler_params=_params(2),
    )(qv, qv, kv, kv, vv, vv, dov, dov, ov, ov, lv, lv)


DILATED = tuple((g, d) for g, d in enumerate(GROUP_DILATIONS) if d > 1)


def _spread(scr, slot, tile, out_ref, dil, col, width=GROUP_WIDTH):
    tm = tile.shape[0]
    buf = scr.at[slot]
    buf[...] = tile
    for r in range(dil):
        c0 = r * width + col
        out_ref[:, c0:c0 + HEAD_DIM] = buf[pl.ds(r, tm // dil, stride=dil), :].astype(out_ref.dtype)


def _collect(scr, slot, in_ref, dil, col, width=GROUP_WIDTH):
    tm = scr.shape[1]
    buf = scr.at[slot]
    for r in range(dil):
        c0 = r * width + col
        buf[pl.ds(r, tm // dil, stride=dil), :] = in_ref[:, c0:c0 + HEAD_DIM].astype(F32)
    return buf[...]


def _view_spec(tm, dil, width=GROUP_WIDTH):
    return pl.BlockSpec((tm // dil, dil * width), lambda i: (i, 0))


def _view_shape(s, dil, dtype, width=GROUP_WIDTH):
    return jax.ShapeDtypeStruct((s // dil, dil * width), dtype)


def _qkv_layout(z, tabs, tm):
    s = z.shape[0]
    scale = 1.0 / math.sqrt(HEAD_DIM)
    qkv_width = 3 * N_GROUPS * GROUP_WIDTH

    def body(z_ref, cs_ref, lo_ref, hi_ref, qk0_ref, *rest):
        views, scr = rest[:-1], rest[-1]
        tabs_ = (cs_ref[...], lo_ref[...], hi_ref[...])
        for part in range(3):
            for g, dil in enumerate(GROUP_DILATIONS):
                if part == 2 and dil == 1:
                    continue
                for h in range(HEADS_PER_GROUP):
                    col = part * N_GROUPS * GROUP_WIDTH + g * GROUP_WIDTH + h * HEAD_DIM
                    t = z_ref[:, col:col + HEAD_DIM].astype(F32)
                    if part < 2:
                        t = _rope(t, *tabs_)
                    if part == 0:
                        t = t * scale
                    if dil == 1:
                        c0 = part * GROUP_WIDTH + h * HEAD_DIM
                        qk0_ref[:, c0:c0 + HEAD_DIM] = t.astype(BF16)
                    else:
                        out = views[3 * [gg for gg, _ in DILATED].index(g) + part]
                        _spread(scr, h, t, out, dil, h * HEAD_DIM)

    row = lambda i: (i, 0)
    out_shape = [jax.ShapeDtypeStruct((s, 2 * GROUP_WIDTH), BF16)]
    out_specs = [pl.BlockSpec((tm, 2 * GROUP_WIDTH), row)]
    for _, dil in DILATED:
        out_shape += [_view_shape(s, dil, BF16)] * 3
        out_specs += [_view_spec(tm, dil)] * 3
    res = pl.pallas_call(
        body, name="qkv_layout", grid=(s // tm,),
        in_specs=[pl.BlockSpec((tm, qkv_width), row)] + [pl.BlockSpec((tm, HEAD_DIM), row)] * 3,
        out_specs=out_specs, out_shape=out_shape,
        scratch_shapes=[pltpu.VMEM((HEADS_PER_GROUP, tm, HEAD_DIM), F32)], compiler_params=_params(1),
    )(z, *tabs)
    return res[0], [tuple(res[1 + 3 * n:4 + 3 * n]) for n in range(len(DILATED))]


def _attn_merge(o0, l0, dilated, tm):
    s = o0.shape[0]
    n_d = len(DILATED)

    def body(*refs):
        o0_ref, l0_ref = refs[:2]
        in_views = refs[2:2 + 2 * n_d]
        attn_ref, lse_ref = refs[2 + 2 * n_d:4 + 2 * n_d]
        out_views = refs[4 + 2 * n_d:4 + 4 * n_d]
        scr = refs[-1]
        l_rows = [l0_ref[...]] + [_collect(scr, n, in_views[2 * n + 1], dil, 0, LSE_WIDTH) for n, (_, dil) in enumerate(DILATED)]
        lse_heads = []
        for h in range(HEADS_PER_GROUP):
            cols = slice(h * HEAD_DIM, (h + 1) * HEAD_DIM)
            os_ = [o0_ref[:, cols].astype(F32)]
            for n, (_, dil) in enumerate(DILATED):
                os_.append(_collect(scr, n_d + n, in_views[2 * n], dil, h * HEAD_DIM))
            ls_ = [lr[:, h * LSE_LANES:h * LSE_LANES + 1] for lr in l_rows]
            m = ls_[0]
            for l_ in ls_[1:]:
                m = jnp.maximum(m, l_)
            es = [jnp.exp(l_ - m) for l_ in ls_]
            den = es[0]
            num = es[0] * os_[0]
            for e, o in zip(es[1:], os_[1:]):
                den = den + e
                num = num + e * o
            attn = num * (1.0 / den)
            lse_heads.append(jnp.broadcast_to(m + jnp.log(den), (tm, LSE_LANES)))
            attn_ref[:, cols] = attn.astype(BF16)
            for n, (_, dil) in enumerate(DILATED):
                _spread(scr, 2 * n_d, attn, out_views[2 * n], dil, h * HEAD_DIM)
        lse = jnp.concatenate(lse_heads, axis=1)
        lse_ref[...] = lse
        for n, (_, dil) in enumerate(DILATED):
            _spread(scr, 2 * n_d, lse, out_views[2 * n + 1], dil, 0, LSE_WIDTH)

    row = lambda i: (i, 0)
    nat = pl.BlockSpec((tm, GROUP_WIDTH), row)
    nat_l = pl.BlockSpec((tm, LSE_WIDTH), row)
    in_specs = [nat, nat_l]
    args = [o0, l0]
    out_specs = [nat, nat_l]
    out_shape = [jax.ShapeDtypeStruct((s, GROUP_WIDTH), BF16), jax.ShapeDtypeStruct((s, LSE_WIDTH), F32)]
    for (_, dil), (ov, lv) in zip(DILATED, dilated):
        in_specs += [_view_spec(tm, dil), _view_spec(tm, dil, LSE_WIDTH)]
        args += [ov, lv]
        out_specs += [_view_spec(tm, dil), _view_spec(tm, dil, LSE_WIDTH)]
        out_shape += [_view_shape(s, dil, BF16), _view_shape(s, dil, F32, LSE_WIDTH)]
    res = pl.pallas_call(
        body, name="attn_merge", grid=(s // tm,), in_specs=in_specs, out_specs=out_specs, out_shape=out_shape,
        scratch_shapes=[pltpu.VMEM((2 * n_d + 1, tm, HEAD_DIM), F32)], compiler_params=_params(1),
    )(*args)
    return res[0], res[1], [tuple(res[2 + 2 * n:4 + 2 * n]) for n in range(n_d)]


def _to_views(a, tm):
    s = a.shape[0]

    def body(a_ref, *rest):
        outs, scr = rest[:-1], rest[-1]
        for h in range(HEADS_PER_GROUP):
            t = a_ref[:, h * HEAD_DIM:(h + 1) * HEAD_DIM].astype(F32)
            for n, (_, dil) in enumerate(DILATED):
                _spread(scr, n, t, outs[n], dil, h * HEAD_DIM)

    return pl.pallas_call(
        body, name="to_views", grid=(s // tm,), in_specs=[pl.BlockSpec((tm, GROUP_WIDTH), lambda i: (i, 0))],
        out_specs=[_view_spec(tm, dil) for _, dil in DILATED], out_shape=[_view_shape(s, dil, BF16) for _, dil in DILATED],
        scratch_shapes=[pltpu.VMEM((len(DILATED), tm, HEAD_DIM), F32)], compiler_params=_params(1),
    )(a)


def _dz_layout(grads, du, dga, dgs, tabs, tm, comm=None):
    s = du.shape[0]
    scale = 1.0 / math.sqrt(HEAD_DIM)
    n_steps = s // tm
    c_ins = list(comm["ins"]) if comm else []
    c_outs = list(comm["outs"]) if comm else []
    c_sems = list(comm["sems"]) if comm else []
    n_fixed = 3 * N_GROUPS + 6

    def body(*refs):
        g_refs = refs[:3 * N_GROUPS]
        du_ref, dga_ref, dgs_ref, cs_ref, lo_ref, hi_ref = refs[3 * N_GROUPS:n_fixed]
        comm_in = refs[n_fixed:n_fixed + len(c_ins)]
        dz_ref = refs[n_fixed + len(c_ins)]
        comm_out = refs[n_fixed + len(c_ins) + 1:n_fixed + len(c_ins) + 1 + len(c_outs)]
        scr = refs[n_fixed + len(c_ins) + 1 + len(c_outs)]
        sems = refs[n_fixed + len(c_ins) + 2 + len(c_outs):]
        if comm:
            @pl.when(pl.program_id(0) == 0)
            def _():
                comm["start"](comm_in, comm_out, sems)

            @pl.when(pl.program_id(0) == n_steps - 1)
            def _():
                comm["finish"](comm_in, comm_out, sems)

        tabs_ = (cs_ref[...], lo_ref[...], hi_ref[...])
        for part in range(3):
            for g, dil in enumerate(GROUP_DILATIONS):
                src = g_refs[3 * g + part]
                for h in range(HEADS_PER_GROUP):
                    if dil == 1:
                        t = src[:, h * HEAD_DIM:(h + 1) * HEAD_DIM].astype(F32)
                    else:
                        t = _collect(scr, h, src, dil, h * HEAD_DIM)
                    if part < 2:
                        t = _rope_t(t, *tabs_)
                    if part == 0:
                        t = t * scale
                    col = part * N_GROUPS * GROUP_WIDTH + g * GROUP_WIDTH + h * HEAD_DIM
                    dz_ref[:, col:col + HEAD_DIM] = t.astype(BF16)
        dz_ref[:, COL_U:COL_GA] = du_ref[...]
        dz_ref[:, COL_GA:COL_GS] = dga_ref[...]
        dz_ref[:, COL_GS:IN_WIDTH] = dgs_ref[...]

    row = lambda i: (i, 0)
    in_specs, args = [], []
    for (g, dil), trio in zip(enumerate(GROUP_DILATIONS), grads):
        in_specs += [pl.BlockSpec((tm, GROUP_WIDTH), row) if dil == 1 else _view_spec(tm, dil)] * 3
        args += list(trio)
    in_specs += [pl.BlockSpec((tm, SSM_WIDTH), row), pl.BlockSpec((tm, D_MODEL), row), pl.BlockSpec((tm, D_MODEL), row)]
    in_specs += [pl.BlockSpec((tm, HEAD_DIM), row)] * 3
    in_specs += [pl.BlockSpec(memory_space=pl.ANY)] * len(c_ins)
    res = pl.pallas_call(
        body, name="dz_layout", grid=(n_steps,), in_specs=in_specs,
        out_specs=[pl.BlockSpec((tm, IN_WIDTH), row)] + [pl.BlockSpec(memory_space=pl.ANY)] * len(c_outs),
        out_shape=[jax.ShapeDtypeStruct((s, IN_WIDTH), BF16)] + c_outs,
        scratch_shapes=[pltpu.VMEM((HEADS_PER_GROUP, tm, HEAD_DIM), F32)] + [pltpu.SemaphoreType.DMA((n,)) for n in c_sems],
        compiler_params=_params(1),
    )(*args, du, dga, dgs, *tabs, *c_ins)
    return res[0], list(res[1:])


def _discretise(a_re, a_im, log_dt, bt_re, bt_im):
    dt = jnp.exp(log_dt)
    mag = jnp.exp(a_re * dt)
    bar_re = mag * jnp.cos(a_im * dt)
    bar_im = mag * jnp.sin(a_im * dt)
    nr = bar_re - 1.0
    ni = bar_im
    den = a_re * a_re + a_im * a_im
    z_re = (nr * a_re + ni * a_im) / den
    z_im = (ni * a_re - nr * a_im) / den
    bb_re = z_re[:, None, :] * bt_re - z_im[:, None, :] * bt_im
    bb_im = z_re[:, None, :] * bt_im + z_im[:, None, :] * bt_re
    return bar_re, bar_im, bb_re, bb_im


def _ssm_prep(a_re, a_im, log_dt, bt_re, bt_im):
    def body(ar, ai, ld, br, bi, o_lr, o_li, o_br, o_bi):
        lr, li, bbr, bbi = _discretise(ar[...], ai[...], ld[...], br[...], bi[...])
        o_lr[...] = lr
        o_li[...] = li
        o_br[...] = bbr
        o_bi[...] = bbi

    sm = jax.ShapeDtypeStruct((SSM_GROUPS, SSM_STATE), F32)
    bg = jax.ShapeDtypeStruct((SSM_GROUPS, SSM_GROUP, SSM_STATE), F32)
    return pl.pallas_call(body, name="ssm_prep", out_shape=[sm, sm, bg, bg])(a_re, a_im, log_dt, bt_re, bt_im)


def _ssm_param_bwd(a_re, a_im, log_dt, bt_re, bt_im, d_lr, d_li, d_bbr, d_bbi):
    def body(ar, ai, ld, br, bi, g_lr, g_li, g_br, g_bi, o_ar, o_ai, o_ld, o_br, o_bi):
        _, vjp = jax.vjp(_discretise, ar[...], ai[...], ld[...], br[...], bi[...])
        d_ar, d_ai, d_ld, d_br, d_bi = vjp((g_lr[...], g_li[...], g_br[...], g_bi[...]))
        o_ar[...] = d_ar
        o_ai[...] = d_ai
        o_ld[...] = d_ld
        o_br[...] = d_br
        o_bi[...] = d_bi

    sm = jax.ShapeDtypeStruct((SSM_GROUPS, SSM_STATE), F32)
    col = jax.ShapeDtypeStruct((SSM_GROUPS, 1), F32)
    bg = jax.ShapeDtypeStruct((SSM_GROUPS, SSM_GROUP, SSM_STATE), F32)
    return pl.pallas_call(body, name="ssm_param_bwd", out_shape=[sm, sm, col, bg, bg])(
        a_re, a_im, log_dt, bt_re, bt_im, d_lr, d_li, d_bbr, d_bbi)


def _block_diag(t, rows_per, cols_per):
    t4 = t.reshape(SSM_SUPER, 8, rows_per, cols_per)
    eye = jnp.eye(8, dtype=t.dtype)
    return jnp.einsum("bgrc,gh->bgrhc", t4, eye).reshape(SSM_SUPER, 8 * rows_per, 8 * cols_per)


def _block_diag_t(dense, rows_per, cols_per):
    t = dense.reshape(SSM_SUPER, 8, rows_per, 8, cols_per)
    eye = jnp.eye(8, dtype=dense.dtype)
    return jnp.einsum("bgrhc,gh->bgrc", t, eye).reshape(SSM_GROUPS, rows_per, cols_per)


def _gelu(v):
    c = math.sqrt(2.0 / math.pi)
    return 0.5 * v * (1.0 + jnp.tanh(c * (v + 0.044715 * v * v * v)))


def _gelu_grad(v):
    c = math.sqrt(2.0 / math.pi)
    t = jnp.tanh(c * (v + 0.044715 * v * v * v))
    return 0.5 * (1.0 + t) + 0.5 * v * (1.0 - t * t) * c * (1.0 + 3.0 * 0.044715 * v * v)


SUB = 8


SCAN_STEPS = (1, 2, 4)
N_SCAN_TABLES = 2 + 2 * len(SCAN_STEPS)


def _scan_tables(tab_ref, lam_re, lam_im, reverse, conj):
    lr = lam_re
    li = -lam_im if conj else lam_im
    powers = [(lr, li)]
    for _ in range(SUB - 1):
        pr, pi = powers[-1]
        powers.append((pr * lr - pi * li, pr * li + pi * lr))
    row = lax.broadcasted_iota(jnp.int32, (SUB, N_STATE), 0)
    if reverse:
        row = SUB - 1 - row
    wide = lambda v: jnp.broadcast_to(v, (SUB, N_STATE))
    p_re = jnp.zeros((SUB, N_STATE), F32)
    p_im = jnp.zeros((SUB, N_STATE), F32)
    for j in range(SUB):
        p_re = jnp.where(row == j, wide(powers[j][0]), p_re)
        p_im = jnp.where(row == j, wide(powers[j][1]), p_im)
    tab_ref[0] = p_re
    tab_ref[1] = p_im
    for idx, k in enumerate(SCAN_STEPS):
        tab_ref[2 + 2 * idx] = jnp.where(row >= k, wide(powers[k - 1][0]), 0.0)
        tab_ref[3 + 2 * idx] = jnp.where(row >= k, wide(powers[k - 1][1]), 0.0)


def _scan_rows(g_re_ref, g_im_ref, tab_ref, carry, n_rows, reverse):
    last = 0 if reverse else SUB - 1

    def tile_step(tt, state):
        cr, ci = state
        t8 = (n_rows // SUB - 1 - tt) if reverse else tt
        start = pl.multiple_of(t8 * SUB, SUB)
        xr = g_re_ref[pl.ds(start, SUB), :]
        xi = g_im_ref[pl.ds(start, SUB), :]
        for idx, k in enumerate(SCAN_STEPS):
            mr = tab_ref[2 + 2 * idx]
            mi = tab_ref[3 + 2 * idx]
            shift = SUB - k if reverse else k
            sr = pltpu.roll(xr, shift, 0)
            si = pltpu.roll(xi, shift, 0)
            xr, xi = xr + (mr * sr - mi * si), xi + (mr * si + mi * sr)
        pr = tab_ref[0]
        pi = tab_ref[1]
        xr, xi = xr + (pr * cr - pi * ci), xi + (pr * ci + pi * cr)
        g_re_ref[pl.ds(start, SUB), :] = xr
        g_im_ref[pl.ds(start, SUB), :] = xi
        return (jnp.broadcast_to(xr[last:last + 1, :], (SUB, N_STATE)),
                jnp.broadcast_to(xi[last:last + 1, :], (SUB, N_STATE)))

    return lax.fori_loop(0, n_rows // SUB, tile_step, carry)


def _ssm_fwd(z, b_re, b_im, c_re, c_im, lam_re, lam_im, d_skip, chunk):
    s = z.shape[0]

    def body(u_ref, bre, bim, cre, cim, lre, lim, dsk, hre_ref, him_ref, ys_ref, yg_ref, car_re, car_im, tabs):
        i = pl.program_id(0)

        @pl.when(i == 0)
        def _():
            car_re[...] = jnp.zeros_like(car_re)
            car_im[...] = jnp.zeros_like(car_im)
            _scan_tables(tabs, lre[...], lim[...], False, False)

        u = u_ref[...]
        for b in range(SSM_SUPER):
            ub = u[:, b * 128:(b + 1) * 128]
            st = slice(b * 512, (b + 1) * 512)
            hre_ref[:, st] = _dot(ub, bre[b], "nn")
            him_ref[:, st] = _dot(ub, bim[b], "nn")
        sr, si = _scan_rows(hre_ref, him_ref, tabs, (car_re[...], car_im[...]), chunk, False)
        car_re[...] = sr
        car_im[...] = si
        uf = u.astype(F32)
        for b in range(SSM_SUPER):
            st = slice(b * 512, (b + 1) * 512)
            ch = slice(b * 128, (b + 1) * 128)
            y = _dot(hre_ref[:, st].astype(BF16), cre[b], "nn") - _dot(him_ref[:, st].astype(BF16), cim[b], "nn")
            y = y + dsk[:, ch] * uf[:, ch]
            ys_ref[:, ch] = y
            yg_ref[:, ch] = _gelu(y).astype(BF16)

    full3 = lambda i: (0, 0, 0)
    full2 = lambda i: (0, 0)
    row = lambda i: (i, 0)
    u_col = COL_U // SSM_WIDTH
    return pl.pallas_call(
        body, name="ssm_fwd", grid=(s // chunk,),
        in_specs=[pl.BlockSpec((chunk, SSM_WIDTH), lambda i: (i, u_col)),
                  pl.BlockSpec((SSM_SUPER, 128, 512), full3), pl.BlockSpec((SSM_SUPER, 128, 512), full3),
                  pl.BlockSpec((SSM_SUPER, 512, 128), full3), pl.BlockSpec((SSM_SUPER, 512, 128), full3),
                  pl.BlockSpec((1, N_STATE), full2), pl.BlockSpec((1, N_STATE), full2), pl.BlockSpec((1, SSM_WIDTH), full2)],
        out_specs=[pl.BlockSpec((chunk, N_STATE), row), pl.BlockSpec((chunk, N_STATE), row),
                   pl.BlockSpec((chunk, SSM_WIDTH), row), pl.BlockSpec((chunk, SSM_WIDTH), row)],
        out_shape=[jax.ShapeDtypeStruct((s, N_STATE), F32), jax.ShapeDtypeStruct((s, N_STATE), F32),
                   jax.ShapeDtypeStruct((s, SSM_WIDTH), F32), jax.ShapeDtypeStruct((s, SSM_WIDTH), BF16)],
        scratch_shapes=[pltpu.VMEM((SUB, N_STATE), F32), pltpu.VMEM((SUB, N_STATE), F32),
                        pltpu.VMEM((N_SCAN_TABLES, SUB, N_STATE), F32)],
        compiler_params=_params(1),
    )(z, b_re, b_im, c_re, c_im, lam_re, lam_im, d_skip)


def _ssm_bwd(dys, z, h_re, h_im, b_re, b_im, c_re, c_im, lam_re, lam_im, d_skip, chunk):
    s = z.shape[0]
    n_chunks = s // chunk

    def body(dy_ref, u_ref, hre_ref, him_ref, hpr_ref, hpi_ref, bre, bim, cre, cim, lre, lim, dsk,
             du_ref, dlr_ref, dli_ref, dbr_ref, dbi_ref, dcr_ref, dci_ref, dd_ref, are, aim, car_re, car_im, tabs):
        i = pl.program_id(0)
        n = n_chunks - 1 - i

        @pl.when(i == 0)
        def _():
            car_re[...] = jnp.zeros_like(car_re)
            car_im[...] = jnp.zeros_like(car_im)
            _scan_tables(tabs, lre[...], lim[...], True, True)
            for r in (dlr_ref, dli_ref, dbr_ref, dbi_ref, dcr_ref, dci_ref, dd_ref):
                r[...] = jnp.zeros_like(r)

        dy = dy_ref[...]
        dyb = dy.astype(BF16)
        u = u_ref[...]
        for b in range(SSM_SUPER):
            ch = slice(b * 128, (b + 1) * 128)
            st = slice(b * 512, (b + 1) * 512)
            are[:, st] = _dot(dyb[:, ch], cre[b], "nt")
            aim[:, st] = -_dot(dyb[:, ch], cim[b], "nt")
        sr, si = _scan_rows(are, aim, tabs, (car_re[...], car_im[...]), chunk, True)
        car_re[...] = sr
        car_im[...] = si
        row_id = lax.broadcasted_iota(jnp.int32, (chunk, N_STATE), 0)
        top_scale = jnp.where(n > 0, 1.0, 0.0)
        h_r = hre_ref[...]
        h_i = him_ref[...]
        hp_r = jnp.where(row_id == 0, hpr_ref[SUB - 1:SUB, :] * top_scale, pltpu.roll(h_r, 1, 0))
        hp_i = jnp.where(row_id == 0, hpi_ref[SUB - 1:SUB, :] * top_scale, pltpu.roll(h_i, 1, 0))
        a_r = are[...]
        a_i = aim[...]
        dlr_ref[...] += jnp.sum(a_r * hp_r + a_i * hp_i, axis=0, keepdims=True)
        dli_ref[...] += jnp.sum(a_i * hp_r - a_r * hp_i, axis=0, keepdims=True)
        dd_ref[...] += jnp.sum(dy * u.astype(F32), axis=0, keepdims=True)
        a_rb = a_r.astype(BF16)
        a_ib = a_i.astype(BF16)
        h_rb = h_r.astype(BF16)
        h_ib = h_i.astype(BF16)
        for b in range(SSM_SUPER):
            ch = slice(b * 128, (b + 1) * 128)
            st = slice(b * 512, (b + 1) * 512)
            dbr_ref[b] += _dot(u[:, ch], a_rb[:, st], "tn")
            dbi_ref[b] += _dot(u[:, ch], a_ib[:, st], "tn")
            dcr_ref[b] += _dot(h_rb[:, st], dyb[:, ch], "tn")
            dci_ref[b] += -_dot(h_ib[:, st], dyb[:, ch], "tn")
            du = _dot(a_rb[:, st], bre[b], "nt") + _dot(a_ib[:, st], bim[b], "nt") + dsk[:, ch] * dy[:, ch]
            du_ref[:, ch] = du.astype(du_ref.dtype)

    full3 = lambda i: (0, 0, 0)
    full2 = lambda i: (0, 0)
    rev = lambda i: (n_chunks - 1 - i, 0)
    above = lambda i: (jnp.maximum((n_chunks - 1 - i) * (chunk // SUB) - 1, 0), 0)
    u_col = COL_U // SSM_WIDTH
    b_spec = pl.BlockSpec((SSM_SUPER, 128, 512), full3)
    c_spec = pl.BlockSpec((SSM_SUPER, 512, 128), full3)
    vec = pl.BlockSpec((1, N_STATE), full2)
    return pl.pallas_call(
        body, name="ssm_bwd", grid=(n_chunks,),
        in_specs=[pl.BlockSpec((chunk, SSM_WIDTH), rev),
                  pl.BlockSpec((chunk, SSM_WIDTH), lambda i: (n_chunks - 1 - i, u_col)),
                  pl.BlockSpec((chunk, N_STATE), rev), pl.BlockSpec((chunk, N_STATE), rev),
                  pl.BlockSpec((SUB, N_STATE), above), pl.BlockSpec((SUB, N_STATE), above),
                  b_spec, b_spec, c_spec, c_spec, vec, vec, pl.BlockSpec((1, SSM_WIDTH), full2)],
        out_specs=[pl.BlockSpec((chunk, SSM_WIDTH), rev), vec, vec, b_spec, b_spec, c_spec, c_spec,
                   pl.BlockSpec((1, SSM_WIDTH), full2)],
        out_shape=[jax.ShapeDtypeStruct((s, SSM_WIDTH), BF16),
                   jax.ShapeDtypeStruct((1, N_STATE), F32), jax.ShapeDtypeStruct((1, N_STATE), F32),
                   jax.ShapeDtypeStruct((SSM_SUPER, 128, 512), F32), jax.ShapeDtypeStruct((SSM_SUPER, 128, 512), F32),
                   jax.ShapeDtypeStruct((SSM_SUPER, 512, 128), F32), jax.ShapeDtypeStruct((SSM_SUPER, 512, 128), F32),
                   jax.ShapeDtypeStruct((1, SSM_WIDTH), F32)],
        scratch_shapes=[pltpu.VMEM((chunk, N_STATE), F32), pltpu.VMEM((chunk, N_STATE), F32),
                        pltpu.VMEM((SUB, N_STATE), F32), pltpu.VMEM((SUB, N_STATE), F32),
                        pltpu.VMEM((N_SCAN_TABLES, SUB, N_STATE), F32)],
        compiler_params=_params(1),
    )(dys, z, h_re, h_im, h_re, h_im, b_re, b_im, c_re, c_im, lam_re, lam_im, d_skip)


def _local_step(x, p, pos, tgt, sm, w_in_own, w_in_buf, late_bufs, place):
    s = x.shape[0]
    tm = min(512, s)
    ts = min(2048, s)
    chunk = min(256, s)
    ni = s // tm
    nk = s // ts
    g_mix, g_ffn, g_final = sm["g_mix"], sm["g_ffn"], sm["g_final"]
    rowblk, rowmap = _rows(tm, D_MODEL)
    vec1k = ((1, D_MODEL), lambda *_: (0, 0))

    half_in = IN_WIDTH // 8
    tmb = min(1024, s)
    nib = s // tmb
    chip_w = IN_WIDTH // N_CHIPS
    w_start, w_forward, w_finish = _gather_stages(1)
    gather_in = dict(ins=[w_in_buf], outs=[jax.ShapeDtypeStruct(w_in_buf.shape, w_in_buf.dtype)], aliased=True,
                     sems=[3] * 4, stages=[(0, w_start), (nib - 1, w_forward), (nib - 1, w_finish)])
    a_rows = lambda i, j, k, pv: (i, 0)
    z_own, n1, w_in_all = _mm("in_proj_own", (nib, 1, 1),
                              [(x, (tmb, D_MODEL), a_rows, w_in_own, (D_MODEL, chip_w), lambda i, j, k, pv: (0, 0))], "nn",
                              [((s, IN_WIDTH), BF16, (tmb, chip_w), lambda i, j, k, pv: (i, pv[P_CHIP])),
                               ((s, D_MODEL), BF16, (tmb, D_MODEL), a_rows)],
                              extras=[(g_mix, (1, D_MODEL), lambda i, j, k, pv: (0, 0))],
                              epilogue=lambda acc, g: ((acc,), ()), prologue=_rms_fwd_tile, comm=gather_in, place=place)
    w_in = w_in_all.reshape(N_CHIPS, D_MODEL, chip_w)
    n_late = len(late_bufs)
    g_start, g_forward, g_finish = _gather_stages(n_late)
    in_steps = (N_CHIPS - 1) * nib
    gather = dict(ins=late_bufs, outs=[jax.ShapeDtypeStruct(b.shape, b.dtype) for b in late_bufs], aliased=True,
                  sems=[3 * n_late] * 4,
                  stages=[(0, g_start), ((4 * in_steps) // 5, g_forward), (in_steps - 1, g_finish)])
    other = lambda j, pv: (pv[P_CHIP] + 1 + j) % N_CHIPS
    z, *late = _mm("in_proj", (nib, N_CHIPS - 1, 1),
                   [(n1, (tmb, D_MODEL), a_rows, w_in, (None, D_MODEL, chip_w), lambda i, j, k, pv: (other(j, pv), 0, 0))],
                   "nn", [((s, IN_WIDTH), BF16, (tmb, chip_w), lambda i, j, k, pv: (i, other(j, pv)))], j_outer=True,
                   comm=gather, place=place, fill=z_own)
    w_ap, w_ga, w_gb, w_out, w_fg, w_fu, w_fd, w_pg, w_pp = (
        g.reshape(N_CHIPS, 2 * g.shape[2], g.shape[3]) for g in late)
    w_out2 = w_out.reshape(D_MODEL, D_MODEL)
    w_pg2 = w_pg.reshape(D_MODEL, D_MODEL)

    inv = ROPE_THETA ** (-jnp.arange(ROPE_HALF, dtype=F32) * 2.0 / ROPE_DIM)
    inv_row = jnp.concatenate([inv, inv, jnp.zeros((HEAD_DIM - ROPE_DIM,), F32)]).reshape(1, HEAD_DIM)
    tabs = _rope_tables(pos.astype(F32).reshape(s, 1), inv_row, tm)

    qk0, qkv_views = _qkv_layout(z, tabs, tm)
    v0_col = (2 * N_GROUPS * GROUP_WIDTH) // GROUP_WIDTH
    group_in = [((qk0, qk0, z), (0, 1, v0_col))] + [(trio, (0, 0, 0)) for trio in qkv_views]
    fwd_out = [_attn_fwd(*arrs, dil, cols3) for (arrs, cols3), dil in zip(group_in, GROUP_DILATIONS)]
    attn, lse, merged_views = _attn_merge(fwd_out[0][0], fwd_out[0][1], fwd_out[1:], tm)

    def chip_cols(parts):
        return (jnp.concatenate(parts, axis=1),), ()

    def proj_cols(name, a, width, w):
        blk = (None, width, 256)
        pairs = [(a, (tmb, width), lambda i, j, k: (i, 0), w, blk, lambda i, j, k: (0, 0, 0))]
        pairs += [(None, None, None, w, blk, (lambda i, j, k, q=q: (q, 0, 0))) for q in range(1, N_CHIPS)]
        return _mm(name, (nib, 1, 1), pairs, "nn", [((s, D_MODEL), BF16, (tmb, D_MODEL), lambda i, j, k: (i, 0))],
                   epilogue=chip_cols, sum_pairs=False)[0]

    def proj512(name, a, w):
        return proj_cols(name, a, GROUP_WIDTH, w)

    attn_d = proj512("attn_proj", attn, w_ap)

    bt_re = jnp.transpose(sm["b_re"], (0, 2, 1))
    bt_im = jnp.transpose(sm["b_im"], (0, 2, 1))
    log_dt_col = sm["log_dt"].reshape(SSM_GROUPS, 1)
    lam_re, lam_im, bbt_re, bbt_im = _ssm_prep(sm["a_re"], sm["a_im"], log_dt_col, bt_re, bt_im)
    b_re_m = _block_diag(bbt_re, SSM_GROUP, SSM_STATE).astype(BF16)
    b_im_m = _block_diag(bbt_im, SSM_GROUP, SSM_STATE).astype(BF16)
    c_re_m = _block_diag(jnp.transpose(sm["c_re"], (0, 2, 1)), SSM_STATE, SSM_GROUP).astype(BF16)
    c_im_m = _block_diag(jnp.transpose(sm["c_im"], (0, 2, 1)), SSM_STATE, SSM_GROUP).astype(BF16)
    lam_re_row = lam_re.reshape(1, N_STATE)
    lam_im_row = lam_im.reshape(1, N_STATE)
    d_skip_row = sm["d_skip"].reshape(1, SSM_WIDTH)
    h_re, h_im, ys, yg = _ssm_fwd(z, b_re_m, b_im_m, c_re_m, c_im_m, lam_re_row, lam_im_row, d_skip_row, chunk)

    pa = proj512("glu_a", yg, w_ga)
    pb = proj512("glu_b", yg, w_gb)

    ga_blk = ((tm, D_MODEL), lambda i: (i, COL_GA // D_MODEL))
    gs_blk = ((tm, D_MODEL), lambda i: (i, COL_GS // D_MODEL))

    def mix_pro(ad, xr, g, ga, gs, a, b):
        ga, gs, ad, a, b = (t.astype(F32) for t in (ga, gs, ad, a, b))
        return _sig(ga) * ad + _sig(gs) * (a * _sig(b))

    def out_epi(acc, xr, g, *_):
        h1 = acc + xr
        return (h1, _rms_fwd_tile(h1, g)), ()

    m3 = lambda i, j, k: (i, 0)
    w3 = lambda i, j, k: (0, 0)
    tile_d = (tm, D_MODEL)
    h1, n2, mix = _mm("out_proj", (ni, 1, 1), [(attn_d, tile_d, m3, w_out2, (D_MODEL, D_MODEL), w3)], "nn",
                      [((s, D_MODEL), F32, tile_d, m3), ((s, D_MODEL), BF16, tile_d, m3), ((s, D_MODEL), BF16, tile_d, m3)],
                      epilogue=out_epi, prologue=mix_pro,
                      extras=[(x, tile_d, m3), (g_ffn, (1, D_MODEL), w3),
                              (z, tile_d, lambda i, j, k: (i, COL_GA // D_MODEL)), (z, tile_d, lambda i, j, k: (i, COL_GS // D_MODEL)),
                              (pa, tile_d, m3), (pb, tile_d, m3)])

    ffq = (None, tm, D_FF_Q)
    ffq_map = lambda i, j, k: (j, i, 0)

    def ffn_in_epi(parts):
        gts, ups = parts[0::2], parts[1::2]
        acts = [gt * _sig(gt) * u_ for gt, u_ in zip(gts, ups)]
        return (jnp.stack(gts, axis=0), jnp.stack(ups, axis=0), jnp.stack(acts, axis=0)), ()

    w_ffq = (None, D_MODEL, D_FF_Q)
    ff_pairs = []
    for q in range(N_CHIPS):
        blk_q = lambda i, j, k, q=q: (q, 0, 0)
        ff_pairs.append((n2, (tm, D_MODEL), m3, w_fg, w_ffq, blk_q) if q == 0 else (None, None, None, w_fg, w_ffq, blk_q))
        ff_pairs.append((None, None, None, w_fu, w_ffq, blk_q))
    ff_all = (N_CHIPS, tm, D_FF_Q)
    ff_all_map = lambda i, j, k: (0, i, 0)
    gate, up, act = _mm("ffn_gate_up", (ni, 1, 1), ff_pairs, "nn",
                        [((N_CHIPS, s, D_FF_Q), BF16, ff_all, ff_all_map)] * 3, epilogue=ffn_in_epi,
                        sum_pairs=False, resident_b=True)

    (h2,) = _mm("ffn_down", (nib, 1, 1),
                [(act, (None, tmb, D_FF_Q), (lambda i, j, k, q=q: (q, i, 0)), w_fd, (None, D_FF_Q, D_MODEL),
                  (lambda i, j, k, q=q: (q, 0, 0))) for q in range(N_CHIPS)], "nn",
                [((s, D_MODEL), F32, (tmb, D_MODEL), m3)], epilogue=lambda acc, hr: ((acc + hr,), ()),
                extras=[(h1, (tmb, D_MODEL), m3)])

    pp = proj_cols("ple_proj", p, PLE_DIM, w_pp)

    def ple_head_epi(acc, hr, ppr, t, g):
        sg = _sig(acc)
        ppf = ppr.astype(F32)
        h = hr + sg * ppf
        r = lax.rsqrt(jnp.mean(h * h, axis=-1, keepdims=True) + EPS)
        hhat = h * r
        diff = hhat * g - t
        loss = 0.5 * jnp.sum(jnp.mean(diff * diff, axis=-1, keepdims=True))
        dy = diff * (1.0 / D_MODEL)
        gy = dy * g
        dh = r * (gy - hhat * jnp.mean(gy * hhat, axis=-1, keepdims=True))
        return ((dh, dh * ppf * sg * (1.0 - sg), dh * sg),
                (jnp.full((SUB, 128), loss, F32), jnp.sum(dy * hhat, axis=0, keepdims=True)))

    tile_row = (tm, D_MODEL)
    dh3, dgl, dpp, loss_acc, dg_final = _mm(
        "ple_gate_head", (ni, 1, 1), [(h2, tile_row, m3, w_pg2, (D_MODEL, D_MODEL), w3)], "nn",
        [((s, D_MODEL), F32, tile_row, m3), ((s, D_MODEL), BF16, tile_row, m3), ((s, D_MODEL), BF16, tile_row, m3)],
        epilogue=ple_head_epi,
        extras=[(h2, tile_row, m3), (pp, tile_row, m3), (tgt, tile_row, m3), (g_final, (1, D_MODEL), w3)],
        acc_outs=[((SUB, 128), F32), ((1, D_MODEL), F32)])

    def wgrad(name, a, a_block, a_imap, b, b_block, b_imap, out_shape, out_block, out_imap, nj, acc_shape):
        return _mm(name, (1, nj, nk), [(a, a_block, a_imap, b, b_block, b_imap)], "tn",
                   [(out_shape, F32, out_block, out_imap)], acc_shape=acc_shape)[0]

    tk0 = lambda i, j, k: (k, 0)
    tkj = lambda i, j, k: (k, j)
    def wgrad_cols(name, a, width, dy_):
        def split(acc):
            return (jnp.stack([acc[:, q * 256:(q + 1) * 256] for q in range(N_CHIPS)], axis=0),), ()

        return _mm(name, (1, 1, nk), [(a, (ts, width), tk0, dy_, (ts, D_MODEL), tk0)], "tn",
                   [((N_CHIPS, width, 256), F32, (N_CHIPS, width, 256), lambda i, j, k: (0, 0, 0))], epilogue=split,
                   acc_shape=(width, D_MODEL))[0]

    d_w_pp = wgrad_cols("d_ple_proj", p, PLE_DIM, dpp)
    d_w_pg = wgrad("d_ple_gate", h2, (ts, D_MODEL), tk0, dgl, (ts, D_MODEL), tk0, (D_MODEL, D_MODEL),
                   (D_MODEL, D_MODEL), w3, 1, (D_MODEL, D_MODEL))

    (dh2,) = _mm("ple_gate_bwd", (nib, 1, 1), [(dgl, (tmb, D_MODEL), m3, w_pg2, (D_MODEL, D_MODEL), w3)], "nt",
                 [((s, D_MODEL), F32, (tmb, D_MODEL), m3)], epilogue=lambda acc, d_: ((acc + d_,), ()),
                 extras=[(dh3, (tmb, D_MODEL), m3)])

    def ffn_bwd_epi(parts, gt_all, u_all):
        dgs_, dus_ = [], []
        for q, dact in enumerate(parts):
            gt, u_ = gt_all[q].astype(F32), u_all[q].astype(F32)
            sg = _sig(gt)
            dgs_.append(dact * u_ * (sg * (1.0 + gt * (1.0 - sg))))
            dus_.append(dact * gt * sg)
        return (jnp.stack(dgs_, axis=0), jnp.stack(dus_, axis=0)), ()

    fd_pairs = [((dh2, (tm, D_MODEL), m3) if q == 0 else (None, None, None))
                + (w_fd, (None, D_FF_Q, D_MODEL), (lambda i, j, k, q=q: (q, 0, 0))) for q in range(N_CHIPS)]
    dgate, dup = _mm("ffn_down_bwd", (ni, 1, 1), fd_pairs, "nt",
                     [((N_CHIPS, s, D_FF_Q), BF16, ff_all, ff_all_map)] * 2, epilogue=ffn_bwd_epi,
                     extras=[(gate, ff_all, ff_all_map), (up, ff_all, ff_all_map)], sum_pairs=False, resident_b=True)

    ffq_t = (None, ts, D_FF_Q)
    ffq_tmap = lambda i, j, k: (j, k, 0)
    blk_j = lambda i, j, k: (j, 0, 0)
    d_w_fd = wgrad("d_ffn_down", act, ffq_t, ffq_tmap, dh2, (ts, D_MODEL), tk0, (N_CHIPS, D_FF_Q, D_MODEL),
                   (None, D_FF_Q, D_MODEL), blk_j, N_CHIPS, (D_FF_Q, D_MODEL))
    d_w_fg = wgrad("d_ffn_gate", n2, (ts, D_MODEL), tk0, dgate, ffq_t, ffq_tmap, (N_CHIPS, D_MODEL, D_FF_Q),
                   (None, D_MODEL, D_FF_Q), blk_j, N_CHIPS, (D_MODEL, D_FF_Q))
    d_w_fu = wgrad("d_ffn_up", n2, (ts, D_MODEL), tk0, dup, ffq_t, ffq_tmap, (N_CHIPS, D_MODEL, D_FF_Q),
                   (None, D_MODEL, D_FF_Q), blk_j, N_CHIPS, (D_MODEL, D_FF_Q))

    def norm_bwd_epi(acc, h, d_res, g):
        dh, dg = _rms_bwd_tile(acc, h, g)
        return (d_res + dh,), (dg,)

    ffq_k = lambda i, j, k: (k, i, 0)
    blk_k = lambda i, j, k: (k, 0, 0)
    fi_pairs = []
    for q in range(N_CHIPS):
        a_q = lambda i, j, k, q=q: (q, i, 0)
        b_q = lambda i, j, k, q=q: (q, 0, 0)
        fi_pairs.append((dgate, ffq, a_q, w_fg, (None, D_MODEL, D_FF_Q), b_q))
        fi_pairs.append((dup, ffq, a_q, w_fu, (None, D_MODEL, D_FF_Q), b_q))
    dh1, dg_ffn = _mm("ffn_in_bwd", (ni, 1, 1), fi_pairs, "nt",
                      [((s, D_MODEL), F32, (tm, D_MODEL), m3)], epilogue=norm_bwd_epi,
                      extras=[(h1, (tm, D_MODEL), m3), (dh2, (tm, D_MODEL), m3), (g_ffn, (1, D_MODEL), w3)],
                      acc_outs=[((1, D_MODEL), F32)], resident_b=True)

    d_w_out = wgrad("d_out_proj", mix, (ts, D_MODEL), tk0, dh1, (ts, D_MODEL), tk0, (D_MODEL, D_MODEL),
                    (D_MODEL, D_MODEL), w3, 1, (D_MODEL, D_MODEL))

    def mix_bwd_epi(dm, ga, gs, ad, a, b):
        ga, gs, ad, a, b = (t.astype(F32) for t in (ga, gs, ad, a, b))
        s_a, s_s, s_b = _sig(ga), _sig(gs), _sig(b)
        d_ssm = dm * s_s
        return (dm * ad * s_a * (1.0 - s_a), dm * (a * s_b) * s_s * (1.0 - s_s), dm * s_a, d_ssm * s_b,
                d_ssm * a * s_b * (1.0 - s_b)), ()

    tile_m = (tm, D_MODEL)
    dga, dgs, dattn_d, dpa, dpb = _mm(
        "out_proj_bwd", (ni, 1, 1), [(dh1, tile_m, m3, w_out2, (D_MODEL, D_MODEL), w3)], "nt",
        [((s, D_MODEL), BF16, tile_m, m3)] * 5, epilogue=mix_bwd_epi,
        extras=[(z, tile_m, lambda i, j, k: (i, COL_GA // D_MODEL)), (z, tile_m, lambda i, j, k: (i, COL_GS // D_MODEL)),
                (attn_d, tile_m, m3), (pa, tile_m, m3), (pb, tile_m, m3)])

    d_w_ap = wgrad_cols("d_attn_proj", attn, GROUP_WIDTH, dattn_d)
    d_w_ga = wgrad_cols("d_glu_a", yg, GROUP_WIDTH, dpa)
    d_w_gb = wgrad_cols("d_glu_b", yg, GROUP_WIDTH, dpb)

    ik = lambda i, j, k: (i, k)

    def cols_bwd(dy_, w):
        return [(dy_, (tmb, 256), (lambda i, j, k, q=q: (i, q)), w, (None, GROUP_WIDTH, 256),
                 (lambda i, j, k, q=q: (q, 0, 0))) for q in range(N_CHIPS)]

    (dattn,) = _mm("attn_proj_bwd", (nib, 1, 1), cols_bwd(dattn_d, w_ap), "nt",
                   [((s, GROUP_WIDTH), BF16, (tmb, GROUP_WIDTH), m3)])

    (dys,) = _mm("glu_bwd", (nib, 1, 1), cols_bwd(dpa, w_ga) + cols_bwd(dpb, w_gb), "nt",
                 [((s, GROUP_WIDTH), F32, (tmb, GROUP_WIDTH), m3)],
                 epilogue=lambda acc, y_: ((acc * _gelu_grad(y_),), ()),
                 extras=[(ys, (tmb, GROUP_WIDTH), m3)])

    du, d_lr, d_li, d_bre, d_bim, d_cre, d_cim, d_dskip = _ssm_bwd(
        dys, z, h_re, h_im, b_re_m, b_im_m, c_re_m, c_im_m, lam_re_row, lam_im_row, d_skip_row, chunk)

    dattn_views = _to_views(dattn, tm)
    bwd_in = [(dattn, attn, lse)] + [(dv_, ov_, lv_) for dv_, (ov_, lv_) in zip(dattn_views, merged_views)]
    qkv_grads = [_attn_bwd(*arrs, *dol, dil, cols3)
                 for (arrs, cols3), dol, dil in zip(group_in, bwd_in, GROUP_DILATIONS)]
    early = [d_w_ap, d_w_ga, d_w_gb, d_w_out.reshape(N_CHIPS, D_MODEL // N_CHIPS, D_MODEL), d_w_fg, d_w_fu, d_w_fd,
             d_w_pg.reshape(N_CHIPS, D_MODEL // N_CHIPS, D_MODEL), d_w_pp]
    early5 = [g.reshape(N_CHIPS, 2, g.shape[1] // 2, g.shape[2]) for g in early]
    n_e = len(early5)
    p_start, p_finish = _pair_exchange_stages(n_e)
    dz, early_theirs = _dz_layout(
        qkv_grads, du, dga, dgs, tabs, tm,
        comm=dict(ins=early5, outs=_pair_exchange_shapes(early5), sems=[N_CHIPS * n_e] * 2, start=p_start, finish=p_finish))
    early_parts = [_pair_sum(g, t, place) for g, t in zip(early5, early_theirs)]

    chip_in = IN_WIDTH // N_CHIPS
    ip_pairs = [(dz, (tm, chip_in), (lambda i, j, k, q=q: (i, q)), w_in, (None, D_MODEL, chip_in),
                 (lambda i, j, k, q=q: (q, 0, 0))) for q in range(N_CHIPS)]
    grad_x, dg_mix = _mm("in_proj_bwd", (ni, 1, 1), ip_pairs, "nt",
                         [((s, D_MODEL), F32, (tm, D_MODEL), m3)], epilogue=norm_bwd_epi,
                         extras=[(x, (tm, D_MODEL), m3), (dh1, (tm, D_MODEL), m3), (g_mix, (1, D_MODEL), w3)],
                         acc_outs=[((1, D_MODEL), F32)], resident_b=True)

    d_bbt_re = _block_diag_t(d_bre, SSM_GROUP, SSM_STATE)
    d_bbt_im = _block_diag_t(d_bim, SSM_GROUP, SSM_STATE)
    d_a_re, d_a_im, d_log_dt, d_bt_re, d_bt_im = _ssm_param_bwd(
        sm["a_re"], sm["a_im"], log_dt_col, bt_re, bt_im,
        d_lr.reshape(SSM_GROUPS, SSM_STATE), d_li.reshape(SSM_GROUPS, SSM_STATE), d_bbt_re, d_bbt_im)
    small = {
        "g_mix": dg_mix, "a_re": d_a_re, "a_im": d_a_im, "log_dt": d_log_dt,
        "b_re": jnp.transpose(d_bt_re, (0, 2, 1)), "b_im": jnp.transpose(d_bt_im, (0, 2, 1)),
        "c_re": jnp.transpose(_block_diag_t(d_cre, SSM_STATE, SSM_GROUP), (0, 2, 1)),
        "c_im": jnp.transpose(_block_diag_t(d_cim, SSM_STATE, SSM_GROUP), (0, 2, 1)),
        "d_skip": d_dskip, "g_ffn": dg_ffn, "g_final": dg_final,
    }
    vec = _pack([small[n] for n in SMALL] + [loss_acc[0, 0].reshape(1)])

    x_start, x_finish = _chip_exchange_stages(n_e)
    v_start, v_finish = _all_exchange_stages()

    def both(f_chips, f_vec):
        def stage(ins, outs, sems):
            f_chips(ins[:n_e], outs[:n_e], sems[:2])
            f_vec(ins[n_e:], outs[n_e:], sems[2:])
        return stage

    ts_in = min(2048, s)
    win_steps = 8 * (s // ts_in)
    exchange = dict(ins=early_parts + [vec],
                    outs=[jax.ShapeDtypeStruct(t.shape, t.dtype) for t in early_parts]
                    + [jax.ShapeDtypeStruct((8,) + vec.shape, vec.dtype)],
                    aliased=False, sems=[3 * n_e, 3 * n_e, 7, 7],
                    stages=[(0, both(x_start, v_start)), (win_steps - 1, both(x_finish, v_finish))])
    d_w_in, *got = _mm("d_in_proj", (1, 8, s // ts_in), [(n1, (ts_in, D_MODEL), tk0, dz, (ts_in, half_in), tkj)], "tn",
                       [((N_CHIPS, D_MODEL, IN_WIDTH // N_CHIPS), F32, (None, D_MODEL, half_in),
                         lambda i, j, k: (j // 2, 0, j % 2))], acc_shape=(D_MODEL, half_in), comm=exchange)
    return grad_x, d_w_in, early_parts, got[:n_e], vec, got[n_e]


BIG = ("w_in", "w_attn_proj", "w_glu_a", "w_glu_b", "w_out", "w_ffn_gate", "w_ffn_up", "w_ffn_down", "w_ple_gate",
       "w_ple_proj")
SMALL = ("g_mix", "a_re", "a_im", "log_dt", "b_re", "b_im", "c_re", "c_im", "d_skip", "g_ffn", "g_final")
ANY = pl.BlockSpec(memory_space=pl.ANY)


def _place():
    x, y, c = lax.axis_index("x"), lax.axis_index("y"), lax.axis_index("c")
    chips = [(1 - x, y), (x, 1 - y), (1 - x, 1 - y)]
    return x, y, c, chips


def _remote(src, dst, send_sem, recv_sem, to):
    return pltpu.make_async_remote_copy(src_ref=src, dst_ref=dst, send_sem=send_sem, recv_sem=recv_sem, device_id=to,
                                        device_id_type=MESH)


def _comm_call(name, body, ins, out_shapes, n_sems, aliases=None):
    n_w = len(ins)
    return pl.pallas_call(
        body, name=name, in_specs=[ANY] * n_w, out_specs=[ANY] * len(out_shapes), out_shape=out_shapes,
        scratch_shapes=[pltpu.SemaphoreType.DMA((n,)) for n in n_sems], input_output_aliases=aliases or {},
    )(*ins)


def _gather_weights(bufs):
    n_w = len(bufs)
    start, forward, finish = _gather_stages(n_w)

    def body(*refs):
        ins, outs, sems = refs[:n_w], refs[n_w:2 * n_w], refs[2 * n_w:]
        start(ins, outs, sems)
        forward(ins, outs, sems)
        finish(ins, outs, sems)

    out_shapes = [jax.ShapeDtypeStruct(b.shape, b.dtype) for b in bufs]
    return _comm_call("gather_weights", body, bufs, out_shapes, [3 * n_w] * 4, aliases={w: w for w in range(n_w)})


def _gather_stages(n_w):
    def each():
        x, y, c, chips = _place()
        for w in range(n_w):
            for j, (cx, cy) in enumerate(chips):
                yield w, 3 * w + j, 2 * x + y, 2 * cx + cy, (cx, cy, c), (x, y, 1 - c), c

    def start(ins, outs, sems):
        for w, k, me, _, peer, _, c in each():
            mine = outs[w].at[me, c]
            _remote(mine, mine, sems[0].at[k], sems[1].at[k], peer).start()

    def forward(ins, outs, sems):
        for w, k, _, src_chip, peer, sib, c in each():
            landed = outs[w].at[src_chip, c]
            _remote(landed, landed, sems[0].at[k], sems[1].at[k], peer).wait_recv()
            _remote(landed, landed, sems[2].at[k], sems[3].at[k], sib).start()

    def finish(ins, outs, sems):
        for w, k, me, src_chip, peer, sib, c in each():
            other = outs[w].at[src_chip, 1 - c]
            _remote(other, other, sems[2].at[k], sems[3].at[k], sib).wait_recv()
        for w, k, me, src_chip, peer, sib, c in each():
            mine = outs[w].at[me, c]
            _remote(mine, mine, sems[0].at[k], sems[1].at[k], peer).wait_send()
            landed = outs[w].at[src_chip, c]
            _remote(landed, landed, sems[2].at[k], sems[3].at[k], sib).wait_send()

    return start, forward, finish


def _pair_exchange(grads):
    n_w = len(grads)
    start, finish = _pair_exchange_stages(n_w)

    def body(*refs):
        ins, outs, sems = refs[:n_w], refs[n_w:2 * n_w], refs[2 * n_w:]
        start(ins, outs, sems)
        finish(ins, outs, sems)

    return _comm_call("grad_pair_exchange", body, grads, _pair_exchange_shapes(grads), [N_CHIPS * n_w] * 2)


def _pair_exchange_shapes(grads):
    return [jax.ShapeDtypeStruct((N_CHIPS,) + g.shape[2:], g.dtype) for g in grads]


def _pair_exchange_stages(n_w):
    def each():
        x, y, c, _ = _place()
        for w in range(n_w):
            for q in range(N_CHIPS):
                yield w, q, N_CHIPS * w + q, c, (x, y, 1 - c)

    def start(ins, outs, sems):
        for w, q, k, c, sib in each():
            _remote(ins[w].at[q, 1 - c], outs[w].at[q], sems[0].at[k], sems[1].at[k], sib).start()

    def finish(ins, outs, sems):
        for w, q, k, c, sib in each():
            _remote(ins[w].at[q, 1 - c], outs[w].at[q], sems[0].at[k], sems[1].at[k], sib).wait()

    return start, finish


def _chip_exchange(parts):
    n_w = len(parts)

    start, finish = _chip_exchange_stages(n_w)

    def body(*refs):
        ins, outs, sems = refs[:n_w], refs[n_w:2 * n_w], refs[2 * n_w:]
        start(ins, outs, sems)
        finish(ins, outs, sems)

    out_shapes = [jax.ShapeDtypeStruct(t.shape, t.dtype) for t in parts]
    return _comm_call("grad_chip_exchange", body, parts, out_shapes, [3 * n_w, 3 * n_w])


def _chip_exchange_stages(n_w):
    def each():
        x, y, c, chips = _place()
        for w in range(n_w):
            for j, (cx, cy) in enumerate(chips):
                yield w, 3 * w + j, 2 * x + y, 2 * cx + cy, (cx, cy, c)

    def start(ins, outs, sems):
        for w, k, me, peer_chip, peer in each():
            _remote(ins[w].at[peer_chip], outs[w].at[me], sems[0].at[k], sems[1].at[k], peer).start()

    def finish(ins, outs, sems):
        for w, k, me, peer_chip, peer in each():
            got = outs[w].at[peer_chip]
            _remote(got, got, sems[0].at[k], sems[1].at[k], peer).wait_recv()
        for w, k, me, peer_chip, peer in each():
            _remote(ins[w].at[peer_chip], outs[w].at[me], sems[0].at[k], sems[1].at[k], peer).wait_send()

    return start, finish


def _pair_gather(halves):
    n_w = len(halves)

    def body(*refs):
        ins, outs = refs[:n_w], refs[n_w:2 * n_w]
        send, recv = refs[2 * n_w:]
        x, y, c, _ = _place()
        sib = (x, y, 1 - c)
        cps = []
        for w in range(n_w):
            cp = _remote(ins[w], outs[w], send.at[w], recv.at[w], sib)
            cp.start()
            cps.append(cp)
        for cp in cps:
            cp.wait()

    out_shapes = [jax.ShapeDtypeStruct(h.shape, h.dtype) for h in halves]
    return _comm_call("grad_pair_gather", body, halves, out_shapes, [n_w] * 2)


def _all_exchange_stages():
    def each():
        x, y, c, _ = _place()
        for k in range(1, 8):
            px, py, pc = x ^ ((k >> 2) & 1), y ^ ((k >> 1) & 1), c ^ (k & 1)
            yield k - 1, 4 * x + 2 * y + c, 4 * px + 2 * py + pc, (px, py, pc)

    def start(ins, outs, sems):
        for k, me, _, peer in each():
            _remote(ins[0], outs[0].at[me], sems[0].at[k], sems[1].at[k], peer).start()

    def finish(ins, outs, sems):
        for k, me, src, peer in each():
            got = outs[0].at[src]
            _remote(got, got, sems[0].at[k], sems[1].at[k], peer).wait_recv()
        for k, me, src, peer in each():
            _remote(ins[0], outs[0].at[me], sems[0].at[k], sems[1].at[k], peer).wait_send()

    return start, finish


def _row_tile(r):
    for t in (256, 128, 176, 64, 32, 16, 8):
        if r % t == 0:
            return t
    return r


P_C, P_CHIP, P_DEV = 2, 3, 4


def _cast_shard(w2):
    r, c = w2.shape
    t = _row_tile(r)
    blk, imap = _rows(t, c)
    return _ew("cast_own", (r // t,), [(w2, blk, imap)], [((r, c), BF16, blk, imap)], lambda pids, a: ((a,), ()))[0]


def _cast_into_slot(w2, place):
    r, c = w2.shape
    t = _row_tile(r)
    return _ew("cast_shard", (r // t,), [(w2, (t, c), lambda i, pv: (i, 0))],
               [((N_CHIPS, r, c), BF16, (None, t, c), lambda i, pv: (pv[P_CHIP], i, 0))],
               lambda pids, a: ((a,), ()), place=place)[0]


def _pair_sum(mine, theirs, place):
    _, r, c = theirs.shape
    t = _row_tile(r)
    own = ((None, None, t, c), lambda q, i, pv: (q, pv[P_C], i, 0))
    blk = ((None, t, c), lambda q, i, pv: (q, i, 0))
    return _ew("grad_pair_sum", (N_CHIPS, r // t), [(mine, *own), (theirs, *blk)], [((N_CHIPS, r, c), BF16, *blk)],
               lambda pids, a, b: ((a + b,), ()), place=place)[0]


def _chip_sum(own, got, place):
    _, r, c = own.shape
    t = _row_tile(r)
    ins = []
    for q in range(N_CHIPS):
        ins.append((own, (None, t, c), (lambda i, pv, q=q: (q, i, 0))))
        ins.append((got, (None, t, c), (lambda i, pv, q=q: (jnp.where(pv[P_CHIP] == q, (q + 1) % N_CHIPS, q), i, 0))))

    def fn(pids, *tiles):
        me = pids[0][P_CHIP]
        tot = None
        for q in range(N_CHIPS):
            term = jnp.where(me == q, tiles[2 * q], tiles[2 * q + 1]).astype(F32)
            tot = term if tot is None else tot + term
        return (tot,), ()

    return _ew("grad_chip_sum", (r // t,), ins, [((r, c), F32, (t, c), lambda i, pv: (i, 0))], fn, place=place)[0]


def _adamw_tile(w, g, m, v):
    m = ADAM_B1 * m + (1.0 - ADAM_B1) * g
    v = ADAM_B2 * v + (1.0 - ADAM_B2) * (g * g)
    m_hat = m / (1.0 - ADAM_B1 ** ADAM_STEP)
    v_hat = v / (1.0 - ADAM_B2 ** ADAM_STEP)
    delta = -ADAM_LR * (m_hat / (jnp.sqrt(v_hat) + ADAM_EPS) + ADAM_WD * w)
    return delta, m, v


def _adamw(name, g2, w2, m2, v2):
    r, c = w2.shape
    t = _row_tile(r)
    blk, imap = _rows(t, c)

    def fn(pids, g, w, m, v):
        delta, nm, nv = _adamw_tile(w, g, m, v)
        return (g, delta, nm, nv), ()

    return _ew(name, (r // t,), [(a, blk, imap) for a in (g2, w2, m2, v2)], [((r, c), F32, blk, imap)] * 4, fn)


def _adamw_halves(name, mine, theirs, w2, m2, v2, place):
    r, c = w2.shape
    t = _row_tile(r // 2)
    n_t = (r // 2) // t
    half = ((t, c), lambda h, i, pv: (i, 0))
    whole = ((t, c), lambda h, i, pv: (h * n_t + i, 0))

    def fn(pids, ga, gb, w, m, v):
        g = jnp.where(pids[1] == pids[0][P_C], ga, gb)
        delta, nm, nv = _adamw_tile(w, g, m, v)
        return (g, delta, nm, nv), ()

    return _ew(name, (2, n_t), [(mine, *half), (theirs, *half), (w2, *whole), (m2, *whole), (v2, *whole)],
               [((r, c), F32, *whole)] * 4, fn, place=place)


def _device_sum(own, got, place):
    r, c = own.shape
    t = _row_tile(r)
    ins = [(own, (t, c), lambda i, pv: (i, 0))]
    for q in range(8):
        ins.append((got, (None, t, c), (lambda i, pv, q=q: (jnp.where(pv[P_DEV] == q, (q + 1) % 8, q), i, 0))))

    def fn(pids, mine, *parts):
        me = pids[0][P_DEV]
        tot = None
        for q in range(8):
            term = jnp.where(me == q, mine, parts[q])
            tot = term if tot is None else tot + term
        return (tot,), ()

    return _ew("small_device_sum", (r // t,), ins, [((r, c), F32, (t, c), lambda i, pv: (i, 0))], fn, place=place)[0]


def _pack(parts):
    flat = jnp.concatenate([a.reshape(-1) for a in parts])
    pad = (-flat.shape[0]) % (SUB * 128)
    return jnp.pad(flat, (0, pad)).reshape(-1, 128)


def _unpack(mat, shapes):
    flat = mat.reshape(-1)
    out, off = [], 0
    for shp in shapes:
        n = math.prod(shp)
        out.append(flat[off:off + n].reshape(shp))
        off += n
    return out


def kernel(x, p, positions, g_mix, w_in, a_re, a_im, log_dt, b_re, b_im, c_re, c_im, d_skip, w_attn_proj, w_glu_a, w_glu_b, w_out, g_ffn, w_ffn_gate, w_ffn_up, w_ffn_down, w_ple_gate, w_ple_proj, g_final, loss_target, m_g_mix, m_w_in, m_a_re, m_a_im, m_log_dt, m_b_re, m_b_im, m_c_re, m_c_im, m_d_skip, m_w_attn_proj, m_w_glu_a, m_w_glu_b, m_w_out, m_g_ffn, m_w_ffn_gate, m_w_ffn_up, m_w_ffn_down, m_w_ple_gate, m_w_ple_proj, m_g_final, v_g_mix, v_w_in, v_a_re, v_a_im, v_log_dt, v_b_re, v_b_im, v_c_re, v_c_im, v_d_skip, v_w_attn_proj, v_w_glu_a, v_w_glu_b, v_w_out, v_g_ffn, v_w_ffn_gate, v_w_ffn_up, v_w_ffn_down, v_w_ple_gate, v_w_ple_proj, v_g_final):
    given = dict(locals())
    big_w = {n: given[n] for n in BIG}
    w_mats = {n: big_w[n].reshape(big_w[n].shape[1:]) for n in BIG}

    ax, ay, ac = lax.axis_index("x"), lax.axis_index("y"), lax.axis_index("c")
    place = jnp.stack([ax, ay, ac, 2 * ax + ay, 4 * ax + 2 * ay + ac]).astype(jnp.int32)

    bufs = []
    for n in BIG:
        r, c = w_mats[n].shape
        bufs.append(_cast_into_slot(w_mats[n], place).reshape(N_CHIPS, 2, r // 2, c))
    w_in_own = _cast_shard(w_mats["w_in"])

    sm = {
        "g_mix": g_mix.reshape(1, D_MODEL), "g_ffn": g_ffn.reshape(1, D_MODEL), "g_final": g_final.reshape(1, D_MODEL),
        "a_re": a_re[0], "a_im": a_im[0], "log_dt": log_dt[0], "b_re": b_re[0], "b_im": b_im[0], "c_re": c_re[0],
        "c_im": c_im[0], "d_skip": d_skip[0],
    }
    s = x.shape[1]
    grad_x, d_w_in, early_parts, early_got, vec, vec_got = _local_step(
        x[0], p[0, 0], positions[0], loss_target[0], sm, w_in_own, bufs[0], bufs[1:], place)

    r_in, c_in = w_mats["w_in"].shape
    g5_in = [d_w_in.reshape(N_CHIPS, 2, r_in // 2, c_in)]
    in_parts = [_pair_sum(g, t, place) for g, t in zip(g5_in, _pair_exchange(g5_in))]
    chip_parts = in_parts + list(early_parts)
    chip_got = list(_chip_exchange(in_parts)) + list(early_got)
    halves = [_chip_sum(own, got, place) for own, got in zip(chip_parts, chip_got)]
    other_halves = _pair_gather(halves)

    results = {}
    for n, mine, other in zip(BIG, halves, other_halves):
        r, c = w_mats[n].shape
        shp = big_w[n].shape
        outs = _adamw_halves("adamw_" + n, mine, other, w_mats[n], given["m_" + n].reshape(r, c),
                             given["v_" + n].reshape(r, c), place)
        results[n] = [o.reshape(shp) for o in outs]

    small_shapes = [given[n].shape for n in SMALL]
    tot = _device_sum(vec, vec_got, place)
    n_small = sum(math.prod(shp) for shp in small_shapes)
    loss = tot.reshape(-1)[n_small]
    w_s = _pack([given[n] for n in SMALL])
    m_s = _pack([given["m_" + n] for n in SMALL])
    v_s = _pack([given["v_" + n] for n in SMALL])
    rows_s = w_s.shape[0]
    g_s = tot.reshape(-1)[: rows_s * 128].reshape(rows_s, 128)
    outs_s = _adamw("adamw_small", g_s, w_s, m_s, v_s)
    for kind, mat in enumerate(outs_s):
        for n, arr in zip(SMALL, _unpack(mat, small_shapes)):
            results.setdefault(n, [None] * 4)[kind] = arr

    order = ("g_mix", "w_in", "a_re", "a_im", "log_dt", "b_re", "b_im", "c_re", "c_im", "d_skip", "w_attn_proj", "w_glu_a",
             "w_glu_b", "w_out", "g_ffn", "w_ffn_gate", "w_ffn_up", "w_ffn_down", "w_ple_gate", "w_ple_proj", "g_final")
    out = [loss, grad_x.reshape(1, s, D_MODEL)]
    for kind in range(4):
        out += [results[n][kind] for n in order]
    return tuple(out)
```

```python
import math

import jax
import jax.numpy as jnp
from jax import lax
from jax.experimental import pallas as pl
from jax.experimental.pallas import tpu as pltpu

F32 = jnp.float32
BF16 = jnp.bfloat16

D_MODEL = 1024
HEAD_DIM = 128
HEADS_PER_GROUP = 4
GROUP_WIDTH = HEADS_PER_GROUP * HEAD_DIM
GROUP_DILATIONS = (1, 4, 16)
N_GROUPS = len(GROUP_DILATIONS)
LSE_LANES = 32
LSE_WIDTH = HEADS_PER_GROUP * LSE_LANES
ATTN_BLOCK = 128
ROPE_DIM = 32
ROPE_HALF = 16
ROPE_THETA = 500000.0
SSM_WIDTH = 512
SSM_GROUPS = 32
SSM_GROUP = 16
SSM_STATE = 64
N_STATE = SSM_GROUPS * SSM_STATE
SSM_SUPER = 4
IN_WIDTH = 7168
COL_U = 4608
COL_GA = 5120
COL_GS = 6144
D_FF = 2816
N_CHIPS = 4
D_FF_Q = D_FF // N_CHIPS
PLE_DIM = 256
EPS = 1e-6
ADAM_LR = 0.001
ADAM_B1 = 0.9
ADAM_B2 = 0.999
ADAM_EPS = 1e-08
ADAM_WD = 0.01
ADAM_STEP = 10
NEG_BIG = -1e30
VMEM_LIMIT_BYTES = 56 * 1024 * 1024
MESH = pl.DeviceIdType.MESH

_DIMS = {
    "nn": (((1,), (0,)), ((), ())),
    "nt": (((1,), (1,)), ((), ())),
    "tn": (((0,), (0,)), ((), ())),
}


def _params(n_grid):
    return pltpu.CompilerParams(dimension_semantics=("arbitrary",) * n_grid, vmem_limit_bytes=VMEM_LIMIT_BYTES)


def _sig(v):
    return 1.0 / (1.0 + jnp.exp(-v))


def _dot(a, b, mode):
    return lax.dot_general(a, b, _DIMS[mode], preferred_element_type=F32)


def _mm(name, grid, pairs, mode, outs, epilogue=None, extras=(), acc_outs=(), acc_shape=None, j_outer=False,
        sum_pairs=True, resident_b=False, comm=None, place=None, fill=None, prologue=None):
    gi, gj, gk = grid
    n_p, n_e, n_o, n_a = len(pairs), len(extras), len(outs), len(acc_outs)
    assert not n_a or gj == 1
    assert sum_pairs or gk == 1
    run_grid = (gj, gi, gk) if j_outer else grid
    c_ins = list(comm["ins"]) if comm else []
    c_outs = list(comm["outs"]) if comm else []
    c_sems = list(comm["sems"]) if comm else []
    n_ci, n_co, n_cs = len(c_ins), len(c_outs), len(c_sems)
    n_s = 0 if place is None else 1
    n_fill = 0 if fill is None else 1

    def order(imap):
        if place is None:
            return (lambda j, i, k: imap(i, j, k)) if j_outer else imap
        return (lambda j, i, k, pv: imap(i, j, k, pv)) if j_outer else imap

    shared_a = [pr[0] is None for pr in pairs]
    n_in = 2 * n_p - sum(shared_a)

    def body(*refs):
        refs = refs[n_s:]
        pair_refs = list(refs[:n_in])
        extra_refs = refs[n_in: n_in + n_e]
        comm_in = refs[n_in + n_e: n_in + n_e + n_ci]
        at = n_in + n_e + n_ci + n_fill
        out_refs = refs[at: at + n_o]
        sum_refs = refs[at + n_o: at + n_o + n_a]
        comm_out = refs[at + n_o + n_a: at + n_o + n_a + n_co]
        scratch_refs = refs[at + n_o + n_a + n_co:]
        i = pl.program_id(1 if j_outer else 0)
        k = pl.program_id(2)
        if comm:
            step = (pl.program_id(0) * run_grid[1] + pl.program_id(1)) * run_grid[2] + pl.program_id(2)
            sems = scratch_refs[len(scratch_refs) - n_cs:]
            for at_step, stage in comm["stages"]:
                @pl.when(step == at_step)
                def _(stage=stage):
                    stage(comm_in, comm_out, sems)
        part = None if sum_pairs else []
        a = None
        for t in range(n_p):
            if not shared_a[t]:
                a = pair_refs.pop(0)[...]
                if prologue is not None and t == 0:
                    a = prologue(a, *[e[...] for e in extra_refs]).astype(BF16)
                    out_refs[n_o - 1][...] = a
                a = a.astype(BF16)
            b = pair_refs.pop(0)[...].astype(BF16)
            d = _dot(a, b, mode)
            if sum_pairs:
                part = d if part is None else part + d
            else:
                part.append(d)

        def finish(acc):
            tiles, sums = epilogue(acc, *[e[...] for e in extra_refs]) if epilogue is not None else ((acc,), ())
            for o_ref, tile in zip(out_refs, tiles):
                o_ref[...] = tile.astype(o_ref.dtype)
            if n_a:
                @pl.when(i == 0)
                def _():
                    for s_ref in sum_refs:
                        s_ref[...] = jnp.zeros_like(s_ref)

                for s_ref, s in zip(sum_refs, sums):
                    s_ref[...] += s

        if gk == 1:
            finish(part)
        else:
            acc_ref = scratch_refs[0]

            @pl.when(k == 0)
            def _():
                acc_ref[...] = part

            @pl.when(k > 0)
            def _():
                acc_ref[...] += part

            @pl.when(k == gk - 1)
            def _():
                finish(acc_ref[...])

    in_specs, args = [], []
    for a, a_block, a_imap, b, b_block, b_imap in pairs:
        if a is not None:
            in_specs.append(pl.BlockSpec(a_block, order(a_imap)))
            args.append(a)
        if resident_b:
            in_specs.append(pl.BlockSpec(b_block, order(b_imap), pipeline_mode=pl.Buffered(1)))
        else:
            in_specs.append(pl.BlockSpec(b_block, order(b_imap)))
        args.append(b)
    for e, e_block, e_imap in extras:
        in_specs.append(pl.BlockSpec(e_block, order(e_imap)))
        args.append(e)
    first_comm_in = len(args)
    for c_in in c_ins:
        in_specs.append(pl.BlockSpec(memory_space=pl.ANY))
        args.append(c_in)
    if n_fill:
        in_specs.append(pl.BlockSpec(memory_space=pl.ANY))
        args.append(fill)
    out_shape = [jax.ShapeDtypeStruct(shape, dtype) for shape, dtype, _, _ in outs]
    out_specs = [pl.BlockSpec(block, order(imap)) for _, _, block, imap in outs]
    for shape, dtype in acc_outs:
        out_shape.append(jax.ShapeDtypeStruct(shape, dtype))
        out_specs.append(pl.BlockSpec(shape, lambda *_: (0, 0)))
    first_comm_out = len(out_shape)
    for c_out in c_outs:
        out_shape.append(c_out)
        out_specs.append(pl.BlockSpec(memory_space=pl.ANY))
    aliases = {n_s + first_comm_in + n: first_comm_out + n for n in range(n_ci)} if comm and comm["aliased"] else {}
    if n_fill:
        aliases[n_s + len(args) - 1] = 0
    scratch = [pltpu.VMEM(acc_shape, F32)] if gk > 1 else []
    scratch += [pltpu.SemaphoreType.DMA((n,)) for n in c_sems]
    if n_s:
        spec = pltpu.PrefetchScalarGridSpec(num_scalar_prefetch=1, grid=run_grid, in_specs=in_specs, out_specs=out_specs,
                                            scratch_shapes=scratch)
        return pl.pallas_call(body, name=name, grid_spec=spec, out_shape=out_shape, compiler_params=_params(3),
                              input_output_aliases=aliases)(place, *args)
    return pl.pallas_call(
        body, name=name, grid=run_grid, in_specs=in_specs, out_specs=out_specs,
        out_shape=out_shape, scratch_shapes=scratch, compiler_params=_params(3), input_output_aliases=aliases,
    )(*args)


def _ew(name, grid, ins, outs, fn, acc_outs=(), place=None):
    n_i, n_o, n_a = len(ins), len(outs), len(acc_outs)
    ng = len(grid)
    n_s = 0 if place is None else 1

    def body(*refs):
        in_refs = refs[n_s: n_s + n_i]
        out_refs = refs[n_s + n_i: n_s + n_i + n_o]
        sum_refs = refs[n_s + n_i + n_o:]
        pids = tuple(pl.program_id(a) for a in range(ng))
        if n_s:
            pids = (refs[0],) + pids
        tiles, sums = fn(pids, *[r[...] for r in in_refs])
        for o_ref, tile in zip(out_refs, tiles):
            o_ref[...] = tile.astype(o_ref.dtype)
        if n_a:
            first = pids[0] == 0
            for p_ in pids[1:]:
                first = jnp.logical_and(first, p_ == 0)

            @pl.when(first)
            def _():
                for s_ref in sum_refs:
                    s_ref[...] = jnp.zeros_like(s_ref)

            for s_ref, s in zip(sum_refs, sums):
                s_ref[...] += s

    in_specs = [pl.BlockSpec(block, imap) for _, block, imap in ins]
    out_shape = [jax.ShapeDtypeStruct(shape, dtype) for shape, dtype, _, _ in outs]
    out_specs = [pl.BlockSpec(block, imap) for _, _, block, imap in outs]
    for shape, dtype in acc_outs:
        out_shape.append(jax.ShapeDtypeStruct(shape, dtype))
        out_specs.append(pl.BlockSpec(shape, lambda *_, nd=len(shape): (0,) * nd))
    arrays = [a for a, _, _ in ins]
    if n_s:
        assert not n_a
        spec = pltpu.PrefetchScalarGridSpec(num_scalar_prefetch=1, grid=grid, in_specs=in_specs, out_specs=out_specs)
        return pl.pallas_call(body, name=name, grid_spec=spec, out_shape=out_shape, compiler_params=_params(ng))(
            place, *arrays)
    return pl.pallas_call(
        body, name=name, grid=grid, in_specs=in_specs, out_specs=out_specs, out_shape=out_shape,
        compiler_params=_params(ng),
    )(*arrays)


def _rows(tm, width):
    return (tm, width), (lambda i: (i, 0))


def _rms_fwd_tile(h, g):
    r = lax.rsqrt(jnp.mean(h * h, axis=-1, keepdims=True) + EPS)
    return h * r * g


def _rms_bwd_tile(dn, h, g):
    r = lax.rsqrt(jnp.mean(h * h, axis=-1, keepdims=True) + EPS)
    hhat = h * r
    gy = dn * g
    dh = r * (gy - hhat * jnp.mean(gy * hhat, axis=-1, keepdims=True))
    dg = jnp.sum(dn * hhat, axis=0, keepdims=True)
    return dh, dg


def _rope_tables(pos_col, inv_row, tm):
    s = pos_col.shape[0]

    def fn(pids, pos, inv):
        ang = pos * inv
        lane = lax.broadcasted_iota(jnp.int32, ang.shape, 1)
        cs = jnp.where(lane < ROPE_DIM, jnp.cos(ang), 1.0)
        sn = jnp.sin(ang)
        s_lo = jnp.where(lane < ROPE_HALF, -sn, 0.0)
        s_hi = jnp.where(jnp.logical_and(lane >= ROPE_HALF, lane < ROPE_DIM), sn, 0.0)
        return (cs, s_lo, s_hi), ()

    blk, imap = _rows(tm, 128)
    return _ew(
        "rope_tables", (s // tm,),
        [(pos_col, (tm, 1), lambda i: (i, 0)), (inv_row, (1, 128), lambda i: (0, 0))],
        [((s, 128), F32, blk, imap)] * 3, fn,
    )


def _rope(xh, cs, s_lo, s_hi):
    return xh * cs + pltpu.roll(xh, HEAD_DIM - ROPE_HALF, 1) * s_lo + pltpu.roll(xh, ROPE_HALF, 1) * s_hi


def _rope_t(gh, cs, s_lo, s_hi):
    return gh * cs + pltpu.roll(gh * s_lo, ROPE_HALF, 1) + pltpu.roll(gh * s_hi, HEAD_DIM - ROPE_HALF, 1)


def _attn_geometry(length):
    nb = length // ATTN_BLOCK
    gq = min(8, nb)
    assert nb % gq == 0
    return nb, gq, gq * ATTN_BLOCK, nb // gq


def _band_masks():
    qi = lax.broadcasted_iota(jnp.int32, (ATTN_BLOCK, ATTN_BLOCK), 0)
    kj = lax.broadcasted_iota(jnp.int32, (ATTN_BLOCK, ATTN_BLOCK), 1)
    return kj <= qi, kj >= qi


def _band_mask_pair():
    qi = lax.broadcasted_iota(jnp.int32, (ATTN_BLOCK, 2 * ATTN_BLOCK), 0)
    cj = lax.broadcasted_iota(jnp.int32, (ATTN_BLOCK, 2 * ATTN_BLOCK), 1)
    in_cur = cj >= ATTN_BLOCK
    band = jnp.logical_or(jnp.logical_and(in_cur, cj - ATTN_BLOCK <= qi),
                          jnp.logical_and(cj < ATTN_BLOCK, cj >= qi))
    return band, in_cur


def _attn_fwd(qv, kv, vv, dil, cols3=(0, 0, 0)):
    length = qv.shape[0]
    nb, gq, rows, ni = _attn_geometry(length)

    def body(q_ref, kc_ref, kp_ref, vc_ref, vp_ref, o_ref, l_ref):
        i = pl.program_id(1)
        band, in_cur = _band_mask_pair()
        band_first = jnp.logical_and(band, jnp.logical_or(in_cur, i > 0))
        work = []
        for h in range(HEADS_PER_GROUP):
            cols = slice(h * HEAD_DIM, (h + 1) * HEAD_DIM)
            qh = q_ref[:, cols]
            k_all = jnp.concatenate([kp_ref[:, cols], kc_ref[:, cols]], axis=0)
            v_all = jnp.concatenate([vp_ref[:, cols], vc_ref[:, cols]], axis=0)
            for jj in range(gq):
                rws = slice(jj * ATTN_BLOCK, (jj + 1) * ATTN_BLOCK)
                two = slice(jj * ATTN_BLOCK, (jj + 2) * ATTN_BLOCK)
                work.append(dict(h=h, rws=rws, cols=cols, v=v_all[two], first=jj == 0, s=_dot(qh[rws], k_all[two], "nt")))
        for w in work:
            s = jnp.where(band_first if w["first"] else band, w["s"], NEG_BIG)
            m = jnp.max(s, axis=-1, keepdims=True)
            pexp = jnp.exp(s - m)
            w["den"] = jnp.sum(pexp, axis=-1, keepdims=True)
            w["p"] = pexp.astype(BF16)
            w["lse"] = m + jnp.log(w["den"])
        for w in work:
            o = _dot(w["p"], w["v"], "nn")
            o_ref[w["rws"], w["cols"]] = (o * (1.0 / w["den"])).astype(o_ref.dtype)
            l_ref[w["rws"], w["h"] * LSE_LANES:(w["h"] + 1) * LSE_LANES] = jnp.broadcast_to(w["lse"], (ATTN_BLOCK, LSE_LANES))

    def cur(c):
        return pl.BlockSpec((rows, GROUP_WIDTH), lambda r, i: (i, r + c))

    def prev(c):
        return pl.BlockSpec((ATTN_BLOCK, GROUP_WIDTH), lambda r, i: (jnp.maximum(i * gq - 1, 0), r + c))

    cq, ck, cv = cols3
    return pl.pallas_call(
        body, name=f"attn_fwd_d{dil}", grid=(dil, ni),
        in_specs=[cur(cq), cur(ck), prev(ck), cur(cv), prev(cv)],
        out_specs=[cur(0), pl.BlockSpec((rows, LSE_WIDTH), lambda r, i: (i, r))],
        out_shape=[jax.ShapeDtypeStruct((length, dil * GROUP_WIDTH), BF16),
                   jax.ShapeDtypeStruct((length, dil * LSE_WIDTH), F32)],
        compiler_params=_params(2),
    )(qv, kv, kv, vv, vv)


def _attn_bwd(qv, kv, vv, dov, ov, lv, dil, cols3=(0, 0, 0)):
    length = qv.shape[0]
    nb, gq, rows, ni = _attn_geometry(length)
    out_shape = (length, dil * GROUP_WIDTH)

    def body(qc_ref, qn_ref, kc_ref, kp_ref, vc_ref, vp_ref, doc_ref, don_ref, oc_ref, on_ref, lc_ref, ln_ref,
             dq_ref, dk_ref, dv_ref):
        i = pl.program_id(1)
        _, mask_p = _band_masks()
        band, in_cur = _band_mask_pair()
        band_first = jnp.logical_and(band, jnp.logical_or(in_cur, i > 0))
        has_next = i < ni - 1

        last = slice(gq * ATTN_BLOCK, (gq + 1) * ATTN_BLOCK)
        mask_next = jnp.logical_and(mask_p, has_next)

        def rows_of(jj):
            return slice(jj * ATTN_BLOCK, (jj + 1) * ATTN_BLOCK)

        def keys_of(jj):
            return slice(jj * ATTN_BLOCK, (jj + 2) * ATTN_BLOCK)

        heads = []
        for h in range(HEADS_PER_GROUP):
            cols = slice(h * HEAD_DIM, (h + 1) * HEAD_DIM)
            hd = dict(
                cols=cols, q_c=qc_ref[:, cols], q_n=qn_ref[:, cols],
                k_all=jnp.concatenate([kp_ref[:, cols], kc_ref[:, cols]], axis=0),
                v_all=jnp.concatenate([vp_ref[:, cols], vc_ref[:, cols]], axis=0),
                do_c=doc_ref[:, cols], do_n=don_ref[:, cols],
                l_c=lc_ref[:, h * LSE_LANES:h * LSE_LANES + 1], l_n=ln_ref[:, h * LSE_LANES:h * LSE_LANES + 1],
            )
            hd["dl_c"] = jnp.sum(hd["do_c"].astype(F32) * oc_ref[:, cols].astype(F32), axis=-1, keepdims=True)
            hd["dl_n"] = jnp.sum(hd["do_n"].astype(F32) * on_ref[:, cols].astype(F32), axis=-1, keepdims=True)
            hd["s"] = [_dot(hd["q_c"][rows_of(jj)], hd["k_all"][keys_of(jj)], "nt") for jj in range(gq)]
            hd["dp"] = [_dot(hd["do_c"][rows_of(jj)], hd["v_all"][keys_of(jj)], "nt") for jj in range(gq)]
            hd["s"].append(_dot(hd["q_n"], hd["k_all"][last], "nt"))
            hd["dp"].append(_dot(hd["do_n"], hd["v_all"][last], "nt"))
            heads.append(hd)
        for hd in heads:
            hd["p"], hd["ds"] = [], []
            for jj in range(gq + 1):
                if jj < gq:
                    mask, l_col, delta = (band_first if jj == 0 else band), hd["l_c"][rows_of(jj)], hd["dl_c"][rows_of(jj)]
                else:
                    mask, l_col, delta = mask_next, hd["l_n"], hd["dl_n"]
                p = jnp.where(mask, jnp.exp(hd["s"][jj] - l_col), 0.0)
                hd["p"].append(p.astype(BF16))
                hd["ds"].append((p * (hd["dp"][jj] - delta)).astype(BF16))
        for hd in heads:
            cols = hd["cols"]
            dk_blocks, dv_blocks = [None] * (gq + 1), [None] * (gq + 1)

            def add(lst, idx, val):
                lst[idx] = val if lst[idx] is None else lst[idx] + val

            for jj in range(gq):
                qb, dob = hd["q_c"][rows_of(jj)], hd["do_c"][rows_of(jj)]
                dq_ref[rows_of(jj), cols] = _dot(hd["ds"][jj], hd["k_all"][keys_of(jj)], "nn").astype(dq_ref.dtype)
                dk2 = _dot(hd["ds"][jj], qb, "tn")
                dv2 = _dot(hd["p"][jj], dob, "tn")
                add(dk_blocks, jj, dk2[:ATTN_BLOCK])
                add(dk_blocks, jj + 1, dk2[ATTN_BLOCK:])
                add(dv_blocks, jj, dv2[:ATTN_BLOCK])
                add(dv_blocks, jj + 1, dv2[ATTN_BLOCK:])
            add(dk_blocks, gq, _dot(hd["ds"][gq], hd["q_n"], "tn"))
            add(dv_blocks, gq, _dot(hd["p"][gq], hd["do_n"], "tn"))
            for jj in range(gq):
                dk_ref[rows_of(jj), cols] = dk_blocks[jj + 1].astype(dk_ref.dtype)
                dv_ref[rows_of(jj), cols] = dv_blocks[jj + 1].astype(dv_ref.dtype)

    def cur(c):
        return pl.BlockSpec((rows, GROUP_WIDTH), lambda r, i: (i, r + c))

    def prev(c):
        return pl.BlockSpec((ATTN_BLOCK, GROUP_WIDTH), lambda r, i: (jnp.maximum(i * gq - 1, 0), r + c))

    def nxt(c):
        return pl.BlockSpec((ATTN_BLOCK, GROUP_WIDTH), lambda r, i: (jnp.minimum((i + 1) * gq, nb - 1), r + c))

    cq, ck, cv = cols3
    lse_cur = pl.BlockSpec((rows, LSE_WIDTH), lambda r, i: (i, r))
    lse_next = pl.BlockSpec((ATTN_BLOCK, LSE_WIDTH), lambda r, i: (jnp.minimum((i + 1) * gq, nb - 1), r))
    return pl.pallas_call(
        body, name=f"attn_bwd_d{dil}", grid=(dil, ni),
        in_specs=[cur(cq), nxt(cq), cur(ck), prev(ck), cur(cv), prev(cv), cur(0), nxt(0), cur(0), nxt(0), lse_cur, lse_next],
        out_specs=[cur(0), cur(0), cur(0)],
        out_shape=[jax.ShapeDtypeStruct(out_shape, BF16)] * 3,
        compiler_params=_params(2),
    )(qv, qv, kv, kv, vv, vv, dov, dov, ov, ov, lv, lv)


DILATED = tuple((g, d) for g, d in enumerate(GROUP_DILATIONS) if d > 1)


def _spread(scr, slot, tile, out_ref, dil, col, width=GROUP_WIDTH):
    tm = tile.shape[0]
    buf = scr.at[slot]
    buf[...] = tile
    for r in range(dil):
        c0 = r * width + col
        out_ref[:, c0:c0 + HEAD_DIM] = buf[pl.ds(r, tm // dil, stride=dil), :].astype(out_ref.dtype)


def _collect(scr, slot, in_ref, dil, col, width=GROUP_WIDTH):
    tm = scr.shape[1]
    buf = scr.at[slot]
    for r in range(dil):
        c0 = r * width + col
        buf[pl.ds(r, tm // dil, stride=dil), :] = in_ref[:, c0:c0 + HEAD_DIM].astype(F32)
    return buf[...]


def _view_spec(tm, dil, width=GROUP_WIDTH):
    return pl.BlockSpec((tm // dil, dil * width), lambda i: (i, 0))


def _view_shape(s, dil, dtype, width=GROUP_WIDTH):
    return jax.ShapeDtypeStruct((s // dil, dil * width), dtype)


def _qkv_layout(z, tabs, tm):
    s = z.shape[0]
    scale = 1.0 / math.sqrt(HEAD_DIM)
    qkv_width = 3 * N_GROUPS * GROUP_WIDTH

    def body(z_ref, cs_ref, lo_ref, hi_ref, qk0_ref, *rest):
        views, scr = rest[:-1], rest[-1]
        tabs_ = (cs_ref[...], lo_ref[...], hi_ref[...])
        for part in range(3):
            for g, dil in enumerate(GROUP_DILATIONS):
                if part == 2 and dil == 1:
                    continue
                for h in range(HEADS_PER_GROUP):
                    col = part * N_GROUPS * GROUP_WIDTH + g * GROUP_WIDTH + h * HEAD_DIM
                    t = z_ref[:, col:col + HEAD_DIM].astype(F32)
                    if part < 2:
                        t = _rope(t, *tabs_)
                    if part == 0:
                        t = t * scale
                    if dil == 1:
                        c0 = part * GROUP_WIDTH + h * HEAD_DIM
                        qk0_ref[:, c0:c0 + HEAD_DIM] = t.astype(BF16)
                    else:
                        out = views[3 * [gg for gg, _ in DILATED].index(g) + part]
                        _spread(scr, h, t, out, dil, h * HEAD_DIM)

    row = lambda i: (i, 0)
    out_shape = [jax.ShapeDtypeStruct((s, 2 * GROUP_WIDTH), BF16)]
    out_specs = [pl.BlockSpec((tm, 2 * GROUP_WIDTH), row)]
    for _, dil in DILATED:
        out_shape += [_view_shape(s, dil, BF16)] * 3
        out_specs += [_view_spec(tm, dil)] * 3
    res = pl.pallas_call(
        body, name="qkv_layout", grid=(s // tm,),
        in_specs=[pl.BlockSpec((tm, qkv_width), row)] + [pl.BlockSpec((tm, HEAD_DIM), row)] * 3,
        out_specs=out_specs, out_shape=out_shape,
        scratch_shapes=[pltpu.VMEM((HEADS_PER_GROUP, tm, HEAD_DIM), F32)], compiler_params=_params(1),
    )(z, *tabs)
    return res[0], [tuple(res[1 + 3 * n:4 + 3 * n]) for n in range(len(DILATED))]


def _attn_merge(o0, l0, dilated, tm):
    s = o0.shape[0]
    n_d = len(DILATED)

    def body(*refs):
        o0_ref, l0_ref = refs[:2]
        in_views = refs[2:2 + 2 * n_d]
        attn_ref, lse_ref = refs[2 + 2 * n_d:4 + 2 * n_d]
        out_views = refs[4 + 2 * n_d:4 + 4 * n_d]
        scr = refs[-1]
        l_rows = [l0_ref[...]] + [_collect(scr, n, in_views[2 * n + 1], dil, 0, LSE_WIDTH) for n, (_, dil) in enumerate(DILATED)]
        lse_heads = []
        for h in range(HEADS_PER_GROUP):
            cols = slice(h * HEAD_DIM, (h + 1) * HEAD_DIM)
            os_ = [o0_ref[:, cols].astype(F32)]
            for n, (_, dil) in enumerate(DILATED):
                os_.append(_collect(scr, n_d + n, in_views[2 * n], dil, h * HEAD_DIM))
            ls_ = [lr[:, h * LSE_LANES:h * LSE_LANES + 1] for lr in l_rows]
            m = ls_[0]
            for l_ in ls_[1:]:
                m = jnp.maximum(m, l_)
            es = [jnp.exp(l_ - m) for l_ in ls_]
            den = es[0]
            num = es[0] * os_[0]
            for e, o in zip(es[1:], os_[1:]):
                den = den + e
                num = num + e * o
            attn = num * (1.0 / den)
            lse_heads.append(jnp.broadcast_to(m + jnp.log(den), (tm, LSE_LANES)))
            attn_ref[:, cols] = attn.astype(BF16)
            for n, (_, dil) in enumerate(DILATED):
                _spread(scr, 2 * n_d, attn, out_views[2 * n], dil, h * HEAD_DIM)
        lse = jnp.concatenate(lse_heads, axis=1)
        lse_ref[...] = lse
        for n, (_, dil) in enumerate(DILATED):
            _spread(scr, 2 * n_d, lse, out_views[2 * n + 1], dil, 0, LSE_WIDTH)

    row = lambda i: (i, 0)
    nat = pl.BlockSpec((tm, GROUP_WIDTH), row)
    nat_l = pl.BlockSpec((tm, LSE_WIDTH), row)
    in_specs = [nat, nat_l]
    args = [o0, l0]
    out_specs = [nat, nat_l]
    out_shape = [jax.ShapeDtypeStruct((s, GROUP_WIDTH), BF16), jax.ShapeDtypeStruct((s, LSE_WIDTH), F32)]
    for (_, dil), (ov, lv) in zip(DILATED, dilated):
        in_specs += [_view_spec(tm, dil), _view_spec(tm, dil, LSE_WIDTH)]
        args += [ov, lv]
        out_specs += [_view_spec(tm, dil), _view_spec(tm, dil, LSE_WIDTH)]
        out_shape += [_view_shape(s, dil, BF16), _view_shape(s, dil, F32, LSE_WIDTH)]
    res = pl.pallas_call(
        body, name="attn_merge", grid=(s // tm,), in_specs=in_specs, out_specs=out_specs, out_shape=out_shape,
        scratch_shapes=[pltpu.VMEM((2 * n_d + 1, tm, HEAD_DIM), F32)], compiler_params=_params(1),
    )(*args)
    return res[0], res[1], [tuple(res[2 + 2 * n:4 + 2 * n]) for n in range(n_d)]


def _to_views(a, tm):
    s = a.shape[0]

    def body(a_ref, *rest):
        outs, scr = rest[:-1], rest[-1]
        for h in range(HEADS_PER_GROUP):
            t = a_ref[:, h * HEAD_DIM:(h + 1) * HEAD_DIM].astype(F32)
            for n, (_, dil) in enumerate(DILATED):
                _spread(scr, n, t, outs[n], dil, h * HEAD_DIM)

    return pl.pallas_call(
        body, name="to_views", grid=(s // tm,), in_specs=[pl.BlockSpec((tm, GROUP_WIDTH), lambda i: (i, 0))],
        out_specs=[_view_spec(tm, dil) for _, dil in DILATED], out_shape=[_view_shape(s, dil, BF16) for _, dil in DILATED],
        scratch_shapes=[pltpu.VMEM((len(DILATED), tm, HEAD_DIM), F32)], compiler_params=_params(1),
    )(a)


def _dz_layout(grads, du, dga, dgs, tabs, tm, comm=None):
    s = du.shape[0]
    scale = 1.0 / math.sqrt(HEAD_DIM)
    n_steps = s // tm
    c_ins = list(comm["ins"]) if comm else []
    c_outs = list(comm["outs"]) if comm else []
    c_sems = list(comm["sems"]) if comm else []
    n_fixed = 3 * N_GROUPS + 6

    def body(*refs):
        g_refs = refs[:3 * N_GROUPS]
        du_ref, dga_ref, dgs_ref, cs_ref, lo_ref, hi_ref = refs[3 * N_GROUPS:n_fixed]
        comm_in = refs[n_fixed:n_fixed + len(c_ins)]
        dz_ref = refs[n_fixed + len(c_ins)]
        comm_out = refs[n_fixed + len(c_ins) + 1:n_fixed + len(c_ins) + 1 + len(c_outs)]
        scr = refs[n_fixed + len(c_ins) + 1 + len(c_outs)]
        sems = refs[n_fixed + len(c_ins) + 2 + len(c_outs):]
        if comm:
            @pl.when(pl.program_id(0) == 0)
            def _():
                comm["start"](comm_in, comm_out, sems)

            @pl.when(pl.program_id(0) == n_steps - 1)
            def _():
                comm["finish"](comm_in, comm_out, sems)

        tabs_ = (cs_ref[...], lo_ref[...], hi_ref[...])
        for part in range(3):
            for g, dil in enumerate(GROUP_DILATIONS):
                src = g_refs[3 * g + part]
                for h in range(HEADS_PER_GROUP):
                    if dil == 1:
                        t = src[:, h * HEAD_DIM:(h + 1) * HEAD_DIM].astype(F32)
                    else:
                        t = _collect(scr, h, src, dil, h * HEAD_DIM)
                    if part < 2:
                        t = _rope_t(t, *tabs_)
                    if part == 0:
                        t = t * scale
                    col = part * N_GROUPS * GROUP_WIDTH + g * GROUP_WIDTH + h * HEAD_DIM
                    dz_ref[:, col:col + HEAD_DIM] = t.astype(BF16)
        dz_ref[:, COL_U:COL_GA] = du_ref[...]
        dz_ref[:, COL_GA:COL_GS] = dga_ref[...]
        dz_ref[:, COL_GS:IN_WIDTH] = dgs_ref[...]

    row = lambda i: (i, 0)
    in_specs, args = [], []
    for (g, dil), trio in zip(enumerate(GROUP_DILATIONS), grads):
        in_specs += [pl.BlockSpec((tm, GROUP_WIDTH), row) if dil == 1 else _view_spec(tm, dil)] * 3
        args += list(trio)
    in_specs += [pl.BlockSpec((tm, SSM_WIDTH), row), pl.BlockSpec((tm, D_MODEL), row), pl.BlockSpec((tm, D_MODEL), row)]
    in_specs += [pl.BlockSpec((tm, HEAD_DIM), row)] * 3
    in_specs += [pl.BlockSpec(memory_space=pl.ANY)] * len(c_ins)
    res = pl.pallas_call(
        body, name="dz_layout", grid=(n_steps,), in_specs=in_specs,
        out_specs=[pl.BlockSpec((tm, IN_WIDTH), row)] + [pl.BlockSpec(memory_space=pl.ANY)] * len(c_outs),
        out_shape=[jax.ShapeDtypeStruct((s, IN_WIDTH), BF16)] + c_outs,
        scratch_shapes=[pltpu.VMEM((HEADS_PER_GROUP, tm, HEAD_DIM), F32)] + [pltpu.SemaphoreType.DMA((n,)) for n in c_sems],
        compiler_params=_params(1),
    )(*args, du, dga, dgs, *tabs, *c_ins)
    return res[0], list(res[1:])


def _discretise(a_re, a_im, log_dt, bt_re, bt_im):
    dt = jnp.exp(log_dt)
    mag = jnp.exp(a_re * dt)
    bar_re = mag * jnp.cos(a_im * dt)
    bar_im = mag * jnp.sin(a_im * dt)
    nr = bar_re - 1.0
    ni = bar_im
    den = a_re * a_re + a_im * a_im
    z_re = (nr * a_re + ni * a_im) / den
    z_im = (ni * a_re - nr * a_im) / den
    bb_re = z_re[:, None, :] * bt_re - z_im[:, None, :] * bt_im
    bb_im = z_re[:, None, :] * bt_im + z_im[:, None, :] * bt_re
    return bar_re, bar_im, bb_re, bb_im


def _ssm_prep(a_re, a_im, log_dt, bt_re, bt_im):
    def body(ar, ai, ld, br, bi, o_lr, o_li, o_br, o_bi):
        lr, li, bbr, bbi = _discretise(ar[...], ai[...], ld[...], br[...], bi[...])
        o_lr[...] = lr
        o_li[...] = li
        o_br[...] = bbr
        o_bi[...] = bbi

    sm = jax.ShapeDtypeStruct((SSM_GROUPS, SSM_STATE), F32)
    bg = jax.ShapeDtypeStruct((SSM_GROUPS, SSM_GROUP, SSM_STATE), F32)
    return pl.pallas_call(body, name="ssm_prep", out_shape=[sm, sm, bg, bg])(a_re, a_im, log_dt, bt_re, bt_im)


def _ssm_param_bwd(a_re, a_im, log_dt, bt_re, bt_im, d_lr, d_li, d_bbr, d_bbi):
    def body(ar, ai, ld, br, bi, g_lr, g_li, g_br, g_bi, o_ar, o_ai, o_ld, o_br, o_bi):
        _, vjp = jax.vjp(_discretise, ar[...], ai[...], ld[...], br[...], bi[...])
        d_ar, d_ai, d_ld, d_br, d_bi = vjp((g_lr[...], g_li[...], g_br[...], g_bi[...]))
        o_ar[...] = d_ar
        o_ai[...] = d_ai
        o_ld[...] = d_ld
        o_br[...] = d_br
        o_bi[...] = d_bi

    sm = jax.ShapeDtypeStruct((SSM_GROUPS, SSM_STATE), F32)
    col = jax.ShapeDtypeStruct((SSM_GROUPS, 1), F32)
    bg = jax.ShapeDtypeStruct((SSM_GROUPS, SSM_GROUP, SSM_STATE), F32)
    return pl.pallas_call(body, name="ssm_param_bwd", out_shape=[sm, sm, col, bg, bg])(
        a_re, a_im, log_dt, bt_re, bt_im, d_lr, d_li, d_bbr, d_bbi)


def _block_diag(t, rows_per, cols_per):
    t4 = t.reshape(SSM_SUPER, 8, rows_per, cols_per)
    eye = jnp.eye(8, dtype=t.dtype)
    return jnp.einsum("bgrc,gh->bgrhc", t4, eye).reshape(SSM_SUPER, 8 * rows_per, 8 * cols_per)


def _block_diag_t(dense, rows_per, cols_per):
    t = dense.reshape(SSM_SUPER, 8, rows_per, 8, cols_per)
    eye = jnp.eye(8, dtype=dense.dtype)
    return jnp.einsum("bgrhc,gh->bgrc", t, eye).reshape(SSM_GROUPS, rows_per, cols_per)


def _gelu(v):
    c = math.sqrt(2.0 / math.pi)
    return 0.5 * v * (1.0 + jnp.tanh(c * (v + 0.044715 * v * v * v)))


def _gelu_grad(v):
    c = math.sqrt(2.0 / math.pi)
    t = jnp.tanh(c * (v + 0.044715 * v * v * v))
    return 0.5 * (1.0 + t) + 0.5 * v * (1.0 - t * t) * c * (1.0 + 3.0 * 0.044715 * v * v)


SUB = 8


SCAN_STEPS = (1, 2, 4)
N_SCAN_TABLES = 2 + 2 * len(SCAN_STEPS)


def _scan_tables(tab_ref, lam_re, lam_im, reverse, conj):
    lr = lam_re
    li = -lam_im if conj else lam_im
    powers = [(lr, li)]
    for _ in range(SUB - 1):
        pr, pi = powers[-1]
        powers.append((pr * lr - pi * li, pr * li + pi * lr))
    row = lax.broadcasted_iota(jnp.int32, (SUB, N_STATE), 0)
    if reverse:
        row = SUB - 1 - row
    wide = lambda v: jnp.broadcast_to(v, (SUB, N_STATE))
    p_re = jnp.zeros((SUB, N_STATE), F32)
    p_im = jnp.zeros((SUB, N_STATE), F32)
    for j in range(SUB):
        p_re = jnp.where(row == j, wide(powers[j][0]), p_re)
        p_im = jnp.where(row == j, wide(powers[j][1]), p_im)
    tab_ref[0] = p_re
    tab_ref[1] = p_im
    for idx, k in enumerate(SCAN_STEPS):
        tab_ref[2 + 2 * idx] = jnp.where(row >= k, wide(powers[k - 1][0]), 0.0)
        tab_ref[3 + 2 * idx] = jnp.where(row >= k, wide(powers[k - 1][1]), 0.0)


def _scan_rows(g_re_ref, g_im_ref, tab_ref, carry, n_rows, reverse):
    last = 0 if reverse else SUB - 1

    def tile_step(tt, state):
        cr, ci = state
        t8 = (n_rows // SUB - 1 - tt) if reverse else tt
        start = pl.multiple_of(t8 * SUB, SUB)
        xr = g_re_ref[pl.ds(start, SUB), :]
        xi = g_im_ref[pl.ds(start, SUB), :]
        for idx, k in enumerate(SCAN_STEPS):
            mr = tab_ref[2 + 2 * idx]
            mi = tab_ref[3 + 2 * idx]
            shift = SUB - k if reverse else k
            sr = pltpu.roll(xr, shift, 0)
            si = pltpu.roll(xi, shift, 0)
            xr, xi = xr + (mr * sr - mi * si), xi + (mr * si + mi * sr)
        pr = tab_ref[0]
        pi = tab_ref[1]
        xr, xi = xr + (pr * cr - pi * ci), xi + (pr * ci + pi * cr)
        g_re_ref[pl.ds(start, SUB), :] = xr
        g_im_ref[pl.ds(start, SUB), :] = xi
        return (jnp.broadcast_to(xr[last:last + 1, :], (SUB, N_STATE)),
                jnp.broadcast_to(xi[last:last + 1, :], (SUB, N_STATE)))

    return lax.fori_loop(0, n_rows // SUB, tile_step, carry)


def _ssm_fwd(z, b_re, b_im, c_re, c_im, lam_re, lam_im, d_skip, chunk):
    s = z.shape[0]

    def body(u_ref, bre, bim, cre, cim, lre, lim, dsk, hre_ref, him_ref, ys_ref, yg_ref, car_re, car_im, tabs):
        i = pl.program_id(0)

        @pl.when(i == 0)
        def _():
            car_re[...] = jnp.zeros_like(car_re)
            car_im[...] = jnp.zeros_like(car_im)
            _scan_tables(tabs, lre[...], lim[...], False, False)

        u = u_ref[...]
        for b in range(SSM_SUPER):
            ub = u[:, b * 128:(b + 1) * 128]
            st = slice(b * 512, (b + 1) * 512)
            hre_ref[:, st] = _dot(ub, bre[b], "nn")
            him_ref[:, st] = _dot(ub, bim[b], "nn")
        sr, si = _scan_rows(hre_ref, him_ref, tabs, (car_re[...], car_im[...]), chunk, False)
        car_re[...] = sr
        car_im[...] = si
        uf = u.astype(F32)
        for b in range(SSM_SUPER):
            st = slice(b * 512, (b + 1) * 512)
            ch = slice(b * 128, (b + 1) * 128)
            y = _dot(hre_ref[:, st].astype(BF16), cre[b], "nn") - _dot(him_ref[:, st].astype(BF16), cim[b], "nn")
            y = y + dsk[:, ch] * uf[:, ch]
            ys_ref[:, ch] = y
            yg_ref[:, ch] = _gelu(y).astype(BF16)

    full3 = lambda i: (0, 0, 0)
    full2 = lambda i: (0, 0)
    row = lambda i: (i, 0)
    u_col = COL_U // SSM_WIDTH
    return pl.pallas_call(
        body, name="ssm_fwd", grid=(s // chunk,),
        in_specs=[pl.BlockSpec((chunk, SSM_WIDTH), lambda i: (i, u_col)),
                  pl.BlockSpec((SSM_SUPER, 128, 512), full3), pl.BlockSpec((SSM_SUPER, 128, 512), full3),
                  pl.BlockSpec((SSM_SUPER, 512, 128), full3), pl.BlockSpec((SSM_SUPER, 512, 128), full3),
                  pl.BlockSpec((1, N_STATE), full2), pl.BlockSpec((1, N_STATE), full2), pl.BlockSpec((1, SSM_WIDTH), full2)],
        out_specs=[pl.BlockSpec((chunk, N_STATE), row), pl.BlockSpec((chunk, N_STATE), row),
                   pl.BlockSpec((chunk, SSM_WIDTH), row), pl.BlockSpec((chunk, SSM_WIDTH), row)],
        out_shape=[jax.ShapeDtypeStruct((s, N_STATE), F32), jax.ShapeDtypeStruct((s, N_STATE), F32),
                   jax.ShapeDtypeStruct((s, SSM_WIDTH), F32), jax.ShapeDtypeStruct((s, SSM_WIDTH), BF16)],
        scratch_shapes=[pltpu.VMEM((SUB, N_STATE), F32), pltpu.VMEM((SUB, N_STATE), F32),
                        pltpu.VMEM((N_SCAN_TABLES, SUB, N_STATE), F32)],
        compiler_params=_params(1),
    )(z, b_re, b_im, c_re, c_im, lam_re, lam_im, d_skip)


def _ssm_bwd(dys, z, h_re, h_im, b_re, b_im, c_re, c_im, lam_re, lam_im, d_skip, chunk):
    s = z.shape[0]
    n_chunks = s // chunk

    def body(dy_ref, u_ref, hre_ref, him_ref, hpr_ref, hpi_ref, bre, bim, cre, cim, lre, lim, dsk,
             du_ref, dlr_ref, dli_ref, dbr_ref, dbi_ref, dcr_ref, dci_ref, dd_ref, are, aim, car_re, car_im, tabs):
        i = pl.program_id(0)
        n = n_chunks - 1 - i

        @pl.when(i == 0)
        def _():
            car_re[...] = jnp.zeros_like(car_re)
            car_im[...] = jnp.zeros_like(car_im)
            _scan_tables(tabs, lre[...], lim[...], True, True)
            for r in (dlr_ref, dli_ref, dbr_ref, dbi_ref, dcr_ref, dci_ref, dd_ref):
                r[...] = jnp.zeros_like(r)

        dy = dy_ref[...]
        dyb = dy.astype(BF16)
        u = u_ref[...]
        for b in range(SSM_SUPER):
            ch = slice(b * 128, (b + 1) * 128)
            st = slice(b * 512, (b + 1) * 512)
            are[:, st] = _dot(dyb[:, ch], cre[b], "nt")
            aim[:, st] = -_dot(dyb[:, ch], cim[b], "nt")
        sr, si = _scan_rows(are, aim, tabs, (car_re[...], car_im[...]), chunk, True)
        car_re[...] = sr
        car_im[...] = si
        row_id = lax.broadcasted_iota(jnp.int32, (chunk, N_STATE), 0)
        top_scale = jnp.where(n > 0, 1.0, 0.0)
        h_r = hre_ref[...]
        h_i = him_ref[...]
        hp_r = jnp.where(row_id == 0, hpr_ref[SUB - 1:SUB, :] * top_scale, pltpu.roll(h_r, 1, 0))
        hp_i = jnp.where(row_id == 0, hpi_ref[SUB - 1:SUB, :] * top_scale, pltpu.roll(h_i, 1, 0))
        a_r = are[...]
        a_i = aim[...]
        dlr_ref[...] += jnp.sum(a_r * hp_r + a_i * hp_i, axis=0, keepdims=True)
        dli_ref[...] += jnp.sum(a_i * hp_r - a_r * hp_i, axis=0, keepdims=True)
        dd_ref[...] += jnp.sum(dy * u.astype(F32), axis=0, keepdims=True)
        a_rb = a_r.astype(BF16)
        a_ib = a_i.astype(BF16)
        h_rb = h_r.astype(BF16)
        h_ib = h_i.astype(BF16)
        for b in range(SSM_SUPER):
            ch = slice(b * 128, (b + 1) * 128)
            st = slice(b * 512, (b + 1) * 512)
            dbr_ref[b] += _dot(u[:, ch], a_rb[:, st], "tn")
            dbi_ref[b] += _dot(u[:, ch], a_ib[:, st], "tn")
            dcr_ref[b] += _dot(h_rb[:, st], dyb[:, ch], "tn")
            dci_ref[b] += -_dot(h_ib[:, st], dyb[:, ch], "tn")
            du = _dot(a_rb[:, st], bre[b], "nt") + _dot(a_ib[:, st], bim[b], "nt") + dsk[:, ch] * dy[:, ch]
            du_ref[:, ch] = du.astype(du_ref.dtype)

    full3 = lambda i: (0, 0, 0)
    full2 = lambda i: (0, 0)
    rev = lambda i: (n_chunks - 1 - i, 0)
    above = lambda i: (jnp.maximum((n_chunks - 1 - i) * (chunk // SUB) - 1, 0), 0)
    u_col = COL_U // SSM_WIDTH
    b_spec = pl.BlockSpec((SSM_SUPER, 128, 512), full3)
    c_spec = pl.BlockSpec((SSM_SUPER, 512, 128), full3)
    vec = pl.BlockSpec((1, N_STATE), full2)
    return pl.pallas_call(
        body, name="ssm_bwd", grid=(n_chunks,),
        in_specs=[pl.BlockSpec((chunk, SSM_WIDTH), rev),
                  pl.BlockSpec((chunk, SSM_WIDTH), lambda i: (n_chunks - 1 - i, u_col)),
                  pl.BlockSpec((chunk, N_STATE), rev), pl.BlockSpec((chunk, N_STATE), rev),
                  pl.BlockSpec((SUB, N_STATE), above), pl.BlockSpec((SUB, N_STATE), above),
                  b_spec, b_spec, c_spec, c_spec, vec, vec, pl.BlockSpec((1, SSM_WIDTH), full2)],
        out_specs=[pl.BlockSpec((chunk, SSM_WIDTH), rev), vec, vec, b_spec, b_spec, c_spec, c_spec,
                   pl.BlockSpec((1, SSM_WIDTH), full2)],
        out_shape=[jax.ShapeDtypeStruct((s, SSM_WIDTH), BF16),
                   jax.ShapeDtypeStruct((1, N_STATE), F32), jax.ShapeDtypeStruct((1, N_STATE), F32),
                   jax.ShapeDtypeStruct((SSM_SUPER, 128, 512), F32), jax.ShapeDtypeStruct((SSM_SUPER, 128, 512), F32),
                   jax.ShapeDtypeStruct((SSM_SUPER, 512, 128), F32), jax.ShapeDtypeStruct((SSM_SUPER, 512, 128), F32),
                   jax.ShapeDtypeStruct((1, SSM_WIDTH), F32)],
        scratch_shapes=[pltpu.VMEM((chunk, N_STATE), F32), pltpu.VMEM((chunk, N_STATE), F32),
                        pltpu.VMEM((SUB, N_STATE), F32), pltpu.VMEM((SUB, N_STATE), F32),
                        pltpu.VMEM((N_SCAN_TABLES, SUB, N_STATE), F32)],
        compiler_params=_params(1),
    )(dys, z, h_re, h_im, h_re, h_im, b_re, b_im, c_re, c_im, lam_re, lam_im, d_skip)


def _local_step(x, p, pos, tgt, sm, w_in_own, w_in_buf, late_bufs, place):
    s = x.shape[0]
    tm = min(512, s)
    ts = min(2048, s)
    chunk = min(256, s)
    ni = s // tm
    nk = s // ts
    g_mix, g_ffn, g_final = sm["g_mix"], sm["g_ffn"], sm["g_final"]

    half_in = IN_WIDTH // 8
    tmb = min(1024, s)
    nib = s // tmb
    chip_w = IN_WIDTH // N_CHIPS
    w_start, w_forward, w_finish = _gather_stages(1)
    gather_in = dict(ins=[w_in_buf], outs=[jax.ShapeDtypeStruct(w_in_buf.shape, w_in_buf.dtype)], aliased=True,
                     sems=[3] * 4, stages=[(0, w_start), (nib - 1, w_forward), (nib - 1, w_finish)])
    a_rows = lambda i, j, k, pv: (i, 0)
    z_own, n1, w_in_all = _mm("in_proj_own", (nib, 1, 1),
                              [(x, (tmb, D_MODEL), a_rows, w_in_own, (D_MODEL, chip_w), lambda i, j, k, pv: (0, 0))], "nn",
                              [((s, IN_WIDTH), BF16, (tmb, chip_w), lambda i, j, k, pv: (i, pv[P_CHIP])),
                               ((s, D_MODEL), BF16, (tmb, D_MODEL), a_rows)],
                              extras=[(g_mix, (1, D_MODEL), lambda i, j, k, pv: (0, 0))],
                              epilogue=lambda acc, g: ((acc,), ()), prologue=_rms_fwd_tile, comm=gather_in, place=place)
    w_in = w_in_all.reshape(N_CHIPS, D_MODEL, chip_w)
    n_late = len(late_bufs)
    g_start, g_forward, g_finish = _gather_stages(n_late)
    in_steps = (N_CHIPS - 1) * nib
    gather = dict(ins=late_bufs, outs=[jax.ShapeDtypeStruct(b.shape, b.dtype) for b in late_bufs], aliased=True,
                  sems=[3 * n_late] * 4,
                  stages=[(0, g_start), ((4 * in_steps) // 5, g_forward), (in_steps - 1, g_finish)])
    other = lambda j, pv: (pv[P_CHIP] + 1 + j) % N_CHIPS
    z, *late = _mm("in_proj", (nib, N_CHIPS - 1, 1),
                   [(n1, (tmb, D_MODEL), a_rows, w_in, (None, D_MODEL, chip_w), lambda i, j, k, pv: (other(j, pv), 0, 0))],
                   "nn", [((s, IN_WIDTH), BF16, (tmb, chip_w), lambda i, j, k, pv: (i, other(j, pv)))], j_outer=True,
                   comm=gather, place=place, fill=z_own)
    w_ap, w_ga, w_gb, w_out, w_fg, w_fu, w_fd, w_pg, w_pp = (
        g.reshape(N_CHIPS, 2 * g.shape[2], g.shape[3]) for g in late)
    w_out2 = w_out.reshape(D_MODEL, D_MODEL)
    w_pg2 = w_pg.reshape(D_MODEL, D_MODEL)

    inv = ROPE_THETA ** (-jnp.arange(ROPE_HALF, dtype=F32) * 2.0 / ROPE_DIM)
    inv_row = jnp.concatenate([inv, inv, jnp.zeros((HEAD_DIM - ROPE_DIM,), F32)]).reshape(1, HEAD_DIM)
    tabs = _rope_tables(pos.astype(F32).reshape(s, 1), inv_row, tm)

    qk0, qkv_views = _qkv_layout(z, tabs, tm)
    v0_col = (2 * N_GROUPS * GROUP_WIDTH) // GROUP_WIDTH
    group_in = [((qk0, qk0, z), (0, 1, v0_col))] + [(trio, (0, 0, 0)) for trio in qkv_views]
    fwd_out = [_attn_fwd(*arrs, dil, cols3) for (arrs, cols3), dil in zip(group_in, GROUP_DILATIONS)]
    attn, lse, merged_views = _attn_merge(fwd_out[0][0], fwd_out[0][1], fwd_out[1:], tm)

    def chip_cols(parts):
        return (jnp.concatenate(parts, axis=1),), ()

    def proj_cols(name, a, width, w):
        blk = (None, width, 256)
        pairs = [(a, (tmb, width), lambda i, j, k: (i, 0), w, blk, lambda i, j, k: (0, 0, 0))]
        pairs += [(None, None, None, w, blk, (lambda i, j, k, q=q: (q, 0, 0))) for q in range(1, N_CHIPS)]
        return _mm(name, (nib, 1, 1), pairs, "nn", [((s, D_MODEL), BF16, (tmb, D_MODEL), lambda i, j, k: (i, 0))],
                   epilogue=chip_cols, sum_pairs=False)[0]

    def proj512(name, a, w):
        return proj_cols(name, a, GROUP_WIDTH, w)

    attn_d = proj512("attn_proj", attn, w_ap)

    bt_re = jnp.transpose(sm["b_re"], (0, 2, 1))
    bt_im = jnp.transpose(sm["b_im"], (0, 2, 1))
    log_dt_col = sm["log_dt"].reshape(SSM_GROUPS, 1)
    lam_re, lam_im, bbt_re, bbt_im = _ssm_prep(sm["a_re"], sm["a_im"], log_dt_col, bt_re, bt_im)
    b_re_m = _block_diag(bbt_re, SSM_GROUP, SSM_STATE).astype(BF16)
    b_im_m = _block_diag(bbt_im, SSM_GROUP, SSM_STATE).astype(BF16)
    c_re_m = _block_diag(jnp.transpose(sm["c_re"], (0, 2, 1)), SSM_STATE, SSM_GROUP).astype(BF16)
    c_im_m = _block_diag(jnp.transpose(sm["c_im"], (0, 2, 1)), SSM_STATE, SSM_GROUP).astype(BF16)
    lam_re_row = lam_re.reshape(1, N_STATE)
    lam_im_row = lam_im.reshape(1, N_STATE)
    d_skip_row = sm["d_skip"].reshape(1, SSM_WIDTH)
    h_re, h_im, ys, yg = _ssm_fwd(z, b_re_m, b_im_m, c_re_m, c_im_m, lam_re_row, lam_im_row, d_skip_row, chunk)

    pa = proj512("glu_a", yg, w_ga)
    pb = proj512("glu_b", yg, w_gb)

    def mix_pro(ad, xr, g, ga, gs, a, b):
        ga, gs, ad, a, b = (t.astype(F32) for t in (ga, gs, ad, a, b))
        return _sig(ga) * ad + _sig(gs) * (a * _sig(b))

    def out_epi(acc, xr, g, *_):
        h1 = acc + xr
        return (h1, _rms_fwd_tile(h1, g)), ()

    m3 = lambda i, j, k: (i, 0)
    w3 = lambda i, j, k: (0, 0)
    tile_d = (tm, D_MODEL)
    h1, n2, mix = _mm("out_proj", (ni, 1, 1), [(attn_d, tile_d, m3, w_out2, (D_MODEL, D_MODEL), w3)], "nn",
                      [((s, D_MODEL), F32, tile_d, m3), ((s, D_MODEL), BF16, tile_d, m3), ((s, D_MODEL), BF16, tile_d, m3)],
                      epilogue=out_epi, prologue=mix_pro,
                      extras=[(x, tile_d, m3), (g_ffn, (1, D_MODEL), w3),
                              (z, tile_d, lambda i, j, k: (i, COL_GA // D_MODEL)), (z, tile_d, lambda i, j, k: (i, COL_GS // D_MODEL)),
                              (pa, tile_d, m3), (pb, tile_d, m3)])

    ffq = (None, tm, D_FF_Q)
    ffq_map = lambda i, j, k: (j, i, 0)

    def ffn_in_epi(parts):
        gts, ups = parts[0::2], parts[1::2]
        acts = [gt * _sig(gt) * u_ for gt, u_ in zip(gts, ups)]
        return (jnp.stack(gts, axis=0), jnp.stack(ups, axis=0), jnp.stack(acts, axis=0)), ()

    w_ffq = (None, D_MODEL, D_FF_Q)
    ff_pairs = []
    for q in range(N_CHIPS):
        blk_q = lambda i, j, k, q=q: (q, 0, 0)
        ff_pairs.append((n2, (tm, D_MODEL), m3, w_fg, w_ffq, blk_q) if q == 0 else (None, None, None, w_fg, w_ffq, blk_q))
        ff_pairs.append((None, None, None, w_fu, w_ffq, blk_q))
    ff_all = (N_CHIPS, tm, D_FF_Q)
    ff_all_map = lambda i, j, k: (0, i, 0)
    gate, up, act = _mm("ffn_gate_up", (ni, 1, 1), ff_pairs, "nn",
                        [((N_CHIPS, s, D_FF_Q), BF16, ff_all, ff_all_map)] * 3, epilogue=ffn_in_epi,
                        sum_pairs=False, resident_b=True)

    (h2,) = _mm("ffn_down", (nib, 1, 1),
                [(act, (None, tmb, D_FF_Q), (lambda i, j, k, q=q: (q, i, 0)), w_fd, (None, D_FF_Q, D_MODEL),
                  (lambda i, j, k, q=q: (q, 0, 0))) for q in range(N_CHIPS)], "nn",
                [((s, D_MODEL), F32, (tmb, D_MODEL), m3)], epilogue=lambda acc, hr: ((acc + hr,), ()),
                extras=[(h1, (tmb, D_MODEL), m3)])

    pp = proj_cols("ple_proj", p, PLE_DIM, w_pp)

    def ple_head_epi(acc, hr, ppr, t, g):
        sg = _sig(acc)
        ppf = ppr.astype(F32)
        h = hr + sg * ppf
        r = lax.rsqrt(jnp.mean(h * h, axis=-1, keepdims=True) + EPS)
        hhat = h * r
        diff = hhat * g - t
        loss = 0.5 * jnp.sum(jnp.mean(diff * diff, axis=-1, keepdims=True))
        dy = diff * (1.0 / D_MODEL)
        gy = dy * g
        dh = r * (gy - hhat * jnp.mean(gy * hhat, axis=-1, keepdims=True))
        return ((dh, dh * ppf * sg * (1.0 - sg), dh * sg),
                (jnp.full((SUB, 128), loss, F32), jnp.sum(dy * hhat, axis=0, keepdims=True)))

    tile_row = (tm, D_MODEL)
    dh3, dgl, dpp, loss_acc, dg_final = _mm(
        "ple_gate_head", (ni, 1, 1), [(h2, tile_row, m3, w_pg2, (D_MODEL, D_MODEL), w3)], "nn",
        [((s, D_MODEL), F32, tile_row, m3), ((s, D_MODEL), BF16, tile_row, m3), ((s, D_MODEL), BF16, tile_row, m3)],
        epilogue=ple_head_epi,
        extras=[(h2, tile_row, m3), (pp, tile_row, m3), (tgt, tile_row, m3), (g_final, (1, D_MODEL), w3)],
        acc_outs=[((SUB, 128), F32), ((1, D_MODEL), F32)])

    def wgrad(name, a, a_block, a_imap, b, b_block, b_imap, out_shape, out_block, out_imap, nj, acc_shape):
        return _mm(name, (1, nj, nk), [(a, a_block, a_imap, b, b_block, b_imap)], "tn",
                   [(out_shape, F32, out_block, out_imap)], acc_shape=acc_shape)[0]

    tk0 = lambda i, j, k: (k, 0)
    tkj = lambda i, j, k: (k, j)
    def wgrad_cols(name, a, width, dy_):
        def split(acc):
            return (jnp.stack([acc[:, q * 256:(q + 1) * 256] for q in range(N_CHIPS)], axis=0),), ()

        return _mm(name, (1, 1, nk), [(a, (ts, width), tk0, dy_, (ts, D_MODEL), tk0)], "tn",
                   [((N_CHIPS, width, 256), F32, (N_CHIPS, width, 256), lambda i, j, k: (0, 0, 0))], epilogue=split,
                   acc_shape=(width, D_MODEL))[0]

    d_w_pp = wgrad_cols("d_ple_proj", p, PLE_DIM, dpp)
    d_w_pg = wgrad("d_ple_gate", h2, (ts, D_MODEL), tk0, dgl, (ts, D_MODEL), tk0, (D_MODEL, D_MODEL),
                   (D_MODEL, D_MODEL), w3, 1, (D_MODEL, D_MODEL))

    (dh2,) = _mm("ple_gate_bwd", (nib, 1, 1), [(dgl, (tmb, D_MODEL), m3, w_pg2, (D_MODEL, D_MODEL), w3)], "nt",
                 [((s, D_MODEL), F32, (tmb, D_MODEL), m3)], epilogue=lambda acc, d_: ((acc + d_,), ()),
                 extras=[(dh3, (tmb, D_MODEL), m3)])

    def ffn_bwd_epi(parts, gt_all, u_all):
        dgs_, dus_ = [], []
        for q, dact in enumerate(parts):
            gt, u_ = gt_all[q].astype(F32), u_all[q].astype(F32)
            sg = _sig(gt)
            dgs_.append(dact * u_ * (sg * (1.0 + gt * (1.0 - sg))))
            dus_.append(dact * gt * sg)
        return (jnp.stack(dgs_, axis=0), jnp.stack(dus_, axis=0)), ()

    fd_pairs = [((dh2, (tm, D_MODEL), m3) if q == 0 else (None, None, None))
                + (w_fd, (None, D_FF_Q, D_MODEL), (lambda i, j, k, q=q: (q, 0, 0))) for q in range(N_CHIPS)]
    dgate, dup = _mm("ffn_down_bwd", (ni, 1, 1), fd_pairs, "nt",
                     [((N_CHIPS, s, D_FF_Q), BF16, ff_all, ff_all_map)] * 2, epilogue=ffn_bwd_epi,
                     extras=[(gate, ff_all, ff_all_map), (up, ff_all, ff_all_map)], sum_pairs=False, resident_b=True)

    ffq_t = (None, ts, D_FF_Q)
    ffq_tmap = lambda i, j, k: (j, k, 0)
    blk_j = lambda i, j, k: (j, 0, 0)
    d_w_fd = wgrad("d_ffn_down", act, ffq_t, ffq_tmap, dh2, (ts, D_MODEL), tk0, (N_CHIPS, D_FF_Q, D_MODEL),
                   (None, D_FF_Q, D_MODEL), blk_j, N_CHIPS, (D_FF_Q, D_MODEL))
    d_w_fg = wgrad("d_ffn_gate", n2, (ts, D_MODEL), tk0, dgate, ffq_t, ffq_tmap, (N_CHIPS, D_MODEL, D_FF_Q),
                   (None, D_MODEL, D_FF_Q), blk_j, N_CHIPS, (D_MODEL, D_FF_Q))
    d_w_fu = wgrad("d_ffn_up", n2, (ts, D_MODEL), tk0, dup, ffq_t, ffq_tmap, (N_CHIPS, D_MODEL, D_FF_Q),
                   (None, D_MODEL, D_FF_Q), blk_j, N_CHIPS, (D_MODEL, D_FF_Q))

    def norm_bwd_epi(acc, h, d_res, g):
        dh, dg = _rms_bwd_tile(acc, h, g)
        return (d_res + dh,), (dg,)

    fi_pairs = []
    for q in range(N_CHIPS):
        a_q = lambda i, j, k, q=q: (q, i, 0)
        b_q = lambda i, j, k, q=q: (q, 0, 0)
        fi_pairs.append((dgate, ffq, a_q, w_fg, (None, D_MODEL, D_FF_Q), b_q))
        fi_pairs.append((dup, ffq, a_q, w_fu, (None, D_MODEL, D_FF_Q), b_q))
    dh1, dg_ffn = _mm("ffn_in_bwd", (ni, 1, 1), fi_pairs, "nt",
                      [((s, D_MODEL), F32, (tm, D_MODEL), m3)], epilogue=norm_bwd_epi,
                      extras=[(h1, (tm, D_MODEL), m3), (dh2, (tm, D_MODEL), m3), (g_ffn, (1, D_MODEL), w3)],
                      acc_outs=[((1, D_MODEL), F32)], resident_b=True)

    d_w_out = wgrad("d_out_proj", mix, (ts, D_MODEL), tk0, dh1, (ts, D_MODEL), tk0, (D_MODEL, D_MODEL),
                    (D_MODEL, D_MODEL), w3, 1, (D_MODEL, D_MODEL))

    def mix_bwd_epi(dm, ga, gs, ad, a, b):
        ga, gs, ad, a, b = (t.astype(F32) for t in (ga, gs, ad, a, b))
        s_a, s_s, s_b = _sig(ga), _sig(gs), _sig(b)
        d_ssm = dm * s_s
        return (dm * ad * s_a * (1.0 - s_a), dm * (a * s_b) * s_s * (1.0 - s_s), dm * s_a, d_ssm * s_b,
                d_ssm * a * s_b * (1.0 - s_b)), ()

    tile_m = (tm, D_MODEL)
    dga, dgs, dattn_d, dpa, dpb = _mm(
        "out_proj_bwd", (ni, 1, 1), [(dh1, tile_m, m3, w_out2, (D_MODEL, D_MODEL), w3)], "nt",
        [((s, D_MODEL), BF16, tile_m, m3)] * 5, epilogue=mix_bwd_epi,
        extras=[(z, tile_m, lambda i, j, k: (i, COL_GA // D_MODEL)), (z, tile_m, lambda i, j, k: (i, COL_GS // D_MODEL)),
                (attn_d, tile_m, m3), (pa, tile_m, m3), (pb, tile_m, m3)])

    d_w_ap = wgrad_cols("d_attn_proj", attn, GROUP_WIDTH, dattn_d)
    d_w_ga = wgrad_cols("d_glu_a", yg, GROUP_WIDTH, dpa)
    d_w_gb = wgrad_cols("d_glu_b", yg, GROUP_WIDTH, dpb)

    ik = lambda i, j, k: (i, k)

    def cols_bwd(dy_, w):
        return [(dy_, (tmb, 256), (lambda i, j, k, q=q: (i, q)), w, (None, GROUP_WIDTH, 256),
                 (lambda i, j, k, q=q: (q, 0, 0))) for q in range(N_CHIPS)]

    (dattn,) = _mm("attn_proj_bwd", (nib, 1, 1), cols_bwd(dattn_d, w_ap), "nt",
                   [((s, GROUP_WIDTH), BF16, (tmb, GROUP_WIDTH), m3)])

    (dys,) = _mm("glu_bwd", (nib, 1, 1), cols_bwd(dpa, w_ga) + cols_bwd(dpb, w_gb), "nt",
                 [((s, GROUP_WIDTH), F32, (tmb, GROUP_WIDTH), m3)],
                 epilogue=lambda acc, y_: ((acc * _gelu_grad(y_),), ()),
                 extras=[(ys, (tmb, GROUP_WIDTH), m3)])

    du, d_lr, d_li, d_bre, d_bim, d_cre, d_cim, d_dskip = _ssm_bwd(
        dys, z, h_re, h_im, b_re_m, b_im_m, c_re_m, c_im_m, lam_re_row, lam_im_row, d_skip_row, chunk)

    dattn_views = _to_views(dattn, tm)
    bwd_in = [(dattn, attn, lse)] + [(dv_, ov_, lv_) for dv_, (ov_, lv_) in zip(dattn_views, merged_views)]
    qkv_grads = [_attn_bwd(*arrs, *dol, dil, cols3)
                 for (arrs, cols3), dol, dil in zip(group_in, bwd_in, GROUP_DILATIONS)]
    early = [d_w_ap, d_w_ga, d_w_gb, d_w_out.reshape(N_CHIPS, D_MODEL // N_CHIPS, D_MODEL), d_w_fg, d_w_fu, d_w_fd,
             d_w_pg.reshape(N_CHIPS, D_MODEL // N_CHIPS, D_MODEL), d_w_pp]
    early5 = [g.reshape(N_CHIPS, 2, g.shape[1] // 2, g.shape[2]) for g in early]
    n_e = len(early5)
    p_start, p_finish = _pair_exchange_stages(n_e)
    dz, early_theirs = _dz_layout(
        qkv_grads, du, dga, dgs, tabs, tm,
        comm=dict(ins=early5, outs=_pair_exchange_shapes(early5), sems=[N_CHIPS * n_e] * 2, start=p_start, finish=p_finish))
    early_parts = [_pair_sum(g, t, place) for g, t in zip(early5, early_theirs)]

    chip_in = IN_WIDTH // N_CHIPS
    ip_pairs = [(dz, (tm, chip_in), (lambda i, j, k, q=q: (i, q)), w_in, (None, D_MODEL, chip_in),
                 (lambda i, j, k, q=q: (q, 0, 0))) for q in range(N_CHIPS)]
    grad_x, dg_mix = _mm("in_proj_bwd", (ni, 1, 1), ip_pairs, "nt",
                         [((s, D_MODEL), F32, (tm, D_MODEL), m3)], epilogue=norm_bwd_epi,
                         extras=[(x, (tm, D_MODEL), m3), (dh1, (tm, D_MODEL), m3), (g_mix, (1, D_MODEL), w3)],
                         acc_outs=[((1, D_MODEL), F32)], resident_b=True)

    d_bbt_re = _block_diag_t(d_bre, SSM_GROUP, SSM_STATE)
    d_bbt_im = _block_diag_t(d_bim, SSM_GROUP, SSM_STATE)
    d_a_re, d_a_im, d_log_dt, d_bt_re, d_bt_im = _ssm_param_bwd(
        sm["a_re"], sm["a_im"], log_dt_col, bt_re, bt_im,
        d_lr.reshape(SSM_GROUPS, SSM_STATE), d_li.reshape(SSM_GROUPS, SSM_STATE), d_bbt_re, d_bbt_im)
    small = {
        "g_mix": dg_mix, "a_re": d_a_re, "a_im": d_a_im, "log_dt": d_log_dt,
        "b_re": jnp.transpose(d_bt_re, (0, 2, 1)), "b_im": jnp.transpose(d_bt_im, (0, 2, 1)),
        "c_re": jnp.transpose(_block_diag_t(d_cre, SSM_STATE, SSM_GROUP), (0, 2, 1)),
        "c_im": jnp.transpose(_block_diag_t(d_cim, SSM_STATE, SSM_GROUP), (0, 2, 1)),
        "d_skip": d_dskip, "g_ffn": dg_ffn, "g_final": dg_final,
    }
    vec = _pack([small[n] for n in SMALL] + [loss_acc[0, 0].reshape(1)])

    x_start, x_finish = _chip_exchange_stages(n_e)
    v_start, v_finish = _all_exchange_stages()

    def both(f_chips, f_vec):
        def stage(ins, outs, sems):
            f_chips(ins[:n_e], outs[:n_e], sems[:2])
            f_vec(ins[n_e:], outs[n_e:], sems[2:])
        return stage

    ts_in = min(2048, s)
    win_steps = 8 * (s // ts_in)
    exchange = dict(ins=early_parts + [vec],
                    outs=[jax.ShapeDtypeStruct(t.shape, t.dtype) for t in early_parts]
                    + [jax.ShapeDtypeStruct((8,) + vec.shape, vec.dtype)],
                    aliased=False, sems=[3 * n_e, 3 * n_e, 7, 7],
                    stages=[(0, both(x_start, v_start)), (win_steps - 1, both(x_finish, v_finish))])
    d_w_in, *got = _mm("d_in_proj", (1, 8, s // ts_in), [(n1, (ts_in, D_MODEL), tk0, dz, (ts_in, half_in), tkj)], "tn",
                       [((N_CHIPS, D_MODEL, IN_WIDTH // N_CHIPS), F32, (None, D_MODEL, half_in),
                         lambda i, j, k: (j // 2, 0, j % 2))], acc_shape=(D_MODEL, half_in), comm=exchange)
    return grad_x, d_w_in, early_parts, got[:n_e], vec, got[n_e]


BIG = ("w_in", "w_attn_proj", "w_glu_a", "w_glu_b", "w_out", "w_ffn_gate", "w_ffn_up", "w_ffn_down", "w_ple_gate",
       "w_ple_proj")
SMALL = ("g_mix", "a_re", "a_im", "log_dt", "b_re", "b_im", "c_re", "c_im", "d_skip", "g_ffn", "g_final")
ANY = pl.BlockSpec(memory_space=pl.ANY)


def _place():
    x, y, c = lax.axis_index("x"), lax.axis_index("y"), lax.axis_index("c")
    chips = [(1 - x, y), (x, 1 - y), (1 - x, 1 - y)]
    return x, y, c, chips


def _remote(src, dst, send_sem, recv_sem, to):
    return pltpu.make_async_remote_copy(src_ref=src, dst_ref=dst, send_sem=send_sem, recv_sem=recv_sem, device_id=to,
                                        device_id_type=MESH)


def _comm_call(name, body, ins, out_shapes, n_sems, aliases=None):
    n_w = len(ins)
    return pl.pallas_call(
        body, name=name, in_specs=[ANY] * n_w, out_specs=[ANY] * len(out_shapes), out_shape=out_shapes,
        scratch_shapes=[pltpu.SemaphoreType.DMA((n,)) for n in n_sems], input_output_aliases=aliases or {},
    )(*ins)


def _gather_stages(n_w):
    def each():
        x, y, c, chips = _place()
        for w in range(n_w):
            for j, (cx, cy) in enumerate(chips):
                yield w, 3 * w + j, 2 * x + y, 2 * cx + cy, (cx, cy, c), (x, y, 1 - c), c

    def start(ins, outs, sems):
        for w, k, me, _, peer, _, c in each():
            mine = outs[w].at[me, c]
            _remote(mine, mine, sems[0].at[k], sems[1].at[k], peer).start()

    def forward(ins, outs, sems):
        for w, k, _, src_chip, peer, sib, c in each():
            landed = outs[w].at[src_chip, c]
            _remote(landed, landed, sems[0].at[k], sems[1].at[k], peer).wait_recv()
            _remote(landed, landed, sems[2].at[k], sems[3].at[k], sib).start()

    def finish(ins, outs, sems):
        for w, k, me, src_chip, peer, sib, c in each():
            other = outs[w].at[src_chip, 1 - c]
            _remote(other, other, sems[2].at[k], sems[3].at[k], sib).wait_recv()
        for w, k, me, src_chip, peer, sib, c in each():
            mine = outs[w].at[me, c]
            _remote(mine, mine, sems[0].at[k], sems[1].at[k], peer).wait_send()
            landed = outs[w].at[src_chip, c]
            _remote(landed, landed, sems[2].at[k], sems[3].at[k], sib).wait_send()

    return start, forward, finish


def _pair_exchange(grads):
    n_w = len(grads)
    start, finish = _pair_exchange_stages(n_w)

    def body(*refs):
        ins, outs, sems = refs[:n_w], refs[n_w:2 * n_w], refs[2 * n_w:]
        start(ins, outs, sems)
        finish(ins, outs, sems)

    return _comm_call("grad_pair_exchange", body, grads, _pair_exchange_shapes(grads), [N_CHIPS * n_w] * 2)


def _pair_exchange_shapes(grads):
    return [jax.ShapeDtypeStruct((N_CHIPS,) + g.shape[2:], g.dtype) for g in grads]


def _pair_exchange_stages(n_w):
    def each():
        x, y, c, _ = _place()
        for w in range(n_w):
            for q in range(N_CHIPS):
                yield w, q, N_CHIPS * w + q, c, (x, y, 1 - c)

    def start(ins, outs, sems):
        for w, q, k, c, sib in each():
            _remote(ins[w].at[q, 1 - c], outs[w].at[q], sems[0].at[k], sems[1].at[k], sib).start()

    def finish(ins, outs, sems):
        for w, q, k, c, sib in each():
            _remote(ins[w].at[q, 1 - c], outs[w].at[q], sems[0].at[k], sems[1].at[k], sib).wait()

    return start, finish


def _chip_exchange(parts):
    n_w = len(parts)

    start, finish = _chip_exchange_stages(n_w)

    def body(*refs):
        ins, outs, sems = refs[:n_w], refs[n_w:2 * n_w], refs[2 * n_w:]
        start(ins, outs, sems)
        finish(ins, outs, sems)

    out_shapes = [jax.ShapeDtypeStruct(t.shape, t.dtype) for t in parts]
    return _comm_call("grad_chip_exchange", body, parts, out_shapes, [3 * n_w, 3 * n_w])


def _chip_exchange_stages(n_w):
    def each():
        x, y, c, chips = _place()
        for w in range(n_w):
            for j, (cx, cy) in enumerate(chips):
                yield w, 3 * w + j, 2 * x + y, 2 * cx + cy, (cx, cy, c)

    def start(ins, outs, sems):
        for w, k, me, peer_chip, peer in each():
            _remote(ins[w].at[peer_chip], outs[w].at[me], sems[0].at[k], sems[1].at[k], peer).start()

    def finish(ins, outs, sems):
        for w, k, me, peer_chip, peer in each():
            got = outs[w].at[peer_chip]
            _remote(got, got, sems[0].at[k], sems[1].at[k], peer).wait_recv()
        for w, k, me, peer_chip, peer in each():
            _remote(ins[w].at[peer_chip], outs[w].at[me], sems[0].at[k], sems[1].at[k], peer).wait_send()

    return start, finish


def _pair_gather(halves):
    n_w = len(halves)

    def body(*refs):
        ins, outs = refs[:n_w], refs[n_w:2 * n_w]
        send, recv = refs[2 * n_w:]
        x, y, c, _ = _place()
        sib = (x, y, 1 - c)
        cps = []
        for w in range(n_w):
            cp = _remote(ins[w], outs[w], send.at[w], recv.at[w], sib)
            cp.start()
            cps.append(cp)
        for cp in cps:
            cp.wait()

    out_shapes = [jax.ShapeDtypeStruct(h.shape, h.dtype) for h in halves]
    return _comm_call("grad_pair_gather", body, halves, out_shapes, [n_w] * 2)


def _all_exchange_stages():
    def each():
        x, y, c, _ = _place()
        for k in range(1, 8):
            px, py, pc = x ^ ((k >> 2) & 1), y ^ ((k >> 1) & 1), c ^ (k & 1)
            yield k - 1, 4 * x + 2 * y + c, 4 * px + 2 * py + pc, (px, py, pc)

    def start(ins, outs, sems):
        for k, me, _, peer in each():
            _remote(ins[0], outs[0].at[me], sems[0].at[k], sems[1].at[k], peer).start()

    def finish(ins, outs, sems):
        for k, me, src, peer in each():
            got = outs[0].at[src]
            _remote(got, got, sems[0].at[k], sems[1].at[k], peer).wait_recv()
        for k, me, src, peer in each():
            _remote(ins[0], outs[0].at[me], sems[0].at[k], sems[1].at[k], peer).wait_send()

    return start, finish


def _row_tile(r):
    for t in (256, 128, 176, 64, 32, 16, 8):
        if r % t == 0:
            return t
    return r


P_C, P_CHIP, P_DEV = 2, 3, 4


def _cast_shard(w2):
    r, c = w2.shape
    t = _row_tile(r)
    blk, imap = _rows(t, c)
    return _ew("cast_own", (r // t,), [(w2, blk, imap)], [((r, c), BF16, blk, imap)], lambda pids, a: ((a,), ()))[0]


def _cast_into_slot(w2, place):
    r, c = w2.shape
    t = _row_tile(r)
    return _ew("cast_shard", (r // t,), [(w2, (t, c), lambda i, pv: (i, 0))],
               [((N_CHIPS, r, c), BF16, (None, t, c), lambda i, pv: (pv[P_CHIP], i, 0))],
               lambda pids, a: ((a,), ()), place=place)[0]


def _pair_sum(mine, theirs, place):
    _, r, c = theirs.shape
    t = _row_tile(r)
    own = ((None, None, t, c), lambda q, i, pv: (q, pv[P_C], i, 0))
    blk = ((None, t, c), lambda q, i, pv: (q, i, 0))
    return _ew("grad_pair_sum", (N_CHIPS, r // t), [(mine, *own), (theirs, *blk)], [((N_CHIPS, r, c), BF16, *blk)],
               lambda pids, a, b: ((a + b,), ()), place=place)[0]


def _chip_sum(own, got, place):
    _, r, c = own.shape
    t = _row_tile(r)
    ins = []
    for q in range(N_CHIPS):
        ins.append((own, (None, t, c), (lambda i, pv, q=q: (q, i, 0))))
        ins.append((got, (None, t, c), (lambda i, pv, q=q: (jnp.where(pv[P_CHIP] == q, (q + 1) % N_CHIPS, q), i, 0))))

    def fn(pids, *tiles):
        me = pids[0][P_CHIP]
        tot = None
        for q in range(N_CHIPS):
            term = jnp.where(me == q, tiles[2 * q], tiles[2 * q + 1]).astype(F32)
            tot = term if tot is None else tot + term
        return (tot,), ()

    return _ew("grad_chip_sum", (r // t,), ins, [((r, c), F32, (t, c), lambda i, pv: (i, 0))], fn, place=place)[0]


def _adamw_tile(w, g, m, v):
    m = ADAM_B1 * m + (1.0 - ADAM_B1) * g
    v = ADAM_B2 * v + (1.0 - ADAM_B2) * (g * g)
    m_hat = m / (1.0 - ADAM_B1 ** ADAM_STEP)
    v_hat = v / (1.0 - ADAM_B2 ** ADAM_STEP)
    delta = -ADAM_LR * (m_hat / (jnp.sqrt(v_hat) + ADAM_EPS) + ADAM_WD * w)
    return delta, m, v


def _adamw(name, g2, w2, m2, v2):
    r, c = w2.shape
    t = _row_tile(r)
    blk, imap = _rows(t, c)

    def fn(pids, g, w, m, v):
        delta, nm, nv = _adamw_tile(w, g, m, v)
        return (g, delta, nm, nv), ()

    return _ew(name, (r // t,), [(a, blk, imap) for a in (g2, w2, m2, v2)], [((r, c), F32, blk, imap)] * 4, fn)


def _adamw_halves(name, mine, theirs, w2, m2, v2, place):
    r, c = w2.shape
    t = _row_tile(r // 2)
    n_t = (r // 2) // t
    half = ((t, c), lambda h, i, pv: (i, 0))
    whole = ((t, c), lambda h, i, pv: (h * n_t + i, 0))

    def fn(pids, ga, gb, w, m, v):
        g = jnp.where(pids[1] == pids[0][P_C], ga, gb)
        delta, nm, nv = _adamw_tile(w, g, m, v)
        return (g, delta, nm, nv), ()

    return _ew(name, (2, n_t), [(mine, *half), (theirs, *half), (w2, *whole), (m2, *whole), (v2, *whole)],
               [((r, c), F32, *whole)] * 4, fn, place=place)


def _device_sum(own, got, place):
    r, c = own.shape
    t = _row_tile(r)
    ins = [(own, (t, c), lambda i, pv: (i, 0))]
    for q in range(8):
        ins.append((got, (None, t, c), (lambda i, pv, q=q: (jnp.where(pv[P_DEV] == q, (q + 1) % 8, q), i, 0))))

    def fn(pids, mine, *parts):
        me = pids[0][P_DEV]
        tot = None
        for q in range(8):
            term = jnp.where(me == q, mine, parts[q])
            tot = term if tot is None else tot + term
        return (tot,), ()

    return _ew("small_device_sum", (r // t,), ins, [((r, c), F32, (t, c), lambda i, pv: (i, 0))], fn, place=place)[0]


def _pack(parts):
    flat = jnp.concatenate([a.reshape(-1) for a in parts])
    pad = (-flat.shape[0]) % (SUB * 128)
    return jnp.pad(flat, (0, pad)).reshape(-1, 128)


def _unpack(mat, shapes):
    flat = mat.reshape(-1)
    out, off = [], 0
    for shp in shapes:
        n = math.prod(shp)
        out.append(flat[off:off + n].reshape(shp))
        off += n
    return out


def kernel(x, p, positions, g_mix, w_in, a_re, a_im, log_dt, b_re, b_im, c_re, c_im, d_skip, w_attn_proj, w_glu_a, w_glu_b, w_out, g_ffn, w_ffn_gate, w_ffn_up, w_ffn_down, w_ple_gate, w_ple_proj, g_final, loss_target, m_g_mix, m_w_in, m_a_re, m_a_im, m_log_dt, m_b_re, m_b_im, m_c_re, m_c_im, m_d_skip, m_w_attn_proj, m_w_glu_a, m_w_glu_b, m_w_out, m_g_ffn, m_w_ffn_gate, m_w_ffn_up, m_w_ffn_down, m_w_ple_gate, m_w_ple_proj, m_g_final, v_g_mix, v_w_in, v_a_re, v_a_im, v_log_dt, v_b_re, v_b_im, v_c_re, v_c_im, v_d_skip, v_w_attn_proj, v_w_glu_a, v_w_glu_b, v_w_out, v_g_ffn, v_w_ffn_gate, v_w_ffn_up, v_w_ffn_down, v_w_ple_gate, v_w_ple_proj, v_g_final):
    given = dict(locals())
    big_w = {n: given[n] for n in BIG}
    w_mats = {n: big_w[n].reshape(big_w[n].shape[1:]) for n in BIG}

    ax, ay, ac = lax.axis_index("x"), lax.axis_index("y"), lax.axis_index("c")
    place = jnp.stack([ax, ay, ac, 2 * ax + ay, 4 * ax + 2 * ay + ac]).astype(jnp.int32)

    bufs = []
    for n in BIG:
        r, c = w_mats[n].shape
        bufs.append(_cast_into_slot(w_mats[n], place).reshape(N_CHIPS, 2, r // 2, c))
    w_in_own = _cast_shard(w_mats["w_in"])

    sm = {
        "g_mix": g_mix.reshape(1, D_MODEL), "g_ffn": g_ffn.reshape(1, D_MODEL), "g_final": g_final.reshape(1, D_MODEL),
        "a_re": a_re[0], "a_im": a_im[0], "log_dt": log_dt[0], "b_re": b_re[0], "b_im": b_im[0], "c_re": c_re[0],
        "c_im": c_im[0], "d_skip": d_skip[0],
    }
    s = x.shape[1]
    grad_x, d_w_in, early_parts, early_got, vec, vec_got = _local_step(
        x[0], p[0, 0], positions[0], loss_target[0], sm, w_in_own, bufs[0], bufs[1:], place)

    r_in, c_in = w_mats["w_in"].shape
    g5_in = [d_w_in.reshape(N_CHIPS, 2, r_in // 2, c_in)]
    in_parts = [_pair_sum(g, t, place) for g, t in zip(g5_in, _pair_exchange(g5_in))]
    chip_parts = in_parts + list(early_parts)
    chip_got = list(_chip_exchange(in_parts)) + list(early_got)
    halves = [_chip_sum(own, got, place) for own, got in zip(chip_parts, chip_got)]
    other_halves = _pair_gather(halves)

    results = {}
    for n, mine, other in zip(BIG, halves, other_halves):
        r, c = w_mats[n].shape
        shp = big_w[n].shape
        outs = _adamw_halves("adamw_" + n, mine, other, w_mats[n], given["m_" + n].reshape(r, c),
                             given["v_" + n].reshape(r, c), place)
        results[n] = [o.reshape(shp) for o in outs]

    small_shapes = [given[n].shape for n in SMALL]
    tot = _device_sum(vec, vec_got, place)
    n_small = sum(math.prod(shp) for shp in small_shapes)
    loss = tot.reshape(-1)[n_small]
    w_s = _pack([given[n] for n in SMALL])
    m_s = _pack([given["m_" + n] for n in SMALL])
    v_s = _pack([given["v_" + n] for n in SMALL])
    rows_s = w_s.shape[0]
    g_s = tot.reshape(-1)[: rows_s * 128].reshape(rows_s, 128)
    outs_s = _adamw("adamw_small", g_s, w_s, m_s, v_s)
    for kind, mat in enumerate(outs_s):
        for n, arr in zip(SMALL, _unpack(mat, small_shapes)):
            results.setdefault(n, [None] * 4)[kind] = arr

    order = ("g_mix", "w_in", "a_re", "a_im", "log_dt", "b_re", "b_im", "c_re", "c_im", "d_skip", "w_attn_proj", "w_glu_a",
             "w_glu_b", "w_out", "g_ffn", "w_ffn_gate", "w_ffn_up", "w_ffn_down", "w_ple_gate", "w_ple_proj", "g_final")
    out = [loss, grad_x.reshape(1, s, D_MODEL)]
    for kind in range(4):
        out += [results[n][kind] for n in order]
    return tuple(out)
```

```python
import math

import jax
import jax.numpy as jnp
from jax import lax
from jax.experimental import pallas as pl
from jax.experimental.pallas import tpu as pltpu

F32 = jnp.float32
BF16 = jnp.bfloat16

D_MODEL = 1024
HEAD_DIM = 128
HEADS_PER_GROUP = 4
GROUP_WIDTH = HEADS_PER_GROUP * HEAD_DIM
GROUP_DILATIONS = (1, 4, 16)
N_GROUPS = len(GROUP_DILATIONS)
LSE_LANES = 32
LSE_WIDTH = HEADS_PER_GROUP * LSE_LANES
ATTN_BLOCK = 128
ROPE_DIM = 32
ROPE_HALF = 16
ROPE_THETA = 500000.0
SSM_WIDTH = 512
SSM_GROUPS = 32
SSM_GROUP = 16
SSM_STATE = 64
N_STATE = SSM_GROUPS * SSM_STATE
SSM_SUPER = 4
IN_WIDTH = 7168
COL_U = 4608
COL_GA = 5120
COL_GS = 6144
D_FF = 2816
N_CHIPS = 4
D_FF_Q = D_FF // N_CHIPS
PLE_DIM = 256
EPS = 1e-6
ADAM_LR = 0.001
ADAM_B1 = 0.9
ADAM_B2 = 0.999
ADAM_EPS = 1e-08
ADAM_WD = 0.01
ADAM_STEP = 10
NEG_BIG = -1e30
VMEM_LIMIT_BYTES = 56 * 1024 * 1024
MESH = pl.DeviceIdType.MESH

_DIMS = {
    "nn": (((1,), (0,)), ((), ())),
    "nt": (((1,), (1,)), ((), ())),
    "tn": (((0,), (0,)), ((), ())),
}


def _params(n_grid):
    return pltpu.CompilerParams(dimension_semantics=("arbitrary",) * n_grid, vmem_limit_bytes=VMEM_LIMIT_BYTES)


def _sig(v):
    return 1.0 / (1.0 + jnp.exp(-v))


def _dot(a, b, mode):
    return lax.dot_general(a, b, _DIMS[mode], preferred_element_type=F32)


def _mm(name, grid, pairs, mode, outs, epilogue=None, extras=(), acc_outs=(), acc_shape=None, j_outer=False,
        sum_pairs=True, resident_b=False, comm=None, place=None, fill=None, prologue=None):
    gi, gj, gk = grid
    n_p, n_e, n_o, n_a = len(pairs), len(extras), len(outs), len(acc_outs)
    assert not n_a or gj == 1
    assert sum_pairs or gk == 1
    run_grid = (gj, gi, gk) if j_outer else grid
    c_ins = list(comm["ins"]) if comm else []
    c_outs = list(comm["outs"]) if comm else []
    c_sems = list(comm["sems"]) if comm else []
    n_ci, n_co, n_cs = len(c_ins), len(c_outs), len(c_sems)
    n_s = 0 if place is None else 1
    n_fill = 0 if fill is None else 1

    def order(imap):
        if place is None:
            return (lambda j, i, k: imap(i, j, k)) if j_outer else imap
        return (lambda j, i, k, pv: imap(i, j, k, pv)) if j_outer else imap

    shared_a = [pr[0] is None for pr in pairs]
    n_in = 2 * n_p - sum(shared_a)

    def body(*refs):
        refs = refs[n_s:]
        pair_refs = list(refs[:n_in])
        extra_refs = refs[n_in: n_in + n_e]
        comm_in = refs[n_in + n_e: n_in + n_e + n_ci]
        at = n_in + n_e + n_ci + n_fill
        out_refs = refs[at: at + n_o]
        sum_refs = refs[at + n_o: at + n_o + n_a]
        comm_out = refs[at + n_o + n_a: at + n_o + n_a + n_co]
        scratch_refs = refs[at + n_o + n_a + n_co:]
        i = pl.program_id(1 if j_outer else 0)
        k = pl.program_id(2)
        if comm:
            step = (pl.program_id(0) * run_grid[1] + pl.program_id(1)) * run_grid[2] + pl.program_id(2)
            sems = scratch_refs[len(scratch_refs) - n_cs:]
            for at_step, stage in comm["stages"]:
                @pl.when(step == at_step)
                def _(stage=stage):
                    stage(comm_in, comm_out, sems)
        part = None if sum_pairs else []
        a = None
        for t in range(n_p):
            if not shared_a[t]:
                a = pair_refs.pop(0)[...]
                if prologue is not None and t == 0:
                    a = prologue(a, *[e[...] for e in extra_refs]).astype(BF16)
                    out_refs[n_o - 1][...] = a
                a = a.astype(BF16)
            b = pair_refs.pop(0)[...].astype(BF16)
            d = _dot(a, b, mode)
            if sum_pairs:
                part = d if part is None else part + d
            else:
                part.append(d)

        def finish(acc):
            tiles, sums = epilogue(acc, *[e[...] for e in extra_refs]) if epilogue is not None else ((acc,), ())
            for o_ref, tile in zip(out_refs, tiles):
                o_ref[...] = tile.astype(o_ref.dtype)
            if n_a:
                @pl.when(i == 0)
                def _():
                    for s_ref in sum_refs:
                        s_ref[...] = jnp.zeros_like(s_ref)

                for s_ref, s in zip(sum_refs, sums):
                    s_ref[...] += s

        if gk == 1:
            finish(part)
        else:
            acc_ref = scratch_refs[0]

            @pl.when(k == 0)
            def _():
                acc_ref[...] = part

            @pl.when(k > 0)
            def _():
                acc_ref[...] += part

            @pl.when(k == gk - 1)
            def _():
                finish(acc_ref[...])

    in_specs, args = [], []
    for a, a_block, a_imap, b, b_block, b_imap in pairs:
        if a is not None:
            in_specs.append(pl.BlockSpec(a_block, order(a_imap)))
            args.append(a)
        if resident_b:
            in_specs.append(pl.BlockSpec(b_block, order(b_imap), pipeline_mode=pl.Buffered(1)))
        else:
            in_specs.append(pl.BlockSpec(b_block, order(b_imap)))
        args.append(b)
    for e, e_block, e_imap in extras:
        in_specs.append(pl.BlockSpec(e_block, order(e_imap)))
        args.append(e)
    first_comm_in = len(args)
    for c_in in c_ins:
        in_specs.append(pl.BlockSpec(memory_space=pl.ANY))
        args.append(c_in)
    if n_fill:
        in_specs.append(pl.BlockSpec(memory_space=pl.ANY))
        args.append(fill)
    out_shape = [jax.ShapeDtypeStruct(shape, dtype) for shape, dtype, _, _ in outs]
    out_specs = [pl.BlockSpec(block, order(imap)) for _, _, block, imap in outs]
    for shape, dtype in acc_outs:
        out_shape.append(jax.ShapeDtypeStruct(shape, dtype))
        out_specs.append(pl.BlockSpec(shape, lambda *_: (0, 0)))
    first_comm_out = len(out_shape)
    for c_out in c_outs:
        out_shape.append(c_out)
        out_specs.append(pl.BlockSpec(memory_space=pl.ANY))
    aliases = {n_s + first_comm_in + n: first_comm_out + n for n in range(n_ci)} if comm and comm["aliased"] else {}
    if n_fill:
        aliases[n_s + len(args) - 1] = 0
    scratch = [pltpu.VMEM(acc_shape, F32)] if gk > 1 else []
    scratch += [pltpu.SemaphoreType.DMA((n,)) for n in c_sems]
    if n_s:
        spec = pltpu.PrefetchScalarGridSpec(num_scalar_prefetch=1, grid=run_grid, in_specs=in_specs, out_specs=out_specs,
                                            scratch_shapes=scratch)
        return pl.pallas_call(body, name=name, grid_spec=spec, out_shape=out_shape, compiler_params=_params(3),
                              input_output_aliases=aliases)(place, *args)
    return pl.pallas_call(
        body, name=name, grid=run_grid, in_specs=in_specs, out_specs=out_specs,
        out_shape=out_shape, scratch_shapes=scratch, compiler_params=_params(3), input_output_aliases=aliases,
    )(*args)


def _ew(name, grid, ins, outs, fn, acc_outs=(), place=None):
    n_i, n_o, n_a = len(ins), len(outs), len(acc_outs)
    ng = len(grid)
    n_s = 0 if place is None else 1

    def body(*refs):
        in_refs = refs[n_s: n_s + n_i]
        out_refs = refs[n_s + n_i: n_s + n_i + n_o]
        sum_refs = refs[n_s + n_i + n_o:]
        pids = tuple(pl.program_id(a) for a in range(ng))
        if n_s:
            pids = (refs[0],) + pids
        tiles, sums = fn(pids, *[r[...] for r in in_refs])
        for o_ref, tile in zip(out_refs, tiles):
            o_ref[...] = tile.astype(o_ref.dtype)
        if n_a:
            first = pids[0] == 0
            for p_ in pids[1:]:
                first = jnp.logical_and(first, p_ == 0)

            @pl.when(first)
            def _():
                for s_ref in sum_refs:
                    s_ref[...] = jnp.zeros_like(s_ref)

            for s_ref, s in zip(sum_refs, sums):
                s_ref[...] += s

    in_specs = [pl.BlockSpec(block, imap) for _, block, imap in ins]
    out_shape = [jax.ShapeDtypeStruct(shape, dtype) for shape, dtype, _, _ in outs]
    out_specs = [pl.BlockSpec(block, imap) for _, _, block, imap in outs]
    for shape, dtype in acc_outs:
        out_shape.append(jax.ShapeDtypeStruct(shape, dtype))
        out_specs.append(pl.BlockSpec(shape, lambda *_, nd=len(shape): (0,) * nd))
    arrays = [a for a, _, _ in ins]
    if n_s:
        assert not n_a
        spec = pltpu.PrefetchScalarGridSpec(num_scalar_prefetch=1, grid=grid, in_specs=in_specs, out_specs=out_specs)
        return pl.pallas_call(body, name=name, grid_spec=spec, out_shape=out_shape, compiler_params=_params(ng))(
            place, *arrays)
    return pl.pallas_call(
        body, name=name, grid=grid, in_specs=in_specs, out_specs=out_specs, out_shape=out_shape,
        compiler_params=_params(ng),
    )(*arrays)


def _rows(tm, width):
    return (tm, width), (lambda i: (i, 0))


def _rms_fwd_tile(h, g):
    r = lax.rsqrt(jnp.mean(h * h, axis=-1, keepdims=True) + EPS)
    return h * r * g


def _rms_bwd_tile(dn, h, g):
    r = lax.rsqrt(jnp.mean(h * h, axis=-1, keepdims=True) + EPS)
    hhat = h * r
    gy = dn * g
    dh = r * (gy - hhat * jnp.mean(gy * hhat, axis=-1, keepdims=True))
    dg = jnp.sum(dn * hhat, axis=0, keepdims=True)
    return dh, dg


def _rope_tables(pos_col, inv_row, tm):
    s = pos_col.shape[0]

    def fn(pids, pos, inv):
        ang = pos * inv
        lane = lax.broadcasted_iota(jnp.int32, ang.shape, 1)
        cs = jnp.where(lane < ROPE_DIM, jnp.cos(ang), 1.0)
        sn = jnp.sin(ang)
        s_lo = jnp.where(lane < ROPE_HALF, -sn, 0.0)
        s_hi = jnp.where(jnp.logical_and(lane >= ROPE_HALF, lane < ROPE_DIM), sn, 0.0)
        return (cs, s_lo, s_hi), ()

    blk, imap = _rows(tm, 128)
    return _ew(
        "rope_tables", (s // tm,),
        [(pos_col, (tm, 1), lambda i: (i, 0)), (inv_row, (1, 128), lambda i: (0, 0))],
        [((s, 128), F32, blk, imap)] * 3, fn,
    )


def _rope(xh, cs, s_lo, s_hi):
    return xh * cs + pltpu.roll(xh, HEAD_DIM - ROPE_HALF, 1) * s_lo + pltpu.roll(xh, ROPE_HALF, 1) * s_hi


def _rope_t(gh, cs, s_lo, s_hi):
    return gh * cs + pltpu.roll(gh * s_lo, ROPE_HALF, 1) + pltpu.roll(gh * s_hi, HEAD_DIM - ROPE_HALF, 1)


def _attn_geometry(length):
    nb = length // ATTN_BLOCK
    gq = min(8, nb)
    assert nb % gq == 0
    return nb, gq, gq * ATTN_BLOCK, nb // gq


def _band_masks():
    qi = lax.broadcasted_iota(jnp.int32, (ATTN_BLOCK, ATTN_BLOCK), 0)
    kj = lax.broadcasted_iota(jnp.int32, (ATTN_BLOCK, ATTN_BLOCK), 1)
    return kj <= qi, kj >= qi


def _band_mask_pair():
    qi = lax.broadcasted_iota(jnp.int32, (ATTN_BLOCK, 2 * ATTN_BLOCK), 0)
    cj = lax.broadcasted_iota(jnp.int32, (ATTN_BLOCK, 2 * ATTN_BLOCK), 1)
    in_cur = cj >= ATTN_BLOCK
    band = jnp.logical_or(jnp.logical_and(in_cur, cj - ATTN_BLOCK <= qi),
                          jnp.logical_and(cj < ATTN_BLOCK, cj >= qi))
    return band, in_cur


def _attn_fwd(qv, kv, vv, dil, cols3=(0, 0, 0)):
    length = qv.shape[0]
    nb, gq, rows, ni = _attn_geometry(length)

    def body(q_ref, kc_ref, kp_ref, vc_ref, vp_ref, o_ref, l_ref):
        i = pl.program_id(1)
        band, in_cur = _band_mask_pair()
        band_first = jnp.logical_and(band, jnp.logical_or(in_cur, i > 0))
        work = []
        for h in range(HEADS_PER_GROUP):
            cols = slice(h * HEAD_DIM, (h + 1) * HEAD_DIM)
            qh = q_ref[:, cols]
            k_all = jnp.concatenate([kp_ref[:, cols], kc_ref[:, cols]], axis=0)
            v_all = jnp.concatenate([vp_ref[:, cols], vc_ref[:, cols]], axis=0)
            for jj in range(gq):
                rws = slice(jj * ATTN_BLOCK, (jj + 1) * ATTN_BLOCK)
                two = slice(jj * ATTN_BLOCK, (jj + 2) * ATTN_BLOCK)
                work.append(dict(h=h, rws=rws, cols=cols, v=v_all[two], first=jj == 0, s=_dot(qh[rws], k_all[two], "nt")))
        for w in work:
            s = jnp.where(band_first if w["first"] else band, w["s"], NEG_BIG)
            m = jnp.max(s, axis=-1, keepdims=True)
            pexp = jnp.exp(s - m)
            w["den"] = jnp.sum(pexp, axis=-1, keepdims=True)
            w["p"] = pexp.astype(BF16)
            w["lse"] = m + jnp.log(w["den"])
        for w in work:
            o = _dot(w["p"], w["v"], "nn")
            o_ref[w["rws"], w["cols"]] = (o * (1.0 / w["den"])).astype(o_ref.dtype)
            l_ref[w["rws"], w["h"] * LSE_LANES:(w["h"] + 1) * LSE_LANES] = jnp.broadcast_to(w["lse"], (ATTN_BLOCK, LSE_LANES))

    def cur(c):
        return pl.BlockSpec((rows, GROUP_WIDTH), lambda r, i: (i, r + c))

    def prev(c):
        return pl.BlockSpec((ATTN_BLOCK, GROUP_WIDTH), lambda r, i: (jnp.maximum(i * gq - 1, 0), r + c))

    cq, ck, cv = cols3
    return pl.pallas_call(
        body, name=f"attn_fwd_d{dil}", grid=(dil, ni),
        in_specs=[cur(cq), cur(ck), prev(ck), cur(cv), prev(cv)],
        out_specs=[cur(0), pl.BlockSpec((rows, LSE_WIDTH), lambda r, i: (i, r))],
        out_shape=[jax.ShapeDtypeStruct((length, dil * GROUP_WIDTH), BF16),
                   jax.ShapeDtypeStruct((length, dil * LSE_WIDTH), F32)],
        compiler_params=_params(2),
    )(qv, kv, kv, vv, vv)


def _attn_bwd(qv, kv, vv, dov, ov, lv, dil, cols3=(0, 0, 0)):
    length = qv.shape[0]
    nb, gq, rows, ni = _attn_geometry(length)
    out_shape = (length, dil * GROUP_WIDTH)

    def body(qc_ref, qn_ref, kc_ref, kp_ref, vc_ref, vp_ref, doc_ref, don_ref, oc_ref, on_ref, lc_ref, ln_ref,
             dq_ref, dk_ref, dv_ref):
        i = pl.program_id(1)
        _, mask_p = _band_masks()
        band, in_cur = _band_mask_pair()
        band_first = jnp.logical_and(band, jnp.logical_or(in_cur, i > 0))
        has_next = i < ni - 1

        last = slice(gq * ATTN_BLOCK, (gq + 1) * ATTN_BLOCK)
        mask_next = jnp.logical_and(mask_p, has_next)

        def rows_of(jj):
            return slice(jj * ATTN_BLOCK, (jj + 1) * ATTN_BLOCK)

        def keys_of(jj):
            return slice(jj * ATTN_BLOCK, (jj + 2) * ATTN_BLOCK)

        heads = []
        for h in range(HEADS_PER_GROUP):
            cols = slice(h * HEAD_DIM, (h + 1) * HEAD_DIM)
            hd = dict(
                cols=cols, q_c=qc_ref[:, cols], q_n=qn_ref[:, cols],
                k_all=jnp.concatenate([kp_ref[:, cols], kc_ref[:, cols]], axis=0),
                v_all=jnp.concatenate([vp_ref[:, cols], vc_ref[:, cols]], axis=0),
                do_c=doc_ref[:, cols], do_n=don_ref[:, cols],
                l_c=lc_ref[:, h * LSE_LANES:h * LSE_LANES + 1], l_n=ln_ref[:, h * LSE_LANES:h * LSE_LANES + 1],
            )
            hd["dl_c"] = jnp.sum(hd["do_c"].astype(F32) * oc_ref[:, cols].astype(F32), axis=-1, keepdims=True)
            hd["dl_n"] = jnp.sum(hd["do_n"].astype(F32) * on_ref[:, cols].astype(F32), axis=-1, keepdims=True)
            hd["s"] = [_dot(hd["q_c"][rows_of(jj)], hd["k_all"][keys_of(jj)], "nt") for jj in range(gq)]
            hd["dp"] = [_dot(hd["do_c"][rows_of(jj)], hd["v_all"][keys_of(jj)], "nt") for jj in range(gq)]
            hd["s"].append(_dot(hd["q_n"], hd["k_all"][last], "nt"))
            hd["dp"].append(_dot(hd["do_n"], hd["v_all"][last], "nt"))
            heads.append(hd)
        for hd in heads:
            hd["p"], hd["ds"] = [], []
            for jj in range(gq + 1):
                if jj < gq:
                    mask, l_col, delta = (band_first if jj == 0 else band), hd["l_c"][rows_of(jj)], hd["dl_c"][rows_of(jj)]
                else:
                    mask, l_col, delta = mask_next, hd["l_n"], hd["dl_n"]
                p = jnp.where(mask, jnp.exp(hd["s"][jj] - l_col), 0.0)
                hd["p"].append(p.astype(BF16))
                hd["ds"].append((p * (hd["dp"][jj] - delta)).astype(BF16))
        for hd in heads:
            cols = hd["cols"]
            dk_blocks, dv_blocks = [None] * (gq + 1), [None] * (gq + 1)

            def add(lst, idx, val):
                lst[idx] = val if lst[idx] is None else lst[idx] + val

            for jj in range(gq):
                qb, dob = hd["q_c"][rows_of(jj)], hd["do_c"][rows_of(jj)]
                dq_ref[rows_of(jj), cols] = _dot(hd["ds"][jj], hd["k_all"][keys_of(jj)], "nn").astype(dq_ref.dtype)
                dk2 = _dot(hd["ds"][jj], qb, "tn")
                dv2 = _dot(hd["p"][jj], dob, "tn")
                add(dk_blocks, jj, dk2[:ATTN_BLOCK])
                add(dk_blocks, jj + 1, dk2[ATTN_BLOCK:])
                add(dv_blocks, jj, dv2[:ATTN_BLOCK])
                add(dv_blocks, jj + 1, dv2[ATTN_BLOCK:])
            add(dk_blocks, gq, _dot(hd["ds"][gq], hd["q_n"], "tn"))
            add(dv_blocks, gq, _dot(hd["p"][gq], hd["do_n"], "tn"))
            for jj in range(gq):
                dk_ref[rows_of(jj), cols] = dk_blocks[jj + 1].astype(dk_ref.dtype)
                dv_ref[rows_of(jj), cols] = dv_blocks[jj + 1].astype(dv_ref.dtype)

    def cur(c):
        return pl.BlockSpec((rows, GROUP_WIDTH), lambda r, i: (i, r + c))

    def prev(c):
        return pl.BlockSpec((ATTN_BLOCK, GROUP_WIDTH), lambda r, i: (jnp.maximum(i * gq - 1, 0), r + c))

    def nxt(c):
        return pl.BlockSpec((ATTN_BLOCK, GROUP_WIDTH), lambda r, i: (jnp.minimum((i + 1) * gq, nb - 1), r + c))

    cq, ck, cv = cols3
    lse_cur = pl.BlockSpec((rows, LSE_WIDTH), lambda r, i: (i, r))
    lse_next = pl.BlockSpec((ATTN_BLOCK, LSE_WIDTH), lambda r, i: (jnp.minimum((i + 1) * gq, nb - 1), r))
    return pl.pallas_call(
        body, name=f"attn_bwd_d{dil}", grid=(dil, ni),
        in_specs=[cur(cq), nxt(cq), cur(ck), prev(ck), cur(cv), prev(cv), cur(0), nxt(0), cur(0), nxt(0), lse_cur, lse_next],
        out_specs=[cur(0), cur(0), cur(0)],
        out_shape=[jax.ShapeDtypeStruct(out_shape, BF16)] * 3,
        compiler_params=_params(2),
    )(qv, qv, kv, kv, vv, vv, dov, dov, ov, ov, lv, lv)


DILATED = tuple((g, d) for g, d in enumerate(GROUP_DILATIONS) if d > 1)


def _spread(scr, slot, tile, out_ref, dil, col, width=GROUP_WIDTH):
    tm = tile.shape[0]
    buf = scr.at[slot]
    buf[...] = tile
    for r in range(dil):
        c0 = r * width + col
        out_ref[:, c0:c0 + HEAD_DIM] = buf[pl.ds(r, tm // dil, stride=dil), :].astype(out_ref.dtype)


def _collect(scr, slot, in_ref, dil, col, width=GROUP_WIDTH):
    tm = scr.shape[1]
    buf = scr.at[slot]
    for r in range(dil):
        c0 = r * width + col
        buf[pl.ds(r, tm // dil, stride=dil), :] = in_ref[:, c0:c0 + HEAD_DIM].astype(F32)
    return buf[...]


def _view_spec(tm, dil, width=GROUP_WIDTH):
    return pl.BlockSpec((tm // dil, dil * width), lambda i: (i, 0))


def _view_shape(s, dil, dtype, width=GROUP_WIDTH):
    return jax.ShapeDtypeStruct((s // dil, dil * width), dtype)


def _qkv_layout(z, tabs, tm):
    s = z.shape[0]
    scale = 1.0 / math.sqrt(HEAD_DIM)
    qkv_width = 3 * N_GROUPS * GROUP_WIDTH

    def body(z_ref, cs_ref, lo_ref, hi_ref, qk0_ref, *rest):
        views, scr = rest[:-1], rest[-1]
        tabs_ = (cs_ref[...], lo_ref[...], hi_ref[...])
        for part in range(3):
            for g, dil in enumerate(GROUP_DILATIONS):
                if part == 2 and dil == 1:
                    continue
                for h in range(HEADS_PER_GROUP):
                    col = part * N_GROUPS * GROUP_WIDTH + g * GROUP_WIDTH + h * HEAD_DIM
                    t = z_ref[:, col:col + HEAD_DIM].astype(F32)
                    if part < 2:
                        t = _rope(t, *tabs_)
                    if part == 0:
                        t = t * scale
                    if dil == 1:
                        c0 = part * GROUP_WIDTH + h * HEAD_DIM
                        qk0_ref[:, c0:c0 + HEAD_DIM] = t.astype(BF16)
                    else:
                        out = views[3 * [gg for gg, _ in DILATED].index(g) + part]
                        _spread(scr, h, t, out, dil, h * HEAD_DIM)

    row = lambda i: (i, 0)
    out_shape = [jax.ShapeDtypeStruct((s, 2 * GROUP_WIDTH), BF16)]
    out_specs = [pl.BlockSpec((tm, 2 * GROUP_WIDTH), row)]
    for _, dil in DILATED:
        out_shape += [_view_shape(s, dil, BF16)] * 3
        out_specs += [_view_spec(tm, dil)] * 3
    res = pl.pallas_call(
        body, name="qkv_layout", grid=(s // tm,),
        in_specs=[pl.BlockSpec((tm, qkv_width), row)] + [pl.BlockSpec((tm, HEAD_DIM), row)] * 3,
        out_specs=out_specs, out_shape=out_shape,
        scratch_shapes=[pltpu.VMEM((HEADS_PER_GROUP, tm, HEAD_DIM), F32)], compiler_params=_params(1),
    )(z, *tabs)
    return res[0], [tuple(res[1 + 3 * n:4 + 3 * n]) for n in range(len(DILATED))]


def _attn_merge(o0, l0, dilated, tm):
    s = o0.shape[0]
    n_d = len(DILATED)

    def body(*refs):
        o0_ref, l0_ref = refs[:2]
        in_views = refs[2:2 + 2 * n_d]
        attn_ref, lse_ref = refs[2 + 2 * n_d:4 + 2 * n_d]
        out_views = refs[4 + 2 * n_d:4 + 4 * n_d]
        scr = refs[-1]
        l_rows = [l0_ref[...]] + [_collect(scr, n, in_views[2 * n + 1], dil, 0, LSE_WIDTH) for n, (_, dil) in enumerate(DILATED)]
        lse_heads = []
        for h in range(HEADS_PER_GROUP):
            cols = slice(h * HEAD_DIM, (h + 1) * HEAD_DIM)
            os_ = [o0_ref[:, cols].astype(F32)]
            for n, (_, dil) in enumerate(DILATED):
                os_.append(_collect(scr, n_d + n, in_views[2 * n], dil, h * HEAD_DIM))
            ls_ = [lr[:, h * LSE_LANES:h * LSE_LANES + 1] for lr in l_rows]
            m = ls_[0]
            for l_ in ls_[1:]:
                m = jnp.maximum(m, l_)
            es = [jnp.exp(l_ - m) for l_ in ls_]
            den = es[0]
            num = es[0] * os_[0]
            for e, o in zip(es[1:], os_[1:]):
                den = den + e
                num = num + e * o
            attn = num * (1.0 / den)
            lse_heads.append(jnp.broadcast_to(m + jnp.log(den), (tm, LSE_LANES)))
            attn_ref[:, cols] = attn.astype(BF16)
            for n, (_, dil) in enumerate(DILATED):
                _spread(scr, 2 * n_d, attn, out_views[2 * n], dil, h * HEAD_DIM)
        lse = jnp.concatenate(lse_heads, axis=1)
        lse_ref[...] = lse
        for n, (_, dil) in enumerate(DILATED):
            _spread(scr, 2 * n_d, lse, out_views[2 * n + 1], dil, 0, LSE_WIDTH)

    row = lambda i: (i, 0)
    nat = pl.BlockSpec((tm, GROUP_WIDTH), row)
    nat_l = pl.BlockSpec((tm, LSE_WIDTH), row)
    in_specs = [nat, nat_l]
    args = [o0, l0]
    out_specs = [nat, nat_l]
    out_shape = [jax.ShapeDtypeStruct((s, GROUP_WIDTH), BF16), jax.ShapeDtypeStruct((s, LSE_WIDTH), F32)]
    for (_, dil), (ov, lv) in zip(DILATED, dilated):
        in_specs += [_view_spec(tm, dil), _view_spec(tm, dil, LSE_WIDTH)]
        args += [ov, lv]
        out_specs += [_view_spec(tm, dil), _view_spec(tm, dil, LSE_WIDTH)]
        out_shape += [_view_shape(s, dil, BF16), _view_shape(s, dil, F32, LSE_WIDTH)]
    res = pl.pallas_call(
        body, name="attn_merge", grid=(s // tm,), in_specs=in_specs, out_specs=out_specs, out_shape=out_shape,
        scratch_shapes=[pltpu.VMEM((2 * n_d + 1, tm, HEAD_DIM), F32)], compiler_params=_params(1),
    )(*args)
    return res[0], res[1], [tuple(res[2 + 2 * n:4 + 2 * n]) for n in range(n_d)]


def _to_views(a, tm):
    s = a.shape[0]

    def body(a_ref, *rest):
        outs, scr = rest[:-1], rest[-1]
        for h in range(HEADS_PER_GROUP):
            t = a_ref[:, h * HEAD_DIM:(h + 1) * HEAD_DIM].astype(F32)
            for n, (_, dil) in enumerate(DILATED):
                _spread(scr, n, t, outs[n], dil, h * HEAD_DIM)

    return pl.pallas_call(
        body, name="to_views", grid=(s // tm,), in_specs=[pl.BlockSpec((tm, GROUP_WIDTH), lambda i: (i, 0))],
        out_specs=[_view_spec(tm, dil) for _, dil in DILATED], out_shape=[_view_shape(s, dil, BF16) for _, dil in DILATED],
        scratch_shapes=[pltpu.VMEM((len(DILATED), tm, HEAD_DIM), F32)], compiler_params=_params(1),
    )(a)


def _dz_layout(grads, du, dga, dgs, tabs, tm, comm=None):
    s = du.shape[0]
    scale = 1.0 / math.sqrt(HEAD_DIM)
    n_steps = s // tm
    c_ins = list(comm["ins"]) if comm else []
    c_outs = list(comm["outs"]) if comm else []
    c_sems = list(comm["sems"]) if comm else []
    n_fixed = 3 * N_GROUPS + 6

    def body(*refs):
        g_refs = refs[:3 * N_GROUPS]
        du_ref, dga_ref, dgs_ref, cs_ref, lo_ref, hi_ref = refs[3 * N_GROUPS:n_fixed]
        comm_in = refs[n_fixed:n_fixed + len(c_ins)]
        dz_ref = refs[n_fixed + len(c_ins)]
        comm_out = refs[n_fixed + len(c_ins) + 1:n_fixed + len(c_ins) + 1 + len(c_outs)]
        scr = refs[n_fixed + len(c_ins) + 1 + len(c_outs)]
        sems = refs[n_fixed + len(c_ins) + 2 + len(c_outs):]
        if comm:
            @pl.when(pl.program_id(0) == 0)
            def _():
                comm["start"](comm_in, comm_out, sems)

            @pl.when(pl.program_id(0) == n_steps - 1)
            def _():
                comm["finish"](comm_in, comm_out, sems)

        tabs_ = (cs_ref[...], lo_ref[...], hi_ref[...])
        for part in range(3):
            for g, dil in enumerate(GROUP_DILATIONS):
                src = g_refs[3 * g + part]
                for h in range(HEADS_PER_GROUP):
                    if dil == 1:
                        t = src[:, h * HEAD_DIM:(h + 1) * HEAD_DIM].astype(F32)
                    else:
                        t = _collect(scr, h, src, dil, h * HEAD_DIM)
                    if part < 2:
                        t = _rope_t(t, *tabs_)
                    if part == 0:
                        t = t * scale
                    col = part * N_GROUPS * GROUP_WIDTH + g * GROUP_WIDTH + h * HEAD_DIM
                    dz_ref[:, col:col + HEAD_DIM] = t.astype(BF16)
        dz_ref[:, COL_U:COL_GA] = du_ref[...]
        dz_ref[:, COL_GA:COL_GS] = dga_ref[...]
        dz_ref[:, COL_GS:IN_WIDTH] = dgs_ref[...]

    row = lambda i: (i, 0)
    in_specs, args = [], []
    for (g, dil), trio in zip(enumerate(GROUP_DILATIONS), grads):
        in_specs += [pl.BlockSpec((tm, GROUP_WIDTH), row) if dil == 1 else _view_spec(tm, dil)] * 3
        args += list(trio)
    in_specs += [pl.BlockSpec((tm, SSM_WIDTH), row), pl.BlockSpec((tm, D_MODEL), row), pl.BlockSpec((tm, D_MODEL), row)]
    in_specs += [pl.BlockSpec((tm, HEAD_DIM), row)] * 3
    in_specs += [pl.BlockSpec(memory_space=pl.ANY)] * len(c_ins)
    res = pl.pallas_call(
        body, name="dz_layout", grid=(n_steps,), in_specs=in_specs,
        out_specs=[pl.BlockSpec((tm, IN_WIDTH), row)] + [pl.BlockSpec(memory_space=pl.ANY)] * len(c_outs),
        out_shape=[jax.ShapeDtypeStruct((s, IN_WIDTH), BF16)] + c_outs,
        scratch_shapes=[pltpu.VMEM((HEADS_PER_GROUP, tm, HEAD_DIM), F32)] + [pltpu.SemaphoreType.DMA((n,)) for n in c_sems],
        compiler_params=_params(1),
    )(*args, du, dga, dgs, *tabs, *c_ins)
    return res[0], list(res[1:])


def _discretise(a_re, a_im, log_dt, bt_re, bt_im):
    dt = jnp.exp(log_dt)
    mag = jnp.exp(a_re * dt)
    bar_re = mag * jnp.cos(a_im * dt)
    bar_im = mag * jnp.sin(a_im * dt)
    nr = bar_re - 1.0
    ni = bar_im
    den = a_re * a_re + a_im * a_im
    z_re = (nr * a_re + ni * a_im) / den
    z_im = (ni * a_re - nr * a_im) / den
    bb_re = z_re[:, None, :] * bt_re - z_im[:, None, :] * bt_im
    bb_im = z_re[:, None, :] * bt_im + z_im[:, None, :] * bt_re
    return bar_re, bar_im, bb_re, bb_im


def _ssm_prep(a_re, a_im, log_dt, bt_re, bt_im):
    def body(ar, ai, ld, br, bi, o_lr, o_li, o_br, o_bi):
        lr, li, bbr, bbi = _discretise(ar[...], ai[...], ld[...], br[...], bi[...])
        o_lr[...] = lr
        o_li[...] = li
        o_br[...] = bbr
        o_bi[...] = bbi

    sm = jax.ShapeDtypeStruct((SSM_GROUPS, SSM_STATE), F32)
    bg = jax.ShapeDtypeStruct((SSM_GROUPS, SSM_GROUP, SSM_STATE), F32)
    return pl.pallas_call(body, name="ssm_prep", out_shape=[sm, sm, bg, bg])(a_re, a_im, log_dt, bt_re, bt_im)


def _ssm_param_bwd(a_re, a_im, log_dt, bt_re, bt_im, d_lr, d_li, d_bbr, d_bbi):
    def body(ar, ai, ld, br, bi, g_lr, g_li, g_br, g_bi, o_ar, o_ai, o_ld, o_br, o_bi):
        _, vjp = jax.vjp(_discretise, ar[...], ai[...], ld[...], br[...], bi[...])
        d_ar, d_ai, d_ld, d_br, d_bi = vjp((g_lr[...], g_li[...], g_br[...], g_bi[...]))
        o_ar[...] = d_ar
        o_ai[...] = d_ai
        o_ld[...] = d_ld
        o_br[...] = d_br
        o_bi[...] = d_bi

    sm = jax.ShapeDtypeStruct((SSM_GROUPS, SSM_STATE), F32)
    col = jax.ShapeDtypeStruct((SSM_GROUPS, 1), F32)
    bg = jax.ShapeDtypeStruct((SSM_GROUPS, SSM_GROUP, SSM_STATE), F32)
    return pl.pallas_call(body, name="ssm_param_bwd", out_shape=[sm, sm, col, bg, bg])(
        a_re, a_im, log_dt, bt_re, bt_im, d_lr, d_li, d_bbr, d_bbi)


def _block_diag(t, rows_per, cols_per):
    t4 = t.reshape(SSM_SUPER, 8, rows_per, cols_per)
    eye = jnp.eye(8, dtype=t.dtype)
    return jnp.einsum("bgrc,gh->bgrhc", t4, eye).reshape(SSM_SUPER, 8 * rows_per, 8 * cols_per)


def _block_diag_t(dense, rows_per, cols_per):
    t = dense.reshape(SSM_SUPER, 8, rows_per, 8, cols_per)
    eye = jnp.eye(8, dtype=dense.dtype)
    return jnp.einsum("bgrhc,gh->bgrc", t, eye).reshape(SSM_GROUPS, rows_per, cols_per)


def _gelu(v):
    c = math.sqrt(2.0 / math.pi)
    return 0.5 * v * (1.0 + jnp.tanh(c * (v + 0.044715 * v * v * v)))


def _gelu_grad(v):
    c = math.sqrt(2.0 / math.pi)
    t = jnp.tanh(c * (v + 0.044715 * v * v * v))
    return 0.5 * (1.0 + t) + 0.5 * v * (1.0 - t * t) * c * (1.0 + 3.0 * 0.044715 * v * v)


SUB = 8


SCAN_STEPS = (1, 2, 4)
N_SCAN_TABLES = 2 + 2 * len(SCAN_STEPS)


def _scan_tables(tab_ref, lam_re, lam_im, reverse, conj):
    lr = lam_re
    li = -lam_im if conj else lam_im
    powers = [(lr, li)]
    for _ in range(SUB - 1):
        pr, pi = powers[-1]
        powers.append((pr * lr - pi * li, pr * li + pi * lr))
    row = lax.broadcasted_iota(jnp.int32, (SUB, N_STATE), 0)
    if reverse:
        row = SUB - 1 - row
    wide = lambda v: jnp.broadcast_to(v, (SUB, N_STATE))
    p_re = jnp.zeros((SUB, N_STATE), F32)
    p_im = jnp.zeros((SUB, N_STATE), F32)
    for j in range(SUB):
        p_re = jnp.where(row == j, wide(powers[j][0]), p_re)
        p_im = jnp.where(row == j, wide(powers[j][1]), p_im)
    tab_ref[0] = p_re
    tab_ref[1] = p_im
    for idx, k in enumerate(SCAN_STEPS):
        tab_ref[2 + 2 * idx] = jnp.where(row >= k, wide(powers[k - 1][0]), 0.0)
        tab_ref[3 + 2 * idx] = jnp.where(row >= k, wide(powers[k - 1][1]), 0.0)


def _scan_rows(g_re_ref, g_im_ref, tab_ref, carry, n_rows, reverse):
    last = 0 if reverse else SUB - 1

    def tile_step(tt, state):
        cr, ci = state
        t8 = (n_rows // SUB - 1 - tt) if reverse else tt
        start = pl.multiple_of(t8 * SUB, SUB)
        xr = g_re_ref[pl.ds(start, SUB), :]
        xi = g_im_ref[pl.ds(start, SUB), :]
        for idx, k in enumerate(SCAN_STEPS):
            mr = tab_ref[2 + 2 * idx]
            mi = tab_ref[3 + 2 * idx]
            shift = SUB - k if reverse else k
            sr = pltpu.roll(xr, shift, 0)
            si = pltpu.roll(xi, shift, 0)
            xr, xi = xr + (mr * sr - mi * si), xi + (mr * si + mi * sr)
        pr = tab_ref[0]
        pi = tab_ref[1]
        xr, xi = xr + (pr * cr - pi * ci), xi + (pr * ci + pi * cr)
        g_re_ref[pl.ds(start, SUB), :] = xr
        g_im_ref[pl.ds(start, SUB), :] = xi
        return (jnp.broadcast_to(xr[last:last + 1, :], (SUB, N_STATE)),
                jnp.broadcast_to(xi[last:last + 1, :], (SUB, N_STATE)))

    return lax.fori_loop(0, n_rows // SUB, tile_step, carry)


def _ssm_fwd(z, b_re, b_im, c_re, c_im, lam_re, lam_im, d_skip, chunk):
    s = z.shape[0]

    def body(u_ref, bre, bim, cre, cim, lre, lim, dsk, hre_ref, him_ref, ys_ref, yg_ref, car_re, car_im, tabs):
        i = pl.program_id(0)

        @pl.when(i == 0)
        def _():
            car_re[...] = jnp.zeros_like(car_re)
            car_im[...] = jnp.zeros_like(car_im)
            _scan_tables(tabs, lre[...], lim[...], False, False)

        u = u_ref[...]
        for b in range(SSM_SUPER):
            ub = u[:, b * 128:(b + 1) * 128]
            st = slice(b * 512, (b + 1) * 512)
            hre_ref[:, st] = _dot(ub, bre[b], "nn")
            him_ref[:, st] = _dot(ub, bim[b], "nn")
        sr, si = _scan_rows(hre_ref, him_ref, tabs, (car_re[...], car_im[...]), chunk, False)
        car_re[...] = sr
        car_im[...] = si
        uf = u.astype(F32)
        for b in range(SSM_SUPER):
            st = slice(b * 512, (b + 1) * 512)
            ch = slice(b * 128, (b + 1) * 128)
            y = _dot(hre_ref[:, st].astype(BF16), cre[b], "nn") - _dot(him_ref[:, st].astype(BF16), cim[b], "nn")
            y = y + dsk[:, ch] * uf[:, ch]
            ys_ref[:, ch] = y
            yg_ref[:, ch] = _gelu(y).astype(BF16)

    full3 = lambda i: (0, 0, 0)
    full2 = lambda i: (0, 0)
    row = lambda i: (i, 0)
    u_col = COL_U // SSM_WIDTH
    return pl.pallas_call(
        body, name="ssm_fwd", grid=(s // chunk,),
        in_specs=[pl.BlockSpec((chunk, SSM_WIDTH), lambda i: (i, u_col)),
                  pl.BlockSpec((SSM_SUPER, 128, 512), full3), pl.BlockSpec((SSM_SUPER, 128, 512), full3),
                  pl.BlockSpec((SSM_SUPER, 512, 128), full3), pl.BlockSpec((SSM_SUPER, 512, 128), full3),
                  pl.BlockSpec((1, N_STATE), full2), pl.BlockSpec((1, N_STATE), full2), pl.BlockSpec((1, SSM_WIDTH), full2)],
        out_specs=[pl.BlockSpec((chunk, N_STATE), row), pl.BlockSpec((chunk, N_STATE), row),
                   pl.BlockSpec((chunk, SSM_WIDTH), row), pl.BlockSpec((chunk, SSM_WIDTH), row)],
        out_shape=[jax.ShapeDtypeStruct((s, N_STATE), F32), jax.ShapeDtypeStruct((s, N_STATE), F32),
                   jax.ShapeDtypeStruct((s, SSM_WIDTH), F32), jax.ShapeDtypeStruct((s, SSM_WIDTH), BF16)],
        scratch_shapes=[pltpu.VMEM((SUB, N_STATE), F32), pltpu.VMEM((SUB, N_STATE), F32),
                        pltpu.VMEM((N_SCAN_TABLES, SUB, N_STATE), F32)],
        compiler_params=_params(1),
    )(z, b_re, b_im, c_re, c_im, lam_re, lam_im, d_skip)


def _ssm_bwd(dys, z, h_re, h_im, b_re, b_im, c_re, c_im, lam_re, lam_im, d_skip, chunk):
    s = z.shape[0]
    n_chunks = s // chunk

    def body(dy_ref, u_ref, hre_ref, him_ref, hpr_ref, hpi_ref, bre, bim, cre, cim, lre, lim, dsk,
             du_ref, dlr_ref, dli_ref, dbr_ref, dbi_ref, dcr_ref, dci_ref, dd_ref, are, aim, car_re, car_im, tabs):
        i = pl.program_id(0)
        n = n_chunks - 1 - i

        @pl.when(i == 0)
        def _():
            car_re[...] = jnp.zeros_like(car_re)
            car_im[...] = jnp.zeros_like(car_im)
            _scan_tables(tabs, lre[...], lim[...], True, True)
            for r in (dlr_ref, dli_ref, dbr_ref, dbi_ref, dcr_ref, dci_ref, dd_ref):
                r[...] = jnp.zeros_like(r)

        dy = dy_ref[...]
        dyb = dy.astype(BF16)
        u = u_ref[...]
        for b in range(SSM_SUPER):
            ch = slice(b * 128, (b + 1) * 128)
            st = slice(b * 512, (b + 1) * 512)
            are[:, st] = _dot(dyb[:, ch], cre[b], "nt")
            aim[:, st] = -_dot(dyb[:, ch], cim[b], "nt")
        sr, si = _scan_rows(are, aim, tabs, (car_re[...], car_im[...]), chunk, True)
        car_re[...] = sr
        car_im[...] = si
        row_id = lax.broadcasted_iota(jnp.int32, (chunk, N_STATE), 0)
        top_scale = jnp.where(n > 0, 1.0, 0.0)
        h_r = hre_ref[...]
        h_i = him_ref[...]
        hp_r = jnp.where(row_id == 0, hpr_ref[SUB - 1:SUB, :] * top_scale, pltpu.roll(h_r, 1, 0))
        hp_i = jnp.where(row_id == 0, hpi_ref[SUB - 1:SUB, :] * top_scale, pltpu.roll(h_i, 1, 0))
        a_r = are[...]
        a_i = aim[...]
        dlr_ref[...] += jnp.sum(a_r * hp_r + a_i * hp_i, axis=0, keepdims=True)
        dli_ref[...] += jnp.sum(a_i * hp_r - a_r * hp_i, axis=0, keepdims=True)
        dd_ref[...] += jnp.sum(dy * u.astype(F32), axis=0, keepdims=True)
        a_rb = a_r.astype(BF16)
        a_ib = a_i.astype(BF16)
        h_rb = h_r.astype(BF16)
        h_ib = h_i.astype(BF16)
        for b in range(SSM_SUPER):
            ch = slice(b * 128, (b + 1) * 128)
            st = slice(b * 512, (b + 1) * 512)
            dbr_ref[b] += _dot(u[:, ch], a_rb[:, st], "tn")
            dbi_ref[b] += _dot(u[:, ch], a_ib[:, st], "tn")
            dcr_ref[b] += _dot(h_rb[:, st], dyb[:, ch], "tn")
            dci_ref[b] += -_dot(h_ib[:, st], dyb[:, ch], "tn")
            du = _dot(a_rb[:, st], bre[b], "nt") + _dot(a_ib[:, st], bim[b], "nt") + dsk[:, ch] * dy[:, ch]
            du_ref[:, ch] = du.astype(du_ref.dtype)

    full3 = lambda i: (0, 0, 0)
    full2 = lambda i: (0, 0)
    rev = lambda i: (n_chunks - 1 - i, 0)
    above = lambda i: (jnp.maximum((n_chunks - 1 - i) * (chunk // SUB) - 1, 0), 0)
    u_col = COL_U // SSM_WIDTH
    b_spec = pl.BlockSpec((SSM_SUPER, 128, 512), full3)
    c_spec = pl.BlockSpec((SSM_SUPER, 512, 128), full3)
    vec = pl.BlockSpec((1, N_STATE), full2)
    return pl.pallas_call(
        body, name="ssm_bwd", grid=(n_chunks,),
        in_specs=[pl.BlockSpec((chunk, SSM_WIDTH), rev),
                  pl.BlockSpec((chunk, SSM_WIDTH), lambda i: (n_chunks - 1 - i, u_col)),
                  pl.BlockSpec((chunk, N_STATE), rev), pl.BlockSpec((chunk, N_STATE), rev),
                  pl.BlockSpec((SUB, N_STATE), above), pl.BlockSpec((SUB, N_STATE), above),
                  b_spec, b_spec, c_spec, c_spec, vec, vec, pl.BlockSpec((1, SSM_WIDTH), full2)],
        out_specs=[pl.BlockSpec((chunk, SSM_WIDTH), rev), vec, vec, b_spec, b_spec, c_spec, c_spec,
                   pl.BlockSpec((1, SSM_WIDTH), full2)],
        out_shape=[jax.ShapeDtypeStruct((s, SSM_WIDTH), BF16),
                   jax.ShapeDtypeStruct((1, N_STATE), F32), jax.ShapeDtypeStruct((1, N_STATE), F32),
                   jax.ShapeDtypeStruct((SSM_SUPER, 128, 512), F32), jax.ShapeDtypeStruct((SSM_SUPER, 128, 512), F32),
                   jax.ShapeDtypeStruct((SSM_SUPER, 512, 128), F32), jax.ShapeDtypeStruct((SSM_SUPER, 512, 128), F32),
                   jax.ShapeDtypeStruct((1, SSM_WIDTH), F32)],
        scratch_shapes=[pltpu.VMEM((chunk, N_STATE), F32), pltpu.VMEM((chunk, N_STATE), F32),
                        pltpu.VMEM((SUB, N_STATE), F32), pltpu.VMEM((SUB, N_STATE), F32),
                        pltpu.VMEM((N_SCAN_TABLES, SUB, N_STATE), F32)],
        compiler_params=_params(1),
    )(dys, z, h_re, h_im, h_re, h_im, b_re, b_im, c_re, c_im, lam_re, lam_im, d_skip)


def _local_step(x, p, pos, tgt, sm, w_in_own, w_in_buf, late_bufs, place):
    s = x.shape[0]
    tm = min(512, s)
    ts = min(2048, s)
    chunk = min(256, s)
    ni = s // tm
    nk = s // ts
    g_mix, g_ffn, g_final = sm["g_mix"], sm["g_ffn"], sm["g_final"]

    tmb = min(1024, s)
    nib = s // tmb
    chip_w = IN_WIDTH // N_CHIPS
    w_start, w_forward, w_finish = _gather_stages(1)
    gather_in = dict(ins=[w_in_buf], outs=[jax.ShapeDtypeStruct(w_in_buf.shape, w_in_buf.dtype)], aliased=True,
                     sems=[3] * 4, stages=[(0, w_start), (nib - 1, w_forward), (nib - 1, w_finish)])
    a_rows = lambda i, j, k, pv: (i, 0)
    z_own, n1, w_in_all = _mm("in_proj_own", (nib, 1, 1),
                              [(x, (tmb, D_MODEL), a_rows, w_in_own, (D_MODEL, chip_w), lambda i, j, k, pv: (0, 0))], "nn",
                              [((s, IN_WIDTH), BF16, (tmb, chip_w), lambda i, j, k, pv: (i, pv[P_CHIP])),
                               ((s, D_MODEL), BF16, (tmb, D_MODEL), a_rows)],
                              extras=[(g_mix, (1, D_MODEL), lambda i, j, k, pv: (0, 0))],
                              epilogue=lambda acc, g: ((acc,), ()), prologue=_rms_fwd_tile, comm=gather_in, place=place)
    w_in = w_in_all.reshape(N_CHIPS, D_MODEL, chip_w)
    n_late = len(late_bufs)
    g_start, g_forward, g_finish = _gather_stages(n_late)
    in_steps = (N_CHIPS - 1) * nib
    gather = dict(ins=late_bufs, outs=[jax.ShapeDtypeStruct(b.shape, b.dtype) for b in late_bufs], aliased=True,
                  sems=[3 * n_late] * 4,
                  stages=[(0, g_start), ((4 * in_steps) // 5, g_forward), (in_steps - 1, g_finish)])
    other = lambda j, pv: (pv[P_CHIP] + 1 + j) % N_CHIPS
    z, *late = _mm("in_proj", (nib, N_CHIPS - 1, 1),
                   [(n1, (tmb, D_MODEL), a_rows, w_in, (None, D_MODEL, chip_w), lambda i, j, k, pv: (other(j, pv), 0, 0))],
                   "nn", [((s, IN_WIDTH), BF16, (tmb, chip_w), lambda i, j, k, pv: (i, other(j, pv)))], j_outer=True,
                   comm=gather, place=place, fill=z_own)
    w_ap, w_ga, w_gb, w_out, w_fg, w_fu, w_fd, w_pg, w_pp = (
        g.reshape(N_CHIPS, 2 * g.shape[2], g.shape[3]) for g in late)
    w_out2 = w_out.reshape(D_MODEL, D_MODEL)
    w_pg2 = w_pg.reshape(D_MODEL, D_MODEL)

    inv = ROPE_THETA ** (-jnp.arange(ROPE_HALF, dtype=F32) * 2.0 / ROPE_DIM)
    inv_row = jnp.concatenate([inv, inv, jnp.zeros((HEAD_DIM - ROPE_DIM,), F32)]).reshape(1, HEAD_DIM)
    tabs = _rope_tables(pos.astype(F32).reshape(s, 1), inv_row, tm)

    qk0, qkv_views = _qkv_layout(z, tabs, tm)
    v0_col = (2 * N_GROUPS * GROUP_WIDTH) // GROUP_WIDTH
    group_in = [((qk0, qk0, z), (0, 1, v0_col))] + [(trio, (0, 0, 0)) for trio in qkv_views]
    fwd_out = [_attn_fwd(*arrs, dil, cols3) for (arrs, cols3), dil in zip(group_in, GROUP_DILATIONS)]
    attn, lse, merged_views = _attn_merge(fwd_out[0][0], fwd_out[0][1], fwd_out[1:], tm)

    def chip_cols(parts):
        return (jnp.concatenate(parts, axis=1),), ()

    def proj_cols(name, a, width, w):
        blk = (None, width, 256)
        pairs = [(a, (tmb, width), lambda i, j, k: (i, 0), w, blk, lambda i, j, k: (0, 0, 0))]
        pairs += [(None, None, None, w, blk, (lambda i, j, k, q=q: (q, 0, 0))) for q in range(1, N_CHIPS)]
        return _mm(name, (nib, 1, 1), pairs, "nn", [((s, D_MODEL), BF16, (tmb, D_MODEL), lambda i, j, k: (i, 0))],
                   epilogue=chip_cols, sum_pairs=False)[0]

    def proj512(name, a, w):
        return proj_cols(name, a, GROUP_WIDTH, w)

    attn_d = proj512("attn_proj", attn, w_ap)

    bt_re = jnp.transpose(sm["b_re"], (0, 2, 1))
    bt_im = jnp.transpose(sm["b_im"], (0, 2, 1))
    log_dt_col = sm["log_dt"].reshape(SSM_GROUPS, 1)
    lam_re, lam_im, bbt_re, bbt_im = _ssm_prep(sm["a_re"], sm["a_im"], log_dt_col, bt_re, bt_im)
    b_re_m = _block_diag(bbt_re, SSM_GROUP, SSM_STATE).astype(BF16)
    b_im_m = _block_diag(bbt_im, SSM_GROUP, SSM_STATE).astype(BF16)
    c_re_m = _block_diag(jnp.transpose(sm["c_re"], (0, 2, 1)), SSM_STATE, SSM_GROUP).astype(BF16)
    c_im_m = _block_diag(jnp.transpose(sm["c_im"], (0, 2, 1)), SSM_STATE, SSM_GROUP).astype(BF16)
    lam_re_row = lam_re.reshape(1, N_STATE)
    lam_im_row = lam_im.reshape(1, N_STATE)
    d_skip_row = sm["d_skip"].reshape(1, SSM_WIDTH)
    h_re, h_im, ys, yg = _ssm_fwd(z, b_re_m, b_im_m, c_re_m, c_im_m, lam_re_row, lam_im_row, d_skip_row, min(2 * chunk, s))

    pa = proj512("glu_a", yg, w_ga)
    pb = proj512("glu_b", yg, w_gb)

    def mix_pro(ad, xr, g, ga, gs, a, b):
        ga, gs, ad, a, b = (t.astype(F32) for t in (ga, gs, ad, a, b))
        return _sig(ga) * ad + _sig(gs) * (a * _sig(b))

    def out_epi(acc, xr, g, *_):
        h1 = acc + xr
        return (h1, _rms_fwd_tile(h1, g)), ()

    m3 = lambda i, j, k: (i, 0)
    w3 = lambda i, j, k: (0, 0)
    tile_d = (tm, D_MODEL)
    h1, n2, mix = _mm("out_proj", (ni, 1, 1), [(attn_d, tile_d, m3, w_out2, (D_MODEL, D_MODEL), w3)], "nn",
                      [((s, D_MODEL), F32, tile_d, m3), ((s, D_MODEL), BF16, tile_d, m3), ((s, D_MODEL), BF16, tile_d, m3)],
                      epilogue=out_epi, prologue=mix_pro,
                      extras=[(x, tile_d, m3), (g_ffn, (1, D_MODEL), w3),
                              (z, tile_d, lambda i, j, k: (i, COL_GA // D_MODEL)), (z, tile_d, lambda i, j, k: (i, COL_GS // D_MODEL)),
                              (pa, tile_d, m3), (pb, tile_d, m3)])

    ffq = (None, tm, D_FF_Q)
    ffq_map = lambda i, j, k: (j, i, 0)

    def ffn_in_epi(parts):
        gts, ups = parts[0::2], parts[1::2]
        acts = [gt * _sig(gt) * u_ for gt, u_ in zip(gts, ups)]
        return (jnp.stack(gts, axis=0), jnp.stack(ups, axis=0), jnp.stack(acts, axis=0)), ()

    w_ffq = (None, D_MODEL, D_FF_Q)
    ff_pairs = []
    for q in range(N_CHIPS):
        blk_q = lambda i, j, k, q=q: (q, 0, 0)
        ff_pairs.append((n2, (tm, D_MODEL), m3, w_fg, w_ffq, blk_q) if q == 0 else (None, None, None, w_fg, w_ffq, blk_q))
        ff_pairs.append((None, None, None, w_fu, w_ffq, blk_q))
    ff_all = (N_CHIPS, tm, D_FF_Q)
    ff_all_map = lambda i, j, k: (0, i, 0)
    gate, up, act = _mm("ffn_gate_up", (ni, 1, 1), ff_pairs, "nn",
                        [((N_CHIPS, s, D_FF_Q), BF16, ff_all, ff_all_map)] * 3, epilogue=ffn_in_epi,
                        sum_pairs=False, resident_b=True)

    (h2,) = _mm("ffn_down", (nib, 1, 1),
                [(act, (None, tmb, D_FF_Q), (lambda i, j, k, q=q: (q, i, 0)), w_fd, (None, D_FF_Q, D_MODEL),
                  (lambda i, j, k, q=q: (q, 0, 0))) for q in range(N_CHIPS)], "nn",
                [((s, D_MODEL), F32, (tmb, D_MODEL), m3)], epilogue=lambda acc, hr: ((acc + hr,), ()),
                extras=[(h1, (tmb, D_MODEL), m3)])

    pp = proj_cols("ple_proj", p, PLE_DIM, w_pp)

    def ple_head_epi(acc, hr, ppr, t, g):
        sg = _sig(acc)
        ppf = ppr.astype(F32)
        h = hr + sg * ppf
        r = lax.rsqrt(jnp.mean(h * h, axis=-1, keepdims=True) + EPS)
        hhat = h * r
        diff = hhat * g - t
        loss = 0.5 * jnp.sum(jnp.mean(diff * diff, axis=-1, keepdims=True))
        dy = diff * (1.0 / D_MODEL)
        gy = dy * g
        dh = r * (gy - hhat * jnp.mean(gy * hhat, axis=-1, keepdims=True))
        return ((dh, dh * ppf * sg * (1.0 - sg), dh * sg),
                (jnp.full((SUB, 128), loss, F32), jnp.sum(dy * hhat, axis=0, keepdims=True)))

    tile_row = (tm, D_MODEL)
    dh3, dgl, dpp, loss_acc, dg_final = _mm(
        "ple_gate_head", (ni, 1, 1), [(h2, tile_row, m3, w_pg2, (D_MODEL, D_MODEL), w3)], "nn",
        [((s, D_MODEL), F32, tile_row, m3), ((s, D_MODEL), BF16, tile_row, m3), ((s, D_MODEL), BF16, tile_row, m3)],
        epilogue=ple_head_epi,
        extras=[(h2, tile_row, m3), (pp, tile_row, m3), (tgt, tile_row, m3), (g_final, (1, D_MODEL), w3)],
        acc_outs=[((SUB, 128), F32), ((1, D_MODEL), F32)])

    def wgrad(name, a, a_block, a_imap, b, b_block, b_imap, out_shape, out_block, out_imap, nj, acc_shape):
        return _mm(name, (1, nj, nk), [(a, a_block, a_imap, b, b_block, b_imap)], "tn",
                   [(out_shape, F32, out_block, out_imap)], acc_shape=acc_shape)[0]

    tk0 = lambda i, j, k: (k, 0)
    tkj = lambda i, j, k: (k, j)
    def wgrad_cols(name, a, width, dy_):
        def split(acc):
            return (jnp.stack([acc[:, q * 256:(q + 1) * 256] for q in range(N_CHIPS)], axis=0),), ()

        return _mm(name, (1, 1, nk), [(a, (ts, width), tk0, dy_, (ts, D_MODEL), tk0)], "tn",
                   [((N_CHIPS, width, 256), F32, (N_CHIPS, width, 256), lambda i, j, k: (0, 0, 0))], epilogue=split,
                   acc_shape=(width, D_MODEL))[0]

    d_w_pp = wgrad_cols("d_ple_proj", p, PLE_DIM, dpp)
    d_w_pg = wgrad("d_ple_gate", h2, (ts, D_MODEL), tk0, dgl, (ts, D_MODEL), tk0, (D_MODEL, D_MODEL),
                   (D_MODEL, D_MODEL), w3, 1, (D_MODEL, D_MODEL))

    (dh2,) = _mm("ple_gate_bwd", (nib, 1, 1), [(dgl, (tmb, D_MODEL), m3, w_pg2, (D_MODEL, D_MODEL), w3)], "nt",
                 [((s, D_MODEL), F32, (tmb, D_MODEL), m3)], epilogue=lambda acc, d_: ((acc + d_,), ()),
                 extras=[(dh3, (tmb, D_MODEL), m3)])

    def ffn_bwd_epi(parts, gt_all, u_all):
        dgs_, dus_ = [], []
        for q, dact in enumerate(parts):
            gt, u_ = gt_all[q].astype(F32), u_all[q].astype(F32)
            sg = _sig(gt)
            dgs_.append(dact * u_ * (sg * (1.0 + gt * (1.0 - sg))))
            dus_.append(dact * gt * sg)
        return (jnp.stack(dgs_, axis=0), jnp.stack(dus_, axis=0)), ()

    fd_pairs = [((dh2, (tm, D_MODEL), m3) if q == 0 else (None, None, None))
                + (w_fd, (None, D_FF_Q, D_MODEL), (lambda i, j, k, q=q: (q, 0, 0))) for q in range(N_CHIPS)]
    dgate, dup = _mm("ffn_down_bwd", (ni, 1, 1), fd_pairs, "nt",
                     [((N_CHIPS, s, D_FF_Q), BF16, ff_all, ff_all_map)] * 2, epilogue=ffn_bwd_epi,
                     extras=[(gate, ff_all, ff_all_map), (up, ff_all, ff_all_map)], sum_pairs=False, resident_b=True)

    ffq_t = (None, ts, D_FF_Q)
    ffq_tmap = lambda i, j, k: (j, k, 0)
    blk_j = lambda i, j, k: (j, 0, 0)
    d_w_fd = wgrad("d_ffn_down", act, ffq_t, ffq_tmap, dh2, (ts, D_MODEL), tk0, (N_CHIPS, D_FF_Q, D_MODEL),
                   (None, D_FF_Q, D_MODEL), blk_j, N_CHIPS, (D_FF_Q, D_MODEL))
    d_w_fg = wgrad("d_ffn_gate", n2, (ts, D_MODEL), tk0, dgate, ffq_t, ffq_tmap, (N_CHIPS, D_MODEL, D_FF_Q),
                   (None, D_MODEL, D_FF_Q), blk_j, N_CHIPS, (D_MODEL, D_FF_Q))
    d_w_fu = wgrad("d_ffn_up", n2, (ts, D_MODEL), tk0, dup, ffq_t, ffq_tmap, (N_CHIPS, D_MODEL, D_FF_Q),
                   (None, D_MODEL, D_FF_Q), blk_j, N_CHIPS, (D_MODEL, D_FF_Q))

    def norm_bwd_epi(acc, h, d_res, g):
        dh, dg = _rms_bwd_tile(acc, h, g)
        return (d_res + dh,), (dg,)

    fi_pairs = []
    for q in range(N_CHIPS):
        a_q = lambda i, j, k, q=q: (q, i, 0)
        b_q = lambda i, j, k, q=q: (q, 0, 0)
        fi_pairs.append((dgate, ffq, a_q, w_fg, (None, D_MODEL, D_FF_Q), b_q))
        fi_pairs.append((dup, ffq, a_q, w_fu, (None, D_MODEL, D_FF_Q), b_q))
    dh1, dg_ffn = _mm("ffn_in_bwd", (ni, 1, 1), fi_pairs, "nt",
                      [((s, D_MODEL), F32, (tm, D_MODEL), m3)], epilogue=norm_bwd_epi,
                      extras=[(h1, (tm, D_MODEL), m3), (dh2, (tm, D_MODEL), m3), (g_ffn, (1, D_MODEL), w3)],
                      acc_outs=[((1, D_MODEL), F32)], resident_b=True)

    d_w_out = wgrad("d_out_proj", mix, (ts, D_MODEL), tk0, dh1, (ts, D_MODEL), tk0, (D_MODEL, D_MODEL),
                    (D_MODEL, D_MODEL), w3, 1, (D_MODEL, D_MODEL))

    def mix_bwd_epi(dm, ga, gs, ad, a, b):
        ga, gs, ad, a, b = (t.astype(F32) for t in (ga, gs, ad, a, b))
        s_a, s_s, s_b = _sig(ga), _sig(gs), _sig(b)
        d_ssm = dm * s_s
        return (dm * ad * s_a * (1.0 - s_a), dm * (a * s_b) * s_s * (1.0 - s_s), dm * s_a, d_ssm * s_b,
                d_ssm * a * s_b * (1.0 - s_b)), ()

    tile_m = (tm, D_MODEL)
    dga, dgs, dattn_d, dpa, dpb = _mm(
        "out_proj_bwd", (ni, 1, 1), [(dh1, tile_m, m3, w_out2, (D_MODEL, D_MODEL), w3)], "nt",
        [((s, D_MODEL), BF16, tile_m, m3)] * 5, epilogue=mix_bwd_epi,
        extras=[(z, tile_m, lambda i, j, k: (i, COL_GA // D_MODEL)), (z, tile_m, lambda i, j, k: (i, COL_GS // D_MODEL)),
                (attn_d, tile_m, m3), (pa, tile_m, m3), (pb, tile_m, m3)])

    d_w_ap = wgrad_cols("d_attn_proj", attn, GROUP_WIDTH, dattn_d)
    d_w_ga = wgrad_cols("d_glu_a", yg, GROUP_WIDTH, dpa)
    d_w_gb = wgrad_cols("d_glu_b", yg, GROUP_WIDTH, dpb)

    ik = lambda i, j, k: (i, k)

    def cols_bwd(dy_, w):
        return [(dy_, (tmb, 256), (lambda i, j, k, q=q: (i, q)), w, (None, GROUP_WIDTH, 256),
                 (lambda i, j, k, q=q: (q, 0, 0))) for q in range(N_CHIPS)]

    (dattn,) = _mm("attn_proj_bwd", (nib, 1, 1), cols_bwd(dattn_d, w_ap), "nt",
                   [((s, GROUP_WIDTH), BF16, (tmb, GROUP_WIDTH), m3)])

    (dys,) = _mm("glu_bwd", (nib, 1, 1), cols_bwd(dpa, w_ga) + cols_bwd(dpb, w_gb), "nt",
                 [((s, GROUP_WIDTH), F32, (tmb, GROUP_WIDTH), m3)],
                 epilogue=lambda acc, y_: ((acc * _gelu_grad(y_),), ()),
                 extras=[(ys, (tmb, GROUP_WIDTH), m3)])

    du, d_lr, d_li, d_bre, d_bim, d_cre, d_cim, d_dskip = _ssm_bwd(
        dys, z, h_re, h_im, b_re_m, b_im_m, c_re_m, c_im_m, lam_re_row, lam_im_row, d_skip_row, chunk)

    dattn_views = _to_views(dattn, tm)
    bwd_in = [(dattn, attn, lse)] + [(dv_, ov_, lv_) for dv_, (ov_, lv_) in zip(dattn_views, merged_views)]
    qkv_grads = [_attn_bwd(*arrs, *dol, dil, cols3)
                 for (arrs, cols3), dol, dil in zip(group_in, bwd_in, GROUP_DILATIONS)]
    early = [d_w_ap, d_w_ga, d_w_gb, d_w_out.reshape(N_CHIPS, D_MODEL // N_CHIPS, D_MODEL), d_w_fg, d_w_fu, d_w_fd,
             d_w_pg.reshape(N_CHIPS, D_MODEL // N_CHIPS, D_MODEL), d_w_pp]
    early5 = [g.reshape(N_CHIPS, 2, g.shape[1] // 2, g.shape[2]) for g in early]
    n_e = len(early5)
    p_start, p_finish = _pair_exchange_stages(n_e)
    dz, early_theirs = _dz_layout(
        qkv_grads, du, dga, dgs, tabs, tm,
        comm=dict(ins=early5, outs=_pair_exchange_shapes(early5), sems=[N_CHIPS * n_e] * 2, start=p_start, finish=p_finish))
    early_parts = [_pair_sum(g, t, place) for g, t in zip(early5, early_theirs)]

    chip_in = IN_WIDTH // N_CHIPS
    ip_pairs = [(dz, (tm, chip_in), (lambda i, j, k, q=q: (i, q)), w_in, (None, D_MODEL, chip_in),
                 (lambda i, j, k, q=q: (q, 0, 0))) for q in range(N_CHIPS)]
    grad_x, dg_mix = _mm("in_proj_bwd", (ni, 1, 1), ip_pairs, "nt",
                         [((s, D_MODEL), F32, (tm, D_MODEL), m3)], epilogue=norm_bwd_epi,
                         extras=[(x, (tm, D_MODEL), m3), (dh1, (tm, D_MODEL), m3), (g_mix, (1, D_MODEL), w3)],
                         acc_outs=[((1, D_MODEL), F32)], resident_b=True)

    d_bbt_re = _block_diag_t(d_bre, SSM_GROUP, SSM_STATE)
    d_bbt_im = _block_diag_t(d_bim, SSM_GROUP, SSM_STATE)
    d_a_re, d_a_im, d_log_dt, d_bt_re, d_bt_im = _ssm_param_bwd(
        sm["a_re"], sm["a_im"], log_dt_col, bt_re, bt_im,
        d_lr.reshape(SSM_GROUPS, SSM_STATE), d_li.reshape(SSM_GROUPS, SSM_STATE), d_bbt_re, d_bbt_im)
    small = {
        "g_mix": dg_mix, "a_re": d_a_re, "a_im": d_a_im, "log_dt": d_log_dt,
        "b_re": jnp.transpose(d_bt_re, (0, 2, 1)), "b_im": jnp.transpose(d_bt_im, (0, 2, 1)),
        "c_re": jnp.transpose(_block_diag_t(d_cre, SSM_STATE, SSM_GROUP), (0, 2, 1)),
        "c_im": jnp.transpose(_block_diag_t(d_cim, SSM_STATE, SSM_GROUP), (0, 2, 1)),
        "d_skip": d_dskip, "g_ffn": dg_ffn, "g_final": dg_final,
    }
    vec = _pack([small[n] for n in SMALL] + [loss_acc[0, 0].reshape(1)])

    x_start, x_finish = _chip_exchange_stages(n_e)
    v_start, v_finish = _all_exchange_stages()

    def both(f_chips, f_vec):
        def stage(ins, outs, sems):
            f_chips(ins[:n_e], outs[:n_e], sems[:2])
            f_vec(ins[n_e:], outs[n_e:], sems[2:])
        return stage

    half_in = chip_in // 2
    win_steps = 8 * nk
    exchange = dict(ins=early_parts + [vec],
                    outs=[jax.ShapeDtypeStruct(t.shape, t.dtype) for t in early_parts]
                    + [jax.ShapeDtypeStruct((8,) + vec.shape, vec.dtype)],
                    aliased=False, sems=[3 * n_e, 3 * n_e, 7, 7],
                    stages=[(0, both(x_start, v_start)), (win_steps - 1, both(x_finish, v_finish))])
    d_w_in, *got = _mm("d_in_proj", (1, 8, nk), [(n1, (ts, D_MODEL), tk0, dz, (ts, half_in), tkj)], "tn",
                       [((N_CHIPS, D_MODEL, chip_in), F32, (None, D_MODEL, half_in), lambda i, j, k: (j // 2, 0, j % 2))],
                       acc_shape=(D_MODEL, half_in), comm=exchange)
    return grad_x, d_w_in, early_parts, got[:n_e], vec, got[n_e]


BIG = ("w_in", "w_attn_proj", "w_glu_a", "w_glu_b", "w_out", "w_ffn_gate", "w_ffn_up", "w_ffn_down", "w_ple_gate",
       "w_ple_proj")
SMALL = ("g_mix", "a_re", "a_im", "log_dt", "b_re", "b_im", "c_re", "c_im", "d_skip", "g_ffn", "g_final")
ANY = pl.BlockSpec(memory_space=pl.ANY)


def _place():
    x, y, c = lax.axis_index("x"), lax.axis_index("y"), lax.axis_index("c")
    chips = [(1 - x, y), (x, 1 - y), (1 - x, 1 - y)]
    return x, y, c, chips


def _remote(src, dst, send_sem, recv_sem, to):
    return pltpu.make_async_remote_copy(src_ref=src, dst_ref=dst, send_sem=send_sem, recv_sem=recv_sem, device_id=to,
                                        device_id_type=MESH)


def _comm_call(name, body, ins, out_shapes, n_sems, aliases=None):
    n_w = len(ins)
    return pl.pallas_call(
        body, name=name, in_specs=[ANY] * n_w, out_specs=[ANY] * len(out_shapes), out_shape=out_shapes,
        scratch_shapes=[pltpu.SemaphoreType.DMA((n,)) for n in n_sems], input_output_aliases=aliases or {},
    )(*ins)


def _gather_stages(n_w):
    def each():
        x, y, c, chips = _place()
        for w in range(n_w):
            for j, (cx, cy) in enumerate(chips):
                yield w, 3 * w + j, 2 * x + y, 2 * cx + cy, (cx, cy, c), (x, y, 1 - c), c

    def start(ins, outs, sems):
        for w, k, me, _, peer, _, c in each():
            mine = outs[w].at[me, c]
            _remote(mine, mine, sems[0].at[k], sems[1].at[k], peer).start()

    def forward(ins, outs, sems):
        for w, k, _, src_chip, peer, sib, c in each():
            landed = outs[w].at[src_chip, c]
            _remote(landed, landed, sems[0].at[k], sems[1].at[k], peer).wait_recv()
            _remote(landed, landed, sems[2].at[k], sems[3].at[k], sib).start()

    def finish(ins, outs, sems):
        for w, k, me, src_chip, peer, sib, c in each():
            other = outs[w].at[src_chip, 1 - c]
            _remote(other, other, sems[2].at[k], sems[3].at[k], sib).wait_recv()
        for w, k, me, src_chip, peer, sib, c in each():
            mine = outs[w].at[me, c]
            _remote(mine, mine, sems[0].at[k], sems[1].at[k], peer).wait_send()
            landed = outs[w].at[src_chip, c]
            _remote(landed, landed, sems[2].at[k], sems[3].at[k], sib).wait_send()

    return start, forward, finish


def _pair_exchange(grads):
    n_w = len(grads)
    start, finish = _pair_exchange_stages(n_w)

    def body(*refs):
        ins, outs, sems = refs[:n_w], refs[n_w:2 * n_w], refs[2 * n_w:]
        start(ins, outs, sems)
        finish(ins, outs, sems)

    return _comm_call("grad_pair_exchange", body, grads, _pair_exchange_shapes(grads), [N_CHIPS * n_w] * 2)


def _pair_exchange_shapes(grads):
    return [jax.ShapeDtypeStruct((N_CHIPS,) + g.shape[2:], g.dtype) for g in grads]


def _pair_exchange_stages(n_w):
    def each():
        x, y, c, _ = _place()
        for w in range(n_w):
            for q in range(N_CHIPS):
                yield w, q, N_CHIPS * w + q, c, (x, y, 1 - c)

    def start(ins, outs, sems):
        for w, q, k, c, sib in each():
            _remote(ins[w].at[q, 1 - c], outs[w].at[q], sems[0].at[k], sems[1].at[k], sib).start()

    def finish(ins, outs, sems):
        for w, q, k, c, sib in each():
            _remote(ins[w].at[q, 1 - c], outs[w].at[q], sems[0].at[k], sems[1].at[k], sib).wait()

    return start, finish


def _chip_exchange(parts):
    n_w = len(parts)

    start, finish = _chip_exchange_stages(n_w)

    def body(*refs):
        ins, outs, sems = refs[:n_w], refs[n_w:2 * n_w], refs[2 * n_w:]
        start(ins, outs, sems)
        finish(ins, outs, sems)

    out_shapes = [jax.ShapeDtypeStruct(t.shape, t.dtype) for t in parts]
    return _comm_call("grad_chip_exchange", body, parts, out_shapes, [3 * n_w, 3 * n_w])


def _chip_exchange_stages(n_w):
    def each():
        x, y, c, chips = _place()
        for w in range(n_w):
            for j, (cx, cy) in enumerate(chips):
                yield w, 3 * w + j, 2 * x + y, 2 * cx + cy, (cx, cy, c)

    def start(ins, outs, sems):
        for w, k, me, peer_chip, peer in each():
            _remote(ins[w].at[peer_chip], outs[w].at[me], sems[0].at[k], sems[1].at[k], peer).start()

    def finish(ins, outs, sems):
        for w, k, me, peer_chip, peer in each():
            got = outs[w].at[peer_chip]
            _remote(got, got, sems[0].at[k], sems[1].at[k], peer).wait_recv()
        for w, k, me, peer_chip, peer in each():
            _remote(ins[w].at[peer_chip], outs[w].at[me], sems[0].at[k], sems[1].at[k], peer).wait_send()

    return start, finish


def _pair_gather(halves):
    n_w = len(halves)

    def body(*refs):
        ins, outs = refs[:n_w], refs[n_w:2 * n_w]
        send, recv = refs[2 * n_w:]
        x, y, c, _ = _place()
        sib = (x, y, 1 - c)
        cps = []
        for w in range(n_w):
            cp = _remote(ins[w], outs[w], send.at[w], recv.at[w], sib)
            cp.start()
            cps.append(cp)
        for cp in cps:
            cp.wait()

    out_shapes = [jax.ShapeDtypeStruct(h.shape, h.dtype) for h in halves]
    return _comm_call("grad_pair_gather", body, halves, out_shapes, [n_w] * 2)


def _all_exchange_stages():
    def each():
        x, y, c, _ = _place()
        for k in range(1, 8):
            px, py, pc = x ^ ((k >> 2) & 1), y ^ ((k >> 1) & 1), c ^ (k & 1)
            yield k - 1, 4 * x + 2 * y + c, 4 * px + 2 * py + pc, (px, py, pc)

    def start(ins, outs, sems):
        for k, me, _, peer in each():
            _remote(ins[0], outs[0].at[me], sems[0].at[k], sems[1].at[k], peer).start()

    def finish(ins, outs, sems):
        for k, me, src, peer in each():
            got = outs[0].at[src]
            _remote(got, got, sems[0].at[k], sems[1].at[k], peer).wait_recv()
        for k, me, src, peer in each():
            _remote(ins[0], outs[0].at[me], sems[0].at[k], sems[1].at[k], peer).wait_send()

    return start, finish


def _row_tile(r):
    for t in (256, 128, 176, 64, 32, 16, 8):
        if r % t == 0:
            return t
    return r


P_C, P_CHIP, P_DEV = 2, 3, 4


def _cast_shard(w2):
    r, c = w2.shape
    t = _row_tile(r)
    blk, imap = _rows(t, c)
    return _ew("cast_own", (r // t,), [(w2, blk, imap)], [((r, c), BF16, blk, imap)], lambda pids, a: ((a,), ()))[0]


def _cast_into_slot(w2, place):
    r, c = w2.shape
    t = _row_tile(r)
    return _ew("cast_shard", (r // t,), [(w2, (t, c), lambda i, pv: (i, 0))],
               [((N_CHIPS, r, c), BF16, (None, t, c), lambda i, pv: (pv[P_CHIP], i, 0))],
               lambda pids, a: ((a,), ()), place=place)[0]


def _pair_sum(mine, theirs, place):
    _, r, c = theirs.shape
    t = _row_tile(r)
    own = ((None, None, t, c), lambda q, i, pv: (q, pv[P_C], i, 0))
    blk = ((None, t, c), lambda q, i, pv: (q, i, 0))
    return _ew("grad_pair_sum", (N_CHIPS, r // t), [(mine, *own), (theirs, *blk)], [((N_CHIPS, r, c), BF16, *blk)],
               lambda pids, a, b: ((a + b,), ()), place=place)[0]


def _chip_sum(own, got, place):
    _, r, c = own.shape
    t = _row_tile(r)
    ins = []
    for q in range(N_CHIPS):
        ins.append((own, (None, t, c), (lambda i, pv, q=q: (q, i, 0))))
        ins.append((got, (None, t, c), (lambda i, pv, q=q: (jnp.where(pv[P_CHIP] == q, (q + 1) % N_CHIPS, q), i, 0))))

    def fn(pids, *tiles):
        me = pids[0][P_CHIP]
        tot = None
        for q in range(N_CHIPS):
            term = jnp.where(me == q, tiles[2 * q], tiles[2 * q + 1]).astype(F32)
            tot = term if tot is None else tot + term
        return (tot,), ()

    return _ew("grad_chip_sum", (r // t,), ins, [((r, c), F32, (t, c), lambda i, pv: (i, 0))], fn, place=place)[0]


def _adamw_tile(w, g, m, v):
    m = ADAM_B1 * m + (1.0 - ADAM_B1) * g
    v = ADAM_B2 * v + (1.0 - ADAM_B2) * (g * g)
    m_hat = m / (1.0 - ADAM_B1 ** ADAM_STEP)
    v_hat = v / (1.0 - ADAM_B2 ** ADAM_STEP)
    delta = -ADAM_LR * (m_hat / (jnp.sqrt(v_hat) + ADAM_EPS) + ADAM_WD * w)
    return delta, m, v


def _adamw(name, g2, w2, m2, v2):
    r, c = w2.shape
    t = _row_tile(r)
    blk, imap = _rows(t, c)

    def fn(pids, g, w, m, v):
        delta, nm, nv = _adamw_tile(w, g, m, v)
        return (g, delta, nm, nv), ()

    return _ew(name, (r // t,), [(a, blk, imap) for a in (g2, w2, m2, v2)], [((r, c), F32, blk, imap)] * 4, fn)


def _adamw_halves(name, mine, theirs, w2, m2, v2, place):
    r, c = w2.shape
    t = _row_tile(r // 2)
    n_t = (r // 2) // t
    half = ((t, c), lambda h, i, pv: (i, 0))
    whole = ((t, c), lambda h, i, pv: (h * n_t + i, 0))

    def fn(pids, ga, gb, w, m, v):
        g = jnp.where(pids[1] == pids[0][P_C], ga, gb)
        delta, nm, nv = _adamw_tile(w, g, m, v)
        return (g, delta, nm, nv), ()

    return _ew(name, (2, n_t), [(mine, *half), (theirs, *half), (w2, *whole), (m2, *whole), (v2, *whole)],
               [((r, c), F32, *whole)] * 4, fn, place=place)


def _device_sum(own, got, place):
    r, c = own.shape
    t = _row_tile(r)
    ins = [(own, (t, c), lambda i, pv: (i, 0))]
    for q in range(8):
        ins.append((got, (None, t, c), (lambda i, pv, q=q: (jnp.where(pv[P_DEV] == q, (q + 1) % 8, q), i, 0))))

    def fn(pids, mine, *parts):
        me = pids[0][P_DEV]
        tot = None
        for q in range(8):
            term = jnp.where(me == q, mine, parts[q])
            tot = term if tot is None else tot + term
        return (tot,), ()

    return _ew("small_device_sum", (r // t,), ins, [((r, c), F32, (t, c), lambda i, pv: (i, 0))], fn, place=place)[0]


def _pack(parts):
    flat = jnp.concatenate([a.reshape(-1) for a in parts])
    pad = (-flat.shape[0]) % (SUB * 128)
    return jnp.pad(flat, (0, pad)).reshape(-1, 128)


def _unpack(mat, shapes):
    flat = mat.reshape(-1)
    out, off = [], 0
    for shp in shapes:
        n = math.prod(shp)
        out.append(flat[off:off + n].reshape(shp))
        off += n
    return out


def kernel(x, p, positions, g_mix, w_in, a_re, a_im, log_dt, b_re, b_im, c_re, c_im, d_skip, w_attn_proj, w_glu_a, w_glu_b, w_out, g_ffn, w_ffn_gate, w_ffn_up, w_ffn_down, w_ple_gate, w_ple_proj, g_final, loss_target, m_g_mix, m_w_in, m_a_re, m_a_im, m_log_dt, m_b_re, m_b_im, m_c_re, m_c_im, m_d_skip, m_w_attn_proj, m_w_glu_a, m_w_glu_b, m_w_out, m_g_ffn, m_w_ffn_gate, m_w_ffn_up, m_w_ffn_down, m_w_ple_gate, m_w_ple_proj, m_g_final, v_g_mix, v_w_in, v_a_re, v_a_im, v_log_dt, v_b_re, v_b_im, v_c_re, v_c_im, v_d_skip, v_w_attn_proj, v_w_glu_a, v_w_glu_b, v_w_out, v_g_ffn, v_w_ffn_gate, v_w_ffn_up, v_w_ffn_down, v_w_ple_gate, v_w_ple_proj, v_g_final):
    given = dict(locals())
    big_w = {n: given[n] for n in BIG}
    w_mats = {n: big_w[n].reshape(big_w[n].shape[1:]) for n in BIG}

    ax, ay, ac = lax.axis_index("x"), lax.axis_index("y"), lax.axis_index("c")
    place = jnp.stack([ax, ay, ac, 2 * ax + ay, 4 * ax + 2 * ay + ac]).astype(jnp.int32)

    bufs = []
    for n in BIG:
        r, c = w_mats[n].shape
        bufs.append(_cast_into_slot(w_mats[n], place).reshape(N_CHIPS, 2, r // 2, c))
    w_in_own = _cast_shard(w_mats["w_in"])

    sm = {
        "g_mix": g_mix.reshape(1, D_MODEL), "g_ffn": g_ffn.reshape(1, D_MODEL), "g_final": g_final.reshape(1, D_MODEL),
        "a_re": a_re[0], "a_im": a_im[0], "log_dt": log_dt[0], "b_re": b_re[0], "b_im": b_im[0], "c_re": c_re[0],
        "c_im": c_im[0], "d_skip": d_skip[0],
    }
    s = x.shape[1]
    grad_x, d_w_in, early_parts, early_got, vec, vec_got = _local_step(
        x[0], p[0, 0], positions[0], loss_target[0], sm, w_in_own, bufs[0], bufs[1:], place)

    r_in, c_in = w_mats["w_in"].shape
    g5_in = [d_w_in.reshape(N_CHIPS, 2, r_in // 2, c_in)]
    in_parts = [_pair_sum(g, t, place) for g, t in zip(g5_in, _pair_exchange(g5_in))]
    chip_parts = in_parts + list(early_parts)
    chip_got = list(_chip_exchange(in_parts)) + list(early_got)
    halves = [_chip_sum(own, got, place) for own, got in zip(chip_parts, chip_got)]
    other_halves = _pair_gather(halves)

    results = {}
    for n, mine, other in zip(BIG, halves, other_halves):
        r, c = w_mats[n].shape
        shp = big_w[n].shape
        outs = _adamw_halves("adamw_" + n, mine, other, w_mats[n], given["m_" + n].reshape(r, c),
                             given["v_" + n].reshape(r, c), place)
        results[n] = [o.reshape(shp) for o in outs]

    small_shapes = [given[n].shape for n in SMALL]
    tot = _device_sum(vec, vec_got, place)
    n_small = sum(math.prod(shp) for shp in small_shapes)
    loss = tot.reshape(-1)[n_small]
    w_s = _pack([given[n] for n in SMALL])
    m_s = _pack([given["m_" + n] for n in SMALL])
    v_s = _pack([given["v_" + n] for n in SMALL])
    rows_s = w_s.shape[0]
    g_s = tot.reshape(-1)[: rows_s * 128].reshape(rows_s, 128)
    outs_s = _adamw("adamw_small", g_s, w_s, m_s, v_s)
    for kind, mat in enumerate(outs_s):
        for n, arr in zip(SMALL, _unpack(mat, small_shapes)):
            results.setdefault(n, [None] * 4)[kind] = arr

    order = ("g_mix", "w_in", "a_re", "a_im", "log_dt", "b_re", "b_im", "c_re", "c_im", "d_skip", "w_attn_proj", "w_glu_a",
             "w_glu_b", "w_out", "g_ffn", "w_ffn_gate", "w_ffn_up", "w_ffn_down", "w_ple_gate", "w_ple_proj", "g_final")
    out = [loss, grad_x.reshape(1, s, D_MODEL)]
    for kind in range(4):
        out += [results[n][kind] for n in order]
    return tuple(out)
```

```python
import math

import jax
import jax.numpy as jnp
from jax import lax
from jax.experimental import pallas as pl
from jax.experimental.pallas import tpu as pltpu

F32 = jnp.float32
BF16 = jnp.bfloat16

D_MODEL = 1024
HEAD_DIM = 128
HEADS_PER_GROUP = 4
GROUP_WIDTH = HEADS_PER_GROUP * HEAD_DIM
GROUP_DILATIONS = (1, 4, 16)
N_GROUPS = len(GROUP_DILATIONS)
LSE_LANES = 32
LSE_WIDTH = HEADS_PER_GROUP * LSE_LANES
ATTN_BLOCK = 128
ROPE_DIM = 32
ROPE_HALF = 16
ROPE_THETA = 500000.0
SSM_WIDTH = 512
SSM_GROUPS = 32
SSM_GROUP = 16
SSM_STATE = 64
N_STATE = SSM_GROUPS * SSM_STATE
SSM_SUPER = 4
IN_WIDTH = 7168
COL_U = 4608
COL_GA = 5120
COL_GS = 6144
D_FF = 2816
N_CHIPS = 4
D_FF_Q = D_FF // N_CHIPS
PLE_DIM = 256
EPS = 1e-6
ADAM_LR = 0.001
ADAM_B1 = 0.9
ADAM_B2 = 0.999
ADAM_EPS = 1e-08
ADAM_WD = 0.01
ADAM_STEP = 10
NEG_BIG = -1e30
VMEM_LIMIT_BYTES = 56 * 1024 * 1024
MESH = pl.DeviceIdType.MESH

_DIMS = {
    "nn": (((1,), (0,)), ((), ())),
    "nt": (((1,), (1,)), ((), ())),
    "tn": (((0,), (0,)), ((), ())),
}


def _params(n_grid):
    return pltpu.CompilerParams(dimension_semantics=("arbitrary",) * n_grid, vmem_limit_bytes=VMEM_LIMIT_BYTES)


def _sig(v):
    return 1.0 / (1.0 + jnp.exp(-v))


def _dot(a, b, mode):
    return lax.dot_general(a, b, _DIMS[mode], preferred_element_type=F32)


def _mm(name, grid, pairs, mode, outs, epilogue=None, extras=(), acc_outs=(), acc_shape=None, j_outer=False,
        sum_pairs=True, resident_b=False, comm=None, place=None, fill=None, prologue=None):
    gi, gj, gk = grid
    n_p, n_e, n_o, n_a = len(pairs), len(extras), len(outs), len(acc_outs)
    assert not n_a or gj == 1
    assert sum_pairs or gk == 1
    run_grid = (gj, gi, gk) if j_outer else grid
    c_ins = list(comm["ins"]) if comm else []
    c_outs = list(comm["outs"]) if comm else []
    c_sems = list(comm["sems"]) if comm else []
    n_ci, n_co, n_cs = len(c_ins), len(c_outs), len(c_sems)
    n_s = 0 if place is None else 1
    n_fill = 0 if fill is None else 1

    def order(imap):
        if place is None:
            return (lambda j, i, k: imap(i, j, k)) if j_outer else imap
        return (lambda j, i, k, pv: imap(i, j, k, pv)) if j_outer else imap

    shared_a = [pr[0] is None for pr in pairs]
    n_in = 2 * n_p - sum(shared_a)

    def body(*refs):
        refs = refs[n_s:]
        pair_refs = list(refs[:n_in])
        extra_refs = refs[n_in: n_in + n_e]
        comm_in = refs[n_in + n_e: n_in + n_e + n_ci]
        at = n_in + n_e + n_ci + n_fill
        out_refs = refs[at: at + n_o]
        sum_refs = refs[at + n_o: at + n_o + n_a]
        comm_out = refs[at + n_o + n_a: at + n_o + n_a + n_co]
        scratch_refs = refs[at + n_o + n_a + n_co:]
        i = pl.program_id(1 if j_outer else 0)
        k = pl.program_id(2)
        if comm:
            step = (pl.program_id(0) * run_grid[1] + pl.program_id(1)) * run_grid[2] + pl.program_id(2)
            sems = scratch_refs[len(scratch_refs) - n_cs:]
            for at_step, stage in comm["stages"]:
                @pl.when(step == at_step)
                def _(stage=stage):
                    stage(comm_in, comm_out, sems)
        part = None if sum_pairs else []
        a = None
        for t in range(n_p):
            if not shared_a[t]:
                a = pair_refs.pop(0)[...]
                if prologue is not None and t == 0:
                    a = prologue(a, *[e[...] for e in extra_refs]).astype(BF16)
                    out_refs[n_o - 1][...] = a
                a = a.astype(BF16)
            b = pair_refs.pop(0)[...].astype(BF16)
            d = _dot(a, b, mode)
            if sum_pairs:
                part = d if part is None else part + d
            else:
                part.append(d)

        def finish(acc):
            tiles, sums = epilogue(acc, *[e[...] for e in extra_refs]) if epilogue is not None else ((acc,), ())
            for o_ref, tile in zip(out_refs, tiles):
                o_ref[...] = tile.astype(o_ref.dtype)
            if n_a:
                @pl.when(i == 0)
                def _():
                    for s_ref in sum_refs:
                        s_ref[...] = jnp.zeros_like(s_ref)

                for s_ref, s in zip(sum_refs, sums):
                    s_ref[...] += s

        if gk == 1:
            finish(part)
        else:
            acc_ref = scratch_refs[0]

            @pl.when(k == 0)
            def _():
                acc_ref[...] = part

            @pl.when(k > 0)
            def _():
                acc_ref[...] += part

            @pl.when(k == gk - 1)
            def _():
                finish(acc_ref[...])

    in_specs, args = [], []
    for a, a_block, a_imap, b, b_block, b_imap in pairs:
        if a is not None:
            in_specs.append(pl.BlockSpec(a_block, order(a_imap)))
            args.append(a)
        if resident_b:
            in_specs.append(pl.BlockSpec(b_block, order(b_imap), pipeline_mode=pl.Buffered(1)))
        else:
            in_specs.append(pl.BlockSpec(b_block, order(b_imap)))
        args.append(b)
    for e, e_block, e_imap in extras:
        in_specs.append(pl.BlockSpec(e_block, order(e_imap)))
        args.append(e)
    first_comm_in = len(args)
    for c_in in c_ins:
        in_specs.append(pl.BlockSpec(memory_space=pl.ANY))
        args.append(c_in)
    if n_fill:
        in_specs.append(pl.BlockSpec(memory_space=pl.ANY))
        args.append(fill)
    out_shape = [jax.ShapeDtypeStruct(shape, dtype) for shape, dtype, _, _ in outs]
    out_specs = [pl.BlockSpec(block, order(imap)) for _, _, block, imap in outs]
    for shape, dtype in acc_outs:
        out_shape.append(jax.ShapeDtypeStruct(shape, dtype))
        out_specs.append(pl.BlockSpec(shape, lambda *_: (0, 0)))
    first_comm_out = len(out_shape)
    for c_out in c_outs:
        out_shape.append(c_out)
        out_specs.append(pl.BlockSpec(memory_space=pl.ANY))
    aliases = {n_s + first_comm_in + n: first_comm_out + n for n in range(n_ci)} if comm and comm["aliased"] else {}
    if n_fill:
        aliases[n_s + len(args) - 1] = 0
    scratch = [pltpu.VMEM(acc_shape, F32)] if gk > 1 else []
    scratch += [pltpu.SemaphoreType.DMA((n,)) for n in c_sems]
    if n_s:
        spec = pltpu.PrefetchScalarGridSpec(num_scalar_prefetch=1, grid=run_grid, in_specs=in_specs, out_specs=out_specs,
                                            scratch_shapes=scratch)
        return pl.pallas_call(body, name=name, grid_spec=spec, out_shape=out_shape, compiler_params=_params(3),
                              input_output_aliases=aliases)(place, *args)
    return pl.pallas_call(
        body, name=name, grid=run_grid, in_specs=in_specs, out_specs=out_specs,
        out_shape=out_shape, scratch_shapes=scratch, compiler_params=_params(3), input_output_aliases=aliases,
    )(*args)


def _ew(name, grid, ins, outs, fn, acc_outs=(), place=None):
    n_i, n_o, n_a = len(ins), len(outs), len(acc_outs)
    ng = len(grid)
    n_s = 0 if place is None else 1

    def body(*refs):
        in_refs = refs[n_s: n_s + n_i]
        out_refs = refs[n_s + n_i: n_s + n_i + n_o]
        sum_refs = refs[n_s + n_i + n_o:]
        pids = tuple(pl.program_id(a) for a in range(ng))
        if n_s:
            pids = (refs[0],) + pids
        tiles, sums = fn(pids, *[r[...] for r in in_refs])
        for o_ref, tile in zip(out_refs, tiles):
            o_ref[...] = tile.astype(o_ref.dtype)
        if n_a:
            first = pids[0] == 0
            for p_ in pids[1:]:
                first = jnp.logical_and(first, p_ == 0)

            @pl.when(first)
            def _():
                for s_ref in sum_refs:
                    s_ref[...] = jnp.zeros_like(s_ref)

            for s_ref, s in zip(sum_refs, sums):
                s_ref[...] += s

    in_specs = [pl.BlockSpec(block, imap) for _, block, imap in ins]
    out_shape = [jax.ShapeDtypeStruct(shape, dtype) for shape, dtype, _, _ in outs]
    out_specs = [pl.BlockSpec(block, imap) for _, _, block, imap in outs]
    for shape, dtype in acc_outs:
        out_shape.append(jax.ShapeDtypeStruct(shape, dtype))
        out_specs.append(pl.BlockSpec(shape, lambda *_, nd=len(shape): (0,) * nd))
    arrays = [a for a, _, _ in ins]
    if n_s:
        assert not n_a
        spec = pltpu.PrefetchScalarGridSpec(num_scalar_prefetch=1, grid=grid, in_specs=in_specs, out_specs=out_specs)
        return pl.pallas_call(body, name=name, grid_spec=spec, out_shape=out_shape, compiler_params=_params(ng))(
            place, *arrays)
    return pl.pallas_call(
        body, name=name, grid=grid, in_specs=in_specs, out_specs=out_specs, out_shape=out_shape,
        compiler_params=_params(ng),
    )(*arrays)


def _rows(tm, width):
    return (tm, width), (lambda i: (i, 0))


def _rms_fwd_tile(h, g):
    r = lax.rsqrt(jnp.mean(h * h, axis=-1, keepdims=True) + EPS)
    return h * r * g


def _rms_bwd_tile(dn, h, g):
    r = lax.rsqrt(jnp.mean(h * h, axis=-1, keepdims=True) + EPS)
    hhat = h * r
    gy = dn * g
    dh = r * (gy - hhat * jnp.mean(gy * hhat, axis=-1, keepdims=True))
    dg = jnp.sum(dn * hhat, axis=0, keepdims=True)
    return dh, dg


def _rope_tables(pos_col, inv_row, tm):
    s = pos_col.shape[0]

    def fn(pids, pos, inv):
        ang = pos * inv
        lane = lax.broadcasted_iota(jnp.int32, ang.shape, 1)
        cs = jnp.where(lane < ROPE_DIM, jnp.cos(ang), 1.0)
        sn = jnp.sin(ang)
        s_lo = jnp.where(lane < ROPE_HALF, -sn, 0.0)
        s_hi = jnp.where(jnp.logical_and(lane >= ROPE_HALF, lane < ROPE_DIM), sn, 0.0)
        return (cs, s_lo, s_hi), ()

    blk, imap = _rows(tm, 128)
    return _ew(
        "rope_tables", (s // tm,),
        [(pos_col, (tm, 1), lambda i: (i, 0)), (inv_row, (1, 128), lambda i: (0, 0))],
        [((s, 128), F32, blk, imap)] * 3, fn,
    )


def _rope(xh, cs, s_lo, s_hi):
    return xh * cs + pltpu.roll(xh, HEAD_DIM - ROPE_HALF, 1) * s_lo + pltpu.roll(xh, ROPE_HALF, 1) * s_hi


def _rope_t(gh, cs, s_lo, s_hi):
    return gh * cs + pltpu.roll(gh * s_lo, ROPE_HALF, 1) + pltpu.roll(gh * s_hi, HEAD_DIM - ROPE_HALF, 1)


def _attn_geometry(length):
    nb = length // ATTN_BLOCK
    gq = min(8, nb)
    assert nb % gq == 0
    return nb, gq, gq * ATTN_BLOCK, nb // gq


def _band_masks():
    qi = lax.broadcasted_iota(jnp.int32, (ATTN_BLOCK, ATTN_BLOCK), 0)
    kj = lax.broadcasted_iota(jnp.int32, (ATTN_BLOCK, ATTN_BLOCK), 1)
    return kj <= qi, kj >= qi


def _band_mask_pair():
    qi = lax.broadcasted_iota(jnp.int32, (ATTN_BLOCK, 2 * ATTN_BLOCK), 0)
    cj = lax.broadcasted_iota(jnp.int32, (ATTN_BLOCK, 2 * ATTN_BLOCK), 1)
    in_cur = cj >= ATTN_BLOCK
    band = jnp.logical_or(jnp.logical_and(in_cur, cj - ATTN_BLOCK <= qi),
                          jnp.logical_and(cj < ATTN_BLOCK, cj >= qi))
    return band, in_cur


def _attn_fwd(qv, kv, vv, dil, cols3=(0, 0, 0)):
    length = qv.shape[0]
    nb, gq, rows, ni = _attn_geometry(length)

    def body(q_ref, kc_ref, kp_ref, vc_ref, vp_ref, o_ref, l_ref):
        i = pl.program_id(1)
        band, in_cur = _band_mask_pair()
        band_first = jnp.logical_and(band, jnp.logical_or(in_cur, i > 0))
        work = []
        for h in range(HEADS_PER_GROUP):
            cols = slice(h * HEAD_DIM, (h + 1) * HEAD_DIM)
            qh = q_ref[:, cols]
            k_all = jnp.concatenate([kp_ref[:, cols], kc_ref[:, cols]], axis=0)
            v_all = jnp.concatenate([vp_ref[:, cols], vc_ref[:, cols]], axis=0)
            for jj in range(gq):
                rws = slice(jj * ATTN_BLOCK, (jj + 1) * ATTN_BLOCK)
                two = slice(jj * ATTN_BLOCK, (jj + 2) * ATTN_BLOCK)
                work.append(dict(h=h, rws=rws, cols=cols, v=v_all[two], first=jj == 0, s=_dot(qh[rws], k_all[two], "nt")))
        for w in work:
            s = jnp.where(band_first if w["first"] else band, w["s"], NEG_BIG)
            m = jnp.max(s, axis=-1, keepdims=True)
            pexp = jnp.exp(s - m)
            w["den"] = jnp.sum(pexp, axis=-1, keepdims=True)
            w["p"] = pexp.astype(BF16)
            w["lse"] = m + jnp.log(w["den"])
        for w in work:
            o = _dot(w["p"], w["v"], "nn")
            o_ref[w["rws"], w["cols"]] = (o * (1.0 / w["den"])).astype(o_ref.dtype)
            l_ref[w["rws"], w["h"] * LSE_LANES:(w["h"] + 1) * LSE_LANES] = jnp.broadcast_to(w["lse"], (ATTN_BLOCK, LSE_LANES))

    def cur(c):
        return pl.BlockSpec((rows, GROUP_WIDTH), lambda r, i: (i, r + c))

    def prev(c):
        return pl.BlockSpec((ATTN_BLOCK, GROUP_WIDTH), lambda r, i: (jnp.maximum(i * gq - 1, 0), r + c))

    cq, ck, cv = cols3
    return pl.pallas_call(
        body, name=f"attn_fwd_d{dil}", grid=(dil, ni),
        in_specs=[cur(cq), cur(ck), prev(ck), cur(cv), prev(cv)],
        out_specs=[cur(0), pl.BlockSpec((rows, LSE_WIDTH), lambda r, i: (i, r))],
        out_shape=[jax.ShapeDtypeStruct((length, dil * GROUP_WIDTH), BF16),
                   jax.ShapeDtypeStruct((length, dil * LSE_WIDTH), F32)],
        compiler_params=_params(2),
    )(qv, kv, kv, vv, vv)


def _attn_bwd(qv, kv, vv, dov, ov, lv, dil, cols3=(0, 0, 0)):
    length = qv.shape[0]
    nb, gq, rows, ni = _attn_geometry(length)
    out_shape = (length, dil * GROUP_WIDTH)

    def body(qc_ref, qn_ref, kc_ref, kp_ref, vc_ref, vp_ref, doc_ref, don_ref, oc_ref, on_ref, lc_ref, ln_ref,
             dq_ref, dk_ref, dv_ref):
        i = pl.program_id(1)
        _, mask_p = _band_masks()
        band, in_cur = _band_mask_pair()
        band_first = jnp.logical_and(band, jnp.logical_or(in_cur, i > 0))
        has_next = i < ni - 1

        last = slice(gq * ATTN_BLOCK, (gq + 1) * ATTN_BLOCK)
        mask_next = jnp.logical_and(mask_p, has_next)

        def rows_of(jj):
            return slice(jj * ATTN_BLOCK, (jj + 1) * ATTN_BLOCK)

        def keys_of(jj):
            return slice(jj * ATTN_BLOCK, (jj + 2) * ATTN_BLOCK)

        heads = []
        for h in range(HEADS_PER_GROUP):
            cols = slice(h * HEAD_DIM, (h + 1) * HEAD_DIM)
            hd = dict(
                cols=cols, q_c=qc_ref[:, cols], q_n=qn_ref[:, cols],
                k_all=jnp.concatenate([kp_ref[:, cols], kc_ref[:, cols]], axis=0),
                v_all=jnp.concatenate([vp_ref[:, cols], vc_ref[:, cols]], axis=0),
                do_c=doc_ref[:, cols], do_n=don_ref[:, cols],
                l_c=lc_ref[:, h * LSE_LANES:h * LSE_LANES + 1], l_n=ln_ref[:, h * LSE_LANES:h * LSE_LANES + 1],
            )
            hd["dl_c"] = jnp.sum(hd["do_c"].astype(F32) * oc_ref[:, cols].astype(F32), axis=-1, keepdims=True)
            hd["dl_n"] = jnp.sum(hd["do_n"].astype(F32) * on_ref[:, cols].astype(F32), axis=-1, keepdims=True)
            hd["s"] = [_dot(hd["q_c"][rows_of(jj)], hd["k_all"][keys_of(jj)], "nt") for jj in range(gq)]
            hd["dp"] = [_dot(hd["do_c"][rows_of(jj)], hd["v_all"][keys_of(jj)], "nt") for jj in range(gq)]
            hd["s"].append(_dot(hd["q_n"], hd["k_all"][last], "nt"))
            hd["dp"].append(_dot(hd["do_n"], hd["v_all"][last], "nt"))
            heads.append(hd)
        for hd in heads:
            hd["p"], hd["ds"] = [], []
            for jj in range(gq + 1):
                if jj < gq:
                    mask, l_col, delta = (band_first if jj == 0 else band), hd["l_c"][rows_of(jj)], hd["dl_c"][rows_of(jj)]
                else:
                    mask, l_col, delta = mask_next, hd["l_n"], hd["dl_n"]
                p = jnp.where(mask, jnp.exp(hd["s"][jj] - l_col), 0.0)
                hd["p"].append(p.astype(BF16))
                hd["ds"].append((p * (hd["dp"][jj] - delta)).astype(BF16))
        for hd in heads:
            cols = hd["cols"]
            dk_blocks, dv_blocks = [None] * (gq + 1), [None] * (gq + 1)

            def add(lst, idx, val):
                lst[idx] = val if lst[idx] is None else lst[idx] + val

            for jj in range(gq):
                qb, dob = hd["q_c"][rows_of(jj)], hd["do_c"][rows_of(jj)]
                dq_ref[rows_of(jj), cols] = _dot(hd["ds"][jj], hd["k_all"][keys_of(jj)], "nn").astype(dq_ref.dtype)
                dk2 = _dot(hd["ds"][jj], qb, "tn")
                dv2 = _dot(hd["p"][jj], dob, "tn")
                add(dk_blocks, jj, dk2[:ATTN_BLOCK])
                add(dk_blocks, jj + 1, dk2[ATTN_BLOCK:])
                add(dv_blocks, jj, dv2[:ATTN_BLOCK])
                add(dv_blocks, jj + 1, dv2[ATTN_BLOCK:])
            add(dk_blocks, gq, _dot(hd["ds"][gq], hd["q_n"], "tn"))
            add(dv_blocks, gq, _dot(hd["p"][gq], hd["do_n"], "tn"))
            for jj in range(gq):
                dk_ref[rows_of(jj), cols] = dk_blocks[jj + 1].astype(dk_ref.dtype)
                dv_ref[rows_of(jj), cols] = dv_blocks[jj + 1].astype(dv_ref.dtype)

    def cur(c):
        return pl.BlockSpec((rows, GROUP_WIDTH), lambda r, i: (i, r + c))

    def prev(c):
        return pl.BlockSpec((ATTN_BLOCK, GROUP_WIDTH), lambda r, i: (jnp.maximum(i * gq - 1, 0), r + c))

    def nxt(c):
        return pl.BlockSpec((ATTN_BLOCK, GROUP_WIDTH), lambda r, i: (jnp.minimum((i + 1) * gq, nb - 1), r + c))

    cq, ck, cv = cols3
    lse_cur = pl.BlockSpec((rows, LSE_WIDTH), lambda r, i: (i, r))
    lse_next = pl.BlockSpec((ATTN_BLOCK, LSE_WIDTH), lambda r, i: (jnp.minimum((i + 1) * gq, nb - 1), r))
    return pl.pallas_call(
        body, name=f"attn_bwd_d{dil}", grid=(dil, ni),
        in_specs=[cur(cq), nxt(cq), cur(ck), prev(ck), cur(cv), prev(cv), cur(0), nxt(0), cur(0), nxt(0), lse_cur, lse_next],
        out_specs=[cur(0), cur(0), cur(0)],
        out_shape=[jax.ShapeDtypeStruct(out_shape, BF16)] * 3,
        compiler_params=_params(2),
    )(qv, qv, kv, kv, vv, vv, dov, dov, ov, ov, lv, lv)


DILATED = tuple((g, d) for g, d in enumerate(GROUP_DILATIONS) if d > 1)


def _spread(scr, slot, tile, out_ref, dil, col, width=GROUP_WIDTH):
    tm = tile.shape[0]
    buf = scr.at[slot]
    buf[...] = tile
    for r in range(dil):
        c0 = r * width + col
        out_ref[:, c0:c0 + HEAD_DIM] = buf[pl.ds(r, tm // dil, stride=dil), :].astype(out_ref.dtype)


def _collect(scr, slot, in_ref, dil, col, width=GROUP_WIDTH):
    tm = scr.shape[1]
    buf = scr.at[slot]
    for r in range(dil):
        c0 = r * width + col
        buf[pl.ds(r, tm // dil, stride=dil), :] = in_ref[:, c0:c0 + HEAD_DIM].astype(F32)
    return buf[...]


def _view_spec(tm, dil, width=GROUP_WIDTH):
    return pl.BlockSpec((tm // dil, dil * width), lambda i: (i, 0))


def _view_shape(s, dil, dtype, width=GROUP_WIDTH):
    return jax.ShapeDtypeStruct((s // dil, dil * width), dtype)


def _qkv_layout(z, tabs, tm):
    s = z.shape[0]
    scale = 1.0 / math.sqrt(HEAD_DIM)
    qkv_width = 3 * N_GROUPS * GROUP_WIDTH

    def body(z_ref, cs_ref, lo_ref, hi_ref, qk0_ref, *rest):
        views, scr = rest[:-1], rest[-1]
        tabs_ = (cs_ref[...], lo_ref[...], hi_ref[...])
        for part in range(3):
            for g, dil in enumerate(GROUP_DILATIONS):
                if part == 2 and dil == 1:
                    continue
                for h in range(HEADS_PER_GROUP):
                    col = part * N_GROUPS * GROUP_WIDTH + g * GROUP_WIDTH + h * HEAD_DIM
                    t = z_ref[:, col:col + HEAD_DIM].astype(F32)
                    if part < 2:
                        t = _rope(t, *tabs_)
                    if part == 0:
                        t = t * scale
                    if dil == 1:
                        c0 = part * GROUP_WIDTH + h * HEAD_DIM
                        qk0_ref[:, c0:c0 + HEAD_DIM] = t.astype(BF16)
                    else:
                        out = views[3 * [gg for gg, _ in DILATED].index(g) + part]
                        _spread(scr, h, t, out, dil, h * HEAD_DIM)

    row = lambda i: (i, 0)
    out_shape = [jax.ShapeDtypeStruct((s, 2 * GROUP_WIDTH), BF16)]
    out_specs = [pl.BlockSpec((tm, 2 * GROUP_WIDTH), row)]
    for _, dil in DILATED:
        out_shape += [_view_shape(s, dil, BF16)] * 3
        out_specs += [_view_spec(tm, dil)] * 3
    res = pl.pallas_call(
        body, name="qkv_layout", grid=(s // tm,),
        in_specs=[pl.BlockSpec((tm, qkv_width), row)] + [pl.BlockSpec((tm, HEAD_DIM), row)] * 3,
        out_specs=out_specs, out_shape=out_shape,
        scratch_shapes=[pltpu.VMEM((HEADS_PER_GROUP, tm, HEAD_DIM), F32)], compiler_params=_params(1),
    )(z, *tabs)
    return res[0], [tuple(res[1 + 3 * n:4 + 3 * n]) for n in range(len(DILATED))]


def _attn_merge(o0, l0, dilated, tm):
    s = o0.shape[0]
    n_d = len(DILATED)

    def body(*refs):
        o0_ref, l0_ref = refs[:2]
        in_views = refs[2:2 + 2 * n_d]
        attn_ref, lse_ref = refs[2 + 2 * n_d:4 + 2 * n_d]
        out_views = refs[4 + 2 * n_d:4 + 4 * n_d]
        scr = refs[-1]
        l_rows = [l0_ref[...]] + [_collect(scr, n, in_views[2 * n + 1], dil, 0, LSE_WIDTH) for n, (_, dil) in enumerate(DILATED)]
        lse_heads = []
        for h in range(HEADS_PER_GROUP):
            cols = slice(h * HEAD_DIM, (h + 1) * HEAD_DIM)
            os_ = [o0_ref[:, cols].astype(F32)]
            for n, (_, dil) in enumerate(DILATED):
                os_.append(_collect(scr, n_d + n, in_views[2 * n], dil, h * HEAD_DIM))
            ls_ = [lr[:, h * LSE_LANES:h * LSE_LANES + 1] for lr in l_rows]
            m = ls_[0]
            for l_ in ls_[1:]:
                m = jnp.maximum(m, l_)
            es = [jnp.exp(l_ - m) for l_ in ls_]
            den = es[0]
            num = es[0] * os_[0]
            for e, o in zip(es[1:], os_[1:]):
                den = den + e
                num = num + e * o
            attn = num * (1.0 / den)
            lse_heads.append(jnp.broadcast_to(m + jnp.log(den), (tm, LSE_LANES)))
            attn_ref[:, cols] = attn.astype(BF16)
            for n, (_, dil) in enumerate(DILATED):
                _spread(scr, 2 * n_d, attn, out_views[2 * n], dil, h * HEAD_DIM)
        lse = jnp.concatenate(lse_heads, axis=1)
        lse_ref[...] = lse
        for n, (_, dil) in enumerate(DILATED):
            _spread(scr, 2 * n_d, lse, out_views[2 * n + 1], dil, 0, LSE_WIDTH)

    row = lambda i: (i, 0)
    nat = pl.BlockSpec((tm, GROUP_WIDTH), row)
    nat_l = pl.BlockSpec((tm, LSE_WIDTH), row)
    in_specs = [nat, nat_l]
    args = [o0, l0]
    out_specs = [nat, nat_l]
    out_shape = [jax.ShapeDtypeStruct((s, GROUP_WIDTH), BF16), jax.ShapeDtypeStruct((s, LSE_WIDTH), F32)]
    for (_, dil), (ov, lv) in zip(DILATED, dilated):
        in_specs += [_view_spec(tm, dil), _view_spec(tm, dil, LSE_WIDTH)]
        args += [ov, lv]
        out_specs += [_view_spec(tm, dil), _view_spec(tm, dil, LSE_WIDTH)]
        out_shape += [_view_shape(s, dil, BF16), _view_shape(s, dil, F32, LSE_WIDTH)]
    res = pl.pallas_call(
        body, name="attn_merge", grid=(s // tm,), in_specs=in_specs, out_specs=out_specs, out_shape=out_shape,
        scratch_shapes=[pltpu.VMEM((2 * n_d + 1, tm, HEAD_DIM), F32)], compiler_params=_params(1),
    )(*args)
    return res[0], res[1], [tuple(res[2 + 2 * n:4 + 2 * n]) for n in range(n_d)]


def _to_views(a, tm):
    s = a.shape[0]

    def body(a_ref, *rest):
        outs, scr = rest[:-1], rest[-1]
        for h in range(HEADS_PER_GROUP):
            t = a_ref[:, h * HEAD_DIM:(h + 1) * HEAD_DIM].astype(F32)
            for n, (_, dil) in enumerate(DILATED):
                _spread(scr, n, t, outs[n], dil, h * HEAD_DIM)

    return pl.pallas_call(
        body, name="to_views", grid=(s // tm,), in_specs=[pl.BlockSpec((tm, GROUP_WIDTH), lambda i: (i, 0))],
        out_specs=[_view_spec(tm, dil) for _, dil in DILATED], out_shape=[_view_shape(s, dil, BF16) for _, dil in DILATED],
        scratch_shapes=[pltpu.VMEM((len(DILATED), tm, HEAD_DIM), F32)], compiler_params=_params(1),
    )(a)


def _dz_layout(grads, du, dga, dgs, tabs, tm, comm=None):
    s = du.shape[0]
    scale = 1.0 / math.sqrt(HEAD_DIM)
    n_steps = s // tm
    c_ins = list(comm["ins"]) if comm else []
    c_outs = list(comm["outs"]) if comm else []
    c_sems = list(comm["sems"]) if comm else []
    n_fixed = 3 * N_GROUPS + 6

    def body(*refs):
        g_refs = refs[:3 * N_GROUPS]
        du_ref, dga_ref, dgs_ref, cs_ref, lo_ref, hi_ref = refs[3 * N_GROUPS:n_fixed]
        comm_in = refs[n_fixed:n_fixed + len(c_ins)]
        dz_ref = refs[n_fixed + len(c_ins)]
        comm_out = refs[n_fixed + len(c_ins) + 1:n_fixed + len(c_ins) + 1 + len(c_outs)]
        scr = refs[n_fixed + len(c_ins) + 1 + len(c_outs)]
        sems = refs[n_fixed + len(c_ins) + 2 + len(c_outs):]
        if comm:
            @pl.when(pl.program_id(0) == 0)
            def _():
                comm["start"](comm_in, comm_out, sems)

            @pl.when(pl.program_id(0) == n_steps - 1)
            def _():
                comm["finish"](comm_in, comm_out, sems)

        tabs_ = (cs_ref[...], lo_ref[...], hi_ref[...])
        for part in range(3):
            for g, dil in enumerate(GROUP_DILATIONS):
                src = g_refs[3 * g + part]
                for h in range(HEADS_PER_GROUP):
                    if dil == 1:
                        t = src[:, h * HEAD_DIM:(h + 1) * HEAD_DIM].astype(F32)
                    else:
                        t = _collect(scr, h, src, dil, h * HEAD_DIM)
                    if part < 2:
                        t = _rope_t(t, *tabs_)
                    if part == 0:
                        t = t * scale
                    col = part * N_GROUPS * GROUP_WIDTH + g * GROUP_WIDTH + h * HEAD_DIM
                    dz_ref[:, col:col + HEAD_DIM] = t.astype(BF16)
        dz_ref[:, COL_U:COL_GA] = du_ref[...]
        dz_ref[:, COL_GA:COL_GS] = dga_ref[...]
        dz_ref[:, COL_GS:IN_WIDTH] = dgs_ref[...]

    row = lambda i: (i, 0)
    in_specs, args = [], []
    for (g, dil), trio in zip(enumerate(GROUP_DILATIONS), grads):
        in_specs += [pl.BlockSpec((tm, GROUP_WIDTH), row) if dil == 1 else _view_spec(tm, dil)] * 3
        args += list(trio)
    in_specs += [pl.BlockSpec((tm, SSM_WIDTH), row), pl.BlockSpec((tm, D_MODEL), row), pl.BlockSpec((tm, D_MODEL), row)]
    in_specs += [pl.BlockSpec((tm, HEAD_DIM), row)] * 3
    in_specs += [pl.BlockSpec(memory_space=pl.ANY)] * len(c_ins)
    res = pl.pallas_call(
        body, name="dz_layout", grid=(n_steps,), in_specs=in_specs,
        out_specs=[pl.BlockSpec((tm, IN_WIDTH), row)] + [pl.BlockSpec(memory_space=pl.ANY)] * len(c_outs),
        out_shape=[jax.ShapeDtypeStruct((s, IN_WIDTH), BF16)] + c_outs,
        scratch_shapes=[pltpu.VMEM((HEADS_PER_GROUP, tm, HEAD_DIM), F32)] + [pltpu.SemaphoreType.DMA((n,)) for n in c_sems],
        compiler_params=_params(1),
    )(*args, du, dga, dgs, *tabs, *c_ins)
    return res[0], list(res[1:])


def _discretise(a_re, a_im, log_dt, bt_re, bt_im):
    dt = jnp.exp(log_dt)
    mag = jnp.exp(a_re * dt)
    bar_re = mag * jnp.cos(a_im * dt)
    bar_im = mag * jnp.sin(a_im * dt)
    nr = bar_re - 1.0
    ni = bar_im
    den = a_re * a_re + a_im * a_im
    z_re = (nr * a_re + ni * a_im) / den
    z_im = (ni * a_re - nr * a_im) / den
    bb_re = z_re[:, None, :] * bt_re - z_im[:, None, :] * bt_im
    bb_im = z_re[:, None, :] * bt_im + z_im[:, None, :] * bt_re
    return bar_re, bar_im, bb_re, bb_im


def _ssm_prep(a_re, a_im, log_dt, bt_re, bt_im):
    def body(ar, ai, ld, br, bi, o_lr, o_li, o_br, o_bi):
        lr, li, bbr, bbi = _discretise(ar[...], ai[...], ld[...], br[...], bi[...])
        o_lr[...] = lr
        o_li[...] = li
        o_br[...] = bbr
        o_bi[...] = bbi

    sm = jax.ShapeDtypeStruct((SSM_GROUPS, SSM_STATE), F32)
    bg = jax.ShapeDtypeStruct((SSM_GROUPS, SSM_GROUP, SSM_STATE), F32)
    return pl.pallas_call(body, name="ssm_prep", out_shape=[sm, sm, bg, bg])(a_re, a_im, log_dt, bt_re, bt_im)


def _ssm_param_bwd(a_re, a_im, log_dt, bt_re, bt_im, d_lr, d_li, d_bbr, d_bbi):
    def body(ar, ai, ld, br, bi, g_lr, g_li, g_br, g_bi, o_ar, o_ai, o_ld, o_br, o_bi):
        _, vjp = jax.vjp(_discretise, ar[...], ai[...], ld[...], br[...], bi[...])
        d_ar, d_ai, d_ld, d_br, d_bi = vjp((g_lr[...], g_li[...], g_br[...], g_bi[...]))
        o_ar[...] = d_ar
        o_ai[...] = d_ai
        o_ld[...] = d_ld
        o_br[...] = d_br
        o_bi[...] = d_bi

    sm = jax.ShapeDtypeStruct((SSM_GROUPS, SSM_STATE), F32)
    col = jax.ShapeDtypeStruct((SSM_GROUPS, 1), F32)
    bg = jax.ShapeDtypeStruct((SSM_GROUPS, SSM_GROUP, SSM_STATE), F32)
    return pl.pallas_call(body, name="ssm_param_bwd", out_shape=[sm, sm, col, bg, bg])(
        a_re, a_im, log_dt, bt_re, bt_im, d_lr, d_li, d_bbr, d_bbi)


def _block_diag(t, rows_per, cols_per):
    t4 = t.reshape(SSM_SUPER, 8, rows_per, cols_per)
    eye = jnp.eye(8, dtype=t.dtype)
    return jnp.einsum("bgrc,gh->bgrhc", t4, eye).reshape(SSM_SUPER, 8 * rows_per, 8 * cols_per)


def _block_diag_t(dense, rows_per, cols_per):
    t = dense.reshape(SSM_SUPER, 8, rows_per, 8, cols_per)
    eye = jnp.eye(8, dtype=dense.dtype)
    return jnp.einsum("bgrhc,gh->bgrc", t, eye).reshape(SSM_GROUPS, rows_per, cols_per)


def _gelu(v):
    c = math.sqrt(2.0 / math.pi)
    return 0.5 * v * (1.0 + jnp.tanh(c * (v + 0.044715 * v * v * v)))


def _gelu_grad(v):
    c = math.sqrt(2.0 / math.pi)
    t = jnp.tanh(c * (v + 0.044715 * v * v * v))
    return 0.5 * (1.0 + t) + 0.5 * v * (1.0 - t * t) * c * (1.0 + 3.0 * 0.044715 * v * v)


SUB = 8


SCAN_STEPS = (1, 2, 4)
N_SCAN_TABLES = 2 + 2 * len(SCAN_STEPS)


def _scan_tables(tab_ref, lam_re, lam_im, reverse, conj):
    lr = lam_re
    li = -lam_im if conj else lam_im
    powers = [(lr, li)]
    for _ in range(SUB - 1):
        pr, pi = powers[-1]
        powers.append((pr * lr - pi * li, pr * li + pi * lr))
    row = lax.broadcasted_iota(jnp.int32, (SUB, N_STATE), 0)
    if reverse:
        row = SUB - 1 - row
    wide = lambda v: jnp.broadcast_to(v, (SUB, N_STATE))
    p_re = jnp.zeros((SUB, N_STATE), F32)
    p_im = jnp.zeros((SUB, N_STATE), F32)
    for j in range(SUB):
        p_re = jnp.where(row == j, wide(powers[j][0]), p_re)
        p_im = jnp.where(row == j, wide(powers[j][1]), p_im)
    tab_ref[0] = p_re
    tab_ref[1] = p_im
    for idx, k in enumerate(SCAN_STEPS):
        tab_ref[2 + 2 * idx] = jnp.where(row >= k, wide(powers[k - 1][0]), 0.0)
        tab_ref[3 + 2 * idx] = jnp.where(row >= k, wide(powers[k - 1][1]), 0.0)


def _scan_rows(g_re_ref, g_im_ref, tab_ref, carry, n_rows, reverse):
    last = 0 if reverse else SUB - 1

    def tile_step(tt, state):
        cr, ci = state
        t8 = (n_rows // SUB - 1 - tt) if reverse else tt
        start = pl.multiple_of(t8 * SUB, SUB)
        xr = g_re_ref[pl.ds(start, SUB), :]
        xi = g_im_ref[pl.ds(start, SUB), :]
        for idx, k in enumerate(SCAN_STEPS):
            mr = tab_ref[2 + 2 * idx]
            mi = tab_ref[3 + 2 * idx]
            shift = SUB - k if reverse else k
            sr = pltpu.roll(xr, shift, 0)
            si = pltpu.roll(xi, shift, 0)
            xr, xi = xr + (mr * sr - mi * si), xi + (mr * si + mi * sr)
        pr = tab_ref[0]
        pi = tab_ref[1]
        xr, xi = xr + (pr * cr - pi * ci), xi + (pr * ci + pi * cr)
        g_re_ref[pl.ds(start, SUB), :] = xr
        g_im_ref[pl.ds(start, SUB), :] = xi
        return (jnp.broadcast_to(xr[last:last + 1, :], (SUB, N_STATE)),
                jnp.broadcast_to(xi[last:last + 1, :], (SUB, N_STATE)))

    return lax.fori_loop(0, n_rows // SUB, tile_step, carry)


def _ssm_fwd(z, b_re, b_im, c_re, c_im, lam_re, lam_im, d_skip, chunk):
    s = z.shape[0]

    def body(u_ref, bre, bim, cre, cim, lre, lim, dsk, hre_ref, him_ref, ys_ref, yg_ref, car_re, car_im, tabs):
        i = pl.program_id(0)

        @pl.when(i == 0)
        def _():
            car_re[...] = jnp.zeros_like(car_re)
            car_im[...] = jnp.zeros_like(car_im)
            _scan_tables(tabs, lre[...], lim[...], False, False)

        u = u_ref[...]
        for b in range(SSM_SUPER):
            ub = u[:, b * 128:(b + 1) * 128]
            st = slice(b * 512, (b + 1) * 512)
            hre_ref[:, st] = _dot(ub, bre[b], "nn")
            him_ref[:, st] = _dot(ub, bim[b], "nn")
        sr, si = _scan_rows(hre_ref, him_ref, tabs, (car_re[...], car_im[...]), chunk, False)
        car_re[...] = sr
        car_im[...] = si
        uf = u.astype(F32)
        for b in range(SSM_SUPER):
            st = slice(b * 512, (b + 1) * 512)
            ch = slice(b * 128, (b + 1) * 128)
            y = _dot(hre_ref[:, st].astype(BF16), cre[b], "nn") - _dot(him_ref[:, st].astype(BF16), cim[b], "nn")
            y = y + dsk[:, ch] * uf[:, ch]
            ys_ref[:, ch] = y
            yg_ref[:, ch] = _gelu(y).astype(BF16)

    full3 = lambda i: (0, 0, 0)
    full2 = lambda i: (0, 0)
    row = lambda i: (i, 0)
    u_col = COL_U // SSM_WIDTH
    return pl.pallas_call(
        body, name="ssm_fwd", grid=(s // chunk,),
        in_specs=[pl.BlockSpec((chunk, SSM_WIDTH), lambda i: (i, u_col)),
                  pl.BlockSpec((SSM_SUPER, 128, 512), full3), pl.BlockSpec((SSM_SUPER, 128, 512), full3),
                  pl.BlockSpec((SSM_SUPER, 512, 128), full3), pl.BlockSpec((SSM_SUPER, 512, 128), full3),
                  pl.BlockSpec((1, N_STATE), full2), pl.BlockSpec((1, N_STATE), full2), pl.BlockSpec((1, SSM_WIDTH), full2)],
        out_specs=[pl.BlockSpec((chunk, N_STATE), row), pl.BlockSpec((chunk, N_STATE), row),
                   pl.BlockSpec((chunk, SSM_WIDTH), row), pl.BlockSpec((chunk, SSM_WIDTH), row)],
        out_shape=[jax.ShapeDtypeStruct((s, N_STATE), F32), jax.ShapeDtypeStruct((s, N_STATE), F32),
                   jax.ShapeDtypeStruct((s, SSM_WIDTH), F32), jax.ShapeDtypeStruct((s, SSM_WIDTH), BF16)],
        scratch_shapes=[pltpu.VMEM((SUB, N_STATE), F32), pltpu.VMEM((SUB, N_STATE), F32),
                        pltpu.VMEM((N_SCAN_TABLES, SUB, N_STATE), F32)],
        compiler_params=_params(1),
    )(z, b_re, b_im, c_re, c_im, lam_re, lam_im, d_skip)


def _ssm_bwd(dys, z, h_re, h_im, b_re, b_im, c_re, c_im, lam_re, lam_im, d_skip, chunk):
    s = z.shape[0]
    n_chunks = s // chunk

    def body(dy_ref, u_ref, hre_ref, him_ref, hpr_ref, hpi_ref, bre, bim, cre, cim, lre, lim, dsk,
             du_ref, dlr_ref, dli_ref, dbr_ref, dbi_ref, dcr_ref, dci_ref, dd_ref, are, aim, car_re, car_im, tabs):
        i = pl.program_id(0)
        n = n_chunks - 1 - i

        @pl.when(i == 0)
        def _():
            car_re[...] = jnp.zeros_like(car_re)
            car_im[...] = jnp.zeros_like(car_im)
            _scan_tables(tabs, lre[...], lim[...], True, True)
            for r in (dlr_ref, dli_ref, dbr_ref, dbi_ref, dcr_ref, dci_ref, dd_ref):
                r[...] = jnp.zeros_like(r)

        dy = dy_ref[...]
        dyb = dy.astype(BF16)
        u = u_ref[...]
        for b in range(SSM_SUPER):
            ch = slice(b * 128, (b + 1) * 128)
            st = slice(b * 512, (b + 1) * 512)
            are[:, st] = _dot(dyb[:, ch], cre[b], "nt")
            aim[:, st] = -_dot(dyb[:, ch], cim[b], "nt")
        sr, si = _scan_rows(are, aim, tabs, (car_re[...], car_im[...]), chunk, True)
        car_re[...] = sr
        car_im[...] = si
        dd_ref[...] += jnp.sum(dy * u.astype(F32), axis=0, keepdims=True)
        row_id = lax.broadcasted_iota(jnp.int32, (chunk, 512), 0)
        top_scale = jnp.where(n > 0, 1.0, 0.0)
        for b in range(SSM_SUPER):
            ch = slice(b * 128, (b + 1) * 128)
            st = slice(b * 512, (b + 1) * 512)
            h_r = hre_ref[:, st]
            h_i = him_ref[:, st]
            hp_r = jnp.where(row_id == 0, hpr_ref[SUB - 1:SUB, st] * top_scale, pltpu.roll(h_r, 1, 0))
            hp_i = jnp.where(row_id == 0, hpi_ref[SUB - 1:SUB, st] * top_scale, pltpu.roll(h_i, 1, 0))
            a_r = are[:, st]
            a_i = aim[:, st]
            dlr_ref[:, st] += jnp.sum(a_r * hp_r + a_i * hp_i, axis=0, keepdims=True)
            dli_ref[:, st] += jnp.sum(a_i * hp_r - a_r * hp_i, axis=0, keepdims=True)
            a_rb = a_r.astype(BF16)
            a_ib = a_i.astype(BF16)
            dbr_ref[b] += _dot(u[:, ch], a_rb, "tn")
            dbi_ref[b] += _dot(u[:, ch], a_ib, "tn")
            dcr_ref[b] += _dot(dyb[:, ch], h_r.astype(BF16), "tn")
            dci_ref[b] += -_dot(dyb[:, ch], h_i.astype(BF16), "tn")
            du = _dot(a_rb, bre[b], "nt") + _dot(a_ib, bim[b], "nt") + dsk[:, ch] * dy[:, ch]
            du_ref[:, ch] = du.astype(du_ref.dtype)

    full3 = lambda i: (0, 0, 0)
    full2 = lambda i: (0, 0)
    rev = lambda i: (n_chunks - 1 - i, 0)
    above = lambda i: (jnp.maximum((n_chunks - 1 - i) * (chunk // SUB) - 1, 0), 0)
    u_col = COL_U // SSM_WIDTH
    b_spec = pl.BlockSpec((SSM_SUPER, 128, 512), full3)
    c_spec = pl.BlockSpec((SSM_SUPER, 512, 128), full3)
    vec = pl.BlockSpec((1, N_STATE), full2)
    return pl.pallas_call(
        body, name="ssm_bwd", grid=(n_chunks,),
        in_specs=[pl.BlockSpec((chunk, SSM_WIDTH), rev),
                  pl.BlockSpec((chunk, SSM_WIDTH), lambda i: (n_chunks - 1 - i, u_col)),
                  pl.BlockSpec((chunk, N_STATE), rev), pl.BlockSpec((chunk, N_STATE), rev),
                  pl.BlockSpec((SUB, N_STATE), above), pl.BlockSpec((SUB, N_STATE), above),
                  b_spec, b_spec, c_spec, c_spec, vec, vec, pl.BlockSpec((1, SSM_WIDTH), full2)],
        out_specs=[pl.BlockSpec((chunk, SSM_WIDTH), rev), vec, vec, b_spec, b_spec, b_spec, b_spec,
                   pl.BlockSpec((1, SSM_WIDTH), full2)],
        out_shape=[jax.ShapeDtypeStruct((s, SSM_WIDTH), BF16),
                   jax.ShapeDtypeStruct((1, N_STATE), F32), jax.ShapeDtypeStruct((1, N_STATE), F32)]
        + [jax.ShapeDtypeStruct((SSM_SUPER, 128, 512), F32)] * 4 + [jax.ShapeDtypeStruct((1, SSM_WIDTH), F32)],
        scratch_shapes=[pltpu.VMEM((chunk, N_STATE), F32), pltpu.VMEM((chunk, N_STATE), F32),
                        pltpu.VMEM((SUB, N_STATE), F32), pltpu.VMEM((SUB, N_STATE), F32),
                        pltpu.VMEM((N_SCAN_TABLES, SUB, N_STATE), F32)],
        compiler_params=_params(1),
    )(dys, z, h_re, h_im, h_re, h_im, b_re, b_im, c_re, c_im, lam_re, lam_im, d_skip)


def _local_step(x, p, pos, tgt, sm, w_in_own, w_in_buf, late_bufs, place):
    s = x.shape[0]
    tm = min(512, s)
    ts = min(2048, s)
    chunk = min(256, s)
    ni = s // tm
    nk = s // ts
    g_mix, g_ffn, g_final = sm["g_mix"], sm["g_ffn"], sm["g_final"]

    tmb = min(1024, s)
    nib = s // tmb
    chip_w = IN_WIDTH // N_CHIPS
    w_start, w_forward, w_finish = _gather_stages(1)
    gather_in = dict(ins=[w_in_buf], outs=[jax.ShapeDtypeStruct(w_in_buf.shape, w_in_buf.dtype)], aliased=True,
                     sems=[3] * 4, stages=[(0, w_start), (nib - 1, w_forward), (nib - 1, w_finish)])
    a_rows = lambda i, j, k, pv: (i, 0)
    z_own, n1, w_in_all = _mm("in_proj_own", (nib, 1, 1),
                              [(x, (tmb, D_MODEL), a_rows, w_in_own, (D_MODEL, chip_w), lambda i, j, k, pv: (0, 0))], "nn",
                              [((s, IN_WIDTH), BF16, (tmb, chip_w), lambda i, j, k, pv: (i, pv[P_CHIP])),
                               ((s, D_MODEL), BF16, (tmb, D_MODEL), a_rows)],
                              extras=[(g_mix, (1, D_MODEL), lambda i, j, k, pv: (0, 0))],
                              epilogue=lambda acc, g: ((acc,), ()), prologue=_rms_fwd_tile, comm=gather_in, place=place)
    w_in = w_in_all.reshape(N_CHIPS, D_MODEL, chip_w)
    n_late = len(late_bufs)
    g_start, g_forward, g_finish = _gather_stages(n_late)
    in_steps = (N_CHIPS - 1) * nib
    gather = dict(ins=late_bufs, outs=[jax.ShapeDtypeStruct(b.shape, b.dtype) for b in late_bufs], aliased=True,
                  sems=[3 * n_late] * 4,
                  stages=[(0, g_start), ((4 * in_steps) // 5, g_forward), (in_steps - 1, g_finish)])
    other = lambda j, pv: (pv[P_CHIP] + 1 + j) % N_CHIPS
    z, *late = _mm("in_proj", (nib, N_CHIPS - 1, 1),
                   [(n1, (tmb, D_MODEL), a_rows, w_in, (None, D_MODEL, chip_w), lambda i, j, k, pv: (other(j, pv), 0, 0))],
                   "nn", [((s, IN_WIDTH), BF16, (tmb, chip_w), lambda i, j, k, pv: (i, other(j, pv)))], j_outer=True,
                   comm=gather, place=place, fill=z_own)
    w_ap, w_ga, w_gb, w_out, w_fg, w_fu, w_fd, w_pg, w_pp = (
        g.reshape(N_CHIPS, 2 * g.shape[2], g.shape[3]) for g in late)
    w_out2 = w_out.reshape(D_MODEL, D_MODEL)
    w_pg2 = w_pg.reshape(D_MODEL, D_MODEL)

    inv = ROPE_THETA ** (-jnp.arange(ROPE_HALF, dtype=F32) * 2.0 / ROPE_DIM)
    inv_row = jnp.concatenate([inv, inv, jnp.zeros((HEAD_DIM - ROPE_DIM,), F32)]).reshape(1, HEAD_DIM)
    tabs = _rope_tables(pos.astype(F32).reshape(s, 1), inv_row, tm)

    qk0, qkv_views = _qkv_layout(z, tabs, tm)
    v0_col = (2 * N_GROUPS * GROUP_WIDTH) // GROUP_WIDTH
    group_in = [((qk0, qk0, z), (0, 1, v0_col))] + [(trio, (0, 0, 0)) for trio in qkv_views]
    fwd_out = [_attn_fwd(*arrs, dil, cols3) for (arrs, cols3), dil in zip(group_in, GROUP_DILATIONS)]
    attn, lse, merged_views = _attn_merge(fwd_out[0][0], fwd_out[0][1], fwd_out[1:], tm)

    def chip_cols(parts):
        return (jnp.concatenate(parts, axis=1),), ()

    def proj_cols(name, a, width, w):
        blk = (None, width, 256)
        pairs = [(a, (tmb, width), lambda i, j, k: (i, 0), w, blk, lambda i, j, k: (0, 0, 0))]
        pairs += [(None, None, None, w, blk, (lambda i, j, k, q=q: (q, 0, 0))) for q in range(1, N_CHIPS)]
        return _mm(name, (nib, 1, 1), pairs, "nn", [((s, D_MODEL), BF16, (tmb, D_MODEL), lambda i, j, k: (i, 0))],
                   epilogue=chip_cols, sum_pairs=False)[0]

    def proj512(name, a, w):
        return proj_cols(name, a, GROUP_WIDTH, w)

    attn_d = proj512("attn_proj", attn, w_ap)

    bt_re = jnp.transpose(sm["b_re"], (0, 2, 1))
    bt_im = jnp.transpose(sm["b_im"], (0, 2, 1))
    log_dt_col = sm["log_dt"].reshape(SSM_GROUPS, 1)
    lam_re, lam_im, bbt_re, bbt_im = _ssm_prep(sm["a_re"], sm["a_im"], log_dt_col, bt_re, bt_im)
    b_re_m = _block_diag(bbt_re, SSM_GROUP, SSM_STATE).astype(BF16)
    b_im_m = _block_diag(bbt_im, SSM_GROUP, SSM_STATE).astype(BF16)
    c_re_m = _block_diag(jnp.transpose(sm["c_re"], (0, 2, 1)), SSM_STATE, SSM_GROUP).astype(BF16)
    c_im_m = _block_diag(jnp.transpose(sm["c_im"], (0, 2, 1)), SSM_STATE, SSM_GROUP).astype(BF16)
    lam_re_row = lam_re.reshape(1, N_STATE)
    lam_im_row = lam_im.reshape(1, N_STATE)
    d_skip_row = sm["d_skip"].reshape(1, SSM_WIDTH)
    h_re, h_im, ys, yg = _ssm_fwd(z, b_re_m, b_im_m, c_re_m, c_im_m, lam_re_row, lam_im_row, d_skip_row, min(2 * chunk, s))

    pa = proj512("glu_a", yg, w_ga)
    pb = proj512("glu_b", yg, w_gb)

    def mix_pro(ad, xr, g, ga, gs, a, b):
        ga, gs, ad, a, b = (t.astype(F32) for t in (ga, gs, ad, a, b))
        return _sig(ga) * ad + _sig(gs) * (a * _sig(b))

    def out_epi(acc, xr, g, *_):
        h1 = acc + xr
        return (h1, _rms_fwd_tile(h1, g)), ()

    m3 = lambda i, j, k: (i, 0)
    w3 = lambda i, j, k: (0, 0)
    tile_d = (tm, D_MODEL)
    h1, n2, mix = _mm("out_proj", (ni, 1, 1), [(attn_d, tile_d, m3, w_out2, (D_MODEL, D_MODEL), w3)], "nn",
                      [((s, D_MODEL), F32, tile_d, m3), ((s, D_MODEL), BF16, tile_d, m3), ((s, D_MODEL), BF16, tile_d, m3)],
                      epilogue=out_epi, prologue=mix_pro,
                      extras=[(x, tile_d, m3), (g_ffn, (1, D_MODEL), w3),
                              (z, tile_d, lambda i, j, k: (i, COL_GA // D_MODEL)), (z, tile_d, lambda i, j, k: (i, COL_GS // D_MODEL)),
                              (pa, tile_d, m3), (pb, tile_d, m3)])

    ffq = (None, tm, D_FF_Q)
    ffq_map = lambda i, j, k: (j, i, 0)

    def ffn_in_epi(parts):
        gts, ups = parts[0::2], parts[1::2]
        acts = [gt * _sig(gt) * u_ for gt, u_ in zip(gts, ups)]
        return (jnp.stack(gts, axis=0), jnp.stack(ups, axis=0), jnp.stack(acts, axis=0)), ()

    w_ffq = (None, D_MODEL, D_FF_Q)
    ff_pairs = []
    for q in range(N_CHIPS):
        blk_q = lambda i, j, k, q=q: (q, 0, 0)
        ff_pairs.append((n2, (tm, D_MODEL), m3, w_fg, w_ffq, blk_q) if q == 0 else (None, None, None, w_fg, w_ffq, blk_q))
        ff_pairs.append((None, None, None, w_fu, w_ffq, blk_q))
    ff_all = (N_CHIPS, tm, D_FF_Q)
    ff_all_map = lambda i, j, k: (0, i, 0)
    gate, up, act = _mm("ffn_gate_up", (ni, 1, 1), ff_pairs, "nn",
                        [((N_CHIPS, s, D_FF_Q), BF16, ff_all, ff_all_map)] * 3, epilogue=ffn_in_epi,
                        sum_pairs=False, resident_b=True)

    (h2,) = _mm("ffn_down", (nib, 1, 1),
                [(act, (None, tmb, D_FF_Q), (lambda i, j, k, q=q: (q, i, 0)), w_fd, (None, D_FF_Q, D_MODEL),
                  (lambda i, j, k, q=q: (q, 0, 0))) for q in range(N_CHIPS)], "nn",
                [((s, D_MODEL), F32, (tmb, D_MODEL), m3)], epilogue=lambda acc, hr: ((acc + hr,), ()),
                extras=[(h1, (tmb, D_MODEL), m3)])

    pp = proj_cols("ple_proj", p, PLE_DIM, w_pp)

    def ple_head_epi(acc, hr, ppr, t, g):
        sg = _sig(acc)
        ppf = ppr.astype(F32)
        h = hr + sg * ppf
        r = lax.rsqrt(jnp.mean(h * h, axis=-1, keepdims=True) + EPS)
        hhat = h * r
        diff = hhat * g - t
        loss = 0.5 * jnp.sum(jnp.mean(diff * diff, axis=-1, keepdims=True))
        dy = diff * (1.0 / D_MODEL)
        gy = dy * g
        dh = r * (gy - hhat * jnp.mean(gy * hhat, axis=-1, keepdims=True))
        return ((dh, dh * ppf * sg * (1.0 - sg), dh * sg),
                (jnp.full((SUB, 128), loss, F32), jnp.sum(dy * hhat, axis=0, keepdims=True)))

    tile_row = (tm, D_MODEL)
    dh3, dgl, dpp, loss_acc, dg_final = _mm(
        "ple_gate_head", (ni, 1, 1), [(h2, tile_row, m3, w_pg2, (D_MODEL, D_MODEL), w3)], "nn",
        [((s, D_MODEL), F32, tile_row, m3), ((s, D_MODEL), BF16, tile_row, m3), ((s, D_MODEL), BF16, tile_row, m3)],
        epilogue=ple_head_epi,
        extras=[(h2, tile_row, m3), (pp, tile_row, m3), (tgt, tile_row, m3), (g_final, (1, D_MODEL), w3)],
        acc_outs=[((SUB, 128), F32), ((1, D_MODEL), F32)])

    def wgrad(name, a, a_block, a_imap, b, b_block, b_imap, out_shape, out_block, out_imap, nj, acc_shape):
        return _mm(name, (1, nj, nk), [(a, a_block, a_imap, b, b_block, b_imap)], "tn",
                   [(out_shape, F32, out_block, out_imap)], acc_shape=acc_shape)[0]

    tk0 = lambda i, j, k: (k, 0)
    tkj = lambda i, j, k: (k, j)
    def wgrad_cols(name, a, width, dy_):
        def split(acc):
            return (jnp.stack([acc[:, q * 256:(q + 1) * 256] for q in range(N_CHIPS)], axis=0),), ()

        return _mm(name, (1, 1, nk), [(a, (ts, width), tk0, dy_, (ts, D_MODEL), tk0)], "tn",
                   [((N_CHIPS, width, 256), F32, (N_CHIPS, width, 256), lambda i, j, k: (0, 0, 0))], epilogue=split,
                   acc_shape=(width, D_MODEL))[0]

    d_w_pp = wgrad_cols("d_ple_proj", p, PLE_DIM, dpp)
    d_w_pg = wgrad("d_ple_gate", h2, (ts, D_MODEL), tk0, dgl, (ts, D_MODEL), tk0, (D_MODEL, D_MODEL),
                   (D_MODEL, D_MODEL), w3, 1, (D_MODEL, D_MODEL))

    (dh2,) = _mm("ple_gate_bwd", (nib, 1, 1), [(dgl, (tmb, D_MODEL), m3, w_pg2, (D_MODEL, D_MODEL), w3)], "nt",
                 [((s, D_MODEL), F32, (tmb, D_MODEL), m3)], epilogue=lambda acc, d_: ((acc + d_,), ()),
                 extras=[(dh3, (tmb, D_MODEL), m3)])

    def ffn_bwd_epi(parts, gt_all, u_all):
        dgs_, dus_ = [], []
        for q, dact in enumerate(parts):
            gt, u_ = gt_all[q].astype(F32), u_all[q].astype(F32)
            sg = _sig(gt)
            dgs_.append(dact * u_ * (sg * (1.0 + gt * (1.0 - sg))))
            dus_.append(dact * gt * sg)
        return (jnp.stack(dgs_, axis=0), jnp.stack(dus_, axis=0)), ()

    fd_pairs = [((dh2, (tm, D_MODEL), m3) if q == 0 else (None, None, None))
                + (w_fd, (None, D_FF_Q, D_MODEL), (lambda i, j, k, q=q: (q, 0, 0))) for q in range(N_CHIPS)]
    dgate, dup = _mm("ffn_down_bwd", (ni, 1, 1), fd_pairs, "nt",
                     [((N_CHIPS, s, D_FF_Q), BF16, ff_all, ff_all_map)] * 2, epilogue=ffn_bwd_epi,
                     extras=[(gate, ff_all, ff_all_map), (up, ff_all, ff_all_map)], sum_pairs=False, resident_b=True)

    ffq_t = (None, ts, D_FF_Q)
    ffq_tmap = lambda i, j, k: (j, k, 0)
    blk_j = lambda i, j, k: (j, 0, 0)
    d_w_fd = wgrad("d_ffn_down", act, ffq_t, ffq_tmap, dh2, (ts, D_MODEL), tk0, (N_CHIPS, D_FF_Q, D_MODEL),
                   (None, D_FF_Q, D_MODEL), blk_j, N_CHIPS, (D_FF_Q, D_MODEL))
    d_w_fg = wgrad("d_ffn_gate", n2, (ts, D_MODEL), tk0, dgate, ffq_t, ffq_tmap, (N_CHIPS, D_MODEL, D_FF_Q),
                   (None, D_MODEL, D_FF_Q), blk_j, N_CHIPS, (D_MODEL, D_FF_Q))
    d_w_fu = wgrad("d_ffn_up", n2, (ts, D_MODEL), tk0, dup, ffq_t, ffq_tmap, (N_CHIPS, D_MODEL, D_FF_Q),
                   (None, D_MODEL, D_FF_Q), blk_j, N_CHIPS, (D_MODEL, D_FF_Q))

    def norm_bwd_epi(acc, h, d_res, g):
        dh, dg = _rms_bwd_tile(acc, h, g)
        return (d_res + dh,), (dg,)

    fi_pairs = []
    for q in range(N_CHIPS):
        a_q = lambda i, j, k, q=q: (q, i, 0)
        b_q = lambda i, j, k, q=q: (q, 0, 0)
        fi_pairs.append((dgate, ffq, a_q, w_fg, (None, D_MODEL, D_FF_Q), b_q))
        fi_pairs.append((dup, ffq, a_q, w_fu, (None, D_MODEL, D_FF_Q), b_q))
    dh1, dg_ffn = _mm("ffn_in_bwd", (ni, 1, 1), fi_pairs, "nt",
                      [((s, D_MODEL), F32, (tm, D_MODEL), m3)], epilogue=norm_bwd_epi,
                      extras=[(h1, (tm, D_MODEL), m3), (dh2, (tm, D_MODEL), m3), (g_ffn, (1, D_MODEL), w3)],
                      acc_outs=[((1, D_MODEL), F32)], resident_b=True)

    d_w_out = wgrad("d_out_proj", mix, (ts, D_MODEL), tk0, dh1, (ts, D_MODEL), tk0, (D_MODEL, D_MODEL),
                    (D_MODEL, D_MODEL), w3, 1, (D_MODEL, D_MODEL))

    def mix_bwd_epi(dm, ga, gs, ad, a, b):
        ga, gs, ad, a, b = (t.astype(F32) for t in (ga, gs, ad, a, b))
        s_a, s_s, s_b = _sig(ga), _sig(gs), _sig(b)
        d_ssm = dm * s_s
        return (dm * ad * s_a * (1.0 - s_a), dm * (a * s_b) * s_s * (1.0 - s_s), dm * s_a, d_ssm * s_b,
                d_ssm * a * s_b * (1.0 - s_b)), ()

    tile_m = (tm, D_MODEL)
    dga, dgs, dattn_d, dpa, dpb = _mm(
        "out_proj_bwd", (ni, 1, 1), [(dh1, tile_m, m3, w_out2, (D_MODEL, D_MODEL), w3)], "nt",
        [((s, D_MODEL), BF16, tile_m, m3)] * 5, epilogue=mix_bwd_epi,
        extras=[(z, tile_m, lambda i, j, k: (i, COL_GA // D_MODEL)), (z, tile_m, lambda i, j, k: (i, COL_GS // D_MODEL)),
                (attn_d, tile_m, m3), (pa, tile_m, m3), (pb, tile_m, m3)])

    d_w_ap = wgrad_cols("d_attn_proj", attn, GROUP_WIDTH, dattn_d)
    d_w_ga = wgrad_cols("d_glu_a", yg, GROUP_WIDTH, dpa)
    d_w_gb = wgrad_cols("d_glu_b", yg, GROUP_WIDTH, dpb)

    ik = lambda i, j, k: (i, k)

    def cols_bwd(dy_, w):
        return [(dy_, (tmb, 256), (lambda i, j, k, q=q: (i, q)), w, (None, GROUP_WIDTH, 256),
                 (lambda i, j, k, q=q: (q, 0, 0))) for q in range(N_CHIPS)]

    (dattn,) = _mm("attn_proj_bwd", (nib, 1, 1), cols_bwd(dattn_d, w_ap), "nt",
                   [((s, GROUP_WIDTH), BF16, (tmb, GROUP_WIDTH), m3)])

    (dys,) = _mm("glu_bwd", (nib, 1, 1), cols_bwd(dpa, w_ga) + cols_bwd(dpb, w_gb), "nt",
                 [((s, GROUP_WIDTH), F32, (tmb, GROUP_WIDTH), m3)],
                 epilogue=lambda acc, y_: ((acc * _gelu_grad(y_),), ()),
                 extras=[(ys, (tmb, GROUP_WIDTH), m3)])

    du, d_lr, d_li, d_bre, d_bim, d_cre, d_cim, d_dskip = _ssm_bwd(
        dys, z, h_re, h_im, b_re_m, b_im_m, c_re_m, c_im_m, lam_re_row, lam_im_row, d_skip_row, chunk)

    dattn_views = _to_views(dattn, tm)
    bwd_in = [(dattn, attn, lse)] + [(dv_, ov_, lv_) for dv_, (ov_, lv_) in zip(dattn_views, merged_views)]
    qkv_grads = [_attn_bwd(*arrs, *dol, dil, cols3)
                 for (arrs, cols3), dol, dil in zip(group_in, bwd_in, GROUP_DILATIONS)]
    early = [d_w_ap, d_w_ga, d_w_gb, d_w_out.reshape(N_CHIPS, D_MODEL // N_CHIPS, D_MODEL), d_w_fg, d_w_fu, d_w_fd,
             d_w_pg.reshape(N_CHIPS, D_MODEL // N_CHIPS, D_MODEL), d_w_pp]
    early5 = [g.reshape(N_CHIPS, 2, g.shape[1] // 2, g.shape[2]) for g in early]
    n_e = len(early5)
    p_start, p_finish = _pair_exchange_stages(n_e)
    dz, early_theirs = _dz_layout(
        qkv_grads, du, dga, dgs, tabs, tm,
        comm=dict(ins=early5, outs=_pair_exchange_shapes(early5), sems=[N_CHIPS * n_e] * 2, start=p_start, finish=p_finish))
    early_parts = [_pair_sum(g, t, place) for g, t in zip(early5, early_theirs)]

    chip_in = IN_WIDTH // N_CHIPS
    ip_pairs = [(dz, (tm, chip_in), (lambda i, j, k, q=q: (i, q)), w_in, (None, D_MODEL, chip_in),
                 (lambda i, j, k, q=q: (q, 0, 0))) for q in range(N_CHIPS)]
    grad_x, dg_mix = _mm("in_proj_bwd", (ni, 1, 1), ip_pairs, "nt",
                         [((s, D_MODEL), F32, (tm, D_MODEL), m3)], epilogue=norm_bwd_epi,
                         extras=[(x, (tm, D_MODEL), m3), (dh1, (tm, D_MODEL), m3), (g_mix, (1, D_MODEL), w3)],
                         acc_outs=[((1, D_MODEL), F32)], resident_b=True)

    d_bbt_re = _block_diag_t(d_bre, SSM_GROUP, SSM_STATE)
    d_bbt_im = _block_diag_t(d_bim, SSM_GROUP, SSM_STATE)
    d_a_re, d_a_im, d_log_dt, d_bt_re, d_bt_im = _ssm_param_bwd(
        sm["a_re"], sm["a_im"], log_dt_col, bt_re, bt_im,
        d_lr.reshape(SSM_GROUPS, SSM_STATE), d_li.reshape(SSM_GROUPS, SSM_STATE), d_bbt_re, d_bbt_im)
    small = {
        "g_mix": dg_mix, "a_re": d_a_re, "a_im": d_a_im, "log_dt": d_log_dt,
        "b_re": jnp.transpose(d_bt_re, (0, 2, 1)), "b_im": jnp.transpose(d_bt_im, (0, 2, 1)),
        "c_re": _block_diag_t(d_cre, SSM_GROUP, SSM_STATE), "c_im": _block_diag_t(d_cim, SSM_GROUP, SSM_STATE),
        "d_skip": d_dskip, "g_ffn": dg_ffn, "g_final": dg_final,
    }
    vec = _pack([small[n] for n in SMALL] + [loss_acc[0, 0].reshape(1)])

    x_start, x_finish = _chip_exchange_stages(n_e)
    v_start, v_finish = _all_exchange_stages()

    def both(f_chips, f_vec):
        def stage(ins, outs, sems):
            f_chips(ins[:n_e], outs[:n_e], sems[:2])
            f_vec(ins[n_e:], outs[n_e:], sems[2:])
        return stage

    half_in = chip_in // 2
    win_steps = 8 * nk
    exchange = dict(ins=early_parts + [vec],
                    outs=[jax.ShapeDtypeStruct(t.shape, t.dtype) for t in early_parts]
                    + [jax.ShapeDtypeStruct((8,) + vec.shape, vec.dtype)],
                    aliased=False, sems=[3 * n_e, 3 * n_e, 7, 7],
                    stages=[(0, both(x_start, v_start)), (win_steps - 1, both(x_finish, v_finish))])
    d_w_in, *got = _mm("d_in_proj", (1, 8, nk), [(n1, (ts, D_MODEL), tk0, dz, (ts, half_in), tkj)], "tn",
                       [((N_CHIPS, D_MODEL, chip_in), F32, (None, D_MODEL, half_in), lambda i, j, k: (j // 2, 0, j % 2))],
                       acc_shape=(D_MODEL, half_in), comm=exchange)
    return grad_x, d_w_in, early_parts, got[:n_e], vec, got[n_e]


BIG = ("w_in", "w_attn_proj", "w_glu_a", "w_glu_b", "w_out", "w_ffn_gate", "w_ffn_up", "w_ffn_down", "w_ple_gate",
       "w_ple_proj")
SMALL = ("g_mix", "a_re", "a_im", "log_dt", "b_re", "b_im", "c_re", "c_im", "d_skip", "g_ffn", "g_final")
ANY = pl.BlockSpec(memory_space=pl.ANY)


def _place():
    x, y, c = lax.axis_index("x"), lax.axis_index("y"), lax.axis_index("c")
    chips = [(1 - x, y), (x, 1 - y), (1 - x, 1 - y)]
    return x, y, c, chips


def _remote(src, dst, send_sem, recv_sem, to):
    return pltpu.make_async_remote_copy(src_ref=src, dst_ref=dst, send_sem=send_sem, recv_sem=recv_sem, device_id=to,
                                        device_id_type=MESH)


def _comm_call(name, body, ins, out_shapes, n_sems, aliases=None):
    n_w = len(ins)
    return pl.pallas_call(
        body, name=name, in_specs=[ANY] * n_w, out_specs=[ANY] * len(out_shapes), out_shape=out_shapes,
        scratch_shapes=[pltpu.SemaphoreType.DMA((n,)) for n in n_sems], input_output_aliases=aliases or {},
    )(*ins)


def _gather_stages(n_w):
    def each():
        x, y, c, chips = _place()
        for w in range(n_w):
            for j, (cx, cy) in enumerate(chips):
                yield w, 3 * w + j, 2 * x + y, 2 * cx + cy, (cx, cy, c), (x, y, 1 - c), c

    def start(ins, outs, sems):
        for w, k, me, _, peer, _, c in each():
            mine = outs[w].at[me, c]
            _remote(mine, mine, sems[0].at[k], sems[1].at[k], peer).start()

    def forward(ins, outs, sems):
        for w, k, _, src_chip, peer, sib, c in each():
            landed = outs[w].at[src_chip, c]
            _remote(landed, landed, sems[0].at[k], sems[1].at[k], peer).wait_recv()
            _remote(landed, landed, sems[2].at[k], sems[3].at[k], sib).start()

    def finish(ins, outs, sems):
        for w, k, me, src_chip, peer, sib, c in each():
            other = outs[w].at[src_chip, 1 - c]
            _remote(other, other, sems[2].at[k], sems[3].at[k], sib).wait_recv()
        for w, k, me, src_chip, peer, sib, c in each():
            mine = outs[w].at[me, c]
            _remote(mine, mine, sems[0].at[k], sems[1].at[k], peer).wait_send()
            landed = outs[w].at[src_chip, c]
            _remote(landed, landed, sems[2].at[k], sems[3].at[k], sib).wait_send()

    return start, forward, finish


def _pair_exchange(grads):
    n_w = len(grads)
    start, finish = _pair_exchange_stages(n_w)

    def body(*refs):
        ins, outs, sems = refs[:n_w], refs[n_w:2 * n_w], refs[2 * n_w:]
        start(ins, outs, sems)
        finish(ins, outs, sems)

    return _comm_call("grad_pair_exchange", body, grads, _pair_exchange_shapes(grads), [N_CHIPS * n_w] * 2)


def _pair_exchange_shapes(grads):
    return [jax.ShapeDtypeStruct((N_CHIPS,) + g.shape[2:], g.dtype) for g in grads]


def _pair_exchange_stages(n_w):
    def each():
        x, y, c, _ = _place()
        for w in range(n_w):
            for q in range(N_CHIPS):
                yield w, q, N_CHIPS * w + q, c, (x, y, 1 - c)

    def start(ins, outs, sems):
        for w, q, k, c, sib in each():
            _remote(ins[w].at[q, 1 - c], outs[w].at[q], sems[0].at[k], sems[1].at[k], sib).start()

    def finish(ins, outs, sems):
        for w, q, k, c, sib in each():
            _remote(ins[w].at[q, 1 - c], outs[w].at[q], sems[0].at[k], sems[1].at[k], sib).wait()

    return start, finish


def _chip_exchange(parts):
    n_w = len(parts)

    start, finish = _chip_exchange_stages(n_w)

    def body(*refs):
        ins, outs, sems = refs[:n_w], refs[n_w:2 * n_w], refs[2 * n_w:]
        start(ins, outs, sems)
        finish(ins, outs, sems)

    out_shapes = [jax.ShapeDtypeStruct(t.shape, t.dtype) for t in parts]
    return _comm_call("grad_chip_exchange", body, parts, out_shapes, [3 * n_w, 3 * n_w])


def _chip_exchange_stages(n_w):
    def each():
        x, y, c, chips = _place()
        for w in range(n_w):
            for j, (cx, cy) in enumerate(chips):
                yield w, 3 * w + j, 2 * x + y, 2 * cx + cy, (cx, cy, c)

    def start(ins, outs, sems):
        for w, k, me, peer_chip, peer in each():
            _remote(ins[w].at[peer_chip], outs[w].at[me], sems[0].at[k], sems[1].at[k], peer).start()

    def finish(ins, outs, sems):
        for w, k, me, peer_chip, peer in each():
            got = outs[w].at[peer_chip]
            _remote(got, got, sems[0].at[k], sems[1].at[k], peer).wait_recv()
        for w, k, me, peer_chip, peer in each():
            _remote(ins[w].at[peer_chip], outs[w].at[me], sems[0].at[k], sems[1].at[k], peer).wait_send()

    return start, finish


def _pair_gather(halves):
    n_w = len(halves)

    def body(*refs):
        ins, outs = refs[:n_w], refs[n_w:2 * n_w]
        send, recv = refs[2 * n_w:]
        x, y, c, _ = _place()
        sib = (x, y, 1 - c)
        cps = []
        for w in range(n_w):
            cp = _remote(ins[w], outs[w], send.at[w], recv.at[w], sib)
            cp.start()
            cps.append(cp)
        for cp in cps:
            cp.wait()

    out_shapes = [jax.ShapeDtypeStruct(h.shape, h.dtype) for h in halves]
    return _comm_call("grad_pair_gather", body, halves, out_shapes, [n_w] * 2)


def _all_exchange_stages():
    def each():
        x, y, c, _ = _place()
        for k in range(1, 8):
            px, py, pc = x ^ ((k >> 2) & 1), y ^ ((k >> 1) & 1), c ^ (k & 1)
            yield k - 1, 4 * x + 2 * y + c, 4 * px + 2 * py + pc, (px, py, pc)

    def start(ins, outs, sems):
        for k, me, _, peer in each():
            _remote(ins[0], outs[0].at[me], sems[0].at[k], sems[1].at[k], peer).start()

    def finish(ins, outs, sems):
        for k, me, src, peer in each():
            got = outs[0].at[src]
            _remote(got, got, sems[0].at[k], sems[1].at[k], peer).wait_recv()
        for k, me, src, peer in each():
            _remote(ins[0], outs[0].at[me], sems[0].at[k], sems[1].at[k], peer).wait_send()

    return start, finish


def _row_tile(r):
    for t in (256, 128, 176, 64, 32, 16, 8):
        if r % t == 0:
            return t
    return r


P_C, P_CHIP, P_DEV = 2, 3, 4


def _cast_shard(w2):
    r, c = w2.shape
    t = _row_tile(r)
    blk, imap = _rows(t, c)
    return _ew("cast_own", (r // t,), [(w2, blk, imap)], [((r, c), BF16, blk, imap)], lambda pids, a: ((a,), ()))[0]


def _cast_into_slot(w2, place):
    r, c = w2.shape
    t = _row_tile(r)
    return _ew("cast_shard", (r // t,), [(w2, (t, c), lambda i, pv: (i, 0))],
               [((N_CHIPS, r, c), BF16, (None, t, c), lambda i, pv: (pv[P_CHIP], i, 0))],
               lambda pids, a: ((a,), ()), place=place)[0]


def _pair_sum(mine, theirs, place):
    _, r, c = theirs.shape
    t = _row_tile(r)
    own = ((None, None, t, c), lambda q, i, pv: (q, pv[P_C], i, 0))
    blk = ((None, t, c), lambda q, i, pv: (q, i, 0))
    return _ew("grad_pair_sum", (N_CHIPS, r // t), [(mine, *own), (theirs, *blk)], [((N_CHIPS, r, c), BF16, *blk)],
               lambda pids, a, b: ((a + b,), ()), place=place)[0]


def _chip_sum(own, got, place):
    _, r, c = own.shape
    t = _row_tile(r)
    ins = []
    for q in range(N_CHIPS):
        ins.append((own, (None, t, c), (lambda i, pv, q=q: (q, i, 0))))
        ins.append((got, (None, t, c), (lambda i, pv, q=q: (jnp.where(pv[P_CHIP] == q, (q + 1) % N_CHIPS, q), i, 0))))

    def fn(pids, *tiles):
        me = pids[0][P_CHIP]
        tot = None
        for q in range(N_CHIPS):
            term = jnp.where(me == q, tiles[2 * q], tiles[2 * q + 1]).astype(F32)
            tot = term if tot is None else tot + term
        return (tot,), ()

    return _ew("grad_chip_sum", (r // t,), ins, [((r, c), F32, (t, c), lambda i, pv: (i, 0))], fn, place=place)[0]


def _adamw_tile(w, g, m, v):
    m = ADAM_B1 * m + (1.0 - ADAM_B1) * g
    v = ADAM_B2 * v + (1.0 - ADAM_B2) * (g * g)
    m_hat = m / (1.0 - ADAM_B1 ** ADAM_STEP)
    v_hat = v / (1.0 - ADAM_B2 ** ADAM_STEP)
    delta = -ADAM_LR * (m_hat / (jnp.sqrt(v_hat) + ADAM_EPS) + ADAM_WD * w)
    return delta, m, v


def _adamw(name, g2, w2, m2, v2):
    r, c = w2.shape
    t = _row_tile(r)
    blk, imap = _rows(t, c)

    def fn(pids, g, w, m, v):
        delta, nm, nv = _adamw_tile(w, g, m, v)
        return (g, delta, nm, nv), ()

    return _ew(name, (r // t,), [(a, blk, imap) for a in (g2, w2, m2, v2)], [((r, c), F32, blk, imap)] * 4, fn)


def _adamw_halves(name, mine, theirs, w2, m2, v2, place):
    r, c = w2.shape
    t = _row_tile(r // 2)
    n_t = (r // 2) // t
    half = ((t, c), lambda h, i, pv: (i, 0))
    whole = ((t, c), lambda h, i, pv: (h * n_t + i, 0))

    def fn(pids, ga, gb, w, m, v):
        g = jnp.where(pids[1] == pids[0][P_C], ga, gb)
        delta, nm, nv = _adamw_tile(w, g, m, v)
        return (g, delta, nm, nv), ()

    return _ew(name, (2, n_t), [(mine, *half), (theirs, *half), (w2, *whole), (m2, *whole), (v2, *whole)],
               [((r, c), F32, *whole)] * 4, fn, place=place)


def _device_sum(own, got, place):
    r, c = own.shape
    t = _row_tile(r)
    ins = [(own, (t, c), lambda i, pv: (i, 0))]
    for q in range(8):
        ins.append((got, (None, t, c), (lambda i, pv, q=q: (jnp.where(pv[P_DEV] == q, (q + 1) % 8, q), i, 0))))

    def fn(pids, mine, *parts):
        me = pids[0][P_DEV]
        tot = None
        for q in range(8):
            term = jnp.where(me == q, mine, parts[q])
            tot = term if tot is None else tot + term
        return (tot,), ()

    return _ew("small_device_sum", (r // t,), ins, [((r, c), F32, (t, c), lambda i, pv: (i, 0))], fn, place=place)[0]


def _pack(parts):
    flat = jnp.concatenate([a.reshape(-1) for a in parts])
    pad = (-flat.shape[0]) % (SUB * 128)
    return jnp.pad(flat, (0, pad)).reshape(-1, 128)


def _unpack(mat, shapes):
    flat = mat.reshape(-1)
    out, off = [], 0
    for shp in shapes:
        n = math.prod(shp)
        out.append(flat[off:off + n].reshape(shp))
        off += n
    return out


def kernel(x, p, positions, g_mix, w_in, a_re, a_im, log_dt, b_re, b_im, c_re, c_im, d_skip, w_attn_proj, w_glu_a, w_glu_b, w_out, g_ffn, w_ffn_gate, w_ffn_up, w_ffn_down, w_ple_gate, w_ple_proj, g_final, loss_target, m_g_mix, m_w_in, m_a_re, m_a_im, m_log_dt, m_b_re, m_b_im, m_c_re, m_c_im, m_d_skip, m_w_attn_proj, m_w_glu_a, m_w_glu_b, m_w_out, m_g_ffn, m_w_ffn_gate, m_w_ffn_up, m_w_ffn_down, m_w_ple_gate, m_w_ple_proj, m_g_final, v_g_mix, v_w_in, v_a_re, v_a_im, v_log_dt, v_b_re, v_b_im, v_c_re, v_c_im, v_d_skip, v_w_attn_proj, v_w_glu_a, v_w_glu_b, v_w_out, v_g_ffn, v_w_ffn_gate, v_w_ffn_up, v_w_ffn_down, v_w_ple_gate, v_w_ple_proj, v_g_final):
    given = dict(locals())
    big_w = {n: given[n] for n in BIG}
    w_mats = {n: big_w[n].reshape(big_w[n].shape[1:]) for n in BIG}

    ax, ay, ac = lax.axis_index("x"), lax.axis_index("y"), lax.axis_index("c")
    place = jnp.stack([ax, ay, ac, 2 * ax + ay, 4 * ax + 2 * ay + ac]).astype(jnp.int32)

    bufs = []
    for n in BIG:
        r, c = w_mats[n].shape
        bufs.append(_cast_into_slot(w_mats[n], place).reshape(N_CHIPS, 2, r // 2, c))
    w_in_own = _cast_shard(w_mats["w_in"])

    sm = {
        "g_mix": g_mix.reshape(1, D_MODEL), "g_ffn": g_ffn.reshape(1, D_MODEL), "g_final": g_final.reshape(1, D_MODEL),
        "a_re": a_re[0], "a_im": a_im[0], "log_dt": log_dt[0], "b_re": b_re[0], "b_im": b_im[0], "c_re": c_re[0],
        "c_im": c_im[0], "d_skip": d_skip[0],
    }
    s = x.shape[1]
    grad_x, d_w_in, early_parts, early_got, vec, vec_got = _local_step(
        x[0], p[0, 0], positions[0], loss_target[0], sm, w_in_own, bufs[0], bufs[1:], place)

    r_in, c_in = w_mats["w_in"].shape
    g5_in = [d_w_in.reshape(N_CHIPS, 2, r_in // 2, c_in)]
    in_parts = [_pair_sum(g, t, place) for g, t in zip(g5_in, _pair_exchange(g5_in))]
    chip_parts = in_parts + list(early_parts)
    chip_got = list(_chip_exchange(in_parts)) + list(early_got)
    halves = [_chip_sum(own, got, place) for own, got in zip(chip_parts, chip_got)]
    other_halves = _pair_gather(halves)

    results = {}
    for n, mine, other in zip(BIG, halves, other_halves):
        r, c = w_mats[n].shape
        shp = big_w[n].shape
        outs = _adamw_halves("adamw_" + n, mine, other, w_mats[n], given["m_" + n].reshape(r, c),
                             given["v_" + n].reshape(r, c), place)
        results[n] = [o.reshape(shp) for o in outs]

    small_shapes = [given[n].shape for n in SMALL]
    tot = _device_sum(vec, vec_got, place)
    n_small = sum(math.prod(shp) for shp in small_shapes)
    loss = tot.reshape(-1)[n_small]
    w_s = _pack([given[n] for n in SMALL])
    m_s = _pack([given["m_" + n] for n in SMALL])
    v_s = _pack([given["v_" + n] for n in SMALL])
    rows_s = w_s.shape[0]
    g_s = tot.reshape(-1)[: rows_s * 128].reshape(rows_s, 128)
    outs_s = _adamw("adamw_small", g_s, w_s, m_s, v_s)
    for kind, mat in enumerate(outs_s):
        for n, arr in zip(SMALL, _unpack(mat, small_shapes)):
            results.setdefault(n, [None] * 4)[kind] = arr

    order = ("g_mix", "w_in", "a_re", "a_im", "log_dt", "b_re", "b_im", "c_re", "c_im", "d_skip", "w_attn_proj", "w_glu_a",
             "w_glu_b", "w_out", "g_ffn", "w_ffn_gate", "w_ffn_up", "w_ffn_down", "w_ple_gate", "w_ple_proj", "g_final")
    out = [loss, grad_x.reshape(1, s, D_MODEL)]
    for kind in range(4):
        out += [results[n][kind] for n in order]
    return tuple(out)
```

```python
import math

import jax
import jax.numpy as jnp
from jax import lax
from jax.experimental import pallas as pl
from jax.experimental.pallas import tpu as pltpu

F32 = jnp.float32
BF16 = jnp.bfloat16

D_MODEL = 1024
HEAD_DIM = 128
HEADS_PER_GROUP = 4
GROUP_WIDTH = HEADS_PER_GROUP * HEAD_DIM
GROUP_DILATIONS = (1, 4, 16)
N_GROUPS = len(GROUP_DILATIONS)
LSE_LANES = 32
LSE_WIDTH = HEADS_PER_GROUP * LSE_LANES
ATTN_BLOCK = 128
ROPE_DIM = 32
ROPE_HALF = 16
ROPE_THETA = 500000.0
SSM_WIDTH = 512
SSM_GROUPS = 32
SSM_GROUP = 16
SSM_STATE = 64
N_STATE = SSM_GROUPS * SSM_STATE
SSM_SUPER = 4
IN_WIDTH = 7168
COL_U = 4608
COL_GA = 5120
COL_GS = 6144
D_FF = 2816
N_CHIPS = 4
D_FF_Q = D_FF // N_CHIPS
PLE_DIM = 256
EPS = 1e-6
ADAM_LR = 0.001
ADAM_B1 = 0.9
ADAM_B2 = 0.999
ADAM_EPS = 1e-08
ADAM_WD = 0.01
ADAM_STEP = 10
NEG_BIG = -1e30
VMEM_LIMIT_BYTES = 56 * 1024 * 1024
MESH = pl.DeviceIdType.MESH

_DIMS = {
    "nn": (((1,), (0,)), ((), ())),
    "nt": (((1,), (1,)), ((), ())),
    "tn": (((0,), (0,)), ((), ())),
}


def _params(n_grid):
    return pltpu.CompilerParams(dimension_semantics=("arbitrary",) * n_grid, vmem_limit_bytes=VMEM_LIMIT_BYTES)


def _sig(v):
    return 1.0 / (1.0 + jnp.exp(-v))


def _dot(a, b, mode):
    return lax.dot_general(a, b, _DIMS[mode], preferred_element_type=F32)


def _mm(name, grid, pairs, mode, outs, epilogue=None, extras=(), acc_outs=(), acc_shape=None, j_outer=False,
        sum_pairs=True, resident_b=False, comm=None, place=None, fill=None, prologue=None):
    gi, gj, gk = grid
    n_p, n_e, n_o, n_a = len(pairs), len(extras), len(outs), len(acc_outs)
    assert not n_a or gj == 1
    assert sum_pairs or gk == 1
    run_grid = (gj, gi, gk) if j_outer else grid
    c_ins = list(comm["ins"]) if comm else []
    c_outs = list(comm["outs"]) if comm else []
    c_sems = list(comm["sems"]) if comm else []
    n_ci, n_co, n_cs = len(c_ins), len(c_outs), len(c_sems)
    n_s = 0 if place is None else 1
    n_fill = 0 if fill is None else 1

    def order(imap):
        if place is None:
            return (lambda j, i, k: imap(i, j, k)) if j_outer else imap
        return (lambda j, i, k, pv: imap(i, j, k, pv)) if j_outer else imap

    shared_a = [pr[0] is None for pr in pairs]
    n_in = 2 * n_p - sum(shared_a)

    def body(*refs):
        refs = refs[n_s:]
        pair_refs = list(refs[:n_in])
        extra_refs = refs[n_in: n_in + n_e]
        comm_in = refs[n_in + n_e: n_in + n_e + n_ci]
        at = n_in + n_e + n_ci + n_fill
        out_refs = refs[at: at + n_o]
        sum_refs = refs[at + n_o: at + n_o + n_a]
        comm_out = refs[at + n_o + n_a: at + n_o + n_a + n_co]
        scratch_refs = refs[at + n_o + n_a + n_co:]
        i = pl.program_id(1 if j_outer else 0)
        k = pl.program_id(2)
        if comm:
            step = (pl.program_id(0) * run_grid[1] + pl.program_id(1)) * run_grid[2] + pl.program_id(2)
            sems = scratch_refs[len(scratch_refs) - n_cs:]
            for at_step, stage in comm["stages"]:
                @pl.when(step == at_step)
                def _(stage=stage):
                    stage(comm_in, comm_out, sems)
        part = None if sum_pairs else []
        a = None
        for t in range(n_p):
            if not shared_a[t]:
                a = pair_refs.pop(0)[...]
                if prologue is not None and t == 0:
                    a = prologue(a, *[e[...] for e in extra_refs]).astype(BF16)
                    out_refs[n_o - 1][...] = a
                a = a.astype(BF16)
            b = pair_refs.pop(0)[...].astype(BF16)
            d = _dot(a, b, mode)
            if sum_pairs:
                part = d if part is None else part + d
            else:
                part.append(d)

        def finish(acc):
            tiles, sums = epilogue(acc, *[e[...] for e in extra_refs]) if epilogue is not None else ((acc,), ())
            for o_ref, tile in zip(out_refs, tiles):
                o_ref[...] = tile.astype(o_ref.dtype)
            if n_a:
                @pl.when(i == 0)
                def _():
                    for s_ref in sum_refs:
                        s_ref[...] = jnp.zeros_like(s_ref)

                for s_ref, s in zip(sum_refs, sums):
                    s_ref[...] += s

        if gk == 1:
            finish(part)
        else:
            acc_ref = scratch_refs[0]

            @pl.when(k == 0)
            def _():
                acc_ref[...] = part

            @pl.when(k > 0)
            def _():
                acc_ref[...] += part

            @pl.when(k == gk - 1)
            def _():
                finish(acc_ref[...])

    in_specs, args = [], []
    for a, a_block, a_imap, b, b_block, b_imap in pairs:
        if a is not None:
            in_specs.append(pl.BlockSpec(a_block, order(a_imap)))
            args.append(a)
        if resident_b:
            in_specs.append(pl.BlockSpec(b_block, order(b_imap), pipeline_mode=pl.Buffered(1)))
        else:
            in_specs.append(pl.BlockSpec(b_block, order(b_imap)))
        args.append(b)
    for e, e_block, e_imap in extras:
        in_specs.append(pl.BlockSpec(e_block, order(e_imap)))
        args.append(e)
    first_comm_in = len(args)
    for c_in in c_ins:
        in_specs.append(pl.BlockSpec(memory_space=pl.ANY))
        args.append(c_in)
    if n_fill:
        in_specs.append(pl.BlockSpec(memory_space=pl.ANY))
        args.append(fill)
    out_shape = [jax.ShapeDtypeStruct(shape, dtype) for shape, dtype, _, _ in outs]
    out_specs = [pl.BlockSpec(block, order(imap)) for _, _, block, imap in outs]
    for shape, dtype in acc_outs:
        out_shape.append(jax.ShapeDtypeStruct(shape, dtype))
        out_specs.append(pl.BlockSpec(shape, lambda *_: (0, 0)))
    first_comm_out = len(out_shape)
    for c_out in c_outs:
        out_shape.append(c_out)
        out_specs.append(pl.BlockSpec(memory_space=pl.ANY))
    aliases = {n_s + first_comm_in + n: first_comm_out + n for n in range(n_ci)} if comm and comm["aliased"] else {}
    if n_fill:
        aliases[n_s + len(args) - 1] = 0
    scratch = [pltpu.VMEM(acc_shape, F32)] if gk > 1 else []
    scratch += [pltpu.SemaphoreType.DMA((n,)) for n in c_sems]
    if n_s:
        spec = pltpu.PrefetchScalarGridSpec(num_scalar_prefetch=1, grid=run_grid, in_specs=in_specs, out_specs=out_specs,
                                            scratch_shapes=scratch)
        return pl.pallas_call(body, name=name, grid_spec=spec, out_shape=out_shape, compiler_params=_params(3),
                              input_output_aliases=aliases)(place, *args)
    return pl.pallas_call(
        body, name=name, grid=run_grid, in_specs=in_specs, out_specs=out_specs,
        out_shape=out_shape, scratch_shapes=scratch, compiler_params=_params(3), input_output_aliases=aliases,
    )(*args)


def _ew(name, grid, ins, outs, fn, acc_outs=(), place=None):
    n_i, n_o, n_a = len(ins), len(outs), len(acc_outs)
    ng = len(grid)
    n_s = 0 if place is None else 1

    def body(*refs):
        in_refs = refs[n_s: n_s + n_i]
        out_refs = refs[n_s + n_i: n_s + n_i + n_o]
        sum_refs = refs[n_s + n_i + n_o:]
        pids = tuple(pl.program_id(a) for a in range(ng))
        if n_s:
            pids = (refs[0],) + pids
        tiles, sums = fn(pids, *[r[...] for r in in_refs])
        for o_ref, tile in zip(out_refs, tiles):
            o_ref[...] = tile.astype(o_ref.dtype)
        if n_a:
            first = pids[0] == 0
            for p_ in pids[1:]:
                first = jnp.logical_and(first, p_ == 0)

            @pl.when(first)
            def _():
                for s_ref in sum_refs:
                    s_ref[...] = jnp.zeros_like(s_ref)

            for s_ref, s in zip(sum_refs, sums):
                s_ref[...] += s

    in_specs = [pl.BlockSpec(block, imap) for _, block, imap in ins]
    out_shape = [jax.ShapeDtypeStruct(shape, dtype) for shape, dtype, _, _ in outs]
    out_specs = [pl.BlockSpec(block, imap) for _, _, block, imap in outs]
    for shape, dtype in acc_outs:
        out_shape.append(jax.ShapeDtypeStruct(shape, dtype))
        out_specs.append(pl.BlockSpec(shape, lambda *_, nd=len(shape): (0,) * nd))
    arrays = [a for a, _, _ in ins]
    if n_s:
        assert not n_a
        spec = pltpu.PrefetchScalarGridSpec(num_scalar_prefetch=1, grid=grid, in_specs=in_specs, out_specs=out_specs)
        return pl.pallas_call(body, name=name, grid_spec=spec, out_shape=out_shape, compiler_params=_params(ng))(
            place, *arrays)
    return pl.pallas_call(
        body, name=name, grid=grid, in_specs=in_specs, out_specs=out_specs, out_shape=out_shape,
        compiler_params=_params(ng),
    )(*arrays)


def _rows(tm, width):
    return (tm, width), (lambda i: (i, 0))


def _rms_fwd_tile(h, g):
    r = lax.rsqrt(jnp.mean(h * h, axis=-1, keepdims=True) + EPS)
    return h * r * g


def _rms_bwd_tile(dn, h, g):
    r = lax.rsqrt(jnp.mean(h * h, axis=-1, keepdims=True) + EPS)
    hhat = h * r
    gy = dn * g
    dh = r * (gy - hhat * jnp.mean(gy * hhat, axis=-1, keepdims=True))
    dg = jnp.sum(dn * hhat, axis=0, keepdims=True)
    return dh, dg


def _rope_tables(pos_col, inv_row, tm):
    s = pos_col.shape[0]

    def fn(pids, pos, inv):
        ang = pos * inv
        lane = lax.broadcasted_iota(jnp.int32, ang.shape, 1)
        cs = jnp.where(lane < ROPE_DIM, jnp.cos(ang), 1.0)
        sn = jnp.sin(ang)
        s_lo = jnp.where(lane < ROPE_HALF, -sn, 0.0)
        s_hi = jnp.where(jnp.logical_and(lane >= ROPE_HALF, lane < ROPE_DIM), sn, 0.0)
        return (cs, s_lo, s_hi), ()

    blk, imap = _rows(tm, 128)
    return _ew(
        "rope_tables", (s // tm,),
        [(pos_col, (tm, 1), lambda i: (i, 0)), (inv_row, (1, 128), lambda i: (0, 0))],
        [((s, 128), F32, blk, imap)] * 3, fn,
    )


def _rope(xh, cs, s_lo, s_hi):
    return xh * cs + pltpu.roll(xh, HEAD_DIM - ROPE_HALF, 1) * s_lo + pltpu.roll(xh, ROPE_HALF, 1) * s_hi


def _rope_t(gh, cs, s_lo, s_hi):
    return gh * cs + pltpu.roll(gh * s_lo, ROPE_HALF, 1) + pltpu.roll(gh * s_hi, HEAD_DIM - ROPE_HALF, 1)


def _attn_geometry(length):
    nb = length // ATTN_BLOCK
    gq = min(8, nb)
    assert nb % gq == 0
    return nb, gq, gq * ATTN_BLOCK, nb // gq


def _band_masks():
    qi = lax.broadcasted_iota(jnp.int32, (ATTN_BLOCK, ATTN_BLOCK), 0)
    kj = lax.broadcasted_iota(jnp.int32, (ATTN_BLOCK, ATTN_BLOCK), 1)
    return kj <= qi, kj >= qi


def _band_mask_pair():
    qi = lax.broadcasted_iota(jnp.int32, (ATTN_BLOCK, 2 * ATTN_BLOCK), 0)
    cj = lax.broadcasted_iota(jnp.int32, (ATTN_BLOCK, 2 * ATTN_BLOCK), 1)
    in_cur = cj >= ATTN_BLOCK
    band = jnp.logical_or(jnp.logical_and(in_cur, cj - ATTN_BLOCK <= qi),
                          jnp.logical_and(cj < ATTN_BLOCK, cj >= qi))
    return band, in_cur


def _attn_fwd(qv, kv, vv, dil, cols3=(0, 0, 0)):
    length = qv.shape[0]
    nb, gq, rows, ni = _attn_geometry(length)

    def body(q_ref, kc_ref, kp_ref, vc_ref, vp_ref, o_ref, l_ref):
        i = pl.program_id(1)
        band, in_cur = _band_mask_pair()
        band_first = jnp.logical_and(band, jnp.logical_or(in_cur, i > 0))
        work = []
        for h in range(HEADS_PER_GROUP):
            cols = slice(h * HEAD_DIM, (h + 1) * HEAD_DIM)
            qh = q_ref[:, cols]
            k_all = jnp.concatenate([kp_ref[:, cols], kc_ref[:, cols]], axis=0)
            v_all = jnp.concatenate([vp_ref[:, cols], vc_ref[:, cols]], axis=0)
            for jj in range(gq):
                rws = slice(jj * ATTN_BLOCK, (jj + 1) * ATTN_BLOCK)
                two = slice(jj * ATTN_BLOCK, (jj + 2) * ATTN_BLOCK)
                work.append(dict(h=h, rws=rws, cols=cols, v=v_all[two], first=jj == 0, s=_dot(qh[rws], k_all[two], "nt")))
        for w in work:
            s = jnp.where(band_first if w["first"] else band, w["s"], NEG_BIG)
            m = jnp.max(s, axis=-1, keepdims=True)
            pexp = jnp.exp(s - m)
            w["den"] = jnp.sum(pexp, axis=-1, keepdims=True)
            w["p"] = pexp.astype(BF16)
            w["lse"] = m + jnp.log(w["den"])
        for w in work:
            o = _dot(w["p"], w["v"], "nn")
            o_ref[w["rws"], w["cols"]] = (o * (1.0 / w["den"])).astype(o_ref.dtype)
            l_ref[w["rws"], w["h"] * LSE_LANES:(w["h"] + 1) * LSE_LANES] = jnp.broadcast_to(w["lse"], (ATTN_BLOCK, LSE_LANES))

    def cur(c):
        return pl.BlockSpec((rows, GROUP_WIDTH), lambda r, i: (i, r + c))

    def prev(c):
        return pl.BlockSpec((ATTN_BLOCK, GROUP_WIDTH), lambda r, i: (jnp.maximum(i * gq - 1, 0), r + c))

    cq, ck, cv = cols3
    return pl.pallas_call(
        body, name=f"attn_fwd_d{dil}", grid=(dil, ni),
        in_specs=[cur(cq), cur(ck), prev(ck), cur(cv), prev(cv)],
        out_specs=[cur(0), pl.BlockSpec((rows, LSE_WIDTH), lambda r, i: (i, r))],
        out_shape=[jax.ShapeDtypeStruct((length, dil * GROUP_WIDTH), BF16),
                   jax.ShapeDtypeStruct((length, dil * LSE_WIDTH), F32)],
        compiler_params=_params(2),
    )(qv, kv, kv, vv, vv)


def _attn_bwd(qv, kv, vv, dov, ov, lv, dil, cols3=(0, 0, 0)):
    length = qv.shape[0]
    nb, gq, rows, ni = _attn_geometry(length)
    out_shape = (length, dil * GROUP_WIDTH)

    def body(qc_ref, qn_ref, kc_ref, kp_ref, vc_ref, vp_ref, doc_ref, don_ref, oc_ref, on_ref, lc_ref, ln_ref,
             dq_ref, dk_ref, dv_ref):
        i = pl.program_id(1)
        _, mask_p = _band_masks()
        band, in_cur = _band_mask_pair()
        band_first = jnp.logical_and(band, jnp.logical_or(in_cur, i > 0))
        has_next = i < ni - 1

        last = slice(gq * ATTN_BLOCK, (gq + 1) * ATTN_BLOCK)
        mask_next = jnp.logical_and(mask_p, has_next)

        def rows_of(jj):
            return slice(jj * ATTN_BLOCK, (jj + 1) * ATTN_BLOCK)

        def keys_of(jj):
            return slice(jj * ATTN_BLOCK, (jj + 2) * ATTN_BLOCK)

        heads = []
        for h in range(HEADS_PER_GROUP):
            cols = slice(h * HEAD_DIM, (h + 1) * HEAD_DIM)
            hd = dict(
                cols=cols, q_c=qc_ref[:, cols], q_n=qn_ref[:, cols],
                k_all=jnp.concatenate([kp_ref[:, cols], kc_ref[:, cols]], axis=0),
                v_all=jnp.concatenate([vp_ref[:, cols], vc_ref[:, cols]], axis=0),
                do_c=doc_ref[:, cols], do_n=don_ref[:, cols],
                l_c=lc_ref[:, h * LSE_LANES:h * LSE_LANES + 1], l_n=ln_ref[:, h * LSE_LANES:h * LSE_LANES + 1],
            )
            hd["dl_c"] = jnp.sum(hd["do_c"].astype(F32) * oc_ref[:, cols].astype(F32), axis=-1, keepdims=True)
            hd["dl_n"] = jnp.sum(hd["do_n"].astype(F32) * on_ref[:, cols].astype(F32), axis=-1, keepdims=True)
            hd["s"] = [_dot(hd["q_c"][rows_of(jj)], hd["k_all"][keys_of(jj)], "nt") for jj in range(gq)]
            hd["dp"] = [_dot(hd["do_c"][rows_of(jj)], hd["v_all"][keys_of(jj)], "nt") for jj in range(gq)]
            hd["s"].append(_dot(hd["q_n"], hd["k_all"][last], "nt"))
            hd["dp"].append(_dot(hd["do_n"], hd["v_all"][last], "nt"))
            heads.append(hd)
        for hd in heads:
            hd["p"], hd["ds"] = [], []
            for jj in range(gq + 1):
                if jj < gq:
                    mask, l_col, delta = (band_first if jj == 0 else band), hd["l_c"][rows_of(jj)], hd["dl_c"][rows_of(jj)]
                else:
                    mask, l_col, delta = mask_next, hd["l_n"], hd["dl_n"]
                p = jnp.where(mask, jnp.exp(hd["s"][jj] - l_col), 0.0)
                hd["p"].append(p.astype(BF16))
                hd["ds"].append((p * (hd["dp"][jj] - delta)).astype(BF16))
        for hd in heads:
            cols = hd["cols"]
            dk_blocks, dv_blocks = [None] * (gq + 1), [None] * (gq + 1)

            def add(lst, idx, val):
                lst[idx] = val if lst[idx] is None else lst[idx] + val

            for jj in range(gq):
                qb, dob = hd["q_c"][rows_of(jj)], hd["do_c"][rows_of(jj)]
                dq_ref[rows_of(jj), cols] = _dot(hd["ds"][jj], hd["k_all"][keys_of(jj)], "nn").astype(dq_ref.dtype)
                dk2 = _dot(hd["ds"][jj], qb, "tn")
                dv2 = _dot(hd["p"][jj], dob, "tn")
                add(dk_blocks, jj, dk2[:ATTN_BLOCK])
                add(dk_blocks, jj + 1, dk2[ATTN_BLOCK:])
                add(dv_blocks, jj, dv2[:ATTN_BLOCK])
                add(dv_blocks, jj + 1, dv2[ATTN_BLOCK:])
            add(dk_blocks, gq, _dot(hd["ds"][gq], hd["q_n"], "tn"))
            add(dv_blocks, gq, _dot(hd["p"][gq], hd["do_n"], "tn"))
            for jj in range(gq):
                dk_ref[rows_of(jj), cols] = dk_blocks[jj + 1].astype(dk_ref.dtype)
                dv_ref[rows_of(jj), cols] = dv_blocks[jj + 1].astype(dv_ref.dtype)

    def cur(c):
        return pl.BlockSpec((rows, GROUP_WIDTH), lambda r, i: (i, r + c))

    def prev(c):
        return pl.BlockSpec((ATTN_BLOCK, GROUP_WIDTH), lambda r, i: (jnp.maximum(i * gq - 1, 0), r + c))

    def nxt(c):
        return pl.BlockSpec((ATTN_BLOCK, GROUP_WIDTH), lambda r, i: (jnp.minimum((i + 1) * gq, nb - 1), r + c))

    cq, ck, cv = cols3
    lse_cur = pl.BlockSpec((rows, LSE_WIDTH), lambda r, i: (i, r))
    lse_next = pl.BlockSpec((ATTN_BLOCK, LSE_WIDTH), lambda r, i: (jnp.minimum((i + 1) * gq, nb - 1), r))
    return pl.pallas_call(
        body, name=f"attn_bwd_d{dil}", grid=(dil, ni),
        in_specs=[cur(cq), nxt(cq), cur(ck), prev(ck), cur(cv), prev(cv), cur(0), nxt(0), cur(0), nxt(0), lse_cur, lse_next],
        out_specs=[cur(0), cur(0), cur(0)],
        out_shape=[jax.ShapeDtypeStruct(out_shape, BF16)] * 3,
        compiler_params=_params(2),
    )(qv, qv, kv, kv, vv, vv, dov, dov, ov, ov, lv, lv)


DILATED = tuple((g, d) for g, d in enumerate(GROUP_DILATIONS) if d > 1)


def _spread(scr, slot, tile, out_ref, dil, col, width=GROUP_WIDTH):
    tm = tile.shape[0]
    buf = scr.at[slot]
    buf[...] = tile
    for r in range(dil):
        c0 = r * width + col
        out_ref[:, c0:c0 + HEAD_DIM] = buf[pl.ds(r, tm // dil, stride=dil), :].astype(out_ref.dtype)


def _collect(scr, slot, in_ref, dil, col, width=GROUP_WIDTH):
    tm = scr.shape[1]
    buf = scr.at[slot]
    for r in range(dil):
        c0 = r * width + col
        buf[pl.ds(r, tm // dil, stride=dil), :] = in_ref[:, c0:c0 + HEAD_DIM].astype(F32)
    return buf[...]


def _view_spec(tm, dil, width=GROUP_WIDTH):
    return pl.BlockSpec((tm // dil, dil * width), lambda i: (i, 0))


def _view_shape(s, dil, dtype, width=GROUP_WIDTH):
    return jax.ShapeDtypeStruct((s // dil, dil * width), dtype)


def _qkv_layout(z, tabs, tm):
    s = z.shape[0]
    scale = 1.0 / math.sqrt(HEAD_DIM)
    qkv_width = 3 * N_GROUPS * GROUP_WIDTH

    def body(z_ref, cs_ref, lo_ref, hi_ref, qk0_ref, *rest):
        views, scr = rest[:-1], rest[-1]
        tabs_ = (cs_ref[...], lo_ref[...], hi_ref[...])
        for part in range(3):
            for g, dil in enumerate(GROUP_DILATIONS):
                if part == 2 and dil == 1:
                    continue
                for h in range(HEADS_PER_GROUP):
                    col = part * N_GROUPS * GROUP_WIDTH + g * GROUP_WIDTH + h * HEAD_DIM
                    t = z_ref[:, col:col + HEAD_DIM].astype(F32)
                    if part < 2:
                        t = _rope(t, *tabs_)
                    if part == 0:
                        t = t * scale
                    if dil == 1:
                        c0 = part * GROUP_WIDTH + h * HEAD_DIM
                        qk0_ref[:, c0:c0 + HEAD_DIM] = t.astype(BF16)
                    else:
                        out = views[3 * [gg for gg, _ in DILATED].index(g) + part]
                        _spread(scr, h, t, out, dil, h * HEAD_DIM)

    row = lambda i: (i, 0)
    out_shape = [jax.ShapeDtypeStruct((s, 2 * GROUP_WIDTH), BF16)]
    out_specs = [pl.BlockSpec((tm, 2 * GROUP_WIDTH), row)]
    for _, dil in DILATED:
        out_shape += [_view_shape(s, dil, BF16)] * 3
        out_specs += [_view_spec(tm, dil)] * 3
    res = pl.pallas_call(
        body, name="qkv_layout", grid=(s // tm,),
        in_specs=[pl.BlockSpec((tm, qkv_width), row)] + [pl.BlockSpec((tm, HEAD_DIM), row)] * 3,
        out_specs=out_specs, out_shape=out_shape,
        scratch_shapes=[pltpu.VMEM((HEADS_PER_GROUP, tm, HEAD_DIM), F32)], compiler_params=_params(1),
    )(z, *tabs)
    return res[0], [tuple(res[1 + 3 * n:4 + 3 * n]) for n in range(len(DILATED))]


def _attn_merge(o0, l0, dilated, tm):
    s = o0.shape[0]
    n_d = len(DILATED)

    def body(*refs):
        o0_ref, l0_ref = refs[:2]
        in_views = refs[2:2 + 2 * n_d]
        attn_ref, lse_ref = refs[2 + 2 * n_d:4 + 2 * n_d]
        out_views = refs[4 + 2 * n_d:4 + 4 * n_d]
        scr = refs[-1]
        l_rows = [l0_ref[...]] + [_collect(scr, n, in_views[2 * n + 1], dil, 0, LSE_WIDTH) for n, (_, dil) in enumerate(DILATED)]
        lse_heads = []
        for h in range(HEADS_PER_GROUP):
            cols = slice(h * HEAD_DIM, (h + 1) * HEAD_DIM)
            os_ = [o0_ref[:, cols].astype(F32)]
            for n, (_, dil) in enumerate(DILATED):
                os_.append(_collect(scr, n_d + n, in_views[2 * n], dil, h * HEAD_DIM))
            ls_ = [lr[:, h * LSE_LANES:h * LSE_LANES + 1] for lr in l_rows]
            m = ls_[0]
            for l_ in ls_[1:]:
                m = jnp.maximum(m, l_)
            es = [jnp.exp(l_ - m) for l_ in ls_]
            den = es[0]
            num = es[0] * os_[0]
            for e, o in zip(es[1:], os_[1:]):
                den = den + e
                num = num + e * o
            attn = num * (1.0 / den)
            lse_heads.append(jnp.broadcast_to(m + jnp.log(den), (tm, LSE_LANES)))
            attn_ref[:, cols] = attn.astype(BF16)
            for n, (_, dil) in enumerate(DILATED):
                _spread(scr, 2 * n_d, attn, out_views[2 * n], dil, h * HEAD_DIM)
        lse = jnp.concatenate(lse_heads, axis=1)
        lse_ref[...] = lse
        for n, (_, dil) in enumerate(DILATED):
            _spread(scr, 2 * n_d, lse, out_views[2 * n + 1], dil, 0, LSE_WIDTH)

    row = lambda i: (i, 0)
    nat = pl.BlockSpec((tm, GROUP_WIDTH), row)
    nat_l = pl.BlockSpec((tm, LSE_WIDTH), row)
    in_specs = [nat, nat_l]
    args = [o0, l0]
    out_specs = [nat, nat_l]
    out_shape = [jax.ShapeDtypeStruct((s, GROUP_WIDTH), BF16), jax.ShapeDtypeStruct((s, LSE_WIDTH), F32)]
    for (_, dil), (ov, lv) in zip(DILATED, dilated):
        in_specs += [_view_spec(tm, dil), _view_spec(tm, dil, LSE_WIDTH)]
        args += [ov, lv]
        out_specs += [_view_spec(tm, dil), _view_spec(tm, dil, LSE_WIDTH)]
        out_shape += [_view_shape(s, dil, BF16), _view_shape(s, dil, F32, LSE_WIDTH)]
    res = pl.pallas_call(
        body, name="attn_merge", grid=(s // tm,), in_specs=in_specs, out_specs=out_specs, out_shape=out_shape,
        scratch_shapes=[pltpu.VMEM((2 * n_d + 1, tm, HEAD_DIM), F32)], compiler_params=_params(1),
    )(*args)
    return res[0], res[1], [tuple(res[2 + 2 * n:4 + 2 * n]) for n in range(n_d)]


def _to_views(a, tm):
    s = a.shape[0]

    def body(a_ref, *rest):
        outs, scr = rest[:-1], rest[-1]
        for h in range(HEADS_PER_GROUP):
            t = a_ref[:, h * HEAD_DIM:(h + 1) * HEAD_DIM].astype(F32)
            for n, (_, dil) in enumerate(DILATED):
                _spread(scr, n, t, outs[n], dil, h * HEAD_DIM)

    return pl.pallas_call(
        body, name="to_views", grid=(s // tm,), in_specs=[pl.BlockSpec((tm, GROUP_WIDTH), lambda i: (i, 0))],
        out_specs=[_view_spec(tm, dil) for _, dil in DILATED], out_shape=[_view_shape(s, dil, BF16) for _, dil in DILATED],
        scratch_shapes=[pltpu.VMEM((len(DILATED), tm, HEAD_DIM), F32)], compiler_params=_params(1),
    )(a)


def _dz_layout(grads, du, dga, dgs, tabs, tm, comm=None):
    s = du.shape[0]
    scale = 1.0 / math.sqrt(HEAD_DIM)
    n_steps = s // tm
    c_ins = list(comm["ins"]) if comm else []
    c_outs = list(comm["outs"]) if comm else []
    c_sems = list(comm["sems"]) if comm else []
    n_fixed = 3 * N_GROUPS + 6

    def body(*refs):
        g_refs = refs[:3 * N_GROUPS]
        du_ref, dga_ref, dgs_ref, cs_ref, lo_ref, hi_ref = refs[3 * N_GROUPS:n_fixed]
        comm_in = refs[n_fixed:n_fixed + len(c_ins)]
        dz_ref = refs[n_fixed + len(c_ins)]
        comm_out = refs[n_fixed + len(c_ins) + 1:n_fixed + len(c_ins) + 1 + len(c_outs)]
        scr = refs[n_fixed + len(c_ins) + 1 + len(c_outs)]
        sems = refs[n_fixed + len(c_ins) + 2 + len(c_outs):]
        if comm:
            @pl.when(pl.program_id(0) == 0)
            def _():
                comm["start"](comm_in, comm_out, sems)

            @pl.when(pl.program_id(0) == n_steps - 1)
            def _():
                comm["finish"](comm_in, comm_out, sems)

        tabs_ = (cs_ref[...], lo_ref[...], hi_ref[...])
        for part in range(3):
            for g, dil in enumerate(GROUP_DILATIONS):
                src = g_refs[3 * g + part]
                for h in range(HEADS_PER_GROUP):
                    if dil == 1:
                        t = src[:, h * HEAD_DIM:(h + 1) * HEAD_DIM].astype(F32)
                    else:
                        t = _collect(scr, h, src, dil, h * HEAD_DIM)
                    if part < 2:
                        t = _rope_t(t, *tabs_)
                    if part == 0:
                        t = t * scale
                    col = part * N_GROUPS * GROUP_WIDTH + g * GROUP_WIDTH + h * HEAD_DIM
                    dz_ref[:, col:col + HEAD_DIM] = t.astype(BF16)
        dz_ref[:, COL_U:COL_GA] = du_ref[...]
        dz_ref[:, COL_GA:COL_GS] = dga_ref[...]
        dz_ref[:, COL_GS:IN_WIDTH] = dgs_ref[...]

    row = lambda i: (i, 0)
    in_specs, args = [], []
    for (g, dil), trio in zip(enumerate(GROUP_DILATIONS), grads):
        in_specs += [pl.BlockSpec((tm, GROUP_WIDTH), row) if dil == 1 else _view_spec(tm, dil)] * 3
        args += list(trio)
    in_specs += [pl.BlockSpec((tm, SSM_WIDTH), row), pl.BlockSpec((tm, D_MODEL), row), pl.BlockSpec((tm, D_MODEL), row)]
    in_specs += [pl.BlockSpec((tm, HEAD_DIM), row)] * 3
    in_specs += [pl.BlockSpec(memory_space=pl.ANY)] * len(c_ins)
    res = pl.pallas_call(
        body, name="dz_layout", grid=(n_steps,), in_specs=in_specs,
        out_specs=[pl.BlockSpec((tm, IN_WIDTH), row)] + [pl.BlockSpec(memory_space=pl.ANY)] * len(c_outs),
        out_shape=[jax.ShapeDtypeStruct((s, IN_WIDTH), BF16)] + c_outs,
        scratch_shapes=[pltpu.VMEM((HEADS_PER_GROUP, tm, HEAD_DIM), F32)] + [pltpu.SemaphoreType.DMA((n,)) for n in c_sems],
        compiler_params=_params(1),
    )(*args, du, dga, dgs, *tabs, *c_ins)
    return res[0], list(res[1:])


def _discretise(a_re, a_im, log_dt, bt_re, bt_im):
    dt = jnp.exp(log_dt)
    mag = jnp.exp(a_re * dt)
    bar_re = mag * jnp.cos(a_im * dt)
    bar_im = mag * jnp.sin(a_im * dt)
    nr = bar_re - 1.0
    ni = bar_im
    den = a_re * a_re + a_im * a_im
    z_re = (nr * a_re + ni * a_im) / den
    z_im = (ni * a_re - nr * a_im) / den
    bb_re = z_re[:, None, :] * bt_re - z_im[:, None, :] * bt_im
    bb_im = z_re[:, None, :] * bt_im + z_im[:, None, :] * bt_re
    return bar_re, bar_im, bb_re, bb_im


def _ssm_prep(a_re, a_im, log_dt, bt_re, bt_im):
    def body(ar, ai, ld, br, bi, o_lr, o_li, o_br, o_bi):
        lr, li, bbr, bbi = _discretise(ar[...], ai[...], ld[...], br[...], bi[...])
        o_lr[...] = lr
        o_li[...] = li
        o_br[...] = bbr
        o_bi[...] = bbi

    sm = jax.ShapeDtypeStruct((SSM_GROUPS, SSM_STATE), F32)
    bg = jax.ShapeDtypeStruct((SSM_GROUPS, SSM_GROUP, SSM_STATE), F32)
    return pl.pallas_call(body, name="ssm_prep", out_shape=[sm, sm, bg, bg])(a_re, a_im, log_dt, bt_re, bt_im)


def _ssm_param_bwd(a_re, a_im, log_dt, bt_re, bt_im, d_lr, d_li, d_bbr, d_bbi):
    def body(ar, ai, ld, br, bi, g_lr, g_li, g_br, g_bi, o_ar, o_ai, o_ld, o_br, o_bi):
        _, vjp = jax.vjp(_discretise, ar[...], ai[...], ld[...], br[...], bi[...])
        d_ar, d_ai, d_ld, d_br, d_bi = vjp((g_lr[...], g_li[...], g_br[...], g_bi[...]))
        o_ar[...] = d_ar
        o_ai[...] = d_ai
        o_ld[...] = d_ld
        o_br[...] = d_br
        o_bi[...] = d_bi

    sm = jax.ShapeDtypeStruct((SSM_GROUPS, SSM_STATE), F32)
    col = jax.ShapeDtypeStruct((SSM_GROUPS, 1), F32)
    bg = jax.ShapeDtypeStruct((SSM_GROUPS, SSM_GROUP, SSM_STATE), F32)
    return pl.pallas_call(body, name="ssm_param_bwd", out_shape=[sm, sm, col, bg, bg])(
        a_re, a_im, log_dt, bt_re, bt_im, d_lr, d_li, d_bbr, d_bbi)


def _block_diag(t, rows_per, cols_per):
    t4 = t.reshape(SSM_SUPER, 8, rows_per, cols_per)
    eye = jnp.eye(8, dtype=t.dtype)
    return jnp.einsum("bgrc,gh->bgrhc", t4, eye).reshape(SSM_SUPER, 8 * rows_per, 8 * cols_per)


def _block_diag_t(dense, rows_per, cols_per):
    t = dense.reshape(SSM_SUPER, 8, rows_per, 8, cols_per)
    eye = jnp.eye(8, dtype=dense.dtype)
    return jnp.einsum("bgrhc,gh->bgrc", t, eye).reshape(SSM_GROUPS, rows_per, cols_per)


def _gelu(v):
    c = math.sqrt(2.0 / math.pi)
    return 0.5 * v * (1.0 + jnp.tanh(c * (v + 0.044715 * v * v * v)))


def _gelu_grad(v):
    c = math.sqrt(2.0 / math.pi)
    t = jnp.tanh(c * (v + 0.044715 * v * v * v))
    return 0.5 * (1.0 + t) + 0.5 * v * (1.0 - t * t) * c * (1.0 + 3.0 * 0.044715 * v * v)


SUB = 8


SCAN_STEPS = (1, 2, 4)
N_SCAN_TABLES = 2 + 2 * len(SCAN_STEPS)


def _scan_tables(tab_ref, lam_re, lam_im, reverse, conj):
    lr = lam_re
    li = -lam_im if conj else lam_im
    powers = [(lr, li)]
    for _ in range(SUB - 1):
        pr, pi = powers[-1]
        powers.append((pr * lr - pi * li, pr * li + pi * lr))
    row = lax.broadcasted_iota(jnp.int32, (SUB, N_STATE), 0)
    if reverse:
        row = SUB - 1 - row
    wide = lambda v: jnp.broadcast_to(v, (SUB, N_STATE))
    p_re = jnp.zeros((SUB, N_STATE), F32)
    p_im = jnp.zeros((SUB, N_STATE), F32)
    for j in range(SUB):
        p_re = jnp.where(row == j, wide(powers[j][0]), p_re)
        p_im = jnp.where(row == j, wide(powers[j][1]), p_im)
    tab_ref[0] = p_re
    tab_ref[1] = p_im
    for idx, k in enumerate(SCAN_STEPS):
        tab_ref[2 + 2 * idx] = jnp.where(row >= k, wide(powers[k - 1][0]), 0.0)
        tab_ref[3 + 2 * idx] = jnp.where(row >= k, wide(powers[k - 1][1]), 0.0)


def _scan_rows(g_re_ref, g_im_ref, tab_ref, carry, n_rows, reverse):
    last = 0 if reverse else SUB - 1

    def tile_step(tt, state):
        cr, ci = state
        t8 = (n_rows // SUB - 1 - tt) if reverse else tt
        start = pl.multiple_of(t8 * SUB, SUB)
        xr = g_re_ref[pl.ds(start, SUB), :]
        xi = g_im_ref[pl.ds(start, SUB), :]
        for idx, k in enumerate(SCAN_STEPS):
            mr = tab_ref[2 + 2 * idx]
            mi = tab_ref[3 + 2 * idx]
            shift = SUB - k if reverse else k
            sr = pltpu.roll(xr, shift, 0)
            si = pltpu.roll(xi, shift, 0)
            xr, xi = xr + (mr * sr - mi * si), xi + (mr * si + mi * sr)
        pr = tab_ref[0]
        pi = tab_ref[1]
        xr, xi = xr + (pr * cr - pi * ci), xi + (pr * ci + pi * cr)
        g_re_ref[pl.ds(start, SUB), :] = xr
        g_im_ref[pl.ds(start, SUB), :] = xi
        return (jnp.broadcast_to(xr[last:last + 1, :], (SUB, N_STATE)),
                jnp.broadcast_to(xi[last:last + 1, :], (SUB, N_STATE)))

    return lax.fori_loop(0, n_rows // SUB, tile_step, carry)


def _ssm_fwd(z, b_re, b_im, c_re, c_im, lam_re, lam_im, d_skip, chunk):
    s = z.shape[0]

    def body(u_ref, bre, bim, cre, cim, lre, lim, dsk, hre_ref, him_ref, ys_ref, yg_ref, car_re, car_im, tabs):
        i = pl.program_id(0)

        @pl.when(i == 0)
        def _():
            car_re[...] = jnp.zeros_like(car_re)
            car_im[...] = jnp.zeros_like(car_im)
            _scan_tables(tabs, lre[...], lim[...], False, False)

        u = u_ref[...]
        for b in range(SSM_SUPER):
            ub = u[:, b * 128:(b + 1) * 128]
            st = slice(b * 512, (b + 1) * 512)
            hre_ref[:, st] = _dot(ub, bre[b], "nn")
            him_ref[:, st] = _dot(ub, bim[b], "nn")
        sr, si = _scan_rows(hre_ref, him_ref, tabs, (car_re[...], car_im[...]), chunk, False)
        car_re[...] = sr
        car_im[...] = si
        uf = u.astype(F32)
        for b in range(SSM_SUPER):
            st = slice(b * 512, (b + 1) * 512)
            ch = slice(b * 128, (b + 1) * 128)
            y = _dot(hre_ref[:, st].astype(BF16), cre[b], "nn") - _dot(him_ref[:, st].astype(BF16), cim[b], "nn")
            y = y + dsk[:, ch] * uf[:, ch]
            ys_ref[:, ch] = y
            yg_ref[:, ch] = _gelu(y).astype(BF16)

    full3 = lambda i: (0, 0, 0)
    full2 = lambda i: (0, 0)
    row = lambda i: (i, 0)
    u_col = COL_U // SSM_WIDTH
    return pl.pallas_call(
        body, name="ssm_fwd", grid=(s // chunk,),
        in_specs=[pl.BlockSpec((chunk, SSM_WIDTH), lambda i: (i, u_col)),
                  pl.BlockSpec((SSM_SUPER, 128, 512), full3), pl.BlockSpec((SSM_SUPER, 128, 512), full3),
                  pl.BlockSpec((SSM_SUPER, 512, 128), full3), pl.BlockSpec((SSM_SUPER, 512, 128), full3),
                  pl.BlockSpec((1, N_STATE), full2), pl.BlockSpec((1, N_STATE), full2), pl.BlockSpec((1, SSM_WIDTH), full2)],
        out_specs=[pl.BlockSpec((chunk, N_STATE), row), pl.BlockSpec((chunk, N_STATE), row),
                   pl.BlockSpec((chunk, SSM_WIDTH), row), pl.BlockSpec((chunk, SSM_WIDTH), row)],
        out_shape=[jax.ShapeDtypeStruct((s, N_STATE), F32), jax.ShapeDtypeStruct((s, N_STATE), F32),
                   jax.ShapeDtypeStruct((s, SSM_WIDTH), F32), jax.ShapeDtypeStruct((s, SSM_WIDTH), BF16)],
        scratch_shapes=[pltpu.VMEM((SUB, N_STATE), F32), pltpu.VMEM((SUB, N_STATE), F32),
                        pltpu.VMEM((N_SCAN_TABLES, SUB, N_STATE), F32)],
        compiler_params=_params(1),
    )(z, b_re, b_im, c_re, c_im, lam_re, lam_im, d_skip)


def _ssm_bwd(dys, z, h_re, h_im, b_re, b_im, c_re, c_im, lam_re, lam_im, d_skip, chunk):
    s = z.shape[0]
    n_chunks = s // chunk

    def body(dy_ref, u_ref, hre_ref, him_ref, hpr_ref, hpi_ref, bre, bim, cre, cim, lre, lim, dsk,
             du_ref, dlr_ref, dli_ref, dbr_ref, dbi_ref, dcr_ref, dci_ref, dd_ref, are, aim, car_re, car_im, tabs):
        i = pl.program_id(0)
        n = n_chunks - 1 - i

        @pl.when(i == 0)
        def _():
            car_re[...] = jnp.zeros_like(car_re)
            car_im[...] = jnp.zeros_like(car_im)
            _scan_tables(tabs, lre[...], lim[...], True, True)
            for r in (dlr_ref, dli_ref, dbr_ref, dbi_ref, dcr_ref, dci_ref, dd_ref):
                r[...] = jnp.zeros_like(r)

        dy = dy_ref[...]
        dyb = dy.astype(BF16)
        u = u_ref[...]
        for b in range(SSM_SUPER):
            ch = slice(b * 128, (b + 1) * 128)
            st = slice(b * 512, (b + 1) * 512)
            are[:, st] = _dot(dyb[:, ch], cre[b], "nt")
            aim[:, st] = -_dot(dyb[:, ch], cim[b], "nt")
        sr, si = _scan_rows(are, aim, tabs, (car_re[...], car_im[...]), chunk, True)
        car_re[...] = sr
        car_im[...] = si
        dd_ref[...] += jnp.sum(dy * u.astype(F32), axis=0, keepdims=True)
        row_id = lax.broadcasted_iota(jnp.int32, (chunk, 512), 0)
        top_scale = jnp.where(n > 0, 1.0, 0.0)
        for b in range(SSM_SUPER):
            ch = slice(b * 128, (b + 1) * 128)
            st = slice(b * 512, (b + 1) * 512)
            h_r = hre_ref[:, st]
            h_i = him_ref[:, st]
            hp_r = jnp.where(row_id == 0, hpr_ref[SUB - 1:SUB, st] * top_scale, pltpu.roll(h_r, 1, 0))
            hp_i = jnp.where(row_id == 0, hpi_ref[SUB - 1:SUB, st] * top_scale, pltpu.roll(h_i, 1, 0))
            a_r = are[:, st]
            a_i = aim[:, st]
            dlr_ref[:, st] += jnp.sum(a_r * hp_r + a_i * hp_i, axis=0, keepdims=True)
            dli_ref[:, st] += jnp.sum(a_i * hp_r - a_r * hp_i, axis=0, keepdims=True)
            a_rb = a_r.astype(BF16)
            a_ib = a_i.astype(BF16)
            dbr_ref[b] += _dot(u[:, ch], a_rb, "tn")
            dbi_ref[b] += _dot(u[:, ch], a_ib, "tn")
            dcr_ref[b] += _dot(dyb[:, ch], h_r.astype(BF16), "tn")
            dci_ref[b] += -_dot(dyb[:, ch], h_i.astype(BF16), "tn")
            du = _dot(a_rb, bre[b], "nt") + _dot(a_ib, bim[b], "nt") + dsk[:, ch] * dy[:, ch]
            du_ref[:, ch] = du.astype(du_ref.dtype)

    full3 = lambda i: (0, 0, 0)
    full2 = lambda i: (0, 0)
    rev = lambda i: (n_chunks - 1 - i, 0)
    above = lambda i: (jnp.maximum((n_chunks - 1 - i) * (chunk // SUB) - 1, 0), 0)
    u_col = COL_U // SSM_WIDTH
    b_spec = pl.BlockSpec((SSM_SUPER, 128, 512), full3)
    c_spec = pl.BlockSpec((SSM_SUPER, 512, 128), full3)
    vec = pl.BlockSpec((1, N_STATE), full2)
    return pl.pallas_call(
        body, name="ssm_bwd", grid=(n_chunks,),
        in_specs=[pl.BlockSpec((chunk, SSM_WIDTH), rev),
                  pl.BlockSpec((chunk, SSM_WIDTH), lambda i: (n_chunks - 1 - i, u_col)),
                  pl.BlockSpec((chunk, N_STATE), rev), pl.BlockSpec((chunk, N_STATE), rev),
                  pl.BlockSpec((SUB, N_STATE), above), pl.BlockSpec((SUB, N_STATE), above),
                  b_spec, b_spec, c_spec, c_spec, vec, vec, pl.BlockSpec((1, SSM_WIDTH), full2)],
        out_specs=[pl.BlockSpec((chunk, SSM_WIDTH), rev), vec, vec, b_spec, b_spec, b_spec, b_spec,
                   pl.BlockSpec((1, SSM_WIDTH), full2)],
        out_shape=[jax.ShapeDtypeStruct((s, SSM_WIDTH), BF16),
                   jax.ShapeDtypeStruct((1, N_STATE), F32), jax.ShapeDtypeStruct((1, N_STATE), F32)]
        + [jax.ShapeDtypeStruct((SSM_SUPER, 128, 512), F32)] * 4 + [jax.ShapeDtypeStruct((1, SSM_WIDTH), F32)],
        scratch_shapes=[pltpu.VMEM((chunk, N_STATE), F32), pltpu.VMEM((chunk, N_STATE), F32),
                        pltpu.VMEM((SUB, N_STATE), F32), pltpu.VMEM((SUB, N_STATE), F32),
                        pltpu.VMEM((N_SCAN_TABLES, SUB, N_STATE), F32)],
        compiler_params=_params(1),
    )(dys, z, h_re, h_im, h_re, h_im, b_re, b_im, c_re, c_im, lam_re, lam_im, d_skip)


def _local_step(x, p, pos, tgt, sm, w_in_own, w_in_buf, late_bufs, place):
    s = x.shape[0]
    tm = min(512, s)
    ts = min(2048, s)
    chunk = min(512, s)
    ni = s // tm
    nk = s // ts
    g_mix, g_ffn, g_final = sm["g_mix"], sm["g_ffn"], sm["g_final"]

    tmb = min(1024, s)
    nib = s // tmb
    chip_w = IN_WIDTH // N_CHIPS
    w_start, w_forward, w_finish = _gather_stages(1)
    gather_in = dict(ins=[w_in_buf], outs=[jax.ShapeDtypeStruct(w_in_buf.shape, w_in_buf.dtype)], aliased=True,
                     sems=[3] * 4, stages=[(0, w_start), (nib - 1, w_forward), (nib - 1, w_finish)])
    a_rows = lambda i, j, k, pv: (i, 0)
    z_own, n1, w_in_all = _mm("in_proj_own", (nib, 1, 1),
                              [(x, (tmb, D_MODEL), a_rows, w_in_own, (D_MODEL, chip_w), lambda i, j, k, pv: (0, 0))], "nn",
                              [((s, IN_WIDTH), BF16, (tmb, chip_w), lambda i, j, k, pv: (i, pv[P_CHIP])),
                               ((s, D_MODEL), BF16, (tmb, D_MODEL), a_rows)],
                              extras=[(g_mix, (1, D_MODEL), lambda i, j, k, pv: (0, 0))],
                              epilogue=lambda acc, g: ((acc,), ()), prologue=_rms_fwd_tile, comm=gather_in, place=place)
    w_in = w_in_all.reshape(N_CHIPS, D_MODEL, chip_w)
    n_late = len(late_bufs)
    g_start, g_forward, g_finish = _gather_stages(n_late)
    in_steps = (N_CHIPS - 1) * nib
    gather = dict(ins=late_bufs, outs=[jax.ShapeDtypeStruct(b.shape, b.dtype) for b in late_bufs], aliased=True,
                  sems=[3 * n_late] * 4,
                  stages=[(0, g_start), ((4 * in_steps) // 5, g_forward), (in_steps - 1, g_finish)])
    other = lambda j, pv: (pv[P_CHIP] + 1 + j) % N_CHIPS
    z, *late = _mm("in_proj", (nib, N_CHIPS - 1, 1),
                   [(n1, (tmb, D_MODEL), a_rows, w_in, (None, D_MODEL, chip_w), lambda i, j, k, pv: (other(j, pv), 0, 0))],
                   "nn", [((s, IN_WIDTH), BF16, (tmb, chip_w), lambda i, j, k, pv: (i, other(j, pv)))], j_outer=True,
                   comm=gather, place=place, fill=z_own)
    w_ap, w_ga, w_gb, w_out, w_fg, w_fu, w_fd, w_pg, w_pp = (
        g.reshape(N_CHIPS, 2 * g.shape[2], g.shape[3]) for g in late)
    w_out2 = w_out.reshape(D_MODEL, D_MODEL)
    w_pg2 = w_pg.reshape(D_MODEL, D_MODEL)

    inv = ROPE_THETA ** (-jnp.arange(ROPE_HALF, dtype=F32) * 2.0 / ROPE_DIM)
    inv_row = jnp.concatenate([inv, inv, jnp.zeros((HEAD_DIM - ROPE_DIM,), F32)]).reshape(1, HEAD_DIM)
    tabs = _rope_tables(pos.astype(F32).reshape(s, 1), inv_row, tm)

    qk0, qkv_views = _qkv_layout(z, tabs, tm)
    v0_col = (2 * N_GROUPS * GROUP_WIDTH) // GROUP_WIDTH
    group_in = [((qk0, qk0, z), (0, 1, v0_col))] + [(trio, (0, 0, 0)) for trio in qkv_views]
    fwd_out = [_attn_fwd(*arrs, dil, cols3) for (arrs, cols3), dil in zip(group_in, GROUP_DILATIONS)]
    attn, lse, merged_views = _attn_merge(fwd_out[0][0], fwd_out[0][1], fwd_out[1:], tm)

    def chip_cols(parts):
        return (jnp.concatenate(parts, axis=1),), ()

    def proj_cols(name, a, width, w):
        blk = (None, width, 256)
        pairs = [(a, (tmb, width), lambda i, j, k: (i, 0), w, blk, lambda i, j, k: (0, 0, 0))]
        pairs += [(None, None, None, w, blk, (lambda i, j, k, q=q: (q, 0, 0))) for q in range(1, N_CHIPS)]
        return _mm(name, (nib, 1, 1), pairs, "nn", [((s, D_MODEL), BF16, (tmb, D_MODEL), lambda i, j, k: (i, 0))],
                   epilogue=chip_cols, sum_pairs=False)[0]

    def proj512(name, a, w):
        return proj_cols(name, a, GROUP_WIDTH, w)

    attn_d = proj512("attn_proj", attn, w_ap)

    bt_re = jnp.transpose(sm["b_re"], (0, 2, 1))
    bt_im = jnp.transpose(sm["b_im"], (0, 2, 1))
    log_dt_col = sm["log_dt"].reshape(SSM_GROUPS, 1)
    lam_re, lam_im, bbt_re, bbt_im = _ssm_prep(sm["a_re"], sm["a_im"], log_dt_col, bt_re, bt_im)
    b_re_m = _block_diag(bbt_re, SSM_GROUP, SSM_STATE).astype(BF16)
    b_im_m = _block_diag(bbt_im, SSM_GROUP, SSM_STATE).astype(BF16)
    c_re_m = _block_diag(jnp.transpose(sm["c_re"], (0, 2, 1)), SSM_STATE, SSM_GROUP).astype(BF16)
    c_im_m = _block_diag(jnp.transpose(sm["c_im"], (0, 2, 1)), SSM_STATE, SSM_GROUP).astype(BF16)
    lam_re_row = lam_re.reshape(1, N_STATE)
    lam_im_row = lam_im.reshape(1, N_STATE)
    d_skip_row = sm["d_skip"].reshape(1, SSM_WIDTH)
    h_re, h_im, ys, yg = _ssm_fwd(z, b_re_m, b_im_m, c_re_m, c_im_m, lam_re_row, lam_im_row, d_skip_row, chunk)

    pa = proj512("glu_a", yg, w_ga)
    pb = proj512("glu_b", yg, w_gb)

    def mix_pro(ad, xr, g, ga, gs, a, b):
        ga, gs, ad, a, b = (t.astype(F32) for t in (ga, gs, ad, a, b))
        return _sig(ga) * ad + _sig(gs) * (a * _sig(b))

    def out_epi(acc, xr, g, *_):
        h1 = acc + xr
        return (h1, _rms_fwd_tile(h1, g)), ()

    m3 = lambda i, j, k: (i, 0)
    w3 = lambda i, j, k: (0, 0)
    tile_d = (tm, D_MODEL)
    h1, n2, mix = _mm("out_proj", (ni, 1, 1), [(attn_d, tile_d, m3, w_out2, (D_MODEL, D_MODEL), w3)], "nn",
                      [((s, D_MODEL), F32, tile_d, m3), ((s, D_MODEL), BF16, tile_d, m3), ((s, D_MODEL), BF16, tile_d, m3)],
                      epilogue=out_epi, prologue=mix_pro,
                      extras=[(x, tile_d, m3), (g_ffn, (1, D_MODEL), w3),
                              (z, tile_d, lambda i, j, k: (i, COL_GA // D_MODEL)), (z, tile_d, lambda i, j, k: (i, COL_GS // D_MODEL)),
                              (pa, tile_d, m3), (pb, tile_d, m3)])

    ffq = (None, tm, D_FF_Q)
    ffq_map = lambda i, j, k: (j, i, 0)

    def ffn_in_epi(parts):
        gts, ups = parts[0::2], parts[1::2]
        acts = [gt * _sig(gt) * u_ for gt, u_ in zip(gts, ups)]
        return (jnp.stack(gts, axis=0), jnp.stack(ups, axis=0), jnp.stack(acts, axis=0)), ()

    w_ffq = (None, D_MODEL, D_FF_Q)
    ff_pairs = []
    for q in range(N_CHIPS):
        blk_q = lambda i, j, k, q=q: (q, 0, 0)
        ff_pairs.append((n2, (tm, D_MODEL), m3, w_fg, w_ffq, blk_q) if q == 0 else (None, None, None, w_fg, w_ffq, blk_q))
        ff_pairs.append((None, None, None, w_fu, w_ffq, blk_q))
    ff_all = (N_CHIPS, tm, D_FF_Q)
    ff_all_map = lambda i, j, k: (0, i, 0)
    gate, up, act = _mm("ffn_gate_up", (ni, 1, 1), ff_pairs, "nn",
                        [((N_CHIPS, s, D_FF_Q), BF16, ff_all, ff_all_map)] * 3, epilogue=ffn_in_epi,
                        sum_pairs=False, resident_b=True)

    (h2,) = _mm("ffn_down", (nib, 1, 1),
                [(act, (None, tmb, D_FF_Q), (lambda i, j, k, q=q: (q, i, 0)), w_fd, (None, D_FF_Q, D_MODEL),
                  (lambda i, j, k, q=q: (q, 0, 0))) for q in range(N_CHIPS)], "nn",
                [((s, D_MODEL), F32, (tmb, D_MODEL), m3)], epilogue=lambda acc, hr: ((acc + hr,), ()),
                extras=[(h1, (tmb, D_MODEL), m3)])

    pp = proj_cols("ple_proj", p, PLE_DIM, w_pp)

    def ple_head_epi(acc, hr, ppr, t, g):
        sg = _sig(acc)
        ppf = ppr.astype(F32)
        h = hr + sg * ppf
        r = lax.rsqrt(jnp.mean(h * h, axis=-1, keepdims=True) + EPS)
        hhat = h * r
        diff = hhat * g - t
        loss = 0.5 * jnp.sum(jnp.mean(diff * diff, axis=-1, keepdims=True))
        dy = diff * (1.0 / D_MODEL)
        gy = dy * g
        dh = r * (gy - hhat * jnp.mean(gy * hhat, axis=-1, keepdims=True))
        return ((dh, dh * ppf * sg * (1.0 - sg), dh * sg),
                (jnp.full((SUB, 128), loss, F32), jnp.sum(dy * hhat, axis=0, keepdims=True)))

    tile_row = (tm, D_MODEL)
    dh3, dgl, dpp, loss_acc, dg_final = _mm(
        "ple_gate_head", (ni, 1, 1), [(h2, tile_row, m3, w_pg2, (D_MODEL, D_MODEL), w3)], "nn",
        [((s, D_MODEL), F32, tile_row, m3), ((s, D_MODEL), BF16, tile_row, m3), ((s, D_MODEL), BF16, tile_row, m3)],
        epilogue=ple_head_epi,
        extras=[(h2, tile_row, m3), (pp, tile_row, m3), (tgt, tile_row, m3), (g_final, (1, D_MODEL), w3)],
        acc_outs=[((SUB, 128), F32), ((1, D_MODEL), F32)])

    def wgrad(name, a, a_block, a_imap, b, b_block, b_imap, out_shape, out_block, out_imap, nj, acc_shape):
        return _mm(name, (1, nj, nk), [(a, a_block, a_imap, b, b_block, b_imap)], "tn",
                   [(out_shape, F32, out_block, out_imap)], acc_shape=acc_shape)[0]

    tk0 = lambda i, j, k: (k, 0)
    tkj = lambda i, j, k: (k, j)
    def wgrad_cols(name, a, width, dy_):
        def split(acc):
            return (jnp.stack([acc[:, q * 256:(q + 1) * 256] for q in range(N_CHIPS)], axis=0),), ()

        return _mm(name, (1, 1, nk), [(a, (ts, width), tk0, dy_, (ts, D_MODEL), tk0)], "tn",
                   [((N_CHIPS, width, 256), F32, (N_CHIPS, width, 256), lambda i, j, k: (0, 0, 0))], epilogue=split,
                   acc_shape=(width, D_MODEL))[0]

    d_w_pp = wgrad_cols("d_ple_proj", p, PLE_DIM, dpp)
    d_w_pg = wgrad("d_ple_gate", h2, (ts, D_MODEL), tk0, dgl, (ts, D_MODEL), tk0, (D_MODEL, D_MODEL),
                   (D_MODEL, D_MODEL), w3, 1, (D_MODEL, D_MODEL))

    (dh2,) = _mm("ple_gate_bwd", (nib, 1, 1), [(dgl, (tmb, D_MODEL), m3, w_pg2, (D_MODEL, D_MODEL), w3)], "nt",
                 [((s, D_MODEL), F32, (tmb, D_MODEL), m3)], epilogue=lambda acc, d_: ((acc + d_,), ()),
                 extras=[(dh3, (tmb, D_MODEL), m3)])

    def ffn_bwd_epi(parts, gt_all, u_all):
        dgs_, dus_ = [], []
        for q, dact in enumerate(parts):
            gt, u_ = gt_all[q].astype(F32), u_all[q].astype(F32)
            sg = _sig(gt)
            dgs_.append(dact * u_ * (sg * (1.0 + gt * (1.0 - sg))))
            dus_.append(dact * gt * sg)
        return (jnp.stack(dgs_, axis=0), jnp.stack(dus_, axis=0)), ()

    fd_pairs = [((dh2, (tm, D_MODEL), m3) if q == 0 else (None, None, None))
                + (w_fd, (None, D_FF_Q, D_MODEL), (lambda i, j, k, q=q: (q, 0, 0))) for q in range(N_CHIPS)]
    dgate, dup = _mm("ffn_down_bwd", (ni, 1, 1), fd_pairs, "nt",
                     [((N_CHIPS, s, D_FF_Q), BF16, ff_all, ff_all_map)] * 2, epilogue=ffn_bwd_epi,
                     extras=[(gate, ff_all, ff_all_map), (up, ff_all, ff_all_map)], sum_pairs=False, resident_b=True)

    ffq_t = (None, ts, D_FF_Q)
    ffq_tmap = lambda i, j, k: (j, k, 0)
    blk_j = lambda i, j, k: (j, 0, 0)
    d_w_fd = wgrad("d_ffn_down", act, ffq_t, ffq_tmap, dh2, (ts, D_MODEL), tk0, (N_CHIPS, D_FF_Q, D_MODEL),
                   (None, D_FF_Q, D_MODEL), blk_j, N_CHIPS, (D_FF_Q, D_MODEL))
    d_w_fg = wgrad("d_ffn_gate", n2, (ts, D_MODEL), tk0, dgate, ffq_t, ffq_tmap, (N_CHIPS, D_MODEL, D_FF_Q),
                   (None, D_MODEL, D_FF_Q), blk_j, N_CHIPS, (D_MODEL, D_FF_Q))
    d_w_fu = wgrad("d_ffn_up", n2, (ts, D_MODEL), tk0, dup, ffq_t, ffq_tmap, (N_CHIPS, D_MODEL, D_FF_Q),
                   (None, D_MODEL, D_FF_Q), blk_j, N_CHIPS, (D_MODEL, D_FF_Q))

    def norm_bwd_epi(acc, h, d_res, g):
        dh, dg = _rms_bwd_tile(acc, h, g)
        return (d_res + dh,), (dg,)

    fi_pairs = []
    for q in range(N_CHIPS):
        a_q = lambda i, j, k, q=q: (q, i, 0)
        b_q = lambda i, j, k, q=q: (q, 0, 0)
        fi_pairs.append((dgate, ffq, a_q, w_fg, (None, D_MODEL, D_FF_Q), b_q))
        fi_pairs.append((dup, ffq, a_q, w_fu, (None, D_MODEL, D_FF_Q), b_q))
    dh1, dg_ffn = _mm("ffn_in_bwd", (ni, 1, 1), fi_pairs, "nt",
                      [((s, D_MODEL), F32, (tm, D_MODEL), m3)], epilogue=norm_bwd_epi,
                      extras=[(h1, (tm, D_MODEL), m3), (dh2, (tm, D_MODEL), m3), (g_ffn, (1, D_MODEL), w3)],
                      acc_outs=[((1, D_MODEL), F32)], resident_b=True)

    d_w_out = wgrad("d_out_proj", mix, (ts, D_MODEL), tk0, dh1, (ts, D_MODEL), tk0, (D_MODEL, D_MODEL),
                    (D_MODEL, D_MODEL), w3, 1, (D_MODEL, D_MODEL))

    def mix_bwd_epi(dm, ga, gs, ad, a, b):
        ga, gs, ad, a, b = (t.astype(F32) for t in (ga, gs, ad, a, b))
        s_a, s_s, s_b = _sig(ga), _sig(gs), _sig(b)
        d_ssm = dm * s_s
        return (dm * ad * s_a * (1.0 - s_a), dm * (a * s_b) * s_s * (1.0 - s_s), dm * s_a, d_ssm * s_b,
                d_ssm * a * s_b * (1.0 - s_b)), ()

    tile_m = (tm, D_MODEL)
    dga, dgs, dattn_d, dpa, dpb = _mm(
        "out_proj_bwd", (ni, 1, 1), [(dh1, tile_m, m3, w_out2, (D_MODEL, D_MODEL), w3)], "nt",
        [((s, D_MODEL), BF16, tile_m, m3)] * 5, epilogue=mix_bwd_epi,
        extras=[(z, tile_m, lambda i, j, k: (i, COL_GA // D_MODEL)), (z, tile_m, lambda i, j, k: (i, COL_GS // D_MODEL)),
                (attn_d, tile_m, m3), (pa, tile_m, m3), (pb, tile_m, m3)])

    d_w_ap = wgrad_cols("d_attn_proj", attn, GROUP_WIDTH, dattn_d)
    d_w_ga = wgrad_cols("d_glu_a", yg, GROUP_WIDTH, dpa)
    d_w_gb = wgrad_cols("d_glu_b", yg, GROUP_WIDTH, dpb)

    ik = lambda i, j, k: (i, k)

    def cols_bwd(dy_, w):
        return [(dy_, (tmb, 256), (lambda i, j, k, q=q: (i, q)), w, (None, GROUP_WIDTH, 256),
                 (lambda i, j, k, q=q: (q, 0, 0))) for q in range(N_CHIPS)]

    (dattn,) = _mm("attn_proj_bwd", (nib, 1, 1), cols_bwd(dattn_d, w_ap), "nt",
                   [((s, GROUP_WIDTH), BF16, (tmb, GROUP_WIDTH), m3)])

    (dys,) = _mm("glu_bwd", (nib, 1, 1), cols_bwd(dpa, w_ga) + cols_bwd(dpb, w_gb), "nt",
                 [((s, GROUP_WIDTH), F32, (tmb, GROUP_WIDTH), m3)],
                 epilogue=lambda acc, y_: ((acc * _gelu_grad(y_),), ()),
                 extras=[(ys, (tmb, GROUP_WIDTH), m3)])

    du, d_lr, d_li, d_bre, d_bim, d_cre, d_cim, d_dskip = _ssm_bwd(
        dys, z, h_re, h_im, b_re_m, b_im_m, c_re_m, c_im_m, lam_re_row, lam_im_row, d_skip_row, chunk)

    dattn_views = _to_views(dattn, tm)
    bwd_in = [(dattn, attn, lse)] + [(dv_, ov_, lv_) for dv_, (ov_, lv_) in zip(dattn_views, merged_views)]
    qkv_grads = [_attn_bwd(*arrs, *dol, dil, cols3)
                 for (arrs, cols3), dol, dil in zip(group_in, bwd_in, GROUP_DILATIONS)]
    early = [d_w_ap, d_w_ga, d_w_gb, d_w_out.reshape(N_CHIPS, D_MODEL // N_CHIPS, D_MODEL), d_w_fg, d_w_fu, d_w_fd,
             d_w_pg.reshape(N_CHIPS, D_MODEL // N_CHIPS, D_MODEL), d_w_pp]
    early5 = [g.reshape(N_CHIPS, 2, g.shape[1] // 2, g.shape[2]) for g in early]
    n_e = len(early5)
    p_start, p_finish = _pair_exchange_stages(n_e)
    dz, early_theirs = _dz_layout(
        qkv_grads, du, dga, dgs, tabs, tm,
        comm=dict(ins=early5, outs=_pair_exchange_shapes(early5), sems=[N_CHIPS * n_e] * 2, start=p_start, finish=p_finish))
    early_parts = [_pair_sum(g, t, place) for g, t in zip(early5, early_theirs)]

    chip_in = IN_WIDTH // N_CHIPS
    ip_pairs = [(dz, (tm, chip_in), (lambda i, j, k, q=q: (i, q)), w_in, (None, D_MODEL, chip_in),
                 (lambda i, j, k, q=q: (q, 0, 0))) for q in range(N_CHIPS)]
    grad_x, dg_mix = _mm("in_proj_bwd", (ni, 1, 1), ip_pairs, "nt",
                         [((s, D_MODEL), F32, (tm, D_MODEL), m3)], epilogue=norm_bwd_epi,
                         extras=[(x, (tm, D_MODEL), m3), (dh1, (tm, D_MODEL), m3), (g_mix, (1, D_MODEL), w3)],
                         acc_outs=[((1, D_MODEL), F32)], resident_b=True)

    d_bbt_re = _block_diag_t(d_bre, SSM_GROUP, SSM_STATE)
    d_bbt_im = _block_diag_t(d_bim, SSM_GROUP, SSM_STATE)
    d_a_re, d_a_im, d_log_dt, d_bt_re, d_bt_im = _ssm_param_bwd(
        sm["a_re"], sm["a_im"], log_dt_col, bt_re, bt_im,
        d_lr.reshape(SSM_GROUPS, SSM_STATE), d_li.reshape(SSM_GROUPS, SSM_STATE), d_bbt_re, d_bbt_im)
    small = {
        "g_mix": dg_mix, "a_re": d_a_re, "a_im": d_a_im, "log_dt": d_log_dt,
        "b_re": jnp.transpose(d_bt_re, (0, 2, 1)), "b_im": jnp.transpose(d_bt_im, (0, 2, 1)),
        "c_re": _block_diag_t(d_cre, SSM_GROUP, SSM_STATE), "c_im": _block_diag_t(d_cim, SSM_GROUP, SSM_STATE),
        "d_skip": d_dskip, "g_ffn": dg_ffn, "g_final": dg_final,
    }
    vec = _pack([small[n] for n in SMALL] + [loss_acc[0, 0].reshape(1)])

    x_start, x_finish = _chip_exchange_stages(n_e)
    v_start, v_finish = _all_exchange_stages()

    def both(f_chips, f_vec):
        def stage(ins, outs, sems):
            f_chips(ins[:n_e], outs[:n_e], sems[:2])
            f_vec(ins[n_e:], outs[n_e:], sems[2:])
        return stage

    half_in = chip_in // 2
    win_steps = 8 * nk
    exchange = dict(ins=early_parts + [vec],
                    outs=[jax.ShapeDtypeStruct(t.shape, t.dtype) for t in early_parts]
                    + [jax.ShapeDtypeStruct((8,) + vec.shape, vec.dtype)],
                    aliased=False, sems=[3 * n_e, 3 * n_e, 7, 7],
                    stages=[(0, both(x_start, v_start)), (win_steps - 1, both(x_finish, v_finish))])
    d_w_in, *got = _mm("d_in_proj", (1, 8, nk), [(n1, (ts, D_MODEL), tk0, dz, (ts, half_in), tkj)], "tn",
                       [((N_CHIPS, D_MODEL, chip_in), F32, (None, D_MODEL, half_in), lambda i, j, k: (j // 2, 0, j % 2))],
                       acc_shape=(D_MODEL, half_in), comm=exchange)
    return grad_x, d_w_in, early_parts, got[:n_e], vec, got[n_e]


BIG = ("w_in", "w_attn_proj", "w_glu_a", "w_glu_b", "w_out", "w_ffn_gate", "w_ffn_up", "w_ffn_down", "w_ple_gate",
       "w_ple_proj")
SMALL = ("g_mix", "a_re", "a_im", "log_dt", "b_re", "b_im", "c_re", "c_im", "d_skip", "g_ffn", "g_final")
ANY = pl.BlockSpec(memory_space=pl.ANY)


def _place():
    x, y, c = lax.axis_index("x"), lax.axis_index("y"), lax.axis_index("c")
    chips = [(1 - x, y), (x, 1 - y), (1 - x, 1 - y)]
    return x, y, c, chips


def _remote(src, dst, send_sem, recv_sem, to):
    return pltpu.make_async_remote_copy(src_ref=src, dst_ref=dst, send_sem=send_sem, recv_sem=recv_sem, device_id=to,
                                        device_id_type=MESH)


def _comm_call(name, body, ins, out_shapes, n_sems, aliases=None):
    n_w = len(ins)
    return pl.pallas_call(
        body, name=name, in_specs=[ANY] * n_w, out_specs=[ANY] * len(out_shapes), out_shape=out_shapes,
        scratch_shapes=[pltpu.SemaphoreType.DMA((n,)) for n in n_sems], input_output_aliases=aliases or {},
    )(*ins)


def _gather_stages(n_w):
    def each():
        x, y, c, chips = _place()
        for w in range(n_w):
            for j, (cx, cy) in enumerate(chips):
                yield w, 3 * w + j, 2 * x + y, 2 * cx + cy, (cx, cy, c), (x, y, 1 - c), c

    def start(ins, outs, sems):
        for w, k, me, _, peer, _, c in each():
            mine = outs[w].at[me, c]
            _remote(mine, mine, sems[0].at[k], sems[1].at[k], peer).start()

    def forward(ins, outs, sems):
        for w, k, _, src_chip, peer, sib, c in each():
            landed = outs[w].at[src_chip, c]
            _remote(landed, landed, sems[0].at[k], sems[1].at[k], peer).wait_recv()
            _remote(landed, landed, sems[2].at[k], sems[3].at[k], sib).start()

    def finish(ins, outs, sems):
        for w, k, me, src_chip, peer, sib, c in each():
            other = outs[w].at[src_chip, 1 - c]
            _remote(other, other, sems[2].at[k], sems[3].at[k], sib).wait_recv()
        for w, k, me, src_chip, peer, sib, c in each():
            mine = outs[w].at[me, c]
            _remote(mine, mine, sems[0].at[k], sems[1].at[k], peer).wait_send()
            landed = outs[w].at[src_chip, c]
            _remote(landed, landed, sems[2].at[k], sems[3].at[k], sib).wait_send()

    return start, forward, finish


def _pair_exchange(grads):
    n_w = len(grads)
    start, finish = _pair_exchange_stages(n_w)

    def body(*refs):
        ins, outs, sems = refs[:n_w], refs[n_w:2 * n_w], refs[2 * n_w:]
        start(ins, outs, sems)
        finish(ins, outs, sems)

    return _comm_call("grad_pair_exchange", body, grads, _pair_exchange_shapes(grads), [N_CHIPS * n_w] * 2)


def _pair_exchange_shapes(grads):
    return [jax.ShapeDtypeStruct((N_CHIPS,) + g.shape[2:], g.dtype) for g in grads]


def _pair_exchange_stages(n_w):
    def each():
        x, y, c, _ = _place()
        for w in range(n_w):
            for q in range(N_CHIPS):
                yield w, q, N_CHIPS * w + q, c, (x, y, 1 - c)

    def start(ins, outs, sems):
        for w, q, k, c, sib in each():
            _remote(ins[w].at[q, 1 - c], outs[w].at[q], sems[0].at[k], sems[1].at[k], sib).start()

    def finish(ins, outs, sems):
        for w, q, k, c, sib in each():
            _remote(ins[w].at[q, 1 - c], outs[w].at[q], sems[0].at[k], sems[1].at[k], sib).wait()

    return start, finish


def _chip_exchange(parts):
    n_w = len(parts)

    start, finish = _chip_exchange_stages(n_w)

    def body(*refs):
        ins, outs, sems = refs[:n_w], refs[n_w:2 * n_w], refs[2 * n_w:]
        start(ins, outs, sems)
        finish(ins, outs, sems)

    out_shapes = [jax.ShapeDtypeStruct(t.shape, t.dtype) for t in parts]
    return _comm_call("grad_chip_exchange", body, parts, out_shapes, [3 * n_w, 3 * n_w])


def _chip_exchange_stages(n_w):
    def each():
        x, y, c, chips = _place()
        for w in range(n_w):
            for j, (cx, cy) in enumerate(chips):
                yield w, 3 * w + j, 2 * x + y, 2 * cx + cy, (cx, cy, c)

    def start(ins, outs, sems):
        for w, k, me, peer_chip, peer in each():
            _remote(ins[w].at[peer_chip], outs[w].at[me], sems[0].at[k], sems[1].at[k], peer).start()

    def finish(ins, outs, sems):
        for w, k, me, peer_chip, peer in each():
            got = outs[w].at[peer_chip]
            _remote(got, got, sems[0].at[k], sems[1].at[k], peer).wait_recv()
        for w, k, me, peer_chip, peer in each():
            _remote(ins[w].at[peer_chip], outs[w].at[me], sems[0].at[k], sems[1].at[k], peer).wait_send()

    return start, finish


def _pair_gather(halves):
    n_w = len(halves)

    def body(*refs):
        ins, outs = refs[:n_w], refs[n_w:2 * n_w]
        send, recv = refs[2 * n_w:]
        x, y, c, _ = _place()
        sib = (x, y, 1 - c)
        cps = []
        for w in range(n_w):
            cp = _remote(ins[w], outs[w], send.at[w], recv.at[w], sib)
            cp.start()
            cps.append(cp)
        for cp in cps:
            cp.wait()

    out_shapes = [jax.ShapeDtypeStruct(h.shape, h.dtype) for h in halves]
    return _comm_call("grad_pair_gather", body, halves, out_shapes, [n_w] * 2)


def _all_exchange_stages():
    def each():
        x, y, c, _ = _place()
        for k in range(1, 8):
            px, py, pc = x ^ ((k >> 2) & 1), y ^ ((k >> 1) & 1), c ^ (k & 1)
            yield k - 1, 4 * x + 2 * y + c, 4 * px + 2 * py + pc, (px, py, pc)

    def start(ins, outs, sems):
        for k, me, _, peer in each():
            _remote(ins[0], outs[0].at[me], sems[0].at[k], sems[1].at[k], peer).start()

    def finish(ins, outs, sems):
        for k, me, src, peer in each():
            got = outs[0].at[src]
            _remote(got, got, sems[0].at[k], sems[1].at[k], peer).wait_recv()
        for k, me, src, peer in each():
            _remote(ins[0], outs[0].at[me], sems[0].at[k], sems[1].at[k], peer).wait_send()

    return start, finish


def _row_tile(r):
    for t in (256, 128, 176, 64, 32, 16, 8):
        if r % t == 0:
            return t
    return r


P_C, P_CHIP, P_DEV = 2, 3, 4


def _cast_shard(w2):
    r, c = w2.shape
    t = _row_tile(r)
    blk, imap = _rows(t, c)
    return _ew("cast_own", (r // t,), [(w2, blk, imap)], [((r, c), BF16, blk, imap)], lambda pids, a: ((a,), ()))[0]


def _cast_into_slot(w2, place):
    r, c = w2.shape
    t = _row_tile(r)
    return _ew("cast_shard", (r // t,), [(w2, (t, c), lambda i, pv: (i, 0))],
               [((N_CHIPS, r, c), BF16, (None, t, c), lambda i, pv: (pv[P_CHIP], i, 0))],
               lambda pids, a: ((a,), ()), place=place)[0]


def _pair_sum(mine, theirs, place):
    _, r, c = theirs.shape
    t = _row_tile(r)
    own = ((None, None, t, c), lambda q, i, pv: (q, pv[P_C], i, 0))
    blk = ((None, t, c), lambda q, i, pv: (q, i, 0))
    return _ew("grad_pair_sum", (N_CHIPS, r // t), [(mine, *own), (theirs, *blk)], [((N_CHIPS, r, c), BF16, *blk)],
               lambda pids, a, b: ((a + b,), ()), place=place)[0]


def _chip_sum(own, got, place):
    _, r, c = own.shape
    t = _row_tile(r)
    ins = []
    for q in range(N_CHIPS):
        ins.append((own, (None, t, c), (lambda i, pv, q=q: (q, i, 0))))
        ins.append((got, (None, t, c), (lambda i, pv, q=q: (jnp.where(pv[P_CHIP] == q, (q + 1) % N_CHIPS, q), i, 0))))

    def fn(pids, *tiles):
        me = pids[0][P_CHIP]
        tot = None
        for q in range(N_CHIPS):
            term = jnp.where(me == q, tiles[2 * q], tiles[2 * q + 1]).astype(F32)
            tot = term if tot is None else tot + term
        return (tot,), ()

    return _ew("grad_chip_sum", (r // t,), ins, [((r, c), F32, (t, c), lambda i, pv: (i, 0))], fn, place=place)[0]


def _adamw_tile(w, g, m, v):
    m = ADAM_B1 * m + (1.0 - ADAM_B1) * g
    v = ADAM_B2 * v + (1.0 - ADAM_B2) * (g * g)
    m_hat = m / (1.0 - ADAM_B1 ** ADAM_STEP)
    v_hat = v / (1.0 - ADAM_B2 ** ADAM_STEP)
    delta = -ADAM_LR * (m_hat / (jnp.sqrt(v_hat) + ADAM_EPS) + ADAM_WD * w)
    return delta, m, v


def _adamw(name, g2, w2, m2, v2):
    r, c = w2.shape
    t = _row_tile(r)
    blk, imap = _rows(t, c)

    def fn(pids, g, w, m, v):
        delta, nm, nv = _adamw_tile(w, g, m, v)
        return (g, delta, nm, nv), ()

    return _ew(name, (r // t,), [(a, blk, imap) for a in (g2, w2, m2, v2)], [((r, c), F32, blk, imap)] * 4, fn)


def _adamw_halves(name, mine, theirs, w2, m2, v2, place):
    r, c = w2.shape
    t = _row_tile(r // 2)
    n_t = (r // 2) // t
    half = ((t, c), lambda h, i, pv: (i, 0))
    whole = ((t, c), lambda h, i, pv: (h * n_t + i, 0))

    def fn(pids, ga, gb, w, m, v):
        g = jnp.where(pids[1] == pids[0][P_C], ga, gb)
        delta, nm, nv = _adamw_tile(w, g, m, v)
        return (g, delta, nm, nv), ()

    return _ew(name, (2, n_t), [(mine, *half), (theirs, *half), (w2, *whole), (m2, *whole), (v2, *whole)],
               [((r, c), F32, *whole)] * 4, fn, place=place)


def _device_sum(own, got, place):
    r, c = own.shape
    t = _row_tile(r)
    ins = [(own, (t, c), lambda i, pv: (i, 0))]
    for q in range(8):
        ins.append((got, (None, t, c), (lambda i, pv, q=q: (jnp.where(pv[P_DEV] == q, (q + 1) % 8, q), i, 0))))

    def fn(pids, mine, *parts):
        me = pids[0][P_DEV]
        tot = None
        for q in range(8):
            term = jnp.where(me == q, mine, parts[q])
            tot = term if tot is None else tot + term
        return (tot,), ()

    return _ew("small_device_sum", (r // t,), ins, [((r, c), F32, (t, c), lambda i, pv: (i, 0))], fn, place=place)[0]


def _pack(parts):
    flat = jnp.concatenate([a.reshape(-1) for a in parts])
    pad = (-flat.shape[0]) % (SUB * 128)
    return jnp.pad(flat, (0, pad)).reshape(-1, 128)


def _unpack(mat, shapes):
    flat = mat.reshape(-1)
    out, off = [], 0
    for shp in shapes:
        n = math.prod(shp)
        out.append(flat[off:off + n].reshape(shp))
        off += n
    return out


def kernel(x, p, positions, g_mix, w_in, a_re, a_im, log_dt, b_re, b_im, c_re, c_im, d_skip, w_attn_proj, w_glu_a, w_glu_b, w_out, g_ffn, w_ffn_gate, w_ffn_up, w_ffn_down, w_ple_gate, w_ple_proj, g_final, loss_target, m_g_mix, m_w_in, m_a_re, m_a_im, m_log_dt, m_b_re, m_b_im, m_c_re, m_c_im, m_d_skip, m_w_attn_proj, m_w_glu_a, m_w_glu_b, m_w_out, m_g_ffn, m_w_ffn_gate, m_w_ffn_up, m_w_ffn_down, m_w_ple_gate, m_w_ple_proj, m_g_final, v_g_mix, v_w_in, v_a_re, v_a_im, v_log_dt, v_b_re, v_b_im, v_c_re, v_c_im, v_d_skip, v_w_attn_proj, v_w_glu_a, v_w_glu_b, v_w_out, v_g_ffn, v_w_ffn_gate, v_w_ffn_up, v_w_ffn_down, v_w_ple_gate, v_w_ple_proj, v_g_final):
    given = dict(locals())
    big_w = {n: given[n] for n in BIG}
    w_mats = {n: big_w[n].reshape(big_w[n].shape[1:]) for n in BIG}

    ax, ay, ac = lax.axis_index("x"), lax.axis_index("y"), lax.axis_index("c")
    place = jnp.stack([ax, ay, ac, 2 * ax + ay, 4 * ax + 2 * ay + ac]).astype(jnp.int32)

    bufs = []
    for n in BIG:
        r, c = w_mats[n].shape
        bufs.append(_cast_into_slot(w_mats[n], place).reshape(N_CHIPS, 2, r // 2, c))
    w_in_own = _cast_shard(w_mats["w_in"])

    sm = {
        "g_mix": g_mix.reshape(1, D_MODEL), "g_ffn": g_ffn.reshape(1, D_MODEL), "g_final": g_final.reshape(1, D_MODEL),
        "a_re": a_re[0], "a_im": a_im[0], "log_dt": log_dt[0], "b_re": b_re[0], "b_im": b_im[0], "c_re": c_re[0],
        "c_im": c_im[0], "d_skip": d_skip[0],
    }
    s = x.shape[1]
    grad_x, d_w_in, early_parts, early_got, vec, vec_got = _local_step(
        x[0], p[0, 0], positions[0], loss_target[0], sm, w_in_own, bufs[0], bufs[1:], place)

    r_in, c_in = w_mats["w_in"].shape
    g5_in = [d_w_in.reshape(N_CHIPS, 2, r_in // 2, c_in)]
    in_parts = [_pair_sum(g, t, place) for g, t in zip(g5_in, _pair_exchange(g5_in))]
    chip_parts = in_parts + list(early_parts)
    chip_got = list(_chip_exchange(in_parts)) + list(early_got)
    halves = [_chip_sum(own, got, place) for own, got in zip(chip_parts, chip_got)]
    other_halves = _pair_gather(halves)

    results = {}
    for n, mine, other in zip(BIG, halves, other_halves):
        r, c = w_mats[n].shape
        shp = big_w[n].shape
        outs = _adamw_halves("adamw_" + n, mine, other, w_mats[n], given["m_" + n].reshape(r, c),
                             given["v_" + n].reshape(r, c), place)
        results[n] = [o.reshape(shp) for o in outs]

    small_shapes = [given[n].shape for n in SMALL]
    tot = _device_sum(vec, vec_got, place)
    n_small = sum(math.prod(shp) for shp in small_shapes)
    loss = tot.reshape(-1)[n_small]
    w_s = _pack([given[n] for n in SMALL])
    m_s = _pack([given["m_" + n] for n in SMALL])
    v_s = _pack([given["v_" + n] for n in SMALL])
    rows_s = w_s.shape[0]
    g_s = tot.reshape(-1)[: rows_s * 128].reshape(rows_s, 128)
    outs_s = _adamw("adamw_small", g_s, w_s, m_s, v_s)
    for kind, mat in enumerate(outs_s):
        for n, arr in zip(SMALL, _unpack(mat, small_shapes)):
            results.setdefault(n, [None] * 4)[kind] = arr

    order = ("g_mix", "w_in", "a_re", "a_im", "log_dt", "b_re", "b_im", "c_re", "c_im", "d_skip", "w_attn_proj", "w_glu_a",
             "w_glu_b", "w_out", "g_ffn", "w_ffn_gate", "w_ffn_up", "w_ffn_down", "w_ple_gate", "w_ple_proj", "g_final")
    out = [loss, grad_x.reshape(1, s, D_MODEL)]
    for kind in range(4):
        out += [results[n][kind] for n in order]
    return tuple(out)
```

```python
import math

import jax
import jax.numpy as jnp
from jax import lax
from jax.experimental import pallas as pl
from jax.experimental.pallas import tpu as pltpu

F32 = jnp.float32
BF16 = jnp.bfloat16

D_MODEL = 1024
HEAD_DIM = 128
HEADS_PER_GROUP = 4
GROUP_WIDTH = HEADS_PER_GROUP * HEAD_DIM
GROUP_DILATIONS = (1, 4, 16)
N_GROUPS = len(GROUP_DILATIONS)
LSE_LANES = 32
LSE_WIDTH = HEADS_PER_GROUP * LSE_LANES
ATTN_BLOCK = 128
ROPE_DIM = 32
ROPE_HALF = 16
ROPE_THETA = 500000.0
SSM_WIDTH = 512
SSM_GROUPS = 32
SSM_GROUP = 16
SSM_STATE = 64
N_STATE = SSM_GROUPS * SSM_STATE
SSM_SUPER = 4
IN_WIDTH = 7168
COL_U = 4608
COL_GA = 5120
COL_GS = 6144
D_FF = 2816
N_CHIPS = 4
D_FF_Q = D_FF // N_CHIPS
PLE_DIM = 256
EPS = 1e-6
ADAM_LR = 0.001
ADAM_B1 = 0.9
ADAM_B2 = 0.999
ADAM_EPS = 1e-08
ADAM_WD = 0.01
ADAM_STEP = 10
NEG_BIG = -1e30
VMEM_LIMIT_BYTES = 56 * 1024 * 1024
MESH = pl.DeviceIdType.MESH

_DIMS = {
    "nn": (((1,), (0,)), ((), ())),
    "nt": (((1,), (1,)), ((), ())),
    "tn": (((0,), (0,)), ((), ())),
}


def _params(n_grid):
    return pltpu.CompilerParams(dimension_semantics=("arbitrary",) * n_grid, vmem_limit_bytes=VMEM_LIMIT_BYTES)


def _sig(v):
    return 0.5 * jnp.tanh(0.5 * v) + 0.5


def _dot(a, b, mode):
    return lax.dot_general(a, b, _DIMS[mode], preferred_element_type=F32)


def _mm(name, grid, pairs, mode, outs, epilogue=None, extras=(), acc_outs=(), acc_shape=None, j_outer=False,
        sum_pairs=True, resident_b=False, comm=None, place=None, fill=None, prologue=None):
    gi, gj, gk = grid
    n_p, n_e, n_o, n_a = len(pairs), len(extras), len(outs), len(acc_outs)
    assert not n_a or gj == 1
    assert sum_pairs or gk == 1
    run_grid = (gj, gi, gk) if j_outer else grid
    c_ins = list(comm["ins"]) if comm else []
    c_outs = list(comm["outs"]) if comm else []
    c_sems = list(comm["sems"]) if comm else []
    n_ci, n_co, n_cs = len(c_ins), len(c_outs), len(c_sems)
    n_s = 0 if place is None else 1
    n_fill = 0 if fill is None else 1

    def order(imap):
        if place is None:
            return (lambda j, i, k: imap(i, j, k)) if j_outer else imap
        return (lambda j, i, k, pv: imap(i, j, k, pv)) if j_outer else imap

    shared_a = [pr[0] is None for pr in pairs]
    n_in = 2 * n_p - sum(shared_a)

    def body(*refs):
        refs = refs[n_s:]
        pair_refs = list(refs[:n_in])
        extra_refs = refs[n_in: n_in + n_e]
        comm_in = refs[n_in + n_e: n_in + n_e + n_ci]
        at = n_in + n_e + n_ci + n_fill
        out_refs = refs[at: at + n_o]
        sum_refs = refs[at + n_o: at + n_o + n_a]
        comm_out = refs[at + n_o + n_a: at + n_o + n_a + n_co]
        scratch_refs = refs[at + n_o + n_a + n_co:]
        i = pl.program_id(1 if j_outer else 0)
        k = pl.program_id(2)
        if comm:
            step = (pl.program_id(0) * run_grid[1] + pl.program_id(1)) * run_grid[2] + pl.program_id(2)
            sems = scratch_refs[len(scratch_refs) - n_cs:]
            for at_step, stage in comm["stages"]:
                @pl.when(step == at_step)
                def _(stage=stage):
                    stage(comm_in, comm_out, sems)
        part = None if sum_pairs else []
        a = None
        for t in range(n_p):
            if not shared_a[t]:
                a = pair_refs.pop(0)[...]
                if prologue is not None and t == 0:
                    a = prologue(a, *[e[...] for e in extra_refs]).astype(BF16)
                    out_refs[n_o - 1][...] = a
                a = a.astype(BF16)
            b = pair_refs.pop(0)[...].astype(BF16)
            d = _dot(a, b, mode)
            if sum_pairs:
                part = d if part is None else part + d
            else:
                part.append(d)

        def finish(acc):
            tiles, sums = epilogue(acc, *[e[...] for e in extra_refs]) if epilogue is not None else ((acc,), ())
            for o_ref, tile in zip(out_refs, tiles):
                o_ref[...] = tile.astype(o_ref.dtype)
            if n_a:
                @pl.when(i == 0)
                def _():
                    for s_ref in sum_refs:
                        s_ref[...] = jnp.zeros_like(s_ref)

                for s_ref, s in zip(sum_refs, sums):
                    s_ref[...] += s

        if gk == 1:
            finish(part)
        else:
            acc_ref = scratch_refs[0]

            @pl.when(k == 0)
            def _():
                acc_ref[...] = part

            @pl.when(k > 0)
            def _():
                acc_ref[...] += part

            @pl.when(k == gk - 1)
            def _():
                finish(acc_ref[...])

    in_specs, args = [], []
    for a, a_block, a_imap, b, b_block, b_imap in pairs:
        if a is not None:
            in_specs.append(pl.BlockSpec(a_block, order(a_imap)))
            args.append(a)
        if resident_b:
            in_specs.append(pl.BlockSpec(b_block, order(b_imap), pipeline_mode=pl.Buffered(1)))
        else:
            in_specs.append(pl.BlockSpec(b_block, order(b_imap)))
        args.append(b)
    for e, e_block, e_imap in extras:
        in_specs.append(pl.BlockSpec(e_block, order(e_imap)))
        args.append(e)
    first_comm_in = len(args)
    for c_in in c_ins:
        in_specs.append(pl.BlockSpec(memory_space=pl.ANY))
        args.append(c_in)
    if n_fill:
        in_specs.append(pl.BlockSpec(memory_space=pl.ANY))
        args.append(fill)
    out_shape = [jax.ShapeDtypeStruct(shape, dtype) for shape, dtype, _, _ in outs]
    out_specs = [pl.BlockSpec(block, order(imap)) for _, _, block, imap in outs]
    for shape, dtype in acc_outs:
        out_shape.append(jax.ShapeDtypeStruct(shape, dtype))
        out_specs.append(pl.BlockSpec(shape, lambda *_: (0, 0)))
    first_comm_out = len(out_shape)
    for c_out in c_outs:
        out_shape.append(c_out)
        out_specs.append(pl.BlockSpec(memory_space=pl.ANY))
    aliases = {n_s + first_comm_in + n: first_comm_out + n for n in range(n_ci)} if comm and comm["aliased"] else {}
    if n_fill:
        aliases[n_s + len(args) - 1] = 0
    scratch = [pltpu.VMEM(acc_shape, F32)] if gk > 1 else []
    scratch += [pltpu.SemaphoreType.DMA((n,)) for n in c_sems]
    if n_s:
        spec = pltpu.PrefetchScalarGridSpec(num_scalar_prefetch=1, grid=run_grid, in_specs=in_specs, out_specs=out_specs,
                                            scratch_shapes=scratch)
        return pl.pallas_call(body, name=name, grid_spec=spec, out_shape=out_shape, compiler_params=_params(3),
                              input_output_aliases=aliases)(place, *args)
    return pl.pallas_call(
        body, name=name, grid=run_grid, in_specs=in_specs, out_specs=out_specs,
        out_shape=out_shape, scratch_shapes=scratch, compiler_params=_params(3), input_output_aliases=aliases,
    )(*args)


def _ew(name, grid, ins, outs, fn, acc_outs=(), place=None):
    n_i, n_o, n_a = len(ins), len(outs), len(acc_outs)
    ng = len(grid)
    n_s = 0 if place is None else 1

    def body(*refs):
        in_refs = refs[n_s: n_s + n_i]
        out_refs = refs[n_s + n_i: n_s + n_i + n_o]
        sum_refs = refs[n_s + n_i + n_o:]
        pids = tuple(pl.program_id(a) for a in range(ng))
        if n_s:
            pids = (refs[0],) + pids
        tiles, sums = fn(pids, *[r[...] for r in in_refs])
        for o_ref, tile in zip(out_refs, tiles):
            o_ref[...] = tile.astype(o_ref.dtype)
        if n_a:
            first = pids[0] == 0
            for p_ in pids[1:]:
                first = jnp.logical_and(first, p_ == 0)

            @pl.when(first)
            def _():
                for s_ref in sum_refs:
                    s_ref[...] = jnp.zeros_like(s_ref)

            for s_ref, s in zip(sum_refs, sums):
                s_ref[...] += s

    in_specs = [pl.BlockSpec(block, imap) for _, block, imap in ins]
    out_shape = [jax.ShapeDtypeStruct(shape, dtype) for shape, dtype, _, _ in outs]
    out_specs = [pl.BlockSpec(block, imap) for _, _, block, imap in outs]
    for shape, dtype in acc_outs:
        out_shape.append(jax.ShapeDtypeStruct(shape, dtype))
        out_specs.append(pl.BlockSpec(shape, lambda *_, nd=len(shape): (0,) * nd))
    arrays = [a for a, _, _ in ins]
    if n_s:
        assert not n_a
        spec = pltpu.PrefetchScalarGridSpec(num_scalar_prefetch=1, grid=grid, in_specs=in_specs, out_specs=out_specs)
        return pl.pallas_call(body, name=name, grid_spec=spec, out_shape=out_shape, compiler_params=_params(ng))(
            place, *arrays)
    return pl.pallas_call(
        body, name=name, grid=grid, in_specs=in_specs, out_specs=out_specs, out_shape=out_shape,
        compiler_params=_params(ng),
    )(*arrays)


def _rows(tm, width):
    return (tm, width), (lambda i: (i, 0))


def _rms_fwd_tile(h, g):
    r = lax.rsqrt(jnp.mean(h * h, axis=-1, keepdims=True) + EPS)
    return h * r * g


def _rms_bwd_tile(dn, h, g):
    r = lax.rsqrt(jnp.mean(h * h, axis=-1, keepdims=True) + EPS)
    hhat = h * r
    gy = dn * g
    dh = r * (gy - hhat * jnp.mean(gy * hhat, axis=-1, keepdims=True))
    dg = jnp.sum(dn * hhat, axis=0, keepdims=True)
    return dh, dg


def _rope_tables(pos_col, inv_row, tm):
    s = pos_col.shape[0]

    def fn(pids, pos, inv):
        ang = pos * inv
        lane = lax.broadcasted_iota(jnp.int32, ang.shape, 1)
        cs = jnp.where(lane < ROPE_DIM, jnp.cos(ang), 1.0)
        sn = jnp.sin(ang)
        s_lo = jnp.where(lane < ROPE_HALF, -sn, 0.0)
        s_hi = jnp.where(jnp.logical_and(lane >= ROPE_HALF, lane < ROPE_DIM), sn, 0.0)
        return (cs, s_lo, s_hi), ()

    blk, imap = _rows(tm, 128)
    return _ew(
        "rope_tables", (s // tm,),
        [(pos_col, (tm, 1), lambda i: (i, 0)), (inv_row, (1, 128), lambda i: (0, 0))],
        [((s, 128), F32, blk, imap)] * 3, fn,
    )


def _rope(xh, cs, s_lo, s_hi):
    return xh * cs + pltpu.roll(xh, HEAD_DIM - ROPE_HALF, 1) * s_lo + pltpu.roll(xh, ROPE_HALF, 1) * s_hi


def _rope_t(gh, cs, s_lo, s_hi):
    return gh * cs + pltpu.roll(gh * s_lo, ROPE_HALF, 1) + pltpu.roll(gh * s_hi, HEAD_DIM - ROPE_HALF, 1)


def _attn_geometry(length):
    nb = length // ATTN_BLOCK
    gq = min(8, nb)
    assert nb % gq == 0
    return nb, gq, gq * ATTN_BLOCK, nb // gq


def _band_masks():
    qi = lax.broadcasted_iota(jnp.int32, (ATTN_BLOCK, ATTN_BLOCK), 0)
    kj = lax.broadcasted_iota(jnp.int32, (ATTN_BLOCK, ATTN_BLOCK), 1)
    return kj <= qi, kj >= qi


def _band_mask_pair():
    qi = lax.broadcasted_iota(jnp.int32, (ATTN_BLOCK, 2 * ATTN_BLOCK), 0)
    cj = lax.broadcasted_iota(jnp.int32, (ATTN_BLOCK, 2 * ATTN_BLOCK), 1)
    in_cur = cj >= ATTN_BLOCK
    band = jnp.logical_or(jnp.logical_and(in_cur, cj - ATTN_BLOCK <= qi),
                          jnp.logical_and(cj < ATTN_BLOCK, cj >= qi))
    return band, in_cur


def _attn_fwd(qv, kv, vv, dil, cols3=(0, 0, 0)):
    length = qv.shape[0]
    nb, gq, rows, ni = _attn_geometry(length)

    def body(q_ref, kc_ref, kp_ref, vc_ref, vp_ref, o_ref, l_ref):
        i = pl.program_id(1)
        band, in_cur = _band_mask_pair()
        band_first = jnp.logical_and(band, jnp.logical_or(in_cur, i > 0))
        work = []
        for h in range(HEADS_PER_GROUP):
            cols = slice(h * HEAD_DIM, (h + 1) * HEAD_DIM)
            qh = q_ref[:, cols]
            k_all = jnp.concatenate([kp_ref[:, cols], kc_ref[:, cols]], axis=0)
            v_all = jnp.concatenate([vp_ref[:, cols], vc_ref[:, cols]], axis=0)
            for jj in range(gq):
                rws = slice(jj * ATTN_BLOCK, (jj + 1) * ATTN_BLOCK)
                two = slice(jj * ATTN_BLOCK, (jj + 2) * ATTN_BLOCK)
                work.append(dict(h=h, rws=rws, cols=cols, v=v_all[two], first=jj == 0, s=_dot(qh[rws], k_all[two], "nt")))
        for w in work:
            s = jnp.where(band_first if w["first"] else band, w["s"], NEG_BIG)
            m = jnp.max(s, axis=-1, keepdims=True)
            pexp = jnp.exp(s - m)
            w["den"] = jnp.sum(pexp, axis=-1, keepdims=True)
            w["p"] = pexp.astype(BF16)
            w["lse"] = m + jnp.log(w["den"])
        for w in work:
            o = _dot(w["p"], w["v"], "nn")
            o_ref[w["rws"], w["cols"]] = (o * (1.0 / w["den"])).astype(o_ref.dtype)
            l_ref[w["rws"], w["h"] * LSE_LANES:(w["h"] + 1) * LSE_LANES] = jnp.broadcast_to(w["lse"], (ATTN_BLOCK, LSE_LANES))

    def cur(c):
        return pl.BlockSpec((rows, GROUP_WIDTH), lambda r, i: (i, r + c))

    def prev(c):
        return pl.BlockSpec((ATTN_BLOCK, GROUP_WIDTH), lambda r, i: (jnp.maximum(i * gq - 1, 0), r + c))

    cq, ck, cv = cols3
    return pl.pallas_call(
        body, name=f"attn_fwd_d{dil}", grid=(dil, ni),
        in_specs=[cur(cq), cur(ck), prev(ck), cur(cv), prev(cv)],
        out_specs=[cur(0), pl.BlockSpec((rows, LSE_WIDTH), lambda r, i: (i, r))],
        out_shape=[jax.ShapeDtypeStruct((length, dil * GROUP_WIDTH), BF16),
                   jax.ShapeDtypeStruct((length, dil * LSE_WIDTH), F32)],
        compiler_params=_params(2),
    )(qv, kv, kv, vv, vv)


def _attn_bwd(qv, kv, vv, dov, ov, lv, dil, cols3=(0, 0, 0)):
    length = qv.shape[0]
    nb, gq, rows, ni = _attn_geometry(length)
    out_shape = (length, dil * GROUP_WIDTH)

    def body(qc_ref, qn_ref, kc_ref, kp_ref, vc_ref, vp_ref, doc_ref, don_ref, oc_ref, on_ref, lc_ref, ln_ref,
             dq_ref, dk_ref, dv_ref):
        i = pl.program_id(1)
        _, mask_p = _band_masks()
        band, in_cur = _band_mask_pair()
        band_first = jnp.logical_and(band, jnp.logical_or(in_cur, i > 0))
        has_next = i < ni - 1

        last = slice(gq * ATTN_BLOCK, (gq + 1) * ATTN_BLOCK)
        mask_next = jnp.logical_and(mask_p, has_next)

        def rows_of(jj):
            return slice(jj * ATTN_BLOCK, (jj + 1) * ATTN_BLOCK)

        def keys_of(jj):
            return slice(jj * ATTN_BLOCK, (jj + 2) * ATTN_BLOCK)

        heads = []
        for h in range(HEADS_PER_GROUP):
            cols = slice(h * HEAD_DIM, (h + 1) * HEAD_DIM)
            hd = dict(
                cols=cols, q_c=qc_ref[:, cols], q_n=qn_ref[:, cols],
                k_all=jnp.concatenate([kp_ref[:, cols], kc_ref[:, cols]], axis=0),
                v_all=jnp.concatenate([vp_ref[:, cols], vc_ref[:, cols]], axis=0),
                do_c=doc_ref[:, cols], do_n=don_ref[:, cols],
                l_c=lc_ref[:, h * LSE_LANES:h * LSE_LANES + 1], l_n=ln_ref[:, h * LSE_LANES:h * LSE_LANES + 1],
            )
            hd["dl_c"] = jnp.sum(hd["do_c"].astype(F32) * oc_ref[:, cols].astype(F32), axis=-1, keepdims=True)
            hd["dl_n"] = jnp.sum(hd["do_n"].astype(F32) * on_ref[:, cols].astype(F32), axis=-1, keepdims=True)
            hd["s"] = [_dot(hd["q_c"][rows_of(jj)], hd["k_all"][keys_of(jj)], "nt") for jj in range(gq)]
            hd["dp"] = [_dot(hd["do_c"][rows_of(jj)], hd["v_all"][keys_of(jj)], "nt") for jj in range(gq)]
            hd["s"].append(_dot(hd["q_n"], hd["k_all"][last], "nt"))
            hd["dp"].append(_dot(hd["do_n"], hd["v_all"][last], "nt"))
            heads.append(hd)
        for hd in heads:
            hd["p"], hd["ds"] = [], []
            for jj in range(gq + 1):
                if jj < gq:
                    mask, l_col, delta = (band_first if jj == 0 else band), hd["l_c"][rows_of(jj)], hd["dl_c"][rows_of(jj)]
                else:
                    mask, l_col, delta = mask_next, hd["l_n"], hd["dl_n"]
                p = jnp.where(mask, jnp.exp(hd["s"][jj] - l_col), 0.0)
                hd["p"].append(p.astype(BF16))
                hd["ds"].append((p * (hd["dp"][jj] - delta)).astype(BF16))
        for hd in heads:
            cols = hd["cols"]
            dk_blocks, dv_blocks = [None] * (gq + 1), [None] * (gq + 1)

            def add(lst, idx, val):
                lst[idx] = val if lst[idx] is None else lst[idx] + val

            for jj in range(gq):
                qb, dob = hd["q_c"][rows_of(jj)], hd["do_c"][rows_of(jj)]
                dq_ref[rows_of(jj), cols] = _dot(hd["ds"][jj], hd["k_all"][keys_of(jj)], "nn").astype(dq_ref.dtype)
                dk2 = _dot(hd["ds"][jj], qb, "tn")
                dv2 = _dot(hd["p"][jj], dob, "tn")
                add(dk_blocks, jj, dk2[:ATTN_BLOCK])
                add(dk_blocks, jj + 1, dk2[ATTN_BLOCK:])
                add(dv_blocks, jj, dv2[:ATTN_BLOCK])
                add(dv_blocks, jj + 1, dv2[ATTN_BLOCK:])
            add(dk_blocks, gq, _dot(hd["ds"][gq], hd["q_n"], "tn"))
            add(dv_blocks, gq, _dot(hd["p"][gq], hd["do_n"], "tn"))
            for jj in range(gq):
                dk_ref[rows_of(jj), cols] = dk_blocks[jj + 1].astype(dk_ref.dtype)
                dv_ref[rows_of(jj), cols] = dv_blocks[jj + 1].astype(dv_ref.dtype)

    def cur(c):
        return pl.BlockSpec((rows, GROUP_WIDTH), lambda r, i: (i, r + c))

    def prev(c):
        return pl.BlockSpec((ATTN_BLOCK, GROUP_WIDTH), lambda r, i: (jnp.maximum(i * gq - 1, 0), r + c))

    def nxt(c):
        return pl.BlockSpec((ATTN_BLOCK, GROUP_WIDTH), lambda r, i: (jnp.minimum((i + 1) * gq, nb - 1), r + c))

    cq, ck, cv = cols3
    lse_cur = pl.BlockSpec((rows, LSE_WIDTH), lambda r, i: (i, r))
    lse_next = pl.BlockSpec((ATTN_BLOCK, LSE_WIDTH), lambda r, i: (jnp.minimum((i + 1) * gq, nb - 1), r))
    return pl.pallas_call(
        body, name=f"attn_bwd_d{dil}", grid=(dil, ni),
        in_specs=[cur(cq), nxt(cq), cur(ck), prev(ck), cur(cv), prev(cv), cur(0), nxt(0), cur(0), nxt(0), lse_cur, lse_next],
        out_specs=[cur(0), cur(0), cur(0)],
        out_shape=[jax.ShapeDtypeStruct(out_shape, BF16)] * 3,
        compiler_params=_params(2),
    )(qv, qv, kv, kv, vv, vv, dov, dov, ov, ov, lv, lv)


DILATED = tuple((g, d) for g, d in enumerate(GROUP_DILATIONS) if d > 1)


def _spread(scr, slot, tile, out_ref, dil, col, width=GROUP_WIDTH):
    tm = tile.shape[0]
    buf = scr.at[slot]
    buf[...] = tile
    for r in range(dil):
        c0 = r * width + col
        out_ref[:, c0:c0 + HEAD_DIM] = buf[pl.ds(r, tm // dil, stride=dil), :].astype(out_ref.dtype)


def _collect(scr, slot, in_ref, dil, col, width=GROUP_WIDTH):
    tm = scr.shape[1]
    buf = scr.at[slot]
    for r in range(dil):
        c0 = r * width + col
        buf[pl.ds(r, tm // dil, stride=dil), :] = in_ref[:, c0:c0 + HEAD_DIM].astype(F32)
    return buf[...]


def _view_spec(tm, dil, width=GROUP_WIDTH):
    return pl.BlockSpec((tm // dil, dil * width), lambda i: (i, 0))


def _view_shape(s, dil, dtype, width=GROUP_WIDTH):
    return jax.ShapeDtypeStruct((s // dil, dil * width), dtype)


def _qkv_layout(z, tabs, tm):
    s = z.shape[0]
    scale = 1.0 / math.sqrt(HEAD_DIM)
    qkv_width = 3 * N_GROUPS * GROUP_WIDTH

    def body(z_ref, cs_ref, lo_ref, hi_ref, qk0_ref, *rest):
        views, scr = rest[:-1], rest[-1]
        tabs_ = (cs_ref[...], lo_ref[...], hi_ref[...])
        for part in range(3):
            for g, dil in enumerate(GROUP_DILATIONS):
                if part == 2 and dil == 1:
                    continue
                for h in range(HEADS_PER_GROUP):
                    col = part * N_GROUPS * GROUP_WIDTH + g * GROUP_WIDTH + h * HEAD_DIM
                    t = z_ref[:, col:col + HEAD_DIM].astype(F32)
                    if part < 2:
                        t = _rope(t, *tabs_)
                    if part == 0:
                        t = t * scale
                    if dil == 1:
                        c0 = part * GROUP_WIDTH + h * HEAD_DIM
                        qk0_ref[:, c0:c0 + HEAD_DIM] = t.astype(BF16)
                    else:
                        out = views[3 * [gg for gg, _ in DILATED].index(g) + part]
                        _spread(scr, h, t, out, dil, h * HEAD_DIM)

    row = lambda i: (i, 0)
    out_shape = [jax.ShapeDtypeStruct((s, 2 * GROUP_WIDTH), BF16)]
    out_specs = [pl.BlockSpec((tm, 2 * GROUP_WIDTH), row)]
    for _, dil in DILATED:
        out_shape += [_view_shape(s, dil, BF16)] * 3
        out_specs += [_view_spec(tm, dil)] * 3
    res = pl.pallas_call(
        body, name="qkv_layout", grid=(s // tm,),
        in_specs=[pl.BlockSpec((tm, qkv_width), row)] + [pl.BlockSpec((tm, HEAD_DIM), row)] * 3,
        out_specs=out_specs, out_shape=out_shape,
        scratch_shapes=[pltpu.VMEM((HEADS_PER_GROUP, tm, HEAD_DIM), F32)], compiler_params=_params(1),
    )(z, *tabs)
    return res[0], [tuple(res[1 + 3 * n:4 + 3 * n]) for n in range(len(DILATED))]


def _attn_merge(o0, l0, dilated, tm):
    s = o0.shape[0]
    n_d = len(DILATED)

    def body(*refs):
        o0_ref, l0_ref = refs[:2]
        in_views = refs[2:2 + 2 * n_d]
        attn_ref, lse_ref = refs[2 + 2 * n_d:4 + 2 * n_d]
        out_views = refs[4 + 2 * n_d:4 + 4 * n_d]
        scr = refs[-1]
        l_rows = [l0_ref[...]] + [_collect(scr, n, in_views[2 * n + 1], dil, 0, LSE_WIDTH) for n, (_, dil) in enumerate(DILATED)]
        lse_heads = []
        for h in range(HEADS_PER_GROUP):
            cols = slice(h * HEAD_DIM, (h + 1) * HEAD_DIM)
            os_ = [o0_ref[:, cols].astype(F32)]
            for n, (_, dil) in enumerate(DILATED):
                os_.append(_collect(scr, n_d + n, in_views[2 * n], dil, h * HEAD_DIM))
            ls_ = [lr[:, h * LSE_LANES:h * LSE_LANES + 1] for lr in l_rows]
            m = ls_[0]
            for l_ in ls_[1:]:
                m = jnp.maximum(m, l_)
            es = [jnp.exp(l_ - m) for l_ in ls_]
            den = es[0]
            num = es[0] * os_[0]
            for e, o in zip(es[1:], os_[1:]):
                den = den + e
                num = num + e * o
            attn = num * (1.0 / den)
            lse_heads.append(jnp.broadcast_to(m + jnp.log(den), (tm, LSE_LANES)))
            attn_ref[:, cols] = attn.astype(BF16)
            for n, (_, dil) in enumerate(DILATED):
                _spread(scr, 2 * n_d, attn, out_views[2 * n], dil, h * HEAD_DIM)
        lse = jnp.concatenate(lse_heads, axis=1)
        lse_ref[...] = lse
        for n, (_, dil) in enumerate(DILATED):
            _spread(scr, 2 * n_d, lse, out_views[2 * n + 1], dil, 0, LSE_WIDTH)

    row = lambda i: (i, 0)
    nat = pl.BlockSpec((tm, GROUP_WIDTH), row)
    nat_l = pl.BlockSpec((tm, LSE_WIDTH), row)
    in_specs = [nat, nat_l]
    args = [o0, l0]
    out_specs = [nat, nat_l]
    out_shape = [jax.ShapeDtypeStruct((s, GROUP_WIDTH), BF16), jax.ShapeDtypeStruct((s, LSE_WIDTH), F32)]
    for (_, dil), (ov, lv) in zip(DILATED, dilated):
        in_specs += [_view_spec(tm, dil), _view_spec(tm, dil, LSE_WIDTH)]
        args += [ov, lv]
        out_specs += [_view_spec(tm, dil), _view_spec(tm, dil, LSE_WIDTH)]
        out_shape += [_view_shape(s, dil, BF16), _view_shape(s, dil, F32, LSE_WIDTH)]
    res = pl.pallas_call(
        body, name="attn_merge", grid=(s // tm,), in_specs=in_specs, out_specs=out_specs, out_shape=out_shape,
        scratch_shapes=[pltpu.VMEM((2 * n_d + 1, tm, HEAD_DIM), F32)], compiler_params=_params(1),
    )(*args)
    return res[0], res[1], [tuple(res[2 + 2 * n:4 + 2 * n]) for n in range(n_d)]


def _to_views(a, tm):
    s = a.shape[0]

    def body(a_ref, *rest):
        outs, scr = rest[:-1], rest[-1]
        for h in range(HEADS_PER_GROUP):
            t = a_ref[:, h * HEAD_DIM:(h + 1) * HEAD_DIM].astype(F32)
            for n, (_, dil) in enumerate(DILATED):
                _spread(scr, n, t, outs[n], dil, h * HEAD_DIM)

    return pl.pallas_call(
        body, name="to_views", grid=(s // tm,), in_specs=[pl.BlockSpec((tm, GROUP_WIDTH), lambda i: (i, 0))],
        out_specs=[_view_spec(tm, dil) for _, dil in DILATED], out_shape=[_view_shape(s, dil, BF16) for _, dil in DILATED],
        scratch_shapes=[pltpu.VMEM((len(DILATED), tm, HEAD_DIM), F32)], compiler_params=_params(1),
    )(a)


def _dz_layout(grads, du, dga, dgs, tabs, tm, comm=None):
    s = du.shape[0]
    scale = 1.0 / math.sqrt(HEAD_DIM)
    n_steps = s // tm
    c_ins = list(comm["ins"]) if comm else []
    c_outs = list(comm["outs"]) if comm else []
    c_sems = list(comm["sems"]) if comm else []
    n_fixed = 3 * N_GROUPS + 6

    def body(*refs):
        g_refs = refs[:3 * N_GROUPS]
        du_ref, dga_ref, dgs_ref, cs_ref, lo_ref, hi_ref = refs[3 * N_GROUPS:n_fixed]
        comm_in = refs[n_fixed:n_fixed + len(c_ins)]
        dz_ref = refs[n_fixed + len(c_ins)]
        comm_out = refs[n_fixed + len(c_ins) + 1:n_fixed + len(c_ins) + 1 + len(c_outs)]
        scr = refs[n_fixed + len(c_ins) + 1 + len(c_outs)]
        sems = refs[n_fixed + len(c_ins) + 2 + len(c_outs):]
        if comm:
            @pl.when(pl.program_id(0) == 0)
            def _():
                comm["start"](comm_in, comm_out, sems)

            @pl.when(pl.program_id(0) == n_steps - 1)
            def _():
                comm["finish"](comm_in, comm_out, sems)

        tabs_ = (cs_ref[...], lo_ref[...], hi_ref[...])
        for part in range(3):
            for g, dil in enumerate(GROUP_DILATIONS):
                src = g_refs[3 * g + part]
                for h in range(HEADS_PER_GROUP):
                    if dil == 1:
                        t = src[:, h * HEAD_DIM:(h + 1) * HEAD_DIM].astype(F32)
                    else:
                        t = _collect(scr, h, src, dil, h * HEAD_DIM)
                    if part < 2:
                        t = _rope_t(t, *tabs_)
                    if part == 0:
                        t = t * scale
                    col = part * N_GROUPS * GROUP_WIDTH + g * GROUP_WIDTH + h * HEAD_DIM
                    dz_ref[:, col:col + HEAD_DIM] = t.astype(BF16)
        dz_ref[:, COL_U:COL_GA] = du_ref[...]
        dz_ref[:, COL_GA:COL_GS] = dga_ref[...]
        dz_ref[:, COL_GS:IN_WIDTH] = dgs_ref[...]

    row = lambda i: (i, 0)
    in_specs, args = [], []
    for (g, dil), trio in zip(enumerate(GROUP_DILATIONS), grads):
        in_specs += [pl.BlockSpec((tm, GROUP_WIDTH), row) if dil == 1 else _view_spec(tm, dil)] * 3
        args += list(trio)
    in_specs += [pl.BlockSpec((tm, SSM_WIDTH), row), pl.BlockSpec((tm, D_MODEL), row), pl.BlockSpec((tm, D_MODEL), row)]
    in_specs += [pl.BlockSpec((tm, HEAD_DIM), row)] * 3
    in_specs += [pl.BlockSpec(memory_space=pl.ANY)] * len(c_ins)
    res = pl.pallas_call(
        body, name="dz_layout", grid=(n_steps,), in_specs=in_specs,
        out_specs=[pl.BlockSpec((tm, IN_WIDTH), row)] + [pl.BlockSpec(memory_space=pl.ANY)] * len(c_outs),
        out_shape=[jax.ShapeDtypeStruct((s, IN_WIDTH), BF16)] + c_outs,
        scratch_shapes=[pltpu.VMEM((HEADS_PER_GROUP, tm, HEAD_DIM), F32)] + [pltpu.SemaphoreType.DMA((n,)) for n in c_sems],
        compiler_params=_params(1),
    )(*args, du, dga, dgs, *tabs, *c_ins)
    return res[0], list(res[1:])


def _discretise(a_re, a_im, log_dt, bt_re, bt_im):
    dt = jnp.exp(log_dt)
    mag = jnp.exp(a_re * dt)
    bar_re = mag * jnp.cos(a_im * dt)
    bar_im = mag * jnp.sin(a_im * dt)
    nr = bar_re - 1.0
    ni = bar_im
    den = a_re * a_re + a_im * a_im
    z_re = (nr * a_re + ni * a_im) / den
    z_im = (ni * a_re - nr * a_im) / den
    bb_re = z_re[:, None, :] * bt_re - z_im[:, None, :] * bt_im
    bb_im = z_re[:, None, :] * bt_im + z_im[:, None, :] * bt_re
    return bar_re, bar_im, bb_re, bb_im


def _ssm_prep(a_re, a_im, log_dt, bt_re, bt_im):
    def body(ar, ai, ld, br, bi, o_lr, o_li, o_br, o_bi):
        lr, li, bbr, bbi = _discretise(ar[...], ai[...], ld[...], br[...], bi[...])
        o_lr[...] = lr
        o_li[...] = li
        o_br[...] = bbr
        o_bi[...] = bbi

    sm = jax.ShapeDtypeStruct((SSM_GROUPS, SSM_STATE), F32)
    bg = jax.ShapeDtypeStruct((SSM_GROUPS, SSM_GROUP, SSM_STATE), F32)
    return pl.pallas_call(body, name="ssm_prep", out_shape=[sm, sm, bg, bg])(a_re, a_im, log_dt, bt_re, bt_im)


def _ssm_param_bwd(a_re, a_im, log_dt, bt_re, bt_im, d_lr, d_li, d_bbr, d_bbi):
    def body(ar, ai, ld, br, bi, g_lr, g_li, g_br, g_bi, o_ar, o_ai, o_ld, o_br, o_bi):
        _, vjp = jax.vjp(_discretise, ar[...], ai[...], ld[...], br[...], bi[...])
        d_ar, d_ai, d_ld, d_br, d_bi = vjp((g_lr[...], g_li[...], g_br[...], g_bi[...]))
        o_ar[...] = d_ar
        o_ai[...] = d_ai
        o_ld[...] = d_ld
        o_br[...] = d_br
        o_bi[...] = d_bi

    sm = jax.ShapeDtypeStruct((SSM_GROUPS, SSM_STATE), F32)
    col = jax.ShapeDtypeStruct((SSM_GROUPS, 1), F32)
    bg = jax.ShapeDtypeStruct((SSM_GROUPS, SSM_GROUP, SSM_STATE), F32)
    return pl.pallas_call(body, name="ssm_param_bwd", out_shape=[sm, sm, col, bg, bg])(
        a_re, a_im, log_dt, bt_re, bt_im, d_lr, d_li, d_bbr, d_bbi)


def _block_diag(t, rows_per, cols_per):
    t4 = t.reshape(SSM_SUPER, 8, rows_per, cols_per)
    eye = jnp.eye(8, dtype=t.dtype)
    return jnp.einsum("bgrc,gh->bgrhc", t4, eye).reshape(SSM_SUPER, 8 * rows_per, 8 * cols_per)


def _block_diag_t(dense, rows_per, cols_per):
    t = dense.reshape(SSM_SUPER, 8, rows_per, 8, cols_per)
    eye = jnp.eye(8, dtype=dense.dtype)
    return jnp.einsum("bgrhc,gh->bgrc", t, eye).reshape(SSM_GROUPS, rows_per, cols_per)


def _gelu(v):
    c = math.sqrt(2.0 / math.pi)
    return 0.5 * v * (1.0 + jnp.tanh(c * (v + 0.044715 * v * v * v)))


def _gelu_grad(v):
    c = math.sqrt(2.0 / math.pi)
    t = jnp.tanh(c * (v + 0.044715 * v * v * v))
    return 0.5 * (1.0 + t) + 0.5 * v * (1.0 - t * t) * c * (1.0 + 3.0 * 0.044715 * v * v)


SUB = 8


SCAN_STEPS = (1, 2, 4)
N_SCAN_TABLES = 2 + 2 * len(SCAN_STEPS)


def _scan_tables(tab_ref, lam_re, lam_im, reverse, conj):
    lr = lam_re
    li = -lam_im if conj else lam_im
    powers = [(lr, li)]
    for _ in range(SUB - 1):
        pr, pi = powers[-1]
        powers.append((pr * lr - pi * li, pr * li + pi * lr))
    row = lax.broadcasted_iota(jnp.int32, (SUB, N_STATE), 0)
    if reverse:
        row = SUB - 1 - row
    wide = lambda v: jnp.broadcast_to(v, (SUB, N_STATE))
    p_re = jnp.zeros((SUB, N_STATE), F32)
    p_im = jnp.zeros((SUB, N_STATE), F32)
    for j in range(SUB):
        p_re = jnp.where(row == j, wide(powers[j][0]), p_re)
        p_im = jnp.where(row == j, wide(powers[j][1]), p_im)
    tab_ref[0] = p_re
    tab_ref[1] = p_im
    for idx, k in enumerate(SCAN_STEPS):
        tab_ref[2 + 2 * idx] = jnp.where(row >= k, wide(powers[k - 1][0]), 0.0)
        tab_ref[3 + 2 * idx] = jnp.where(row >= k, wide(powers[k - 1][1]), 0.0)


def _scan_rows(g_re_ref, g_im_ref, tab_ref, carry, n_rows, reverse):
    last = 0 if reverse else SUB - 1

    def tile_step(tt, state):
        cr, ci = state
        t8 = (n_rows // SUB - 1 - tt) if reverse else tt
        start = pl.multiple_of(t8 * SUB, SUB)
        xr = g_re_ref[pl.ds(start, SUB), :]
        xi = g_im_ref[pl.ds(start, SUB), :]
        for idx, k in enumerate(SCAN_STEPS):
            mr = tab_ref[2 + 2 * idx]
            mi = tab_ref[3 + 2 * idx]
            shift = SUB - k if reverse else k
            sr = pltpu.roll(xr, shift, 0)
            si = pltpu.roll(xi, shift, 0)
            xr, xi = xr + (mr * sr - mi * si), xi + (mr * si + mi * sr)
        pr = tab_ref[0]
        pi = tab_ref[1]
        xr, xi = xr + (pr * cr - pi * ci), xi + (pr * ci + pi * cr)
        g_re_ref[pl.ds(start, SUB), :] = xr
        g_im_ref[pl.ds(start, SUB), :] = xi
        return (jnp.broadcast_to(xr[last:last + 1, :], (SUB, N_STATE)),
                jnp.broadcast_to(xi[last:last + 1, :], (SUB, N_STATE)))

    return lax.fori_loop(0, n_rows // SUB, tile_step, carry)


def _ssm_fwd(z, b_re, b_im, c_re, c_im, lam_re, lam_im, d_skip, chunk):
    s = z.shape[0]

    def body(u_ref, bre, bim, cre, cim, lre, lim, dsk, hre_ref, him_ref, ys_ref, yg_ref, car_re, car_im, tabs):
        i = pl.program_id(0)

        @pl.when(i == 0)
        def _():
            car_re[...] = jnp.zeros_like(car_re)
            car_im[...] = jnp.zeros_like(car_im)
            _scan_tables(tabs, lre[...], lim[...], False, False)

        u = u_ref[...]
        for b in range(SSM_SUPER):
            ub = u[:, b * 128:(b + 1) * 128]
            st = slice(b * 512, (b + 1) * 512)
            hre_ref[:, st] = _dot(ub, bre[b], "nn")
            him_ref[:, st] = _dot(ub, bim[b], "nn")
        sr, si = _scan_rows(hre_ref, him_ref, tabs, (car_re[...], car_im[...]), chunk, False)
        car_re[...] = sr
        car_im[...] = si
        uf = u.astype(F32)
        for b in range(SSM_SUPER):
            st = slice(b * 512, (b + 1) * 512)
            ch = slice(b * 128, (b + 1) * 128)
            y = _dot(hre_ref[:, st].astype(BF16), cre[b], "nn") - _dot(him_ref[:, st].astype(BF16), cim[b], "nn")
            y = y + dsk[:, ch] * uf[:, ch]
            ys_ref[:, ch] = y
            yg_ref[:, ch] = _gelu(y).astype(BF16)

    full3 = lambda i: (0, 0, 0)
    full2 = lambda i: (0, 0)
    row = lambda i: (i, 0)
    u_col = COL_U // SSM_WIDTH
    return pl.pallas_call(
        body, name="ssm_fwd", grid=(s // chunk,),
        in_specs=[pl.BlockSpec((chunk, SSM_WIDTH), lambda i: (i, u_col)),
                  pl.BlockSpec((SSM_SUPER, 128, 512), full3), pl.BlockSpec((SSM_SUPER, 128, 512), full3),
                  pl.BlockSpec((SSM_SUPER, 512, 128), full3), pl.BlockSpec((SSM_SUPER, 512, 128), full3),
                  pl.BlockSpec((1, N_STATE), full2), pl.BlockSpec((1, N_STATE), full2), pl.BlockSpec((1, SSM_WIDTH), full2)],
        out_specs=[pl.BlockSpec((chunk, N_STATE), row), pl.BlockSpec((chunk, N_STATE), row),
                   pl.BlockSpec((chunk, SSM_WIDTH), row), pl.BlockSpec((chunk, SSM_WIDTH), row)],
        out_shape=[jax.ShapeDtypeStruct((s, N_STATE), F32), jax.ShapeDtypeStruct((s, N_STATE), F32),
                   jax.ShapeDtypeStruct((s, SSM_WIDTH), F32), jax.ShapeDtypeStruct((s, SSM_WIDTH), BF16)],
        scratch_shapes=[pltpu.VMEM((SUB, N_STATE), F32), pltpu.VMEM((SUB, N_STATE), F32),
                        pltpu.VMEM((N_SCAN_TABLES, SUB, N_STATE), F32)],
        compiler_params=_params(1),
    )(z, b_re, b_im, c_re, c_im, lam_re, lam_im, d_skip)


def _ssm_bwd(dys, z, h_re, h_im, b_re, b_im, c_re, c_im, lam_re, lam_im, d_skip, chunk):
    s = z.shape[0]
    n_chunks = s // chunk

    def body(dy_ref, u_ref, hre_ref, him_ref, hpr_ref, hpi_ref, bre, bim, cre, cim, lre, lim, dsk,
             du_ref, dlr_ref, dli_ref, dbr_ref, dbi_ref, dcr_ref, dci_ref, dd_ref, are, aim, car_re, car_im, tabs):
        i = pl.program_id(0)
        n = n_chunks - 1 - i

        @pl.when(i == 0)
        def _():
            car_re[...] = jnp.zeros_like(car_re)
            car_im[...] = jnp.zeros_like(car_im)
            _scan_tables(tabs, lre[...], lim[...], True, True)
            for r in (dlr_ref, dli_ref, dbr_ref, dbi_ref, dcr_ref, dci_ref, dd_ref):
                r[...] = jnp.zeros_like(r)

        dy = dy_ref[...]
        dyb = dy.astype(BF16)
        u = u_ref[...]
        for b in range(SSM_SUPER):
            ch = slice(b * 128, (b + 1) * 128)
            st = slice(b * 512, (b + 1) * 512)
            are[:, st] = _dot(dyb[:, ch], cre[b], "nt")
            aim[:, st] = -_dot(dyb[:, ch], cim[b], "nt")
        sr, si = _scan_rows(are, aim, tabs, (car_re[...], car_im[...]), chunk, True)
        car_re[...] = sr
        car_im[...] = si
        dd_ref[...] += jnp.sum(dy * u.astype(F32), axis=0, keepdims=True)
        row_id = lax.broadcasted_iota(jnp.int32, (chunk, 512), 0)
        top_scale = jnp.where(n > 0, 1.0, 0.0)
        for b in range(SSM_SUPER):
            ch = slice(b * 128, (b + 1) * 128)
            st = slice(b * 512, (b + 1) * 512)
            h_r = hre_ref[:, st]
            h_i = him_ref[:, st]
            hp_r = jnp.where(row_id == 0, hpr_ref[SUB - 1:SUB, st] * top_scale, pltpu.roll(h_r, 1, 0))
            hp_i = jnp.where(row_id == 0, hpi_ref[SUB - 1:SUB, st] * top_scale, pltpu.roll(h_i, 1, 0))
            a_r = are[:, st]
            a_i = aim[:, st]
            dlr_ref[:, st] += jnp.sum(a_r * hp_r + a_i * hp_i, axis=0, keepdims=True)
            dli_ref[:, st] += jnp.sum(a_i * hp_r - a_r * hp_i, axis=0, keepdims=True)
            a_rb = a_r.astype(BF16)
            a_ib = a_i.astype(BF16)
            dbr_ref[b] += _dot(u[:, ch], a_rb, "tn")
            dbi_ref[b] += _dot(u[:, ch], a_ib, "tn")
            dcr_ref[b] += _dot(dyb[:, ch], h_r.astype(BF16), "tn")
            dci_ref[b] += -_dot(dyb[:, ch], h_i.astype(BF16), "tn")
            du = _dot(a_rb, bre[b], "nt") + _dot(a_ib, bim[b], "nt") + dsk[:, ch] * dy[:, ch]
            du_ref[:, ch] = du.astype(du_ref.dtype)

    full3 = lambda i: (0, 0, 0)
    full2 = lambda i: (0, 0)
    rev = lambda i: (n_chunks - 1 - i, 0)
    above = lambda i: (jnp.maximum((n_chunks - 1 - i) * (chunk // SUB) - 1, 0), 0)
    u_col = COL_U // SSM_WIDTH
    b_spec = pl.BlockSpec((SSM_SUPER, 128, 512), full3)
    c_spec = pl.BlockSpec((SSM_SUPER, 512, 128), full3)
    vec = pl.BlockSpec((1, N_STATE), full2)
    return pl.pallas_call(
        body, name="ssm_bwd", grid=(n_chunks,),
        in_specs=[pl.BlockSpec((chunk, SSM_WIDTH), rev),
                  pl.BlockSpec((chunk, SSM_WIDTH), lambda i: (n_chunks - 1 - i, u_col)),
                  pl.BlockSpec((chunk, N_STATE), rev), pl.BlockSpec((chunk, N_STATE), rev),
                  pl.BlockSpec((SUB, N_STATE), above), pl.BlockSpec((SUB, N_STATE), above),
                  b_spec, b_spec, c_spec, c_spec, vec, vec, pl.BlockSpec((1, SSM_WIDTH), full2)],
        out_specs=[pl.BlockSpec((chunk, SSM_WIDTH), rev), vec, vec, b_spec, b_spec, b_spec, b_spec,
                   pl.BlockSpec((1, SSM_WIDTH), full2)],
        out_shape=[jax.ShapeDtypeStruct((s, SSM_WIDTH), BF16),
                   jax.ShapeDtypeStruct((1, N_STATE), F32), jax.ShapeDtypeStruct((1, N_STATE), F32)]
        + [jax.ShapeDtypeStruct((SSM_SUPER, 128, 512), F32)] * 4 + [jax.ShapeDtypeStruct((1, SSM_WIDTH), F32)],
        scratch_shapes=[pltpu.VMEM((chunk, N_STATE), F32), pltpu.VMEM((chunk, N_STATE), F32),
                        pltpu.VMEM((SUB, N_STATE), F32), pltpu.VMEM((SUB, N_STATE), F32),
                        pltpu.VMEM((N_SCAN_TABLES, SUB, N_STATE), F32)],
        compiler_params=_params(1),
    )(dys, z, h_re, h_im, h_re, h_im, b_re, b_im, c_re, c_im, lam_re, lam_im, d_skip)


def _local_step(x, p, pos, tgt, sm, w_in_own, w_in_buf, late_bufs, place):
    s = x.shape[0]
    tm = min(512, s)
    ts = min(2048, s)
    chunk = min(512, s)
    ni = s // tm
    nk = s // ts
    g_mix, g_ffn, g_final = sm["g_mix"], sm["g_ffn"], sm["g_final"]

    tmb = min(1024, s)
    nib = s // tmb
    chip_w = IN_WIDTH // N_CHIPS
    w_start, w_forward, w_finish = _gather_stages(1)
    gather_in = dict(ins=[w_in_buf], outs=[jax.ShapeDtypeStruct(w_in_buf.shape, w_in_buf.dtype)], aliased=True,
                     sems=[3] * 4, stages=[(0, w_start), (nib - 1, w_forward), (nib - 1, w_finish)])
    a_rows = lambda i, j, k, pv: (i, 0)
    z_own, n1, w_in_all = _mm("in_proj_own", (nib, 1, 1),
                              [(x, (tmb, D_MODEL), a_rows, w_in_own, (D_MODEL, chip_w), lambda i, j, k, pv: (0, 0))], "nn",
                              [((s, IN_WIDTH), BF16, (tmb, chip_w), lambda i, j, k, pv: (i, pv[P_CHIP])),
                               ((s, D_MODEL), BF16, (tmb, D_MODEL), a_rows)],
                              extras=[(g_mix, (1, D_MODEL), lambda i, j, k, pv: (0, 0))],
                              epilogue=lambda acc, g: ((acc,), ()), prologue=_rms_fwd_tile, comm=gather_in, place=place)
    w_in = w_in_all.reshape(N_CHIPS, D_MODEL, chip_w)
    n_late = len(late_bufs)
    g_start, g_forward, g_finish = _gather_stages(n_late)
    in_steps = (N_CHIPS - 1) * nib
    gather = dict(ins=late_bufs, outs=[jax.ShapeDtypeStruct(b.shape, b.dtype) for b in late_bufs], aliased=True,
                  sems=[3 * n_late] * 4,
                  stages=[(0, g_start), ((4 * in_steps) // 5, g_forward), (in_steps - 1, g_finish)])
    other = lambda j, pv: (pv[P_CHIP] + 1 + j) % N_CHIPS
    z, *late = _mm("in_proj", (nib, N_CHIPS - 1, 1),
                   [(n1, (tmb, D_MODEL), a_rows, w_in, (None, D_MODEL, chip_w), lambda i, j, k, pv: (other(j, pv), 0, 0))],
                   "nn", [((s, IN_WIDTH), BF16, (tmb, chip_w), lambda i, j, k, pv: (i, other(j, pv)))], j_outer=True,
                   comm=gather, place=place, fill=z_own)
    w_ap, w_ga, w_gb, w_out, w_fg, w_fu, w_fd, w_pg, w_pp = (
        g.reshape(N_CHIPS, 2 * g.shape[2], g.shape[3]) for g in late)
    w_out2 = w_out.reshape(D_MODEL, D_MODEL)
    w_pg2 = w_pg.reshape(D_MODEL, D_MODEL)

    inv = ROPE_THETA ** (-jnp.arange(ROPE_HALF, dtype=F32) * 2.0 / ROPE_DIM)
    inv_row = jnp.concatenate([inv, inv, jnp.zeros((HEAD_DIM - ROPE_DIM,), F32)]).reshape(1, HEAD_DIM)
    tabs = _rope_tables(pos.astype(F32).reshape(s, 1), inv_row, tm)

    qk0, qkv_views = _qkv_layout(z, tabs, tm)
    v0_col = (2 * N_GROUPS * GROUP_WIDTH) // GROUP_WIDTH
    group_in = [((qk0, qk0, z), (0, 1, v0_col))] + [(trio, (0, 0, 0)) for trio in qkv_views]
    fwd_out = [_attn_fwd(*arrs, dil, cols3) for (arrs, cols3), dil in zip(group_in, GROUP_DILATIONS)]
    attn, lse, merged_views = _attn_merge(fwd_out[0][0], fwd_out[0][1], fwd_out[1:], tm)

    def chip_cols(parts):
        return (jnp.concatenate(parts, axis=1),), ()

    def proj_cols(name, a, width, w):
        blk = (None, width, 256)
        pairs = [(a, (tmb, width), lambda i, j, k: (i, 0), w, blk, lambda i, j, k: (0, 0, 0))]
        pairs += [(None, None, None, w, blk, (lambda i, j, k, q=q: (q, 0, 0))) for q in range(1, N_CHIPS)]
        return _mm(name, (nib, 1, 1), pairs, "nn", [((s, D_MODEL), BF16, (tmb, D_MODEL), lambda i, j, k: (i, 0))],
                   epilogue=chip_cols, sum_pairs=False)[0]

    def proj512(name, a, w):
        return proj_cols(name, a, GROUP_WIDTH, w)

    attn_d = proj512("attn_proj", attn, w_ap)

    bt_re = jnp.transpose(sm["b_re"], (0, 2, 1))
    bt_im = jnp.transpose(sm["b_im"], (0, 2, 1))
    log_dt_col = sm["log_dt"].reshape(SSM_GROUPS, 1)
    lam_re, lam_im, bbt_re, bbt_im = _ssm_prep(sm["a_re"], sm["a_im"], log_dt_col, bt_re, bt_im)
    b_re_m = _block_diag(bbt_re, SSM_GROUP, SSM_STATE).astype(BF16)
    b_im_m = _block_diag(bbt_im, SSM_GROUP, SSM_STATE).astype(BF16)
    c_re_m = _block_diag(jnp.transpose(sm["c_re"], (0, 2, 1)), SSM_STATE, SSM_GROUP).astype(BF16)
    c_im_m = _block_diag(jnp.transpose(sm["c_im"], (0, 2, 1)), SSM_STATE, SSM_GROUP).astype(BF16)
    lam_re_row = lam_re.reshape(1, N_STATE)
    lam_im_row = lam_im.reshape(1, N_STATE)
    d_skip_row = sm["d_skip"].reshape(1, SSM_WIDTH)
    h_re, h_im, ys, yg = _ssm_fwd(z, b_re_m, b_im_m, c_re_m, c_im_m, lam_re_row, lam_im_row, d_skip_row, chunk)

    pa = proj512("glu_a", yg, w_ga)
    pb = proj512("glu_b", yg, w_gb)

    def mix_pro(ad, xr, g, ga, gs, a, b):
        ga, gs, ad, a, b = (t.astype(F32) for t in (ga, gs, ad, a, b))
        return _sig(ga) * ad + _sig(gs) * (a * _sig(b))

    def out_epi(acc, xr, g, *_):
        h1 = acc + xr
        return (h1, _rms_fwd_tile(h1, g)), ()

    m3 = lambda i, j, k: (i, 0)
    w3 = lambda i, j, k: (0, 0)
    tile_d = (tm, D_MODEL)
    h1, n2, mix = _mm("out_proj", (ni, 1, 1), [(attn_d, tile_d, m3, w_out2, (D_MODEL, D_MODEL), w3)], "nn",
                      [((s, D_MODEL), F32, tile_d, m3), ((s, D_MODEL), BF16, tile_d, m3), ((s, D_MODEL), BF16, tile_d, m3)],
                      epilogue=out_epi, prologue=mix_pro,
                      extras=[(x, tile_d, m3), (g_ffn, (1, D_MODEL), w3),
                              (z, tile_d, lambda i, j, k: (i, COL_GA // D_MODEL)), (z, tile_d, lambda i, j, k: (i, COL_GS // D_MODEL)),
                              (pa, tile_d, m3), (pb, tile_d, m3)])

    ffq = (None, tm, D_FF_Q)
    ffq_map = lambda i, j, k: (j, i, 0)

    def ffn_in_epi(parts):
        gts, ups = parts[0::2], parts[1::2]
        acts = [gt * _sig(gt) * u_ for gt, u_ in zip(gts, ups)]
        return (jnp.stack(gts, axis=0), jnp.stack(ups, axis=0), jnp.stack(acts, axis=0)), ()

    w_ffq = (None, D_MODEL, D_FF_Q)
    ff_pairs = []
    for q in range(N_CHIPS):
        blk_q = lambda i, j, k, q=q: (q, 0, 0)
        ff_pairs.append((n2, (tm, D_MODEL), m3, w_fg, w_ffq, blk_q) if q == 0 else (None, None, None, w_fg, w_ffq, blk_q))
        ff_pairs.append((None, None, None, w_fu, w_ffq, blk_q))
    ff_all = (N_CHIPS, tm, D_FF_Q)
    ff_all_map = lambda i, j, k: (0, i, 0)
    gate, up, act = _mm("ffn_gate_up", (ni, 1, 1), ff_pairs, "nn",
                        [((N_CHIPS, s, D_FF_Q), BF16, ff_all, ff_all_map)] * 3, epilogue=ffn_in_epi,
                        sum_pairs=False, resident_b=True)

    (h2,) = _mm("ffn_down", (nib, 1, 1),
                [(act, (None, tmb, D_FF_Q), (lambda i, j, k, q=q: (q, i, 0)), w_fd, (None, D_FF_Q, D_MODEL),
                  (lambda i, j, k, q=q: (q, 0, 0))) for q in range(N_CHIPS)], "nn",
                [((s, D_MODEL), F32, (tmb, D_MODEL), m3)], epilogue=lambda acc, hr: ((acc + hr,), ()),
                extras=[(h1, (tmb, D_MODEL), m3)])

    pp = proj_cols("ple_proj", p, PLE_DIM, w_pp)

    def ple_head_epi(acc, hr, ppr, t, g):
        sg = _sig(acc)
        ppf = ppr.astype(F32)
        h = hr + sg * ppf
        r = lax.rsqrt(jnp.mean(h * h, axis=-1, keepdims=True) + EPS)
        hhat = h * r
        diff = hhat * g - t
        loss = 0.5 * jnp.sum(jnp.mean(diff * diff, axis=-1, keepdims=True))
        dy = diff * (1.0 / D_MODEL)
        gy = dy * g
        dh = r * (gy - hhat * jnp.mean(gy * hhat, axis=-1, keepdims=True))
        return ((dh, dh * ppf * sg * (1.0 - sg), dh * sg),
                (jnp.full((SUB, 128), loss, F32), jnp.sum(dy * hhat, axis=0, keepdims=True)))

    tile_row = (tm, D_MODEL)
    dh3, dgl, dpp, loss_acc, dg_final = _mm(
        "ple_gate_head", (ni, 1, 1), [(h2, tile_row, m3, w_pg2, (D_MODEL, D_MODEL), w3)], "nn",
        [((s, D_MODEL), F32, tile_row, m3), ((s, D_MODEL), BF16, tile_row, m3), ((s, D_MODEL), BF16, tile_row, m3)],
        epilogue=ple_head_epi,
        extras=[(h2, tile_row, m3), (pp, tile_row, m3), (tgt, tile_row, m3), (g_final, (1, D_MODEL), w3)],
        acc_outs=[((SUB, 128), F32), ((1, D_MODEL), F32)])

    def wgrad(name, a, a_block, a_imap, b, b_block, b_imap, out_shape, out_block, out_imap, nj, acc_shape):
        return _mm(name, (1, nj, nk), [(a, a_block, a_imap, b, b_block, b_imap)], "tn",
                   [(out_shape, F32, out_block, out_imap)], acc_shape=acc_shape)[0]

    tk0 = lambda i, j, k: (k, 0)
    tkj = lambda i, j, k: (k, j)
    def wgrad_cols(name, a, width, dy_):
        def split(acc):
            return (jnp.stack([acc[:, q * 256:(q + 1) * 256] for q in range(N_CHIPS)], axis=0),), ()

        return _mm(name, (1, 1, nk), [(a, (ts, width), tk0, dy_, (ts, D_MODEL), tk0)], "tn",
                   [((N_CHIPS, width, 256), F32, (N_CHIPS, width, 256), lambda i, j, k: (0, 0, 0))], epilogue=split,
                   acc_shape=(width, D_MODEL))[0]

    d_w_pp = wgrad_cols("d_ple_proj", p, PLE_DIM, dpp)
    d_w_pg = wgrad("d_ple_gate", h2, (ts, D_MODEL), tk0, dgl, (ts, D_MODEL), tk0, (D_MODEL, D_MODEL),
                   (D_MODEL, D_MODEL), w3, 1, (D_MODEL, D_MODEL))

    (dh2,) = _mm("ple_gate_bwd", (nib, 1, 1), [(dgl, (tmb, D_MODEL), m3, w_pg2, (D_MODEL, D_MODEL), w3)], "nt",
                 [((s, D_MODEL), F32, (tmb, D_MODEL), m3)], epilogue=lambda acc, d_: ((acc + d_,), ()),
                 extras=[(dh3, (tmb, D_MODEL), m3)])

    def ffn_bwd_epi(parts, gt_all, u_all):
        dgs_, dus_ = [], []
        for q, dact in enumerate(parts):
            gt, u_ = gt_all[q].astype(F32), u_all[q].astype(F32)
            sg = _sig(gt)
            dgs_.append(dact * u_ * (sg * (1.0 + gt * (1.0 - sg))))
            dus_.append(dact * gt * sg)
        return (jnp.stack(dgs_, axis=0), jnp.stack(dus_, axis=0)), ()

    fd_pairs = [((dh2, (tm, D_MODEL), m3) if q == 0 else (None, None, None))
                + (w_fd, (None, D_FF_Q, D_MODEL), (lambda i, j, k, q=q: (q, 0, 0))) for q in range(N_CHIPS)]
    dgate, dup = _mm("ffn_down_bwd", (ni, 1, 1), fd_pairs, "nt",
                     [((N_CHIPS, s, D_FF_Q), BF16, ff_all, ff_all_map)] * 2, epilogue=ffn_bwd_epi,
                     extras=[(gate, ff_all, ff_all_map), (up, ff_all, ff_all_map)], sum_pairs=False, resident_b=True)

    ffq_t = (None, ts, D_FF_Q)
    ffq_tmap = lambda i, j, k: (j, k, 0)
    blk_j = lambda i, j, k: (j, 0, 0)
    d_w_fd = wgrad("d_ffn_down", act, ffq_t, ffq_tmap, dh2, (ts, D_MODEL), tk0, (N_CHIPS, D_FF_Q, D_MODEL),
                   (None, D_FF_Q, D_MODEL), blk_j, N_CHIPS, (D_FF_Q, D_MODEL))
    d_w_fg = wgrad("d_ffn_gate", n2, (ts, D_MODEL), tk0, dgate, ffq_t, ffq_tmap, (N_CHIPS, D_MODEL, D_FF_Q),
                   (None, D_MODEL, D_FF_Q), blk_j, N_CHIPS, (D_MODEL, D_FF_Q))
    d_w_fu = wgrad("d_ffn_up", n2, (ts, D_MODEL), tk0, dup, ffq_t, ffq_tmap, (N_CHIPS, D_MODEL, D_FF_Q),
                   (None, D_MODEL, D_FF_Q), blk_j, N_CHIPS, (D_MODEL, D_FF_Q))

    def norm_bwd_epi(acc, h, d_res, g):
        dh, dg = _rms_bwd_tile(acc, h, g)
        return (d_res + dh,), (dg,)

    fi_pairs = []
    for q in range(N_CHIPS):
        a_q = lambda i, j, k, q=q: (q, i, 0)
        b_q = lambda i, j, k, q=q: (q, 0, 0)
        fi_pairs.append((dgate, ffq, a_q, w_fg, (None, D_MODEL, D_FF_Q), b_q))
        fi_pairs.append((dup, ffq, a_q, w_fu, (None, D_MODEL, D_FF_Q), b_q))
    dh1, dg_ffn = _mm("ffn_in_bwd", (ni, 1, 1), fi_pairs, "nt",
                      [((s, D_MODEL), F32, (tm, D_MODEL), m3)], epilogue=norm_bwd_epi,
                      extras=[(h1, (tm, D_MODEL), m3), (dh2, (tm, D_MODEL), m3), (g_ffn, (1, D_MODEL), w3)],
                      acc_outs=[((1, D_MODEL), F32)], resident_b=True)

    d_w_out = wgrad("d_out_proj", mix, (ts, D_MODEL), tk0, dh1, (ts, D_MODEL), tk0, (D_MODEL, D_MODEL),
                    (D_MODEL, D_MODEL), w3, 1, (D_MODEL, D_MODEL))

    def mix_bwd_epi(dm, ga, gs, ad, a, b):
        ga, gs, ad, a, b = (t.astype(F32) for t in (ga, gs, ad, a, b))
        s_a, s_s, s_b = _sig(ga), _sig(gs), _sig(b)
        d_ssm = dm * s_s
        return (dm * ad * s_a * (1.0 - s_a), dm * (a * s_b) * s_s * (1.0 - s_s), dm * s_a, d_ssm * s_b,
                d_ssm * a * s_b * (1.0 - s_b)), ()

    tile_m = (tm, D_MODEL)
    dga, dgs, dattn_d, dpa, dpb = _mm(
        "out_proj_bwd", (ni, 1, 1), [(dh1, tile_m, m3, w_out2, (D_MODEL, D_MODEL), w3)], "nt",
        [((s, D_MODEL), BF16, tile_m, m3)] * 5, epilogue=mix_bwd_epi,
        extras=[(z, tile_m, lambda i, j, k: (i, COL_GA // D_MODEL)), (z, tile_m, lambda i, j, k: (i, COL_GS // D_MODEL)),
                (attn_d, tile_m, m3), (pa, tile_m, m3), (pb, tile_m, m3)])

    d_w_ap = wgrad_cols("d_attn_proj", attn, GROUP_WIDTH, dattn_d)
    d_w_ga = wgrad_cols("d_glu_a", yg, GROUP_WIDTH, dpa)
    d_w_gb = wgrad_cols("d_glu_b", yg, GROUP_WIDTH, dpb)

    ik = lambda i, j, k: (i, k)

    def cols_bwd(dy_, w):
        return [(dy_, (tmb, 256), (lambda i, j, k, q=q: (i, q)), w, (None, GROUP_WIDTH, 256),
                 (lambda i, j, k, q=q: (q, 0, 0))) for q in range(N_CHIPS)]

    (dattn,) = _mm("attn_proj_bwd", (nib, 1, 1), cols_bwd(dattn_d, w_ap), "nt",
                   [((s, GROUP_WIDTH), BF16, (tmb, GROUP_WIDTH), m3)])

    (dys,) = _mm("glu_bwd", (nib, 1, 1), cols_bwd(dpa, w_ga) + cols_bwd(dpb, w_gb), "nt",
                 [((s, GROUP_WIDTH), F32, (tmb, GROUP_WIDTH), m3)],
                 epilogue=lambda acc, y_: ((acc * _gelu_grad(y_),), ()),
                 extras=[(ys, (tmb, GROUP_WIDTH), m3)])

    du, d_lr, d_li, d_bre, d_bim, d_cre, d_cim, d_dskip = _ssm_bwd(
        dys, z, h_re, h_im, b_re_m, b_im_m, c_re_m, c_im_m, lam_re_row, lam_im_row, d_skip_row, chunk)

    dattn_views = _to_views(dattn, tm)
    bwd_in = [(dattn, attn, lse)] + [(dv_, ov_, lv_) for dv_, (ov_, lv_) in zip(dattn_views, merged_views)]
    qkv_grads = [_attn_bwd(*arrs, *dol, dil, cols3)
                 for (arrs, cols3), dol, dil in zip(group_in, bwd_in, GROUP_DILATIONS)]
    early = [d_w_ap, d_w_ga, d_w_gb, d_w_out.reshape(N_CHIPS, D_MODEL // N_CHIPS, D_MODEL), d_w_fg, d_w_fu, d_w_fd,
             d_w_pg.reshape(N_CHIPS, D_MODEL // N_CHIPS, D_MODEL), d_w_pp]
    early5 = [g.reshape(N_CHIPS, 2, g.shape[1] // 2, g.shape[2]) for g in early]
    n_e = len(early5)
    p_start, p_finish = _pair_exchange_stages(n_e)
    dz, early_theirs = _dz_layout(
        qkv_grads, du, dga, dgs, tabs, tm,
        comm=dict(ins=early5, outs=_pair_exchange_shapes(early5), sems=[N_CHIPS * n_e] * 2, start=p_start, finish=p_finish))
    early_parts = [_pair_sum(g, t, place) for g, t in zip(early5, early_theirs)]

    chip_in = IN_WIDTH // N_CHIPS
    ip_pairs = [(dz, (tm, chip_in), (lambda i, j, k, q=q: (i, q)), w_in, (None, D_MODEL, chip_in),
                 (lambda i, j, k, q=q: (q, 0, 0))) for q in range(N_CHIPS)]
    grad_x, dg_mix = _mm("in_proj_bwd", (ni, 1, 1), ip_pairs, "nt",
                         [((s, D_MODEL), F32, (tm, D_MODEL), m3)], epilogue=norm_bwd_epi,
                         extras=[(x, (tm, D_MODEL), m3), (dh1, (tm, D_MODEL), m3), (g_mix, (1, D_MODEL), w3)],
                         acc_outs=[((1, D_MODEL), F32)], resident_b=True)

    d_bbt_re = _block_diag_t(d_bre, SSM_GROUP, SSM_STATE)
    d_bbt_im = _block_diag_t(d_bim, SSM_GROUP, SSM_STATE)
    d_a_re, d_a_im, d_log_dt, d_bt_re, d_bt_im = _ssm_param_bwd(
        sm["a_re"], sm["a_im"], log_dt_col, bt_re, bt_im,
        d_lr.reshape(SSM_GROUPS, SSM_STATE), d_li.reshape(SSM_GROUPS, SSM_STATE), d_bbt_re, d_bbt_im)
    small = {
        "g_mix": dg_mix, "a_re": d_a_re, "a_im": d_a_im, "log_dt": d_log_dt,
        "b_re": jnp.transpose(d_bt_re, (0, 2, 1)), "b_im": jnp.transpose(d_bt_im, (0, 2, 1)),
        "c_re": _block_diag_t(d_cre, SSM_GROUP, SSM_STATE), "c_im": _block_diag_t(d_cim, SSM_GROUP, SSM_STATE),
        "d_skip": d_dskip, "g_ffn": dg_ffn, "g_final": dg_final,
    }
    vec = _pack([small[n] for n in SMALL] + [loss_acc[0, 0].reshape(1)])

    x_start, x_finish = _chip_exchange_stages(n_e)
    v_start, v_finish = _all_exchange_stages()

    def both(f_chips, f_vec):
        def stage(ins, outs, sems):
            f_chips(ins[:n_e], outs[:n_e], sems[:2])
            f_vec(ins[n_e:], outs[n_e:], sems[2:])
        return stage

    half_in = chip_in // 2
    win_steps = 8 * nk
    exchange = dict(ins=early_parts + [vec],
                    outs=[jax.ShapeDtypeStruct(t.shape, t.dtype) for t in early_parts]
                    + [jax.ShapeDtypeStruct((8,) + vec.shape, vec.dtype)],
                    aliased=False, sems=[3 * n_e, 3 * n_e, 7, 7],
                    stages=[(0, both(x_start, v_start)), (win_steps - 1, both(x_finish, v_finish))])
    d_w_in, *got = _mm("d_in_proj", (1, 8, nk), [(n1, (ts, D_MODEL), tk0, dz, (ts, half_in), tkj)], "tn",
                       [((N_CHIPS, D_MODEL, chip_in), F32, (None, D_MODEL, half_in), lambda i, j, k: (j // 2, 0, j % 2))],
                       acc_shape=(D_MODEL, half_in), comm=exchange)
    return grad_x, d_w_in, early_parts, got[:n_e], vec, got[n_e]


BIG = ("w_in", "w_attn_proj", "w_glu_a", "w_glu_b", "w_out", "w_ffn_gate", "w_ffn_up", "w_ffn_down", "w_ple_gate",
       "w_ple_proj")
SMALL = ("g_mix", "a_re", "a_im", "log_dt", "b_re", "b_im", "c_re", "c_im", "d_skip", "g_ffn", "g_final")
ANY = pl.BlockSpec(memory_space=pl.ANY)


def _place():
    x, y, c = lax.axis_index("x"), lax.axis_index("y"), lax.axis_index("c")
    chips = [(1 - x, y), (x, 1 - y), (1 - x, 1 - y)]
    return x, y, c, chips


def _remote(src, dst, send_sem, recv_sem, to):
    return pltpu.make_async_remote_copy(src_ref=src, dst_ref=dst, send_sem=send_sem, recv_sem=recv_sem, device_id=to,
                                        device_id_type=MESH)


def _comm_call(name, body, ins, out_shapes, n_sems, aliases=None):
    n_w = len(ins)
    return pl.pallas_call(
        body, name=name, in_specs=[ANY] * n_w, out_specs=[ANY] * len(out_shapes), out_shape=out_shapes,
        scratch_shapes=[pltpu.SemaphoreType.DMA((n,)) for n in n_sems], input_output_aliases=aliases or {},
    )(*ins)


def _gather_stages(n_w):
    def each():
        x, y, c, chips = _place()
        for w in range(n_w):
            for j, (cx, cy) in enumerate(chips):
                yield w, 3 * w + j, 2 * x + y, 2 * cx + cy, (cx, cy, c), (x, y, 1 - c), c

    def start(ins, outs, sems):
        for w, k, me, _, peer, _, c in each():
            mine = outs[w].at[me, c]
            _remote(mine, mine, sems[0].at[k], sems[1].at[k], peer).start()

    def forward(ins, outs, sems):
        for w, k, _, src_chip, peer, sib, c in each():
            landed = outs[w].at[src_chip, c]
            _remote(landed, landed, sems[0].at[k], sems[1].at[k], peer).wait_recv()
            _remote(landed, landed, sems[2].at[k], sems[3].at[k], sib).start()

    def finish(ins, outs, sems):
        for w, k, me, src_chip, peer, sib, c in each():
            other = outs[w].at[src_chip, 1 - c]
            _remote(other, other, sems[2].at[k], sems[3].at[k], sib).wait_recv()
        for w, k, me, src_chip, peer, sib, c in each():
            mine = outs[w].at[me, c]
            _remote(mine, mine, sems[0].at[k], sems[1].at[k], peer).wait_send()
            landed = outs[w].at[src_chip, c]
            _remote(landed, landed, sems[2].at[k], sems[3].at[k], sib).wait_send()

    return start, forward, finish


def _pair_exchange(grads):
    n_w = len(grads)
    start, finish = _pair_exchange_stages(n_w)

    def body(*refs):
        ins, outs, sems = refs[:n_w], refs[n_w:2 * n_w], refs[2 * n_w:]
        start(ins, outs, sems)
        finish(ins, outs, sems)

    return _comm_call("grad_pair_exchange", body, grads, _pair_exchange_shapes(grads), [N_CHIPS * n_w] * 2)


def _pair_exchange_shapes(grads):
    return [jax.ShapeDtypeStruct((N_CHIPS,) + g.shape[2:], g.dtype) for g in grads]


def _pair_exchange_stages(n_w):
    def each():
        x, y, c, _ = _place()
        for w in range(n_w):
            for q in range(N_CHIPS):
                yield w, q, N_CHIPS * w + q, c, (x, y, 1 - c)

    def start(ins, outs, sems):
        for w, q, k, c, sib in each():
            _remote(ins[w].at[q, 1 - c], outs[w].at[q], sems[0].at[k], sems[1].at[k], sib).start()

    def finish(ins, outs, sems):
        for w, q, k, c, sib in each():
            _remote(ins[w].at[q, 1 - c], outs[w].at[q], sems[0].at[k], sems[1].at[k], sib).wait()

    return start, finish


def _chip_exchange(parts):
    n_w = len(parts)

    start, finish = _chip_exchange_stages(n_w)

    def body(*refs):
        ins, outs, sems = refs[:n_w], refs[n_w:2 * n_w], refs[2 * n_w:]
        start(ins, outs, sems)
        finish(ins, outs, sems)

    out_shapes = [jax.ShapeDtypeStruct(t.shape, t.dtype) for t in parts]
    return _comm_call("grad_chip_exchange", body, parts, out_shapes, [3 * n_w, 3 * n_w])


def _chip_exchange_stages(n_w):
    def each():
        x, y, c, chips = _place()
        for w in range(n_w):
            for j, (cx, cy) in enumerate(chips):
                yield w, 3 * w + j, 2 * x + y, 2 * cx + cy, (cx, cy, c)

    def start(ins, outs, sems):
        for w, k, me, peer_chip, peer in each():
            _remote(ins[w].at[peer_chip], outs[w].at[me], sems[0].at[k], sems[1].at[k], peer).start()

    def finish(ins, outs, sems):
        for w, k, me, peer_chip, peer in each():
            got = outs[w].at[peer_chip]
            _remote(got, got, sems[0].at[k], sems[1].at[k], peer).wait_recv()
        for w, k, me, peer_chip, peer in each():
            _remote(ins[w].at[peer_chip], outs[w].at[me], sems[0].at[k], sems[1].at[k], peer).wait_send()

    return start, finish


def _pair_gather(halves):
    n_w = len(halves)

    def body(*refs):
        ins, outs = refs[:n_w], refs[n_w:2 * n_w]
        send, recv = refs[2 * n_w:]
        x, y, c, _ = _place()
        sib = (x, y, 1 - c)
        cps = []
        for w in range(n_w):
            cp = _remote(ins[w], outs[w], send.at[w], recv.at[w], sib)
            cp.start()
            cps.append(cp)
        for cp in cps:
            cp.wait()

    out_shapes = [jax.ShapeDtypeStruct(h.shape, h.dtype) for h in halves]
    return _comm_call("grad_pair_gather", body, halves, out_shapes, [n_w] * 2)


def _all_exchange_stages():
    def each():
        x, y, c, _ = _place()
        for k in range(1, 8):
            px, py, pc = x ^ ((k >> 2) & 1), y ^ ((k >> 1) & 1), c ^ (k & 1)
            yield k - 1, 4 * x + 2 * y + c, 4 * px + 2 * py + pc, (px, py, pc)

    def start(ins, outs, sems):
        for k, me, _, peer in each():
            _remote(ins[0], outs[0].at[me], sems[0].at[k], sems[1].at[k], peer).start()

    def finish(ins, outs, sems):
        for k, me, src, peer in each():
            got = outs[0].at[src]
            _remote(got, got, sems[0].at[k], sems[1].at[k], peer).wait_recv()
        for k, me, src, peer in each():
            _remote(ins[0], outs[0].at[me], sems[0].at[k], sems[1].at[k], peer).wait_send()

    return start, finish


def _row_tile(r):
    for t in (256, 128, 176, 64, 32, 16, 8):
        if r % t == 0:
            return t
    return r


P_C, P_CHIP, P_DEV = 2, 3, 4


def _cast_shard(w2):
    r, c = w2.shape
    t = _row_tile(r)
    blk, imap = _rows(t, c)
    return _ew("cast_own", (r // t,), [(w2, blk, imap)], [((r, c), BF16, blk, imap)], lambda pids, a: ((a,), ()))[0]


def _cast_into_slot(w2, place):
    r, c = w2.shape
    t = _row_tile(r)
    return _ew("cast_shard", (r // t,), [(w2, (t, c), lambda i, pv: (i, 0))],
               [((N_CHIPS, r, c), BF16, (None, t, c), lambda i, pv: (pv[P_CHIP], i, 0))],
               lambda pids, a: ((a,), ()), place=place)[0]


def _pair_sum(mine, theirs, place):
    _, r, c = theirs.shape
    t = _row_tile(r)
    own = ((None, None, t, c), lambda q, i, pv: (q, pv[P_C], i, 0))
    blk = ((None, t, c), lambda q, i, pv: (q, i, 0))
    return _ew("grad_pair_sum", (N_CHIPS, r // t), [(mine, *own), (theirs, *blk)], [((N_CHIPS, r, c), BF16, *blk)],
               lambda pids, a, b: ((a + b,), ()), place=place)[0]


def _chip_sum(own, got, place):
    _, r, c = own.shape
    t = _row_tile(r)
    ins = []
    for q in range(N_CHIPS):
        ins.append((own, (None, t, c), (lambda i, pv, q=q: (q, i, 0))))
        ins.append((got, (None, t, c), (lambda i, pv, q=q: (jnp.where(pv[P_CHIP] == q, (q + 1) % N_CHIPS, q), i, 0))))

    def fn(pids, *tiles):
        me = pids[0][P_CHIP]
        tot = None
        for q in range(N_CHIPS):
            term = jnp.where(me == q, tiles[2 * q], tiles[2 * q + 1]).astype(F32)
            tot = term if tot is None else tot + term
        return (tot,), ()

    return _ew("grad_chip_sum", (r // t,), ins, [((r, c), F32, (t, c), lambda i, pv: (i, 0))], fn, place=place)[0]


def _adamw_tile(w, g, m, v):
    m = ADAM_B1 * m + (1.0 - ADAM_B1) * g
    v = ADAM_B2 * v + (1.0 - ADAM_B2) * (g * g)
    m_hat = m / (1.0 - ADAM_B1 ** ADAM_STEP)
    v_hat = v / (1.0 - ADAM_B2 ** ADAM_STEP)
    delta = -ADAM_LR * (m_hat / (jnp.sqrt(v_hat) + ADAM_EPS) + ADAM_WD * w)
    return delta, m, v


def _adamw(name, g2, w2, m2, v2):
    r, c = w2.shape
    t = _row_tile(r)
    blk, imap = _rows(t, c)

    def fn(pids, g, w, m, v):
        delta, nm, nv = _adamw_tile(w, g, m, v)
        return (g, delta, nm, nv), ()

    return _ew(name, (r // t,), [(a, blk, imap) for a in (g2, w2, m2, v2)], [((r, c), F32, blk, imap)] * 4, fn)


def _adamw_halves(name, mine, theirs, w2, m2, v2, place):
    r, c = w2.shape
    t = _row_tile(r // 2)
    n_t = (r // 2) // t
    half = ((t, c), lambda h, i, pv: (i, 0))
    whole = ((t, c), lambda h, i, pv: (h * n_t + i, 0))

    def fn(pids, ga, gb, w, m, v):
        g = jnp.where(pids[1] == pids[0][P_C], ga, gb)
        delta, nm, nv = _adamw_tile(w, g, m, v)
        return (g, delta, nm, nv), ()

    return _ew(name, (2, n_t), [(mine, *half), (theirs, *half), (w2, *whole), (m2, *whole), (v2, *whole)],
               [((r, c), F32, *whole)] * 4, fn, place=place)


def _device_sum(own, got, place):
    r, c = own.shape
    t = _row_tile(r)
    ins = [(own, (t, c), lambda i, pv: (i, 0))]
    for q in range(8):
        ins.append((got, (None, t, c), (lambda i, pv, q=q: (jnp.where(pv[P_DEV] == q, (q + 1) % 8, q), i, 0))))

    def fn(pids, mine, *parts):
        me = pids[0][P_DEV]
        tot = None
        for q in range(8):
            term = jnp.where(me == q, mine, parts[q])
            tot = term if tot is None else tot + term
        return (tot,), ()

    return _ew("small_device_sum", (r // t,), ins, [((r, c), F32, (t, c), lambda i, pv: (i, 0))], fn, place=place)[0]


def _pack(parts):
    flat = jnp.concatenate([a.reshape(-1) for a in parts])
    pad = (-flat.shape[0]) % (SUB * 128)
    return jnp.pad(flat, (0, pad)).reshape(-1, 128)


def _unpack(mat, shapes):
    flat = mat.reshape(-1)
    out, off = [], 0
    for shp in shapes:
        n = math.prod(shp)
        out.append(flat[off:off + n].reshape(shp))
        off += n
    return out


def kernel(x, p, positions, g_mix, w_in, a_re, a_im, log_dt, b_re, b_im, c_re, c_im, d_skip, w_attn_proj, w_glu_a, w_glu_b, w_out, g_ffn, w_ffn_gate, w_ffn_up, w_ffn_down, w_ple_gate, w_ple_proj, g_final, loss_target, m_g_mix, m_w_in, m_a_re, m_a_im, m_log_dt, m_b_re, m_b_im, m_c_re, m_c_im, m_d_skip, m_w_attn_proj, m_w_glu_a, m_w_glu_b, m_w_out, m_g_ffn, m_w_ffn_gate, m_w_ffn_up, m_w_ffn_down, m_w_ple_gate, m_w_ple_proj, m_g_final, v_g_mix, v_w_in, v_a_re, v_a_im, v_log_dt, v_b_re, v_b_im, v_c_re, v_c_im, v_d_skip, v_w_attn_proj, v_w_glu_a, v_w_glu_b, v_w_out, v_g_ffn, v_w_ffn_gate, v_w_ffn_up, v_w_ffn_down, v_w_ple_gate, v_w_ple_proj, v_g_final):
    given = dict(locals())
    big_w = {n: given[n] for n in BIG}
    w_mats = {n: big_w[n].reshape(big_w[n].shape[1:]) for n in BIG}

    ax, ay, ac = lax.axis_index("x"), lax.axis_index("y"), lax.axis_index("c")
    place = jnp.stack([ax, ay, ac, 2 * ax + ay, 4 * ax + 2 * ay + ac]).astype(jnp.int32)

    bufs = []
    for n in BIG:
        r, c = w_mats[n].shape
        bufs.append(_cast_into_slot(w_mats[n], place).reshape(N_CHIPS, 2, r // 2, c))
    w_in_own = _cast_shard(w_mats["w_in"])

    sm = {
        "g_mix": g_mix.reshape(1, D_MODEL), "g_ffn": g_ffn.reshape(1, D_MODEL), "g_final": g_final.reshape(1, D_MODEL),
        "a_re": a_re[0], "a_im": a_im[0], "log_dt": log_dt[0], "b_re": b_re[0], "b_im": b_im[0], "c_re": c_re[0],
        "c_im": c_im[0], "d_skip": d_skip[0],
    }
    s = x.shape[1]
    grad_x, d_w_in, early_parts, early_got, vec, vec_got = _local_step(
        x[0], p[0, 0], positions[0], loss_target[0], sm, w_in_own, bufs[0], bufs[1:], place)

    r_in, c_in = w_mats["w_in"].shape
    g5_in = [d_w_in.reshape(N_CHIPS, 2, r_in // 2, c_in)]
    in_parts = [_pair_sum(g, t, place) for g, t in zip(g5_in, _pair_exchange(g5_in))]
    chip_parts = in_parts + list(early_parts)
    chip_got = list(_chip_exchange(in_parts)) + list(early_got)
    halves = [_chip_sum(own, got, place) for own, got in zip(chip_parts, chip_got)]
    other_halves = _pair_gather(halves)

    results = {}
    for n, mine, other in zip(BIG, halves, other_halves):
        r, c = w_mats[n].shape
        shp = big_w[n].shape
        outs = _adamw_halves("adamw_" + n, mine, other, w_mats[n], given["m_" + n].reshape(r, c),
                             given["v_" + n].reshape(r, c), place)
        results[n] = [o.reshape(shp) for o in outs]

    small_shapes = [given[n].shape for n in SMALL]
    tot = _device_sum(vec, vec_got, place)
    n_small = sum(math.prod(shp) for shp in small_shapes)
    loss = tot.reshape(-1)[n_small]
    w_s = _pack([given[n] for n in SMALL])
    m_s = _pack([given["m_" + n] for n in SMALL])
    v_s = _pack([given["v_" + n] for n in SMALL])
    rows_s = w_s.shape[0]
    g_s = tot.reshape(-1)[: rows_s * 128].reshape(rows_s, 128)
    outs_s = _adamw("adamw_small", g_s, w_s, m_s, v_s)
    for kind, mat in enumerate(outs_s):
        for n, arr in zip(SMALL, _unpack(mat, small_shapes)):
            results.setdefault(n, [None] * 4)[kind] = arr

    order = ("g_mix", "w_in", "a_re", "a_im", "log_dt", "b_re", "b_im", "c_re", "c_im", "d_skip", "w_attn_proj", "w_glu_a",
             "w_glu_b", "w_out", "g_ffn", "w_ffn_gate", "w_ffn_up", "w_ffn_down", "w_ple_gate", "w_ple_proj", "g_final")
    out = [loss, grad_x.reshape(1, s, D_MODEL)]
    for kind in range(4):
        out += [results[n][kind] for n in order]
    return tuple(out)
```

```python
import math

import jax
import jax.numpy as jnp
from jax import lax
from jax.experimental import pallas as pl
from jax.experimental.pallas import tpu as pltpu

F32 = jnp.float32
BF16 = jnp.bfloat16

D_MODEL = 1024
HEAD_DIM = 128
HEADS_PER_GROUP = 4
GROUP_WIDTH = HEADS_PER_GROUP * HEAD_DIM
GROUP_DILATIONS = (1, 4, 16)
N_GROUPS = len(GROUP_DILATIONS)
LSE_LANES = 32
LSE_WIDTH = HEADS_PER_GROUP * LSE_LANES
ATTN_BLOCK = 128
ROPE_DIM = 32
ROPE_HALF = 16
ROPE_THETA = 500000.0
SSM_WIDTH = 512
SSM_GROUPS = 32
SSM_GROUP = 16
SSM_STATE = 64
N_STATE = SSM_GROUPS * SSM_STATE
SSM_SUPER = 4
IN_WIDTH = 7168
COL_U = 4608
COL_GA = 5120
COL_GS = 6144
D_FF = 2816
N_CHIPS = 4
D_FF_Q = D_FF // N_CHIPS
PLE_DIM = 256
EPS = 1e-6
ADAM_LR = 0.001
ADAM_B1 = 0.9
ADAM_B2 = 0.999
ADAM_EPS = 1e-08
ADAM_WD = 0.01
ADAM_STEP = 10
NEG_BIG = -1e30
VMEM_LIMIT_BYTES = 56 * 1024 * 1024
MESH = pl.DeviceIdType.MESH

_DIMS = {
    "nn": (((1,), (0,)), ((), ())),
    "nt": (((1,), (1,)), ((), ())),
    "tn": (((0,), (0,)), ((), ())),
}


def _params(n_grid):
    return pltpu.CompilerParams(dimension_semantics=("arbitrary",) * n_grid, vmem_limit_bytes=VMEM_LIMIT_BYTES)


def _sig(v):
    return 0.5 * jnp.tanh(0.5 * v) + 0.5


def _dot(a, b, mode):
    return lax.dot_general(a, b, _DIMS[mode], preferred_element_type=F32)


def _mm(name, grid, pairs, mode, outs, epilogue=None, extras=(), acc_outs=(), acc_shape=None, j_outer=False,
        sum_pairs=True, resident_b=False, comm=None, place=None, fill=None, prologue=None):
    gi, gj, gk = grid
    n_p, n_e, n_o, n_a = len(pairs), len(extras), len(outs), len(acc_outs)
    assert not n_a or gj == 1
    assert sum_pairs or gk == 1
    run_grid = (gj, gi, gk) if j_outer else grid
    c_ins = list(comm["ins"]) if comm else []
    c_outs = list(comm["outs"]) if comm else []
    c_sems = list(comm["sems"]) if comm else []
    n_ci, n_co, n_cs = len(c_ins), len(c_outs), len(c_sems)
    n_s = 0 if place is None else 1
    n_fill = 0 if fill is None else 1

    def order(imap):
        if place is None:
            return (lambda j, i, k: imap(i, j, k)) if j_outer else imap
        return (lambda j, i, k, pv: imap(i, j, k, pv)) if j_outer else imap

    shared_a = [pr[0] is None for pr in pairs]
    n_in = 2 * n_p - sum(shared_a)

    def body(*refs):
        refs = refs[n_s:]
        pair_refs = list(refs[:n_in])
        extra_refs = refs[n_in: n_in + n_e]
        comm_in = refs[n_in + n_e: n_in + n_e + n_ci]
        at = n_in + n_e + n_ci + n_fill
        out_refs = refs[at: at + n_o]
        sum_refs = refs[at + n_o: at + n_o + n_a]
        comm_out = refs[at + n_o + n_a: at + n_o + n_a + n_co]
        scratch_refs = refs[at + n_o + n_a + n_co:]
        i = pl.program_id(1 if j_outer else 0)
        k = pl.program_id(2)
        if comm:
            step = (pl.program_id(0) * run_grid[1] + pl.program_id(1)) * run_grid[2] + pl.program_id(2)
            sems = scratch_refs[len(scratch_refs) - n_cs:]
            for at_step, stage in comm["stages"]:
                @pl.when(step == at_step)
                def _(stage=stage):
                    stage(comm_in, comm_out, sems)
        part = None if sum_pairs else []
        a = None
        for t in range(n_p):
            if not shared_a[t]:
                a = pair_refs.pop(0)[...]
                if prologue is not None and t == 0:
                    a = prologue(a, *[e[...] for e in extra_refs]).astype(BF16)
                    out_refs[n_o - 1][...] = a
                a = a.astype(BF16)
            b = pair_refs.pop(0)[...].astype(BF16)
            d = _dot(a, b, mode)
            if sum_pairs:
                part = d if part is None else part + d
            else:
                part.append(d)

        def finish(acc):
            tiles, sums = epilogue(acc, *[e[...] for e in extra_refs]) if epilogue is not None else ((acc,), ())
            for o_ref, tile in zip(out_refs, tiles):
                o_ref[...] = tile.astype(o_ref.dtype)
            if n_a:
                @pl.when(i == 0)
                def _():
                    for s_ref in sum_refs:
                        s_ref[...] = jnp.zeros_like(s_ref)

                for s_ref, s in zip(sum_refs, sums):
                    s_ref[...] += s

        if gk == 1:
            finish(part)
        else:
            acc_ref = scratch_refs[0]

            @pl.when(k == 0)
            def _():
                acc_ref[...] = part

            @pl.when(k > 0)
            def _():
                acc_ref[...] += part

            @pl.when(k == gk - 1)
            def _():
                finish(acc_ref[...])

    in_specs, args = [], []
    for a, a_block, a_imap, b, b_block, b_imap in pairs:
        if a is not None:
            in_specs.append(pl.BlockSpec(a_block, order(a_imap)))
            args.append(a)
        if resident_b:
            in_specs.append(pl.BlockSpec(b_block, order(b_imap), pipeline_mode=pl.Buffered(1)))
        else:
            in_specs.append(pl.BlockSpec(b_block, order(b_imap)))
        args.append(b)
    for e, e_block, e_imap in extras:
        in_specs.append(pl.BlockSpec(e_block, order(e_imap)))
        args.append(e)
    first_comm_in = len(args)
    for c_in in c_ins:
        in_specs.append(pl.BlockSpec(memory_space=pl.ANY))
        args.append(c_in)
    if n_fill:
        in_specs.append(pl.BlockSpec(memory_space=pl.ANY))
        args.append(fill)
    out_shape = [jax.ShapeDtypeStruct(shape, dtype) for shape, dtype, _, _ in outs]
    out_specs = [pl.BlockSpec(block, order(imap)) for _, _, block, imap in outs]
    for shape, dtype in acc_outs:
        out_shape.append(jax.ShapeDtypeStruct(shape, dtype))
        out_specs.append(pl.BlockSpec(shape, lambda *_: (0, 0)))
    first_comm_out = len(out_shape)
    for c_out in c_outs:
        out_shape.append(c_out)
        out_specs.append(pl.BlockSpec(memory_space=pl.ANY))
    aliases = {n_s + first_comm_in + n: first_comm_out + n for n in range(n_ci)} if comm and comm["aliased"] else {}
    if n_fill:
        aliases[n_s + len(args) - 1] = 0
    scratch = [pltpu.VMEM(acc_shape, F32)] if gk > 1 else []
    scratch += [pltpu.SemaphoreType.DMA((n,)) for n in c_sems]
    if n_s:
        spec = pltpu.PrefetchScalarGridSpec(num_scalar_prefetch=1, grid=run_grid, in_specs=in_specs, out_specs=out_specs,
                                            scratch_shapes=scratch)
        return pl.pallas_call(body, name=name, grid_spec=spec, out_shape=out_shape, compiler_params=_params(3),
                              input_output_aliases=aliases)(place, *args)
    return pl.pallas_call(
        body, name=name, grid=run_grid, in_specs=in_specs, out_specs=out_specs,
        out_shape=out_shape, scratch_shapes=scratch, compiler_params=_params(3), input_output_aliases=aliases,
    )(*args)


def _ew(name, grid, ins, outs, fn, acc_outs=(), place=None):
    n_i, n_o, n_a = len(ins), len(outs), len(acc_outs)
    ng = len(grid)
    n_s = 0 if place is None else 1

    def body(*refs):
        in_refs = refs[n_s: n_s + n_i]
        out_refs = refs[n_s + n_i: n_s + n_i + n_o]
        sum_refs = refs[n_s + n_i + n_o:]
        pids = tuple(pl.program_id(a) for a in range(ng))
        if n_s:
            pids = (refs[0],) + pids
        tiles, sums = fn(pids, *[r[...] for r in in_refs])
        for o_ref, tile in zip(out_refs, tiles):
            o_ref[...] = tile.astype(o_ref.dtype)
        if n_a:
            first = pids[0] == 0
            for p_ in pids[1:]:
                first = jnp.logical_and(first, p_ == 0)

            @pl.when(first)
            def _():
                for s_ref in sum_refs:
                    s_ref[...] = jnp.zeros_like(s_ref)

            for s_ref, s in zip(sum_refs, sums):
                s_ref[...] += s

    in_specs = [pl.BlockSpec(block, imap) for _, block, imap in ins]
    out_shape = [jax.ShapeDtypeStruct(shape, dtype) for shape, dtype, _, _ in outs]
    out_specs = [pl.BlockSpec(block, imap) for _, _, block, imap in outs]
    for shape, dtype in acc_outs:
        out_shape.append(jax.ShapeDtypeStruct(shape, dtype))
        out_specs.append(pl.BlockSpec(shape, lambda *_, nd=len(shape): (0,) * nd))
    arrays = [a for a, _, _ in ins]
    if n_s:
        assert not n_a
        spec = pltpu.PrefetchScalarGridSpec(num_scalar_prefetch=1, grid=grid, in_specs=in_specs, out_specs=out_specs)
        return pl.pallas_call(body, name=name, grid_spec=spec, out_shape=out_shape, compiler_params=_params(ng))(
            place, *arrays)
    return pl.pallas_call(
        body, name=name, grid=grid, in_specs=in_specs, out_specs=out_specs, out_shape=out_shape,
        compiler_params=_params(ng),
    )(*arrays)


def _rows(tm, width):
    return (tm, width), (lambda i: (i, 0))


def _rms_fwd_tile(h, g):
    r = lax.rsqrt(jnp.mean(h * h, axis=-1, keepdims=True) + EPS)
    return h * r * g


def _rms_bwd_tile(dn, h, g):
    r = lax.rsqrt(jnp.mean(h * h, axis=-1, keepdims=True) + EPS)
    hhat = h * r
    gy = dn * g
    dh = r * (gy - hhat * jnp.mean(gy * hhat, axis=-1, keepdims=True))
    dg = jnp.sum(dn * hhat, axis=0, keepdims=True)
    return dh, dg


def _rope_tables(pos_col, inv_row, tm):
    s = pos_col.shape[0]

    def fn(pids, pos, inv):
        ang = pos * inv
        lane = lax.broadcasted_iota(jnp.int32, ang.shape, 1)
        cs = jnp.where(lane < ROPE_DIM, jnp.cos(ang), 1.0)
        sn = jnp.sin(ang)
        s_lo = jnp.where(lane < ROPE_HALF, -sn, 0.0)
        s_hi = jnp.where(jnp.logical_and(lane >= ROPE_HALF, lane < ROPE_DIM), sn, 0.0)
        return (cs, s_lo, s_hi), ()

    blk, imap = _rows(tm, 128)
    return _ew(
        "rope_tables", (s // tm,),
        [(pos_col, (tm, 1), lambda i: (i, 0)), (inv_row, (1, 128), lambda i: (0, 0))],
        [((s, 128), F32, blk, imap)] * 3, fn,
    )


def _rope(xh, cs, s_lo, s_hi):
    return xh * cs + pltpu.roll(xh, HEAD_DIM - ROPE_HALF, 1) * s_lo + pltpu.roll(xh, ROPE_HALF, 1) * s_hi


def _rope_t(gh, cs, s_lo, s_hi):
    return gh * cs + pltpu.roll(gh * s_lo, ROPE_HALF, 1) + pltpu.roll(gh * s_hi, HEAD_DIM - ROPE_HALF, 1)


def _attn_geometry(length, most=8):
    nb = length // ATTN_BLOCK
    gq = min(most, nb)
    assert nb % gq == 0
    return nb, gq, gq * ATTN_BLOCK, nb // gq


def _band_masks():
    qi = lax.broadcasted_iota(jnp.int32, (ATTN_BLOCK, ATTN_BLOCK), 0)
    kj = lax.broadcasted_iota(jnp.int32, (ATTN_BLOCK, ATTN_BLOCK), 1)
    return kj <= qi, kj >= qi


def _band_mask_pair():
    qi = lax.broadcasted_iota(jnp.int32, (ATTN_BLOCK, 2 * ATTN_BLOCK), 0)
    cj = lax.broadcasted_iota(jnp.int32, (ATTN_BLOCK, 2 * ATTN_BLOCK), 1)
    in_cur = cj >= ATTN_BLOCK
    band = jnp.logical_or(jnp.logical_and(in_cur, cj - ATTN_BLOCK <= qi),
                          jnp.logical_and(cj < ATTN_BLOCK, cj >= qi))
    return band, in_cur


def _attn_fwd(qv, kv, vv, dil, cols3=(0, 0, 0)):
    length = qv.shape[0]
    nb, gq, rows, ni = _attn_geometry(length, 16)

    def body(q_ref, kc_ref, kp_ref, vc_ref, vp_ref, o_ref, l_ref):
        i = pl.program_id(1)
        band, in_cur = _band_mask_pair()
        band_first = jnp.logical_and(band, jnp.logical_or(in_cur, i > 0))
        work = []
        for h in range(HEADS_PER_GROUP):
            cols = slice(h * HEAD_DIM, (h + 1) * HEAD_DIM)
            qh = q_ref[:, cols]
            k_all = jnp.concatenate([kp_ref[:, cols], kc_ref[:, cols]], axis=0)
            v_all = jnp.concatenate([vp_ref[:, cols], vc_ref[:, cols]], axis=0)
            for jj in range(gq):
                rws = slice(jj * ATTN_BLOCK, (jj + 1) * ATTN_BLOCK)
                two = slice(jj * ATTN_BLOCK, (jj + 2) * ATTN_BLOCK)
                work.append(dict(h=h, rws=rws, cols=cols, v=v_all[two], first=jj == 0, s=_dot(qh[rws], k_all[two], "nt")))
        for w in work:
            s = jnp.where(band_first if w["first"] else band, w["s"], NEG_BIG)
            m = jnp.max(s, axis=-1, keepdims=True)
            pexp = jnp.exp(s - m)
            w["den"] = jnp.sum(pexp, axis=-1, keepdims=True)
            w["p"] = pexp.astype(BF16)
            w["lse"] = m + jnp.log(w["den"])
        for w in work:
            o = _dot(w["p"], w["v"], "nn")
            o_ref[w["rws"], w["cols"]] = (o * (1.0 / w["den"])).astype(o_ref.dtype)
            l_ref[w["rws"], w["h"] * LSE_LANES:(w["h"] + 1) * LSE_LANES] = jnp.broadcast_to(w["lse"], (ATTN_BLOCK, LSE_LANES))

    def cur(c):
        return pl.BlockSpec((rows, GROUP_WIDTH), lambda r, i: (i, r + c))

    def prev(c):
        return pl.BlockSpec((ATTN_BLOCK, GROUP_WIDTH), lambda r, i: (jnp.maximum(i * gq - 1, 0), r + c))

    cq, ck, cv = cols3
    return pl.pallas_call(
        body, name=f"attn_fwd_d{dil}", grid=(dil, ni),
        in_specs=[cur(cq), cur(ck), prev(ck), cur(cv), prev(cv)],
        out_specs=[cur(0), pl.BlockSpec((rows, LSE_WIDTH), lambda r, i: (i, r))],
        out_shape=[jax.ShapeDtypeStruct((length, dil * GROUP_WIDTH), BF16),
                   jax.ShapeDtypeStruct((length, dil * LSE_WIDTH), F32)],
        compiler_params=_params(2),
    )(qv, kv, kv, vv, vv)


def _attn_bwd(qv, kv, vv, dov, ov, lv, dil, cols3=(0, 0, 0)):
    length = qv.shape[0]
    nb, gq, rows, ni = _attn_geometry(length)
    out_shape = (length, dil * GROUP_WIDTH)

    def body(qc_ref, qn_ref, kc_ref, kp_ref, vc_ref, vp_ref, doc_ref, don_ref, oc_ref, on_ref, lc_ref, ln_ref,
             dq_ref, dk_ref, dv_ref):
        i = pl.program_id(1)
        _, mask_p = _band_masks()
        band, in_cur = _band_mask_pair()
        band_first = jnp.logical_and(band, jnp.logical_or(in_cur, i > 0))
        has_next = i < ni - 1

        last = slice(gq * ATTN_BLOCK, (gq + 1) * ATTN_BLOCK)
        mask_next = jnp.logical_and(mask_p, has_next)

        def rows_of(jj):
            return slice(jj * ATTN_BLOCK, (jj + 1) * ATTN_BLOCK)

        def keys_of(jj):
            return slice(jj * ATTN_BLOCK, (jj + 2) * ATTN_BLOCK)

        heads = []
        for h in range(HEADS_PER_GROUP):
            cols = slice(h * HEAD_DIM, (h + 1) * HEAD_DIM)
            hd = dict(
                cols=cols, q_c=qc_ref[:, cols], q_n=qn_ref[:, cols],
                k_all=jnp.concatenate([kp_ref[:, cols], kc_ref[:, cols]], axis=0),
                v_all=jnp.concatenate([vp_ref[:, cols], vc_ref[:, cols]], axis=0),
                do_c=doc_ref[:, cols], do_n=don_ref[:, cols],
                l_c=lc_ref[:, h * LSE_LANES:h * LSE_LANES + 1], l_n=ln_ref[:, h * LSE_LANES:h * LSE_LANES + 1],
            )
            hd["dl_c"] = jnp.sum(hd["do_c"].astype(F32) * oc_ref[:, cols].astype(F32), axis=-1, keepdims=True)
            hd["dl_n"] = jnp.sum(hd["do_n"].astype(F32) * on_ref[:, cols].astype(F32), axis=-1, keepdims=True)
            hd["s"] = [_dot(hd["q_c"][rows_of(jj)], hd["k_all"][keys_of(jj)], "nt") for jj in range(gq)]
            hd["dp"] = [_dot(hd["do_c"][rows_of(jj)], hd["v_all"][keys_of(jj)], "nt") for jj in range(gq)]
            hd["s"].append(_dot(hd["q_n"], hd["k_all"][last], "nt"))
            hd["dp"].append(_dot(hd["do_n"], hd["v_all"][last], "nt"))
            heads.append(hd)
        for hd in heads:
            hd["p"], hd["ds"] = [], []
            for jj in range(gq + 1):
                if jj < gq:
                    mask, l_col, delta = (band_first if jj == 0 else band), hd["l_c"][rows_of(jj)], hd["dl_c"][rows_of(jj)]
                else:
                    mask, l_col, delta = mask_next, hd["l_n"], hd["dl_n"]
                p = jnp.where(mask, jnp.exp(hd["s"][jj] - l_col), 0.0)
                hd["p"].append(p.astype(BF16))
                hd["ds"].append((p * (hd["dp"][jj] - delta)).astype(BF16))
        for hd in heads:
            cols = hd["cols"]
            dk_blocks, dv_blocks = [None] * (gq + 1), [None] * (gq + 1)

            def add(lst, idx, val):
                lst[idx] = val if lst[idx] is None else lst[idx] + val

            for jj in range(gq):
                qb, dob = hd["q_c"][rows_of(jj)], hd["do_c"][rows_of(jj)]
                dq_ref[rows_of(jj), cols] = _dot(hd["ds"][jj], hd["k_all"][keys_of(jj)], "nn").astype(dq_ref.dtype)
                dk2 = _dot(hd["ds"][jj], qb, "tn")
                dv2 = _dot(hd["p"][jj], dob, "tn")
                add(dk_blocks, jj, dk2[:ATTN_BLOCK])
                add(dk_blocks, jj + 1, dk2[ATTN_BLOCK:])
                add(dv_blocks, jj, dv2[:ATTN_BLOCK])
                add(dv_blocks, jj + 1, dv2[ATTN_BLOCK:])
            add(dk_blocks, gq, _dot(hd["ds"][gq], hd["q_n"], "tn"))
            add(dv_blocks, gq, _dot(hd["p"][gq], hd["do_n"], "tn"))
            for jj in range(gq):
                dk_ref[rows_of(jj), cols] = dk_blocks[jj + 1].astype(dk_ref.dtype)
                dv_ref[rows_of(jj), cols] = dv_blocks[jj + 1].astype(dv_ref.dtype)

    def cur(c):
        return pl.BlockSpec((rows, GROUP_WIDTH), lambda r, i: (i, r + c))

    def prev(c):
        return pl.BlockSpec((ATTN_BLOCK, GROUP_WIDTH), lambda r, i: (jnp.maximum(i * gq - 1, 0), r + c))

    def nxt(c):
        return pl.BlockSpec((ATTN_BLOCK, GROUP_WIDTH), lambda r, i: (jnp.minimum((i + 1) * gq, nb - 1), r + c))

    cq, ck, cv = cols3
    lse_cur = pl.BlockSpec((rows, LSE_WIDTH), lambda r, i: (i, r))
    lse_next = pl.BlockSpec((ATTN_BLOCK, LSE_WIDTH), lambda r, i: (jnp.minimum((i + 1) * gq, nb - 1), r))
    return pl.pallas_call(
        body, name=f"attn_bwd_d{dil}", grid=(dil, ni),
        in_specs=[cur(cq), nxt(cq), cur(ck), prev(ck), cur(cv), prev(cv), cur(0), nxt(0), cur(0), nxt(0), lse_cur, lse_next],
        out_specs=[cur(0), cur(0), cur(0)],
        out_shape=[jax.ShapeDtypeStruct(out_shape, BF16)] * 3,
        compiler_params=_params(2),
    )(qv, qv, kv, kv, vv, vv, dov, dov, ov, ov, lv, lv)


DILATED = tuple((g, d) for g, d in enumerate(GROUP_DILATIONS) if d > 1)


def _spread(scr, slot, tile, out_ref, dil, col, width=GROUP_WIDTH):
    tm = tile.shape[0]
    buf = scr.at[slot]
    buf[...] = tile
    for r in range(dil):
        c0 = r * width + col
        out_ref[:, c0:c0 + HEAD_DIM] = buf[pl.ds(r, tm // dil, stride=dil), :].astype(out_ref.dtype)


def _collect(scr, slot, in_ref, dil, col, width=GROUP_WIDTH):
    tm = scr.shape[1]
    buf = scr.at[slot]
    for r in range(dil):
        c0 = r * width + col
        buf[pl.ds(r, tm // dil, stride=dil), :] = in_ref[:, c0:c0 + HEAD_DIM].astype(F32)
    return buf[...]


def _view_spec(tm, dil, width=GROUP_WIDTH):
    return pl.BlockSpec((tm // dil, dil * width), lambda i: (i, 0))


def _view_shape(s, dil, dtype, width=GROUP_WIDTH):
    return jax.ShapeDtypeStruct((s // dil, dil * width), dtype)


def _qkv_layout(z, tabs, tm):
    s = z.shape[0]
    scale = 1.0 / math.sqrt(HEAD_DIM)
    qkv_width = 3 * N_GROUPS * GROUP_WIDTH

    def body(z_ref, cs_ref, lo_ref, hi_ref, qk0_ref, *rest):
        views, scr = rest[:-1], rest[-1]
        tabs_ = (cs_ref[...], lo_ref[...], hi_ref[...])
        for part in range(3):
            for g, dil in enumerate(GROUP_DILATIONS):
                if part == 2 and dil == 1:
                    continue
                for h in range(HEADS_PER_GROUP):
                    col = part * N_GROUPS * GROUP_WIDTH + g * GROUP_WIDTH + h * HEAD_DIM
                    t = z_ref[:, col:col + HEAD_DIM].astype(F32)
                    if part < 2:
                        t = _rope(t, *tabs_)
                    if part == 0:
                        t = t * scale
                    if dil == 1:
                        c0 = part * GROUP_WIDTH + h * HEAD_DIM
                        qk0_ref[:, c0:c0 + HEAD_DIM] = t.astype(BF16)
                    else:
                        out = views[3 * [gg for gg, _ in DILATED].index(g) + part]
                        _spread(scr, h, t, out, dil, h * HEAD_DIM)

    row = lambda i: (i, 0)
    out_shape = [jax.ShapeDtypeStruct((s, 2 * GROUP_WIDTH), BF16)]
    out_specs = [pl.BlockSpec((tm, 2 * GROUP_WIDTH), row)]
    for _, dil in DILATED:
        out_shape += [_view_shape(s, dil, BF16)] * 3
        out_specs += [_view_spec(tm, dil)] * 3
    res = pl.pallas_call(
        body, name="qkv_layout", grid=(s // tm,),
        in_specs=[pl.BlockSpec((tm, qkv_width), row)] + [pl.BlockSpec((tm, HEAD_DIM), row)] * 3,
        out_specs=out_specs, out_shape=out_shape,
        scratch_shapes=[pltpu.VMEM((HEADS_PER_GROUP, tm, HEAD_DIM), F32)], compiler_params=_params(1),
    )(z, *tabs)
    return res[0], [tuple(res[1 + 3 * n:4 + 3 * n]) for n in range(len(DILATED))]


def _attn_merge(o0, l0, dilated, tm):
    s = o0.shape[0]
    n_d = len(DILATED)

    def body(*refs):
        o0_ref, l0_ref = refs[:2]
        in_views = refs[2:2 + 2 * n_d]
        attn_ref, lse_ref = refs[2 + 2 * n_d:4 + 2 * n_d]
        out_views = refs[4 + 2 * n_d:4 + 4 * n_d]
        scr = refs[-1]
        l_rows = [l0_ref[...]] + [_collect(scr, n, in_views[2 * n + 1], dil, 0, LSE_WIDTH) for n, (_, dil) in enumerate(DILATED)]
        lse_heads = []
        for h in range(HEADS_PER_GROUP):
            cols = slice(h * HEAD_DIM, (h + 1) * HEAD_DIM)
            os_ = [o0_ref[:, cols].astype(F32)]
            for n, (_, dil) in enumerate(DILATED):
                os_.append(_collect(scr, n_d + n, in_views[2 * n], dil, h * HEAD_DIM))
            ls_ = [lr[:, h * LSE_LANES:h * LSE_LANES + 1] for lr in l_rows]
            m = ls_[0]
            for l_ in ls_[1:]:
                m = jnp.maximum(m, l_)
            es = [jnp.exp(l_ - m) for l_ in ls_]
            den = es[0]
            num = es[0] * os_[0]
            for e, o in zip(es[1:], os_[1:]):
                den = den + e
                num = num + e * o
            attn = num * (1.0 / den)
            lse_heads.append(jnp.broadcast_to(m + jnp.log(den), (tm, LSE_LANES)))
            attn_ref[:, cols] = attn.astype(BF16)
            for n, (_, dil) in enumerate(DILATED):
                _spread(scr, 2 * n_d, attn, out_views[2 * n], dil, h * HEAD_DIM)
        lse = jnp.concatenate(lse_heads, axis=1)
        lse_ref[...] = lse
        for n, (_, dil) in enumerate(DILATED):
            _spread(scr, 2 * n_d, lse, out_views[2 * n + 1], dil, 0, LSE_WIDTH)

    row = lambda i: (i, 0)
    nat = pl.BlockSpec((tm, GROUP_WIDTH), row)
    nat_l = pl.BlockSpec((tm, LSE_WIDTH), row)
    in_specs = [nat, nat_l]
    args = [o0, l0]
    out_specs = [nat, nat_l]
    out_shape = [jax.ShapeDtypeStruct((s, GROUP_WIDTH), BF16), jax.ShapeDtypeStruct((s, LSE_WIDTH), F32)]
    for (_, dil), (ov, lv) in zip(DILATED, dilated):
        in_specs += [_view_spec(tm, dil), _view_spec(tm, dil, LSE_WIDTH)]
        args += [ov, lv]
        out_specs += [_view_spec(tm, dil), _view_spec(tm, dil, LSE_WIDTH)]
        out_shape += [_view_shape(s, dil, BF16), _view_shape(s, dil, F32, LSE_WIDTH)]
    res = pl.pallas_call(
        body, name="attn_merge", grid=(s // tm,), in_specs=in_specs, out_specs=out_specs, out_shape=out_shape,
        scratch_shapes=[pltpu.VMEM((2 * n_d + 1, tm, HEAD_DIM), F32)], compiler_params=_params(1),
    )(*args)
    return res[0], res[1], [tuple(res[2 + 2 * n:4 + 2 * n]) for n in range(n_d)]


def _to_views(a, tm):
    s = a.shape[0]

    def body(a_ref, *rest):
        outs, scr = rest[:-1], rest[-1]
        for h in range(HEADS_PER_GROUP):
            t = a_ref[:, h * HEAD_DIM:(h + 1) * HEAD_DIM].astype(F32)
            for n, (_, dil) in enumerate(DILATED):
                _spread(scr, n, t, outs[n], dil, h * HEAD_DIM)

    return pl.pallas_call(
        body, name="to_views", grid=(s // tm,), in_specs=[pl.BlockSpec((tm, GROUP_WIDTH), lambda i: (i, 0))],
        out_specs=[_view_spec(tm, dil) for _, dil in DILATED], out_shape=[_view_shape(s, dil, BF16) for _, dil in DILATED],
        scratch_shapes=[pltpu.VMEM((len(DILATED), tm, HEAD_DIM), F32)], compiler_params=_params(1),
    )(a)


def _dz_layout(grads, du, dga, dgs, tabs, tm, comm=None):
    s = du.shape[0]
    scale = 1.0 / math.sqrt(HEAD_DIM)
    n_steps = s // tm
    c_ins = list(comm["ins"]) if comm else []
    c_outs = list(comm["outs"]) if comm else []
    c_sems = list(comm["sems"]) if comm else []
    n_fixed = 3 * N_GROUPS + 6

    def body(*refs):
        g_refs = refs[:3 * N_GROUPS]
        du_ref, dga_ref, dgs_ref, cs_ref, lo_ref, hi_ref = refs[3 * N_GROUPS:n_fixed]
        comm_in = refs[n_fixed:n_fixed + len(c_ins)]
        dz_ref = refs[n_fixed + len(c_ins)]
        comm_out = refs[n_fixed + len(c_ins) + 1:n_fixed + len(c_ins) + 1 + len(c_outs)]
        scr = refs[n_fixed + len(c_ins) + 1 + len(c_outs)]
        sems = refs[n_fixed + len(c_ins) + 2 + len(c_outs):]
        if comm:
            @pl.when(pl.program_id(0) == 0)
            def _():
                comm["start"](comm_in, comm_out, sems)

            @pl.when(pl.program_id(0) == n_steps - 1)
            def _():
                comm["finish"](comm_in, comm_out, sems)

        tabs_ = (cs_ref[...], lo_ref[...], hi_ref[...])
        for part in range(3):
            for g, dil in enumerate(GROUP_DILATIONS):
                src = g_refs[3 * g + part]
                for h in range(HEADS_PER_GROUP):
                    if dil == 1:
                        t = src[:, h * HEAD_DIM:(h + 1) * HEAD_DIM].astype(F32)
                    else:
                        t = _collect(scr, h, src, dil, h * HEAD_DIM)
                    if part < 2:
                        t = _rope_t(t, *tabs_)
                    if part == 0:
                        t = t * scale
                    col = part * N_GROUPS * GROUP_WIDTH + g * GROUP_WIDTH + h * HEAD_DIM
                    dz_ref[:, col:col + HEAD_DIM] = t.astype(BF16)
        dz_ref[:, COL_U:COL_GA] = du_ref[...]
        dz_ref[:, COL_GA:COL_GS] = dga_ref[...]
        dz_ref[:, COL_GS:IN_WIDTH] = dgs_ref[...]

    row = lambda i: (i, 0)
    in_specs, args = [], []
    for (g, dil), trio in zip(enumerate(GROUP_DILATIONS), grads):
        in_specs += [pl.BlockSpec((tm, GROUP_WIDTH), row) if dil == 1 else _view_spec(tm, dil)] * 3
        args += list(trio)
    in_specs += [pl.BlockSpec((tm, SSM_WIDTH), row), pl.BlockSpec((tm, D_MODEL), row), pl.BlockSpec((tm, D_MODEL), row)]
    in_specs += [pl.BlockSpec((tm, HEAD_DIM), row)] * 3
    in_specs += [pl.BlockSpec(memory_space=pl.ANY)] * len(c_ins)
    res = pl.pallas_call(
        body, name="dz_layout", grid=(n_steps,), in_specs=in_specs,
        out_specs=[pl.BlockSpec((tm, IN_WIDTH), row)] + [pl.BlockSpec(memory_space=pl.ANY)] * len(c_outs),
        out_shape=[jax.ShapeDtypeStruct((s, IN_WIDTH), BF16)] + c_outs,
        scratch_shapes=[pltpu.VMEM((HEADS_PER_GROUP, tm, HEAD_DIM), F32)] + [pltpu.SemaphoreType.DMA((n,)) for n in c_sems],
        compiler_params=_params(1),
    )(*args, du, dga, dgs, *tabs, *c_ins)
    return res[0], list(res[1:])


def _discretise(a_re, a_im, log_dt, bt_re, bt_im):
    dt = jnp.exp(log_dt)
    mag = jnp.exp(a_re * dt)
    bar_re = mag * jnp.cos(a_im * dt)
    bar_im = mag * jnp.sin(a_im * dt)
    nr = bar_re - 1.0
    ni = bar_im
    den = a_re * a_re + a_im * a_im
    z_re = (nr * a_re + ni * a_im) / den
    z_im = (ni * a_re - nr * a_im) / den
    bb_re = z_re[:, None, :] * bt_re - z_im[:, None, :] * bt_im
    bb_im = z_re[:, None, :] * bt_im + z_im[:, None, :] * bt_re
    return bar_re, bar_im, bb_re, bb_im


def _ssm_prep(a_re, a_im, log_dt, bt_re, bt_im):
    def body(ar, ai, ld, br, bi, o_lr, o_li, o_br, o_bi):
        lr, li, bbr, bbi = _discretise(ar[...], ai[...], ld[...], br[...], bi[...])
        o_lr[...] = lr
        o_li[...] = li
        o_br[...] = bbr
        o_bi[...] = bbi

    sm = jax.ShapeDtypeStruct((SSM_GROUPS, SSM_STATE), F32)
    bg = jax.ShapeDtypeStruct((SSM_GROUPS, SSM_GROUP, SSM_STATE), F32)
    return pl.pallas_call(body, name="ssm_prep", out_shape=[sm, sm, bg, bg])(a_re, a_im, log_dt, bt_re, bt_im)


def _ssm_param_bwd(a_re, a_im, log_dt, bt_re, bt_im, d_lr, d_li, d_bbr, d_bbi):
    def body(ar, ai, ld, br, bi, g_lr, g_li, g_br, g_bi, o_ar, o_ai, o_ld, o_br, o_bi):
        _, vjp = jax.vjp(_discretise, ar[...], ai[...], ld[...], br[...], bi[...])
        d_ar, d_ai, d_ld, d_br, d_bi = vjp((g_lr[...], g_li[...], g_br[...], g_bi[...]))
        o_ar[...] = d_ar
        o_ai[...] = d_ai
        o_ld[...] = d_ld
        o_br[...] = d_br
        o_bi[...] = d_bi

    sm = jax.ShapeDtypeStruct((SSM_GROUPS, SSM_STATE), F32)
    col = jax.ShapeDtypeStruct((SSM_GROUPS, 1), F32)
    bg = jax.ShapeDtypeStruct((SSM_GROUPS, SSM_GROUP, SSM_STATE), F32)
    return pl.pallas_call(body, name="ssm_param_bwd", out_shape=[sm, sm, col, bg, bg])(
        a_re, a_im, log_dt, bt_re, bt_im, d_lr, d_li, d_bbr, d_bbi)


def _block_diag(t, rows_per, cols_per):
    t4 = t.reshape(SSM_SUPER, 8, rows_per, cols_per)
    eye = jnp.eye(8, dtype=t.dtype)
    return jnp.einsum("bgrc,gh->bgrhc", t4, eye).reshape(SSM_SUPER, 8 * rows_per, 8 * cols_per)


def _block_diag_t(dense, rows_per, cols_per):
    t = dense.reshape(SSM_SUPER, 8, rows_per, 8, cols_per)
    eye = jnp.eye(8, dtype=dense.dtype)
    return jnp.einsum("bgrhc,gh->bgrc", t, eye).reshape(SSM_GROUPS, rows_per, cols_per)


def _gelu(v):
    c = math.sqrt(2.0 / math.pi)
    return 0.5 * v * (1.0 + jnp.tanh(c * (v + 0.044715 * v * v * v)))


def _gelu_grad(v):
    c = math.sqrt(2.0 / math.pi)
    t = jnp.tanh(c * (v + 0.044715 * v * v * v))
    return 0.5 * (1.0 + t) + 0.5 * v * (1.0 - t * t) * c * (1.0 + 3.0 * 0.044715 * v * v)


SUB = 8


SCAN_STEPS = (1, 2, 4)
N_SCAN_TABLES = 2 + 2 * len(SCAN_STEPS)


def _scan_tables(tab_ref, lam_re, lam_im, reverse, conj):
    lr = lam_re
    li = -lam_im if conj else lam_im
    powers = [(lr, li)]
    for _ in range(SUB - 1):
        pr, pi = powers[-1]
        powers.append((pr * lr - pi * li, pr * li + pi * lr))
    row = lax.broadcasted_iota(jnp.int32, (SUB, N_STATE), 0)
    if reverse:
        row = SUB - 1 - row
    wide = lambda v: jnp.broadcast_to(v, (SUB, N_STATE))
    p_re = jnp.zeros((SUB, N_STATE), F32)
    p_im = jnp.zeros((SUB, N_STATE), F32)
    for j in range(SUB):
        p_re = jnp.where(row == j, wide(powers[j][0]), p_re)
        p_im = jnp.where(row == j, wide(powers[j][1]), p_im)
    tab_ref[0] = p_re
    tab_ref[1] = p_im
    for idx, k in enumerate(SCAN_STEPS):
        tab_ref[2 + 2 * idx] = jnp.where(row >= k, wide(powers[k - 1][0]), 0.0)
        tab_ref[3 + 2 * idx] = jnp.where(row >= k, wide(powers[k - 1][1]), 0.0)


def _scan_rows(g_re_ref, g_im_ref, tab_ref, carry, n_rows, reverse):
    last = 0 if reverse else SUB - 1

    def tile_step(tt, state):
        cr, ci = state
        t8 = (n_rows // SUB - 1 - tt) if reverse else tt
        start = pl.multiple_of(t8 * SUB, SUB)
        xr = g_re_ref[pl.ds(start, SUB), :]
        xi = g_im_ref[pl.ds(start, SUB), :]
        for idx, k in enumerate(SCAN_STEPS):
            mr = tab_ref[2 + 2 * idx]
            mi = tab_ref[3 + 2 * idx]
            shift = SUB - k if reverse else k
            sr = pltpu.roll(xr, shift, 0)
            si = pltpu.roll(xi, shift, 0)
            xr, xi = xr + (mr * sr - mi * si), xi + (mr * si + mi * sr)
        pr = tab_ref[0]
        pi = tab_ref[1]
        xr, xi = xr + (pr * cr - pi * ci), xi + (pr * ci + pi * cr)
        g_re_ref[pl.ds(start, SUB), :] = xr
        g_im_ref[pl.ds(start, SUB), :] = xi
        return (jnp.broadcast_to(xr[last:last + 1, :], (SUB, N_STATE)),
                jnp.broadcast_to(xi[last:last + 1, :], (SUB, N_STATE)))

    return lax.fori_loop(0, n_rows // SUB, tile_step, carry)


def _ssm_fwd(z, b_re, b_im, c_re, c_im, lam_re, lam_im, d_skip, chunk):
    s = z.shape[0]

    def body(u_ref, bre, bim, cre, cim, lre, lim, dsk, hre_ref, him_ref, ys_ref, yg_ref, car_re, car_im, tabs):
        i = pl.program_id(0)

        @pl.when(i == 0)
        def _():
            car_re[...] = jnp.zeros_like(car_re)
            car_im[...] = jnp.zeros_like(car_im)
            _scan_tables(tabs, lre[...], lim[...], False, False)

        u = u_ref[...]
        for b in range(SSM_SUPER):
            ub = u[:, b * 128:(b + 1) * 128]
            st = slice(b * 512, (b + 1) * 512)
            hre_ref[:, st] = _dot(ub, bre[b], "nn")
            him_ref[:, st] = _dot(ub, bim[b], "nn")
        sr, si = _scan_rows(hre_ref, him_ref, tabs, (car_re[...], car_im[...]), chunk, False)
        car_re[...] = sr
        car_im[...] = si
        uf = u.astype(F32)
        for b in range(SSM_SUPER):
            st = slice(b * 512, (b + 1) * 512)
            ch = slice(b * 128, (b + 1) * 128)
            y = _dot(hre_ref[:, st].astype(BF16), cre[b], "nn") - _dot(him_ref[:, st].astype(BF16), cim[b], "nn")
            y = y + dsk[:, ch] * uf[:, ch]
            ys_ref[:, ch] = y
            yg_ref[:, ch] = _gelu(y).astype(BF16)

    full3 = lambda i: (0, 0, 0)
    full2 = lambda i: (0, 0)
    row = lambda i: (i, 0)
    u_col = COL_U // SSM_WIDTH
    return pl.pallas_call(
        body, name="ssm_fwd", grid=(s // chunk,),
        in_specs=[pl.BlockSpec((chunk, SSM_WIDTH), lambda i: (i, u_col)),
                  pl.BlockSpec((SSM_SUPER, 128, 512), full3), pl.BlockSpec((SSM_SUPER, 128, 512), full3),
                  pl.BlockSpec((SSM_SUPER, 512, 128), full3), pl.BlockSpec((SSM_SUPER, 512, 128), full3),
                  pl.BlockSpec((1, N_STATE), full2), pl.BlockSpec((1, N_STATE), full2), pl.BlockSpec((1, SSM_WIDTH), full2)],
        out_specs=[pl.BlockSpec((chunk, N_STATE), row), pl.BlockSpec((chunk, N_STATE), row),
                   pl.BlockSpec((chunk, SSM_WIDTH), row), pl.BlockSpec((chunk, SSM_WIDTH), row)],
        out_shape=[jax.ShapeDtypeStruct((s, N_STATE), F32), jax.ShapeDtypeStruct((s, N_STATE), F32),
                   jax.ShapeDtypeStruct((s, SSM_WIDTH), F32), jax.ShapeDtypeStruct((s, SSM_WIDTH), BF16)],
        scratch_shapes=[pltpu.VMEM((SUB, N_STATE), F32), pltpu.VMEM((SUB, N_STATE), F32),
                        pltpu.VMEM((N_SCAN_TABLES, SUB, N_STATE), F32)],
        compiler_params=_params(1),
    )(z, b_re, b_im, c_re, c_im, lam_re, lam_im, d_skip)


def _ssm_bwd(dys, z, h_re, h_im, b_re, b_im, c_re, c_im, lam_re, lam_im, d_skip, chunk):
    s = z.shape[0]
    n_chunks = s // chunk

    def body(dy_ref, u_ref, hre_ref, him_ref, hpr_ref, hpi_ref, bre, bim, cre, cim, lre, lim, dsk,
             du_ref, dlr_ref, dli_ref, dbr_ref, dbi_ref, dcr_ref, dci_ref, dd_ref, are, aim, car_re, car_im, tabs):
        i = pl.program_id(0)
        n = n_chunks - 1 - i

        @pl.when(i == 0)
        def _():
            car_re[...] = jnp.zeros_like(car_re)
            car_im[...] = jnp.zeros_like(car_im)
            _scan_tables(tabs, lre[...], lim[...], True, True)
            for r in (dlr_ref, dli_ref, dbr_ref, dbi_ref, dcr_ref, dci_ref, dd_ref):
                r[...] = jnp.zeros_like(r)

        dy = dy_ref[...]
        dyb = dy.astype(BF16)
        u = u_ref[...]
        for b in range(SSM_SUPER):
            ch = slice(b * 128, (b + 1) * 128)
            st = slice(b * 512, (b + 1) * 512)
            are[:, st] = _dot(dyb[:, ch], cre[b], "nt")
            aim[:, st] = -_dot(dyb[:, ch], cim[b], "nt")
        sr, si = _scan_rows(are, aim, tabs, (car_re[...], car_im[...]), chunk, True)
        car_re[...] = sr
        car_im[...] = si
        dd_ref[...] += jnp.sum(dy * u.astype(F32), axis=0, keepdims=True)
        row_id = lax.broadcasted_iota(jnp.int32, (chunk, 512), 0)
        top_scale = jnp.where(n > 0, 1.0, 0.0)
        for b in range(SSM_SUPER):
            ch = slice(b * 128, (b + 1) * 128)
            st = slice(b * 512, (b + 1) * 512)
            h_r = hre_ref[:, st]
            h_i = him_ref[:, st]
            hp_r = jnp.where(row_id == 0, hpr_ref[SUB - 1:SUB, st] * top_scale, pltpu.roll(h_r, 1, 0))
            hp_i = jnp.where(row_id == 0, hpi_ref[SUB - 1:SUB, st] * top_scale, pltpu.roll(h_i, 1, 0))
            a_r = are[:, st]
            a_i = aim[:, st]
            dlr_ref[:, st] += jnp.sum(a_r * hp_r + a_i * hp_i, axis=0, keepdims=True)
            dli_ref[:, st] += jnp.sum(a_i * hp_r - a_r * hp_i, axis=0, keepdims=True)
            a_rb = a_r.astype(BF16)
            a_ib = a_i.astype(BF16)
            dbr_ref[b] += _dot(u[:, ch], a_rb, "tn")
            dbi_ref[b] += _dot(u[:, ch], a_ib, "tn")
            dcr_ref[b] += _dot(dyb[:, ch], h_r.astype(BF16), "tn")
            dci_ref[b] += -_dot(dyb[:, ch], h_i.astype(BF16), "tn")
            du = _dot(a_rb, bre[b], "nt") + _dot(a_ib, bim[b], "nt") + dsk[:, ch] * dy[:, ch]
            du_ref[:, ch] = du.astype(du_ref.dtype)

    full3 = lambda i: (0, 0, 0)
    full2 = lambda i: (0, 0)
    rev = lambda i: (n_chunks - 1 - i, 0)
    above = lambda i: (jnp.maximum((n_chunks - 1 - i) * (chunk // SUB) - 1, 0), 0)
    u_col = COL_U // SSM_WIDTH
    b_spec = pl.BlockSpec((SSM_SUPER, 128, 512), full3)
    c_spec = pl.BlockSpec((SSM_SUPER, 512, 128), full3)
    vec = pl.BlockSpec((1, N_STATE), full2)
    return pl.pallas_call(
        body, name="ssm_bwd", grid=(n_chunks,),
        in_specs=[pl.BlockSpec((chunk, SSM_WIDTH), rev),
                  pl.BlockSpec((chunk, SSM_WIDTH), lambda i: (n_chunks - 1 - i, u_col)),
                  pl.BlockSpec((chunk, N_STATE), rev), pl.BlockSpec((chunk, N_STATE), rev),
                  pl.BlockSpec((SUB, N_STATE), above), pl.BlockSpec((SUB, N_STATE), above),
                  b_spec, b_spec, c_spec, c_spec, vec, vec, pl.BlockSpec((1, SSM_WIDTH), full2)],
        out_specs=[pl.BlockSpec((chunk, SSM_WIDTH), rev), vec, vec, b_spec, b_spec, b_spec, b_spec,
                   pl.BlockSpec((1, SSM_WIDTH), full2)],
        out_shape=[jax.ShapeDtypeStruct((s, SSM_WIDTH), BF16),
                   jax.ShapeDtypeStruct((1, N_STATE), F32), jax.ShapeDtypeStruct((1, N_STATE), F32)]
        + [jax.ShapeDtypeStruct((SSM_SUPER, 128, 512), F32)] * 4 + [jax.ShapeDtypeStruct((1, SSM_WIDTH), F32)],
        scratch_shapes=[pltpu.VMEM((chunk, N_STATE), F32), pltpu.VMEM((chunk, N_STATE), F32),
                        pltpu.VMEM((SUB, N_STATE), F32), pltpu.VMEM((SUB, N_STATE), F32),
                        pltpu.VMEM((N_SCAN_TABLES, SUB, N_STATE), F32)],
        compiler_params=_params(1),
    )(dys, z, h_re, h_im, h_re, h_im, b_re, b_im, c_re, c_im, lam_re, lam_im, d_skip)


def _local_step(x, p, pos, tgt, sm, w_in_own, w_in_buf, late_bufs, place):
    s = x.shape[0]
    tm = min(512, s)
    ts = min(2048, s)
    chunk = min(512, s)
    ni = s // tm
    nk = s // ts
    g_mix, g_ffn, g_final = sm["g_mix"], sm["g_ffn"], sm["g_final"]

    tmb = min(1024, s)
    nib = s // tmb
    chip_w = IN_WIDTH // N_CHIPS
    w_start, w_forward, w_finish = _gather_stages(1)
    gather_in = dict(ins=[w_in_buf], outs=[jax.ShapeDtypeStruct(w_in_buf.shape, w_in_buf.dtype)], aliased=True,
                     sems=[3] * 4, stages=[(0, w_start), (nib - 1, w_forward), (nib - 1, w_finish)])
    a_rows = lambda i, j, k, pv: (i, 0)
    z_own, n1, w_in_all = _mm("in_proj_own", (nib, 1, 1),
                              [(x, (tmb, D_MODEL), a_rows, w_in_own, (D_MODEL, chip_w), lambda i, j, k, pv: (0, 0))], "nn",
                              [((s, IN_WIDTH), BF16, (tmb, chip_w), lambda i, j, k, pv: (i, pv[P_CHIP])),
                               ((s, D_MODEL), BF16, (tmb, D_MODEL), a_rows)],
                              extras=[(g_mix, (1, D_MODEL), lambda i, j, k, pv: (0, 0))],
                              epilogue=lambda acc, g: ((acc,), ()), prologue=_rms_fwd_tile, comm=gather_in, place=place)
    w_in = w_in_all.reshape(N_CHIPS, D_MODEL, chip_w)
    n_late = len(late_bufs)
    g_start, g_forward, g_finish = _gather_stages(n_late)
    in_steps = (N_CHIPS - 1) * nib
    gather = dict(ins=late_bufs, outs=[jax.ShapeDtypeStruct(b.shape, b.dtype) for b in late_bufs], aliased=True,
                  sems=[3 * n_late] * 4,
                  stages=[(0, g_start), ((4 * in_steps) // 5, g_forward), (in_steps - 1, g_finish)])
    other = lambda j, pv: (pv[P_CHIP] + 1 + j) % N_CHIPS
    z, *late = _mm("in_proj", (nib, N_CHIPS - 1, 1),
                   [(n1, (tmb, D_MODEL), a_rows, w_in, (None, D_MODEL, chip_w), lambda i, j, k, pv: (other(j, pv), 0, 0))],
                   "nn", [((s, IN_WIDTH), BF16, (tmb, chip_w), lambda i, j, k, pv: (i, other(j, pv)))], j_outer=True,
                   comm=gather, place=place, fill=z_own)
    w_ap, w_ga, w_gb, w_out, w_fg, w_fu, w_fd, w_pg, w_pp = (
        g.reshape(N_CHIPS, 2 * g.shape[2], g.shape[3]) for g in late)
    w_out2 = w_out.reshape(D_MODEL, D_MODEL)
    w_pg2 = w_pg.reshape(D_MODEL, D_MODEL)

    inv = ROPE_THETA ** (-jnp.arange(ROPE_HALF, dtype=F32) * 2.0 / ROPE_DIM)
    inv_row = jnp.concatenate([inv, inv, jnp.zeros((HEAD_DIM - ROPE_DIM,), F32)]).reshape(1, HEAD_DIM)
    tabs = _rope_tables(pos.astype(F32).reshape(s, 1), inv_row, tm)

    qk0, qkv_views = _qkv_layout(z, tabs, tm)
    v0_col = (2 * N_GROUPS * GROUP_WIDTH) // GROUP_WIDTH
    group_in = [((qk0, qk0, z), (0, 1, v0_col))] + [(trio, (0, 0, 0)) for trio in qkv_views]
    fwd_out = [_attn_fwd(*arrs, dil, cols3) for (arrs, cols3), dil in zip(group_in, GROUP_DILATIONS)]
    attn, lse, merged_views = _attn_merge(fwd_out[0][0], fwd_out[0][1], fwd_out[1:], tm)

    def chip_cols(parts):
        return (jnp.concatenate(parts, axis=1),), ()

    def proj_cols(name, a, width, w):
        blk = (None, width, 256)
        pairs = [(a, (tmb, width), lambda i, j, k: (i, 0), w, blk, lambda i, j, k: (0, 0, 0))]
        pairs += [(None, None, None, w, blk, (lambda i, j, k, q=q: (q, 0, 0))) for q in range(1, N_CHIPS)]
        return _mm(name, (nib, 1, 1), pairs, "nn", [((s, D_MODEL), BF16, (tmb, D_MODEL), lambda i, j, k: (i, 0))],
                   epilogue=chip_cols, sum_pairs=False)[0]

    def proj512(name, a, w):
        return proj_cols(name, a, GROUP_WIDTH, w)

    attn_d = proj512("attn_proj", attn, w_ap)

    bt_re = jnp.transpose(sm["b_re"], (0, 2, 1))
    bt_im = jnp.transpose(sm["b_im"], (0, 2, 1))
    log_dt_col = sm["log_dt"].reshape(SSM_GROUPS, 1)
    lam_re, lam_im, bbt_re, bbt_im = _ssm_prep(sm["a_re"], sm["a_im"], log_dt_col, bt_re, bt_im)
    b_re_m = _block_diag(bbt_re, SSM_GROUP, SSM_STATE).astype(BF16)
    b_im_m = _block_diag(bbt_im, SSM_GROUP, SSM_STATE).astype(BF16)
    c_re_m = _block_diag(jnp.transpose(sm["c_re"], (0, 2, 1)), SSM_STATE, SSM_GROUP).astype(BF16)
    c_im_m = _block_diag(jnp.transpose(sm["c_im"], (0, 2, 1)), SSM_STATE, SSM_GROUP).astype(BF16)
    lam_re_row = lam_re.reshape(1, N_STATE)
    lam_im_row = lam_im.reshape(1, N_STATE)
    d_skip_row = sm["d_skip"].reshape(1, SSM_WIDTH)
    h_re, h_im, ys, yg = _ssm_fwd(z, b_re_m, b_im_m, c_re_m, c_im_m, lam_re_row, lam_im_row, d_skip_row, chunk)

    pa = proj512("glu_a", yg, w_ga)
    pb = proj512("glu_b", yg, w_gb)

    def mix_pro(ad, xr, g, ga, gs, a, b):
        ga, gs, ad, a, b = (t.astype(F32) for t in (ga, gs, ad, a, b))
        return _sig(ga) * ad + _sig(gs) * (a * _sig(b))

    def out_epi(acc, xr, g, *_):
        h1 = acc + xr
        return (h1, _rms_fwd_tile(h1, g)), ()

    m3 = lambda i, j, k: (i, 0)
    w3 = lambda i, j, k: (0, 0)
    tile_d = (tm, D_MODEL)
    h1, n2, mix = _mm("out_proj", (ni, 1, 1), [(attn_d, tile_d, m3, w_out2, (D_MODEL, D_MODEL), w3)], "nn",
                      [((s, D_MODEL), F32, tile_d, m3), ((s, D_MODEL), BF16, tile_d, m3), ((s, D_MODEL), BF16, tile_d, m3)],
                      epilogue=out_epi, prologue=mix_pro,
                      extras=[(x, tile_d, m3), (g_ffn, (1, D_MODEL), w3),
                              (z, tile_d, lambda i, j, k: (i, COL_GA // D_MODEL)), (z, tile_d, lambda i, j, k: (i, COL_GS // D_MODEL)),
                              (pa, tile_d, m3), (pb, tile_d, m3)])

    ffq = (None, tm, D_FF_Q)
    ffq_map = lambda i, j, k: (j, i, 0)

    def ffn_in_epi(parts):
        gts, ups = parts[0::2], parts[1::2]
        acts = [gt * _sig(gt) * u_ for gt, u_ in zip(gts, ups)]
        return (jnp.stack(gts, axis=0), jnp.stack(ups, axis=0), jnp.stack(acts, axis=0)), ()

    w_ffq = (None, D_MODEL, D_FF_Q)
    ff_pairs = []
    for q in range(N_CHIPS):
        blk_q = lambda i, j, k, q=q: (q, 0, 0)
        ff_pairs.append((n2, (tm, D_MODEL), m3, w_fg, w_ffq, blk_q) if q == 0 else (None, None, None, w_fg, w_ffq, blk_q))
        ff_pairs.append((None, None, None, w_fu, w_ffq, blk_q))
    ff_all = (N_CHIPS, tm, D_FF_Q)
    ff_all_map = lambda i, j, k: (0, i, 0)
    gate, up, act = _mm("ffn_gate_up", (ni, 1, 1), ff_pairs, "nn",
                        [((N_CHIPS, s, D_FF_Q), BF16, ff_all, ff_all_map)] * 3, epilogue=ffn_in_epi,
                        sum_pairs=False, resident_b=True)

    (h2,) = _mm("ffn_down", (nib, 1, 1),
                [(act, (None, tmb, D_FF_Q), (lambda i, j, k, q=q: (q, i, 0)), w_fd, (None, D_FF_Q, D_MODEL),
                  (lambda i, j, k, q=q: (q, 0, 0))) for q in range(N_CHIPS)], "nn",
                [((s, D_MODEL), F32, (tmb, D_MODEL), m3)], epilogue=lambda acc, hr: ((acc + hr,), ()),
                extras=[(h1, (tmb, D_MODEL), m3)])

    pp = proj_cols("ple_proj", p, PLE_DIM, w_pp)

    def ple_head_epi(acc, hr, ppr, t, g):
        sg = _sig(acc)
        ppf = ppr.astype(F32)
        h = hr + sg * ppf
        r = lax.rsqrt(jnp.mean(h * h, axis=-1, keepdims=True) + EPS)
        hhat = h * r
        diff = hhat * g - t
        loss = 0.5 * jnp.sum(jnp.mean(diff * diff, axis=-1, keepdims=True))
        dy = diff * (1.0 / D_MODEL)
        gy = dy * g
        dh = r * (gy - hhat * jnp.mean(gy * hhat, axis=-1, keepdims=True))
        return ((dh, dh * ppf * sg * (1.0 - sg), dh * sg),
                (jnp.full((SUB, 128), loss, F32), jnp.sum(dy * hhat, axis=0, keepdims=True)))

    tile_row = (tm, D_MODEL)
    dh3, dgl, dpp, loss_acc, dg_final = _mm(
        "ple_gate_head", (ni, 1, 1), [(h2, tile_row, m3, w_pg2, (D_MODEL, D_MODEL), w3)], "nn",
        [((s, D_MODEL), F32, tile_row, m3), ((s, D_MODEL), BF16, tile_row, m3), ((s, D_MODEL), BF16, tile_row, m3)],
        epilogue=ple_head_epi,
        extras=[(h2, tile_row, m3), (pp, tile_row, m3), (tgt, tile_row, m3), (g_final, (1, D_MODEL), w3)],
        acc_outs=[((SUB, 128), F32), ((1, D_MODEL), F32)])

    def wgrad(name, a, a_block, a_imap, b, b_block, b_imap, out_shape, out_block, out_imap, nj, acc_shape):
        return _mm(name, (1, nj, nk), [(a, a_block, a_imap, b, b_block, b_imap)], "tn",
                   [(out_shape, F32, out_block, out_imap)], acc_shape=acc_shape)[0]

    tk0 = lambda i, j, k: (k, 0)
    tkj = lambda i, j, k: (k, j)
    def wgrad_cols(name, a, width, dy_):
        def split(acc):
            return (jnp.stack([acc[:, q * 256:(q + 1) * 256] for q in range(N_CHIPS)], axis=0),), ()

        return _mm(name, (1, 1, nk), [(a, (ts, width), tk0, dy_, (ts, D_MODEL), tk0)], "tn",
                   [((N_CHIPS, width, 256), F32, (N_CHIPS, width, 256), lambda i, j, k: (0, 0, 0))], epilogue=split,
                   acc_shape=(width, D_MODEL))[0]

    d_w_pp = wgrad_cols("d_ple_proj", p, PLE_DIM, dpp)
    d_w_pg = wgrad("d_ple_gate", h2, (ts, D_MODEL), tk0, dgl, (ts, D_MODEL), tk0, (D_MODEL, D_MODEL),
                   (D_MODEL, D_MODEL), w3, 1, (D_MODEL, D_MODEL))

    (dh2,) = _mm("ple_gate_bwd", (nib, 1, 1), [(dgl, (tmb, D_MODEL), m3, w_pg2, (D_MODEL, D_MODEL), w3)], "nt",
                 [((s, D_MODEL), F32, (tmb, D_MODEL), m3)], epilogue=lambda acc, d_: ((acc + d_,), ()),
                 extras=[(dh3, (tmb, D_MODEL), m3)])

    def ffn_bwd_epi(parts, gt_all, u_all):
        dgs_, dus_ = [], []
        for q, dact in enumerate(parts):
            gt, u_ = gt_all[q].astype(F32), u_all[q].astype(F32)
            sg = _sig(gt)
            dgs_.append(dact * u_ * (sg * (1.0 + gt * (1.0 - sg))))
            dus_.append(dact * gt * sg)
        return (jnp.stack(dgs_, axis=0), jnp.stack(dus_, axis=0)), ()

    fd_pairs = [((dh2, (tm, D_MODEL), m3) if q == 0 else (None, None, None))
                + (w_fd, (None, D_FF_Q, D_MODEL), (lambda i, j, k, q=q: (q, 0, 0))) for q in range(N_CHIPS)]
    dgate, dup = _mm("ffn_down_bwd", (ni, 1, 1), fd_pairs, "nt",
                     [((N_CHIPS, s, D_FF_Q), BF16, ff_all, ff_all_map)] * 2, epilogue=ffn_bwd_epi,
                     extras=[(gate, ff_all, ff_all_map), (up, ff_all, ff_all_map)], sum_pairs=False, resident_b=True)

    ffq_t = (None, ts, D_FF_Q)
    ffq_tmap = lambda i, j, k: (j, k, 0)
    blk_j = lambda i, j, k: (j, 0, 0)
    d_w_fd = wgrad("d_ffn_down", act, ffq_t, ffq_tmap, dh2, (ts, D_MODEL), tk0, (N_CHIPS, D_FF_Q, D_MODEL),
                   (None, D_FF_Q, D_MODEL), blk_j, N_CHIPS, (D_FF_Q, D_MODEL))
    d_w_fg = wgrad("d_ffn_gate", n2, (ts, D_MODEL), tk0, dgate, ffq_t, ffq_tmap, (N_CHIPS, D_MODEL, D_FF_Q),
                   (None, D_MODEL, D_FF_Q), blk_j, N_CHIPS, (D_MODEL, D_FF_Q))
    d_w_fu = wgrad("d_ffn_up", n2, (ts, D_MODEL), tk0, dup, ffq_t, ffq_tmap, (N_CHIPS, D_MODEL, D_FF_Q),
                   (None, D_MODEL, D_FF_Q), blk_j, N_CHIPS, (D_MODEL, D_FF_Q))

    def norm_bwd_epi(acc, h, d_res, g):
        dh, dg = _rms_bwd_tile(acc, h, g)
        return (d_res + dh,), (dg,)

    fi_pairs = []
    for q in range(N_CHIPS):
        a_q = lambda i, j, k, q=q: (q, i, 0)
        b_q = lambda i, j, k, q=q: (q, 0, 0)
        fi_pairs.append((dgate, ffq, a_q, w_fg, (None, D_MODEL, D_FF_Q), b_q))
        fi_pairs.append((dup, ffq, a_q, w_fu, (None, D_MODEL, D_FF_Q), b_q))
    dh1, dg_ffn = _mm("ffn_in_bwd", (ni, 1, 1), fi_pairs, "nt",
                      [((s, D_MODEL), F32, (tm, D_MODEL), m3)], epilogue=norm_bwd_epi,
                      extras=[(h1, (tm, D_MODEL), m3), (dh2, (tm, D_MODEL), m3), (g_ffn, (1, D_MODEL), w3)],
                      acc_outs=[((1, D_MODEL), F32)], resident_b=True)

    d_w_out = wgrad("d_out_proj", mix, (ts, D_MODEL), tk0, dh1, (ts, D_MODEL), tk0, (D_MODEL, D_MODEL),
                    (D_MODEL, D_MODEL), w3, 1, (D_MODEL, D_MODEL))

    def mix_bwd_epi(dm, ga, gs, ad, a, b):
        ga, gs, ad, a, b = (t.astype(F32) for t in (ga, gs, ad, a, b))
        s_a, s_s, s_b = _sig(ga), _sig(gs), _sig(b)
        d_ssm = dm * s_s
        return (dm * ad * s_a * (1.0 - s_a), dm * (a * s_b) * s_s * (1.0 - s_s), dm * s_a, d_ssm * s_b,
                d_ssm * a * s_b * (1.0 - s_b)), ()

    tile_m = (tm, D_MODEL)
    dga, dgs, dattn_d, dpa, dpb = _mm(
        "out_proj_bwd", (ni, 1, 1), [(dh1, tile_m, m3, w_out2, (D_MODEL, D_MODEL), w3)], "nt",
        [((s, D_MODEL), BF16, tile_m, m3)] * 5, epilogue=mix_bwd_epi,
        extras=[(z, tile_m, lambda i, j, k: (i, COL_GA // D_MODEL)), (z, tile_m, lambda i, j, k: (i, COL_GS // D_MODEL)),
                (attn_d, tile_m, m3), (pa, tile_m, m3), (pb, tile_m, m3)])

    d_w_ap = wgrad_cols("d_attn_proj", attn, GROUP_WIDTH, dattn_d)
    d_w_ga = wgrad_cols("d_glu_a", yg, GROUP_WIDTH, dpa)
    d_w_gb = wgrad_cols("d_glu_b", yg, GROUP_WIDTH, dpb)

    ik = lambda i, j, k: (i, k)

    def cols_bwd(dy_, w):
        return [(dy_, (tmb, 256), (lambda i, j, k, q=q: (i, q)), w, (None, GROUP_WIDTH, 256),
                 (lambda i, j, k, q=q: (q, 0, 0))) for q in range(N_CHIPS)]

    (dattn,) = _mm("attn_proj_bwd", (nib, 1, 1), cols_bwd(dattn_d, w_ap), "nt",
                   [((s, GROUP_WIDTH), BF16, (tmb, GROUP_WIDTH), m3)])

    (dys,) = _mm("glu_bwd", (nib, 1, 1), cols_bwd(dpa, w_ga) + cols_bwd(dpb, w_gb), "nt",
                 [((s, GROUP_WIDTH), F32, (tmb, GROUP_WIDTH), m3)],
                 epilogue=lambda acc, y_: ((acc * _gelu_grad(y_),), ()),
                 extras=[(ys, (tmb, GROUP_WIDTH), m3)])

    du, d_lr, d_li, d_bre, d_bim, d_cre, d_cim, d_dskip = _ssm_bwd(
        dys, z, h_re, h_im, b_re_m, b_im_m, c_re_m, c_im_m, lam_re_row, lam_im_row, d_skip_row, chunk)

    dattn_views = _to_views(dattn, tm)
    bwd_in = [(dattn, attn, lse)] + [(dv_, ov_, lv_) for dv_, (ov_, lv_) in zip(dattn_views, merged_views)]
    qkv_grads = [_attn_bwd(*arrs, *dol, dil, cols3)
                 for (arrs, cols3), dol, dil in zip(group_in, bwd_in, GROUP_DILATIONS)]
    early = [d_w_ap, d_w_ga, d_w_gb, d_w_out.reshape(N_CHIPS, D_MODEL // N_CHIPS, D_MODEL), d_w_fg, d_w_fu, d_w_fd,
             d_w_pg.reshape(N_CHIPS, D_MODEL // N_CHIPS, D_MODEL), d_w_pp]
    early5 = [g.reshape(N_CHIPS, 2, g.shape[1] // 2, g.shape[2]) for g in early]
    n_e = len(early5)
    p_start, p_finish = _pair_exchange_stages(n_e)
    dz, early_theirs = _dz_layout(
        qkv_grads, du, dga, dgs, tabs, tm,
        comm=dict(ins=early5, outs=_pair_exchange_shapes(early5), sems=[N_CHIPS * n_e] * 2, start=p_start, finish=p_finish))
    early_parts = [_pair_sum(g, t, place) for g, t in zip(early5, early_theirs)]

    chip_in = IN_WIDTH // N_CHIPS
    ip_pairs = [(dz, (tm, chip_in), (lambda i, j, k, q=q: (i, q)), w_in, (None, D_MODEL, chip_in),
                 (lambda i, j, k, q=q: (q, 0, 0))) for q in range(N_CHIPS)]
    grad_x, dg_mix = _mm("in_proj_bwd", (ni, 1, 1), ip_pairs, "nt",
                         [((s, D_MODEL), F32, (tm, D_MODEL), m3)], epilogue=norm_bwd_epi,
                         extras=[(x, (tm, D_MODEL), m3), (dh1, (tm, D_MODEL), m3), (g_mix, (1, D_MODEL), w3)],
                         acc_outs=[((1, D_MODEL), F32)], resident_b=True)

    d_bbt_re = _block_diag_t(d_bre, SSM_GROUP, SSM_STATE)
    d_bbt_im = _block_diag_t(d_bim, SSM_GROUP, SSM_STATE)
    d_a_re, d_a_im, d_log_dt, d_bt_re, d_bt_im = _ssm_param_bwd(
        sm["a_re"], sm["a_im"], log_dt_col, bt_re, bt_im,
        d_lr.reshape(SSM_GROUPS, SSM_STATE), d_li.reshape(SSM_GROUPS, SSM_STATE), d_bbt_re, d_bbt_im)
    small = {
        "g_mix": dg_mix, "a_re": d_a_re, "a_im": d_a_im, "log_dt": d_log_dt,
        "b_re": jnp.transpose(d_bt_re, (0, 2, 1)), "b_im": jnp.transpose(d_bt_im, (0, 2, 1)),
        "c_re": _block_diag_t(d_cre, SSM_GROUP, SSM_STATE), "c_im": _block_diag_t(d_cim, SSM_GROUP, SSM_STATE),
        "d_skip": d_dskip, "g_ffn": dg_ffn, "g_final": dg_final,
    }
    vec = _pack([small[n] for n in SMALL] + [loss_acc[0, 0].reshape(1)])

    x_start, x_finish = _chip_exchange_stages(n_e)
    v_start, v_finish = _all_exchange_stages()

    def both(f_chips, f_vec):
        def stage(ins, outs, sems):
            f_chips(ins[:n_e], outs[:n_e], sems[:2])
            f_vec(ins[n_e:], outs[n_e:], sems[2:])
        return stage

    half_in = chip_in // 2
    win_steps = 8 * nk
    exchange = dict(ins=early_parts + [vec],
                    outs=[jax.ShapeDtypeStruct(t.shape, t.dtype) for t in early_parts]
                    + [jax.ShapeDtypeStruct((8,) + vec.shape, vec.dtype)],
                    aliased=False, sems=[3 * n_e, 3 * n_e, 7, 7],
                    stages=[(0, both(x_start, v_start)), (win_steps - 1, both(x_finish, v_finish))])
    d_w_in, *got = _mm("d_in_proj", (1, 8, nk), [(n1, (ts, D_MODEL), tk0, dz, (ts, half_in), tkj)], "tn",
                       [((N_CHIPS, D_MODEL, chip_in), F32, (None, D_MODEL, half_in), lambda i, j, k: (j // 2, 0, j % 2))],
                       acc_shape=(D_MODEL, half_in), comm=exchange)
    return grad_x, d_w_in, early_parts, got[:n_e], vec, got[n_e]


BIG = ("w_in", "w_attn_proj", "w_glu_a", "w_glu_b", "w_out", "w_ffn_gate", "w_ffn_up", "w_ffn_down", "w_ple_gate",
       "w_ple_proj")
SMALL = ("g_mix", "a_re", "a_im", "log_dt", "b_re", "b_im", "c_re", "c_im", "d_skip", "g_ffn", "g_final")
ANY = pl.BlockSpec(memory_space=pl.ANY)


def _place():
    x, y, c = lax.axis_index("x"), lax.axis_index("y"), lax.axis_index("c")
    chips = [(1 - x, y), (x, 1 - y), (1 - x, 1 - y)]
    return x, y, c, chips


def _remote(src, dst, send_sem, recv_sem, to):
    return pltpu.make_async_remote_copy(src_ref=src, dst_ref=dst, send_sem=send_sem, recv_sem=recv_sem, device_id=to,
                                        device_id_type=MESH)


def _comm_call(name, body, ins, out_shapes, n_sems, aliases=None):
    n_w = len(ins)
    return pl.pallas_call(
        body, name=name, in_specs=[ANY] * n_w, out_specs=[ANY] * len(out_shapes), out_shape=out_shapes,
        scratch_shapes=[pltpu.SemaphoreType.DMA((n,)) for n in n_sems], input_output_aliases=aliases or {},
    )(*ins)


def _gather_stages(n_w):
    def each():
        x, y, c, chips = _place()
        for w in range(n_w):
            for j, (cx, cy) in enumerate(chips):
                yield w, 3 * w + j, 2 * x + y, 2 * cx + cy, (cx, cy, c), (x, y, 1 - c), c

    def start(ins, outs, sems):
        for w, k, me, _, peer, _, c in each():
            mine = outs[w].at[me, c]
            _remote(mine, mine, sems[0].at[k], sems[1].at[k], peer).start()

    def forward(ins, outs, sems):
        for w, k, _, src_chip, peer, sib, c in each():
            landed = outs[w].at[src_chip, c]
            _remote(landed, landed, sems[0].at[k], sems[1].at[k], peer).wait_recv()
            _remote(landed, landed, sems[2].at[k], sems[3].at[k], sib).start()

    def finish(ins, outs, sems):
        for w, k, me, src_chip, peer, sib, c in each():
            other = outs[w].at[src_chip, 1 - c]
            _remote(other, other, sems[2].at[k], sems[3].at[k], sib).wait_recv()
        for w, k, me, src_chip, peer, sib, c in each():
            mine = outs[w].at[me, c]
            _remote(mine, mine, sems[0].at[k], sems[1].at[k], peer).wait_send()
            landed = outs[w].at[src_chip, c]
            _remote(landed, landed, sems[2].at[k], sems[3].at[k], sib).wait_send()

    return start, forward, finish


def _pair_exchange(grads):
    n_w = len(grads)
    start, finish = _pair_exchange_stages(n_w)

    def body(*refs):
        ins, outs, sems = refs[:n_w], refs[n_w:2 * n_w], refs[2 * n_w:]
        start(ins, outs, sems)
        finish(ins, outs, sems)

    return _comm_call("grad_pair_exchange", body, grads, _pair_exchange_shapes(grads), [N_CHIPS * n_w] * 2)


def _pair_exchange_shapes(grads):
    return [jax.ShapeDtypeStruct((N_CHIPS,) + g.shape[2:], g.dtype) for g in grads]


def _pair_exchange_stages(n_w):
    def each():
        x, y, c, _ = _place()
        for w in range(n_w):
            for q in range(N_CHIPS):
                yield w, q, N_CHIPS * w + q, c, (x, y, 1 - c)

    def start(ins, outs, sems):
        for w, q, k, c, sib in each():
            _remote(ins[w].at[q, 1 - c], outs[w].at[q], sems[0].at[k], sems[1].at[k], sib).start()

    def finish(ins, outs, sems):
        for w, q, k, c, sib in each():
            _remote(ins[w].at[q, 1 - c], outs[w].at[q], sems[0].at[k], sems[1].at[k], sib).wait()

    return start, finish


def _chip_exchange(parts):
    n_w = len(parts)

    start, finish = _chip_exchange_stages(n_w)

    def body(*refs):
        ins, outs, sems = refs[:n_w], refs[n_w:2 * n_w], refs[2 * n_w:]
        start(ins, outs, sems)
        finish(ins, outs, sems)

    out_shapes = [jax.ShapeDtypeStruct(t.shape, t.dtype) for t in parts]
    return _comm_call("grad_chip_exchange", body, parts, out_shapes, [3 * n_w, 3 * n_w])


def _chip_exchange_stages(n_w):
    def each():
        x, y, c, chips = _place()
        for w in range(n_w):
            for j, (cx, cy) in enumerate(chips):
                yield w, 3 * w + j, 2 * x + y, 2 * cx + cy, (cx, cy, c)

    def start(ins, outs, sems):
        for w, k, me, peer_chip, peer in each():
            _remote(ins[w].at[peer_chip], outs[w].at[me], sems[0].at[k], sems[1].at[k], peer).start()

    def finish(ins, outs, sems):
        for w, k, me, peer_chip, peer in each():
            got = outs[w].at[peer_chip]
            _remote(got, got, sems[0].at[k], sems[1].at[k], peer).wait_recv()
        for w, k, me, peer_chip, peer in each():
            _remote(ins[w].at[peer_chip], outs[w].at[me], sems[0].at[k], sems[1].at[k], peer).wait_send()

    return start, finish


def _pair_gather(halves):
    n_w = len(halves)

    def body(*refs):
        ins, outs = refs[:n_w], refs[n_w:2 * n_w]
        send, recv = refs[2 * n_w:]
        x, y, c, _ = _place()
        sib = (x, y, 1 - c)
        cps = []
        for w in range(n_w):
            cp = _remote(ins[w], outs[w], send.at[w], recv.at[w], sib)
            cp.start()
            cps.append(cp)
        for cp in cps:
            cp.wait()

    out_shapes = [jax.ShapeDtypeStruct(h.shape, h.dtype) for h in halves]
    return _comm_call("grad_pair_gather", body, halves, out_shapes, [n_w] * 2)


def _all_exchange_stages():
    def each():
        x, y, c, _ = _place()
        for k in range(1, 8):
            px, py, pc = x ^ ((k >> 2) & 1), y ^ ((k >> 1) & 1), c ^ (k & 1)
            yield k - 1, 4 * x + 2 * y + c, 4 * px + 2 * py + pc, (px, py, pc)

    def start(ins, outs, sems):
        for k, me, _, peer in each():
            _remote(ins[0], outs[0].at[me], sems[0].at[k], sems[1].at[k], peer).start()

    def finish(ins, outs, sems):
        for k, me, src, peer in each():
            got = outs[0].at[src]
            _remote(got, got, sems[0].at[k], sems[1].at[k], peer).wait_recv()
        for k, me, src, peer in each():
            _remote(ins[0], outs[0].at[me], sems[0].at[k], sems[1].at[k], peer).wait_send()

    return start, finish


def _row_tile(r):
    for t in (256, 128, 176, 64, 32, 16, 8):
        if r % t == 0:
            return t
    return r


P_C, P_CHIP, P_DEV = 2, 3, 4


def _cast_shard(w2):
    r, c = w2.shape
    t = _row_tile(r)
    blk, imap = _rows(t, c)
    return _ew("cast_own", (r // t,), [(w2, blk, imap)], [((r, c), BF16, blk, imap)], lambda pids, a: ((a,), ()))[0]


def _cast_into_slot(w2, place):
    r, c = w2.shape
    t = _row_tile(r)
    return _ew("cast_shard", (r // t,), [(w2, (t, c), lambda i, pv: (i, 0))],
               [((N_CHIPS, r, c), BF16, (None, t, c), lambda i, pv: (pv[P_CHIP], i, 0))],
               lambda pids, a: ((a,), ()), place=place)[0]


def _pair_sum(mine, theirs, place):
    _, r, c = theirs.shape
    t = _row_tile(r)
    own = ((None, None, t, c), lambda q, i, pv: (q, pv[P_C], i, 0))
    blk = ((None, t, c), lambda q, i, pv: (q, i, 0))
    return _ew("grad_pair_sum", (N_CHIPS, r // t), [(mine, *own), (theirs, *blk)], [((N_CHIPS, r, c), BF16, *blk)],
               lambda pids, a, b: ((a + b,), ()), place=place)[0]


def _chip_sum(own, got, place):
    _, r, c = own.shape
    t = _row_tile(r)
    ins = []
    for q in range(N_CHIPS):
        ins.append((own, (None, t, c), (lambda i, pv, q=q: (q, i, 0))))
        ins.append((got, (None, t, c), (lambda i, pv, q=q: (jnp.where(pv[P_CHIP] == q, (q + 1) % N_CHIPS, q), i, 0))))

    def fn(pids, *tiles):
        me = pids[0][P_CHIP]
        tot = None
        for q in range(N_CHIPS):
            term = jnp.where(me == q, tiles[2 * q], tiles[2 * q + 1]).astype(F32)
            tot = term if tot is None else tot + term
        return (tot,), ()

    return _ew("grad_chip_sum", (r // t,), ins, [((r, c), F32, (t, c), lambda i, pv: (i, 0))], fn, place=place)[0]


def _adamw_tile(w, g, m, v):
    m = ADAM_B1 * m + (1.0 - ADAM_B1) * g
    v = ADAM_B2 * v + (1.0 - ADAM_B2) * (g * g)
    m_hat = m / (1.0 - ADAM_B1 ** ADAM_STEP)
    v_hat = v / (1.0 - ADAM_B2 ** ADAM_STEP)
    delta = -ADAM_LR * (m_hat / (jnp.sqrt(v_hat) + ADAM_EPS) + ADAM_WD * w)
    return delta, m, v


def _adamw(name, g2, w2, m2, v2):
    r, c = w2.shape
    t = _row_tile(r)
    blk, imap = _rows(t, c)

    def fn(pids, g, w, m, v):
        delta, nm, nv = _adamw_tile(w, g, m, v)
        return (g, delta, nm, nv), ()

    return _ew(name, (r // t,), [(a, blk, imap) for a in (g2, w2, m2, v2)], [((r, c), F32, blk, imap)] * 4, fn)


def _adamw_halves(name, mine, theirs, w2, m2, v2, place):
    r, c = w2.shape
    t = _row_tile(r // 2)
    n_t = (r // 2) // t
    half = ((t, c), lambda h, i, pv: (i, 0))
    whole = ((t, c), lambda h, i, pv: (h * n_t + i, 0))

    def fn(pids, ga, gb, w, m, v):
        g = jnp.where(pids[1] == pids[0][P_C], ga, gb)
        delta, nm, nv = _adamw_tile(w, g, m, v)
        return (g, delta, nm, nv), ()

    return _ew(name, (2, n_t), [(mine, *half), (theirs, *half), (w2, *whole), (m2, *whole), (v2, *whole)],
               [((r, c), F32, *whole)] * 4, fn, place=place)


def _device_sum(own, got, place):
    r, c = own.shape
    t = _row_tile(r)
    ins = [(own, (t, c), lambda i, pv: (i, 0))]
    for q in range(8):
        ins.append((got, (None, t, c), (lambda i, pv, q=q: (jnp.where(pv[P_DEV] == q, (q + 1) % 8, q), i, 0))))

    def fn(pids, mine, *parts):
        me = pids[0][P_DEV]
        tot = None
        for q in range(8):
            term = jnp.where(me == q, mine, parts[q])
            tot = term if tot is None else tot + term
        return (tot,), ()

    return _ew("small_device_sum", (r // t,), ins, [((r, c), F32, (t, c), lambda i, pv: (i, 0))], fn, place=place)[0]


def _pack(parts):
    flat = jnp.concatenate([a.reshape(-1) for a in parts])
    pad = (-flat.shape[0]) % (SUB * 128)
    return jnp.pad(flat, (0, pad)).reshape(-1, 128)


def _unpack(mat, shapes):
    flat = mat.reshape(-1)
    out, off = [], 0
    for shp in shapes:
        n = math.prod(shp)
        out.append(flat[off:off + n].reshape(shp))
        off += n
    return out


def kernel(x, p, positions, g_mix, w_in, a_re, a_im, log_dt, b_re, b_im, c_re, c_im, d_skip, w_attn_proj, w_glu_a, w_glu_b, w_out, g_ffn, w_ffn_gate, w_ffn_up, w_ffn_down, w_ple_gate, w_ple_proj, g_final, loss_target, m_g_mix, m_w_in, m_a_re, m_a_im, m_log_dt, m_b_re, m_b_im, m_c_re, m_c_im, m_d_skip, m_w_attn_proj, m_w_glu_a, m_w_glu_b, m_w_out, m_g_ffn, m_w_ffn_gate, m_w_ffn_up, m_w_ffn_down, m_w_ple_gate, m_w_ple_proj, m_g_final, v_g_mix, v_w_in, v_a_re, v_a_im, v_log_dt, v_b_re, v_b_im, v_c_re, v_c_im, v_d_skip, v_w_attn_proj, v_w_glu_a, v_w_glu_b, v_w_out, v_g_ffn, v_w_ffn_gate, v_w_ffn_up, v_w_ffn_down, v_w_ple_gate, v_w_ple_proj, v_g_final):
    given = dict(locals())
    big_w = {n: given[n] for n in BIG}
    w_mats = {n: big_w[n].reshape(big_w[n].shape[1:]) for n in BIG}

    ax, ay, ac = lax.axis_index("x"), lax.axis_index("y"), lax.axis_index("c")
    place = jnp.stack([ax, ay, ac, 2 * ax + ay, 4 * ax + 2 * ay + ac]).astype(jnp.int32)

    bufs = []
    for n in BIG:
        r, c = w_mats[n].shape
        bufs.append(_cast_into_slot(w_mats[n], place).reshape(N_CHIPS, 2, r // 2, c))
    w_in_own = _cast_shard(w_mats["w_in"])

    sm = {
        "g_mix": g_mix.reshape(1, D_MODEL), "g_ffn": g_ffn.reshape(1, D_MODEL), "g_final": g_final.reshape(1, D_MODEL),
        "a_re": a_re[0], "a_im": a_im[0], "log_dt": log_dt[0], "b_re": b_re[0], "b_im": b_im[0], "c_re": c_re[0],
        "c_im": c_im[0], "d_skip": d_skip[0],
    }
    s = x.shape[1]
    grad_x, d_w_in, early_parts, early_got, vec, vec_got = _local_step(
        x[0], p[0, 0], positions[0], loss_target[0], sm, w_in_own, bufs[0], bufs[1:], place)

    r_in, c_in = w_mats["w_in"].shape
    g5_in = [d_w_in.reshape(N_CHIPS, 2, r_in // 2, c_in)]
    in_parts = [_pair_sum(g, t, place) for g, t in zip(g5_in, _pair_exchange(g5_in))]
    chip_parts = in_parts + list(early_parts)
    chip_got = list(_chip_exchange(in_parts)) + list(early_got)
    halves = [_chip_sum(own, got, place) for own, got in zip(chip_parts, chip_got)]
    other_halves = _pair_gather(halves)

    results = {}
    for n, mine, other in zip(BIG, halves, other_halves):
        r, c = w_mats[n].shape
        shp = big_w[n].shape
        outs = _adamw_halves("adamw_" + n, mine, other, w_mats[n], given["m_" + n].reshape(r, c),
                             given["v_" + n].reshape(r, c), place)
        results[n] = [o.reshape(shp) for o in outs]

    small_shapes = [given[n].shape for n in SMALL]
    tot = _device_sum(vec, vec_got, place)
    n_small = sum(math.prod(shp) for shp in small_shapes)
    loss = tot.reshape(-1)[n_small]
    w_s = _pack([given[n] for n in SMALL])
    m_s = _pack([given["m_" + n] for n in SMALL])
    v_s = _pack([given["v_" + n] for n in SMALL])
    rows_s = w_s.shape[0]
    g_s = tot.reshape(-1)[: rows_s * 128].reshape(rows_s, 128)
    outs_s = _adamw("adamw_small", g_s, w_s, m_s, v_s)
    for kind, mat in enumerate(outs_s):
        for n, arr in zip(SMALL, _unpack(mat, small_shapes)):
            results.setdefault(n, [None] * 4)[kind] = arr

    order = ("g_mix", "w_in", "a_re", "a_im", "log_dt", "b_re", "b_im", "c_re", "c_im", "d_skip", "w_attn_proj", "w_glu_a",
             "w_glu_b", "w_out", "g_ffn", "w_ffn_gate", "w_ffn_up", "w_ffn_down", "w_ple_gate", "w_ple_proj", "g_final")
    out = [loss, grad_x.reshape(1, s, D_MODEL)]
    for kind in range(4):
        out += [results[n][kind] for n in order]
    return tuple(out)
```

```python
import math

import jax
import jax.numpy as jnp
from jax import lax
from jax.experimental import pallas as pl
from jax.experimental.pallas import tpu as pltpu

F32 = jnp.float32
BF16 = jnp.bfloat16

D_MODEL = 1024
HEAD_DIM = 128
HEADS_PER_GROUP = 4
GROUP_WIDTH = HEADS_PER_GROUP * HEAD_DIM
GROUP_DILATIONS = (1, 4, 16)
N_GROUPS = len(GROUP_DILATIONS)
LSE_LANES = 32
LSE_WIDTH = HEADS_PER_GROUP * LSE_LANES
ATTN_BLOCK = 128
ROPE_DIM = 32
ROPE_HALF = 16
ROPE_THETA = 500000.0
SSM_WIDTH = 512
SSM_GROUPS = 32
SSM_GROUP = 16
SSM_STATE = 64
N_STATE = SSM_GROUPS * SSM_STATE
SSM_SUPER = 4
IN_WIDTH = 7168
COL_U = 4608
COL_GA = 5120
COL_GS = 6144
D_FF = 2816
N_CHIPS = 4
D_FF_Q = D_FF // N_CHIPS
PLE_DIM = 256
EPS = 1e-6
ADAM_LR = 0.001
ADAM_B1 = 0.9
ADAM_B2 = 0.999
ADAM_EPS = 1e-08
ADAM_WD = 0.01
ADAM_STEP = 10
NEG_BIG = -1e30
VMEM_LIMIT_BYTES = 56 * 1024 * 1024
MESH = pl.DeviceIdType.MESH

_DIMS = {
    "nn": (((1,), (0,)), ((), ())),
    "nt": (((1,), (1,)), ((), ())),
    "tn": (((0,), (0,)), ((), ())),
}


def _params(n_grid):
    return pltpu.CompilerParams(dimension_semantics=("arbitrary",) * n_grid, vmem_limit_bytes=VMEM_LIMIT_BYTES)


def _sig(v):
    return 0.5 * jnp.tanh(0.5 * v) + 0.5


def _dot(a, b, mode):
    return lax.dot_general(a, b, _DIMS[mode], preferred_element_type=F32)


def _mm(name, grid, pairs, mode, outs, epilogue=None, extras=(), acc_outs=(), acc_shape=None, j_outer=False,
        sum_pairs=True, resident_b=False, comm=None, place=None, fill=None, prologue=None):
    gi, gj, gk = grid
    n_p, n_e, n_o, n_a = len(pairs), len(extras), len(outs), len(acc_outs)
    assert not n_a or gj == 1
    assert sum_pairs or gk == 1
    run_grid = (gj, gi, gk) if j_outer else grid
    c_ins = list(comm["ins"]) if comm else []
    c_outs = list(comm["outs"]) if comm else []
    c_sems = list(comm["sems"]) if comm else []
    n_ci, n_co, n_cs = len(c_ins), len(c_outs), len(c_sems)
    n_s = 0 if place is None else 1
    n_fill = 0 if fill is None else 1

    def order(imap):
        if place is None:
            return (lambda j, i, k: imap(i, j, k)) if j_outer else imap
        return (lambda j, i, k, pv: imap(i, j, k, pv)) if j_outer else imap

    shared_a = [pr[0] is None for pr in pairs]
    n_in = 2 * n_p - sum(shared_a)

    def body(*refs):
        refs = refs[n_s:]
        pair_refs = list(refs[:n_in])
        extra_refs = refs[n_in: n_in + n_e]
        comm_in = refs[n_in + n_e: n_in + n_e + n_ci]
        at = n_in + n_e + n_ci + n_fill
        out_refs = refs[at: at + n_o]
        sum_refs = refs[at + n_o: at + n_o + n_a]
        comm_out = refs[at + n_o + n_a: at + n_o + n_a + n_co]
        scratch_refs = refs[at + n_o + n_a + n_co:]
        i = pl.program_id(1 if j_outer else 0)
        k = pl.program_id(2)
        if comm:
            step = (pl.program_id(0) * run_grid[1] + pl.program_id(1)) * run_grid[2] + pl.program_id(2)
            sems = scratch_refs[len(scratch_refs) - n_cs:]
            for at_step, stage in comm["stages"]:
                @pl.when(step == at_step)
                def _(stage=stage):
                    stage(comm_in, comm_out, sems)
        part = None if sum_pairs else []
        a = None
        for t in range(n_p):
            if not shared_a[t]:
                a = pair_refs.pop(0)[...]
                if prologue is not None and t == 0:
                    a = prologue(a, *[e[...] for e in extra_refs]).astype(BF16)
                    out_refs[n_o - 1][...] = a
                a = a.astype(BF16)
            b = pair_refs.pop(0)[...].astype(BF16)
            d = _dot(a, b, mode)
            if sum_pairs:
                part = d if part is None else part + d
            else:
                part.append(d)

        def finish(acc):
            tiles, sums = epilogue(acc, *[e[...] for e in extra_refs]) if epilogue is not None else ((acc,), ())
            for o_ref, tile in zip(out_refs, tiles):
                o_ref[...] = tile.astype(o_ref.dtype)
            if n_a:
                @pl.when(i == 0)
                def _():
                    for s_ref in sum_refs:
                        s_ref[...] = jnp.zeros_like(s_ref)

                for s_ref, s in zip(sum_refs, sums):
                    s_ref[...] += s

        if gk == 1:
            finish(part)
        else:
            acc_ref = scratch_refs[0]

            @pl.when(k == 0)
            def _():
                acc_ref[...] = part

            @pl.when(k > 0)
            def _():
                acc_ref[...] += part

            @pl.when(k == gk - 1)
            def _():
                finish(acc_ref[...])

    in_specs, args = [], []
    for a, a_block, a_imap, b, b_block, b_imap in pairs:
        if a is not None:
            in_specs.append(pl.BlockSpec(a_block, order(a_imap)))
            args.append(a)
        if resident_b:
            in_specs.append(pl.BlockSpec(b_block, order(b_imap), pipeline_mode=pl.Buffered(1)))
        else:
            in_specs.append(pl.BlockSpec(b_block, order(b_imap)))
        args.append(b)
    for e, e_block, e_imap in extras:
        in_specs.append(pl.BlockSpec(e_block, order(e_imap)))
        args.append(e)
    first_comm_in = len(args)
    for c_in in c_ins:
        in_specs.append(pl.BlockSpec(memory_space=pl.ANY))
        args.append(c_in)
    if n_fill:
        in_specs.append(pl.BlockSpec(memory_space=pl.ANY))
        args.append(fill)
    out_shape = [jax.ShapeDtypeStruct(shape, dtype) for shape, dtype, _, _ in outs]
    out_specs = [pl.BlockSpec(block, order(imap)) for _, _, block, imap in outs]
    for shape, dtype in acc_outs:
        out_shape.append(jax.ShapeDtypeStruct(shape, dtype))
        out_specs.append(pl.BlockSpec(shape, lambda *_: (0, 0)))
    first_comm_out = len(out_shape)
    for c_out in c_outs:
        out_shape.append(c_out)
        out_specs.append(pl.BlockSpec(memory_space=pl.ANY))
    aliases = {n_s + first_comm_in + n: first_comm_out + n for n in range(n_ci)} if comm and comm["aliased"] else {}
    if n_fill:
        aliases[n_s + len(args) - 1] = 0
    scratch = [pltpu.VMEM(acc_shape, F32)] if gk > 1 else []
    scratch += [pltpu.SemaphoreType.DMA((n,)) for n in c_sems]
    if n_s:
        spec = pltpu.PrefetchScalarGridSpec(num_scalar_prefetch=1, grid=run_grid, in_specs=in_specs, out_specs=out_specs,
                                            scratch_shapes=scratch)
        return pl.pallas_call(body, name=name, grid_spec=spec, out_shape=out_shape, compiler_params=_params(3),
                              input_output_aliases=aliases)(place, *args)
    return pl.pallas_call(
        body, name=name, grid=run_grid, in_specs=in_specs, out_specs=out_specs,
        out_shape=out_shape, scratch_shapes=scratch, compiler_params=_params(3), input_output_aliases=aliases,
    )(*args)


def _ew(name, grid, ins, outs, fn, acc_outs=(), place=None):
    n_i, n_o, n_a = len(ins), len(outs), len(acc_outs)
    ng = len(grid)
    n_s = 0 if place is None else 1

    def body(*refs):
        in_refs = refs[n_s: n_s + n_i]
        out_refs = refs[n_s + n_i: n_s + n_i + n_o]
        sum_refs = refs[n_s + n_i + n_o:]
        pids = tuple(pl.program_id(a) for a in range(ng))
        if n_s:
            pids = (refs[0],) + pids
        tiles, sums = fn(pids, *[r[...] for r in in_refs])
        for o_ref, tile in zip(out_refs, tiles):
            o_ref[...] = tile.astype(o_ref.dtype)
        if n_a:
            first = pids[0] == 0
            for p_ in pids[1:]:
                first = jnp.logical_and(first, p_ == 0)

            @pl.when(first)
            def _():
                for s_ref in sum_refs:
                    s_ref[...] = jnp.zeros_like(s_ref)

            for s_ref, s in zip(sum_refs, sums):
                s_ref[...] += s

    in_specs = [pl.BlockSpec(block, imap) for _, block, imap in ins]
    out_shape = [jax.ShapeDtypeStruct(shape, dtype) for shape, dtype, _, _ in outs]
    out_specs = [pl.BlockSpec(block, imap) for _, _, block, imap in outs]
    for shape, dtype in acc_outs:
        out_shape.append(jax.ShapeDtypeStruct(shape, dtype))
        out_specs.append(pl.BlockSpec(shape, lambda *_, nd=len(shape): (0,) * nd))
    arrays = [a for a, _, _ in ins]
    if n_s:
        assert not n_a
        spec = pltpu.PrefetchScalarGridSpec(num_scalar_prefetch=1, grid=grid, in_specs=in_specs, out_specs=out_specs)
        return pl.pallas_call(body, name=name, grid_spec=spec, out_shape=out_shape, compiler_params=_params(ng))(
            place, *arrays)
    return pl.pallas_call(
        body, name=name, grid=grid, in_specs=in_specs, out_specs=out_specs, out_shape=out_shape,
        compiler_params=_params(ng),
    )(*arrays)


def _rows(tm, width):
    return (tm, width), (lambda i: (i, 0))


def _rms_fwd_tile(h, g):
    r = lax.rsqrt(jnp.mean(h * h, axis=-1, keepdims=True) + EPS)
    return h * r * g


def _rms_bwd_tile(dn, h, g):
    r = lax.rsqrt(jnp.mean(h * h, axis=-1, keepdims=True) + EPS)
    hhat = h * r
    gy = dn * g
    dh = r * (gy - hhat * jnp.mean(gy * hhat, axis=-1, keepdims=True))
    dg = jnp.sum(dn * hhat, axis=0, keepdims=True)
    return dh, dg


def _rope_tables(pos_col, inv_row, tm):
    s = pos_col.shape[0]

    def fn(pids, pos, inv):
        ang = pos * inv
        lane = lax.broadcasted_iota(jnp.int32, ang.shape, 1)
        cs = jnp.where(lane < ROPE_DIM, jnp.cos(ang), 1.0)
        sn = jnp.sin(ang)
        s_lo = jnp.where(lane < ROPE_HALF, -sn, 0.0)
        s_hi = jnp.where(jnp.logical_and(lane >= ROPE_HALF, lane < ROPE_DIM), sn, 0.0)
        return (cs, s_lo, s_hi), ()

    blk, imap = _rows(tm, 128)
    return _ew(
        "rope_tables", (s // tm,),
        [(pos_col, (tm, 1), lambda i: (i, 0)), (inv_row, (1, 128), lambda i: (0, 0))],
        [((s, 128), F32, blk, imap)] * 3, fn,
    )


def _rope(xh, cs, s_lo, s_hi):
    return xh * cs + pltpu.roll(xh, HEAD_DIM - ROPE_HALF, 1) * s_lo + pltpu.roll(xh, ROPE_HALF, 1) * s_hi


def _rope_t(gh, cs, s_lo, s_hi):
    return gh * cs + pltpu.roll(gh * s_lo, ROPE_HALF, 1) + pltpu.roll(gh * s_hi, HEAD_DIM - ROPE_HALF, 1)


def _attn_geometry(length):
    nb = length // ATTN_BLOCK
    gq = min(8, nb)
    assert nb % gq == 0
    return nb, gq, gq * ATTN_BLOCK, nb // gq


def _band_masks():
    qi = lax.broadcasted_iota(jnp.int32, (ATTN_BLOCK, ATTN_BLOCK), 0)
    kj = lax.broadcasted_iota(jnp.int32, (ATTN_BLOCK, ATTN_BLOCK), 1)
    return kj <= qi, kj >= qi


def _band_mask_pair():
    qi = lax.broadcasted_iota(jnp.int32, (ATTN_BLOCK, 2 * ATTN_BLOCK), 0)
    cj = lax.broadcasted_iota(jnp.int32, (ATTN_BLOCK, 2 * ATTN_BLOCK), 1)
    in_cur = cj >= ATTN_BLOCK
    band = jnp.logical_or(jnp.logical_and(in_cur, cj - ATTN_BLOCK <= qi),
                          jnp.logical_and(cj < ATTN_BLOCK, cj >= qi))
    return band, in_cur


def _attn_fwd(qv, kv, vv, dil, cols3=(0, 0, 0)):
    length = qv.shape[0]
    nb, gq, rows, ni = _attn_geometry(length)

    def body(q_ref, kc_ref, kp_ref, vc_ref, vp_ref, o_ref, l_ref):
        i = pl.program_id(1)
        band, in_cur = _band_mask_pair()
        band_first = jnp.logical_and(band, jnp.logical_or(in_cur, i > 0))
        work = []
        for h in range(HEADS_PER_GROUP):
            cols = slice(h * HEAD_DIM, (h + 1) * HEAD_DIM)
            qh = q_ref[:, cols]
            k_all = jnp.concatenate([kp_ref[:, cols], kc_ref[:, cols]], axis=0)
            v_all = jnp.concatenate([vp_ref[:, cols], vc_ref[:, cols]], axis=0)
            for jj in range(gq):
                rws = slice(jj * ATTN_BLOCK, (jj + 1) * ATTN_BLOCK)
                two = slice(jj * ATTN_BLOCK, (jj + 2) * ATTN_BLOCK)
                work.append(dict(h=h, rws=rws, cols=cols, v=v_all[two], first=jj == 0, s=_dot(qh[rws], k_all[two], "nt")))
        for w in work:
            s = jnp.where(band_first if w["first"] else band, w["s"], NEG_BIG)
            m = jnp.max(s, axis=-1, keepdims=True)
            pexp = jnp.exp(s - m)
            w["den"] = jnp.sum(pexp, axis=-1, keepdims=True)
            w["p"] = pexp.astype(BF16)
            w["lse"] = m + jnp.log(w["den"])
        for w in work:
            o = _dot(w["p"], w["v"], "nn")
            o_ref[w["rws"], w["cols"]] = (o * (1.0 / w["den"])).astype(o_ref.dtype)
            l_ref[w["rws"], w["h"] * LSE_LANES:(w["h"] + 1) * LSE_LANES] = jnp.broadcast_to(w["lse"], (ATTN_BLOCK, LSE_LANES))

    def cur(c):
        return pl.BlockSpec((rows, GROUP_WIDTH), lambda r, i: (i, r + c))

    def prev(c):
        return pl.BlockSpec((ATTN_BLOCK, GROUP_WIDTH), lambda r, i: (jnp.maximum(i * gq - 1, 0), r + c))

    cq, ck, cv = cols3
    return pl.pallas_call(
        body, name=f"attn_fwd_d{dil}", grid=(dil, ni),
        in_specs=[cur(cq), cur(ck), prev(ck), cur(cv), prev(cv)],
        out_specs=[cur(0), pl.BlockSpec((rows, LSE_WIDTH), lambda r, i: (i, r))],
        out_shape=[jax.ShapeDtypeStruct((length, dil * GROUP_WIDTH), BF16),
                   jax.ShapeDtypeStruct((length, dil * LSE_WIDTH), F32)],
        compiler_params=_params(2),
    )(qv, kv, kv, vv, vv)


def _attn_bwd(qv, kv, vv, dov, ov, lv, dil, cols3=(0, 0, 0)):
    length = qv.shape[0]
    nb, gq, rows, ni = _attn_geometry(length)
    out_shape = (length, dil * GROUP_WIDTH)

    def body(qc_ref, qn_ref, kc_ref, kp_ref, vc_ref, vp_ref, doc_ref, don_ref, oc_ref, on_ref, lc_ref, ln_ref,
             dq_ref, dk_ref, dv_ref):
        i = pl.program_id(1)
        _, mask_p = _band_masks()
        band, in_cur = _band_mask_pair()
        band_first = jnp.logical_and(band, jnp.logical_or(in_cur, i > 0))
        has_next = i < ni - 1

        last = slice(gq * ATTN_BLOCK, (gq + 1) * ATTN_BLOCK)
        mask_next = jnp.logical_and(mask_p, has_next)

        def rows_of(jj):
            return slice(jj * ATTN_BLOCK, (jj + 1) * ATTN_BLOCK)

        def keys_of(jj):
            return slice(jj * ATTN_BLOCK, (jj + 2) * ATTN_BLOCK)

        heads = []
        for h in range(HEADS_PER_GROUP):
            cols = slice(h * HEAD_DIM, (h + 1) * HEAD_DIM)
            hd = dict(
                cols=cols, q_c=qc_ref[:, cols], q_n=qn_ref[:, cols],
                k_all=jnp.concatenate([kp_ref[:, cols], kc_ref[:, cols]], axis=0),
                v_all=jnp.concatenate([vp_ref[:, cols], vc_ref[:, cols]], axis=0),
                do_c=doc_ref[:, cols], do_n=don_ref[:, cols],
                l_c=lc_ref[:, h * LSE_LANES:h * LSE_LANES + 1], l_n=ln_ref[:, h * LSE_LANES:h * LSE_LANES + 1],
            )
            hd["dl_c"] = jnp.sum(hd["do_c"].astype(F32) * oc_ref[:, cols].astype(F32), axis=-1, keepdims=True)
            hd["dl_n"] = jnp.sum(hd["do_n"].astype(F32) * on_ref[:, cols].astype(F32), axis=-1, keepdims=True)
            hd["s"] = [_dot(hd["q_c"][rows_of(jj)], hd["k_all"][keys_of(jj)], "nt") for jj in range(gq)]
            hd["dp"] = [_dot(hd["do_c"][rows_of(jj)], hd["v_all"][keys_of(jj)], "nt") for jj in range(gq)]
            hd["s"].append(_dot(hd["q_n"], hd["k_all"][last], "nt"))
            hd["dp"].append(_dot(hd["do_n"], hd["v_all"][last], "nt"))
            heads.append(hd)
        for hd in heads:
            hd["p"], hd["ds"] = [], []
            for jj in range(gq + 1):
                if jj < gq:
                    mask, l_col, delta = (band_first if jj == 0 else band), hd["l_c"][rows_of(jj)], hd["dl_c"][rows_of(jj)]
                else:
                    mask, l_col, delta = mask_next, hd["l_n"], hd["dl_n"]
                p = jnp.where(mask, jnp.exp(hd["s"][jj] - l_col), 0.0)
                hd["p"].append(p.astype(BF16))
                hd["ds"].append((p * (hd["dp"][jj] - delta)).astype(BF16))
        for hd in heads:
            cols = hd["cols"]
            dk_blocks, dv_blocks = [None] * (gq + 1), [None] * (gq + 1)

            def add(lst, idx, val):
                lst[idx] = val if lst[idx] is None else lst[idx] + val

            for jj in range(gq):
                qb, dob = hd["q_c"][rows_of(jj)], hd["do_c"][rows_of(jj)]
                dq_ref[rows_of(jj), cols] = _dot(hd["ds"][jj], hd["k_all"][keys_of(jj)], "nn").astype(dq_ref.dtype)
                dk2 = _dot(hd["ds"][jj], qb, "tn")
                dv2 = _dot(hd["p"][jj], dob, "tn")
                add(dk_blocks, jj, dk2[:ATTN_BLOCK])
                add(dk_blocks, jj + 1, dk2[ATTN_BLOCK:])
                add(dv_blocks, jj, dv2[:ATTN_BLOCK])
                add(dv_blocks, jj + 1, dv2[ATTN_BLOCK:])
            add(dk_blocks, gq, _dot(hd["ds"][gq], hd["q_n"], "tn"))
            add(dv_blocks, gq, _dot(hd["p"][gq], hd["do_n"], "tn"))
            for jj in range(gq):
                dk_ref[rows_of(jj), cols] = dk_blocks[jj + 1].astype(dk_ref.dtype)
                dv_ref[rows_of(jj), cols] = dv_blocks[jj + 1].astype(dv_ref.dtype)

    def cur(c):
        return pl.BlockSpec((rows, GROUP_WIDTH), lambda r, i: (i, r + c))

    def prev(c):
        return pl.BlockSpec((ATTN_BLOCK, GROUP_WIDTH), lambda r, i: (jnp.maximum(i * gq - 1, 0), r + c))

    def nxt(c):
        return pl.BlockSpec((ATTN_BLOCK, GROUP_WIDTH), lambda r, i: (jnp.minimum((i + 1) * gq, nb - 1), r + c))

    cq, ck, cv = cols3
    lse_cur = pl.BlockSpec((rows, LSE_WIDTH), lambda r, i: (i, r))
    lse_next = pl.BlockSpec((ATTN_BLOCK, LSE_WIDTH), lambda r, i: (jnp.minimum((i + 1) * gq, nb - 1), r))
    return pl.pallas_call(
        body, name=f"attn_bwd_d{dil}", grid=(dil, ni),
        in_specs=[cur(cq), nxt(cq), cur(ck), prev(ck), cur(cv), prev(cv), cur(0), nxt(0), cur(0), nxt(0), lse_cur, lse_next],
        out_specs=[cur(0), cur(0), cur(0)],
        out_shape=[jax.ShapeDtypeStruct(out_shape, BF16)] * 3,
        compiler_params=_params(2),
    )(qv, qv, kv, kv, vv, vv, dov, dov, ov, ov, lv, lv)


DILATED = tuple((g, d) for g, d in enumerate(GROUP_DILATIONS) if d > 1)


def _spread(scr, slot, tile, out_ref, dil, col, width=GROUP_WIDTH):
    tm = tile.shape[0]
    buf = scr.at[slot]
    buf[...] = tile
    for r in range(dil):
        c0 = r * width + col
        out_ref[:, c0:c0 + HEAD_DIM] = buf[pl.ds(r, tm // dil, stride=dil), :].astype(out_ref.dtype)


def _collect(scr, slot, in_ref, dil, col, width=GROUP_WIDTH):
    tm = scr.shape[1]
    buf = scr.at[slot]
    for r in range(dil):
        c0 = r * width + col
        buf[pl.ds(r, tm // dil, stride=dil), :] = in_ref[:, c0:c0 + HEAD_DIM].astype(F32)
    return buf[...]


def _view_spec(tm, dil, width=GROUP_WIDTH):
    return pl.BlockSpec((tm // dil, dil * width), lambda i: (i, 0))


def _view_shape(s, dil, dtype, width=GROUP_WIDTH):
    return jax.ShapeDtypeStruct((s // dil, dil * width), dtype)


def _qkv_layout(z, tabs, tm):
    s = z.shape[0]
    scale = 1.0 / math.sqrt(HEAD_DIM)
    qkv_width = 3 * N_GROUPS * GROUP_WIDTH

    def body(z_ref, cs_ref, lo_ref, hi_ref, qk0_ref, *rest):
        views, scr = rest[:-1], rest[-1]
        tabs_ = (cs_ref[...], lo_ref[...], hi_ref[...])
        for part in range(3):
            for g, dil in enumerate(GROUP_DILATIONS):
                if part == 2 and dil == 1:
                    continue
                for h in range(HEADS_PER_GROUP):
                    col = part * N_GROUPS * GROUP_WIDTH + g * GROUP_WIDTH + h * HEAD_DIM
                    t = z_ref[:, col:col + HEAD_DIM].astype(F32)
                    if part < 2:
                        t = _rope(t, *tabs_)
                    if part == 0:
                        t = t * scale
                    if dil == 1:
                        c0 = part * GROUP_WIDTH + h * HEAD_DIM
                        qk0_ref[:, c0:c0 + HEAD_DIM] = t.astype(BF16)
                    else:
                        out = views[3 * [gg for gg, _ in DILATED].index(g) + part]
                        _spread(scr, h, t, out, dil, h * HEAD_DIM)

    row = lambda i: (i, 0)
    out_shape = [jax.ShapeDtypeStruct((s, 2 * GROUP_WIDTH), BF16)]
    out_specs = [pl.BlockSpec((tm, 2 * GROUP_WIDTH), row)]
    for _, dil in DILATED:
        out_shape += [_view_shape(s, dil, BF16)] * 3
        out_specs += [_view_spec(tm, dil)] * 3
    res = pl.pallas_call(
        body, name="qkv_layout", grid=(s // tm,),
        in_specs=[pl.BlockSpec((tm, qkv_width), row)] + [pl.BlockSpec((tm, HEAD_DIM), row)] * 3,
        out_specs=out_specs, out_shape=out_shape,
        scratch_shapes=[pltpu.VMEM((HEADS_PER_GROUP, tm, HEAD_DIM), F32)], compiler_params=_params(1),
    )(z, *tabs)
    return res[0], [tuple(res[1 + 3 * n:4 + 3 * n]) for n in range(len(DILATED))]


def _attn_merge(o0, l0, dilated, tm):
    s = o0.shape[0]
    n_d = len(DILATED)

    def body(*refs):
        o0_ref, l0_ref = refs[:2]
        in_views = refs[2:2 + 2 * n_d]
        attn_ref, lse_ref = refs[2 + 2 * n_d:4 + 2 * n_d]
        out_views = refs[4 + 2 * n_d:4 + 4 * n_d]
        scr = refs[-1]
        l_rows = [l0_ref[...]] + [_collect(scr, n, in_views[2 * n + 1], dil, 0, LSE_WIDTH) for n, (_, dil) in enumerate(DILATED)]
        lse_heads = []
        for h in range(HEADS_PER_GROUP):
            cols = slice(h * HEAD_DIM, (h + 1) * HEAD_DIM)
            os_ = [o0_ref[:, cols].astype(F32)]
            for n, (_, dil) in enumerate(DILATED):
                os_.append(_collect(scr, n_d + n, in_views[2 * n], dil, h * HEAD_DIM))
            ls_ = [lr[:, h * LSE_LANES:h * LSE_LANES + 1] for lr in l_rows]
            m = ls_[0]
            for l_ in ls_[1:]:
                m = jnp.maximum(m, l_)
            es = [jnp.exp(l_ - m) for l_ in ls_]
            den = es[0]
            num = es[0] * os_[0]
            for e, o in zip(es[1:], os_[1:]):
                den = den + e
                num = num + e * o
            attn = num * (1.0 / den)
            lse_heads.append(jnp.broadcast_to(m + jnp.log(den), (tm, LSE_LANES)))
            attn_ref[:, cols] = attn.astype(BF16)
            for n, (_, dil) in enumerate(DILATED):
                _spread(scr, 2 * n_d, attn, out_views[2 * n], dil, h * HEAD_DIM)
        lse = jnp.concatenate(lse_heads, axis=1)
        lse_ref[...] = lse
        for n, (_, dil) in enumerate(DILATED):
            _spread(scr, 2 * n_d, lse, out_views[2 * n + 1], dil, 0, LSE_WIDTH)

    row = lambda i: (i, 0)
    nat = pl.BlockSpec((tm, GROUP_WIDTH), row)
    nat_l = pl.BlockSpec((tm, LSE_WIDTH), row)
    in_specs = [nat, nat_l]
    args = [o0, l0]
    out_specs = [nat, nat_l]
    out_shape = [jax.ShapeDtypeStruct((s, GROUP_WIDTH), BF16), jax.ShapeDtypeStruct((s, LSE_WIDTH), F32)]
    for (_, dil), (ov, lv) in zip(DILATED, dilated):
        in_specs += [_view_spec(tm, dil), _view_spec(tm, dil, LSE_WIDTH)]
        args += [ov, lv]
        out_specs += [_view_spec(tm, dil), _view_spec(tm, dil, LSE_WIDTH)]
        out_shape += [_view_shape(s, dil, BF16), _view_shape(s, dil, F32, LSE_WIDTH)]
    res = pl.pallas_call(
        body, name="attn_merge", grid=(s // tm,), in_specs=in_specs, out_specs=out_specs, out_shape=out_shape,
        scratch_shapes=[pltpu.VMEM((2 * n_d + 1, tm, HEAD_DIM), F32)], compiler_params=_params(1),
    )(*args)
    return res[0], res[1], [tuple(res[2 + 2 * n:4 + 2 * n]) for n in range(n_d)]


def _to_views(a, tm):
    s = a.shape[0]

    def body(a_ref, *rest):
        outs, scr = rest[:-1], rest[-1]
        for h in range(HEADS_PER_GROUP):
            t = a_ref[:, h * HEAD_DIM:(h + 1) * HEAD_DIM].astype(F32)
            for n, (_, dil) in enumerate(DILATED):
                _spread(scr, n, t, outs[n], dil, h * HEAD_DIM)

    return pl.pallas_call(
        body, name="to_views", grid=(s // tm,), in_specs=[pl.BlockSpec((tm, GROUP_WIDTH), lambda i: (i, 0))],
        out_specs=[_view_spec(tm, dil) for _, dil in DILATED], out_shape=[_view_shape(s, dil, BF16) for _, dil in DILATED],
        scratch_shapes=[pltpu.VMEM((len(DILATED), tm, HEAD_DIM), F32)], compiler_params=_params(1),
    )(a)


def _dz_layout(grads, du, dga, dgs, tabs, tm, comm=None):
    s = du.shape[0]
    scale = 1.0 / math.sqrt(HEAD_DIM)
    n_steps = s // tm
    c_ins = list(comm["ins"]) if comm else []
    c_outs = list(comm["outs"]) if comm else []
    c_sems = list(comm["sems"]) if comm else []
    n_fixed = 3 * N_GROUPS + 6

    def body(*refs):
        g_refs = refs[:3 * N_GROUPS]
        du_ref, dga_ref, dgs_ref, cs_ref, lo_ref, hi_ref = refs[3 * N_GROUPS:n_fixed]
        comm_in = refs[n_fixed:n_fixed + len(c_ins)]
        dz_ref = refs[n_fixed + len(c_ins)]
        comm_out = refs[n_fixed + len(c_ins) + 1:n_fixed + len(c_ins) + 1 + len(c_outs)]
        scr = refs[n_fixed + len(c_ins) + 1 + len(c_outs)]
        sems = refs[n_fixed + len(c_ins) + 2 + len(c_outs):]
        if comm:
            @pl.when(pl.program_id(0) == 0)
            def _():
                comm["start"](comm_in, comm_out, sems)

            @pl.when(pl.program_id(0) == n_steps - 1)
            def _():
                comm["finish"](comm_in, comm_out, sems)

        tabs_ = (cs_ref[...], lo_ref[...], hi_ref[...])
        for part in range(3):
            for g, dil in enumerate(GROUP_DILATIONS):
                src = g_refs[3 * g + part]
                for h in range(HEADS_PER_GROUP):
                    if dil == 1:
                        t = src[:, h * HEAD_DIM:(h + 1) * HEAD_DIM].astype(F32)
                    else:
                        t = _collect(scr, h, src, dil, h * HEAD_DIM)
                    if part < 2:
                        t = _rope_t(t, *tabs_)
                    if part == 0:
                        t = t * scale
                    col = part * N_GROUPS * GROUP_WIDTH + g * GROUP_WIDTH + h * HEAD_DIM
                    dz_ref[:, col:col + HEAD_DIM] = t.astype(BF16)
        dz_ref[:, COL_U:COL_GA] = du_ref[...]
        dz_ref[:, COL_GA:COL_GS] = dga_ref[...]
        dz_ref[:, COL_GS:IN_WIDTH] = dgs_ref[...]

    row = lambda i: (i, 0)
    in_specs, args = [], []
    for (g, dil), trio in zip(enumerate(GROUP_DILATIONS), grads):
        in_specs += [pl.BlockSpec((tm, GROUP_WIDTH), row) if dil == 1 else _view_spec(tm, dil)] * 3
        args += list(trio)
    in_specs += [pl.BlockSpec((tm, SSM_WIDTH), row), pl.BlockSpec((tm, D_MODEL), row), pl.BlockSpec((tm, D_MODEL), row)]
    in_specs += [pl.BlockSpec((tm, HEAD_DIM), row)] * 3
    in_specs += [pl.BlockSpec(memory_space=pl.ANY)] * len(c_ins)
    res = pl.pallas_call(
        body, name="dz_layout", grid=(n_steps,), in_specs=in_specs,
        out_specs=[pl.BlockSpec((tm, IN_WIDTH), row)] + [pl.BlockSpec(memory_space=pl.ANY)] * len(c_outs),
        out_shape=[jax.ShapeDtypeStruct((s, IN_WIDTH), BF16)] + c_outs,
        scratch_shapes=[pltpu.VMEM((HEADS_PER_GROUP, tm, HEAD_DIM), F32)] + [pltpu.SemaphoreType.DMA((n,)) for n in c_sems],
        compiler_params=_params(1),
    )(*args, du, dga, dgs, *tabs, *c_ins)
    return res[0], list(res[1:])


def _discretise(a_re, a_im, log_dt, bt_re, bt_im):
    dt = jnp.exp(log_dt)
    mag = jnp.exp(a_re * dt)
    bar_re = mag * jnp.cos(a_im * dt)
    bar_im = mag * jnp.sin(a_im * dt)
    nr = bar_re - 1.0
    ni = bar_im
    den = a_re * a_re + a_im * a_im
    z_re = (nr * a_re + ni * a_im) / den
    z_im = (ni * a_re - nr * a_im) / den
    bb_re = z_re[:, None, :] * bt_re - z_im[:, None, :] * bt_im
    bb_im = z_re[:, None, :] * bt_im + z_im[:, None, :] * bt_re
    return bar_re, bar_im, bb_re, bb_im


def _ssm_prep(a_re, a_im, log_dt, bt_re, bt_im):
    def body(ar, ai, ld, br, bi, o_lr, o_li, o_br, o_bi):
        lr, li, bbr, bbi = _discretise(ar[...], ai[...], ld[...], br[...], bi[...])
        o_lr[...] = lr
        o_li[...] = li
        o_br[...] = bbr
        o_bi[...] = bbi

    sm = jax.ShapeDtypeStruct((SSM_GROUPS, SSM_STATE), F32)
    bg = jax.ShapeDtypeStruct((SSM_GROUPS, SSM_GROUP, SSM_STATE), F32)
    return pl.pallas_call(body, name="ssm_prep", out_shape=[sm, sm, bg, bg])(a_re, a_im, log_dt, bt_re, bt_im)


def _ssm_param_bwd(a_re, a_im, log_dt, bt_re, bt_im, d_lr, d_li, d_bbr, d_bbi):
    def body(ar, ai, ld, br, bi, g_lr, g_li, g_br, g_bi, o_ar, o_ai, o_ld, o_br, o_bi):
        _, vjp = jax.vjp(_discretise, ar[...], ai[...], ld[...], br[...], bi[...])
        d_ar, d_ai, d_ld, d_br, d_bi = vjp((g_lr[...], g_li[...], g_br[...], g_bi[...]))
        o_ar[...] = d_ar
        o_ai[...] = d_ai
        o_ld[...] = d_ld
        o_br[...] = d_br
        o_bi[...] = d_bi

    sm = jax.ShapeDtypeStruct((SSM_GROUPS, SSM_STATE), F32)
    col = jax.ShapeDtypeStruct((SSM_GROUPS, 1), F32)
    bg = jax.ShapeDtypeStruct((SSM_GROUPS, SSM_GROUP, SSM_STATE), F32)
    return pl.pallas_call(body, name="ssm_param_bwd", out_shape=[sm, sm, col, bg, bg])(
        a_re, a_im, log_dt, bt_re, bt_im, d_lr, d_li, d_bbr, d_bbi)


def _block_diag(t, rows_per, cols_per):
    t4 = t.reshape(SSM_SUPER, 8, rows_per, cols_per)
    eye = jnp.eye(8, dtype=t.dtype)
    return jnp.einsum("bgrc,gh->bgrhc", t4, eye).reshape(SSM_SUPER, 8 * rows_per, 8 * cols_per)


def _block_diag_t(dense, rows_per, cols_per):
    t = dense.reshape(SSM_SUPER, 8, rows_per, 8, cols_per)
    eye = jnp.eye(8, dtype=dense.dtype)
    return jnp.einsum("bgrhc,gh->bgrc", t, eye).reshape(SSM_GROUPS, rows_per, cols_per)


def _gelu(v):
    c = math.sqrt(2.0 / math.pi)
    return 0.5 * v * (1.0 + jnp.tanh(c * (v + 0.044715 * v * v * v)))


def _gelu_grad(v):
    c = math.sqrt(2.0 / math.pi)
    t = jnp.tanh(c * (v + 0.044715 * v * v * v))
    return 0.5 * (1.0 + t) + 0.5 * v * (1.0 - t * t) * c * (1.0 + 3.0 * 0.044715 * v * v)


SUB = 8


SCAN_STEPS = (1, 2, 4)
N_SCAN_TABLES = 2 + 2 * len(SCAN_STEPS)


def _scan_tables(tab_ref, lam_re, lam_im, reverse, conj):
    lr = lam_re
    li = -lam_im if conj else lam_im
    powers = [(lr, li)]
    for _ in range(SUB - 1):
        pr, pi = powers[-1]
        powers.append((pr * lr - pi * li, pr * li + pi * lr))
    row = lax.broadcasted_iota(jnp.int32, (SUB, N_STATE), 0)
    if reverse:
        row = SUB - 1 - row
    wide = lambda v: jnp.broadcast_to(v, (SUB, N_STATE))
    p_re = jnp.zeros((SUB, N_STATE), F32)
    p_im = jnp.zeros((SUB, N_STATE), F32)
    for j in range(SUB):
        p_re = jnp.where(row == j, wide(powers[j][0]), p_re)
        p_im = jnp.where(row == j, wide(powers[j][1]), p_im)
    tab_ref[0] = p_re
    tab_ref[1] = p_im
    for idx, k in enumerate(SCAN_STEPS):
        tab_ref[2 + 2 * idx] = jnp.where(row >= k, wide(powers[k - 1][0]), 0.0)
        tab_ref[3 + 2 * idx] = jnp.where(row >= k, wide(powers[k - 1][1]), 0.0)


def _scan_rows(g_re_ref, g_im_ref, tab_ref, carry, n_rows, reverse):
    last = 0 if reverse else SUB - 1

    def tile_step(tt, state):
        cr, ci = state
        t8 = (n_rows // SUB - 1 - tt) if reverse else tt
        start = pl.multiple_of(t8 * SUB, SUB)
        xr = g_re_ref[pl.ds(start, SUB), :]
        xi = g_im_ref[pl.ds(start, SUB), :]
        for idx, k in enumerate(SCAN_STEPS):
            mr = tab_ref[2 + 2 * idx]
            mi = tab_ref[3 + 2 * idx]
            shift = SUB - k if reverse else k
            sr = pltpu.roll(xr, shift, 0)
            si = pltpu.roll(xi, shift, 0)
            xr, xi = xr + (mr * sr - mi * si), xi + (mr * si + mi * sr)
        pr = tab_ref[0]
        pi = tab_ref[1]
        xr, xi = xr + (pr * cr - pi * ci), xi + (pr * ci + pi * cr)
        g_re_ref[pl.ds(start, SUB), :] = xr
        g_im_ref[pl.ds(start, SUB), :] = xi
        return (jnp.broadcast_to(xr[last:last + 1, :], (SUB, N_STATE)),
                jnp.broadcast_to(xi[last:last + 1, :], (SUB, N_STATE)))

    return lax.fori_loop(0, n_rows // SUB, tile_step, carry)


def _ssm_fwd(z, b_re, b_im, c_re, c_im, lam_re, lam_im, d_skip, chunk):
    s = z.shape[0]

    def body(u_ref, bre, bim, cre, cim, lre, lim, dsk, hre_ref, him_ref, ys_ref, yg_ref, car_re, car_im, tabs):
        i = pl.program_id(0)

        @pl.when(i == 0)
        def _():
            car_re[...] = jnp.zeros_like(car_re)
            car_im[...] = jnp.zeros_like(car_im)
            _scan_tables(tabs, lre[...], lim[...], False, False)

        u = u_ref[...]
        for b in range(SSM_SUPER):
            ub = u[:, b * 128:(b + 1) * 128]
            st = slice(b * 512, (b + 1) * 512)
            hre_ref[:, st] = _dot(ub, bre[b], "nn")
            him_ref[:, st] = _dot(ub, bim[b], "nn")
        sr, si = _scan_rows(hre_ref, him_ref, tabs, (car_re[...], car_im[...]), chunk, False)
        car_re[...] = sr
        car_im[...] = si
        uf = u.astype(F32)
        for b in range(SSM_SUPER):
            st = slice(b * 512, (b + 1) * 512)
            ch = slice(b * 128, (b + 1) * 128)
            y = _dot(hre_ref[:, st].astype(BF16), cre[b], "nn") - _dot(him_ref[:, st].astype(BF16), cim[b], "nn")
            y = y + dsk[:, ch] * uf[:, ch]
            ys_ref[:, ch] = y
            yg_ref[:, ch] = _gelu(y).astype(BF16)

    full3 = lambda i: (0, 0, 0)
    full2 = lambda i: (0, 0)
    row = lambda i: (i, 0)
    u_col = COL_U // SSM_WIDTH
    return pl.pallas_call(
        body, name="ssm_fwd", grid=(s // chunk,),
        in_specs=[pl.BlockSpec((chunk, SSM_WIDTH), lambda i: (i, u_col)),
                  pl.BlockSpec((SSM_SUPER, 128, 512), full3), pl.BlockSpec((SSM_SUPER, 128, 512), full3),
                  pl.BlockSpec((SSM_SUPER, 512, 128), full3), pl.BlockSpec((SSM_SUPER, 512, 128), full3),
                  pl.BlockSpec((1, N_STATE), full2), pl.BlockSpec((1, N_STATE), full2), pl.BlockSpec((1, SSM_WIDTH), full2)],
        out_specs=[pl.BlockSpec((chunk, N_STATE), row), pl.BlockSpec((chunk, N_STATE), row),
                   pl.BlockSpec((chunk, SSM_WIDTH), row), pl.BlockSpec((chunk, SSM_WIDTH), row)],
        out_shape=[jax.ShapeDtypeStruct((s, N_STATE), F32), jax.ShapeDtypeStruct((s, N_STATE), F32),
                   jax.ShapeDtypeStruct((s, SSM_WIDTH), F32), jax.ShapeDtypeStruct((s, SSM_WIDTH), BF16)],
        scratch_shapes=[pltpu.VMEM((SUB, N_STATE), F32), pltpu.VMEM((SUB, N_STATE), F32),
                        pltpu.VMEM((N_SCAN_TABLES, SUB, N_STATE), F32)],
        compiler_params=_params(1),
    )(z, b_re, b_im, c_re, c_im, lam_re, lam_im, d_skip)


def _ssm_bwd(dys, z, h_re, h_im, b_re, b_im, c_re, c_im, lam_re, lam_im, d_skip, chunk):
    s = z.shape[0]
    n_chunks = s // chunk

    def body(dy_ref, u_ref, hre_ref, him_ref, hpr_ref, hpi_ref, bre, bim, cre, cim, lre, lim, dsk,
             du_ref, dlr_ref, dli_ref, dbr_ref, dbi_ref, dcr_ref, dci_ref, dd_ref, are, aim, car_re, car_im, tabs):
        i = pl.program_id(0)
        n = n_chunks - 1 - i

        @pl.when(i == 0)
        def _():
            car_re[...] = jnp.zeros_like(car_re)
            car_im[...] = jnp.zeros_like(car_im)
            _scan_tables(tabs, lre[...], lim[...], True, True)
            for r in (dlr_ref, dli_ref, dbr_ref, dbi_ref, dcr_ref, dci_ref, dd_ref):
                r[...] = jnp.zeros_like(r)

        dy = dy_ref[...]
        dyb = dy.astype(BF16)
        u = u_ref[...]
        for b in range(SSM_SUPER):
            ch = slice(b * 128, (b + 1) * 128)
            st = slice(b * 512, (b + 1) * 512)
            are[:, st] = _dot(dyb[:, ch], cre[b], "nt")
            aim[:, st] = -_dot(dyb[:, ch], cim[b], "nt")
        sr, si = _scan_rows(are, aim, tabs, (car_re[...], car_im[...]), chunk, True)
        car_re[...] = sr
        car_im[...] = si
        dd_ref[...] += jnp.sum(dy * u.astype(F32), axis=0, keepdims=True)
        half = min(256, chunk)
        row_id = lax.broadcasted_iota(jnp.int32, (half, 512), 0)
        top_scale = jnp.where(n > 0, 1.0, 0.0)
        for b in range(SSM_SUPER):
            ch = slice(b * 128, (b + 1) * 128)
            st = slice(b * 512, (b + 1) * 512)
            for r0 in range(0, chunk, half):
                rs = slice(r0, r0 + half)
                h_r = hre_ref[rs, st]
                h_i = him_ref[rs, st]
                if r0 == 0:
                    above_r, above_i = hpr_ref[SUB - 1:SUB, st] * top_scale, hpi_ref[SUB - 1:SUB, st] * top_scale
                else:
                    above_r, above_i = hre_ref[r0 - 1:r0, st], him_ref[r0 - 1:r0, st]
                hp_r = jnp.where(row_id == 0, above_r, pltpu.roll(h_r, 1, 0))
                hp_i = jnp.where(row_id == 0, above_i, pltpu.roll(h_i, 1, 0))
                a_r = are[rs, st]
                a_i = aim[rs, st]
                dlr_ref[:, st] += jnp.sum(a_r * hp_r + a_i * hp_i, axis=0, keepdims=True)
                dli_ref[:, st] += jnp.sum(a_i * hp_r - a_r * hp_i, axis=0, keepdims=True)
                a_rb = a_r.astype(BF16)
                a_ib = a_i.astype(BF16)
                dbr_ref[b] += _dot(u[rs, ch], a_rb, "tn")
                dbi_ref[b] += _dot(u[rs, ch], a_ib, "tn")
                dcr_ref[b] += _dot(dyb[rs, ch], h_r.astype(BF16), "tn")
                dci_ref[b] += -_dot(dyb[rs, ch], h_i.astype(BF16), "tn")
                du = _dot(a_rb, bre[b], "nt") + _dot(a_ib, bim[b], "nt") + dsk[:, ch] * dy[rs, ch]
                du_ref[rs, ch] = du.astype(du_ref.dtype)

    full3 = lambda i: (0, 0, 0)
    full2 = lambda i: (0, 0)
    rev = lambda i: (n_chunks - 1 - i, 0)
    above = lambda i: (jnp.maximum((n_chunks - 1 - i) * (chunk // SUB) - 1, 0), 0)
    u_col = COL_U // SSM_WIDTH
    b_spec = pl.BlockSpec((SSM_SUPER, 128, 512), full3)
    c_spec = pl.BlockSpec((SSM_SUPER, 512, 128), full3)
    vec = pl.BlockSpec((1, N_STATE), full2)
    return pl.pallas_call(
        body, name="ssm_bwd", grid=(n_chunks,),
        in_specs=[pl.BlockSpec((chunk, SSM_WIDTH), rev),
                  pl.BlockSpec((chunk, SSM_WIDTH), lambda i: (n_chunks - 1 - i, u_col)),
                  pl.BlockSpec((chunk, N_STATE), rev), pl.BlockSpec((chunk, N_STATE), rev),
                  pl.BlockSpec((SUB, N_STATE), above), pl.BlockSpec((SUB, N_STATE), above),
                  b_spec, b_spec, c_spec, c_spec, vec, vec, pl.BlockSpec((1, SSM_WIDTH), full2)],
        out_specs=[pl.BlockSpec((chunk, SSM_WIDTH), rev), vec, vec, b_spec, b_spec, b_spec, b_spec,
                   pl.BlockSpec((1, SSM_WIDTH), full2)],
        out_shape=[jax.ShapeDtypeStruct((s, SSM_WIDTH), BF16),
                   jax.ShapeDtypeStruct((1, N_STATE), F32), jax.ShapeDtypeStruct((1, N_STATE), F32)]
        + [jax.ShapeDtypeStruct((SSM_SUPER, 128, 512), F32)] * 4 + [jax.ShapeDtypeStruct((1, SSM_WIDTH), F32)],
        scratch_shapes=[pltpu.VMEM((chunk, N_STATE), F32), pltpu.VMEM((chunk, N_STATE), F32),
                        pltpu.VMEM((SUB, N_STATE), F32), pltpu.VMEM((SUB, N_STATE), F32),
                        pltpu.VMEM((N_SCAN_TABLES, SUB, N_STATE), F32)],
        compiler_params=_params(1),
    )(dys, z, h_re, h_im, h_re, h_im, b_re, b_im, c_re, c_im, lam_re, lam_im, d_skip)


def _local_step(x, p, pos, tgt, sm, w_in_own, w_in_buf, late_bufs, place):
    s = x.shape[0]
    tm = min(512, s)
    ts = min(2048, s)
    chunk = min(512, s)
    ni = s // tm
    nk = s // ts
    g_mix, g_ffn, g_final = sm["g_mix"], sm["g_ffn"], sm["g_final"]

    tmb = min(1024, s)
    nib = s // tmb
    chip_w = IN_WIDTH // N_CHIPS
    w_start, w_forward, w_finish = _gather_stages(1)
    gather_in = dict(ins=[w_in_buf], outs=[jax.ShapeDtypeStruct(w_in_buf.shape, w_in_buf.dtype)], aliased=True,
                     sems=[3] * 4, stages=[(0, w_start), (nib - 1, w_forward), (nib - 1, w_finish)])
    a_rows = lambda i, j, k, pv: (i, 0)
    z_own, n1, w_in_all = _mm("in_proj_own", (nib, 1, 1),
                              [(x, (tmb, D_MODEL), a_rows, w_in_own, (D_MODEL, chip_w), lambda i, j, k, pv: (0, 0))], "nn",
                              [((s, IN_WIDTH), BF16, (tmb, chip_w), lambda i, j, k, pv: (i, pv[P_CHIP])),
                               ((s, D_MODEL), BF16, (tmb, D_MODEL), a_rows)],
                              extras=[(g_mix, (1, D_MODEL), lambda i, j, k, pv: (0, 0))],
                              epilogue=lambda acc, g: ((acc,), ()), prologue=_rms_fwd_tile, comm=gather_in, place=place)
    w_in = w_in_all.reshape(N_CHIPS, D_MODEL, chip_w)
    n_late = len(late_bufs)
    g_start, g_forward, g_finish = _gather_stages(n_late)
    in_steps = (N_CHIPS - 1) * nib
    gather = dict(ins=late_bufs, outs=[jax.ShapeDtypeStruct(b.shape, b.dtype) for b in late_bufs], aliased=True,
                  sems=[3 * n_late] * 4,
                  stages=[(0, g_start), ((4 * in_steps) // 5, g_forward), (in_steps - 1, g_finish)])
    other = lambda j, pv: (pv[P_CHIP] + 1 + j) % N_CHIPS
    z, *late = _mm("in_proj", (nib, N_CHIPS - 1, 1),
                   [(n1, (tmb, D_MODEL), a_rows, w_in, (None, D_MODEL, chip_w), lambda i, j, k, pv: (other(j, pv), 0, 0))],
                   "nn", [((s, IN_WIDTH), BF16, (tmb, chip_w), lambda i, j, k, pv: (i, other(j, pv)))], j_outer=True,
                   comm=gather, place=place, fill=z_own)
    w_ap, w_ga, w_gb, w_out, w_fg, w_fu, w_fd, w_pg, w_pp = (
        g.reshape(N_CHIPS, 2 * g.shape[2], g.shape[3]) for g in late)
    w_out2 = w_out.reshape(D_MODEL, D_MODEL)
    w_pg2 = w_pg.reshape(D_MODEL, D_MODEL)

    inv = ROPE_THETA ** (-jnp.arange(ROPE_HALF, dtype=F32) * 2.0 / ROPE_DIM)
    inv_row = jnp.concatenate([inv, inv, jnp.zeros((HEAD_DIM - ROPE_DIM,), F32)]).reshape(1, HEAD_DIM)
    tabs = _rope_tables(pos.astype(F32).reshape(s, 1), inv_row, tm)

    qk0, qkv_views = _qkv_layout(z, tabs, tm)
    v0_col = (2 * N_GROUPS * GROUP_WIDTH) // GROUP_WIDTH
    group_in = [((qk0, qk0, z), (0, 1, v0_col))] + [(trio, (0, 0, 0)) for trio in qkv_views]
    fwd_out = [_attn_fwd(*arrs, dil, cols3) for (arrs, cols3), dil in zip(group_in, GROUP_DILATIONS)]
    attn, lse, merged_views = _attn_merge(fwd_out[0][0], fwd_out[0][1], fwd_out[1:], tm)

    def chip_cols(parts):
        return (jnp.concatenate(parts, axis=1),), ()

    def proj_cols(name, a, width, w):
        blk = (None, width, 256)
        pairs = [(a, (tmb, width), lambda i, j, k: (i, 0), w, blk, lambda i, j, k: (0, 0, 0))]
        pairs += [(None, None, None, w, blk, (lambda i, j, k, q=q: (q, 0, 0))) for q in range(1, N_CHIPS)]
        return _mm(name, (nib, 1, 1), pairs, "nn", [((s, D_MODEL), BF16, (tmb, D_MODEL), lambda i, j, k: (i, 0))],
                   epilogue=chip_cols, sum_pairs=False)[0]

    def proj512(name, a, w):
        return proj_cols(name, a, GROUP_WIDTH, w)

    attn_d = proj512("attn_proj", attn, w_ap)

    bt_re = jnp.transpose(sm["b_re"], (0, 2, 1))
    bt_im = jnp.transpose(sm["b_im"], (0, 2, 1))
    log_dt_col = sm["log_dt"].reshape(SSM_GROUPS, 1)
    lam_re, lam_im, bbt_re, bbt_im = _ssm_prep(sm["a_re"], sm["a_im"], log_dt_col, bt_re, bt_im)
    b_re_m = _block_diag(bbt_re, SSM_GROUP, SSM_STATE).astype(BF16)
    b_im_m = _block_diag(bbt_im, SSM_GROUP, SSM_STATE).astype(BF16)
    c_re_m = _block_diag(jnp.transpose(sm["c_re"], (0, 2, 1)), SSM_STATE, SSM_GROUP).astype(BF16)
    c_im_m = _block_diag(jnp.transpose(sm["c_im"], (0, 2, 1)), SSM_STATE, SSM_GROUP).astype(BF16)
    lam_re_row = lam_re.reshape(1, N_STATE)
    lam_im_row = lam_im.reshape(1, N_STATE)
    d_skip_row = sm["d_skip"].reshape(1, SSM_WIDTH)
    h_re, h_im, ys, yg = _ssm_fwd(z, b_re_m, b_im_m, c_re_m, c_im_m, lam_re_row, lam_im_row, d_skip_row, chunk)

    pa = proj512("glu_a", yg, w_ga)
    pb = proj512("glu_b", yg, w_gb)

    def mix_pro(ad, xr, g, ga, gs, a, b):
        ga, gs, ad, a, b = (t.astype(F32) for t in (ga, gs, ad, a, b))
        return _sig(ga) * ad + _sig(gs) * (a * _sig(b))

    def out_epi(acc, xr, g, *_):
        h1 = acc + xr
        return (h1, _rms_fwd_tile(h1, g)), ()

    m3 = lambda i, j, k: (i, 0)
    w3 = lambda i, j, k: (0, 0)
    tile_d = (tm, D_MODEL)
    h1, n2, mix = _mm("out_proj", (ni, 1, 1), [(attn_d, tile_d, m3, w_out2, (D_MODEL, D_MODEL), w3)], "nn",
                      [((s, D_MODEL), F32, tile_d, m3), ((s, D_MODEL), BF16, tile_d, m3), ((s, D_MODEL), BF16, tile_d, m3)],
                      epilogue=out_epi, prologue=mix_pro,
                      extras=[(x, tile_d, m3), (g_ffn, (1, D_MODEL), w3),
                              (z, tile_d, lambda i, j, k: (i, COL_GA // D_MODEL)), (z, tile_d, lambda i, j, k: (i, COL_GS // D_MODEL)),
                              (pa, tile_d, m3), (pb, tile_d, m3)])

    ffq = (None, tm, D_FF_Q)
    ffq_map = lambda i, j, k: (j, i, 0)

    def ffn_in_epi(parts):
        gts, ups = parts[0::2], parts[1::2]
        acts = [gt * _sig(gt) * u_ for gt, u_ in zip(gts, ups)]
        return (jnp.stack(gts, axis=0), jnp.stack(ups, axis=0), jnp.stack(acts, axis=0)), ()

    w_ffq = (None, D_MODEL, D_FF_Q)
    ff_pairs = []
    for q in range(N_CHIPS):
        blk_q = lambda i, j, k, q=q: (q, 0, 0)
        ff_pairs.append((n2, (tm, D_MODEL), m3, w_fg, w_ffq, blk_q) if q == 0 else (None, None, None, w_fg, w_ffq, blk_q))
        ff_pairs.append((None, None, None, w_fu, w_ffq, blk_q))
    ff_all = (N_CHIPS, tm, D_FF_Q)
    ff_all_map = lambda i, j, k: (0, i, 0)
    gate, up, act = _mm("ffn_gate_up", (ni, 1, 1), ff_pairs, "nn",
                        [((N_CHIPS, s, D_FF_Q), BF16, ff_all, ff_all_map)] * 3, epilogue=ffn_in_epi,
                        sum_pairs=False, resident_b=True)

    (h2,) = _mm("ffn_down", (nib, 1, 1),
                [(act, (None, tmb, D_FF_Q), (lambda i, j, k, q=q: (q, i, 0)), w_fd, (None, D_FF_Q, D_MODEL),
                  (lambda i, j, k, q=q: (q, 0, 0))) for q in range(N_CHIPS)], "nn",
                [((s, D_MODEL), F32, (tmb, D_MODEL), m3)], epilogue=lambda acc, hr: ((acc + hr,), ()),
                extras=[(h1, (tmb, D_MODEL), m3)])

    pp = proj_cols("ple_proj", p, PLE_DIM, w_pp)

    def ple_head_epi(acc, hr, ppr, t, g):
        sg = _sig(acc)
        ppf = ppr.astype(F32)
        h = hr + sg * ppf
        r = lax.rsqrt(jnp.mean(h * h, axis=-1, keepdims=True) + EPS)
        hhat = h * r
        diff = hhat * g - t
        loss = 0.5 * jnp.sum(jnp.mean(diff * diff, axis=-1, keepdims=True))
        dy = diff * (1.0 / D_MODEL)
        gy = dy * g
        dh = r * (gy - hhat * jnp.mean(gy * hhat, axis=-1, keepdims=True))
        return ((dh, dh * ppf * sg * (1.0 - sg), dh * sg),
                (jnp.full((SUB, 128), loss, F32), jnp.sum(dy * hhat, axis=0, keepdims=True)))

    tile_row = (tm, D_MODEL)
    dh3, dgl, dpp, loss_acc, dg_final = _mm(
        "ple_gate_head", (ni, 1, 1), [(h2, tile_row, m3, w_pg2, (D_MODEL, D_MODEL), w3)], "nn",
        [((s, D_MODEL), F32, tile_row, m3), ((s, D_MODEL), BF16, tile_row, m3), ((s, D_MODEL), BF16, tile_row, m3)],
        epilogue=ple_head_epi,
        extras=[(h2, tile_row, m3), (pp, tile_row, m3), (tgt, tile_row, m3), (g_final, (1, D_MODEL), w3)],
        acc_outs=[((SUB, 128), F32), ((1, D_MODEL), F32)])

    def wgrad(name, a, a_block, a_imap, b, b_block, b_imap, out_shape, out_block, out_imap, nj, acc_shape):
        return _mm(name, (1, nj, nk), [(a, a_block, a_imap, b, b_block, b_imap)], "tn",
                   [(out_shape, F32, out_block, out_imap)], acc_shape=acc_shape)[0]

    tk0 = lambda i, j, k: (k, 0)
    tkj = lambda i, j, k: (k, j)
    def wgrad_cols(name, a, width, dy_):
        def split(acc):
            return (jnp.stack([acc[:, q * 256:(q + 1) * 256] for q in range(N_CHIPS)], axis=0),), ()

        return _mm(name, (1, 1, nk), [(a, (ts, width), tk0, dy_, (ts, D_MODEL), tk0)], "tn",
                   [((N_CHIPS, width, 256), F32, (N_CHIPS, width, 256), lambda i, j, k: (0, 0, 0))], epilogue=split,
                   acc_shape=(width, D_MODEL))[0]

    d_w_pp = wgrad_cols("d_ple_proj", p, PLE_DIM, dpp)
    d_w_pg = wgrad("d_ple_gate", h2, (ts, D_MODEL), tk0, dgl, (ts, D_MODEL), tk0, (D_MODEL, D_MODEL),
                   (D_MODEL, D_MODEL), w3, 1, (D_MODEL, D_MODEL))

    (dh2,) = _mm("ple_gate_bwd", (nib, 1, 1), [(dgl, (tmb, D_MODEL), m3, w_pg2, (D_MODEL, D_MODEL), w3)], "nt",
                 [((s, D_MODEL), F32, (tmb, D_MODEL), m3)], epilogue=lambda acc, d_: ((acc + d_,), ()),
                 extras=[(dh3, (tmb, D_MODEL), m3)])

    def ffn_bwd_epi(parts, gt_all, u_all):
        dgs_, dus_ = [], []
        for q, dact in enumerate(parts):
            gt, u_ = gt_all[q].astype(F32), u_all[q].astype(F32)
            sg = _sig(gt)
            dgs_.append(dact * u_ * (sg * (1.0 + gt * (1.0 - sg))))
            dus_.append(dact * gt * sg)
        return (jnp.stack(dgs_, axis=0), jnp.stack(dus_, axis=0)), ()

    fd_pairs = [((dh2, (tm, D_MODEL), m3) if q == 0 else (None, None, None))
                + (w_fd, (None, D_FF_Q, D_MODEL), (lambda i, j, k, q=q: (q, 0, 0))) for q in range(N_CHIPS)]
    dgate, dup = _mm("ffn_down_bwd", (ni, 1, 1), fd_pairs, "nt",
                     [((N_CHIPS, s, D_FF_Q), BF16, ff_all, ff_all_map)] * 2, epilogue=ffn_bwd_epi,
                     extras=[(gate, ff_all, ff_all_map), (up, ff_all, ff_all_map)], sum_pairs=False, resident_b=True)

    ffq_t = (None, ts, D_FF_Q)
    ffq_tmap = lambda i, j, k: (j, k, 0)
    blk_j = lambda i, j, k: (j, 0, 0)
    d_w_fd = wgrad("d_ffn_down", act, ffq_t, ffq_tmap, dh2, (ts, D_MODEL), tk0, (N_CHIPS, D_FF_Q, D_MODEL),
                   (None, D_FF_Q, D_MODEL), blk_j, N_CHIPS, (D_FF_Q, D_MODEL))
    d_w_fg = wgrad("d_ffn_gate", n2, (ts, D_MODEL), tk0, dgate, ffq_t, ffq_tmap, (N_CHIPS, D_MODEL, D_FF_Q),
                   (None, D_MODEL, D_FF_Q), blk_j, N_CHIPS, (D_MODEL, D_FF_Q))
    d_w_fu = wgrad("d_ffn_up", n2, (ts, D_MODEL), tk0, dup, ffq_t, ffq_tmap, (N_CHIPS, D_MODEL, D_FF_Q),
                   (None, D_MODEL, D_FF_Q), blk_j, N_CHIPS, (D_MODEL, D_FF_Q))

    def norm_bwd_epi(acc, h, d_res, g):
        dh, dg = _rms_bwd_tile(acc, h, g)
        return (d_res + dh,), (dg,)

    fi_pairs = []
    for q in range(N_CHIPS):
        a_q = lambda i, j, k, q=q: (q, i, 0)
        b_q = lambda i, j, k, q=q: (q, 0, 0)
        fi_pairs.append((dgate, ffq, a_q, w_fg, (None, D_MODEL, D_FF_Q), b_q))
        fi_pairs.append((dup, ffq, a_q, w_fu, (None, D_MODEL, D_FF_Q), b_q))
    dh1, dg_ffn = _mm("ffn_in_bwd", (ni, 1, 1), fi_pairs, "nt",
                      [((s, D_MODEL), F32, (tm, D_MODEL), m3)], epilogue=norm_bwd_epi,
                      extras=[(h1, (tm, D_MODEL), m3), (dh2, (tm, D_MODEL), m3), (g_ffn, (1, D_MODEL), w3)],
                      acc_outs=[((1, D_MODEL), F32)], resident_b=True)

    d_w_out = wgrad("d_out_proj", mix, (ts, D_MODEL), tk0, dh1, (ts, D_MODEL), tk0, (D_MODEL, D_MODEL),
                    (D_MODEL, D_MODEL), w3, 1, (D_MODEL, D_MODEL))

    def mix_bwd_epi(dm, ga, gs, ad, a, b):
        ga, gs, ad, a, b = (t.astype(F32) for t in (ga, gs, ad, a, b))
        s_a, s_s, s_b = _sig(ga), _sig(gs), _sig(b)
        d_ssm = dm * s_s
        return (dm * ad * s_a * (1.0 - s_a), dm * (a * s_b) * s_s * (1.0 - s_s), dm * s_a, d_ssm * s_b,
                d_ssm * a * s_b * (1.0 - s_b)), ()

    tile_m = (tm, D_MODEL)
    dga, dgs, dattn_d, dpa, dpb = _mm(
        "out_proj_bwd", (ni, 1, 1), [(dh1, tile_m, m3, w_out2, (D_MODEL, D_MODEL), w3)], "nt",
        [((s, D_MODEL), BF16, tile_m, m3)] * 5, epilogue=mix_bwd_epi,
        extras=[(z, tile_m, lambda i, j, k: (i, COL_GA // D_MODEL)), (z, tile_m, lambda i, j, k: (i, COL_GS // D_MODEL)),
                (attn_d, tile_m, m3), (pa, tile_m, m3), (pb, tile_m, m3)])

    d_w_ap = wgrad_cols("d_attn_proj", attn, GROUP_WIDTH, dattn_d)
    d_w_ga = wgrad_cols("d_glu_a", yg, GROUP_WIDTH, dpa)
    d_w_gb = wgrad_cols("d_glu_b", yg, GROUP_WIDTH, dpb)

    ik = lambda i, j, k: (i, k)

    def cols_bwd(dy_, w):
        return [(dy_, (tmb, 256), (lambda i, j, k, q=q: (i, q)), w, (None, GROUP_WIDTH, 256),
                 (lambda i, j, k, q=q: (q, 0, 0))) for q in range(N_CHIPS)]

    (dattn,) = _mm("attn_proj_bwd", (nib, 1, 1), cols_bwd(dattn_d, w_ap), "nt",
                   [((s, GROUP_WIDTH), BF16, (tmb, GROUP_WIDTH), m3)])

    (dys,) = _mm("glu_bwd", (nib, 1, 1), cols_bwd(dpa, w_ga) + cols_bwd(dpb, w_gb), "nt",
                 [((s, GROUP_WIDTH), F32, (tmb, GROUP_WIDTH), m3)],
                 epilogue=lambda acc, y_: ((acc * _gelu_grad(y_),), ()),
                 extras=[(ys, (tmb, GROUP_WIDTH), m3)])

    du, d_lr, d_li, d_bre, d_bim, d_cre, d_cim, d_dskip = _ssm_bwd(
        dys, z, h_re, h_im, b_re_m, b_im_m, c_re_m, c_im_m, lam_re_row, lam_im_row, d_skip_row, chunk)

    dattn_views = _to_views(dattn, tm)
    bwd_in = [(dattn, attn, lse)] + [(dv_, ov_, lv_) for dv_, (ov_, lv_) in zip(dattn_views, merged_views)]
    qkv_grads = [_attn_bwd(*arrs, *dol, dil, cols3)
                 for (arrs, cols3), dol, dil in zip(group_in, bwd_in, GROUP_DILATIONS)]
    early = [d_w_ap, d_w_ga, d_w_gb, d_w_out.reshape(N_CHIPS, D_MODEL // N_CHIPS, D_MODEL), d_w_fg, d_w_fu, d_w_fd,
             d_w_pg.reshape(N_CHIPS, D_MODEL // N_CHIPS, D_MODEL), d_w_pp]
    early5 = [g.reshape(N_CHIPS, 2, g.shape[1] // 2, g.shape[2]) for g in early]
    n_e = len(early5)
    p_start, p_finish = _pair_exchange_stages(n_e)
    dz, early_theirs = _dz_layout(
        qkv_grads, du, dga, dgs, tabs, tm,
        comm=dict(ins=early5, outs=_pair_exchange_shapes(early5), sems=[N_CHIPS * n_e] * 2, start=p_start, finish=p_finish))
    early_parts = [_pair_sum(g, t, place) for g, t in zip(early5, early_theirs)]

    chip_in = IN_WIDTH // N_CHIPS
    ip_pairs = [(dz, (tm, chip_in), (lambda i, j, k, q=q: (i, q)), w_in, (None, D_MODEL, chip_in),
                 (lambda i, j, k, q=q: (q, 0, 0))) for q in range(N_CHIPS)]
    grad_x, dg_mix = _mm("in_proj_bwd", (ni, 1, 1), ip_pairs, "nt",
                         [((s, D_MODEL), F32, (tm, D_MODEL), m3)], epilogue=norm_bwd_epi,
                         extras=[(x, (tm, D_MODEL), m3), (dh1, (tm, D_MODEL), m3), (g_mix, (1, D_MODEL), w3)],
                         acc_outs=[((1, D_MODEL), F32)], resident_b=True)

    d_bbt_re = _block_diag_t(d_bre, SSM_GROUP, SSM_STATE)
    d_bbt_im = _block_diag_t(d_bim, SSM_GROUP, SSM_STATE)
    d_a_re, d_a_im, d_log_dt, d_bt_re, d_bt_im = _ssm_param_bwd(
        sm["a_re"], sm["a_im"], log_dt_col, bt_re, bt_im,
        d_lr.reshape(SSM_GROUPS, SSM_STATE), d_li.reshape(SSM_GROUPS, SSM_STATE), d_bbt_re, d_bbt_im)
    small = {
        "g_mix": dg_mix, "a_re": d_a_re, "a_im": d_a_im, "log_dt": d_log_dt,
        "b_re": jnp.transpose(d_bt_re, (0, 2, 1)), "b_im": jnp.transpose(d_bt_im, (0, 2, 1)),
        "c_re": _block_diag_t(d_cre, SSM_GROUP, SSM_STATE), "c_im": _block_diag_t(d_cim, SSM_GROUP, SSM_STATE),
        "d_skip": d_dskip, "g_ffn": dg_ffn, "g_final": dg_final,
    }
    vec = _pack([small[n] for n in SMALL] + [loss_acc[0, 0].reshape(1)])

    x_start, x_finish = _chip_exchange_stages(n_e)
    v_start, v_finish = _all_exchange_stages()

    def both(f_chips, f_vec):
        def stage(ins, outs, sems):
            f_chips(ins[:n_e], outs[:n_e], sems[:2])
            f_vec(ins[n_e:], outs[n_e:], sems[2:])
        return stage

    half_in = chip_in // 2
    win_steps = 8 * nk
    exchange = dict(ins=early_parts + [vec],
                    outs=[jax.ShapeDtypeStruct(t.shape, t.dtype) for t in early_parts]
                    + [jax.ShapeDtypeStruct((8,) + vec.shape, vec.dtype)],
                    aliased=False, sems=[3 * n_e, 3 * n_e, 7, 7],
                    stages=[(0, both(x_start, v_start)), (win_steps - 1, both(x_finish, v_finish))])
    d_w_in, *got = _mm("d_in_proj", (1, 8, nk), [(n1, (ts, D_MODEL), tk0, dz, (ts, half_in), tkj)], "tn",
                       [((N_CHIPS, D_MODEL, chip_in), F32, (None, D_MODEL, half_in), lambda i, j, k: (j // 2, 0, j % 2))],
                       acc_shape=(D_MODEL, half_in), comm=exchange)
    return grad_x, d_w_in, early_parts, got[:n_e], vec, got[n_e]


BIG = ("w_in", "w_attn_proj", "w_glu_a", "w_glu_b", "w_out", "w_ffn_gate", "w_ffn_up", "w_ffn_down", "w_ple_gate",
       "w_ple_proj")
SMALL = ("g_mix", "a_re", "a_im", "log_dt", "b_re", "b_im", "c_re", "c_im", "d_skip", "g_ffn", "g_final")
ANY = pl.BlockSpec(memory_space=pl.ANY)


def _place():
    x, y, c = lax.axis_index("x"), lax.axis_index("y"), lax.axis_index("c")
    chips = [(1 - x, y), (x, 1 - y), (1 - x, 1 - y)]
    return x, y, c, chips


def _remote(src, dst, send_sem, recv_sem, to):
    return pltpu.make_async_remote_copy(src_ref=src, dst_ref=dst, send_sem=send_sem, recv_sem=recv_sem, device_id=to,
                                        device_id_type=MESH)


def _comm_call(name, body, ins, out_shapes, n_sems, aliases=None):
    n_w = len(ins)
    return pl.pallas_call(
        body, name=name, in_specs=[ANY] * n_w, out_specs=[ANY] * len(out_shapes), out_shape=out_shapes,
        scratch_shapes=[pltpu.SemaphoreType.DMA((n,)) for n in n_sems], input_output_aliases=aliases or {},
    )(*ins)


def _gather_stages(n_w):
    def each():
        x, y, c, chips = _place()
        for w in range(n_w):
            for j, (cx, cy) in enumerate(chips):
                yield w, 3 * w + j, 2 * x + y, 2 * cx + cy, (cx, cy, c), (x, y, 1 - c), c

    def start(ins, outs, sems):
        for w, k, me, _, peer, _, c in each():
            mine = outs[w].at[me, c]
            _remote(mine, mine, sems[0].at[k], sems[1].at[k], peer).start()

    def forward(ins, outs, sems):
        for w, k, _, src_chip, peer, sib, c in each():
            landed = outs[w].at[src_chip, c]
            _remote(landed, landed, sems[0].at[k], sems[1].at[k], peer).wait_recv()
            _remote(landed, landed, sems[2].at[k], sems[3].at[k], sib).start()

    def finish(ins, outs, sems):
        for w, k, me, src_chip, peer, sib, c in each():
            other = outs[w].at[src_chip, 1 - c]
            _remote(other, other, sems[2].at[k], sems[3].at[k], sib).wait_recv()
        for w, k, me, src_chip, peer, sib, c in each():
            mine = outs[w].at[me, c]
            _remote(mine, mine, sems[0].at[k], sems[1].at[k], peer).wait_send()
            landed = outs[w].at[src_chip, c]
            _remote(landed, landed, sems[2].at[k], sems[3].at[k], sib).wait_send()

    return start, forward, finish


def _pair_exchange(grads):
    n_w = len(grads)
    start, finish = _pair_exchange_stages(n_w)

    def body(*refs):
        ins, outs, sems = refs[:n_w], refs[n_w:2 * n_w], refs[2 * n_w:]
        start(ins, outs, sems)
        finish(ins, outs, sems)

    return _comm_call("grad_pair_exchange", body, grads, _pair_exchange_shapes(grads), [N_CHIPS * n_w] * 2)


def _pair_exchange_shapes(grads):
    return [jax.ShapeDtypeStruct((N_CHIPS,) + g.shape[2:], g.dtype) for g in grads]


def _pair_exchange_stages(n_w):
    def each():
        x, y, c, _ = _place()
        for w in range(n_w):
            for q in range(N_CHIPS):
                yield w, q, N_CHIPS * w + q, c, (x, y, 1 - c)

    def start(ins, outs, sems):
        for w, q, k, c, sib in each():
            _remote(ins[w].at[q, 1 - c], outs[w].at[q], sems[0].at[k], sems[1].at[k], sib).start()

    def finish(ins, outs, sems):
        for w, q, k, c, sib in each():
            _remote(ins[w].at[q, 1 - c], outs[w].at[q], sems[0].at[k], sems[1].at[k], sib).wait()

    return start, finish


def _chip_exchange(parts):
    n_w = len(parts)

    start, finish = _chip_exchange_stages(n_w)

    def body(*refs):
        ins, outs, sems = refs[:n_w], refs[n_w:2 * n_w], refs[2 * n_w:]
        start(ins, outs, sems)
        finish(ins, outs, sems)

    out_shapes = [jax.ShapeDtypeStruct(t.shape, t.dtype) for t in parts]
    return _comm_call("grad_chip_exchange", body, parts, out_shapes, [3 * n_w, 3 * n_w])


def _chip_exchange_stages(n_w):
    def each():
        x, y, c, chips = _place()
        for w in range(n_w):
            for j, (cx, cy) in enumerate(chips):
                yield w, 3 * w + j, 2 * x + y, 2 * cx + cy, (cx, cy, c)

    def start(ins, outs, sems):
        for w, k, me, peer_chip, peer in each():
            _remote(ins[w].at[peer_chip], outs[w].at[me], sems[0].at[k], sems[1].at[k], peer).start()

    def finish(ins, outs, sems):
        for w, k, me, peer_chip, peer in each():
            got = outs[w].at[peer_chip]
            _remote(got, got, sems[0].at[k], sems[1].at[k], peer).wait_recv()
        for w, k, me, peer_chip, peer in each():
            _remote(ins[w].at[peer_chip], outs[w].at[me], sems[0].at[k], sems[1].at[k], peer).wait_send()

    return start, finish


def _pair_gather(halves):
    n_w = len(halves)

    def body(*refs):
        ins, outs = refs[:n_w], refs[n_w:2 * n_w]
        send, recv = refs[2 * n_w:]
        x, y, c, _ = _place()
        sib = (x, y, 1 - c)
        cps = []
        for w in range(n_w):
            cp = _remote(ins[w], outs[w], send.at[w], recv.at[w], sib)
            cp.start()
            cps.append(cp)
        for cp in cps:
            cp.wait()

    out_shapes = [jax.ShapeDtypeStruct(h.shape, h.dtype) for h in halves]
    return _comm_call("grad_pair_gather", body, halves, out_shapes, [n_w] * 2)


def _all_exchange_stages():
    def each():
        x, y, c, _ = _place()
        for k in range(1, 8):
            px, py, pc = x ^ ((k >> 2) & 1), y ^ ((k >> 1) & 1), c ^ (k & 1)
            yield k - 1, 4 * x + 2 * y + c, 4 * px + 2 * py + pc, (px, py, pc)

    def start(ins, outs, sems):
        for k, me, _, peer in each():
            _remote(ins[0], outs[0].at[me], sems[0].at[k], sems[1].at[k], peer).start()

    def finish(ins, outs, sems):
        for k, me, src, peer in each():
            got = outs[0].at[src]
            _remote(got, got, sems[0].at[k], sems[1].at[k], peer).wait_recv()
        for k, me, src, peer in each():
            _remote(ins[0], outs[0].at[me], sems[0].at[k], sems[1].at[k], peer).wait_send()

    return start, finish


def _row_tile(r):
    for t in (256, 128, 176, 64, 32, 16, 8):
        if r % t == 0:
            return t
    return r


P_C, P_CHIP, P_DEV = 2, 3, 4


def _cast_shard(w2):
    r, c = w2.shape
    t = _row_tile(r)
    blk, imap = _rows(t, c)
    return _ew("cast_own", (r // t,), [(w2, blk, imap)], [((r, c), BF16, blk, imap)], lambda pids, a: ((a,), ()))[0]


def _cast_into_slot(w2, place):
    r, c = w2.shape
    t = _row_tile(r)
    return _ew("cast_shard", (r // t,), [(w2, (t, c), lambda i, pv: (i, 0))],
               [((N_CHIPS, r, c), BF16, (None, t, c), lambda i, pv: (pv[P_CHIP], i, 0))],
               lambda pids, a: ((a,), ()), place=place)[0]


def _pair_sum(mine, theirs, place):
    _, r, c = theirs.shape
    t = _row_tile(r)
    own = ((None, None, t, c), lambda q, i, pv: (q, pv[P_C], i, 0))
    blk = ((None, t, c), lambda q, i, pv: (q, i, 0))
    return _ew("grad_pair_sum", (N_CHIPS, r // t), [(mine, *own), (theirs, *blk)], [((N_CHIPS, r, c), BF16, *blk)],
               lambda pids, a, b: ((a + b,), ()), place=place)[0]


def _chip_sum(own, got, place):
    _, r, c = own.shape
    t = _row_tile(r)
    ins = []
    for q in range(N_CHIPS):
        ins.append((own, (None, t, c), (lambda i, pv, q=q: (q, i, 0))))
        ins.append((got, (None, t, c), (lambda i, pv, q=q: (jnp.where(pv[P_CHIP] == q, (q + 1) % N_CHIPS, q), i, 0))))

    def fn(pids, *tiles):
        me = pids[0][P_CHIP]
        tot = None
        for q in range(N_CHIPS):
            term = jnp.where(me == q, tiles[2 * q], tiles[2 * q + 1]).astype(F32)
            tot = term if tot is None else tot + term
        return (tot,), ()

    return _ew("grad_chip_sum", (r // t,), ins, [((r, c), F32, (t, c), lambda i, pv: (i, 0))], fn, place=place)[0]


def _adamw_tile(w, g, m, v):
    m = ADAM_B1 * m + (1.0 - ADAM_B1) * g
    v = ADAM_B2 * v + (1.0 - ADAM_B2) * (g * g)
    m_hat = m / (1.0 - ADAM_B1 ** ADAM_STEP)
    v_hat = v / (1.0 - ADAM_B2 ** ADAM_STEP)
    delta = -ADAM_LR * (m_hat / (jnp.sqrt(v_hat) + ADAM_EPS) + ADAM_WD * w)
    return delta, m, v


def _adamw(name, g2, w2, m2, v2):
    r, c = w2.shape
    t = _row_tile(r)
    blk, imap = _rows(t, c)

    def fn(pids, g, w, m, v):
        delta, nm, nv = _adamw_tile(w, g, m, v)
        return (g, delta, nm, nv), ()

    return _ew(name, (r // t,), [(a, blk, imap) for a in (g2, w2, m2, v2)], [((r, c), F32, blk, imap)] * 4, fn)


def _adamw_halves(name, mine, theirs, w2, m2, v2, place):
    r, c = w2.shape
    t = _row_tile(r // 2)
    n_t = (r // 2) // t
    half = ((t, c), lambda h, i, pv: (i, 0))
    whole = ((t, c), lambda h, i, pv: (h * n_t + i, 0))

    def fn(pids, ga, gb, w, m, v):
        g = jnp.where(pids[1] == pids[0][P_C], ga, gb)
        delta, nm, nv = _adamw_tile(w, g, m, v)
        return (g, delta, nm, nv), ()

    return _ew(name, (2, n_t), [(mine, *half), (theirs, *half), (w2, *whole), (m2, *whole), (v2, *whole)],
               [((r, c), F32, *whole)] * 4, fn, place=place)


def _device_sum(own, got, place):
    r, c = own.shape
    t = _row_tile(r)
    ins = [(own, (t, c), lambda i, pv: (i, 0))]
    for q in range(8):
        ins.append((got, (None, t, c), (lambda i, pv, q=q: (jnp.where(pv[P_DEV] == q, (q + 1) % 8, q), i, 0))))

    def fn(pids, mine, *parts):
        me = pids[0][P_DEV]
        tot = None
        for q in range(8):
            term = jnp.where(me == q, mine, parts[q])
            tot = term if tot is None else tot + term
        return (tot,), ()

    return _ew("small_device_sum", (r // t,), ins, [((r, c), F32, (t, c), lambda i, pv: (i, 0))], fn, place=place)[0]


def _pack(parts):
    flat = jnp.concatenate([a.reshape(-1) for a in parts])
    pad = (-flat.shape[0]) % (SUB * 128)
    return jnp.pad(flat, (0, pad)).reshape(-1, 128)


def _unpack(mat, shapes):
    flat = mat.reshape(-1)
    out, off = [], 0
    for shp in shapes:
        n = math.prod(shp)
        out.append(flat[off:off + n].reshape(shp))
        off += n
    return out


def kernel(x, p, positions, g_mix, w_in, a_re, a_im, log_dt, b_re, b_im, c_re, c_im, d_skip, w_attn_proj, w_glu_a, w_glu_b, w_out, g_ffn, w_ffn_gate, w_ffn_up, w_ffn_down, w_ple_gate, w_ple_proj, g_final, loss_target, m_g_mix, m_w_in, m_a_re, m_a_im, m_log_dt, m_b_re, m_b_im, m_c_re, m_c_im, m_d_skip, m_w_attn_proj, m_w_glu_a, m_w_glu_b, m_w_out, m_g_ffn, m_w_ffn_gate, m_w_ffn_up, m_w_ffn_down, m_w_ple_gate, m_w_ple_proj, m_g_final, v_g_mix, v_w_in, v_a_re, v_a_im, v_log_dt, v_b_re, v_b_im, v_c_re, v_c_im, v_d_skip, v_w_attn_proj, v_w_glu_a, v_w_glu_b, v_w_out, v_g_ffn, v_w_ffn_gate, v_w_ffn_up, v_w_ffn_down, v_w_ple_gate, v_w_ple_proj, v_g_final):
    given = dict(locals())
    big_w = {n: given[n] for n in BIG}
    w_mats = {n: big_w[n].reshape(big_w[n].shape[1:]) for n in BIG}

    ax, ay, ac = lax.axis_index("x"), lax.axis_index("y"), lax.axis_index("c")
    place = jnp.stack([ax, ay, ac, 2 * ax + ay, 4 * ax + 2 * ay + ac]).astype(jnp.int32)

    bufs = []
    for n in BIG:
        r, c = w_mats[n].shape
        bufs.append(_cast_into_slot(w_mats[n], place).reshape(N_CHIPS, 2, r // 2, c))
    w_in_own = _cast_shard(w_mats["w_in"])

    sm = {
        "g_mix": g_mix.reshape(1, D_MODEL), "g_ffn": g_ffn.reshape(1, D_MODEL), "g_final": g_final.reshape(1, D_MODEL),
        "a_re": a_re[0], "a_im": a_im[0], "log_dt": log_dt[0], "b_re": b_re[0], "b_im": b_im[0], "c_re": c_re[0],
        "c_im": c_im[0], "d_skip": d_skip[0],
    }
    s = x.shape[1]
    grad_x, d_w_in, early_parts, early_got, vec, vec_got = _local_step(
        x[0], p[0, 0], positions[0], loss_target[0], sm, w_in_own, bufs[0], bufs[1:], place)

    r_in, c_in = w_mats["w_in"].shape
    g5_in = [d_w_in.reshape(N_CHIPS, 2, r_in // 2, c_in)]
    in_parts = [_pair_sum(g, t, place) for g, t in zip(g5_in, _pair_exchange(g5_in))]
    chip_parts = in_parts + list(early_parts)
    chip_got = list(_chip_exchange(in_parts)) + list(early_got)
    halves = [_chip_sum(own, got, place) for own, got in zip(chip_parts, chip_got)]
    other_halves = _pair_gather(halves)

    results = {}
    for n, mine, other in zip(BIG, halves, other_halves):
        r, c = w_mats[n].shape
        shp = big_w[n].shape
        outs = _adamw_halves("adamw_" + n, mine, other, w_mats[n], given["m_" + n].reshape(r, c),
                             given["v_" + n].reshape(r, c), place)
        results[n] = [o.reshape(shp) for o in outs]

    small_shapes = [given[n].shape for n in SMALL]
    tot = _device_sum(vec, vec_got, place)
    n_small = sum(math.prod(shp) for shp in small_shapes)
    loss = tot.reshape(-1)[n_small]
    w_s = _pack([given[n] for n in SMALL])
    m_s = _pack([given["m_" + n] for n in SMALL])
    v_s = _pack([given["v_" + n] for n in SMALL])
    rows_s = w_s.shape[0]
    g_s = tot.reshape(-1)[: rows_s * 128].reshape(rows_s, 128)
    outs_s = _adamw("adamw_small", g_s, w_s, m_s, v_s)
    for kind, mat in enumerate(outs_s):
        for n, arr in zip(SMALL, _unpack(mat, small_shapes)):
            results.setdefault(n, [None] * 4)[kind] = arr

    order = ("g_mix", "w_in", "a_re", "a_im", "log_dt", "b_re", "b_im", "c_re", "c_im", "d_skip", "w_attn_proj", "w_glu_a",
             "w_glu_b", "w_out", "g_ffn", "w_ffn_gate", "w_ffn_up", "w_ffn_down", "w_ple_gate", "w_ple_proj", "g_final")
    out = [loss, grad_x.reshape(1, s, D_MODEL)]
    for kind in range(4):
        out += [results[n][kind] for n in order]
    return tuple(out)
```
